```python
import jax, jax.numpy as jnp
from jax import lax
import numpy as np

D_MODEL = 1024
BATCH = 32
SEQ = 2048
DEPTH = 1

MLA_HEADS = 8
MLA_NOPE = 64
MLA_ROPE = 32
MLA_V = 64
Q_LORA = 384
KV_LORA = 256
MLA_WIDTH = MLA_HEADS * MLA_V

SWA_HEADS = 8
SWA_KV_HEADS = 2
SWA_HEAD_DIM = 64
SWA_GROUP = SWA_HEADS // SWA_KV_HEADS
SWA_WIDTH = SWA_HEADS * SWA_HEAD_DIM
SWA_KV_WIDTH = SWA_KV_HEADS * SWA_HEAD_DIM
WINDOW = 128

D_MIX = MLA_WIDTH + SWA_WIDTH
Q_BLOCK = 128
ROPE_THETA = 10000.0
EPS = 1e-6
ALIBI_MAX_EXP = 8.0

IN_SPLITS = (Q_LORA, KV_LORA, MLA_ROPE, MLA_WIDTH,
             SWA_WIDTH, SWA_KV_WIDTH, SWA_KV_WIDTH, SWA_WIDTH)
D_IN = int(sum(IN_SPLITS))
SPLIT_IDX = [int(v) for v in np.cumsum(IN_SPLITS)[:-1]]

kernel_name = 'hymba_mla_swa_adaln_block'


def rmsnorm(t, gain):
    tf = t.astype(jnp.float32)
    y = tf * lax.rsqrt(jnp.mean(tf * tf, axis=-1, keepdims=True) + EPS)
    return (y * gain.astype(jnp.float32)).astype(t.dtype)


def rope_cos_sin(positions):
    inv = ROPE_THETA ** (-jnp.arange(0, MLA_ROPE, 2, dtype=jnp.float32) / MLA_ROPE)
    ang = positions.astype(jnp.float32)[..., None] * inv
    return jnp.cos(ang), jnp.sin(ang)


def apply_rope(t, cos, sin):
    tf = t.astype(jnp.float32)
    t1, t2 = jnp.split(tf, 2, axis=-1)
    return jnp.concatenate([t1 * cos - t2 * sin, t2 * cos + t1 * sin], axis=-1).astype(t.dtype)


def alibi_slopes(n_heads):
    h = jnp.arange(1, n_heads + 1, dtype=jnp.float32)
    return 2.0 ** (-ALIBI_MAX_EXP * h / n_heads)


def mla_attention(q_nope, q_pe, k_nope, k_pe, v):
    B, S = q_nope.shape[0], q_nope.shape[1]
    nb = S // Q_BLOCK
    scale = (MLA_NOPE + MLA_ROPE) ** -0.5
    key_idx = jnp.arange(S)

    def to_blocks(t):
        return jnp.moveaxis(t.reshape(B, nb, Q_BLOCK, *t.shape[2:]), 1, 0)

    def one_block(args):
        qn, qp, blk = args
        s = (jnp.einsum('bqhd,bshd->bhqs', qn, k_nope, preferred_element_type=jnp.float32)
             + jnp.einsum('bqhr,bsr->bhqs', qp, k_pe, preferred_element_type=jnp.float32)) * scale
        q_idx = blk * Q_BLOCK + jnp.arange(Q_BLOCK)
        s = jnp.where(key_idx[None, :] <= q_idx[:, None], s, -jnp.inf)
        p = jax.nn.softmax(s, axis=-1).astype(v.dtype)
        return jnp.einsum('bhqs,bshd->bqhd', p, v)

    o = lax.map(one_block, (to_blocks(q_nope), to_blocks(q_pe), jnp.arange(nb)))
    return jnp.moveaxis(o, 0, 1).reshape(B, S, MLA_WIDTH)


def swa_attention(q, k, v, positions, slopes, sinks):
    B, S = q.shape[0], q.shape[1]
    nb = S // WINDOW
    scale = SWA_HEAD_DIM ** -0.5

    def band(t):
        tb = t.reshape(B, nb, WINDOW, *t.shape[2:])
        prev = jnp.concatenate([jnp.zeros_like(tb[:, :1]), tb[:, :-1]], axis=1)
        return jnp.concatenate([prev, tb], axis=2)

    qb = q.reshape(B, nb, WINDOW, SWA_KV_HEADS, SWA_GROUP, SWA_HEAD_DIM)
    kb = band(k.reshape(B, S, SWA_KV_HEADS, SWA_HEAD_DIM))
    vb = band(v.reshape(B, S, SWA_KV_HEADS, SWA_HEAD_DIM))
    s = jnp.einsum('bnqkgd,bnskd->bnkgqs', qb, kb, preferred_element_type=jnp.float32) * scale
    pq = positions.reshape(B, nb, WINDOW)
    pk = band(positions)
    dist = (pq[..., :, None] - pk[..., None, :]).astype(jnp.float32)
    s = s - slopes.reshape(SWA_KV_HEADS, SWA_GROUP)[None, None, :, :, None, None] * dist[:, :, None, None]
    i = jnp.arange(WINDOW)[:, None]
    j = jnp.arange(2 * WINDOW)[None, :]
    rel = WINDOW + i - j
    blk = jnp.arange(nb)[:, None, None]
    valid = (rel >= 0) & (rel < WINDOW) & ((blk > 0) | (j >= WINDOW))
    s = jnp.where(valid[None, :, None, None], s, -jnp.inf)
    sink = jnp.broadcast_to(sinks.astype(jnp.float32).reshape(SWA_KV_HEADS, SWA_GROUP)[None, None, :, :, None, None],
                            s.shape[:-1] + (1,))
    p = jax.nn.softmax(jnp.concatenate([s, sink], axis=-1), axis=-1)[..., :-1].astype(v.dtype)
    o = jnp.einsum('bnkgqs,bnskd->bnqkgd', p, vb)
    return o.reshape(B, S, SWA_WIDTH)


def _fwd_setup_inputs(seed: int = 0) -> dict:
    key = jax.random.key(seed)
    ks = jax.random.split(key, 16)
    f32 = jnp.float32
    nrm = lambda k, shape, s: jax.random.normal(k, shape, f32) * s
    x = jax.random.normal(ks[0], (BATCH, SEQ, D_MODEL), f32)
    c = jax.random.normal(ks[1], (BATCH, D_MODEL), f32)
    offs = jax.random.randint(ks[2], (BATCH, 1), 0, 1024, dtype=jnp.int32)
    positions = offs + jnp.arange(SEQ, dtype=jnp.int32)[None, :]
    return {
        'x': x,
        'c': c,
        'positions': positions,
        'w_ada': nrm(ks[3], (DEPTH, D_MODEL, 3 * D_MODEL), D_MODEL ** -0.5),
        'b_ada': nrm(ks[4], (DEPTH, 3 * D_MODEL), 0.02),
        'norm_gain': 1.0 + nrm(ks[5], (DEPTH, D_MODEL), 0.02),
        'w_in': nrm(ks[6], (DEPTH, D_MODEL, D_IN), D_MODEL ** -0.5),
        'q_norm_gain': 1.0 + nrm(ks[7], (DEPTH, Q_LORA), 0.02),
        'kv_norm_gain': 1.0 + nrm(ks[8], (DEPTH, KV_LORA), 0.02),
        'w_uq': nrm(ks[9], (DEPTH, Q_LORA, MLA_HEADS * (MLA_NOPE + MLA_ROPE)), Q_LORA ** -0.5),
        'w_ukv': nrm(ks[10], (DEPTH, KV_LORA, MLA_HEADS * (MLA_NOPE + MLA_V)), KV_LORA ** -0.5),
        'swa_sinks': nrm(ks[11], (DEPTH, SWA_HEADS), 1.0),
        'w_out': nrm(ks[12], (DEPTH, D_MIX, D_MODEL), D_MIX ** -0.5),
        'final_gain': 1.0 + nrm(ks[13], (D_MODEL,), 0.02),
    }


def _fwd_reference(x, c, positions, w_ada, b_ada, norm_gain, w_in, q_norm_gain, kv_norm_gain,
              w_uq, w_ukv, swa_sinks, w_out, final_gain):
    B, S, _ = x.shape
    cos, sin = rope_cos_sin(positions)
    slopes = alibi_slopes(SWA_HEADS)
    c_act = jax.nn.silu(c)
    for l in range(DEPTH):
        mod = c_act @ w_ada[l] + b_ada[l]
        shift, scale, gate = jnp.split(mod, 3, axis=-1)
        h = rmsnorm(x, norm_gain[l]) * (1.0 + scale[:, None, :]) + shift[:, None, :]
        z = h @ w_in[l]
        zq, zkv, kr, g_mla, q_s, k_s, v_s, g_swa = jnp.split(z, SPLIT_IDX, axis=-1)
        q = (rmsnorm(zq, q_norm_gain[l]) @ w_uq[l]).reshape(B, S, MLA_HEADS, MLA_NOPE + MLA_ROPE)
        q_nope, q_pe = q[..., :MLA_NOPE], apply_rope(q[..., MLA_NOPE:], cos[:, :, None, :], sin[:, :, None, :])
        kv = (rmsnorm(zkv, kv_norm_gain[l]) @ w_ukv[l]).reshape(B, S, MLA_HEADS, MLA_NOPE + MLA_V)
        k_nope, v_mla = kv[..., :MLA_NOPE], kv[..., MLA_NOPE:]
        k_pe = apply_rope(kr, cos, sin)
        o_mla = mla_attention(q_nope, q_pe, k_nope, k_pe, v_mla).astype(x.dtype)
        o_swa = swa_attention(q_s, k_s, v_s, positions, slopes, swa_sinks[l]).astype(x.dtype)
        y = jnp.concatenate([o_mla * jax.nn.silu(g_mla), o_swa * jax.nn.silu(g_swa)], axis=-1) @ w_out[l]
        x = x + gate[:, None, :] * y
    return rmsnorm(x, final_gain)


import jax as _jax
import jax.numpy as _jnp

TWIN_FORMAT = 'train_step'
FWD_PARAMS = ['x', 'c', 'positions', 'w_ada', 'b_ada', 'norm_gain', 'w_in', 'q_norm_gain', 'kv_norm_gain', 'w_uq', 'w_ukv', 'swa_sinks', 'w_out', 'final_gain']
TWIN_WEIGHTS = ['w_ada', 'b_ada', 'norm_gain', 'w_in', 'q_norm_gain', 'kv_norm_gain', 'w_uq', 'w_ukv', 'swa_sinks', 'w_out', 'final_gain']
TWIN_DIFF_INPUT = 'x'
TWIN_INPUTS = ['x', 'c', 'positions', 'w_ada', 'b_ada', 'norm_gain', 'w_in', 'q_norm_gain', 'kv_norm_gain', 'w_uq', 'w_ukv', 'swa_sinks', 'w_out', 'final_gain', 'loss_target', 'm_w_ada', 'm_b_ada', 'm_norm_gain', 'm_w_in', 'm_q_norm_gain', 'm_kv_norm_gain', 'm_w_uq', 'm_w_ukv', 'm_swa_sinks', 'm_w_out', 'm_final_gain', 'v_w_ada', 'v_b_ada', 'v_norm_gain', 'v_w_in', 'v_q_norm_gain', 'v_kv_norm_gain', 'v_w_uq', 'v_w_ukv', 'v_swa_sinks', 'v_w_out', 'v_final_gain']
TWIN_OUTPUTS = ['loss', 'grad_x', 'grad_w_ada', 'grad_b_ada', 'grad_norm_gain', 'grad_w_in', 'grad_q_norm_gain', 'grad_kv_norm_gain', 'grad_w_uq', 'grad_w_ukv', 'grad_swa_sinks', 'grad_w_out', 'grad_final_gain', 'delta_w_ada', 'delta_b_ada', 'delta_norm_gain', 'delta_w_in', 'delta_q_norm_gain', 'delta_kv_norm_gain', 'delta_w_uq', 'delta_w_ukv', 'delta_swa_sinks', 'delta_w_out', 'delta_final_gain', 'new_m_w_ada', 'new_m_b_ada', 'new_m_norm_gain', 'new_m_w_in', 'new_m_q_norm_gain', 'new_m_kv_norm_gain', 'new_m_w_uq', 'new_m_w_ukv', 'new_m_swa_sinks', 'new_m_w_out', 'new_m_final_gain', 'new_v_w_ada', 'new_v_b_ada', 'new_v_norm_gain', 'new_v_w_in', 'new_v_q_norm_gain', 'new_v_kv_norm_gain', 'new_v_w_uq', 'new_v_w_ukv', 'new_v_swa_sinks', 'new_v_w_out', 'new_v_final_gain']
TWIN_LEAF_KINDS = {'loss': 'loss', 'grad_x': 'grad_x', 'grad_w_ada': 'grad_w', 'grad_b_ada': 'grad_w', 'grad_norm_gain': 'grad_w', 'grad_w_in': 'grad_w', 'grad_q_norm_gain': 'grad_w', 'grad_kv_norm_gain': 'grad_w', 'grad_w_uq': 'grad_w', 'grad_w_ukv': 'grad_w', 'grad_swa_sinks': 'grad_w', 'grad_w_out': 'grad_w', 'grad_final_gain': 'grad_w', 'delta_w_ada': 'delta_w', 'delta_b_ada': 'delta_w', 'delta_norm_gain': 'delta_w', 'delta_w_in': 'delta_w', 'delta_q_norm_gain': 'delta_w', 'delta_kv_norm_gain': 'delta_w', 'delta_w_uq': 'delta_w', 'delta_w_ukv': 'delta_w', 'delta_swa_sinks': 'delta_w', 'delta_w_out': 'delta_w', 'delta_final_gain': 'delta_w', 'new_m_w_ada': 'new_m', 'new_m_b_ada': 'new_m', 'new_m_norm_gain': 'new_m', 'new_m_w_in': 'new_m', 'new_m_q_norm_gain': 'new_m', 'new_m_kv_norm_gain': 'new_m', 'new_m_w_uq': 'new_m', 'new_m_w_ukv': 'new_m', 'new_m_swa_sinks': 'new_m', 'new_m_w_out': 'new_m', 'new_m_final_gain': 'new_m', 'new_v_w_ada': 'new_v', 'new_v_b_ada': 'new_v', 'new_v_norm_gain': 'new_v', 'new_v_w_in': 'new_v', 'new_v_q_norm_gain': 'new_v', 'new_v_kv_norm_gain': 'new_v', 'new_v_w_uq': 'new_v', 'new_v_w_ukv': 'new_v', 'new_v_swa_sinks': 'new_v', 'new_v_w_out': 'new_v', 'new_v_final_gain': 'new_v'}


def _forward(args):
    return _fwd_reference(*[args[k] for k in FWD_PARAMS])


def _output_shape():
    out = _jax.eval_shape(lambda: _forward(_fwd_setup_inputs(0)))
    return out.shape, out.dtype

N_MICROBATCH = 1
ADAM_LR = 0.001
ADAM_B1 = 0.9
ADAM_B2 = 0.999
ADAM_EPS = 1e-08
ADAM_WD = 0.01
ADAM_STEP = 10
PER_EXAMPLE_BATCH_AXIS = {'x': 0, 'c': 0, 'positions': 0, 'loss_target': 0}
SHARED_INPUTS = []
_WEIGHT_DTYPES = {'w_ada': _jnp.float32, 'b_ada': _jnp.float32, 'norm_gain': _jnp.float32, 'w_in': _jnp.float32, 'q_norm_gain': _jnp.float32, 'kv_norm_gain': _jnp.float32, 'w_uq': _jnp.float32, 'w_ukv': _jnp.float32, 'swa_sinks': _jnp.float32, 'w_out': _jnp.float32, 'final_gain': _jnp.float32}
MOMENT_SCALE = {'w_ada': 8.201775e-02, 'b_ada': 1.345800e-01, 'norm_gain': 1.040184e-01, 'w_in': 8.288441e-02, 'q_norm_gain': 2.350463e-02, 'kv_norm_gain': 9.633602e-02, 'w_uq': 1.666570e-02, 'w_ukv': 4.561222e-02, 'swa_sinks': 8.989145e-02, 'w_out': 7.790294e-02, 'final_gain': 6.408235e+01}


def _to_microbatches(a, axis):
    t = _jnp.moveaxis(a, axis, 0)
    t = t.reshape((N_MICROBATCH, t.shape[0] // N_MICROBATCH) + t.shape[1:])
    return _jnp.moveaxis(t, 1, axis + 1)


def setup_inputs(seed: int = 0) -> dict:
    inp = _fwd_setup_inputs(seed)
    key = _jax.random.fold_in(_jax.random.key(seed), 7919)
    shape, _ = _output_shape()
    out = dict(inp)
    out["loss_target"] = _jax.random.normal(_jax.random.fold_in(key, 0), shape, _jnp.float32)
    for i, name in enumerate(TWIN_WEIGHTS):
        w = inp[name].astype(_jnp.float32)
        if MOMENT_SCALE is None:
            s = _jnp.sqrt(_jnp.mean(_jnp.square(w)) + 1e-30)
        else:
            s = MOMENT_SCALE[name]
        km, kv = _jax.random.split(_jax.random.fold_in(key, i + 1))
        out[name] = w
        out["m_" + name] = s * _jax.random.normal(km, w.shape, _jnp.float32)
        out["v_" + name] = (s * s) * _jax.random.uniform(kv, w.shape, _jnp.float32, 0.5, 1.5)
    if N_MICROBATCH > 1:
        for name, axis in PER_EXAMPLE_BATCH_AXIS.items():
            out[name] = _to_microbatches(out[name], axis)
    return {'x': out['x'], 'c': out['c'], 'positions': out['positions'], 'w_ada': out['w_ada'], 'b_ada': out['b_ada'], 'norm_gain': out['norm_gain'], 'w_in': out['w_in'], 'q_norm_gain': out['q_norm_gain'], 'kv_norm_gain': out['kv_norm_gain'], 'w_uq': out['w_uq'], 'w_ukv': out['w_ukv'], 'swa_sinks': out['swa_sinks'], 'w_out': out['w_out'], 'final_gain': out['final_gain'], 'loss_target': out['loss_target'], 'm_w_ada': out['m_w_ada'], 'm_b_ada': out['m_b_ada'], 'm_norm_gain': out['m_norm_gain'], 'm_w_in': out['m_w_in'], 'm_q_norm_gain': out['m_q_norm_gain'], 'm_kv_norm_gain': out['m_kv_norm_gain'], 'm_w_uq': out['m_w_uq'], 'm_w_ukv': out['m_w_ukv'], 'm_swa_sinks': out['m_swa_sinks'], 'm_w_out': out['m_w_out'], 'm_final_gain': out['m_final_gain'], 'v_w_ada': out['v_w_ada'], 'v_b_ada': out['v_b_ada'], 'v_norm_gain': out['v_norm_gain'], 'v_w_in': out['v_w_in'], 'v_q_norm_gain': out['v_q_norm_gain'], 'v_kv_norm_gain': out['v_kv_norm_gain'], 'v_w_uq': out['v_w_uq'], 'v_w_ukv': out['v_w_ukv'], 'v_swa_sinks': out['v_swa_sinks'], 'v_w_out': out['v_w_out'], 'v_final_gain': out['v_final_gain']}


def _loss(weights, diff, rest, loss_target):
    with _jax.named_scope("forward"):
        args = {**rest, TWIN_DIFF_INPUT: diff, **{k: w.astype(_WEIGHT_DTYPES[k]) for k, w in weights.items()}}
        y = _forward(args)
    with _jax.named_scope("loss_head"):
        err = _jnp.square(y.astype(_jnp.float32) - loss_target)
        return 0.5 * _jnp.sum(_jnp.mean(err, axis=-1)) if err.ndim else 0.5 * err


def _adamw(w, g, m, v):
    m = ADAM_B1 * m + (1.0 - ADAM_B1) * g
    v = ADAM_B2 * v + (1.0 - ADAM_B2) * _jnp.square(g)
    m_hat = m / (1.0 - ADAM_B1 ** ADAM_STEP)
    v_hat = v / (1.0 - ADAM_B2 ** ADAM_STEP)
    delta = -ADAM_LR * (m_hat / (_jnp.sqrt(v_hat) + ADAM_EPS) + ADAM_WD * w)
    return delta, m, v


def reference(x, c, positions, w_ada, b_ada, norm_gain, w_in, q_norm_gain, kv_norm_gain, w_uq, w_ukv, swa_sinks, w_out, final_gain, loss_target, m_w_ada, m_b_ada, m_norm_gain, m_w_in, m_q_norm_gain, m_kv_norm_gain, m_w_uq, m_w_ukv, m_swa_sinks, m_w_out, m_final_gain, v_w_ada, v_b_ada, v_norm_gain, v_w_in, v_q_norm_gain, v_kv_norm_gain, v_w_uq, v_w_ukv, v_swa_sinks, v_w_out, v_final_gain):
    given = dict(x=x, c=c, positions=positions, w_ada=w_ada, b_ada=b_ada, norm_gain=norm_gain, w_in=w_in, q_norm_gain=q_norm_gain, kv_norm_gain=kv_norm_gain, w_uq=w_uq, w_ukv=w_ukv, swa_sinks=swa_sinks, w_out=w_out, final_gain=final_gain, loss_target=loss_target, m_w_ada=m_w_ada, m_b_ada=m_b_ada, m_norm_gain=m_norm_gain, m_w_in=m_w_in, m_q_norm_gain=m_q_norm_gain, m_kv_norm_gain=m_kv_norm_gain, m_w_uq=m_w_uq, m_w_ukv=m_w_ukv, m_swa_sinks=m_swa_sinks, m_w_out=m_w_out, m_final_gain=m_final_gain, v_w_ada=v_w_ada, v_b_ada=v_b_ada, v_norm_gain=v_norm_gain, v_w_in=v_w_in, v_q_norm_gain=v_q_norm_gain, v_kv_norm_gain=v_kv_norm_gain, v_w_uq=v_w_uq, v_w_ukv=v_w_ukv, v_swa_sinks=v_swa_sinks, v_w_out=v_w_out, v_final_gain=v_final_gain)
    weights = {n: given[n] for n in TWIN_WEIGHTS}
    shared = {n: given[n] for n in SHARED_INPUTS}
    per_example = {n: given[n] for n in ['x', 'c', 'positions']}
    grad_fn = _jax.value_and_grad(_loss, argnums=(0, 1))

    def one_microbatch(ex, loss_target):
        ex = dict(ex)
        diff = ex.pop(TWIN_DIFF_INPUT)
        return grad_fn(weights, diff, {**shared, **ex}, loss_target)

    if N_MICROBATCH == 1:
        loss, (grad_w, grad_x) = one_microbatch(per_example, given["loss_target"])
    else:
        def body(carry, xs):
            loss_sum, grad_sum = carry
            l_k, (gw_k, gx_k) = one_microbatch(xs[0], xs[1])
            with _jax.named_scope("update"):
                return (loss_sum + l_k, _jax.tree.map(_jnp.add, grad_sum, gw_k)), gx_k

        init = (_jnp.zeros((), _jnp.float32), _jax.tree.map(_jnp.zeros_like, weights))
        (loss, grad_w), grad_x = _jax.lax.scan(body, init, (per_example, given["loss_target"]))
    with _jax.named_scope("update"):
        delta_w, new_m, new_v = {}, {}, {}
        for n in TWIN_WEIGHTS:
            delta_w[n], new_m[n], new_v[n] = _adamw(weights[n], grad_w[n], given["m_" + n], given["v_" + n])
    return (loss, grad_x, *[grad_w[n] for n in TWIN_WEIGHTS], *[delta_w[n] for n in TWIN_WEIGHTS],
            *[new_m[n] for n in TWIN_WEIGHTS], *[new_v[n] for n in TWIN_WEIGHTS])
```

```python
import functools

import numpy as np
import jax
import jax.numpy as jnp
from jax import lax
from jax.experimental import pallas as pl
from jax.experimental.pallas import tpu as pltpu

F32 = jnp.float32
_MXU_DTYPE = jnp.bfloat16

N_DEV = 8
D_MODEL = 1024
MLA_HEADS = 8
MLA_NOPE = 64
MLA_ROPE = 32
MLA_V = 64
Q_LORA = 384
KV_LORA = 256
SWA_HEADS = 8
SWA_KV_HEADS = 2
SWA_HEAD_DIM = 64
WINDOW = 128
ROPE_THETA = 10000.0
EPS = 1e-6
MLA_SCALE = float((MLA_NOPE + MLA_ROPE) ** -0.5)
SWA_SCALE = float(SWA_HEAD_DIM ** -0.5)
D_IN = 2464
D_IN_PAD = 2560
PAIR_ORDER = (0, 4, 1, 5, 2, 6, 3, 7)
PAIR_INV = (0, 2, 4, 6, 1, 3, 5, 7)

ADAM_LR = 0.001
ADAM_B1 = 0.9
ADAM_B2 = 0.999
ADAM_EPS = 1e-08
ADAM_WD = 0.01
ADAM_STEP = 10

LANE = 128
VMEM_LIMIT = 56 * 1024 * 1024

MESH = pl.DeviceIdType.MESH
NEG_INF = float("-inf")


def _mx(a):
    return a.astype(_MXU_DTYPE)


def _dot(a, b):
    return jnp.dot(a, b, preferred_element_type=F32)


def _dot_nt(a, b):
    return lax.dot_general(a, b, (((1,), (1,)), ((), ())), preferred_element_type=F32)


def _dot_tn(a, b):
    return lax.dot_general(a, b, (((0,), (0,)), ((), ())), preferred_element_type=F32)


def _cparams(sem=None):
    return pltpu.CompilerParams(dimension_semantics=sem, vmem_limit_bytes=VMEM_LIMIT)


def _vmem():
    return pl.BlockSpec(memory_space=pltpu.VMEM)


def _lane_iota(shape):
    return lax.broadcasted_iota(jnp.int32, shape, len(shape) - 1)


def _all_gather(arrs, name):
    n = len(arrs)

    def body(*refs):
        ins, outs = refs[:n], refs[n:2 * n]
        send_sems, recv_sems, local_sems = refs[2 * n:]
        x, y, c = lax.axis_index("x"), lax.axis_index("y"), lax.axis_index("c")
        me, sibling = (x, y, c), (x, y, 1 - c)
        chips = [(1 - x, y), (x, 1 - y), (1 - x, 1 - y)]

        def slot(a, dev):
            return outs[a].at[4 * dev[0] + 2 * dev[1] + dev[2]]

        def copy(a, k, block, to, src=None):
            return pltpu.make_async_remote_copy(
                src_ref=slot(a, block) if src is None else src, dst_ref=slot(a, block),
                send_sem=send_sems.at[7 * a + k], recv_sem=recv_sems.at[7 * a + k],
                device_id=to, device_id_type=MESH)

        mine = [pltpu.make_async_copy(ins[a], slot(a, me), local_sems.at[a]) for a in range(n)]
        for cp in mine:
            cp.start()
        first = []
        for a in range(n):
            first.append(copy(a, 0, me, sibling, src=ins[a]))
            first += [copy(a, 1 + j, me, (*chip, c), src=ins[a]) for j, chip in enumerate(chips)]
        for cp in first:
            cp.start()
        passed = []
        for j, chip in enumerate(chips):
            for a in range(n):
                copy(a, 1 + j, (*chip, c), me).wait_recv()
                cp = copy(a, 4 + j, (*chip, c), sibling)
                cp.start()
                passed.append(cp)
        for a in range(n):
            copy(a, 0, sibling, me).wait_recv()
            for j, chip in enumerate(chips):
                copy(a, 4 + j, (*chip, 1 - c), me).wait_recv()
        for cp in first + passed:
            cp.wait_send()
        for cp in mine:
            cp.wait()

    return pl.pallas_call(
        body, name=name,
        out_shape=[jax.ShapeDtypeStruct((N_DEV,) + a.shape, a.dtype) for a in arrs],
        in_specs=[_vmem()] * n, out_specs=[_vmem()] * n,
        scratch_shapes=[pltpu.SemaphoreType.DMA((7 * n,)), pltpu.SemaphoreType.DMA((7 * n,)),
                        pltpu.SemaphoreType.DMA((n,))],
        compiler_params=pltpu.CompilerParams(vmem_limit_bytes=VMEM_LIMIT),
    )(*arrs)


def _reduce_scatter(arrs, name):
    n = len(arrs)

    def body(*refs):
        ins, outs = refs[:n], refs[n:2 * n]
        recvs = refs[2 * n:3 * n]
        send_sems, recv_sems, local_sems = refs[3 * n:]
        x, y, c = lax.axis_index("x"), lax.axis_index("y"), lax.axis_index("c")
        me = 4 * x + 2 * y + c

        def peer(k):
            px = 1 - x if (k >> 2) & 1 else x
            py = 1 - y if (k >> 1) & 1 else y
            pc = 1 - c if k & 1 else c
            return (px, py, pc)

        def copy(a, k):
            p = peer(k)
            return pltpu.make_async_remote_copy(
                src_ref=ins[a].at[4 * p[0] + 2 * p[1] + p[2]], dst_ref=recvs[a].at[me],
                send_sem=send_sems.at[7 * a + k - 1], recv_sem=recv_sems.at[7 * a + k - 1],
                device_id=p, device_id_type=MESH)

        mine = [pltpu.make_async_copy(ins[a].at[me], recvs[a].at[me], local_sems.at[a]) for a in range(n)]
        for cp in mine:
            cp.start()
        sends = [copy(a, k) for a in range(n) for k in range(1, N_DEV)]
        for cp in sends:
            cp.start()
        for cp in mine:
            cp.wait()
        for cp in sends:
            cp.wait_recv()
        for a in range(n):
            acc = recvs[a][0]
            for j in range(1, N_DEV):
                acc = acc + recvs[a][j]
            outs[a][...] = acc
        for cp in sends:
            cp.wait_send()

    return pl.pallas_call(
        body, name=name,
        out_shape=[jax.ShapeDtypeStruct(a.shape[1:], a.dtype) for a in arrs],
        in_specs=[pl.BlockSpec(memory_space=pl.ANY)] * n, out_specs=[_vmem()] * n,
        scratch_shapes=[pltpu.VMEM(a.shape, a.dtype) for a in arrs]
        + [pltpu.SemaphoreType.DMA((7 * n,)), pltpu.SemaphoreType.DMA((7 * n,)), pltpu.SemaphoreType.DMA((n,))],
        compiler_params=pltpu.CompilerParams(vmem_limit_bytes=VMEM_LIMIT),
    )(*arrs)


def _silu(t):
    return t * (1.0 / (1.0 + jnp.exp(-t)))


def _ada_fwd(c_act_all, w_ada_shard, b_shard):
    def body(c_ref, w_ref, b_ref, o_ref):
        o_ref[...] = _dot(_mx(_silu(c_ref[...])), _mx(w_ref[...])) + b_ref[...]

    return pl.pallas_call(
        body, name="ada_fwd",
        out_shape=jax.ShapeDtypeStruct((c_act_all.shape[0], w_ada_shard.shape[1]), F32),
        in_specs=[_vmem()] * 3, out_specs=_vmem(),
        compiler_params=_cparams(),
    )(c_act_all, w_ada_shard, b_shard)


def _rope_tables_tiled(cq, sa, sb):
    return (jnp.concatenate([cq] * MLA_HEADS, axis=1), jnp.concatenate([sa] * MLA_HEADS, axis=1),
            jnp.concatenate([sb] * MLA_HEADS, axis=1))


def _pre_fwd(x2, shift, scale, ng, w_in, gq, gkv, w_uq, w_uk, w_uv, cosq, cosk, sina, sinb, S, ts):
    T, D = x2.shape
    nsb = S // ts
    WQ = MLA_HEADS * LANE

    def body(x_ref, sh_ref, sc_ref, ng_ref, w_ref, gq_ref, gkv_ref, wuq_ref, wuk_ref, wuv_ref,
             cq_ref, ck_ref, sa_ref, sb_ref,
             hb_ref, zq_ref, zkv_ref, ql_ref, kvl_ref, q_ref, k_ref, v_ref, qs_ref, ks_ref, vs_ref, g_ref):
        x = x_ref[...]
        r1 = lax.rsqrt(jnp.mean(x * x, axis=-1, keepdims=True) + EPS)
        h = ((x * r1) * ng_ref[...]) * (1.0 + sc_ref[0]) + sh_ref[0]
        hb = _mx(h)
        hb_ref[...] = hb
        zq = _dot(hb, w_ref[:, 0:384])
        zq_ref[...] = zq
        rq = lax.rsqrt(jnp.mean(zq * zq, axis=-1, keepdims=True) + EPS)
        ql = _mx((zq * rq) * gq_ref[...])
        ql_ref[...] = ql
        q = _dot(ql, wuq_ref[...])
        cq, sa, sb = _rope_tables_tiled(cq_ref[...], sa_ref[...], sb_ref[...])
        q = q * cq + pltpu.roll(q, WQ - 16, 1) * sa + pltpu.roll(q, 16, 1) * sb
        q_ref[...] = _mx(q)
        zkv = _dot(hb, w_ref[:, 384:640])
        zkv_ref[...] = zkv
        rkv = lax.rsqrt(jnp.mean(zkv * zkv, axis=-1, keepdims=True) + EPS)
        kvl = _mx((zkv * rkv) * gkv_ref[...])
        kvl_ref[...] = kvl
        kr = _dot(hb, w_ref[:, 640:768])
        kpe = kr * ck_ref[...] + pltpu.roll(kr, LANE - 16, 1) * sa_ref[...] + pltpu.roll(kr, 16, 1) * sb_ref[...]
        kf = _dot(kvl, wuk_ref[...])
        k_ref[...] = _mx(kf + jnp.concatenate([kpe] * MLA_HEADS, axis=1))
        v_ref[...] = _mx(_dot(kvl, wuv_ref[...]))
        g_ref[:, 0:512] = _dot(hb, w_ref[:, 768:1280])
        qs_ref[...] = _mx(_dot(hb, w_ref[:, 1280:1792]) * SWA_SCALE)
        ks_ref[...] = _mx(_dot(hb, w_ref[:, 1792:1920]))
        vs_ref[...] = _mx(_dot(hb, w_ref[:, 1920:2048]))
        g_ref[:, 512:1024] = _dot(hb, w_ref[:, 2048:2560])

    row = lambda w: pl.BlockSpec((ts, w), lambda i: (i, 0))
    full = lambda a: pl.BlockSpec(a.shape, lambda i: (0,) * a.ndim)
    per_b = pl.BlockSpec((1, 1, D), lambda i: (i // nsb, 0, 0))
    out_w = [(D, _MXU_DTYPE), (384, F32), (256, F32), (384, _MXU_DTYPE), (256, _MXU_DTYPE), (WQ, _MXU_DTYPE),
             (WQ, _MXU_DTYPE), (512, _MXU_DTYPE), (512, _MXU_DTYPE), (128, _MXU_DTYPE), (128, _MXU_DTYPE), (1024, F32)]
    return pl.pallas_call(
        body, name="pre_fwd", grid=(T // ts,),
        out_shape=[jax.ShapeDtypeStruct((T, w), dt) for w, dt in out_w],
        in_specs=[row(D), per_b, per_b, full(ng), full(w_in), full(gq), full(gkv), full(w_uq), full(w_uk), full(w_uv),
                  row(LANE), row(LANE), row(LANE), row(LANE)],
        out_specs=[row(w) for w, _ in out_w],
        compiler_params=_cparams(("arbitrary",)),
    )(x2, shift, scale, ng, w_in, gq, gkv, w_uq, w_uk, w_uv, cosq, cosk, sina, sinb)


def _mla_fwd(q3, k3, v3, tq):
    B, S, _ = q3.shape
    nq = S // tq
    tk = tq

    def body(q_ref, k_ref, v_ref, o_ref, lse_ref):
        qi = pl.program_id(2)
        rows = lax.broadcasted_iota(jnp.int32, (tq, tk), 0)
        cols = lax.broadcasted_iota(jnp.int32, (tq, tk), 1)
        res = []
        for j in range(2):
            lanes = slice(LANE * j, LANE * (j + 1))
            q = q_ref[0, :, lanes]

            def step(kt, carry, masked, q=q, lanes=lanes):
                m, l, acc = carry
                r0 = pl.multiple_of(kt * tk, tk)
                s = _dot_nt(q, k_ref[0, pl.ds(r0, tk), lanes]) * MLA_SCALE
                if masked:
                    s = jnp.where(rows >= cols, s, NEG_INF)
                m_new = jnp.maximum(m, jnp.max(s, axis=1, keepdims=True))
                alpha = jnp.exp(m - m_new)
                p = jnp.exp(s - m_new)
                l = alpha * l + jnp.sum(p, axis=1, keepdims=True)
                acc = alpha * acc + _dot(_mx(p), v_ref[0, pl.ds(r0, tk), :])
                return m_new, l, acc

            init = (jnp.full((tq, 1), NEG_INF, F32), jnp.zeros((tq, 1), F32), jnp.zeros((tq, LANE), F32))
            carry = lax.fori_loop(0, qi, functools.partial(step, masked=False), init)
            m, l, acc = step(qi, carry, True)
            res.append((m + jnp.log(l), acc / l))
        low = _lane_iota((tq, LANE)) < 64
        o_ref[0] = jnp.where(low, res[0][1], res[1][1])
        lse_ref[0, 0] = jnp.where(low, res[0][0], res[1][0])

    return pl.pallas_call(
        body, name="mla_fwd", grid=(B, MLA_HEADS // 2, nq),
        out_shape=[jax.ShapeDtypeStruct((B, S, 512), F32), jax.ShapeDtypeStruct((B, MLA_HEADS // 2, S, LANE), F32)],
        in_specs=[pl.BlockSpec((1, tq, 2 * LANE), lambda b, hp, i: (b, i, hp)),
                  pl.BlockSpec((1, S, 2 * LANE), lambda b, hp, i: (b, 0, hp)),
                  pl.BlockSpec((1, S, LANE), lambda b, hp, i: (b, 0, hp))],
        out_specs=[pl.BlockSpec((1, tq, LANE), lambda b, hp, i: (b, i, hp)),
                   pl.BlockSpec((1, 1, tq, LANE), lambda b, hp, i: (b, hp, i, 0))],
        compiler_params=_cparams(("arbitrary", "arbitrary", "arbitrary")),
    )(q3, k3, v3)


def _mla_bwd(q3, k3, v3, o3, do3, lse, tq):
    B, S, _ = q3.shape
    nq = S // tq
    tk = tq

    def body(q_ref, k_ref, v_ref, o_ref, do_ref, lse_ref, dq_ref, dk_ref, dv_ref):
        dk_ref[...] = jnp.zeros_like(dk_ref)
        dv_ref[...] = jnp.zeros_like(dv_ref)
        rows = lax.broadcasted_iota(jnp.int32, (tq, tk), 0)
        cols = lax.broadcasted_iota(jnp.int32, (tq, tk), 1)
        lane = _lane_iota((tq, LANE))
        for j in range(2):
            lanes = slice(LANE * j, LANE * (j + 1))
            mine = (lane < 64) if j == 0 else (lane >= 64)

            def q_tile(qi, _, lanes=lanes, mine=mine, j=j):
                r = pl.ds(pl.multiple_of(qi * tq, tq), tq)
                q = q_ref[0, r, lanes]
                do = jnp.where(mine, do_ref[0, r, :], 0.0)
                dcol = jnp.sum(do * o_ref[0, r, :], axis=1, keepdims=True)
                dob = _mx(do)
                lse_c = lse_ref[0, 0, r, :][:, 64 * j:64 * j + 1]

                def k_tile(kt, dq, masked):
                    kr = pl.ds(pl.multiple_of(kt * tk, tk), tk)
                    k = k_ref[0, kr, lanes]
                    s = _dot_nt(q, k) * MLA_SCALE
                    if masked:
                        s = jnp.where(rows >= cols, s, NEG_INF)
                    p = jnp.exp(s - lse_c)
                    dp = _dot_nt(dob, v_ref[0, kr, :])
                    dsb = _mx(p * (dp - dcol) * MLA_SCALE)
                    dk_ref[0, kr, lanes] += _dot_tn(dsb, q)
                    dv_ref[0, kr, :] += _dot_tn(_mx(p), dob)
                    return dq + _dot(dsb, k)

                dq = lax.fori_loop(0, qi, functools.partial(k_tile, masked=False), jnp.zeros((tq, LANE), F32))
                dq_ref[0, r, lanes] = k_tile(qi, dq, True)
                return 0

            lax.fori_loop(0, nq, q_tile, 0)

    pair = lambda w: pl.BlockSpec((1, S, w), lambda b, hp: (b, 0, hp))
    return pl.pallas_call(
        body, name="mla_bwd", grid=(B, MLA_HEADS // 2),
        out_shape=[jax.ShapeDtypeStruct((B, S, 1024), F32), jax.ShapeDtypeStruct((B, S, 1024), F32),
                   jax.ShapeDtypeStruct((B, S, 512), F32)],
        in_specs=[pair(2 * LANE), pair(2 * LANE), pair(LANE), pair(LANE), pair(LANE),
                  pl.BlockSpec((1, 1, S, LANE), lambda b, hp: (b, hp, 0, 0))],
        out_specs=[pair(2 * LANE), pair(2 * LANE), pair(LANE)],
        compiler_params=_cparams(("arbitrary", "arbitrary")),
    )(q3, k3, v3, o3, do3, lse)


def _swa_scores(q4, k2, pq, pk, n, k0, sink_ref):
    W = WINDOW
    lane = _lane_iota((W, LANE))
    ti = n * W + lax.broadcasted_iota(jnp.int32, (W, 2 * W), 0)
    sj = k0 + lax.broadcasted_iota(jnp.int32, (W, 2 * W), 1)
    rel = ti - sj
    valid = jnp.logical_and(rel >= 0, rel < W)
    dist = pq - pk
    out = []
    for g in range(SWA_KV_HEADS):
        mine = (lane < 64) if g == 0 else (lane >= 64)
        qst = jnp.concatenate([jnp.where(mine, q4[:, LANE * p:LANE * (p + 1)], 0) for p in range(4)], axis=0)
        s = _dot_nt(qst, k2)
        heads = []
        for p in range(4):
            h = p + 4 * g
            slope = float(2.0 ** (-8.0 * (h + 1) / SWA_HEADS))
            sh = jnp.where(valid, s[W * p:W * (p + 1)] - slope * dist, NEG_INF)
            heads.append((h, sh, sink_ref[:, h:h + 1]))
        out.append((mine, qst, heads))
    return out


def _swa_fwd(qs3, ks3, vs3, posb, posr, sinks):
    B, S, _ = qs3.shape
    W = WINDOW
    nb = S // W

    def body(q_ref, k_ref, v_ref, pb_ref, pr_ref, sink_ref, o_ref, lse_ref):
        lane = _lane_iota((W, LANE))

        def blk(n, _):
            kb = jnp.maximum(n - 1, 0)
            r = pl.ds(pl.multiple_of(n * W, W), W)
            kr = pl.ds(pl.multiple_of(kb * W, W), 2 * W)
            q4, k2, v2 = q_ref[0, r, :], k_ref[0, kr, :], v_ref[0, kr, :]
            pq = pb_ref[0, r, :][:, 0:1]
            pk = jnp.concatenate([pr_ref[0, pl.ds(kb, 1), :], pr_ref[0, pl.ds(kb + 1, 1), :]], axis=1)
            groups = _swa_scores(q4, k2, pq, pk, n, kb * W, sink_ref)
            lse_t = jnp.zeros((W, LANE), F32)
            o_g = []
            for mine, _, heads in groups:
                ps = []
                for h, sh, sink in heads:
                    m = jnp.maximum(jnp.max(sh, axis=1, keepdims=True), sink)
                    e = jnp.exp(sh - m)
                    l = jnp.sum(e, axis=1, keepdims=True) + jnp.exp(sink - m)
                    ps.append(_mx(e / l))
                    lse_t = jnp.where(lane == h, m + jnp.log(l), lse_t)
                o_g.append(_dot(jnp.concatenate(ps, axis=0), v2))
            for p in range(4):
                o_ref[0, r, LANE * p:LANE * (p + 1)] = jnp.where(lane < 64, o_g[0][W * p:W * (p + 1)],
                                                                 o_g[1][W * p:W * (p + 1)])
            lse_ref[0, r, :] = lse_t
            return 0

        lax.fori_loop(0, nb, blk, 0)

    seq = lambda w: pl.BlockSpec((1, S, w), lambda b: (b, 0, 0))
    return pl.pallas_call(
        body, name="swa_fwd", grid=(B,),
        out_shape=[jax.ShapeDtypeStruct((B, S, 512), F32), jax.ShapeDtypeStruct((B, S, LANE), F32)],
        in_specs=[seq(512), seq(LANE), seq(LANE), seq(LANE), pl.BlockSpec((1, nb, W), lambda b: (b, 0, 0)),
                  pl.BlockSpec((1, LANE), lambda b: (0, 0))],
        out_specs=[seq(512), seq(LANE)],
        compiler_params=_cparams(("arbitrary",)),
    )(qs3, ks3, vs3, posb, posr, sinks)


def _swa_bwd(qs3, ks3, vs3, posb, posr, sinks, os3, do3, lse):
    B, S, _ = qs3.shape
    W = WINDOW
    nb = S // W

    def body(q_ref, k_ref, v_ref, pb_ref, pr_ref, sink_ref, o_ref, do_ref, lse_ref, dq_ref, dk_ref, dv_ref, dsink_ref):
        lane = _lane_iota((W, LANE))
        dk_ref[...] = jnp.zeros_like(dk_ref)
        dv_ref[...] = jnp.zeros_like(dv_ref)

        @pl.when(pl.program_id(0) == 0)
        def _():
            dsink_ref[...] = jnp.zeros_like(dsink_ref)

        def blk(n, dsink):
            kb = jnp.maximum(n - 1, 0)
            r = pl.ds(pl.multiple_of(n * W, W), W)
            kr = pl.ds(pl.multiple_of(kb * W, W), 2 * W)
            q4, k2, v2 = q_ref[0, r, :], k_ref[0, kr, :], v_ref[0, kr, :]
            pq = pb_ref[0, r, :][:, 0:1]
            pk = jnp.concatenate([pr_ref[0, pl.ds(kb, 1), :], pr_ref[0, pl.ds(kb + 1, 1), :]], axis=1)
            groups = _swa_scores(q4, k2, pq, pk, n, kb * W, sink_ref)
            lse_t = lse_ref[0, r, :]
            dq_g = []
            dk2 = jnp.zeros((2 * W, LANE), F32)
            dv2 = jnp.zeros((2 * W, LANE), F32)
            for mine, qst, heads in groups:
                do_l = [jnp.where(mine, do_ref[0, r, LANE * p:LANE * (p + 1)], 0.0) for p in range(4)]
                dost = _mx(jnp.concatenate(do_l, axis=0))
                dp = _dot_nt(dost, v2)
                ps, dss = [], []
                for p, (h, sh, sink) in enumerate(heads):
                    lse_c = lse_t[:, h:h + 1]
                    pr = jnp.exp(sh - lse_c)
                    dcol = jnp.sum(do_l[p] * o_ref[0, r, LANE * p:LANE * (p + 1)], axis=1, keepdims=True)
                    ds = pr * (dp[W * p:W * (p + 1)] - dcol)
                    psink = jnp.exp(sink - lse_c)
                    dsink = dsink - jnp.where(lane[0:1] == h, jnp.sum(psink * dcol, axis=0, keepdims=True), 0.0)
                    ps.append(_mx(pr))
                    dss.append(_mx(ds))
                pst, dst = jnp.concatenate(ps, axis=0), jnp.concatenate(dss, axis=0)
                dq_g.append(_dot(dst, k2))
                dk2 = dk2 + _dot_tn(dst, qst)
                dv2 = dv2 + _dot_tn(pst, dost)
            for p in range(4):
                dq_ref[0, r, LANE * p:LANE * (p + 1)] = SWA_SCALE * jnp.where(
                    lane < 64, dq_g[0][W * p:W * (p + 1)], dq_g[1][W * p:W * (p + 1)])
            dk_ref[0, kr, :] += dk2
            dv_ref[0, kr, :] += dv2
            return dsink

        dsink_ref[...] += lax.fori_loop(0, nb, blk, jnp.zeros((1, LANE), F32))

    seq = lambda w: pl.BlockSpec((1, S, w), lambda b: (b, 0, 0))
    return pl.pallas_call(
        body, name="swa_bwd", grid=(B,),
        out_shape=[jax.ShapeDtypeStruct((B, S, 512), F32), jax.ShapeDtypeStruct((B, S, LANE), F32),
                   jax.ShapeDtypeStruct((B, S, LANE), F32), jax.ShapeDtypeStruct((1, LANE), F32)],
        in_specs=[seq(512), seq(LANE), seq(LANE), seq(LANE), pl.BlockSpec((1, nb, W), lambda b: (b, 0, 0)),
                  pl.BlockSpec((1, LANE), lambda b: (0, 0)), seq(512),
                  pl.BlockSpec((1, S, 512), lambda b: (b, 0, 1)), seq(LANE)],
        out_specs=[seq(512), seq(LANE), seq(LANE), pl.BlockSpec((1, LANE), lambda b: (0, 0))],
        compiler_params=_cparams(("arbitrary",)),
    )(qs3, ks3, vs3, posb, posr, sinks, os3, do3, lse)


def _post(om, osw, g, w_out, x2, gate, fg, tgt, S, ts):
    T, D = x2.shape
    nsb = S // ts

    def body(om_ref, os_ref, g_ref, w_ref, x_ref, gate_ref, fg_ref, t_ref,
             dx2_ref, ab_ref, dyb_ref, do_ref, dg_ref, loss_ref, dfg_ref, dgate_ref):
        i = pl.program_id(0)
        gv = g_ref[...]
        sg = 1.0 / (1.0 + jnp.exp(-gv))
        silu = gv * sg
        o = jnp.concatenate([om_ref[...], os_ref[...]], axis=1)
        ab = _mx(o * silu)
        ab_ref[...] = ab
        y = _dot(ab, w_ref[...])
        gate = gate_ref[0]
        xo = x_ref[...] + gate * y
        r2 = lax.rsqrt(jnp.mean(xo * xo, axis=-1, keepdims=True) + EPS)
        xh = xo * r2
        fg = fg_ref[...]
        diff = xh * fg - t_ref[...]
        sq = jnp.sum(diff * diff, axis=0, keepdims=True)
        part = sq[:, 0:LANE]
        for t in range(1, D // LANE):
            part = part + sq[:, LANE * t:LANE * (t + 1)]
        dout = diff * (1.0 / D)
        dxh = dout * fg
        dx2 = r2 * (dxh - xh * jnp.mean(dxh * xh, axis=-1, keepdims=True))
        dx2_ref[...] = dx2
        dyb = _mx(dx2 * gate)
        dyb_ref[...] = dyb
        da = _dot_nt(dyb, w_ref[...])
        do_ref[...] = da * silu
        dg_ref[...] = _mx(da * o * (sg * (1.0 + gv * (1.0 - sg))))

        @pl.when(i == 0)
        def _():
            loss_ref[...] = jnp.zeros_like(loss_ref)
            dfg_ref[...] = jnp.zeros_like(dfg_ref)

        @pl.when(i % nsb == 0)
        def _():
            dgate_ref[...] = jnp.zeros_like(dgate_ref)

        loss_ref[...] += (0.5 / D) * part
        dfg_ref[...] += jnp.sum(dout * xh, axis=0, keepdims=True)
        dgate_ref[0] += jnp.sum(dx2 * y, axis=0, keepdims=True)

    row = lambda w: pl.BlockSpec((ts, w), lambda i: (i, 0))
    full = lambda a: pl.BlockSpec(a.shape, lambda i: (0,) * a.ndim)
    per_b = pl.BlockSpec((1, 1, D), lambda i: (i // nsb, 0, 0))
    return pl.pallas_call(
        body, name="post", grid=(T // ts,),
        out_shape=[jax.ShapeDtypeStruct((T, D), F32), jax.ShapeDtypeStruct((T, D), _MXU_DTYPE),
                   jax.ShapeDtypeStruct((T, D), _MXU_DTYPE), jax.ShapeDtypeStruct((T, 1024), F32),
                   jax.ShapeDtypeStruct((T, 1024), _MXU_DTYPE), jax.ShapeDtypeStruct((1, LANE), F32),
                   jax.ShapeDtypeStruct((1, D), F32), jax.ShapeDtypeStruct(gate.shape, F32)],
        in_specs=[row(512), row(512), row(1024), full(w_out), row(D), per_b, full(fg), row(D)],
        out_specs=[row(D), row(D), row(D), row(1024), row(1024),
                   pl.BlockSpec((1, LANE), lambda i: (0, 0)), pl.BlockSpec((1, D), lambda i: (0, 0)), per_b],
        compiler_params=_cparams(("arbitrary",)),
    )(om, osw, g, w_out, x2, gate, fg, tgt)


def _pre_bwd(dq, dk, dv, dqs, dks, dvs, dg, zq, zkv, x2, dx2, scale, ng, w_in, gq, gkv, w_uq, w_uk, w_uv,
             cosq, cosk, sina, sinb, S, ts):
    T, D = x2.shape
    nsb = S // ts
    WQ = MLA_HEADS * LANE

    def body(dq_ref, dk_ref, dv_ref, dqs_ref, dks_ref, dvs_ref, dg_ref, zq_ref, zkv_ref, x_ref, dx2_ref, sc_ref,
             ng_ref, w_ref, gq_ref, gkv_ref, wuq_ref, wuk_ref, wuv_ref, cq_ref, ck_ref, sa_ref, sb_ref,
             gx_ref, dqb_ref, dkb_ref, dvb_ref, dz_ref, dgq_ref, dgkv_ref, dng_ref, dsh_ref, dsc_ref):
        i = pl.program_id(0)

        @pl.when(i == 0)
        def _():
            dgq_ref[...] = jnp.zeros_like(dgq_ref)
            dgkv_ref[...] = jnp.zeros_like(dgkv_ref)
            dng_ref[...] = jnp.zeros_like(dng_ref)

        @pl.when(i % nsb == 0)
        def _():
            dsh_ref[...] = jnp.zeros_like(dsh_ref)
            dsc_ref[...] = jnp.zeros_like(dsc_ref)

        def norm_bwd(z, dy, gain):
            r = lax.rsqrt(jnp.mean(z * z, axis=-1, keepdims=True) + EPS)
            zh = z * r
            dzh = dy * gain
            return r * (dzh - zh * jnp.mean(dzh * zh, axis=-1, keepdims=True)), jnp.sum(dy * zh, axis=0, keepdims=True)

        cq, sa, sb = _rope_tables_tiled(cq_ref[...], sa_ref[...], sb_ref[...])
        dqr = dq_ref[...]
        dqb = _mx(dqr * cq + pltpu.roll(dqr * sa, 16, 1) + pltpu.roll(dqr * sb, WQ - 16, 1))
        dqb_ref[...] = dqb
        dzq, dgq = norm_bwd(zq_ref[...], _dot_nt(dqb, wuq_ref[...]), gq_ref[...])
        dgq_ref[...] += dgq
        dkr = dk_ref[...]
        dkb = _mx(dkr)
        dkb_ref[...] = dkb
        dvb = _mx(dv_ref[...])
        dvb_ref[...] = dvb
        dzkv, dgkv = norm_bwd(zkv_ref[...], _dot_nt(dkb, wuk_ref[...]) + _dot_nt(dvb, wuv_ref[...]), gkv_ref[...])
        dgkv_ref[...] += dgkv
        dkpe = dkr[:, 0:LANE]
        for h in range(1, MLA_HEADS):
            dkpe = dkpe + dkr[:, LANE * h:LANE * (h + 1)]
        dkro = (dkpe * ck_ref[...] + pltpu.roll(dkpe * sa_ref[...], 16, 1)
                + pltpu.roll(dkpe * sb_ref[...], LANE - 16, 1))
        dz_ref[:, 0:384] = _mx(dzq)
        dz_ref[:, 384:640] = _mx(dzkv)
        dz_ref[:, 640:768] = _mx(dkro)
        dz_ref[:, 768:1280] = dg_ref[:, 0:512]
        dz_ref[:, 1280:1792] = _mx(dqs_ref[...])
        dz_ref[:, 1792:1920] = _mx(dks_ref[...])
        dz_ref[:, 1920:2048] = _mx(dvs_ref[...])
        dz_ref[:, 2048:2560] = dg_ref[:, 512:1024]
        dh = _dot_nt(dz_ref[...], w_ref[...])
        x = x_ref[...]
        r1 = lax.rsqrt(jnp.mean(x * x, axis=-1, keepdims=True) + EPS)
        xn = x * r1
        ng = ng_ref[...]
        sc1 = 1.0 + sc_ref[0]
        dsh_ref[0] += jnp.sum(dh, axis=0, keepdims=True)
        dsc_ref[0] += jnp.sum(dh * (xn * ng), axis=0, keepdims=True)
        dng_ref[...] += jnp.sum(dh * xn * sc1, axis=0, keepdims=True)
        dxn = dh * ng * sc1
        gx_ref[...] = dx2_ref[...] + r1 * (dxn - xn * jnp.mean(dxn * xn, axis=-1, keepdims=True))

    row = lambda w: pl.BlockSpec((ts, w), lambda i: (i, 0))
    full = lambda a: pl.BlockSpec(a.shape, lambda i: (0,) * a.ndim)
    per_b = pl.BlockSpec((1, 1, D), lambda i: (i // nsb, 0, 0))
    vec = lambda w: pl.BlockSpec((1, w), lambda i: (0, 0))
    return pl.pallas_call(
        body, name="pre_bwd", grid=(T // ts,),
        out_shape=[jax.ShapeDtypeStruct((T, D), F32), jax.ShapeDtypeStruct((T, WQ), _MXU_DTYPE),
                   jax.ShapeDtypeStruct((T, WQ), _MXU_DTYPE), jax.ShapeDtypeStruct((T, 512), _MXU_DTYPE),
                   jax.ShapeDtypeStruct((T, D_IN_PAD), _MXU_DTYPE), jax.ShapeDtypeStruct((1, 384), F32),
                   jax.ShapeDtypeStruct((1, 256), F32), jax.ShapeDtypeStruct((1, D), F32),
                   jax.ShapeDtypeStruct(scale.shape, F32), jax.ShapeDtypeStruct(scale.shape, F32)],
        in_specs=[row(WQ), row(WQ), row(512), row(512), row(LANE), row(LANE), row(1024), row(384), row(256), row(D),
                  row(D), per_b, full(ng), full(w_in), full(gq), full(gkv), full(w_uq), full(w_uk), full(w_uv),
                  row(LANE), row(LANE), row(LANE), row(LANE)],
        out_specs=[row(D), row(WQ), row(WQ), row(512), row(D_IN_PAD), vec(384), vec(256), vec(D), per_b, per_b],
        compiler_params=_cparams(("arbitrary",)),
    )(dq, dk, dv, dqs, dks, dvs, dg, zq, zkv, x2, dx2, scale, ng, w_in, gq, gkv, w_uq, w_uk, w_uv,
      cosq, cosk, sina, sinb)


def _tn_matmul(a, b, tn, tk, name):
    T, M = a.shape
    N = b.shape[1]

    def body(a_ref, b_ref, o_ref):
        @pl.when(pl.program_id(1) == 0)
        def _():
            o_ref[...] = jnp.zeros_like(o_ref)

        o_ref[...] += _dot_tn(a_ref[...], b_ref[...])

    return pl.pallas_call(
        body, name=name, grid=(N // tn, T // tk),
        out_shape=jax.ShapeDtypeStruct((M, N), F32),
        in_specs=[pl.BlockSpec((tk, M), lambda j, k: (k, 0)), pl.BlockSpec((tk, tn), lambda j, k: (k, j))],
        out_specs=pl.BlockSpec((M, tn), lambda j, k: (0, j)),
        compiler_params=_cparams(("arbitrary", "arbitrary")),
    )(a, b)


def _finalize(parts_all, dmod_all, dmod_cols, c_act_all):
    nparts = parts_all.shape[-1]
    nmod = dmod_all.shape[-1]

    def body(p_ref, dm_ref, dmc_ref, c_ref, ps_ref, loss_ref, db_ref, dw_ref):
        acc = p_ref[0]
        for j in range(1, N_DEV):
            acc = acc + p_ref[j]
        ps_ref[...] = acc
        loss_ref[...] = jnp.sum(acc[:, 0:LANE], axis=1, keepdims=True)
        db = dm_ref[0:1, :]
        for j in range(1, dm_ref.shape[0]):
            db = db + dm_ref[j:j + 1, :]
        db_ref[...] = db
        dw_ref[...] = _dot_tn(_mx(_silu(c_ref[...])), _mx(dmc_ref[...]))

    return pl.pallas_call(
        body, name="finalize",
        out_shape=[jax.ShapeDtypeStruct((1, nparts), F32), jax.ShapeDtypeStruct((1, 1), F32),
                   jax.ShapeDtypeStruct((1, nmod), F32),
                   jax.ShapeDtypeStruct((c_act_all.shape[1], dmod_cols.shape[1]), F32)],
        in_specs=[_vmem()] * 4, out_specs=[_vmem()] * 4,
        compiler_params=_cparams(),
    )(parts_all, dmod_all, dmod_cols, c_act_all)


def _adamw(w, g, m, v, name):
    def body(w_ref, g_ref, m_ref, v_ref, d_ref, nm_ref, nv_ref):
        gv = g_ref[...]
        nm = ADAM_B1 * m_ref[...] + (1.0 - ADAM_B1) * gv
        nv = ADAM_B2 * v_ref[...] + (1.0 - ADAM_B2) * (gv * gv)
        m_hat = nm / (1.0 - ADAM_B1 ** ADAM_STEP)
        v_hat = nv / (1.0 - ADAM_B2 ** ADAM_STEP)
        d_ref[...] = -ADAM_LR * (m_hat / (jnp.sqrt(v_hat) + ADAM_EPS) + ADAM_WD * w_ref[...])
        nm_ref[...] = nm
        nv_ref[...] = nv

    return pl.pallas_call(
        body, name=name,
        out_shape=[jax.ShapeDtypeStruct(w.shape, F32)] * 3,
        in_specs=[_vmem()] * 4, out_specs=[_vmem()] * 3,
        compiler_params=_cparams(),
    )(w, g, m, v)


def _pair_perm(a, axis, order):
    a = jnp.moveaxis(a, axis, -1)
    lead = a.shape[:-1]
    a = a.reshape(lead + (8, 64))[..., list(order), :].reshape(lead + (512,))
    return jnp.moveaxis(a, -1, axis)


def _pad_w_in(w):
    z = lambda n: jnp.zeros((w.shape[0], n), w.dtype)
    return jnp.concatenate([
        w[:, 0:640], z(64), w[:, 640:672], z(32), w[:, 672:1184],
        _pair_perm(w[:, 1184:1696], 1, PAIR_ORDER), w[:, 1696:1952], _pair_perm(w[:, 1952:2464], 1, PAIR_ORDER)], axis=1)


def _unpad_w_in(g):
    return jnp.concatenate([
        g[:, 0:640], g[:, 704:736], g[:, 768:1280],
        _pair_perm(g[:, 1280:1792], 1, PAIR_INV), g[:, 1792:2048], _pair_perm(g[:, 2048:2560], 1, PAIR_INV)], axis=1)


def _rope_tables(positions):
    T = positions.size
    inv = ROPE_THETA ** (-jnp.arange(0, MLA_ROPE, 2, dtype=F32) / MLA_ROPE)
    ang = positions.reshape(T).astype(F32)[:, None] * inv
    cos, sin = jnp.cos(ang), jnp.sin(ang)
    z = lambda n: jnp.zeros((T, n), F32)
    cosk = jnp.concatenate([z(64), cos, cos, z(32)], axis=1)
    cosq = jnp.concatenate([jnp.ones((T, 64), F32), cos, cos, z(32)], axis=1)
    sina = jnp.concatenate([z(64), -sin, z(48)], axis=1)
    sinb = jnp.concatenate([z(80), sin, z(32)], axis=1)
    return cosq, cosk, sina, sinb


def _local_step(x, mod, positions, ng, w_in, gq, gkv, w_uq, w_ukv, sinks, w_out, fg, tgt,
                ts=512, tq=512):
    B, S, D = x.shape
    T = B * S
    x2 = x.reshape(T, D)
    shift, scale, gate = (mod[:, None, k * D:(k + 1) * D] for k in range(3))
    w_in_p = _pad_w_in(w_in)
    w_uq_p = jnp.pad(w_uq.reshape(Q_LORA, MLA_HEADS, 96), ((0, 0), (0, 0), (0, 32))).reshape(Q_LORA, MLA_HEADS * LANE)
    w_ukv3 = w_ukv.reshape(KV_LORA, MLA_HEADS, 128)
    w_uk_p = jnp.pad(w_ukv3[:, :, :64], ((0, 0), (0, 0), (0, 64))).reshape(KV_LORA, MLA_HEADS * LANE)
    w_uv = w_ukv3[:, :, 64:].reshape(KV_LORA, 512)
    w_out_p = jnp.concatenate([w_out[:512], _pair_perm(w_out[512:], 0, PAIR_ORDER)], axis=0)
    cosq, cosk, sina, sinb = _rope_tables(positions)
    posf = positions.astype(F32)
    posb = jnp.broadcast_to(posf[:, :, None], (B, S, LANE))
    posr = posf.reshape(B, S // WINDOW, WINDOW)
    sinks_l = jnp.pad(sinks.reshape(1, SWA_HEADS), ((0, 0), (0, LANE - SWA_HEADS)))

    (hb, zq, zkv, ql, kvl, q, k, v, qs, ks, vs, g) = _pre_fwd(
        x2, shift, scale, ng, w_in_p, gq, gkv, w_uq_p, w_uk_p, w_uv, cosq, cosk, sina, sinb, S, ts)
    r3 = lambda a: a.reshape(B, S, a.shape[-1])
    om, lse_m = _mla_fwd(r3(q), r3(k), r3(v), tq)
    osw, lse_s = _swa_fwd(r3(qs), r3(ks), r3(vs), posb, posr, sinks_l)
    dx2, ab, dyb, do, dg, loss_v, dfg, dgate = _post(
        om.reshape(T, 512), osw.reshape(T, 512), g, w_out_p, x2, gate, fg.reshape(1, D), tgt.reshape(T, D), S, ts)
    do3 = r3(do)
    dq, dk, dv = _mla_bwd(r3(q), r3(k), r3(v), om, do3, lse_m, min(tq, 256))
    dqs, dks, dvs, dsink = _swa_bwd(r3(qs), r3(ks), r3(vs), posb, posr, sinks_l, osw, do3, lse_s)
    f2 = lambda a: a.reshape(T, a.shape[-1])
    gx, dqb, dkb, dvb, dz, dgq, dgkv, dng, dsh, dsc = _pre_bwd(
        f2(dq), f2(dk), f2(dv), f2(dqs), f2(dks), f2(dvs), dg, zq, zkv, x2, dx2, scale, ng, w_in_p, gq, gkv,
        w_uq_p, w_uk_p, w_uv, cosq, cosk, sina, sinb, S, ts // 2)
    tk = min(T, 1024)
    dw_in = _unpad_w_in(_tn_matmul(hb, dz, 512, tk, "dw_in"))
    dw_out_p = _tn_matmul(ab, dyb, 512, tk, "dw_out")
    dw_out = jnp.concatenate([dw_out_p[:512], _pair_perm(dw_out_p[512:], 0, PAIR_INV)], axis=0)
    dw_uq = _tn_matmul(ql, dqb, 512, tk, "dw_uq").reshape(Q_LORA, MLA_HEADS, LANE)[:, :, :96].reshape(Q_LORA, 768)
    dw_uk = _tn_matmul(kvl, dkb, 512, tk, "dw_uk").reshape(KV_LORA, MLA_HEADS, LANE)[:, :, :64]
    dw_uv = _tn_matmul(kvl, dvb, 512, tk, "dw_uv").reshape(KV_LORA, MLA_HEADS, 64)
    dw_ukv = jnp.concatenate([dw_uk, dw_uv], axis=2).reshape(KV_LORA, 1024)
    parts = jnp.concatenate([loss_v, dfg, dng, dgq, dgkv, dsink], axis=1)
    dmod = jnp.concatenate([dsh, dsc, dgate], axis=2).reshape(B, 3 * D)
    return gx.reshape(B, S, D), dw_in, dw_uq, dw_ukv, dw_out, parts, dmod


def kernel(x, c, positions, w_ada, b_ada, norm_gain, w_in, q_norm_gain, kv_norm_gain, w_uq, w_ukv, swa_sinks, w_out, final_gain, loss_target, m_w_ada, m_b_ada, m_norm_gain, m_w_in, m_q_norm_gain, m_kv_norm_gain, m_w_uq, m_w_ukv, m_swa_sinks, m_w_out, m_final_gain, v_w_ada, v_b_ada, v_norm_gain, v_w_in, v_q_norm_gain, v_kv_norm_gain, v_w_uq, v_w_ukv, v_swa_sinks, v_w_out, v_final_gain):
    B, S, D = x.shape
    me = 4 * lax.axis_index("x") + 2 * lax.axis_index("y") + lax.axis_index("c")
    bf = _MXU_DTYPE

    c_all, win_g, wuq_g, wukv_g, wout_g = _all_gather(
        [c, w_in[0].astype(bf), w_uq[0].astype(bf), w_ukv[0].astype(bf), w_out[0].astype(bf)], "ag_weights")
    c_all = c_all.reshape(N_DEV * B, D)
    cat_cols = lambda a: jnp.transpose(a, (1, 0, 2)).reshape(a.shape[1], N_DEV * a.shape[2])
    w_in_f, w_uq_f, w_ukv_f = cat_cols(win_g), cat_cols(wuq_g), cat_cols(wukv_g)
    w_out_f = wout_g.reshape(D, D)

    ncol = w_ada.shape[2]
    b_cols = lax.dynamic_slice_in_dim(b_ada, me * ncol, ncol, axis=1)
    (mod_g,) = _all_gather([_ada_fwd(c_all, w_ada[0], b_cols)], "ag_mod")
    mod = lax.dynamic_slice_in_dim(mod_g, me * B, B, axis=1)
    mod = jnp.transpose(mod, (1, 0, 2)).reshape(B, 3 * D)

    gx, dw_in, dw_uq, dw_ukv, dw_out, parts, dmod = _local_step(
        x, mod, positions, norm_gain, w_in_f, q_norm_gain, kv_norm_gain, w_uq_f, w_ukv_f, swa_sinks,
        w_out_f, final_gain, loss_target)

    parts_g, dmod_g = _all_gather([parts, dmod], "ag_small")
    dmod_all = dmod_g.reshape(N_DEV * B, 3 * D)
    dmod_cols = lax.dynamic_slice_in_dim(dmod_all, me * ncol, ncol, axis=1)
    psum, loss, g_b_ada, g_w_ada = _finalize(parts_g, dmod_all, dmod_cols, c_all)
    loss = loss.reshape(())
    o = LANE
    g_final_gain = psum[0, o:o + D]
    g_norm_gain = psum[:, o + D:o + 2 * D]
    o += 2 * D
    g_q_norm_gain = psum[:, o:o + Q_LORA]
    g_kv_norm_gain = psum[:, o + Q_LORA:o + Q_LORA + KV_LORA]
    o += Q_LORA + KV_LORA
    g_sinks = psum[:, o:o + SWA_HEADS]

    split_cols = lambda a: jnp.transpose(a.reshape(a.shape[0], N_DEV, a.shape[1] // N_DEV), (1, 0, 2))
    g_w_in, g_w_uq, g_w_ukv, g_w_out = _reduce_scatter(
        [split_cols(dw_in), split_cols(dw_uq), split_cols(dw_ukv), dw_out.reshape(N_DEV, D // N_DEV, D)], "rs_grads")

    grads = [g_w_ada, g_b_ada, g_norm_gain, g_w_in, g_q_norm_gain, g_kv_norm_gain, g_w_uq, g_w_ukv, g_sinks, g_w_out,
             g_final_gain]
    ws = [w_ada, b_ada, norm_gain, w_in, q_norm_gain, kv_norm_gain, w_uq, w_ukv, swa_sinks, w_out, final_gain]
    ms = [m_w_ada, m_b_ada, m_norm_gain, m_w_in, m_q_norm_gain, m_kv_norm_gain, m_w_uq, m_w_ukv, m_swa_sinks, m_w_out,
          m_final_gain]
    vs = [v_w_ada, v_b_ada, v_norm_gain, v_w_in, v_q_norm_gain, v_kv_norm_gain, v_w_uq, v_w_ukv, v_swa_sinks, v_w_out,
          v_final_gain]
    names = ["w_ada", "b_ada", "norm_gain", "w_in", "q_norm_gain", "kv_norm_gain", "w_uq", "w_ukv", "swa_sinks",
             "w_out", "final_gain"]
    out_g, out_d, out_m, out_v = [], [], [], []
    for name, w, g, m, v in zip(names, ws, grads, ms, vs):
        two_d = (1, w.shape[0]) if w.ndim == 1 else w.shape[-2:]
        d, nm, nv = _adamw(w.reshape(two_d), g.reshape(two_d), m.reshape(two_d), v.reshape(two_d), "adamw_" + name)
        out_g.append(g.reshape(w.shape))
        out_d.append(d.reshape(w.shape))
        out_m.append(nm.reshape(w.shape))
        out_v.append(nv.reshape(w.shape))
    return (loss, gx, *out_g, *out_d, *out_m, *out_v)
```

```python
import functools

import numpy as np
import jax
import jax.numpy as jnp
from jax import lax
from jax.experimental import pallas as pl
from jax.experimental.pallas import tpu as pltpu

F32 = jnp.float32
_MXU_DTYPE = jnp.bfloat16

N_DEV = 8
D_MODEL = 1024
MLA_HEADS = 8
MLA_NOPE = 64
MLA_ROPE = 32
MLA_V = 64
Q_LORA = 384
KV_LORA = 256
SWA_HEADS = 8
SWA_KV_HEADS = 2
SWA_HEAD_DIM = 64
WINDOW = 128
ROPE_THETA = 10000.0
EPS = 1e-6
MLA_SCALE = float((MLA_NOPE + MLA_ROPE) ** -0.5)
SWA_SCALE = float(SWA_HEAD_DIM ** -0.5)
LOG2E = 1.4426950408889634
MLA_QSCALE = MLA_SCALE * LOG2E
D_IN = 2464
D_IN_PAD = 2560
PAIR_ORDER = (0, 4, 1, 5, 2, 6, 3, 7)
PAIR_INV = (0, 2, 4, 6, 1, 3, 5, 7)

ADAM_LR = 0.001
ADAM_B1 = 0.9
ADAM_B2 = 0.999
ADAM_EPS = 1e-08
ADAM_WD = 0.01
ADAM_STEP = 10

LANE = 128
VMEM_LIMIT = 56 * 1024 * 1024

MESH = pl.DeviceIdType.MESH
NEG_INF = float("-inf")
SWA_SEQ_SPLIT = 2


def _mx(a):
    return a.astype(_MXU_DTYPE)


def _dot(a, b):
    return jnp.dot(a, b, preferred_element_type=F32)


def _dot_nt(a, b):
    return lax.dot_general(a, b, (((1,), (1,)), ((), ())), preferred_element_type=F32)


def _dot_tn(a, b):
    return lax.dot_general(a, b, (((0,), (0,)), ((), ())), preferred_element_type=F32)


def _cparams(sem=None):
    return pltpu.CompilerParams(dimension_semantics=sem, vmem_limit_bytes=VMEM_LIMIT)


def _vmem():
    return pl.BlockSpec(memory_space=pltpu.VMEM)


def _lane_iota(shape):
    return lax.broadcasted_iota(jnp.int32, shape, len(shape) - 1)


def _all_gather(arrs, name):
    n = len(arrs)

    def body(*refs):
        ins, outs = refs[:n], refs[n:2 * n]
        send_sems, recv_sems, local_sems = refs[2 * n:]
        x, y, c = lax.axis_index("x"), lax.axis_index("y"), lax.axis_index("c")
        me, sibling = (x, y, c), (x, y, 1 - c)
        chips = [(1 - x, y), (x, 1 - y), (1 - x, 1 - y)]

        def slot(a, dev):
            return outs[a].at[4 * dev[0] + 2 * dev[1] + dev[2]]

        def copy(a, k, block, to, src=None):
            return pltpu.make_async_remote_copy(
                src_ref=slot(a, block) if src is None else src, dst_ref=slot(a, block),
                send_sem=send_sems.at[7 * a + k], recv_sem=recv_sems.at[7 * a + k],
                device_id=to, device_id_type=MESH)

        mine = [pltpu.make_async_copy(ins[a], slot(a, me), local_sems.at[a]) for a in range(n)]
        for cp in mine:
            cp.start()
        first = []
        for a in range(n):
            first.append(copy(a, 0, me, sibling, src=ins[a]))
            first += [copy(a, 1 + j, me, (*chip, c), src=ins[a]) for j, chip in enumerate(chips)]
        for cp in first:
            cp.start()
        passed = []
        for j, chip in enumerate(chips):
            for a in range(n):
                copy(a, 1 + j, (*chip, c), me).wait_recv()
                cp = copy(a, 4 + j, (*chip, c), sibling)
                cp.start()
                passed.append(cp)
        for a in range(n):
            copy(a, 0, sibling, me).wait_recv()
            for j, chip in enumerate(chips):
                copy(a, 4 + j, (*chip, 1 - c), me).wait_recv()
        for cp in first + passed:
            cp.wait_send()
        for cp in mine:
            cp.wait()

    return pl.pallas_call(
        body, name=name,
        out_shape=[jax.ShapeDtypeStruct((N_DEV,) + a.shape, a.dtype) for a in arrs],
        in_specs=[_vmem()] * n, out_specs=[_vmem()] * n,
        scratch_shapes=[pltpu.SemaphoreType.DMA((7 * n,)), pltpu.SemaphoreType.DMA((7 * n,)),
                        pltpu.SemaphoreType.DMA((n,))],
        compiler_params=pltpu.CompilerParams(vmem_limit_bytes=VMEM_LIMIT),
    )(*arrs)


def _reduce_scatter(arrs, name):
    n = len(arrs)

    def body(*refs):
        ins, outs = refs[:n], refs[n:2 * n]
        recvs = refs[2 * n:3 * n]
        send_sems, recv_sems, local_sems = refs[3 * n:]
        x, y, c = lax.axis_index("x"), lax.axis_index("y"), lax.axis_index("c")
        me = 4 * x + 2 * y + c

        def peer(k):
            px = 1 - x if (k >> 2) & 1 else x
            py = 1 - y if (k >> 1) & 1 else y
            pc = 1 - c if k & 1 else c
            return (px, py, pc)

        def copy(a, k):
            p = peer(k)
            return pltpu.make_async_remote_copy(
                src_ref=ins[a].at[4 * p[0] + 2 * p[1] + p[2]], dst_ref=recvs[a].at[me],
                send_sem=send_sems.at[7 * a + k - 1], recv_sem=recv_sems.at[7 * a + k - 1],
                device_id=p, device_id_type=MESH)

        mine = [pltpu.make_async_copy(ins[a].at[me], recvs[a].at[me], local_sems.at[a]) for a in range(n)]
        for cp in mine:
            cp.start()
        sends = [copy(a, k) for a in range(n) for k in range(1, N_DEV)]
        for cp in sends:
            cp.start()
        for cp in mine:
            cp.wait()
        for cp in sends:
            cp.wait_recv()
        for a in range(n):
            acc = recvs[a][0]
            for j in range(1, N_DEV):
                acc = acc + recvs[a][j]
            outs[a][...] = acc
        for cp in sends:
            cp.wait_send()

    return pl.pallas_call(
        body, name=name,
        out_shape=[jax.ShapeDtypeStruct(a.shape[1:], a.dtype) for a in arrs],
        in_specs=[pl.BlockSpec(memory_space=pl.ANY)] * n, out_specs=[_vmem()] * n,
        scratch_shapes=[pltpu.VMEM(a.shape, a.dtype) for a in arrs]
        + [pltpu.SemaphoreType.DMA((7 * n,)), pltpu.SemaphoreType.DMA((7 * n,)), pltpu.SemaphoreType.DMA((n,))],
        compiler_params=pltpu.CompilerParams(vmem_limit_bytes=VMEM_LIMIT),
    )(*arrs)


def _silu(t):
    return t * (1.0 / (1.0 + jnp.exp(-t)))


def _ada_fwd(c_act_all, w_ada_shard, b_shard):
    def body(c_ref, w_ref, b_ref, o_ref):
        o_ref[...] = _dot(_mx(_silu(c_ref[...])), _mx(w_ref[...])) + b_ref[...]

    return pl.pallas_call(
        body, name="ada_fwd",
        out_shape=jax.ShapeDtypeStruct((c_act_all.shape[0], w_ada_shard.shape[1]), F32),
        in_specs=[_vmem()] * 3, out_specs=_vmem(),
        compiler_params=_cparams(),
    )(c_act_all, w_ada_shard, b_shard)


def _rope_tables_tiled(cq, sa, sb):
    return (jnp.concatenate([cq] * MLA_HEADS, axis=1), jnp.concatenate([sa] * MLA_HEADS, axis=1),
            jnp.concatenate([sb] * MLA_HEADS, axis=1))


def _pre_fwd(x2, shift, scale, ng, w_in, gq, gkv, w_uq, w_uk, w_uv, cosq, cosk, sina, sinb, S, ts):
    T, D = x2.shape
    nsb = S // ts
    WQ = MLA_HEADS * LANE

    def body(x_ref, sh_ref, sc_ref, ng_ref, w_ref, gq_ref, gkv_ref, wuq_ref, wuk_ref, wuv_ref,
             cq_ref, ck_ref, sa_ref, sb_ref,
             hb_ref, zq_ref, zkv_ref, ql_ref, kvl_ref, q_ref, k_ref, v_ref, qs_ref, ks_ref, vs_ref, g_ref):
        x = x_ref[...]
        r1 = lax.rsqrt(jnp.mean(x * x, axis=-1, keepdims=True) + EPS)
        h = ((x * r1) * ng_ref[...]) * (1.0 + sc_ref[0]) + sh_ref[0]
        hb = _mx(h)
        hb_ref[...] = hb
        zq = _dot(hb, w_ref[:, 0:384])
        zq_ref[...] = zq
        rq = lax.rsqrt(jnp.mean(zq * zq, axis=-1, keepdims=True) + EPS)
        ql = _mx((zq * rq) * gq_ref[...])
        ql_ref[...] = ql
        q = _dot(ql, wuq_ref[...])
        cq, sa, sb = _rope_tables_tiled(cq_ref[...], sa_ref[...], sb_ref[...])
        q = q * cq + pltpu.roll(q, WQ - 16, 1) * sa + pltpu.roll(q, 16, 1) * sb
        q_ref[...] = _mx(q * MLA_QSCALE)
        zkv = _dot(hb, w_ref[:, 384:640])
        zkv_ref[...] = zkv
        rkv = lax.rsqrt(jnp.mean(zkv * zkv, axis=-1, keepdims=True) + EPS)
        kvl = _mx((zkv * rkv) * gkv_ref[...])
        kvl_ref[...] = kvl
        kr = _dot(hb, w_ref[:, 640:768])
        kpe = kr * ck_ref[...] + pltpu.roll(kr, LANE - 16, 1) * sa_ref[...] + pltpu.roll(kr, 16, 1) * sb_ref[...]
        kf = _dot(kvl, wuk_ref[...])
        k_ref[...] = _mx(kf + jnp.concatenate([kpe] * MLA_HEADS, axis=1))
        v_ref[...] = _mx(_dot(kvl, wuv_ref[...]))
        g_ref[:, 0:512] = _dot(hb, w_ref[:, 768:1280])
        qs_ref[...] = _mx(_dot(hb, w_ref[:, 1280:1792]) * SWA_SCALE)
        ks_ref[...] = _mx(_dot(hb, w_ref[:, 1792:1920]))
        vs_ref[...] = _mx(_dot(hb, w_ref[:, 1920:2048]))
        g_ref[:, 512:1024] = _dot(hb, w_ref[:, 2048:2560])

    row = lambda w: pl.BlockSpec((ts, w), lambda i: (i, 0))
    full = lambda a: pl.BlockSpec(a.shape, lambda i: (0,) * a.ndim)
    per_b = pl.BlockSpec((1, 1, D), lambda i: (i // nsb, 0, 0))
    out_w = [(D, _MXU_DTYPE), (384, F32), (256, F32), (384, _MXU_DTYPE), (256, _MXU_DTYPE), (WQ, _MXU_DTYPE),
             (WQ, _MXU_DTYPE), (512, _MXU_DTYPE), (512, _MXU_DTYPE), (128, _MXU_DTYPE), (128, _MXU_DTYPE), (1024, F32)]
    return pl.pallas_call(
        body, name="pre_fwd", grid=(T // ts,),
        out_shape=[jax.ShapeDtypeStruct((T, w), dt) for w, dt in out_w],
        in_specs=[row(D), per_b, per_b, full(ng), full(w_in), full(gq), full(gkv), full(w_uq), full(w_uk), full(w_uv),
                  row(LANE), row(LANE), row(LANE), row(LANE)],
        out_specs=[row(w) for w, _ in out_w],
        compiler_params=_cparams(("arbitrary",)),
    )(x2, shift, scale, ng, w_in, gq, gkv, w_uq, w_uk, w_uv, cosq, cosk, sina, sinb)


def _mla_fwd(q3, k3, v3, tq, tk):
    B, S, _ = q3.shape
    nq = S // tq
    assert tk % tq == 0 or tq % tk == 0
    n_masked = max(1, tq // tk)

    def body(q_ref, k_ref, v_ref, o_ref, lse_ref):
        qi = pl.program_id(2)
        rows = lax.broadcasted_iota(jnp.int32, (tq, tk), 0)
        cols = lax.broadcasted_iota(jnp.int32, (tq, tk), 1)
        qs = [q_ref[0, :, LANE * j:LANE * (j + 1)] for j in range(2)]
        n_full = (qi * tq) // tk

        def step(kt, carry, masked):
            r0 = pl.multiple_of(kt * tk, tk)
            v2 = v_ref[0, pl.ds(r0, tk), :]
            out = []
            for j in range(2):
                m, l, acc = carry[j]
                s = _dot_nt(qs[j], k_ref[0, pl.ds(r0, tk), LANE * j:LANE * (j + 1)])
                if masked:
                    s = jnp.where(rows + qi * tq >= cols + kt * tk, s, NEG_INF)
                m_new = jnp.maximum(m, jnp.max(s, axis=1, keepdims=True))
                alpha = jnp.exp2(m - m_new)
                p = jnp.exp2(s - m_new)
                l = alpha * l + jnp.sum(p, axis=1, keepdims=True)
                acc = alpha * acc + _dot(_mx(p), v2)
                out.append((m_new, l, acc))
            return tuple(out)

        init = (jnp.full((tq, 1), NEG_INF, F32), jnp.zeros((tq, 1), F32), jnp.zeros((tq, LANE), F32))
        carry = lax.fori_loop(0, n_full, functools.partial(step, masked=False), (init, init))
        for t in range(n_masked):
            carry = step(n_full + t, carry, True)
        (m0, l0, a0), (m1, l1, a1) = carry
        o_ref[0] = jnp.where(_lane_iota((tq, LANE)) < 64, a0 / l0, a1 / l1)
        lse_ref[0, 0] = jnp.broadcast_to(m0 + jnp.log2(l0), (tq, LANE))
        lse_ref[0, 1] = jnp.broadcast_to(m1 + jnp.log2(l1), (tq, LANE))

    return pl.pallas_call(
        body, name="mla_fwd", grid=(B, MLA_HEADS // 2, nq),
        out_shape=[jax.ShapeDtypeStruct((B, S, 512), F32), jax.ShapeDtypeStruct((B, MLA_HEADS, S, LANE), F32)],
        in_specs=[pl.BlockSpec((1, tq, 2 * LANE), lambda b, hp, i: (b, i, hp)),
                  pl.BlockSpec((1, S, 2 * LANE), lambda b, hp, i: (b, 0, hp)),
                  pl.BlockSpec((1, S, LANE), lambda b, hp, i: (b, 0, hp))],
        out_specs=[pl.BlockSpec((1, tq, LANE), lambda b, hp, i: (b, i, hp)),
                   pl.BlockSpec((1, 2, tq, LANE), lambda b, hp, i: (b, hp, i, 0))],
        compiler_params=_cparams(("arbitrary", "arbitrary", "arbitrary")),
    )(q3, k3, v3)


def _mla_bwd(q3, k3, v3, o3, do3, lse, tq, tk):
    B, S, _ = q3.shape
    nq, nk = S // tq, S // tk
    assert tk % tq == 0

    def body(q_ref, k_ref, v_ref, o_ref, do_ref, lse_ref, dq_ref, dk_ref, dv_ref, dkt_ref, dvt_ref):
        dkt_ref[...] = jnp.zeros_like(dkt_ref)
        dvt_ref[...] = jnp.zeros_like(dvt_ref)
        rows = lax.broadcasted_iota(jnp.int32, (tq, tk), 0)
        cols = lax.broadcasted_iota(jnp.int32, (tq, tk), 1)
        lane = _lane_iota((tq, LANE))

        def q_tile(qi, _):
            r = pl.ds(pl.multiple_of(qi * tq, tq), tq)
            do2, o2 = do_ref[0, r, :], o_ref[0, r, :]
            heads = []
            for j in range(2):
                lanes = slice(LANE * j, LANE * (j + 1))
                do = jnp.where((lane < 64) if j == 0 else (lane >= 64), do2, 0.0)
                q = q_ref[0, r, lanes]
                heads.append((lanes, q, _mx(q.astype(F32).T), _mx(do), _mx(do.T),
                              jnp.sum(do * o2, axis=1, keepdims=True),
                              jnp.concatenate([lse_ref[0, j, r, :]] * (tk // LANE), axis=1)))
            n_full = (qi * tq) // tk

            def k_tile(kt, dqs, masked):
                kr = pl.ds(pl.multiple_of(kt * tk, tk), tk)
                v2 = v_ref[0, kr, :]
                out = []
                dvt = None
                for j, (lanes, q, qt, dob, dot_, dcol, lse_c) in enumerate(heads):
                    k = k_ref[0, kr, lanes]
                    s = _dot_nt(q, k)
                    if masked:
                        s = jnp.where(rows + qi * tq >= cols + kt * tk, s, NEG_INF)
                    p = jnp.exp2(s - lse_c)
                    dp = _dot_nt(dob, v2)
                    dsb = _mx(p * (dp - dcol))
                    dkt_ref[j, kt] += _dot(qt, dsb)
                    pv = _dot(dot_, _mx(p))
                    dvt = pv if dvt is None else dvt + pv
                    out.append(dqs[j] + _dot(dsb, k))
                dvt_ref[kt] += dvt
                return tuple(out)

            zero = jnp.zeros((tq, LANE), F32)
            dqs = lax.fori_loop(0, n_full, functools.partial(k_tile, masked=False), (zero, zero))
            dqs = k_tile(n_full, dqs, True)
            for j in range(2):
                dq_ref[0, r, heads[j][0]] = MLA_SCALE * dqs[j]
            return 0

        lax.fori_loop(0, nq, q_tile, 0)

        def flush(kt, _):
            kr = pl.ds(pl.multiple_of(kt * tk, tk), tk)
            for j in range(2):
                dk_ref[0, kr, LANE * j:LANE * (j + 1)] = (1.0 / LOG2E) * dkt_ref[j, kt].T
            dv_ref[0, kr, :] = dvt_ref[kt].T
            return 0

        lax.fori_loop(0, nk, flush, 0)

    pair = lambda w: pl.BlockSpec((1, S, w), lambda b, hp: (b, 0, hp))
    return pl.pallas_call(
        body, name="mla_bwd", grid=(B, MLA_HEADS // 2),
        out_shape=[jax.ShapeDtypeStruct((B, S, 1024), F32), jax.ShapeDtypeStruct((B, S, 1024), F32),
                   jax.ShapeDtypeStruct((B, S, 512), F32)],
        in_specs=[pair(2 * LANE), pair(2 * LANE), pair(LANE), pair(LANE), pair(LANE),
                  pl.BlockSpec((1, 2, S, LANE), lambda b, hp: (b, hp, 0, 0))],
        out_specs=[pair(2 * LANE), pair(2 * LANE), pair(LANE)],
        scratch_shapes=[pltpu.VMEM((2, nk, LANE, tk), F32), pltpu.VMEM((nk, LANE, tk), F32)],
        compiler_params=_cparams(("arbitrary", "arbitrary")),
    )(q3, k3, v3, o3, do3, lse)


def _swa_consts(sink_ref):
    W = WINDOW
    row = lax.broadcasted_iota(jnp.int32, (4 * W, LANE), 0)
    out = []
    for g in range(SWA_KV_HEADS):
        slope = jnp.zeros((4 * W, LANE), F32)
        sink = jnp.zeros((4 * W, LANE), F32)
        for p in range(4):
            h = p + 4 * g
            here = jnp.logical_and(row >= W * p, row < W * (p + 1))
            slope = jnp.where(here, float(2.0 ** (-8.0 * (h + 1) / SWA_HEADS)), slope)
            sink = jnp.where(here, sink_ref[:, h:h + 1], sink)
        out.append((jnp.concatenate([slope, slope], axis=1), sink))
    return out


def _wide(col):
    return jnp.concatenate([col, col], axis=1)


def _swa_block(q_ref, k_ref, v_ref, pb_ref, pr_ref, n, i, ij):
    W = WINDOW
    kb = jnp.maximum(n - 1, 0)
    r = pl.ds(pl.multiple_of(i * W, W), W)
    kr = pl.ds(pl.multiple_of(kb * W, W), 2 * W)
    q4, k2, v2 = q_ref[0, r, :], k_ref[0, kr, :], v_ref[0, kr, :]
    pq = _wide(pb_ref[0, r, :])
    pk = jnp.concatenate([pr_ref[0, pl.ds(kb, 1), :], pr_ref[0, pl.ds(kb + 1, 1), :]], axis=1)
    rel = ij + (n - kb) * W
    mask = jnp.where(jnp.logical_and(rel >= 0, rel < W), 0.0, NEG_INF)
    dist4 = jnp.concatenate([pq - pk] * 4, axis=0)
    mask4 = jnp.concatenate([mask] * 4, axis=0)
    return r, kb, kr, q4, k2, v2, dist4, mask4


def _swa_stack(x4, g, dtype):
    lane = _lane_iota((WINDOW, LANE))
    mine = (lane < 64) if g == 0 else (lane >= 64)
    return jnp.concatenate([jnp.where(mine, x4[:, LANE * p:LANE * (p + 1)], 0).astype(dtype) for p in range(4)], axis=0)


def _swa_unstack(ref, r, lo, hi, scale=None):
    W = WINDOW
    low = _lane_iota((W, LANE)) < 64
    for p in range(4):
        t = jnp.where(low, lo[W * p:W * (p + 1)], hi[W * p:W * (p + 1)])
        ref[0, r, LANE * p:LANE * (p + 1)] = t if scale is None else scale * t


def _swa_fwd(qs3, ks3, vs3, posb, posr, sinks):
    B, S, _ = qs3.shape
    W = WINDOW
    nb = S // W
    nh = SWA_SEQ_SPLIT
    nbh = nb // nh

    def body(q_ref, k_ref, v_ref, pb_ref, pr_ref, sink_ref, o_ref, lse_ref):
        ij = lax.broadcasted_iota(jnp.int32, (W, 2 * W), 0) - lax.broadcasted_iota(jnp.int32, (W, 2 * W), 1)
        consts = _swa_consts(sink_ref)
        n0 = pl.program_id(1) * nbh

        def blk(i, _):
            n = n0 + i
            r, _, _, q4, k2, v2, dist4, mask4 = _swa_block(q_ref, k_ref, v_ref, pb_ref, pr_ref, n, i, ij)
            o_g = []
            for g, (slope, sink) in enumerate(consts):
                s = _dot_nt(_swa_stack(q4, g, q4.dtype), k2) - slope * dist4 + mask4
                m = jnp.maximum(jnp.max(s, axis=1, keepdims=True), sink)
                e = jnp.exp(s - _wide(m))
                l = jnp.sum(e, axis=1, keepdims=True) + jnp.exp(sink - m)
                o_g.append(_dot(_mx(e * _wide(1.0 / l)), v2))
                lse_ref[0, i, g] = m + jnp.log(l)
            _swa_unstack(o_ref, r, o_g[0], o_g[1])
            return 0

        lax.fori_loop(0, nbh, blk, 0)

    seq = lambda w: pl.BlockSpec((1, S, w), lambda b, h: (b, 0, 0))
    part = lambda w: pl.BlockSpec((1, S // nh, w), lambda b, h: (b, h, 0))
    lse_spec = pl.BlockSpec((1, nbh, 2, 4 * W, LANE), lambda b, h: (b, h, 0, 0, 0))
    return pl.pallas_call(
        body, name="swa_fwd", grid=(B, nh),
        out_shape=[jax.ShapeDtypeStruct((B, S, 512), F32), jax.ShapeDtypeStruct((B, nb, 2, 4 * W, LANE), F32)],
        in_specs=[part(512), seq(LANE), seq(LANE), part(LANE), pl.BlockSpec((1, nb, W), lambda b, h: (b, 0, 0)),
                  pl.BlockSpec((1, LANE), lambda b, h: (0, 0))],
        out_specs=[part(512), lse_spec],
        compiler_params=_cparams(("arbitrary", "arbitrary")),
    )(qs3, ks3, vs3, posb, posr, sinks)


def _swa_bwd(qs3, ks3, vs3, posb, posr, sinks, os3, do3, lse):
    B, S, _ = qs3.shape
    W = WINDOW
    nb = S // W
    nh = SWA_SEQ_SPLIT
    nbh = nb // nh

    def body(q_ref, k_ref, v_ref, pb_ref, pr_ref, sink_ref, o_ref, do_ref, lse_ref, dq_ref, dk_ref, dv_ref, dsink_ref,
             dkt_ref, dvt_ref):
        lane1 = _lane_iota((1, LANE))
        ij = lax.broadcasted_iota(jnp.int32, (W, 2 * W), 0) - lax.broadcasted_iota(jnp.int32, (W, 2 * W), 1)
        consts = _swa_consts(sink_ref)
        hh = pl.program_id(1)
        n0 = hh * nbh

        @pl.when(hh == 0)
        def _():
            dkt_ref[...] = jnp.zeros_like(dkt_ref)
            dvt_ref[...] = jnp.zeros_like(dvt_ref)

        @pl.when(jnp.logical_and(pl.program_id(0) == 0, hh == 0))
        def _():
            dsink_ref[...] = jnp.zeros_like(dsink_ref)

        def blk(i, dsink):
            n = n0 + i
            r, kb, _, q4, k2, v2, dist4, mask4 = _swa_block(q_ref, k_ref, v_ref, pb_ref, pr_ref, n, i, ij)
            o4, do4 = o_ref[0, r, :], do_ref[0, r, :]
            dq_g = []
            dkt = jnp.zeros((LANE, 2 * W), F32)
            dvt = jnp.zeros((LANE, 2 * W), F32)
            for g, (slope, sink) in enumerate(consts):
                q_st = _swa_stack(q4, g, F32)
                do_st = _swa_stack(do4, g, F32)
                dcol = jnp.sum(do_st * _swa_stack(o4, g, F32), axis=1, keepdims=True)
                qst, dob = _mx(q_st), _mx(do_st)
                lse_c = lse_ref[0, i, g]
                pr = jnp.exp(_dot_nt(qst, k2) - slope * dist4 + mask4 - _wide(lse_c))
                dsb = _mx(pr * (_dot_nt(dob, v2) - dcol))
                psd = jnp.exp(sink - lse_c)[:, 0:1] * dcol
                for p in range(4):
                    dsink = dsink - jnp.where(lane1 == p + 4 * g,
                                              jnp.sum(psd[W * p:W * (p + 1)], axis=0, keepdims=True), 0.0)
                dq_g.append(_dot(dsb, k2))
                dkt = dkt + _dot(_mx(q_st.T), dsb)
                dvt = dvt + _dot(_mx(do_st.T), _mx(pr))
            _swa_unstack(dq_ref, r, dq_g[0], dq_g[1], SWA_SCALE)
            dkt_ref[kb] += dkt[:, 0:W]
            dkt_ref[kb + 1] += dkt[:, W:2 * W]
            dvt_ref[kb] += dvt[:, 0:W]
            dvt_ref[kb + 1] += dvt[:, W:2 * W]
            return dsink

        dsink_ref[...] += lax.fori_loop(0, nbh, blk, jnp.zeros((1, LANE), F32))

        @pl.when(hh == nh - 1)
        def _():
            def flush(n, _):
                r = pl.ds(pl.multiple_of(n * W, W), W)
                dk_ref[0, r, :] = dkt_ref[n].T
                dv_ref[0, r, :] = dvt_ref[n].T
                return 0

            lax.fori_loop(0, nb, flush, 0)

    seq = lambda w: pl.BlockSpec((1, S, w), lambda b, h: (b, 0, 0))
    part = lambda w: pl.BlockSpec((1, S // nh, w), lambda b, h: (b, h, 0))
    return pl.pallas_call(
        body, name="swa_bwd", grid=(B, nh),
        out_shape=[jax.ShapeDtypeStruct((B, S, 512), F32), jax.ShapeDtypeStruct((B, S, LANE), F32),
                   jax.ShapeDtypeStruct((B, S, LANE), F32), jax.ShapeDtypeStruct((1, LANE), F32)],
        in_specs=[part(512), seq(LANE), seq(LANE), part(LANE), pl.BlockSpec((1, nb, W), lambda b, h: (b, 0, 0)),
                  pl.BlockSpec((1, LANE), lambda b, h: (0, 0)), part(512),
                  pl.BlockSpec((1, S // nh, 512), lambda b, h: (b, h, 1)),
                  pl.BlockSpec((1, nbh, 2, 4 * W, LANE), lambda b, h: (b, h, 0, 0, 0))],
        out_specs=[part(512), seq(LANE), seq(LANE), pl.BlockSpec((1, LANE), lambda b, h: (0, 0))],
        scratch_shapes=[pltpu.VMEM((nb, LANE, W), F32), pltpu.VMEM((nb, LANE, W), F32)],
        compiler_params=_cparams(("arbitrary", "arbitrary")),
    )(qs3, ks3, vs3, posb, posr, sinks, os3, do3, lse)


def _post(om, osw, g, w_out, x2, gate, fg, tgt, S, ts):
    T, D = x2.shape
    nsb = S // ts

    def body(om_ref, os_ref, g_ref, w_ref, x_ref, gate_ref, fg_ref, t_ref,
             dx2_ref, ab_ref, dyb_ref, do_ref, dg_ref, loss_ref, dfg_ref, dgate_ref):
        i = pl.program_id(0)
        gv = g_ref[...]
        sg = 1.0 / (1.0 + jnp.exp(-gv))
        silu = gv * sg
        o = jnp.concatenate([om_ref[...], os_ref[...]], axis=1)
        ab = _mx(o * silu)
        ab_ref[...] = ab
        y = _dot(ab, w_ref[...])
        gate = gate_ref[0]
        xo = x_ref[...] + gate * y
        r2 = lax.rsqrt(jnp.mean(xo * xo, axis=-1, keepdims=True) + EPS)
        xh = xo * r2
        fg = fg_ref[...]
        diff = xh * fg - t_ref[...]
        sq = jnp.sum(diff * diff, axis=0, keepdims=True)
        part = sq[:, 0:LANE]
        for t in range(1, D // LANE):
            part = part + sq[:, LANE * t:LANE * (t + 1)]
        dout = diff * (1.0 / D)
        dxh = dout * fg
        dx2 = r2 * (dxh - xh * jnp.mean(dxh * xh, axis=-1, keepdims=True))
        dx2_ref[...] = dx2
        dyb = _mx(dx2 * gate)
        dyb_ref[...] = dyb
        da = _dot_nt(dyb, w_ref[...])
        do_ref[...] = da * silu
        dg_ref[...] = _mx(da * o * (sg * (1.0 + gv * (1.0 - sg))))

        @pl.when(i == 0)
        def _():
            loss_ref[...] = jnp.zeros_like(loss_ref)
            dfg_ref[...] = jnp.zeros_like(dfg_ref)

        @pl.when(i % nsb == 0)
        def _():
            dgate_ref[...] = jnp.zeros_like(dgate_ref)

        loss_ref[...] += (0.5 / D) * part
        dfg_ref[...] += jnp.sum(dout * xh, axis=0, keepdims=True)
        dgate_ref[0] += jnp.sum(dx2 * y, axis=0, keepdims=True)

    row = lambda w: pl.BlockSpec((ts, w), lambda i: (i, 0))
    full = lambda a: pl.BlockSpec(a.shape, lambda i: (0,) * a.ndim)
    per_b = pl.BlockSpec((1, 1, D), lambda i: (i // nsb, 0, 0))
    return pl.pallas_call(
        body, name="post", grid=(T // ts,),
        out_shape=[jax.ShapeDtypeStruct((T, D), F32), jax.ShapeDtypeStruct((T, D), _MXU_DTYPE),
                   jax.ShapeDtypeStruct((T, D), _MXU_DTYPE), jax.ShapeDtypeStruct((T, 1024), F32),
                   jax.ShapeDtypeStruct((T, 1024), _MXU_DTYPE), jax.ShapeDtypeStruct((1, LANE), F32),
                   jax.ShapeDtypeStruct((1, D), F32), jax.ShapeDtypeStruct(gate.shape, F32)],
        in_specs=[row(512), row(512), row(1024), full(w_out), row(D), per_b, full(fg), row(D)],
        out_specs=[row(D), row(D), row(D), row(1024), row(1024),
                   pl.BlockSpec((1, LANE), lambda i: (0, 0)), pl.BlockSpec((1, D), lambda i: (0, 0)), per_b],
        compiler_params=_cparams(("arbitrary",)),
    )(om, osw, g, w_out, x2, gate, fg, tgt)


def _pre_bwd(dq, dk, dv, dqs, dks, dvs, dg, zq, zkv, x2, dx2, scale, ng, w_in, gq, gkv, w_uq, w_uk, w_uv,
             cosq, cosk, sina, sinb, S, ts):
    T, D = x2.shape
    nsb = S // ts
    WQ = MLA_HEADS * LANE

    def body(dq_ref, dk_ref, dv_ref, dqs_ref, dks_ref, dvs_ref, dg_ref, zq_ref, zkv_ref, x_ref, dx2_ref, sc_ref,
             ng_ref, w_ref, gq_ref, gkv_ref, wuq_ref, wuk_ref, wuv_ref, cq_ref, ck_ref, sa_ref, sb_ref,
             gx_ref, dqb_ref, dkb_ref, dvb_ref, dz_ref, dgq_ref, dgkv_ref, dng_ref, dsh_ref, dsc_ref):
        i = pl.program_id(0)

        @pl.when(i == 0)
        def _():
            dgq_ref[...] = jnp.zeros_like(dgq_ref)
            dgkv_ref[...] = jnp.zeros_like(dgkv_ref)
            dng_ref[...] = jnp.zeros_like(dng_ref)

        @pl.when(i % nsb == 0)
        def _():
            dsh_ref[...] = jnp.zeros_like(dsh_ref)
            dsc_ref[...] = jnp.zeros_like(dsc_ref)

        def norm_bwd(z, dy, gain):
            r = lax.rsqrt(jnp.mean(z * z, axis=-1, keepdims=True) + EPS)
            zh = z * r
            dzh = dy * gain
            return r * (dzh - zh * jnp.mean(dzh * zh, axis=-1, keepdims=True)), jnp.sum(dy * zh, axis=0, keepdims=True)

        cq, sa, sb = _rope_tables_tiled(cq_ref[...], sa_ref[...], sb_ref[...])
        dqr = dq_ref[...]
        dqb = _mx(dqr * cq + pltpu.roll(dqr * sa, 16, 1) + pltpu.roll(dqr * sb, WQ - 16, 1))
        dqb_ref[...] = dqb
        dzq, dgq = norm_bwd(zq_ref[...], _dot_nt(dqb, wuq_ref[...]), gq_ref[...])
        dgq_ref[...] += dgq
        dkr = dk_ref[...]
        dkb = _mx(dkr)
        dkb_ref[...] = dkb
        dvb = _mx(dv_ref[...])
        dvb_ref[...] = dvb
        dzkv, dgkv = norm_bwd(zkv_ref[...], _dot_nt(dkb, wuk_ref[...]) + _dot_nt(dvb, wuv_ref[...]), gkv_ref[...])
        dgkv_ref[...] += dgkv
        dkpe = dkr[:, 0:LANE]
        for h in range(1, MLA_HEADS):
            dkpe = dkpe + dkr[:, LANE * h:LANE * (h + 1)]
        dkro = (dkpe * ck_ref[...] + pltpu.roll(dkpe * sa_ref[...], 16, 1)
                + pltpu.roll(dkpe * sb_ref[...], LANE - 16, 1))
        dz_ref[:, 0:384] = _mx(dzq)
        dz_ref[:, 384:640] = _mx(dzkv)
        dz_ref[:, 640:768] = _mx(dkro)
        dz_ref[:, 768:1280] = dg_ref[:, 0:512]
        dz_ref[:, 1280:1792] = _mx(dqs_ref[...])
        dz_ref[:, 1792:1920] = _mx(dks_ref[...])
        dz_ref[:, 1920:2048] = _mx(dvs_ref[...])
        dz_ref[:, 2048:2560] = dg_ref[:, 512:1024]
        dh = _dot_nt(dz_ref[...], w_ref[...])
        x = x_ref[...]
        r1 = lax.rsqrt(jnp.mean(x * x, axis=-1, keepdims=True) + EPS)
        xn = x * r1
        ng = ng_ref[...]
        sc1 = 1.0 + sc_ref[0]
        dsh_ref[0] += jnp.sum(dh, axis=0, keepdims=True)
        dsc_ref[0] += jnp.sum(dh * (xn * ng), axis=0, keepdims=True)
        dng_ref[...] += jnp.sum(dh * xn * sc1, axis=0, keepdims=True)
        dxn = dh * ng * sc1
        gx_ref[...] = dx2_ref[...] + r1 * (dxn - xn * jnp.mean(dxn * xn, axis=-1, keepdims=True))

    row = lambda w: pl.BlockSpec((ts, w), lambda i: (i, 0))
    full = lambda a: pl.BlockSpec(a.shape, lambda i: (0,) * a.ndim)
    per_b = pl.BlockSpec((1, 1, D), lambda i: (i // nsb, 0, 0))
    vec = lambda w: pl.BlockSpec((1, w), lambda i: (0, 0))
    return pl.pallas_call(
        body, name="pre_bwd", grid=(T // ts,),
        out_shape=[jax.ShapeDtypeStruct((T, D), F32), jax.ShapeDtypeStruct((T, WQ), _MXU_DTYPE),
                   jax.ShapeDtypeStruct((T, WQ), _MXU_DTYPE), jax.ShapeDtypeStruct((T, 512), _MXU_DTYPE),
                   jax.ShapeDtypeStruct((T, D_IN_PAD), _MXU_DTYPE), jax.ShapeDtypeStruct((1, 384), F32),
                   jax.ShapeDtypeStruct((1, 256), F32), jax.ShapeDtypeStruct((1, D), F32),
                   jax.ShapeDtypeStruct(scale.shape, F32), jax.ShapeDtypeStruct(scale.shape, F32)],
        in_specs=[row(WQ), row(WQ), row(512), row(512), row(LANE), row(LANE), row(1024), row(384), row(256), row(D),
                  row(D), per_b, full(ng), full(w_in), full(gq), full(gkv), full(w_uq), full(w_uk), full(w_uv),
                  row(LANE), row(LANE), row(LANE), row(LANE)],
        out_specs=[row(D), row(WQ), row(WQ), row(512), row(D_IN_PAD), vec(384), vec(256), vec(D), per_b, per_b],
        compiler_params=_cparams(("arbitrary",)),
    )(dq, dk, dv, dqs, dks, dvs, dg, zq, zkv, x2, dx2, scale, ng, w_in, gq, gkv, w_uq, w_uk, w_uv,
      cosq, cosk, sina, sinb)


def _tn_matmul(a, b, tn, tk, name):
    T, M = a.shape
    N = b.shape[1]

    def body(a_ref, b_ref, o_ref):
        @pl.when(pl.program_id(1) == 0)
        def _():
            o_ref[...] = jnp.zeros_like(o_ref)

        o_ref[...] += _dot_tn(a_ref[...], b_ref[...])

    return pl.pallas_call(
        body, name=name, grid=(N // tn, T // tk),
        out_shape=jax.ShapeDtypeStruct((M, N), F32),
        in_specs=[pl.BlockSpec((tk, M), lambda j, k: (k, 0)), pl.BlockSpec((tk, tn), lambda j, k: (k, j))],
        out_specs=pl.BlockSpec((M, tn), lambda j, k: (0, j)),
        compiler_params=_cparams(("arbitrary", "arbitrary")),
    )(a, b)


def _finalize(parts_all, dmod_all, dmod_cols, c_act_all):
    nparts = parts_all.shape[-1]
    nmod = dmod_all.shape[-1]

    def body(p_ref, dm_ref, dmc_ref, c_ref, ps_ref, loss_ref, db_ref, dw_ref):
        acc = p_ref[0]
        for j in range(1, N_DEV):
            acc = acc + p_ref[j]
        ps_ref[...] = acc
        loss_ref[...] = jnp.sum(acc[:, 0:LANE], axis=1, keepdims=True)
        db = dm_ref[0:1, :]
        for j in range(1, dm_ref.shape[0]):
            db = db + dm_ref[j:j + 1, :]
        db_ref[...] = db
        dw_ref[...] = _dot_tn(_mx(_silu(c_ref[...])), _mx(dmc_ref[...]))

    return pl.pallas_call(
        body, name="finalize",
        out_shape=[jax.ShapeDtypeStruct((1, nparts), F32), jax.ShapeDtypeStruct((1, 1), F32),
                   jax.ShapeDtypeStruct((1, nmod), F32),
                   jax.ShapeDtypeStruct((c_act_all.shape[1], dmod_cols.shape[1]), F32)],
        in_specs=[_vmem()] * 4, out_specs=[_vmem()] * 4,
        compiler_params=_cparams(),
    )(parts_all, dmod_all, dmod_cols, c_act_all)


def _adamw(w, g, m, v, name):
    def body(w_ref, g_ref, m_ref, v_ref, d_ref, nm_ref, nv_ref):
        gv = g_ref[...]
        nm = ADAM_B1 * m_ref[...] + (1.0 - ADAM_B1) * gv
        nv = ADAM_B2 * v_ref[...] + (1.0 - ADAM_B2) * (gv * gv)
        m_hat = nm / (1.0 - ADAM_B1 ** ADAM_STEP)
        v_hat = nv / (1.0 - ADAM_B2 ** ADAM_STEP)
        d_ref[...] = -ADAM_LR * (m_hat / (jnp.sqrt(v_hat) + ADAM_EPS) + ADAM_WD * w_ref[...])
        nm_ref[...] = nm
        nv_ref[...] = nv

    return pl.pallas_call(
        body, name=name,
        out_shape=[jax.ShapeDtypeStruct(w.shape, F32)] * 3,
        in_specs=[_vmem()] * 4, out_specs=[_vmem()] * 3,
        compiler_params=_cparams(),
    )(w, g, m, v)


def _pair_perm(a, axis, order):
    a = jnp.moveaxis(a, axis, -1)
    lead = a.shape[:-1]
    a = a.reshape(lead + (8, 64))[..., list(order), :].reshape(lead + (512,))
    return jnp.moveaxis(a, -1, axis)


def _pad_w_in(w):
    z = lambda n: jnp.zeros((w.shape[0], n), w.dtype)
    return jnp.concatenate([
        w[:, 0:640], z(64), w[:, 640:672], z(32), w[:, 672:1184],
        _pair_perm(w[:, 1184:1696], 1, PAIR_ORDER), w[:, 1696:1952], _pair_perm(w[:, 1952:2464], 1, PAIR_ORDER)], axis=1)


def _unpad_w_in(g):
    return jnp.concatenate([
        g[:, 0:640], g[:, 704:736], g[:, 768:1280],
        _pair_perm(g[:, 1280:1792], 1, PAIR_INV), g[:, 1792:2048], _pair_perm(g[:, 2048:2560], 1, PAIR_INV)], axis=1)


def _rope_tables(positions):
    T = positions.size
    inv = ROPE_THETA ** (-jnp.arange(0, MLA_ROPE, 2, dtype=F32) / MLA_ROPE)
    ang = positions.reshape(T).astype(F32)[:, None] * inv
    cos, sin = jnp.cos(ang), jnp.sin(ang)
    z = lambda n: jnp.zeros((T, n), F32)
    cosk = jnp.concatenate([z(64), cos, cos, z(32)], axis=1)
    cosq = jnp.concatenate([jnp.ones((T, 64), F32), cos, cos, z(32)], axis=1)
    sina = jnp.concatenate([z(64), -sin, z(48)], axis=1)
    sinb = jnp.concatenate([z(80), sin, z(32)], axis=1)
    return cosq, cosk, sina, sinb


def _local_step(x, mod, positions, ng, w_in, gq, gkv, w_uq, w_ukv, sinks, w_out, fg, tgt,
                ts=512, fq=512, fk=512, bq=512, bk=512):
    B, S, D = x.shape
    T = B * S
    x2 = x.reshape(T, D)
    shift, scale, gate = (mod[:, None, k * D:(k + 1) * D] for k in range(3))
    w_in_p = _pad_w_in(w_in)
    w_uq_p = jnp.pad(w_uq.reshape(Q_LORA, MLA_HEADS, 96), ((0, 0), (0, 0), (0, 32))).reshape(Q_LORA, MLA_HEADS * LANE)
    w_ukv3 = w_ukv.reshape(KV_LORA, MLA_HEADS, 128)
    w_uk_p = jnp.pad(w_ukv3[:, :, :64], ((0, 0), (0, 0), (0, 64))).reshape(KV_LORA, MLA_HEADS * LANE)
    w_uv = w_ukv3[:, :, 64:].reshape(KV_LORA, 512)
    w_out_p = jnp.concatenate([w_out[:512], _pair_perm(w_out[512:], 0, PAIR_ORDER)], axis=0)
    cosq, cosk, sina, sinb = _rope_tables(positions)
    posf = positions.astype(F32)
    posb = jnp.broadcast_to(posf[:, :, None], (B, S, LANE))
    posr = posf.reshape(B, S // WINDOW, WINDOW)
    sinks_l = jnp.pad(sinks.reshape(1, SWA_HEADS), ((0, 0), (0, LANE - SWA_HEADS)))

    (hb, zq, zkv, ql, kvl, q, k, v, qs, ks, vs, g) = _pre_fwd(
        x2, shift, scale, ng, w_in_p, gq, gkv, w_uq_p, w_uk_p, w_uv, cosq, cosk, sina, sinb, S, ts)
    r3 = lambda a: a.reshape(B, S, a.shape[-1])
    om, lse_m = _mla_fwd(r3(q), r3(k), r3(v), fq, fk)
    osw, lse_s = _swa_fwd(r3(qs), r3(ks), r3(vs), posb, posr, sinks_l)
    dx2, ab, dyb, do, dg, loss_v, dfg, dgate = _post(
        om.reshape(T, 512), osw.reshape(T, 512), g, w_out_p, x2, gate, fg.reshape(1, D), tgt.reshape(T, D), S, ts)
    do3 = r3(do)
    dq, dk, dv = _mla_bwd(r3(q), r3(k), r3(v), om, do3, lse_m, bq, bk)
    dqs, dks, dvs, dsink = _swa_bwd(r3(qs), r3(ks), r3(vs), posb, posr, sinks_l, osw, do3, lse_s)
    f2 = lambda a: a.reshape(T, a.shape[-1])
    gx, dqb, dkb, dvb, dz, dgq, dgkv, dng, dsh, dsc = _pre_bwd(
        f2(dq), f2(dk), f2(dv), f2(dqs), f2(dks), f2(dvs), dg, zq, zkv, x2, dx2, scale, ng, w_in_p, gq, gkv,
        w_uq_p, w_uk_p, w_uv, cosq, cosk, sina, sinb, S, ts // 2)
    tk = min(T, 1024)
    dw_in = _unpad_w_in(_tn_matmul(hb, dz, 512, tk, "dw_in"))
    dw_out_p = _tn_matmul(ab, dyb, 512, tk, "dw_out")
    dw_out = jnp.concatenate([dw_out_p[:512], _pair_perm(dw_out_p[512:], 0, PAIR_INV)], axis=0)
    dw_uq = _tn_matmul(ql, dqb, 512, tk, "dw_uq").reshape(Q_LORA, MLA_HEADS, LANE)[:, :, :96].reshape(Q_LORA, 768)
    dw_uk = _tn_matmul(kvl, dkb, 512, tk, "dw_uk").reshape(KV_LORA, MLA_HEADS, LANE)[:, :, :64]
    dw_uv = _tn_matmul(kvl, dvb, 512, tk, "dw_uv").reshape(KV_LORA, MLA_HEADS, 64)
    dw_ukv = jnp.concatenate([dw_uk, dw_uv], axis=2).reshape(KV_LORA, 1024)
    parts = jnp.concatenate([loss_v, dfg, dng, dgq, dgkv, dsink], axis=1)
    dmod = jnp.concatenate([dsh, dsc, dgate], axis=2).reshape(B, 3 * D)
    return gx.reshape(B, S, D), dw_in, dw_uq, dw_ukv, dw_out, parts, dmod


def kernel(x, c, positions, w_ada, b_ada, norm_gain, w_in, q_norm_gain, kv_norm_gain, w_uq, w_ukv, swa_sinks, w_out, final_gain, loss_target, m_w_ada, m_b_ada, m_norm_gain, m_w_in, m_q_norm_gain, m_kv_norm_gain, m_w_uq, m_w_ukv, m_swa_sinks, m_w_out, m_final_gain, v_w_ada, v_b_ada, v_norm_gain, v_w_in, v_q_norm_gain, v_kv_norm_gain, v_w_uq, v_w_ukv, v_swa_sinks, v_w_out, v_final_gain):
    B, S, D = x.shape
    me = 4 * lax.axis_index("x") + 2 * lax.axis_index("y") + lax.axis_index("c")
    bf = _MXU_DTYPE

    c_all, win_g, wuq_g, wukv_g, wout_g = _all_gather(
        [c, w_in[0].astype(bf), w_uq[0].astype(bf), w_ukv[0].astype(bf), w_out[0].astype(bf)], "ag_weights")
    c_all = c_all.reshape(N_DEV * B, D)
    cat_cols = lambda a: jnp.transpose(a, (1, 0, 2)).reshape(a.shape[1], N_DEV * a.shape[2])
    w_in_f, w_uq_f, w_ukv_f = cat_cols(win_g), cat_cols(wuq_g), cat_cols(wukv_g)
    w_out_f = wout_g.reshape(D, D)

    ncol = w_ada.shape[2]
    b_cols = lax.dynamic_slice_in_dim(b_ada, me * ncol, ncol, axis=1)
    (mod_g,) = _all_gather([_ada_fwd(c_all, w_ada[0], b_cols)], "ag_mod")
    mod = lax.dynamic_slice_in_dim(mod_g, me * B, B, axis=1)
    mod = jnp.transpose(mod, (1, 0, 2)).reshape(B, 3 * D)

    gx, dw_in, dw_uq, dw_ukv, dw_out, parts, dmod = _local_step(
        x, mod, positions, norm_gain, w_in_f, q_norm_gain, kv_norm_gain, w_uq_f, w_ukv_f, swa_sinks,
        w_out_f, final_gain, loss_target)

    parts_g, dmod_g = _all_gather([parts, dmod], "ag_small")
    dmod_all = dmod_g.reshape(N_DEV * B, 3 * D)
    dmod_cols = lax.dynamic_slice_in_dim(dmod_all, me * ncol, ncol, axis=1)
    psum, loss, g_b_ada, g_w_ada = _finalize(parts_g, dmod_all, dmod_cols, c_all)
    loss = loss.reshape(())
    o = LANE
    g_final_gain = psum[0, o:o + D]
    g_norm_gain = psum[:, o + D:o + 2 * D]
    o += 2 * D
    g_q_norm_gain = psum[:, o:o + Q_LORA]
    g_kv_norm_gain = psum[:, o + Q_LORA:o + Q_LORA + KV_LORA]
    o += Q_LORA + KV_LORA
    g_sinks = psum[:, o:o + SWA_HEADS]

    split_cols = lambda a: jnp.transpose(a.reshape(a.shape[0], N_DEV, a.shape[1] // N_DEV), (1, 0, 2))
    g_w_in, g_w_uq, g_w_ukv, g_w_out = _reduce_scatter(
        [split_cols(dw_in), split_cols(dw_uq), split_cols(dw_ukv), dw_out.reshape(N_DEV, D // N_DEV, D)], "rs_grads")

    grads = [g_w_ada, g_b_ada, g_norm_gain, g_w_in, g_q_norm_gain, g_kv_norm_gain, g_w_uq, g_w_ukv, g_sinks, g_w_out,
             g_final_gain]
    ws = [w_ada, b_ada, norm_gain, w_in, q_norm_gain, kv_norm_gain, w_uq, w_ukv, swa_sinks, w_out, final_gain]
    ms = [m_w_ada, m_b_ada, m_norm_gain, m_w_in, m_q_norm_gain, m_kv_norm_gain, m_w_uq, m_w_ukv, m_swa_sinks, m_w_out,
          m_final_gain]
    vs = [v_w_ada, v_b_ada, v_norm_gain, v_w_in, v_q_norm_gain, v_kv_norm_gain, v_w_uq, v_w_ukv, v_swa_sinks, v_w_out,
          v_final_gain]
    names = ["w_ada", "b_ada", "norm_gain", "w_in", "q_norm_gain", "kv_norm_gain", "w_uq", "w_ukv", "swa_sinks",
             "w_out", "final_gain"]
    out_g, out_d, out_m, out_v = [], [], [], []
    for name, w, g, m, v in zip(names, ws, grads, ms, vs):
        two_d = (1, w.shape[0]) if w.ndim == 1 else w.shape[-2:]
        d, nm, nv = _adamw(w.reshape(two_d), g.reshape(two_d), m.reshape(two_d), v.reshape(two_d), "adamw_" + name)
        out_g.append(g.reshape(w.shape))
        out_d.append(d.reshape(w.shape))
        out_m.append(nm.reshape(w.shape))
        out_v.append(nv.reshape(w.shape))
    return (loss, gx, *out_g, *out_d, *out_m, *out_v)
```

```python
import functools

import numpy as np
import jax
import jax.numpy as jnp
from jax import lax
from jax.experimental import pallas as pl
from jax.experimental.pallas import tpu as pltpu

F32 = jnp.float32
_MXU_DTYPE = jnp.bfloat16

N_DEV = 8
D_MODEL = 1024
MLA_HEADS = 8
MLA_NOPE = 64
MLA_ROPE = 32
MLA_V = 64
Q_LORA = 384
KV_LORA = 256
SWA_HEADS = 8
SWA_KV_HEADS = 2
SWA_HEAD_DIM = 64
WINDOW = 128
ROPE_THETA = 10000.0
EPS = 1e-6
MLA_SCALE = float((MLA_NOPE + MLA_ROPE) ** -0.5)
SWA_SCALE = float(SWA_HEAD_DIM ** -0.5)
LOG2E = 1.4426950408889634
MLA_QSCALE = MLA_SCALE * LOG2E
D_IN = 2464
D_IN_PAD = 2560
PAIR_ORDER = (0, 4, 1, 5, 2, 6, 3, 7)
PAIR_INV = (0, 2, 4, 6, 1, 3, 5, 7)

ADAM_LR = 0.001
ADAM_B1 = 0.9
ADAM_B2 = 0.999
ADAM_EPS = 1e-08
ADAM_WD = 0.01
ADAM_STEP = 10

LANE = 128
VMEM_LIMIT = 56 * 1024 * 1024

MESH = pl.DeviceIdType.MESH
NEG_INF = float("-inf")
SWA_SEQ_SPLIT = 2


def _mx(a):
    return a.astype(_MXU_DTYPE)


def _dot(a, b):
    return jnp.dot(a, b, preferred_element_type=F32)


def _dot_nt(a, b):
    return lax.dot_general(a, b, (((1,), (1,)), ((), ())), preferred_element_type=F32)


def _dot_tn(a, b):
    return lax.dot_general(a, b, (((0,), (0,)), ((), ())), preferred_element_type=F32)


def _cparams(sem=None):
    return pltpu.CompilerParams(dimension_semantics=sem, vmem_limit_bytes=VMEM_LIMIT)


def _vmem():
    return pl.BlockSpec(memory_space=pltpu.VMEM)


def _lane_iota(shape):
    return lax.broadcasted_iota(jnp.int32, shape, len(shape) - 1)


def _all_gather(arrs, name):
    n = len(arrs)

    def body(*refs):
        ins, outs = refs[:n], refs[n:2 * n]
        send_sems, recv_sems, local_sems = refs[2 * n:]
        x, y, c = lax.axis_index("x"), lax.axis_index("y"), lax.axis_index("c")
        me, sibling = (x, y, c), (x, y, 1 - c)
        chips = [(1 - x, y), (x, 1 - y), (1 - x, 1 - y)]

        def slot(a, dev):
            return outs[a].at[4 * dev[0] + 2 * dev[1] + dev[2]]

        def copy(a, k, block, to, src=None):
            return pltpu.make_async_remote_copy(
                src_ref=slot(a, block) if src is None else src, dst_ref=slot(a, block),
                send_sem=send_sems.at[7 * a + k], recv_sem=recv_sems.at[7 * a + k],
                device_id=to, device_id_type=MESH)

        mine = [pltpu.make_async_copy(ins[a], slot(a, me), local_sems.at[a]) for a in range(n)]
        for cp in mine:
            cp.start()
        first = []
        for a in range(n):
            first.append(copy(a, 0, me, sibling, src=ins[a]))
            first += [copy(a, 1 + j, me, (*chip, c), src=ins[a]) for j, chip in enumerate(chips)]
        for cp in first:
            cp.start()
        passed = []
        for j, chip in enumerate(chips):
            for a in range(n):
                copy(a, 1 + j, (*chip, c), me).wait_recv()
                cp = copy(a, 4 + j, (*chip, c), sibling)
                cp.start()
                passed.append(cp)
        for a in range(n):
            copy(a, 0, sibling, me).wait_recv()
            for j, chip in enumerate(chips):
                copy(a, 4 + j, (*chip, 1 - c), me).wait_recv()
        for cp in first + passed:
            cp.wait_send()
        for cp in mine:
            cp.wait()

    return pl.pallas_call(
        body, name=name,
        out_shape=[jax.ShapeDtypeStruct((N_DEV,) + a.shape, a.dtype) for a in arrs],
        in_specs=[_vmem()] * n, out_specs=[_vmem()] * n,
        scratch_shapes=[pltpu.SemaphoreType.DMA((7 * n,)), pltpu.SemaphoreType.DMA((7 * n,)),
                        pltpu.SemaphoreType.DMA((n,))],
        compiler_params=pltpu.CompilerParams(vmem_limit_bytes=VMEM_LIMIT),
    )(*arrs)


def _reduce_scatter(arrs, name):
    n = len(arrs)
    halves = [a.astype(jnp.bfloat16) for a in arrs]

    def body(*refs):
        xs, xbs, outs = refs[:n], refs[n:2 * n], refs[2 * n:3 * n]
        parts, recv_a, send_b, recv_b = (refs[(3 + t) * n:(4 + t) * n] for t in range(4))
        send_sems, recv_sems, local_sems = refs[7 * n:]
        x, y, c = lax.axis_index("x"), lax.axis_index("y"), lax.axis_index("c")
        myq = 2 * x + y

        def chip(k):
            return (1 - x if k & 2 else x, 1 - y if k & 1 else y)

        def from_sibling(a):
            return pltpu.make_async_remote_copy(
                src_ref=xbs[a].at[1 - c], dst_ref=recv_a[a], send_sem=send_sems.at[4 * a], recv_sem=recv_sems.at[4 * a],
                device_id=(x, y, 1 - c), device_id_type=MESH)

        def to_owner(a, k):
            qx, qy = chip(k)
            return pltpu.make_async_remote_copy(
                src_ref=send_b[a].at[2 * qx + qy], dst_ref=recv_b[a].at[myq],
                send_sem=send_sems.at[4 * a + k], recv_sem=recv_sems.at[4 * a + k],
                device_id=(qx, qy, c), device_id_type=MESH)

        mine = [pltpu.make_async_copy(xs[a].at[c], parts[a], local_sems.at[a]) for a in range(n)]
        first = [from_sibling(a) for a in range(n)]
        for cp in mine + first:
            cp.start()
        second = []
        for a in range(n):
            mine[a].wait()
            first[a].wait_recv()
            parts[a][...] = parts[a][...] + recv_a[a][...].astype(F32)
            send_b[a][...] = parts[a][...].astype(jnp.bfloat16)
            for k in range(1, 4):
                cp = to_owner(a, k)
                cp.start()
                second.append(cp)
        for a in range(n):
            acc = parts[a][myq]
            for k in range(1, 4):
                to_owner(a, k).wait_recv()
                qx, qy = chip(k)
                acc = acc + recv_b[a][2 * qx + qy].astype(F32)
            outs[a][...] = acc
        for cp in first + second:
            cp.wait_send()

    quarter = lambda a, dt: pltpu.VMEM(a.shape[1:], dt)
    return pl.pallas_call(
        body, name=name,
        out_shape=[jax.ShapeDtypeStruct(a.shape[2:], F32) for a in arrs],
        in_specs=[pl.BlockSpec(memory_space=pl.ANY)] * (2 * n), out_specs=[_vmem()] * n,
        scratch_shapes=[quarter(a, F32) for a in arrs] + [quarter(a, jnp.bfloat16) for a in arrs] * 3
        + [pltpu.SemaphoreType.DMA((4 * n,)), pltpu.SemaphoreType.DMA((4 * n,)), pltpu.SemaphoreType.DMA((n,))],
        compiler_params=pltpu.CompilerParams(vmem_limit_bytes=VMEM_LIMIT),
    )(*arrs, *halves)


def _silu(t):
    return t * (1.0 / (1.0 + jnp.exp(-t)))


def _ada_fwd(c_act_all, w_ada_shard, b_shard):
    def body(c_ref, w_ref, b_ref, o_ref):
        o_ref[...] = _dot(_mx(_silu(c_ref[...])), _mx(w_ref[...])) + b_ref[...]

    return pl.pallas_call(
        body, name="ada_fwd",
        out_shape=jax.ShapeDtypeStruct((c_act_all.shape[0], w_ada_shard.shape[1]), F32),
        in_specs=[_vmem()] * 3, out_specs=_vmem(),
        compiler_params=_cparams(),
    )(c_act_all, w_ada_shard, b_shard)


def _rope_split(rt):
    lane = _lane_iota(rt.shape)
    sn = pltpu.roll(rt, 32, 1)
    ck = jnp.where(jnp.logical_and(lane >= 64, lane < 96), rt, 0.0)
    cq = jnp.where(lane < 64, 1.0, ck)
    sa = jnp.where(jnp.logical_and(lane >= 64, lane < 80), sn, 0.0)
    sb = jnp.where(jnp.logical_and(lane >= 80, lane < 96), sn, 0.0)
    return cq, ck, sa, sb


def _tile_heads(t):
    return jnp.concatenate([t] * MLA_HEADS, axis=1)


def _pre_fwd(x2, shift, scale, ng, w_in, gq, gkv, w_uq, w_uk, w_uv, rope, S, ts):
    T, D = x2.shape
    nsb = S // ts
    WQ = MLA_HEADS * LANE

    def body(x_ref, sh_ref, sc_ref, ng_ref, w_ref, gq_ref, gkv_ref, wuq_ref, wuk_ref, wuv_ref,
             rt_ref,
             hb_ref, zq_ref, zkv_ref, ql_ref, kvl_ref, q_ref, k_ref, v_ref, qs_ref, ks_ref, vs_ref, g_ref):
        x = x_ref[...]
        r1 = lax.rsqrt(jnp.mean(x * x, axis=-1, keepdims=True) + EPS)
        h = ((x * r1) * ng_ref[...]) * (1.0 + sc_ref[0]) + sh_ref[0]
        hb = _mx(h)
        hb_ref[...] = hb
        zq = _dot(hb, w_ref[:, 0:384])
        zq_ref[...] = zq
        rq = lax.rsqrt(jnp.mean(zq * zq, axis=-1, keepdims=True) + EPS)
        ql = _mx((zq * rq) * gq_ref[...])
        ql_ref[...] = ql
        q = _dot(ql, wuq_ref[...])
        cq, ck, sa, sb = _rope_split(rt_ref[...])
        q = (q * _tile_heads(cq) + pltpu.roll(q, WQ - 16, 1) * _tile_heads(sa)
             + pltpu.roll(q, 16, 1) * _tile_heads(sb))
        q_ref[...] = _mx(q * MLA_QSCALE)
        zkv = _dot(hb, w_ref[:, 384:640])
        zkv_ref[...] = zkv
        rkv = lax.rsqrt(jnp.mean(zkv * zkv, axis=-1, keepdims=True) + EPS)
        kvl = _mx((zkv * rkv) * gkv_ref[...])
        kvl_ref[...] = kvl
        kr = _dot(hb, w_ref[:, 640:768])
        kpe = kr * ck + pltpu.roll(kr, LANE - 16, 1) * sa + pltpu.roll(kr, 16, 1) * sb
        kf = _dot(kvl, wuk_ref[...])
        k_ref[...] = _mx(kf + jnp.concatenate([kpe] * MLA_HEADS, axis=1))
        v_ref[...] = _mx(_dot(kvl, wuv_ref[...]))
        g_ref[:, 0:512] = _dot(hb, w_ref[:, 768:1280])
        qs_ref[...] = _mx(_dot(hb, w_ref[:, 1280:1792]) * SWA_SCALE)
        ks_ref[...] = _mx(_dot(hb, w_ref[:, 1792:1920]))
        vs_ref[...] = _mx(_dot(hb, w_ref[:, 1920:2048]))
        g_ref[:, 512:1024] = _dot(hb, w_ref[:, 2048:2560])

    row = lambda w: pl.BlockSpec((ts, w), lambda i: (i, 0))
    full = lambda a: pl.BlockSpec(a.shape, lambda i: (0,) * a.ndim)
    per_b = pl.BlockSpec((1, 1, D), lambda i: (i // nsb, 0, 0))
    out_w = [(D, _MXU_DTYPE), (384, F32), (256, F32), (384, _MXU_DTYPE), (256, _MXU_DTYPE), (WQ, _MXU_DTYPE),
             (WQ, _MXU_DTYPE), (512, _MXU_DTYPE), (512, _MXU_DTYPE), (128, _MXU_DTYPE), (128, _MXU_DTYPE), (1024, F32)]
    return pl.pallas_call(
        body, name="pre_fwd", grid=(T // ts,),
        out_shape=[jax.ShapeDtypeStruct((T, w), dt) for w, dt in out_w],
        in_specs=[row(D), per_b, per_b, full(ng), full(w_in), full(gq), full(gkv), full(w_uq), full(w_uk), full(w_uv),
                  row(LANE)],
        out_specs=[row(w) for w, _ in out_w],
        compiler_params=_cparams(("arbitrary",)),
    )(x2, shift, scale, ng, w_in, gq, gkv, w_uq, w_uk, w_uv, rope)


def _mla_fwd(q3, k3, v3, tq, tk):
    B, S, _ = q3.shape
    nq = S // tq
    assert tk % tq == 0 or tq % tk == 0
    n_masked = max(1, tq // tk)

    def body(q_ref, k_ref, v_ref, o_ref, lse_ref):
        qi = pl.program_id(2)
        rows = lax.broadcasted_iota(jnp.int32, (tq, tk), 0)
        cols = lax.broadcasted_iota(jnp.int32, (tq, tk), 1)
        qs = [q_ref[0, :, LANE * j:LANE * (j + 1)] for j in range(2)]
        n_full = (qi * tq) // tk

        def step(kt, carry, masked):
            r0 = pl.multiple_of(kt * tk, tk)
            v2 = v_ref[0, pl.ds(r0, tk), :]
            out = []
            for j in range(2):
                m, l, acc = carry[j]
                s = _dot_nt(qs[j], k_ref[0, pl.ds(r0, tk), LANE * j:LANE * (j + 1)])
                if masked:
                    s = jnp.where(rows + qi * tq >= cols + kt * tk, s, NEG_INF)
                m_new = jnp.maximum(m, jnp.max(s, axis=1, keepdims=True))
                alpha = jnp.exp2(m - m_new)
                p = jnp.exp2(s - m_new)
                l = alpha * l + jnp.sum(p, axis=1, keepdims=True)
                acc = alpha * acc + _dot(_mx(p), v2)
                out.append((m_new, l, acc))
            return tuple(out)

        init = (jnp.full((tq, 1), NEG_INF, F32), jnp.zeros((tq, 1), F32), jnp.zeros((tq, LANE), F32))
        carry = lax.fori_loop(0, n_full, functools.partial(step, masked=False), (init, init))
        for t in range(n_masked):
            carry = step(n_full + t, carry, True)
        (m0, l0, a0), (m1, l1, a1) = carry
        o_ref[0] = jnp.where(_lane_iota((tq, LANE)) < 64, a0 / l0, a1 / l1)
        lse_ref[0, 0] = jnp.broadcast_to(m0 + jnp.log2(l0), (tq, LANE))
        lse_ref[0, 1] = jnp.broadcast_to(m1 + jnp.log2(l1), (tq, LANE))

    return pl.pallas_call(
        body, name="mla_fwd", grid=(B, MLA_HEADS // 2, nq),
        out_shape=[jax.ShapeDtypeStruct((B, S, 512), F32), jax.ShapeDtypeStruct((B, MLA_HEADS, S, LANE), F32)],
        in_specs=[pl.BlockSpec((1, tq, 2 * LANE), lambda b, hp, i: (b, i, hp)),
                  pl.BlockSpec((1, S, 2 * LANE), lambda b, hp, i: (b, 0, hp)),
                  pl.BlockSpec((1, S, LANE), lambda b, hp, i: (b, 0, hp))],
        out_specs=[pl.BlockSpec((1, tq, LANE), lambda b, hp, i: (b, i, hp)),
                   pl.BlockSpec((1, 2, tq, LANE), lambda b, hp, i: (b, hp, i, 0))],
        compiler_params=_cparams(("arbitrary", "arbitrary", "arbitrary")),
    )(q3, k3, v3)


def _mla_bwd(q3, k3, v3, o3, do3, lse, tq, tk):
    B, S, _ = q3.shape
    nq, nk = S // tq, S // tk
    assert tk % tq == 0

    def body(q_ref, k_ref, v_ref, o_ref, do_ref, lse_ref, dq_ref, dk_ref, dv_ref, dkt_ref, dvt_ref):
        dkt_ref[...] = jnp.zeros_like(dkt_ref)
        dvt_ref[...] = jnp.zeros_like(dvt_ref)
        rows = lax.broadcasted_iota(jnp.int32, (tq, tk), 0)
        cols = lax.broadcasted_iota(jnp.int32, (tq, tk), 1)
        lane = _lane_iota((tq, LANE))

        def q_tile(qi, _):
            r = pl.ds(pl.multiple_of(qi * tq, tq), tq)
            do2, o2 = do_ref[0, r, :], o_ref[0, r, :]
            heads = []
            for j in range(2):
                lanes = slice(LANE * j, LANE * (j + 1))
                do = jnp.where((lane < 64) if j == 0 else (lane >= 64), do2, 0.0)
                q = q_ref[0, r, lanes]
                heads.append((lanes, q, _mx(q.astype(F32).T), _mx(do), _mx(do.T),
                              jnp.sum(do * o2, axis=1, keepdims=True),
                              jnp.concatenate([lse_ref[0, j, r, :]] * (tk // LANE), axis=1)))
            n_full = (qi * tq) // tk

            def k_tile(kt, dqs, masked):
                kr = pl.ds(pl.multiple_of(kt * tk, tk), tk)
                v2 = v_ref[0, kr, :]
                out = []
                dvt = None
                for j, (lanes, q, qt, dob, dot_, dcol, lse_c) in enumerate(heads):
                    k = k_ref[0, kr, lanes]
                    s = _dot_nt(q, k)
                    if masked:
                        s = jnp.where(rows + qi * tq >= cols + kt * tk, s, NEG_INF)
                    p = jnp.exp2(s - lse_c)
                    dp = _dot_nt(dob, v2)
                    dsb = _mx(p * (dp - dcol))
                    dkt_ref[j, kt] += _dot(qt, dsb)
                    pv = _dot(dot_, _mx(p))
                    dvt = pv if dvt is None else dvt + pv
                    out.append(dqs[j] + _dot(dsb, k))
                dvt_ref[kt] += dvt
                return tuple(out)

            zero = jnp.zeros((tq, LANE), F32)
            dqs = lax.fori_loop(0, n_full, functools.partial(k_tile, masked=False), (zero, zero))
            dqs = k_tile(n_full, dqs, True)
            for j in range(2):
                dq_ref[0, r, heads[j][0]] = MLA_SCALE * dqs[j]
            return 0

        lax.fori_loop(0, nq, q_tile, 0)

        def flush(kt, _):
            kr = pl.ds(pl.multiple_of(kt * tk, tk), tk)
            for j in range(2):
                dk_ref[0, kr, LANE * j:LANE * (j + 1)] = (1.0 / LOG2E) * dkt_ref[j, kt].T
            dv_ref[0, kr, :] = dvt_ref[kt].T
            return 0

        lax.fori_loop(0, nk, flush, 0)

    pair = lambda w: pl.BlockSpec((1, S, w), lambda b, hp: (b, 0, hp))
    return pl.pallas_call(
        body, name="mla_bwd", grid=(B, MLA_HEADS // 2),
        out_shape=[jax.ShapeDtypeStruct((B, S, 1024), F32), jax.ShapeDtypeStruct((B, S, 1024), F32),
                   jax.ShapeDtypeStruct((B, S, 512), F32)],
        in_specs=[pair(2 * LANE), pair(2 * LANE), pair(LANE), pair(LANE), pair(LANE),
                  pl.BlockSpec((1, 2, S, LANE), lambda b, hp: (b, hp, 0, 0))],
        out_specs=[pair(2 * LANE), pair(2 * LANE), pair(LANE)],
        scratch_shapes=[pltpu.VMEM((2, nk, LANE, tk), F32), pltpu.VMEM((nk, LANE, tk), F32)],
        compiler_params=_cparams(("arbitrary", "arbitrary")),
    )(q3, k3, v3, o3, do3, lse)


def _swa_consts(sink_ref):
    W = WINDOW
    row = lax.broadcasted_iota(jnp.int32, (4 * W, LANE), 0)
    out = []
    for g in range(SWA_KV_HEADS):
        slope = jnp.zeros((4 * W, LANE), F32)
        sink = jnp.zeros((4 * W, LANE), F32)
        for p in range(4):
            h = p + 4 * g
            here = jnp.logical_and(row >= W * p, row < W * (p + 1))
            slope = jnp.where(here, float(2.0 ** (-8.0 * (h + 1) / SWA_HEADS)), slope)
            sink = jnp.where(here, sink_ref[:, h:h + 1], sink)
        out.append((jnp.concatenate([slope, slope], axis=1), sink))
    return out


def _wide(col):
    return jnp.concatenate([col, col], axis=1)


def _swa_block(q_ref, k_ref, v_ref, pb_ref, pr_ref, n, i, ij):
    W = WINDOW
    kb = jnp.maximum(n - 1, 0)
    r = pl.ds(pl.multiple_of(i * W, W), W)
    kr = pl.ds(pl.multiple_of(kb * W, W), 2 * W)
    q4, k2, v2 = q_ref[0, r, :], k_ref[0, kr, :], v_ref[0, kr, :]
    pq = _wide(pb_ref[0, r, :])
    pk = jnp.concatenate([pr_ref[0, pl.ds(kb, 1), :], pr_ref[0, pl.ds(kb + 1, 1), :]], axis=1)
    rel = ij + (n - kb) * W
    mask = jnp.where(jnp.logical_and(rel >= 0, rel < W), 0.0, NEG_INF)
    dist4 = jnp.concatenate([pq - pk] * 4, axis=0)
    mask4 = jnp.concatenate([mask] * 4, axis=0)
    return r, kb, kr, q4, k2, v2, dist4, mask4


def _swa_stack(x4, g, dtype):
    lane = _lane_iota((WINDOW, LANE))
    mine = (lane < 64) if g == 0 else (lane >= 64)
    return jnp.concatenate([jnp.where(mine, x4[:, LANE * p:LANE * (p + 1)], 0).astype(dtype) for p in range(4)], axis=0)


def _swa_unstack(ref, r, lo, hi, scale=None):
    W = WINDOW
    low = _lane_iota((W, LANE)) < 64
    for p in range(4):
        t = jnp.where(low, lo[W * p:W * (p + 1)], hi[W * p:W * (p + 1)])
        ref[0, r, LANE * p:LANE * (p + 1)] = t if scale is None else scale * t


def _swa_fwd(qs3, ks3, vs3, posb, posr, sinks):
    B, S, _ = qs3.shape
    W = WINDOW
    nb = S // W
    nh = SWA_SEQ_SPLIT
    nbh = nb // nh

    def body(q_ref, k_ref, v_ref, pb_ref, pr_ref, sink_ref, o_ref, lse_ref):
        ij = lax.broadcasted_iota(jnp.int32, (W, 2 * W), 0) - lax.broadcasted_iota(jnp.int32, (W, 2 * W), 1)
        consts = _swa_consts(sink_ref)
        n0 = pl.program_id(1) * nbh

        def blk(i, _):
            n = n0 + i
            r, _, _, q4, k2, v2, dist4, mask4 = _swa_block(q_ref, k_ref, v_ref, pb_ref, pr_ref, n, i, ij)
            o_g = []
            for g, (slope, sink) in enumerate(consts):
                s = _dot_nt(_swa_stack(q4, g, q4.dtype), k2) - slope * dist4 + mask4
                m = jnp.maximum(jnp.max(s, axis=1, keepdims=True), sink)
                e = jnp.exp(s - _wide(m))
                l = jnp.sum(e, axis=1, keepdims=True) + jnp.exp(sink - m)
                o_g.append(_dot(_mx(e * _wide(1.0 / l)), v2))
                lse_ref[0, i, g] = m + jnp.log(l)
            _swa_unstack(o_ref, r, o_g[0], o_g[1])
            return 0

        lax.fori_loop(0, nbh, blk, 0)

    seq = lambda w: pl.BlockSpec((1, S, w), lambda b, h: (b, 0, 0))
    part = lambda w: pl.BlockSpec((1, S // nh, w), lambda b, h: (b, h, 0))
    lse_spec = pl.BlockSpec((1, nbh, 2, 4 * W, LANE), lambda b, h: (b, h, 0, 0, 0))
    return pl.pallas_call(
        body, name="swa_fwd", grid=(B, nh),
        out_shape=[jax.ShapeDtypeStruct((B, S, 512), F32), jax.ShapeDtypeStruct((B, nb, 2, 4 * W, LANE), F32)],
        in_specs=[part(512), seq(LANE), seq(LANE), part(LANE), pl.BlockSpec((1, nb, W), lambda b, h: (b, 0, 0)),
                  pl.BlockSpec((1, LANE), lambda b, h: (0, 0))],
        out_specs=[part(512), lse_spec],
        compiler_params=_cparams(("arbitrary", "arbitrary")),
    )(qs3, ks3, vs3, posb, posr, sinks)


def _swa_bwd(qs3, ks3, vs3, posb, posr, sinks, os3, do3, lse):
    B, S, _ = qs3.shape
    W = WINDOW
    nb = S // W
    nh = SWA_SEQ_SPLIT
    nbh = nb // nh

    def body(q_ref, k_ref, v_ref, pb_ref, pr_ref, sink_ref, o_ref, do_ref, lse_ref, dq_ref, dk_ref, dv_ref, dsink_ref,
             dkt_ref, dvt_ref):
        lane1 = _lane_iota((1, LANE))
        ij = lax.broadcasted_iota(jnp.int32, (W, 2 * W), 0) - lax.broadcasted_iota(jnp.int32, (W, 2 * W), 1)
        consts = _swa_consts(sink_ref)
        hh = pl.program_id(1)
        n0 = hh * nbh

        @pl.when(hh == 0)
        def _():
            dkt_ref[...] = jnp.zeros_like(dkt_ref)
            dvt_ref[...] = jnp.zeros_like(dvt_ref)

        @pl.when(jnp.logical_and(pl.program_id(0) == 0, hh == 0))
        def _():
            dsink_ref[...] = jnp.zeros_like(dsink_ref)

        def blk(i, dsink):
            n = n0 + i
            r, kb, _, q4, k2, v2, dist4, mask4 = _swa_block(q_ref, k_ref, v_ref, pb_ref, pr_ref, n, i, ij)
            o4, do4 = o_ref[0, r, :], do_ref[0, r, :]
            dq_g = []
            dkt = jnp.zeros((LANE, 2 * W), F32)
            dvt = jnp.zeros((LANE, 2 * W), F32)
            for g, (slope, sink) in enumerate(consts):
                q_st = _swa_stack(q4, g, F32)
                do_st = _swa_stack(do4, g, F32)
                dcol = jnp.sum(do_st * _swa_stack(o4, g, F32), axis=1, keepdims=True)
                qst, dob = _mx(q_st), _mx(do_st)
                lse_c = lse_ref[0, i, g]
                pr = jnp.exp(_dot_nt(qst, k2) - slope * dist4 + mask4 - _wide(lse_c))
                dsb = _mx(pr * (_dot_nt(dob, v2) - dcol))
                psd = jnp.exp(sink - lse_c)[:, 0:1] * dcol
                for p in range(4):
                    dsink = dsink - jnp.where(lane1 == p + 4 * g,
                                              jnp.sum(psd[W * p:W * (p + 1)], axis=0, keepdims=True), 0.0)
                dq_g.append(_dot(dsb, k2))
                dkt = dkt + _dot(_mx(q_st.T), dsb)
                dvt = dvt + _dot(_mx(do_st.T), _mx(pr))
            _swa_unstack(dq_ref, r, dq_g[0], dq_g[1], SWA_SCALE)
            dkt_ref[kb] += dkt[:, 0:W]
            dkt_ref[kb + 1] += dkt[:, W:2 * W]
            dvt_ref[kb] += dvt[:, 0:W]
            dvt_ref[kb + 1] += dvt[:, W:2 * W]
            return dsink

        dsink_ref[...] += lax.fori_loop(0, nbh, blk, jnp.zeros((1, LANE), F32))

        @pl.when(hh == nh - 1)
        def _():
            def flush(n, _):
                r = pl.ds(pl.multiple_of(n * W, W), W)
                dk_ref[0, r, :] = dkt_ref[n].T
                dv_ref[0, r, :] = dvt_ref[n].T
                return 0

            lax.fori_loop(0, nb, flush, 0)

    seq = lambda w: pl.BlockSpec((1, S, w), lambda b, h: (b, 0, 0))
    part = lambda w: pl.BlockSpec((1, S // nh, w), lambda b, h: (b, h, 0))
    return pl.pallas_call(
        body, name="swa_bwd", grid=(B, nh),
        out_shape=[jax.ShapeDtypeStruct((B, S, 512), F32), jax.ShapeDtypeStruct((B, S, LANE), F32),
                   jax.ShapeDtypeStruct((B, S, LANE), F32), jax.ShapeDtypeStruct((1, LANE), F32)],
        in_specs=[part(512), seq(LANE), seq(LANE), part(LANE), pl.BlockSpec((1, nb, W), lambda b, h: (b, 0, 0)),
                  pl.BlockSpec((1, LANE), lambda b, h: (0, 0)), part(512),
                  pl.BlockSpec((1, S // nh, 512), lambda b, h: (b, h, 1)),
                  pl.BlockSpec((1, nbh, 2, 4 * W, LANE), lambda b, h: (b, h, 0, 0, 0))],
        out_specs=[part(512), seq(LANE), seq(LANE), pl.BlockSpec((1, LANE), lambda b, h: (0, 0))],
        scratch_shapes=[pltpu.VMEM((nb, LANE, W), F32), pltpu.VMEM((nb, LANE, W), F32)],
        compiler_params=_cparams(("arbitrary", "arbitrary")),
    )(qs3, ks3, vs3, posb, posr, sinks, os3, do3, lse)


def _post(om, osw, g, w_out, x2, gate, fg, tgt, S, ts):
    T, D = x2.shape
    nsb = S // ts

    def body(om_ref, os_ref, g_ref, w_ref, x_ref, gate_ref, fg_ref, t_ref,
             dx2_ref, ab_ref, dyb_ref, do_ref, dg_ref, loss_ref, dfg_ref, dgate_ref):
        i = pl.program_id(0)
        gv = g_ref[...]
        sg = 1.0 / (1.0 + jnp.exp(-gv))
        silu = gv * sg
        o = jnp.concatenate([om_ref[...], os_ref[...]], axis=1)
        ab = _mx(o * silu)
        ab_ref[...] = ab
        y = _dot(ab, w_ref[...])
        gate = gate_ref[0]
        xo = x_ref[...] + gate * y
        r2 = lax.rsqrt(jnp.mean(xo * xo, axis=-1, keepdims=True) + EPS)
        xh = xo * r2
        fg = fg_ref[...]
        diff = xh * fg - t_ref[...]
        sq = jnp.sum(diff * diff, axis=0, keepdims=True)
        part = sq[:, 0:LANE]
        for t in range(1, D // LANE):
            part = part + sq[:, LANE * t:LANE * (t + 1)]
        dout = diff * (1.0 / D)
        dxh = dout * fg
        dx2 = r2 * (dxh - xh * jnp.mean(dxh * xh, axis=-1, keepdims=True))
        dx2_ref[...] = dx2
        dyb = _mx(dx2 * gate)
        dyb_ref[...] = dyb
        da = _dot_nt(dyb, w_ref[...])
        do_ref[...] = da * silu
        dg_ref[...] = _mx(da * o * (sg * (1.0 + gv * (1.0 - sg))))

        @pl.when(i == 0)
        def _():
            loss_ref[...] = jnp.zeros_like(loss_ref)
            dfg_ref[...] = jnp.zeros_like(dfg_ref)

        @pl.when(i % nsb == 0)
        def _():
            dgate_ref[...] = jnp.zeros_like(dgate_ref)

        loss_ref[...] += (0.5 / D) * part
        dfg_ref[...] += jnp.sum(dout * xh, axis=0, keepdims=True)
        dgate_ref[0] += jnp.sum(dx2 * y, axis=0, keepdims=True)

    row = lambda w: pl.BlockSpec((ts, w), lambda i: (i, 0))
    full = lambda a: pl.BlockSpec(a.shape, lambda i: (0,) * a.ndim)
    per_b = pl.BlockSpec((1, 1, D), lambda i: (i // nsb, 0, 0))
    return pl.pallas_call(
        body, name="post", grid=(T // ts,),
        out_shape=[jax.ShapeDtypeStruct((T, D), F32), jax.ShapeDtypeStruct((T, D), _MXU_DTYPE),
                   jax.ShapeDtypeStruct((T, D), _MXU_DTYPE), jax.ShapeDtypeStruct((T, 1024), F32),
                   jax.ShapeDtypeStruct((T, 1024), _MXU_DTYPE), jax.ShapeDtypeStruct((1, LANE), F32),
                   jax.ShapeDtypeStruct((1, D), F32), jax.ShapeDtypeStruct(gate.shape, F32)],
        in_specs=[row(512), row(512), row(1024), full(w_out), row(D), per_b, full(fg), row(D)],
        out_specs=[row(D), row(D), row(D), row(1024), row(1024),
                   pl.BlockSpec((1, LANE), lambda i: (0, 0)), pl.BlockSpec((1, D), lambda i: (0, 0)), per_b],
        compiler_params=_cparams(("arbitrary",)),
    )(om, osw, g, w_out, x2, gate, fg, tgt)


def _pre_bwd(dq, dk, dv, dqs, dks, dvs, dg, zq, zkv, x2, dx2, scale, ng, w_in, gq, gkv, w_uq, w_uk, w_uv,
             rope, S, ts):
    T, D = x2.shape
    nsb = S // ts
    WQ = MLA_HEADS * LANE

    def body(dq_ref, dk_ref, dv_ref, dqs_ref, dks_ref, dvs_ref, dg_ref, zq_ref, zkv_ref, x_ref, dx2_ref, sc_ref,
             ng_ref, w_ref, gq_ref, gkv_ref, wuq_ref, wuk_ref, wuv_ref, rt_ref,
             gx_ref, dqb_ref, dkb_ref, dvb_ref, dz_ref, dgq_ref, dgkv_ref, dng_ref, dsh_ref, dsc_ref):
        i = pl.program_id(0)

        @pl.when(i == 0)
        def _():
            dgq_ref[...] = jnp.zeros_like(dgq_ref)
            dgkv_ref[...] = jnp.zeros_like(dgkv_ref)
            dng_ref[...] = jnp.zeros_like(dng_ref)

        @pl.when(i % nsb == 0)
        def _():
            dsh_ref[...] = jnp.zeros_like(dsh_ref)
            dsc_ref[...] = jnp.zeros_like(dsc_ref)

        def norm_bwd(z, dy, gain):
            r = lax.rsqrt(jnp.mean(z * z, axis=-1, keepdims=True) + EPS)
            zh = z * r
            dzh = dy * gain
            return r * (dzh - zh * jnp.mean(dzh * zh, axis=-1, keepdims=True)), jnp.sum(dy * zh, axis=0, keepdims=True)

        cq, ck, sa, sb = _rope_split(rt_ref[...])
        dqr = dq_ref[...]
        dqb = _mx(dqr * _tile_heads(cq) + pltpu.roll(dqr * _tile_heads(sa), 16, 1)
                  + pltpu.roll(dqr * _tile_heads(sb), WQ - 16, 1))
        dqb_ref[...] = dqb
        dzq, dgq = norm_bwd(zq_ref[...], _dot_nt(dqb, wuq_ref[...]), gq_ref[...])
        dgq_ref[...] += dgq
        dkr = dk_ref[...]
        dkb = _mx(dkr)
        dkb_ref[...] = dkb
        dvb = _mx(dv_ref[...])
        dvb_ref[...] = dvb
        dzkv, dgkv = norm_bwd(zkv_ref[...], _dot_nt(dkb, wuk_ref[...]) + _dot_nt(dvb, wuv_ref[...]), gkv_ref[...])
        dgkv_ref[...] += dgkv
        dkpe = dkr[:, 0:LANE]
        for h in range(1, MLA_HEADS):
            dkpe = dkpe + dkr[:, LANE * h:LANE * (h + 1)]
        dkro = dkpe * ck + pltpu.roll(dkpe * sa, 16, 1) + pltpu.roll(dkpe * sb, LANE - 16, 1)
        dz_ref[:, 0:384] = _mx(dzq)
        dz_ref[:, 384:640] = _mx(dzkv)
        dz_ref[:, 640:768] = _mx(dkro)
        dz_ref[:, 768:1280] = dg_ref[:, 0:512]
        dz_ref[:, 1280:1792] = _mx(dqs_ref[...])
        dz_ref[:, 1792:1920] = _mx(dks_ref[...])
        dz_ref[:, 1920:2048] = _mx(dvs_ref[...])
        dz_ref[:, 2048:2560] = dg_ref[:, 512:1024]
        dh = _dot_nt(dz_ref[...], w_ref[...])
        x = x_ref[...]
        r1 = lax.rsqrt(jnp.mean(x * x, axis=-1, keepdims=True) + EPS)
        xn = x * r1
        ng = ng_ref[...]
        sc1 = 1.0 + sc_ref[0]
        dsh_ref[0] += jnp.sum(dh, axis=0, keepdims=True)
        dsc_ref[0] += jnp.sum(dh * (xn * ng), axis=0, keepdims=True)
        dng_ref[...] += jnp.sum(dh * xn * sc1, axis=0, keepdims=True)
        dxn = dh * ng * sc1
        gx_ref[...] = dx2_ref[...] + r1 * (dxn - xn * jnp.mean(dxn * xn, axis=-1, keepdims=True))

    row = lambda w: pl.BlockSpec((ts, w), lambda i: (i, 0))
    full = lambda a: pl.BlockSpec(a.shape, lambda i: (0,) * a.ndim)
    per_b = pl.BlockSpec((1, 1, D), lambda i: (i // nsb, 0, 0))
    vec = lambda w: pl.BlockSpec((1, w), lambda i: (0, 0))
    return pl.pallas_call(
        body, name="pre_bwd", grid=(T // ts,),
        out_shape=[jax.ShapeDtypeStruct((T, D), F32), jax.ShapeDtypeStruct((T, WQ), _MXU_DTYPE),
                   jax.ShapeDtypeStruct((T, WQ), _MXU_DTYPE), jax.ShapeDtypeStruct((T, 512), _MXU_DTYPE),
                   jax.ShapeDtypeStruct((T, D_IN_PAD), _MXU_DTYPE), jax.ShapeDtypeStruct((1, 384), F32),
                   jax.ShapeDtypeStruct((1, 256), F32), jax.ShapeDtypeStruct((1, D), F32),
                   jax.ShapeDtypeStruct(scale.shape, F32), jax.ShapeDtypeStruct(scale.shape, F32)],
        in_specs=[row(WQ), row(WQ), row(512), row(512), row(LANE), row(LANE), row(1024), row(384), row(256), row(D),
                  row(D), per_b, full(ng), full(w_in), full(gq), full(gkv), full(w_uq), full(w_uk), full(w_uv),
                  row(LANE)],
        out_specs=[row(D), row(WQ), row(WQ), row(512), row(D_IN_PAD), vec(384), vec(256), vec(D), per_b, per_b],
        compiler_params=_cparams(("arbitrary",)),
    )(dq, dk, dv, dqs, dks, dvs, dg, zq, zkv, x2, dx2, scale, ng, w_in, gq, gkv, w_uq, w_uk, w_uv, rope)


def _tn_matmul(a, b, tn, tk, name):
    T, M = a.shape
    N = b.shape[1]

    def body(a_ref, b_ref, o_ref):
        @pl.when(pl.program_id(1) == 0)
        def _():
            o_ref[...] = jnp.zeros_like(o_ref)

        o_ref[...] += _dot_tn(a_ref[...], b_ref[...])

    return pl.pallas_call(
        body, name=name, grid=(N // tn, T // tk),
        out_shape=jax.ShapeDtypeStruct((M, N), F32),
        in_specs=[pl.BlockSpec((tk, M), lambda j, k: (k, 0)), pl.BlockSpec((tk, tn), lambda j, k: (k, j))],
        out_specs=pl.BlockSpec((M, tn), lambda j, k: (0, j)),
        compiler_params=_cparams(("arbitrary", "arbitrary")),
    )(a, b)


def _finalize(parts_all, dmod_all, dmod_cols, c_act_all):
    nparts = parts_all.shape[-1]
    nmod = dmod_all.shape[-1]

    def body(p_ref, dm_ref, dmc_ref, c_ref, ps_ref, loss_ref, db_ref, dw_ref):
        acc = p_ref[0]
        for j in range(1, N_DEV):
            acc = acc + p_ref[j]
        ps_ref[...] = acc
        loss_ref[...] = jnp.sum(acc[:, 0:LANE], axis=1, keepdims=True)
        db = dm_ref[0:1, :]
        for j in range(1, dm_ref.shape[0]):
            db = db + dm_ref[j:j + 1, :]
        db_ref[...] = db
        dw_ref[...] = _dot_tn(_mx(_silu(c_ref[...])), _mx(dmc_ref[...]))

    return pl.pallas_call(
        body, name="finalize",
        out_shape=[jax.ShapeDtypeStruct((1, nparts), F32), jax.ShapeDtypeStruct((1, 1), F32),
                   jax.ShapeDtypeStruct((1, nmod), F32),
                   jax.ShapeDtypeStruct((c_act_all.shape[1], dmod_cols.shape[1]), F32)],
        in_specs=[_vmem()] * 4, out_specs=[_vmem()] * 4,
        compiler_params=_cparams(),
    )(parts_all, dmod_all, dmod_cols, c_act_all)


def _adamw(ws, gs, ms, vs):
    n = len(ws)

    def body(*refs):
        w_refs, g_refs, m_refs, v_refs, d_refs, nm_refs, nv_refs = (refs[t * n:(t + 1) * n] for t in range(7))
        for t in range(n):
            gv = g_refs[t][...]
            nm = ADAM_B1 * m_refs[t][...] + (1.0 - ADAM_B1) * gv
            nv = ADAM_B2 * v_refs[t][...] + (1.0 - ADAM_B2) * (gv * gv)
            m_hat = nm / (1.0 - ADAM_B1 ** ADAM_STEP)
            v_hat = nv / (1.0 - ADAM_B2 ** ADAM_STEP)
            d_refs[t][...] = -ADAM_LR * (m_hat / (jnp.sqrt(v_hat) + ADAM_EPS) + ADAM_WD * w_refs[t][...])
            nm_refs[t][...] = nm
            nv_refs[t][...] = nv

    out = pl.pallas_call(
        body, name="adamw",
        out_shape=[jax.ShapeDtypeStruct(w.shape, F32) for w in ws] * 3,
        in_specs=[_vmem()] * (4 * n), out_specs=[_vmem()] * (3 * n),
        compiler_params=_cparams(),
    )(*ws, *gs, *ms, *vs)
    return out[:n], out[n:2 * n], out[2 * n:]


def _pair_perm(a, axis, order):
    a = jnp.moveaxis(a, axis, -1)
    lead = a.shape[:-1]
    a = a.reshape(lead + (8, 64))[..., list(order), :].reshape(lead + (512,))
    return jnp.moveaxis(a, -1, axis)


def _pad_w_in(w):
    z = lambda n: jnp.zeros((w.shape[0], n), w.dtype)
    return jnp.concatenate([
        w[:, 0:640], z(64), w[:, 640:672], z(32), w[:, 672:1184],
        _pair_perm(w[:, 1184:1696], 1, PAIR_ORDER), w[:, 1696:1952], _pair_perm(w[:, 1952:2464], 1, PAIR_ORDER)], axis=1)


def _unpad_w_in(g):
    return jnp.concatenate([
        g[:, 0:640], g[:, 704:736], g[:, 768:1280],
        _pair_perm(g[:, 1280:1792], 1, PAIR_INV), g[:, 1792:2048], _pair_perm(g[:, 2048:2560], 1, PAIR_INV)], axis=1)


def _rope_table(positions):
    T = positions.size
    inv = ROPE_THETA ** (-jnp.arange(0, MLA_ROPE, 2, dtype=F32) / MLA_ROPE)
    ang = positions.reshape(T).astype(F32)[:, None] * inv
    cos, sin = jnp.cos(ang), jnp.sin(ang)
    z = jnp.zeros((T, 32), F32)
    return jnp.concatenate([z, -sin, sin, cos, cos, z], axis=1)


def _local_step(x, mod, positions, ng, w_in, gq, gkv, w_uq, w_ukv, sinks, w_out, fg, tgt,
                ts=512, fq=512, fk=512, bq=512, bk=512):
    B, S, D = x.shape
    T = B * S
    x2 = x.reshape(T, D)
    shift, scale, gate = (mod[:, None, k * D:(k + 1) * D] for k in range(3))
    w_in_p = _pad_w_in(w_in)
    w_uq_p = jnp.pad(w_uq.reshape(Q_LORA, MLA_HEADS, 96), ((0, 0), (0, 0), (0, 32))).reshape(Q_LORA, MLA_HEADS * LANE)
    w_ukv3 = w_ukv.reshape(KV_LORA, MLA_HEADS, 128)
    w_uk_p = jnp.pad(w_ukv3[:, :, :64], ((0, 0), (0, 0), (0, 64))).reshape(KV_LORA, MLA_HEADS * LANE)
    w_uv = w_ukv3[:, :, 64:].reshape(KV_LORA, 512)
    w_out_p = jnp.concatenate([w_out[:512], _pair_perm(w_out[512:], 0, PAIR_ORDER)], axis=0)
    rope = _rope_table(positions)
    posf = positions.astype(F32)
    posb = jnp.broadcast_to(posf[:, :, None], (B, S, LANE))
    posr = posf.reshape(B, S // WINDOW, WINDOW)
    sinks_l = jnp.pad(sinks.reshape(1, SWA_HEADS), ((0, 0), (0, LANE - SWA_HEADS)))

    (hb, zq, zkv, ql, kvl, q, k, v, qs, ks, vs, g) = _pre_fwd(
        x2, shift, scale, ng, w_in_p, gq, gkv, w_uq_p, w_uk_p, w_uv, rope, S, ts)
    r3 = lambda a: a.reshape(B, S, a.shape[-1])
    om, lse_m = _mla_fwd(r3(q), r3(k), r3(v), fq, fk)
    osw, lse_s = _swa_fwd(r3(qs), r3(ks), r3(vs), posb, posr, sinks_l)
    dx2, ab, dyb, do, dg, loss_v, dfg, dgate = _post(
        om.reshape(T, 512), osw.reshape(T, 512), g, w_out_p, x2, gate, fg.reshape(1, D), tgt.reshape(T, D), S, ts)
    do3 = r3(do)
    dq, dk, dv = _mla_bwd(r3(q), r3(k), r3(v), om, do3, lse_m, bq, bk)
    dqs, dks, dvs, dsink = _swa_bwd(r3(qs), r3(ks), r3(vs), posb, posr, sinks_l, osw, do3, lse_s)
    f2 = lambda a: a.reshape(T, a.shape[-1])
    gx, dqb, dkb, dvb, dz, dgq, dgkv, dng, dsh, dsc = _pre_bwd(
        f2(dq), f2(dk), f2(dv), f2(dqs), f2(dks), f2(dvs), dg, zq, zkv, x2, dx2, scale, ng, w_in_p, gq, gkv,
        w_uq_p, w_uk_p, w_uv, rope, S, ts // 2)
    tk = min(T, 1024)
    dw_in = _unpad_w_in(_tn_matmul(hb, dz, 512, tk, "dw_in"))
    dw_out_p = _tn_matmul(ab, dyb, 512, tk, "dw_out")
    dw_out = jnp.concatenate([dw_out_p[:512], _pair_perm(dw_out_p[512:], 0, PAIR_INV)], axis=0)
    dw_uq = _tn_matmul(ql, dqb, 512, tk, "dw_uq").reshape(Q_LORA, MLA_HEADS, LANE)[:, :, :96].reshape(Q_LORA, 768)
    dw_uk = _tn_matmul(kvl, dkb, 512, tk, "dw_uk").reshape(KV_LORA, MLA_HEADS, LANE)[:, :, :64]
    dw_uv = _tn_matmul(kvl, dvb, 512, tk, "dw_uv").reshape(KV_LORA, MLA_HEADS, 64)
    dw_ukv = jnp.concatenate([dw_uk, dw_uv], axis=2).reshape(KV_LORA, 1024)
    parts = jnp.concatenate([loss_v, dfg, dng, dgq, dgkv, dsink], axis=1)
    dmod = jnp.concatenate([dsh, dsc, dgate], axis=2).reshape(B, 3 * D)
    return gx.reshape(B, S, D), dw_in, dw_uq, dw_ukv, dw_out, parts, dmod


def kernel(x, c, positions, w_ada, b_ada, norm_gain, w_in, q_norm_gain, kv_norm_gain, w_uq, w_ukv, swa_sinks, w_out, final_gain, loss_target, m_w_ada, m_b_ada, m_norm_gain, m_w_in, m_q_norm_gain, m_kv_norm_gain, m_w_uq, m_w_ukv, m_swa_sinks, m_w_out, m_final_gain, v_w_ada, v_b_ada, v_norm_gain, v_w_in, v_q_norm_gain, v_kv_norm_gain, v_w_uq, v_w_ukv, v_swa_sinks, v_w_out, v_final_gain):
    B, S, D = x.shape
    me = 4 * lax.axis_index("x") + 2 * lax.axis_index("y") + lax.axis_index("c")
    bf = _MXU_DTYPE

    c_all, win_g, wuq_g, wukv_g, wout_g = _all_gather(
        [c, w_in[0].astype(bf), w_uq[0].astype(bf), w_ukv[0].astype(bf), w_out[0].astype(bf)], "ag_weights")
    c_all = c_all.reshape(N_DEV * B, D)
    cat_cols = lambda a: jnp.transpose(a, (1, 0, 2)).reshape(a.shape[1], N_DEV * a.shape[2])
    w_in_f, w_uq_f, w_ukv_f = cat_cols(win_g), cat_cols(wuq_g), cat_cols(wukv_g)
    w_out_f = wout_g.reshape(D, D)

    ncol = w_ada.shape[2]
    b_cols = lax.dynamic_slice_in_dim(b_ada, me * ncol, ncol, axis=1)
    (mod_g,) = _all_gather([_ada_fwd(c_all, w_ada[0], b_cols)], "ag_mod")
    mod = lax.dynamic_slice_in_dim(mod_g, me * B, B, axis=1)
    mod = jnp.transpose(mod, (1, 0, 2)).reshape(B, 3 * D)

    gx, dw_in, dw_uq, dw_ukv, dw_out, parts, dmod = _local_step(
        x, mod, positions, norm_gain, w_in_f, q_norm_gain, kv_norm_gain, w_uq_f, w_ukv_f, swa_sinks,
        w_out_f, final_gain, loss_target)

    parts_g, dmod_g = _all_gather([parts, dmod], "ag_small")
    dmod_all = dmod_g.reshape(N_DEV * B, 3 * D)
    dmod_cols = lax.dynamic_slice_in_dim(dmod_all, me * ncol, ncol, axis=1)
    psum, loss, g_b_ada, g_w_ada = _finalize(parts_g, dmod_all, dmod_cols, c_all)
    loss = loss.reshape(())
    o = LANE
    g_final_gain = psum[0, o:o + D]
    g_norm_gain = psum[:, o + D:o + 2 * D]
    o += 2 * D
    g_q_norm_gain = psum[:, o:o + Q_LORA]
    g_kv_norm_gain = psum[:, o + Q_LORA:o + Q_LORA + KV_LORA]
    o += Q_LORA + KV_LORA
    g_sinks = psum[:, o:o + SWA_HEADS]

    split_cols = lambda a: jnp.transpose(a.reshape(a.shape[0], 4, 2, a.shape[1] // N_DEV), (2, 1, 0, 3))
    split_rows = lambda a: jnp.transpose(a.reshape(4, 2, a.shape[0] // N_DEV, a.shape[1]), (1, 0, 2, 3))
    g_w_in, g_w_uq, g_w_ukv, g_w_out = _reduce_scatter(
        [split_cols(dw_in), split_cols(dw_uq), split_cols(dw_ukv), split_rows(dw_out)], "rs_grads")

    grads = [g_w_ada, g_b_ada, g_norm_gain, g_w_in, g_q_norm_gain, g_kv_norm_gain, g_w_uq, g_w_ukv, g_sinks, g_w_out,
             g_final_gain]
    ws = [w_ada, b_ada, norm_gain, w_in, q_norm_gain, kv_norm_gain, w_uq, w_ukv, swa_sinks, w_out, final_gain]
    ms = [m_w_ada, m_b_ada, m_norm_gain, m_w_in, m_q_norm_gain, m_kv_norm_gain, m_w_uq, m_w_ukv, m_swa_sinks, m_w_out,
          m_final_gain]
    vs = [v_w_ada, v_b_ada, v_norm_gain, v_w_in, v_q_norm_gain, v_kv_norm_gain, v_w_uq, v_w_ukv, v_swa_sinks, v_w_out,
          v_final_gain]
    two_d = [(1, w.shape[0]) if w.ndim == 1 else w.shape[-2:] for w in ws]
    flat = lambda arrs: [a.reshape(s) for a, s in zip(arrs, two_d)]
    deltas, new_ms, new_vs = _adamw(flat(ws), flat(grads), flat(ms), flat(vs))
    shaped = lambda arrs: [a.reshape(w.shape) for a, w in zip(arrs, ws)]
    return (loss, gx, *shaped(grads), *shaped(deltas), *shaped(new_ms), *shaped(new_vs))
```

```python
import functools

import numpy as np
import jax
import jax.numpy as jnp
from jax import lax
from jax.experimental import pallas as pl
from jax.experimental.pallas import tpu as pltpu

F32 = jnp.float32
_MXU_DTYPE = jnp.bfloat16

N_DEV = 8
D_MODEL = 1024
MLA_HEADS = 8
MLA_NOPE = 64
MLA_ROPE = 32
MLA_V = 64
Q_LORA = 384
KV_LORA = 256
SWA_HEADS = 8
SWA_KV_HEADS = 2
SWA_HEAD_DIM = 64
WINDOW = 128
ROPE_THETA = 10000.0
EPS = 1e-6
MLA_SCALE = float((MLA_NOPE + MLA_ROPE) ** -0.5)
SWA_SCALE = float(SWA_HEAD_DIM ** -0.5)
LOG2E = 1.4426950408889634
MLA_QSCALE = MLA_SCALE * LOG2E
D_IN = 2464
D_IN_PAD = 2560
PAIR_ORDER = (0, 4, 1, 5, 2, 6, 3, 7)
PAIR_INV = (0, 2, 4, 6, 1, 3, 5, 7)

ADAM_LR = 0.001
ADAM_B1 = 0.9
ADAM_B2 = 0.999
ADAM_EPS = 1e-08
ADAM_WD = 0.01
ADAM_STEP = 10

LANE = 128
VMEM_LIMIT = 56 * 1024 * 1024

MESH = pl.DeviceIdType.MESH
NEG_INF = float("-inf")
SWA_SEQ_SPLIT = 2


def _mx(a):
    return a.astype(_MXU_DTYPE)


def _dot(a, b):
    return jnp.dot(a, b, preferred_element_type=F32)


def _dot_nt(a, b):
    return lax.dot_general(a, b, (((1,), (1,)), ((), ())), preferred_element_type=F32)


def _dot_tn(a, b):
    return lax.dot_general(a, b, (((0,), (0,)), ((), ())), preferred_element_type=F32)


def _cparams(sem=None):
    return pltpu.CompilerParams(dimension_semantics=sem, vmem_limit_bytes=VMEM_LIMIT)


def _vmem():
    return pl.BlockSpec(memory_space=pltpu.VMEM)


def _lane_iota(shape):
    return lax.broadcasted_iota(jnp.int32, shape, len(shape) - 1)


def _all_gather(arrs, name):
    n = len(arrs)

    def body(*refs):
        ins, outs = refs[:n], refs[n:2 * n]
        send_sems, recv_sems, local_sems = refs[2 * n:]
        x, y, c = lax.axis_index("x"), lax.axis_index("y"), lax.axis_index("c")
        me, sibling = (x, y, c), (x, y, 1 - c)
        chips = [(1 - x, y), (x, 1 - y), (1 - x, 1 - y)]

        def slot(a, dev):
            return outs[a].at[4 * dev[0] + 2 * dev[1] + dev[2]]

        def copy(a, k, block, to, src=None):
            return pltpu.make_async_remote_copy(
                src_ref=slot(a, block) if src is None else src, dst_ref=slot(a, block),
                send_sem=send_sems.at[7 * a + k], recv_sem=recv_sems.at[7 * a + k],
                device_id=to, device_id_type=MESH)

        mine = [pltpu.make_async_copy(ins[a], slot(a, me), local_sems.at[a]) for a in range(n)]
        for cp in mine:
            cp.start()
        first = []
        for a in range(n):
            first.append(copy(a, 0, me, sibling, src=ins[a]))
            first += [copy(a, 1 + j, me, (*chip, c), src=ins[a]) for j, chip in enumerate(chips)]
        for cp in first:
            cp.start()
        passed = []
        for j, chip in enumerate(chips):
            for a in range(n):
                copy(a, 1 + j, (*chip, c), me).wait_recv()
                cp = copy(a, 4 + j, (*chip, c), sibling)
                cp.start()
                passed.append(cp)
        for a in range(n):
            copy(a, 0, sibling, me).wait_recv()
            for j, chip in enumerate(chips):
                copy(a, 4 + j, (*chip, 1 - c), me).wait_recv()
        for cp in first + passed:
            cp.wait_send()
        for cp in mine:
            cp.wait()

    return pl.pallas_call(
        body, name=name,
        out_shape=[jax.ShapeDtypeStruct((N_DEV,) + a.shape, a.dtype) for a in arrs],
        in_specs=[_vmem()] * n, out_specs=[_vmem()] * n,
        scratch_shapes=[pltpu.SemaphoreType.DMA((7 * n,)), pltpu.SemaphoreType.DMA((7 * n,)),
                        pltpu.SemaphoreType.DMA((n,))],
        compiler_params=pltpu.CompilerParams(vmem_limit_bytes=VMEM_LIMIT),
    )(*arrs)


def _reduce_scatter(arrs, name):
    n = len(arrs)
    halves = [a.astype(jnp.bfloat16) for a in arrs]

    def body(*refs):
        xs, xbs, outs = refs[:n], refs[n:2 * n], refs[2 * n:3 * n]
        parts, recv_a, send_b, recv_b = (refs[(3 + t) * n:(4 + t) * n] for t in range(4))
        send_sems, recv_sems, local_sems = refs[7 * n:]
        x, y, c = lax.axis_index("x"), lax.axis_index("y"), lax.axis_index("c")
        myq = 2 * x + y

        def chip(k):
            return (1 - x if k & 2 else x, 1 - y if k & 1 else y)

        def from_sibling(a):
            return pltpu.make_async_remote_copy(
                src_ref=xbs[a].at[1 - c], dst_ref=recv_a[a], send_sem=send_sems.at[4 * a], recv_sem=recv_sems.at[4 * a],
                device_id=(x, y, 1 - c), device_id_type=MESH)

        def to_owner(a, k):
            qx, qy = chip(k)
            return pltpu.make_async_remote_copy(
                src_ref=send_b[a].at[2 * qx + qy], dst_ref=recv_b[a].at[myq],
                send_sem=send_sems.at[4 * a + k], recv_sem=recv_sems.at[4 * a + k],
                device_id=(qx, qy, c), device_id_type=MESH)

        mine = [pltpu.make_async_copy(xs[a].at[c], parts[a], local_sems.at[a]) for a in range(n)]
        first = [from_sibling(a) for a in range(n)]
        for cp in mine + first:
            cp.start()
        second = []
        for a in range(n):
            mine[a].wait()
            first[a].wait_recv()
            parts[a][...] = parts[a][...] + recv_a[a][...].astype(F32)
            send_b[a][...] = parts[a][...].astype(jnp.bfloat16)
            for k in range(1, 4):
                cp = to_owner(a, k)
                cp.start()
                second.append(cp)
        for a in range(n):
            acc = parts[a][myq]
            for k in range(1, 4):
                to_owner(a, k).wait_recv()
                qx, qy = chip(k)
                acc = acc + recv_b[a][2 * qx + qy].astype(F32)
            outs[a][...] = acc
        for cp in first + second:
            cp.wait_send()

    quarter = lambda a, dt: pltpu.VMEM(a.shape[1:], dt)
    return pl.pallas_call(
        body, name=name,
        out_shape=[jax.ShapeDtypeStruct(a.shape[2:], F32) for a in arrs],
        in_specs=[pl.BlockSpec(memory_space=pl.ANY)] * (2 * n), out_specs=[_vmem()] * n,
        scratch_shapes=[quarter(a, F32) for a in arrs] + [quarter(a, jnp.bfloat16) for a in arrs] * 3
        + [pltpu.SemaphoreType.DMA((4 * n,)), pltpu.SemaphoreType.DMA((4 * n,)), pltpu.SemaphoreType.DMA((n,))],
        compiler_params=pltpu.CompilerParams(vmem_limit_bytes=VMEM_LIMIT),
    )(*arrs, *halves)


def _silu(t):
    return t * (1.0 / (1.0 + jnp.exp(-t)))


def _ada_fwd(c_act_all, w_ada_shard, b_shard):
    def body(c_ref, w_ref, b_ref, o_ref):
        o_ref[...] = _dot(_mx(_silu(c_ref[...])), _mx(w_ref[...])) + b_ref[...]

    return pl.pallas_call(
        body, name="ada_fwd",
        out_shape=jax.ShapeDtypeStruct((c_act_all.shape[0], w_ada_shard.shape[1]), F32),
        in_specs=[_vmem()] * 3, out_specs=_vmem(),
        compiler_params=_cparams(),
    )(c_act_all, w_ada_shard, b_shard)


def _rope_split(cd, sd):
    n = cd.shape[0]

    def expand(d):
        rep = jnp.broadcast_to(d[:, None, :], (n, 8, LANE)).reshape(8 * n, LANE)
        return pltpu.roll(rep, 0, 1, stride=16, stride_axis=0)

    c, s = expand(cd), expand(sd)
    lane = _lane_iota(c.shape)
    first = jnp.logical_and(lane >= 64, lane < 80)
    second = jnp.logical_and(lane >= 80, lane < 96)
    ck = jnp.where(first, pltpu.roll(c, 80, 1), jnp.where(second, pltpu.roll(c, 96, 1), 0.0))
    cq = jnp.where(lane < 64, 1.0, ck)
    sa = jnp.where(first, -pltpu.roll(s, 80, 1), 0.0)
    sb = jnp.where(second, pltpu.roll(s, 96, 1), 0.0)
    return cq, ck, sa, sb


def _tile_heads(t):
    return jnp.concatenate([t] * MLA_HEADS, axis=1)


def _pre_fwd(x2, shift, scale, ng, w_in, gq, gkv, w_uq, w_uk, w_uv, rope, S, ts):
    T, D = x2.shape
    nsb = S // ts
    WQ = MLA_HEADS * LANE

    def body(x_ref, sh_ref, sc_ref, ng_ref, w_ref, gq_ref, gkv_ref, wuq_ref, wuk_ref, wuv_ref,
             cd_ref, sd_ref,
             hb_ref, zq_ref, zkv_ref, ql_ref, kvl_ref, q_ref, k_ref, v_ref, qs_ref, ks_ref, vs_ref, g_ref):
        x = x_ref[...]
        r1 = lax.rsqrt(jnp.mean(x * x, axis=-1, keepdims=True) + EPS)
        h = ((x * r1) * ng_ref[...]) * (1.0 + sc_ref[0]) + sh_ref[0]
        hb = _mx(h)
        hb_ref[...] = hb
        zq = _dot(hb, w_ref[:, 0:384])
        zq_ref[...] = zq
        rq = lax.rsqrt(jnp.mean(zq * zq, axis=-1, keepdims=True) + EPS)
        ql = _mx((zq * rq) * gq_ref[...])
        ql_ref[...] = ql
        q = _dot(ql, wuq_ref[...])
        cq, ck, sa, sb = _rope_split(cd_ref[...], sd_ref[...])
        q = (q * _tile_heads(cq) + pltpu.roll(q, WQ - 16, 1) * _tile_heads(sa)
             + pltpu.roll(q, 16, 1) * _tile_heads(sb))
        q_ref[...] = _mx(q * MLA_QSCALE)
        zkv = _dot(hb, w_ref[:, 384:640])
        zkv_ref[...] = zkv
        rkv = lax.rsqrt(jnp.mean(zkv * zkv, axis=-1, keepdims=True) + EPS)
        kvl = _mx((zkv * rkv) * gkv_ref[...])
        kvl_ref[...] = kvl
        kr = _dot(hb, w_ref[:, 640:768])
        kpe = kr * ck + pltpu.roll(kr, LANE - 16, 1) * sa + pltpu.roll(kr, 16, 1) * sb
        kf = _dot(kvl, wuk_ref[...])
        k_ref[...] = _mx(kf + jnp.concatenate([kpe] * MLA_HEADS, axis=1))
        v_ref[...] = _mx(_dot(kvl, wuv_ref[...]))
        g_ref[:, 0:512] = _dot(hb, w_ref[:, 768:1280])
        qs_ref[...] = _mx(_dot(hb, w_ref[:, 1280:1792]) * SWA_SCALE)
        ks_ref[...] = _mx(_dot(hb, w_ref[:, 1792:1920]))
        vs_ref[...] = _mx(_dot(hb, w_ref[:, 1920:2048]))
        g_ref[:, 512:1024] = _dot(hb, w_ref[:, 2048:2560])

    row = lambda w: pl.BlockSpec((ts, w), lambda i: (i, 0))
    dense = pl.BlockSpec((ts // 8, LANE), lambda i: (i, 0))
    full = lambda a: pl.BlockSpec(a.shape, lambda i: (0,) * a.ndim)
    per_b = pl.BlockSpec((1, 1, D), lambda i: (i // nsb, 0, 0))
    out_w = [(D, _MXU_DTYPE), (384, F32), (256, F32), (384, _MXU_DTYPE), (256, _MXU_DTYPE), (WQ, _MXU_DTYPE),
             (WQ, _MXU_DTYPE), (512, _MXU_DTYPE), (512, _MXU_DTYPE), (128, _MXU_DTYPE), (128, _MXU_DTYPE), (1024, F32)]
    return pl.pallas_call(
        body, name="pre_fwd", grid=(T // ts,),
        out_shape=[jax.ShapeDtypeStruct((T, w), dt) for w, dt in out_w],
        in_specs=[row(D), per_b, per_b, full(ng), full(w_in), full(gq), full(gkv), full(w_uq), full(w_uk), full(w_uv),
                  dense, dense],
        out_specs=[row(w) for w, _ in out_w],
        compiler_params=_cparams(("arbitrary",)),
    )(x2, shift, scale, ng, w_in, gq, gkv, w_uq, w_uk, w_uv, *rope)


def _mla_fwd(q3, k3, v3, tq, tk):
    B, S, _ = q3.shape
    nq = S // tq
    assert tk % tq == 0 or tq % tk == 0
    n_masked = max(1, tq // tk)

    def body(q_ref, k_ref, v_ref, o_ref, lse_ref):
        qi = pl.program_id(2)
        rows = lax.broadcasted_iota(jnp.int32, (tq, tk), 0)
        cols = lax.broadcasted_iota(jnp.int32, (tq, tk), 1)
        qs = [q_ref[0, :, LANE * j:LANE * (j + 1)] for j in range(2)]
        n_full = (qi * tq) // tk

        def step(kt, carry, masked):
            r0 = pl.multiple_of(kt * tk, tk)
            v2 = v_ref[0, pl.ds(r0, tk), :]
            out = []
            for j in range(2):
                m, l, acc = carry[j]
                s = _dot_nt(qs[j], k_ref[0, pl.ds(r0, tk), LANE * j:LANE * (j + 1)])
                if masked:
                    s = jnp.where(rows + qi * tq >= cols + kt * tk, s, NEG_INF)
                m_new = jnp.maximum(m, jnp.max(s, axis=1, keepdims=True))
                alpha = jnp.exp2(m - m_new)
                p = jnp.exp2(s - m_new)
                l = alpha * l + jnp.sum(p, axis=1, keepdims=True)
                acc = alpha * acc + _dot(_mx(p), v2)
                out.append((m_new, l, acc))
            return tuple(out)

        init = (jnp.full((tq, 1), NEG_INF, F32), jnp.zeros((tq, 1), F32), jnp.zeros((tq, LANE), F32))
        carry = lax.fori_loop(0, n_full, functools.partial(step, masked=False), (init, init))
        for t in range(n_masked):
            carry = step(n_full + t, carry, True)
        (m0, l0, a0), (m1, l1, a1) = carry
        o_ref[0] = jnp.where(_lane_iota((tq, LANE)) < 64, a0 / l0, a1 / l1)
        lse_ref[0, 0] = jnp.broadcast_to(m0 + jnp.log2(l0), (tq, LANE))
        lse_ref[0, 1] = jnp.broadcast_to(m1 + jnp.log2(l1), (tq, LANE))

    return pl.pallas_call(
        body, name="mla_fwd", grid=(B, MLA_HEADS // 2, nq),
        out_shape=[jax.ShapeDtypeStruct((B, S, 512), F32), jax.ShapeDtypeStruct((B, MLA_HEADS, S, LANE), F32)],
        in_specs=[pl.BlockSpec((1, tq, 2 * LANE), lambda b, hp, i: (b, i, hp)),
                  pl.BlockSpec((1, S, 2 * LANE), lambda b, hp, i: (b, 0, hp)),
                  pl.BlockSpec((1, S, LANE), lambda b, hp, i: (b, 0, hp))],
        out_specs=[pl.BlockSpec((1, tq, LANE), lambda b, hp, i: (b, i, hp)),
                   pl.BlockSpec((1, 2, tq, LANE), lambda b, hp, i: (b, hp, i, 0))],
        compiler_params=_cparams(("arbitrary", "arbitrary", "arbitrary")),
    )(q3, k3, v3)


def _mla_bwd(q3, k3, v3, o3, do3, lse, tq, tk):
    B, S, _ = q3.shape
    nq, nk = S // tq, S // tk
    assert tk % tq == 0

    def body(q_ref, k_ref, v_ref, o_ref, do_ref, lse_ref, dq_ref, dk_ref, dv_ref, dkt_ref, dvt_ref):
        dkt_ref[...] = jnp.zeros_like(dkt_ref)
        dvt_ref[...] = jnp.zeros_like(dvt_ref)
        rows = lax.broadcasted_iota(jnp.int32, (tq, tk), 0)
        cols = lax.broadcasted_iota(jnp.int32, (tq, tk), 1)
        lane = _lane_iota((tq, LANE))

        def q_tile(qi, _):
            r = pl.ds(pl.multiple_of(qi * tq, tq), tq)
            do2, o2 = do_ref[0, r, :], o_ref[0, r, :]
            heads = []
            for j in range(2):
                lanes = slice(LANE * j, LANE * (j + 1))
                do = jnp.where((lane < 64) if j == 0 else (lane >= 64), do2, 0.0)
                q = q_ref[0, r, lanes]
                heads.append((lanes, q, _mx(q.astype(F32).T), _mx(do), _mx(do.T),
                              jnp.sum(do * o2, axis=1, keepdims=True),
                              jnp.concatenate([lse_ref[0, j, r, :]] * (tk // LANE), axis=1)))
            n_full = (qi * tq) // tk

            def k_tile(kt, dqs, masked):
                kr = pl.ds(pl.multiple_of(kt * tk, tk), tk)
                v2 = v_ref[0, kr, :]
                out = []
                dvt = None
                for j, (lanes, q, qt, dob, dot_, dcol, lse_c) in enumerate(heads):
                    k = k_ref[0, kr, lanes]
                    s = _dot_nt(q, k)
                    if masked:
                        s = jnp.where(rows + qi * tq >= cols + kt * tk, s, NEG_INF)
                    p = jnp.exp2(s - lse_c)
                    dp = _dot_nt(dob, v2)
                    dsb = _mx(p * (dp - dcol))
                    dkt_ref[j, kt] += _dot(qt, dsb)
                    pv = _dot(dot_, _mx(p))
                    dvt = pv if dvt is None else dvt + pv
                    out.append(dqs[j] + _dot(dsb, k))
                dvt_ref[kt] += dvt
                return tuple(out)

            zero = jnp.zeros((tq, LANE), F32)
            dqs = lax.fori_loop(0, n_full, functools.partial(k_tile, masked=False), (zero, zero))
            dqs = k_tile(n_full, dqs, True)
            for j in range(2):
                dq_ref[0, r, heads[j][0]] = MLA_SCALE * dqs[j]
            return 0

        lax.fori_loop(0, nq, q_tile, 0)

        def flush(kt, _):
            kr = pl.ds(pl.multiple_of(kt * tk, tk), tk)
            for j in range(2):
                dk_ref[0, kr, LANE * j:LANE * (j + 1)] = (1.0 / LOG2E) * dkt_ref[j, kt].T
            dv_ref[0, kr, :] = dvt_ref[kt].T
            return 0

        lax.fori_loop(0, nk, flush, 0)

    pair = lambda w: pl.BlockSpec((1, S, w), lambda b, hp: (b, 0, hp))
    return pl.pallas_call(
        body, name="mla_bwd", grid=(B, MLA_HEADS // 2),
        out_shape=[jax.ShapeDtypeStruct((B, S, 1024), F32), jax.ShapeDtypeStruct((B, S, 1024), F32),
                   jax.ShapeDtypeStruct((B, S, 512), F32)],
        in_specs=[pair(2 * LANE), pair(2 * LANE), pair(LANE), pair(LANE), pair(LANE),
                  pl.BlockSpec((1, 2, S, LANE), lambda b, hp: (b, hp, 0, 0))],
        out_specs=[pair(2 * LANE), pair(2 * LANE), pair(LANE)],
        scratch_shapes=[pltpu.VMEM((2, nk, LANE, tk), F32), pltpu.VMEM((nk, LANE, tk), F32)],
        compiler_params=_cparams(("arbitrary", "arbitrary")),
    )(q3, k3, v3, o3, do3, lse)


def _swa_consts(sink_ref):
    W = WINDOW
    row = lax.broadcasted_iota(jnp.int32, (4 * W, LANE), 0)
    out = []
    for g in range(SWA_KV_HEADS):
        slope = jnp.zeros((4 * W, LANE), F32)
        sink = jnp.zeros((4 * W, LANE), F32)
        for p in range(4):
            h = p + 4 * g
            here = jnp.logical_and(row >= W * p, row < W * (p + 1))
            slope = jnp.where(here, float(2.0 ** (-8.0 * (h + 1) / SWA_HEADS)), slope)
            sink = jnp.where(here, sink_ref[:, h:h + 1], sink)
        out.append((jnp.concatenate([slope, slope], axis=1), sink))
    return out


def _wide(col):
    return jnp.concatenate([col, col], axis=1)


def _swa_block(q_ref, k_ref, v_ref, pb_ref, pr_ref, n, i, ij):
    W = WINDOW
    kb = jnp.maximum(n - 1, 0)
    r = pl.ds(pl.multiple_of(i * W, W), W)
    kr = pl.ds(pl.multiple_of(kb * W, W), 2 * W)
    q4, k2, v2 = q_ref[0, r, :], k_ref[0, kr, :], v_ref[0, kr, :]
    pq = _wide(pb_ref[0, r, :])
    pk = jnp.concatenate([pr_ref[0, pl.ds(kb, 1), :], pr_ref[0, pl.ds(kb + 1, 1), :]], axis=1)
    rel = ij + (n - kb) * W
    mask = jnp.where(jnp.logical_and(rel >= 0, rel < W), 0.0, NEG_INF)
    dist4 = jnp.concatenate([pq - pk] * 4, axis=0)
    mask4 = jnp.concatenate([mask] * 4, axis=0)
    return r, kb, kr, q4, k2, v2, dist4, mask4


def _swa_stack(x4, g, dtype):
    lane = _lane_iota((WINDOW, LANE))
    mine = (lane < 64) if g == 0 else (lane >= 64)
    return jnp.concatenate([jnp.where(mine, x4[:, LANE * p:LANE * (p + 1)], 0).astype(dtype) for p in range(4)], axis=0)


def _swa_unstack(ref, r, lo, hi, scale=None):
    W = WINDOW
    low = _lane_iota((W, LANE)) < 64
    for p in range(4):
        t = jnp.where(low, lo[W * p:W * (p + 1)], hi[W * p:W * (p + 1)])
        ref[0, r, LANE * p:LANE * (p + 1)] = t if scale is None else scale * t


def _swa_fwd(qs3, ks3, vs3, posb, posr, sinks):
    B, S, _ = qs3.shape
    W = WINDOW
    nb = S // W
    nh = SWA_SEQ_SPLIT
    nbh = nb // nh

    def body(q_ref, k_ref, v_ref, pb_ref, pr_ref, sink_ref, o_ref, lse_ref):
        ij = lax.broadcasted_iota(jnp.int32, (W, 2 * W), 0) - lax.broadcasted_iota(jnp.int32, (W, 2 * W), 1)
        consts = _swa_consts(sink_ref)
        n0 = pl.program_id(1) * nbh

        def blk(i, _):
            n = n0 + i
            r, _, _, q4, k2, v2, dist4, mask4 = _swa_block(q_ref, k_ref, v_ref, pb_ref, pr_ref, n, i, ij)
            o_g = []
            for g, (slope, sink) in enumerate(consts):
                s = _dot_nt(_swa_stack(q4, g, q4.dtype), k2) - slope * dist4 + mask4
                m = jnp.maximum(jnp.max(s, axis=1, keepdims=True), sink)
                e = jnp.exp(s - _wide(m))
                l = jnp.sum(e, axis=1, keepdims=True) + jnp.exp(sink - m)
                o_g.append(_dot(_mx(e * _wide(1.0 / l)), v2))
                lse_ref[0, i, g] = m + jnp.log(l)
            _swa_unstack(o_ref, r, o_g[0], o_g[1])
            return 0

        lax.fori_loop(0, nbh, blk, 0)

    seq = lambda w: pl.BlockSpec((1, S, w), lambda b, h: (b, 0, 0))
    part = lambda w: pl.BlockSpec((1, S // nh, w), lambda b, h: (b, h, 0))
    lse_spec = pl.BlockSpec((1, nbh, 2, 4 * W, LANE), lambda b, h: (b, h, 0, 0, 0))
    return pl.pallas_call(
        body, name="swa_fwd", grid=(B, nh),
        out_shape=[jax.ShapeDtypeStruct((B, S, 512), F32), jax.ShapeDtypeStruct((B, nb, 2, 4 * W, LANE), F32)],
        in_specs=[part(512), seq(LANE), seq(LANE), part(LANE), pl.BlockSpec((1, nb, W), lambda b, h: (b, 0, 0)),
                  pl.BlockSpec((1, LANE), lambda b, h: (0, 0))],
        out_specs=[part(512), lse_spec],
        compiler_params=_cparams(("arbitrary", "arbitrary")),
    )(qs3, ks3, vs3, posb, posr, sinks)


def _swa_bwd(qs3, ks3, vs3, posb, posr, sinks, os3, do3, lse):
    B, S, _ = qs3.shape
    W = WINDOW
    nb = S // W
    nh = SWA_SEQ_SPLIT
    nbh = nb // nh

    def body(q_ref, k_ref, v_ref, pb_ref, pr_ref, sink_ref, o_ref, do_ref, lse_ref, dq_ref, dk_ref, dv_ref, dsink_ref,
             dkt_ref, dvt_ref):
        lane1 = _lane_iota((1, LANE))
        ij = lax.broadcasted_iota(jnp.int32, (W, 2 * W), 0) - lax.broadcasted_iota(jnp.int32, (W, 2 * W), 1)
        consts = _swa_consts(sink_ref)
        hh = pl.program_id(1)
        n0 = hh * nbh

        @pl.when(hh == 0)
        def _():
            dkt_ref[...] = jnp.zeros_like(dkt_ref)
            dvt_ref[...] = jnp.zeros_like(dvt_ref)

        @pl.when(jnp.logical_and(pl.program_id(0) == 0, hh == 0))
        def _():
            dsink_ref[...] = jnp.zeros_like(dsink_ref)

        def blk(i, dsink):
            n = n0 + i
            r, kb, _, q4, k2, v2, dist4, mask4 = _swa_block(q_ref, k_ref, v_ref, pb_ref, pr_ref, n, i, ij)
            o4, do4 = o_ref[0, r, :], do_ref[0, r, :]
            dq_g = []
            dkt = jnp.zeros((LANE, 2 * W), F32)
            dvt = jnp.zeros((LANE, 2 * W), F32)
            for g, (slope, sink) in enumerate(consts):
                q_st = _swa_stack(q4, g, F32)
                do_st = _swa_stack(do4, g, F32)
                dcol = jnp.sum(do_st * _swa_stack(o4, g, F32), axis=1, keepdims=True)
                qst, dob = _mx(q_st), _mx(do_st)
                lse_c = lse_ref[0, i, g]
                pr = jnp.exp(_dot_nt(qst, k2) - slope * dist4 + mask4 - _wide(lse_c))
                dsb = _mx(pr * (_dot_nt(dob, v2) - dcol))
                psd = jnp.exp(sink - lse_c)[:, 0:1] * dcol
                for p in range(4):
                    dsink = dsink - jnp.where(lane1 == p + 4 * g,
                                              jnp.sum(psd[W * p:W * (p + 1)], axis=0, keepdims=True), 0.0)
                dq_g.append(_dot(dsb, k2))
                dkt = dkt + _dot(_mx(q_st.T), dsb)
                dvt = dvt + _dot(_mx(do_st.T), _mx(pr))
            _swa_unstack(dq_ref, r, dq_g[0], dq_g[1], SWA_SCALE)
            dkt_ref[kb] += dkt[:, 0:W]
            dkt_ref[kb + 1] += dkt[:, W:2 * W]
            dvt_ref[kb] += dvt[:, 0:W]
            dvt_ref[kb + 1] += dvt[:, W:2 * W]
            return dsink

        dsink_ref[...] += lax.fori_loop(0, nbh, blk, jnp.zeros((1, LANE), F32))

        @pl.when(hh == nh - 1)
        def _():
            def flush(n, _):
                r = pl.ds(pl.multiple_of(n * W, W), W)
                dk_ref[0, r, :] = dkt_ref[n].T
                dv_ref[0, r, :] = dvt_ref[n].T
                return 0

            lax.fori_loop(0, nb, flush, 0)

    seq = lambda w: pl.BlockSpec((1, S, w), lambda b, h: (b, 0, 0))
    part = lambda w: pl.BlockSpec((1, S // nh, w), lambda b, h: (b, h, 0))
    return pl.pallas_call(
        body, name="swa_bwd", grid=(B, nh),
        out_shape=[jax.ShapeDtypeStruct((B, S, 512), F32), jax.ShapeDtypeStruct((B, S, LANE), F32),
                   jax.ShapeDtypeStruct((B, S, LANE), F32), jax.ShapeDtypeStruct((1, LANE), F32)],
        in_specs=[part(512), seq(LANE), seq(LANE), part(LANE), pl.BlockSpec((1, nb, W), lambda b, h: (b, 0, 0)),
                  pl.BlockSpec((1, LANE), lambda b, h: (0, 0)), part(512),
                  pl.BlockSpec((1, S // nh, 512), lambda b, h: (b, h, 1)),
                  pl.BlockSpec((1, nbh, 2, 4 * W, LANE), lambda b, h: (b, h, 0, 0, 0))],
        out_specs=[part(512), seq(LANE), seq(LANE), pl.BlockSpec((1, LANE), lambda b, h: (0, 0))],
        scratch_shapes=[pltpu.VMEM((nb, LANE, W), F32), pltpu.VMEM((nb, LANE, W), F32)],
        compiler_params=_cparams(("arbitrary", "arbitrary")),
    )(qs3, ks3, vs3, posb, posr, sinks, os3, do3, lse)


def _post(om, osw, g, w_out, x2, gate, fg, tgt, S, ts):
    T, D = x2.shape
    nsb = S // ts

    def body(om_ref, os_ref, g_ref, w_ref, x_ref, gate_ref, fg_ref, t_ref,
             dx2_ref, do_ref, dg_ref, loss_ref, dfg_ref, dgate_ref, dw_ref):
        i = pl.program_id(0)
        gv = g_ref[...]
        sg = 1.0 / (1.0 + jnp.exp(-gv))
        silu = gv * sg
        o = jnp.concatenate([om_ref[...], os_ref[...]], axis=1)
        ab = _mx(o * silu)
        y = _dot(ab, w_ref[...])
        gate = gate_ref[0]
        xo = x_ref[...] + gate * y
        r2 = lax.rsqrt(jnp.mean(xo * xo, axis=-1, keepdims=True) + EPS)
        xh = xo * r2
        fg = fg_ref[...]
        diff = xh * fg - t_ref[...]
        sq = jnp.sum(diff * diff, axis=0, keepdims=True)
        part = sq[:, 0:LANE]
        for t in range(1, D // LANE):
            part = part + sq[:, LANE * t:LANE * (t + 1)]
        dout = diff * (1.0 / D)
        dxh = dout * fg
        dx2 = r2 * (dxh - xh * jnp.mean(dxh * xh, axis=-1, keepdims=True))
        dx2_ref[...] = dx2
        dyb = _mx(dx2 * gate)
        da = _dot_nt(dyb, w_ref[...])
        do_ref[...] = da * silu
        dg_ref[...] = _mx(da * o * (sg * (1.0 + gv * (1.0 - sg))))

        @pl.when(i == 0)
        def _():
            loss_ref[...] = jnp.zeros_like(loss_ref)
            dfg_ref[...] = jnp.zeros_like(dfg_ref)
            dw_ref[...] = jnp.zeros_like(dw_ref)

        @pl.when(i % nsb == 0)
        def _():
            dgate_ref[...] = jnp.zeros_like(dgate_ref)

        loss_ref[...] += (0.5 / D) * part
        dfg_ref[...] += jnp.sum(dout * xh, axis=0, keepdims=True)
        dgate_ref[0] += jnp.sum(dx2 * y, axis=0, keepdims=True)
        dw_ref[...] += _dot_tn(ab, dyb)

    row = lambda w: pl.BlockSpec((ts, w), lambda i: (i, 0))
    full = lambda a: pl.BlockSpec(a.shape, lambda i: (0,) * a.ndim)
    per_b = pl.BlockSpec((1, 1, D), lambda i: (i // nsb, 0, 0))
    return pl.pallas_call(
        body, name="post", grid=(T // ts,),
        out_shape=[jax.ShapeDtypeStruct((T, D), F32), jax.ShapeDtypeStruct((T, 1024), F32),
                   jax.ShapeDtypeStruct((T, 1024), _MXU_DTYPE), jax.ShapeDtypeStruct((1, LANE), F32),
                   jax.ShapeDtypeStruct((1, D), F32), jax.ShapeDtypeStruct(gate.shape, F32),
                   jax.ShapeDtypeStruct(w_out.shape, F32)],
        in_specs=[row(512), row(512), row(1024), full(w_out), row(D), per_b, full(fg), row(D)],
        out_specs=[row(D), row(1024), row(1024),
                   pl.BlockSpec((1, LANE), lambda i: (0, 0)), pl.BlockSpec((1, D), lambda i: (0, 0)), per_b,
                   full(w_out)],
        compiler_params=_cparams(("arbitrary",)),
    )(om, osw, g, w_out, x2, gate, fg, tgt)


def _pre_bwd(dq, dk, dv, dqs, dks, dvs, dg, zq, zkv, ql, kvl, x2, dx2, scale, ng, w_in, gq, gkv, w_uq, w_uk, w_uv,
             rope, S, ts):
    T, D = x2.shape
    nsb = S // ts
    WQ = MLA_HEADS * LANE

    def body(dq_ref, dk_ref, dv_ref, dqs_ref, dks_ref, dvs_ref, dg_ref, zq_ref, zkv_ref, ql_ref, kvl_ref, x_ref, dx2_ref,
             sc_ref,
             ng_ref, w_ref, gq_ref, gkv_ref, wuq_ref, wuk_ref, wuv_ref, cd_ref, sd_ref,
             gx_ref, dz_ref, dgq_ref, dgkv_ref, dng_ref, dsh_ref, dsc_ref, dwuq_ref, dwuk_ref, dwuv_ref):
        i = pl.program_id(0)

        @pl.when(i == 0)
        def _():
            dgq_ref[...] = jnp.zeros_like(dgq_ref)
            dgkv_ref[...] = jnp.zeros_like(dgkv_ref)
            dng_ref[...] = jnp.zeros_like(dng_ref)
            dwuq_ref[...] = jnp.zeros_like(dwuq_ref)
            dwuk_ref[...] = jnp.zeros_like(dwuk_ref)
            dwuv_ref[...] = jnp.zeros_like(dwuv_ref)

        @pl.when(i % nsb == 0)
        def _():
            dsh_ref[...] = jnp.zeros_like(dsh_ref)
            dsc_ref[...] = jnp.zeros_like(dsc_ref)

        def norm_bwd(z, dy, gain):
            r = lax.rsqrt(jnp.mean(z * z, axis=-1, keepdims=True) + EPS)
            zh = z * r
            dzh = dy * gain
            return r * (dzh - zh * jnp.mean(dzh * zh, axis=-1, keepdims=True)), jnp.sum(dy * zh, axis=0, keepdims=True)

        cq, ck, sa, sb = _rope_split(cd_ref[...], sd_ref[...])
        dqr = dq_ref[...]
        dqb = _mx(dqr * _tile_heads(cq) + pltpu.roll(dqr * _tile_heads(sa), 16, 1)
                  + pltpu.roll(dqr * _tile_heads(sb), WQ - 16, 1))
        dwuq_ref[...] += _dot_tn(ql_ref[...], dqb)
        dzq, dgq = norm_bwd(zq_ref[...], _dot_nt(dqb, wuq_ref[...]), gq_ref[...])
        dgq_ref[...] += dgq
        dkr = dk_ref[...]
        dkb = _mx(dkr)
        dwuk_ref[...] += _dot_tn(kvl_ref[...], dkb)
        dvb = _mx(dv_ref[...])
        dwuv_ref[...] += _dot_tn(kvl_ref[...], dvb)
        dzkv, dgkv = norm_bwd(zkv_ref[...], _dot_nt(dkb, wuk_ref[...]) + _dot_nt(dvb, wuv_ref[...]), gkv_ref[...])
        dgkv_ref[...] += dgkv
        dkpe = dkr[:, 0:LANE]
        for h in range(1, MLA_HEADS):
            dkpe = dkpe + dkr[:, LANE * h:LANE * (h + 1)]
        dkro = dkpe * ck + pltpu.roll(dkpe * sa, 16, 1) + pltpu.roll(dkpe * sb, LANE - 16, 1)
        dz_ref[:, 0:384] = _mx(dzq)
        dz_ref[:, 384:640] = _mx(dzkv)
        dz_ref[:, 640:768] = _mx(dkro)
        dz_ref[:, 768:1280] = dg_ref[:, 0:512]
        dz_ref[:, 1280:1792] = _mx(dqs_ref[...])
        dz_ref[:, 1792:1920] = _mx(dks_ref[...])
        dz_ref[:, 1920:2048] = _mx(dvs_ref[...])
        dz_ref[:, 2048:2560] = dg_ref[:, 512:1024]
        dh = _dot_nt(dz_ref[...], w_ref[...])
        x = x_ref[...]
        r1 = lax.rsqrt(jnp.mean(x * x, axis=-1, keepdims=True) + EPS)
        xn = x * r1
        ng = ng_ref[...]
        sc1 = 1.0 + sc_ref[0]
        dsh_ref[0] += jnp.sum(dh, axis=0, keepdims=True)
        dsc_ref[0] += jnp.sum(dh * (xn * ng), axis=0, keepdims=True)
        dng_ref[...] += jnp.sum(dh * xn * sc1, axis=0, keepdims=True)
        dxn = dh * ng * sc1
        gx_ref[...] = dx2_ref[...] + r1 * (dxn - xn * jnp.mean(dxn * xn, axis=-1, keepdims=True))

    row = lambda w: pl.BlockSpec((ts, w), lambda i: (i, 0))
    full = lambda a: pl.BlockSpec(a.shape, lambda i: (0,) * a.ndim)
    per_b = pl.BlockSpec((1, 1, D), lambda i: (i // nsb, 0, 0))
    dense = pl.BlockSpec((ts // 8, LANE), lambda i: (i, 0))
    vec = lambda w: pl.BlockSpec((1, w), lambda i: (0, 0))
    return pl.pallas_call(
        body, name="pre_bwd", grid=(T // ts,),
        out_shape=[jax.ShapeDtypeStruct((T, D), F32), jax.ShapeDtypeStruct((T, D_IN_PAD), _MXU_DTYPE), jax.ShapeDtypeStruct((1, 384), F32),
                   jax.ShapeDtypeStruct((1, 256), F32), jax.ShapeDtypeStruct((1, D), F32),
                   jax.ShapeDtypeStruct(scale.shape, F32), jax.ShapeDtypeStruct(scale.shape, F32),
                   jax.ShapeDtypeStruct(w_uq.shape, F32), jax.ShapeDtypeStruct(w_uk.shape, F32),
                   jax.ShapeDtypeStruct(w_uv.shape, F32)],
        in_specs=[row(WQ), row(WQ), row(512), row(512), row(LANE), row(LANE), row(1024), row(384), row(256), row(384),
                  row(256), row(D), row(D), per_b, full(ng), full(w_in), full(gq), full(gkv), full(w_uq), full(w_uk), full(w_uv),
                  dense, dense],
        out_specs=[row(D), row(D_IN_PAD), vec(384), vec(256), vec(D), per_b, per_b, full(w_uq), full(w_uk), full(w_uv)],
        compiler_params=_cparams(("arbitrary",)),
    )(dq, dk, dv, dqs, dks, dvs, dg, zq, zkv, ql, kvl, x2, dx2, scale, ng, w_in, gq, gkv, w_uq, w_uk, w_uv, *rope)


def _tn_matmul(a, b, tn, tk, name):
    T, M = a.shape
    N = b.shape[1]

    def body(a_ref, b_ref, o_ref):
        @pl.when(pl.program_id(1) == 0)
        def _():
            o_ref[...] = jnp.zeros_like(o_ref)

        o_ref[...] += _dot_tn(a_ref[...], b_ref[...])

    return pl.pallas_call(
        body, name=name, grid=(N // tn, T // tk),
        out_shape=jax.ShapeDtypeStruct((M, N), F32),
        in_specs=[pl.BlockSpec((tk, M), lambda j, k: (k, 0)), pl.BlockSpec((tk, tn), lambda j, k: (k, j))],
        out_specs=pl.BlockSpec((M, tn), lambda j, k: (0, j)),
        compiler_params=_cparams(("arbitrary", "arbitrary")),
    )(a, b)


def _finalize(parts_all, dmod_all, dmod_cols, c_act_all):
    nparts = parts_all.shape[-1]
    nmod = dmod_all.shape[-1]

    def body(p_ref, dm_ref, dmc_ref, c_ref, ps_ref, loss_ref, db_ref, dw_ref):
        acc = p_ref[0]
        for j in range(1, N_DEV):
            acc = acc + p_ref[j]
        ps_ref[...] = acc
        loss_ref[...] = jnp.sum(acc[:, 0:LANE], axis=1, keepdims=True)
        db = dm_ref[0:1, :]
        for j in range(1, dm_ref.shape[0]):
            db = db + dm_ref[j:j + 1, :]
        db_ref[...] = db
        dw_ref[...] = _dot_tn(_mx(_silu(c_ref[...])), _mx(dmc_ref[...]))

    return pl.pallas_call(
        body, name="finalize",
        out_shape=[jax.ShapeDtypeStruct((1, nparts), F32), jax.ShapeDtypeStruct((1, 1), F32),
                   jax.ShapeDtypeStruct((1, nmod), F32),
                   jax.ShapeDtypeStruct((c_act_all.shape[1], dmod_cols.shape[1]), F32)],
        in_specs=[_vmem()] * 4, out_specs=[_vmem()] * 4,
        compiler_params=_cparams(),
    )(parts_all, dmod_all, dmod_cols, c_act_all)


def _adamw(ws, gs, ms, vs):
    n = len(ws)

    def body(*refs):
        w_refs, g_refs, m_refs, v_refs, d_refs, nm_refs, nv_refs = (refs[t * n:(t + 1) * n] for t in range(7))
        for t in range(n):
            gv = g_refs[t][...]
            nm = ADAM_B1 * m_refs[t][...] + (1.0 - ADAM_B1) * gv
            nv = ADAM_B2 * v_refs[t][...] + (1.0 - ADAM_B2) * (gv * gv)
            m_hat = nm / (1.0 - ADAM_B1 ** ADAM_STEP)
            v_hat = nv / (1.0 - ADAM_B2 ** ADAM_STEP)
            d_refs[t][...] = -ADAM_LR * (m_hat / (jnp.sqrt(v_hat) + ADAM_EPS) + ADAM_WD * w_refs[t][...])
            nm_refs[t][...] = nm
            nv_refs[t][...] = nv

    out = pl.pallas_call(
        body, name="adamw",
        out_shape=[jax.ShapeDtypeStruct(w.shape, F32) for w in ws] * 3,
        in_specs=[_vmem()] * (4 * n), out_specs=[_vmem()] * (3 * n),
        compiler_params=_cparams(),
    )(*ws, *gs, *ms, *vs)
    return out[:n], out[n:2 * n], out[2 * n:]


def _pair_perm(a, axis, order):
    a = jnp.moveaxis(a, axis, -1)
    lead = a.shape[:-1]
    a = a.reshape(lead + (8, 64))[..., list(order), :].reshape(lead + (512,))
    return jnp.moveaxis(a, -1, axis)


def _pad_w_in(w):
    z = lambda n: jnp.zeros((w.shape[0], n), w.dtype)
    return jnp.concatenate([
        w[:, 0:640], z(64), w[:, 640:672], z(32), w[:, 672:1184],
        _pair_perm(w[:, 1184:1696], 1, PAIR_ORDER), w[:, 1696:1952], _pair_perm(w[:, 1952:2464], 1, PAIR_ORDER)], axis=1)


def _unpad_w_in(g):
    return jnp.concatenate([
        g[:, 0:640], g[:, 704:736], g[:, 768:1280],
        _pair_perm(g[:, 1280:1792], 1, PAIR_INV), g[:, 1792:2048], _pair_perm(g[:, 2048:2560], 1, PAIR_INV)], axis=1)


def _rope_table(positions):
    T = positions.size
    inv = ROPE_THETA ** (-jnp.arange(0, MLA_ROPE, 2, dtype=F32) / MLA_ROPE)
    pos = jnp.repeat(positions.reshape(T // 8, 8)[:, ::-1].astype(F32), MLA_ROPE // 2, axis=1)
    ang = pos * jnp.tile(inv, 8)[None, :]
    return jnp.cos(ang), jnp.sin(ang)


def _local_step(x, mod, positions, ng, w_in, gq, gkv, w_uq, w_ukv, sinks, w_out, fg, tgt,
                ts=512, fq=512, fk=512, bq=512, bk=512):
    B, S, D = x.shape
    T = B * S
    x2 = x.reshape(T, D)
    shift, scale, gate = (mod[:, None, k * D:(k + 1) * D] for k in range(3))
    w_in_p = _pad_w_in(w_in)
    w_uq_p = jnp.pad(w_uq.reshape(Q_LORA, MLA_HEADS, 96), ((0, 0), (0, 0), (0, 32))).reshape(Q_LORA, MLA_HEADS * LANE)
    w_ukv3 = w_ukv.reshape(KV_LORA, MLA_HEADS, 128)
    w_uk_p = jnp.pad(w_ukv3[:, :, :64], ((0, 0), (0, 0), (0, 64))).reshape(KV_LORA, MLA_HEADS * LANE)
    w_uv = w_ukv3[:, :, 64:].reshape(KV_LORA, 512)
    w_out_p = jnp.concatenate([w_out[:512], _pair_perm(w_out[512:], 0, PAIR_ORDER)], axis=0)
    rope = _rope_table(positions)
    posf = positions.astype(F32)
    posb = jnp.broadcast_to(posf[:, :, None], (B, S, LANE))
    posr = posf.reshape(B, S // WINDOW, WINDOW)
    sinks_l = jnp.pad(sinks.reshape(1, SWA_HEADS), ((0, 0), (0, LANE - SWA_HEADS)))

    (hb, zq, zkv, ql, kvl, q, k, v, qs, ks, vs, g) = _pre_fwd(
        x2, shift, scale, ng, w_in_p, gq, gkv, w_uq_p, w_uk_p, w_uv, rope, S, ts)
    r3 = lambda a: a.reshape(B, S, a.shape[-1])
    om, lse_m = _mla_fwd(r3(q), r3(k), r3(v), fq, fk)
    osw, lse_s = _swa_fwd(r3(qs), r3(ks), r3(vs), posb, posr, sinks_l)
    dx2, do, dg, loss_v, dfg, dgate, dw_out_p = _post(
        om.reshape(T, 512), osw.reshape(T, 512), g, w_out_p, x2, gate, fg.reshape(1, D), tgt.reshape(T, D), S, ts)
    do3 = r3(do)
    dq, dk, dv = _mla_bwd(r3(q), r3(k), r3(v), om, do3, lse_m, bq, bk)
    dqs, dks, dvs, dsink = _swa_bwd(r3(qs), r3(ks), r3(vs), posb, posr, sinks_l, osw, do3, lse_s)
    f2 = lambda a: a.reshape(T, a.shape[-1])
    gx, dz, dgq, dgkv, dng, dsh, dsc, dw_uq_p, dw_uk_p, dw_uv = _pre_bwd(
        f2(dq), f2(dk), f2(dv), f2(dqs), f2(dks), f2(dvs), dg, zq, zkv, ql, kvl, x2, dx2, scale, ng, w_in_p, gq, gkv,
        w_uq_p, w_uk_p, w_uv, rope, S, ts // 2)
    tk = min(T, 1024)
    dw_in = _unpad_w_in(_tn_matmul(hb, dz, 512, tk, "dw_in"))
    dw_out = jnp.concatenate([dw_out_p[:512], _pair_perm(dw_out_p[512:], 0, PAIR_INV)], axis=0)
    dw_uq = dw_uq_p.reshape(Q_LORA, MLA_HEADS, LANE)[:, :, :96].reshape(Q_LORA, 768)
    dw_uk = dw_uk_p.reshape(KV_LORA, MLA_HEADS, LANE)[:, :, :64]
    dw_uv = dw_uv.reshape(KV_LORA, MLA_HEADS, 64)
    dw_ukv = jnp.concatenate([dw_uk, dw_uv], axis=2).reshape(KV_LORA, 1024)
    parts = jnp.concatenate([loss_v, dfg, dng, dgq, dgkv, dsink], axis=1)
    dmod = jnp.concatenate([dsh, dsc, dgate], axis=2).reshape(B, 3 * D)
    return gx.reshape(B, S, D), dw_in, dw_uq, dw_ukv, dw_out, parts, dmod


def kernel(x, c, positions, w_ada, b_ada, norm_gain, w_in, q_norm_gain, kv_norm_gain, w_uq, w_ukv, swa_sinks, w_out, final_gain, loss_target, m_w_ada, m_b_ada, m_norm_gain, m_w_in, m_q_norm_gain, m_kv_norm_gain, m_w_uq, m_w_ukv, m_swa_sinks, m_w_out, m_final_gain, v_w_ada, v_b_ada, v_norm_gain, v_w_in, v_q_norm_gain, v_kv_norm_gain, v_w_uq, v_w_ukv, v_swa_sinks, v_w_out, v_final_gain):
    B, S, D = x.shape
    me = 4 * lax.axis_index("x") + 2 * lax.axis_index("y") + lax.axis_index("c")
    bf = _MXU_DTYPE

    c_all, win_g, wuq_g, wukv_g, wout_g = _all_gather(
        [c, w_in[0].astype(bf), w_uq[0].astype(bf), w_ukv[0].astype(bf), w_out[0].astype(bf)], "ag_weights")
    c_all = c_all.reshape(N_DEV * B, D)
    cat_cols = lambda a: jnp.transpose(a, (1, 0, 2)).reshape(a.shape[1], N_DEV * a.shape[2])
    w_in_f, w_uq_f, w_ukv_f = cat_cols(win_g), cat_cols(wuq_g), cat_cols(wukv_g)
    w_out_f = wout_g.reshape(D, D)

    ncol = w_ada.shape[2]
    b_cols = lax.dynamic_slice_in_dim(b_ada, me * ncol, ncol, axis=1)
    (mod_g,) = _all_gather([_ada_fwd(c_all, w_ada[0], b_cols)], "ag_mod")
    mod = lax.dynamic_slice_in_dim(mod_g, me * B, B, axis=1)
    mod = jnp.transpose(mod, (1, 0, 2)).reshape(B, 3 * D)

    gx, dw_in, dw_uq, dw_ukv, dw_out, parts, dmod = _local_step(
        x, mod, positions, norm_gain, w_in_f, q_norm_gain, kv_norm_gain, w_uq_f, w_ukv_f, swa_sinks,
        w_out_f, final_gain, loss_target)

    parts_g, dmod_g = _all_gather([parts, dmod], "ag_small")
    dmod_all = dmod_g.reshape(N_DEV * B, 3 * D)
    dmod_cols = lax.dynamic_slice_in_dim(dmod_all, me * ncol, ncol, axis=1)
    psum, loss, g_b_ada, g_w_ada = _finalize(parts_g, dmod_all, dmod_cols, c_all)
    loss = loss.reshape(())
    o = LANE
    g_final_gain = psum[0, o:o + D]
    g_norm_gain = psum[:, o + D:o + 2 * D]
    o += 2 * D
    g_q_norm_gain = psum[:, o:o + Q_LORA]
    g_kv_norm_gain = psum[:, o + Q_LORA:o + Q_LORA + KV_LORA]
    o += Q_LORA + KV_LORA
    g_sinks = psum[:, o:o + SWA_HEADS]

    split_cols = lambda a: jnp.transpose(a.reshape(a.shape[0], 4, 2, a.shape[1] // N_DEV), (2, 1, 0, 3))
    split_rows = lambda a: jnp.transpose(a.reshape(4, 2, a.shape[0] // N_DEV, a.shape[1]), (1, 0, 2, 3))
    g_w_in, g_w_uq, g_w_ukv, g_w_out = _reduce_scatter(
        [split_cols(dw_in), split_cols(dw_uq), split_cols(dw_ukv), split_rows(dw_out)], "rs_grads")

    grads = [g_w_ada, g_b_ada, g_norm_gain, g_w_in, g_q_norm_gain, g_kv_norm_gain, g_w_uq, g_w_ukv, g_sinks, g_w_out,
             g_final_gain]
    ws = [w_ada, b_ada, norm_gain, w_in, q_norm_gain, kv_norm_gain, w_uq, w_ukv, swa_sinks, w_out, final_gain]
    ms = [m_w_ada, m_b_ada, m_norm_gain, m_w_in, m_q_norm_gain, m_kv_norm_gain, m_w_uq, m_w_ukv, m_swa_sinks, m_w_out,
          m_final_gain]
    vs = [v_w_ada, v_b_ada, v_norm_gain, v_w_in, v_q_norm_gain, v_kv_norm_gain, v_w_uq, v_w_ukv, v_swa_sinks, v_w_out,
          v_final_gain]
    two_d = [(1, w.shape[0]) if w.ndim == 1 else w.shape[-2:] for w in ws]
    flat = lambda arrs: [a.reshape(s) for a, s in zip(arrs, two_d)]
    deltas, new_ms, new_vs = _adamw(flat(ws), flat(grads), flat(ms), flat(vs))
    shaped = lambda arrs: [a.reshape(w.shape) for a, w in zip(arrs, ws)]
    return (loss, gx, *shaped(grads), *shaped(deltas), *shaped(new_ms), *shaped(new_vs))
```

```python
import functools

import numpy as np
import jax
import jax.numpy as jnp
from jax import lax
from jax.experimental import pallas as pl
from jax.experimental.pallas import tpu as pltpu

F32 = jnp.float32
_MXU_DTYPE = jnp.bfloat16

N_DEV = 8
D_MODEL = 1024
MLA_HEADS = 8
MLA_NOPE = 64
MLA_ROPE = 32
MLA_V = 64
Q_LORA = 384
KV_LORA = 256
SWA_HEADS = 8
SWA_KV_HEADS = 2
SWA_HEAD_DIM = 64
WINDOW = 128
ROPE_THETA = 10000.0
EPS = 1e-6
MLA_SCALE = float((MLA_NOPE + MLA_ROPE) ** -0.5)
SWA_SCALE = float(SWA_HEAD_DIM ** -0.5)
LOG2E = 1.4426950408889634
MLA_QSCALE = MLA_SCALE * LOG2E
D_IN = 2464
D_IN_PAD = 2560
PAIR_ORDER = (0, 4, 1, 5, 2, 6, 3, 7)
PAIR_INV = (0, 2, 4, 6, 1, 3, 5, 7)

ADAM_LR = 0.001
ADAM_B1 = 0.9
ADAM_B2 = 0.999
ADAM_EPS = 1e-08
ADAM_WD = 0.01
ADAM_STEP = 10

LANE = 128
VMEM_LIMIT = 56 * 1024 * 1024

MESH = pl.DeviceIdType.MESH
NEG_INF = float("-inf")
SWA_SEQ_SPLIT = 2


def _mx(a):
    return a.astype(_MXU_DTYPE)


def _dot(a, b):
    return jnp.dot(a, b, preferred_element_type=F32)


def _dot_nt(a, b):
    return lax.dot_general(a, b, (((1,), (1,)), ((), ())), preferred_element_type=F32)


def _dot_tn(a, b):
    return lax.dot_general(a, b, (((0,), (0,)), ((), ())), preferred_element_type=F32)


def _cparams(sem=None):
    return pltpu.CompilerParams(dimension_semantics=sem, vmem_limit_bytes=VMEM_LIMIT)


def _vmem():
    return pl.BlockSpec(memory_space=pltpu.VMEM)


def _lane_iota(shape):
    return lax.broadcasted_iota(jnp.int32, shape, len(shape) - 1)


def _all_gather(arrs, name):
    n = len(arrs)

    def body(*refs):
        ins, outs = refs[:n], refs[n:2 * n]
        send_sems, recv_sems, local_sems = refs[2 * n:]
        x, y, c = lax.axis_index("x"), lax.axis_index("y"), lax.axis_index("c")
        me, sibling = (x, y, c), (x, y, 1 - c)
        chips = [(1 - x, y), (x, 1 - y), (1 - x, 1 - y)]

        def slot(a, dev):
            return outs[a].at[4 * dev[0] + 2 * dev[1] + dev[2]]

        def copy(a, k, block, to, src=None):
            return pltpu.make_async_remote_copy(
                src_ref=slot(a, block) if src is None else src, dst_ref=slot(a, block),
                send_sem=send_sems.at[7 * a + k], recv_sem=recv_sems.at[7 * a + k],
                device_id=to, device_id_type=MESH)

        mine = [pltpu.make_async_copy(ins[a], slot(a, me), local_sems.at[a]) for a in range(n)]
        for cp in mine:
            cp.start()
        first = []
        for a in range(n):
            first.append(copy(a, 0, me, sibling, src=ins[a]))
            first += [copy(a, 1 + j, me, (*chip, c), src=ins[a]) for j, chip in enumerate(chips)]
        for cp in first:
            cp.start()
        passed = []
        for j, chip in enumerate(chips):
            for a in range(n):
                copy(a, 1 + j, (*chip, c), me).wait_recv()
                cp = copy(a, 4 + j, (*chip, c), sibling)
                cp.start()
                passed.append(cp)
        for a in range(n):
            copy(a, 0, sibling, me).wait_recv()
            for j, chip in enumerate(chips):
                copy(a, 4 + j, (*chip, 1 - c), me).wait_recv()
        for cp in first + passed:
            cp.wait_send()
        for cp in mine:
            cp.wait()

    return pl.pallas_call(
        body, name=name,
        out_shape=[jax.ShapeDtypeStruct((N_DEV,) + a.shape, a.dtype) for a in arrs],
        in_specs=[_vmem()] * n, out_specs=[_vmem()] * n,
        scratch_shapes=[pltpu.SemaphoreType.DMA((7 * n,)), pltpu.SemaphoreType.DMA((7 * n,)),
                        pltpu.SemaphoreType.DMA((n,))],
        compiler_params=pltpu.CompilerParams(vmem_limit_bytes=VMEM_LIMIT),
    )(*arrs)


def _reduce_scatter(arrs, name):
    n = len(arrs)
    halves = [a.astype(jnp.bfloat16) for a in arrs]

    def body(*refs):
        xs, xbs, outs = refs[:n], refs[n:2 * n], refs[2 * n:3 * n]
        parts, recv_a, send_b, recv_b = (refs[(3 + t) * n:(4 + t) * n] for t in range(4))
        send_sems, recv_sems, local_sems = refs[7 * n:]
        x, y, c = lax.axis_index("x"), lax.axis_index("y"), lax.axis_index("c")
        myq = 2 * x + y

        def chip(k):
            return (1 - x if k & 2 else x, 1 - y if k & 1 else y)

        def from_sibling(a):
            return pltpu.make_async_remote_copy(
                src_ref=xbs[a].at[:, 1 - c], dst_ref=recv_a[a], send_sem=send_sems.at[4 * a], recv_sem=recv_sems.at[4 * a],
                device_id=(x, y, 1 - c), device_id_type=MESH)

        def to_owner(a, k):
            qx, qy = chip(k)
            return pltpu.make_async_remote_copy(
                src_ref=send_b[a].at[2 * qx + qy], dst_ref=recv_b[a].at[myq],
                send_sem=send_sems.at[4 * a + k], recv_sem=recv_sems.at[4 * a + k],
                device_id=(qx, qy, c), device_id_type=MESH)

        mine = [pltpu.make_async_copy(xs[a].at[:, c], parts[a], local_sems.at[a]) for a in range(n)]
        first = [from_sibling(a) for a in range(n)]
        for cp in mine + first:
            cp.start()
        second = []
        for a in range(n):
            mine[a].wait()
            first[a].wait_recv()
            parts[a][...] = parts[a][...] + recv_a[a][...].astype(F32)
            send_b[a][...] = parts[a][...].astype(jnp.bfloat16)
            for k in range(1, 4):
                cp = to_owner(a, k)
                cp.start()
                second.append(cp)
        for a in range(n):
            acc = parts[a][myq]
            for k in range(1, 4):
                to_owner(a, k).wait_recv()
                qx, qy = chip(k)
                acc = acc + recv_b[a][2 * qx + qy].astype(F32)
            outs[a][...] = acc
        for cp in first + second:
            cp.wait_send()

    quarter = lambda a, dt: pltpu.VMEM((4,) + a.shape[2:], dt)
    return pl.pallas_call(
        body, name=name,
        out_shape=[jax.ShapeDtypeStruct(a.shape[2:], F32) for a in arrs],
        in_specs=[pl.BlockSpec(memory_space=pl.ANY)] * (2 * n), out_specs=[_vmem()] * n,
        scratch_shapes=[quarter(a, F32) for a in arrs] + [quarter(a, jnp.bfloat16) for a in arrs] * 3
        + [pltpu.SemaphoreType.DMA((4 * n,)), pltpu.SemaphoreType.DMA((4 * n,)), pltpu.SemaphoreType.DMA((n,))],
        compiler_params=pltpu.CompilerParams(vmem_limit_bytes=VMEM_LIMIT),
    )(*arrs, *halves)


def _silu(t):
    return t * (1.0 / (1.0 + jnp.exp(-t)))


def _ada_fwd(c_act_all, w_ada_shard, b_shard):
    def body(c_ref, w_ref, b_ref, o_ref):
        o_ref[...] = _dot(_mx(_silu(c_ref[...])), _mx(w_ref[...])) + b_ref[...]

    return pl.pallas_call(
        body, name="ada_fwd",
        out_shape=jax.ShapeDtypeStruct((c_act_all.shape[0], w_ada_shard.shape[1]), F32),
        in_specs=[_vmem()] * 3, out_specs=_vmem(),
        compiler_params=_cparams(),
    )(c_act_all, w_ada_shard, b_shard)


def _rope_split(cd, sd):
    n = cd.shape[0]

    def expand(d):
        rep = jnp.broadcast_to(d[:, None, :], (n, 8, LANE)).reshape(8 * n, LANE)
        return pltpu.roll(rep, 0, 1, stride=16, stride_axis=0)

    c, s = expand(cd), expand(sd)
    lane = _lane_iota(c.shape)
    first = jnp.logical_and(lane >= 64, lane < 80)
    second = jnp.logical_and(lane >= 80, lane < 96)
    ck = jnp.where(first, pltpu.roll(c, 80, 1), jnp.where(second, pltpu.roll(c, 96, 1), 0.0))
    cq = jnp.where(lane < 64, 1.0, ck)
    sa = jnp.where(first, -pltpu.roll(s, 80, 1), 0.0)
    sb = jnp.where(second, pltpu.roll(s, 96, 1), 0.0)
    return cq, ck, sa, sb


def _tile_heads(t):
    return jnp.concatenate([t] * MLA_HEADS, axis=1)


def _pre_fwd(x2, shift, scale, ng, w_in, gq, gkv, w_uq, w_uk, w_uv, rope, S, ts):
    T, D = x2.shape
    nsb = S // ts
    WQ = MLA_HEADS * LANE

    def body(x_ref, sh_ref, sc_ref, ng_ref, w_ref, gq_ref, gkv_ref, wuq_ref, wuk_ref, wuv_ref,
             cd_ref, sd_ref,
             hb_ref, zq_ref, zkv_ref, ql_ref, kvl_ref, q_ref, k_ref, v_ref, qs_ref, ks_ref, vs_ref, g_ref):
        x = x_ref[...]
        r1 = lax.rsqrt(jnp.mean(x * x, axis=-1, keepdims=True) + EPS)
        h = ((x * r1) * ng_ref[...]) * (1.0 + sc_ref[0]) + sh_ref[0]
        hb = _mx(h)
        hb_ref[...] = hb
        zq = _dot(hb, w_ref[:, 0:384])
        zq_ref[...] = zq
        rq = lax.rsqrt(jnp.mean(zq * zq, axis=-1, keepdims=True) + EPS)
        ql = _mx((zq * rq) * gq_ref[...])
        ql_ref[...] = ql
        q = _dot(ql, wuq_ref[...])
        cq, ck, sa, sb = _rope_split(cd_ref[...], sd_ref[...])
        q = (q * _tile_heads(cq) + pltpu.roll(q, WQ - 16, 1) * _tile_heads(sa)
             + pltpu.roll(q, 16, 1) * _tile_heads(sb))
        q_ref[...] = _mx(q * MLA_QSCALE)
        zkv = _dot(hb, w_ref[:, 384:640])
        zkv_ref[...] = zkv
        rkv = lax.rsqrt(jnp.mean(zkv * zkv, axis=-1, keepdims=True) + EPS)
        kvl = _mx((zkv * rkv) * gkv_ref[...])
        kvl_ref[...] = kvl
        kr = _dot(hb, w_ref[:, 640:768])
        kpe = kr * ck + pltpu.roll(kr, LANE - 16, 1) * sa + pltpu.roll(kr, 16, 1) * sb
        kf = _dot(kvl, wuk_ref[...])
        k_ref[...] = _mx(kf + jnp.concatenate([kpe] * MLA_HEADS, axis=1))
        v_ref[...] = _mx(_dot(kvl, wuv_ref[...]))
        g_ref[:, 0:512] = _dot(hb, w_ref[:, 768:1280])
        qs_ref[...] = _mx(_dot(hb, w_ref[:, 1280:1792]) * SWA_SCALE)
        ks_ref[...] = _mx(_dot(hb, w_ref[:, 1792:1920]))
        vs_ref[...] = _mx(_dot(hb, w_ref[:, 1920:2048]))
        g_ref[:, 512:1024] = _dot(hb, w_ref[:, 2048:2560])

    row = lambda w: pl.BlockSpec((ts, w), lambda i: (i, 0))
    dense = pl.BlockSpec((ts // 8, LANE), lambda i: (i, 0))
    full = lambda a: pl.BlockSpec(a.shape, lambda i: (0,) * a.ndim)
    per_b = pl.BlockSpec((1, 1, D), lambda i: (i // nsb, 0, 0))
    out_w = [(D, _MXU_DTYPE), (384, F32), (256, F32), (384, _MXU_DTYPE), (256, _MXU_DTYPE), (WQ, _MXU_DTYPE),
             (WQ, _MXU_DTYPE), (512, _MXU_DTYPE), (512, _MXU_DTYPE), (128, _MXU_DTYPE), (128, _MXU_DTYPE), (1024, F32)]
    return pl.pallas_call(
        body, name="pre_fwd", grid=(T // ts,),
        out_shape=[jax.ShapeDtypeStruct((T, w), dt) for w, dt in out_w],
        in_specs=[row(D), per_b, per_b, full(ng), full(w_in), full(gq), full(gkv), full(w_uq), full(w_uk), full(w_uv),
                  dense, dense],
        out_specs=[row(w) for w, _ in out_w],
        compiler_params=_cparams(("arbitrary",)),
    )(x2, shift, scale, ng, w_in, gq, gkv, w_uq, w_uk, w_uv, *rope)


def _mla_fwd(q3, k3, v3, tq, tk):
    B, S, _ = q3.shape
    nq = S // tq
    assert tk % tq == 0 or tq % tk == 0
    n_masked = max(1, tq // tk)

    def body(q_ref, k_ref, v_ref, o_ref, lse_ref):
        qi = pl.program_id(2)
        rows = lax.broadcasted_iota(jnp.int32, (tq, tk), 0)
        cols = lax.broadcasted_iota(jnp.int32, (tq, tk), 1)
        qs = [q_ref[0, :, LANE * j:LANE * (j + 1)] for j in range(2)]
        n_full = (qi * tq) // tk

        def step(kt, carry, masked):
            r0 = pl.multiple_of(kt * tk, tk)
            v2 = v_ref[0, pl.ds(r0, tk), :]
            out = []
            for j in range(2):
                m, l, acc = carry[j]
                s = _dot_nt(qs[j], k_ref[0, pl.ds(r0, tk), LANE * j:LANE * (j + 1)])
                if masked:
                    s = jnp.where(rows + qi * tq >= cols + kt * tk, s, NEG_INF)
                m_new = jnp.maximum(m, jnp.max(s, axis=1, keepdims=True))
                alpha = jnp.exp2(m - m_new)
                p = jnp.exp2(s - m_new)
                l = alpha * l + jnp.sum(p, axis=1, keepdims=True)
                acc = alpha * acc + _dot(_mx(p), v2)
                out.append((m_new, l, acc))
            return tuple(out)

        init = (jnp.full((tq, 1), NEG_INF, F32), jnp.zeros((tq, 1), F32), jnp.zeros((tq, LANE), F32))
        carry = lax.fori_loop(0, n_full, functools.partial(step, masked=False), (init, init))
        for t in range(n_masked):
            carry = step(n_full + t, carry, True)
        (m0, l0, a0), (m1, l1, a1) = carry
        o_ref[0] = jnp.where(_lane_iota((tq, LANE)) < 64, a0 / l0, a1 / l1)
        lse_ref[0, 0] = jnp.broadcast_to(m0 + jnp.log2(l0), (tq, LANE))
        lse_ref[0, 1] = jnp.broadcast_to(m1 + jnp.log2(l1), (tq, LANE))

    return pl.pallas_call(
        body, name="mla_fwd", grid=(B, MLA_HEADS // 2, nq),
        out_shape=[jax.ShapeDtypeStruct((B, S, 512), F32), jax.ShapeDtypeStruct((B, MLA_HEADS, S, LANE), F32)],
        in_specs=[pl.BlockSpec((1, tq, 2 * LANE), lambda b, hp, i: (b, i, hp)),
                  pl.BlockSpec((1, S, 2 * LANE), lambda b, hp, i: (b, 0, hp)),
                  pl.BlockSpec((1, S, LANE), lambda b, hp, i: (b, 0, hp))],
        out_specs=[pl.BlockSpec((1, tq, LANE), lambda b, hp, i: (b, i, hp)),
                   pl.BlockSpec((1, 2, tq, LANE), lambda b, hp, i: (b, hp, i, 0))],
        compiler_params=_cparams(("arbitrary", "arbitrary", "arbitrary")),
    )(q3, k3, v3)


def _mla_bwd(q3, k3, v3, o3, do3, lse, tq, tk):
    B, S, _ = q3.shape
    nq, nk = S // tq, S // tk
    assert tk % tq == 0

    def body(q_ref, k_ref, v_ref, o_ref, do_ref, lse_ref, dq_ref, dk_ref, dv_ref, dkt_ref, dvt_ref):
        dkt_ref[...] = jnp.zeros_like(dkt_ref)
        dvt_ref[...] = jnp.zeros_like(dvt_ref)
        rows = lax.broadcasted_iota(jnp.int32, (tq, tk), 0)
        cols = lax.broadcasted_iota(jnp.int32, (tq, tk), 1)
        lane = _lane_iota((tq, LANE))

        def q_tile(qi, _):
            r = pl.ds(pl.multiple_of(qi * tq, tq), tq)
            do2, o2 = do_ref[0, r, :], o_ref[0, r, :]
            heads = []
            for j in range(2):
                lanes = slice(LANE * j, LANE * (j + 1))
                do = jnp.where((lane < 64) if j == 0 else (lane >= 64), do2, 0.0)
                q = q_ref[0, r, lanes]
                heads.append((lanes, q, _mx(q.astype(F32).T), _mx(do), _mx(do.T),
                              jnp.sum(do * o2, axis=1, keepdims=True),
                              jnp.concatenate([lse_ref[0, j, r, :]] * (tk // LANE), axis=1)))
            n_full = (qi * tq) // tk

            def k_tile(kt, dqs, masked):
                kr = pl.ds(pl.multiple_of(kt * tk, tk), tk)
                v2 = v_ref[0, kr, :]
                out = []
                dvt = None
                for j, (lanes, q, qt, dob, dot_, dcol, lse_c) in enumerate(heads):
                    k = k_ref[0, kr, lanes]
                    s = _dot_nt(q, k)
                    if masked:
                        s = jnp.where(rows + qi * tq >= cols + kt * tk, s, NEG_INF)
                    p = jnp.exp2(s - lse_c)
                    dp = _dot_nt(dob, v2)
                    dsb = _mx(p * (dp - dcol))
                    dkt_ref[j, kt] += _dot(qt, dsb)
                    pv = _dot(dot_, _mx(p))
                    dvt = pv if dvt is None else dvt + pv
                    out.append(dqs[j] + _dot(dsb, k))
                dvt_ref[kt] += dvt
                return tuple(out)

            zero = jnp.zeros((tq, LANE), F32)
            dqs = lax.fori_loop(0, n_full, functools.partial(k_tile, masked=False), (zero, zero))
            dqs = k_tile(n_full, dqs, True)
            for j in range(2):
                dq_ref[0, r, heads[j][0]] = MLA_SCALE * dqs[j]
            return 0

        lax.fori_loop(0, nq, q_tile, 0)

        def flush(kt, _):
            kr = pl.ds(pl.multiple_of(kt * tk, tk), tk)
            for j in range(2):
                dk_ref[0, kr, LANE * j:LANE * (j + 1)] = (1.0 / LOG2E) * dkt_ref[j, kt].T
            dv_ref[0, kr, :] = dvt_ref[kt].T
            return 0

        lax.fori_loop(0, nk, flush, 0)

    pair = lambda w: pl.BlockSpec((1, S, w), lambda b, hp: (b, 0, hp))
    return pl.pallas_call(
        body, name="mla_bwd", grid=(B, MLA_HEADS // 2),
        out_shape=[jax.ShapeDtypeStruct((B, S, 1024), F32), jax.ShapeDtypeStruct((B, S, 1024), F32),
                   jax.ShapeDtypeStruct((B, S, 512), F32)],
        in_specs=[pair(2 * LANE), pair(2 * LANE), pair(LANE), pair(LANE), pair(LANE),
                  pl.BlockSpec((1, 2, S, LANE), lambda b, hp: (b, hp, 0, 0))],
        out_specs=[pair(2 * LANE), pair(2 * LANE), pair(LANE)],
        scratch_shapes=[pltpu.VMEM((2, nk, LANE, tk), F32), pltpu.VMEM((nk, LANE, tk), F32)],
        compiler_params=_cparams(("arbitrary", "arbitrary")),
    )(q3, k3, v3, o3, do3, lse)


def _swa_consts(sink_ref):
    W = WINDOW
    row = lax.broadcasted_iota(jnp.int32, (4 * W, LANE), 0)
    out = []
    for g in range(SWA_KV_HEADS):
        slope = jnp.zeros((4 * W, LANE), F32)
        sink = jnp.zeros((4 * W, LANE), F32)
        for p in range(4):
            h = p + 4 * g
            here = jnp.logical_and(row >= W * p, row < W * (p + 1))
            slope = jnp.where(here, float(2.0 ** (-8.0 * (h + 1) / SWA_HEADS)), slope)
            sink = jnp.where(here, sink_ref[:, h:h + 1], sink)
        out.append((jnp.concatenate([slope, slope], axis=1), sink))
    return out


def _wide(col):
    return jnp.concatenate([col, col], axis=1)


def _swa_block(q_ref, k_ref, v_ref, pb_ref, pr_ref, n, i, ij):
    W = WINDOW
    kb = jnp.maximum(n - 1, 0)
    r = pl.ds(pl.multiple_of(i * W, W), W)
    kr = pl.ds(pl.multiple_of(kb * W, W), 2 * W)
    q4, k2, v2 = q_ref[0, r, :], k_ref[0, kr, :], v_ref[0, kr, :]
    pq = _wide(pb_ref[0, r, :])
    pk = jnp.concatenate([pr_ref[0, pl.ds(kb, 1), :], pr_ref[0, pl.ds(kb + 1, 1), :]], axis=1)
    rel = ij + (n - kb) * W
    mask = jnp.where(jnp.logical_and(rel >= 0, rel < W), 0.0, NEG_INF)
    dist4 = jnp.concatenate([pq - pk] * 4, axis=0)
    mask4 = jnp.concatenate([mask] * 4, axis=0)
    return r, kb, kr, q4, k2, v2, dist4, mask4


def _swa_stack(x4, g, dtype):
    lane = _lane_iota((WINDOW, LANE))
    mine = (lane < 64) if g == 0 else (lane >= 64)
    return jnp.concatenate([jnp.where(mine, x4[:, LANE * p:LANE * (p + 1)], 0).astype(dtype) for p in range(4)], axis=0)


def _swa_unstack(ref, r, lo, hi, scale=None):
    W = WINDOW
    low = _lane_iota((W, LANE)) < 64
    for p in range(4):
        t = jnp.where(low, lo[W * p:W * (p + 1)], hi[W * p:W * (p + 1)])
        ref[0, r, LANE * p:LANE * (p + 1)] = t if scale is None else scale * t


def _swa_fwd(qs3, ks3, vs3, posb, posr, sinks):
    B, S, _ = qs3.shape
    W = WINDOW
    nb = S // W
    nh = SWA_SEQ_SPLIT
    nbh = nb // nh

    def body(q_ref, k_ref, v_ref, pb_ref, pr_ref, sink_ref, o_ref, lse_ref):
        ij = lax.broadcasted_iota(jnp.int32, (W, 2 * W), 0) - lax.broadcasted_iota(jnp.int32, (W, 2 * W), 1)
        consts = _swa_consts(sink_ref)
        n0 = pl.program_id(1) * nbh

        def blk(i, _):
            n = n0 + i
            r, _, _, q4, k2, v2, dist4, mask4 = _swa_block(q_ref, k_ref, v_ref, pb_ref, pr_ref, n, i, ij)
            o_g = []
            for g, (slope, sink) in enumerate(consts):
                s = _dot_nt(_swa_stack(q4, g, q4.dtype), k2) - slope * dist4 + mask4
                m = jnp.maximum(jnp.max(s, axis=1, keepdims=True), sink)
                e = jnp.exp(s - _wide(m))
                l = jnp.sum(e, axis=1, keepdims=True) + jnp.exp(sink - m)
                o_g.append(_dot(_mx(e * _wide(1.0 / l)), v2))
                lse_ref[0, i, g] = m + jnp.log(l)
            _swa_unstack(o_ref, r, o_g[0], o_g[1])
            return 0

        lax.fori_loop(0, nbh, blk, 0)

    seq = lambda w: pl.BlockSpec((1, S, w), lambda b, h: (b, 0, 0))
    part = lambda w: pl.BlockSpec((1, S // nh, w), lambda b, h: (b, h, 0))
    lse_spec = pl.BlockSpec((1, nbh, 2, 4 * W, LANE), lambda b, h: (b, h, 0, 0, 0))
    return pl.pallas_call(
        body, name="swa_fwd", grid=(B, nh),
        out_shape=[jax.ShapeDtypeStruct((B, S, 512), F32), jax.ShapeDtypeStruct((B, nb, 2, 4 * W, LANE), F32)],
        in_specs=[part(512), seq(LANE), seq(LANE), part(LANE), pl.BlockSpec((1, nb, W), lambda b, h: (b, 0, 0)),
                  pl.BlockSpec((1, LANE), lambda b, h: (0, 0))],
        out_specs=[part(512), lse_spec],
        compiler_params=_cparams(("arbitrary", "arbitrary")),
    )(qs3, ks3, vs3, posb, posr, sinks)


def _swa_bwd(qs3, ks3, vs3, posb, posr, sinks, os3, do3, lse):
    B, S, _ = qs3.shape
    W = WINDOW
    nb = S // W
    nh = SWA_SEQ_SPLIT
    nbh = nb // nh

    def body(q_ref, k_ref, v_ref, pb_ref, pr_ref, sink_ref, o_ref, do_ref, lse_ref, dq_ref, dk_ref, dv_ref, dsink_ref,
             dkt_ref, dvt_ref):
        lane1 = _lane_iota((1, LANE))
        ij = lax.broadcasted_iota(jnp.int32, (W, 2 * W), 0) - lax.broadcasted_iota(jnp.int32, (W, 2 * W), 1)
        consts = _swa_consts(sink_ref)
        hh = pl.program_id(1)
        n0 = hh * nbh

        @pl.when(hh == 0)
        def _():
            dkt_ref[...] = jnp.zeros_like(dkt_ref)
            dvt_ref[...] = jnp.zeros_like(dvt_ref)

        @pl.when(jnp.logical_and(pl.program_id(0) == 0, hh == 0))
        def _():
            dsink_ref[...] = jnp.zeros_like(dsink_ref)

        def blk(i, dsink):
            n = n0 + i
            r, kb, _, q4, k2, v2, dist4, mask4 = _swa_block(q_ref, k_ref, v_ref, pb_ref, pr_ref, n, i, ij)
            o4, do4 = o_ref[0, r, :], do_ref[0, r, :]
            dq_g = []
            dkt = jnp.zeros((LANE, 2 * W), F32)
            dvt = jnp.zeros((LANE, 2 * W), F32)
            for g, (slope, sink) in enumerate(consts):
                q_st = _swa_stack(q4, g, F32)
                do_st = _swa_stack(do4, g, F32)
                dcol = jnp.sum(do_st * _swa_stack(o4, g, F32), axis=1, keepdims=True)
                qst, dob = _mx(q_st), _mx(do_st)
                lse_c = lse_ref[0, i, g]
                pr = jnp.exp(_dot_nt(qst, k2) - slope * dist4 + mask4 - _wide(lse_c))
                dsb = _mx(pr * (_dot_nt(dob, v2) - dcol))
                psd = jnp.exp(sink - lse_c)[:, 0:1] * dcol
                for p in range(4):
                    dsink = dsink - jnp.where(lane1 == p + 4 * g,
                                              jnp.sum(psd[W * p:W * (p + 1)], axis=0, keepdims=True), 0.0)
                dq_g.append(_dot(dsb, k2))
                dkt = dkt + _dot(_mx(q_st.T), dsb)
                dvt = dvt + _dot(_mx(do_st.T), _mx(pr))
            _swa_unstack(dq_ref, r, dq_g[0], dq_g[1], SWA_SCALE)
            dkt_ref[kb] += dkt[:, 0:W]
            dkt_ref[kb + 1] += dkt[:, W:2 * W]
            dvt_ref[kb] += dvt[:, 0:W]
            dvt_ref[kb + 1] += dvt[:, W:2 * W]
            return dsink

        dsink_ref[...] += lax.fori_loop(0, nbh, blk, jnp.zeros((1, LANE), F32))

        @pl.when(hh == nh - 1)
        def _():
            def flush(n, _):
                r = pl.ds(pl.multiple_of(n * W, W), W)
                dk_ref[0, r, :] = dkt_ref[n].T
                dv_ref[0, r, :] = dvt_ref[n].T
                return 0

            lax.fori_loop(0, nb, flush, 0)

    seq = lambda w: pl.BlockSpec((1, S, w), lambda b, h: (b, 0, 0))
    part = lambda w: pl.BlockSpec((1, S // nh, w), lambda b, h: (b, h, 0))
    return pl.pallas_call(
        body, name="swa_bwd", grid=(B, nh),
        out_shape=[jax.ShapeDtypeStruct((B, S, 512), F32), jax.ShapeDtypeStruct((B, S, LANE), F32),
                   jax.ShapeDtypeStruct((B, S, LANE), F32), jax.ShapeDtypeStruct((1, LANE), F32)],
        in_specs=[part(512), seq(LANE), seq(LANE), part(LANE), pl.BlockSpec((1, nb, W), lambda b, h: (b, 0, 0)),
                  pl.BlockSpec((1, LANE), lambda b, h: (0, 0)), part(512),
                  pl.BlockSpec((1, S // nh, 512), lambda b, h: (b, h, 1)),
                  pl.BlockSpec((1, nbh, 2, 4 * W, LANE), lambda b, h: (b, h, 0, 0, 0))],
        out_specs=[part(512), seq(LANE), seq(LANE), pl.BlockSpec((1, LANE), lambda b, h: (0, 0))],
        scratch_shapes=[pltpu.VMEM((nb, LANE, W), F32), pltpu.VMEM((nb, LANE, W), F32)],
        compiler_params=_cparams(("arbitrary", "arbitrary")),
    )(qs3, ks3, vs3, posb, posr, sinks, os3, do3, lse)


def _post(om, osw, g, w_out, x2, gate, fg, tgt, S, ts):
    T, D = x2.shape
    nsb = S // ts

    def body(om_ref, os_ref, g_ref, w_ref, x_ref, gate_ref, fg_ref, t_ref,
             dx2_ref, do_ref, dg_ref, loss_ref, dfg_ref, dgate_ref, dw_ref):
        i = pl.program_id(0)
        gv = g_ref[...]
        sg = 1.0 / (1.0 + jnp.exp(-gv))
        silu = gv * sg
        o = jnp.concatenate([om_ref[...], os_ref[...]], axis=1)
        ab = _mx(o * silu)
        y = _dot(ab, w_ref[...])
        gate = gate_ref[0]
        xo = x_ref[...] + gate * y
        r2 = lax.rsqrt(jnp.mean(xo * xo, axis=-1, keepdims=True) + EPS)
        xh = xo * r2
        fg = fg_ref[...]
        diff = xh * fg - t_ref[...]
        sq = jnp.sum(diff * diff, axis=0, keepdims=True)
        part = sq[:, 0:LANE]
        for t in range(1, D // LANE):
            part = part + sq[:, LANE * t:LANE * (t + 1)]
        dout = diff * (1.0 / D)
        dxh = dout * fg
        dx2 = r2 * (dxh - xh * jnp.mean(dxh * xh, axis=-1, keepdims=True))
        dx2_ref[...] = dx2
        dyb = _mx(dx2 * gate)
        da = _dot_nt(dyb, w_ref[...])
        do_ref[...] = da * silu
        dg_ref[...] = _mx(da * o * (sg * (1.0 + gv * (1.0 - sg))))

        @pl.when(i == 0)
        def _():
            loss_ref[...] = jnp.zeros_like(loss_ref)
            dfg_ref[...] = jnp.zeros_like(dfg_ref)
            dw_ref[...] = jnp.zeros_like(dw_ref)

        @pl.when(i % nsb == 0)
        def _():
            dgate_ref[...] = jnp.zeros_like(dgate_ref)

        loss_ref[...] += (0.5 / D) * part
        dfg_ref[...] += jnp.sum(dout * xh, axis=0, keepdims=True)
        dgate_ref[0] += jnp.sum(dx2 * y, axis=0, keepdims=True)
        dw_ref[...] += _dot_tn(ab, dyb)

    row = lambda w: pl.BlockSpec((ts, w), lambda i: (i, 0))
    full = lambda a: pl.BlockSpec(a.shape, lambda i: (0,) * a.ndim)
    per_b = pl.BlockSpec((1, 1, D), lambda i: (i // nsb, 0, 0))
    return pl.pallas_call(
        body, name="post", grid=(T // ts,),
        out_shape=[jax.ShapeDtypeStruct((T, D), F32), jax.ShapeDtypeStruct((T, 1024), F32),
                   jax.ShapeDtypeStruct((T, 1024), _MXU_DTYPE), jax.ShapeDtypeStruct((1, LANE), F32),
                   jax.ShapeDtypeStruct((1, D), F32), jax.ShapeDtypeStruct(gate.shape, F32),
                   jax.ShapeDtypeStruct(w_out.shape, F32)],
        in_specs=[row(512), row(512), row(1024), full(w_out), row(D), per_b, full(fg), row(D)],
        out_specs=[row(D), row(1024), row(1024),
                   pl.BlockSpec((1, LANE), lambda i: (0, 0)), pl.BlockSpec((1, D), lambda i: (0, 0)), per_b,
                   full(w_out)],
        compiler_params=_cparams(("arbitrary",)),
    )(om, osw, g, w_out, x2, gate, fg, tgt)


def _pre_bwd(dq, dk, dv, dqs, dks, dvs, dg, zq, zkv, ql, kvl, x2, dx2, scale, ng, w_in, gq, gkv, w_uq, w_uk, w_uv,
             rope, S, ts):
    T, D = x2.shape
    nsb = S // ts
    WQ = MLA_HEADS * LANE

    def body(dq_ref, dk_ref, dv_ref, dqs_ref, dks_ref, dvs_ref, dg_ref, zq_ref, zkv_ref, ql_ref, kvl_ref, x_ref, dx2_ref,
             sc_ref,
             ng_ref, w_ref, gq_ref, gkv_ref, wuq_ref, wuk_ref, wuv_ref, cd_ref, sd_ref,
             gx_ref, dz_ref, dgq_ref, dgkv_ref, dng_ref, dsh_ref, dsc_ref, dwuq_ref, dwuk_ref, dwuv_ref):
        i = pl.program_id(0)

        @pl.when(i == 0)
        def _():
            dgq_ref[...] = jnp.zeros_like(dgq_ref)
            dgkv_ref[...] = jnp.zeros_like(dgkv_ref)
            dng_ref[...] = jnp.zeros_like(dng_ref)
            dwuq_ref[...] = jnp.zeros_like(dwuq_ref)
            dwuk_ref[...] = jnp.zeros_like(dwuk_ref)
            dwuv_ref[...] = jnp.zeros_like(dwuv_ref)

        @pl.when(i % nsb == 0)
        def _():
            dsh_ref[...] = jnp.zeros_like(dsh_ref)
            dsc_ref[...] = jnp.zeros_like(dsc_ref)

        def norm_bwd(z, dy, gain):
            r = lax.rsqrt(jnp.mean(z * z, axis=-1, keepdims=True) + EPS)
            zh = z * r
            dzh = dy * gain
            return r * (dzh - zh * jnp.mean(dzh * zh, axis=-1, keepdims=True)), jnp.sum(dy * zh, axis=0, keepdims=True)

        cq, ck, sa, sb = _rope_split(cd_ref[...], sd_ref[...])
        dqr = dq_ref[...]
        dqb = _mx(dqr * _tile_heads(cq) + pltpu.roll(dqr * _tile_heads(sa), 16, 1)
                  + pltpu.roll(dqr * _tile_heads(sb), WQ - 16, 1))
        dwuq_ref[...] += _dot_tn(ql_ref[...], dqb)
        dzq, dgq = norm_bwd(zq_ref[...], _dot_nt(dqb, wuq_ref[...]), gq_ref[...])
        dgq_ref[...] += dgq
        dkr = dk_ref[...]
        dkb = _mx(dkr)
        dwuk_ref[...] += _dot_tn(kvl_ref[...], dkb)
        dvb = _mx(dv_ref[...])
        dwuv_ref[...] += _dot_tn(kvl_ref[...], dvb)
        dzkv, dgkv = norm_bwd(zkv_ref[...], _dot_nt(dkb, wuk_ref[...]) + _dot_nt(dvb, wuv_ref[...]), gkv_ref[...])
        dgkv_ref[...] += dgkv
        dkpe = dkr[:, 0:LANE]
        for h in range(1, MLA_HEADS):
            dkpe = dkpe + dkr[:, LANE * h:LANE * (h + 1)]
        dkro = dkpe * ck + pltpu.roll(dkpe * sa, 16, 1) + pltpu.roll(dkpe * sb, LANE - 16, 1)
        dz_ref[:, 0:384] = _mx(dzq)
        dz_ref[:, 384:640] = _mx(dzkv)
        dz_ref[:, 640:768] = _mx(dkro)
        dz_ref[:, 768:1280] = dg_ref[:, 0:512]
        dz_ref[:, 1280:1792] = _mx(dqs_ref[...])
        dz_ref[:, 1792:1920] = _mx(dks_ref[...])
        dz_ref[:, 1920:2048] = _mx(dvs_ref[...])
        dz_ref[:, 2048:2560] = dg_ref[:, 512:1024]
        dh = _dot_nt(dz_ref[...], w_ref[...])
        x = x_ref[...]
        r1 = lax.rsqrt(jnp.mean(x * x, axis=-1, keepdims=True) + EPS)
        xn = x * r1
        ng = ng_ref[...]
        sc1 = 1.0 + sc_ref[0]
        dsh_ref[0] += jnp.sum(dh, axis=0, keepdims=True)
        dsc_ref[0] += jnp.sum(dh * (xn * ng), axis=0, keepdims=True)
        dng_ref[...] += jnp.sum(dh * xn * sc1, axis=0, keepdims=True)
        dxn = dh * ng * sc1
        gx_ref[...] = dx2_ref[...] + r1 * (dxn - xn * jnp.mean(dxn * xn, axis=-1, keepdims=True))

    row = lambda w: pl.BlockSpec((ts, w), lambda i: (i, 0))
    full = lambda a: pl.BlockSpec(a.shape, lambda i: (0,) * a.ndim)
    per_b = pl.BlockSpec((1, 1, D), lambda i: (i // nsb, 0, 0))
    dense = pl.BlockSpec((ts // 8, LANE), lambda i: (i, 0))
    vec = lambda w: pl.BlockSpec((1, w), lambda i: (0, 0))
    return pl.pallas_call(
        body, name="pre_bwd", grid=(T // ts,),
        out_shape=[jax.ShapeDtypeStruct((T, D), F32), jax.ShapeDtypeStruct((T, D_IN_PAD), _MXU_DTYPE), jax.ShapeDtypeStruct((1, 384), F32),
                   jax.ShapeDtypeStruct((1, 256), F32), jax.ShapeDtypeStruct((1, D), F32),
                   jax.ShapeDtypeStruct(scale.shape, F32), jax.ShapeDtypeStruct(scale.shape, F32),
                   jax.ShapeDtypeStruct(w_uq.shape, F32), jax.ShapeDtypeStruct(w_uk.shape, F32),
                   jax.ShapeDtypeStruct(w_uv.shape, F32)],
        in_specs=[row(WQ), row(WQ), row(512), row(512), row(LANE), row(LANE), row(1024), row(384), row(256), row(384),
                  row(256), row(D), row(D), per_b, full(ng), full(w_in), full(gq), full(gkv), full(w_uq), full(w_uk), full(w_uv),
                  dense, dense],
        out_specs=[row(D), row(D_IN_PAD), vec(384), vec(256), vec(D), per_b, per_b, full(w_uq), full(w_uk), full(w_uv)],
        compiler_params=_cparams(("arbitrary",)),
    )(dq, dk, dv, dqs, dks, dvs, dg, zq, zkv, ql, kvl, x2, dx2, scale, ng, w_in, gq, gkv, w_uq, w_uk, w_uv, *rope)


def _tn_matmul(a, b, tn, tk, name):
    T, M = a.shape
    N = b.shape[1]

    def body(a_ref, b_ref, o_ref):
        @pl.when(pl.program_id(1) == 0)
        def _():
            o_ref[...] = jnp.zeros_like(o_ref)

        o_ref[...] += _dot_tn(a_ref[...], b_ref[...])

    return pl.pallas_call(
        body, name=name, grid=(N // tn, T // tk),
        out_shape=jax.ShapeDtypeStruct((M, N), F32),
        in_specs=[pl.BlockSpec((tk, M), lambda j, k: (k, 0)), pl.BlockSpec((tk, tn), lambda j, k: (k, j))],
        out_specs=pl.BlockSpec((M, tn), lambda j, k: (0, j)),
        compiler_params=_cparams(("arbitrary", "arbitrary")),
    )(a, b)


def _finalize(parts_all, dmod_all, dmod_cols, c_act_all):
    nparts = parts_all.shape[-1]
    nmod = dmod_all.shape[-1]

    def body(p_ref, dm_ref, dmc_ref, c_ref, ps_ref, loss_ref, db_ref, dw_ref):
        acc = p_ref[0]
        for j in range(1, N_DEV):
            acc = acc + p_ref[j]
        ps_ref[...] = acc
        loss_ref[...] = jnp.sum(acc[:, 0:LANE], axis=1, keepdims=True)
        db = dm_ref[0:1, :]
        for j in range(1, dm_ref.shape[0]):
            db = db + dm_ref[j:j + 1, :]
        db_ref[...] = db
        dw_ref[...] = _dot_tn(_mx(_silu(c_ref[...])), _mx(dmc_ref[...]))

    return pl.pallas_call(
        body, name="finalize",
        out_shape=[jax.ShapeDtypeStruct((1, nparts), F32), jax.ShapeDtypeStruct((1, 1), F32),
                   jax.ShapeDtypeStruct((1, nmod), F32),
                   jax.ShapeDtypeStruct((c_act_all.shape[1], dmod_cols.shape[1]), F32)],
        in_specs=[_vmem()] * 4, out_specs=[_vmem()] * 4,
        compiler_params=_cparams(),
    )(parts_all, dmod_all, dmod_cols, c_act_all)


def _adamw(ws, gs, ms, vs):
    n = len(ws)

    def body(*refs):
        w_refs, g_refs, m_refs, v_refs, d_refs, nm_refs, nv_refs = (refs[t * n:(t + 1) * n] for t in range(7))
        for t in range(n):
            gv = g_refs[t][...]
            nm = ADAM_B1 * m_refs[t][...] + (1.0 - ADAM_B1) * gv
            nv = ADAM_B2 * v_refs[t][...] + (1.0 - ADAM_B2) * (gv * gv)
            m_hat = nm / (1.0 - ADAM_B1 ** ADAM_STEP)
            v_hat = nv / (1.0 - ADAM_B2 ** ADAM_STEP)
            d_refs[t][...] = -ADAM_LR * (m_hat / (jnp.sqrt(v_hat) + ADAM_EPS) + ADAM_WD * w_refs[t][...])
            nm_refs[t][...] = nm
            nv_refs[t][...] = nv

    out = pl.pallas_call(
        body, name="adamw",
        out_shape=[jax.ShapeDtypeStruct(w.shape, F32) for w in ws] * 3,
        in_specs=[_vmem()] * (4 * n), out_specs=[_vmem()] * (3 * n),
        compiler_params=_cparams(),
    )(*ws, *gs, *ms, *vs)
    return out[:n], out[n:2 * n], out[2 * n:]


def _pair_perm(a, axis, order):
    a = jnp.moveaxis(a, axis, -1)
    lead = a.shape[:-1]
    a = a.reshape(lead + (8, 64))[..., list(order), :].reshape(lead + (512,))
    return jnp.moveaxis(a, -1, axis)


def _pad_w_in(w):
    z = lambda n: jnp.zeros((w.shape[0], n), w.dtype)
    return jnp.concatenate([
        w[:, 0:640], z(64), w[:, 640:672], z(32), w[:, 672:1184],
        _pair_perm(w[:, 1184:1696], 1, PAIR_ORDER), w[:, 1696:1952], _pair_perm(w[:, 1952:2464], 1, PAIR_ORDER)], axis=1)


def _unpad_w_in_t(g):
    return jnp.concatenate([
        g[0:640], g[704:736], g[768:1280],
        _pair_perm(g[1280:1792], 0, PAIR_INV), g[1792:2048], _pair_perm(g[2048:2560], 0, PAIR_INV)], axis=0)


def _rope_table(positions):
    T = positions.size
    inv = ROPE_THETA ** (-jnp.arange(0, MLA_ROPE, 2, dtype=F32) / MLA_ROPE)
    pos = jnp.repeat(positions.reshape(T // 8, 8)[:, ::-1].astype(F32), MLA_ROPE // 2, axis=1)
    ang = pos * jnp.tile(inv, 8)[None, :]
    return jnp.cos(ang), jnp.sin(ang)


def _local_step(x, mod, positions, ng, w_in, gq, gkv, w_uq, w_ukv, sinks, w_out, fg, tgt,
                ts=512, fq=512, fk=512, bq=512, bk=512):
    B, S, D = x.shape
    T = B * S
    x2 = x.reshape(T, D)
    shift, scale, gate = (mod[:, None, k * D:(k + 1) * D] for k in range(3))
    w_in_p = _pad_w_in(w_in)
    w_uq_p = jnp.pad(w_uq.reshape(Q_LORA, MLA_HEADS, 96), ((0, 0), (0, 0), (0, 32))).reshape(Q_LORA, MLA_HEADS * LANE)
    w_ukv3 = w_ukv.reshape(KV_LORA, MLA_HEADS, 128)
    w_uk_p = jnp.pad(w_ukv3[:, :, :64], ((0, 0), (0, 0), (0, 64))).reshape(KV_LORA, MLA_HEADS * LANE)
    w_uv = w_ukv3[:, :, 64:].reshape(KV_LORA, 512)
    w_out_p = jnp.concatenate([w_out[:512], _pair_perm(w_out[512:], 0, PAIR_ORDER)], axis=0)
    rope = _rope_table(positions)
    posf = positions.astype(F32)
    posb = jnp.broadcast_to(posf[:, :, None], (B, S, LANE))
    posr = posf.reshape(B, S // WINDOW, WINDOW)
    sinks_l = jnp.pad(sinks.reshape(1, SWA_HEADS), ((0, 0), (0, LANE - SWA_HEADS)))

    (hb, zq, zkv, ql, kvl, q, k, v, qs, ks, vs, g) = _pre_fwd(
        x2, shift, scale, ng, w_in_p, gq, gkv, w_uq_p, w_uk_p, w_uv, rope, S, ts)
    r3 = lambda a: a.reshape(B, S, a.shape[-1])
    om, lse_m = _mla_fwd(r3(q), r3(k), r3(v), fq, fk)
    osw, lse_s = _swa_fwd(r3(qs), r3(ks), r3(vs), posb, posr, sinks_l)
    dx2, do, dg, loss_v, dfg, dgate, dw_out_p = _post(
        om.reshape(T, 512), osw.reshape(T, 512), g, w_out_p, x2, gate, fg.reshape(1, D), tgt.reshape(T, D), S, ts)
    do3 = r3(do)
    dq, dk, dv = _mla_bwd(r3(q), r3(k), r3(v), om, do3, lse_m, bq, bk)
    dqs, dks, dvs, dsink = _swa_bwd(r3(qs), r3(ks), r3(vs), posb, posr, sinks_l, osw, do3, lse_s)
    f2 = lambda a: a.reshape(T, a.shape[-1])
    gx, dz, dgq, dgkv, dng, dsh, dsc, dw_uq_p, dw_uk_p, dw_uv = _pre_bwd(
        f2(dq), f2(dk), f2(dv), f2(dqs), f2(dks), f2(dvs), dg, zq, zkv, ql, kvl, x2, dx2, scale, ng, w_in_p, gq, gkv,
        w_uq_p, w_uk_p, w_uv, rope, S, ts // 2)
    tk = min(T, 1024)
    dw_in_t = _unpad_w_in_t(_tn_matmul(dz, hb, 512, tk, "dw_in"))
    dw_out = jnp.concatenate([dw_out_p[:512], _pair_perm(dw_out_p[512:], 0, PAIR_INV)], axis=0)
    dw_uq = dw_uq_p.reshape(Q_LORA, MLA_HEADS, LANE)[:, :, :96].reshape(Q_LORA, 768)
    dw_uk = dw_uk_p.reshape(KV_LORA, MLA_HEADS, LANE)[:, :, :64]
    dw_uv = dw_uv.reshape(KV_LORA, MLA_HEADS, 64)
    dw_ukv = jnp.concatenate([dw_uk, dw_uv], axis=2).reshape(KV_LORA, 1024)
    parts = jnp.concatenate([loss_v, dfg, dng, dgq, dgkv, dsink], axis=1)
    dmod = jnp.concatenate([dsh, dsc, dgate], axis=2).reshape(B, 3 * D)
    return gx.reshape(B, S, D), dw_in_t, dw_uq, dw_ukv, dw_out, parts, dmod


def kernel(x, c, positions, w_ada, b_ada, norm_gain, w_in, q_norm_gain, kv_norm_gain, w_uq, w_ukv, swa_sinks, w_out, final_gain, loss_target, m_w_ada, m_b_ada, m_norm_gain, m_w_in, m_q_norm_gain, m_kv_norm_gain, m_w_uq, m_w_ukv, m_swa_sinks, m_w_out, m_final_gain, v_w_ada, v_b_ada, v_norm_gain, v_w_in, v_q_norm_gain, v_kv_norm_gain, v_w_uq, v_w_ukv, v_swa_sinks, v_w_out, v_final_gain):
    B, S, D = x.shape
    me = 4 * lax.axis_index("x") + 2 * lax.axis_index("y") + lax.axis_index("c")
    bf = _MXU_DTYPE

    c_all, win_g, wuq_g, wukv_g, wout_g = _all_gather(
        [c, w_in[0].astype(bf), w_uq[0].astype(bf), w_ukv[0].astype(bf), w_out[0].astype(bf)], "ag_weights")
    c_all = c_all.reshape(N_DEV * B, D)
    cat_cols = lambda a: jnp.transpose(a, (1, 0, 2)).reshape(a.shape[1], N_DEV * a.shape[2])
    w_in_f, w_uq_f, w_ukv_f = cat_cols(win_g), cat_cols(wuq_g), cat_cols(wukv_g)
    w_out_f = wout_g.reshape(D, D)

    ncol = w_ada.shape[2]
    b_cols = lax.dynamic_slice_in_dim(b_ada, me * ncol, ncol, axis=1)
    (mod_g,) = _all_gather([_ada_fwd(c_all, w_ada[0], b_cols)], "ag_mod")
    mod = lax.dynamic_slice_in_dim(mod_g, me * B, B, axis=1)
    mod = jnp.transpose(mod, (1, 0, 2)).reshape(B, 3 * D)

    gx, dw_in_t, dw_uq, dw_ukv, dw_out, parts, dmod = _local_step(
        x, mod, positions, norm_gain, w_in_f, q_norm_gain, kv_norm_gain, w_uq_f, w_ukv_f, swa_sinks,
        w_out_f, final_gain, loss_target)

    parts_g, dmod_g = _all_gather([parts, dmod], "ag_small")
    dmod_all = dmod_g.reshape(N_DEV * B, 3 * D)
    dmod_cols = lax.dynamic_slice_in_dim(dmod_all, me * ncol, ncol, axis=1)
    psum, loss, g_b_ada, g_w_ada = _finalize(parts_g, dmod_all, dmod_cols, c_all)
    loss = loss.reshape(())
    o = LANE
    g_final_gain = psum[0, o:o + D]
    g_norm_gain = psum[:, o + D:o + 2 * D]
    o += 2 * D
    g_q_norm_gain = psum[:, o:o + Q_LORA]
    g_kv_norm_gain = psum[:, o + Q_LORA:o + Q_LORA + KV_LORA]
    o += Q_LORA + KV_LORA
    g_sinks = psum[:, o:o + SWA_HEADS]

    split_cols = lambda a: jnp.transpose(a.reshape(a.shape[0], 4, 2, a.shape[1] // N_DEV), (1, 2, 0, 3))
    split_rows = lambda a: a.reshape(4, 2, a.shape[0] // N_DEV, a.shape[1])
    g_w_in_t, g_w_uq, g_w_ukv, g_w_out = _reduce_scatter(
        [split_rows(dw_in_t), split_cols(dw_uq), split_cols(dw_ukv), split_rows(dw_out)], "rs_grads")
    g_w_in = g_w_in_t.T

    grads = [g_w_ada, g_b_ada, g_norm_gain, g_w_in, g_q_norm_gain, g_kv_norm_gain, g_w_uq, g_w_ukv, g_sinks, g_w_out,
             g_final_gain]
    ws = [w_ada, b_ada, norm_gain, w_in, q_norm_gain, kv_norm_gain, w_uq, w_ukv, swa_sinks, w_out, final_gain]
    ms = [m_w_ada, m_b_ada, m_norm_gain, m_w_in, m_q_norm_gain, m_kv_norm_gain, m_w_uq, m_w_ukv, m_swa_sinks, m_w_out,
          m_final_gain]
    vs = [v_w_ada, v_b_ada, v_norm_gain, v_w_in, v_q_norm_gain, v_kv_norm_gain, v_w_uq, v_w_ukv, v_swa_sinks, v_w_out,
          v_final_gain]
    two_d = [(1, w.shape[0]) if w.ndim == 1 else w.shape[-2:] for w in ws]
    flat = lambda arrs: [a.reshape(s) for a, s in zip(arrs, two_d)]
    deltas, new_ms, new_vs = _adamw(flat(ws), flat(grads), flat(ms), flat(vs))
    shaped = lambda arrs: [a.reshape(w.shape) for a, w in zip(arrs, ws)]
    return (loss, gx, *shaped(grads), *shaped(deltas), *shaped(new_ms), *shaped(new_vs))
```

```python
import functools

import numpy as np
import jax
import jax.numpy as jnp
from jax import lax
from jax.experimental import pallas as pl
from jax.experimental.pallas import tpu as pltpu

F32 = jnp.float32
_MXU_DTYPE = jnp.bfloat16

N_DEV = 8
D_MODEL = 1024
MLA_HEADS = 8
MLA_NOPE = 64
MLA_ROPE = 32
MLA_V = 64
Q_LORA = 384
KV_LORA = 256
SWA_HEADS = 8
SWA_KV_HEADS = 2
SWA_HEAD_DIM = 64
WINDOW = 128
ROPE_THETA = 10000.0
EPS = 1e-6
MLA_SCALE = float((MLA_NOPE + MLA_ROPE) ** -0.5)
SWA_SCALE = float(SWA_HEAD_DIM ** -0.5)
LOG2E = 1.4426950408889634
MLA_QSCALE = MLA_SCALE * LOG2E
D_IN = 2464
D_IN_PAD = 2560
PAIR_ORDER = (0, 4, 1, 5, 2, 6, 3, 7)
PAIR_INV = (0, 2, 4, 6, 1, 3, 5, 7)

ADAM_LR = 0.001
ADAM_B1 = 0.9
ADAM_B2 = 0.999
ADAM_EPS = 1e-08
ADAM_WD = 0.01
ADAM_STEP = 10

LANE = 128
VMEM_LIMIT = 56 * 1024 * 1024

MESH = pl.DeviceIdType.MESH
NEG_INF = float("-inf")
SWA_SEQ_SPLIT = 2


def _mx(a):
    return a.astype(_MXU_DTYPE)


def _dot(a, b):
    return jnp.dot(a, b, preferred_element_type=F32)


def _dot_nt(a, b):
    return lax.dot_general(a, b, (((1,), (1,)), ((), ())), preferred_element_type=F32)


def _dot_tn(a, b):
    return lax.dot_general(a, b, (((0,), (0,)), ((), ())), preferred_element_type=F32)


def _cparams(sem=None):
    return pltpu.CompilerParams(dimension_semantics=sem, vmem_limit_bytes=VMEM_LIMIT)


def _vmem():
    return pl.BlockSpec(memory_space=pltpu.VMEM)


def _lane_iota(shape):
    return lax.broadcasted_iota(jnp.int32, shape, len(shape) - 1)


def _all_gather(arrs, name):
    n = len(arrs)

    def body(*refs):
        ins, outs = refs[:n], refs[n:2 * n]
        send_sems, recv_sems, local_sems = refs[2 * n:]
        x, y, c = lax.axis_index("x"), lax.axis_index("y"), lax.axis_index("c")
        me, sibling = (x, y, c), (x, y, 1 - c)
        chips = [(1 - x, y), (x, 1 - y), (1 - x, 1 - y)]

        def slot(a, dev):
            return outs[a].at[4 * dev[0] + 2 * dev[1] + dev[2]]

        def copy(a, k, block, to, src=None):
            return pltpu.make_async_remote_copy(
                src_ref=slot(a, block) if src is None else src, dst_ref=slot(a, block),
                send_sem=send_sems.at[7 * a + k], recv_sem=recv_sems.at[7 * a + k],
                device_id=to, device_id_type=MESH)

        mine = [pltpu.make_async_copy(ins[a], slot(a, me), local_sems.at[a]) for a in range(n)]
        for cp in mine:
            cp.start()
        first = []
        for a in range(n):
            first.append(copy(a, 0, me, sibling, src=ins[a]))
            first += [copy(a, 1 + j, me, (*chip, c), src=ins[a]) for j, chip in enumerate(chips)]
        for cp in first:
            cp.start()
        passed = []
        for j, chip in enumerate(chips):
            for a in range(n):
                copy(a, 1 + j, (*chip, c), me).wait_recv()
                cp = copy(a, 4 + j, (*chip, c), sibling)
                cp.start()
                passed.append(cp)
        for a in range(n):
            copy(a, 0, sibling, me).wait_recv()
            for j, chip in enumerate(chips):
                copy(a, 4 + j, (*chip, 1 - c), me).wait_recv()
        for cp in first + passed:
            cp.wait_send()
        for cp in mine:
            cp.wait()

    return pl.pallas_call(
        body, name=name,
        out_shape=[jax.ShapeDtypeStruct((N_DEV,) + a.shape, a.dtype) for a in arrs],
        in_specs=[_vmem()] * n, out_specs=[_vmem()] * n,
        scratch_shapes=[pltpu.SemaphoreType.DMA((7 * n,)), pltpu.SemaphoreType.DMA((7 * n,)),
                        pltpu.SemaphoreType.DMA((n,))],
        compiler_params=pltpu.CompilerParams(vmem_limit_bytes=VMEM_LIMIT),
    )(*arrs)


def _reduce_scatter(arrs, name):
    n = len(arrs)
    halves = [a.astype(jnp.bfloat16) for a in arrs]

    def body(*refs):
        xs, xbs, outs = refs[:n], refs[n:2 * n], refs[2 * n:3 * n]
        parts, recv_a, send_b, recv_b = (refs[(3 + t) * n:(4 + t) * n] for t in range(4))
        send_sems, recv_sems, local_sems = refs[7 * n:]
        x, y, c = lax.axis_index("x"), lax.axis_index("y"), lax.axis_index("c")
        myq = 2 * x + y

        def chip(k):
            return (1 - x if k & 2 else x, 1 - y if k & 1 else y)

        def from_sibling(a):
            return pltpu.make_async_remote_copy(
                src_ref=xbs[a].at[:, 1 - c], dst_ref=recv_a[a], send_sem=send_sems.at[4 * a], recv_sem=recv_sems.at[4 * a],
                device_id=(x, y, 1 - c), device_id_type=MESH)

        def to_owner(a, k):
            qx, qy = chip(k)
            return pltpu.make_async_remote_copy(
                src_ref=send_b[a].at[2 * qx + qy], dst_ref=recv_b[a].at[myq],
                send_sem=send_sems.at[4 * a + k], recv_sem=recv_sems.at[4 * a + k],
                device_id=(qx, qy, c), device_id_type=MESH)

        mine = [pltpu.make_async_copy(xs[a].at[:, c], parts[a], local_sems.at[a]) for a in range(n)]
        first = [from_sibling(a) for a in range(n)]
        for cp in mine + first:
            cp.start()
        second = []
        for a in range(n):
            mine[a].wait()
            first[a].wait_recv()
            parts[a][...] = parts[a][...] + recv_a[a][...].astype(F32)
            send_b[a][...] = parts[a][...].astype(jnp.bfloat16)
            for k in range(1, 4):
                cp = to_owner(a, k)
                cp.start()
                second.append(cp)
        for a in range(n):
            acc = parts[a][myq]
            for k in range(1, 4):
                to_owner(a, k).wait_recv()
                qx, qy = chip(k)
                acc = acc + recv_b[a][2 * qx + qy].astype(F32)
            outs[a][...] = acc
        for cp in first + second:
            cp.wait_send()

    quarter = lambda a, dt: pltpu.VMEM((4,) + a.shape[2:], dt)
    return pl.pallas_call(
        body, name=name,
        out_shape=[jax.ShapeDtypeStruct(a.shape[2:], F32) for a in arrs],
        in_specs=[pl.BlockSpec(memory_space=pl.ANY)] * (2 * n), out_specs=[_vmem()] * n,
        scratch_shapes=[quarter(a, F32) for a in arrs] + [quarter(a, jnp.bfloat16) for a in arrs] * 3
        + [pltpu.SemaphoreType.DMA((4 * n,)), pltpu.SemaphoreType.DMA((4 * n,)), pltpu.SemaphoreType.DMA((n,))],
        compiler_params=pltpu.CompilerParams(vmem_limit_bytes=VMEM_LIMIT),
    )(*arrs, *halves)


def _silu(t):
    return t * (1.0 / (1.0 + jnp.exp(-t)))


def _ada_fwd(c_act_all, w_ada_shard, b_shard):
    def body(c_ref, w_ref, b_ref, o_ref):
        o_ref[...] = _dot(_mx(_silu(c_ref[...])), _mx(w_ref[...])) + b_ref[...]

    return pl.pallas_call(
        body, name="ada_fwd",
        out_shape=jax.ShapeDtypeStruct((c_act_all.shape[0], w_ada_shard.shape[1]), F32),
        in_specs=[_vmem()] * 3, out_specs=_vmem(),
        compiler_params=_cparams(),
    )(c_act_all, w_ada_shard, b_shard)


def _rope_split(cd, sd):
    n = cd.shape[0]

    def expand(d):
        rep = jnp.broadcast_to(d[:, None, :], (n, 8, LANE)).reshape(8 * n, LANE)
        return pltpu.roll(rep, 0, 1, stride=16, stride_axis=0)

    c, s = expand(cd), expand(sd)
    lane = _lane_iota(c.shape)
    first = jnp.logical_and(lane >= 64, lane < 80)
    second = jnp.logical_and(lane >= 80, lane < 96)
    ck = jnp.where(first, pltpu.roll(c, 80, 1), jnp.where(second, pltpu.roll(c, 96, 1), 0.0))
    cq = jnp.where(lane < 64, 1.0, ck)
    sa = jnp.where(first, -pltpu.roll(s, 80, 1), 0.0)
    sb = jnp.where(second, pltpu.roll(s, 96, 1), 0.0)
    return cq, ck, sa, sb


def _tile_heads(t):
    return jnp.concatenate([t] * MLA_HEADS, axis=1)


def _pre_fwd(x2, shift, scale, ng, w_in, gq, gkv, w_uq, w_uk, w_uv, rope, S, ts):
    T, D = x2.shape
    nsb = S // ts
    WQ = MLA_HEADS * LANE

    def body(x_ref, sh_ref, sc_ref, ng_ref, w_ref, gq_ref, gkv_ref, wuq_ref, wuk_ref, wuv_ref,
             cd_ref, sd_ref,
             hb_ref, zq_ref, zkv_ref, ql_ref, kvl_ref, q_ref, k_ref, v_ref, qs_ref, ks_ref, vs_ref, g_ref):
        x = x_ref[...]
        r1 = lax.rsqrt(jnp.mean(x * x, axis=-1, keepdims=True) + EPS)
        h = ((x * r1) * ng_ref[...]) * (1.0 + sc_ref[0]) + sh_ref[0]
        hb = _mx(h)
        hb_ref[...] = hb
        zq = _dot(hb, w_ref[:, 0:384])
        zq_ref[...] = zq
        rq = lax.rsqrt(jnp.mean(zq * zq, axis=-1, keepdims=True) + EPS)
        ql = _mx((zq * rq) * gq_ref[...])
        ql_ref[...] = ql
        q = _dot(ql, wuq_ref[...])
        cq, ck, sa, sb = _rope_split(cd_ref[...], sd_ref[...])
        q = (q * _tile_heads(cq) + pltpu.roll(q, WQ - 16, 1) * _tile_heads(sa)
             + pltpu.roll(q, 16, 1) * _tile_heads(sb))
        q_ref[...] = _mx(q * MLA_QSCALE)
        zkv = _dot(hb, w_ref[:, 384:640])
        zkv_ref[...] = zkv
        rkv = lax.rsqrt(jnp.mean(zkv * zkv, axis=-1, keepdims=True) + EPS)
        kvl = _mx((zkv * rkv) * gkv_ref[...])
        kvl_ref[...] = kvl
        kr = _dot(hb, w_ref[:, 640:768])
        kpe = kr * ck + pltpu.roll(kr, LANE - 16, 1) * sa + pltpu.roll(kr, 16, 1) * sb
        kf = _dot(kvl, wuk_ref[...])
        k_ref[...] = _mx(kf + jnp.concatenate([kpe] * MLA_HEADS, axis=1))
        v_ref[...] = _mx(_dot(kvl, wuv_ref[...]))
        g_ref[:, 0:512] = _dot(hb, w_ref[:, 768:1280])
        qs_ref[...] = _mx(_dot(hb, w_ref[:, 1280:1792]) * (SWA_SCALE * LOG2E))
        ks_ref[...] = _mx(_dot(hb, w_ref[:, 1792:1920]))
        vs_ref[...] = _mx(_dot(hb, w_ref[:, 1920:2048]))
        g_ref[:, 512:1024] = _dot(hb, w_ref[:, 2048:2560])

    row = lambda w: pl.BlockSpec((ts, w), lambda i: (i, 0))
    dense = pl.BlockSpec((ts // 8, LANE), lambda i: (i, 0))
    full = lambda a: pl.BlockSpec(a.shape, lambda i: (0,) * a.ndim)
    per_b = pl.BlockSpec((1, 1, D), lambda i: (i // nsb, 0, 0))
    out_w = [(D, _MXU_DTYPE), (384, F32), (256, F32), (384, _MXU_DTYPE), (256, _MXU_DTYPE), (WQ, _MXU_DTYPE),
             (WQ, _MXU_DTYPE), (512, _MXU_DTYPE), (512, _MXU_DTYPE), (128, _MXU_DTYPE), (128, _MXU_DTYPE), (1024, F32)]
    return pl.pallas_call(
        body, name="pre_fwd", grid=(T // ts,),
        out_shape=[jax.ShapeDtypeStruct((T, w), dt) for w, dt in out_w],
        in_specs=[row(D), per_b, per_b, full(ng), full(w_in), full(gq), full(gkv), full(w_uq), full(w_uk), full(w_uv),
                  dense, dense],
        out_specs=[row(w) for w, _ in out_w],
        compiler_params=_cparams(("arbitrary",)),
    )(x2, shift, scale, ng, w_in, gq, gkv, w_uq, w_uk, w_uv, *rope)


def _mla_fwd(q3, k3, v3, tq, tk):
    B, S, _ = q3.shape
    nq = S // tq
    assert tk % tq == 0 or tq % tk == 0
    n_masked = max(1, tq // tk)

    def body(q_ref, k_ref, v_ref, o_ref, lse_ref):
        qi = pl.program_id(2)
        rows = lax.broadcasted_iota(jnp.int32, (tq, tk), 0)
        cols = lax.broadcasted_iota(jnp.int32, (tq, tk), 1)
        qs = [q_ref[0, :, LANE * j:LANE * (j + 1)] for j in range(2)]
        n_full = (qi * tq) // tk

        def step(kt, carry, masked):
            r0 = pl.multiple_of(kt * tk, tk)
            v2 = v_ref[0, pl.ds(r0, tk), :]
            out = []
            for j in range(2):
                m, l, acc = carry[j]
                s = _dot_nt(qs[j], k_ref[0, pl.ds(r0, tk), LANE * j:LANE * (j + 1)])
                if masked:
                    s = jnp.where(rows + qi * tq >= cols + kt * tk, s, NEG_INF)
                m_new = jnp.maximum(m, jnp.max(s, axis=1, keepdims=True))
                alpha = jnp.exp2(m - m_new)
                p = jnp.exp2(s - m_new)
                l = alpha * l + jnp.sum(p, axis=1, keepdims=True)
                acc = alpha * acc + _dot(_mx(p), v2)
                out.append((m_new, l, acc))
            return tuple(out)

        init = (jnp.full((tq, 1), NEG_INF, F32), jnp.zeros((tq, 1), F32), jnp.zeros((tq, LANE), F32))
        carry = lax.fori_loop(0, n_full, functools.partial(step, masked=False), (init, init))
        for t in range(n_masked):
            carry = step(n_full + t, carry, True)
        (m0, l0, a0), (m1, l1, a1) = carry
        o_ref[0] = jnp.where(_lane_iota((tq, LANE)) < 64, a0 / l0, a1 / l1)
        lse_ref[0, 0] = jnp.broadcast_to(m0 + jnp.log2(l0), (tq, LANE))
        lse_ref[0, 1] = jnp.broadcast_to(m1 + jnp.log2(l1), (tq, LANE))

    return pl.pallas_call(
        body, name="mla_fwd", grid=(B, MLA_HEADS // 2, nq),
        out_shape=[jax.ShapeDtypeStruct((B, S, 512), F32), jax.ShapeDtypeStruct((B, MLA_HEADS, S, LANE), F32)],
        in_specs=[pl.BlockSpec((1, tq, 2 * LANE), lambda b, hp, i: (b, i, hp)),
                  pl.BlockSpec((1, S, 2 * LANE), lambda b, hp, i: (b, 0, hp)),
                  pl.BlockSpec((1, S, LANE), lambda b, hp, i: (b, 0, hp))],
        out_specs=[pl.BlockSpec((1, tq, LANE), lambda b, hp, i: (b, i, hp)),
                   pl.BlockSpec((1, 2, tq, LANE), lambda b, hp, i: (b, hp, i, 0))],
        compiler_params=_cparams(("arbitrary", "arbitrary", "arbitrary")),
    )(q3, k3, v3)


def _mla_bwd(q3, k3, v3, o3, do3, lse, tq, tk):
    B, S, _ = q3.shape
    nq, nk = S // tq, S // tk
    assert tk % tq == 0

    def body(q_ref, k_ref, v_ref, o_ref, do_ref, lse_ref, dq_ref, dk_ref, dv_ref, dkt_ref, dvt_ref):
        dkt_ref[...] = jnp.zeros_like(dkt_ref)
        dvt_ref[...] = jnp.zeros_like(dvt_ref)
        rows = lax.broadcasted_iota(jnp.int32, (tq, tk), 0)
        cols = lax.broadcasted_iota(jnp.int32, (tq, tk), 1)
        lane = _lane_iota((tq, LANE))

        def q_tile(qi, _):
            r = pl.ds(pl.multiple_of(qi * tq, tq), tq)
            do2, o2 = do_ref[0, r, :], o_ref[0, r, :]
            heads = []
            for j in range(2):
                lanes = slice(LANE * j, LANE * (j + 1))
                do = jnp.where((lane < 64) if j == 0 else (lane >= 64), do2, 0.0)
                q = q_ref[0, r, lanes]
                heads.append((lanes, q, _mx(q.astype(F32).T), _mx(do), _mx(do.T),
                              jnp.sum(do * o2, axis=1, keepdims=True),
                              jnp.concatenate([lse_ref[0, j, r, :]] * (tk // LANE), axis=1)))
            n_full = (qi * tq) // tk

            def k_tile(kt, dqs, masked):
                kr = pl.ds(pl.multiple_of(kt * tk, tk), tk)
                v2 = v_ref[0, kr, :]
                out = []
                dvt = None
                for j, (lanes, q, qt, dob, dot_, dcol, lse_c) in enumerate(heads):
                    k = k_ref[0, kr, lanes]
                    s = _dot_nt(q, k)
                    if masked:
                        s = jnp.where(rows + qi * tq >= cols + kt * tk, s, NEG_INF)
                    p = jnp.exp2(s - lse_c)
                    dp = _dot_nt(dob, v2)
                    dsb = _mx(p * (dp - dcol))
                    dkt_ref[j, kt] += _dot(qt, dsb)
                    pv = _dot(dot_, _mx(p))
                    dvt = pv if dvt is None else dvt + pv
                    out.append(dqs[j] + _dot(dsb, k))
                dvt_ref[kt] += dvt
                return tuple(out)

            zero = jnp.zeros((tq, LANE), F32)
            dqs = lax.fori_loop(0, n_full, functools.partial(k_tile, masked=False), (zero, zero))
            dqs = k_tile(n_full, dqs, True)
            for j in range(2):
                dq_ref[0, r, heads[j][0]] = MLA_SCALE * dqs[j]
            return 0

        lax.fori_loop(0, nq, q_tile, 0)

        def flush(kt, _):
            kr = pl.ds(pl.multiple_of(kt * tk, tk), tk)
            for j in range(2):
                dk_ref[0, kr, LANE * j:LANE * (j + 1)] = (1.0 / LOG2E) * dkt_ref[j, kt].T
            dv_ref[0, kr, :] = dvt_ref[kt].T
            return 0

        lax.fori_loop(0, nk, flush, 0)

    pair = lambda w: pl.BlockSpec((1, S, w), lambda b, hp: (b, 0, hp))
    return pl.pallas_call(
        body, name="mla_bwd", grid=(B, MLA_HEADS // 2),
        out_shape=[jax.ShapeDtypeStruct((B, S, 1024), F32), jax.ShapeDtypeStruct((B, S, 1024), F32),
                   jax.ShapeDtypeStruct((B, S, 512), F32)],
        in_specs=[pair(2 * LANE), pair(2 * LANE), pair(LANE), pair(LANE), pair(LANE),
                  pl.BlockSpec((1, 2, S, LANE), lambda b, hp: (b, hp, 0, 0))],
        out_specs=[pair(2 * LANE), pair(2 * LANE), pair(LANE)],
        scratch_shapes=[pltpu.VMEM((2, nk, LANE, tk), F32), pltpu.VMEM((nk, LANE, tk), F32)],
        compiler_params=_cparams(("arbitrary", "arbitrary")),
    )(q3, k3, v3, o3, do3, lse)


def _swa_consts(sink_ref):
    W = WINDOW
    row = lax.broadcasted_iota(jnp.int32, (4 * W, LANE), 0)
    out = []
    for g in range(SWA_KV_HEADS):
        slope = jnp.zeros((4 * W, LANE), F32)
        sink = jnp.zeros((4 * W, LANE), F32)
        for p in range(4):
            h = p + 4 * g
            here = jnp.logical_and(row >= W * p, row < W * (p + 1))
            slope = jnp.where(here, float(LOG2E * 2.0 ** (-8.0 * (h + 1) / SWA_HEADS)), slope)
            sink = jnp.where(here, LOG2E * sink_ref[:, h:h + 1], sink)
        out.append((jnp.concatenate([slope, slope], axis=1), sink))
    return out


def _wide(col):
    return jnp.concatenate([col, col], axis=1)


def _swa_block(q_ref, k_ref, v_ref, pb_ref, pr_ref, n, i, ij):
    W = WINDOW
    kb = jnp.maximum(n - 1, 0)
    r = pl.ds(pl.multiple_of(i * W, W), W)
    kr = pl.ds(pl.multiple_of(kb * W, W), 2 * W)
    q4, k2, v2 = q_ref[0, r, :], k_ref[0, kr, :], v_ref[0, kr, :]
    pq = _wide(pb_ref[0, r, :])
    pk = jnp.concatenate([pr_ref[0, pl.ds(kb, 1), :], pr_ref[0, pl.ds(kb + 1, 1), :]], axis=1)
    rel = ij + (n - kb) * W
    mask = jnp.where(jnp.logical_and(rel >= 0, rel < W), 0.0, NEG_INF)
    dist4 = jnp.concatenate([pq - pk] * 4, axis=0)
    mask4 = jnp.concatenate([mask] * 4, axis=0)
    return r, kb, kr, q4, k2, v2, dist4, mask4


def _swa_stack(x4, g, dtype):
    lane = _lane_iota((WINDOW, LANE))
    mine = (lane < 64) if g == 0 else (lane >= 64)
    return jnp.concatenate([jnp.where(mine, x4[:, LANE * p:LANE * (p + 1)], 0).astype(dtype) for p in range(4)], axis=0)


def _swa_unstack(ref, r, lo, hi, scale=None):
    W = WINDOW
    low = _lane_iota((W, LANE)) < 64
    for p in range(4):
        t = jnp.where(low, lo[W * p:W * (p + 1)], hi[W * p:W * (p + 1)])
        ref[0, r, LANE * p:LANE * (p + 1)] = t if scale is None else scale * t


def _swa_fwd(qs3, ks3, vs3, posb, posr, sinks):
    B, S, _ = qs3.shape
    W = WINDOW
    nb = S // W
    nh = SWA_SEQ_SPLIT
    nbh = nb // nh

    def body(q_ref, k_ref, v_ref, pb_ref, pr_ref, sink_ref, o_ref, lse_ref):
        ij = lax.broadcasted_iota(jnp.int32, (W, 2 * W), 0) - lax.broadcasted_iota(jnp.int32, (W, 2 * W), 1)
        consts = _swa_consts(sink_ref)
        n0 = pl.program_id(1) * nbh

        def blk(i, _):
            n = n0 + i
            r, _, _, q4, k2, v2, dist4, mask4 = _swa_block(q_ref, k_ref, v_ref, pb_ref, pr_ref, n, i, ij)
            o_g = []
            for g, (slope, sink) in enumerate(consts):
                s = _dot_nt(_swa_stack(q4, g, q4.dtype), k2) - slope * dist4 + mask4
                m = jnp.maximum(jnp.max(s, axis=1, keepdims=True), sink)
                e = jnp.exp2(s - _wide(m))
                l = jnp.sum(e, axis=1, keepdims=True) + jnp.exp2(sink - m)
                o_g.append(_dot(_mx(e), v2) * (1.0 / l))
                lse_ref[0, i, g] = m + jnp.log2(l)
            _swa_unstack(o_ref, r, o_g[0], o_g[1])
            return 0

        lax.fori_loop(0, nbh, blk, 0)

    seq = lambda w: pl.BlockSpec((1, S, w), lambda b, h: (b, 0, 0))
    part = lambda w: pl.BlockSpec((1, S // nh, w), lambda b, h: (b, h, 0))
    lse_spec = pl.BlockSpec((1, nbh, 2, 4 * W, LANE), lambda b, h: (b, h, 0, 0, 0))
    return pl.pallas_call(
        body, name="swa_fwd", grid=(B, nh),
        out_shape=[jax.ShapeDtypeStruct((B, S, 512), F32), jax.ShapeDtypeStruct((B, nb, 2, 4 * W, LANE), F32)],
        in_specs=[part(512), seq(LANE), seq(LANE), part(LANE), pl.BlockSpec((1, nb, W), lambda b, h: (b, 0, 0)),
                  pl.BlockSpec((1, LANE), lambda b, h: (0, 0))],
        out_specs=[part(512), lse_spec],
        compiler_params=_cparams(("arbitrary", "arbitrary")),
    )(qs3, ks3, vs3, posb, posr, sinks)


def _swa_bwd(qs3, ks3, vs3, posb, posr, sinks, os3, do3, lse):
    B, S, _ = qs3.shape
    W = WINDOW
    nb = S // W
    nh = SWA_SEQ_SPLIT
    nbh = nb // nh

    def body(q_ref, k_ref, v_ref, pb_ref, pr_ref, sink_ref, o_ref, do_ref, lse_ref, dq_ref, dk_ref, dv_ref, dsink_ref,
             dkt_ref, dvt_ref):
        lane1 = _lane_iota((1, LANE))
        ij = lax.broadcasted_iota(jnp.int32, (W, 2 * W), 0) - lax.broadcasted_iota(jnp.int32, (W, 2 * W), 1)
        consts = _swa_consts(sink_ref)
        hh = pl.program_id(1)
        n0 = hh * nbh

        @pl.when(hh == 0)
        def _():
            dkt_ref[...] = jnp.zeros_like(dkt_ref)
            dvt_ref[...] = jnp.zeros_like(dvt_ref)

        @pl.when(jnp.logical_and(pl.program_id(0) == 0, hh == 0))
        def _():
            dsink_ref[...] = jnp.zeros_like(dsink_ref)

        def blk(i, dsink):
            n = n0 + i
            r, kb, _, q4, k2, v2, dist4, mask4 = _swa_block(q_ref, k_ref, v_ref, pb_ref, pr_ref, n, i, ij)
            o4, do4 = o_ref[0, r, :], do_ref[0, r, :]
            dq_g = []
            dkt = jnp.zeros((LANE, 2 * W), F32)
            dvt = jnp.zeros((LANE, 2 * W), F32)
            for g, (slope, sink) in enumerate(consts):
                q_st = _swa_stack(q4, g, F32)
                do_st = _swa_stack(do4, g, F32)
                dcol = jnp.sum(do_st * _swa_stack(o4, g, F32), axis=1, keepdims=True)
                qst, dob = _mx(q_st), _mx(do_st)
                lse_c = lse_ref[0, i, g]
                pr = jnp.exp2(_dot_nt(qst, k2) - slope * dist4 + mask4 - _wide(lse_c))
                dsb = _mx(pr * (_dot_nt(dob, v2) - dcol))
                psd = jnp.exp2(sink - lse_c)[:, 0:1] * dcol
                for p in range(4):
                    dsink = dsink - jnp.where(lane1 == p + 4 * g,
                                              jnp.sum(psd[W * p:W * (p + 1)], axis=0, keepdims=True), 0.0)
                dq_g.append(_dot(dsb, k2))
                dkt = dkt + _dot(_mx(q_st.T), dsb)
                dvt = dvt + _dot(_mx(do_st.T), _mx(pr))
            _swa_unstack(dq_ref, r, dq_g[0], dq_g[1], SWA_SCALE)
            dkt_ref[kb] += dkt[:, 0:W]
            dkt_ref[kb + 1] += dkt[:, W:2 * W]
            dvt_ref[kb] += dvt[:, 0:W]
            dvt_ref[kb + 1] += dvt[:, W:2 * W]
            return dsink

        dsink_ref[...] += lax.fori_loop(0, nbh, blk, jnp.zeros((1, LANE), F32))

        @pl.when(hh == nh - 1)
        def _():
            def flush(n, _):
                r = pl.ds(pl.multiple_of(n * W, W), W)
                dk_ref[0, r, :] = (1.0 / LOG2E) * dkt_ref[n].T
                dv_ref[0, r, :] = dvt_ref[n].T
                return 0

            lax.fori_loop(0, nb, flush, 0)

    seq = lambda w: pl.BlockSpec((1, S, w), lambda b, h: (b, 0, 0))
    part = lambda w: pl.BlockSpec((1, S // nh, w), lambda b, h: (b, h, 0))
    return pl.pallas_call(
        body, name="swa_bwd", grid=(B, nh),
        out_shape=[jax.ShapeDtypeStruct((B, S, 512), F32), jax.ShapeDtypeStruct((B, S, LANE), F32),
                   jax.ShapeDtypeStruct((B, S, LANE), F32), jax.ShapeDtypeStruct((1, LANE), F32)],
        in_specs=[part(512), seq(LANE), seq(LANE), part(LANE), pl.BlockSpec((1, nb, W), lambda b, h: (b, 0, 0)),
                  pl.BlockSpec((1, LANE), lambda b, h: (0, 0)), part(512),
                  pl.BlockSpec((1, S // nh, 512), lambda b, h: (b, h, 1)),
                  pl.BlockSpec((1, nbh, 2, 4 * W, LANE), lambda b, h: (b, h, 0, 0, 0))],
        out_specs=[part(512), seq(LANE), seq(LANE), pl.BlockSpec((1, LANE), lambda b, h: (0, 0))],
        scratch_shapes=[pltpu.VMEM((nb, LANE, W), F32), pltpu.VMEM((nb, LANE, W), F32)],
        compiler_params=_cparams(("arbitrary", "arbitrary")),
    )(qs3, ks3, vs3, posb, posr, sinks, os3, do3, lse)


def _post(om, osw, g, w_out, x2, gate, fg, tgt, S, ts):
    T, D = x2.shape
    nsb = S // ts

    def body(om_ref, os_ref, g_ref, w_ref, x_ref, gate_ref, fg_ref, t_ref,
             dx2_ref, do_ref, dg_ref, loss_ref, dfg_ref, dgate_ref, dw_ref):
        i = pl.program_id(0)
        gv = g_ref[...]
        sg = 1.0 / (1.0 + jnp.exp(-gv))
        silu = gv * sg
        o = jnp.concatenate([om_ref[...], os_ref[...]], axis=1)
        ab = _mx(o * silu)
        y = _dot(ab, w_ref[...])
        gate = gate_ref[0]
        xo = x_ref[...] + gate * y
        r2 = lax.rsqrt(jnp.mean(xo * xo, axis=-1, keepdims=True) + EPS)
        xh = xo * r2
        fg = fg_ref[...]
        diff = xh * fg - t_ref[...]
        sq = jnp.sum(diff * diff, axis=0, keepdims=True)
        part = sq[:, 0:LANE]
        for t in range(1, D // LANE):
            part = part + sq[:, LANE * t:LANE * (t + 1)]
        dout = diff * (1.0 / D)
        dxh = dout * fg
        dx2 = r2 * (dxh - xh * jnp.mean(dxh * xh, axis=-1, keepdims=True))
        dx2_ref[...] = dx2
        dyb = _mx(dx2 * gate)
        da = _dot_nt(dyb, w_ref[...])
        do_ref[...] = da * silu
        dg_ref[...] = _mx(da * o * (sg * (1.0 + gv * (1.0 - sg))))

        @pl.when(i == 0)
        def _():
            loss_ref[...] = jnp.zeros_like(loss_ref)
            dfg_ref[...] = jnp.zeros_like(dfg_ref)
            dw_ref[...] = jnp.zeros_like(dw_ref)

        @pl.when(i % nsb == 0)
        def _():
            dgate_ref[...] = jnp.zeros_like(dgate_ref)

        loss_ref[...] += (0.5 / D) * part
        dfg_ref[...] += jnp.sum(dout * xh, axis=0, keepdims=True)
        dgate_ref[0] += jnp.sum(dx2 * y, axis=0, keepdims=True)
        dw_ref[...] += _dot_tn(ab, dyb)

    row = lambda w: pl.BlockSpec((ts, w), lambda i: (i, 0))
    full = lambda a: pl.BlockSpec(a.shape, lambda i: (0,) * a.ndim)
    per_b = pl.BlockSpec((1, 1, D), lambda i: (i // nsb, 0, 0))
    return pl.pallas_call(
        body, name="post", grid=(T // ts,),
        out_shape=[jax.ShapeDtypeStruct((T, D), F32), jax.ShapeDtypeStruct((T, 1024), F32),
                   jax.ShapeDtypeStruct((T, 1024), _MXU_DTYPE), jax.ShapeDtypeStruct((1, LANE), F32),
                   jax.ShapeDtypeStruct((1, D), F32), jax.ShapeDtypeStruct(gate.shape, F32),
                   jax.ShapeDtypeStruct(w_out.shape, F32)],
        in_specs=[row(512), row(512), row(1024), full(w_out), row(D), per_b, full(fg), row(D)],
        out_specs=[row(D), row(1024), row(1024),
                   pl.BlockSpec((1, LANE), lambda i: (0, 0)), pl.BlockSpec((1, D), lambda i: (0, 0)), per_b,
                   full(w_out)],
        compiler_params=_cparams(("arbitrary",)),
    )(om, osw, g, w_out, x2, gate, fg, tgt)


def _pre_bwd(dq, dk, dv, dqs, dks, dvs, dg, zq, zkv, ql, kvl, x2, dx2, scale, ng, w_in, gq, gkv, w_uq, w_uk, w_uv,
             rope, S, ts):
    T, D = x2.shape
    nsb = S // ts
    WQ = MLA_HEADS * LANE

    def body(dq_ref, dk_ref, dv_ref, dqs_ref, dks_ref, dvs_ref, dg_ref, zq_ref, zkv_ref, ql_ref, kvl_ref, x_ref, dx2_ref,
             sc_ref,
             ng_ref, w_ref, gq_ref, gkv_ref, wuq_ref, wuk_ref, wuv_ref, cd_ref, sd_ref,
             gx_ref, dz_ref, dgq_ref, dgkv_ref, dng_ref, dsh_ref, dsc_ref, dwuq_ref, dwuk_ref, dwuv_ref):
        i = pl.program_id(0)

        @pl.when(i == 0)
        def _():
            dgq_ref[...] = jnp.zeros_like(dgq_ref)
            dgkv_ref[...] = jnp.zeros_like(dgkv_ref)
            dng_ref[...] = jnp.zeros_like(dng_ref)
            dwuq_ref[...] = jnp.zeros_like(dwuq_ref)
            dwuk_ref[...] = jnp.zeros_like(dwuk_ref)
            dwuv_ref[...] = jnp.zeros_like(dwuv_ref)

        @pl.when(i % nsb == 0)
        def _():
            dsh_ref[...] = jnp.zeros_like(dsh_ref)
            dsc_ref[...] = jnp.zeros_like(dsc_ref)

        def norm_bwd(z, dy, gain):
            r = lax.rsqrt(jnp.mean(z * z, axis=-1, keepdims=True) + EPS)
            zh = z * r
            dzh = dy * gain
            return r * (dzh - zh * jnp.mean(dzh * zh, axis=-1, keepdims=True)), jnp.sum(dy * zh, axis=0, keepdims=True)

        tables = _rope_split(cd_ref[...], sd_ref[...])
        ng = ng_ref[...]
        sc1 = 1.0 + sc_ref[0]

        def rows_chain(rs):
            cq, ck, sa, sb = (t[rs] for t in tables)
            dqr = dq_ref[rs, :]
            dqb = _mx(dqr * _tile_heads(cq) + pltpu.roll(dqr * _tile_heads(sa), 16, 1)
                      + pltpu.roll(dqr * _tile_heads(sb), WQ - 16, 1))
            p_uq = _dot_tn(ql_ref[rs, :], dqb)
            dzq, dgq = norm_bwd(zq_ref[rs, :], _dot_nt(dqb, wuq_ref[...]), gq_ref[...])
            dkr = dk_ref[rs, :]
            dkb = _mx(dkr)
            p_uk = _dot_tn(kvl_ref[rs, :], dkb)
            dvb = _mx(dv_ref[rs, :])
            p_uv = _dot_tn(kvl_ref[rs, :], dvb)
            dzkv, dgkv = norm_bwd(zkv_ref[rs, :], _dot_nt(dkb, wuk_ref[...]) + _dot_nt(dvb, wuv_ref[...]), gkv_ref[...])
            dkpe = dkr[:, 0:LANE]
            for h in range(1, MLA_HEADS):
                dkpe = dkpe + dkr[:, LANE * h:LANE * (h + 1)]
            dkro = dkpe * ck + pltpu.roll(dkpe * sa, 16, 1) + pltpu.roll(dkpe * sb, LANE - 16, 1)
            dz_ref[rs, 0:384] = _mx(dzq)
            dz_ref[rs, 384:640] = _mx(dzkv)
            dz_ref[rs, 640:768] = _mx(dkro)
            dz_ref[rs, 768:1280] = dg_ref[rs, 0:512]
            dz_ref[rs, 1280:1792] = _mx(dqs_ref[rs, :])
            dz_ref[rs, 1792:1920] = _mx(dks_ref[rs, :])
            dz_ref[rs, 1920:2048] = _mx(dvs_ref[rs, :])
            dz_ref[rs, 2048:2560] = dg_ref[rs, 512:1024]
            dh = _dot_nt(dz_ref[rs, :], w_ref[...])
            x = x_ref[rs, :]
            r1 = lax.rsqrt(jnp.mean(x * x, axis=-1, keepdims=True) + EPS)
            xn = x * r1
            dxn = dh * ng * sc1
            gx_ref[rs, :] = dx2_ref[rs, :] + r1 * (dxn - xn * jnp.mean(dxn * xn, axis=-1, keepdims=True))
            return (p_uq, p_uk, p_uv, dgq, dgkv, jnp.sum(dh, axis=0, keepdims=True),
                    jnp.sum(dh * (xn * ng), axis=0, keepdims=True), jnp.sum(dh * xn * sc1, axis=0, keepdims=True))

        hr = ts // 2
        parts = [rows_chain(slice(hr * t, hr * (t + 1))) for t in range(2)]
        p_uq, p_uk, p_uv, dgq, dgkv, dsh, dsc, dng = (a + b for a, b in zip(*parts))
        dwuq_ref[...] += p_uq
        dwuk_ref[...] += p_uk
        dwuv_ref[...] += p_uv
        dgq_ref[...] += dgq
        dgkv_ref[...] += dgkv
        dsh_ref[0] += dsh
        dsc_ref[0] += dsc
        dng_ref[...] += dng

    row = lambda w: pl.BlockSpec((ts, w), lambda i: (i, 0))
    full = lambda a: pl.BlockSpec(a.shape, lambda i: (0,) * a.ndim, pipeline_mode=pl.Buffered(1))
    per_b = pl.BlockSpec((1, 1, D), lambda i: (i // nsb, 0, 0))
    dense = pl.BlockSpec((ts // 8, LANE), lambda i: (i, 0))
    vec = lambda w: pl.BlockSpec((1, w), lambda i: (0, 0))
    return pl.pallas_call(
        body, name="pre_bwd", grid=(T // ts,),
        out_shape=[jax.ShapeDtypeStruct((T, D), F32), jax.ShapeDtypeStruct((T, D_IN_PAD), _MXU_DTYPE), jax.ShapeDtypeStruct((1, 384), F32),
                   jax.ShapeDtypeStruct((1, 256), F32), jax.ShapeDtypeStruct((1, D), F32),
                   jax.ShapeDtypeStruct(scale.shape, F32), jax.ShapeDtypeStruct(scale.shape, F32),
                   jax.ShapeDtypeStruct(w_uq.shape, F32), jax.ShapeDtypeStruct(w_uk.shape, F32),
                   jax.ShapeDtypeStruct(w_uv.shape, F32)],
        in_specs=[row(WQ), row(WQ), row(512), row(512), row(LANE), row(LANE), row(1024), row(384), row(256), row(384),
                  row(256), row(D), row(D), per_b, full(ng), full(w_in), full(gq), full(gkv), full(w_uq), full(w_uk), full(w_uv),
                  dense, dense],
        out_specs=[row(D), row(D_IN_PAD), vec(384), vec(256), vec(D), per_b, per_b, full(w_uq), full(w_uk), full(w_uv)],
        compiler_params=_cparams(("arbitrary",)),
    )(dq, dk, dv, dqs, dks, dvs, dg, zq, zkv, ql, kvl, x2, dx2, scale, ng, w_in, gq, gkv, w_uq, w_uk, w_uv, *rope)


def _tn_matmul(a, b, tn, tk, name):
    T, M = a.shape
    N = b.shape[1]

    def body(a_ref, b_ref, o_ref):
        @pl.when(pl.program_id(1) == 0)
        def _():
            o_ref[...] = jnp.zeros_like(o_ref)

        o_ref[...] += _dot_tn(a_ref[...], b_ref[...])

    return pl.pallas_call(
        body, name=name, grid=(N // tn, T // tk),
        out_shape=jax.ShapeDtypeStruct((M, N), F32),
        in_specs=[pl.BlockSpec((tk, M), lambda j, k: (k, 0)), pl.BlockSpec((tk, tn), lambda j, k: (k, j))],
        out_specs=pl.BlockSpec((M, tn), lambda j, k: (0, j)),
        compiler_params=_cparams(("arbitrary", "arbitrary")),
    )(a, b)


def _finalize(parts_all, dmod_all, dmod_cols, c_act_all):
    nparts = parts_all.shape[-1]
    nmod = dmod_all.shape[-1]

    def body(p_ref, dm_ref, dmc_ref, c_ref, ps_ref, loss_ref, db_ref, dw_ref):
        acc = p_ref[0]
        for j in range(1, N_DEV):
            acc = acc + p_ref[j]
        ps_ref[...] = acc
        loss_ref[...] = jnp.sum(acc[:, 0:LANE], axis=1, keepdims=True)
        db = dm_ref[0:1, :]
        for j in range(1, dm_ref.shape[0]):
            db = db + dm_ref[j:j + 1, :]
        db_ref[...] = db
        dw_ref[...] = _dot_tn(_mx(_silu(c_ref[...])), _mx(dmc_ref[...]))

    return pl.pallas_call(
        body, name="finalize",
        out_shape=[jax.ShapeDtypeStruct((1, nparts), F32), jax.ShapeDtypeStruct((1, 1), F32),
                   jax.ShapeDtypeStruct((1, nmod), F32),
                   jax.ShapeDtypeStruct((c_act_all.shape[1], dmod_cols.shape[1]), F32)],
        in_specs=[_vmem()] * 4, out_specs=[_vmem()] * 4,
        compiler_params=_cparams(),
    )(parts_all, dmod_all, dmod_cols, c_act_all)


def _adamw(ws, gs, ms, vs):
    n = len(ws)

    def body(*refs):
        w_refs, g_refs, m_refs, v_refs, d_refs, nm_refs, nv_refs = (refs[t * n:(t + 1) * n] for t in range(7))
        for t in range(n):
            gv = g_refs[t][...]
            nm = ADAM_B1 * m_refs[t][...] + (1.0 - ADAM_B1) * gv
            nv = ADAM_B2 * v_refs[t][...] + (1.0 - ADAM_B2) * (gv * gv)
            m_hat = nm / (1.0 - ADAM_B1 ** ADAM_STEP)
            v_hat = nv / (1.0 - ADAM_B2 ** ADAM_STEP)
            d_refs[t][...] = -ADAM_LR * (m_hat / (jnp.sqrt(v_hat) + ADAM_EPS) + ADAM_WD * w_refs[t][...])
            nm_refs[t][...] = nm
            nv_refs[t][...] = nv

    out = pl.pallas_call(
        body, name="adamw",
        out_shape=[jax.ShapeDtypeStruct(w.shape, F32) for w in ws] * 3,
        in_specs=[_vmem()] * (4 * n), out_specs=[_vmem()] * (3 * n),
        compiler_params=_cparams(),
    )(*ws, *gs, *ms, *vs)
    return out[:n], out[n:2 * n], out[2 * n:]


def _pair_perm(a, axis, order):
    a = jnp.moveaxis(a, axis, -1)
    lead = a.shape[:-1]
    a = a.reshape(lead + (8, 64))[..., list(order), :].reshape(lead + (512,))
    return jnp.moveaxis(a, -1, axis)


def _pad_w_in(w):
    z = lambda n: jnp.zeros((w.shape[0], n), w.dtype)
    return jnp.concatenate([
        w[:, 0:640], z(64), w[:, 640:672], z(32), w[:, 672:1184],
        _pair_perm(w[:, 1184:1696], 1, PAIR_ORDER), w[:, 1696:1952], _pair_perm(w[:, 1952:2464], 1, PAIR_ORDER)], axis=1)


def _unpad_w_in_t(g):
    return jnp.concatenate([
        g[0:640], g[704:736], g[768:1280],
        _pair_perm(g[1280:1792], 0, PAIR_INV), g[1792:2048], _pair_perm(g[2048:2560], 0, PAIR_INV)], axis=0)


def _rope_table(positions):
    T = positions.size
    inv = ROPE_THETA ** (-jnp.arange(0, MLA_ROPE, 2, dtype=F32) / MLA_ROPE)
    pos = jnp.repeat(positions.reshape(T // 8, 8)[:, ::-1].astype(F32), MLA_ROPE // 2, axis=1)
    ang = pos * jnp.tile(inv, 8)[None, :]
    return jnp.cos(ang), jnp.sin(ang)


def _local_step(x, mod, positions, ng, w_in, gq, gkv, w_uq, w_ukv, sinks, w_out, fg, tgt,
                ts=512, fq=512, fk=512, bq=512, bk=512):
    B, S, D = x.shape
    T = B * S
    x2 = x.reshape(T, D)
    shift, scale, gate = (mod[:, None, k * D:(k + 1) * D] for k in range(3))
    w_in_p = _pad_w_in(w_in)
    w_uq_p = jnp.pad(w_uq.reshape(Q_LORA, MLA_HEADS, 96), ((0, 0), (0, 0), (0, 32))).reshape(Q_LORA, MLA_HEADS * LANE)
    w_ukv3 = w_ukv.reshape(KV_LORA, MLA_HEADS, 128)
    w_uk_p = jnp.pad(w_ukv3[:, :, :64], ((0, 0), (0, 0), (0, 64))).reshape(KV_LORA, MLA_HEADS * LANE)
    w_uv = w_ukv3[:, :, 64:].reshape(KV_LORA, 512)
    w_out_p = jnp.concatenate([w_out[:512], _pair_perm(w_out[512:], 0, PAIR_ORDER)], axis=0)
    rope = _rope_table(positions)
    posf = positions.astype(F32)
    posb = jnp.broadcast_to(posf[:, :, None], (B, S, LANE))
    posr = posf.reshape(B, S // WINDOW, WINDOW)
    sinks_l = jnp.pad(sinks.reshape(1, SWA_HEADS), ((0, 0), (0, LANE - SWA_HEADS)))

    (hb, zq, zkv, ql, kvl, q, k, v, qs, ks, vs, g) = _pre_fwd(
        x2, shift, scale, ng, w_in_p, gq, gkv, w_uq_p, w_uk_p, w_uv, rope, S, ts)
    r3 = lambda a: a.reshape(B, S, a.shape[-1])
    om, lse_m = _mla_fwd(r3(q), r3(k), r3(v), fq, fk)
    osw, lse_s = _swa_fwd(r3(qs), r3(ks), r3(vs), posb, posr, sinks_l)
    dx2, do, dg, loss_v, dfg, dgate, dw_out_p = _post(
        om.reshape(T, 512), osw.reshape(T, 512), g, w_out_p, x2, gate, fg.reshape(1, D), tgt.reshape(T, D), S, ts)
    do3 = r3(do)
    dq, dk, dv = _mla_bwd(r3(q), r3(k), r3(v), om, do3, lse_m, bq, bk)
    dqs, dks, dvs, dsink = _swa_bwd(r3(qs), r3(ks), r3(vs), posb, posr, sinks_l, osw, do3, lse_s)
    f2 = lambda a: a.reshape(T, a.shape[-1])
    gx, dz, dgq, dgkv, dng, dsh, dsc, dw_uq_p, dw_uk_p, dw_uv = _pre_bwd(
        f2(dq), f2(dk), f2(dv), f2(dqs), f2(dks), f2(dvs), dg, zq, zkv, ql, kvl, x2, dx2, scale, ng, w_in_p, gq, gkv,
        w_uq_p, w_uk_p, w_uv, rope, S, ts)
    tk = min(T, 1024)
    dw_in_t = _unpad_w_in_t(_tn_matmul(dz, hb, 512, tk, "dw_in"))
    dw_out = jnp.concatenate([dw_out_p[:512], _pair_perm(dw_out_p[512:], 0, PAIR_INV)], axis=0)
    dw_uq = dw_uq_p.reshape(Q_LORA, MLA_HEADS, LANE)[:, :, :96].reshape(Q_LORA, 768)
    dw_uk = dw_uk_p.reshape(KV_LORA, MLA_HEADS, LANE)[:, :, :64]
    dw_uv = dw_uv.reshape(KV_LORA, MLA_HEADS, 64)
    dw_ukv = jnp.concatenate([dw_uk, dw_uv], axis=2).reshape(KV_LORA, 1024)
    parts = jnp.concatenate([loss_v, dfg, dng, dgq, dgkv, dsink], axis=1)
    dmod = jnp.concatenate([dsh, dsc, dgate], axis=2).reshape(B, 3 * D)
    return gx.reshape(B, S, D), dw_in_t, dw_uq, dw_ukv, dw_out, parts, dmod


def kernel(x, c, positions, w_ada, b_ada, norm_gain, w_in, q_norm_gain, kv_norm_gain, w_uq, w_ukv, swa_sinks, w_out, final_gain, loss_target, m_w_ada, m_b_ada, m_norm_gain, m_w_in, m_q_norm_gain, m_kv_norm_gain, m_w_uq, m_w_ukv, m_swa_sinks, m_w_out, m_final_gain, v_w_ada, v_b_ada, v_norm_gain, v_w_in, v_q_norm_gain, v_kv_norm_gain, v_w_uq, v_w_ukv, v_swa_sinks, v_w_out, v_final_gain):
    B, S, D = x.shape
    me = 4 * lax.axis_index("x") + 2 * lax.axis_index("y") + lax.axis_index("c")
    bf = _MXU_DTYPE

    c_all, win_g, wuq_g, wukv_g, wout_g = _all_gather(
        [c, w_in[0].astype(bf), w_uq[0].astype(bf), w_ukv[0].astype(bf), w_out[0].astype(bf)], "ag_weights")
    c_all = c_all.reshape(N_DEV * B, D)
    cat_cols = lambda a: jnp.transpose(a, (1, 0, 2)).reshape(a.shape[1], N_DEV * a.shape[2])
    w_in_f, w_uq_f, w_ukv_f = cat_cols(win_g), cat_cols(wuq_g), cat_cols(wukv_g)
    w_out_f = wout_g.reshape(D, D)

    ncol = w_ada.shape[2]
    b_cols = lax.dynamic_slice_in_dim(b_ada, me * ncol, ncol, axis=1)
    (mod_g,) = _all_gather([_ada_fwd(c_all, w_ada[0], b_cols)], "ag_mod")
    mod = lax.dynamic_slice_in_dim(mod_g, me * B, B, axis=1)
    mod = jnp.transpose(mod, (1, 0, 2)).reshape(B, 3 * D)

    gx, dw_in_t, dw_uq, dw_ukv, dw_out, parts, dmod = _local_step(
        x, mod, positions, norm_gain, w_in_f, q_norm_gain, kv_norm_gain, w_uq_f, w_ukv_f, swa_sinks,
        w_out_f, final_gain, loss_target)

    parts_g, dmod_g = _all_gather([parts, dmod], "ag_small")
    dmod_all = dmod_g.reshape(N_DEV * B, 3 * D)
    dmod_cols = lax.dynamic_slice_in_dim(dmod_all, me * ncol, ncol, axis=1)
    psum, loss, g_b_ada, g_w_ada = _finalize(parts_g, dmod_all, dmod_cols, c_all)
    loss = loss.reshape(())
    o = LANE
    g_final_gain = psum[0, o:o + D]
    g_norm_gain = psum[:, o + D:o + 2 * D]
    o += 2 * D
    g_q_norm_gain = psum[:, o:o + Q_LORA]
    g_kv_norm_gain = psum[:, o + Q_LORA:o + Q_LORA + KV_LORA]
    o += Q_LORA + KV_LORA
    g_sinks = psum[:, o:o + SWA_HEADS]

    split_cols = lambda a: jnp.transpose(a.reshape(a.shape[0], 4, 2, a.shape[1] // N_DEV), (1, 2, 0, 3))
    split_rows = lambda a: a.reshape(4, 2, a.shape[0] // N_DEV, a.shape[1])
    g_w_in_t, g_w_uq, g_w_ukv, g_w_out = _reduce_scatter(
        [split_rows(dw_in_t), split_cols(dw_uq), split_cols(dw_ukv), split_rows(dw_out)], "rs_grads")
    g_w_in = g_w_in_t.T

    grads = [g_w_ada, g_b_ada, g_norm_gain, g_w_in, g_q_norm_gain, g_kv_norm_gain, g_w_uq, g_w_ukv, g_sinks, g_w_out,
             g_final_gain]
    ws = [w_ada, b_ada, norm_gain, w_in, q_norm_gain, kv_norm_gain, w_uq, w_ukv, swa_sinks, w_out, final_gain]
    ms = [m_w_ada, m_b_ada, m_norm_gain, m_w_in, m_q_norm_gain, m_kv_norm_gain, m_w_uq, m_w_ukv, m_swa_sinks, m_w_out,
          m_final_gain]
    vs = [v_w_ada, v_b_ada, v_norm_gain, v_w_in, v_q_norm_gain, v_kv_norm_gain, v_w_uq, v_w_ukv, v_swa_sinks, v_w_out,
          v_final_gain]
    two_d = [(1, w.shape[0]) if w.ndim == 1 else w.shape[-2:] for w in ws]
    flat = lambda arrs: [a.reshape(s) for a, s in zip(arrs, two_d)]
    deltas, new_ms, new_vs = _adamw(flat(ws), flat(grads), flat(ms), flat(vs))
    shaped = lambda arrs: [a.reshape(w.shape) for a, w in zip(arrs, ws)]
    return (loss, gx, *shaped(grads), *shaped(deltas), *shaped(new_ms), *shaped(new_vs))
```

```python
import functools

import numpy as np
import jax
import jax.numpy as jnp
from jax import lax
from jax.experimental import pallas as pl
from jax.experimental.pallas import tpu as pltpu

F32 = jnp.float32
_MXU_DTYPE = jnp.bfloat16

N_DEV = 8
D_MODEL = 1024
MLA_HEADS = 8
MLA_NOPE = 64
MLA_ROPE = 32
MLA_V = 64
Q_LORA = 384
KV_LORA = 256
SWA_HEADS = 8
SWA_KV_HEADS = 2
SWA_HEAD_DIM = 64
WINDOW = 128
ROPE_THETA = 10000.0
EPS = 1e-6
MLA_SCALE = float((MLA_NOPE + MLA_ROPE) ** -0.5)
SWA_SCALE = float(SWA_HEAD_DIM ** -0.5)
LOG2E = 1.4426950408889634
MLA_QSCALE = MLA_SCALE * LOG2E
D_IN = 2464
D_IN_PAD = 2560
PAIR_ORDER = (0, 4, 1, 5, 2, 6, 3, 7)
PAIR_INV = (0, 2, 4, 6, 1, 3, 5, 7)

ADAM_LR = 0.001
ADAM_B1 = 0.9
ADAM_B2 = 0.999
ADAM_EPS = 1e-08
ADAM_WD = 0.01
ADAM_STEP = 10

LANE = 128
VMEM_LIMIT = 56 * 1024 * 1024

MESH = pl.DeviceIdType.MESH
NEG_INF = float("-inf")
SWA_SEQ_SPLIT = 2


def _mx(a):
    return a.astype(_MXU_DTYPE)


def _dot(a, b):
    return jnp.dot(a, b, preferred_element_type=F32)


def _dot_nt(a, b):
    return lax.dot_general(a, b, (((1,), (1,)), ((), ())), preferred_element_type=F32)


def _dot_tn(a, b):
    return lax.dot_general(a, b, (((0,), (0,)), ((), ())), preferred_element_type=F32)


def _cparams(sem=None):
    return pltpu.CompilerParams(dimension_semantics=sem, vmem_limit_bytes=VMEM_LIMIT)


def _vmem():
    return pl.BlockSpec(memory_space=pltpu.VMEM)


def _lane_iota(shape):
    return lax.broadcasted_iota(jnp.int32, shape, len(shape) - 1)


def _all_gather(arrs, name):
    n = len(arrs)

    def body(*refs):
        ins, outs = refs[:n], refs[n:2 * n]
        send_sems, recv_sems, local_sems = refs[2 * n:]
        x, y, c = lax.axis_index("x"), lax.axis_index("y"), lax.axis_index("c")
        me, sibling = (x, y, c), (x, y, 1 - c)
        chips = [(1 - x, y), (x, 1 - y), (1 - x, 1 - y)]

        def slot(a, dev):
            return outs[a].at[4 * dev[0] + 2 * dev[1] + dev[2]]

        def copy(a, k, block, to, src=None):
            return pltpu.make_async_remote_copy(
                src_ref=slot(a, block) if src is None else src, dst_ref=slot(a, block),
                send_sem=send_sems.at[7 * a + k], recv_sem=recv_sems.at[7 * a + k],
                device_id=to, device_id_type=MESH)

        mine = [pltpu.make_async_copy(ins[a], slot(a, me), local_sems.at[a]) for a in range(n)]
        for cp in mine:
            cp.start()
        first = []
        for a in range(n):
            first.append(copy(a, 0, me, sibling, src=ins[a]))
            first += [copy(a, 1 + j, me, (*chip, c), src=ins[a]) for j, chip in enumerate(chips)]
        for cp in first:
            cp.start()
        passed = []
        for j, chip in enumerate(chips):
            for a in range(n):
                copy(a, 1 + j, (*chip, c), me).wait_recv()
                cp = copy(a, 4 + j, (*chip, c), sibling)
                cp.start()
                passed.append(cp)
        for a in range(n):
            copy(a, 0, sibling, me).wait_recv()
            for j, chip in enumerate(chips):
                copy(a, 4 + j, (*chip, 1 - c), me).wait_recv()
        for cp in first + passed:
            cp.wait_send()
        for cp in mine:
            cp.wait()

    return pl.pallas_call(
        body, name=name,
        out_shape=[jax.ShapeDtypeStruct((N_DEV,) + a.shape, a.dtype) for a in arrs],
        in_specs=[_vmem()] * n, out_specs=[_vmem()] * n,
        scratch_shapes=[pltpu.SemaphoreType.DMA((7 * n,)), pltpu.SemaphoreType.DMA((7 * n,)),
                        pltpu.SemaphoreType.DMA((n,))],
        compiler_params=pltpu.CompilerParams(vmem_limit_bytes=VMEM_LIMIT),
    )(*arrs)


def _reduce_scatter(arrs, name):
    n = len(arrs)
    halves = [a.astype(jnp.bfloat16) for a in arrs]

    def body(*refs):
        xs, xbs, outs = refs[:n], refs[n:2 * n], refs[2 * n:3 * n]
        parts, recv_a, send_b, recv_b = (refs[(3 + t) * n:(4 + t) * n] for t in range(4))
        send_sems, recv_sems, local_sems = refs[7 * n:]
        x, y, c = lax.axis_index("x"), lax.axis_index("y"), lax.axis_index("c")
        myq = 2 * x + y

        def chip(k):
            return (1 - x if k & 2 else x, 1 - y if k & 1 else y)

        def from_sibling(a):
            return pltpu.make_async_remote_copy(
                src_ref=xbs[a].at[:, 1 - c], dst_ref=recv_a[a], send_sem=send_sems.at[4 * a], recv_sem=recv_sems.at[4 * a],
                device_id=(x, y, 1 - c), device_id_type=MESH)

        def to_owner(a, k):
            qx, qy = chip(k)
            return pltpu.make_async_remote_copy(
                src_ref=send_b[a].at[2 * qx + qy], dst_ref=recv_b[a].at[myq],
                send_sem=send_sems.at[4 * a + k], recv_sem=recv_sems.at[4 * a + k],
                device_id=(qx, qy, c), device_id_type=MESH)

        mine = [pltpu.make_async_copy(xs[a].at[:, c], parts[a], local_sems.at[a]) for a in range(n)]
        first = [from_sibling(a) for a in range(n)]
        for cp in mine + first:
            cp.start()
        second = []
        for a in range(n):
            mine[a].wait()
            first[a].wait_recv()
            parts[a][...] = parts[a][...] + recv_a[a][...].astype(F32)
            send_b[a][...] = parts[a][...].astype(jnp.bfloat16)
            for k in range(1, 4):
                cp = to_owner(a, k)
                cp.start()
                second.append(cp)
        for a in range(n):
            acc = parts[a][myq]
            for k in range(1, 4):
                to_owner(a, k).wait_recv()
                qx, qy = chip(k)
                acc = acc + recv_b[a][2 * qx + qy].astype(F32)
            outs[a][...] = acc
        for cp in first + second:
            cp.wait_send()

    quarter = lambda a, dt: pltpu.VMEM((4,) + a.shape[2:], dt)
    return pl.pallas_call(
        body, name=name,
        out_shape=[jax.ShapeDtypeStruct(a.shape[2:], F32) for a in arrs],
        in_specs=[pl.BlockSpec(memory_space=pl.ANY)] * (2 * n), out_specs=[_vmem()] * n,
        scratch_shapes=[quarter(a, F32) for a in arrs] + [quarter(a, jnp.bfloat16) for a in arrs] * 3
        + [pltpu.SemaphoreType.DMA((4 * n,)), pltpu.SemaphoreType.DMA((4 * n,)), pltpu.SemaphoreType.DMA((n,))],
        compiler_params=pltpu.CompilerParams(vmem_limit_bytes=VMEM_LIMIT),
    )(*arrs, *halves)


def _silu(t):
    return t * (1.0 / (1.0 + jnp.exp(-t)))


def _ada_fwd(c_act_all, w_ada_shard, b_shard):
    def body(c_ref, w_ref, b_ref, o_ref):
        o_ref[...] = _dot(_mx(_silu(c_ref[...])), _mx(w_ref[...])) + b_ref[...]

    return pl.pallas_call(
        body, name="ada_fwd",
        out_shape=jax.ShapeDtypeStruct((c_act_all.shape[0], w_ada_shard.shape[1]), F32),
        in_specs=[_vmem()] * 3, out_specs=_vmem(),
        compiler_params=_cparams(),
    )(c_act_all, w_ada_shard, b_shard)


def _rope_split(cd, sd):
    n = cd.shape[0]

    def expand(d):
        rep = jnp.broadcast_to(d[:, None, :], (n, 8, LANE)).reshape(8 * n, LANE)
        return pltpu.roll(rep, 0, 1, stride=16, stride_axis=0)

    c, s = expand(cd), expand(sd)
    lane = _lane_iota(c.shape)
    first = jnp.logical_and(lane >= 64, lane < 80)
    second = jnp.logical_and(lane >= 80, lane < 96)
    ck = jnp.where(first, pltpu.roll(c, 80, 1), jnp.where(second, pltpu.roll(c, 96, 1), 0.0))
    cq = jnp.where(lane < 64, 1.0, ck)
    sa = jnp.where(first, -pltpu.roll(s, 80, 1), 0.0)
    sb = jnp.where(second, pltpu.roll(s, 96, 1), 0.0)
    return cq, ck, sa, sb


def _tile_heads(t):
    return jnp.concatenate([t] * MLA_HEADS, axis=1)


def _pre_fwd(x2, shift, scale, ng, w_in, gq, gkv, w_uq, w_uk, w_uv, rope, S, ts):
    T, D = x2.shape
    nsb = S // ts
    WQ = MLA_HEADS * LANE

    def body(x_ref, sh_ref, sc_ref, ng_ref, w_ref, gq_ref, gkv_ref, wuq_ref, wuk_ref, wuv_ref,
             cd_ref, sd_ref,
             hb_ref, zq_ref, zkv_ref, ql_ref, kvl_ref, q_ref, k_ref, v_ref, qs_ref, ks_ref, vs_ref, g_ref):
        x = x_ref[...]
        r1 = lax.rsqrt(jnp.mean(x * x, axis=-1, keepdims=True) + EPS)
        h = ((x * r1) * ng_ref[...]) * (1.0 + sc_ref[0]) + sh_ref[0]
        hb = _mx(h)
        hb_ref[...] = hb
        zq = _dot(hb, w_ref[:, 0:384])
        zq_ref[...] = zq
        rq = lax.rsqrt(jnp.mean(zq * zq, axis=-1, keepdims=True) + EPS)
        ql = _mx((zq * rq) * gq_ref[...])
        ql_ref[...] = ql
        q = _dot(ql, wuq_ref[...])
        cq, ck, sa, sb = _rope_split(cd_ref[...], sd_ref[...])
        q = (q * _tile_heads(cq) + pltpu.roll(q, WQ - 16, 1) * _tile_heads(sa)
             + pltpu.roll(q, 16, 1) * _tile_heads(sb))
        q_ref[...] = _mx(q * MLA_QSCALE)
        zkv = _dot(hb, w_ref[:, 384:640])
        zkv_ref[...] = zkv
        rkv = lax.rsqrt(jnp.mean(zkv * zkv, axis=-1, keepdims=True) + EPS)
        kvl = _mx((zkv * rkv) * gkv_ref[...])
        kvl_ref[...] = kvl
        kr = _dot(hb, w_ref[:, 640:768])
        kpe = kr * ck + pltpu.roll(kr, LANE - 16, 1) * sa + pltpu.roll(kr, 16, 1) * sb
        kf = _dot(kvl, wuk_ref[...])
        k_ref[...] = _mx(kf + jnp.concatenate([kpe] * MLA_HEADS, axis=1))
        v_ref[...] = _mx(_dot(kvl, wuv_ref[...]))
        g_ref[:, 0:512] = _dot(hb, w_ref[:, 768:1280])
        qs_ref[...] = _mx(_dot(hb, w_ref[:, 1280:1792]) * (SWA_SCALE * LOG2E))
        ks_ref[...] = _mx(_dot(hb, w_ref[:, 1792:1920]))
        vs_ref[...] = _mx(_dot(hb, w_ref[:, 1920:2048]))
        g_ref[:, 512:1024] = _dot(hb, w_ref[:, 2048:2560])

    row = lambda w: pl.BlockSpec((ts, w), lambda i: (i, 0))
    dense = pl.BlockSpec((ts // 8, LANE), lambda i: (i, 0))
    full = lambda a: pl.BlockSpec(a.shape, lambda i: (0,) * a.ndim)
    per_b = pl.BlockSpec((1, 1, D), lambda i: (i // nsb, 0, 0))
    out_w = [(D, _MXU_DTYPE), (384, F32), (256, F32), (384, _MXU_DTYPE), (256, _MXU_DTYPE), (WQ, _MXU_DTYPE),
             (WQ, _MXU_DTYPE), (512, _MXU_DTYPE), (512, _MXU_DTYPE), (128, _MXU_DTYPE), (128, _MXU_DTYPE), (1024, F32)]
    return pl.pallas_call(
        body, name="pre_fwd", grid=(T // ts,),
        out_shape=[jax.ShapeDtypeStruct((T, w), dt) for w, dt in out_w],
        in_specs=[row(D), per_b, per_b, full(ng), full(w_in), full(gq), full(gkv), full(w_uq), full(w_uk), full(w_uv),
                  dense, dense],
        out_specs=[row(w) for w, _ in out_w],
        compiler_params=_cparams(("arbitrary",)),
    )(x2, shift, scale, ng, w_in, gq, gkv, w_uq, w_uk, w_uv, *rope)


def _mla_fwd(q3, k3, v3, tq, tk):
    B, S, _ = q3.shape
    nq = S // tq
    assert tk % tq == 0 or tq % tk == 0
    n_masked = max(1, tq // tk)

    def body(q_ref, k_ref, v_ref, o_ref, lse_ref):
        qi = pl.program_id(2)
        rows = lax.broadcasted_iota(jnp.int32, (tq, tk), 0)
        cols = lax.broadcasted_iota(jnp.int32, (tq, tk), 1)
        qs = [q_ref[0, :, LANE * j:LANE * (j + 1)] for j in range(2)]
        n_full = (qi * tq) // tk

        low_k = _lane_iota((tk, LANE)) < 64

        def step(kt, carry, masked):
            r0 = pl.multiple_of(kt * tk, tk)
            v2 = v_ref[0, pl.ds(r0, tk), :]
            vj = (jnp.where(low_k, v2, 1.0).astype(v2.dtype), jnp.where(low_k, 1.0, v2).astype(v2.dtype))
            out = []
            for j in range(2):
                m, acc = carry[j]
                s = _dot_nt(qs[j], k_ref[0, pl.ds(r0, tk), LANE * j:LANE * (j + 1)])
                if masked:
                    s = jnp.where(rows + qi * tq >= cols + kt * tk, s, NEG_INF)
                m_new = jnp.maximum(m, jnp.max(s, axis=1, keepdims=True))
                alpha = jnp.exp2(m - m_new)
                p = jnp.exp2(s - m_new)
                acc = alpha * acc + _dot(_mx(p), vj[j])
                out.append((m_new, acc))
            return tuple(out)

        init = (jnp.full((tq, 1), NEG_INF, F32), jnp.zeros((tq, LANE), F32))
        carry = lax.fori_loop(0, n_full, functools.partial(step, masked=False), (init, init))
        for t in range(n_masked):
            carry = step(n_full + t, carry, True)
        (m0, a0), (m1, a1) = carry
        low = _lane_iota((tq, LANE)) < 64
        l0 = jnp.where(low, pltpu.roll(a0, 64, 1), a0)
        l1 = jnp.where(low, a1, pltpu.roll(a1, 64, 1))
        o_ref[0] = jnp.where(low, a0 / l0, a1 / l1)
        lse_ref[0, 0] = m0 + jnp.log2(l0)
        lse_ref[0, 1] = m1 + jnp.log2(l1)

    return pl.pallas_call(
        body, name="mla_fwd", grid=(B, MLA_HEADS // 2, nq),
        out_shape=[jax.ShapeDtypeStruct((B, S, 512), F32), jax.ShapeDtypeStruct((B, MLA_HEADS, S, LANE), F32)],
        in_specs=[pl.BlockSpec((1, tq, 2 * LANE), lambda b, hp, i: (b, i, hp)),
                  pl.BlockSpec((1, S, 2 * LANE), lambda b, hp, i: (b, 0, hp)),
                  pl.BlockSpec((1, S, LANE), lambda b, hp, i: (b, 0, hp))],
        out_specs=[pl.BlockSpec((1, tq, LANE), lambda b, hp, i: (b, i, hp)),
                   pl.BlockSpec((1, 2, tq, LANE), lambda b, hp, i: (b, hp, i, 0))],
        compiler_params=_cparams(("arbitrary", "arbitrary", "arbitrary")),
    )(q3, k3, v3)


def _mla_bwd(q3, k3, v3, o3, do3, lse, tq, tk):
    B, S, _ = q3.shape
    nq, nk = S // tq, S // tk
    assert tk % tq == 0

    def body(q_ref, k_ref, v_ref, o_ref, do_ref, lse_ref, dq_ref, dk_ref, dv_ref, dkt_ref, dvt_ref):
        dkt_ref[...] = jnp.zeros_like(dkt_ref)
        dvt_ref[...] = jnp.zeros_like(dvt_ref)
        rows = lax.broadcasted_iota(jnp.int32, (tq, tk), 0)
        cols = lax.broadcasted_iota(jnp.int32, (tq, tk), 1)
        lane = _lane_iota((tq, LANE))

        def q_tile(qi, _):
            r = pl.ds(pl.multiple_of(qi * tq, tq), tq)
            do2, o2 = do_ref[0, r, :], o_ref[0, r, :]
            heads = []
            for j in range(2):
                lanes = slice(LANE * j, LANE * (j + 1))
                do = jnp.where((lane < 64) if j == 0 else (lane >= 64), do2, 0.0)
                q = q_ref[0, r, lanes]
                heads.append((lanes, q, _mx(q.astype(F32).T), _mx(do), _mx(do.T),
                              jnp.sum(do * o2, axis=1, keepdims=True),
                              jnp.concatenate([lse_ref[0, j, r, :]] * (tk // LANE), axis=1)))
            n_full = (qi * tq) // tk

            def k_tile(kt, dqs, masked):
                kr = pl.ds(pl.multiple_of(kt * tk, tk), tk)
                v2 = v_ref[0, kr, :]
                out = []
                dvt = None
                for j, (lanes, q, qt, dob, dot_, dcol, lse_c) in enumerate(heads):
                    k = k_ref[0, kr, lanes]
                    s = _dot_nt(q, k)
                    if masked:
                        s = jnp.where(rows + qi * tq >= cols + kt * tk, s, NEG_INF)
                    p = jnp.exp2(s - lse_c)
                    dp = _dot_nt(dob, v2)
                    dsb = _mx(p * (dp - dcol))
                    dkt_ref[j, kt] += _dot(qt, dsb)
                    pv = _dot(dot_, _mx(p))
                    dvt = pv if dvt is None else dvt + pv
                    out.append(dqs[j] + _dot(dsb, k))
                dvt_ref[kt] += dvt
                return tuple(out)

            zero = jnp.zeros((tq, LANE), F32)
            dqs = lax.fori_loop(0, n_full, functools.partial(k_tile, masked=False), (zero, zero))
            dqs = k_tile(n_full, dqs, True)
            for j in range(2):
                dq_ref[0, r, heads[j][0]] = MLA_SCALE * dqs[j]
            return 0

        lax.fori_loop(0, nq, q_tile, 0)

        def flush(kt, _):
            kr = pl.ds(pl.multiple_of(kt * tk, tk), tk)
            for j in range(2):
                dk_ref[0, kr, LANE * j:LANE * (j + 1)] = (1.0 / LOG2E) * dkt_ref[j, kt].T
            dv_ref[0, kr, :] = dvt_ref[kt].T
            return 0

        lax.fori_loop(0, nk, flush, 0)

    pair = lambda w: pl.BlockSpec((1, S, w), lambda b, hp: (b, 0, hp))
    return pl.pallas_call(
        body, name="mla_bwd", grid=(B, MLA_HEADS // 2),
        out_shape=[jax.ShapeDtypeStruct((B, S, 1024), F32), jax.ShapeDtypeStruct((B, S, 1024), F32),
                   jax.ShapeDtypeStruct((B, S, 512), F32)],
        in_specs=[pair(2 * LANE), pair(2 * LANE), pair(LANE), pair(LANE), pair(LANE),
                  pl.BlockSpec((1, 2, S, LANE), lambda b, hp: (b, hp, 0, 0))],
        out_specs=[pair(2 * LANE), pair(2 * LANE), pair(LANE)],
        scratch_shapes=[pltpu.VMEM((2, nk, LANE, tk), F32), pltpu.VMEM((nk, LANE, tk), F32)],
        compiler_params=_cparams(("arbitrary", "arbitrary")),
    )(q3, k3, v3, o3, do3, lse)


def _swa_consts(sink_ref):
    W = WINDOW
    row = lax.broadcasted_iota(jnp.int32, (4 * W, LANE), 0)
    out = []
    for g in range(SWA_KV_HEADS):
        slope = jnp.zeros((4 * W, LANE), F32)
        sink = jnp.zeros((4 * W, LANE), F32)
        for p in range(4):
            h = p + 4 * g
            here = jnp.logical_and(row >= W * p, row < W * (p + 1))
            slope = jnp.where(here, float(LOG2E * 2.0 ** (-8.0 * (h + 1) / SWA_HEADS)), slope)
            sink = jnp.where(here, LOG2E * sink_ref[:, h:h + 1], sink)
        out.append((jnp.concatenate([slope, slope], axis=1), sink))
    return out


def _wide(col):
    return jnp.concatenate([col, col], axis=1)


def _swa_block(q_ref, k_ref, v_ref, pb_ref, pr_ref, n, i, ij):
    W = WINDOW
    kb = jnp.maximum(n - 1, 0)
    r = pl.ds(pl.multiple_of(i * W, W), W)
    kr = pl.ds(pl.multiple_of(kb * W, W), 2 * W)
    q4, k2, v2 = q_ref[0, r, :], k_ref[0, kr, :], v_ref[0, kr, :]
    pq = _wide(pb_ref[0, r, :])
    pk = jnp.concatenate([pr_ref[0, pl.ds(kb, 1), :], pr_ref[0, pl.ds(kb + 1, 1), :]], axis=1)
    rel = ij + (n - kb) * W
    mask = jnp.where(jnp.logical_and(rel >= 0, rel < W), 0.0, NEG_INF)
    dist4 = jnp.concatenate([pq - pk] * 4, axis=0)
    mask4 = jnp.concatenate([mask] * 4, axis=0)
    return r, kb, kr, q4, k2, v2, dist4, mask4


def _swa_stack(x4, g, dtype):
    lane = _lane_iota((WINDOW, LANE))
    mine = (lane < 64) if g == 0 else (lane >= 64)
    return jnp.concatenate([jnp.where(mine, x4[:, LANE * p:LANE * (p + 1)], 0).astype(dtype) for p in range(4)], axis=0)


def _swa_unstack(ref, r, lo, hi, scale=None):
    W = WINDOW
    low = _lane_iota((W, LANE)) < 64
    for p in range(4):
        t = jnp.where(low, lo[W * p:W * (p + 1)], hi[W * p:W * (p + 1)])
        ref[0, r, LANE * p:LANE * (p + 1)] = t if scale is None else scale * t


def _swa_fwd(qs3, ks3, vs3, posb, posr, sinks):
    B, S, _ = qs3.shape
    W = WINDOW
    nb = S // W
    nh = SWA_SEQ_SPLIT
    nbh = nb // nh

    def body(q_ref, k_ref, v_ref, pb_ref, pr_ref, sink_ref, o_ref, lse_ref):
        ij = lax.broadcasted_iota(jnp.int32, (W, 2 * W), 0) - lax.broadcasted_iota(jnp.int32, (W, 2 * W), 1)
        consts = _swa_consts(sink_ref)
        n0 = pl.program_id(1) * nbh

        def blk(i, _):
            n = n0 + i
            r, _, _, q4, k2, v2, dist4, mask4 = _swa_block(q_ref, k_ref, v_ref, pb_ref, pr_ref, n, i, ij)
            o_g = []
            for g, (slope, sink) in enumerate(consts):
                s = _dot_nt(_swa_stack(q4, g, q4.dtype), k2) - slope * dist4 + mask4
                m = jnp.maximum(jnp.max(s, axis=1, keepdims=True), sink)
                e = jnp.exp2(s - _wide(m))
                l = jnp.sum(e, axis=1, keepdims=True) + jnp.exp2(sink - m)
                o_g.append(_dot(_mx(e), v2) * (1.0 / l))
                lse_ref[0, i, g] = m + jnp.log2(l)
            _swa_unstack(o_ref, r, o_g[0], o_g[1])
            return 0

        lax.fori_loop(0, nbh, blk, 0)

    seq = lambda w: pl.BlockSpec((1, S, w), lambda b, h: (b, 0, 0))
    part = lambda w: pl.BlockSpec((1, S // nh, w), lambda b, h: (b, h, 0))
    lse_spec = pl.BlockSpec((1, nbh, 2, 4 * W, LANE), lambda b, h: (b, h, 0, 0, 0))
    return pl.pallas_call(
        body, name="swa_fwd", grid=(B, nh),
        out_shape=[jax.ShapeDtypeStruct((B, S, 512), F32), jax.ShapeDtypeStruct((B, nb, 2, 4 * W, LANE), F32)],
        in_specs=[part(512), seq(LANE), seq(LANE), part(LANE), pl.BlockSpec((1, nb, W), lambda b, h: (b, 0, 0)),
                  pl.BlockSpec((1, LANE), lambda b, h: (0, 0))],
        out_specs=[part(512), lse_spec],
        compiler_params=_cparams(("arbitrary", "arbitrary")),
    )(qs3, ks3, vs3, posb, posr, sinks)


def _swa_bwd(qs3, ks3, vs3, posb, posr, sinks, os3, do3, lse):
    B, S, _ = qs3.shape
    W = WINDOW
    nb = S // W
    nh = SWA_SEQ_SPLIT
    nbh = nb // nh

    def body(q_ref, k_ref, v_ref, pb_ref, pr_ref, sink_ref, o_ref, do_ref, lse_ref, dq_ref, dk_ref, dv_ref, dsink_ref,
             dkt_ref, dvt_ref):
        lane1 = _lane_iota((1, LANE))
        ij = lax.broadcasted_iota(jnp.int32, (W, 2 * W), 0) - lax.broadcasted_iota(jnp.int32, (W, 2 * W), 1)
        consts = _swa_consts(sink_ref)
        hh = pl.program_id(1)
        n0 = hh * nbh

        @pl.when(hh == 0)
        def _():
            dkt_ref[...] = jnp.zeros_like(dkt_ref)
            dvt_ref[...] = jnp.zeros_like(dvt_ref)

        @pl.when(jnp.logical_and(pl.program_id(0) == 0, hh == 0))
        def _():
            dsink_ref[...] = jnp.zeros_like(dsink_ref)

        def blk(i, dsink):
            n = n0 + i
            r, kb, _, q4, k2, v2, dist4, mask4 = _swa_block(q_ref, k_ref, v_ref, pb_ref, pr_ref, n, i, ij)
            o4, do4 = o_ref[0, r, :], do_ref[0, r, :]
            dq_g = []
            dkt = jnp.zeros((LANE, 2 * W), F32)
            dvt = jnp.zeros((LANE, 2 * W), F32)
            for g, (slope, sink) in enumerate(consts):
                q_st = _swa_stack(q4, g, F32)
                do_st = _swa_stack(do4, g, F32)
                dcol = jnp.sum(do_st * _swa_stack(o4, g, F32), axis=1, keepdims=True)
                qst, dob = _mx(q_st), _mx(do_st)
                lse_c = lse_ref[0, i, g]
                pr = jnp.exp2(_dot_nt(qst, k2) - slope * dist4 + mask4 - _wide(lse_c))
                dsb = _mx(pr * (_dot_nt(dob, v2) - dcol))
                psd = jnp.exp2(sink - lse_c)[:, 0:1] * dcol
                for p in range(4):
                    dsink = dsink - jnp.where(lane1 == p + 4 * g,
                                              jnp.sum(psd[W * p:W * (p + 1)], axis=0, keepdims=True), 0.0)
                dq_g.append(_dot(dsb, k2))
                dkt = dkt + _dot(_mx(q_st.T), dsb)
                dvt = dvt + _dot(_mx(do_st.T), _mx(pr))
            _swa_unstack(dq_ref, r, dq_g[0], dq_g[1], SWA_SCALE)
            dkt_ref[kb] += dkt[:, 0:W]
            dkt_ref[kb + 1] += dkt[:, W:2 * W]
            dvt_ref[kb] += dvt[:, 0:W]
            dvt_ref[kb + 1] += dvt[:, W:2 * W]
            return dsink

        dsink_ref[...] += lax.fori_loop(0, nbh, blk, jnp.zeros((1, LANE), F32))

        @pl.when(hh == nh - 1)
        def _():
            def flush(n, _):
                r = pl.ds(pl.multiple_of(n * W, W), W)
                dk_ref[0, r, :] = (1.0 / LOG2E) * dkt_ref[n].T
                dv_ref[0, r, :] = dvt_ref[n].T
                return 0

            lax.fori_loop(0, nb, flush, 0)

    seq = lambda w: pl.BlockSpec((1, S, w), lambda b, h: (b, 0, 0))
    part = lambda w: pl.BlockSpec((1, S // nh, w), lambda b, h: (b, h, 0))
    return pl.pallas_call(
        body, name="swa_bwd", grid=(B, nh),
        out_shape=[jax.ShapeDtypeStruct((B, S, 512), F32), jax.ShapeDtypeStruct((B, S, LANE), F32),
                   jax.ShapeDtypeStruct((B, S, LANE), F32), jax.ShapeDtypeStruct((1, LANE), F32)],
        in_specs=[part(512), seq(LANE), seq(LANE), part(LANE), pl.BlockSpec((1, nb, W), lambda b, h: (b, 0, 0)),
                  pl.BlockSpec((1, LANE), lambda b, h: (0, 0)), part(512),
                  pl.BlockSpec((1, S // nh, 512), lambda b, h: (b, h, 1)),
                  pl.BlockSpec((1, nbh, 2, 4 * W, LANE), lambda b, h: (b, h, 0, 0, 0))],
        out_specs=[part(512), seq(LANE), seq(LANE), pl.BlockSpec((1, LANE), lambda b, h: (0, 0))],
        scratch_shapes=[pltpu.VMEM((nb, LANE, W), F32), pltpu.VMEM((nb, LANE, W), F32)],
        compiler_params=_cparams(("arbitrary", "arbitrary")),
    )(qs3, ks3, vs3, posb, posr, sinks, os3, do3, lse)


def _post(om, osw, g, w_out, x2, gate, fg, tgt, S, ts):
    T, D = x2.shape
    nsb = S // ts

    def body(om_ref, os_ref, g_ref, w_ref, x_ref, gate_ref, fg_ref, t_ref,
             dx2_ref, do_ref, dg_ref, loss_ref, dfg_ref, dgate_ref, dw_ref):
        i = pl.program_id(0)
        gv = g_ref[...]
        sg = 1.0 / (1.0 + jnp.exp(-gv))
        silu = gv * sg
        o = jnp.concatenate([om_ref[...], os_ref[...]], axis=1)
        ab = _mx(o * silu)
        y = _dot(ab, w_ref[...])
        gate = gate_ref[0]
        xo = x_ref[...] + gate * y
        r2 = lax.rsqrt(jnp.mean(xo * xo, axis=-1, keepdims=True) + EPS)
        xh = xo * r2
        fg = fg_ref[...]
        diff = xh * fg - t_ref[...]
        sq = jnp.sum(diff * diff, axis=0, keepdims=True)
        part = sq[:, 0:LANE]
        for t in range(1, D // LANE):
            part = part + sq[:, LANE * t:LANE * (t + 1)]
        dout = diff * (1.0 / D)
        dxh = dout * fg
        dx2 = r2 * (dxh - xh * jnp.mean(dxh * xh, axis=-1, keepdims=True))
        dx2_ref[...] = dx2
        dyb = _mx(dx2 * gate)
        da = _dot_nt(dyb, w_ref[...])
        do_ref[...] = da * silu
        dg_ref[...] = _mx(da * o * (sg * (1.0 + gv * (1.0 - sg))))

        @pl.when(i == 0)
        def _():
            loss_ref[...] = jnp.zeros_like(loss_ref)
            dfg_ref[...] = jnp.zeros_like(dfg_ref)
            dw_ref[...] = jnp.zeros_like(dw_ref)

        @pl.when(i % nsb == 0)
        def _():
            dgate_ref[...] = jnp.zeros_like(dgate_ref)

        loss_ref[...] += (0.5 / D) * part
        dfg_ref[...] += jnp.sum(dout * xh, axis=0, keepdims=True)
        dgate_ref[0] += jnp.sum(dx2 * y, axis=0, keepdims=True)
        dw_ref[...] += _dot_tn(ab, dyb)

    row = lambda w: pl.BlockSpec((ts, w), lambda i: (i, 0))
    full = lambda a: pl.BlockSpec(a.shape, lambda i: (0,) * a.ndim)
    per_b = pl.BlockSpec((1, 1, D), lambda i: (i // nsb, 0, 0))
    return pl.pallas_call(
        body, name="post", grid=(T // ts,),
        out_shape=[jax.ShapeDtypeStruct((T, D), F32), jax.ShapeDtypeStruct((T, 1024), F32),
                   jax.ShapeDtypeStruct((T, 1024), _MXU_DTYPE), jax.ShapeDtypeStruct((1, LANE), F32),
                   jax.ShapeDtypeStruct((1, D), F32), jax.ShapeDtypeStruct(gate.shape, F32),
                   jax.ShapeDtypeStruct(w_out.shape, F32)],
        in_specs=[row(512), row(512), row(1024), full(w_out), row(D), per_b, full(fg), row(D)],
        out_specs=[row(D), row(1024), row(1024),
                   pl.BlockSpec((1, LANE), lambda i: (0, 0)), pl.BlockSpec((1, D), lambda i: (0, 0)), per_b,
                   full(w_out)],
        compiler_params=_cparams(("arbitrary",)),
    )(om, osw, g, w_out, x2, gate, fg, tgt)


def _pre_bwd(dq, dk, dv, dqs, dks, dvs, dg, zq, zkv, ql, kvl, x2, dx2, scale, ng, w_in, gq, gkv, w_uq, w_uk, w_uv,
             rope, S, ts):
    T, D = x2.shape
    nsb = S // ts
    WQ = MLA_HEADS * LANE

    def body(dq_ref, dk_ref, dv_ref, dqs_ref, dks_ref, dvs_ref, dg_ref, zq_ref, zkv_ref, ql_ref, kvl_ref, x_ref, dx2_ref,
             sc_ref,
             ng_ref, w_ref, gq_ref, gkv_ref, wuq_ref, wuk_ref, wuv_ref, cd_ref, sd_ref,
             gx_ref, dz_ref, dgq_ref, dgkv_ref, dng_ref, dsh_ref, dsc_ref, dwuq_ref, dwuk_ref, dwuv_ref):
        i = pl.program_id(0)

        @pl.when(i == 0)
        def _():
            dgq_ref[...] = jnp.zeros_like(dgq_ref)
            dgkv_ref[...] = jnp.zeros_like(dgkv_ref)
            dng_ref[...] = jnp.zeros_like(dng_ref)
            dwuq_ref[...] = jnp.zeros_like(dwuq_ref)
            dwuk_ref[...] = jnp.zeros_like(dwuk_ref)
            dwuv_ref[...] = jnp.zeros_like(dwuv_ref)

        @pl.when(i % nsb == 0)
        def _():
            dsh_ref[...] = jnp.zeros_like(dsh_ref)
            dsc_ref[...] = jnp.zeros_like(dsc_ref)

        def norm_bwd(z, dy, gain):
            r = lax.rsqrt(jnp.mean(z * z, axis=-1, keepdims=True) + EPS)
            zh = z * r
            dzh = dy * gain
            return r * (dzh - zh * jnp.mean(dzh * zh, axis=-1, keepdims=True)), jnp.sum(dy * zh, axis=0, keepdims=True)

        tables = _rope_split(cd_ref[...], sd_ref[...])
        ng = ng_ref[...]
        sc1 = 1.0 + sc_ref[0]

        def rows_chain(rs):
            cq, ck, sa, sb = (t[rs] for t in tables)
            dqr = dq_ref[rs, :]
            dqb = _mx(dqr * _tile_heads(cq) + pltpu.roll(dqr * _tile_heads(sa), 16, 1)
                      + pltpu.roll(dqr * _tile_heads(sb), WQ - 16, 1))
            p_uq = _dot_tn(ql_ref[rs, :], dqb)
            dzq, dgq = norm_bwd(zq_ref[rs, :], _dot_nt(dqb, wuq_ref[...]), gq_ref[...])
            dkr = dk_ref[rs, :]
            dkb = _mx(dkr)
            p_uk = _dot_tn(kvl_ref[rs, :], dkb)
            dvb = _mx(dv_ref[rs, :])
            p_uv = _dot_tn(kvl_ref[rs, :], dvb)
            dzkv, dgkv = norm_bwd(zkv_ref[rs, :], _dot_nt(dkb, wuk_ref[...]) + _dot_nt(dvb, wuv_ref[...]), gkv_ref[...])
            dkpe = dkr[:, 0:LANE]
            for h in range(1, MLA_HEADS):
                dkpe = dkpe + dkr[:, LANE * h:LANE * (h + 1)]
            dkro = dkpe * ck + pltpu.roll(dkpe * sa, 16, 1) + pltpu.roll(dkpe * sb, LANE - 16, 1)
            dz_ref[rs, 0:384] = _mx(dzq)
            dz_ref[rs, 384:640] = _mx(dzkv)
            dz_ref[rs, 640:768] = _mx(dkro)
            dz_ref[rs, 768:1280] = dg_ref[rs, 0:512]
            dz_ref[rs, 1280:1792] = _mx(dqs_ref[rs, :])
            dz_ref[rs, 1792:1920] = _mx(dks_ref[rs, :])
            dz_ref[rs, 1920:2048] = _mx(dvs_ref[rs, :])
            dz_ref[rs, 2048:2560] = dg_ref[rs, 512:1024]
            dh = _dot_nt(dz_ref[rs, :], w_ref[...])
            x = x_ref[rs, :]
            r1 = lax.rsqrt(jnp.mean(x * x, axis=-1, keepdims=True) + EPS)
            xn = x * r1
            dxn = dh * ng * sc1
            gx_ref[rs, :] = dx2_ref[rs, :] + r1 * (dxn - xn * jnp.mean(dxn * xn, axis=-1, keepdims=True))
            return (p_uq, p_uk, p_uv, dgq, dgkv, jnp.sum(dh, axis=0, keepdims=True),
                    jnp.sum(dh * (xn * ng), axis=0, keepdims=True), jnp.sum(dh * xn * sc1, axis=0, keepdims=True))

        hr = ts // 2
        parts = [rows_chain(slice(hr * t, hr * (t + 1))) for t in range(2)]
        p_uq, p_uk, p_uv, dgq, dgkv, dsh, dsc, dng = (a + b for a, b in zip(*parts))
        dwuq_ref[...] += p_uq
        dwuk_ref[...] += p_uk
        dwuv_ref[...] += p_uv
        dgq_ref[...] += dgq
        dgkv_ref[...] += dgkv
        dsh_ref[0] += dsh
        dsc_ref[0] += dsc
        dng_ref[...] += dng

    row = lambda w: pl.BlockSpec((ts, w), lambda i: (i, 0))
    full = lambda a: pl.BlockSpec(a.shape, lambda i: (0,) * a.ndim, pipeline_mode=pl.Buffered(1))
    per_b = pl.BlockSpec((1, 1, D), lambda i: (i // nsb, 0, 0))
    dense = pl.BlockSpec((ts // 8, LANE), lambda i: (i, 0))
    vec = lambda w: pl.BlockSpec((1, w), lambda i: (0, 0))
    return pl.pallas_call(
        body, name="pre_bwd", grid=(T // ts,),
        out_shape=[jax.ShapeDtypeStruct((T, D), F32), jax.ShapeDtypeStruct((T, D_IN_PAD), _MXU_DTYPE), jax.ShapeDtypeStruct((1, 384), F32),
                   jax.ShapeDtypeStruct((1, 256), F32), jax.ShapeDtypeStruct((1, D), F32),
                   jax.ShapeDtypeStruct(scale.shape, F32), jax.ShapeDtypeStruct(scale.shape, F32),
                   jax.ShapeDtypeStruct(w_uq.shape, F32), jax.ShapeDtypeStruct(w_uk.shape, F32),
                   jax.ShapeDtypeStruct(w_uv.shape, F32)],
        in_specs=[row(WQ), row(WQ), row(512), row(512), row(LANE), row(LANE), row(1024), row(384), row(256), row(384),
                  row(256), row(D), row(D), per_b, full(ng), full(w_in), full(gq), full(gkv), full(w_uq), full(w_uk), full(w_uv),
                  dense, dense],
        out_specs=[row(D), row(D_IN_PAD), vec(384), vec(256), vec(D), per_b, per_b, full(w_uq), full(w_uk), full(w_uv)],
        compiler_params=_cparams(("arbitrary",)),
    )(dq, dk, dv, dqs, dks, dvs, dg, zq, zkv, ql, kvl, x2, dx2, scale, ng, w_in, gq, gkv, w_uq, w_uk, w_uv, *rope)


def _tn_matmul(a, b, tn, tk, name):
    T, M = a.shape
    N = b.shape[1]

    def body(a_ref, b_ref, o_ref):
        @pl.when(pl.program_id(1) == 0)
        def _():
            o_ref[...] = jnp.zeros_like(o_ref)

        o_ref[...] += _dot_tn(a_ref[...], b_ref[...])

    return pl.pallas_call(
        body, name=name, grid=(N // tn, T // tk),
        out_shape=jax.ShapeDtypeStruct((M, N), F32),
        in_specs=[pl.BlockSpec((tk, M), lambda j, k: (k, 0)), pl.BlockSpec((tk, tn), lambda j, k: (k, j))],
        out_specs=pl.BlockSpec((M, tn), lambda j, k: (0, j)),
        compiler_params=_cparams(("arbitrary", "arbitrary")),
    )(a, b)


def _finalize(parts_all, dmod_all, dmod_cols, c_act_all):
    nparts = parts_all.shape[-1]
    nmod = dmod_all.shape[-1]

    def body(p_ref, dm_ref, dmc_ref, c_ref, ps_ref, loss_ref, db_ref, dw_ref):
        acc = p_ref[0]
        for j in range(1, N_DEV):
            acc = acc + p_ref[j]
        ps_ref[...] = acc
        loss_ref[...] = jnp.sum(acc[:, 0:LANE], axis=1, keepdims=True)
        db = dm_ref[0:1, :]
        for j in range(1, dm_ref.shape[0]):
            db = db + dm_ref[j:j + 1, :]
        db_ref[...] = db
        dw_ref[...] = _dot_tn(_mx(_silu(c_ref[...])), _mx(dmc_ref[...]))

    return pl.pallas_call(
        body, name="finalize",
        out_shape=[jax.ShapeDtypeStruct((1, nparts), F32), jax.ShapeDtypeStruct((1, 1), F32),
                   jax.ShapeDtypeStruct((1, nmod), F32),
                   jax.ShapeDtypeStruct((c_act_all.shape[1], dmod_cols.shape[1]), F32)],
        in_specs=[_vmem()] * 4, out_specs=[_vmem()] * 4,
        compiler_params=_cparams(),
    )(parts_all, dmod_all, dmod_cols, c_act_all)


def _adamw(ws, gs, ms, vs):
    n = len(ws)

    def body(*refs):
        w_refs, g_refs, m_refs, v_refs, d_refs, nm_refs, nv_refs = (refs[t * n:(t + 1) * n] for t in range(7))
        for t in range(n):
            gv = g_refs[t][...]
            nm = ADAM_B1 * m_refs[t][...] + (1.0 - ADAM_B1) * gv
            nv = ADAM_B2 * v_refs[t][...] + (1.0 - ADAM_B2) * (gv * gv)
            m_hat = nm / (1.0 - ADAM_B1 ** ADAM_STEP)
            v_hat = nv / (1.0 - ADAM_B2 ** ADAM_STEP)
            d_refs[t][...] = -ADAM_LR * (m_hat / (jnp.sqrt(v_hat) + ADAM_EPS) + ADAM_WD * w_refs[t][...])
            nm_refs[t][...] = nm
            nv_refs[t][...] = nv

    out = pl.pallas_call(
        body, name="adamw",
        out_shape=[jax.ShapeDtypeStruct(w.shape, F32) for w in ws] * 3,
        in_specs=[_vmem()] * (4 * n), out_specs=[_vmem()] * (3 * n),
        compiler_params=_cparams(),
    )(*ws, *gs, *ms, *vs)
    return out[:n], out[n:2 * n], out[2 * n:]


def _pair_perm(a, axis, order):
    a = jnp.moveaxis(a, axis, -1)
    lead = a.shape[:-1]
    a = a.reshape(lead + (8, 64))[..., list(order), :].reshape(lead + (512,))
    return jnp.moveaxis(a, -1, axis)


def _pad_w_in(w):
    z = lambda n: jnp.zeros((w.shape[0], n), w.dtype)
    return jnp.concatenate([
        w[:, 0:640], z(64), w[:, 640:672], z(32), w[:, 672:1184],
        _pair_perm(w[:, 1184:1696], 1, PAIR_ORDER), w[:, 1696:1952], _pair_perm(w[:, 1952:2464], 1, PAIR_ORDER)], axis=1)


def _unpad_w_in_t(g):
    return jnp.concatenate([
        g[0:640], g[704:736], g[768:1280],
        _pair_perm(g[1280:1792], 0, PAIR_INV), g[1792:2048], _pair_perm(g[2048:2560], 0, PAIR_INV)], axis=0)


def _rope_table(positions):
    T = positions.size
    inv = ROPE_THETA ** (-jnp.arange(0, MLA_ROPE, 2, dtype=F32) / MLA_ROPE)
    pos = jnp.repeat(positions.reshape(T // 8, 8)[:, ::-1].astype(F32), MLA_ROPE // 2, axis=1)
    ang = pos * jnp.tile(inv, 8)[None, :]
    return jnp.cos(ang), jnp.sin(ang)


def _local_step(x, mod, positions, ng, w_in, gq, gkv, w_uq, w_ukv, sinks, w_out, fg, tgt,
                ts=512, fq=512, fk=512, bq=512, bk=512):
    B, S, D = x.shape
    T = B * S
    x2 = x.reshape(T, D)
    shift, scale, gate = (mod[:, None, k * D:(k + 1) * D] for k in range(3))
    w_in_p = _pad_w_in(w_in)
    w_uq_p = jnp.pad(w_uq.reshape(Q_LORA, MLA_HEADS, 96), ((0, 0), (0, 0), (0, 32))).reshape(Q_LORA, MLA_HEADS * LANE)
    w_ukv3 = w_ukv.reshape(KV_LORA, MLA_HEADS, 128)
    w_uk_p = jnp.pad(w_ukv3[:, :, :64], ((0, 0), (0, 0), (0, 64))).reshape(KV_LORA, MLA_HEADS * LANE)
    w_uv = w_ukv3[:, :, 64:].reshape(KV_LORA, 512)
    w_out_p = jnp.concatenate([w_out[:512], _pair_perm(w_out[512:], 0, PAIR_ORDER)], axis=0)
    rope = _rope_table(positions)
    posf = positions.astype(F32)
    posb = jnp.broadcast_to(posf[:, :, None], (B, S, LANE))
    posr = posf.reshape(B, S // WINDOW, WINDOW)
    sinks_l = jnp.pad(sinks.reshape(1, SWA_HEADS), ((0, 0), (0, LANE - SWA_HEADS)))

    (hb, zq, zkv, ql, kvl, q, k, v, qs, ks, vs, g) = _pre_fwd(
        x2, shift, scale, ng, w_in_p, gq, gkv, w_uq_p, w_uk_p, w_uv, rope, S, ts)
    r3 = lambda a: a.reshape(B, S, a.shape[-1])
    om, lse_m = _mla_fwd(r3(q), r3(k), r3(v), fq, fk)
    osw, lse_s = _swa_fwd(r3(qs), r3(ks), r3(vs), posb, posr, sinks_l)
    dx2, do, dg, loss_v, dfg, dgate, dw_out_p = _post(
        om.reshape(T, 512), osw.reshape(T, 512), g, w_out_p, x2, gate, fg.reshape(1, D), tgt.reshape(T, D), S, ts)
    do3 = r3(do)
    dq, dk, dv = _mla_bwd(r3(q), r3(k), r3(v), om, do3, lse_m, bq, bk)
    dqs, dks, dvs, dsink = _swa_bwd(r3(qs), r3(ks), r3(vs), posb, posr, sinks_l, osw, do3, lse_s)
    f2 = lambda a: a.reshape(T, a.shape[-1])
    gx, dz, dgq, dgkv, dng, dsh, dsc, dw_uq_p, dw_uk_p, dw_uv = _pre_bwd(
        f2(dq), f2(dk), f2(dv), f2(dqs), f2(dks), f2(dvs), dg, zq, zkv, ql, kvl, x2, dx2, scale, ng, w_in_p, gq, gkv,
        w_uq_p, w_uk_p, w_uv, rope, S, ts)
    tk = min(T, 1024)
    dw_in_t = _unpad_w_in_t(_tn_matmul(dz, hb, 512, tk, "dw_in"))
    dw_out = jnp.concatenate([dw_out_p[:512], _pair_perm(dw_out_p[512:], 0, PAIR_INV)], axis=0)
    dw_uq = dw_uq_p.reshape(Q_LORA, MLA_HEADS, LANE)[:, :, :96].reshape(Q_LORA, 768)
    dw_uk = dw_uk_p.reshape(KV_LORA, MLA_HEADS, LANE)[:, :, :64]
    dw_uv = dw_uv.reshape(KV_LORA, MLA_HEADS, 64)
    dw_ukv = jnp.concatenate([dw_uk, dw_uv], axis=2).reshape(KV_LORA, 1024)
    parts = jnp.concatenate([loss_v, dfg, dng, dgq, dgkv, dsink], axis=1)
    dmod = jnp.concatenate([dsh, dsc, dgate], axis=2).reshape(B, 3 * D)
    return gx.reshape(B, S, D), dw_in_t, dw_uq, dw_ukv, dw_out, parts, dmod


def kernel(x, c, positions, w_ada, b_ada, norm_gain, w_in, q_norm_gain, kv_norm_gain, w_uq, w_ukv, swa_sinks, w_out, final_gain, loss_target, m_w_ada, m_b_ada, m_norm_gain, m_w_in, m_q_norm_gain, m_kv_norm_gain, m_w_uq, m_w_ukv, m_swa_sinks, m_w_out, m_final_gain, v_w_ada, v_b_ada, v_norm_gain, v_w_in, v_q_norm_gain, v_kv_norm_gain, v_w_uq, v_w_ukv, v_swa_sinks, v_w_out, v_final_gain):
    B, S, D = x.shape
    me = 4 * lax.axis_index("x") + 2 * lax.axis_index("y") + lax.axis_index("c")
    bf = _MXU_DTYPE

    c_all, win_g, wuq_g, wukv_g, wout_g = _all_gather(
        [c, w_in[0].astype(bf), w_uq[0].astype(bf), w_ukv[0].astype(bf), w_out[0].astype(bf)], "ag_weights")
    c_all = c_all.reshape(N_DEV * B, D)
    cat_cols = lambda a: jnp.transpose(a, (1, 0, 2)).reshape(a.shape[1], N_DEV * a.shape[2])
    w_in_f, w_uq_f, w_ukv_f = cat_cols(win_g), cat_cols(wuq_g), cat_cols(wukv_g)
    w_out_f = wout_g.reshape(D, D)

    ncol = w_ada.shape[2]
    b_cols = lax.dynamic_slice_in_dim(b_ada, me * ncol, ncol, axis=1)
    (mod_g,) = _all_gather([_ada_fwd(c_all, w_ada[0], b_cols)], "ag_mod")
    mod = lax.dynamic_slice_in_dim(mod_g, me * B, B, axis=1)
    mod = jnp.transpose(mod, (1, 0, 2)).reshape(B, 3 * D)

    gx, dw_in_t, dw_uq, dw_ukv, dw_out, parts, dmod = _local_step(
        x, mod, positions, norm_gain, w_in_f, q_norm_gain, kv_norm_gain, w_uq_f, w_ukv_f, swa_sinks,
        w_out_f, final_gain, loss_target)

    parts_g, dmod_g = _all_gather([parts, dmod], "ag_small")
    dmod_all = dmod_g.reshape(N_DEV * B, 3 * D)
    dmod_cols = lax.dynamic_slice_in_dim(dmod_all, me * ncol, ncol, axis=1)
    psum, loss, g_b_ada, g_w_ada = _finalize(parts_g, dmod_all, dmod_cols, c_all)
    loss = loss.reshape(())
    o = LANE
    g_final_gain = psum[0, o:o + D]
    g_norm_gain = psum[:, o + D:o + 2 * D]
    o += 2 * D
    g_q_norm_gain = psum[:, o:o + Q_LORA]
    g_kv_norm_gain = psum[:, o + Q_LORA:o + Q_LORA + KV_LORA]
    o += Q_LORA + KV_LORA
    g_sinks = psum[:, o:o + SWA_HEADS]

    split_cols = lambda a: jnp.transpose(a.reshape(a.shape[0], 4, 2, a.shape[1] // N_DEV), (1, 2, 0, 3))
    split_rows = lambda a: a.reshape(4, 2, a.shape[0] // N_DEV, a.shape[1])
    g_w_in_t, g_w_uq, g_w_ukv, g_w_out = _reduce_scatter(
        [split_rows(dw_in_t), split_cols(dw_uq), split_cols(dw_ukv), split_rows(dw_out)], "rs_grads")
    g_w_in = g_w_in_t.T

    grads = [g_w_ada, g_b_ada, g_norm_gain, g_w_in, g_q_norm_gain, g_kv_norm_gain, g_w_uq, g_w_ukv, g_sinks, g_w_out,
             g_final_gain]
    ws = [w_ada, b_ada, norm_gain, w_in, q_norm_gain, kv_norm_gain, w_uq, w_ukv, swa_sinks, w_out, final_gain]
    ms = [m_w_ada, m_b_ada, m_norm_gain, m_w_in, m_q_norm_gain, m_kv_norm_gain, m_w_uq, m_w_ukv, m_swa_sinks, m_w_out,
          m_final_gain]
    vs = [v_w_ada, v_b_ada, v_norm_gain, v_w_in, v_q_norm_gain, v_kv_norm_gain, v_w_uq, v_w_ukv, v_swa_sinks, v_w_out,
          v_final_gain]
    two_d = [(1, w.shape[0]) if w.ndim == 1 else w.shape[-2:] for w in ws]
    flat = lambda arrs: [a.reshape(s) for a, s in zip(arrs, two_d)]
    deltas, new_ms, new_vs = _adamw(flat(ws), flat(grads), flat(ms), flat(vs))
    shaped = lambda arrs: [a.reshape(w.shape) for a, w in zip(arrs, ws)]
    return (loss, gx, *shaped(grads), *shaped(deltas), *shaped(new_ms), *shaped(new_vs))
```

```python
import functools

import numpy as np
import jax
import jax.numpy as jnp
from jax import lax
from jax.experimental import pallas as pl
from jax.experimental.pallas import tpu as pltpu

F32 = jnp.float32
_MXU_DTYPE = jnp.bfloat16

N_DEV = 8
D_MODEL = 1024
MLA_HEADS = 8
MLA_NOPE = 64
MLA_ROPE = 32
MLA_V = 64
Q_LORA = 384
KV_LORA = 256
SWA_HEADS = 8
SWA_KV_HEADS = 2
SWA_HEAD_DIM = 64
WINDOW = 128
ROPE_THETA = 10000.0
EPS = 1e-6
MLA_SCALE = float((MLA_NOPE + MLA_ROPE) ** -0.5)
SWA_SCALE = float(SWA_HEAD_DIM ** -0.5)
LOG2E = 1.4426950408889634
MLA_QSCALE = MLA_SCALE * LOG2E
D_IN = 2464
D_IN_PAD = 2560
PAIR_ORDER = (0, 4, 1, 5, 2, 6, 3, 7)
PAIR_INV = (0, 2, 4, 6, 1, 3, 5, 7)

ADAM_LR = 0.001
ADAM_B1 = 0.9
ADAM_B2 = 0.999
ADAM_EPS = 1e-08
ADAM_WD = 0.01
ADAM_STEP = 10

LANE = 128
VMEM_LIMIT = 56 * 1024 * 1024

MESH = pl.DeviceIdType.MESH
NEG_INF = float("-inf")
SWA_SEQ_SPLIT = 2


def _mx(a):
    return a.astype(_MXU_DTYPE)


def _dot(a, b):
    return jnp.dot(a, b, preferred_element_type=F32)


def _dot_nt(a, b):
    return lax.dot_general(a, b, (((1,), (1,)), ((), ())), preferred_element_type=F32)


def _dot_tn(a, b):
    return lax.dot_general(a, b, (((0,), (0,)), ((), ())), preferred_element_type=F32)


def _cparams(sem=None):
    return pltpu.CompilerParams(dimension_semantics=sem, vmem_limit_bytes=VMEM_LIMIT)


def _vmem():
    return pl.BlockSpec(memory_space=pltpu.VMEM)


def _lane_iota(shape):
    return lax.broadcasted_iota(jnp.int32, shape, len(shape) - 1)


def _all_gather(arrs, name, fused=None):
    n = len(arrs)
    extra, fn, piece = fused if fused else ((), None, None)
    ne, m = len(extra), n + (1 if fused else 0)

    def body(*refs):
        ins, ex, outs = refs[:n], refs[n:n + ne], refs[n + ne:n + ne + m]
        rest = refs[n + ne + m:]
        srcs = list(ins) + ([rest[0]] if fused else [])
        send_sems, recv_sems, local_sems = rest[-3:]
        x, y, c = lax.axis_index("x"), lax.axis_index("y"), lax.axis_index("c")
        me, sibling = (x, y, c), (x, y, 1 - c)
        chips = [(1 - x, y), (x, 1 - y), (1 - x, 1 - y)]

        def slot(a, dev):
            return outs[a].at[4 * dev[0] + 2 * dev[1] + dev[2]]

        def copy(a, k, block, to, src=None):
            return pltpu.make_async_remote_copy(
                src_ref=slot(a, block) if src is None else src, dst_ref=slot(a, block),
                send_sem=send_sems.at[7 * a + k], recv_sem=recv_sems.at[7 * a + k],
                device_id=to, device_id_type=MESH)

        def local(a):
            return pltpu.make_async_copy(srcs[a], slot(a, me), local_sems.at[a])

        def start(a):
            local(a).start()
            cps = [copy(a, 0, me, sibling, src=srcs[a])]
            cps += [copy(a, 1 + j, me, (*chip, c), src=srcs[a]) for j, chip in enumerate(chips)]
            for cp in cps:
                cp.start()
            return cps

        def finish(group):
            cps = []
            for j, chip in enumerate(chips):
                for a in group:
                    copy(a, 1 + j, (*chip, c), me).wait_recv()
                    cp = copy(a, 4 + j, (*chip, c), sibling)
                    cp.start()
                    cps.append(cp)
            for a in group:
                copy(a, 0, sibling, me).wait_recv()
                for j, chip in enumerate(chips):
                    copy(a, 4 + j, (*chip, 1 - c), me).wait_recv()
                local(a).wait()
            return cps

        pending = []
        for a in range(n):
            pending += start(a)
        if fused:
            pending += finish([0])
            fn(srcs[n], outs[0], *ex)
            pending += start(n)
            pending += finish([n] + list(range(1, n)))
        else:
            pending += finish(list(range(n)))
        for cp in pending:
            cp.wait_send()

    out_shape = [jax.ShapeDtypeStruct((N_DEV,) + a.shape, a.dtype) for a in arrs]
    scratch = []
    if fused:
        out_shape.append(jax.ShapeDtypeStruct((N_DEV,) + piece.shape, piece.dtype))
        scratch.append(pltpu.VMEM(piece.shape, piece.dtype))
    return pl.pallas_call(
        body, name=name, out_shape=out_shape,
        in_specs=[_vmem()] * (n + ne), out_specs=[_vmem()] * m,
        scratch_shapes=scratch + [pltpu.SemaphoreType.DMA((7 * m,)), pltpu.SemaphoreType.DMA((7 * m,)),
                                  pltpu.SemaphoreType.DMA((m,))],
        compiler_params=pltpu.CompilerParams(vmem_limit_bytes=VMEM_LIMIT),
    )(*arrs, *extra)


def _reduce_scatter(arrs, name):
    n = len(arrs)
    halves = [a.astype(jnp.bfloat16) for a in arrs]

    def body(*refs):
        xs, xbs, outs = refs[:n], refs[n:2 * n], refs[2 * n:3 * n]
        parts, recv_a, send_b, recv_b = (refs[(3 + t) * n:(4 + t) * n] for t in range(4))
        send_sems, recv_sems, local_sems = refs[7 * n:]
        x, y, c = lax.axis_index("x"), lax.axis_index("y"), lax.axis_index("c")
        myq = 2 * x + y

        def chip(k):
            return (1 - x if k & 2 else x, 1 - y if k & 1 else y)

        def from_sibling(a):
            return pltpu.make_async_remote_copy(
                src_ref=xbs[a].at[:, 1 - c], dst_ref=recv_a[a], send_sem=send_sems.at[4 * a], recv_sem=recv_sems.at[4 * a],
                device_id=(x, y, 1 - c), device_id_type=MESH)

        def to_owner(a, k):
            qx, qy = chip(k)
            return pltpu.make_async_remote_copy(
                src_ref=send_b[a].at[2 * qx + qy], dst_ref=recv_b[a].at[myq],
                send_sem=send_sems.at[4 * a + k], recv_sem=recv_sems.at[4 * a + k],
                device_id=(qx, qy, c), device_id_type=MESH)

        mine = [pltpu.make_async_copy(xs[a].at[:, c], parts[a], local_sems.at[a]) for a in range(n)]
        first = [from_sibling(a) for a in range(n)]
        for cp in mine + first:
            cp.start()
        second = []
        for a in range(n):
            mine[a].wait()
            first[a].wait_recv()
            parts[a][...] = parts[a][...] + recv_a[a][...].astype(F32)
            send_b[a][...] = parts[a][...].astype(jnp.bfloat16)
            for k in range(1, 4):
                cp = to_owner(a, k)
                cp.start()
                second.append(cp)
        for a in range(n):
            acc = parts[a][myq]
            for k in range(1, 4):
                to_owner(a, k).wait_recv()
                qx, qy = chip(k)
                acc = acc + recv_b[a][2 * qx + qy].astype(F32)
            outs[a][...] = acc
        for cp in first + second:
            cp.wait_send()

    quarter = lambda a, dt: pltpu.VMEM((4,) + a.shape[2:], dt)
    return pl.pallas_call(
        body, name=name,
        out_shape=[jax.ShapeDtypeStruct(a.shape[2:], F32) for a in arrs],
        in_specs=[pl.BlockSpec(memory_space=pl.ANY)] * (2 * n), out_specs=[_vmem()] * n,
        scratch_shapes=[quarter(a, F32) for a in arrs] + [quarter(a, jnp.bfloat16) for a in arrs] * 3
        + [pltpu.SemaphoreType.DMA((4 * n,)), pltpu.SemaphoreType.DMA((4 * n,)), pltpu.SemaphoreType.DMA((n,))],
        compiler_params=pltpu.CompilerParams(vmem_limit_bytes=VMEM_LIMIT),
    )(*arrs, *halves)


def _silu(t):
    return t * (1.0 / (1.0 + jnp.exp(-t)))


def _ada_cols(piece_ref, c_all_ref, w_ref, b_ref):
    w = _mx(w_ref[...])
    for d in range(N_DEV):
        piece_ref[d] = _dot(_mx(_silu(c_all_ref[d])), w) + b_ref[...]


def _rope_split(cd, sd):
    n = cd.shape[0]

    def expand(d):
        rep = jnp.broadcast_to(d[:, None, :], (n, 8, LANE)).reshape(8 * n, LANE)
        return pltpu.roll(rep, 0, 1, stride=16, stride_axis=0)

    c, s = expand(cd), expand(sd)
    lane = _lane_iota(c.shape)
    first = jnp.logical_and(lane >= 64, lane < 80)
    second = jnp.logical_and(lane >= 80, lane < 96)
    ck = jnp.where(first, pltpu.roll(c, 80, 1), jnp.where(second, pltpu.roll(c, 96, 1), 0.0))
    cq = jnp.where(lane < 64, 1.0, ck)
    sa = jnp.where(first, -pltpu.roll(s, 80, 1), 0.0)
    sb = jnp.where(second, pltpu.roll(s, 96, 1), 0.0)
    return cq, ck, sa, sb


def _tile_heads(t):
    return jnp.concatenate([t] * MLA_HEADS, axis=1)


def _pre_fwd(x2, shift, scale, ng, w_in, gq, gkv, w_uq, w_uk, w_uv, rope, S, ts):
    T, D = x2.shape
    nsb = S // ts
    WQ = MLA_HEADS * LANE

    def body(x_ref, sh_ref, sc_ref, ng_ref, w_ref, gq_ref, gkv_ref, wuq_ref, wuk_ref, wuv_ref,
             cd_ref, sd_ref,
             hb_ref, zq_ref, zkv_ref, ql_ref, kvl_ref, q_ref, k_ref, v_ref, qs_ref, ks_ref, vs_ref, g_ref):
        x = x_ref[...]
        r1 = lax.rsqrt(jnp.mean(x * x, axis=-1, keepdims=True) + EPS)
        h = ((x * r1) * ng_ref[...]) * (1.0 + sc_ref[0]) + sh_ref[0]
        hb = _mx(h)
        hb_ref[...] = hb
        zq = _dot(hb, w_ref[:, 0:384])
        zq_ref[...] = zq
        rq = lax.rsqrt(jnp.mean(zq * zq, axis=-1, keepdims=True) + EPS)
        ql = _mx((zq * rq) * gq_ref[...])
        ql_ref[...] = ql
        q = _dot(ql, wuq_ref[...])
        cq, ck, sa, sb = _rope_split(cd_ref[...], sd_ref[...])
        q = (q * _tile_heads(cq) + pltpu.roll(q, WQ - 16, 1) * _tile_heads(sa)
             + pltpu.roll(q, 16, 1) * _tile_heads(sb))
        q_ref[...] = _mx(q * MLA_QSCALE)
        zkv = _dot(hb, w_ref[:, 384:640])
        zkv_ref[...] = zkv
        rkv = lax.rsqrt(jnp.mean(zkv * zkv, axis=-1, keepdims=True) + EPS)
        kvl = _mx((zkv * rkv) * gkv_ref[...])
        kvl_ref[...] = kvl
        kr = _dot(hb, w_ref[:, 640:768])
        kpe = kr * ck + pltpu.roll(kr, LANE - 16, 1) * sa + pltpu.roll(kr, 16, 1) * sb
        kf = _dot(kvl, wuk_ref[...])
        k_ref[...] = _mx(kf + jnp.concatenate([kpe] * MLA_HEADS, axis=1))
        v_ref[...] = _mx(_dot(kvl, wuv_ref[...]))
        g_ref[:, 0:512] = _dot(hb, w_ref[:, 768:1280])
        qs_ref[...] = _mx(_dot(hb, w_ref[:, 1280:1792]) * (SWA_SCALE * LOG2E))
        ks_ref[...] = _mx(_dot(hb, w_ref[:, 1792:1920]))
        vs_ref[...] = _mx(_dot(hb, w_ref[:, 1920:2048]))
        g_ref[:, 512:1024] = _dot(hb, w_ref[:, 2048:2560])

    row = lambda w: pl.BlockSpec((ts, w), lambda i: (i, 0))
    dense = pl.BlockSpec((ts // 8, LANE), lambda i: (i, 0))
    full = lambda a: pl.BlockSpec(a.shape, lambda i: (0,) * a.ndim)
    per_b = pl.BlockSpec((1, 1, D), lambda i: (i // nsb, 0, 0))
    out_w = [(D, _MXU_DTYPE), (384, F32), (256, F32), (384, _MXU_DTYPE), (256, _MXU_DTYPE), (WQ, _MXU_DTYPE),
             (WQ, _MXU_DTYPE), (512, _MXU_DTYPE), (512, _MXU_DTYPE), (128, _MXU_DTYPE), (128, _MXU_DTYPE), (1024, F32)]
    return pl.pallas_call(
        body, name="pre_fwd", grid=(T // ts,),
        out_shape=[jax.ShapeDtypeStruct((T, w), dt) for w, dt in out_w],
        in_specs=[row(D), per_b, per_b, full(ng), full(w_in), full(gq), full(gkv), full(w_uq), full(w_uk), full(w_uv),
                  dense, dense],
        out_specs=[row(w) for w, _ in out_w],
        compiler_params=_cparams(("arbitrary",)),
    )(x2, shift, scale, ng, w_in, gq, gkv, w_uq, w_uk, w_uv, *rope)


def _mla_fwd(q3, k3, v3, tq, tk):
    B, S, _ = q3.shape
    nq = S // tq
    assert tk % tq == 0 or tq % tk == 0
    n_masked = max(1, tq // tk)

    def body(q_ref, k_ref, v_ref, o_ref, lse_ref):
        qi = pl.program_id(2)
        rows = lax.broadcasted_iota(jnp.int32, (tq, tk), 0)
        cols = lax.broadcasted_iota(jnp.int32, (tq, tk), 1)
        qs = [q_ref[0, :, LANE * j:LANE * (j + 1)] for j in range(2)]
        n_full = (qi * tq) // tk

        low_k = _lane_iota((tk, LANE)) < 64

        def step(kt, carry, masked):
            r0 = pl.multiple_of(kt * tk, tk)
            v2 = v_ref[0, pl.ds(r0, tk), :]
            vj = (jnp.where(low_k, v2, 1.0).astype(v2.dtype), jnp.where(low_k, 1.0, v2).astype(v2.dtype))
            out = []
            for j in range(2):
                m, acc = carry[j]
                s = _dot_nt(qs[j], k_ref[0, pl.ds(r0, tk), LANE * j:LANE * (j + 1)])
                if masked:
                    s = jnp.where(rows + qi * tq >= cols + kt * tk, s, NEG_INF)
                m_new = jnp.maximum(m, jnp.max(s, axis=1, keepdims=True))
                alpha = jnp.exp2(m - m_new)
                p = jnp.exp2(s - m_new)
                acc = alpha * acc + _dot(_mx(p), vj[j])
                out.append((m_new, acc))
            return tuple(out)

        init = (jnp.full((tq, 1), NEG_INF, F32), jnp.zeros((tq, LANE), F32))
        carry = lax.fori_loop(0, n_full, functools.partial(step, masked=False), (init, init))
        for t in range(n_masked):
            carry = step(n_full + t, carry, True)
        (m0, a0), (m1, a1) = carry
        low = _lane_iota((tq, LANE)) < 64
        l0 = jnp.where(low, pltpu.roll(a0, 64, 1), a0)
        l1 = jnp.where(low, a1, pltpu.roll(a1, 64, 1))
        o_ref[0] = jnp.where(low, a0 / l0, a1 / l1)
        lse_ref[0, 0] = m0 + jnp.log2(l0)
        lse_ref[0, 1] = m1 + jnp.log2(l1)

    return pl.pallas_call(
        body, name="mla_fwd", grid=(B, MLA_HEADS // 2, nq),
        out_shape=[jax.ShapeDtypeStruct((B, S, 512), F32), jax.ShapeDtypeStruct((B, MLA_HEADS, S, LANE), F32)],
        in_specs=[pl.BlockSpec((1, tq, 2 * LANE), lambda b, hp, i: (b, i, hp)),
                  pl.BlockSpec((1, S, 2 * LANE), lambda b, hp, i: (b, 0, hp)),
                  pl.BlockSpec((1, S, LANE), lambda b, hp, i: (b, 0, hp))],
        out_specs=[pl.BlockSpec((1, tq, LANE), lambda b, hp, i: (b, i, hp)),
                   pl.BlockSpec((1, 2, tq, LANE), lambda b, hp, i: (b, hp, i, 0))],
        compiler_params=_cparams(("arbitrary", "arbitrary", "arbitrary")),
    )(q3, k3, v3)


def _mla_bwd(q3, k3, v3, o3, do3, lse, tq, tk):
    B, S, _ = q3.shape
    nq, nk = S // tq, S // tk
    assert tk % tq == 0

    def body(q_ref, k_ref, v_ref, o_ref, do_ref, lse_ref, dq_ref, dk_ref, dv_ref, dkt_ref, dvt_ref):
        dkt_ref[...] = jnp.zeros_like(dkt_ref)
        dvt_ref[...] = jnp.zeros_like(dvt_ref)
        rows = lax.broadcasted_iota(jnp.int32, (tq, tk), 0)
        cols = lax.broadcasted_iota(jnp.int32, (tq, tk), 1)
        lane = _lane_iota((tq, LANE))

        def q_tile(qi, _):
            r = pl.ds(pl.multiple_of(qi * tq, tq), tq)
            do2, o2 = do_ref[0, r, :], o_ref[0, r, :]
            heads = []
            for j in range(2):
                lanes = slice(LANE * j, LANE * (j + 1))
                do = jnp.where((lane < 64) if j == 0 else (lane >= 64), do2, 0.0)
                q = q_ref[0, r, lanes]
                heads.append((lanes, q, _mx(q.astype(F32).T), _mx(do), _mx(do.T),
                              jnp.sum(do * o2, axis=1, keepdims=True),
                              jnp.concatenate([lse_ref[0, j, r, :]] * (tk // LANE), axis=1)))
            n_full = (qi * tq) // tk

            def k_tile(kt, dqs, masked):
                kr = pl.ds(pl.multiple_of(kt * tk, tk), tk)
                v2 = v_ref[0, kr, :]
                out = []
                dvt = None
                for j, (lanes, q, qt, dob, dot_, dcol, lse_c) in enumerate(heads):
                    k = k_ref[0, kr, lanes]
                    s = _dot_nt(q, k)
                    if masked:
                        s = jnp.where(rows + qi * tq >= cols + kt * tk, s, NEG_INF)
                    p = jnp.exp2(s - lse_c)
                    dp = _dot_nt(dob, v2)
                    dsb = _mx(p * (dp - dcol))
                    dkt_ref[j, kt] += _dot(qt, dsb)
                    pv = _dot(dot_, _mx(p))
                    dvt = pv if dvt is None else dvt + pv
                    out.append(dqs[j] + _dot(dsb, k))
                dvt_ref[kt] += dvt
                return tuple(out)

            zero = jnp.zeros((tq, LANE), F32)
            dqs = lax.fori_loop(0, n_full, functools.partial(k_tile, masked=False), (zero, zero))
            dqs = k_tile(n_full, dqs, True)
            for j in range(2):
                dq_ref[0, r, heads[j][0]] = MLA_SCALE * dqs[j]
            return 0

        lax.fori_loop(0, nq, q_tile, 0)

        def flush(kt, _):
            kr = pl.ds(pl.multiple_of(kt * tk, tk), tk)
            for j in range(2):
                dk_ref[0, kr, LANE * j:LANE * (j + 1)] = (1.0 / LOG2E) * dkt_ref[j, kt].T
            dv_ref[0, kr, :] = dvt_ref[kt].T
            return 0

        lax.fori_loop(0, nk, flush, 0)

    pair = lambda w: pl.BlockSpec((1, S, w), lambda b, hp: (b, 0, hp))
    return pl.pallas_call(
        body, name="mla_bwd", grid=(B, MLA_HEADS // 2),
        out_shape=[jax.ShapeDtypeStruct((B, S, 1024), F32), jax.ShapeDtypeStruct((B, S, 1024), F32),
                   jax.ShapeDtypeStruct((B, S, 512), F32)],
        in_specs=[pair(2 * LANE), pair(2 * LANE), pair(LANE), pair(LANE), pair(LANE),
                  pl.BlockSpec((1, 2, S, LANE), lambda b, hp: (b, hp, 0, 0))],
        out_specs=[pair(2 * LANE), pair(2 * LANE), pair(LANE)],
        scratch_shapes=[pltpu.VMEM((2, nk, LANE, tk), F32), pltpu.VMEM((nk, LANE, tk), F32)],
        compiler_params=_cparams(("arbitrary", "arbitrary")),
    )(q3, k3, v3, o3, do3, lse)


def _swa_consts(sink_ref):
    W = WINDOW
    row = lax.broadcasted_iota(jnp.int32, (4 * W, LANE), 0)
    out = []
    for g in range(SWA_KV_HEADS):
        slope = jnp.zeros((4 * W, LANE), F32)
        sink = jnp.zeros((4 * W, LANE), F32)
        for p in range(4):
            h = p + 4 * g
            here = jnp.logical_and(row >= W * p, row < W * (p + 1))
            slope = jnp.where(here, float(LOG2E * 2.0 ** (-8.0 * (h + 1) / SWA_HEADS)), slope)
            sink = jnp.where(here, LOG2E * sink_ref[:, h:h + 1], sink)
        out.append((jnp.concatenate([slope, slope], axis=1), sink))
    return out


def _wide(col):
    return jnp.concatenate([col, col], axis=1)


def _swa_block(q_ref, k_ref, v_ref, pb_ref, pr_ref, n, i, ij):
    W = WINDOW
    kb = jnp.maximum(n - 1, 0)
    r = pl.ds(pl.multiple_of(i * W, W), W)
    kr = pl.ds(pl.multiple_of(kb * W, W), 2 * W)
    q4, k2, v2 = q_ref[0, r, :], k_ref[0, kr, :], v_ref[0, kr, :]
    pq = _wide(pb_ref[0, r, :])
    pk = jnp.concatenate([pr_ref[0, pl.ds(kb, 1), :], pr_ref[0, pl.ds(kb + 1, 1), :]], axis=1)
    rel = ij + (n - kb) * W
    mask = jnp.where(jnp.logical_and(rel >= 0, rel < W), 0.0, NEG_INF)
    dist4 = jnp.concatenate([pq - pk] * 4, axis=0)
    mask4 = jnp.concatenate([mask] * 4, axis=0)
    return r, kb, kr, q4, k2, v2, dist4, mask4


def _swa_stack(x4, g, dtype):
    lane = _lane_iota((WINDOW, LANE))
    mine = (lane < 64) if g == 0 else (lane >= 64)
    return jnp.concatenate([jnp.where(mine, x4[:, LANE * p:LANE * (p + 1)], 0).astype(dtype) for p in range(4)], axis=0)


def _swa_unstack(ref, r, lo, hi, scale=None):
    W = WINDOW
    low = _lane_iota((W, LANE)) < 64
    for p in range(4):
        t = jnp.where(low, lo[W * p:W * (p + 1)], hi[W * p:W * (p + 1)])
        ref[0, r, LANE * p:LANE * (p + 1)] = t if scale is None else scale * t


def _swa_fwd(qs3, ks3, vs3, posb, posr, sinks):
    B, S, _ = qs3.shape
    W = WINDOW
    nb = S // W
    nh = SWA_SEQ_SPLIT
    nbh = nb // nh

    def body(q_ref, k_ref, v_ref, pb_ref, pr_ref, sink_ref, o_ref, lse_ref):
        ij = lax.broadcasted_iota(jnp.int32, (W, 2 * W), 0) - lax.broadcasted_iota(jnp.int32, (W, 2 * W), 1)
        consts = _swa_consts(sink_ref)
        n0 = pl.program_id(1) * nbh

        def blk(i, _):
            n = n0 + i
            r, _, _, q4, k2, v2, dist4, mask4 = _swa_block(q_ref, k_ref, v_ref, pb_ref, pr_ref, n, i, ij)
            o_g = []
            for g, (slope, sink) in enumerate(consts):
                s = _dot_nt(_swa_stack(q4, g, q4.dtype), k2) - slope * dist4 + mask4
                m = jnp.maximum(jnp.max(s, axis=1, keepdims=True), sink)
                e = jnp.exp2(s - _wide(m))
                l = jnp.sum(e, axis=1, keepdims=True) + jnp.exp2(sink - m)
                o_g.append(_dot(_mx(e), v2) * (1.0 / l))
                lse_ref[0, i, g] = m + jnp.log2(l)
            _swa_unstack(o_ref, r, o_g[0], o_g[1])
            return 0

        lax.fori_loop(0, nbh, blk, 0)

    seq = lambda w: pl.BlockSpec((1, S, w), lambda b, h: (b, 0, 0))
    part = lambda w: pl.BlockSpec((1, S // nh, w), lambda b, h: (b, h, 0))
    lse_spec = pl.BlockSpec((1, nbh, 2, 4 * W, LANE), lambda b, h: (b, h, 0, 0, 0))
    return pl.pallas_call(
        body, name="swa_fwd", grid=(B, nh),
        out_shape=[jax.ShapeDtypeStruct((B, S, 512), F32), jax.ShapeDtypeStruct((B, nb, 2, 4 * W, LANE), F32)],
        in_specs=[part(512), seq(LANE), seq(LANE), part(LANE), pl.BlockSpec((1, nb, W), lambda b, h: (b, 0, 0)),
                  pl.BlockSpec((1, LANE), lambda b, h: (0, 0))],
        out_specs=[part(512), lse_spec],
        compiler_params=_cparams(("arbitrary", "arbitrary")),
    )(qs3, ks3, vs3, posb, posr, sinks)


def _swa_bwd(qs3, ks3, vs3, posb, posr, sinks, os3, do3, lse):
    B, S, _ = qs3.shape
    W = WINDOW
    nb = S // W
    nh = SWA_SEQ_SPLIT
    nbh = nb // nh

    def body(q_ref, k_ref, v_ref, pb_ref, pr_ref, sink_ref, o_ref, do_ref, lse_ref, dq_ref, dk_ref, dv_ref, dsink_ref,
             dkt_ref, dvt_ref):
        lane1 = _lane_iota((1, LANE))
        ij = lax.broadcasted_iota(jnp.int32, (W, 2 * W), 0) - lax.broadcasted_iota(jnp.int32, (W, 2 * W), 1)
        consts = _swa_consts(sink_ref)
        hh = pl.program_id(1)
        n0 = hh * nbh

        @pl.when(hh == 0)
        def _():
            dkt_ref[...] = jnp.zeros_like(dkt_ref)
            dvt_ref[...] = jnp.zeros_like(dvt_ref)

        @pl.when(jnp.logical_and(pl.program_id(0) == 0, hh == 0))
        def _():
            dsink_ref[...] = jnp.zeros_like(dsink_ref)

        def blk(i, dsink):
            n = n0 + i
            r, kb, _, q4, k2, v2, dist4, mask4 = _swa_block(q_ref, k_ref, v_ref, pb_ref, pr_ref, n, i, ij)
            o4, do4 = o_ref[0, r, :], do_ref[0, r, :]
            dq_g = []
            dkt = jnp.zeros((LANE, 2 * W), F32)
            dvt = jnp.zeros((LANE, 2 * W), F32)
            for g, (slope, sink) in enumerate(consts):
                q_st = _swa_stack(q4, g, F32)
                do_st = _swa_stack(do4, g, F32)
                dcol = jnp.sum(do_st * _swa_stack(o4, g, F32), axis=1, keepdims=True)
                qst, dob = _mx(q_st), _mx(do_st)
                lse_c = lse_ref[0, i, g]
                pr = jnp.exp2(_dot_nt(qst, k2) - slope * dist4 + mask4 - _wide(lse_c))
                dsb = _mx(pr * (_dot_nt(dob, v2) - dcol))
                psd = jnp.exp2(sink - lse_c)[:, 0:1] * dcol
                for p in range(4):
                    dsink = dsink - jnp.where(lane1 == p + 4 * g,
                                              jnp.sum(psd[W * p:W * (p + 1)], axis=0, keepdims=True), 0.0)
                dq_g.append(_dot(dsb, k2))
                dkt = dkt + _dot(_mx(q_st.T), dsb)
                dvt = dvt + _dot(_mx(do_st.T), _mx(pr))
            _swa_unstack(dq_ref, r, dq_g[0], dq_g[1], SWA_SCALE)
            dkt_ref[kb] += dkt[:, 0:W]
            dkt_ref[kb + 1] += dkt[:, W:2 * W]
            dvt_ref[kb] += dvt[:, 0:W]
            dvt_ref[kb + 1] += dvt[:, W:2 * W]
            return dsink

        dsink_ref[...] += lax.fori_loop(0, nbh, blk, jnp.zeros((1, LANE), F32))

        @pl.when(hh == nh - 1)
        def _():
            def flush(n, _):
                r = pl.ds(pl.multiple_of(n * W, W), W)
                dk_ref[0, r, :] = (1.0 / LOG2E) * dkt_ref[n].T
                dv_ref[0, r, :] = dvt_ref[n].T
                return 0

            lax.fori_loop(0, nb, flush, 0)

    seq = lambda w: pl.BlockSpec((1, S, w), lambda b, h: (b, 0, 0))
    part = lambda w: pl.BlockSpec((1, S // nh, w), lambda b, h: (b, h, 0))
    return pl.pallas_call(
        body, name="swa_bwd", grid=(B, nh),
        out_shape=[jax.ShapeDtypeStruct((B, S, 512), F32), jax.ShapeDtypeStruct((B, S, LANE), F32),
                   jax.ShapeDtypeStruct((B, S, LANE), F32), jax.ShapeDtypeStruct((1, LANE), F32)],
        in_specs=[part(512), seq(LANE), seq(LANE), part(LANE), pl.BlockSpec((1, nb, W), lambda b, h: (b, 0, 0)),
                  pl.BlockSpec((1, LANE), lambda b, h: (0, 0)), part(512),
                  pl.BlockSpec((1, S // nh, 512), lambda b, h: (b, h, 1)),
                  pl.BlockSpec((1, nbh, 2, 4 * W, LANE), lambda b, h: (b, h, 0, 0, 0))],
        out_specs=[part(512), seq(LANE), seq(LANE), pl.BlockSpec((1, LANE), lambda b, h: (0, 0))],
        scratch_shapes=[pltpu.VMEM((nb, LANE, W), F32), pltpu.VMEM((nb, LANE, W), F32)],
        compiler_params=_cparams(("arbitrary", "arbitrary")),
    )(qs3, ks3, vs3, posb, posr, sinks, os3, do3, lse)


def _post(om, osw, g, w_out, x2, gate, fg, tgt, S, ts):
    T, D = x2.shape
    nsb = S // ts

    def body(om_ref, os_ref, g_ref, w_ref, x_ref, gate_ref, fg_ref, t_ref,
             dx2_ref, do_ref, dg_ref, loss_ref, dfg_ref, dgate_ref, dw_ref):
        i = pl.program_id(0)
        gv = g_ref[...]
        sg = 1.0 / (1.0 + jnp.exp(-gv))
        silu = gv * sg
        o = jnp.concatenate([om_ref[...], os_ref[...]], axis=1)
        ab = _mx(o * silu)
        y = _dot(ab, w_ref[...])
        gate = gate_ref[0]
        xo = x_ref[...] + gate * y
        r2 = lax.rsqrt(jnp.mean(xo * xo, axis=-1, keepdims=True) + EPS)
        xh = xo * r2
        fg = fg_ref[...]
        diff = xh * fg - t_ref[...]
        sq = jnp.sum(diff * diff, axis=0, keepdims=True)
        part = sq[:, 0:LANE]
        for t in range(1, D // LANE):
            part = part + sq[:, LANE * t:LANE * (t + 1)]
        dout = diff * (1.0 / D)
        dxh = dout * fg
        dx2 = r2 * (dxh - xh * jnp.mean(dxh * xh, axis=-1, keepdims=True))
        dx2_ref[...] = dx2
        dyb = _mx(dx2 * gate)
        da = _dot_nt(dyb, w_ref[...])
        do_ref[...] = da * silu
        dg_ref[...] = _mx(da * o * (sg * (1.0 + gv * (1.0 - sg))))

        @pl.when(i == 0)
        def _():
            loss_ref[...] = jnp.zeros_like(loss_ref)
            dfg_ref[...] = jnp.zeros_like(dfg_ref)
            dw_ref[...] = jnp.zeros_like(dw_ref)

        @pl.when(i % nsb == 0)
        def _():
            dgate_ref[...] = jnp.zeros_like(dgate_ref)

        loss_ref[...] += (0.5 / D) * part
        dfg_ref[...] += jnp.sum(dout * xh, axis=0, keepdims=True)
        dgate_ref[0] += jnp.sum(dx2 * y, axis=0, keepdims=True)
        dw_ref[...] += _dot_tn(ab, dyb)

    row = lambda w: pl.BlockSpec((ts, w), lambda i: (i, 0))
    full = lambda a: pl.BlockSpec(a.shape, lambda i: (0,) * a.ndim)
    per_b = pl.BlockSpec((1, 1, D), lambda i: (i // nsb, 0, 0))
    return pl.pallas_call(
        body, name="post", grid=(T // ts,),
        out_shape=[jax.ShapeDtypeStruct((T, D), F32), jax.ShapeDtypeStruct((T, 1024), F32),
                   jax.ShapeDtypeStruct((T, 1024), _MXU_DTYPE), jax.ShapeDtypeStruct((1, LANE), F32),
                   jax.ShapeDtypeStruct((1, D), F32), jax.ShapeDtypeStruct(gate.shape, F32),
                   jax.ShapeDtypeStruct(w_out.shape, F32)],
        in_specs=[row(512), row(512), row(1024), full(w_out), row(D), per_b, full(fg), row(D)],
        out_specs=[row(D), row(1024), row(1024),
                   pl.BlockSpec((1, LANE), lambda i: (0, 0)), pl.BlockSpec((1, D), lambda i: (0, 0)), per_b,
                   full(w_out)],
        compiler_params=_cparams(("arbitrary",)),
    )(om, osw, g, w_out, x2, gate, fg, tgt)


def _pre_bwd(dq, dk, dv, dqs, dks, dvs, dg, zq, zkv, ql, kvl, x2, dx2, scale, ng, w_in, gq, gkv, w_uq, w_uk, w_uv,
             rope, S, ts):
    T, D = x2.shape
    nsb = S // ts
    WQ = MLA_HEADS * LANE

    def body(dq_ref, dk_ref, dv_ref, dqs_ref, dks_ref, dvs_ref, dg_ref, zq_ref, zkv_ref, ql_ref, kvl_ref, x_ref, dx2_ref,
             sc_ref,
             ng_ref, w_ref, gq_ref, gkv_ref, wuq_ref, wuk_ref, wuv_ref, cd_ref, sd_ref,
             gx_ref, dz_ref, dgq_ref, dgkv_ref, dng_ref, dsh_ref, dsc_ref, dwuq_ref, dwuk_ref, dwuv_ref):
        i = pl.program_id(0)

        @pl.when(i == 0)
        def _():
            dgq_ref[...] = jnp.zeros_like(dgq_ref)
            dgkv_ref[...] = jnp.zeros_like(dgkv_ref)
            dng_ref[...] = jnp.zeros_like(dng_ref)
            dwuq_ref[...] = jnp.zeros_like(dwuq_ref)
            dwuk_ref[...] = jnp.zeros_like(dwuk_ref)
            dwuv_ref[...] = jnp.zeros_like(dwuv_ref)

        @pl.when(i % nsb == 0)
        def _():
            dsh_ref[...] = jnp.zeros_like(dsh_ref)
            dsc_ref[...] = jnp.zeros_like(dsc_ref)

        def norm_bwd(z, dy, gain):
            r = lax.rsqrt(jnp.mean(z * z, axis=-1, keepdims=True) + EPS)
            zh = z * r
            dzh = dy * gain
            return r * (dzh - zh * jnp.mean(dzh * zh, axis=-1, keepdims=True)), jnp.sum(dy * zh, axis=0, keepdims=True)

        tables = _rope_split(cd_ref[...], sd_ref[...])
        ng = ng_ref[...]
        sc1 = 1.0 + sc_ref[0]

        def rows_chain(rs):
            cq, ck, sa, sb = (t[rs] for t in tables)
            dqr = dq_ref[rs, :]
            dqb = _mx(dqr * _tile_heads(cq) + pltpu.roll(dqr * _tile_heads(sa), 16, 1)
                      + pltpu.roll(dqr * _tile_heads(sb), WQ - 16, 1))
            p_uq = _dot_tn(ql_ref[rs, :], dqb)
            dzq, dgq = norm_bwd(zq_ref[rs, :], _dot_nt(dqb, wuq_ref[...]), gq_ref[...])
            dkr = dk_ref[rs, :]
            dkb = _mx(dkr)
            p_uk = _dot_tn(kvl_ref[rs, :], dkb)
            dvb = _mx(dv_ref[rs, :])
            p_uv = _dot_tn(kvl_ref[rs, :], dvb)
            dzkv, dgkv = norm_bwd(zkv_ref[rs, :], _dot_nt(dkb, wuk_ref[...]) + _dot_nt(dvb, wuv_ref[...]), gkv_ref[...])
            dkpe = dkr[:, 0:LANE]
            for h in range(1, MLA_HEADS):
                dkpe = dkpe + dkr[:, LANE * h:LANE * (h + 1)]
            dkro = dkpe * ck + pltpu.roll(dkpe * sa, 16, 1) + pltpu.roll(dkpe * sb, LANE - 16, 1)
            dz_ref[rs, 0:384] = _mx(dzq)
            dz_ref[rs, 384:640] = _mx(dzkv)
            dz_ref[rs, 640:768] = _mx(dkro)
            dz_ref[rs, 768:1280] = dg_ref[rs, 0:512]
            dz_ref[rs, 1280:1792] = _mx(dqs_ref[rs, :])
            dz_ref[rs, 1792:1920] = _mx(dks_ref[rs, :])
            dz_ref[rs, 1920:2048] = _mx(dvs_ref[rs, :])
            dz_ref[rs, 2048:2560] = dg_ref[rs, 512:1024]
            dh = _dot_nt(dz_ref[rs, :], w_ref[...])
            x = x_ref[rs, :]
            r1 = lax.rsqrt(jnp.mean(x * x, axis=-1, keepdims=True) + EPS)
            xn = x * r1
            dxn = dh * ng * sc1
            gx_ref[rs, :] = dx2_ref[rs, :] + r1 * (dxn - xn * jnp.mean(dxn * xn, axis=-1, keepdims=True))
            return (p_uq, p_uk, p_uv, dgq, dgkv, jnp.sum(dh, axis=0, keepdims=True),
                    jnp.sum(dh * (xn * ng), axis=0, keepdims=True), jnp.sum(dh * xn * sc1, axis=0, keepdims=True))

        hr = ts // 2
        parts = [rows_chain(slice(hr * t, hr * (t + 1))) for t in range(2)]
        p_uq, p_uk, p_uv, dgq, dgkv, dsh, dsc, dng = (a + b for a, b in zip(*parts))
        dwuq_ref[...] += p_uq
        dwuk_ref[...] += p_uk
        dwuv_ref[...] += p_uv
        dgq_ref[...] += dgq
        dgkv_ref[...] += dgkv
        dsh_ref[0] += dsh
        dsc_ref[0] += dsc
        dng_ref[...] += dng

    row = lambda w: pl.BlockSpec((ts, w), lambda i: (i, 0))
    full = lambda a: pl.BlockSpec(a.shape, lambda i: (0,) * a.ndim, pipeline_mode=pl.Buffered(1))
    per_b = pl.BlockSpec((1, 1, D), lambda i: (i // nsb, 0, 0))
    dense = pl.BlockSpec((ts // 8, LANE), lambda i: (i, 0))
    vec = lambda w: pl.BlockSpec((1, w), lambda i: (0, 0))
    return pl.pallas_call(
        body, name="pre_bwd", grid=(T // ts,),
        out_shape=[jax.ShapeDtypeStruct((T, D), F32), jax.ShapeDtypeStruct((T, D_IN_PAD), _MXU_DTYPE), jax.ShapeDtypeStruct((1, 384), F32),
                   jax.ShapeDtypeStruct((1, 256), F32), jax.ShapeDtypeStruct((1, D), F32),
                   jax.ShapeDtypeStruct(scale.shape, F32), jax.ShapeDtypeStruct(scale.shape, F32),
                   jax.ShapeDtypeStruct(w_uq.shape, F32), jax.ShapeDtypeStruct(w_uk.shape, F32),
                   jax.ShapeDtypeStruct(w_uv.shape, F32)],
        in_specs=[row(WQ), row(WQ), row(512), row(512), row(LANE), row(LANE), row(1024), row(384), row(256), row(384),
                  row(256), row(D), row(D), per_b, full(ng), full(w_in), full(gq), full(gkv), full(w_uq), full(w_uk), full(w_uv),
                  dense, dense],
        out_specs=[row(D), row(D_IN_PAD), vec(384), vec(256), vec(D), per_b, per_b, full(w_uq), full(w_uk), full(w_uv)],
        compiler_params=_cparams(("arbitrary",)),
    )(dq, dk, dv, dqs, dks, dvs, dg, zq, zkv, ql, kvl, x2, dx2, scale, ng, w_in, gq, gkv, w_uq, w_uk, w_uv, *rope)


def _tn_matmul(a, b, tn, tk, name):
    T, M = a.shape
    N = b.shape[1]

    def body(a_ref, b_ref, o_ref):
        @pl.when(pl.program_id(1) == 0)
        def _():
            o_ref[...] = jnp.zeros_like(o_ref)

        o_ref[...] += _dot_tn(a_ref[...], b_ref[...])

    return pl.pallas_call(
        body, name=name, grid=(N // tn, T // tk),
        out_shape=jax.ShapeDtypeStruct((M, N), F32),
        in_specs=[pl.BlockSpec((tk, M), lambda j, k: (k, 0)), pl.BlockSpec((tk, tn), lambda j, k: (k, j))],
        out_specs=pl.BlockSpec((M, tn), lambda j, k: (0, j)),
        compiler_params=_cparams(("arbitrary", "arbitrary")),
    )(a, b)


def _finalize(parts_all, dmod_all, dmod_cols, c_act_all):
    nparts = parts_all.shape[-1]
    nmod = dmod_all.shape[-1]

    def body(p_ref, dm_ref, dmc_ref, c_ref, ps_ref, loss_ref, db_ref, dw_ref):
        acc = p_ref[0]
        for j in range(1, N_DEV):
            acc = acc + p_ref[j]
        ps_ref[...] = acc
        loss_ref[...] = jnp.sum(acc[:, 0:LANE], axis=1, keepdims=True)
        db = dm_ref[0:1, :]
        for j in range(1, dm_ref.shape[0]):
            db = db + dm_ref[j:j + 1, :]
        db_ref[...] = db
        dw_ref[...] = _dot_tn(_mx(_silu(c_ref[...])), _mx(dmc_ref[...]))

    return pl.pallas_call(
        body, name="finalize",
        out_shape=[jax.ShapeDtypeStruct((1, nparts), F32), jax.ShapeDtypeStruct((1, 1), F32),
                   jax.ShapeDtypeStruct((1, nmod), F32),
                   jax.ShapeDtypeStruct((c_act_all.shape[1], dmod_cols.shape[1]), F32)],
        in_specs=[_vmem()] * 4, out_specs=[_vmem()] * 4,
        compiler_params=_cparams(),
    )(parts_all, dmod_all, dmod_cols, c_act_all)


def _adamw(ws, gs, ms, vs):
    n = len(ws)

    def body(*refs):
        w_refs, g_refs, m_refs, v_refs, d_refs, nm_refs, nv_refs = (refs[t * n:(t + 1) * n] for t in range(7))
        for t in range(n):
            gv = g_refs[t][...]
            nm = ADAM_B1 * m_refs[t][...] + (1.0 - ADAM_B1) * gv
            nv = ADAM_B2 * v_refs[t][...] + (1.0 - ADAM_B2) * (gv * gv)
            m_hat = nm / (1.0 - ADAM_B1 ** ADAM_STEP)
            v_hat = nv / (1.0 - ADAM_B2 ** ADAM_STEP)
            d_refs[t][...] = -ADAM_LR * (m_hat / (jnp.sqrt(v_hat) + ADAM_EPS) + ADAM_WD * w_refs[t][...])
            nm_refs[t][...] = nm
            nv_refs[t][...] = nv

    out = pl.pallas_call(
        body, name="adamw",
        out_shape=[jax.ShapeDtypeStruct(w.shape, F32) for w in ws] * 3,
        in_specs=[_vmem()] * (4 * n), out_specs=[_vmem()] * (3 * n),
        compiler_params=_cparams(),
    )(*ws, *gs, *ms, *vs)
    return out[:n], out[n:2 * n], out[2 * n:]


def _pair_perm(a, axis, order):
    a = jnp.moveaxis(a, axis, -1)
    lead = a.shape[:-1]
    a = a.reshape(lead + (8, 64))[..., list(order), :].reshape(lead + (512,))
    return jnp.moveaxis(a, -1, axis)


def _pad_w_in(w):
    z = lambda n: jnp.zeros((w.shape[0], n), w.dtype)
    return jnp.concatenate([
        w[:, 0:640], z(64), w[:, 640:672], z(32), w[:, 672:1184],
        _pair_perm(w[:, 1184:1696], 1, PAIR_ORDER), w[:, 1696:1952], _pair_perm(w[:, 1952:2464], 1, PAIR_ORDER)], axis=1)


def _unpad_w_in_t(g):
    return jnp.concatenate([
        g[0:640], g[704:736], g[768:1280],
        _pair_perm(g[1280:1792], 0, PAIR_INV), g[1792:2048], _pair_perm(g[2048:2560], 0, PAIR_INV)], axis=0)


def _rope_table(positions):
    T = positions.size
    inv = ROPE_THETA ** (-jnp.arange(0, MLA_ROPE, 2, dtype=F32) / MLA_ROPE)
    pos = jnp.repeat(positions.reshape(T // 8, 8)[:, ::-1].astype(F32), MLA_ROPE // 2, axis=1)
    ang = pos * jnp.tile(inv, 8)[None, :]
    return jnp.cos(ang), jnp.sin(ang)


def _local_step(x, mod, positions, ng, w_in, gq, gkv, w_uq, w_ukv, sinks, w_out, fg, tgt,
                ts=512, fq=512, fk=512, bq=512, bk=512):
    B, S, D = x.shape
    T = B * S
    x2 = x.reshape(T, D)
    shift, scale, gate = (mod[:, None, k * D:(k + 1) * D] for k in range(3))
    w_in_p = _pad_w_in(w_in)
    w_uq_p = jnp.pad(w_uq.reshape(Q_LORA, MLA_HEADS, 96), ((0, 0), (0, 0), (0, 32))).reshape(Q_LORA, MLA_HEADS * LANE)
    w_ukv3 = w_ukv.reshape(KV_LORA, MLA_HEADS, 128)
    w_uk_p = jnp.pad(w_ukv3[:, :, :64], ((0, 0), (0, 0), (0, 64))).reshape(KV_LORA, MLA_HEADS * LANE)
    w_uv = w_ukv3[:, :, 64:].reshape(KV_LORA, 512)
    w_out_p = jnp.concatenate([w_out[:512], _pair_perm(w_out[512:], 0, PAIR_ORDER)], axis=0)
    rope = _rope_table(positions)
    posf = positions.astype(F32)
    posb = jnp.broadcast_to(posf[:, :, None], (B, S, LANE))
    posr = posf.reshape(B, S // WINDOW, WINDOW)
    sinks_l = jnp.pad(sinks.reshape(1, SWA_HEADS), ((0, 0), (0, LANE - SWA_HEADS)))

    (hb, zq, zkv, ql, kvl, q, k, v, qs, ks, vs, g) = _pre_fwd(
        x2, shift, scale, ng, w_in_p, gq, gkv, w_uq_p, w_uk_p, w_uv, rope, S, ts)
    r3 = lambda a: a.reshape(B, S, a.shape[-1])
    om, lse_m = _mla_fwd(r3(q), r3(k), r3(v), fq, fk)
    osw, lse_s = _swa_fwd(r3(qs), r3(ks), r3(vs), posb, posr, sinks_l)
    dx2, do, dg, loss_v, dfg, dgate, dw_out_p = _post(
        om.reshape(T, 512), osw.reshape(T, 512), g, w_out_p, x2, gate, fg.reshape(1, D), tgt.reshape(T, D), S, ts)
    do3 = r3(do)
    dq, dk, dv = _mla_bwd(r3(q), r3(k), r3(v), om, do3, lse_m, bq, bk)
    dqs, dks, dvs, dsink = _swa_bwd(r3(qs), r3(ks), r3(vs), posb, posr, sinks_l, osw, do3, lse_s)
    f2 = lambda a: a.reshape(T, a.shape[-1])
    gx, dz, dgq, dgkv, dng, dsh, dsc, dw_uq_p, dw_uk_p, dw_uv = _pre_bwd(
        f2(dq), f2(dk), f2(dv), f2(dqs), f2(dks), f2(dvs), dg, zq, zkv, ql, kvl, x2, dx2, scale, ng, w_in_p, gq, gkv,
        w_uq_p, w_uk_p, w_uv, rope, S, ts)
    tk = min(T, 1024)
    dw_in_t = _unpad_w_in_t(_tn_matmul(dz, hb, 512, tk, "dw_in"))
    dw_out = jnp.concatenate([dw_out_p[:512], _pair_perm(dw_out_p[512:], 0, PAIR_INV)], axis=0)
    dw_uq = dw_uq_p.reshape(Q_LORA, MLA_HEADS, LANE)[:, :, :96].reshape(Q_LORA, 768)
    dw_uk = dw_uk_p.reshape(KV_LORA, MLA_HEADS, LANE)[:, :, :64]
    dw_uv = dw_uv.reshape(KV_LORA, MLA_HEADS, 64)
    dw_ukv = jnp.concatenate([dw_uk, dw_uv], axis=2).reshape(KV_LORA, 1024)
    parts = jnp.concatenate([loss_v, dfg, dng, dgq, dgkv, dsink], axis=1)
    dmod = jnp.concatenate([dsh, dsc, dgate], axis=2).reshape(B, 3 * D)
    return gx.reshape(B, S, D), dw_in_t, dw_uq, dw_ukv, dw_out, parts, dmod


def kernel(x, c, positions, w_ada, b_ada, norm_gain, w_in, q_norm_gain, kv_norm_gain, w_uq, w_ukv, swa_sinks, w_out, final_gain, loss_target, m_w_ada, m_b_ada, m_norm_gain, m_w_in, m_q_norm_gain, m_kv_norm_gain, m_w_uq, m_w_ukv, m_swa_sinks, m_w_out, m_final_gain, v_w_ada, v_b_ada, v_norm_gain, v_w_in, v_q_norm_gain, v_kv_norm_gain, v_w_uq, v_w_ukv, v_swa_sinks, v_w_out, v_final_gain):
    B, S, D = x.shape
    me = 4 * lax.axis_index("x") + 2 * lax.axis_index("y") + lax.axis_index("c")
    bf = _MXU_DTYPE

    ncol = w_ada.shape[2]
    b_cols = lax.dynamic_slice_in_dim(b_ada, me * ncol, ncol, axis=1)
    c_all, win_g, wuq_g, wukv_g, wout_g, mod_g = _all_gather(
        [c, w_in[0].astype(bf), w_uq[0].astype(bf), w_ukv[0].astype(bf), w_out[0].astype(bf)], "ag_weights",
        fused=((w_ada[0], b_cols), _ada_cols, jax.ShapeDtypeStruct((N_DEV, B, ncol), F32)))
    c_all = c_all.reshape(N_DEV * B, D)
    cat_cols = lambda a: jnp.transpose(a, (1, 0, 2)).reshape(a.shape[1], N_DEV * a.shape[2])
    w_in_f, w_uq_f, w_ukv_f = cat_cols(win_g), cat_cols(wuq_g), cat_cols(wukv_g)
    w_out_f = wout_g.reshape(D, D)

    mod = lax.dynamic_index_in_dim(mod_g, me, axis=1, keepdims=False)
    mod = jnp.transpose(mod, (1, 0, 2)).reshape(B, 3 * D)

    gx, dw_in_t, dw_uq, dw_ukv, dw_out, parts, dmod = _local_step(
        x, mod, positions, norm_gain, w_in_f, q_norm_gain, kv_norm_gain, w_uq_f, w_ukv_f, swa_sinks,
        w_out_f, final_gain, loss_target)

    parts_g, dmod_g = _all_gather([parts, dmod], "ag_small")
    dmod_all = dmod_g.reshape(N_DEV * B, 3 * D)
    dmod_cols = lax.dynamic_slice_in_dim(dmod_all, me * ncol, ncol, axis=1)
    psum, loss, g_b_ada, g_w_ada = _finalize(parts_g, dmod_all, dmod_cols, c_all)
    loss = loss.reshape(())
    o = LANE
    g_final_gain = psum[0, o:o + D]
    g_norm_gain = psum[:, o + D:o + 2 * D]
    o += 2 * D
    g_q_norm_gain = psum[:, o:o + Q_LORA]
    g_kv_norm_gain = psum[:, o + Q_LORA:o + Q_LORA + KV_LORA]
    o += Q_LORA + KV_LORA
    g_sinks = psum[:, o:o + SWA_HEADS]

    split_cols = lambda a: jnp.transpose(a.reshape(a.shape[0], 4, 2, a.shape[1] // N_DEV), (1, 2, 0, 3))
    split_rows = lambda a: a.reshape(4, 2, a.shape[0] // N_DEV, a.shape[1])
    g_w_out, g_w_uq, g_w_ukv, g_w_in_t = _reduce_scatter(
        [split_rows(dw_out), split_cols(dw_uq), split_cols(dw_ukv), split_rows(dw_in_t)], "rs_grads")
    g_w_in = g_w_in_t.T

    grads = [g_w_ada, g_b_ada, g_norm_gain, g_w_in, g_q_norm_gain, g_kv_norm_gain, g_w_uq, g_w_ukv, g_sinks, g_w_out,
             g_final_gain]
    ws = [w_ada, b_ada, norm_gain, w_in, q_norm_gain, kv_norm_gain, w_uq, w_ukv, swa_sinks, w_out, final_gain]
    ms = [m_w_ada, m_b_ada, m_norm_gain, m_w_in, m_q_norm_gain, m_kv_norm_gain, m_w_uq, m_w_ukv, m_swa_sinks, m_w_out,
          m_final_gain]
    vs = [v_w_ada, v_b_ada, v_norm_gain, v_w_in, v_q_norm_gain, v_kv_norm_gain, v_w_uq, v_w_ukv, v_swa_sinks, v_w_out,
          v_final_gain]
    two_d = [(1, w.shape[0]) if w.ndim == 1 else w.shape[-2:] for w in ws]
    flat = lambda arrs: [a.reshape(s) for a, s in zip(arrs, two_d)]
    deltas, new_ms, new_vs = _adamw(flat(ws), flat(grads), flat(ms), flat(vs))
    shaped = lambda arrs: [a.reshape(w.shape) for a, w in zip(arrs, ws)]
    return (loss, gx, *shaped(grads), *shaped(deltas), *shaped(new_ms), *shaped(new_vs))
```

```python
import functools

import numpy as np
import jax
import jax.numpy as jnp
from jax import lax
from jax.experimental import pallas as pl
from jax.experimental.pallas import tpu as pltpu

F32 = jnp.float32
_MXU_DTYPE = jnp.bfloat16
MLA_GRAD_DTYPE = jnp.bfloat16

N_DEV = 8
D_MODEL = 1024
MLA_HEADS = 8
MLA_NOPE = 64
MLA_ROPE = 32
MLA_V = 64
Q_LORA = 384
KV_LORA = 256
SWA_HEADS = 8
SWA_KV_HEADS = 2
SWA_HEAD_DIM = 64
WINDOW = 128
ROPE_THETA = 10000.0
EPS = 1e-6
MLA_SCALE = float((MLA_NOPE + MLA_ROPE) ** -0.5)
SWA_SCALE = float(SWA_HEAD_DIM ** -0.5)
LOG2E = 1.4426950408889634
MLA_QSCALE = MLA_SCALE * LOG2E
D_IN = 2464
D_IN_PAD = 2560
PAIR_ORDER = (0, 4, 1, 5, 2, 6, 3, 7)
PAIR_INV = (0, 2, 4, 6, 1, 3, 5, 7)

ADAM_LR = 0.001
ADAM_B1 = 0.9
ADAM_B2 = 0.999
ADAM_EPS = 1e-08
ADAM_WD = 0.01
ADAM_STEP = 10

LANE = 128
VMEM_LIMIT = 56 * 1024 * 1024

MESH = pl.DeviceIdType.MESH
NEG_INF = float("-inf")
SWA_SEQ_SPLIT = 2


def _mx(a):
    return a.astype(_MXU_DTYPE)


def _dot(a, b):
    return jnp.dot(a, b, preferred_element_type=F32)


def _dot_nt(a, b):
    return lax.dot_general(a, b, (((1,), (1,)), ((), ())), preferred_element_type=F32)


def _dot_tn(a, b):
    return lax.dot_general(a, b, (((0,), (0,)), ((), ())), preferred_element_type=F32)


def _cparams(sem=None):
    return pltpu.CompilerParams(dimension_semantics=sem, vmem_limit_bytes=VMEM_LIMIT)


def _vmem():
    return pl.BlockSpec(memory_space=pltpu.VMEM)


def _lane_iota(shape):
    return lax.broadcasted_iota(jnp.int32, shape, len(shape) - 1)


def _all_gather(arrs, name, fused=None):
    n = len(arrs)
    extra, fn, piece = fused if fused else ((), None, None)
    ne, m = len(extra), n + (1 if fused else 0)

    def body(*refs):
        ins, ex, outs = refs[:n], refs[n:n + ne], refs[n + ne:n + ne + m]
        rest = refs[n + ne + m:]
        srcs = list(ins) + ([rest[0]] if fused else [])
        send_sems, recv_sems, local_sems = rest[-3:]
        x, y, c = lax.axis_index("x"), lax.axis_index("y"), lax.axis_index("c")
        me, sibling = (x, y, c), (x, y, 1 - c)
        chips = [(1 - x, y), (x, 1 - y), (1 - x, 1 - y)]

        def slot(a, dev):
            return outs[a].at[4 * dev[0] + 2 * dev[1] + dev[2]]

        def copy(a, k, block, to, src=None):
            return pltpu.make_async_remote_copy(
                src_ref=slot(a, block) if src is None else src, dst_ref=slot(a, block),
                send_sem=send_sems.at[7 * a + k], recv_sem=recv_sems.at[7 * a + k],
                device_id=to, device_id_type=MESH)

        def local(a):
            return pltpu.make_async_copy(srcs[a], slot(a, me), local_sems.at[a])

        def start(a):
            local(a).start()
            cps = [copy(a, 0, me, sibling, src=srcs[a])]
            cps += [copy(a, 1 + j, me, (*chip, c), src=srcs[a]) for j, chip in enumerate(chips)]
            for cp in cps:
                cp.start()
            return cps

        def finish(group):
            cps = []
            for j, chip in enumerate(chips):
                for a in group:
                    copy(a, 1 + j, (*chip, c), me).wait_recv()
                    cp = copy(a, 4 + j, (*chip, c), sibling)
                    cp.start()
                    cps.append(cp)
            for a in group:
                copy(a, 0, sibling, me).wait_recv()
                for j, chip in enumerate(chips):
                    copy(a, 4 + j, (*chip, 1 - c), me).wait_recv()
                local(a).wait()
            return cps

        pending = []
        for a in range(n):
            pending += start(a)
        if fused:
            pending += finish([0])
            fn(srcs[n], outs[0], *ex)
            pending += start(n)
            pending += finish([n] + list(range(1, n)))
        else:
            pending += finish(list(range(n)))
        for cp in pending:
            cp.wait_send()

    out_shape = [jax.ShapeDtypeStruct((N_DEV,) + a.shape, a.dtype) for a in arrs]
    scratch = []
    if fused:
        out_shape.append(jax.ShapeDtypeStruct((N_DEV,) + piece.shape, piece.dtype))
        scratch.append(pltpu.VMEM(piece.shape, piece.dtype))
    return pl.pallas_call(
        body, name=name, out_shape=out_shape,
        in_specs=[_vmem()] * (n + ne), out_specs=[_vmem()] * m,
        scratch_shapes=scratch + [pltpu.SemaphoreType.DMA((7 * m,)), pltpu.SemaphoreType.DMA((7 * m,)),
                                  pltpu.SemaphoreType.DMA((m,))],
        compiler_params=pltpu.CompilerParams(vmem_limit_bytes=VMEM_LIMIT),
    )(*arrs, *extra)


def _reduce_scatter(arrs, name):
    n = len(arrs)
    halves = [a.astype(jnp.bfloat16) for a in arrs]

    def body(*refs):
        xs, xbs, outs = refs[:n], refs[n:2 * n], refs[2 * n:3 * n]
        parts, recv_a, send_b, recv_b = (refs[(3 + t) * n:(4 + t) * n] for t in range(4))
        send_sems, recv_sems, local_sems = refs[7 * n:]
        x, y, c = lax.axis_index("x"), lax.axis_index("y"), lax.axis_index("c")
        myq = 2 * x + y

        def chip(k):
            return (1 - x if k & 2 else x, 1 - y if k & 1 else y)

        def from_sibling(a):
            return pltpu.make_async_remote_copy(
                src_ref=xbs[a].at[:, 1 - c], dst_ref=recv_a[a], send_sem=send_sems.at[4 * a], recv_sem=recv_sems.at[4 * a],
                device_id=(x, y, 1 - c), device_id_type=MESH)

        def to_owner(a, k):
            qx, qy = chip(k)
            return pltpu.make_async_remote_copy(
                src_ref=send_b[a].at[2 * qx + qy], dst_ref=recv_b[a].at[myq],
                send_sem=send_sems.at[4 * a + k], recv_sem=recv_sems.at[4 * a + k],
                device_id=(qx, qy, c), device_id_type=MESH)

        mine = [pltpu.make_async_copy(xs[a].at[:, c], parts[a], local_sems.at[a]) for a in range(n)]
        first = [from_sibling(a) for a in range(n)]
        for cp in mine + first:
            cp.start()
        second = []
        for a in range(n):
            mine[a].wait()
            first[a].wait_recv()
            parts[a][...] = parts[a][...] + recv_a[a][...].astype(F32)
            send_b[a][...] = parts[a][...].astype(jnp.bfloat16)
            for k in range(1, 4):
                cp = to_owner(a, k)
                cp.start()
                second.append(cp)
        for a in range(n):
            acc = parts[a][myq]
            for k in range(1, 4):
                to_owner(a, k).wait_recv()
                qx, qy = chip(k)
                acc = acc + recv_b[a][2 * qx + qy].astype(F32)
            outs[a][...] = acc
        for cp in first + second:
            cp.wait_send()

    quarter = lambda a, dt: pltpu.VMEM((4,) + a.shape[2:], dt)
    return pl.pallas_call(
        body, name=name,
        out_shape=[jax.ShapeDtypeStruct(a.shape[2:], F32) for a in arrs],
        in_specs=[pl.BlockSpec(memory_space=pl.ANY)] * (2 * n), out_specs=[_vmem()] * n,
        scratch_shapes=[quarter(a, F32) for a in arrs] + [quarter(a, jnp.bfloat16) for a in arrs] * 3
        + [pltpu.SemaphoreType.DMA((4 * n,)), pltpu.SemaphoreType.DMA((4 * n,)), pltpu.SemaphoreType.DMA((n,))],
        compiler_params=pltpu.CompilerParams(vmem_limit_bytes=VMEM_LIMIT),
    )(*arrs, *halves)


def _silu(t):
    return t * (1.0 / (1.0 + jnp.exp(-t)))


def _ada_cols(piece_ref, c_all_ref, w_ref, b_ref):
    w = _mx(w_ref[...])
    for d in range(N_DEV):
        piece_ref[d] = _dot(_mx(_silu(c_all_ref[d])), w) + b_ref[...]


def _rope_split(cd, sd):
    n = cd.shape[0]

    def expand(d):
        rep = jnp.broadcast_to(d[:, None, :], (n, 8, LANE)).reshape(8 * n, LANE)
        return pltpu.roll(rep, 0, 1, stride=16, stride_axis=0)

    c, s = expand(cd), expand(sd)
    lane = _lane_iota(c.shape)
    first = jnp.logical_and(lane >= 64, lane < 80)
    second = jnp.logical_and(lane >= 80, lane < 96)
    ck = jnp.where(first, pltpu.roll(c, 80, 1), jnp.where(second, pltpu.roll(c, 96, 1), 0.0))
    cq = jnp.where(lane < 64, 1.0, ck)
    sa = jnp.where(first, -pltpu.roll(s, 80, 1), 0.0)
    sb = jnp.where(second, pltpu.roll(s, 96, 1), 0.0)
    return cq, ck, sa, sb


def _tile_heads(t):
    return jnp.concatenate([t] * MLA_HEADS, axis=1)


def _pre_fwd(x2, shift, scale, ng, w_in, gq, gkv, w_uq, w_uk, w_uv, rope, S, ts):
    T, D = x2.shape
    nsb = S // ts
    WQ = MLA_HEADS * LANE

    def body(x_ref, sh_ref, sc_ref, ng_ref, w_ref, gq_ref, gkv_ref, wuq_ref, wuk_ref, wuv_ref,
             cd_ref, sd_ref,
             hb_ref, zq_ref, zkv_ref, ql_ref, kvl_ref, q_ref, k_ref, v_ref, qs_ref, ks_ref, vs_ref, g_ref):
        x = x_ref[...]
        r1 = lax.rsqrt(jnp.mean(x * x, axis=-1, keepdims=True) + EPS)
        h = ((x * r1) * ng_ref[...]) * (1.0 + sc_ref[0]) + sh_ref[0]
        hb = _mx(h)
        hb_ref[...] = hb
        zq = _dot(hb, w_ref[:, 0:384])
        zq_ref[...] = zq
        rq = lax.rsqrt(jnp.mean(zq * zq, axis=-1, keepdims=True) + EPS)
        ql = _mx((zq * rq) * gq_ref[...])
        ql_ref[...] = ql
        q = _dot(ql, wuq_ref[...])
        cq, ck, sa, sb = _rope_split(cd_ref[...], sd_ref[...])
        q = (q * _tile_heads(cq) + pltpu.roll(q, WQ - 16, 1) * _tile_heads(sa)
             + pltpu.roll(q, 16, 1) * _tile_heads(sb))
        q_ref[...] = _mx(q * MLA_QSCALE)
        zkv = _dot(hb, w_ref[:, 384:640])
        zkv_ref[...] = zkv
        rkv = lax.rsqrt(jnp.mean(zkv * zkv, axis=-1, keepdims=True) + EPS)
        kvl = _mx((zkv * rkv) * gkv_ref[...])
        kvl_ref[...] = kvl
        kr = _dot(hb, w_ref[:, 640:768])
        kpe = kr * ck + pltpu.roll(kr, LANE - 16, 1) * sa + pltpu.roll(kr, 16, 1) * sb
        kf = _dot(kvl, wuk_ref[...])
        k_ref[...] = _mx(kf + jnp.concatenate([kpe] * MLA_HEADS, axis=1))
        v_ref[...] = _mx(_dot(kvl, wuv_ref[...]))
        g_ref[:, 0:512] = _dot(hb, w_ref[:, 768:1280])
        qs_ref[...] = _mx(_dot(hb, w_ref[:, 1280:1792]) * (SWA_SCALE * LOG2E))
        ks_ref[...] = _mx(_dot(hb, w_ref[:, 1792:1920]))
        vs_ref[...] = _mx(_dot(hb, w_ref[:, 1920:2048]))
        g_ref[:, 512:1024] = _dot(hb, w_ref[:, 2048:2560])

    row = lambda w: pl.BlockSpec((ts, w), lambda i: (i, 0))
    dense = pl.BlockSpec((ts // 8, LANE), lambda i: (i, 0))
    full = lambda a: pl.BlockSpec(a.shape, lambda i: (0,) * a.ndim)
    per_b = pl.BlockSpec((1, 1, D), lambda i: (i // nsb, 0, 0))
    out_w = [(D, _MXU_DTYPE), (384, F32), (256, F32), (384, _MXU_DTYPE), (256, _MXU_DTYPE), (WQ, _MXU_DTYPE),
             (WQ, _MXU_DTYPE), (512, _MXU_DTYPE), (512, _MXU_DTYPE), (128, _MXU_DTYPE), (128, _MXU_DTYPE), (1024, F32)]
    return pl.pallas_call(
        body, name="pre_fwd", grid=(T // ts,),
        out_shape=[jax.ShapeDtypeStruct((T, w), dt) for w, dt in out_w],
        in_specs=[row(D), per_b, per_b, full(ng), full(w_in), full(gq), full(gkv), full(w_uq), full(w_uk), full(w_uv),
                  dense, dense],
        out_specs=[row(w) for w, _ in out_w],
        compiler_params=_cparams(("arbitrary",)),
    )(x2, shift, scale, ng, w_in, gq, gkv, w_uq, w_uk, w_uv, *rope)


def _mla_fwd(q3, k3, v3, tq, tk):
    B, S, _ = q3.shape
    nq = S // tq
    assert tk % tq == 0 or tq % tk == 0
    n_masked = max(1, tq // tk)

    HPS = 2
    NH = 2 * HPS

    def body(q_ref, k_ref, v_ref, o_ref, lse_ref):
        qi = pl.program_id(2)
        rows = lax.broadcasted_iota(jnp.int32, (tq, tk), 0)
        cols = lax.broadcasted_iota(jnp.int32, (tq, tk), 1)
        qs = [q_ref[0, :, LANE * j:LANE * (j + 1)] for j in range(NH)]
        n_full = (qi * tq) // tk

        low_k = _lane_iota((tk, LANE)) < 64

        def step(kt, carry, masked):
            r0 = pl.multiple_of(kt * tk, tk)
            out = []
            for j in range(NH):
                v2 = v_ref[0, pl.ds(r0, tk), LANE * (j // 2):LANE * (j // 2 + 1)]
                vj = (jnp.where(low_k, v2, 1.0) if j % 2 == 0 else jnp.where(low_k, 1.0, v2)).astype(v2.dtype)
                m, acc = carry[j]
                s = _dot_nt(qs[j], k_ref[0, pl.ds(r0, tk), LANE * j:LANE * (j + 1)])
                if masked:
                    s = jnp.where(rows + qi * tq >= cols + kt * tk, s, NEG_INF)
                m_new = jnp.maximum(m, jnp.max(s, axis=1, keepdims=True))
                alpha = jnp.exp2(m - m_new)
                p = jnp.exp2(s - m_new)
                acc = alpha * acc + _dot(_mx(p), vj)
                out.append((m_new, acc))
            return tuple(out)

        init = (jnp.full((tq, 1), NEG_INF, F32), jnp.zeros((tq, LANE), F32))
        carry = lax.fori_loop(0, n_full, functools.partial(step, masked=False), (init,) * NH)
        for t in range(n_masked):
            carry = step(n_full + t, carry, True)
        low = _lane_iota((tq, LANE)) < 64
        for t in range(HPS):
            (m0, a0), (m1, a1) = carry[2 * t], carry[2 * t + 1]
            l0 = jnp.where(low, pltpu.roll(a0, 64, 1), a0)
            l1 = jnp.where(low, a1, pltpu.roll(a1, 64, 1))
            o_ref[0, :, LANE * t:LANE * (t + 1)] = jnp.where(low, a0 / l0, a1 / l1)
            lse_ref[0, 2 * t] = m0 + jnp.log2(l0)
            lse_ref[0, 2 * t + 1] = m1 + jnp.log2(l1)

    return pl.pallas_call(
        body, name="mla_fwd", grid=(B, MLA_HEADS // NH, nq),
        out_shape=[jax.ShapeDtypeStruct((B, S, 512), F32), jax.ShapeDtypeStruct((B, MLA_HEADS, S, LANE), F32)],
        in_specs=[pl.BlockSpec((1, tq, NH * LANE), lambda b, hp, i: (b, i, hp)),
                  pl.BlockSpec((1, S, NH * LANE), lambda b, hp, i: (b, 0, hp)),
                  pl.BlockSpec((1, S, HPS * LANE), lambda b, hp, i: (b, 0, hp))],
        out_specs=[pl.BlockSpec((1, tq, HPS * LANE), lambda b, hp, i: (b, i, hp)),
                   pl.BlockSpec((1, NH, tq, LANE), lambda b, hp, i: (b, hp, i, 0))],
        compiler_params=_cparams(("arbitrary", "arbitrary", "arbitrary")),
    )(q3, k3, v3)


def _mla_bwd(q3, k3, v3, o3, do3, lse, tq, tk, hps=2):
    B, S, _ = q3.shape
    nq, nk = S // tq, S // tk
    nh = 2 * hps
    assert tk % tq == 0

    def body(q_ref, k_ref, v_ref, o_ref, do_ref, lse_ref, dq_ref, dk_ref, dv_ref, dkt_ref, dvt_ref):
        dkt_ref[...] = jnp.zeros_like(dkt_ref)
        dvt_ref[...] = jnp.zeros_like(dvt_ref)
        rows = lax.broadcasted_iota(jnp.int32, (tq, tk), 0)
        cols = lax.broadcasted_iota(jnp.int32, (tq, tk), 1)
        lane = _lane_iota((tq, LANE))

        def q_tile(qi, _):
            r = pl.ds(pl.multiple_of(qi * tq, tq), tq)
            heads = []
            for j in range(nh):
                lanes = slice(LANE * j, LANE * (j + 1))
                pair = slice(LANE * (j // 2), LANE * (j // 2 + 1))
                do = jnp.where((lane < 64) if j % 2 == 0 else (lane >= 64), do_ref[0, r, pair], 0.0)
                q = q_ref[0, r, lanes]
                heads.append((lanes, pair, q, _mx(q.astype(F32).T), _mx(do), _mx(do.T),
                              jnp.sum(do * o_ref[0, r, pair], axis=1, keepdims=True),
                              jnp.concatenate([lse_ref[0, j, r, :]] * (tk // LANE), axis=1)))
            n_full = (qi * tq) // tk

            def k_tile(kt, dqs, masked):
                kr = pl.ds(pl.multiple_of(kt * tk, tk), tk)
                out = []
                dvt = [None] * hps
                for j, (lanes, pair, q, qt, dob, dot_, dcol, lse_c) in enumerate(heads):
                    k = k_ref[0, kr, lanes]
                    s = _dot_nt(q, k)
                    if masked:
                        s = jnp.where(rows + qi * tq >= cols + kt * tk, s, NEG_INF)
                    p = jnp.exp2(s - lse_c)
                    dp = _dot_nt(dob, v_ref[0, kr, pair])
                    dsb = _mx(p * (dp - dcol))
                    dkt_ref[j, kt] += _dot(qt, dsb)
                    pv = _dot(dot_, _mx(p))
                    dvt[j // 2] = pv if dvt[j // 2] is None else dvt[j // 2] + pv
                    out.append(dqs[j] + _dot(dsb, k))
                for t in range(hps):
                    dvt_ref[t, kt] += dvt[t]
                return tuple(out)

            zero = jnp.zeros((tq, LANE), F32)
            dqs = lax.fori_loop(0, n_full, functools.partial(k_tile, masked=False), (zero,) * nh)
            dqs = k_tile(n_full, dqs, True)
            for j in range(nh):
                dq_ref[0, r, heads[j][0]] = (MLA_SCALE * dqs[j]).astype(dq_ref.dtype)
            return 0

        lax.fori_loop(0, nq, q_tile, 0)

        def flush(kt, _):
            kr = pl.ds(pl.multiple_of(kt * tk, tk), tk)
            for j in range(nh):
                dk_ref[0, kr, LANE * j:LANE * (j + 1)] = ((1.0 / LOG2E) * dkt_ref[j, kt].T).astype(dk_ref.dtype)
            for t in range(hps):
                dv_ref[0, kr, LANE * t:LANE * (t + 1)] = dvt_ref[t, kt].T.astype(dv_ref.dtype)
            return 0

        lax.fori_loop(0, nk, flush, 0)

    grp = lambda w: pl.BlockSpec((1, S, w), lambda b, hp: (b, 0, hp))
    return pl.pallas_call(
        body, name="mla_bwd", grid=(B, MLA_HEADS // nh),
        out_shape=[jax.ShapeDtypeStruct((B, S, 1024), MLA_GRAD_DTYPE), jax.ShapeDtypeStruct((B, S, 1024), MLA_GRAD_DTYPE),
                   jax.ShapeDtypeStruct((B, S, 512), MLA_GRAD_DTYPE)],
        in_specs=[grp(nh * LANE), grp(nh * LANE), grp(hps * LANE), grp(hps * LANE), grp(hps * LANE),
                  pl.BlockSpec((1, nh, S, LANE), lambda b, hp: (b, hp, 0, 0))],
        out_specs=[grp(nh * LANE), grp(nh * LANE), grp(hps * LANE)],
        scratch_shapes=[pltpu.VMEM((nh, nk, LANE, tk), F32), pltpu.VMEM((hps, nk, LANE, tk), F32)],
        compiler_params=_cparams(("arbitrary", "arbitrary")),
    )(q3, k3, v3, o3, do3, lse)


def _swa_consts(sink_ref):
    W = WINDOW
    row = lax.broadcasted_iota(jnp.int32, (4 * W, LANE), 0)
    out = []
    for g in range(SWA_KV_HEADS):
        slope = jnp.zeros((4 * W, LANE), F32)
        sink = jnp.zeros((4 * W, LANE), F32)
        for p in range(4):
            h = p + 4 * g
            here = jnp.logical_and(row >= W * p, row < W * (p + 1))
            slope = jnp.where(here, float(LOG2E * 2.0 ** (-8.0 * (h + 1) / SWA_HEADS)), slope)
            sink = jnp.where(here, LOG2E * sink_ref[:, h:h + 1], sink)
        out.append((jnp.concatenate([slope, slope], axis=1), sink))
    return out


def _wide(col):
    return jnp.concatenate([col, col], axis=1)


def _swa_block(q_ref, k_ref, v_ref, pb_ref, pr_ref, n, i, ij):
    W = WINDOW
    kb = jnp.maximum(n - 1, 0)
    r = pl.ds(pl.multiple_of(i * W, W), W)
    kr = pl.ds(pl.multiple_of(kb * W, W), 2 * W)
    q4, k2, v2 = q_ref[0, r, :], k_ref[0, kr, :], v_ref[0, kr, :]
    pq = _wide(pb_ref[0, r, :])
    pk = jnp.concatenate([pr_ref[0, pl.ds(kb, 1), :], pr_ref[0, pl.ds(kb + 1, 1), :]], axis=1)
    rel = ij + (n - kb) * W
    mask = jnp.where(jnp.logical_and(rel >= 0, rel < W), 0.0, NEG_INF)
    dist4 = jnp.concatenate([pq - pk] * 4, axis=0)
    mask4 = jnp.concatenate([mask] * 4, axis=0)
    return r, kb, kr, q4, k2, v2, dist4, mask4


def _swa_stack(x4, g, dtype):
    lane = _lane_iota((WINDOW, LANE))
    mine = (lane < 64) if g == 0 else (lane >= 64)
    return jnp.concatenate([jnp.where(mine, x4[:, LANE * p:LANE * (p + 1)], 0).astype(dtype) for p in range(4)], axis=0)


def _swa_unstack(ref, r, lo, hi, scale=None):
    W = WINDOW
    low = _lane_iota((W, LANE)) < 64
    for p in range(4):
        t = jnp.where(low, lo[W * p:W * (p + 1)], hi[W * p:W * (p + 1)])
        ref[0, r, LANE * p:LANE * (p + 1)] = t if scale is None else scale * t


def _swa_fwd(qs3, ks3, vs3, posb, posr, sinks):
    B, S, _ = qs3.shape
    W = WINDOW
    nb = S // W
    nh = SWA_SEQ_SPLIT
    nbh = nb // nh

    def body(q_ref, k_ref, v_ref, pb_ref, pr_ref, sink_ref, o_ref, lse_ref):
        ij = lax.broadcasted_iota(jnp.int32, (W, 2 * W), 0) - lax.broadcasted_iota(jnp.int32, (W, 2 * W), 1)
        consts = _swa_consts(sink_ref)
        n0 = pl.program_id(1) * nbh

        def blk(i, _):
            n = n0 + i
            r, _, _, q4, k2, v2, dist4, mask4 = _swa_block(q_ref, k_ref, v_ref, pb_ref, pr_ref, n, i, ij)
            o_g = []
            for g, (slope, sink) in enumerate(consts):
                s = _dot_nt(_swa_stack(q4, g, q4.dtype), k2) - slope * dist4 + mask4
                m = jnp.maximum(jnp.max(s, axis=1, keepdims=True), sink)
                e = jnp.exp2(s - _wide(m))
                l = jnp.sum(e, axis=1, keepdims=True) + jnp.exp2(sink - m)
                o_g.append(_dot(_mx(e), v2) * (1.0 / l))
                lse_ref[0, i, g] = m + jnp.log2(l)
            _swa_unstack(o_ref, r, o_g[0], o_g[1])
            return 0

        lax.fori_loop(0, nbh, blk, 0)

    seq = lambda w: pl.BlockSpec((1, S, w), lambda b, h: (b, 0, 0))
    part = lambda w: pl.BlockSpec((1, S // nh, w), lambda b, h: (b, h, 0))
    lse_spec = pl.BlockSpec((1, nbh, 2, 4 * W, LANE), lambda b, h: (b, h, 0, 0, 0))
    return pl.pallas_call(
        body, name="swa_fwd", grid=(B, nh),
        out_shape=[jax.ShapeDtypeStruct((B, S, 512), F32), jax.ShapeDtypeStruct((B, nb, 2, 4 * W, LANE), F32)],
        in_specs=[part(512), seq(LANE), seq(LANE), part(LANE), pl.BlockSpec((1, nb, W), lambda b, h: (b, 0, 0)),
                  pl.BlockSpec((1, LANE), lambda b, h: (0, 0))],
        out_specs=[part(512), lse_spec],
        compiler_params=_cparams(("arbitrary", "arbitrary")),
    )(qs3, ks3, vs3, posb, posr, sinks)


def _swa_bwd(qs3, ks3, vs3, posb, posr, sinks, os3, do3, lse):
    B, S, _ = qs3.shape
    W = WINDOW
    nb = S // W
    nh = SWA_SEQ_SPLIT
    nbh = nb // nh

    def body(q_ref, k_ref, v_ref, pb_ref, pr_ref, sink_ref, o_ref, do_ref, lse_ref, dq_ref, dk_ref, dv_ref, dsink_ref,
             dkt_ref, dvt_ref):
        lane1 = _lane_iota((1, LANE))
        ij = lax.broadcasted_iota(jnp.int32, (W, 2 * W), 0) - lax.broadcasted_iota(jnp.int32, (W, 2 * W), 1)
        consts = _swa_consts(sink_ref)
        hh = pl.program_id(1)
        n0 = hh * nbh

        @pl.when(hh == 0)
        def _():
            dkt_ref[...] = jnp.zeros_like(dkt_ref)
            dvt_ref[...] = jnp.zeros_like(dvt_ref)

        @pl.when(jnp.logical_and(pl.program_id(0) == 0, hh == 0))
        def _():
            dsink_ref[...] = jnp.zeros_like(dsink_ref)

        def blk(i, dsink):
            n = n0 + i
            r, kb, _, q4, k2, v2, dist4, mask4 = _swa_block(q_ref, k_ref, v_ref, pb_ref, pr_ref, n, i, ij)
            o4, do4 = o_ref[0, r, :], do_ref[0, r, :]
            dq_g = []
            dkt = jnp.zeros((LANE, 2 * W), F32)
            dvt = jnp.zeros((LANE, 2 * W), F32)
            for g, (slope, sink) in enumerate(consts):
                q_st = _swa_stack(q4, g, F32)
                do_st = _swa_stack(do4, g, F32)
                dcol = jnp.sum(do_st * _swa_stack(o4, g, F32), axis=1, keepdims=True)
                qst, dob = _mx(q_st), _mx(do_st)
                lse_c = lse_ref[0, i, g]
                pr = jnp.exp2(_dot_nt(qst, k2) - slope * dist4 + mask4 - _wide(lse_c))
                dsb = _mx(pr * (_dot_nt(dob, v2) - dcol))
                psd = jnp.exp2(sink - lse_c)[:, 0:1] * dcol
                for p in range(4):
                    dsink = dsink - jnp.where(lane1 == p + 4 * g,
                                              jnp.sum(psd[W * p:W * (p + 1)], axis=0, keepdims=True), 0.0)
                dq_g.append(_dot(dsb, k2))
                dkt = dkt + _dot(_mx(q_st.T), dsb)
                dvt = dvt + _dot(_mx(do_st.T), _mx(pr))
            _swa_unstack(dq_ref, r, dq_g[0], dq_g[1], SWA_SCALE)
            dkt_ref[kb] += dkt[:, 0:W]
            dkt_ref[kb + 1] += dkt[:, W:2 * W]
            dvt_ref[kb] += dvt[:, 0:W]
            dvt_ref[kb + 1] += dvt[:, W:2 * W]
            return dsink

        dsink_ref[...] += lax.fori_loop(0, nbh, blk, jnp.zeros((1, LANE), F32))

        @pl.when(hh == nh - 1)
        def _():
            def flush(n, _):
                r = pl.ds(pl.multiple_of(n * W, W), W)
                dk_ref[0, r, :] = (1.0 / LOG2E) * dkt_ref[n].T
                dv_ref[0, r, :] = dvt_ref[n].T
                return 0

            lax.fori_loop(0, nb, flush, 0)

    seq = lambda w: pl.BlockSpec((1, S, w), lambda b, h: (b, 0, 0))
    part = lambda w: pl.BlockSpec((1, S // nh, w), lambda b, h: (b, h, 0))
    return pl.pallas_call(
        body, name="swa_bwd", grid=(B, nh),
        out_shape=[jax.ShapeDtypeStruct((B, S, 512), F32), jax.ShapeDtypeStruct((B, S, LANE), F32),
                   jax.ShapeDtypeStruct((B, S, LANE), F32), jax.ShapeDtypeStruct((1, LANE), F32)],
        in_specs=[part(512), seq(LANE), seq(LANE), part(LANE), pl.BlockSpec((1, nb, W), lambda b, h: (b, 0, 0)),
                  pl.BlockSpec((1, LANE), lambda b, h: (0, 0)), part(512),
                  pl.BlockSpec((1, S // nh, 512), lambda b, h: (b, h, 1)),
                  pl.BlockSpec((1, nbh, 2, 4 * W, LANE), lambda b, h: (b, h, 0, 0, 0))],
        out_specs=[part(512), seq(LANE), seq(LANE), pl.BlockSpec((1, LANE), lambda b, h: (0, 0))],
        scratch_shapes=[pltpu.VMEM((nb, LANE, W), F32), pltpu.VMEM((nb, LANE, W), F32)],
        compiler_params=_cparams(("arbitrary", "arbitrary")),
    )(qs3, ks3, vs3, posb, posr, sinks, os3, do3, lse)


def _post(om, osw, g, w_out, x2, gate, fg, tgt, S, ts):
    T, D = x2.shape
    nsb = S // ts

    def body(om_ref, os_ref, g_ref, w_ref, x_ref, gate_ref, fg_ref, t_ref,
             dx2_ref, do_ref, dg_ref, loss_ref, dfg_ref, dgate_ref, dw_ref):
        i = pl.program_id(0)
        gv = g_ref[...]
        sg = 1.0 / (1.0 + jnp.exp(-gv))
        silu = gv * sg
        o = jnp.concatenate([om_ref[...], os_ref[...]], axis=1)
        ab = _mx(o * silu)
        y = _dot(ab, w_ref[...])
        gate = gate_ref[0]
        xo = x_ref[...] + gate * y
        r2 = lax.rsqrt(jnp.mean(xo * xo, axis=-1, keepdims=True) + EPS)
        xh = xo * r2
        fg = fg_ref[...]
        diff = xh * fg - t_ref[...]
        sq = jnp.sum(diff * diff, axis=0, keepdims=True)
        part = sq[:, 0:LANE]
        for t in range(1, D // LANE):
            part = part + sq[:, LANE * t:LANE * (t + 1)]
        dout = diff * (1.0 / D)
        dxh = dout * fg
        dx2 = r2 * (dxh - xh * jnp.mean(dxh * xh, axis=-1, keepdims=True))
        dx2_ref[...] = dx2
        dyb = _mx(dx2 * gate)
        da = _dot_nt(dyb, w_ref[...])
        do_ref[...] = da * silu
        dg_ref[...] = _mx(da * o * (sg * (1.0 + gv * (1.0 - sg))))

        @pl.when(i == 0)
        def _():
            loss_ref[...] = jnp.zeros_like(loss_ref)
            dfg_ref[...] = jnp.zeros_like(dfg_ref)
            dw_ref[...] = jnp.zeros_like(dw_ref)

        @pl.when(i % nsb == 0)
        def _():
            dgate_ref[...] = jnp.zeros_like(dgate_ref)

        loss_ref[...] += (0.5 / D) * part
        dfg_ref[...] += jnp.sum(dout * xh, axis=0, keepdims=True)
        dgate_ref[0] += jnp.sum(dx2 * y, axis=0, keepdims=True)
        dw_ref[...] += _dot_tn(ab, dyb)

    row = lambda w: pl.BlockSpec((ts, w), lambda i: (i, 0))
    full = lambda a: pl.BlockSpec(a.shape, lambda i: (0,) * a.ndim)
    per_b = pl.BlockSpec((1, 1, D), lambda i: (i // nsb, 0, 0))
    return pl.pallas_call(
        body, name="post", grid=(T // ts,),
        out_shape=[jax.ShapeDtypeStruct((T, D), F32), jax.ShapeDtypeStruct((T, 1024), F32),
                   jax.ShapeDtypeStruct((T, 1024), _MXU_DTYPE), jax.ShapeDtypeStruct((1, LANE), F32),
                   jax.ShapeDtypeStruct((1, D), F32), jax.ShapeDtypeStruct(gate.shape, F32),
                   jax.ShapeDtypeStruct(w_out.shape, F32)],
        in_specs=[row(512), row(512), row(1024), full(w_out), row(D), per_b, full(fg), row(D)],
        out_specs=[row(D), row(1024), row(1024),
                   pl.BlockSpec((1, LANE), lambda i: (0, 0)), pl.BlockSpec((1, D), lambda i: (0, 0)), per_b,
                   full(w_out)],
        compiler_params=_cparams(("arbitrary",)),
    )(om, osw, g, w_out, x2, gate, fg, tgt)


def _pre_bwd(dq, dk, dv, dqs, dks, dvs, dg, zq, zkv, ql, kvl, x2, dx2, scale, ng, w_in, gq, gkv, w_uq, w_uk, w_uv,
             rope, S, ts):
    T, D = x2.shape
    nsb = S // ts
    WQ = MLA_HEADS * LANE

    def body(dq_ref, dk_ref, dv_ref, dqs_ref, dks_ref, dvs_ref, dg_ref, zq_ref, zkv_ref, ql_ref, kvl_ref, x_ref, dx2_ref,
             sc_ref,
             ng_ref, w_ref, gq_ref, gkv_ref, wuq_ref, wuk_ref, wuv_ref, cd_ref, sd_ref,
             gx_ref, dz_ref, dgq_ref, dgkv_ref, dng_ref, dsh_ref, dsc_ref, dwuq_ref, dwuk_ref, dwuv_ref):
        i = pl.program_id(0)

        @pl.when(i == 0)
        def _():
            dgq_ref[...] = jnp.zeros_like(dgq_ref)
            dgkv_ref[...] = jnp.zeros_like(dgkv_ref)
            dng_ref[...] = jnp.zeros_like(dng_ref)
            dwuq_ref[...] = jnp.zeros_like(dwuq_ref)
            dwuk_ref[...] = jnp.zeros_like(dwuk_ref)
            dwuv_ref[...] = jnp.zeros_like(dwuv_ref)

        @pl.when(i % nsb == 0)
        def _():
            dsh_ref[...] = jnp.zeros_like(dsh_ref)
            dsc_ref[...] = jnp.zeros_like(dsc_ref)

        def norm_bwd(z, dy, gain):
            r = lax.rsqrt(jnp.mean(z * z, axis=-1, keepdims=True) + EPS)
            zh = z * r
            dzh = dy * gain
            return r * (dzh - zh * jnp.mean(dzh * zh, axis=-1, keepdims=True)), jnp.sum(dy * zh, axis=0, keepdims=True)

        tables = _rope_split(cd_ref[...], sd_ref[...])
        ng = ng_ref[...]
        sc1 = 1.0 + sc_ref[0]

        def rows_chain(rs):
            cq, ck, sa, sb = (t[rs] for t in tables)
            dqr = dq_ref[rs, :].astype(F32)
            dqb = _mx(dqr * _tile_heads(cq) + pltpu.roll(dqr * _tile_heads(sa), 16, 1)
                      + pltpu.roll(dqr * _tile_heads(sb), WQ - 16, 1))
            p_uq = _dot_tn(ql_ref[rs, :], dqb)
            dzq, dgq = norm_bwd(zq_ref[rs, :], _dot_nt(dqb, wuq_ref[...]), gq_ref[...])
            dkr = dk_ref[rs, :].astype(F32)
            dkb = _mx(dkr)
            p_uk = _dot_tn(kvl_ref[rs, :], dkb)
            dvb = _mx(dv_ref[rs, :])
            p_uv = _dot_tn(kvl_ref[rs, :], dvb)
            dzkv, dgkv = norm_bwd(zkv_ref[rs, :], _dot_nt(dkb, wuk_ref[...]) + _dot_nt(dvb, wuv_ref[...]), gkv_ref[...])
            dkpe = dkr[:, 0:LANE]
            for h in range(1, MLA_HEADS):
                dkpe = dkpe + dkr[:, LANE * h:LANE * (h + 1)]
            dkro = dkpe * ck + pltpu.roll(dkpe * sa, 16, 1) + pltpu.roll(dkpe * sb, LANE - 16, 1)
            dz_ref[rs, 0:384] = _mx(dzq)
            dz_ref[rs, 384:640] = _mx(dzkv)
            dz_ref[rs, 640:768] = _mx(dkro)
            dz_ref[rs, 768:1280] = dg_ref[rs, 0:512]
            dz_ref[rs, 1280:1792] = _mx(dqs_ref[rs, :])
            dz_ref[rs, 1792:1920] = _mx(dks_ref[rs, :])
            dz_ref[rs, 1920:2048] = _mx(dvs_ref[rs, :])
            dz_ref[rs, 2048:2560] = dg_ref[rs, 512:1024]
            dh = _dot_nt(dz_ref[rs, :], w_ref[...])
            x = x_ref[rs, :]
            r1 = lax.rsqrt(jnp.mean(x * x, axis=-1, keepdims=True) + EPS)
            xn = x * r1
            dxn = dh * ng * sc1
            gx_ref[rs, :] = dx2_ref[rs, :] + r1 * (dxn - xn * jnp.mean(dxn * xn, axis=-1, keepdims=True))
            return (p_uq, p_uk, p_uv, dgq, dgkv, jnp.sum(dh, axis=0, keepdims=True),
                    jnp.sum(dh * (xn * ng), axis=0, keepdims=True), jnp.sum(dh * xn * sc1, axis=0, keepdims=True))

        hr = ts // 2
        parts = [rows_chain(slice(hr * t, hr * (t + 1))) for t in range(2)]
        p_uq, p_uk, p_uv, dgq, dgkv, dsh, dsc, dng = (a + b for a, b in zip(*parts))
        dwuq_ref[...] += p_uq
        dwuk_ref[...] += p_uk
        dwuv_ref[...] += p_uv
        dgq_ref[...] += dgq
        dgkv_ref[...] += dgkv
        dsh_ref[0] += dsh
        dsc_ref[0] += dsc
        dng_ref[...] += dng

    row = lambda w: pl.BlockSpec((ts, w), lambda i: (i, 0))
    full = lambda a: pl.BlockSpec(a.shape, lambda i: (0,) * a.ndim, pipeline_mode=pl.Buffered(1))
    per_b = pl.BlockSpec((1, 1, D), lambda i: (i // nsb, 0, 0))
    dense = pl.BlockSpec((ts // 8, LANE), lambda i: (i, 0))
    vec = lambda w: pl.BlockSpec((1, w), lambda i: (0, 0))
    return pl.pallas_call(
        body, name="pre_bwd", grid=(T // ts,),
        out_shape=[jax.ShapeDtypeStruct((T, D), F32), jax.ShapeDtypeStruct((T, D_IN_PAD), _MXU_DTYPE), jax.ShapeDtypeStruct((1, 384), F32),
                   jax.ShapeDtypeStruct((1, 256), F32), jax.ShapeDtypeStruct((1, D), F32),
                   jax.ShapeDtypeStruct(scale.shape, F32), jax.ShapeDtypeStruct(scale.shape, F32),
                   jax.ShapeDtypeStruct(w_uq.shape, F32), jax.ShapeDtypeStruct(w_uk.shape, F32),
                   jax.ShapeDtypeStruct(w_uv.shape, F32)],
        in_specs=[row(WQ), row(WQ), row(512), row(512), row(LANE), row(LANE), row(1024), row(384), row(256), row(384),
                  row(256), row(D), row(D), per_b, full(ng), full(w_in), full(gq), full(gkv), full(w_uq), full(w_uk), full(w_uv),
                  dense, dense],
        out_specs=[row(D), row(D_IN_PAD), vec(384), vec(256), vec(D), per_b, per_b, full(w_uq), full(w_uk), full(w_uv)],
        compiler_params=_cparams(("arbitrary",)),
    )(dq, dk, dv, dqs, dks, dvs, dg, zq, zkv, ql, kvl, x2, dx2, scale, ng, w_in, gq, gkv, w_uq, w_uk, w_uv, *rope)


def _tn_matmul(a, b, tn, tk, name):
    T, M = a.shape
    N = b.shape[1]

    def body(a_ref, b_ref, o_ref):
        @pl.when(pl.program_id(1) == 0)
        def _():
            o_ref[...] = jnp.zeros_like(o_ref)

        o_ref[...] += _dot_tn(a_ref[...], b_ref[...])

    return pl.pallas_call(
        body, name=name, grid=(N // tn, T // tk),
        out_shape=jax.ShapeDtypeStruct((M, N), F32),
        in_specs=[pl.BlockSpec((tk, M), lambda j, k: (k, 0)), pl.BlockSpec((tk, tn), lambda j, k: (k, j))],
        out_specs=pl.BlockSpec((M, tn), lambda j, k: (0, j)),
        compiler_params=_cparams(("arbitrary", "arbitrary")),
    )(a, b)


def _finalize(parts_all, dmod_all, dmod_cols, c_act_all):
    nparts = parts_all.shape[-1]
    nmod = dmod_all.shape[-1]

    def body(p_ref, dm_ref, dmc_ref, c_ref, ps_ref, loss_ref, db_ref, dw_ref):
        acc = p_ref[0]
        for j in range(1, N_DEV):
            acc = acc + p_ref[j]
        ps_ref[...] = acc
        loss_ref[...] = jnp.sum(acc[:, 0:LANE], axis=1, keepdims=True)
        db = dm_ref[0:1, :]
        for j in range(1, dm_ref.shape[0]):
            db = db + dm_ref[j:j + 1, :]
        db_ref[...] = db
        dw_ref[...] = _dot_tn(_mx(_silu(c_ref[...])), _mx(dmc_ref[...]))

    return pl.pallas_call(
        body, name="finalize",
        out_shape=[jax.ShapeDtypeStruct((1, nparts), F32), jax.ShapeDtypeStruct((1, 1), F32),
                   jax.ShapeDtypeStruct((1, nmod), F32),
                   jax.ShapeDtypeStruct((c_act_all.shape[1], dmod_cols.shape[1]), F32)],
        in_specs=[_vmem()] * 4, out_specs=[_vmem()] * 4,
        compiler_params=_cparams(),
    )(parts_all, dmod_all, dmod_cols, c_act_all)


def _adamw(ws, gs, ms, vs):
    n = len(ws)

    def body(*refs):
        w_refs, g_refs, m_refs, v_refs, d_refs, nm_refs, nv_refs = (refs[t * n:(t + 1) * n] for t in range(7))
        for t in range(n):
            gv = g_refs[t][...]
            nm = ADAM_B1 * m_refs[t][...] + (1.0 - ADAM_B1) * gv
            nv = ADAM_B2 * v_refs[t][...] + (1.0 - ADAM_B2) * (gv * gv)
            m_hat = nm / (1.0 - ADAM_B1 ** ADAM_STEP)
            v_hat = nv / (1.0 - ADAM_B2 ** ADAM_STEP)
            d_refs[t][...] = -ADAM_LR * (m_hat / (jnp.sqrt(v_hat) + ADAM_EPS) + ADAM_WD * w_refs[t][...])
            nm_refs[t][...] = nm
            nv_refs[t][...] = nv

    out = pl.pallas_call(
        body, name="adamw",
        out_shape=[jax.ShapeDtypeStruct(w.shape, F32) for w in ws] * 3,
        in_specs=[_vmem()] * (4 * n), out_specs=[_vmem()] * (3 * n),
        compiler_params=_cparams(),
    )(*ws, *gs, *ms, *vs)
    return out[:n], out[n:2 * n], out[2 * n:]


def _pair_perm(a, axis, order):
    a = jnp.moveaxis(a, axis, -1)
    lead = a.shape[:-1]
    a = a.reshape(lead + (8, 64))[..., list(order), :].reshape(lead + (512,))
    return jnp.moveaxis(a, -1, axis)


def _pad_w_in(w):
    z = lambda n: jnp.zeros((w.shape[0], n), w.dtype)
    return jnp.concatenate([
        w[:, 0:640], z(64), w[:, 640:672], z(32), w[:, 672:1184],
        _pair_perm(w[:, 1184:1696], 1, PAIR_ORDER), w[:, 1696:1952], _pair_perm(w[:, 1952:2464], 1, PAIR_ORDER)], axis=1)


def _unpad_w_in_t(g):
    return jnp.concatenate([
        g[0:640], g[704:736], g[768:1280],
        _pair_perm(g[1280:1792], 0, PAIR_INV), g[1792:2048], _pair_perm(g[2048:2560], 0, PAIR_INV)], axis=0)


def _rope_table(positions):
    T = positions.size
    inv = ROPE_THETA ** (-jnp.arange(0, MLA_ROPE, 2, dtype=F32) / MLA_ROPE)
    pos = jnp.repeat(positions.reshape(T // 8, 8)[:, ::-1].astype(F32), MLA_ROPE // 2, axis=1)
    ang = pos * jnp.tile(inv, 8)[None, :]
    return jnp.cos(ang), jnp.sin(ang)


def _local_step(x, mod, positions, ng, w_in, gq, gkv, w_uq, w_ukv, sinks, w_out, fg, tgt,
                ts=512, fq=512, fk=512, bq=512, bk=512):
    B, S, D = x.shape
    T = B * S
    x2 = x.reshape(T, D)
    shift, scale, gate = (mod[:, None, k * D:(k + 1) * D] for k in range(3))
    w_in_p = _pad_w_in(w_in)
    w_uq_p = jnp.pad(w_uq.reshape(Q_LORA, MLA_HEADS, 96), ((0, 0), (0, 0), (0, 32))).reshape(Q_LORA, MLA_HEADS * LANE)
    w_ukv3 = w_ukv.reshape(KV_LORA, MLA_HEADS, 128)
    w_uk_p = jnp.pad(w_ukv3[:, :, :64], ((0, 0), (0, 0), (0, 64))).reshape(KV_LORA, MLA_HEADS * LANE)
    w_uv = w_ukv3[:, :, 64:].reshape(KV_LORA, 512)
    w_out_p = jnp.concatenate([w_out[:512], _pair_perm(w_out[512:], 0, PAIR_ORDER)], axis=0)
    rope = _rope_table(positions)
    posf = positions.astype(F32)
    posb = jnp.broadcast_to(posf[:, :, None], (B, S, LANE))
    posr = posf.reshape(B, S // WINDOW, WINDOW)
    sinks_l = jnp.pad(sinks.reshape(1, SWA_HEADS), ((0, 0), (0, LANE - SWA_HEADS)))

    (hb, zq, zkv, ql, kvl, q, k, v, qs, ks, vs, g) = _pre_fwd(
        x2, shift, scale, ng, w_in_p, gq, gkv, w_uq_p, w_uk_p, w_uv, rope, S, ts)
    r3 = lambda a: a.reshape(B, S, a.shape[-1])
    om, lse_m = _mla_fwd(r3(q), r3(k), r3(v), fq, fk)
    osw, lse_s = _swa_fwd(r3(qs), r3(ks), r3(vs), posb, posr, sinks_l)
    dx2, do, dg, loss_v, dfg, dgate, dw_out_p = _post(
        om.reshape(T, 512), osw.reshape(T, 512), g, w_out_p, x2, gate, fg.reshape(1, D), tgt.reshape(T, D), S, ts)
    do3 = r3(do)
    dq, dk, dv = _mla_bwd(r3(q), r3(k), r3(v), om, do3, lse_m, bq, bk)
    dqs, dks, dvs, dsink = _swa_bwd(r3(qs), r3(ks), r3(vs), posb, posr, sinks_l, osw, do3, lse_s)
    f2 = lambda a: a.reshape(T, a.shape[-1])
    gx, dz, dgq, dgkv, dng, dsh, dsc, dw_uq_p, dw_uk_p, dw_uv = _pre_bwd(
        f2(dq), f2(dk), f2(dv), f2(dqs), f2(dks), f2(dvs), dg, zq, zkv, ql, kvl, x2, dx2, scale, ng, w_in_p, gq, gkv,
        w_uq_p, w_uk_p, w_uv, rope, S, ts)
    tk = min(T, 1024)
    dw_in_t = _unpad_w_in_t(_tn_matmul(dz, hb, 512, tk, "dw_in"))
    dw_out = jnp.concatenate([dw_out_p[:512], _pair_perm(dw_out_p[512:], 0, PAIR_INV)], axis=0)
    dw_uq = dw_uq_p.reshape(Q_LORA, MLA_HEADS, LANE)[:, :, :96].reshape(Q_LORA, 768)
    dw_uk = dw_uk_p.reshape(KV_LORA, MLA_HEADS, LANE)[:, :, :64]
    dw_uv = dw_uv.reshape(KV_LORA, MLA_HEADS, 64)
    dw_ukv = jnp.concatenate([dw_uk, dw_uv], axis=2).reshape(KV_LORA, 1024)
    parts = jnp.concatenate([loss_v, dfg, dng, dgq, dgkv, dsink], axis=1)
    dmod = jnp.concatenate([dsh, dsc, dgate], axis=2).reshape(B, 3 * D)
    return gx.reshape(B, S, D), dw_in_t, dw_uq, dw_ukv, dw_out, parts, dmod


def kernel(x, c, positions, w_ada, b_ada, norm_gain, w_in, q_norm_gain, kv_norm_gain, w_uq, w_ukv, swa_sinks, w_out, final_gain, loss_target, m_w_ada, m_b_ada, m_norm_gain, m_w_in, m_q_norm_gain, m_kv_norm_gain, m_w_uq, m_w_ukv, m_swa_sinks, m_w_out, m_final_gain, v_w_ada, v_b_ada, v_norm_gain, v_w_in, v_q_norm_gain, v_kv_norm_gain, v_w_uq, v_w_ukv, v_swa_sinks, v_w_out, v_final_gain):
    B, S, D = x.shape
    me = 4 * lax.axis_index("x") + 2 * lax.axis_index("y") + lax.axis_index("c")
    bf = _MXU_DTYPE

    ncol = w_ada.shape[2]
    b_cols = lax.dynamic_slice_in_dim(b_ada, me * ncol, ncol, axis=1)
    c_all, win_g, wuq_g, wukv_g, wout_g, mod_g = _all_gather(
        [c, w_in[0].astype(bf), w_uq[0].astype(bf), w_ukv[0].astype(bf), w_out[0].astype(bf)], "ag_weights",
        fused=((w_ada[0], b_cols), _ada_cols, jax.ShapeDtypeStruct((N_DEV, B, ncol), F32)))
    c_all = c_all.reshape(N_DEV * B, D)
    cat_cols = lambda a: jnp.transpose(a, (1, 0, 2)).reshape(a.shape[1], N_DEV * a.shape[2])
    w_in_f, w_uq_f, w_ukv_f = cat_cols(win_g), cat_cols(wuq_g), cat_cols(wukv_g)
    w_out_f = wout_g.reshape(D, D)

    mod = lax.dynamic_index_in_dim(mod_g, me, axis=1, keepdims=False)
    mod = jnp.transpose(mod, (1, 0, 2)).reshape(B, 3 * D)

    gx, dw_in_t, dw_uq, dw_ukv, dw_out, parts, dmod = _local_step(
        x, mod, positions, norm_gain, w_in_f, q_norm_gain, kv_norm_gain, w_uq_f, w_ukv_f, swa_sinks,
        w_out_f, final_gain, loss_target)

    parts_g, dmod_g = _all_gather([parts, dmod], "ag_small")
    dmod_all = dmod_g.reshape(N_DEV * B, 3 * D)
    dmod_cols = lax.dynamic_slice_in_dim(dmod_all, me * ncol, ncol, axis=1)
    psum, loss, g_b_ada, g_w_ada = _finalize(parts_g, dmod_all, dmod_cols, c_all)
    loss = loss.reshape(())
    o = LANE
    g_final_gain = psum[0, o:o + D]
    g_norm_gain = psum[:, o + D:o + 2 * D]
    o += 2 * D
    g_q_norm_gain = psum[:, o:o + Q_LORA]
    g_kv_norm_gain = psum[:, o + Q_LORA:o + Q_LORA + KV_LORA]
    o += Q_LORA + KV_LORA
    g_sinks = psum[:, o:o + SWA_HEADS]

    split_cols = lambda a: jnp.transpose(a.reshape(a.shape[0], 4, 2, a.shape[1] // N_DEV), (1, 2, 0, 3))
    split_rows = lambda a: a.reshape(4, 2, a.shape[0] // N_DEV, a.shape[1])
    g_w_in_t, g_w_uq, g_w_ukv, g_w_out = _reduce_scatter(
        [split_rows(dw_in_t), split_cols(dw_uq), split_cols(dw_ukv), split_rows(dw_out)], "rs_grads")
    g_w_in = g_w_in_t.T

    grads = [g_w_ada, g_b_ada, g_norm_gain, g_w_in, g_q_norm_gain, g_kv_norm_gain, g_w_uq, g_w_ukv, g_sinks, g_w_out,
             g_final_gain]
    ws = [w_ada, b_ada, norm_gain, w_in, q_norm_gain, kv_norm_gain, w_uq, w_ukv, swa_sinks, w_out, final_gain]
    ms = [m_w_ada, m_b_ada, m_norm_gain, m_w_in, m_q_norm_gain, m_kv_norm_gain, m_w_uq, m_w_ukv, m_swa_sinks, m_w_out,
          m_final_gain]
    vs = [v_w_ada, v_b_ada, v_norm_gain, v_w_in, v_q_norm_gain, v_kv_norm_gain, v_w_uq, v_w_ukv, v_swa_sinks, v_w_out,
          v_final_gain]
    two_d = [(1, w.shape[0]) if w.ndim == 1 else w.shape[-2:] for w in ws]
    flat = lambda arrs: [a.reshape(s) for a, s in zip(arrs, two_d)]
    deltas, new_ms, new_vs = _adamw(flat(ws), flat(grads), flat(ms), flat(vs))
    shaped = lambda arrs: [a.reshape(w.shape) for a, w in zip(arrs, ws)]
    return (loss, gx, *shaped(grads), *shaped(deltas), *shaped(new_ms), *shaped(new_vs))
```

```python
import functools

import numpy as np
import jax
import jax.numpy as jnp
from jax import lax
from jax.experimental import pallas as pl
from jax.experimental.pallas import tpu as pltpu

F32 = jnp.float32
_MXU_DTYPE = jnp.bfloat16
MLA_GRAD_DTYPE = jnp.bfloat16

N_DEV = 8
D_MODEL = 1024
MLA_HEADS = 8
MLA_NOPE = 64
MLA_ROPE = 32
MLA_V = 64
Q_LORA = 384
KV_LORA = 256
SWA_HEADS = 8
SWA_KV_HEADS = 2
SWA_HEAD_DIM = 64
WINDOW = 128
ROPE_THETA = 10000.0
EPS = 1e-6
MLA_SCALE = float((MLA_NOPE + MLA_ROPE) ** -0.5)
SWA_SCALE = float(SWA_HEAD_DIM ** -0.5)
LOG2E = 1.4426950408889634
MLA_QSCALE = MLA_SCALE * LOG2E
D_IN = 2464
D_IN_PAD = 2560
PAIR_ORDER = (0, 4, 1, 5, 2, 6, 3, 7)
PAIR_INV = (0, 2, 4, 6, 1, 3, 5, 7)

ADAM_LR = 0.001
ADAM_B1 = 0.9
ADAM_B2 = 0.999
ADAM_EPS = 1e-08
ADAM_WD = 0.01
ADAM_STEP = 10

LANE = 128
VMEM_LIMIT = 56 * 1024 * 1024

MESH = pl.DeviceIdType.MESH
NEG_INF = float("-inf")
SWA_SEQ_SPLIT = 2


def _mx(a):
    return a.astype(_MXU_DTYPE)


def _dot(a, b):
    return jnp.dot(a, b, preferred_element_type=F32)


def _dot_nt(a, b):
    return lax.dot_general(a, b, (((1,), (1,)), ((), ())), preferred_element_type=F32)


def _dot_tn(a, b):
    return lax.dot_general(a, b, (((0,), (0,)), ((), ())), preferred_element_type=F32)


def _cparams(sem=None):
    return pltpu.CompilerParams(dimension_semantics=sem, vmem_limit_bytes=VMEM_LIMIT)


def _vmem():
    return pl.BlockSpec(memory_space=pltpu.VMEM)


def _lane_iota(shape):
    return lax.broadcasted_iota(jnp.int32, shape, len(shape) - 1)


def _all_gather(arrs, name, fused=None):
    n = len(arrs)
    extra, fn, piece = fused if fused else ((), None, None)
    ne, m = len(extra), n + (1 if fused else 0)

    def body(*refs):
        ins, ex, outs = refs[:n], refs[n:n + ne], refs[n + ne:n + ne + m]
        rest = refs[n + ne + m:]
        srcs = list(ins) + ([rest[0]] if fused else [])
        send_sems, recv_sems, local_sems = rest[-3:]
        x, y, c = lax.axis_index("x"), lax.axis_index("y"), lax.axis_index("c")
        me, sibling = (x, y, c), (x, y, 1 - c)
        chips = [(1 - x, y), (x, 1 - y), (1 - x, 1 - y)]

        def slot(a, dev):
            return outs[a].at[4 * dev[0] + 2 * dev[1] + dev[2]]

        def copy(a, k, block, to, src=None):
            return pltpu.make_async_remote_copy(
                src_ref=slot(a, block) if src is None else src, dst_ref=slot(a, block),
                send_sem=send_sems.at[7 * a + k], recv_sem=recv_sems.at[7 * a + k],
                device_id=to, device_id_type=MESH)

        def local(a):
            return pltpu.make_async_copy(srcs[a], slot(a, me), local_sems.at[a])

        def start(a):
            local(a).start()
            cps = [copy(a, 0, me, sibling, src=srcs[a])]
            cps += [copy(a, 1 + j, me, (*chip, c), src=srcs[a]) for j, chip in enumerate(chips)]
            for cp in cps:
                cp.start()
            return cps

        def finish(group):
            cps = []
            for j, chip in enumerate(chips):
                for a in group:
                    copy(a, 1 + j, (*chip, c), me).wait_recv()
                    cp = copy(a, 4 + j, (*chip, c), sibling)
                    cp.start()
                    cps.append(cp)
            for a in group:
                copy(a, 0, sibling, me).wait_recv()
                for j, chip in enumerate(chips):
                    copy(a, 4 + j, (*chip, 1 - c), me).wait_recv()
                local(a).wait()
            return cps

        pending = []
        for a in range(n):
            pending += start(a)
        if fused:
            pending += finish([0])
            fn(srcs[n], outs[0], *ex)
            pending += start(n)
            pending += finish([n] + list(range(1, n)))
        else:
            pending += finish(list(range(n)))
        for cp in pending:
            cp.wait_send()

    out_shape = [jax.ShapeDtypeStruct((N_DEV,) + a.shape, a.dtype) for a in arrs]
    scratch = []
    if fused:
        out_shape.append(jax.ShapeDtypeStruct((N_DEV,) + piece.shape, piece.dtype))
        scratch.append(pltpu.VMEM(piece.shape, piece.dtype))
    return pl.pallas_call(
        body, name=name, out_shape=out_shape,
        in_specs=[_vmem()] * (n + ne), out_specs=[_vmem()] * m,
        scratch_shapes=scratch + [pltpu.SemaphoreType.DMA((7 * m,)), pltpu.SemaphoreType.DMA((7 * m,)),
                                  pltpu.SemaphoreType.DMA((m,))],
        compiler_params=pltpu.CompilerParams(vmem_limit_bytes=VMEM_LIMIT),
    )(*arrs, *extra)


def _reduce_scatter(arrs, name):
    n = len(arrs)
    halves = [a.astype(jnp.bfloat16) for a in arrs]

    def body(*refs):
        xs, xbs, outs = refs[:n], refs[n:2 * n], refs[2 * n:3 * n]
        parts, recv_a, send_b, recv_b = (refs[(3 + t) * n:(4 + t) * n] for t in range(4))
        send_sems, recv_sems, local_sems = refs[7 * n:]
        x, y, c = lax.axis_index("x"), lax.axis_index("y"), lax.axis_index("c")
        myq = 2 * x + y

        def chip(k):
            return (1 - x if k & 2 else x, 1 - y if k & 1 else y)

        def from_sibling(a):
            return pltpu.make_async_remote_copy(
                src_ref=xbs[a].at[:, 1 - c], dst_ref=recv_a[a], send_sem=send_sems.at[4 * a], recv_sem=recv_sems.at[4 * a],
                device_id=(x, y, 1 - c), device_id_type=MESH)

        def to_owner(a, k):
            qx, qy = chip(k)
            return pltpu.make_async_remote_copy(
                src_ref=send_b[a].at[2 * qx + qy], dst_ref=recv_b[a].at[myq],
                send_sem=send_sems.at[4 * a + k], recv_sem=recv_sems.at[4 * a + k],
                device_id=(qx, qy, c), device_id_type=MESH)

        mine = [pltpu.make_async_copy(xs[a].at[:, c], parts[a], local_sems.at[a]) for a in range(n)]
        first = [from_sibling(a) for a in range(n)]
        for cp in mine + first:
            cp.start()
        second = []
        for a in range(n):
            mine[a].wait()
            first[a].wait_recv()
            parts[a][...] = parts[a][...] + recv_a[a][...].astype(F32)
            send_b[a][...] = parts[a][...].astype(jnp.bfloat16)
            for k in range(1, 4):
                cp = to_owner(a, k)
                cp.start()
                second.append(cp)
        for a in range(n):
            acc = parts[a][myq]
            for k in range(1, 4):
                to_owner(a, k).wait_recv()
                qx, qy = chip(k)
                acc = acc + recv_b[a][2 * qx + qy].astype(F32)
            outs[a][...] = acc
        for cp in first + second:
            cp.wait_send()

    quarter = lambda a, dt: pltpu.VMEM((4,) + a.shape[2:], dt)
    return pl.pallas_call(
        body, name=name,
        out_shape=[jax.ShapeDtypeStruct(a.shape[2:], F32) for a in arrs],
        in_specs=[pl.BlockSpec(memory_space=pl.ANY)] * (2 * n), out_specs=[_vmem()] * n,
        scratch_shapes=[quarter(a, F32) for a in arrs] + [quarter(a, jnp.bfloat16) for a in arrs] * 3
        + [pltpu.SemaphoreType.DMA((4 * n,)), pltpu.SemaphoreType.DMA((4 * n,)), pltpu.SemaphoreType.DMA((n,))],
        compiler_params=pltpu.CompilerParams(vmem_limit_bytes=VMEM_LIMIT),
    )(*arrs, *halves)


def _silu(t):
    return t * (1.0 / (1.0 + jnp.exp(-t)))


def _ada_cols(piece_ref, c_all_ref, w_ref, b_ref):
    w = _mx(w_ref[...])
    for d in range(N_DEV):
        piece_ref[d] = _dot(_mx(_silu(c_all_ref[d])), w) + b_ref[...]


def _rope_split(cd, sd):
    n = cd.shape[0]

    def expand(d):
        rep = jnp.broadcast_to(d[:, None, :], (n, 8, LANE)).reshape(8 * n, LANE)
        return pltpu.roll(rep, 0, 1, stride=16, stride_axis=0)

    c, s = expand(cd), expand(sd)
    lane = _lane_iota(c.shape)
    first = jnp.logical_and(lane >= 64, lane < 80)
    second = jnp.logical_and(lane >= 80, lane < 96)
    ck = jnp.where(first, pltpu.roll(c, 80, 1), jnp.where(second, pltpu.roll(c, 96, 1), 0.0))
    cq = jnp.where(lane < 64, 1.0, ck)
    sa = jnp.where(first, -pltpu.roll(s, 80, 1), 0.0)
    sb = jnp.where(second, pltpu.roll(s, 96, 1), 0.0)
    return cq, ck, sa, sb


def _tile_heads(t):
    return jnp.concatenate([t] * MLA_HEADS, axis=1)


def _pre_fwd(x2, shift, scale, ng, w_in, gq, gkv, w_uq, w_uk, w_uv, rope, S, ts):
    T, D = x2.shape
    nsb = S // ts
    WQ = MLA_HEADS * LANE

    def body(x_ref, sh_ref, sc_ref, ng_ref, w_ref, gq_ref, gkv_ref, wuq_ref, wuk_ref, wuv_ref,
             cd_ref, sd_ref,
             hb_ref, zq_ref, zkv_ref, ql_ref, kvl_ref, q_ref, k_ref, v_ref, qs_ref, ks_ref, vs_ref, g_ref):
        x = x_ref[...]
        r1 = lax.rsqrt(jnp.mean(x * x, axis=-1, keepdims=True) + EPS)
        h = ((x * r1) * ng_ref[...]) * (1.0 + sc_ref[0]) + sh_ref[0]
        hb = _mx(h)
        hb_ref[...] = hb
        zq = _dot(hb, w_ref[:, 0:384])
        zq_ref[...] = zq
        rq = lax.rsqrt(jnp.mean(zq * zq, axis=-1, keepdims=True) + EPS)
        ql = _mx((zq * rq) * gq_ref[...])
        ql_ref[...] = ql
        q = _dot(ql, wuq_ref[...])
        cq, ck, sa, sb = _rope_split(cd_ref[...], sd_ref[...])
        q = (q * _tile_heads(cq) + pltpu.roll(q, WQ - 16, 1) * _tile_heads(sa)
             + pltpu.roll(q, 16, 1) * _tile_heads(sb))
        q_ref[...] = _mx(q * MLA_QSCALE)
        zkv = _dot(hb, w_ref[:, 384:640])
        zkv_ref[...] = zkv
        rkv = lax.rsqrt(jnp.mean(zkv * zkv, axis=-1, keepdims=True) + EPS)
        kvl = _mx((zkv * rkv) * gkv_ref[...])
        kvl_ref[...] = kvl
        kr = _dot(hb, w_ref[:, 640:768])
        kpe = kr * ck + pltpu.roll(kr, LANE - 16, 1) * sa + pltpu.roll(kr, 16, 1) * sb
        kf = _dot(kvl, wuk_ref[...])
        k_ref[...] = _mx(kf + jnp.concatenate([kpe] * MLA_HEADS, axis=1))
        v_ref[...] = _mx(_dot(kvl, wuv_ref[...]))
        g_ref[:, 0:512] = _dot(hb, w_ref[:, 768:1280])
        qs_ref[...] = _mx(_dot(hb, w_ref[:, 1280:1792]) * (SWA_SCALE * LOG2E))
        ks_ref[...] = _mx(_dot(hb, w_ref[:, 1792:1920]))
        vs_ref[...] = _mx(_dot(hb, w_ref[:, 1920:2048]))
        g_ref[:, 512:1024] = _dot(hb, w_ref[:, 2048:2560])

    row = lambda w: pl.BlockSpec((ts, w), lambda i: (i, 0))
    dense = pl.BlockSpec((ts // 8, LANE), lambda i: (i, 0))
    full = lambda a: pl.BlockSpec(a.shape, lambda i: (0,) * a.ndim)
    per_b = pl.BlockSpec((1, 1, D), lambda i: (i // nsb, 0, 0))
    out_w = [(D, _MXU_DTYPE), (384, F32), (256, F32), (384, _MXU_DTYPE), (256, _MXU_DTYPE), (WQ, _MXU_DTYPE),
             (WQ, _MXU_DTYPE), (512, _MXU_DTYPE), (512, _MXU_DTYPE), (128, _MXU_DTYPE), (128, _MXU_DTYPE), (1024, F32)]
    return pl.pallas_call(
        body, name="pre_fwd", grid=(T // ts,),
        out_shape=[jax.ShapeDtypeStruct((T, w), dt) for w, dt in out_w],
        in_specs=[row(D), per_b, per_b, full(ng), full(w_in), full(gq), full(gkv), full(w_uq), full(w_uk), full(w_uv),
                  dense, dense],
        out_specs=[row(w) for w, _ in out_w],
        compiler_params=_cparams(("arbitrary",)),
    )(x2, shift, scale, ng, w_in, gq, gkv, w_uq, w_uk, w_uv, *rope)


def _mla_fwd(q3, k3, v3, tq, tk):
    B, S, _ = q3.shape
    nq = S // tq
    assert tk % tq == 0 or tq % tk == 0
    n_masked = max(1, tq // tk)

    HPS = 2
    NH = 2 * HPS

    def body(q_ref, k_ref, v_ref, o_ref, lse_ref):
        qi = pl.program_id(2)
        rows = lax.broadcasted_iota(jnp.int32, (tq, tk), 0)
        cols = lax.broadcasted_iota(jnp.int32, (tq, tk), 1)
        qs = [q_ref[0, :, LANE * j:LANE * (j + 1)] for j in range(NH)]
        n_full = (qi * tq) // tk

        low_k = _lane_iota((tk, LANE)) < 64

        def step(kt, carry, masked):
            r0 = pl.multiple_of(kt * tk, tk)
            out = []
            for j in range(NH):
                v2 = v_ref[0, pl.ds(r0, tk), LANE * (j // 2):LANE * (j // 2 + 1)]
                vj = (jnp.where(low_k, v2, 1.0) if j % 2 == 0 else jnp.where(low_k, 1.0, v2)).astype(v2.dtype)
                m, acc = carry[j]
                s = _dot_nt(qs[j], k_ref[0, pl.ds(r0, tk), LANE * j:LANE * (j + 1)])
                if masked:
                    s = jnp.where(rows + qi * tq >= cols + kt * tk, s, NEG_INF)
                m_new = jnp.maximum(m, jnp.max(s, axis=1, keepdims=True))
                alpha = jnp.exp2(m - m_new)
                p = jnp.exp2(s - m_new)
                acc = alpha * acc + _dot(_mx(p), vj)
                out.append((m_new, acc))
            return tuple(out)

        init = (jnp.full((tq, 1), NEG_INF, F32), jnp.zeros((tq, LANE), F32))
        carry = lax.fori_loop(0, n_full, functools.partial(step, masked=False), (init,) * NH)
        for t in range(n_masked):
            carry = step(n_full + t, carry, True)
        low = _lane_iota((tq, LANE)) < 64
        for t in range(HPS):
            (m0, a0), (m1, a1) = carry[2 * t], carry[2 * t + 1]
            l0 = jnp.where(low, pltpu.roll(a0, 64, 1), a0)
            l1 = jnp.where(low, a1, pltpu.roll(a1, 64, 1))
            o_ref[0, :, LANE * t:LANE * (t + 1)] = jnp.where(low, a0 / l0, a1 / l1)
            lse_ref[0, 2 * t] = m0 + jnp.log2(l0)
            lse_ref[0, 2 * t + 1] = m1 + jnp.log2(l1)

    return pl.pallas_call(
        body, name="mla_fwd", grid=(B, MLA_HEADS // NH, nq),
        out_shape=[jax.ShapeDtypeStruct((B, S, 512), F32), jax.ShapeDtypeStruct((B, MLA_HEADS, S, LANE), F32)],
        in_specs=[pl.BlockSpec((1, tq, NH * LANE), lambda b, hp, i: (b, i, hp)),
                  pl.BlockSpec((1, S, NH * LANE), lambda b, hp, i: (b, 0, hp)),
                  pl.BlockSpec((1, S, HPS * LANE), lambda b, hp, i: (b, 0, hp))],
        out_specs=[pl.BlockSpec((1, tq, HPS * LANE), lambda b, hp, i: (b, i, hp)),
                   pl.BlockSpec((1, NH, tq, LANE), lambda b, hp, i: (b, hp, i, 0))],
        compiler_params=_cparams(("arbitrary", "arbitrary", "arbitrary")),
    )(q3, k3, v3)


def _mla_bwd(q3, k3, v3, o3, do3, lse, tq, tk, hps=2):
    B, S, _ = q3.shape
    nq, nk = S // tq, S // tk
    nh = 2 * hps
    assert tk % tq == 0

    def body(q_ref, k_ref, v_ref, o_ref, do_ref, lse_ref, dq_ref, dk_ref, dv_ref, dkt_ref, dvt_ref):
        dkt_ref[...] = jnp.zeros_like(dkt_ref)
        dvt_ref[...] = jnp.zeros_like(dvt_ref)
        rows = lax.broadcasted_iota(jnp.int32, (tq, tk), 0)
        cols = lax.broadcasted_iota(jnp.int32, (tq, tk), 1)
        lane = _lane_iota((tq, LANE))

        def q_tile(qi, _):
            r = pl.ds(pl.multiple_of(qi * tq, tq), tq)
            heads = []
            for j in range(nh):
                lanes = slice(LANE * j, LANE * (j + 1))
                pair = slice(LANE * (j // 2), LANE * (j // 2 + 1))
                do = jnp.where((lane < 64) if j % 2 == 0 else (lane >= 64), do_ref[0, r, pair], 0.0)
                q = q_ref[0, r, lanes]
                heads.append((lanes, pair, q, _mx(q.astype(F32).T), _mx(do), _mx(do.T),
                              jnp.sum(do * o_ref[0, r, pair], axis=1, keepdims=True),
                              jnp.concatenate([lse_ref[0, j, r, :]] * (tk // LANE), axis=1)))
            n_full = (qi * tq) // tk

            def k_tile(kt, dqs, masked):
                kr = pl.ds(pl.multiple_of(kt * tk, tk), tk)
                out = []
                dvt = [None] * hps
                for j, (lanes, pair, q, qt, dob, dot_, dcol, lse_c) in enumerate(heads):
                    k = k_ref[0, kr, lanes]
                    s = _dot_nt(q, k)
                    if masked:
                        s = jnp.where(rows + qi * tq >= cols + kt * tk, s, NEG_INF)
                    p = jnp.exp2(s - lse_c)
                    dp = _dot_nt(dob, v_ref[0, kr, pair])
                    dsb = _mx(p * (dp - dcol))
                    dkt_ref[j, kt] += _dot(qt, dsb)
                    pv = _dot(dot_, _mx(p))
                    dvt[j // 2] = pv if dvt[j // 2] is None else dvt[j // 2] + pv
                    out.append(dqs[j] + _dot(dsb, k))
                for t in range(hps):
                    dvt_ref[t, kt] += dvt[t]
                return tuple(out)

            zero = jnp.zeros((tq, LANE), F32)
            dqs = lax.fori_loop(0, n_full, functools.partial(k_tile, masked=False), (zero,) * nh)
            dqs = k_tile(n_full, dqs, True)
            for j in range(nh):
                dq_ref[0, r, heads[j][0]] = (MLA_SCALE * dqs[j]).astype(dq_ref.dtype)
            return 0

        lax.fori_loop(0, nq, q_tile, 0)

        def flush(kt, _):
            kr = pl.ds(pl.multiple_of(kt * tk, tk), tk)
            for j in range(nh):
                dk_ref[0, kr, LANE * j:LANE * (j + 1)] = ((1.0 / LOG2E) * dkt_ref[j, kt].T).astype(dk_ref.dtype)
            for t in range(hps):
                dv_ref[0, kr, LANE * t:LANE * (t + 1)] = dvt_ref[t, kt].T.astype(dv_ref.dtype)
            return 0

        lax.fori_loop(0, nk, flush, 0)

    grp = lambda w: pl.BlockSpec((1, S, w), lambda b, hp: (b, 0, hp))
    return pl.pallas_call(
        body, name="mla_bwd", grid=(B, MLA_HEADS // nh),
        out_shape=[jax.ShapeDtypeStruct((B, S, 1024), MLA_GRAD_DTYPE), jax.ShapeDtypeStruct((B, S, 1024), MLA_GRAD_DTYPE),
                   jax.ShapeDtypeStruct((B, S, 512), MLA_GRAD_DTYPE)],
        in_specs=[grp(nh * LANE), grp(nh * LANE), grp(hps * LANE), grp(hps * LANE), grp(hps * LANE),
                  pl.BlockSpec((1, nh, S, LANE), lambda b, hp: (b, hp, 0, 0))],
        out_specs=[grp(nh * LANE), grp(nh * LANE), grp(hps * LANE)],
        scratch_shapes=[pltpu.VMEM((nh, nk, LANE, tk), F32), pltpu.VMEM((hps, nk, LANE, tk), F32)],
        compiler_params=_cparams(("arbitrary", "arbitrary")),
    )(q3, k3, v3, o3, do3, lse)


def _swa_consts(sink_ref):
    W = WINDOW
    row = lax.broadcasted_iota(jnp.int32, (4 * W, LANE), 0)
    out = []
    for g in range(SWA_KV_HEADS):
        slope = jnp.zeros((4 * W, LANE), F32)
        sink = jnp.zeros((4 * W, LANE), F32)
        for p in range(4):
            h = p + 4 * g
            here = jnp.logical_and(row >= W * p, row < W * (p + 1))
            slope = jnp.where(here, float(LOG2E * 2.0 ** (-8.0 * (h + 1) / SWA_HEADS)), slope)
            sink = jnp.where(here, LOG2E * sink_ref[:, h:h + 1], sink)
        out.append((jnp.concatenate([slope, slope], axis=1), sink))
    return out


def _wide(col):
    return jnp.concatenate([col, col], axis=1)


def _swa_block(q_ref, k_ref, v_ref, pb_ref, pr_ref, n, i, ij):
    W = WINDOW
    kb = jnp.maximum(n - 1, 0)
    r = pl.ds(pl.multiple_of(i * W, W), W)
    kr = pl.ds(pl.multiple_of(kb * W, W), 2 * W)
    q4, k2, v2 = q_ref[0, r, :], k_ref[0, kr, :], v_ref[0, kr, :]
    pq = _wide(pb_ref[0, r, :])
    pk = jnp.concatenate([pr_ref[0, pl.ds(kb, 1), :], pr_ref[0, pl.ds(kb + 1, 1), :]], axis=1)
    rel = ij + (n - kb) * W
    mask = jnp.where(jnp.logical_and(rel >= 0, rel < W), 0.0, NEG_INF)
    dist4 = jnp.concatenate([pq - pk] * 4, axis=0)
    mask4 = jnp.concatenate([mask] * 4, axis=0)
    return r, kb, kr, q4, k2, v2, dist4, mask4


def _swa_stack(x4, g, dtype):
    lane = _lane_iota((WINDOW, LANE))
    mine = (lane < 64) if g == 0 else (lane >= 64)
    return jnp.concatenate([jnp.where(mine, x4[:, LANE * p:LANE * (p + 1)], 0).astype(dtype) for p in range(4)], axis=0)


def _swa_unstack(ref, r, lo, hi, scale=None):
    W = WINDOW
    low = _lane_iota((W, LANE)) < 64
    for p in range(4):
        t = jnp.where(low, lo[W * p:W * (p + 1)], hi[W * p:W * (p + 1)])
        ref[0, r, LANE * p:LANE * (p + 1)] = t if scale is None else scale * t


def _swa_fwd(qs3, ks3, vs3, posb, posr, sinks):
    B, S, _ = qs3.shape
    W = WINDOW
    nb = S // W
    nh = SWA_SEQ_SPLIT
    nbh = nb // nh

    def body(q_ref, k_ref, v_ref, pb_ref, pr_ref, sink_ref, o_ref, lse_ref):
        ij = lax.broadcasted_iota(jnp.int32, (W, 2 * W), 0) - lax.broadcasted_iota(jnp.int32, (W, 2 * W), 1)
        consts = _swa_consts(sink_ref)
        n0 = pl.program_id(1) * nbh

        def blk(i, _):
            n = n0 + i
            r, _, _, q4, k2, v2, dist4, mask4 = _swa_block(q_ref, k_ref, v_ref, pb_ref, pr_ref, n, i, ij)
            o_g = []
            for g, (slope, sink) in enumerate(consts):
                s = _dot_nt(_swa_stack(q4, g, q4.dtype), k2) - slope * dist4 + mask4
                m = jnp.maximum(jnp.max(s, axis=1, keepdims=True), sink)
                e = jnp.exp2(s - _wide(m))
                l = jnp.sum(e, axis=1, keepdims=True) + jnp.exp2(sink - m)
                o_g.append(_dot(_mx(e), v2) * (1.0 / l))
                lse_ref[0, i, g] = m + jnp.log2(l)
            _swa_unstack(o_ref, r, o_g[0], o_g[1])
            return 0

        lax.fori_loop(0, nbh, blk, 0, unroll=2)

    seq = lambda w: pl.BlockSpec((1, S, w), lambda b, h: (b, 0, 0))
    part = lambda w: pl.BlockSpec((1, S // nh, w), lambda b, h: (b, h, 0))
    lse_spec = pl.BlockSpec((1, nbh, 2, 4 * W, LANE), lambda b, h: (b, h, 0, 0, 0))
    return pl.pallas_call(
        body, name="swa_fwd", grid=(B, nh),
        out_shape=[jax.ShapeDtypeStruct((B, S, 512), F32), jax.ShapeDtypeStruct((B, nb, 2, 4 * W, LANE), F32)],
        in_specs=[part(512), seq(LANE), seq(LANE), part(LANE), pl.BlockSpec((1, nb, W), lambda b, h: (b, 0, 0)),
                  pl.BlockSpec((1, LANE), lambda b, h: (0, 0))],
        out_specs=[part(512), lse_spec],
        compiler_params=_cparams(("arbitrary", "arbitrary")),
    )(qs3, ks3, vs3, posb, posr, sinks)


def _swa_bwd(qs3, ks3, vs3, posb, posr, sinks, os3, do3, lse):
    B, S, _ = qs3.shape
    W = WINDOW
    nb = S // W
    nh = SWA_SEQ_SPLIT
    nbh = nb // nh
    assert nbh % 2 == 0 and nbh * nh * W == S

    def body(q_ref, k_ref, v_ref, pb_ref, pr_ref, sink_ref, o_ref, do_ref, lse_ref, dq_ref, dk_ref, dv_ref, dsink_ref,
             dkt_ref, dvt_ref):
        lane1 = _lane_iota((1, LANE))
        ij = lax.broadcasted_iota(jnp.int32, (W, 2 * W), 0) - lax.broadcasted_iota(jnp.int32, (W, 2 * W), 1)
        consts = _swa_consts(sink_ref)
        hh = pl.program_id(1)
        n0 = hh * nbh

        @pl.when(hh == 0)
        def _():
            dkt_ref[...] = jnp.zeros_like(dkt_ref)
            dvt_ref[...] = jnp.zeros_like(dvt_ref)

        @pl.when(jnp.logical_and(pl.program_id(0) == 0, hh == 0))
        def _():
            dsink_ref[...] = jnp.zeros_like(dsink_ref)

        def blk(i, dsink):
            n = n0 + i
            r, kb, _, q4, k2, v2, dist4, mask4 = _swa_block(q_ref, k_ref, v_ref, pb_ref, pr_ref, n, i, ij)
            o4, do4 = o_ref[0, r, :], do_ref[0, r, :]
            dq_g = []
            dkt = jnp.zeros((LANE, 2 * W), F32)
            dvt = jnp.zeros((LANE, 2 * W), F32)
            for g, (slope, sink) in enumerate(consts):
                q_st = _swa_stack(q4, g, F32)
                do_st = _swa_stack(do4, g, F32)
                dcol = jnp.sum(do_st * _swa_stack(o4, g, F32), axis=1, keepdims=True)
                qst, dob = _mx(q_st), _mx(do_st)
                lse_c = lse_ref[0, i, g]
                pr = jnp.exp2(_dot_nt(qst, k2) - slope * dist4 + mask4 - _wide(lse_c))
                dsb = _mx(pr * (_dot_nt(dob, v2) - dcol))
                psd = jnp.exp2(sink - lse_c)[:, 0:1] * dcol
                for p in range(4):
                    dsink = dsink - jnp.where(lane1 == p + 4 * g,
                                              jnp.sum(psd[W * p:W * (p + 1)], axis=0, keepdims=True), 0.0)
                dq_g.append(_dot(dsb, k2))
                dkt = dkt + _dot(_mx(q_st.T), dsb)
                dvt = dvt + _dot(_mx(do_st.T), _mx(pr))
            _swa_unstack(dq_ref, r, dq_g[0], dq_g[1], SWA_SCALE)
            dkt_ref[kb] += dkt[:, 0:W]
            dkt_ref[kb + 1] += dkt[:, W:2 * W]
            dvt_ref[kb] += dvt[:, 0:W]
            dvt_ref[kb + 1] += dvt[:, W:2 * W]
            return dsink

        dsink_ref[...] += lax.fori_loop(0, nbh // 2, lambda t, d: blk(2 * t + 1, blk(2 * t, d)), jnp.zeros((1, LANE), F32))

        @pl.when(hh == nh - 1)
        def _():
            def flush(n, _):
                r = pl.ds(pl.multiple_of(n * W, W), W)
                dk_ref[0, r, :] = (1.0 / LOG2E) * dkt_ref[n].T
                dv_ref[0, r, :] = dvt_ref[n].T
                return 0

            lax.fori_loop(0, nb, flush, 0)

    seq = lambda w: pl.BlockSpec((1, S, w), lambda b, h: (b, 0, 0))
    part = lambda w: pl.BlockSpec((1, S // nh, w), lambda b, h: (b, h, 0))
    return pl.pallas_call(
        body, name="swa_bwd", grid=(B, nh),
        out_shape=[jax.ShapeDtypeStruct((B, S, 512), F32), jax.ShapeDtypeStruct((B, S, LANE), F32),
                   jax.ShapeDtypeStruct((B, S, LANE), F32), jax.ShapeDtypeStruct((1, LANE), F32)],
        in_specs=[part(512), seq(LANE), seq(LANE), part(LANE), pl.BlockSpec((1, nb, W), lambda b, h: (b, 0, 0)),
                  pl.BlockSpec((1, LANE), lambda b, h: (0, 0)), part(512),
                  pl.BlockSpec((1, S // nh, 512), lambda b, h: (b, h, 1)),
                  pl.BlockSpec((1, nbh, 2, 4 * W, LANE), lambda b, h: (b, h, 0, 0, 0))],
        out_specs=[part(512), seq(LANE), seq(LANE), pl.BlockSpec((1, LANE), lambda b, h: (0, 0))],
        scratch_shapes=[pltpu.VMEM((nb, LANE, W), F32), pltpu.VMEM((nb, LANE, W), F32)],
        compiler_params=_cparams(("arbitrary", "arbitrary")),
    )(qs3, ks3, vs3, posb, posr, sinks, os3, do3, lse)


def _post(om, osw, g, w_out, x2, gate, fg, tgt, S, ts):
    T, D = x2.shape
    nsb = S // ts

    def body(om_ref, os_ref, g_ref, w_ref, x_ref, gate_ref, fg_ref, t_ref,
             dx2_ref, do_ref, dg_ref, loss_ref, dfg_ref, dgate_ref, dw_ref):
        i = pl.program_id(0)
        gv = g_ref[...]
        sg = 1.0 / (1.0 + jnp.exp(-gv))
        silu = gv * sg
        o = jnp.concatenate([om_ref[...], os_ref[...]], axis=1)
        ab = _mx(o * silu)
        y = _dot(ab, w_ref[...])
        gate = gate_ref[0]
        xo = x_ref[...] + gate * y
        r2 = lax.rsqrt(jnp.mean(xo * xo, axis=-1, keepdims=True) + EPS)
        xh = xo * r2
        fg = fg_ref[...]
        diff = xh * fg - t_ref[...]
        sq = jnp.sum(diff * diff, axis=0, keepdims=True)
        part = sq[:, 0:LANE]
        for t in range(1, D // LANE):
            part = part + sq[:, LANE * t:LANE * (t + 1)]
        dout = diff * (1.0 / D)
        dxh = dout * fg
        dx2 = r2 * (dxh - xh * jnp.mean(dxh * xh, axis=-1, keepdims=True))
        dx2_ref[...] = dx2
        dyb = _mx(dx2 * gate)
        da = _dot_nt(dyb, w_ref[...])
        do_ref[...] = da * silu
        dg_ref[...] = _mx(da * o * (sg * (1.0 + gv * (1.0 - sg))))

        @pl.when(i == 0)
        def _():
            loss_ref[...] = jnp.zeros_like(loss_ref)
            dfg_ref[...] = jnp.zeros_like(dfg_ref)
            dw_ref[...] = jnp.zeros_like(dw_ref)

        @pl.when(i % nsb == 0)
        def _():
            dgate_ref[...] = jnp.zeros_like(dgate_ref)

        loss_ref[...] += (0.5 / D) * part
        dfg_ref[...] += jnp.sum(dout * xh, axis=0, keepdims=True)
        dgate_ref[0] += jnp.sum(dx2 * y, axis=0, keepdims=True)
        dw_ref[...] += _dot_tn(ab, dyb)

        @pl.when(i == T // ts - 1)
        def _():
            chunks = [dw_ref[512 + 64 * PAIR_INV[h]:512 + 64 * (PAIR_INV[h] + 1), :] for h in range(SWA_HEADS)]
            for h in range(SWA_HEADS):
                dw_ref[512 + 64 * h:512 + 64 * (h + 1), :] = chunks[h]

    row = lambda w: pl.BlockSpec((ts, w), lambda i: (i, 0))
    full = lambda a: pl.BlockSpec(a.shape, lambda i: (0,) * a.ndim)
    per_b = pl.BlockSpec((1, 1, D), lambda i: (i // nsb, 0, 0))
    return pl.pallas_call(
        body, name="post", grid=(T // ts,),
        out_shape=[jax.ShapeDtypeStruct((T, D), F32), jax.ShapeDtypeStruct((T, 1024), F32),
                   jax.ShapeDtypeStruct((T, 1024), _MXU_DTYPE), jax.ShapeDtypeStruct((1, LANE), F32),
                   jax.ShapeDtypeStruct((1, D), F32), jax.ShapeDtypeStruct(gate.shape, F32),
                   jax.ShapeDtypeStruct(w_out.shape, F32)],
        in_specs=[row(512), row(512), row(1024), full(w_out), row(D), per_b, full(fg), row(D)],
        out_specs=[row(D), row(1024), row(1024),
                   pl.BlockSpec((1, LANE), lambda i: (0, 0)), pl.BlockSpec((1, D), lambda i: (0, 0)), per_b,
                   full(w_out)],
        compiler_params=_cparams(("arbitrary",)),
    )(om, osw, g, w_out, x2, gate, fg, tgt)


def _pre_bwd(dq, dk, dv, dqs, dks, dvs, dg, zq, zkv, ql, kvl, x2, dx2, scale, ng, w_in, gq, gkv, w_uq, w_uk, w_uv,
             rope, S, ts):
    T, D = x2.shape
    nsb = S // ts
    WQ = MLA_HEADS * LANE

    def body(dq_ref, dk_ref, dv_ref, dqs_ref, dks_ref, dvs_ref, dg_ref, zq_ref, zkv_ref, ql_ref, kvl_ref, x_ref, dx2_ref,
             sc_ref,
             ng_ref, w_ref, gq_ref, gkv_ref, wuq_ref, wuk_ref, wuv_ref, cd_ref, sd_ref,
             gx_ref, dz_ref, dgq_ref, dgkv_ref, dng_ref, dsh_ref, dsc_ref, dwuq_ref, dwuk_ref, dwuv_ref):
        i = pl.program_id(0)

        @pl.when(i == 0)
        def _():
            dgq_ref[...] = jnp.zeros_like(dgq_ref)
            dgkv_ref[...] = jnp.zeros_like(dgkv_ref)
            dng_ref[...] = jnp.zeros_like(dng_ref)
            dwuq_ref[...] = jnp.zeros_like(dwuq_ref)
            dwuk_ref[...] = jnp.zeros_like(dwuk_ref)
            dwuv_ref[...] = jnp.zeros_like(dwuv_ref)

        @pl.when(i % nsb == 0)
        def _():
            dsh_ref[...] = jnp.zeros_like(dsh_ref)
            dsc_ref[...] = jnp.zeros_like(dsc_ref)

        def norm_bwd(z, dy, gain):
            r = lax.rsqrt(jnp.mean(z * z, axis=-1, keepdims=True) + EPS)
            zh = z * r
            dzh = dy * gain
            return r * (dzh - zh * jnp.mean(dzh * zh, axis=-1, keepdims=True)), jnp.sum(dy * zh, axis=0, keepdims=True)

        tables = _rope_split(cd_ref[...], sd_ref[...])
        ng = ng_ref[...]
        sc1 = 1.0 + sc_ref[0]

        def rows_chain(rs):
            cq, ck, sa, sb = (t[rs] for t in tables)
            dqr = dq_ref[rs, :].astype(F32)
            dqb = _mx(dqr * _tile_heads(cq) + pltpu.roll(dqr * _tile_heads(sa), 16, 1)
                      + pltpu.roll(dqr * _tile_heads(sb), WQ - 16, 1))
            p_uq = _dot_tn(ql_ref[rs, :], dqb)
            dzq, dgq = norm_bwd(zq_ref[rs, :], _dot_nt(dqb, wuq_ref[...]), gq_ref[...])
            dkr = dk_ref[rs, :].astype(F32)
            dkb = _mx(dkr)
            p_uk = _dot_tn(kvl_ref[rs, :], dkb)
            dvb = _mx(dv_ref[rs, :])
            p_uv = _dot_tn(kvl_ref[rs, :], dvb)
            dzkv, dgkv = norm_bwd(zkv_ref[rs, :], _dot_nt(dkb, wuk_ref[...]) + _dot_nt(dvb, wuv_ref[...]), gkv_ref[...])
            dkpe = dkr[:, 0:LANE]
            for h in range(1, MLA_HEADS):
                dkpe = dkpe + dkr[:, LANE * h:LANE * (h + 1)]
            dkro = dkpe * ck + pltpu.roll(dkpe * sa, 16, 1) + pltpu.roll(dkpe * sb, LANE - 16, 1)
            dz_ref[rs, 0:384] = _mx(dzq)
            dz_ref[rs, 384:640] = _mx(dzkv)
            dz_ref[rs, 640:768] = _mx(dkro)
            dz_ref[rs, 768:1280] = dg_ref[rs, 0:512]
            dz_ref[rs, 1280:1792] = _mx(dqs_ref[rs, :])
            dz_ref[rs, 1792:1920] = _mx(dks_ref[rs, :])
            dz_ref[rs, 1920:2048] = _mx(dvs_ref[rs, :])
            dz_ref[rs, 2048:2560] = dg_ref[rs, 512:1024]
            dh = _dot_nt(dz_ref[rs, :], w_ref[...])
            x = x_ref[rs, :]
            r1 = lax.rsqrt(jnp.mean(x * x, axis=-1, keepdims=True) + EPS)
            xn = x * r1
            dxn = dh * ng * sc1
            gx_ref[rs, :] = dx2_ref[rs, :] + r1 * (dxn - xn * jnp.mean(dxn * xn, axis=-1, keepdims=True))
            return (p_uq, p_uk, p_uv, dgq, dgkv, jnp.sum(dh, axis=0, keepdims=True),
                    jnp.sum(dh * (xn * ng), axis=0, keepdims=True), jnp.sum(dh * xn * sc1, axis=0, keepdims=True))

        hr = ts // 2
        parts = [rows_chain(slice(hr * t, hr * (t + 1))) for t in range(2)]
        p_uq, p_uk, p_uv, dgq, dgkv, dsh, dsc, dng = (a + b for a, b in zip(*parts))
        dwuq_ref[...] += p_uq
        dwuk_ref[...] += p_uk
        dwuv_ref[...] += p_uv
        dgq_ref[...] += dgq
        dgkv_ref[...] += dgkv
        dsh_ref[0] += dsh
        dsc_ref[0] += dsc
        dng_ref[...] += dng

    row = lambda w: pl.BlockSpec((ts, w), lambda i: (i, 0))
    full = lambda a: pl.BlockSpec(a.shape, lambda i: (0,) * a.ndim, pipeline_mode=pl.Buffered(1))
    per_b = pl.BlockSpec((1, 1, D), lambda i: (i // nsb, 0, 0))
    dense = pl.BlockSpec((ts // 8, LANE), lambda i: (i, 0))
    vec = lambda w: pl.BlockSpec((1, w), lambda i: (0, 0))
    return pl.pallas_call(
        body, name="pre_bwd", grid=(T // ts,),
        out_shape=[jax.ShapeDtypeStruct((T, D), F32), jax.ShapeDtypeStruct((T, D_IN_PAD), _MXU_DTYPE), jax.ShapeDtypeStruct((1, 384), F32),
                   jax.ShapeDtypeStruct((1, 256), F32), jax.ShapeDtypeStruct((1, D), F32),
                   jax.ShapeDtypeStruct(scale.shape, F32), jax.ShapeDtypeStruct(scale.shape, F32),
                   jax.ShapeDtypeStruct(w_uq.shape, F32), jax.ShapeDtypeStruct(w_uk.shape, F32),
                   jax.ShapeDtypeStruct(w_uv.shape, F32)],
        in_specs=[row(WQ), row(WQ), row(512), row(512), row(LANE), row(LANE), row(1024), row(384), row(256), row(384),
                  row(256), row(D), row(D), per_b, full(ng), full(w_in), full(gq), full(gkv), full(w_uq), full(w_uk), full(w_uv),
                  dense, dense],
        out_specs=[row(D), row(D_IN_PAD), vec(384), vec(256), vec(D), per_b, per_b, full(w_uq), full(w_uk), full(w_uv)],
        compiler_params=_cparams(("arbitrary",)),
    )(dq, dk, dv, dqs, dks, dvs, dg, zq, zkv, ql, kvl, x2, dx2, scale, ng, w_in, gq, gkv, w_uq, w_uk, w_uv, *rope)


def _w_in_row_runs():
    runs = [(0, 0, 640), (640, 704, 32), (672, 768, 512)]
    runs += [(1184 + 64 * h, 1280 + 64 * PAIR_INV[h], 64) for h in range(8)]
    runs += [(1696, 1792, 256)]
    runs += [(1952 + 64 * h, 2048 + 64 * PAIR_INV[h], 64) for h in range(8)]
    return runs


def _dw_in_t(dz, hb, tn, tk):
    T, M = dz.shape
    N = hb.shape[1]
    nk = T // tk

    def body(a_ref, b_ref, o_ref, acc_ref):
        k = pl.program_id(1)

        @pl.when(k == 0)
        def _():
            acc_ref[...] = jnp.zeros_like(acc_ref)

        acc_ref[...] += _dot_tn(a_ref[...], b_ref[...])

        @pl.when(k == nk - 1)
        def _():
            for nat, pad, size in _w_in_row_runs():
                o_ref[nat:nat + size, :] = acc_ref[pad:pad + size, :]

    return pl.pallas_call(
        body, name="dw_in", grid=(N // tn, nk),
        out_shape=jax.ShapeDtypeStruct((D_IN, N), F32),
        in_specs=[pl.BlockSpec((tk, M), lambda j, k: (k, 0)), pl.BlockSpec((tk, tn), lambda j, k: (k, j))],
        out_specs=pl.BlockSpec((D_IN, tn), lambda j, k: (0, j)),
        scratch_shapes=[pltpu.VMEM((M, tn), F32)],
        compiler_params=_cparams(("arbitrary", "arbitrary")),
    )(dz, hb)


def _finalize(parts_all, dmod_all, dmod_cols, c_act_all):
    nparts = parts_all.shape[-1]
    nmod = dmod_all.shape[-1]

    def body(p_ref, dm_ref, dmc_ref, c_ref, ps_ref, loss_ref, db_ref, dw_ref):
        acc = p_ref[0]
        for j in range(1, N_DEV):
            acc = acc + p_ref[j]
        ps_ref[...] = acc
        loss_ref[...] = jnp.sum(acc[:, 0:LANE], axis=1, keepdims=True)
        db = dm_ref[0:1, :]
        for j in range(1, dm_ref.shape[0]):
            db = db + dm_ref[j:j + 1, :]
        db_ref[...] = db
        dw_ref[...] = _dot_tn(_mx(_silu(c_ref[...])), _mx(dmc_ref[...]))

    return pl.pallas_call(
        body, name="finalize",
        out_shape=[jax.ShapeDtypeStruct((1, nparts), F32), jax.ShapeDtypeStruct((1, 1), F32),
                   jax.ShapeDtypeStruct((1, nmod), F32),
                   jax.ShapeDtypeStruct((c_act_all.shape[1], dmod_cols.shape[1]), F32)],
        in_specs=[_vmem()] * 4, out_specs=[_vmem()] * 4,
        compiler_params=_cparams(),
    )(parts_all, dmod_all, dmod_cols, c_act_all)


def _adamw(ws, gs, ms, vs):
    n = len(ws)

    def body(*refs):
        w_refs, g_refs, m_refs, v_refs, d_refs, nm_refs, nv_refs = (refs[t * n:(t + 1) * n] for t in range(7))
        for t in range(n):
            gv = g_refs[t][...]
            nm = ADAM_B1 * m_refs[t][...] + (1.0 - ADAM_B1) * gv
            nv = ADAM_B2 * v_refs[t][...] + (1.0 - ADAM_B2) * (gv * gv)
            m_hat = nm / (1.0 - ADAM_B1 ** ADAM_STEP)
            v_hat = nv / (1.0 - ADAM_B2 ** ADAM_STEP)
            d_refs[t][...] = -ADAM_LR * (m_hat / (jnp.sqrt(v_hat) + ADAM_EPS) + ADAM_WD * w_refs[t][...])
            nm_refs[t][...] = nm
            nv_refs[t][...] = nv

    out = pl.pallas_call(
        body, name="adamw",
        out_shape=[jax.ShapeDtypeStruct(w.shape, F32) for w in ws] * 3,
        in_specs=[_vmem()] * (4 * n), out_specs=[_vmem()] * (3 * n),
        compiler_params=_cparams(),
    )(*ws, *gs, *ms, *vs)
    return out[:n], out[n:2 * n], out[2 * n:]


def _pair_perm(a, axis, order):
    a = jnp.moveaxis(a, axis, -1)
    lead = a.shape[:-1]
    a = a.reshape(lead + (8, 64))[..., list(order), :].reshape(lead + (512,))
    return jnp.moveaxis(a, -1, axis)


def _pad_w_in(w):
    z = lambda n: jnp.zeros((w.shape[0], n), w.dtype)
    return jnp.concatenate([
        w[:, 0:640], z(64), w[:, 640:672], z(32), w[:, 672:1184],
        _pair_perm(w[:, 1184:1696], 1, PAIR_ORDER), w[:, 1696:1952], _pair_perm(w[:, 1952:2464], 1, PAIR_ORDER)], axis=1)


def _rope_table(positions):
    T = positions.size
    inv = ROPE_THETA ** (-jnp.arange(0, MLA_ROPE, 2, dtype=F32) / MLA_ROPE)
    pos = jnp.repeat(positions.reshape(T // 8, 8)[:, ::-1].astype(F32), MLA_ROPE // 2, axis=1)
    ang = pos * jnp.tile(inv, 8)[None, :]
    return jnp.cos(ang), jnp.sin(ang)


def _local_step(x, mod, positions, ng, w_in, gq, gkv, w_uq, w_ukv, sinks, w_out, fg, tgt,
                ts=512, fq=512, fk=512, bq=512, bk=512):
    B, S, D = x.shape
    T = B * S
    x2 = x.reshape(T, D)
    shift, scale, gate = (mod[:, None, k * D:(k + 1) * D] for k in range(3))
    w_in_p = _pad_w_in(w_in)
    w_uq_p = jnp.pad(w_uq.reshape(Q_LORA, MLA_HEADS, 96), ((0, 0), (0, 0), (0, 32))).reshape(Q_LORA, MLA_HEADS * LANE)
    w_ukv3 = w_ukv.reshape(KV_LORA, MLA_HEADS, 128)
    w_uk_p = jnp.pad(w_ukv3[:, :, :64], ((0, 0), (0, 0), (0, 64))).reshape(KV_LORA, MLA_HEADS * LANE)
    w_uv = w_ukv3[:, :, 64:].reshape(KV_LORA, 512)
    w_out_p = jnp.concatenate([w_out[:512], _pair_perm(w_out[512:], 0, PAIR_ORDER)], axis=0)
    rope = _rope_table(positions)
    posf = positions.astype(F32)
    posb = jnp.broadcast_to(posf[:, :, None], (B, S, LANE))
    posr = posf.reshape(B, S // WINDOW, WINDOW)
    sinks_l = jnp.pad(sinks.reshape(1, SWA_HEADS), ((0, 0), (0, LANE - SWA_HEADS)))

    (hb, zq, zkv, ql, kvl, q, k, v, qs, ks, vs, g) = _pre_fwd(
        x2, shift, scale, ng, w_in_p, gq, gkv, w_uq_p, w_uk_p, w_uv, rope, S, ts)
    r3 = lambda a: a.reshape(B, S, a.shape[-1])
    om, lse_m = _mla_fwd(r3(q), r3(k), r3(v), fq, fk)
    osw, lse_s = _swa_fwd(r3(qs), r3(ks), r3(vs), posb, posr, sinks_l)
    dx2, do, dg, loss_v, dfg, dgate, dw_out = _post(
        om.reshape(T, 512), osw.reshape(T, 512), g, w_out_p, x2, gate, fg.reshape(1, D), tgt.reshape(T, D), S, ts)
    do3 = r3(do)
    dq, dk, dv = _mla_bwd(r3(q), r3(k), r3(v), om, do3, lse_m, bq, bk)
    dqs, dks, dvs, dsink = _swa_bwd(r3(qs), r3(ks), r3(vs), posb, posr, sinks_l, osw, do3, lse_s)
    f2 = lambda a: a.reshape(T, a.shape[-1])
    gx, dz, dgq, dgkv, dng, dsh, dsc, dw_uq_p, dw_uk_p, dw_uv = _pre_bwd(
        f2(dq), f2(dk), f2(dv), f2(dqs), f2(dks), f2(dvs), dg, zq, zkv, ql, kvl, x2, dx2, scale, ng, w_in_p, gq, gkv,
        w_uq_p, w_uk_p, w_uv, rope, S, ts)
    tk = min(T, 1024)
    dw_in_t = _dw_in_t(dz, hb, 512, tk)
    dw_uq = dw_uq_p.reshape(Q_LORA, MLA_HEADS, LANE)[:, :, :96].reshape(Q_LORA, 768)
    dw_uk = dw_uk_p.reshape(KV_LORA, MLA_HEADS, LANE)[:, :, :64]
    dw_uv = dw_uv.reshape(KV_LORA, MLA_HEADS, 64)
    dw_ukv = jnp.concatenate([dw_uk, dw_uv], axis=2).reshape(KV_LORA, 1024)
    parts = jnp.concatenate([loss_v, dfg, dng, dgq, dgkv, dsink], axis=1)
    dmod = jnp.concatenate([dsh, dsc, dgate], axis=2).reshape(B, 3 * D)
    return gx.reshape(B, S, D), dw_in_t, dw_uq, dw_ukv, dw_out, parts, dmod


def kernel(x, c, positions, w_ada, b_ada, norm_gain, w_in, q_norm_gain, kv_norm_gain, w_uq, w_ukv, swa_sinks, w_out, final_gain, loss_target, m_w_ada, m_b_ada, m_norm_gain, m_w_in, m_q_norm_gain, m_kv_norm_gain, m_w_uq, m_w_ukv, m_swa_sinks, m_w_out, m_final_gain, v_w_ada, v_b_ada, v_norm_gain, v_w_in, v_q_norm_gain, v_kv_norm_gain, v_w_uq, v_w_ukv, v_swa_sinks, v_w_out, v_final_gain):
    B, S, D = x.shape
    me = 4 * lax.axis_index("x") + 2 * lax.axis_index("y") + lax.axis_index("c")
    bf = _MXU_DTYPE

    ncol = w_ada.shape[2]
    b_cols = lax.dynamic_slice_in_dim(b_ada, me * ncol, ncol, axis=1)
    c_all, win_g, wuq_g, wukv_g, wout_g, mod_g = _all_gather(
        [c, w_in[0].astype(bf), w_uq[0].astype(bf), w_ukv[0].astype(bf), w_out[0].astype(bf)], "ag_weights",
        fused=((w_ada[0], b_cols), _ada_cols, jax.ShapeDtypeStruct((N_DEV, B, ncol), F32)))
    c_all = c_all.reshape(N_DEV * B, D)
    cat_cols = lambda a: jnp.transpose(a, (1, 0, 2)).reshape(a.shape[1], N_DEV * a.shape[2])
    w_in_f, w_uq_f, w_ukv_f = cat_cols(win_g), cat_cols(wuq_g), cat_cols(wukv_g)
    w_out_f = wout_g.reshape(D, D)

    mod = lax.dynamic_index_in_dim(mod_g, me, axis=1, keepdims=False)
    mod = jnp.transpose(mod, (1, 0, 2)).reshape(B, 3 * D)

    gx, dw_in_t, dw_uq, dw_ukv, dw_out, parts, dmod = _local_step(
        x, mod, positions, norm_gain, w_in_f, q_norm_gain, kv_norm_gain, w_uq_f, w_ukv_f, swa_sinks,
        w_out_f, final_gain, loss_target)

    parts_g, dmod_g = _all_gather([parts, dmod], "ag_small")
    dmod_all = dmod_g.reshape(N_DEV * B, 3 * D)
    dmod_cols = lax.dynamic_slice_in_dim(dmod_all, me * ncol, ncol, axis=1)
    psum, loss, g_b_ada, g_w_ada = _finalize(parts_g, dmod_all, dmod_cols, c_all)
    loss = loss.reshape(())
    o = LANE
    g_final_gain = psum[0, o:o + D]
    g_norm_gain = psum[:, o + D:o + 2 * D]
    o += 2 * D
    g_q_norm_gain = psum[:, o:o + Q_LORA]
    g_kv_norm_gain = psum[:, o + Q_LORA:o + Q_LORA + KV_LORA]
    o += Q_LORA + KV_LORA
    g_sinks = psum[:, o:o + SWA_HEADS]

    split_cols = lambda a: jnp.transpose(a.reshape(a.shape[0], 4, 2, a.shape[1] // N_DEV), (1, 2, 0, 3))
    split_rows = lambda a: a.reshape(4, 2, a.shape[0] // N_DEV, a.shape[1])
    g_w_in_t, g_w_uq, g_w_ukv, g_w_out = _reduce_scatter(
        [split_rows(dw_in_t), split_cols(dw_uq), split_cols(dw_ukv), split_rows(dw_out)], "rs_grads")
    g_w_in = g_w_in_t.T

    grads = [g_w_ada, g_b_ada, g_norm_gain, g_w_in, g_q_norm_gain, g_kv_norm_gain, g_w_uq, g_w_ukv, g_sinks, g_w_out,
             g_final_gain]
    ws = [w_ada, b_ada, norm_gain, w_in, q_norm_gain, kv_norm_gain, w_uq, w_ukv, swa_sinks, w_out, final_gain]
    ms = [m_w_ada, m_b_ada, m_norm_gain, m_w_in, m_q_norm_gain, m_kv_norm_gain, m_w_uq, m_w_ukv, m_swa_sinks, m_w_out,
          m_final_gain]
    vs = [v_w_ada, v_b_ada, v_norm_gain, v_w_in, v_q_norm_gain, v_kv_norm_gain, v_w_uq, v_w_ukv, v_swa_sinks, v_w_out,
          v_final_gain]
    two_d = [(1, w.shape[0]) if w.ndim == 1 else w.shape[-2:] for w in ws]
    flat = lambda arrs: [a.reshape(s) for a, s in zip(arrs, two_d)]
    deltas, new_ms, new_vs = _adamw(flat(ws), flat(grads), flat(ms), flat(vs))
    shaped = lambda arrs: [a.reshape(w.shape) for a, w in zip(arrs, ws)]
    return (loss, gx, *shaped(grads), *shaped(deltas), *shaped(new_ms), *shaped(new_vs))
```

```python
import functools

import numpy as np
import jax
import jax.numpy as jnp
from jax import lax
from jax.experimental import pallas as pl
from jax.experimental.pallas import tpu as pltpu

F32 = jnp.float32
_MXU_DTYPE = jnp.bfloat16
MLA_GRAD_DTYPE = jnp.bfloat16

N_DEV = 8
D_MODEL = 1024
MLA_HEADS = 8
MLA_NOPE = 64
MLA_ROPE = 32
MLA_V = 64
Q_LORA = 384
KV_LORA = 256
SWA_HEADS = 8
SWA_KV_HEADS = 2
SWA_HEAD_DIM = 64
WINDOW = 128
ROPE_THETA = 10000.0
EPS = 1e-6
MLA_SCALE = float((MLA_NOPE + MLA_ROPE) ** -0.5)
SWA_SCALE = float(SWA_HEAD_DIM ** -0.5)
LOG2E = 1.4426950408889634
MLA_QSCALE = MLA_SCALE * LOG2E
D_IN = 2464
D_IN_PAD = 2560
PAIR_ORDER = (0, 4, 1, 5, 2, 6, 3, 7)
PAIR_INV = (0, 2, 4, 6, 1, 3, 5, 7)

ADAM_LR = 0.001
ADAM_B1 = 0.9
ADAM_B2 = 0.999
ADAM_EPS = 1e-08
ADAM_WD = 0.01
ADAM_STEP = 10

LANE = 128
VMEM_LIMIT = 56 * 1024 * 1024

MESH = pl.DeviceIdType.MESH
NEG_INF = float("-inf")
SWA_SEQ_SPLIT = 2


def _mx(a):
    return a.astype(_MXU_DTYPE)


def _dot(a, b):
    return jnp.dot(a, b, preferred_element_type=F32)


def _dot_nt(a, b):
    return lax.dot_general(a, b, (((1,), (1,)), ((), ())), preferred_element_type=F32)


def _dot_tn(a, b):
    return lax.dot_general(a, b, (((0,), (0,)), ((), ())), preferred_element_type=F32)


def _cparams(sem=None):
    return pltpu.CompilerParams(dimension_semantics=sem, vmem_limit_bytes=VMEM_LIMIT)


def _vmem():
    return pl.BlockSpec(memory_space=pltpu.VMEM)


def _lane_iota(shape):
    return lax.broadcasted_iota(jnp.int32, shape, len(shape) - 1)


def _all_gather(arrs, name, fused=None):
    n = len(arrs)
    extra, fn, piece = fused if fused else ((), None, None)
    ne, m = len(extra), n + (1 if fused else 0)

    def body(*refs):
        ins, ex, outs = refs[:n], refs[n:n + ne], refs[n + ne:n + ne + m]
        rest = refs[n + ne + m:]
        srcs = list(ins) + ([rest[0]] if fused else [])
        send_sems, recv_sems, local_sems = rest[-3:]
        x, y, c = lax.axis_index("x"), lax.axis_index("y"), lax.axis_index("c")
        me, sibling = (x, y, c), (x, y, 1 - c)
        chips = [(1 - x, y), (x, 1 - y), (1 - x, 1 - y)]

        def slot(a, dev):
            return outs[a].at[4 * dev[0] + 2 * dev[1] + dev[2]]

        def copy(a, k, block, to, src=None):
            return pltpu.make_async_remote_copy(
                src_ref=slot(a, block) if src is None else src, dst_ref=slot(a, block),
                send_sem=send_sems.at[7 * a + k], recv_sem=recv_sems.at[7 * a + k],
                device_id=to, device_id_type=MESH)

        def local(a):
            return pltpu.make_async_copy(srcs[a], slot(a, me), local_sems.at[a])

        def start(a):
            local(a).start()
            cps = [copy(a, 0, me, sibling, src=srcs[a])]
            cps += [copy(a, 1 + j, me, (*chip, c), src=srcs[a]) for j, chip in enumerate(chips)]
            for cp in cps:
                cp.start()
            return cps

        def finish(group):
            cps = []
            for j, chip in enumerate(chips):
                for a in group:
                    copy(a, 1 + j, (*chip, c), me).wait_recv()
                    cp = copy(a, 4 + j, (*chip, c), sibling)
                    cp.start()
                    cps.append(cp)
            for a in group:
                copy(a, 0, sibling, me).wait_recv()
                for j, chip in enumerate(chips):
                    copy(a, 4 + j, (*chip, 1 - c), me).wait_recv()
                local(a).wait()
            return cps

        pending = []
        for a in range(n):
            pending += start(a)
        if fused:
            pending += finish([0])
            fn(srcs[n], outs[0], *ex)
            pending += start(n)
            pending += finish([n] + list(range(1, n)))
        else:
            pending += finish(list(range(n)))
        for cp in pending:
            cp.wait_send()

    out_shape = [jax.ShapeDtypeStruct((N_DEV,) + a.shape, a.dtype) for a in arrs]
    scratch = []
    if fused:
        out_shape.append(jax.ShapeDtypeStruct((N_DEV,) + piece.shape, piece.dtype))
        scratch.append(pltpu.VMEM(piece.shape, piece.dtype))
    return pl.pallas_call(
        body, name=name, out_shape=out_shape,
        in_specs=[_vmem()] * (n + ne), out_specs=[_vmem()] * m,
        scratch_shapes=scratch + [pltpu.SemaphoreType.DMA((7 * m,)), pltpu.SemaphoreType.DMA((7 * m,)),
                                  pltpu.SemaphoreType.DMA((m,))],
        compiler_params=pltpu.CompilerParams(vmem_limit_bytes=VMEM_LIMIT),
    )(*arrs, *extra)


def _reduce_scatter(arrs, name):
    n = len(arrs)
    halves = [a.astype(jnp.bfloat16) for a in arrs]

    def body(*refs):
        xs, xbs, outs = refs[:n], refs[n:2 * n], refs[2 * n:3 * n]
        parts, recv_a, send_b, recv_b = (refs[(3 + t) * n:(4 + t) * n] for t in range(4))
        send_sems, recv_sems, local_sems = refs[7 * n:]
        x, y, c = lax.axis_index("x"), lax.axis_index("y"), lax.axis_index("c")
        myq = 2 * x + y

        def chip(k):
            return (1 - x if k & 2 else x, 1 - y if k & 1 else y)

        def from_sibling(a):
            return pltpu.make_async_remote_copy(
                src_ref=xbs[a].at[:, 1 - c], dst_ref=recv_a[a], send_sem=send_sems.at[4 * a], recv_sem=recv_sems.at[4 * a],
                device_id=(x, y, 1 - c), device_id_type=MESH)

        def to_owner(a, k):
            qx, qy = chip(k)
            return pltpu.make_async_remote_copy(
                src_ref=send_b[a].at[2 * qx + qy], dst_ref=recv_b[a].at[myq],
                send_sem=send_sems.at[4 * a + k], recv_sem=recv_sems.at[4 * a + k],
                device_id=(qx, qy, c), device_id_type=MESH)

        mine = [pltpu.make_async_copy(xs[a].at[:, c], parts[a], local_sems.at[a]) for a in range(n)]
        first = [from_sibling(a) for a in range(n)]
        for cp in mine + first:
            cp.start()
        second = []
        for a in range(n):
            mine[a].wait()
            first[a].wait_recv()
            parts[a][...] = parts[a][...] + recv_a[a][...].astype(F32)
            send_b[a][...] = parts[a][...].astype(jnp.bfloat16)
            for k in range(1, 4):
                cp = to_owner(a, k)
                cp.start()
                second.append(cp)
        for a in range(n):
            acc = parts[a][myq]
            for k in range(1, 4):
                to_owner(a, k).wait_recv()
                qx, qy = chip(k)
                acc = acc + recv_b[a][2 * qx + qy].astype(F32)
            outs[a][...] = acc
        for cp in first + second:
            cp.wait_send()

    quarter = lambda a, dt: pltpu.VMEM((4,) + a.shape[2:], dt)
    return pl.pallas_call(
        body, name=name,
        out_shape=[jax.ShapeDtypeStruct(a.shape[2:], F32) for a in arrs],
        in_specs=[pl.BlockSpec(memory_space=pl.ANY)] * (2 * n), out_specs=[_vmem()] * n,
        scratch_shapes=[quarter(a, F32) for a in arrs] + [quarter(a, jnp.bfloat16) for a in arrs] * 3
        + [pltpu.SemaphoreType.DMA((4 * n,)), pltpu.SemaphoreType.DMA((4 * n,)), pltpu.SemaphoreType.DMA((n,))],
        compiler_params=pltpu.CompilerParams(vmem_limit_bytes=VMEM_LIMIT),
    )(*arrs, *halves)


def _silu(t):
    return t * (1.0 / (1.0 + jnp.exp(-t)))


def _ada_cols(piece_ref, c_all_ref, w_ref, b_ref):
    w = _mx(w_ref[...])
    for d in range(N_DEV):
        piece_ref[d] = _dot(_mx(_silu(c_all_ref[d])), w) + b_ref[...]


def _rope_split(cd, sd):
    n = cd.shape[0]

    def expand(d):
        rep = jnp.broadcast_to(d[:, None, :], (n, 8, LANE)).reshape(8 * n, LANE)
        return pltpu.roll(rep, 0, 1, stride=16, stride_axis=0)

    c, s = expand(cd), expand(sd)
    lane = _lane_iota(c.shape)
    first = jnp.logical_and(lane >= 64, lane < 80)
    second = jnp.logical_and(lane >= 80, lane < 96)
    ck = jnp.where(first, pltpu.roll(c, 80, 1), jnp.where(second, pltpu.roll(c, 96, 1), 0.0))
    cq = jnp.where(lane < 64, 1.0, ck)
    sa = jnp.where(first, -pltpu.roll(s, 80, 1), 0.0)
    sb = jnp.where(second, pltpu.roll(s, 96, 1), 0.0)
    return cq, ck, sa, sb


def _tile_heads(t):
    return jnp.concatenate([t] * MLA_HEADS, axis=1)


def _pre_fwd(x2, shift, scale, ng, w_in, gq, gkv, w_uq, w_uk, w_uv, rope, S, ts):
    T, D = x2.shape
    nsb = S // ts
    WQ = MLA_HEADS * LANE

    def body(x_ref, sh_ref, sc_ref, ng_ref, w_ref, gq_ref, gkv_ref, wuq_ref, wuk_ref, wuv_ref,
             cd_ref, sd_ref,
             hb_ref, zq_ref, zkv_ref, ql_ref, kvl_ref, q_ref, k_ref, v_ref, qs_ref, ks_ref, vs_ref, g_ref):
        x = x_ref[...]
        r1 = lax.rsqrt(jnp.mean(x * x, axis=-1, keepdims=True) + EPS)
        h = ((x * r1) * ng_ref[...]) * (1.0 + sc_ref[0]) + sh_ref[0]
        hb = _mx(h)
        hb_ref[...] = hb
        zq = _dot_nt(hb, w_ref[0:384, :])
        zq_ref[...] = zq
        rq = lax.rsqrt(jnp.mean(zq * zq, axis=-1, keepdims=True) + EPS)
        ql = _mx((zq * rq) * gq_ref[...])
        ql_ref[...] = ql
        q = _dot(ql, wuq_ref[...])
        cq, ck, sa, sb = _rope_split(cd_ref[...], sd_ref[...])
        q = (q * _tile_heads(cq) + pltpu.roll(q, WQ - 16, 1) * _tile_heads(sa)
             + pltpu.roll(q, 16, 1) * _tile_heads(sb))
        q_ref[...] = _mx(q * MLA_QSCALE)
        zkv = _dot_nt(hb, w_ref[384:640, :])
        zkv_ref[...] = zkv
        rkv = lax.rsqrt(jnp.mean(zkv * zkv, axis=-1, keepdims=True) + EPS)
        kvl = _mx((zkv * rkv) * gkv_ref[...])
        kvl_ref[...] = kvl
        kr = _dot_nt(hb, w_ref[640:768, :])
        kpe = kr * ck + pltpu.roll(kr, LANE - 16, 1) * sa + pltpu.roll(kr, 16, 1) * sb
        kf = _dot(kvl, wuk_ref[...])
        k_ref[...] = _mx(kf + jnp.concatenate([kpe] * MLA_HEADS, axis=1))
        v_ref[...] = _mx(_dot(kvl, wuv_ref[...]))
        g_ref[:, 0:512] = _dot_nt(hb, w_ref[768:1280, :])
        qs_ref[...] = _mx(_dot_nt(hb, w_ref[1280:1792, :]) * (SWA_SCALE * LOG2E))
        ks_ref[...] = _mx(_dot_nt(hb, w_ref[1792:1920, :]))
        vs_ref[...] = _mx(_dot_nt(hb, w_ref[1920:2048, :]))
        g_ref[:, 512:1024] = _dot_nt(hb, w_ref[2048:2560, :])

    row = lambda w: pl.BlockSpec((ts, w), lambda i: (i, 0))
    dense = pl.BlockSpec((ts // 8, LANE), lambda i: (i, 0))
    full = lambda a: pl.BlockSpec(a.shape, lambda i: (0,) * a.ndim)
    per_b = pl.BlockSpec((1, 1, D), lambda i: (i // nsb, 0, 0))
    out_w = [(D, _MXU_DTYPE), (384, F32), (256, F32), (384, _MXU_DTYPE), (256, _MXU_DTYPE), (WQ, _MXU_DTYPE),
             (WQ, _MXU_DTYPE), (512, _MXU_DTYPE), (512, _MXU_DTYPE), (128, _MXU_DTYPE), (128, _MXU_DTYPE), (1024, F32)]
    return pl.pallas_call(
        body, name="pre_fwd", grid=(T // ts,),
        out_shape=[jax.ShapeDtypeStruct((T, w), dt) for w, dt in out_w],
        in_specs=[row(D), per_b, per_b, full(ng), full(w_in), full(gq), full(gkv), full(w_uq), full(w_uk), full(w_uv),
                  dense, dense],
        out_specs=[row(w) for w, _ in out_w],
        compiler_params=_cparams(("arbitrary",)),
    )(x2, shift, scale, ng, w_in, gq, gkv, w_uq, w_uk, w_uv, *rope)


def _mla_fwd(q3, k3, v3, tq, tk):
    B, S, _ = q3.shape
    nq = S // tq
    assert tk % tq == 0 or tq % tk == 0
    n_masked = max(1, tq // tk)

    HPS = 2
    NH = 2 * HPS

    def body(q_ref, k_ref, v_ref, o_ref, lse_ref):
        qi = pl.program_id(2)
        rows = lax.broadcasted_iota(jnp.int32, (tq, tk), 0)
        cols = lax.broadcasted_iota(jnp.int32, (tq, tk), 1)
        qs = [q_ref[0, :, LANE * j:LANE * (j + 1)] for j in range(NH)]
        n_full = (qi * tq) // tk

        low_k = _lane_iota((tk, LANE)) < 64

        def step(kt, carry, masked):
            r0 = pl.multiple_of(kt * tk, tk)
            out = []
            for j in range(NH):
                v2 = v_ref[0, pl.ds(r0, tk), LANE * (j // 2):LANE * (j // 2 + 1)]
                vj = (jnp.where(low_k, v2, 1.0) if j % 2 == 0 else jnp.where(low_k, 1.0, v2)).astype(v2.dtype)
                m, acc = carry[j]
                s = _dot_nt(qs[j], k_ref[0, pl.ds(r0, tk), LANE * j:LANE * (j + 1)])
                if masked:
                    s = jnp.where(rows + qi * tq >= cols + kt * tk, s, NEG_INF)
                m_new = jnp.maximum(m, jnp.max(s, axis=1, keepdims=True))
                alpha = jnp.exp2(m - m_new)
                p = jnp.exp2(s - m_new)
                acc = alpha * acc + _dot(_mx(p), vj)
                out.append((m_new, acc))
            return tuple(out)

        init = (jnp.full((tq, 1), NEG_INF, F32), jnp.zeros((tq, LANE), F32))
        carry = lax.fori_loop(0, n_full, functools.partial(step, masked=False), (init,) * NH)
        for t in range(n_masked):
            carry = step(n_full + t, carry, True)
        low = _lane_iota((tq, LANE)) < 64
        for t in range(HPS):
            (m0, a0), (m1, a1) = carry[2 * t], carry[2 * t + 1]
            l0 = jnp.where(low, pltpu.roll(a0, 64, 1), a0)
            l1 = jnp.where(low, a1, pltpu.roll(a1, 64, 1))
            o_ref[0, :, LANE * t:LANE * (t + 1)] = jnp.where(low, a0 / l0, a1 / l1)
            lse_ref[0, 2 * t] = m0 + jnp.log2(l0)
            lse_ref[0, 2 * t + 1] = m1 + jnp.log2(l1)

    return pl.pallas_call(
        body, name="mla_fwd", grid=(B, MLA_HEADS // NH, nq),
        out_shape=[jax.ShapeDtypeStruct((B, S, 512), F32), jax.ShapeDtypeStruct((B, MLA_HEADS, S, LANE), F32)],
        in_specs=[pl.BlockSpec((1, tq, NH * LANE), lambda b, hp, i: (b, i, hp)),
                  pl.BlockSpec((1, S, NH * LANE), lambda b, hp, i: (b, 0, hp)),
                  pl.BlockSpec((1, S, HPS * LANE), lambda b, hp, i: (b, 0, hp))],
        out_specs=[pl.BlockSpec((1, tq, HPS * LANE), lambda b, hp, i: (b, i, hp)),
                   pl.BlockSpec((1, NH, tq, LANE), lambda b, hp, i: (b, hp, i, 0))],
        compiler_params=_cparams(("arbitrary", "arbitrary", "arbitrary")),
    )(q3, k3, v3)


def _mla_bwd(q3, k3, v3, o3, do3, lse, tq, tk, hps=2):
    B, S, _ = q3.shape
    nq, nk = S // tq, S // tk
    nh = 2 * hps
    assert tk % tq == 0

    def body(q_ref, k_ref, v_ref, o_ref, do_ref, lse_ref, dq_ref, dk_ref, dv_ref, dkt_ref, dvt_ref):
        dkt_ref[...] = jnp.zeros_like(dkt_ref)
        dvt_ref[...] = jnp.zeros_like(dvt_ref)
        rows = lax.broadcasted_iota(jnp.int32, (tq, tk), 0)
        cols = lax.broadcasted_iota(jnp.int32, (tq, tk), 1)
        lane = _lane_iota((tq, LANE))

        def q_tile(qi, _):
            r = pl.ds(pl.multiple_of(qi * tq, tq), tq)
            heads = []
            for j in range(nh):
                lanes = slice(LANE * j, LANE * (j + 1))
                pair = slice(LANE * (j // 2), LANE * (j // 2 + 1))
                do = jnp.where((lane < 64) if j % 2 == 0 else (lane >= 64), do_ref[0, r, pair], 0.0)
                q = q_ref[0, r, lanes]
                heads.append((lanes, pair, q, _mx(q.astype(F32).T), _mx(do), _mx(do.T),
                              jnp.sum(do * o_ref[0, r, pair], axis=1, keepdims=True),
                              jnp.concatenate([lse_ref[0, j, r, :]] * (tk // LANE), axis=1)))
            n_full = (qi * tq) // tk

            def k_tile(kt, dqs, masked):
                kr = pl.ds(pl.multiple_of(kt * tk, tk), tk)
                out = []
                dvt = [None] * hps
                for j, (lanes, pair, q, qt, dob, dot_, dcol, lse_c) in enumerate(heads):
                    k = k_ref[0, kr, lanes]
                    s = _dot_nt(q, k)
                    if masked:
                        s = jnp.where(rows + qi * tq >= cols + kt * tk, s, NEG_INF)
                    p = jnp.exp2(s - lse_c)
                    dp = _dot_nt(dob, v_ref[0, kr, pair])
                    dsb = _mx(p * (dp - dcol))
                    dkt_ref[j, kt] += _dot(qt, dsb)
                    pv = _dot(dot_, _mx(p))
                    dvt[j // 2] = pv if dvt[j // 2] is None else dvt[j // 2] + pv
                    out.append(dqs[j] + _dot(dsb, k))
                for t in range(hps):
                    dvt_ref[t, kt] += dvt[t]
                return tuple(out)

            zero = jnp.zeros((tq, LANE), F32)
            dqs = lax.fori_loop(0, n_full, functools.partial(k_tile, masked=False), (zero,) * nh)
            dqs = k_tile(n_full, dqs, True)
            for j in range(nh):
                dq_ref[0, r, heads[j][0]] = (MLA_SCALE * dqs[j]).astype(dq_ref.dtype)
            return 0

        lax.fori_loop(0, nq, q_tile, 0)

        def flush(kt, _):
            kr = pl.ds(pl.multiple_of(kt * tk, tk), tk)
            for j in range(nh):
                dk_ref[0, kr, LANE * j:LANE * (j + 1)] = ((1.0 / LOG2E) * dkt_ref[j, kt].T).astype(dk_ref.dtype)
            for t in range(hps):
                dv_ref[0, kr, LANE * t:LANE * (t + 1)] = dvt_ref[t, kt].T.astype(dv_ref.dtype)
            return 0

        lax.fori_loop(0, nk, flush, 0)

    grp = lambda w: pl.BlockSpec((1, S, w), lambda b, hp: (b, 0, hp))
    return pl.pallas_call(
        body, name="mla_bwd", grid=(B, MLA_HEADS // nh),
        out_shape=[jax.ShapeDtypeStruct((B, S, 1024), MLA_GRAD_DTYPE), jax.ShapeDtypeStruct((B, S, 1024), MLA_GRAD_DTYPE),
                   jax.ShapeDtypeStruct((B, S, 512), MLA_GRAD_DTYPE)],
        in_specs=[grp(nh * LANE), grp(nh * LANE), grp(hps * LANE), grp(hps * LANE), grp(hps * LANE),
                  pl.BlockSpec((1, nh, S, LANE), lambda b, hp: (b, hp, 0, 0))],
        out_specs=[grp(nh * LANE), grp(nh * LANE), grp(hps * LANE)],
        scratch_shapes=[pltpu.VMEM((nh, nk, LANE, tk), F32), pltpu.VMEM((hps, nk, LANE, tk), F32)],
        compiler_params=_cparams(("arbitrary", "arbitrary")),
    )(q3, k3, v3, o3, do3, lse)


def _swa_consts(sink_ref):
    W = WINDOW
    row = lax.broadcasted_iota(jnp.int32, (4 * W, LANE), 0)
    out = []
    for g in range(SWA_KV_HEADS):
        slope = jnp.zeros((4 * W, LANE), F32)
        sink = jnp.zeros((4 * W, LANE), F32)
        for p in range(4):
            h = p + 4 * g
            here = jnp.logical_and(row >= W * p, row < W * (p + 1))
            slope = jnp.where(here, float(LOG2E * 2.0 ** (-8.0 * (h + 1) / SWA_HEADS)), slope)
            sink = jnp.where(here, LOG2E * sink_ref[:, h:h + 1], sink)
        out.append((jnp.concatenate([slope, slope], axis=1), sink))
    return out


def _wide(col):
    return jnp.concatenate([col, col], axis=1)


def _swa_block(q_ref, k_ref, v_ref, pb_ref, pr_ref, n, i, ij):
    W = WINDOW
    kb = jnp.maximum(n - 1, 0)
    r = pl.ds(pl.multiple_of(i * W, W), W)
    kr = pl.ds(pl.multiple_of(kb * W, W), 2 * W)
    q4, k2, v2 = q_ref[0, r, :], k_ref[0, kr, :], v_ref[0, kr, :]
    pq = _wide(pb_ref[0, r, :])
    pk = jnp.concatenate([pr_ref[0, pl.ds(kb, 1), :], pr_ref[0, pl.ds(kb + 1, 1), :]], axis=1)
    rel = ij + (n - kb) * W
    mask = jnp.where(jnp.logical_and(rel >= 0, rel < W), 0.0, NEG_INF)
    dist4 = jnp.concatenate([pq - pk] * 4, axis=0)
    mask4 = jnp.concatenate([mask] * 4, axis=0)
    return r, kb, kr, q4, k2, v2, dist4, mask4


def _swa_stack(x4, g, dtype):
    lane = _lane_iota((WINDOW, LANE))
    mine = (lane < 64) if g == 0 else (lane >= 64)
    return jnp.concatenate([jnp.where(mine, x4[:, LANE * p:LANE * (p + 1)], 0).astype(dtype) for p in range(4)], axis=0)


def _swa_unstack(ref, r, lo, hi, scale=None):
    W = WINDOW
    low = _lane_iota((W, LANE)) < 64
    for p in range(4):
        t = jnp.where(low, lo[W * p:W * (p + 1)], hi[W * p:W * (p + 1)])
        ref[0, r, LANE * p:LANE * (p + 1)] = t if scale is None else scale * t


def _swa_fwd(qs3, ks3, vs3, posb, posr, sinks):
    B, S, _ = qs3.shape
    W = WINDOW
    nb = S // W
    nh = SWA_SEQ_SPLIT
    nbh = nb // nh

    def body(q_ref, k_ref, v_ref, pb_ref, pr_ref, sink_ref, o_ref, lse_ref):
        ij = lax.broadcasted_iota(jnp.int32, (W, 2 * W), 0) - lax.broadcasted_iota(jnp.int32, (W, 2 * W), 1)
        consts = _swa_consts(sink_ref)
        n0 = pl.program_id(1) * nbh

        def blk(i, _):
            n = n0 + i
            r, _, _, q4, k2, v2, dist4, mask4 = _swa_block(q_ref, k_ref, v_ref, pb_ref, pr_ref, n, i, ij)
            o_g = []
            for g, (slope, sink) in enumerate(consts):
                s = _dot_nt(_swa_stack(q4, g, q4.dtype), k2) - slope * dist4 + mask4
                m = jnp.maximum(jnp.max(s, axis=1, keepdims=True), sink)
                e = jnp.exp2(s - _wide(m))
                l = jnp.sum(e, axis=1, keepdims=True) + jnp.exp2(sink - m)
                o_g.append(_dot(_mx(e), v2) * (1.0 / l))
                lse_ref[0, i, g] = m + jnp.log2(l)
            _swa_unstack(o_ref, r, o_g[0], o_g[1])
            return 0

        lax.fori_loop(0, nbh, blk, 0, unroll=2)

    seq = lambda w: pl.BlockSpec((1, S, w), lambda b, h: (b, 0, 0))
    part = lambda w: pl.BlockSpec((1, S // nh, w), lambda b, h: (b, h, 0))
    lse_spec = pl.BlockSpec((1, nbh, 2, 4 * W, LANE), lambda b, h: (b, h, 0, 0, 0))
    return pl.pallas_call(
        body, name="swa_fwd", grid=(B, nh),
        out_shape=[jax.ShapeDtypeStruct((B, S, 512), F32), jax.ShapeDtypeStruct((B, nb, 2, 4 * W, LANE), F32)],
        in_specs=[part(512), seq(LANE), seq(LANE), part(LANE), pl.BlockSpec((1, nb, W), lambda b, h: (b, 0, 0)),
                  pl.BlockSpec((1, LANE), lambda b, h: (0, 0))],
        out_specs=[part(512), lse_spec],
        compiler_params=_cparams(("arbitrary", "arbitrary")),
    )(qs3, ks3, vs3, posb, posr, sinks)


def _swa_bwd(qs3, ks3, vs3, posb, posr, sinks, os3, do3, lse):
    B, S, _ = qs3.shape
    W = WINDOW
    nb = S // W
    nh = SWA_SEQ_SPLIT
    nbh = nb // nh
    assert nbh % 2 == 0 and nbh * nh * W == S

    def body(q_ref, k_ref, v_ref, pb_ref, pr_ref, sink_ref, o_ref, do_ref, lse_ref, dq_ref, dk_ref, dv_ref, dsink_ref,
             dkt_ref, dvt_ref):
        lane1 = _lane_iota((1, LANE))
        ij = lax.broadcasted_iota(jnp.int32, (W, 2 * W), 0) - lax.broadcasted_iota(jnp.int32, (W, 2 * W), 1)
        consts = _swa_consts(sink_ref)
        hh = pl.program_id(1)
        n0 = hh * nbh

        @pl.when(hh == 0)
        def _():
            dkt_ref[...] = jnp.zeros_like(dkt_ref)
            dvt_ref[...] = jnp.zeros_like(dvt_ref)

        @pl.when(jnp.logical_and(pl.program_id(0) == 0, hh == 0))
        def _():
            dsink_ref[...] = jnp.zeros_like(dsink_ref)

        def blk(i, dsink):
            n = n0 + i
            r, kb, _, q4, k2, v2, dist4, mask4 = _swa_block(q_ref, k_ref, v_ref, pb_ref, pr_ref, n, i, ij)
            o4, do4 = o_ref[0, r, :], do_ref[0, r, :]
            dq_g = []
            dkt = jnp.zeros((LANE, 2 * W), F32)
            dvt = jnp.zeros((LANE, 2 * W), F32)
            for g, (slope, sink) in enumerate(consts):
                q_st = _swa_stack(q4, g, F32)
                do_st = _swa_stack(do4, g, F32)
                dcol = jnp.sum(do_st * _swa_stack(o4, g, F32), axis=1, keepdims=True)
                qst, dob = _mx(q_st), _mx(do_st)
                lse_c = lse_ref[0, i, g]
                pr = jnp.exp2(_dot_nt(qst, k2) - slope * dist4 + mask4 - _wide(lse_c))
                dsb = _mx(pr * (_dot_nt(dob, v2) - dcol))
                psd = jnp.exp2(sink - lse_c)[:, 0:1] * dcol
                for p in range(4):
                    dsink = dsink - jnp.where(lane1 == p + 4 * g,
                                              jnp.sum(psd[W * p:W * (p + 1)], axis=0, keepdims=True), 0.0)
                dq_g.append(_dot(dsb, k2))
                dkt = dkt + _dot(_mx(q_st.T), dsb)
                dvt = dvt + _dot(_mx(do_st.T), _mx(pr))
            _swa_unstack(dq_ref, r, dq_g[0], dq_g[1], SWA_SCALE)
            dkt_ref[kb] += dkt[:, 0:W]
            dkt_ref[kb + 1] += dkt[:, W:2 * W]
            dvt_ref[kb] += dvt[:, 0:W]
            dvt_ref[kb + 1] += dvt[:, W:2 * W]
            return dsink

        dsink_ref[...] += lax.fori_loop(0, nbh // 2, lambda t, d: blk(2 * t + 1, blk(2 * t, d)), jnp.zeros((1, LANE), F32))

        @pl.when(hh == nh - 1)
        def _():
            def flush(n, _):
                r = pl.ds(pl.multiple_of(n * W, W), W)
                dk_ref[0, r, :] = (1.0 / LOG2E) * dkt_ref[n].T
                dv_ref[0, r, :] = dvt_ref[n].T
                return 0

            lax.fori_loop(0, nb, flush, 0)

    seq = lambda w: pl.BlockSpec((1, S, w), lambda b, h: (b, 0, 0))
    part = lambda w: pl.BlockSpec((1, S // nh, w), lambda b, h: (b, h, 0))
    return pl.pallas_call(
        body, name="swa_bwd", grid=(B, nh),
        out_shape=[jax.ShapeDtypeStruct((B, S, 512), F32), jax.ShapeDtypeStruct((B, S, LANE), F32),
                   jax.ShapeDtypeStruct((B, S, LANE), F32), jax.ShapeDtypeStruct((1, LANE), F32)],
        in_specs=[part(512), seq(LANE), seq(LANE), part(LANE), pl.BlockSpec((1, nb, W), lambda b, h: (b, 0, 0)),
                  pl.BlockSpec((1, LANE), lambda b, h: (0, 0)), part(512),
                  pl.BlockSpec((1, S // nh, 512), lambda b, h: (b, h, 1)),
                  pl.BlockSpec((1, nbh, 2, 4 * W, LANE), lambda b, h: (b, h, 0, 0, 0))],
        out_specs=[part(512), seq(LANE), seq(LANE), pl.BlockSpec((1, LANE), lambda b, h: (0, 0))],
        scratch_shapes=[pltpu.VMEM((nb, LANE, W), F32), pltpu.VMEM((nb, LANE, W), F32)],
        compiler_params=_cparams(("arbitrary", "arbitrary")),
    )(qs3, ks3, vs3, posb, posr, sinks, os3, do3, lse)


def _post(om, osw, g, w_out, x2, gate, fg, tgt, S, ts):
    T, D = x2.shape
    nsb = S // ts

    def body(om_ref, os_ref, g_ref, w_ref, x_ref, gate_ref, fg_ref, t_ref,
             dx2_ref, do_ref, dg_ref, loss_ref, dfg_ref, dgate_ref, dw_ref):
        i = pl.program_id(0)
        gv = g_ref[...]
        sg = 1.0 / (1.0 + jnp.exp(-gv))
        silu = gv * sg
        o = jnp.concatenate([om_ref[...], os_ref[...]], axis=1)
        ab = _mx(o * silu)
        y = _dot(ab, w_ref[...])
        gate = gate_ref[0]
        xo = x_ref[...] + gate * y
        r2 = lax.rsqrt(jnp.mean(xo * xo, axis=-1, keepdims=True) + EPS)
        xh = xo * r2
        fg = fg_ref[...]
        diff = xh * fg - t_ref[...]
        sq = jnp.sum(diff * diff, axis=0, keepdims=True)
        part = sq[:, 0:LANE]
        for t in range(1, D // LANE):
            part = part + sq[:, LANE * t:LANE * (t + 1)]
        dout = diff * (1.0 / D)
        dxh = dout * fg
        dx2 = r2 * (dxh - xh * jnp.mean(dxh * xh, axis=-1, keepdims=True))
        dx2_ref[...] = dx2
        dyb = _mx(dx2 * gate)
        da = _dot_nt(dyb, w_ref[...])
        do_ref[...] = da * silu
        dg_ref[...] = _mx(da * o * (sg * (1.0 + gv * (1.0 - sg))))

        @pl.when(i == 0)
        def _():
            loss_ref[...] = jnp.zeros_like(loss_ref)
            dfg_ref[...] = jnp.zeros_like(dfg_ref)
            dw_ref[...] = jnp.zeros_like(dw_ref)

        @pl.when(i % nsb == 0)
        def _():
            dgate_ref[...] = jnp.zeros_like(dgate_ref)

        loss_ref[...] += (0.5 / D) * part
        dfg_ref[...] += jnp.sum(dout * xh, axis=0, keepdims=True)
        dgate_ref[0] += jnp.sum(dx2 * y, axis=0, keepdims=True)
        dw_ref[...] += _dot_tn(ab, dyb)

        @pl.when(i == T // ts - 1)
        def _():
            chunks = [dw_ref[512 + 64 * PAIR_INV[h]:512 + 64 * (PAIR_INV[h] + 1), :] for h in range(SWA_HEADS)]
            for h in range(SWA_HEADS):
                dw_ref[512 + 64 * h:512 + 64 * (h + 1), :] = chunks[h]

    row = lambda w: pl.BlockSpec((ts, w), lambda i: (i, 0))
    full = lambda a: pl.BlockSpec(a.shape, lambda i: (0,) * a.ndim)
    per_b = pl.BlockSpec((1, 1, D), lambda i: (i // nsb, 0, 0))
    return pl.pallas_call(
        body, name="post", grid=(T // ts,),
        out_shape=[jax.ShapeDtypeStruct((T, D), F32), jax.ShapeDtypeStruct((T, 1024), F32),
                   jax.ShapeDtypeStruct((T, 1024), _MXU_DTYPE), jax.ShapeDtypeStruct((1, LANE), F32),
                   jax.ShapeDtypeStruct((1, D), F32), jax.ShapeDtypeStruct(gate.shape, F32),
                   jax.ShapeDtypeStruct(w_out.shape, F32)],
        in_specs=[row(512), row(512), row(1024), full(w_out), row(D), per_b, full(fg), row(D)],
        out_specs=[row(D), row(1024), row(1024),
                   pl.BlockSpec((1, LANE), lambda i: (0, 0)), pl.BlockSpec((1, D), lambda i: (0, 0)), per_b,
                   full(w_out)],
        compiler_params=_cparams(("arbitrary",)),
    )(om, osw, g, w_out, x2, gate, fg, tgt)


def _pre_bwd(dq, dk, dv, dqs, dks, dvs, dg, zq, zkv, ql, kvl, x2, dx2, scale, ng, w_in, gq, gkv, w_uq, w_uk, w_uv,
             rope, S, ts):
    T, D = x2.shape
    nsb = S // ts
    WQ = MLA_HEADS * LANE

    def body(dq_ref, dk_ref, dv_ref, dqs_ref, dks_ref, dvs_ref, dg_ref, zq_ref, zkv_ref, ql_ref, kvl_ref, x_ref, dx2_ref,
             sc_ref,
             ng_ref, w_ref, gq_ref, gkv_ref, wuq_ref, wuk_ref, wuv_ref, cd_ref, sd_ref,
             gx_ref, dz_ref, dgq_ref, dgkv_ref, dng_ref, dsh_ref, dsc_ref, dwuq_ref, dwuk_ref, dwuv_ref):
        i = pl.program_id(0)

        @pl.when(i == 0)
        def _():
            dgq_ref[...] = jnp.zeros_like(dgq_ref)
            dgkv_ref[...] = jnp.zeros_like(dgkv_ref)
            dng_ref[...] = jnp.zeros_like(dng_ref)
            dwuq_ref[...] = jnp.zeros_like(dwuq_ref)
            dwuk_ref[...] = jnp.zeros_like(dwuk_ref)
            dwuv_ref[...] = jnp.zeros_like(dwuv_ref)

        @pl.when(i % nsb == 0)
        def _():
            dsh_ref[...] = jnp.zeros_like(dsh_ref)
            dsc_ref[...] = jnp.zeros_like(dsc_ref)

        def norm_bwd(z, dy, gain):
            r = lax.rsqrt(jnp.mean(z * z, axis=-1, keepdims=True) + EPS)
            zh = z * r
            dzh = dy * gain
            return r * (dzh - zh * jnp.mean(dzh * zh, axis=-1, keepdims=True)), jnp.sum(dy * zh, axis=0, keepdims=True)

        tables = _rope_split(cd_ref[...], sd_ref[...])
        ng = ng_ref[...]
        sc1 = 1.0 + sc_ref[0]

        def rows_chain(rs):
            cq, ck, sa, sb = (t[rs] for t in tables)
            dqr = dq_ref[rs, :].astype(F32)
            dqb = _mx(dqr * _tile_heads(cq) + pltpu.roll(dqr * _tile_heads(sa), 16, 1)
                      + pltpu.roll(dqr * _tile_heads(sb), WQ - 16, 1))
            p_uq = _dot_tn(ql_ref[rs, :], dqb)
            dzq, dgq = norm_bwd(zq_ref[rs, :], _dot_nt(dqb, wuq_ref[...]), gq_ref[...])
            dkr = dk_ref[rs, :].astype(F32)
            dkb = _mx(dkr)
            p_uk = _dot_tn(kvl_ref[rs, :], dkb)
            dvb = _mx(dv_ref[rs, :])
            p_uv = _dot_tn(kvl_ref[rs, :], dvb)
            dzkv, dgkv = norm_bwd(zkv_ref[rs, :], _dot_nt(dkb, wuk_ref[...]) + _dot_nt(dvb, wuv_ref[...]), gkv_ref[...])
            dkpe = dkr[:, 0:LANE]
            for h in range(1, MLA_HEADS):
                dkpe = dkpe + dkr[:, LANE * h:LANE * (h + 1)]
            dkro = dkpe * ck + pltpu.roll(dkpe * sa, 16, 1) + pltpu.roll(dkpe * sb, LANE - 16, 1)
            dz_ref[rs, 0:384] = _mx(dzq)
            dz_ref[rs, 384:640] = _mx(dzkv)
            dz_ref[rs, 640:768] = _mx(dkro)
            dz_ref[rs, 768:1280] = dg_ref[rs, 0:512]
            dz_ref[rs, 1280:1792] = _mx(dqs_ref[rs, :])
            dz_ref[rs, 1792:1920] = _mx(dks_ref[rs, :])
            dz_ref[rs, 1920:2048] = _mx(dvs_ref[rs, :])
            dz_ref[rs, 2048:2560] = dg_ref[rs, 512:1024]
            dh = _dot(dz_ref[rs, :], w_ref[...])
            x = x_ref[rs, :]
            r1 = lax.rsqrt(jnp.mean(x * x, axis=-1, keepdims=True) + EPS)
            xn = x * r1
            dxn = dh * ng * sc1
            gx_ref[rs, :] = dx2_ref[rs, :] + r1 * (dxn - xn * jnp.mean(dxn * xn, axis=-1, keepdims=True))
            return (p_uq, p_uk, p_uv, dgq, dgkv, jnp.sum(dh, axis=0, keepdims=True),
                    jnp.sum(dh * (xn * ng), axis=0, keepdims=True), jnp.sum(dh * xn * sc1, axis=0, keepdims=True))

        hr = ts // 2
        parts = [rows_chain(slice(hr * t, hr * (t + 1))) for t in range(2)]
        p_uq, p_uk, p_uv, dgq, dgkv, dsh, dsc, dng = (a + b for a, b in zip(*parts))
        dwuq_ref[...] += p_uq
        dwuk_ref[...] += p_uk
        dwuv_ref[...] += p_uv
        dgq_ref[...] += dgq
        dgkv_ref[...] += dgkv
        dsh_ref[0] += dsh
        dsc_ref[0] += dsc
        dng_ref[...] += dng

    row = lambda w: pl.BlockSpec((ts, w), lambda i: (i, 0))
    full = lambda a: pl.BlockSpec(a.shape, lambda i: (0,) * a.ndim, pipeline_mode=pl.Buffered(1))
    per_b = pl.BlockSpec((1, 1, D), lambda i: (i // nsb, 0, 0))
    dense = pl.BlockSpec((ts // 8, LANE), lambda i: (i, 0))
    vec = lambda w: pl.BlockSpec((1, w), lambda i: (0, 0))
    return pl.pallas_call(
        body, name="pre_bwd", grid=(T // ts,),
        out_shape=[jax.ShapeDtypeStruct((T, D), F32), jax.ShapeDtypeStruct((T, D_IN_PAD), _MXU_DTYPE), jax.ShapeDtypeStruct((1, 384), F32),
                   jax.ShapeDtypeStruct((1, 256), F32), jax.ShapeDtypeStruct((1, D), F32),
                   jax.ShapeDtypeStruct(scale.shape, F32), jax.ShapeDtypeStruct(scale.shape, F32),
                   jax.ShapeDtypeStruct(w_uq.shape, F32), jax.ShapeDtypeStruct(w_uk.shape, F32),
                   jax.ShapeDtypeStruct(w_uv.shape, F32)],
        in_specs=[row(WQ), row(WQ), row(512), row(512), row(LANE), row(LANE), row(1024), row(384), row(256), row(384),
                  row(256), row(D), row(D), per_b, full(ng), full(w_in), full(gq), full(gkv), full(w_uq), full(w_uk), full(w_uv),
                  dense, dense],
        out_specs=[row(D), row(D_IN_PAD), vec(384), vec(256), vec(D), per_b, per_b, full(w_uq), full(w_uk), full(w_uv)],
        compiler_params=_cparams(("arbitrary",)),
    )(dq, dk, dv, dqs, dks, dvs, dg, zq, zkv, ql, kvl, x2, dx2, scale, ng, w_in, gq, gkv, w_uq, w_uk, w_uv, *rope)


def _w_in_row_runs():
    runs = [(0, 0, 640), (640, 704, 32), (672, 768, 512)]
    runs += [(1184 + 64 * h, 1280 + 64 * PAIR_INV[h], 64) for h in range(8)]
    runs += [(1696, 1792, 256)]
    runs += [(1952 + 64 * h, 2048 + 64 * PAIR_INV[h], 64) for h in range(8)]
    return runs


def _dw_in_t(dz, hb, tn, tk):
    T, M = dz.shape
    N = hb.shape[1]
    nk = T // tk

    def body(a_ref, b_ref, o_ref, acc_ref):
        k = pl.program_id(1)

        @pl.when(k == 0)
        def _():
            acc_ref[...] = jnp.zeros_like(acc_ref)

        acc_ref[...] += _dot_tn(a_ref[...], b_ref[...])

        @pl.when(k == nk - 1)
        def _():
            for nat, pad, size in _w_in_row_runs():
                o_ref[nat:nat + size, :] = acc_ref[pad:pad + size, :]

    return pl.pallas_call(
        body, name="dw_in", grid=(N // tn, nk),
        out_shape=jax.ShapeDtypeStruct((D_IN, N), F32),
        in_specs=[pl.BlockSpec((tk, M), lambda j, k: (k, 0)), pl.BlockSpec((tk, tn), lambda j, k: (k, j))],
        out_specs=pl.BlockSpec((D_IN, tn), lambda j, k: (0, j)),
        scratch_shapes=[pltpu.VMEM((M, tn), F32)],
        compiler_params=_cparams(("arbitrary", "arbitrary")),
    )(dz, hb)


def _finalize(parts_all, dmod_all, dmod_cols, c_act_all):
    nparts = parts_all.shape[-1]
    nmod = dmod_all.shape[-1]

    def body(p_ref, dm_ref, dmc_ref, c_ref, ps_ref, loss_ref, db_ref, dw_ref):
        acc = p_ref[0]
        for j in range(1, N_DEV):
            acc = acc + p_ref[j]
        ps_ref[...] = acc
        loss_ref[...] = jnp.sum(acc[:, 0:LANE], axis=1, keepdims=True)
        db = dm_ref[0:1, :]
        for j in range(1, dm_ref.shape[0]):
            db = db + dm_ref[j:j + 1, :]
        db_ref[...] = db
        dw_ref[...] = _dot_tn(_mx(_silu(c_ref[...])), _mx(dmc_ref[...]))

    return pl.pallas_call(
        body, name="finalize",
        out_shape=[jax.ShapeDtypeStruct((1, nparts), F32), jax.ShapeDtypeStruct((1, 1), F32),
                   jax.ShapeDtypeStruct((1, nmod), F32),
                   jax.ShapeDtypeStruct((c_act_all.shape[1], dmod_cols.shape[1]), F32)],
        in_specs=[_vmem()] * 4, out_specs=[_vmem()] * 4,
        compiler_params=_cparams(),
    )(parts_all, dmod_all, dmod_cols, c_act_all)


def _adamw(ws, gs, ms, vs):
    n = len(ws)

    def body(*refs):
        w_refs, g_refs, m_refs, v_refs, d_refs, nm_refs, nv_refs = (refs[t * n:(t + 1) * n] for t in range(7))
        for t in range(n):
            gv = g_refs[t][...]
            nm = ADAM_B1 * m_refs[t][...] + (1.0 - ADAM_B1) * gv
            nv = ADAM_B2 * v_refs[t][...] + (1.0 - ADAM_B2) * (gv * gv)
            m_hat = nm / (1.0 - ADAM_B1 ** ADAM_STEP)
            v_hat = nv / (1.0 - ADAM_B2 ** ADAM_STEP)
            d_refs[t][...] = -ADAM_LR * (m_hat / (jnp.sqrt(v_hat) + ADAM_EPS) + ADAM_WD * w_refs[t][...])
            nm_refs[t][...] = nm
            nv_refs[t][...] = nv

    out = pl.pallas_call(
        body, name="adamw",
        out_shape=[jax.ShapeDtypeStruct(w.shape, F32) for w in ws] * 3,
        in_specs=[_vmem()] * (4 * n), out_specs=[_vmem()] * (3 * n),
        compiler_params=_cparams(),
    )(*ws, *gs, *ms, *vs)
    return out[:n], out[n:2 * n], out[2 * n:]


def _pair_perm(a, axis, order):
    a = jnp.moveaxis(a, axis, -1)
    lead = a.shape[:-1]
    a = a.reshape(lead + (8, 64))[..., list(order), :].reshape(lead + (512,))
    return jnp.moveaxis(a, -1, axis)


def _pad_w_in_t(wt):
    z = lambda n: jnp.zeros((n, wt.shape[1]), wt.dtype)
    return jnp.concatenate([
        wt[0:640], z(64), wt[640:672], z(32), wt[672:1184],
        _pair_perm(wt[1184:1696], 0, PAIR_ORDER), wt[1696:1952], _pair_perm(wt[1952:2464], 0, PAIR_ORDER)], axis=0)


def _rope_table(positions):
    T = positions.size
    inv = ROPE_THETA ** (-jnp.arange(0, MLA_ROPE, 2, dtype=F32) / MLA_ROPE)
    pos = jnp.repeat(positions.reshape(T // 8, 8)[:, ::-1].astype(F32), MLA_ROPE // 2, axis=1)
    ang = pos * jnp.tile(inv, 8)[None, :]
    return jnp.cos(ang), jnp.sin(ang)


def _local_step(x, mod, positions, ng, w_in_t, gq, gkv, w_uq, w_ukv, sinks, w_out, fg, tgt,
                ts=512, fq=512, fk=512, bq=512, bk=512):
    B, S, D = x.shape
    T = B * S
    x2 = x.reshape(T, D)
    shift, scale, gate = (mod[:, None, k * D:(k + 1) * D] for k in range(3))
    w_in_p = _pad_w_in_t(w_in_t)
    w_uq_p = jnp.pad(w_uq.reshape(Q_LORA, MLA_HEADS, 96), ((0, 0), (0, 0), (0, 32))).reshape(Q_LORA, MLA_HEADS * LANE)
    w_ukv3 = w_ukv.reshape(KV_LORA, MLA_HEADS, 128)
    w_uk_p = jnp.pad(w_ukv3[:, :, :64], ((0, 0), (0, 0), (0, 64))).reshape(KV_LORA, MLA_HEADS * LANE)
    w_uv = w_ukv3[:, :, 64:].reshape(KV_LORA, 512)
    w_out_p = jnp.concatenate([w_out[:512], _pair_perm(w_out[512:], 0, PAIR_ORDER)], axis=0)
    rope = _rope_table(positions)
    posf = positions.astype(F32)
    posb = jnp.broadcast_to(posf[:, :, None], (B, S, LANE))
    posr = posf.reshape(B, S // WINDOW, WINDOW)
    sinks_l = jnp.pad(sinks.reshape(1, SWA_HEADS), ((0, 0), (0, LANE - SWA_HEADS)))

    (hb, zq, zkv, ql, kvl, q, k, v, qs, ks, vs, g) = _pre_fwd(
        x2, shift, scale, ng, w_in_p, gq, gkv, w_uq_p, w_uk_p, w_uv, rope, S, ts)
    r3 = lambda a: a.reshape(B, S, a.shape[-1])
    om, lse_m = _mla_fwd(r3(q), r3(k), r3(v), fq, fk)
    osw, lse_s = _swa_fwd(r3(qs), r3(ks), r3(vs), posb, posr, sinks_l)
    dx2, do, dg, loss_v, dfg, dgate, dw_out = _post(
        om.reshape(T, 512), osw.reshape(T, 512), g, w_out_p, x2, gate, fg.reshape(1, D), tgt.reshape(T, D), S, ts)
    do3 = r3(do)
    dq, dk, dv = _mla_bwd(r3(q), r3(k), r3(v), om, do3, lse_m, bq, bk)
    dqs, dks, dvs, dsink = _swa_bwd(r3(qs), r3(ks), r3(vs), posb, posr, sinks_l, osw, do3, lse_s)
    f2 = lambda a: a.reshape(T, a.shape[-1])
    gx, dz, dgq, dgkv, dng, dsh, dsc, dw_uq_p, dw_uk_p, dw_uv = _pre_bwd(
        f2(dq), f2(dk), f2(dv), f2(dqs), f2(dks), f2(dvs), dg, zq, zkv, ql, kvl, x2, dx2, scale, ng, w_in_p, gq, gkv,
        w_uq_p, w_uk_p, w_uv, rope, S, ts)
    tk = min(T, 1024)
    dw_in_t = _dw_in_t(dz, hb, 512, tk)
    dw_uq = dw_uq_p.reshape(Q_LORA, MLA_HEADS, LANE)[:, :, :96].reshape(Q_LORA, 768)
    dw_uk = dw_uk_p.reshape(KV_LORA, MLA_HEADS, LANE)[:, :, :64]
    dw_uv = dw_uv.reshape(KV_LORA, MLA_HEADS, 64)
    dw_ukv = jnp.concatenate([dw_uk, dw_uv], axis=2).reshape(KV_LORA, 1024)
    parts = jnp.concatenate([loss_v, dfg, dng, dgq, dgkv, dsink], axis=1)
    dmod = jnp.concatenate([dsh, dsc, dgate], axis=2).reshape(B, 3 * D)
    return gx.reshape(B, S, D), dw_in_t, dw_uq, dw_ukv, dw_out, parts, dmod


def kernel(x, c, positions, w_ada, b_ada, norm_gain, w_in, q_norm_gain, kv_norm_gain, w_uq, w_ukv, swa_sinks, w_out, final_gain, loss_target, m_w_ada, m_b_ada, m_norm_gain, m_w_in, m_q_norm_gain, m_kv_norm_gain, m_w_uq, m_w_ukv, m_swa_sinks, m_w_out, m_final_gain, v_w_ada, v_b_ada, v_norm_gain, v_w_in, v_q_norm_gain, v_kv_norm_gain, v_w_uq, v_w_ukv, v_swa_sinks, v_w_out, v_final_gain):
    B, S, D = x.shape
    me = 4 * lax.axis_index("x") + 2 * lax.axis_index("y") + lax.axis_index("c")
    bf = _MXU_DTYPE

    ncol = w_ada.shape[2]
    b_cols = lax.dynamic_slice_in_dim(b_ada, me * ncol, ncol, axis=1)
    c_all, win_g, wuq_g, wukv_g, wout_g, mod_g = _all_gather(
        [c, w_in[0].T.astype(bf), w_uq[0].astype(bf), w_ukv[0].astype(bf), w_out[0].astype(bf)], "ag_weights",
        fused=((w_ada[0], b_cols), _ada_cols, jax.ShapeDtypeStruct((N_DEV, B, ncol), F32)))
    c_all = c_all.reshape(N_DEV * B, D)
    cat_cols = lambda a: jnp.transpose(a, (1, 0, 2)).reshape(a.shape[1], N_DEV * a.shape[2])
    w_in_t, w_uq_f, w_ukv_f = win_g.reshape(D_IN, D), cat_cols(wuq_g), cat_cols(wukv_g)
    w_out_f = wout_g.reshape(D, D)

    mod = lax.dynamic_index_in_dim(mod_g, me, axis=1, keepdims=False)
    mod = jnp.transpose(mod, (1, 0, 2)).reshape(B, 3 * D)

    gx, dw_in_t, dw_uq, dw_ukv, dw_out, parts, dmod = _local_step(
        x, mod, positions, norm_gain, w_in_t, q_norm_gain, kv_norm_gain, w_uq_f, w_ukv_f, swa_sinks,
        w_out_f, final_gain, loss_target)

    parts_g, dmod_g = _all_gather([parts, dmod], "ag_small")
    dmod_all = dmod_g.reshape(N_DEV * B, 3 * D)
    dmod_cols = lax.dynamic_slice_in_dim(dmod_all, me * ncol, ncol, axis=1)
    psum, loss, g_b_ada, g_w_ada = _finalize(parts_g, dmod_all, dmod_cols, c_all)
    loss = loss.reshape(())
    o = LANE
    g_final_gain = psum[0, o:o + D]
    g_norm_gain = psum[:, o + D:o + 2 * D]
    o += 2 * D
    g_q_norm_gain = psum[:, o:o + Q_LORA]
    g_kv_norm_gain = psum[:, o + Q_LORA:o + Q_LORA + KV_LORA]
    o += Q_LORA + KV_LORA
    g_sinks = psum[:, o:o + SWA_HEADS]

    split_cols = lambda a: jnp.transpose(a.reshape(a.shape[0], 4, 2, a.shape[1] // N_DEV), (1, 2, 0, 3))
    split_rows = lambda a: a.reshape(4, 2, a.shape[0] // N_DEV, a.shape[1])
    g_w_in_t, g_w_uq, g_w_ukv, g_w_out = _reduce_scatter(
        [split_rows(dw_in_t), split_cols(dw_uq), split_cols(dw_ukv), split_rows(dw_out)], "rs_grads")
    g_w_in = g_w_in_t.T

    grads = [g_w_ada, g_b_ada, g_norm_gain, g_w_in, g_q_norm_gain, g_kv_norm_gain, g_w_uq, g_w_ukv, g_sinks, g_w_out,
             g_final_gain]
    ws = [w_ada, b_ada, norm_gain, w_in, q_norm_gain, kv_norm_gain, w_uq, w_ukv, swa_sinks, w_out, final_gain]
    ms = [m_w_ada, m_b_ada, m_norm_gain, m_w_in, m_q_norm_gain, m_kv_norm_gain, m_w_uq, m_w_ukv, m_swa_sinks, m_w_out,
          m_final_gain]
    vs = [v_w_ada, v_b_ada, v_norm_gain, v_w_in, v_q_norm_gain, v_kv_norm_gain, v_w_uq, v_w_ukv, v_swa_sinks, v_w_out,
          v_final_gain]
    two_d = [(1, w.shape[0]) if w.ndim == 1 else w.shape[-2:] for w in ws]
    flat = lambda arrs: [a.reshape(s) for a, s in zip(arrs, two_d)]
    deltas, new_ms, new_vs = _adamw(flat(ws), flat(grads), flat(ms), flat(vs))
    shaped = lambda arrs: [a.reshape(w.shape) for a, w in zip(arrs, ws)]
    return (loss, gx, *shaped(grads), *shaped(deltas), *shaped(new_ms), *shaped(new_vs))
```

```python
import functools

import numpy as np
import jax
import jax.numpy as jnp
from jax import lax
from jax.experimental import pallas as pl
from jax.experimental.pallas import tpu as pltpu

F32 = jnp.float32
_MXU_DTYPE = jnp.bfloat16
MLA_GRAD_DTYPE = jnp.bfloat16

N_DEV = 8
D_MODEL = 1024
MLA_HEADS = 8
MLA_NOPE = 64
MLA_ROPE = 32
MLA_V = 64
Q_LORA = 384
KV_LORA = 256
SWA_HEADS = 8
SWA_KV_HEADS = 2
SWA_HEAD_DIM = 64
WINDOW = 128
ROPE_THETA = 10000.0
EPS = 1e-6
MLA_SCALE = float((MLA_NOPE + MLA_ROPE) ** -0.5)
SWA_SCALE = float(SWA_HEAD_DIM ** -0.5)
LOG2E = 1.4426950408889634
MLA_QSCALE = MLA_SCALE * LOG2E
D_IN = 2464
D_IN_PAD = 2560
PAIR_ORDER = (0, 4, 1, 5, 2, 6, 3, 7)
PAIR_INV = (0, 2, 4, 6, 1, 3, 5, 7)

ADAM_LR = 0.001
ADAM_B1 = 0.9
ADAM_B2 = 0.999
ADAM_EPS = 1e-08
ADAM_WD = 0.01
ADAM_STEP = 10

LANE = 128
VMEM_LIMIT = 56 * 1024 * 1024

MESH = pl.DeviceIdType.MESH
NEG_INF = float("-inf")
SWA_SEQ_SPLIT = 2


def _mx(a):
    return a.astype(_MXU_DTYPE)


def _dot(a, b):
    return jnp.dot(a, b, preferred_element_type=F32)


def _dot_nt(a, b):
    return lax.dot_general(a, b, (((1,), (1,)), ((), ())), preferred_element_type=F32)


def _dot_tn(a, b):
    return lax.dot_general(a, b, (((0,), (0,)), ((), ())), preferred_element_type=F32)


def _cparams(sem=None):
    return pltpu.CompilerParams(dimension_semantics=sem, vmem_limit_bytes=VMEM_LIMIT)


def _vmem():
    return pl.BlockSpec(memory_space=pltpu.VMEM)


def _lane_iota(shape):
    return lax.broadcasted_iota(jnp.int32, shape, len(shape) - 1)


def _all_gather(arrs, name, fused=None):
    n = len(arrs)
    extra, fn, piece = fused if fused else ((), None, None)
    ne, m = len(extra), n + (1 if fused else 0)

    def body(*refs):
        ins, ex, outs = refs[:n], refs[n:n + ne], refs[n + ne:n + ne + m]
        rest = refs[n + ne + m:]
        srcs = list(ins) + ([rest[0]] if fused else [])
        send_sems, recv_sems, local_sems = rest[-3:]
        x, y, c = lax.axis_index("x"), lax.axis_index("y"), lax.axis_index("c")
        me, sibling = (x, y, c), (x, y, 1 - c)
        chips = [(1 - x, y), (x, 1 - y), (1 - x, 1 - y)]

        def slot(a, dev):
            return outs[a].at[4 * dev[0] + 2 * dev[1] + dev[2]]

        def copy(a, k, block, to, src=None):
            return pltpu.make_async_remote_copy(
                src_ref=slot(a, block) if src is None else src, dst_ref=slot(a, block),
                send_sem=send_sems.at[7 * a + k], recv_sem=recv_sems.at[7 * a + k],
                device_id=to, device_id_type=MESH)

        def local(a):
            return pltpu.make_async_copy(srcs[a], slot(a, me), local_sems.at[a])

        def start(a):
            local(a).start()
            cps = [copy(a, 0, me, sibling, src=srcs[a])]
            cps += [copy(a, 1 + j, me, (*chip, c), src=srcs[a]) for j, chip in enumerate(chips)]
            for cp in cps:
                cp.start()
            return cps

        def finish(group):
            cps = []
            for j, chip in enumerate(chips):
                for a in group:
                    copy(a, 1 + j, (*chip, c), me).wait_recv()
                    cp = copy(a, 4 + j, (*chip, c), sibling)
                    cp.start()
                    cps.append(cp)
            for a in group:
                copy(a, 0, sibling, me).wait_recv()
                for j, chip in enumerate(chips):
                    copy(a, 4 + j, (*chip, 1 - c), me).wait_recv()
                local(a).wait()
            return cps

        pending = []
        for a in range(n):
            pending += start(a)
        if fused:
            pending += finish([0])
            fn(srcs[n], outs[0], *ex)
            pending += start(n)
            pending += finish([n] + list(range(1, n)))
        else:
            pending += finish(list(range(n)))
        for cp in pending:
            cp.wait_send()

    out_shape = [jax.ShapeDtypeStruct((N_DEV,) + a.shape, a.dtype) for a in arrs]
    scratch = []
    if fused:
        out_shape.append(jax.ShapeDtypeStruct((N_DEV,) + piece.shape, piece.dtype))
        scratch.append(pltpu.VMEM(piece.shape, piece.dtype))
    return pl.pallas_call(
        body, name=name, out_shape=out_shape,
        in_specs=[_vmem()] * (n + ne), out_specs=[_vmem()] * m,
        scratch_shapes=scratch + [pltpu.SemaphoreType.DMA((7 * m,)), pltpu.SemaphoreType.DMA((7 * m,)),
                                  pltpu.SemaphoreType.DMA((m,))],
        compiler_params=pltpu.CompilerParams(vmem_limit_bytes=VMEM_LIMIT),
    )(*arrs, *extra)


def _reduce_scatter(arrs, name):
    n = len(arrs)
    halves = [a.astype(jnp.bfloat16) for a in arrs]

    def body(*refs):
        xs, xbs, outs = refs[:n], refs[n:2 * n], refs[2 * n:3 * n]
        parts, recv_a, send_b, recv_b = (refs[(3 + t) * n:(4 + t) * n] for t in range(4))
        send_sems, recv_sems, local_sems = refs[7 * n:]
        x, y, c = lax.axis_index("x"), lax.axis_index("y"), lax.axis_index("c")
        myq = 2 * x + y

        def chip(k):
            return (1 - x if k & 2 else x, 1 - y if k & 1 else y)

        def from_sibling(a):
            return pltpu.make_async_remote_copy(
                src_ref=xbs[a].at[:, 1 - c], dst_ref=recv_a[a], send_sem=send_sems.at[4 * a], recv_sem=recv_sems.at[4 * a],
                device_id=(x, y, 1 - c), device_id_type=MESH)

        def to_owner(a, k):
            qx, qy = chip(k)
            return pltpu.make_async_remote_copy(
                src_ref=send_b[a].at[2 * qx + qy], dst_ref=recv_b[a].at[myq],
                send_sem=send_sems.at[4 * a + k], recv_sem=recv_sems.at[4 * a + k],
                device_id=(qx, qy, c), device_id_type=MESH)

        mine = [pltpu.make_async_copy(xs[a].at[:, c], parts[a], local_sems.at[a]) for a in range(n)]
        first = [from_sibling(a) for a in range(n)]
        for cp in mine + first:
            cp.start()
        second = []
        for a in range(n):
            mine[a].wait()
            first[a].wait_recv()
            parts[a][...] = parts[a][...] + recv_a[a][...].astype(F32)
            send_b[a][...] = parts[a][...].astype(jnp.bfloat16)
            for k in range(1, 4):
                cp = to_owner(a, k)
                cp.start()
                second.append(cp)
        for a in range(n):
            acc = parts[a][myq]
            for k in range(1, 4):
                to_owner(a, k).wait_recv()
                qx, qy = chip(k)
                acc = acc + recv_b[a][2 * qx + qy].astype(F32)
            outs[a][...] = acc
        for cp in first + second:
            cp.wait_send()

    quarter = lambda a, dt: pltpu.VMEM((4,) + a.shape[2:], dt)
    return pl.pallas_call(
        body, name=name,
        out_shape=[jax.ShapeDtypeStruct(a.shape[2:], F32) for a in arrs],
        in_specs=[pl.BlockSpec(memory_space=pl.ANY)] * (2 * n), out_specs=[_vmem()] * n,
        scratch_shapes=[quarter(a, F32) for a in arrs] + [quarter(a, jnp.bfloat16) for a in arrs] * 3
        + [pltpu.SemaphoreType.DMA((4 * n,)), pltpu.SemaphoreType.DMA((4 * n,)), pltpu.SemaphoreType.DMA((n,))],
        compiler_params=pltpu.CompilerParams(vmem_limit_bytes=VMEM_LIMIT),
    )(*arrs, *halves)


def _silu(t):
    return t * (1.0 / (1.0 + jnp.exp(-t)))


def _ada_cols(piece_ref, c_all_ref, w_ref, b_ref):
    w = _mx(w_ref[...])
    for d in range(N_DEV):
        piece_ref[d] = _dot(_mx(_silu(c_all_ref[d])), w) + b_ref[...]


def _rope_split(cd, sd):
    n = cd.shape[0]

    def expand(d):
        rep = jnp.broadcast_to(d[:, None, :], (n, 8, LANE)).reshape(8 * n, LANE)
        return pltpu.roll(rep, 0, 1, stride=16, stride_axis=0)

    c, s = expand(cd), expand(sd)
    lane = _lane_iota(c.shape)
    first = jnp.logical_and(lane >= 64, lane < 80)
    second = jnp.logical_and(lane >= 80, lane < 96)
    ck = jnp.where(first, pltpu.roll(c, 80, 1), jnp.where(second, pltpu.roll(c, 96, 1), 0.0))
    cq = jnp.where(lane < 64, 1.0, ck)
    sa = jnp.where(first, -pltpu.roll(s, 80, 1), 0.0)
    sb = jnp.where(second, pltpu.roll(s, 96, 1), 0.0)
    return cq, ck, sa, sb


def _tile_heads(t):
    return jnp.concatenate([t] * MLA_HEADS, axis=1)


def _pre_fwd(x2, shift, scale, ng, w_in, gq, gkv, w_uq, w_uk, w_uv, rope, S, ts):
    T, D = x2.shape
    nsb = S // ts
    WQ = MLA_HEADS * LANE

    def body(x_ref, sh_ref, sc_ref, ng_ref, wn_ref, gq_ref, gkv_ref, wuq_ref, wuk_ref, wuv_ref,
             cd_ref, sd_ref,
             hb_ref, zq_ref, zkv_ref, ql_ref, kvl_ref, q_ref, k_ref, v_ref, qs_ref, ks_ref, vs_ref, g_ref, wp_ref):
        @pl.when(pl.program_id(0) == 0)
        def _():
            wp_ref[640:704, :] = jnp.zeros((64, D), wp_ref.dtype)
            wp_ref[736:768, :] = jnp.zeros((32, D), wp_ref.dtype)
            for nat, pad, size in _w_in_row_runs():
                wp_ref[pad:pad + size, :] = wn_ref[nat:nat + size, :]

        w_ref = wp_ref
        x = x_ref[...]
        r1 = lax.rsqrt(jnp.mean(x * x, axis=-1, keepdims=True) + EPS)
        h = ((x * r1) * ng_ref[...]) * (1.0 + sc_ref[0]) + sh_ref[0]
        hb = _mx(h)
        hb_ref[...] = hb
        zq = _dot_nt(hb, w_ref[0:384, :])
        zq_ref[...] = zq
        rq = lax.rsqrt(jnp.mean(zq * zq, axis=-1, keepdims=True) + EPS)
        ql = _mx((zq * rq) * gq_ref[...])
        ql_ref[...] = ql
        q = _dot(ql, wuq_ref[...])
        cq, ck, sa, sb = _rope_split(cd_ref[...], sd_ref[...])
        q = (q * _tile_heads(cq) + pltpu.roll(q, WQ - 16, 1) * _tile_heads(sa)
             + pltpu.roll(q, 16, 1) * _tile_heads(sb))
        q_ref[...] = _mx(q * MLA_QSCALE)
        zkv = _dot_nt(hb, w_ref[384:640, :])
        zkv_ref[...] = zkv
        rkv = lax.rsqrt(jnp.mean(zkv * zkv, axis=-1, keepdims=True) + EPS)
        kvl = _mx((zkv * rkv) * gkv_ref[...])
        kvl_ref[...] = kvl
        kr = _dot_nt(hb, w_ref[640:768, :])
        kpe = kr * ck + pltpu.roll(kr, LANE - 16, 1) * sa + pltpu.roll(kr, 16, 1) * sb
        kf = _dot(kvl, wuk_ref[...])
        k_ref[...] = _mx(kf + jnp.concatenate([kpe] * MLA_HEADS, axis=1))
        v_ref[...] = _mx(_dot(kvl, wuv_ref[...]))
        g_ref[:, 0:512] = _dot_nt(hb, w_ref[768:1280, :])
        qs_ref[...] = _mx(_dot_nt(hb, w_ref[1280:1792, :]) * (SWA_SCALE * LOG2E))
        ks_ref[...] = _mx(_dot_nt(hb, w_ref[1792:1920, :]))
        vs_ref[...] = _mx(_dot_nt(hb, w_ref[1920:2048, :]))
        g_ref[:, 512:1024] = _dot_nt(hb, w_ref[2048:2560, :])

    row = lambda w: pl.BlockSpec((ts, w), lambda i: (i, 0))
    dense = pl.BlockSpec((ts // 8, LANE), lambda i: (i, 0))
    full = lambda a: pl.BlockSpec(a.shape, lambda i: (0,) * a.ndim)
    per_b = pl.BlockSpec((1, 1, D), lambda i: (i // nsb, 0, 0))
    out_w = [(D, _MXU_DTYPE), (384, F32), (256, F32), (384, _MXU_DTYPE), (256, _MXU_DTYPE), (WQ, _MXU_DTYPE),
             (WQ, _MXU_DTYPE), (512, _MXU_DTYPE), (512, _MXU_DTYPE), (128, _MXU_DTYPE), (128, _MXU_DTYPE), (1024, F32)]
    return pl.pallas_call(
        body, name="pre_fwd", grid=(T // ts,),
        out_shape=[jax.ShapeDtypeStruct((T, w), dt) for w, dt in out_w] + [jax.ShapeDtypeStruct((D_IN_PAD, D), w_in.dtype)],
        in_specs=[row(D), per_b, per_b, full(ng), full(w_in), full(gq), full(gkv), full(w_uq), full(w_uk), full(w_uv),
                  dense, dense],
        out_specs=[row(w) for w, _ in out_w] + [pl.BlockSpec((D_IN_PAD, D), lambda i: (0, 0))],
        compiler_params=_cparams(("arbitrary",)),
    )(x2, shift, scale, ng, w_in, gq, gkv, w_uq, w_uk, w_uv, *rope)


def _mla_fwd(q3, k3, v3, tq, tk):
    B, S, _ = q3.shape
    nq = S // tq
    assert tk % tq == 0 or tq % tk == 0
    n_masked = max(1, tq // tk)

    HPS = 2
    NH = 2 * HPS

    def body(q_ref, k_ref, v_ref, o_ref, lse_ref):
        qi = pl.program_id(2)
        rows = lax.broadcasted_iota(jnp.int32, (tq, tk), 0)
        cols = lax.broadcasted_iota(jnp.int32, (tq, tk), 1)
        qs = [q_ref[0, :, LANE * j:LANE * (j + 1)] for j in range(NH)]
        n_full = (qi * tq) // tk

        low_k = _lane_iota((tk, LANE)) < 64

        def step(kt, carry, masked):
            r0 = pl.multiple_of(kt * tk, tk)
            out = []
            for j in range(NH):
                v2 = v_ref[0, pl.ds(r0, tk), LANE * (j // 2):LANE * (j // 2 + 1)]
                vj = (jnp.where(low_k, v2, 1.0) if j % 2 == 0 else jnp.where(low_k, 1.0, v2)).astype(v2.dtype)
                m, acc = carry[j]
                s = _dot_nt(qs[j], k_ref[0, pl.ds(r0, tk), LANE * j:LANE * (j + 1)])
                if masked:
                    s = jnp.where(rows + qi * tq >= cols + kt * tk, s, NEG_INF)
                m_new = jnp.maximum(m, jnp.max(s, axis=1, keepdims=True))
                alpha = jnp.exp2(m - m_new)
                p = jnp.exp2(s - m_new)
                acc = alpha * acc + _dot(_mx(p), vj)
                out.append((m_new, acc))
            return tuple(out)

        init = (jnp.full((tq, 1), NEG_INF, F32), jnp.zeros((tq, LANE), F32))
        carry = lax.fori_loop(0, n_full, functools.partial(step, masked=False), (init,) * NH)
        for t in range(n_masked):
            carry = step(n_full + t, carry, True)
        low = _lane_iota((tq, LANE)) < 64
        for t in range(HPS):
            (m0, a0), (m1, a1) = carry[2 * t], carry[2 * t + 1]
            l0 = jnp.where(low, pltpu.roll(a0, 64, 1), a0)
            l1 = jnp.where(low, a1, pltpu.roll(a1, 64, 1))
            o_ref[0, :, LANE * t:LANE * (t + 1)] = jnp.where(low, a0 / l0, a1 / l1)
            lse_ref[0, 2 * t] = m0 + jnp.log2(l0)
            lse_ref[0, 2 * t + 1] = m1 + jnp.log2(l1)

    return pl.pallas_call(
        body, name="mla_fwd", grid=(B, MLA_HEADS // NH, nq),
        out_shape=[jax.ShapeDtypeStruct((B, S, 512), F32), jax.ShapeDtypeStruct((B, MLA_HEADS, S, LANE), F32)],
        in_specs=[pl.BlockSpec((1, tq, NH * LANE), lambda b, hp, i: (b, i, hp)),
                  pl.BlockSpec((1, S, NH * LANE), lambda b, hp, i: (b, 0, hp)),
                  pl.BlockSpec((1, S, HPS * LANE), lambda b, hp, i: (b, 0, hp))],
        out_specs=[pl.BlockSpec((1, tq, HPS * LANE), lambda b, hp, i: (b, i, hp)),
                   pl.BlockSpec((1, NH, tq, LANE), lambda b, hp, i: (b, hp, i, 0))],
        compiler_params=_cparams(("arbitrary", "arbitrary", "arbitrary")),
    )(q3, k3, v3)


def _mla_bwd(q3, k3, v3, o3, do3, lse, tq, tk, hps=2):
    B, S, _ = q3.shape
    nq, nk = S // tq, S // tk
    nh = 2 * hps
    assert tk % tq == 0

    def body(q_ref, k_ref, v_ref, o_ref, do_ref, lse_ref, dq_ref, dk_ref, dv_ref, dkt_ref, dvt_ref):
        dkt_ref[...] = jnp.zeros_like(dkt_ref)
        dvt_ref[...] = jnp.zeros_like(dvt_ref)
        rows = lax.broadcasted_iota(jnp.int32, (tq, tk), 0)
        cols = lax.broadcasted_iota(jnp.int32, (tq, tk), 1)
        lane = _lane_iota((tq, LANE))

        def q_tile(qi, _):
            r = pl.ds(pl.multiple_of(qi * tq, tq), tq)
            heads = []
            for j in range(nh):
                lanes = slice(LANE * j, LANE * (j + 1))
                pair = slice(LANE * (j // 2), LANE * (j // 2 + 1))
                do = jnp.where((lane < 64) if j % 2 == 0 else (lane >= 64), do_ref[0, r, pair], 0.0)
                q = q_ref[0, r, lanes]
                heads.append((lanes, pair, q, _mx(q.astype(F32).T), _mx(do), _mx(do.T),
                              jnp.sum(do * o_ref[0, r, pair], axis=1, keepdims=True),
                              jnp.concatenate([lse_ref[0, j, r, :]] * (tk // LANE), axis=1)))
            n_full = (qi * tq) // tk

            def k_tile(kt, dqs, masked):
                kr = pl.ds(pl.multiple_of(kt * tk, tk), tk)
                out = []
                dvt = [None] * hps
                for j, (lanes, pair, q, qt, dob, dot_, dcol, lse_c) in enumerate(heads):
                    k = k_ref[0, kr, lanes]
                    s = _dot_nt(q, k)
                    if masked:
                        s = jnp.where(rows + qi * tq >= cols + kt * tk, s, NEG_INF)
                    p = jnp.exp2(s - lse_c)
                    dp = _dot_nt(dob, v_ref[0, kr, pair])
                    dsb = _mx(p * (dp - dcol))
                    dkt_ref[j, kt] += _dot(qt, dsb)
                    pv = _dot(dot_, _mx(p))
                    dvt[j // 2] = pv if dvt[j // 2] is None else dvt[j // 2] + pv
                    out.append(dqs[j] + _dot(dsb, k))
                for t in range(hps):
                    dvt_ref[t, kt] += dvt[t]
                return tuple(out)

            zero = jnp.zeros((tq, LANE), F32)
            dqs = lax.fori_loop(0, n_full, functools.partial(k_tile, masked=False), (zero,) * nh)
            dqs = k_tile(n_full, dqs, True)
            for j in range(nh):
                dq_ref[0, r, heads[j][0]] = (MLA_SCALE * dqs[j]).astype(dq_ref.dtype)
            return 0

        lax.fori_loop(0, nq, q_tile, 0)

        def flush(kt, _):
            kr = pl.ds(pl.multiple_of(kt * tk, tk), tk)
            for j in range(nh):
                dk_ref[0, kr, LANE * j:LANE * (j + 1)] = ((1.0 / LOG2E) * dkt_ref[j, kt].T).astype(dk_ref.dtype)
            for t in range(hps):
                dv_ref[0, kr, LANE * t:LANE * (t + 1)] = dvt_ref[t, kt].T.astype(dv_ref.dtype)
            return 0

        lax.fori_loop(0, nk, flush, 0)

    grp = lambda w: pl.BlockSpec((1, S, w), lambda b, hp: (b, 0, hp))
    return pl.pallas_call(
        body, name="mla_bwd", grid=(B, MLA_HEADS // nh),
        out_shape=[jax.ShapeDtypeStruct((B, S, 1024), MLA_GRAD_DTYPE), jax.ShapeDtypeStruct((B, S, 1024), MLA_GRAD_DTYPE),
                   jax.ShapeDtypeStruct((B, S, 512), MLA_GRAD_DTYPE)],
        in_specs=[grp(nh * LANE), grp(nh * LANE), grp(hps * LANE), grp(hps * LANE), grp(hps * LANE),
                  pl.BlockSpec((1, nh, S, LANE), lambda b, hp: (b, hp, 0, 0))],
        out_specs=[grp(nh * LANE), grp(nh * LANE), grp(hps * LANE)],
        scratch_shapes=[pltpu.VMEM((nh, nk, LANE, tk), F32), pltpu.VMEM((hps, nk, LANE, tk), F32)],
        compiler_params=_cparams(("arbitrary", "arbitrary")),
    )(q3, k3, v3, o3, do3, lse)


def _swa_consts(sink_ref):
    W = WINDOW
    row = lax.broadcasted_iota(jnp.int32, (4 * W, LANE), 0)
    out = []
    for g in range(SWA_KV_HEADS):
        slope = jnp.zeros((4 * W, LANE), F32)
        sink = jnp.zeros((4 * W, LANE), F32)
        for p in range(4):
            h = p + 4 * g
            here = jnp.logical_and(row >= W * p, row < W * (p + 1))
            slope = jnp.where(here, float(LOG2E * 2.0 ** (-8.0 * (h + 1) / SWA_HEADS)), slope)
            sink = jnp.where(here, LOG2E * sink_ref[:, h:h + 1], sink)
        out.append((jnp.concatenate([slope, slope], axis=1), sink))
    return out


def _wide(col):
    return jnp.concatenate([col, col], axis=1)


def _swa_block(q_ref, k_ref, v_ref, pb_ref, pr_ref, n, i, ij):
    W = WINDOW
    kb = jnp.maximum(n - 1, 0)
    r = pl.ds(pl.multiple_of(i * W, W), W)
    kr = pl.ds(pl.multiple_of(kb * W, W), 2 * W)
    q4, k2, v2 = q_ref[0, r, :], k_ref[0, kr, :], v_ref[0, kr, :]
    pq = _wide(pb_ref[0, r, :])
    pk = jnp.concatenate([pr_ref[0, pl.ds(kb, 1), :], pr_ref[0, pl.ds(kb + 1, 1), :]], axis=1)
    rel = ij + (n - kb) * W
    mask = jnp.where(jnp.logical_and(rel >= 0, rel < W), 0.0, NEG_INF)
    dist4 = jnp.concatenate([pq - pk] * 4, axis=0)
    mask4 = jnp.concatenate([mask] * 4, axis=0)
    return r, kb, kr, q4, k2, v2, dist4, mask4


def _swa_stack(x4, g, dtype):
    lane = _lane_iota((WINDOW, LANE))
    mine = (lane < 64) if g == 0 else (lane >= 64)
    return jnp.concatenate([jnp.where(mine, x4[:, LANE * p:LANE * (p + 1)], 0).astype(dtype) for p in range(4)], axis=0)


def _swa_unstack(ref, r, lo, hi, scale=None):
    W = WINDOW
    low = _lane_iota((W, LANE)) < 64
    for p in range(4):
        t = jnp.where(low, lo[W * p:W * (p + 1)], hi[W * p:W * (p + 1)])
        ref[0, r, LANE * p:LANE * (p + 1)] = t if scale is None else scale * t


def _swa_fwd(qs3, ks3, vs3, posb, posr, sinks):
    B, S, _ = qs3.shape
    W = WINDOW
    nb = S // W
    nh = SWA_SEQ_SPLIT
    nbh = nb // nh

    def body(q_ref, k_ref, v_ref, pb_ref, pr_ref, sink_ref, o_ref, lse_ref):
        ij = lax.broadcasted_iota(jnp.int32, (W, 2 * W), 0) - lax.broadcasted_iota(jnp.int32, (W, 2 * W), 1)
        consts = _swa_consts(sink_ref)
        n0 = pl.program_id(1) * nbh

        def blk(i, _):
            n = n0 + i
            r, _, _, q4, k2, v2, dist4, mask4 = _swa_block(q_ref, k_ref, v_ref, pb_ref, pr_ref, n, i, ij)
            o_g = []
            for g, (slope, sink) in enumerate(consts):
                s = _dot_nt(_swa_stack(q4, g, q4.dtype), k2) - slope * dist4 + mask4
                m = jnp.maximum(jnp.max(s, axis=1, keepdims=True), sink)
                e = jnp.exp2(s - _wide(m))
                l = jnp.sum(e, axis=1, keepdims=True) + jnp.exp2(sink - m)
                o_g.append(_dot(_mx(e), v2) * (1.0 / l))
                lse_ref[0, i, g] = m + jnp.log2(l)
            _swa_unstack(o_ref, r, o_g[0], o_g[1])
            return 0

        lax.fori_loop(0, nbh, blk, 0, unroll=2)

    seq = lambda w: pl.BlockSpec((1, S, w), lambda b, h: (b, 0, 0))
    part = lambda w: pl.BlockSpec((1, S // nh, w), lambda b, h: (b, h, 0))
    lse_spec = pl.BlockSpec((1, nbh, 2, 4 * W, LANE), lambda b, h: (b, h, 0, 0, 0))
    return pl.pallas_call(
        body, name="swa_fwd", grid=(B, nh),
        out_shape=[jax.ShapeDtypeStruct((B, S, 512), F32), jax.ShapeDtypeStruct((B, nb, 2, 4 * W, LANE), F32)],
        in_specs=[part(512), seq(LANE), seq(LANE), part(LANE), pl.BlockSpec((1, nb, W), lambda b, h: (b, 0, 0)),
                  pl.BlockSpec((1, LANE), lambda b, h: (0, 0))],
        out_specs=[part(512), lse_spec],
        compiler_params=_cparams(("arbitrary", "arbitrary")),
    )(qs3, ks3, vs3, posb, posr, sinks)


def _swa_bwd(qs3, ks3, vs3, posb, posr, sinks, os3, do3, lse):
    B, S, _ = qs3.shape
    W = WINDOW
    nb = S // W
    nh = SWA_SEQ_SPLIT
    nbh = nb // nh
    assert nbh % 2 == 0 and nbh * nh * W == S

    def body(q_ref, k_ref, v_ref, pb_ref, pr_ref, sink_ref, o_ref, do_ref, lse_ref, dq_ref, dk_ref, dv_ref, dsink_ref,
             dkt_ref, dvt_ref):
        lane1 = _lane_iota((1, LANE))
        ij = lax.broadcasted_iota(jnp.int32, (W, 2 * W), 0) - lax.broadcasted_iota(jnp.int32, (W, 2 * W), 1)
        consts = _swa_consts(sink_ref)
        hh = pl.program_id(1)
        n0 = hh * nbh

        @pl.when(hh == 0)
        def _():
            dkt_ref[...] = jnp.zeros_like(dkt_ref)
            dvt_ref[...] = jnp.zeros_like(dvt_ref)

        @pl.when(jnp.logical_and(pl.program_id(0) == 0, hh == 0))
        def _():
            dsink_ref[...] = jnp.zeros_like(dsink_ref)

        def blk(i, dsink):
            n = n0 + i
            r, kb, _, q4, k2, v2, dist4, mask4 = _swa_block(q_ref, k_ref, v_ref, pb_ref, pr_ref, n, i, ij)
            o4, do4 = o_ref[0, r, :], do_ref[0, r, :]
            dq_g = []
            dkt = jnp.zeros((LANE, 2 * W), F32)
            dvt = jnp.zeros((LANE, 2 * W), F32)
            for g, (slope, sink) in enumerate(consts):
                q_st = _swa_stack(q4, g, F32)
                do_st = _swa_stack(do4, g, F32)
                dcol = jnp.sum(do_st * _swa_stack(o4, g, F32), axis=1, keepdims=True)
                qst, dob = _mx(q_st), _mx(do_st)
                lse_c = lse_ref[0, i, g]
                pr = jnp.exp2(_dot_nt(qst, k2) - slope * dist4 + mask4 - _wide(lse_c))
                dsb = _mx(pr * (_dot_nt(dob, v2) - dcol))
                psd = jnp.exp2(sink - lse_c)[:, 0:1] * dcol
                for p in range(4):
                    dsink = dsink - jnp.where(lane1 == p + 4 * g,
                                              jnp.sum(psd[W * p:W * (p + 1)], axis=0, keepdims=True), 0.0)
                dq_g.append(_dot(dsb, k2))
                dkt = dkt + _dot(_mx(q_st.T), dsb)
                dvt = dvt + _dot(_mx(do_st.T), _mx(pr))
            _swa_unstack(dq_ref, r, dq_g[0], dq_g[1], SWA_SCALE)
            dkt_ref[kb] += dkt[:, 0:W]
            dkt_ref[kb + 1] += dkt[:, W:2 * W]
            dvt_ref[kb] += dvt[:, 0:W]
            dvt_ref[kb + 1] += dvt[:, W:2 * W]
            return dsink

        dsink_ref[...] += lax.fori_loop(0, nbh // 2, lambda t, d: blk(2 * t + 1, blk(2 * t, d)), jnp.zeros((1, LANE), F32))

        @pl.when(hh == nh - 1)
        def _():
            def flush(n, _):
                r = pl.ds(pl.multiple_of(n * W, W), W)
                dk_ref[0, r, :] = (1.0 / LOG2E) * dkt_ref[n].T
                dv_ref[0, r, :] = dvt_ref[n].T
                return 0

            lax.fori_loop(0, nb, flush, 0)

    seq = lambda w: pl.BlockSpec((1, S, w), lambda b, h: (b, 0, 0))
    part = lambda w: pl.BlockSpec((1, S // nh, w), lambda b, h: (b, h, 0))
    return pl.pallas_call(
        body, name="swa_bwd", grid=(B, nh),
        out_shape=[jax.ShapeDtypeStruct((B, S, 512), F32), jax.ShapeDtypeStruct((B, S, LANE), F32),
                   jax.ShapeDtypeStruct((B, S, LANE), F32), jax.ShapeDtypeStruct((1, LANE), F32)],
        in_specs=[part(512), seq(LANE), seq(LANE), part(LANE), pl.BlockSpec((1, nb, W), lambda b, h: (b, 0, 0)),
                  pl.BlockSpec((1, LANE), lambda b, h: (0, 0)), part(512),
                  pl.BlockSpec((1, S // nh, 512), lambda b, h: (b, h, 1)),
                  pl.BlockSpec((1, nbh, 2, 4 * W, LANE), lambda b, h: (b, h, 0, 0, 0))],
        out_specs=[part(512), seq(LANE), seq(LANE), pl.BlockSpec((1, LANE), lambda b, h: (0, 0))],
        scratch_shapes=[pltpu.VMEM((nb, LANE, W), F32), pltpu.VMEM((nb, LANE, W), F32)],
        compiler_params=_cparams(("arbitrary", "arbitrary")),
    )(qs3, ks3, vs3, posb, posr, sinks, os3, do3, lse)


def _post(om, osw, g, w_out, x2, gate, fg, tgt, S, ts):
    T, D = x2.shape
    nsb = S // ts

    def body(om_ref, os_ref, g_ref, w_ref, x_ref, gate_ref, fg_ref, t_ref,
             dx2_ref, do_ref, dg_ref, loss_ref, dfg_ref, dgate_ref, dw_ref):
        i = pl.program_id(0)
        gv = g_ref[...]
        sg = 1.0 / (1.0 + jnp.exp(-gv))
        silu = gv * sg
        o = jnp.concatenate([om_ref[...], os_ref[...]], axis=1)
        ab = _mx(o * silu)
        y = _dot(ab, w_ref[...])
        gate = gate_ref[0]
        xo = x_ref[...] + gate * y
        r2 = lax.rsqrt(jnp.mean(xo * xo, axis=-1, keepdims=True) + EPS)
        xh = xo * r2
        fg = fg_ref[...]
        diff = xh * fg - t_ref[...]
        sq = jnp.sum(diff * diff, axis=0, keepdims=True)
        part = sq[:, 0:LANE]
        for t in range(1, D // LANE):
            part = part + sq[:, LANE * t:LANE * (t + 1)]
        dout = diff * (1.0 / D)
        dxh = dout * fg
        dx2 = r2 * (dxh - xh * jnp.mean(dxh * xh, axis=-1, keepdims=True))
        dx2_ref[...] = dx2
        dyb = _mx(dx2 * gate)
        da = _dot_nt(dyb, w_ref[...])
        do_ref[...] = da * silu
        dg_ref[...] = _mx(da * o * (sg * (1.0 + gv * (1.0 - sg))))

        @pl.when(i == 0)
        def _():
            loss_ref[...] = jnp.zeros_like(loss_ref)
            dfg_ref[...] = jnp.zeros_like(dfg_ref)
            dw_ref[...] = jnp.zeros_like(dw_ref)

        @pl.when(i % nsb == 0)
        def _():
            dgate_ref[...] = jnp.zeros_like(dgate_ref)

        loss_ref[...] += (0.5 / D) * part
        dfg_ref[...] += jnp.sum(dout * xh, axis=0, keepdims=True)
        dgate_ref[0] += jnp.sum(dx2 * y, axis=0, keepdims=True)
        dw_ref[...] += _dot_tn(ab, dyb)

        @pl.when(i == T // ts - 1)
        def _():
            chunks = [dw_ref[512 + 64 * PAIR_INV[h]:512 + 64 * (PAIR_INV[h] + 1), :] for h in range(SWA_HEADS)]
            for h in range(SWA_HEADS):
                dw_ref[512 + 64 * h:512 + 64 * (h + 1), :] = chunks[h]

    row = lambda w: pl.BlockSpec((ts, w), lambda i: (i, 0))
    full = lambda a: pl.BlockSpec(a.shape, lambda i: (0,) * a.ndim)
    per_b = pl.BlockSpec((1, 1, D), lambda i: (i // nsb, 0, 0))
    return pl.pallas_call(
        body, name="post", grid=(T // ts,),
        out_shape=[jax.ShapeDtypeStruct((T, D), F32), jax.ShapeDtypeStruct((T, 1024), F32),
                   jax.ShapeDtypeStruct((T, 1024), _MXU_DTYPE), jax.ShapeDtypeStruct((1, LANE), F32),
                   jax.ShapeDtypeStruct((1, D), F32), jax.ShapeDtypeStruct(gate.shape, F32),
                   jax.ShapeDtypeStruct(w_out.shape, F32)],
        in_specs=[row(512), row(512), row(1024), full(w_out), row(D), per_b, full(fg), row(D)],
        out_specs=[row(D), row(1024), row(1024),
                   pl.BlockSpec((1, LANE), lambda i: (0, 0)), pl.BlockSpec((1, D), lambda i: (0, 0)), per_b,
                   full(w_out)],
        compiler_params=_cparams(("arbitrary",)),
    )(om, osw, g, w_out, x2, gate, fg, tgt)


def _pre_bwd(dq, dk, dv, dqs, dks, dvs, dg, zq, zkv, ql, kvl, x2, dx2, scale, ng, w_in, gq, gkv, w_uq, w_uk, w_uv,
             rope, S, ts):
    T, D = x2.shape
    nsb = S // ts
    WQ = MLA_HEADS * LANE

    def body(dq_ref, dk_ref, dv_ref, dqs_ref, dks_ref, dvs_ref, dg_ref, zq_ref, zkv_ref, ql_ref, kvl_ref, x_ref, dx2_ref,
             sc_ref,
             ng_ref, w_ref, gq_ref, gkv_ref, wuq_ref, wuk_ref, wuv_ref, cd_ref, sd_ref,
             gx_ref, dz_ref, dgq_ref, dgkv_ref, dng_ref, dsh_ref, dsc_ref, dwuq_ref, dwuk_ref, dwuv_ref):
        i = pl.program_id(0)

        @pl.when(i == 0)
        def _():
            dgq_ref[...] = jnp.zeros_like(dgq_ref)
            dgkv_ref[...] = jnp.zeros_like(dgkv_ref)
            dng_ref[...] = jnp.zeros_like(dng_ref)
            dwuq_ref[...] = jnp.zeros_like(dwuq_ref)
            dwuk_ref[...] = jnp.zeros_like(dwuk_ref)
            dwuv_ref[...] = jnp.zeros_like(dwuv_ref)

        @pl.when(i % nsb == 0)
        def _():
            dsh_ref[...] = jnp.zeros_like(dsh_ref)
            dsc_ref[...] = jnp.zeros_like(dsc_ref)

        def norm_bwd(z, dy, gain):
            r = lax.rsqrt(jnp.mean(z * z, axis=-1, keepdims=True) + EPS)
            zh = z * r
            dzh = dy * gain
            return r * (dzh - zh * jnp.mean(dzh * zh, axis=-1, keepdims=True)), jnp.sum(dy * zh, axis=0, keepdims=True)

        tables = _rope_split(cd_ref[...], sd_ref[...])
        ng = ng_ref[...]
        sc1 = 1.0 + sc_ref[0]

        def rows_chain(rs):
            cq, ck, sa, sb = (t[rs] for t in tables)
            dqr = dq_ref[rs, :].astype(F32)
            dqb = _mx(dqr * _tile_heads(cq) + pltpu.roll(dqr * _tile_heads(sa), 16, 1)
                      + pltpu.roll(dqr * _tile_heads(sb), WQ - 16, 1))
            p_uq = _dot_tn(ql_ref[rs, :], dqb)
            dzq, dgq = norm_bwd(zq_ref[rs, :], _dot_nt(dqb, wuq_ref[...]), gq_ref[...])
            dkr = dk_ref[rs, :].astype(F32)
            dkb = _mx(dkr)
            p_uk = _dot_tn(kvl_ref[rs, :], dkb)
            dvb = _mx(dv_ref[rs, :])
            p_uv = _dot_tn(kvl_ref[rs, :], dvb)
            dzkv, dgkv = norm_bwd(zkv_ref[rs, :], _dot_nt(dkb, wuk_ref[...]) + _dot_nt(dvb, wuv_ref[...]), gkv_ref[...])
            dkpe = dkr[:, 0:LANE]
            for h in range(1, MLA_HEADS):
                dkpe = dkpe + dkr[:, LANE * h:LANE * (h + 1)]
            dkro = dkpe * ck + pltpu.roll(dkpe * sa, 16, 1) + pltpu.roll(dkpe * sb, LANE - 16, 1)
            dz_ref[rs, 0:384] = _mx(dzq)
            dz_ref[rs, 384:640] = _mx(dzkv)
            dz_ref[rs, 640:768] = _mx(dkro)
            dz_ref[rs, 768:1280] = dg_ref[rs, 0:512]
            dz_ref[rs, 1280:1792] = _mx(dqs_ref[rs, :])
            dz_ref[rs, 1792:1920] = _mx(dks_ref[rs, :])
            dz_ref[rs, 1920:2048] = _mx(dvs_ref[rs, :])
            dz_ref[rs, 2048:2560] = dg_ref[rs, 512:1024]
            dh = _dot(dz_ref[rs, :], w_ref[...])
            x = x_ref[rs, :]
            r1 = lax.rsqrt(jnp.mean(x * x, axis=-1, keepdims=True) + EPS)
            xn = x * r1
            dxn = dh * ng * sc1
            gx_ref[rs, :] = dx2_ref[rs, :] + r1 * (dxn - xn * jnp.mean(dxn * xn, axis=-1, keepdims=True))
            return (p_uq, p_uk, p_uv, dgq, dgkv, jnp.sum(dh, axis=0, keepdims=True),
                    jnp.sum(dh * (xn * ng), axis=0, keepdims=True), jnp.sum(dh * xn * sc1, axis=0, keepdims=True))

        hr = ts // 2
        parts = [rows_chain(slice(hr * t, hr * (t + 1))) for t in range(2)]
        p_uq, p_uk, p_uv, dgq, dgkv, dsh, dsc, dng = (a + b for a, b in zip(*parts))
        dwuq_ref[...] += p_uq
        dwuk_ref[...] += p_uk
        dwuv_ref[...] += p_uv
        dgq_ref[...] += dgq
        dgkv_ref[...] += dgkv
        dsh_ref[0] += dsh
        dsc_ref[0] += dsc
        dng_ref[...] += dng

    row = lambda w: pl.BlockSpec((ts, w), lambda i: (i, 0))
    full = lambda a: pl.BlockSpec(a.shape, lambda i: (0,) * a.ndim, pipeline_mode=pl.Buffered(1))
    per_b = pl.BlockSpec((1, 1, D), lambda i: (i // nsb, 0, 0))
    dense = pl.BlockSpec((ts // 8, LANE), lambda i: (i, 0))
    vec = lambda w: pl.BlockSpec((1, w), lambda i: (0, 0))
    return pl.pallas_call(
        body, name="pre_bwd", grid=(T // ts,),
        out_shape=[jax.ShapeDtypeStruct((T, D), F32), jax.ShapeDtypeStruct((T, D_IN_PAD), _MXU_DTYPE), jax.ShapeDtypeStruct((1, 384), F32),
                   jax.ShapeDtypeStruct((1, 256), F32), jax.ShapeDtypeStruct((1, D), F32),
                   jax.ShapeDtypeStruct(scale.shape, F32), jax.ShapeDtypeStruct(scale.shape, F32),
                   jax.ShapeDtypeStruct(w_uq.shape, F32), jax.ShapeDtypeStruct(w_uk.shape, F32),
                   jax.ShapeDtypeStruct(w_uv.shape, F32)],
        in_specs=[row(WQ), row(WQ), row(512), row(512), row(LANE), row(LANE), row(1024), row(384), row(256), row(384),
                  row(256), row(D), row(D), per_b, full(ng), full(w_in), full(gq), full(gkv), full(w_uq), full(w_uk), full(w_uv),
                  dense, dense],
        out_specs=[row(D), row(D_IN_PAD), vec(384), vec(256), vec(D), per_b, per_b, full(w_uq), full(w_uk), full(w_uv)],
        compiler_params=_cparams(("arbitrary",)),
    )(dq, dk, dv, dqs, dks, dvs, dg, zq, zkv, ql, kvl, x2, dx2, scale, ng, w_in, gq, gkv, w_uq, w_uk, w_uv, *rope)


def _w_in_row_runs():
    runs = [(0, 0, 640), (640, 704, 32), (672, 768, 512)]
    runs += [(1184 + 64 * h, 1280 + 64 * PAIR_INV[h], 64) for h in range(8)]
    runs += [(1696, 1792, 256)]
    runs += [(1952 + 64 * h, 2048 + 64 * PAIR_INV[h], 64) for h in range(8)]
    return runs


def _dw_in_t(dz, hb, tn, tk):
    T, M = dz.shape
    N = hb.shape[1]
    nk = T // tk

    def body(a_ref, b_ref, o_ref, acc_ref):
        k = pl.program_id(1)

        @pl.when(k == 0)
        def _():
            acc_ref[...] = jnp.zeros_like(acc_ref)

        acc_ref[...] += _dot_tn(a_ref[...], b_ref[...])

        @pl.when(k == nk - 1)
        def _():
            for nat, pad, size in _w_in_row_runs():
                o_ref[nat:nat + size, :] = acc_ref[pad:pad + size, :]

    return pl.pallas_call(
        body, name="dw_in", grid=(N // tn, nk),
        out_shape=jax.ShapeDtypeStruct((D_IN, N), F32),
        in_specs=[pl.BlockSpec((tk, M), lambda j, k: (k, 0)), pl.BlockSpec((tk, tn), lambda j, k: (k, j))],
        out_specs=pl.BlockSpec((D_IN, tn), lambda j, k: (0, j)),
        scratch_shapes=[pltpu.VMEM((M, tn), F32)],
        compiler_params=_cparams(("arbitrary", "arbitrary")),
    )(dz, hb)


def _finalize(parts_all, dmod_all, dmod_cols, c_act_all):
    nparts = parts_all.shape[-1]
    nmod = dmod_all.shape[-1]

    def body(p_ref, dm_ref, dmc_ref, c_ref, ps_ref, loss_ref, db_ref, dw_ref):
        acc = p_ref[0]
        for j in range(1, N_DEV):
            acc = acc + p_ref[j]
        ps_ref[...] = acc
        loss_ref[...] = jnp.sum(acc[:, 0:LANE], axis=1, keepdims=True)
        db = dm_ref[0:1, :]
        for j in range(1, dm_ref.shape[0]):
            db = db + dm_ref[j:j + 1, :]
        db_ref[...] = db
        dw_ref[...] = _dot_tn(_mx(_silu(c_ref[...])), _mx(dmc_ref[...]))

    return pl.pallas_call(
        body, name="finalize",
        out_shape=[jax.ShapeDtypeStruct((1, nparts), F32), jax.ShapeDtypeStruct((1, 1), F32),
                   jax.ShapeDtypeStruct((1, nmod), F32),
                   jax.ShapeDtypeStruct((c_act_all.shape[1], dmod_cols.shape[1]), F32)],
        in_specs=[_vmem()] * 4, out_specs=[_vmem()] * 4,
        compiler_params=_cparams(),
    )(parts_all, dmod_all, dmod_cols, c_act_all)


def _adamw(ws, gs, ms, vs):
    n = len(ws)

    def body(*refs):
        w_refs, g_refs, m_refs, v_refs, d_refs, nm_refs, nv_refs = (refs[t * n:(t + 1) * n] for t in range(7))
        for t in range(n):
            gv = g_refs[t][...]
            nm = ADAM_B1 * m_refs[t][...] + (1.0 - ADAM_B1) * gv
            nv = ADAM_B2 * v_refs[t][...] + (1.0 - ADAM_B2) * (gv * gv)
            m_hat = nm / (1.0 - ADAM_B1 ** ADAM_STEP)
            v_hat = nv / (1.0 - ADAM_B2 ** ADAM_STEP)
            d_refs[t][...] = -ADAM_LR * (m_hat / (jnp.sqrt(v_hat) + ADAM_EPS) + ADAM_WD * w_refs[t][...])
            nm_refs[t][...] = nm
            nv_refs[t][...] = nv

    out = pl.pallas_call(
        body, name="adamw",
        out_shape=[jax.ShapeDtypeStruct(w.shape, F32) for w in ws] * 3,
        in_specs=[_vmem()] * (4 * n), out_specs=[_vmem()] * (3 * n),
        compiler_params=_cparams(),
    )(*ws, *gs, *ms, *vs)
    return out[:n], out[n:2 * n], out[2 * n:]


def _pair_perm(a, axis, order):
    a = jnp.moveaxis(a, axis, -1)
    lead = a.shape[:-1]
    a = a.reshape(lead + (8, 64))[..., list(order), :].reshape(lead + (512,))
    return jnp.moveaxis(a, -1, axis)


def _rope_table(positions):
    T = positions.size
    inv = ROPE_THETA ** (-jnp.arange(0, MLA_ROPE, 2, dtype=F32) / MLA_ROPE)
    pos = jnp.repeat(positions.reshape(T // 8, 8)[:, ::-1].astype(F32), MLA_ROPE // 2, axis=1)
    ang = pos * jnp.tile(inv, 8)[None, :]
    return jnp.cos(ang), jnp.sin(ang)


def _local_step(x, mod, positions, ng, w_in_t, gq, gkv, w_uq, w_ukv, sinks, w_out, fg, tgt,
                ts=512, fq=512, fk=512, bq=512, bk=512):
    B, S, D = x.shape
    T = B * S
    x2 = x.reshape(T, D)
    shift, scale, gate = (mod[:, None, k * D:(k + 1) * D] for k in range(3))
    w_uq_p = jnp.pad(w_uq.reshape(Q_LORA, MLA_HEADS, 96), ((0, 0), (0, 0), (0, 32))).reshape(Q_LORA, MLA_HEADS * LANE)
    w_ukv3 = w_ukv.reshape(KV_LORA, MLA_HEADS, 128)
    w_uk_p = jnp.pad(w_ukv3[:, :, :64], ((0, 0), (0, 0), (0, 64))).reshape(KV_LORA, MLA_HEADS * LANE)
    w_uv = w_ukv3[:, :, 64:].reshape(KV_LORA, 512)
    w_out_p = jnp.concatenate([w_out[:512], _pair_perm(w_out[512:], 0, PAIR_ORDER)], axis=0)
    rope = _rope_table(positions)
    posf = positions.astype(F32)
    posb = jnp.broadcast_to(posf[:, :, None], (B, S, LANE))
    posr = posf.reshape(B, S // WINDOW, WINDOW)
    sinks_l = jnp.pad(sinks.reshape(1, SWA_HEADS), ((0, 0), (0, LANE - SWA_HEADS)))

    (hb, zq, zkv, ql, kvl, q, k, v, qs, ks, vs, g, w_in_p) = _pre_fwd(
        x2, shift, scale, ng, w_in_t, gq, gkv, w_uq_p, w_uk_p, w_uv, rope, S, ts)
    r3 = lambda a: a.reshape(B, S, a.shape[-1])
    om, lse_m = _mla_fwd(r3(q), r3(k), r3(v), fq, fk)
    osw, lse_s = _swa_fwd(r3(qs), r3(ks), r3(vs), posb, posr, sinks_l)
    dx2, do, dg, loss_v, dfg, dgate, dw_out = _post(
        om.reshape(T, 512), osw.reshape(T, 512), g, w_out_p, x2, gate, fg.reshape(1, D), tgt.reshape(T, D), S, ts)
    do3 = r3(do)
    dq, dk, dv = _mla_bwd(r3(q), r3(k), r3(v), om, do3, lse_m, bq, bk)
    dqs, dks, dvs, dsink = _swa_bwd(r3(qs), r3(ks), r3(vs), posb, posr, sinks_l, osw, do3, lse_s)
    f2 = lambda a: a.reshape(T, a.shape[-1])
    gx, dz, dgq, dgkv, dng, dsh, dsc, dw_uq_p, dw_uk_p, dw_uv = _pre_bwd(
        f2(dq), f2(dk), f2(dv), f2(dqs), f2(dks), f2(dvs), dg, zq, zkv, ql, kvl, x2, dx2, scale, ng, w_in_p, gq, gkv,
        w_uq_p, w_uk_p, w_uv, rope, S, ts)
    tk = min(T, 1024)
    dw_in_t = _dw_in_t(dz, hb, 512, tk)
    dw_uq = dw_uq_p.reshape(Q_LORA, MLA_HEADS, LANE)[:, :, :96].reshape(Q_LORA, 768)
    dw_uk = dw_uk_p.reshape(KV_LORA, MLA_HEADS, LANE)[:, :, :64]
    dw_uv = dw_uv.reshape(KV_LORA, MLA_HEADS, 64)
    dw_ukv = jnp.concatenate([dw_uk, dw_uv], axis=2).reshape(KV_LORA, 1024)
    parts = jnp.concatenate([loss_v, dfg, dng, dgq, dgkv, dsink], axis=1)
    dmod = jnp.concatenate([dsh, dsc, dgate], axis=2).reshape(B, 3 * D)
    return gx.reshape(B, S, D), dw_in_t, dw_uq, dw_ukv, dw_out, parts, dmod


def kernel(x, c, positions, w_ada, b_ada, norm_gain, w_in, q_norm_gain, kv_norm_gain, w_uq, w_ukv, swa_sinks, w_out, final_gain, loss_target, m_w_ada, m_b_ada, m_norm_gain, m_w_in, m_q_norm_gain, m_kv_norm_gain, m_w_uq, m_w_ukv, m_swa_sinks, m_w_out, m_final_gain, v_w_ada, v_b_ada, v_norm_gain, v_w_in, v_q_norm_gain, v_kv_norm_gain, v_w_uq, v_w_ukv, v_swa_sinks, v_w_out, v_final_gain):
    B, S, D = x.shape
    me = 4 * lax.axis_index("x") + 2 * lax.axis_index("y") + lax.axis_index("c")
    bf = _MXU_DTYPE

    ncol = w_ada.shape[2]
    b_cols = lax.dynamic_slice_in_dim(b_ada, me * ncol, ncol, axis=1)
    c_all, win_g, wuq_g, wukv_g, wout_g, mod_g = _all_gather(
        [c, w_in[0].T.astype(bf), w_uq[0].astype(bf), w_ukv[0].astype(bf), w_out[0].astype(bf)], "ag_weights",
        fused=((w_ada[0], b_cols), _ada_cols, jax.ShapeDtypeStruct((N_DEV, B, ncol), F32)))
    c_all = c_all.reshape(N_DEV * B, D)
    cat_cols = lambda a: jnp.transpose(a, (1, 0, 2)).reshape(a.shape[1], N_DEV * a.shape[2])
    w_in_t, w_uq_f, w_ukv_f = win_g.reshape(D_IN, D), cat_cols(wuq_g), cat_cols(wukv_g)
    w_out_f = wout_g.reshape(D, D)

    mod = lax.dynamic_index_in_dim(mod_g, me, axis=1, keepdims=False)
    mod = jnp.transpose(mod, (1, 0, 2)).reshape(B, 3 * D)

    gx, dw_in_t, dw_uq, dw_ukv, dw_out, parts, dmod = _local_step(
        x, mod, positions, norm_gain, w_in_t, q_norm_gain, kv_norm_gain, w_uq_f, w_ukv_f, swa_sinks,
        w_out_f, final_gain, loss_target)

    parts_g, dmod_g = _all_gather([parts, dmod], "ag_small")
    dmod_all = dmod_g.reshape(N_DEV * B, 3 * D)
    dmod_cols = lax.dynamic_slice_in_dim(dmod_all, me * ncol, ncol, axis=1)
    psum, loss, g_b_ada, g_w_ada = _finalize(parts_g, dmod_all, dmod_cols, c_all)
    loss = loss.reshape(())
    o = LANE
    g_final_gain = psum[0, o:o + D]
    g_norm_gain = psum[:, o + D:o + 2 * D]
    o += 2 * D
    g_q_norm_gain = psum[:, o:o + Q_LORA]
    g_kv_norm_gain = psum[:, o + Q_LORA:o + Q_LORA + KV_LORA]
    o += Q_LORA + KV_LORA
    g_sinks = psum[:, o:o + SWA_HEADS]

    split_cols = lambda a: jnp.transpose(a.reshape(a.shape[0], 4, 2, a.shape[1] // N_DEV), (1, 2, 0, 3))
    split_rows = lambda a: a.reshape(4, 2, a.shape[0] // N_DEV, a.shape[1])
    g_w_in_t, g_w_uq, g_w_ukv, g_w_out = _reduce_scatter(
        [split_rows(dw_in_t), split_cols(dw_uq), split_cols(dw_ukv), split_rows(dw_out)], "rs_grads")
    g_w_in = g_w_in_t.T

    grads = [g_w_ada, g_b_ada, g_norm_gain, g_w_in, g_q_norm_gain, g_kv_norm_gain, g_w_uq, g_w_ukv, g_sinks, g_w_out,
             g_final_gain]
    ws = [w_ada, b_ada, norm_gain, w_in, q_norm_gain, kv_norm_gain, w_uq, w_ukv, swa_sinks, w_out, final_gain]
    ms = [m_w_ada, m_b_ada, m_norm_gain, m_w_in, m_q_norm_gain, m_kv_norm_gain, m_w_uq, m_w_ukv, m_swa_sinks, m_w_out,
          m_final_gain]
    vs = [v_w_ada, v_b_ada, v_norm_gain, v_w_in, v_q_norm_gain, v_kv_norm_gain, v_w_uq, v_w_ukv, v_swa_sinks, v_w_out,
          v_final_gain]
    two_d = [(1, w.shape[0]) if w.ndim == 1 else w.shape[-2:] for w in ws]
    flat = lambda arrs: [a.reshape(s) for a, s in zip(arrs, two_d)]
    deltas, new_ms, new_vs = _adamw(flat(ws), flat(grads), flat(ms), flat(vs))
    shaped = lambda arrs: [a.reshape(w.shape) for a, w in zip(arrs, ws)]
    return (loss, gx, *shaped(grads), *shaped(deltas), *shaped(new_ms), *shaped(new_vs))
```

```python
import functools

import jax
import jax.numpy as jnp
from jax import lax
from jax.experimental import pallas as pl
from jax.experimental.pallas import tpu as pltpu

F32 = jnp.float32
_MXU_DTYPE = jnp.bfloat16
MLA_GRAD_DTYPE = jnp.bfloat16

N_DEV = 8
MLA_HEADS = 8
MLA_NOPE = 64
MLA_ROPE = 32
Q_LORA = 384
KV_LORA = 256
SWA_HEADS = 8
SWA_KV_HEADS = 2
SWA_HEAD_DIM = 64
WINDOW = 128
ROPE_THETA = 10000.0
EPS = 1e-6
MLA_SCALE = float((MLA_NOPE + MLA_ROPE) ** -0.5)
SWA_SCALE = float(SWA_HEAD_DIM ** -0.5)
LOG2E = 1.4426950408889634
MLA_QSCALE = MLA_SCALE * LOG2E
D_IN = 2464
D_IN_PAD = 2560
PAIR_ORDER = (0, 4, 1, 5, 2, 6, 3, 7)
PAIR_INV = (0, 2, 4, 6, 1, 3, 5, 7)

ADAM_LR = 0.001
ADAM_B1 = 0.9
ADAM_B2 = 0.999
ADAM_EPS = 1e-08
ADAM_WD = 0.01
ADAM_STEP = 10

LANE = 128
VMEM_LIMIT = 56 * 1024 * 1024

MESH = pl.DeviceIdType.MESH
NEG_INF = float("-inf")
SWA_SEQ_SPLIT = 2


def _mx(a):
    return a.astype(_MXU_DTYPE)


def _dot(a, b):
    return jnp.dot(a, b, preferred_element_type=F32)


def _dot_nt(a, b):
    return lax.dot_general(a, b, (((1,), (1,)), ((), ())), preferred_element_type=F32)


def _dot_tn(a, b):
    return lax.dot_general(a, b, (((0,), (0,)), ((), ())), preferred_element_type=F32)


def _cparams(sem=None):
    return pltpu.CompilerParams(dimension_semantics=sem, vmem_limit_bytes=VMEM_LIMIT)


def _vmem():
    return pl.BlockSpec(memory_space=pltpu.VMEM)


def _lane_iota(shape):
    return lax.broadcasted_iota(jnp.int32, shape, len(shape) - 1)


def _gather_program(srcs, outs, send_sems, recv_sems, local_sems):
    x, y, c = lax.axis_index("x"), lax.axis_index("y"), lax.axis_index("c")
    me, sibling = (x, y, c), (x, y, 1 - c)
    chips = [(1 - x, y), (x, 1 - y), (1 - x, 1 - y)]

    def slot(a, dev):
        return outs[a].at[4 * dev[0] + 2 * dev[1] + dev[2]]

    def copy(a, k, block, to, src=None):
        return pltpu.make_async_remote_copy(
            src_ref=slot(a, block) if src is None else src, dst_ref=slot(a, block),
            send_sem=send_sems.at[7 * a + k], recv_sem=recv_sems.at[7 * a + k],
            device_id=to, device_id_type=MESH)

    def local(a):
        return pltpu.make_async_copy(srcs[a], slot(a, me), local_sems.at[a])

    def start(a):
        local(a).start()
        cps = [copy(a, 0, me, sibling, src=srcs[a])]
        cps += [copy(a, 1 + j, me, (*chip, c), src=srcs[a]) for j, chip in enumerate(chips)]
        for cp in cps:
            cp.start()
        return cps

    def finish(group):
        cps = []
        for j, chip in enumerate(chips):
            for a in group:
                copy(a, 1 + j, (*chip, c), me).wait_recv()
                cp = copy(a, 4 + j, (*chip, c), sibling)
                cp.start()
                cps.append(cp)
        for a in group:
            copy(a, 0, sibling, me).wait_recv()
            for j, chip in enumerate(chips):
                copy(a, 4 + j, (*chip, 1 - c), me).wait_recv()
            local(a).wait()
        return cps

    return start, finish


def _gather_sems(m):
    return [pltpu.SemaphoreType.DMA((7 * m,)), pltpu.SemaphoreType.DMA((7 * m,)), pltpu.SemaphoreType.DMA((m,))]


def _all_gather(arrs, name, fused=None):
    n = len(arrs)
    extra, fn, piece = fused if fused else ((), None, None)
    ne, m = len(extra), n + (1 if fused else 0)

    def body(*refs):
        ins, ex, outs = refs[:n], refs[n:n + ne], refs[n + ne:n + ne + m]
        rest = refs[n + ne + m:]
        srcs = list(ins) + ([rest[0]] if fused else [])
        start, finish = _gather_program(srcs, outs, *rest[-3:])
        pending = []
        for a in range(n):
            pending += start(a)
        if fused:
            pending += finish([0])
            fn(srcs[n], outs[0], *ex)
            pending += start(n)
            pending += finish([n] + list(range(1, n)))
        else:
            pending += finish(list(range(n)))
        for cp in pending:
            cp.wait_send()

    out_shape = [jax.ShapeDtypeStruct((N_DEV,) + a.shape, a.dtype) for a in arrs]
    scratch = []
    if fused:
        out_shape.append(jax.ShapeDtypeStruct((N_DEV,) + piece.shape, piece.dtype))
        scratch.append(pltpu.VMEM(piece.shape, piece.dtype))
    return pl.pallas_call(
        body, name=name, out_shape=out_shape,
        in_specs=[_vmem()] * (n + ne), out_specs=[_vmem()] * m,
        scratch_shapes=scratch + _gather_sems(m),
        compiler_params=pltpu.CompilerParams(vmem_limit_bytes=VMEM_LIMIT),
    )(*arrs, *extra)


def _reduce_scatter(arrs, name, gather=()):
    n, g = len(arrs), len(gather)
    halves = [a.astype(jnp.bfloat16) for a in arrs]

    def body(*refs):
        xs, xbs, gins = refs[:n], refs[n:2 * n], refs[2 * n:2 * n + g]
        outs, gouts = refs[2 * n + g:3 * n + g], refs[3 * n + g:3 * n + 2 * g]
        rest = refs[3 * n + 2 * g:]
        parts, recv_a, send_b, recv_b = (rest[t * n:(t + 1) * n] for t in range(4))
        send_sems, recv_sems, local_sems = rest[4 * n:4 * n + 3]
        x, y, c = lax.axis_index("x"), lax.axis_index("y"), lax.axis_index("c")
        myq = 2 * x + y
        gather_start, gather_finish = _gather_program(gins, gouts, *rest[4 * n + 3:])

        def chip(k):
            return (1 - x if k & 2 else x, 1 - y if k & 1 else y)

        def from_sibling(a):
            return pltpu.make_async_remote_copy(
                src_ref=xbs[a].at[:, 1 - c], dst_ref=recv_a[a], send_sem=send_sems.at[4 * a], recv_sem=recv_sems.at[4 * a],
                device_id=(x, y, 1 - c), device_id_type=MESH)

        def to_owner(a, k):
            qx, qy = chip(k)
            return pltpu.make_async_remote_copy(
                src_ref=send_b[a].at[2 * qx + qy], dst_ref=recv_b[a].at[myq],
                send_sem=send_sems.at[4 * a + k], recv_sem=recv_sems.at[4 * a + k],
                device_id=(qx, qy, c), device_id_type=MESH)

        mine = [pltpu.make_async_copy(xs[a].at[:, c], parts[a], local_sems.at[a]) for a in range(n)]
        first = [from_sibling(a) for a in range(n)]
        for cp in mine + first:
            cp.start()
        second = []
        for b in range(g):
            second += gather_start(b)
        for a in range(n):
            mine[a].wait()
            first[a].wait_recv()
            parts[a][...] = parts[a][...] + recv_a[a][...].astype(F32)
            send_b[a][...] = parts[a][...].astype(jnp.bfloat16)
            for k in range(1, 4):
                cp = to_owner(a, k)
                cp.start()
                second.append(cp)
        second += gather_finish(list(range(g)))
        for a in range(n):
            acc = parts[a][myq]
            for k in range(1, 4):
                to_owner(a, k).wait_recv()
                qx, qy = chip(k)
                acc = acc + recv_b[a][2 * qx + qy].astype(F32)
            outs[a][...] = acc
        for cp in first + second:
            cp.wait_send()

    quarter = lambda a, dt: pltpu.VMEM((4,) + a.shape[2:], dt)
    return pl.pallas_call(
        body, name=name,
        out_shape=[jax.ShapeDtypeStruct(a.shape[2:], F32) for a in arrs]
        + [jax.ShapeDtypeStruct((N_DEV,) + a.shape, a.dtype) for a in gather],
        in_specs=[pl.BlockSpec(memory_space=pl.ANY)] * (2 * n) + [_vmem()] * g, out_specs=[_vmem()] * (n + g),
        scratch_shapes=[quarter(a, F32) for a in arrs] + [quarter(a, jnp.bfloat16) for a in arrs] * 3
        + [pltpu.SemaphoreType.DMA((4 * n,)), pltpu.SemaphoreType.DMA((4 * n,)), pltpu.SemaphoreType.DMA((n,))]
        + _gather_sems(g),
        compiler_params=pltpu.CompilerParams(vmem_limit_bytes=VMEM_LIMIT),
    )(*arrs, *halves, *gather)


def _silu(t):
    return t * (1.0 / (1.0 + jnp.exp(-t)))


def _ada_cols(piece_ref, c_all_ref, w_ref, b_ref):
    w = _mx(w_ref[...])
    for d in range(N_DEV):
        piece_ref[d] = _dot(_mx(_silu(c_all_ref[d])), w) + b_ref[...]


def _rope_split(cd, sd):
    n = cd.shape[0]

    def expand(d):
        rep = jnp.broadcast_to(d[:, None, :], (n, 8, LANE)).reshape(8 * n, LANE)
        return pltpu.roll(rep, 0, 1, stride=16, stride_axis=0)

    c, s = expand(cd), expand(sd)
    lane = _lane_iota(c.shape)
    first = jnp.logical_and(lane >= 64, lane < 80)
    second = jnp.logical_and(lane >= 80, lane < 96)
    ck = jnp.where(first, pltpu.roll(c, 80, 1), jnp.where(second, pltpu.roll(c, 96, 1), 0.0))
    cq = jnp.where(lane < 64, 1.0, ck)
    sa = jnp.where(first, -pltpu.roll(s, 80, 1), 0.0)
    sb = jnp.where(second, pltpu.roll(s, 96, 1), 0.0)
    return cq, ck, sa, sb


def _tile_heads(t):
    return jnp.concatenate([t] * MLA_HEADS, axis=1)


def _pre_fwd(x2, shift, scale, ng, w_in, gq, gkv, w_uq, w_uk, w_uv, rope, S, ts):
    T, D = x2.shape
    nsb = S // ts
    WQ = MLA_HEADS * LANE

    def body(x_ref, sh_ref, sc_ref, ng_ref, wn_ref, gq_ref, gkv_ref, wuq_ref, wuk_ref, wuv_ref,
             cd_ref, sd_ref,
             hb_ref, zq_ref, zkv_ref, ql_ref, kvl_ref, q_ref, k_ref, v_ref, qs_ref, ks_ref, vs_ref, g_ref, wp_ref):
        @pl.when(pl.program_id(0) == 0)
        def _():
            wp_ref[640:704, :] = jnp.zeros((64, D), wp_ref.dtype)
            wp_ref[736:768, :] = jnp.zeros((32, D), wp_ref.dtype)
            for nat, pad, size in _w_in_row_runs():
                wp_ref[pad:pad + size, :] = wn_ref[nat:nat + size, :]

        w_ref = wp_ref
        x = x_ref[...]
        r1 = lax.rsqrt(jnp.mean(x * x, axis=-1, keepdims=True) + EPS)
        h = ((x * r1) * ng_ref[...]) * (1.0 + sc_ref[0]) + sh_ref[0]
        hb = _mx(h)
        hb_ref[...] = hb
        zq = _dot_nt(hb, w_ref[0:384, :])
        zq_ref[...] = zq
        rq = lax.rsqrt(jnp.mean(zq * zq, axis=-1, keepdims=True) + EPS)
        ql = _mx((zq * rq) * gq_ref[...])
        ql_ref[...] = ql
        q = _dot(ql, wuq_ref[...])
        cq, ck, sa, sb = _rope_split(cd_ref[...], sd_ref[...])
        q = (q * _tile_heads(cq) + pltpu.roll(q, WQ - 16, 1) * _tile_heads(sa)
             + pltpu.roll(q, 16, 1) * _tile_heads(sb))
        q_ref[...] = _mx(q * MLA_QSCALE)
        zkv = _dot_nt(hb, w_ref[384:640, :])
        zkv_ref[...] = zkv
        rkv = lax.rsqrt(jnp.mean(zkv * zkv, axis=-1, keepdims=True) + EPS)
        kvl = _mx((zkv * rkv) * gkv_ref[...])
        kvl_ref[...] = kvl
        kr = _dot_nt(hb, w_ref[640:768, :])
        kpe = kr * ck + pltpu.roll(kr, LANE - 16, 1) * sa + pltpu.roll(kr, 16, 1) * sb
        kf = _dot(kvl, wuk_ref[...])
        k_ref[...] = _mx(kf + jnp.concatenate([kpe] * MLA_HEADS, axis=1))
        v_ref[...] = _mx(_dot(kvl, wuv_ref[...]))
        g_ref[:, 0:512] = _dot_nt(hb, w_ref[768:1280, :])
        qs_ref[...] = _mx(_dot_nt(hb, w_ref[1280:1792, :]) * (SWA_SCALE * LOG2E))
        ks_ref[...] = _mx(_dot_nt(hb, w_ref[1792:1920, :]))
        vs_ref[...] = _mx(_dot_nt(hb, w_ref[1920:2048, :]))
        g_ref[:, 512:1024] = _dot_nt(hb, w_ref[2048:2560, :])

    row = lambda w: pl.BlockSpec((ts, w), lambda i: (i, 0))
    dense = pl.BlockSpec((ts // 8, LANE), lambda i: (i, 0))
    full = lambda a: pl.BlockSpec(a.shape, lambda i: (0,) * a.ndim)
    per_b = pl.BlockSpec((1, 1, D), lambda i: (i // nsb, 0, 0))
    out_w = [(D, _MXU_DTYPE), (384, F32), (256, F32), (384, _MXU_DTYPE), (256, _MXU_DTYPE), (WQ, _MXU_DTYPE),
             (WQ, _MXU_DTYPE), (512, _MXU_DTYPE), (512, _MXU_DTYPE), (128, _MXU_DTYPE), (128, _MXU_DTYPE), (1024, F32)]
    return pl.pallas_call(
        body, name="pre_fwd", grid=(T // ts,),
        out_shape=[jax.ShapeDtypeStruct((T, w), dt) for w, dt in out_w] + [jax.ShapeDtypeStruct((D_IN_PAD, D), w_in.dtype)],
        in_specs=[row(D), per_b, per_b, full(ng), full(w_in), full(gq), full(gkv), full(w_uq), full(w_uk), full(w_uv),
                  dense, dense],
        out_specs=[row(w) for w, _ in out_w] + [pl.BlockSpec((D_IN_PAD, D), lambda i: (0, 0))],
        compiler_params=_cparams(("arbitrary",)),
    )(x2, shift, scale, ng, w_in, gq, gkv, w_uq, w_uk, w_uv, *rope)


def _mla_fwd(q3, k3, v3, tq, tk):
    B, S, _ = q3.shape
    nq = S // tq
    assert tk % tq == 0 or tq % tk == 0
    n_masked = max(1, tq // tk)

    HPS = 2
    NH = 2 * HPS

    def body(q_ref, k_ref, v_ref, o_ref, lse_ref):
        qi = pl.program_id(2)
        rows = lax.broadcasted_iota(jnp.int32, (tq, tk), 0)
        cols = lax.broadcasted_iota(jnp.int32, (tq, tk), 1)
        qs = [q_ref[0, :, LANE * j:LANE * (j + 1)] for j in range(NH)]
        n_full = (qi * tq) // tk

        low_k = _lane_iota((tk, LANE)) < 64

        def step(kt, carry, masked):
            r0 = pl.multiple_of(kt * tk, tk)
            out = []
            for j in range(NH):
                v2 = v_ref[0, pl.ds(r0, tk), LANE * (j // 2):LANE * (j // 2 + 1)]
                vj = (jnp.where(low_k, v2, 1.0) if j % 2 == 0 else jnp.where(low_k, 1.0, v2)).astype(v2.dtype)
                m, acc = carry[j]
                s = _dot_nt(qs[j], k_ref[0, pl.ds(r0, tk), LANE * j:LANE * (j + 1)])
                if masked:
                    s = jnp.where(rows + qi * tq >= cols + kt * tk, s, NEG_INF)
                m_new = jnp.maximum(m, jnp.max(s, axis=1, keepdims=True))
                alpha = jnp.exp2(m - m_new)
                p = jnp.exp2(s - m_new)
                acc = alpha * acc + _dot(_mx(p), vj)
                out.append((m_new, acc))
            return tuple(out)

        init = (jnp.full((tq, 1), NEG_INF, F32), jnp.zeros((tq, LANE), F32))
        carry = lax.fori_loop(0, n_full, functools.partial(step, masked=False), (init,) * NH)
        for t in range(n_masked):
            carry = step(n_full + t, carry, True)
        low = _lane_iota((tq, LANE)) < 64
        for t in range(HPS):
            (m0, a0), (m1, a1) = carry[2 * t], carry[2 * t + 1]
            l0 = jnp.where(low, pltpu.roll(a0, 64, 1), a0)
            l1 = jnp.where(low, a1, pltpu.roll(a1, 64, 1))
            o_ref[0, :, LANE * t:LANE * (t + 1)] = jnp.where(low, a0 / l0, a1 / l1)
            lse_ref[0, 2 * t] = m0 + jnp.log2(l0)
            lse_ref[0, 2 * t + 1] = m1 + jnp.log2(l1)

    return pl.pallas_call(
        body, name="mla_fwd", grid=(B, MLA_HEADS // NH, nq),
        out_shape=[jax.ShapeDtypeStruct((B, S, 512), F32), jax.ShapeDtypeStruct((B, MLA_HEADS, S, LANE), F32)],
        in_specs=[pl.BlockSpec((1, tq, NH * LANE), lambda b, hp, i: (b, i, hp)),
                  pl.BlockSpec((1, S, NH * LANE), lambda b, hp, i: (b, 0, hp)),
                  pl.BlockSpec((1, S, HPS * LANE), lambda b, hp, i: (b, 0, hp))],
        out_specs=[pl.BlockSpec((1, tq, HPS * LANE), lambda b, hp, i: (b, i, hp)),
                   pl.BlockSpec((1, NH, tq, LANE), lambda b, hp, i: (b, hp, i, 0))],
        compiler_params=_cparams(("arbitrary", "arbitrary", "arbitrary")),
    )(q3, k3, v3)


def _mla_bwd(q3, k3, v3, o3, do3, lse, tq, tk, hps=2):
    B, S, _ = q3.shape
    nq, nk = S // tq, S // tk
    nh = 2 * hps
    assert tk % tq == 0

    def body(q_ref, k_ref, v_ref, o_ref, do_ref, lse_ref, dq_ref, dk_ref, dv_ref, dkt_ref, dvt_ref):
        dkt_ref[...] = jnp.zeros_like(dkt_ref)
        dvt_ref[...] = jnp.zeros_like(dvt_ref)
        rows = lax.broadcasted_iota(jnp.int32, (tq, tk), 0)
        cols = lax.broadcasted_iota(jnp.int32, (tq, tk), 1)
        lane = _lane_iota((tq, LANE))

        def q_tile(qi, _):
            r = pl.ds(pl.multiple_of(qi * tq, tq), tq)
            heads = []
            for j in range(nh):
                lanes = slice(LANE * j, LANE * (j + 1))
                pair = slice(LANE * (j // 2), LANE * (j // 2 + 1))
                do = jnp.where((lane < 64) if j % 2 == 0 else (lane >= 64), do_ref[0, r, pair], 0.0)
                q = q_ref[0, r, lanes]
                heads.append((lanes, pair, q, _mx(q.astype(F32).T), _mx(do), _mx(do.T),
                              jnp.sum(do * o_ref[0, r, pair], axis=1, keepdims=True),
                              jnp.concatenate([lse_ref[0, j, r, :]] * (tk // LANE), axis=1)))
            n_full = (qi * tq) // tk

            def k_tile(kt, dqs, masked):
                kr = pl.ds(pl.multiple_of(kt * tk, tk), tk)
                out = []
                dvt = [None] * hps
                for j, (lanes, pair, q, qt, dob, dot_, dcol, lse_c) in enumerate(heads):
                    k = k_ref[0, kr, lanes]
                    s = _dot_nt(q, k)
                    if masked:
                        s = jnp.where(rows + qi * tq >= cols + kt * tk, s, NEG_INF)
                    p = jnp.exp2(s - lse_c)
                    dp = _dot_nt(dob, v_ref[0, kr, pair])
                    dsb = _mx(p * (dp - dcol))
                    dkt_ref[j, kt] += _dot(qt, dsb)
                    pv = _dot(dot_, _mx(p))
                    dvt[j // 2] = pv if dvt[j // 2] is None else dvt[j // 2] + pv
                    out.append(dqs[j] + _dot(dsb, k))
                for t in range(hps):
                    dvt_ref[t, kt] += dvt[t]
                return tuple(out)

            zero = jnp.zeros((tq, LANE), F32)
            dqs = lax.fori_loop(0, n_full, functools.partial(k_tile, masked=False), (zero,) * nh)
            dqs = k_tile(n_full, dqs, True)
            for j in range(nh):
                dq_ref[0, r, heads[j][0]] = (MLA_SCALE * dqs[j]).astype(dq_ref.dtype)
            return 0

        lax.fori_loop(0, nq, q_tile, 0)

        def flush(kt, _):
            kr = pl.ds(pl.multiple_of(kt * tk, tk), tk)
            for j in range(nh):
                dk_ref[0, kr, LANE * j:LANE * (j + 1)] = ((1.0 / LOG2E) * dkt_ref[j, kt].T).astype(dk_ref.dtype)
            for t in range(hps):
                dv_ref[0, kr, LANE * t:LANE * (t + 1)] = dvt_ref[t, kt].T.astype(dv_ref.dtype)
            return 0

        lax.fori_loop(0, nk, flush, 0)

    grp = lambda w: pl.BlockSpec((1, S, w), lambda b, hp: (b, 0, hp))
    return pl.pallas_call(
        body, name="mla_bwd", grid=(B, MLA_HEADS // nh),
        out_shape=[jax.ShapeDtypeStruct((B, S, 1024), MLA_GRAD_DTYPE), jax.ShapeDtypeStruct((B, S, 1024), MLA_GRAD_DTYPE),
                   jax.ShapeDtypeStruct((B, S, 512), MLA_GRAD_DTYPE)],
        in_specs=[grp(nh * LANE), grp(nh * LANE), grp(hps * LANE), grp(hps * LANE), grp(hps * LANE),
                  pl.BlockSpec((1, nh, S, LANE), lambda b, hp: (b, hp, 0, 0))],
        out_specs=[grp(nh * LANE), grp(nh * LANE), grp(hps * LANE)],
        scratch_shapes=[pltpu.VMEM((nh, nk, LANE, tk), F32), pltpu.VMEM((hps, nk, LANE, tk), F32)],
        compiler_params=_cparams(("arbitrary", "arbitrary")),
    )(q3, k3, v3, o3, do3, lse)


def _swa_consts(sink_ref):
    W = WINDOW
    row = lax.broadcasted_iota(jnp.int32, (4 * W, LANE), 0)
    out = []
    for g in range(SWA_KV_HEADS):
        slope = jnp.zeros((4 * W, LANE), F32)
        sink = jnp.zeros((4 * W, LANE), F32)
        for p in range(4):
            h = p + 4 * g
            here = jnp.logical_and(row >= W * p, row < W * (p + 1))
            slope = jnp.where(here, float(LOG2E * 2.0 ** (-8.0 * (h + 1) / SWA_HEADS)), slope)
            sink = jnp.where(here, LOG2E * sink_ref[:, h:h + 1], sink)
        out.append((jnp.concatenate([slope, slope], axis=1), sink))
    return out


def _wide(col):
    return jnp.concatenate([col, col], axis=1)


def _swa_block(q_ref, k_ref, v_ref, pb_ref, pr_ref, n, i, ij):
    W = WINDOW
    kb = jnp.maximum(n - 1, 0)
    r = pl.ds(pl.multiple_of(i * W, W), W)
    kr = pl.ds(pl.multiple_of(kb * W, W), 2 * W)
    q4, k2, v2 = q_ref[0, r, :], k_ref[0, kr, :], v_ref[0, kr, :]
    pq = _wide(pb_ref[0, r, :])
    pk = jnp.concatenate([pr_ref[0, pl.ds(kb, 1), :], pr_ref[0, pl.ds(kb + 1, 1), :]], axis=1)
    rel = ij + (n - kb) * W
    mask = jnp.where(jnp.logical_and(rel >= 0, rel < W), 0.0, NEG_INF)
    dist4 = jnp.concatenate([pq - pk] * 4, axis=0)
    mask4 = jnp.concatenate([mask] * 4, axis=0)
    return r, kb, kr, q4, k2, v2, dist4, mask4


def _swa_stack(x4, g, dtype):
    lane = _lane_iota((WINDOW, LANE))
    mine = (lane < 64) if g == 0 else (lane >= 64)
    return jnp.concatenate([jnp.where(mine, x4[:, LANE * p:LANE * (p + 1)], 0).astype(dtype) for p in range(4)], axis=0)


def _swa_unstack(ref, r, lo, hi, scale=None):
    W = WINDOW
    low = _lane_iota((W, LANE)) < 64
    for p in range(4):
        t = jnp.where(low, lo[W * p:W * (p + 1)], hi[W * p:W * (p + 1)])
        ref[0, r, LANE * p:LANE * (p + 1)] = t if scale is None else scale * t


def _swa_fwd(qs3, ks3, vs3, posb, posr, sinks):
    B, S, _ = qs3.shape
    W = WINDOW
    nb = S // W
    nh = SWA_SEQ_SPLIT
    nbh = nb // nh

    def body(q_ref, k_ref, v_ref, pb_ref, pr_ref, sink_ref, o_ref, lse_ref):
        ij = lax.broadcasted_iota(jnp.int32, (W, 2 * W), 0) - lax.broadcasted_iota(jnp.int32, (W, 2 * W), 1)
        consts = _swa_consts(sink_ref)
        n0 = pl.program_id(1) * nbh

        def blk(i, _):
            n = n0 + i
            r, _, _, q4, k2, v2, dist4, mask4 = _swa_block(q_ref, k_ref, v_ref, pb_ref, pr_ref, n, i, ij)
            o_g = []
            for g, (slope, sink) in enumerate(consts):
                s = _dot_nt(_swa_stack(q4, g, q4.dtype), k2) - slope * dist4 + mask4
                m = jnp.maximum(jnp.max(s, axis=1, keepdims=True), sink)
                e = jnp.exp2(s - _wide(m))
                l = jnp.sum(e, axis=1, keepdims=True) + jnp.exp2(sink - m)
                o_g.append(_dot(_mx(e), v2) * (1.0 / l))
                lse_ref[0, i, g] = m + jnp.log2(l)
            _swa_unstack(o_ref, r, o_g[0], o_g[1])
            return 0

        lax.fori_loop(0, nbh, blk, 0, unroll=2)

    seq = lambda w: pl.BlockSpec((1, S, w), lambda b, h: (b, 0, 0))
    part = lambda w: pl.BlockSpec((1, S // nh, w), lambda b, h: (b, h, 0))
    lse_spec = pl.BlockSpec((1, nbh, 2, 4 * W, LANE), lambda b, h: (b, h, 0, 0, 0))
    return pl.pallas_call(
        body, name="swa_fwd", grid=(B, nh),
        out_shape=[jax.ShapeDtypeStruct((B, S, 512), F32), jax.ShapeDtypeStruct((B, nb, 2, 4 * W, LANE), F32)],
        in_specs=[part(512), seq(LANE), seq(LANE), part(LANE), pl.BlockSpec((1, nb, W), lambda b, h: (b, 0, 0)),
                  pl.BlockSpec((1, LANE), lambda b, h: (0, 0))],
        out_specs=[part(512), lse_spec],
        compiler_params=_cparams(("arbitrary", "arbitrary")),
    )(qs3, ks3, vs3, posb, posr, sinks)


def _swa_bwd(qs3, ks3, vs3, posb, posr, sinks, os3, do3, lse):
    B, S, _ = qs3.shape
    W = WINDOW
    nb = S // W
    nh = SWA_SEQ_SPLIT
    nbh = nb // nh
    assert nbh % 2 == 0 and nbh * nh * W == S

    def body(q_ref, k_ref, v_ref, pb_ref, pr_ref, sink_ref, o_ref, do_ref, lse_ref, dq_ref, dk_ref, dv_ref, dsink_ref,
             dkt_ref, dvt_ref):
        lane1 = _lane_iota((1, LANE))
        ij = lax.broadcasted_iota(jnp.int32, (W, 2 * W), 0) - lax.broadcasted_iota(jnp.int32, (W, 2 * W), 1)
        consts = _swa_consts(sink_ref)
        hh = pl.program_id(1)
        n0 = hh * nbh

        @pl.when(hh == 0)
        def _():
            dkt_ref[...] = jnp.zeros_like(dkt_ref)
            dvt_ref[...] = jnp.zeros_like(dvt_ref)

        @pl.when(jnp.logical_and(pl.program_id(0) == 0, hh == 0))
        def _():
            dsink_ref[...] = jnp.zeros_like(dsink_ref)

        def blk(i, dsink):
            n = n0 + i
            r, kb, _, q4, k2, v2, dist4, mask4 = _swa_block(q_ref, k_ref, v_ref, pb_ref, pr_ref, n, i, ij)
            o4, do4 = o_ref[0, r, :], do_ref[0, r, :]
            dq_g = []
            dkt = jnp.zeros((LANE, 2 * W), F32)
            dvt = jnp.zeros((LANE, 2 * W), F32)
            for g, (slope, sink) in enumerate(consts):
                q_st = _swa_stack(q4, g, F32)
                do_st = _swa_stack(do4, g, F32)
                dcol = jnp.sum(do_st * _swa_stack(o4, g, F32), axis=1, keepdims=True)
                qst, dob = _mx(q_st), _mx(do_st)
                lse_c = lse_ref[0, i, g]
                pr = jnp.exp2(_dot_nt(qst, k2) - slope * dist4 + mask4 - _wide(lse_c))
                dsb = _mx(pr * (_dot_nt(dob, v2) - dcol))
                psd = jnp.exp2(sink - lse_c)[:, 0:1] * dcol
                for p in range(4):
                    dsink = dsink - jnp.where(lane1 == p + 4 * g,
                                              jnp.sum(psd[W * p:W * (p + 1)], axis=0, keepdims=True), 0.0)
                dq_g.append(_dot(dsb, k2))
                dkt = dkt + _dot(_mx(q_st.T), dsb)
                dvt = dvt + _dot(_mx(do_st.T), _mx(pr))
            _swa_unstack(dq_ref, r, dq_g[0], dq_g[1], SWA_SCALE)
            dkt_ref[kb] += dkt[:, 0:W]
            dkt_ref[kb + 1] += dkt[:, W:2 * W]
            dvt_ref[kb] += dvt[:, 0:W]
            dvt_ref[kb + 1] += dvt[:, W:2 * W]
            return dsink

        dsink_ref[...] += lax.fori_loop(0, nbh // 2, lambda t, d: blk(2 * t + 1, blk(2 * t, d)), jnp.zeros((1, LANE), F32))

        @pl.when(hh == nh - 1)
        def _():
            def flush(n, _):
                r = pl.ds(pl.multiple_of(n * W, W), W)
                dk_ref[0, r, :] = (1.0 / LOG2E) * dkt_ref[n].T
                dv_ref[0, r, :] = dvt_ref[n].T
                return 0

            lax.fori_loop(0, nb, flush, 0)

    seq = lambda w: pl.BlockSpec((1, S, w), lambda b, h: (b, 0, 0))
    part = lambda w: pl.BlockSpec((1, S // nh, w), lambda b, h: (b, h, 0))
    return pl.pallas_call(
        body, name="swa_bwd", grid=(B, nh),
        out_shape=[jax.ShapeDtypeStruct((B, S, 512), F32), jax.ShapeDtypeStruct((B, S, LANE), F32),
                   jax.ShapeDtypeStruct((B, S, LANE), F32), jax.ShapeDtypeStruct((1, LANE), F32)],
        in_specs=[part(512), seq(LANE), seq(LANE), part(LANE), pl.BlockSpec((1, nb, W), lambda b, h: (b, 0, 0)),
                  pl.BlockSpec((1, LANE), lambda b, h: (0, 0)), part(512),
                  pl.BlockSpec((1, S // nh, 512), lambda b, h: (b, h, 1)),
                  pl.BlockSpec((1, nbh, 2, 4 * W, LANE), lambda b, h: (b, h, 0, 0, 0))],
        out_specs=[part(512), seq(LANE), seq(LANE), pl.BlockSpec((1, LANE), lambda b, h: (0, 0))],
        scratch_shapes=[pltpu.VMEM((nb, LANE, W), F32), pltpu.VMEM((nb, LANE, W), F32)],
        compiler_params=_cparams(("arbitrary", "arbitrary")),
    )(qs3, ks3, vs3, posb, posr, sinks, os3, do3, lse)


def _post(om, osw, g, w_out, x2, gate, fg, tgt, S, ts):
    T, D = x2.shape
    nsb = S // ts

    def body(om_ref, os_ref, g_ref, w_ref, x_ref, gate_ref, fg_ref, t_ref,
             dx2_ref, do_ref, dg_ref, loss_ref, dfg_ref, dgate_ref, dw_ref):
        i = pl.program_id(0)
        gv = g_ref[...]
        sg = 1.0 / (1.0 + jnp.exp(-gv))
        silu = gv * sg
        o = jnp.concatenate([om_ref[...], os_ref[...]], axis=1)
        ab = _mx(o * silu)
        y = _dot(ab, w_ref[...])
        gate = gate_ref[0]
        xo = x_ref[...] + gate * y
        r2 = lax.rsqrt(jnp.mean(xo * xo, axis=-1, keepdims=True) + EPS)
        xh = xo * r2
        fg = fg_ref[...]
        diff = xh * fg - t_ref[...]
        sq = jnp.sum(diff * diff, axis=0, keepdims=True)
        part = sq[:, 0:LANE]
        for t in range(1, D // LANE):
            part = part + sq[:, LANE * t:LANE * (t + 1)]
        dout = diff * (1.0 / D)
        dxh = dout * fg
        dx2 = r2 * (dxh - xh * jnp.mean(dxh * xh, axis=-1, keepdims=True))
        dx2_ref[...] = dx2
        dyb = _mx(dx2 * gate)
        da = _dot_nt(dyb, w_ref[...])
        do_ref[...] = da * silu
        dg_ref[...] = _mx(da * o * (sg * (1.0 + gv * (1.0 - sg))))

        @pl.when(i == 0)
        def _():
            loss_ref[...] = jnp.zeros_like(loss_ref)
            dfg_ref[...] = jnp.zeros_like(dfg_ref)
            dw_ref[...] = jnp.zeros_like(dw_ref)

        @pl.when(i % nsb == 0)
        def _():
            dgate_ref[...] = jnp.zeros_like(dgate_ref)

        loss_ref[...] += (0.5 / D) * part
        dfg_ref[...] += jnp.sum(dout * xh, axis=0, keepdims=True)
        dgate_ref[0] += jnp.sum(dx2 * y, axis=0, keepdims=True)
        dw_ref[...] += _dot_tn(ab, dyb)

        @pl.when(i == T // ts - 1)
        def _():
            chunks = [dw_ref[512 + 64 * PAIR_INV[h]:512 + 64 * (PAIR_INV[h] + 1), :] for h in range(SWA_HEADS)]
            for h in range(SWA_HEADS):
                dw_ref[512 + 64 * h:512 + 64 * (h + 1), :] = chunks[h]

    row = lambda w: pl.BlockSpec((ts, w), lambda i: (i, 0))
    full = lambda a: pl.BlockSpec(a.shape, lambda i: (0,) * a.ndim)
    per_b = pl.BlockSpec((1, 1, D), lambda i: (i // nsb, 0, 0))
    return pl.pallas_call(
        body, name="post", grid=(T // ts,),
        out_shape=[jax.ShapeDtypeStruct((T, D), F32), jax.ShapeDtypeStruct((T, 1024), F32),
                   jax.ShapeDtypeStruct((T, 1024), _MXU_DTYPE), jax.ShapeDtypeStruct((1, LANE), F32),
                   jax.ShapeDtypeStruct((1, D), F32), jax.ShapeDtypeStruct(gate.shape, F32),
                   jax.ShapeDtypeStruct(w_out.shape, F32)],
        in_specs=[row(512), row(512), row(1024), full(w_out), row(D), per_b, full(fg), row(D)],
        out_specs=[row(D), row(1024), row(1024),
                   pl.BlockSpec((1, LANE), lambda i: (0, 0)), pl.BlockSpec((1, D), lambda i: (0, 0)), per_b,
                   full(w_out)],
        compiler_params=_cparams(("arbitrary",)),
    )(om, osw, g, w_out, x2, gate, fg, tgt)


def _pre_bwd(dq, dk, dv, dqs, dks, dvs, dg, zq, zkv, ql, kvl, x2, dx2, scale, ng, w_in, gq, gkv, w_uq, w_uk, w_uv,
             rope, S, ts):
    T, D = x2.shape
    nsb = S // ts
    WQ = MLA_HEADS * LANE

    def body(dq_ref, dk_ref, dv_ref, dqs_ref, dks_ref, dvs_ref, dg_ref, zq_ref, zkv_ref, ql_ref, kvl_ref, x_ref, dx2_ref,
             sc_ref,
             ng_ref, w_ref, gq_ref, gkv_ref, wuq_ref, wuk_ref, wuv_ref, cd_ref, sd_ref,
             gx_ref, dz_ref, dgq_ref, dgkv_ref, dng_ref, dsh_ref, dsc_ref, dwuq_ref, dwuk_ref, dwuv_ref):
        i = pl.program_id(0)

        @pl.when(i == 0)
        def _():
            dgq_ref[...] = jnp.zeros_like(dgq_ref)
            dgkv_ref[...] = jnp.zeros_like(dgkv_ref)
            dng_ref[...] = jnp.zeros_like(dng_ref)
            dwuq_ref[...] = jnp.zeros_like(dwuq_ref)
            dwuk_ref[...] = jnp.zeros_like(dwuk_ref)
            dwuv_ref[...] = jnp.zeros_like(dwuv_ref)

        @pl.when(i % nsb == 0)
        def _():
            dsh_ref[...] = jnp.zeros_like(dsh_ref)
            dsc_ref[...] = jnp.zeros_like(dsc_ref)

        def norm_bwd(z, dy, gain):
            r = lax.rsqrt(jnp.mean(z * z, axis=-1, keepdims=True) + EPS)
            zh = z * r
            dzh = dy * gain
            return r * (dzh - zh * jnp.mean(dzh * zh, axis=-1, keepdims=True)), jnp.sum(dy * zh, axis=0, keepdims=True)

        tables = _rope_split(cd_ref[...], sd_ref[...])
        ng = ng_ref[...]
        sc1 = 1.0 + sc_ref[0]

        def rows_chain(rs):
            cq, ck, sa, sb = (t[rs] for t in tables)
            dqr = dq_ref[rs, :].astype(F32)
            dqb = _mx(dqr * _tile_heads(cq) + pltpu.roll(dqr * _tile_heads(sa), 16, 1)
                      + pltpu.roll(dqr * _tile_heads(sb), WQ - 16, 1))
            p_uq = _dot_tn(ql_ref[rs, :], dqb)
            dzq, dgq = norm_bwd(zq_ref[rs, :], _dot_nt(dqb, wuq_ref[...]), gq_ref[...])
            dkr = dk_ref[rs, :].astype(F32)
            dkb = _mx(dkr)
            p_uk = _dot_tn(kvl_ref[rs, :], dkb)
            dvb = _mx(dv_ref[rs, :])
            p_uv = _dot_tn(kvl_ref[rs, :], dvb)
            dzkv, dgkv = norm_bwd(zkv_ref[rs, :], _dot_nt(dkb, wuk_ref[...]) + _dot_nt(dvb, wuv_ref[...]), gkv_ref[...])
            dkpe = dkr[:, 0:LANE]
            for h in range(1, MLA_HEADS):
                dkpe = dkpe + dkr[:, LANE * h:LANE * (h + 1)]
            dkro = dkpe * ck + pltpu.roll(dkpe * sa, 16, 1) + pltpu.roll(dkpe * sb, LANE - 16, 1)
            dz_ref[rs, 0:384] = _mx(dzq)
            dz_ref[rs, 384:640] = _mx(dzkv)
            dz_ref[rs, 640:768] = _mx(dkro)
            dz_ref[rs, 768:1280] = dg_ref[rs, 0:512]
            dz_ref[rs, 1280:1792] = _mx(dqs_ref[rs, :])
            dz_ref[rs, 1792:1920] = _mx(dks_ref[rs, :])
            dz_ref[rs, 1920:2048] = _mx(dvs_ref[rs, :])
            dz_ref[rs, 2048:2560] = dg_ref[rs, 512:1024]
            dh = _dot(dz_ref[rs, :], w_ref[...])
            x = x_ref[rs, :]
            r1 = lax.rsqrt(jnp.mean(x * x, axis=-1, keepdims=True) + EPS)
            xn = x * r1
            dxn = dh * ng * sc1
            gx_ref[rs, :] = dx2_ref[rs, :] + r1 * (dxn - xn * jnp.mean(dxn * xn, axis=-1, keepdims=True))
            return (p_uq, p_uk, p_uv, dgq, dgkv, jnp.sum(dh, axis=0, keepdims=True),
                    jnp.sum(dh * (xn * ng), axis=0, keepdims=True), jnp.sum(dh * xn * sc1, axis=0, keepdims=True))

        hr = ts // 2
        parts = [rows_chain(slice(hr * t, hr * (t + 1))) for t in range(2)]
        p_uq, p_uk, p_uv, dgq, dgkv, dsh, dsc, dng = (a + b for a, b in zip(*parts))
        dwuq_ref[...] += p_uq
        dwuk_ref[...] += p_uk
        dwuv_ref[...] += p_uv
        dgq_ref[...] += dgq
        dgkv_ref[...] += dgkv
        dsh_ref[0] += dsh
        dsc_ref[0] += dsc
        dng_ref[...] += dng

    row = lambda w: pl.BlockSpec((ts, w), lambda i: (i, 0))
    full = lambda a: pl.BlockSpec(a.shape, lambda i: (0,) * a.ndim, pipeline_mode=pl.Buffered(1))
    per_b = pl.BlockSpec((1, 1, D), lambda i: (i // nsb, 0, 0))
    dense = pl.BlockSpec((ts // 8, LANE), lambda i: (i, 0))
    vec = lambda w: pl.BlockSpec((1, w), lambda i: (0, 0))
    return pl.pallas_call(
        body, name="pre_bwd", grid=(T // ts,),
        out_shape=[jax.ShapeDtypeStruct((T, D), F32), jax.ShapeDtypeStruct((T, D_IN_PAD), _MXU_DTYPE), jax.ShapeDtypeStruct((1, 384), F32),
                   jax.ShapeDtypeStruct((1, 256), F32), jax.ShapeDtypeStruct((1, D), F32),
                   jax.ShapeDtypeStruct(scale.shape, F32), jax.ShapeDtypeStruct(scale.shape, F32),
                   jax.ShapeDtypeStruct(w_uq.shape, F32), jax.ShapeDtypeStruct(w_uk.shape, F32),
                   jax.ShapeDtypeStruct(w_uv.shape, F32)],
        in_specs=[row(WQ), row(WQ), row(512), row(512), row(LANE), row(LANE), row(1024), row(384), row(256), row(384),
                  row(256), row(D), row(D), per_b, full(ng), full(w_in), full(gq), full(gkv), full(w_uq), full(w_uk), full(w_uv),
                  dense, dense],
        out_specs=[row(D), row(D_IN_PAD), vec(384), vec(256), vec(D), per_b, per_b, full(w_uq), full(w_uk), full(w_uv)],
        compiler_params=_cparams(("arbitrary",)),
    )(dq, dk, dv, dqs, dks, dvs, dg, zq, zkv, ql, kvl, x2, dx2, scale, ng, w_in, gq, gkv, w_uq, w_uk, w_uv, *rope)


def _w_in_row_runs():
    runs = [(0, 0, 640), (640, 704, 32), (672, 768, 512)]
    runs += [(1184 + 64 * h, 1280 + 64 * PAIR_INV[h], 64) for h in range(8)]
    runs += [(1696, 1792, 256)]
    runs += [(1952 + 64 * h, 2048 + 64 * PAIR_INV[h], 64) for h in range(8)]
    return runs


def _dw_in_t(dz, hb, tn, tk):
    T, M = dz.shape
    N = hb.shape[1]
    nk = T // tk

    def body(a_ref, b_ref, o_ref, acc_ref):
        k = pl.program_id(1)

        @pl.when(k == 0)
        def _():
            acc_ref[...] = jnp.zeros_like(acc_ref)

        acc_ref[...] += _dot_tn(a_ref[...], b_ref[...])

        @pl.when(k == nk - 1)
        def _():
            for nat, pad, size in _w_in_row_runs():
                o_ref[nat:nat + size, :] = acc_ref[pad:pad + size, :]

    return pl.pallas_call(
        body, name="dw_in", grid=(N // tn, nk),
        out_shape=jax.ShapeDtypeStruct((D_IN, N), F32),
        in_specs=[pl.BlockSpec((tk, M), lambda j, k: (k, 0)), pl.BlockSpec((tk, tn), lambda j, k: (k, j))],
        out_specs=pl.BlockSpec((D_IN, tn), lambda j, k: (0, j)),
        scratch_shapes=[pltpu.VMEM((M, tn), F32)],
        compiler_params=_cparams(("arbitrary", "arbitrary")),
    )(dz, hb)


def _finalize(parts_all, dmod_all, dmod_cols, c_act_all):
    nparts = parts_all.shape[-1]
    nmod = dmod_all.shape[-1]

    def body(p_ref, dm_ref, dmc_ref, c_ref, ps_ref, loss_ref, db_ref, dw_ref):
        acc = p_ref[0]
        for j in range(1, N_DEV):
            acc = acc + p_ref[j]
        ps_ref[...] = acc
        loss_ref[...] = jnp.sum(acc[:, 0:LANE], axis=1, keepdims=True)
        db = dm_ref[0:1, :]
        for j in range(1, dm_ref.shape[0]):
            db = db + dm_ref[j:j + 1, :]
        db_ref[...] = db
        dw_ref[...] = _dot_tn(_mx(_silu(c_ref[...])), _mx(dmc_ref[...]))

    return pl.pallas_call(
        body, name="finalize",
        out_shape=[jax.ShapeDtypeStruct((1, nparts), F32), jax.ShapeDtypeStruct((1, 1), F32),
                   jax.ShapeDtypeStruct((1, nmod), F32),
                   jax.ShapeDtypeStruct((c_act_all.shape[1], dmod_cols.shape[1]), F32)],
        in_specs=[_vmem()] * 4, out_specs=[_vmem()] * 4,
        compiler_params=_cparams(),
    )(parts_all, dmod_all, dmod_cols, c_act_all)


def _adamw(ws, gs, ms, vs):
    n = len(ws)

    def body(*refs):
        w_refs, g_refs, m_refs, v_refs, d_refs, nm_refs, nv_refs = (refs[t * n:(t + 1) * n] for t in range(7))
        for t in range(n):
            gv = g_refs[t][...]
            nm = ADAM_B1 * m_refs[t][...] + (1.0 - ADAM_B1) * gv
            nv = ADAM_B2 * v_refs[t][...] + (1.0 - ADAM_B2) * (gv * gv)
            m_hat = nm / (1.0 - ADAM_B1 ** ADAM_STEP)
            v_hat = nv / (1.0 - ADAM_B2 ** ADAM_STEP)
            d_refs[t][...] = -ADAM_LR * (m_hat / (jnp.sqrt(v_hat) + ADAM_EPS) + ADAM_WD * w_refs[t][...])
            nm_refs[t][...] = nm
            nv_refs[t][...] = nv

    out = pl.pallas_call(
        body, name="adamw",
        out_shape=[jax.ShapeDtypeStruct(w.shape, F32) for w in ws] * 3,
        in_specs=[_vmem()] * (4 * n), out_specs=[_vmem()] * (3 * n),
        compiler_params=_cparams(),
    )(*ws, *gs, *ms, *vs)
    return out[:n], out[n:2 * n], out[2 * n:]


def _pair_perm(a, axis, order):
    a = jnp.moveaxis(a, axis, -1)
    lead = a.shape[:-1]
    a = a.reshape(lead + (8, 64))[..., list(order), :].reshape(lead + (512,))
    return jnp.moveaxis(a, -1, axis)


def _rope_table(positions):
    T = positions.size
    inv = ROPE_THETA ** (-jnp.arange(0, MLA_ROPE, 2, dtype=F32) / MLA_ROPE)
    pos = jnp.repeat(positions.reshape(T // 8, 8)[:, ::-1].astype(F32), MLA_ROPE // 2, axis=1)
    ang = pos * jnp.tile(inv, 8)[None, :]
    return jnp.cos(ang), jnp.sin(ang)


def _local_step(x, mod, positions, ng, w_in_t, gq, gkv, w_uq, w_ukv, sinks, w_out, fg, tgt,
                ts=512, fq=512, fk=512, bq=512, bk=512):
    B, S, D = x.shape
    T = B * S
    x2 = x.reshape(T, D)
    shift, scale, gate = (mod[:, None, k * D:(k + 1) * D] for k in range(3))
    w_uq_p = jnp.pad(w_uq.reshape(Q_LORA, MLA_HEADS, 96), ((0, 0), (0, 0), (0, 32))).reshape(Q_LORA, MLA_HEADS * LANE)
    w_ukv3 = w_ukv.reshape(KV_LORA, MLA_HEADS, 128)
    w_uk_p = jnp.pad(w_ukv3[:, :, :64], ((0, 0), (0, 0), (0, 64))).reshape(KV_LORA, MLA_HEADS * LANE)
    w_uv = w_ukv3[:, :, 64:].reshape(KV_LORA, 512)
    w_out_p = jnp.concatenate([w_out[:512], _pair_perm(w_out[512:], 0, PAIR_ORDER)], axis=0)
    rope = _rope_table(positions)
    posf = positions.astype(F32)
    posb = jnp.broadcast_to(posf[:, :, None], (B, S, LANE))
    posr = posf.reshape(B, S // WINDOW, WINDOW)
    sinks_l = jnp.pad(sinks.reshape(1, SWA_HEADS), ((0, 0), (0, LANE - SWA_HEADS)))

    (hb, zq, zkv, ql, kvl, q, k, v, qs, ks, vs, g, w_in_p) = _pre_fwd(
        x2, shift, scale, ng, w_in_t, gq, gkv, w_uq_p, w_uk_p, w_uv, rope, S, ts)
    r3 = lambda a: a.reshape(B, S, a.shape[-1])
    om, lse_m = _mla_fwd(r3(q), r3(k), r3(v), fq, fk)
    osw, lse_s = _swa_fwd(r3(qs), r3(ks), r3(vs), posb, posr, sinks_l)
    dx2, do, dg, loss_v, dfg, dgate, dw_out = _post(
        om.reshape(T, 512), osw.reshape(T, 512), g, w_out_p, x2, gate, fg.reshape(1, D), tgt.reshape(T, D), S, ts)
    do3 = r3(do)
    dq, dk, dv = _mla_bwd(r3(q), r3(k), r3(v), om, do3, lse_m, bq, bk)
    dqs, dks, dvs, dsink = _swa_bwd(r3(qs), r3(ks), r3(vs), posb, posr, sinks_l, osw, do3, lse_s)
    f2 = lambda a: a.reshape(T, a.shape[-1])
    gx, dz, dgq, dgkv, dng, dsh, dsc, dw_uq_p, dw_uk_p, dw_uv = _pre_bwd(
        f2(dq), f2(dk), f2(dv), f2(dqs), f2(dks), f2(dvs), dg, zq, zkv, ql, kvl, x2, dx2, scale, ng, w_in_p, gq, gkv,
        w_uq_p, w_uk_p, w_uv, rope, S, ts)
    tk = min(T, 1024)
    dw_in_t = _dw_in_t(dz, hb, 512, tk)
    dw_uq = dw_uq_p.reshape(Q_LORA, MLA_HEADS, LANE)[:, :, :96].reshape(Q_LORA, 768)
    dw_uk = dw_uk_p.reshape(KV_LORA, MLA_HEADS, LANE)[:, :, :64]
    dw_uv = dw_uv.reshape(KV_LORA, MLA_HEADS, 64)
    dw_ukv = jnp.concatenate([dw_uk, dw_uv], axis=2).reshape(KV_LORA, 1024)
    parts = jnp.concatenate([loss_v, dfg, dng, dgq, dgkv, dsink], axis=1)
    dmod = jnp.concatenate([dsh, dsc, dgate], axis=2).reshape(B, 3 * D)
    return gx.reshape(B, S, D), dw_in_t, dw_uq, dw_ukv, dw_out, parts, dmod


def kernel(x, c, positions, w_ada, b_ada, norm_gain, w_in, q_norm_gain, kv_norm_gain, w_uq, w_ukv, swa_sinks, w_out, final_gain, loss_target, m_w_ada, m_b_ada, m_norm_gain, m_w_in, m_q_norm_gain, m_kv_norm_gain, m_w_uq, m_w_ukv, m_swa_sinks, m_w_out, m_final_gain, v_w_ada, v_b_ada, v_norm_gain, v_w_in, v_q_norm_gain, v_kv_norm_gain, v_w_uq, v_w_ukv, v_swa_sinks, v_w_out, v_final_gain):
    B, S, D = x.shape
    me = 4 * lax.axis_index("x") + 2 * lax.axis_index("y") + lax.axis_index("c")
    bf = _MXU_DTYPE

    ncol = w_ada.shape[2]
    b_cols = lax.dynamic_slice_in_dim(b_ada, me * ncol, ncol, axis=1)
    c_all, win_g, wuq_g, wukv_g, wout_g, mod_g = _all_gather(
        [c, w_in[0].T.astype(bf), w_uq[0].astype(bf), w_ukv[0].astype(bf), w_out[0].astype(bf)], "ag_weights",
        fused=((w_ada[0], b_cols), _ada_cols, jax.ShapeDtypeStruct((N_DEV, B, ncol), F32)))
    c_all = c_all.reshape(N_DEV * B, D)
    cat_cols = lambda a: jnp.transpose(a, (1, 0, 2)).reshape(a.shape[1], N_DEV * a.shape[2])
    w_in_t, w_uq_f, w_ukv_f = win_g.reshape(D_IN, D), cat_cols(wuq_g), cat_cols(wukv_g)
    w_out_f = wout_g.reshape(D, D)

    mod = lax.dynamic_index_in_dim(mod_g, me, axis=1, keepdims=False)
    mod = jnp.transpose(mod, (1, 0, 2)).reshape(B, 3 * D)

    gx, dw_in_t, dw_uq, dw_ukv, dw_out, parts, dmod = _local_step(
        x, mod, positions, norm_gain, w_in_t, q_norm_gain, kv_norm_gain, w_uq_f, w_ukv_f, swa_sinks,
        w_out_f, final_gain, loss_target)

    split_cols = lambda a: jnp.transpose(a.reshape(a.shape[0], 4, 2, a.shape[1] // N_DEV), (1, 2, 0, 3))
    split_rows = lambda a: a.reshape(4, 2, a.shape[0] // N_DEV, a.shape[1])
    g_w_in_t, g_w_uq, g_w_ukv, g_w_out, parts_g, dmod_g = _reduce_scatter(
        [split_rows(dw_in_t), split_cols(dw_uq), split_cols(dw_ukv), split_rows(dw_out)], "rs_grads",
        gather=(parts, dmod))
    g_w_in = g_w_in_t.T

    dmod_all = dmod_g.reshape(N_DEV * B, 3 * D)
    dmod_cols = lax.dynamic_slice_in_dim(dmod_all, me * ncol, ncol, axis=1)
    psum, loss, g_b_ada, g_w_ada = _finalize(parts_g, dmod_all, dmod_cols, c_all)
    loss = loss.reshape(())
    o = LANE
    g_final_gain = psum[0, o:o + D]
    g_norm_gain = psum[:, o + D:o + 2 * D]
    o += 2 * D
    g_q_norm_gain = psum[:, o:o + Q_LORA]
    g_kv_norm_gain = psum[:, o + Q_LORA:o + Q_LORA + KV_LORA]
    o += Q_LORA + KV_LORA
    g_sinks = psum[:, o:o + SWA_HEADS]

    grads = [g_w_ada, g_b_ada, g_norm_gain, g_w_in, g_q_norm_gain, g_kv_norm_gain, g_w_uq, g_w_ukv, g_sinks, g_w_out,
             g_final_gain]
    ws = [w_ada, b_ada, norm_gain, w_in, q_norm_gain, kv_norm_gain, w_uq, w_ukv, swa_sinks, w_out, final_gain]
    ms = [m_w_ada, m_b_ada, m_norm_gain, m_w_in, m_q_norm_gain, m_kv_norm_gain, m_w_uq, m_w_ukv, m_swa_sinks, m_w_out,
          m_final_gain]
    vs = [v_w_ada, v_b_ada, v_norm_gain, v_w_in, v_q_norm_gain, v_kv_norm_gain, v_w_uq, v_w_ukv, v_swa_sinks, v_w_out,
          v_final_gain]
    two_d = [(1, w.shape[0]) if w.ndim == 1 else w.shape[-2:] for w in ws]
    flat = lambda arrs: [a.reshape(s) for a, s in zip(arrs, two_d)]
    deltas, new_ms, new_vs = _adamw(flat(ws), flat(grads), flat(ms), flat(vs))
    shaped = lambda arrs: [a.reshape(w.shape) for a, w in zip(arrs, ws)]
    return (loss, gx, *shaped(grads), *shaped(deltas), *shaped(new_ms), *shaped(new_vs))
```

```python
import functools

import jax
import jax.numpy as jnp
from jax import lax
from jax.experimental import pallas as pl
from jax.experimental.pallas import tpu as pltpu

F32 = jnp.float32
_MXU_DTYPE = jnp.bfloat16
MLA_GRAD_DTYPE = jnp.bfloat16

N_DEV = 8
MLA_HEADS = 8
MLA_NOPE = 64
MLA_ROPE = 32
Q_LORA = 384
KV_LORA = 256
SWA_HEADS = 8
SWA_KV_HEADS = 2
SWA_HEAD_DIM = 64
WINDOW = 128
ROPE_THETA = 10000.0
EPS = 1e-6
MLA_SCALE = float((MLA_NOPE + MLA_ROPE) ** -0.5)
SWA_SCALE = float(SWA_HEAD_DIM ** -0.5)
LOG2E = 1.4426950408889634
MLA_QSCALE = MLA_SCALE * LOG2E
D_IN = 2464
D_IN_PAD = 2560
PAIR_ORDER = (0, 4, 1, 5, 2, 6, 3, 7)
PAIR_INV = (0, 2, 4, 6, 1, 3, 5, 7)

ADAM_LR = 0.001
ADAM_B1 = 0.9
ADAM_B2 = 0.999
ADAM_EPS = 1e-08
ADAM_WD = 0.01
ADAM_STEP = 10

LANE = 128
VMEM_LIMIT = 56 * 1024 * 1024

MESH = pl.DeviceIdType.MESH
NEG_INF = float("-inf")
SWA_SEQ_SPLIT = 2


def _mx(a):
    return a.astype(_MXU_DTYPE)


def _dot(a, b):
    return jnp.dot(a, b, preferred_element_type=F32)


def _dot_nt(a, b):
    return lax.dot_general(a, b, (((1,), (1,)), ((), ())), preferred_element_type=F32)


def _dot_tn(a, b):
    return lax.dot_general(a, b, (((0,), (0,)), ((), ())), preferred_element_type=F32)


def _cparams(sem=None):
    return pltpu.CompilerParams(dimension_semantics=sem, vmem_limit_bytes=VMEM_LIMIT)


def _vmem():
    return pl.BlockSpec(memory_space=pltpu.VMEM)


def _lane_iota(shape):
    return lax.broadcasted_iota(jnp.int32, shape, len(shape) - 1)


def _gather_program(srcs, outs, send_sems, recv_sems, local_sems):
    x, y, c = lax.axis_index("x"), lax.axis_index("y"), lax.axis_index("c")
    me, sibling = (x, y, c), (x, y, 1 - c)
    chips = [(1 - x, y), (x, 1 - y), (1 - x, 1 - y)]

    def slot(a, dev):
        return outs[a].at[4 * dev[0] + 2 * dev[1] + dev[2]]

    def copy(a, k, block, to, src=None):
        return pltpu.make_async_remote_copy(
            src_ref=slot(a, block) if src is None else src, dst_ref=slot(a, block),
            send_sem=send_sems.at[7 * a + k], recv_sem=recv_sems.at[7 * a + k],
            device_id=to, device_id_type=MESH)

    def local(a):
        return pltpu.make_async_copy(srcs[a], slot(a, me), local_sems.at[a])

    def start(a):
        local(a).start()
        cps = [copy(a, 0, me, sibling, src=srcs[a])]
        cps += [copy(a, 1 + j, me, (*chip, c), src=srcs[a]) for j, chip in enumerate(chips)]
        for cp in cps:
            cp.start()
        return cps

    def finish(group):
        cps = []
        for j, chip in enumerate(chips):
            for a in group:
                copy(a, 1 + j, (*chip, c), me).wait_recv()
                cp = copy(a, 4 + j, (*chip, c), sibling)
                cp.start()
                cps.append(cp)
        for a in group:
            copy(a, 0, sibling, me).wait_recv()
            for j, chip in enumerate(chips):
                copy(a, 4 + j, (*chip, 1 - c), me).wait_recv()
            local(a).wait()
        return cps

    return start, finish


def _gather_sems(m):
    return [pltpu.SemaphoreType.DMA((7 * m,)), pltpu.SemaphoreType.DMA((7 * m,)), pltpu.SemaphoreType.DMA((m,))]


def _all_gather(arrs, name, fused=None):
    n = len(arrs)
    extra, fn, piece = fused if fused else ((), None, None)
    ne, m = len(extra), n + (1 if fused else 0)

    def body(*refs):
        ins, ex, outs = refs[:n], refs[n:n + ne], refs[n + ne:n + ne + m]
        rest = refs[n + ne + m:]
        srcs = list(ins) + ([rest[0]] if fused else [])
        start, finish = _gather_program(srcs, outs, *rest[-3:])
        pending = []
        for a in range(n):
            pending += start(a)
        if fused:
            pending += finish([0])
            fn(srcs[n], outs[0], *ex)
            pending += start(n)
            pending += finish([n] + list(range(1, n)))
        else:
            pending += finish(list(range(n)))
        for cp in pending:
            cp.wait_send()

    out_shape = [jax.ShapeDtypeStruct((N_DEV,) + a.shape, a.dtype) for a in arrs]
    scratch = []
    if fused:
        out_shape.append(jax.ShapeDtypeStruct((N_DEV,) + piece.shape, piece.dtype))
        scratch.append(pltpu.VMEM(piece.shape, piece.dtype))
    return pl.pallas_call(
        body, name=name, out_shape=out_shape,
        in_specs=[_vmem()] * (n + ne), out_specs=[_vmem()] * m,
        scratch_shapes=scratch + _gather_sems(m),
        compiler_params=pltpu.CompilerParams(vmem_limit_bytes=VMEM_LIMIT),
    )(*arrs, *extra)


def _reduce_scatter(arrs, name, gather=()):
    n, g = len(arrs), len(gather)
    halves = [a.astype(jnp.bfloat16) for a in arrs]

    def body(*refs):
        xs, xbs, gins = refs[:n], refs[n:2 * n], refs[2 * n:2 * n + g]
        outs, gouts = refs[2 * n + g:3 * n + g], refs[3 * n + g:3 * n + 2 * g]
        rest = refs[3 * n + 2 * g:]
        parts, recv_a, send_b, recv_b = (rest[t * n:(t + 1) * n] for t in range(4))
        send_sems, recv_sems, local_sems = rest[4 * n:4 * n + 3]
        x, y, c = lax.axis_index("x"), lax.axis_index("y"), lax.axis_index("c")
        myq = 2 * x + y
        gather_start, gather_finish = _gather_program(gins, gouts, *rest[4 * n + 3:])

        def chip(k):
            return (1 - x if k & 2 else x, 1 - y if k & 1 else y)

        def from_sibling(a):
            return pltpu.make_async_remote_copy(
                src_ref=xbs[a].at[:, 1 - c], dst_ref=recv_a[a], send_sem=send_sems.at[4 * a], recv_sem=recv_sems.at[4 * a],
                device_id=(x, y, 1 - c), device_id_type=MESH)

        def to_owner(a, k):
            qx, qy = chip(k)
            return pltpu.make_async_remote_copy(
                src_ref=send_b[a].at[2 * qx + qy], dst_ref=recv_b[a].at[myq],
                send_sem=send_sems.at[4 * a + k], recv_sem=recv_sems.at[4 * a + k],
                device_id=(qx, qy, c), device_id_type=MESH)

        mine = [pltpu.make_async_copy(xs[a].at[:, c], parts[a], local_sems.at[a]) for a in range(n)]
        first = [from_sibling(a) for a in range(n)]
        for cp in mine + first:
            cp.start()
        second = []
        for b in range(g):
            second += gather_start(b)
        for a in range(n):
            mine[a].wait()
            first[a].wait_recv()
            parts[a][...] = parts[a][...] + recv_a[a][...].astype(F32)
            send_b[a][...] = parts[a][...].astype(jnp.bfloat16)
            for k in range(1, 4):
                cp = to_owner(a, k)
                cp.start()
                second.append(cp)
        second += gather_finish(list(range(g)))
        for a in range(n):
            acc = parts[a][myq]
            for k in range(1, 4):
                to_owner(a, k).wait_recv()
                qx, qy = chip(k)
                acc = acc + recv_b[a][2 * qx + qy].astype(F32)
            outs[a][...] = acc
        for cp in first + second:
            cp.wait_send()

    quarter = lambda a, dt: pltpu.VMEM((4,) + a.shape[2:], dt)
    return pl.pallas_call(
        body, name=name,
        out_shape=[jax.ShapeDtypeStruct(a.shape[2:], F32) for a in arrs]
        + [jax.ShapeDtypeStruct((N_DEV,) + a.shape, a.dtype) for a in gather],
        in_specs=[pl.BlockSpec(memory_space=pl.ANY)] * (2 * n) + [_vmem()] * g, out_specs=[_vmem()] * (n + g),
        scratch_shapes=[quarter(a, F32) for a in arrs] + [quarter(a, jnp.bfloat16) for a in arrs] * 3
        + [pltpu.SemaphoreType.DMA((4 * n,)), pltpu.SemaphoreType.DMA((4 * n,)), pltpu.SemaphoreType.DMA((n,))]
        + _gather_sems(g),
        compiler_params=pltpu.CompilerParams(vmem_limit_bytes=VMEM_LIMIT),
    )(*arrs, *halves, *gather)


def _silu(t):
    return t * (1.0 / (1.0 + jnp.exp(-t)))


def _ada_cols(piece_ref, c_all_ref, w_ref, b_ref):
    w = _mx(w_ref[...])
    for d in range(N_DEV):
        piece_ref[d] = _dot(_mx(_silu(c_all_ref[d])), w) + b_ref[...]


def _rope_split(cd, sd):
    n = cd.shape[0]

    def expand(d):
        rep = jnp.broadcast_to(d[:, None, :], (n, 8, LANE)).reshape(8 * n, LANE)
        return pltpu.roll(rep, 0, 1, stride=16, stride_axis=0)

    c, s = expand(cd), expand(sd)
    lane = _lane_iota(c.shape)
    first = jnp.logical_and(lane >= 64, lane < 80)
    second = jnp.logical_and(lane >= 80, lane < 96)
    ck = jnp.where(first, pltpu.roll(c, 80, 1), jnp.where(second, pltpu.roll(c, 96, 1), 0.0))
    cq = jnp.where(lane < 64, 1.0, ck)
    sa = jnp.where(first, -pltpu.roll(s, 80, 1), 0.0)
    sb = jnp.where(second, pltpu.roll(s, 96, 1), 0.0)
    return cq, ck, sa, sb


def _tile_heads(t):
    return jnp.concatenate([t] * MLA_HEADS, axis=1)


def _pre_fwd(x2, shift, scale, ng, w_in, gq, gkv, w_uq, w_uk, w_uv, rope, S, ts):
    T, D = x2.shape
    nsb = S // ts
    WQ = MLA_HEADS * LANE

    def body(x_ref, sh_ref, sc_ref, ng_ref, wn_ref, gq_ref, gkv_ref, wuq_ref, wuk_ref, wuv_ref,
             cd_ref, sd_ref,
             hb_ref, zq_ref, zkv_ref, ql_ref, kvl_ref, q_ref, k_ref, v_ref, qs_ref, ks_ref, vs_ref, g_ref, wp_ref):
        @pl.when(pl.program_id(0) == 0)
        def _():
            wp_ref[640:704, :] = jnp.zeros((64, D), wp_ref.dtype)
            wp_ref[736:768, :] = jnp.zeros((32, D), wp_ref.dtype)
            for nat, pad, size in _w_in_row_runs():
                wp_ref[pad:pad + size, :] = wn_ref[nat:nat + size, :]

        w_ref = wp_ref
        x = x_ref[...]
        r1 = lax.rsqrt(jnp.mean(x * x, axis=-1, keepdims=True) + EPS)
        h = ((x * r1) * ng_ref[...]) * (1.0 + sc_ref[0]) + sh_ref[0]
        hb = _mx(h)
        hb_ref[...] = hb
        zq = _dot_nt(hb, w_ref[0:384, :])
        zq_ref[...] = zq
        rq = lax.rsqrt(jnp.mean(zq * zq, axis=-1, keepdims=True) + EPS)
        ql = _mx((zq * rq) * gq_ref[...])
        ql_ref[...] = ql
        q = _dot(ql, wuq_ref[...])
        cq, ck, sa, sb = _rope_split(cd_ref[...], sd_ref[...])
        q = (q * _tile_heads(cq) + pltpu.roll(q, WQ - 16, 1) * _tile_heads(sa)
             + pltpu.roll(q, 16, 1) * _tile_heads(sb))
        q_ref[...] = _mx(q * MLA_QSCALE)
        zkv = _dot_nt(hb, w_ref[384:640, :])
        zkv_ref[...] = zkv
        rkv = lax.rsqrt(jnp.mean(zkv * zkv, axis=-1, keepdims=True) + EPS)
        kvl = _mx((zkv * rkv) * gkv_ref[...])
        kvl_ref[...] = kvl
        kr = _dot_nt(hb, w_ref[640:768, :])
        kpe = kr * ck + pltpu.roll(kr, LANE - 16, 1) * sa + pltpu.roll(kr, 16, 1) * sb
        kf = _dot(kvl, wuk_ref[...])
        k_ref[...] = _mx(kf + jnp.concatenate([kpe] * MLA_HEADS, axis=1))
        v_ref[...] = _mx(_dot(kvl, wuv_ref[...]))
        g_ref[:, 0:512] = _dot_nt(hb, w_ref[768:1280, :])
        qs_ref[...] = _mx(_dot_nt(hb, w_ref[1280:1792, :]) * (SWA_SCALE * LOG2E))
        ks_ref[...] = _mx(_dot_nt(hb, w_ref[1792:1920, :]))
        vs_ref[...] = _mx(_dot_nt(hb, w_ref[1920:2048, :]))
        g_ref[:, 512:1024] = _dot_nt(hb, w_ref[2048:2560, :])

    row = lambda w: pl.BlockSpec((ts, w), lambda i: (i, 0))
    dense = pl.BlockSpec((ts // 8, LANE), lambda i: (i, 0))
    full = lambda a: pl.BlockSpec(a.shape, lambda i: (0,) * a.ndim)
    per_b = pl.BlockSpec((1, 1, D), lambda i: (i // nsb, 0, 0))
    out_w = [(D, _MXU_DTYPE), (384, F32), (256, F32), (384, _MXU_DTYPE), (256, _MXU_DTYPE), (WQ, _MXU_DTYPE),
             (WQ, _MXU_DTYPE), (512, _MXU_DTYPE), (512, _MXU_DTYPE), (128, _MXU_DTYPE), (128, _MXU_DTYPE), (1024, F32)]
    return pl.pallas_call(
        body, name="pre_fwd", grid=(T // ts,),
        out_shape=[jax.ShapeDtypeStruct((T, w), dt) for w, dt in out_w] + [jax.ShapeDtypeStruct((D_IN_PAD, D), w_in.dtype)],
        in_specs=[row(D), per_b, per_b, full(ng), full(w_in), full(gq), full(gkv), full(w_uq), full(w_uk), full(w_uv),
                  dense, dense],
        out_specs=[row(w) for w, _ in out_w] + [pl.BlockSpec((D_IN_PAD, D), lambda i: (0, 0))],
        compiler_params=_cparams(("arbitrary",)),
    )(x2, shift, scale, ng, w_in, gq, gkv, w_uq, w_uk, w_uv, *rope)


def _mla_fwd(q3, k3, v3, tq, tk):
    B, S, _ = q3.shape
    nq = S // tq
    assert tq == tk
    HPS = 2
    NH = 2 * HPS

    def body(q_ref, k_ref, v_ref, o_ref, lse_ref):
        rows = lax.broadcasted_iota(jnp.int32, (tq, tk), 0)
        cols = lax.broadcasted_iota(jnp.int32, (tq, tk), 1)
        low_k = _lane_iota((tk, LANE)) < 64
        low = _lane_iota((tq, LANE)) < 64

        def step(qs, kt, carry, masked):
            r0 = kt * tk
            out = []
            for j in range(NH):
                v2 = v_ref[0, r0:r0 + tk, LANE * (j // 2):LANE * (j // 2 + 1)]
                vj = (jnp.where(low_k, v2, 1.0) if j % 2 == 0 else jnp.where(low_k, 1.0, v2)).astype(v2.dtype)
                m, acc = carry[j]
                s = _dot_nt(qs[j], k_ref[0, r0:r0 + tk, LANE * j:LANE * (j + 1)])
                if masked:
                    s = jnp.where(rows >= cols, s, NEG_INF)
                m_new = jnp.maximum(m, jnp.max(s, axis=1, keepdims=True))
                alpha = jnp.exp2(m - m_new)
                p = jnp.exp2(s - m_new)
                acc = alpha * acc + _dot(_mx(p), vj)
                out.append((m_new, acc))
            return tuple(out)

        for qi in range(nq):
            r = slice(qi * tq, (qi + 1) * tq)
            qs = [q_ref[0, r, LANE * j:LANE * (j + 1)] for j in range(NH)]
            init = (jnp.full((tq, 1), NEG_INF, F32), jnp.zeros((tq, LANE), F32))
            carry = (init,) * NH
            for kt in range(qi):
                carry = step(qs, kt, carry, False)
            carry = step(qs, qi, carry, True)
            for t in range(HPS):
                (m0, a0), (m1, a1) = carry[2 * t], carry[2 * t + 1]
                l0 = jnp.where(low, pltpu.roll(a0, 64, 1), a0)
                l1 = jnp.where(low, a1, pltpu.roll(a1, 64, 1))
                o_ref[0, r, LANE * t:LANE * (t + 1)] = jnp.where(low, a0 / l0, a1 / l1)
                lse_ref[0, 2 * t, r, :] = m0 + jnp.log2(l0)
                lse_ref[0, 2 * t + 1, r, :] = m1 + jnp.log2(l1)

    grp = lambda w: pl.BlockSpec((1, S, w), lambda b, hp: (b, 0, hp))
    return pl.pallas_call(
        body, name="mla_fwd", grid=(B, MLA_HEADS // NH),
        out_shape=[jax.ShapeDtypeStruct((B, S, 512), F32), jax.ShapeDtypeStruct((B, MLA_HEADS, S, LANE), F32)],
        in_specs=[grp(NH * LANE), grp(NH * LANE), grp(HPS * LANE)],
        out_specs=[grp(HPS * LANE), pl.BlockSpec((1, NH, S, LANE), lambda b, hp: (b, hp, 0, 0))],
        compiler_params=_cparams(("arbitrary", "arbitrary")),
    )(q3, k3, v3)


def _mla_bwd(q3, k3, v3, o3, do3, lse, tq, tk, hps=2):
    B, S, _ = q3.shape
    nq, nk = S // tq, S // tk
    nh = 2 * hps
    assert tk % tq == 0

    def body(q_ref, k_ref, v_ref, o_ref, do_ref, lse_ref, dq_ref, dk_ref, dv_ref, dkt_ref, dvt_ref):
        dkt_ref[...] = jnp.zeros_like(dkt_ref)
        dvt_ref[...] = jnp.zeros_like(dvt_ref)
        rows = lax.broadcasted_iota(jnp.int32, (tq, tk), 0)
        cols = lax.broadcasted_iota(jnp.int32, (tq, tk), 1)
        lane = _lane_iota((tq, LANE))

        def q_tile(qi, _):
            r = pl.ds(pl.multiple_of(qi * tq, tq), tq)
            heads = []
            for j in range(nh):
                lanes = slice(LANE * j, LANE * (j + 1))
                pair = slice(LANE * (j // 2), LANE * (j // 2 + 1))
                do = jnp.where((lane < 64) if j % 2 == 0 else (lane >= 64), do_ref[0, r, pair], 0.0)
                q = q_ref[0, r, lanes]
                heads.append((lanes, pair, q, _mx(q.astype(F32).T), _mx(do), _mx(do.T),
                              jnp.sum(do * o_ref[0, r, pair], axis=1, keepdims=True),
                              jnp.concatenate([lse_ref[0, j, r, :]] * (tk // LANE), axis=1)))
            n_full = (qi * tq) // tk

            def k_tile(kt, dqs, masked):
                kr = pl.ds(pl.multiple_of(kt * tk, tk), tk)
                out = []
                dvt = [None] * hps
                for j, (lanes, pair, q, qt, dob, dot_, dcol, lse_c) in enumerate(heads):
                    k = k_ref[0, kr, lanes]
                    s = _dot_nt(q, k)
                    if masked:
                        s = jnp.where(rows + qi * tq >= cols + kt * tk, s, NEG_INF)
                    p = jnp.exp2(s - lse_c)
                    dp = _dot_nt(dob, v_ref[0, kr, pair])
                    dsb = _mx(p * (dp - dcol))
                    dkt_ref[j, kt] += _dot(qt, dsb)
                    pv = _dot(dot_, _mx(p))
                    dvt[j // 2] = pv if dvt[j // 2] is None else dvt[j // 2] + pv
                    out.append(dqs[j] + _dot(dsb, k))
                for t in range(hps):
                    dvt_ref[t, kt] += dvt[t]
                return tuple(out)

            dqs = (jnp.zeros((tq, LANE), F32),) * nh
            for kt in range(n_full):
                dqs = k_tile(kt, dqs, False)
            dqs = k_tile(n_full, dqs, True)
            for j in range(nh):
                dq_ref[0, r, heads[j][0]] = (MLA_SCALE * dqs[j]).astype(dq_ref.dtype)
            return 0

        for qi in range(nq):
            q_tile(qi, 0)

        def flush(kt, _):
            kr = pl.ds(pl.multiple_of(kt * tk, tk), tk)
            for j in range(nh):
                dk_ref[0, kr, LANE * j:LANE * (j + 1)] = ((1.0 / LOG2E) * dkt_ref[j, kt].T).astype(dk_ref.dtype)
            for t in range(hps):
                dv_ref[0, kr, LANE * t:LANE * (t + 1)] = dvt_ref[t, kt].T.astype(dv_ref.dtype)
            return 0

        lax.fori_loop(0, nk, flush, 0)

    grp = lambda w: pl.BlockSpec((1, S, w), lambda b, hp: (b, 0, hp))
    return pl.pallas_call(
        body, name="mla_bwd", grid=(B, MLA_HEADS // nh),
        out_shape=[jax.ShapeDtypeStruct((B, S, 1024), MLA_GRAD_DTYPE), jax.ShapeDtypeStruct((B, S, 1024), MLA_GRAD_DTYPE),
                   jax.ShapeDtypeStruct((B, S, 512), MLA_GRAD_DTYPE)],
        in_specs=[grp(nh * LANE), grp(nh * LANE), grp(hps * LANE), grp(hps * LANE), grp(hps * LANE),
                  pl.BlockSpec((1, nh, S, LANE), lambda b, hp: (b, hp, 0, 0))],
        out_specs=[grp(nh * LANE), grp(nh * LANE), grp(hps * LANE)],
        scratch_shapes=[pltpu.VMEM((nh, nk, LANE, tk), F32), pltpu.VMEM((hps, nk, LANE, tk), F32)],
        compiler_params=_cparams(("arbitrary", "arbitrary")),
    )(q3, k3, v3, o3, do3, lse)


def _swa_consts(sink_ref):
    W = WINDOW
    row = lax.broadcasted_iota(jnp.int32, (4 * W, LANE), 0)
    out = []
    for g in range(SWA_KV_HEADS):
        slope = jnp.zeros((4 * W, LANE), F32)
        sink = jnp.zeros((4 * W, LANE), F32)
        for p in range(4):
            h = p + 4 * g
            here = jnp.logical_and(row >= W * p, row < W * (p + 1))
            slope = jnp.where(here, float(LOG2E * 2.0 ** (-8.0 * (h + 1) / SWA_HEADS)), slope)
            sink = jnp.where(here, LOG2E * sink_ref[:, h:h + 1], sink)
        out.append((jnp.concatenate([slope, slope], axis=1), sink))
    return out


def _wide(col):
    return jnp.concatenate([col, col], axis=1)


def _swa_block(q_ref, k_ref, v_ref, pb_ref, pr_ref, n, i, ij):
    W = WINDOW
    kb = jnp.maximum(n - 1, 0)
    r = pl.ds(pl.multiple_of(i * W, W), W)
    kr = pl.ds(pl.multiple_of(kb * W, W), 2 * W)
    q4, k2, v2 = q_ref[0, r, :], k_ref[0, kr, :], v_ref[0, kr, :]
    pq = _wide(pb_ref[0, r, :])
    pk = jnp.concatenate([pr_ref[0, pl.ds(kb, 1), :], pr_ref[0, pl.ds(kb + 1, 1), :]], axis=1)
    rel = ij + (n - kb) * W
    mask = jnp.where(jnp.logical_and(rel >= 0, rel < W), 0.0, NEG_INF)
    dist4 = jnp.concatenate([pq - pk] * 4, axis=0)
    mask4 = jnp.concatenate([mask] * 4, axis=0)
    return r, kb, kr, q4, k2, v2, dist4, mask4


def _swa_stack(x4, g, dtype):
    lane = _lane_iota((WINDOW, LANE))
    mine = (lane < 64) if g == 0 else (lane >= 64)
    return jnp.concatenate([jnp.where(mine, x4[:, LANE * p:LANE * (p + 1)], 0).astype(dtype) for p in range(4)], axis=0)


def _swa_unstack(ref, r, lo, hi, scale=None):
    W = WINDOW
    low = _lane_iota((W, LANE)) < 64
    for p in range(4):
        t = jnp.where(low, lo[W * p:W * (p + 1)], hi[W * p:W * (p + 1)])
        ref[0, r, LANE * p:LANE * (p + 1)] = t if scale is None else scale * t


def _swa_fwd(qs3, ks3, vs3, posb, posr, sinks):
    B, S, _ = qs3.shape
    W = WINDOW
    nb = S // W
    nh = SWA_SEQ_SPLIT
    nbh = nb // nh

    def body(q_ref, k_ref, v_ref, pb_ref, pr_ref, sink_ref, o_ref, lse_ref):
        ij = lax.broadcasted_iota(jnp.int32, (W, 2 * W), 0) - lax.broadcasted_iota(jnp.int32, (W, 2 * W), 1)
        consts = _swa_consts(sink_ref)
        n0 = pl.program_id(1) * nbh

        def blk(i, _):
            n = n0 + i
            r, _, _, q4, k2, v2, dist4, mask4 = _swa_block(q_ref, k_ref, v_ref, pb_ref, pr_ref, n, i, ij)
            o_g = []
            for g, (slope, sink) in enumerate(consts):
                s = _dot_nt(_swa_stack(q4, g, q4.dtype), k2) - slope * dist4 + mask4
                m = jnp.maximum(jnp.max(s, axis=1, keepdims=True), sink)
                e = jnp.exp2(s - _wide(m))
                l = jnp.sum(e, axis=1, keepdims=True) + jnp.exp2(sink - m)
                o_g.append(_dot(_mx(e), v2) * (1.0 / l))
                lse_ref[0, i, g] = m + jnp.log2(l)
            _swa_unstack(o_ref, r, o_g[0], o_g[1])
            return 0

        for i in range(nbh):
            blk(i, 0)

    seq = lambda w: pl.BlockSpec((1, S, w), lambda b, h: (b, 0, 0))
    part = lambda w: pl.BlockSpec((1, S // nh, w), lambda b, h: (b, h, 0))
    lse_spec = pl.BlockSpec((1, nbh, 2, 4 * W, LANE), lambda b, h: (b, h, 0, 0, 0))
    return pl.pallas_call(
        body, name="swa_fwd", grid=(B, nh),
        out_shape=[jax.ShapeDtypeStruct((B, S, 512), F32), jax.ShapeDtypeStruct((B, nb, 2, 4 * W, LANE), F32)],
        in_specs=[part(512), seq(LANE), seq(LANE), part(LANE), pl.BlockSpec((1, nb, W), lambda b, h: (b, 0, 0)),
                  pl.BlockSpec((1, LANE), lambda b, h: (0, 0))],
        out_specs=[part(512), lse_spec],
        compiler_params=_cparams(("arbitrary", "arbitrary")),
    )(qs3, ks3, vs3, posb, posr, sinks)


def _swa_bwd(qs3, ks3, vs3, posb, posr, sinks, os3, do3, lse):
    B, S, _ = qs3.shape
    W = WINDOW
    nb = S // W
    nh = SWA_SEQ_SPLIT
    nbh = nb // nh
    assert nbh % 2 == 0 and nbh * nh * W == S

    def body(q_ref, k_ref, v_ref, pb_ref, pr_ref, sink_ref, o_ref, do_ref, lse_ref, dq_ref, dk_ref, dv_ref, dsink_ref,
             dkt_ref, dvt_ref):
        lane1 = _lane_iota((1, LANE))
        ij = lax.broadcasted_iota(jnp.int32, (W, 2 * W), 0) - lax.broadcasted_iota(jnp.int32, (W, 2 * W), 1)
        consts = _swa_consts(sink_ref)
        hh = pl.program_id(1)
        n0 = hh * nbh

        @pl.when(hh == 0)
        def _():
            dkt_ref[...] = jnp.zeros_like(dkt_ref)
            dvt_ref[...] = jnp.zeros_like(dvt_ref)

        @pl.when(jnp.logical_and(pl.program_id(0) == 0, hh == 0))
        def _():
            dsink_ref[...] = jnp.zeros_like(dsink_ref)

        def blk(i, dsink):
            n = n0 + i
            r, kb, _, q4, k2, v2, dist4, mask4 = _swa_block(q_ref, k_ref, v_ref, pb_ref, pr_ref, n, i, ij)
            o4, do4 = o_ref[0, r, :], do_ref[0, r, :]
            dq_g = []
            dkt = jnp.zeros((LANE, 2 * W), F32)
            dvt = jnp.zeros((LANE, 2 * W), F32)
            for g, (slope, sink) in enumerate(consts):
                q_st = _swa_stack(q4, g, F32)
                do_st = _swa_stack(do4, g, F32)
                dcol = jnp.sum(do_st * _swa_stack(o4, g, F32), axis=1, keepdims=True)
                qst, dob = _mx(q_st), _mx(do_st)
                lse_c = lse_ref[0, i, g]
                pr = jnp.exp2(_dot_nt(qst, k2) - slope * dist4 + mask4 - _wide(lse_c))
                dsb = _mx(pr * (_dot_nt(dob, v2) - dcol))
                psd = jnp.exp2(sink - lse_c)[:, 0:1] * dcol
                for p in range(4):
                    dsink = dsink - jnp.where(lane1 == p + 4 * g,
                                              jnp.sum(psd[W * p:W * (p + 1)], axis=0, keepdims=True), 0.0)
                dq_g.append(_dot(dsb, k2))
                dkt = dkt + _dot(_mx(q_st.T), dsb)
                dvt = dvt + _dot(_mx(do_st.T), _mx(pr))
            _swa_unstack(dq_ref, r, dq_g[0], dq_g[1], SWA_SCALE)
            dkt_ref[kb] += dkt[:, 0:W]
            dkt_ref[kb + 1] += dkt[:, W:2 * W]
            dvt_ref[kb] += dvt[:, 0:W]
            dvt_ref[kb + 1] += dvt[:, W:2 * W]
            return dsink

        dsink = jnp.zeros((1, LANE), F32)
        for i in range(nbh):
            dsink = blk(i, dsink)
        dsink_ref[...] += dsink

        @pl.when(hh == nh - 1)
        def _():
            def flush(n, _):
                r = pl.ds(pl.multiple_of(n * W, W), W)
                dk_ref[0, r, :] = (1.0 / LOG2E) * dkt_ref[n].T
                dv_ref[0, r, :] = dvt_ref[n].T
                return 0

            lax.fori_loop(0, nb, flush, 0)

    seq = lambda w: pl.BlockSpec((1, S, w), lambda b, h: (b, 0, 0))
    part = lambda w: pl.BlockSpec((1, S // nh, w), lambda b, h: (b, h, 0))
    return pl.pallas_call(
        body, name="swa_bwd", grid=(B, nh),
        out_shape=[jax.ShapeDtypeStruct((B, S, 512), F32), jax.ShapeDtypeStruct((B, S, LANE), F32),
                   jax.ShapeDtypeStruct((B, S, LANE), F32), jax.ShapeDtypeStruct((1, LANE), F32)],
        in_specs=[part(512), seq(LANE), seq(LANE), part(LANE), pl.BlockSpec((1, nb, W), lambda b, h: (b, 0, 0)),
                  pl.BlockSpec((1, LANE), lambda b, h: (0, 0)), part(512),
                  pl.BlockSpec((1, S // nh, 512), lambda b, h: (b, h, 1)),
                  pl.BlockSpec((1, nbh, 2, 4 * W, LANE), lambda b, h: (b, h, 0, 0, 0))],
        out_specs=[part(512), seq(LANE), seq(LANE), pl.BlockSpec((1, LANE), lambda b, h: (0, 0))],
        scratch_shapes=[pltpu.VMEM((nb, LANE, W), F32), pltpu.VMEM((nb, LANE, W), F32)],
        compiler_params=_cparams(("arbitrary", "arbitrary")),
    )(qs3, ks3, vs3, posb, posr, sinks, os3, do3, lse)


def _post(om, osw, g, w_out, x2, gate, fg, tgt, S, ts):
    T, D = x2.shape
    nsb = S // ts

    def body(om_ref, os_ref, g_ref, w_ref, x_ref, gate_ref, fg_ref, t_ref,
             dx2_ref, do_ref, dg_ref, loss_ref, dfg_ref, dgate_ref, dw_ref):
        i = pl.program_id(0)
        gv = g_ref[...]
        sg = 1.0 / (1.0 + jnp.exp(-gv))
        silu = gv * sg
        o = jnp.concatenate([om_ref[...], os_ref[...]], axis=1)
        ab = _mx(o * silu)
        y = _dot(ab, w_ref[...])
        gate = gate_ref[0]
        xo = x_ref[...] + gate * y
        r2 = lax.rsqrt(jnp.mean(xo * xo, axis=-1, keepdims=True) + EPS)
        xh = xo * r2
        fg = fg_ref[...]
        diff = xh * fg - t_ref[...]
        sq = jnp.sum(diff * diff, axis=0, keepdims=True)
        part = sq[:, 0:LANE]
        for t in range(1, D // LANE):
            part = part + sq[:, LANE * t:LANE * (t + 1)]
        dout = diff * (1.0 / D)
        dxh = dout * fg
        dx2 = r2 * (dxh - xh * jnp.mean(dxh * xh, axis=-1, keepdims=True))
        dx2_ref[...] = dx2
        dyb = _mx(dx2 * gate)
        da = _dot_nt(dyb, w_ref[...])
        do_ref[...] = da * silu
        dg_ref[...] = _mx(da * o * (sg * (1.0 + gv * (1.0 - sg))))

        @pl.when(i == 0)
        def _():
            loss_ref[...] = jnp.zeros_like(loss_ref)
            dfg_ref[...] = jnp.zeros_like(dfg_ref)
            dw_ref[...] = jnp.zeros_like(dw_ref)

        @pl.when(i % nsb == 0)
        def _():
            dgate_ref[...] = jnp.zeros_like(dgate_ref)

        loss_ref[...] += (0.5 / D) * part
        dfg_ref[...] += jnp.sum(dout * xh, axis=0, keepdims=True)
        dgate_ref[0] += jnp.sum(dx2 * y, axis=0, keepdims=True)
        dw_ref[...] += _dot_tn(ab, dyb)

        @pl.when(i == T // ts - 1)
        def _():
            chunks = [dw_ref[512 + 64 * PAIR_INV[h]:512 + 64 * (PAIR_INV[h] + 1), :] for h in range(SWA_HEADS)]
            for h in range(SWA_HEADS):
                dw_ref[512 + 64 * h:512 + 64 * (h + 1), :] = chunks[h]

    row = lambda w: pl.BlockSpec((ts, w), lambda i: (i, 0))
    full = lambda a: pl.BlockSpec(a.shape, lambda i: (0,) * a.ndim)
    per_b = pl.BlockSpec((1, 1, D), lambda i: (i // nsb, 0, 0))
    return pl.pallas_call(
        body, name="post", grid=(T // ts,),
        out_shape=[jax.ShapeDtypeStruct((T, D), F32), jax.ShapeDtypeStruct((T, 1024), F32),
                   jax.ShapeDtypeStruct((T, 1024), _MXU_DTYPE), jax.ShapeDtypeStruct((1, LANE), F32),
                   jax.ShapeDtypeStruct((1, D), F32), jax.ShapeDtypeStruct(gate.shape, F32),
                   jax.ShapeDtypeStruct(w_out.shape, F32)],
        in_specs=[row(512), row(512), row(1024), full(w_out), row(D), per_b, full(fg), row(D)],
        out_specs=[row(D), row(1024), row(1024),
                   pl.BlockSpec((1, LANE), lambda i: (0, 0)), pl.BlockSpec((1, D), lambda i: (0, 0)), per_b,
                   full(w_out)],
        compiler_params=_cparams(("arbitrary",)),
    )(om, osw, g, w_out, x2, gate, fg, tgt)


def _pre_bwd(dq, dk, dv, dqs, dks, dvs, dg, zq, zkv, ql, kvl, x2, dx2, scale, ng, w_in, gq, gkv, w_uq, w_uk, w_uv,
             rope, S, ts):
    T, D = x2.shape
    nsb = S // ts
    WQ = MLA_HEADS * LANE

    def body(dq_ref, dk_ref, dv_ref, dqs_ref, dks_ref, dvs_ref, dg_ref, zq_ref, zkv_ref, ql_ref, kvl_ref, x_ref, dx2_ref,
             sc_ref,
             ng_ref, w_ref, gq_ref, gkv_ref, wuq_ref, wuk_ref, wuv_ref, cd_ref, sd_ref,
             gx_ref, dz_ref, dgq_ref, dgkv_ref, dng_ref, dsh_ref, dsc_ref, dwuq_ref, dwuk_ref, dwuv_ref):
        i = pl.program_id(0)

        @pl.when(i == 0)
        def _():
            dgq_ref[...] = jnp.zeros_like(dgq_ref)
            dgkv_ref[...] = jnp.zeros_like(dgkv_ref)
            dng_ref[...] = jnp.zeros_like(dng_ref)
            dwuq_ref[...] = jnp.zeros_like(dwuq_ref)
            dwuk_ref[...] = jnp.zeros_like(dwuk_ref)
            dwuv_ref[...] = jnp.zeros_like(dwuv_ref)

        @pl.when(i % nsb == 0)
        def _():
            dsh_ref[...] = jnp.zeros_like(dsh_ref)
            dsc_ref[...] = jnp.zeros_like(dsc_ref)

        def norm_bwd(z, dy, gain):
            r = lax.rsqrt(jnp.mean(z * z, axis=-1, keepdims=True) + EPS)
            zh = z * r
            dzh = dy * gain
            return r * (dzh - zh * jnp.mean(dzh * zh, axis=-1, keepdims=True)), jnp.sum(dy * zh, axis=0, keepdims=True)

        tables = _rope_split(cd_ref[...], sd_ref[...])
        ng = ng_ref[...]
        sc1 = 1.0 + sc_ref[0]

        def rows_chain(rs):
            cq, ck, sa, sb = (t[rs] for t in tables)
            dqr = dq_ref[rs, :].astype(F32)
            dqb = _mx(dqr * _tile_heads(cq) + pltpu.roll(dqr * _tile_heads(sa), 16, 1)
                      + pltpu.roll(dqr * _tile_heads(sb), WQ - 16, 1))
            p_uq = _dot_tn(ql_ref[rs, :], dqb)
            dzq, dgq = norm_bwd(zq_ref[rs, :], _dot_nt(dqb, wuq_ref[...]), gq_ref[...])
            dkr = dk_ref[rs, :].astype(F32)
            dkb = _mx(dkr)
            p_uk = _dot_tn(kvl_ref[rs, :], dkb)
            dvb = _mx(dv_ref[rs, :])
            p_uv = _dot_tn(kvl_ref[rs, :], dvb)
            dzkv, dgkv = norm_bwd(zkv_ref[rs, :], _dot_nt(dkb, wuk_ref[...]) + _dot_nt(dvb, wuv_ref[...]), gkv_ref[...])
            dkpe = dkr[:, 0:LANE]
            for h in range(1, MLA_HEADS):
                dkpe = dkpe + dkr[:, LANE * h:LANE * (h + 1)]
            dkro = dkpe * ck + pltpu.roll(dkpe * sa, 16, 1) + pltpu.roll(dkpe * sb, LANE - 16, 1)
            dz_ref[rs, 0:384] = _mx(dzq)
            dz_ref[rs, 384:640] = _mx(dzkv)
            dz_ref[rs, 640:768] = _mx(dkro)
            dz_ref[rs, 768:1280] = dg_ref[rs, 0:512]
            dz_ref[rs, 1280:1792] = _mx(dqs_ref[rs, :])
            dz_ref[rs, 1792:1920] = _mx(dks_ref[rs, :])
            dz_ref[rs, 1920:2048] = _mx(dvs_ref[rs, :])
            dz_ref[rs, 2048:2560] = dg_ref[rs, 512:1024]
            dh = _dot(dz_ref[rs, :], w_ref[...])
            x = x_ref[rs, :]
            r1 = lax.rsqrt(jnp.mean(x * x, axis=-1, keepdims=True) + EPS)
            xn = x * r1
            dxn = dh * ng * sc1
            gx_ref[rs, :] = dx2_ref[rs, :] + r1 * (dxn - xn * jnp.mean(dxn * xn, axis=-1, keepdims=True))
            return (p_uq, p_uk, p_uv, dgq, dgkv, jnp.sum(dh, axis=0, keepdims=True),
                    jnp.sum(dh * (xn * ng), axis=0, keepdims=True), jnp.sum(dh * xn * sc1, axis=0, keepdims=True))

        hr = ts // 2
        parts = [rows_chain(slice(hr * t, hr * (t + 1))) for t in range(2)]
        p_uq, p_uk, p_uv, dgq, dgkv, dsh, dsc, dng = (a + b for a, b in zip(*parts))
        dwuq_ref[...] += p_uq
        dwuk_ref[...] += p_uk
        dwuv_ref[...] += p_uv
        dgq_ref[...] += dgq
        dgkv_ref[...] += dgkv
        dsh_ref[0] += dsh
        dsc_ref[0] += dsc
        dng_ref[...] += dng

    row = lambda w: pl.BlockSpec((ts, w), lambda i: (i, 0))
    full = lambda a: pl.BlockSpec(a.shape, lambda i: (0,) * a.ndim, pipeline_mode=pl.Buffered(1))
    per_b = pl.BlockSpec((1, 1, D), lambda i: (i // nsb, 0, 0))
    dense = pl.BlockSpec((ts // 8, LANE), lambda i: (i, 0))
    vec = lambda w: pl.BlockSpec((1, w), lambda i: (0, 0))
    return pl.pallas_call(
        body, name="pre_bwd", grid=(T // ts,),
        out_shape=[jax.ShapeDtypeStruct((T, D), F32), jax.ShapeDtypeStruct((T, D_IN_PAD), _MXU_DTYPE), jax.ShapeDtypeStruct((1, 384), F32),
                   jax.ShapeDtypeStruct((1, 256), F32), jax.ShapeDtypeStruct((1, D), F32),
                   jax.ShapeDtypeStruct(scale.shape, F32), jax.ShapeDtypeStruct(scale.shape, F32),
                   jax.ShapeDtypeStruct(w_uq.shape, F32), jax.ShapeDtypeStruct(w_uk.shape, F32),
                   jax.ShapeDtypeStruct(w_uv.shape, F32)],
        in_specs=[row(WQ), row(WQ), row(512), row(512), row(LANE), row(LANE), row(1024), row(384), row(256), row(384),
                  row(256), row(D), row(D), per_b, full(ng), full(w_in), full(gq), full(gkv), full(w_uq), full(w_uk), full(w_uv),
                  dense, dense],
        out_specs=[row(D), row(D_IN_PAD), vec(384), vec(256), vec(D), per_b, per_b, full(w_uq), full(w_uk), full(w_uv)],
        compiler_params=_cparams(("arbitrary",)),
    )(dq, dk, dv, dqs, dks, dvs, dg, zq, zkv, ql, kvl, x2, dx2, scale, ng, w_in, gq, gkv, w_uq, w_uk, w_uv, *rope)


def _w_in_row_runs():
    runs = [(0, 0, 640), (640, 704, 32), (672, 768, 512)]
    runs += [(1184 + 64 * h, 1280 + 64 * PAIR_INV[h], 64) for h in range(8)]
    runs += [(1696, 1792, 256)]
    runs += [(1952 + 64 * h, 2048 + 64 * PAIR_INV[h], 64) for h in range(8)]
    return runs


def _dw_in_t(dz, hb, tn, tk):
    T, M = dz.shape
    N = hb.shape[1]
    nk = T // tk

    def body(a_ref, b_ref, o_ref, acc_ref):
        k = pl.program_id(1)

        @pl.when(k == 0)
        def _():
            acc_ref[...] = jnp.zeros_like(acc_ref)

        acc_ref[...] += _dot_tn(a_ref[...], b_ref[...])

        @pl.when(k == nk - 1)
        def _():
            for nat, pad, size in _w_in_row_runs():
                o_ref[nat:nat + size, :] = acc_ref[pad:pad + size, :]

    return pl.pallas_call(
        body, name="dw_in", grid=(N // tn, nk),
        out_shape=jax.ShapeDtypeStruct((D_IN, N), F32),
        in_specs=[pl.BlockSpec((tk, M), lambda j, k: (k, 0)), pl.BlockSpec((tk, tn), lambda j, k: (k, j))],
        out_specs=pl.BlockSpec((D_IN, tn), lambda j, k: (0, j)),
        scratch_shapes=[pltpu.VMEM((M, tn), F32)],
        compiler_params=_cparams(("arbitrary", "arbitrary")),
    )(dz, hb)


def _finalize(parts_all, dmod_all, dmod_cols, c_act_all):
    nparts = parts_all.shape[-1]
    nmod = dmod_all.shape[-1]

    def body(p_ref, dm_ref, dmc_ref, c_ref, ps_ref, loss_ref, db_ref, dw_ref):
        acc = p_ref[0]
        for j in range(1, N_DEV):
            acc = acc + p_ref[j]
        ps_ref[...] = acc
        loss_ref[...] = jnp.sum(acc[:, 0:LANE], axis=1, keepdims=True)
        db = dm_ref[0:1, :]
        for j in range(1, dm_ref.shape[0]):
            db = db + dm_ref[j:j + 1, :]
        db_ref[...] = db
        dw_ref[...] = _dot_tn(_mx(_silu(c_ref[...])), _mx(dmc_ref[...]))

    return pl.pallas_call(
        body, name="finalize",
        out_shape=[jax.ShapeDtypeStruct((1, nparts), F32), jax.ShapeDtypeStruct((1, 1), F32),
                   jax.ShapeDtypeStruct((1, nmod), F32),
                   jax.ShapeDtypeStruct((c_act_all.shape[1], dmod_cols.shape[1]), F32)],
        in_specs=[_vmem()] * 4, out_specs=[_vmem()] * 4,
        compiler_params=_cparams(),
    )(parts_all, dmod_all, dmod_cols, c_act_all)


def _adamw(ws, gs, ms, vs):
    n = len(ws)

    def body(*refs):
        w_refs, g_refs, m_refs, v_refs, d_refs, nm_refs, nv_refs = (refs[t * n:(t + 1) * n] for t in range(7))
        for t in range(n):
            gv = g_refs[t][...]
            nm = ADAM_B1 * m_refs[t][...] + (1.0 - ADAM_B1) * gv
            nv = ADAM_B2 * v_refs[t][...] + (1.0 - ADAM_B2) * (gv * gv)
            m_hat = nm / (1.0 - ADAM_B1 ** ADAM_STEP)
            v_hat = nv / (1.0 - ADAM_B2 ** ADAM_STEP)
            d_refs[t][...] = -ADAM_LR * (m_hat / (jnp.sqrt(v_hat) + ADAM_EPS) + ADAM_WD * w_refs[t][...])
            nm_refs[t][...] = nm
            nv_refs[t][...] = nv

    out = pl.pallas_call(
        body, name="adamw",
        out_shape=[jax.ShapeDtypeStruct(w.shape, F32) for w in ws] * 3,
        in_specs=[_vmem()] * (4 * n), out_specs=[_vmem()] * (3 * n),
        compiler_params=_cparams(),
    )(*ws, *gs, *ms, *vs)
    return out[:n], out[n:2 * n], out[2 * n:]


def _pair_perm(a, axis, order):
    a = jnp.moveaxis(a, axis, -1)
    lead = a.shape[:-1]
    a = a.reshape(lead + (8, 64))[..., list(order), :].reshape(lead + (512,))
    return jnp.moveaxis(a, -1, axis)


def _rope_table(positions):
    T = positions.size
    inv = ROPE_THETA ** (-jnp.arange(0, MLA_ROPE, 2, dtype=F32) / MLA_ROPE)
    pos = jnp.repeat(positions.reshape(T // 8, 8)[:, ::-1].astype(F32), MLA_ROPE // 2, axis=1)
    ang = pos * jnp.tile(inv, 8)[None, :]
    return jnp.cos(ang), jnp.sin(ang)


def _local_step(x, mod, positions, ng, w_in_t, gq, gkv, w_uq, w_ukv, sinks, w_out, fg, tgt,
                ts=512, fq=512, fk=512, bq=512, bk=512):
    B, S, D = x.shape
    T = B * S
    x2 = x.reshape(T, D)
    shift, scale, gate = (mod[:, None, k * D:(k + 1) * D] for k in range(3))
    w_uq_p = jnp.pad(w_uq.reshape(Q_LORA, MLA_HEADS, 96), ((0, 0), (0, 0), (0, 32))).reshape(Q_LORA, MLA_HEADS * LANE)
    w_ukv3 = w_ukv.reshape(KV_LORA, MLA_HEADS, 128)
    w_uk_p = jnp.pad(w_ukv3[:, :, :64], ((0, 0), (0, 0), (0, 64))).reshape(KV_LORA, MLA_HEADS * LANE)
    w_uv = w_ukv3[:, :, 64:].reshape(KV_LORA, 512)
    w_out_p = jnp.concatenate([w_out[:512], _pair_perm(w_out[512:], 0, PAIR_ORDER)], axis=0)
    rope = _rope_table(positions)
    posf = positions.astype(F32)
    posb = jnp.broadcast_to(posf[:, :, None], (B, S, LANE))
    posr = posf.reshape(B, S // WINDOW, WINDOW)
    sinks_l = jnp.pad(sinks.reshape(1, SWA_HEADS), ((0, 0), (0, LANE - SWA_HEADS)))

    (hb, zq, zkv, ql, kvl, q, k, v, qs, ks, vs, g, w_in_p) = _pre_fwd(
        x2, shift, scale, ng, w_in_t, gq, gkv, w_uq_p, w_uk_p, w_uv, rope, S, ts)
    r3 = lambda a: a.reshape(B, S, a.shape[-1])
    om, lse_m = _mla_fwd(r3(q), r3(k), r3(v), fq, fk)
    osw, lse_s = _swa_fwd(r3(qs), r3(ks), r3(vs), posb, posr, sinks_l)
    dx2, do, dg, loss_v, dfg, dgate, dw_out = _post(
        om.reshape(T, 512), osw.reshape(T, 512), g, w_out_p, x2, gate, fg.reshape(1, D), tgt.reshape(T, D), S, ts)
    do3 = r3(do)
    dq, dk, dv = _mla_bwd(r3(q), r3(k), r3(v), om, do3, lse_m, bq, bk)
    dqs, dks, dvs, dsink = _swa_bwd(r3(qs), r3(ks), r3(vs), posb, posr, sinks_l, osw, do3, lse_s)
    f2 = lambda a: a.reshape(T, a.shape[-1])
    gx, dz, dgq, dgkv, dng, dsh, dsc, dw_uq_p, dw_uk_p, dw_uv = _pre_bwd(
        f2(dq), f2(dk), f2(dv), f2(dqs), f2(dks), f2(dvs), dg, zq, zkv, ql, kvl, x2, dx2, scale, ng, w_in_p, gq, gkv,
        w_uq_p, w_uk_p, w_uv, rope, S, ts)
    tk = min(T, 1024)
    dw_in_t = _dw_in_t(dz, hb, 512, tk)
    dw_uq = dw_uq_p.reshape(Q_LORA, MLA_HEADS, LANE)[:, :, :96].reshape(Q_LORA, 768)
    dw_uk = dw_uk_p.reshape(KV_LORA, MLA_HEADS, LANE)[:, :, :64]
    dw_uv = dw_uv.reshape(KV_LORA, MLA_HEADS, 64)
    dw_ukv = jnp.concatenate([dw_uk, dw_uv], axis=2).reshape(KV_LORA, 1024)
    parts = jnp.concatenate([loss_v, dfg, dng, dgq, dgkv, dsink], axis=1)
    dmod = jnp.concatenate([dsh, dsc, dgate], axis=2).reshape(B, 3 * D)
    return gx.reshape(B, S, D), dw_in_t, dw_uq, dw_ukv, dw_out, parts, dmod


def kernel(x, c, positions, w_ada, b_ada, norm_gain, w_in, q_norm_gain, kv_norm_gain, w_uq, w_ukv, swa_sinks, w_out, final_gain, loss_target, m_w_ada, m_b_ada, m_norm_gain, m_w_in, m_q_norm_gain, m_kv_norm_gain, m_w_uq, m_w_ukv, m_swa_sinks, m_w_out, m_final_gain, v_w_ada, v_b_ada, v_norm_gain, v_w_in, v_q_norm_gain, v_kv_norm_gain, v_w_uq, v_w_ukv, v_swa_sinks, v_w_out, v_final_gain):
    B, S, D = x.shape
    me = 4 * lax.axis_index("x") + 2 * lax.axis_index("y") + lax.axis_index("c")
    bf = _MXU_DTYPE

    ncol = w_ada.shape[2]
    b_cols = lax.dynamic_slice_in_dim(b_ada, me * ncol, ncol, axis=1)
    c_all, win_g, wuq_g, wukv_g, wout_g, mod_g = _all_gather(
        [c, w_in[0].T.astype(bf), w_uq[0].astype(bf), w_ukv[0].astype(bf), w_out[0].astype(bf)], "ag_weights",
        fused=((w_ada[0], b_cols), _ada_cols, jax.ShapeDtypeStruct((N_DEV, B, ncol), F32)))
    c_all = c_all.reshape(N_DEV * B, D)
    cat_cols = lambda a: jnp.transpose(a, (1, 0, 2)).reshape(a.shape[1], N_DEV * a.shape[2])
    w_in_t, w_uq_f, w_ukv_f = win_g.reshape(D_IN, D), cat_cols(wuq_g), cat_cols(wukv_g)
    w_out_f = wout_g.reshape(D, D)

    mod = lax.dynamic_index_in_dim(mod_g, me, axis=1, keepdims=False)
    mod = jnp.transpose(mod, (1, 0, 2)).reshape(B, 3 * D)

    gx, dw_in_t, dw_uq, dw_ukv, dw_out, parts, dmod = _local_step(
        x, mod, positions, norm_gain, w_in_t, q_norm_gain, kv_norm_gain, w_uq_f, w_ukv_f, swa_sinks,
        w_out_f, final_gain, loss_target)

    split_cols = lambda a: jnp.transpose(a.reshape(a.shape[0], 4, 2, a.shape[1] // N_DEV), (1, 2, 0, 3))
    split_rows = lambda a: a.reshape(4, 2, a.shape[0] // N_DEV, a.shape[1])
    g_w_in_t, g_w_uq, g_w_ukv, g_w_out, parts_g, dmod_g = _reduce_scatter(
        [split_rows(dw_in_t), split_cols(dw_uq), split_cols(dw_ukv), split_rows(dw_out)], "rs_grads",
        gather=(parts, dmod))
    g_w_in = g_w_in_t.T

    dmod_all = dmod_g.reshape(N_DEV * B, 3 * D)
    dmod_cols = lax.dynamic_slice_in_dim(dmod_all, me * ncol, ncol, axis=1)
    psum, loss, g_b_ada, g_w_ada = _finalize(parts_g, dmod_all, dmod_cols, c_all)
    loss = loss.reshape(())
    o = LANE
    g_final_gain = psum[0, o:o + D]
    g_norm_gain = psum[:, o + D:o + 2 * D]
    o += 2 * D
    g_q_norm_gain = psum[:, o:o + Q_LORA]
    g_kv_norm_gain = psum[:, o + Q_LORA:o + Q_LORA + KV_LORA]
    o += Q_LORA + KV_LORA
    g_sinks = psum[:, o:o + SWA_HEADS]

    grads = [g_w_ada, g_b_ada, g_norm_gain, g_w_in, g_q_norm_gain, g_kv_norm_gain, g_w_uq, g_w_ukv, g_sinks, g_w_out,
             g_final_gain]
    ws = [w_ada, b_ada, norm_gain, w_in, q_norm_gain, kv_norm_gain, w_uq, w_ukv, swa_sinks, w_out, final_gain]
    ms = [m_w_ada, m_b_ada, m_norm_gain, m_w_in, m_q_norm_gain, m_kv_norm_gain, m_w_uq, m_w_ukv, m_swa_sinks, m_w_out,
          m_final_gain]
    vs = [v_w_ada, v_b_ada, v_norm_gain, v_w_in, v_q_norm_gain, v_kv_norm_gain, v_w_uq, v_w_ukv, v_swa_sinks, v_w_out,
          v_final_gain]
    two_d = [(1, w.shape[0]) if w.ndim == 1 else w.shape[-2:] for w in ws]
    flat = lambda arrs: [a.reshape(s) for a, s in zip(arrs, two_d)]
    deltas, new_ms, new_vs = _adamw(flat(ws), flat(grads), flat(ms), flat(vs))
    shaped = lambda arrs: [a.reshape(w.shape) for a, w in zip(arrs, ws)]
    return (loss, gx, *shaped(grads), *shaped(deltas), *shaped(new_ms), *shaped(new_vs))
```

```python
import functools

import jax
import jax.numpy as jnp
from jax import lax
from jax.experimental import pallas as pl
from jax.experimental.pallas import tpu as pltpu

F32 = jnp.float32
_MXU_DTYPE = jnp.bfloat16
MLA_GRAD_DTYPE = jnp.bfloat16

N_DEV = 8
MLA_HEADS = 8
MLA_NOPE = 64
MLA_ROPE = 32
Q_LORA = 384
KV_LORA = 256
SWA_HEADS = 8
SWA_KV_HEADS = 2
SWA_HEAD_DIM = 64
WINDOW = 128
ROPE_THETA = 10000.0
EPS = 1e-6
MLA_SCALE = float((MLA_NOPE + MLA_ROPE) ** -0.5)
SWA_SCALE = float(SWA_HEAD_DIM ** -0.5)
LOG2E = 1.4426950408889634
MLA_QSCALE = MLA_SCALE * LOG2E
D_IN = 2464
D_IN_PAD = 2560
PAIR_ORDER = (0, 4, 1, 5, 2, 6, 3, 7)
PAIR_INV = (0, 2, 4, 6, 1, 3, 5, 7)

ADAM_LR = 0.001
ADAM_B1 = 0.9
ADAM_B2 = 0.999
ADAM_EPS = 1e-08
ADAM_WD = 0.01
ADAM_STEP = 10

LANE = 128
VMEM_LIMIT = 56 * 1024 * 1024

MESH = pl.DeviceIdType.MESH
NEG_INF = float("-inf")
SWA_SEQ_SPLIT = 2


def _mx(a):
    return a.astype(_MXU_DTYPE)


def _dot(a, b):
    return jnp.dot(a, b, preferred_element_type=F32)


def _dot_nt(a, b):
    return lax.dot_general(a, b, (((1,), (1,)), ((), ())), preferred_element_type=F32)


def _dot_tn(a, b):
    return lax.dot_general(a, b, (((0,), (0,)), ((), ())), preferred_element_type=F32)


def _cparams(sem=None):
    return pltpu.CompilerParams(dimension_semantics=sem, vmem_limit_bytes=VMEM_LIMIT)


def _vmem():
    return pl.BlockSpec(memory_space=pltpu.VMEM)


def _lane_iota(shape):
    return lax.broadcasted_iota(jnp.int32, shape, len(shape) - 1)


def _gather_program(srcs, outs, send_sems, recv_sems, local_sems):
    x, y, c = lax.axis_index("x"), lax.axis_index("y"), lax.axis_index("c")
    me, sibling = (x, y, c), (x, y, 1 - c)
    chips = [(1 - x, y), (x, 1 - y), (1 - x, 1 - y)]

    def slot(a, dev):
        return outs[a].at[4 * dev[0] + 2 * dev[1] + dev[2]]

    def copy(a, k, block, to, src=None):
        return pltpu.make_async_remote_copy(
            src_ref=slot(a, block) if src is None else src, dst_ref=slot(a, block),
            send_sem=send_sems.at[7 * a + k], recv_sem=recv_sems.at[7 * a + k],
            device_id=to, device_id_type=MESH)

    def local(a):
        return pltpu.make_async_copy(srcs[a], slot(a, me), local_sems.at[a])

    def start(a):
        local(a).start()
        cps = [copy(a, 0, me, sibling, src=srcs[a])]
        cps += [copy(a, 1 + j, me, (*chip, c), src=srcs[a]) for j, chip in enumerate(chips)]
        for cp in cps:
            cp.start()
        return cps

    def finish(group):
        cps = []
        for j, chip in enumerate(chips):
            for a in group:
                copy(a, 1 + j, (*chip, c), me).wait_recv()
                cp = copy(a, 4 + j, (*chip, c), sibling)
                cp.start()
                cps.append(cp)
        for a in group:
            copy(a, 0, sibling, me).wait_recv()
            for j, chip in enumerate(chips):
                copy(a, 4 + j, (*chip, 1 - c), me).wait_recv()
            local(a).wait()
        return cps

    return start, finish


def _gather_sems(m):
    return [pltpu.SemaphoreType.DMA((7 * m,)), pltpu.SemaphoreType.DMA((7 * m,)), pltpu.SemaphoreType.DMA((m,))]


def _all_gather(arrs, name, fused=None):
    n = len(arrs)
    extra, fn, piece = fused if fused else ((), None, None)
    ne, m = len(extra), n + (1 if fused else 0)

    def body(*refs):
        ins, ex, outs = refs[:n], refs[n:n + ne], refs[n + ne:n + ne + m]
        rest = refs[n + ne + m:]
        srcs = list(ins) + ([rest[0]] if fused else [])
        start, finish = _gather_program(srcs, outs, *rest[-3:])
        pending = []
        for a in range(n):
            pending += start(a)
        if fused:
            pending += finish([0])
            fn(srcs[n], outs[0], *ex)
            pending += start(n)
            pending += finish([n] + list(range(1, n)))
        else:
            pending += finish(list(range(n)))
        for cp in pending:
            cp.wait_send()

    out_shape = [jax.ShapeDtypeStruct((N_DEV,) + a.shape, a.dtype) for a in arrs]
    scratch = []
    if fused:
        out_shape.append(jax.ShapeDtypeStruct((N_DEV,) + piece.shape, piece.dtype))
        scratch.append(pltpu.VMEM(piece.shape, piece.dtype))
    return pl.pallas_call(
        body, name=name, out_shape=out_shape,
        in_specs=[_vmem()] * (n + ne), out_specs=[_vmem()] * m,
        scratch_shapes=scratch + _gather_sems(m),
        compiler_params=pltpu.CompilerParams(vmem_limit_bytes=VMEM_LIMIT),
    )(*arrs, *extra)


def _reduce_scatter(arrs, name, gather=()):
    n, g = len(arrs), len(gather)
    halves = [a[1] if isinstance(a, tuple) else a.astype(jnp.bfloat16) for a in arrs]
    arrs = [a[0] if isinstance(a, tuple) else a for a in arrs]

    def body(*refs):
        xs, xbs, gins = refs[:n], refs[n:2 * n], refs[2 * n:2 * n + g]
        outs, gouts = refs[2 * n + g:3 * n + g], refs[3 * n + g:3 * n + 2 * g]
        rest = refs[3 * n + 2 * g:]
        parts, recv_a, send_b, recv_b = (rest[t * n:(t + 1) * n] for t in range(4))
        send_sems, recv_sems, local_sems = rest[4 * n:4 * n + 3]
        x, y, c = lax.axis_index("x"), lax.axis_index("y"), lax.axis_index("c")
        myq = 2 * x + y
        gather_start, gather_finish = _gather_program(gins, gouts, *rest[4 * n + 3:])

        def chip(k):
            return (1 - x if k & 2 else x, 1 - y if k & 1 else y)

        def from_sibling(a):
            return pltpu.make_async_remote_copy(
                src_ref=xbs[a].at[:, 1 - c], dst_ref=recv_a[a], send_sem=send_sems.at[4 * a], recv_sem=recv_sems.at[4 * a],
                device_id=(x, y, 1 - c), device_id_type=MESH)

        def to_owner(a, k):
            qx, qy = chip(k)
            return pltpu.make_async_remote_copy(
                src_ref=send_b[a].at[2 * qx + qy], dst_ref=recv_b[a].at[myq],
                send_sem=send_sems.at[4 * a + k], recv_sem=recv_sems.at[4 * a + k],
                device_id=(qx, qy, c), device_id_type=MESH)

        mine = [pltpu.make_async_copy(xs[a].at[:, c], parts[a], local_sems.at[a]) for a in range(n)]
        first = [from_sibling(a) for a in range(n)]
        for cp in mine + first:
            cp.start()
        second = []
        for b in range(g):
            second += gather_start(b)
        for a in range(n):
            mine[a].wait()
            first[a].wait_recv()
            parts[a][...] = parts[a][...] + recv_a[a][...].astype(F32)
            send_b[a][...] = parts[a][...].astype(jnp.bfloat16)
            for k in range(1, 4):
                cp = to_owner(a, k)
                cp.start()
                second.append(cp)
        second += gather_finish(list(range(g)))
        for a in range(n):
            acc = parts[a][myq]
            for k in range(1, 4):
                to_owner(a, k).wait_recv()
                qx, qy = chip(k)
                acc = acc + recv_b[a][2 * qx + qy].astype(F32)
            outs[a][...] = acc
        for cp in first + second:
            cp.wait_send()

    quarter = lambda a, dt: pltpu.VMEM((4,) + a.shape[2:], dt)
    return pl.pallas_call(
        body, name=name,
        out_shape=[jax.ShapeDtypeStruct(a.shape[2:], F32) for a in arrs]
        + [jax.ShapeDtypeStruct((N_DEV,) + a.shape, a.dtype) for a in gather],
        in_specs=[pl.BlockSpec(memory_space=pl.ANY)] * (2 * n) + [_vmem()] * g, out_specs=[_vmem()] * (n + g),
        scratch_shapes=[quarter(a, F32) for a in arrs] + [quarter(a, jnp.bfloat16) for a in arrs] * 3
        + [pltpu.SemaphoreType.DMA((4 * n,)), pltpu.SemaphoreType.DMA((4 * n,)), pltpu.SemaphoreType.DMA((n,))]
        + _gather_sems(g),
        compiler_params=pltpu.CompilerParams(vmem_limit_bytes=VMEM_LIMIT),
    )(*arrs, *halves, *gather)


def _silu(t):
    return t * (1.0 / (1.0 + jnp.exp(-t)))


def _ada_cols(piece_ref, c_all_ref, w_ref, b_ref):
    w = _mx(w_ref[...])
    for d in range(N_DEV):
        piece_ref[d] = _dot(_mx(_silu(c_all_ref[d])), w) + b_ref[...]


def _rope_split(cd, sd):
    n = cd.shape[0]

    def expand(d):
        rep = jnp.broadcast_to(d[:, None, :], (n, 8, LANE)).reshape(8 * n, LANE)
        return pltpu.roll(rep, 0, 1, stride=16, stride_axis=0)

    c, s = expand(cd), expand(sd)
    lane = _lane_iota(c.shape)
    first = jnp.logical_and(lane >= 64, lane < 80)
    second = jnp.logical_and(lane >= 80, lane < 96)
    ck = jnp.where(first, pltpu.roll(c, 80, 1), jnp.where(second, pltpu.roll(c, 96, 1), 0.0))
    cq = jnp.where(lane < 64, 1.0, ck)
    sa = jnp.where(first, -pltpu.roll(s, 80, 1), 0.0)
    sb = jnp.where(second, pltpu.roll(s, 96, 1), 0.0)
    return cq, ck, sa, sb


def _tile_heads(t):
    return jnp.concatenate([t] * MLA_HEADS, axis=1)


def _pre_fwd(x2, shift, scale, ng, w_in, gq, gkv, w_uq, w_uk, w_uv, rope, S, ts):
    T, D = x2.shape
    nsb = S // ts
    WQ = MLA_HEADS * LANE

    def body(x_ref, sh_ref, sc_ref, ng_ref, wn_ref, gq_ref, gkv_ref, wuq_ref, wuk_ref, wuv_ref,
             cd_ref, sd_ref,
             hb_ref, zq_ref, zkv_ref, ql_ref, kvl_ref, q_ref, k_ref, v_ref, qs_ref, ks_ref, vs_ref, g_ref, wp_ref):
        @pl.when(pl.program_id(0) == 0)
        def _():
            wp_ref[640:704, :] = jnp.zeros((64, D), wp_ref.dtype)
            wp_ref[736:768, :] = jnp.zeros((32, D), wp_ref.dtype)
            for nat, pad, size in _w_in_row_runs():
                wp_ref[pad:pad + size, :] = wn_ref[nat:nat + size, :]

        w_ref = wp_ref
        x = x_ref[...]
        r1 = lax.rsqrt(jnp.mean(x * x, axis=-1, keepdims=True) + EPS)
        h = ((x * r1) * ng_ref[...]) * (1.0 + sc_ref[0]) + sh_ref[0]
        hb = _mx(h)
        hb_ref[...] = hb
        zq = _dot_nt(hb, w_ref[0:384, :])
        zq_ref[...] = zq
        rq = lax.rsqrt(jnp.mean(zq * zq, axis=-1, keepdims=True) + EPS)
        ql = _mx((zq * rq) * gq_ref[...])
        ql_ref[...] = ql
        q = _dot(ql, wuq_ref[...])
        cq, ck, sa, sb = _rope_split(cd_ref[...], sd_ref[...])
        q = (q * _tile_heads(cq) + pltpu.roll(q, WQ - 16, 1) * _tile_heads(sa)
             + pltpu.roll(q, 16, 1) * _tile_heads(sb))
        q_ref[...] = _mx(q * MLA_QSCALE)
        zkv = _dot_nt(hb, w_ref[384:640, :])
        zkv_ref[...] = zkv
        rkv = lax.rsqrt(jnp.mean(zkv * zkv, axis=-1, keepdims=True) + EPS)
        kvl = _mx((zkv * rkv) * gkv_ref[...])
        kvl_ref[...] = kvl
        kr = _dot_nt(hb, w_ref[640:768, :])
        kpe = kr * ck + pltpu.roll(kr, LANE - 16, 1) * sa + pltpu.roll(kr, 16, 1) * sb
        kf = _dot(kvl, wuk_ref[...])
        k_ref[...] = _mx(kf + jnp.concatenate([kpe] * MLA_HEADS, axis=1))
        v_ref[...] = _mx(_dot(kvl, wuv_ref[...]))
        g_ref[:, 0:512] = _dot_nt(hb, w_ref[768:1280, :])
        qs_ref[...] = _mx(_dot_nt(hb, w_ref[1280:1792, :]) * (SWA_SCALE * LOG2E))
        ks_ref[...] = _mx(_dot_nt(hb, w_ref[1792:1920, :]))
        vs_ref[...] = _mx(_dot_nt(hb, w_ref[1920:2048, :]))
        g_ref[:, 512:1024] = _dot_nt(hb, w_ref[2048:2560, :])

    row = lambda w: pl.BlockSpec((ts, w), lambda i: (i, 0))
    dense = pl.BlockSpec((ts // 8, LANE), lambda i: (i, 0))
    full = lambda a: pl.BlockSpec(a.shape, lambda i: (0,) * a.ndim)
    per_b = pl.BlockSpec((1, 1, D), lambda i: (i // nsb, 0, 0))
    out_w = [(D, _MXU_DTYPE), (384, F32), (256, F32), (384, _MXU_DTYPE), (256, _MXU_DTYPE), (WQ, _MXU_DTYPE),
             (WQ, _MXU_DTYPE), (512, _MXU_DTYPE), (512, _MXU_DTYPE), (128, _MXU_DTYPE), (128, _MXU_DTYPE), (1024, F32)]
    return pl.pallas_call(
        body, name="pre_fwd", grid=(T // ts,),
        out_shape=[jax.ShapeDtypeStruct((T, w), dt) for w, dt in out_w] + [jax.ShapeDtypeStruct((D_IN_PAD, D), w_in.dtype)],
        in_specs=[row(D), per_b, per_b, full(ng), full(w_in), full(gq), full(gkv), full(w_uq), full(w_uk), full(w_uv),
                  dense, dense],
        out_specs=[row(w) for w, _ in out_w] + [pl.BlockSpec((D_IN_PAD, D), lambda i: (0, 0))],
        compiler_params=_cparams(("arbitrary",)),
    )(x2, shift, scale, ng, w_in, gq, gkv, w_uq, w_uk, w_uv, *rope)


def _mla_fwd(q3, k3, v3, tq, tk):
    B, S, _ = q3.shape
    nq = S // tq
    assert tq == tk
    HPS = 2
    NH = 2 * HPS

    def body(q_ref, k_ref, v_ref, o_ref, lse_ref):
        rows = lax.broadcasted_iota(jnp.int32, (tq, tk), 0)
        cols = lax.broadcasted_iota(jnp.int32, (tq, tk), 1)
        low_k = _lane_iota((tk, LANE)) < 64
        low = _lane_iota((tq, LANE)) < 64

        def step(qs, kt, carry, masked):
            r0 = kt * tk
            out = []
            for j in range(NH):
                v2 = v_ref[0, r0:r0 + tk, LANE * (j // 2):LANE * (j // 2 + 1)]
                vj = (jnp.where(low_k, v2, 1.0) if j % 2 == 0 else jnp.where(low_k, 1.0, v2)).astype(v2.dtype)
                m, acc = carry[j]
                s = _dot_nt(qs[j], k_ref[0, r0:r0 + tk, LANE * j:LANE * (j + 1)])
                if masked:
                    s = jnp.where(rows >= cols, s, NEG_INF)
                m_new = jnp.maximum(m, jnp.max(s, axis=1, keepdims=True))
                alpha = jnp.exp2(m - m_new)
                p = jnp.exp2(s - m_new)
                acc = alpha * acc + _dot(_mx(p), vj)
                out.append((m_new, acc))
            return tuple(out)

        for qi in range(nq):
            r = slice(qi * tq, (qi + 1) * tq)
            qs = [q_ref[0, r, LANE * j:LANE * (j + 1)] for j in range(NH)]
            init = (jnp.full((tq, 1), NEG_INF, F32), jnp.zeros((tq, LANE), F32))
            carry = (init,) * NH
            for kt in range(qi):
                carry = step(qs, kt, carry, False)
            carry = step(qs, qi, carry, True)
            for t in range(HPS):
                (m0, a0), (m1, a1) = carry[2 * t], carry[2 * t + 1]
                l0 = jnp.where(low, pltpu.roll(a0, 64, 1), a0)
                l1 = jnp.where(low, a1, pltpu.roll(a1, 64, 1))
                o_ref[0, r, LANE * t:LANE * (t + 1)] = jnp.where(low, a0 / l0, a1 / l1)
                lse_ref[0, 2 * t, r, :] = m0 + jnp.log2(l0)
                lse_ref[0, 2 * t + 1, r, :] = m1 + jnp.log2(l1)

    grp = lambda w: pl.BlockSpec((1, S, w), lambda b, hp: (b, 0, hp))
    return pl.pallas_call(
        body, name="mla_fwd", grid=(B, MLA_HEADS // NH),
        out_shape=[jax.ShapeDtypeStruct((B, S, 512), F32), jax.ShapeDtypeStruct((B, MLA_HEADS, S, LANE), F32)],
        in_specs=[grp(NH * LANE), grp(NH * LANE), grp(HPS * LANE)],
        out_specs=[grp(HPS * LANE), pl.BlockSpec((1, NH, S, LANE), lambda b, hp: (b, hp, 0, 0))],
        compiler_params=_cparams(("arbitrary", "arbitrary")),
    )(q3, k3, v3)


def _mla_bwd(q3, k3, v3, o3, do3, lse, tq, tk, hps=2):
    B, S, _ = q3.shape
    nq, nk = S // tq, S // tk
    nh = 2 * hps
    assert tk % tq == 0

    def body(q_ref, k_ref, v_ref, o_ref, do_ref, lse_ref, dq_ref, dk_ref, dv_ref, dkt_ref, dvt_ref):
        dkt_ref[...] = jnp.zeros_like(dkt_ref)
        dvt_ref[...] = jnp.zeros_like(dvt_ref)
        rows = lax.broadcasted_iota(jnp.int32, (tq, tk), 0)
        cols = lax.broadcasted_iota(jnp.int32, (tq, tk), 1)
        lane = _lane_iota((tq, LANE))

        def q_tile(qi, _):
            r = pl.ds(pl.multiple_of(qi * tq, tq), tq)
            heads = []
            for j in range(nh):
                lanes = slice(LANE * j, LANE * (j + 1))
                pair = slice(LANE * (j // 2), LANE * (j // 2 + 1))
                do = jnp.where((lane < 64) if j % 2 == 0 else (lane >= 64), do_ref[0, r, pair], 0.0)
                q = q_ref[0, r, lanes]
                heads.append((lanes, pair, q, _mx(q.astype(F32).T), _mx(do), _mx(do.T),
                              jnp.sum(do * o_ref[0, r, pair], axis=1, keepdims=True),
                              jnp.concatenate([lse_ref[0, j, r, :]] * (tk // LANE), axis=1)))
            n_full = (qi * tq) // tk

            def k_tile(kt, dqs, masked):
                kr = pl.ds(pl.multiple_of(kt * tk, tk), tk)
                out = []
                dvt = [None] * hps
                for j, (lanes, pair, q, qt, dob, dot_, dcol, lse_c) in enumerate(heads):
                    k = k_ref[0, kr, lanes]
                    s = _dot_nt(q, k)
                    if masked:
                        s = jnp.where(rows + qi * tq >= cols + kt * tk, s, NEG_INF)
                    p = jnp.exp2(s - lse_c)
                    dp = _dot_nt(dob, v_ref[0, kr, pair])
                    dsb = _mx(p * (dp - dcol))
                    dkt_ref[j, kt] += _dot(qt, dsb)
                    pv = _dot(dot_, _mx(p))
                    dvt[j // 2] = pv if dvt[j // 2] is None else dvt[j // 2] + pv
                    out.append(dqs[j] + _dot(dsb, k))
                for t in range(hps):
                    dvt_ref[t, kt] += dvt[t]
                return tuple(out)

            dqs = (jnp.zeros((tq, LANE), F32),) * nh
            for kt in range(n_full):
                dqs = k_tile(kt, dqs, False)
            dqs = k_tile(n_full, dqs, True)
            for j in range(nh):
                dq_ref[0, r, heads[j][0]] = (MLA_SCALE * dqs[j]).astype(dq_ref.dtype)
            return 0

        for qi in range(nq):
            q_tile(qi, 0)

        def flush(kt, _):
            kr = pl.ds(pl.multiple_of(kt * tk, tk), tk)
            for j in range(nh):
                dk_ref[0, kr, LANE * j:LANE * (j + 1)] = ((1.0 / LOG2E) * dkt_ref[j, kt].T).astype(dk_ref.dtype)
            for t in range(hps):
                dv_ref[0, kr, LANE * t:LANE * (t + 1)] = dvt_ref[t, kt].T.astype(dv_ref.dtype)
            return 0

        lax.fori_loop(0, nk, flush, 0)

    grp = lambda w: pl.BlockSpec((1, S, w), lambda b, hp: (b, 0, hp))
    return pl.pallas_call(
        body, name="mla_bwd", grid=(B, MLA_HEADS // nh),
        out_shape=[jax.ShapeDtypeStruct((B, S, 1024), MLA_GRAD_DTYPE), jax.ShapeDtypeStruct((B, S, 1024), MLA_GRAD_DTYPE),
                   jax.ShapeDtypeStruct((B, S, 512), MLA_GRAD_DTYPE)],
        in_specs=[grp(nh * LANE), grp(nh * LANE), grp(hps * LANE), grp(hps * LANE), grp(hps * LANE),
                  pl.BlockSpec((1, nh, S, LANE), lambda b, hp: (b, hp, 0, 0))],
        out_specs=[grp(nh * LANE), grp(nh * LANE), grp(hps * LANE)],
        scratch_shapes=[pltpu.VMEM((nh, nk, LANE, tk), F32), pltpu.VMEM((hps, nk, LANE, tk), F32)],
        compiler_params=_cparams(("arbitrary", "arbitrary")),
    )(q3, k3, v3, o3, do3, lse)


def _swa_consts(sink_ref):
    W = WINDOW
    row = lax.broadcasted_iota(jnp.int32, (4 * W, LANE), 0)
    out = []
    for g in range(SWA_KV_HEADS):
        slope = jnp.zeros((4 * W, LANE), F32)
        sink = jnp.zeros((4 * W, LANE), F32)
        for p in range(4):
            h = p + 4 * g
            here = jnp.logical_and(row >= W * p, row < W * (p + 1))
            slope = jnp.where(here, float(LOG2E * 2.0 ** (-8.0 * (h + 1) / SWA_HEADS)), slope)
            sink = jnp.where(here, LOG2E * sink_ref[:, h:h + 1], sink)
        out.append((jnp.concatenate([slope, slope], axis=1), sink))
    return out


def _wide(col):
    return jnp.concatenate([col, col], axis=1)


def _swa_block(q_ref, k_ref, v_ref, pb_ref, pr_ref, n, i, ij):
    W = WINDOW
    kb = jnp.maximum(n - 1, 0)
    r = pl.ds(pl.multiple_of(i * W, W), W)
    kr = pl.ds(pl.multiple_of(kb * W, W), 2 * W)
    q4, k2, v2 = q_ref[0, r, :], k_ref[0, kr, :], v_ref[0, kr, :]
    pq = _wide(pb_ref[0, r, :])
    pk = jnp.concatenate([pr_ref[0, pl.ds(kb, 1), :], pr_ref[0, pl.ds(kb + 1, 1), :]], axis=1)
    rel = ij + (n - kb) * W
    mask = jnp.where(jnp.logical_and(rel >= 0, rel < W), 0.0, NEG_INF)
    dist4 = jnp.concatenate([pq - pk] * 4, axis=0)
    mask4 = jnp.concatenate([mask] * 4, axis=0)
    return r, kb, kr, q4, k2, v2, dist4, mask4


def _swa_stack(x4, g, dtype):
    lane = _lane_iota((WINDOW, LANE))
    mine = (lane < 64) if g == 0 else (lane >= 64)
    return jnp.concatenate([jnp.where(mine, x4[:, LANE * p:LANE * (p + 1)], 0).astype(dtype) for p in range(4)], axis=0)


def _swa_unstack(ref, r, lo, hi, scale=None):
    W = WINDOW
    low = _lane_iota((W, LANE)) < 64
    for p in range(4):
        t = jnp.where(low, lo[W * p:W * (p + 1)], hi[W * p:W * (p + 1)])
        ref[0, r, LANE * p:LANE * (p + 1)] = t if scale is None else scale * t


def _swa_fwd(qs3, ks3, vs3, posb, posr, sinks):
    B, S, _ = qs3.shape
    W = WINDOW
    nb = S // W
    nh = SWA_SEQ_SPLIT
    nbh = nb // nh

    def body(q_ref, k_ref, v_ref, pb_ref, pr_ref, sink_ref, o_ref, lse_ref):
        ij = lax.broadcasted_iota(jnp.int32, (W, 2 * W), 0) - lax.broadcasted_iota(jnp.int32, (W, 2 * W), 1)
        consts = _swa_consts(sink_ref)
        n0 = pl.program_id(1) * nbh

        def blk(i, _):
            n = n0 + i
            r, _, _, q4, k2, v2, dist4, mask4 = _swa_block(q_ref, k_ref, v_ref, pb_ref, pr_ref, n, i, ij)
            o_g = []
            for g, (slope, sink) in enumerate(consts):
                s = _dot_nt(_swa_stack(q4, g, q4.dtype), k2) - slope * dist4 + mask4
                m = jnp.maximum(jnp.max(s, axis=1, keepdims=True), sink)
                e = jnp.exp2(s - _wide(m))
                l = jnp.sum(e, axis=1, keepdims=True) + jnp.exp2(sink - m)
                o_g.append(_dot(_mx(e), v2) * (1.0 / l))
                lse_ref[0, i, g] = m + jnp.log2(l)
            _swa_unstack(o_ref, r, o_g[0], o_g[1])
            return 0

        for i in range(nbh):
            blk(i, 0)

    seq = lambda w: pl.BlockSpec((1, S, w), lambda b, h: (b, 0, 0))
    part = lambda w: pl.BlockSpec((1, S // nh, w), lambda b, h: (b, h, 0))
    lse_spec = pl.BlockSpec((1, nbh, 2, 4 * W, LANE), lambda b, h: (b, h, 0, 0, 0))
    return pl.pallas_call(
        body, name="swa_fwd", grid=(B, nh),
        out_shape=[jax.ShapeDtypeStruct((B, S, 512), F32), jax.ShapeDtypeStruct((B, nb, 2, 4 * W, LANE), F32)],
        in_specs=[part(512), seq(LANE), seq(LANE), part(LANE), pl.BlockSpec((1, nb, W), lambda b, h: (b, 0, 0)),
                  pl.BlockSpec((1, LANE), lambda b, h: (0, 0))],
        out_specs=[part(512), lse_spec],
        compiler_params=_cparams(("arbitrary", "arbitrary")),
    )(qs3, ks3, vs3, posb, posr, sinks)


def _swa_bwd(qs3, ks3, vs3, posb, posr, sinks, os3, do3, lse):
    B, S, _ = qs3.shape
    W = WINDOW
    nb = S // W
    nh = SWA_SEQ_SPLIT
    nbh = nb // nh
    assert nbh % 2 == 0 and nbh * nh * W == S

    def body(q_ref, k_ref, v_ref, pb_ref, pr_ref, sink_ref, o_ref, do_ref, lse_ref, dq_ref, dk_ref, dv_ref, dsink_ref,
             dkt_ref, dvt_ref):
        lane1 = _lane_iota((1, LANE))
        ij = lax.broadcasted_iota(jnp.int32, (W, 2 * W), 0) - lax.broadcasted_iota(jnp.int32, (W, 2 * W), 1)
        consts = _swa_consts(sink_ref)
        hh = pl.program_id(1)
        n0 = hh * nbh

        @pl.when(hh == 0)
        def _():
            dkt_ref[...] = jnp.zeros_like(dkt_ref)
            dvt_ref[...] = jnp.zeros_like(dvt_ref)

        @pl.when(jnp.logical_and(pl.program_id(0) == 0, hh == 0))
        def _():
            dsink_ref[...] = jnp.zeros_like(dsink_ref)

        def blk(i, dsink):
            n = n0 + i
            r, kb, _, q4, k2, v2, dist4, mask4 = _swa_block(q_ref, k_ref, v_ref, pb_ref, pr_ref, n, i, ij)
            o4, do4 = o_ref[0, r, :], do_ref[0, r, :]
            dq_g = []
            dkt = jnp.zeros((LANE, 2 * W), F32)
            dvt = jnp.zeros((LANE, 2 * W), F32)
            for g, (slope, sink) in enumerate(consts):
                q_st = _swa_stack(q4, g, F32)
                do_st = _swa_stack(do4, g, F32)
                dcol = jnp.sum(do_st * _swa_stack(o4, g, F32), axis=1, keepdims=True)
                qst, dob = _mx(q_st), _mx(do_st)
                lse_c = lse_ref[0, i, g]
                pr = jnp.exp2(_dot_nt(qst, k2) - slope * dist4 + mask4 - _wide(lse_c))
                dsb = _mx(pr * (_dot_nt(dob, v2) - dcol))
                psd = jnp.exp2(sink - lse_c)[:, 0:1] * dcol
                for p in range(4):
                    dsink = dsink - jnp.where(lane1 == p + 4 * g,
                                              jnp.sum(psd[W * p:W * (p + 1)], axis=0, keepdims=True), 0.0)
                dq_g.append(_dot(dsb, k2))
                dkt = dkt + _dot(_mx(q_st.T), dsb)
                dvt = dvt + _dot(_mx(do_st.T), _mx(pr))
            _swa_unstack(dq_ref, r, dq_g[0], dq_g[1], SWA_SCALE)
            dkt_ref[kb] += dkt[:, 0:W]
            dkt_ref[kb + 1] += dkt[:, W:2 * W]
            dvt_ref[kb] += dvt[:, 0:W]
            dvt_ref[kb + 1] += dvt[:, W:2 * W]
            return dsink

        dsink = jnp.zeros((1, LANE), F32)
        for i in range(nbh):
            dsink = blk(i, dsink)
        dsink_ref[...] += dsink

        @pl.when(hh == nh - 1)
        def _():
            def flush(n, _):
                r = pl.ds(pl.multiple_of(n * W, W), W)
                dk_ref[0, r, :] = (1.0 / LOG2E) * dkt_ref[n].T
                dv_ref[0, r, :] = dvt_ref[n].T
                return 0

            lax.fori_loop(0, nb, flush, 0)

    seq = lambda w: pl.BlockSpec((1, S, w), lambda b, h: (b, 0, 0))
    part = lambda w: pl.BlockSpec((1, S // nh, w), lambda b, h: (b, h, 0))
    return pl.pallas_call(
        body, name="swa_bwd", grid=(B, nh),
        out_shape=[jax.ShapeDtypeStruct((B, S, 512), F32), jax.ShapeDtypeStruct((B, S, LANE), F32),
                   jax.ShapeDtypeStruct((B, S, LANE), F32), jax.ShapeDtypeStruct((1, LANE), F32)],
        in_specs=[part(512), seq(LANE), seq(LANE), part(LANE), pl.BlockSpec((1, nb, W), lambda b, h: (b, 0, 0)),
                  pl.BlockSpec((1, LANE), lambda b, h: (0, 0)), part(512),
                  pl.BlockSpec((1, S // nh, 512), lambda b, h: (b, h, 1)),
                  pl.BlockSpec((1, nbh, 2, 4 * W, LANE), lambda b, h: (b, h, 0, 0, 0))],
        out_specs=[part(512), seq(LANE), seq(LANE), pl.BlockSpec((1, LANE), lambda b, h: (0, 0))],
        scratch_shapes=[pltpu.VMEM((nb, LANE, W), F32), pltpu.VMEM((nb, LANE, W), F32)],
        compiler_params=_cparams(("arbitrary", "arbitrary")),
    )(qs3, ks3, vs3, posb, posr, sinks, os3, do3, lse)


def _post(om, osw, g, w_out, x2, gate, fg, tgt, S, ts):
    T, D = x2.shape
    nsb = S // ts

    def body(om_ref, os_ref, g_ref, w_ref, x_ref, gate_ref, fg_ref, t_ref,
             dx2_ref, do_ref, dg_ref, loss_ref, dfg_ref, dgate_ref, dw_ref):
        i = pl.program_id(0)
        gv = g_ref[...]
        sg = 1.0 / (1.0 + jnp.exp(-gv))
        silu = gv * sg
        o = jnp.concatenate([om_ref[...], os_ref[...]], axis=1)
        ab = _mx(o * silu)
        y = _dot(ab, w_ref[...])
        gate = gate_ref[0]
        xo = x_ref[...] + gate * y
        r2 = lax.rsqrt(jnp.mean(xo * xo, axis=-1, keepdims=True) + EPS)
        xh = xo * r2
        fg = fg_ref[...]
        diff = xh * fg - t_ref[...]
        sq = jnp.sum(diff * diff, axis=0, keepdims=True)
        part = sq[:, 0:LANE]
        for t in range(1, D // LANE):
            part = part + sq[:, LANE * t:LANE * (t + 1)]
        dout = diff * (1.0 / D)
        dxh = dout * fg
        dx2 = r2 * (dxh - xh * jnp.mean(dxh * xh, axis=-1, keepdims=True))
        dx2_ref[...] = dx2
        dyb = _mx(dx2 * gate)
        da = _dot_nt(dyb, w_ref[...])
        do_ref[...] = da * silu
        dg_ref[...] = _mx(da * o * (sg * (1.0 + gv * (1.0 - sg))))

        @pl.when(i == 0)
        def _():
            loss_ref[...] = jnp.zeros_like(loss_ref)
            dfg_ref[...] = jnp.zeros_like(dfg_ref)
            dw_ref[...] = jnp.zeros_like(dw_ref)

        @pl.when(i % nsb == 0)
        def _():
            dgate_ref[...] = jnp.zeros_like(dgate_ref)

        loss_ref[...] += (0.5 / D) * part
        dfg_ref[...] += jnp.sum(dout * xh, axis=0, keepdims=True)
        dgate_ref[0] += jnp.sum(dx2 * y, axis=0, keepdims=True)
        dw_ref[...] += _dot_tn(ab, dyb)

        @pl.when(i == T // ts - 1)
        def _():
            chunks = [dw_ref[512 + 64 * PAIR_INV[h]:512 + 64 * (PAIR_INV[h] + 1), :] for h in range(SWA_HEADS)]
            for h in range(SWA_HEADS):
                dw_ref[512 + 64 * h:512 + 64 * (h + 1), :] = chunks[h]

    row = lambda w: pl.BlockSpec((ts, w), lambda i: (i, 0))
    full = lambda a: pl.BlockSpec(a.shape, lambda i: (0,) * a.ndim)
    per_b = pl.BlockSpec((1, 1, D), lambda i: (i // nsb, 0, 0))
    return pl.pallas_call(
        body, name="post", grid=(T // ts,),
        out_shape=[jax.ShapeDtypeStruct((T, D), F32), jax.ShapeDtypeStruct((T, 1024), F32),
                   jax.ShapeDtypeStruct((T, 1024), _MXU_DTYPE), jax.ShapeDtypeStruct((1, LANE), F32),
                   jax.ShapeDtypeStruct((1, D), F32), jax.ShapeDtypeStruct(gate.shape, F32),
                   jax.ShapeDtypeStruct(w_out.shape, F32)],
        in_specs=[row(512), row(512), row(1024), full(w_out), row(D), per_b, full(fg), row(D)],
        out_specs=[row(D), row(1024), row(1024),
                   pl.BlockSpec((1, LANE), lambda i: (0, 0)), pl.BlockSpec((1, D), lambda i: (0, 0)), per_b,
                   full(w_out)],
        compiler_params=_cparams(("arbitrary",)),
    )(om, osw, g, w_out, x2, gate, fg, tgt)


def _pre_bwd(dq, dk, dv, dqs, dks, dvs, dg, zq, zkv, ql, kvl, x2, dx2, scale, ng, w_in, gq, gkv, w_uq, w_uk, w_uv,
             rope, S, ts):
    T, D = x2.shape
    nsb = S // ts
    WQ = MLA_HEADS * LANE

    def body(dq_ref, dk_ref, dv_ref, dqs_ref, dks_ref, dvs_ref, dg_ref, zq_ref, zkv_ref, ql_ref, kvl_ref, x_ref, dx2_ref,
             sc_ref,
             ng_ref, w_ref, gq_ref, gkv_ref, wuq_ref, wuk_ref, wuv_ref, cd_ref, sd_ref,
             gx_ref, dz_ref, dgq_ref, dgkv_ref, dng_ref, dsh_ref, dsc_ref, dwuq_ref, dwuk_ref, dwuv_ref):
        i = pl.program_id(0)

        @pl.when(i == 0)
        def _():
            dgq_ref[...] = jnp.zeros_like(dgq_ref)
            dgkv_ref[...] = jnp.zeros_like(dgkv_ref)
            dng_ref[...] = jnp.zeros_like(dng_ref)
            dwuq_ref[...] = jnp.zeros_like(dwuq_ref)
            dwuk_ref[...] = jnp.zeros_like(dwuk_ref)
            dwuv_ref[...] = jnp.zeros_like(dwuv_ref)

        @pl.when(i % nsb == 0)
        def _():
            dsh_ref[...] = jnp.zeros_like(dsh_ref)
            dsc_ref[...] = jnp.zeros_like(dsc_ref)

        def norm_bwd(z, dy, gain):
            r = lax.rsqrt(jnp.mean(z * z, axis=-1, keepdims=True) + EPS)
            zh = z * r
            dzh = dy * gain
            return r * (dzh - zh * jnp.mean(dzh * zh, axis=-1, keepdims=True)), jnp.sum(dy * zh, axis=0, keepdims=True)

        tables = _rope_split(cd_ref[...], sd_ref[...])
        ng = ng_ref[...]
        sc1 = 1.0 + sc_ref[0]

        def rows_chain(rs):
            cq, ck, sa, sb = (t[rs] for t in tables)
            dqr = dq_ref[rs, :].astype(F32)
            dqb = _mx(dqr * _tile_heads(cq) + pltpu.roll(dqr * _tile_heads(sa), 16, 1)
                      + pltpu.roll(dqr * _tile_heads(sb), WQ - 16, 1))
            p_uq = _dot_tn(ql_ref[rs, :], dqb)
            dzq, dgq = norm_bwd(zq_ref[rs, :], _dot_nt(dqb, wuq_ref[...]), gq_ref[...])
            dkr = dk_ref[rs, :].astype(F32)
            dkb = _mx(dkr)
            p_uk = _dot_tn(kvl_ref[rs, :], dkb)
            dvb = _mx(dv_ref[rs, :])
            p_uv = _dot_tn(kvl_ref[rs, :], dvb)
            dzkv, dgkv = norm_bwd(zkv_ref[rs, :], _dot_nt(dkb, wuk_ref[...]) + _dot_nt(dvb, wuv_ref[...]), gkv_ref[...])
            dkpe = dkr[:, 0:LANE]
            for h in range(1, MLA_HEADS):
                dkpe = dkpe + dkr[:, LANE * h:LANE * (h + 1)]
            dkro = dkpe * ck + pltpu.roll(dkpe * sa, 16, 1) + pltpu.roll(dkpe * sb, LANE - 16, 1)
            dz_ref[rs, 0:384] = _mx(dzq)
            dz_ref[rs, 384:640] = _mx(dzkv)
            dz_ref[rs, 640:768] = _mx(dkro)
            dz_ref[rs, 768:1280] = dg_ref[rs, 0:512]
            dz_ref[rs, 1280:1792] = _mx(dqs_ref[rs, :])
            dz_ref[rs, 1792:1920] = _mx(dks_ref[rs, :])
            dz_ref[rs, 1920:2048] = _mx(dvs_ref[rs, :])
            dz_ref[rs, 2048:2560] = dg_ref[rs, 512:1024]
            dh = _dot(dz_ref[rs, :], w_ref[...])
            x = x_ref[rs, :]
            r1 = lax.rsqrt(jnp.mean(x * x, axis=-1, keepdims=True) + EPS)
            xn = x * r1
            dxn = dh * ng * sc1
            gx_ref[rs, :] = dx2_ref[rs, :] + r1 * (dxn - xn * jnp.mean(dxn * xn, axis=-1, keepdims=True))
            return (p_uq, p_uk, p_uv, dgq, dgkv, jnp.sum(dh, axis=0, keepdims=True),
                    jnp.sum(dh * (xn * ng), axis=0, keepdims=True), jnp.sum(dh * xn * sc1, axis=0, keepdims=True))

        hr = ts // 2
        parts = [rows_chain(slice(hr * t, hr * (t + 1))) for t in range(2)]
        p_uq, p_uk, p_uv, dgq, dgkv, dsh, dsc, dng = (a + b for a, b in zip(*parts))
        dwuq_ref[...] += p_uq
        dwuk_ref[...] += p_uk
        dwuv_ref[...] += p_uv
        dgq_ref[...] += dgq
        dgkv_ref[...] += dgkv
        dsh_ref[0] += dsh
        dsc_ref[0] += dsc
        dng_ref[...] += dng

    row = lambda w: pl.BlockSpec((ts, w), lambda i: (i, 0))
    full = lambda a: pl.BlockSpec(a.shape, lambda i: (0,) * a.ndim, pipeline_mode=pl.Buffered(1))
    per_b = pl.BlockSpec((1, 1, D), lambda i: (i // nsb, 0, 0))
    dense = pl.BlockSpec((ts // 8, LANE), lambda i: (i, 0))
    vec = lambda w: pl.BlockSpec((1, w), lambda i: (0, 0))
    return pl.pallas_call(
        body, name="pre_bwd", grid=(T // ts,),
        out_shape=[jax.ShapeDtypeStruct((T, D), F32), jax.ShapeDtypeStruct((T, D_IN_PAD), _MXU_DTYPE), jax.ShapeDtypeStruct((1, 384), F32),
                   jax.ShapeDtypeStruct((1, 256), F32), jax.ShapeDtypeStruct((1, D), F32),
                   jax.ShapeDtypeStruct(scale.shape, F32), jax.ShapeDtypeStruct(scale.shape, F32),
                   jax.ShapeDtypeStruct(w_uq.shape, F32), jax.ShapeDtypeStruct(w_uk.shape, F32),
                   jax.ShapeDtypeStruct(w_uv.shape, F32)],
        in_specs=[row(WQ), row(WQ), row(512), row(512), row(LANE), row(LANE), row(1024), row(384), row(256), row(384),
                  row(256), row(D), row(D), per_b, full(ng), full(w_in), full(gq), full(gkv), full(w_uq), full(w_uk), full(w_uv),
                  dense, dense],
        out_specs=[row(D), row(D_IN_PAD), vec(384), vec(256), vec(D), per_b, per_b, full(w_uq), full(w_uk), full(w_uv)],
        compiler_params=_cparams(("arbitrary",)),
    )(dq, dk, dv, dqs, dks, dvs, dg, zq, zkv, ql, kvl, x2, dx2, scale, ng, w_in, gq, gkv, w_uq, w_uk, w_uv, *rope)


def _w_in_row_runs():
    runs = [(0, 0, 640), (640, 704, 32), (672, 768, 512)]
    runs += [(1184 + 64 * h, 1280 + 64 * PAIR_INV[h], 64) for h in range(8)]
    runs += [(1696, 1792, 256)]
    runs += [(1952 + 64 * h, 2048 + 64 * PAIR_INV[h], 64) for h in range(8)]
    return runs


W_IN_SHARD = D_IN // N_DEV
W_IN_SLOT = 320


def _dw_in_t(dz, hb, tn, tk):
    T, M = dz.shape
    N = hb.shape[1]
    nk = T // tk

    def body(a_ref, b_ref, o_ref, ob_ref, acc_ref):
        k = pl.program_id(1)

        @pl.when(k == 0)
        def _():
            acc_ref[...] = jnp.zeros_like(acc_ref)

        acc_ref[...] += _dot_tn(a_ref[...], b_ref[...])

        @pl.when(k == nk - 1)
        def _():
            for d in range(N_DEV):
                lo, hi = W_IN_SHARD * d, W_IN_SHARD * (d + 1)
                for nat, pad, size in _w_in_row_runs():
                    a, b = max(nat, lo), min(nat + size, hi)
                    if a < b:
                        piece = acc_ref[pad + a - nat:pad + b - nat, :]
                        o_ref[d, a - lo:b - lo, :] = piece
                        ob_ref[d, a - lo:b - lo, :] = piece.astype(ob_ref.dtype)
                o_ref[d, W_IN_SHARD:W_IN_SLOT, :] = jnp.zeros((W_IN_SLOT - W_IN_SHARD, tn), F32)
                ob_ref[d, W_IN_SHARD:W_IN_SLOT, :] = jnp.zeros((W_IN_SLOT - W_IN_SHARD, tn), ob_ref.dtype)

    slots = pl.BlockSpec((N_DEV, W_IN_SLOT, tn), lambda j, k: (0, 0, j))
    return pl.pallas_call(
        body, name="dw_in", grid=(N // tn, nk),
        out_shape=[jax.ShapeDtypeStruct((N_DEV, W_IN_SLOT, N), F32), jax.ShapeDtypeStruct((N_DEV, W_IN_SLOT, N), jnp.bfloat16)],
        in_specs=[pl.BlockSpec((tk, M), lambda j, k: (k, 0)), pl.BlockSpec((tk, tn), lambda j, k: (k, j))],
        out_specs=[slots, slots],
        scratch_shapes=[pltpu.VMEM((M, tn), F32)],
        compiler_params=_cparams(("arbitrary", "arbitrary")),
    )(dz, hb)


def _finalize(parts_all, dmod_all, dmod_cols, c_act_all):
    nparts = parts_all.shape[-1]
    nmod = dmod_all.shape[-1]

    def body(p_ref, dm_ref, dmc_ref, c_ref, ps_ref, loss_ref, db_ref, dw_ref):
        acc = p_ref[0]
        for j in range(1, N_DEV):
            acc = acc + p_ref[j]
        ps_ref[...] = acc
        loss_ref[...] = jnp.sum(acc[:, 0:LANE], axis=1, keepdims=True)
        db = dm_ref[0:1, :]
        for j in range(1, dm_ref.shape[0]):
            db = db + dm_ref[j:j + 1, :]
        db_ref[...] = db
        dw_ref[...] = _dot_tn(_mx(_silu(c_ref[...])), _mx(dmc_ref[...]))

    return pl.pallas_call(
        body, name="finalize",
        out_shape=[jax.ShapeDtypeStruct((1, nparts), F32), jax.ShapeDtypeStruct((1, 1), F32),
                   jax.ShapeDtypeStruct((1, nmod), F32),
                   jax.ShapeDtypeStruct((c_act_all.shape[1], dmod_cols.shape[1]), F32)],
        in_specs=[_vmem()] * 4, out_specs=[_vmem()] * 4,
        compiler_params=_cparams(),
    )(parts_all, dmod_all, dmod_cols, c_act_all)


def _adamw(ws, gs, ms, vs):
    n = len(ws)

    def body(*refs):
        w_refs, g_refs, m_refs, v_refs, d_refs, nm_refs, nv_refs = (refs[t * n:(t + 1) * n] for t in range(7))
        for t in range(n):
            gv = g_refs[t][...]
            nm = ADAM_B1 * m_refs[t][...] + (1.0 - ADAM_B1) * gv
            nv = ADAM_B2 * v_refs[t][...] + (1.0 - ADAM_B2) * (gv * gv)
            m_hat = nm / (1.0 - ADAM_B1 ** ADAM_STEP)
            v_hat = nv / (1.0 - ADAM_B2 ** ADAM_STEP)
            d_refs[t][...] = -ADAM_LR * (m_hat / (jnp.sqrt(v_hat) + ADAM_EPS) + ADAM_WD * w_refs[t][...])
            nm_refs[t][...] = nm
            nv_refs[t][...] = nv

    out = pl.pallas_call(
        body, name="adamw",
        out_shape=[jax.ShapeDtypeStruct(w.shape, F32) for w in ws] * 3,
        in_specs=[_vmem()] * (4 * n), out_specs=[_vmem()] * (3 * n),
        compiler_params=_cparams(),
    )(*ws, *gs, *ms, *vs)
    return out[:n], out[n:2 * n], out[2 * n:]


def _pair_perm(a, axis, order):
    a = jnp.moveaxis(a, axis, -1)
    lead = a.shape[:-1]
    a = a.reshape(lead + (8, 64))[..., list(order), :].reshape(lead + (512,))
    return jnp.moveaxis(a, -1, axis)


def _rope_table(positions):
    T = positions.size
    inv = ROPE_THETA ** (-jnp.arange(0, MLA_ROPE, 2, dtype=F32) / MLA_ROPE)
    pos = jnp.repeat(positions.reshape(T // 8, 8)[:, ::-1].astype(F32), MLA_ROPE // 2, axis=1)
    ang = pos * jnp.tile(inv, 8)[None, :]
    return jnp.cos(ang), jnp.sin(ang)


def _local_step(x, mod, positions, ng, w_in_t, gq, gkv, w_uq, w_ukv, sinks, w_out, fg, tgt,
                ts=512, fq=512, fk=512, bq=512, bk=512):
    B, S, D = x.shape
    T = B * S
    x2 = x.reshape(T, D)
    shift, scale, gate = (mod[:, None, k * D:(k + 1) * D] for k in range(3))
    w_uq_p = jnp.pad(w_uq.reshape(Q_LORA, MLA_HEADS, 96), ((0, 0), (0, 0), (0, 32))).reshape(Q_LORA, MLA_HEADS * LANE)
    w_ukv3 = w_ukv.reshape(KV_LORA, MLA_HEADS, 128)
    w_uk_p = jnp.pad(w_ukv3[:, :, :64], ((0, 0), (0, 0), (0, 64))).reshape(KV_LORA, MLA_HEADS * LANE)
    w_uv = w_ukv3[:, :, 64:].reshape(KV_LORA, 512)
    w_out_p = jnp.concatenate([w_out[:512], _pair_perm(w_out[512:], 0, PAIR_ORDER)], axis=0)
    rope = _rope_table(positions)
    posf = positions.astype(F32)
    posb = jnp.broadcast_to(posf[:, :, None], (B, S, LANE))
    posr = posf.reshape(B, S // WINDOW, WINDOW)
    sinks_l = jnp.pad(sinks.reshape(1, SWA_HEADS), ((0, 0), (0, LANE - SWA_HEADS)))

    (hb, zq, zkv, ql, kvl, q, k, v, qs, ks, vs, g, w_in_p) = _pre_fwd(
        x2, shift, scale, ng, w_in_t, gq, gkv, w_uq_p, w_uk_p, w_uv, rope, S, ts)
    r3 = lambda a: a.reshape(B, S, a.shape[-1])
    om, lse_m = _mla_fwd(r3(q), r3(k), r3(v), fq, fk)
    osw, lse_s = _swa_fwd(r3(qs), r3(ks), r3(vs), posb, posr, sinks_l)
    dx2, do, dg, loss_v, dfg, dgate, dw_out = _post(
        om.reshape(T, 512), osw.reshape(T, 512), g, w_out_p, x2, gate, fg.reshape(1, D), tgt.reshape(T, D), S, ts)
    do3 = r3(do)
    dq, dk, dv = _mla_bwd(r3(q), r3(k), r3(v), om, do3, lse_m, bq, bk)
    dqs, dks, dvs, dsink = _swa_bwd(r3(qs), r3(ks), r3(vs), posb, posr, sinks_l, osw, do3, lse_s)
    f2 = lambda a: a.reshape(T, a.shape[-1])
    gx, dz, dgq, dgkv, dng, dsh, dsc, dw_uq_p, dw_uk_p, dw_uv = _pre_bwd(
        f2(dq), f2(dk), f2(dv), f2(dqs), f2(dks), f2(dvs), dg, zq, zkv, ql, kvl, x2, dx2, scale, ng, w_in_p, gq, gkv,
        w_uq_p, w_uk_p, w_uv, rope, S, ts)
    tk = min(T, 1024)
    dw_in_t = _dw_in_t(dz, hb, 512, tk)
    dw_uq = dw_uq_p.reshape(Q_LORA, MLA_HEADS, LANE)[:, :, :96].reshape(Q_LORA, 768)
    dw_uk = dw_uk_p.reshape(KV_LORA, MLA_HEADS, LANE)[:, :, :64]
    dw_uv = dw_uv.reshape(KV_LORA, MLA_HEADS, 64)
    dw_ukv = jnp.concatenate([dw_uk, dw_uv], axis=2).reshape(KV_LORA, 1024)
    parts = jnp.concatenate([loss_v, dfg, dng, dgq, dgkv, dsink], axis=1)
    dmod = jnp.concatenate([dsh, dsc, dgate], axis=2).reshape(B, 3 * D)
    return gx.reshape(B, S, D), dw_in_t, dw_uq, dw_ukv, dw_out, parts, dmod


def kernel(x, c, positions, w_ada, b_ada, norm_gain, w_in, q_norm_gain, kv_norm_gain, w_uq, w_ukv, swa_sinks, w_out, final_gain, loss_target, m_w_ada, m_b_ada, m_norm_gain, m_w_in, m_q_norm_gain, m_kv_norm_gain, m_w_uq, m_w_ukv, m_swa_sinks, m_w_out, m_final_gain, v_w_ada, v_b_ada, v_norm_gain, v_w_in, v_q_norm_gain, v_kv_norm_gain, v_w_uq, v_w_ukv, v_swa_sinks, v_w_out, v_final_gain):
    B, S, D = x.shape
    me = 4 * lax.axis_index("x") + 2 * lax.axis_index("y") + lax.axis_index("c")
    bf = _MXU_DTYPE

    ncol = w_ada.shape[2]
    b_cols = lax.dynamic_slice_in_dim(b_ada, me * ncol, ncol, axis=1)
    c_all, win_g, wuq_g, wukv_g, wout_g, mod_g = _all_gather(
        [c, w_in[0].T.astype(bf), w_uq[0].astype(bf), w_ukv[0].astype(bf), w_out[0].astype(bf)], "ag_weights",
        fused=((w_ada[0], b_cols), _ada_cols, jax.ShapeDtypeStruct((N_DEV, B, ncol), F32)))
    c_all = c_all.reshape(N_DEV * B, D)
    cat_cols = lambda a: jnp.transpose(a, (1, 0, 2)).reshape(a.shape[1], N_DEV * a.shape[2])
    w_in_t, w_uq_f, w_ukv_f = win_g.reshape(D_IN, D), cat_cols(wuq_g), cat_cols(wukv_g)
    w_out_f = wout_g.reshape(D, D)

    mod = lax.dynamic_index_in_dim(mod_g, me, axis=1, keepdims=False)
    mod = jnp.transpose(mod, (1, 0, 2)).reshape(B, 3 * D)

    gx, dw_in_t, dw_uq, dw_ukv, dw_out, parts, dmod = _local_step(
        x, mod, positions, norm_gain, w_in_t, q_norm_gain, kv_norm_gain, w_uq_f, w_ukv_f, swa_sinks,
        w_out_f, final_gain, loss_target)

    split_cols = lambda a: jnp.transpose(a.reshape(a.shape[0], 4, 2, a.shape[1] // N_DEV), (1, 2, 0, 3))
    split_rows = lambda a: a.reshape(4, 2, a.shape[0] // N_DEV, a.shape[1])
    slots = lambda a: a.reshape((4, 2) + a.shape[1:])
    g_w_in_t, g_w_uq, g_w_ukv, g_w_out, parts_g, dmod_g = _reduce_scatter(
        [(slots(dw_in_t[0]), slots(dw_in_t[1])), split_cols(dw_uq), split_cols(dw_ukv), split_rows(dw_out)], "rs_grads",
        gather=(parts, dmod))
    g_w_in = g_w_in_t[:W_IN_SHARD].T

    dmod_all = dmod_g.reshape(N_DEV * B, 3 * D)
    dmod_cols = lax.dynamic_slice_in_dim(dmod_all, me * ncol, ncol, axis=1)
    psum, loss, g_b_ada, g_w_ada = _finalize(parts_g, dmod_all, dmod_cols, c_all)
    loss = loss.reshape(())
    o = LANE
    g_final_gain = psum[0, o:o + D]
    g_norm_gain = psum[:, o + D:o + 2 * D]
    o += 2 * D
    g_q_norm_gain = psum[:, o:o + Q_LORA]
    g_kv_norm_gain = psum[:, o + Q_LORA:o + Q_LORA + KV_LORA]
    o += Q_LORA + KV_LORA
    g_sinks = psum[:, o:o + SWA_HEADS]

    grads = [g_w_ada, g_b_ada, g_norm_gain, g_w_in, g_q_norm_gain, g_kv_norm_gain, g_w_uq, g_w_ukv, g_sinks, g_w_out,
             g_final_gain]
    ws = [w_ada, b_ada, norm_gain, w_in, q_norm_gain, kv_norm_gain, w_uq, w_ukv, swa_sinks, w_out, final_gain]
    ms = [m_w_ada, m_b_ada, m_norm_gain, m_w_in, m_q_norm_gain, m_kv_norm_gain, m_w_uq, m_w_ukv, m_swa_sinks, m_w_out,
          m_final_gain]
    vs = [v_w_ada, v_b_ada, v_norm_gain, v_w_in, v_q_norm_gain, v_kv_norm_gain, v_w_uq, v_w_ukv, v_swa_sinks, v_w_out,
          v_final_gain]
    two_d = [(1, w.shape[0]) if w.ndim == 1 else w.shape[-2:] for w in ws]
    flat = lambda arrs: [a.reshape(s) for a, s in zip(arrs, two_d)]
    deltas, new_ms, new_vs = _adamw(flat(ws), flat(grads), flat(ms), flat(vs))
    shaped = lambda arrs: [a.reshape(w.shape) for a, w in zip(arrs, ws)]
    return (loss, gx, *shaped(grads), *shaped(deltas), *shaped(new_ms), *shaped(new_vs))
```

```python
import functools

import jax
import jax.numpy as jnp
from jax import lax
from jax.experimental import pallas as pl
from jax.experimental.pallas import tpu as pltpu

F32 = jnp.float32
_MXU_DTYPE = jnp.bfloat16
MLA_GRAD_DTYPE = jnp.bfloat16

N_DEV = 8
MLA_HEADS = 8
MLA_NOPE = 64
MLA_ROPE = 32
Q_LORA = 384
KV_LORA = 256
SWA_HEADS = 8
SWA_KV_HEADS = 2
SWA_HEAD_DIM = 64
WINDOW = 128
ROPE_THETA = 10000.0
EPS = 1e-6
MLA_SCALE = float((MLA_NOPE + MLA_ROPE) ** -0.5)
SWA_SCALE = float(SWA_HEAD_DIM ** -0.5)
LOG2E = 1.4426950408889634
MLA_QSCALE = MLA_SCALE * LOG2E
D_IN = 2464
D_IN_PAD = 2560
PAIR_ORDER = (0, 4, 1, 5, 2, 6, 3, 7)
PAIR_INV = (0, 2, 4, 6, 1, 3, 5, 7)

ADAM_LR = 0.001
ADAM_B1 = 0.9
ADAM_B2 = 0.999
ADAM_EPS = 1e-08
ADAM_WD = 0.01
ADAM_STEP = 10

LANE = 128
VMEM_LIMIT = 56 * 1024 * 1024

MESH = pl.DeviceIdType.MESH
NEG_INF = float("-inf")
SWA_SEQ_SPLIT = 2


def _mx(a):
    return a.astype(_MXU_DTYPE)


def _dot(a, b):
    return jnp.dot(a, b, preferred_element_type=F32)


def _dot_nt(a, b):
    return lax.dot_general(a, b, (((1,), (1,)), ((), ())), preferred_element_type=F32)


def _dot_tn(a, b):
    return lax.dot_general(a, b, (((0,), (0,)), ((), ())), preferred_element_type=F32)


def _cparams(sem=None):
    return pltpu.CompilerParams(dimension_semantics=sem, vmem_limit_bytes=VMEM_LIMIT)


def _vmem():
    return pl.BlockSpec(memory_space=pltpu.VMEM)


def _lane_iota(shape):
    return lax.broadcasted_iota(jnp.int32, shape, len(shape) - 1)


def _gather_program(srcs, outs, send_sems, recv_sems, local_sems):
    x, y, c = lax.axis_index("x"), lax.axis_index("y"), lax.axis_index("c")
    me, sibling = (x, y, c), (x, y, 1 - c)
    chips = [(1 - x, y), (x, 1 - y), (1 - x, 1 - y)]

    def slot(a, dev):
        return outs[a].at[4 * dev[0] + 2 * dev[1] + dev[2]]

    def copy(a, k, block, to, src=None):
        return pltpu.make_async_remote_copy(
            src_ref=slot(a, block) if src is None else src, dst_ref=slot(a, block),
            send_sem=send_sems.at[7 * a + k], recv_sem=recv_sems.at[7 * a + k],
            device_id=to, device_id_type=MESH)

    def local(a):
        return pltpu.make_async_copy(srcs[a], slot(a, me), local_sems.at[a])

    def start(a):
        local(a).start()
        cps = [copy(a, 0, me, sibling, src=srcs[a])]
        cps += [copy(a, 1 + j, me, (*chip, c), src=srcs[a]) for j, chip in enumerate(chips)]
        for cp in cps:
            cp.start()
        return cps

    def finish(group):
        cps = []
        for j, chip in enumerate(chips):
            for a in group:
                copy(a, 1 + j, (*chip, c), me).wait_recv()
                cp = copy(a, 4 + j, (*chip, c), sibling)
                cp.start()
                cps.append(cp)
        for a in group:
            copy(a, 0, sibling, me).wait_recv()
            for j, chip in enumerate(chips):
                copy(a, 4 + j, (*chip, 1 - c), me).wait_recv()
            local(a).wait()
        return cps

    return start, finish


def _gather_sems(m):
    return [pltpu.SemaphoreType.DMA((7 * m,)), pltpu.SemaphoreType.DMA((7 * m,)), pltpu.SemaphoreType.DMA((m,))]


def _all_gather(arrs, name, fused=None):
    n = len(arrs)
    extra, fn, piece = fused if fused else ((), None, None)
    ne, m = len(extra), n + (1 if fused else 0)

    def body(*refs):
        ins, ex, outs = refs[:n], refs[n:n + ne], refs[n + ne:n + ne + m]
        rest = refs[n + ne + m:]
        srcs = list(ins) + ([rest[0]] if fused else [])
        start, finish = _gather_program(srcs, outs, *rest[-3:])
        pending = []
        for a in range(n):
            pending += start(a)
        if fused:
            pending += finish([0])
            fn(srcs[n], outs[0], *ex)
            pending += start(n)
            pending += finish([n] + list(range(1, n)))
        else:
            pending += finish(list(range(n)))
        for cp in pending:
            cp.wait_send()

    out_shape = [jax.ShapeDtypeStruct((N_DEV,) + a.shape, a.dtype) for a in arrs]
    scratch = []
    if fused:
        out_shape.append(jax.ShapeDtypeStruct((N_DEV,) + piece.shape, piece.dtype))
        scratch.append(pltpu.VMEM(piece.shape, piece.dtype))
    return pl.pallas_call(
        body, name=name, out_shape=out_shape,
        in_specs=[_vmem()] * (n + ne), out_specs=[_vmem()] * m,
        scratch_shapes=scratch + _gather_sems(m),
        compiler_params=pltpu.CompilerParams(vmem_limit_bytes=VMEM_LIMIT),
    )(*arrs, *extra)


def _reduce_scatter(arrs, name, gather=()):
    n, g = len(arrs), len(gather)
    halves = [a[1] if isinstance(a, tuple) else a.astype(jnp.bfloat16) for a in arrs]
    arrs = [a[0] if isinstance(a, tuple) else a for a in arrs]

    def body(*refs):
        xs, xbs, gins = refs[:n], refs[n:2 * n], refs[2 * n:2 * n + g]
        outs, gouts = refs[2 * n + g:3 * n + g], refs[3 * n + g:3 * n + 2 * g]
        rest = refs[3 * n + 2 * g:]
        parts, recv_a, send_b, recv_b = (rest[t * n:(t + 1) * n] for t in range(4))
        send_sems, recv_sems, local_sems = rest[4 * n:4 * n + 3]
        x, y, c = lax.axis_index("x"), lax.axis_index("y"), lax.axis_index("c")
        myq = 2 * x + y
        gather_start, gather_finish = _gather_program(gins, gouts, *rest[4 * n + 3:])

        def chip(k):
            return (1 - x if k & 2 else x, 1 - y if k & 1 else y)

        def from_sibling(a):
            return pltpu.make_async_remote_copy(
                src_ref=xbs[a].at[:, 1 - c], dst_ref=recv_a[a], send_sem=send_sems.at[4 * a], recv_sem=recv_sems.at[4 * a],
                device_id=(x, y, 1 - c), device_id_type=MESH)

        def to_owner(a, k):
            qx, qy = chip(k)
            return pltpu.make_async_remote_copy(
                src_ref=send_b[a].at[2 * qx + qy], dst_ref=recv_b[a].at[myq],
                send_sem=send_sems.at[4 * a + k], recv_sem=recv_sems.at[4 * a + k],
                device_id=(qx, qy, c), device_id_type=MESH)

        mine = [pltpu.make_async_copy(xs[a].at[:, c], parts[a], local_sems.at[a]) for a in range(n)]
        first = [from_sibling(a) for a in range(n)]
        for cp in mine + first:
            cp.start()
        second = []
        for b in range(g):
            second += gather_start(b)
        for a in range(n):
            mine[a].wait()
            first[a].wait_recv()
            parts[a][...] = parts[a][...] + recv_a[a][...].astype(F32)
            send_b[a][...] = parts[a][...].astype(jnp.bfloat16)
            for k in range(1, 4):
                cp = to_owner(a, k)
                cp.start()
                second.append(cp)
        second += gather_finish(list(range(g)))
        for a in range(n):
            acc = parts[a][myq]
            for k in range(1, 4):
                to_owner(a, k).wait_recv()
                qx, qy = chip(k)
                acc = acc + recv_b[a][2 * qx + qy].astype(F32)
            outs[a][...] = acc
        for cp in first + second:
            cp.wait_send()

    quarter = lambda a, dt: pltpu.VMEM((4,) + a.shape[2:], dt)
    return pl.pallas_call(
        body, name=name,
        out_shape=[jax.ShapeDtypeStruct(a.shape[2:], F32) for a in arrs]
        + [jax.ShapeDtypeStruct((N_DEV,) + a.shape, a.dtype) for a in gather],
        in_specs=[pl.BlockSpec(memory_space=pl.ANY)] * (2 * n) + [_vmem()] * g, out_specs=[_vmem()] * (n + g),
        scratch_shapes=[quarter(a, F32) for a in arrs] + [quarter(a, jnp.bfloat16) for a in arrs] * 3
        + [pltpu.SemaphoreType.DMA((4 * n,)), pltpu.SemaphoreType.DMA((4 * n,)), pltpu.SemaphoreType.DMA((n,))]
        + _gather_sems(g),
        compiler_params=pltpu.CompilerParams(vmem_limit_bytes=VMEM_LIMIT),
    )(*arrs, *halves, *gather)


def _silu(t):
    return t * (1.0 / (1.0 + jnp.exp(-t)))


def _ada_cols(piece_ref, c_all_ref, w_ref, b_ref):
    w = _mx(w_ref[...])
    for d in range(N_DEV):
        piece_ref[d] = _dot(_mx(_silu(c_all_ref[d])), w) + b_ref[...]


def _rope_split(cd, sd):
    n = cd.shape[0]

    def expand(d):
        rep = jnp.broadcast_to(d[:, None, :], (n, 8, LANE)).reshape(8 * n, LANE)
        return pltpu.roll(rep, 0, 1, stride=16, stride_axis=0)

    c, s = expand(cd), expand(sd)
    lane = _lane_iota(c.shape)
    first = jnp.logical_and(lane >= 64, lane < 80)
    second = jnp.logical_and(lane >= 80, lane < 96)
    ck = jnp.where(first, pltpu.roll(c, 80, 1), jnp.where(second, pltpu.roll(c, 96, 1), 0.0))
    cq = jnp.where(lane < 64, 1.0, ck)
    sa = jnp.where(first, -pltpu.roll(s, 80, 1), 0.0)
    sb = jnp.where(second, pltpu.roll(s, 96, 1), 0.0)
    return cq, ck, sa, sb


def _tile_heads(t):
    return jnp.concatenate([t] * MLA_HEADS, axis=1)


def _pre_fwd(x2, shift, scale, ng, w_in, gq, gkv, w_uq, w_uk, w_uv, rope, S, ts):
    T, D = x2.shape
    nsb = S // ts
    WQ = MLA_HEADS * LANE

    def body(x_ref, sh_ref, sc_ref, ng_ref, wn_ref, gq_ref, gkv_ref, wuq_ref, wuk_ref, wuv_ref,
             cd_ref, sd_ref,
             hb_ref, zq_ref, zkv_ref, ql_ref, kvl_ref, q_ref, k_ref, v_ref, qs_ref, ks_ref, vs_ref, g_ref, wp_ref):
        @pl.when(pl.program_id(0) == 0)
        def _():
            wp_ref[640:704, :] = jnp.zeros((64, D), wp_ref.dtype)
            wp_ref[736:768, :] = jnp.zeros((32, D), wp_ref.dtype)
            for d in range(N_DEV):
                lo, hi = W_IN_SHARD * d, W_IN_SHARD * (d + 1)
                for nat, pad, size in _w_in_row_runs():
                    a, b = max(nat, lo), min(nat + size, hi)
                    if a < b:
                        wp_ref[pad + a - nat:pad + b - nat, :] = wn_ref[d, a - lo:b - lo, :]

        w_ref = wp_ref
        x = x_ref[...]
        r1 = lax.rsqrt(jnp.mean(x * x, axis=-1, keepdims=True) + EPS)
        h = ((x * r1) * ng_ref[...]) * (1.0 + sc_ref[0]) + sh_ref[0]
        hb = _mx(h)
        hb_ref[...] = hb
        zq = _dot_nt(hb, w_ref[0:384, :])
        zq_ref[...] = zq
        rq = lax.rsqrt(jnp.mean(zq * zq, axis=-1, keepdims=True) + EPS)
        ql = _mx((zq * rq) * gq_ref[...])
        ql_ref[...] = ql
        q = _dot(ql, wuq_ref[...])
        cq, ck, sa, sb = _rope_split(cd_ref[...], sd_ref[...])
        q = (q * _tile_heads(cq) + pltpu.roll(q, WQ - 16, 1) * _tile_heads(sa)
             + pltpu.roll(q, 16, 1) * _tile_heads(sb))
        q_ref[...] = _mx(q * MLA_QSCALE)
        zkv = _dot_nt(hb, w_ref[384:640, :])
        zkv_ref[...] = zkv
        rkv = lax.rsqrt(jnp.mean(zkv * zkv, axis=-1, keepdims=True) + EPS)
        kvl = _mx((zkv * rkv) * gkv_ref[...])
        kvl_ref[...] = kvl
        kr = _dot_nt(hb, w_ref[640:768, :])
        kpe = kr * ck + pltpu.roll(kr, LANE - 16, 1) * sa + pltpu.roll(kr, 16, 1) * sb
        kf = _dot(kvl, wuk_ref[...])
        k_ref[...] = _mx(kf + jnp.concatenate([kpe] * MLA_HEADS, axis=1))
        v_ref[...] = _mx(_dot(kvl, wuv_ref[...]))
        g_ref[:, 0:512] = _dot_nt(hb, w_ref[768:1280, :])
        qs_ref[...] = _mx(_dot_nt(hb, w_ref[1280:1792, :]) * (SWA_SCALE * LOG2E))
        ks_ref[...] = _mx(_dot_nt(hb, w_ref[1792:1920, :]))
        vs_ref[...] = _mx(_dot_nt(hb, w_ref[1920:2048, :]))
        g_ref[:, 512:1024] = _dot_nt(hb, w_ref[2048:2560, :])

    row = lambda w: pl.BlockSpec((ts, w), lambda i: (i, 0))
    dense = pl.BlockSpec((ts // 8, LANE), lambda i: (i, 0))
    full = lambda a: pl.BlockSpec(a.shape, lambda i: (0,) * a.ndim)
    per_b = pl.BlockSpec((1, 1, D), lambda i: (i // nsb, 0, 0))
    out_w = [(D, _MXU_DTYPE), (384, F32), (256, F32), (384, _MXU_DTYPE), (256, _MXU_DTYPE), (WQ, _MXU_DTYPE),
             (WQ, _MXU_DTYPE), (512, _MXU_DTYPE), (512, _MXU_DTYPE), (128, _MXU_DTYPE), (128, _MXU_DTYPE), (1024, F32)]
    return pl.pallas_call(
        body, name="pre_fwd", grid=(T // ts,),
        out_shape=[jax.ShapeDtypeStruct((T, w), dt) for w, dt in out_w] + [jax.ShapeDtypeStruct((D_IN_PAD, D), w_in.dtype)],
        in_specs=[row(D), per_b, per_b, full(ng), full(w_in), full(gq), full(gkv), full(w_uq), full(w_uk), full(w_uv),
                  dense, dense],
        out_specs=[row(w) for w, _ in out_w] + [pl.BlockSpec((D_IN_PAD, D), lambda i: (0, 0))],
        compiler_params=_cparams(("arbitrary",)),
    )(x2, shift, scale, ng, w_in, gq, gkv, w_uq, w_uk, w_uv, *rope)


def _mla_fwd(q3, k3, v3, tq, tk):
    B, S, _ = q3.shape
    nq = S // tq
    assert tq == tk
    HPS = 2
    NH = 2 * HPS

    def body(q_ref, k_ref, v_ref, o_ref, lse_ref):
        rows = lax.broadcasted_iota(jnp.int32, (tq, tk), 0)
        cols = lax.broadcasted_iota(jnp.int32, (tq, tk), 1)
        low_k = _lane_iota((tk, LANE)) < 64
        low = _lane_iota((tq, LANE)) < 64

        def step(qs, kt, carry, masked):
            r0 = kt * tk
            out = []
            for j in range(NH):
                v2 = v_ref[0, r0:r0 + tk, LANE * (j // 2):LANE * (j // 2 + 1)]
                vj = (jnp.where(low_k, v2, 1.0) if j % 2 == 0 else jnp.where(low_k, 1.0, v2)).astype(v2.dtype)
                m, acc = carry[j]
                s = _dot_nt(qs[j], k_ref[0, r0:r0 + tk, LANE * j:LANE * (j + 1)])
                if masked:
                    s = jnp.where(rows >= cols, s, NEG_INF)
                m_new = jnp.maximum(m, jnp.max(s, axis=1, keepdims=True))
                alpha = jnp.exp2(m - m_new)
                p = jnp.exp2(s - m_new)
                acc = alpha * acc + _dot(_mx(p), vj)
                out.append((m_new, acc))
            return tuple(out)

        for qi in range(nq):
            r = slice(qi * tq, (qi + 1) * tq)
            qs = [q_ref[0, r, LANE * j:LANE * (j + 1)] for j in range(NH)]
            init = (jnp.full((tq, 1), NEG_INF, F32), jnp.zeros((tq, LANE), F32))
            carry = (init,) * NH
            for kt in range(qi):
                carry = step(qs, kt, carry, False)
            carry = step(qs, qi, carry, True)
            for t in range(HPS):
                (m0, a0), (m1, a1) = carry[2 * t], carry[2 * t + 1]
                l0 = jnp.where(low, pltpu.roll(a0, 64, 1), a0)
                l1 = jnp.where(low, a1, pltpu.roll(a1, 64, 1))
                o_ref[0, r, LANE * t:LANE * (t + 1)] = jnp.where(low, a0 / l0, a1 / l1)
                lse_ref[0, 2 * t, r, :] = m0 + jnp.log2(l0)
                lse_ref[0, 2 * t + 1, r, :] = m1 + jnp.log2(l1)

    grp = lambda w: pl.BlockSpec((1, S, w), lambda b, hp: (b, 0, hp))
    return pl.pallas_call(
        body, name="mla_fwd", grid=(B, MLA_HEADS // NH),
        out_shape=[jax.ShapeDtypeStruct((B, S, 512), F32), jax.ShapeDtypeStruct((B, MLA_HEADS, S, LANE), F32)],
        in_specs=[grp(NH * LANE), grp(NH * LANE), grp(HPS * LANE)],
        out_specs=[grp(HPS * LANE), pl.BlockSpec((1, NH, S, LANE), lambda b, hp: (b, hp, 0, 0))],
        compiler_params=_cparams(("arbitrary", "arbitrary")),
    )(q3, k3, v3)


def _mla_bwd(q3, k3, v3, o3, do3, lse, tq, tk, hps=2):
    B, S, _ = q3.shape
    nq, nk = S // tq, S // tk
    nh = 2 * hps
    assert tk % tq == 0

    def body(q_ref, k_ref, v_ref, o_ref, do_ref, lse_ref, dq_ref, dk_ref, dv_ref, dkt_ref, dvt_ref):
        dkt_ref[...] = jnp.zeros_like(dkt_ref)
        dvt_ref[...] = jnp.zeros_like(dvt_ref)
        rows = lax.broadcasted_iota(jnp.int32, (tq, tk), 0)
        cols = lax.broadcasted_iota(jnp.int32, (tq, tk), 1)
        lane = _lane_iota((tq, LANE))

        def q_tile(qi, _):
            r = pl.ds(pl.multiple_of(qi * tq, tq), tq)
            heads = []
            for j in range(nh):
                lanes = slice(LANE * j, LANE * (j + 1))
                pair = slice(LANE * (j // 2), LANE * (j // 2 + 1))
                do = jnp.where((lane < 64) if j % 2 == 0 else (lane >= 64), do_ref[0, r, pair], 0.0)
                q = q_ref[0, r, lanes]
                heads.append((lanes, pair, q, _mx(q.astype(F32).T), _mx(do), _mx(do.T),
                              jnp.sum(do * o_ref[0, r, pair], axis=1, keepdims=True),
                              jnp.concatenate([lse_ref[0, j, r, :]] * (tk // LANE), axis=1)))
            n_full = (qi * tq) // tk

            def k_tile(kt, dqs, masked):
                kr = pl.ds(pl.multiple_of(kt * tk, tk), tk)
                out = []
                dvt = [None] * hps
                for j, (lanes, pair, q, qt, dob, dot_, dcol, lse_c) in enumerate(heads):
                    k = k_ref[0, kr, lanes]
                    s = _dot_nt(q, k)
                    if masked:
                        s = jnp.where(rows + qi * tq >= cols + kt * tk, s, NEG_INF)
                    p = jnp.exp2(s - lse_c)
                    dp = _dot_nt(dob, v_ref[0, kr, pair])
                    dsb = _mx(p * (dp - dcol))
                    dkt_ref[j, kt] += _dot(qt, dsb)
                    pv = _dot(dot_, _mx(p))
                    dvt[j // 2] = pv if dvt[j // 2] is None else dvt[j // 2] + pv
                    out.append(dqs[j] + _dot(dsb, k))
                for t in range(hps):
                    dvt_ref[t, kt] += dvt[t]
                return tuple(out)

            dqs = (jnp.zeros((tq, LANE), F32),) * nh
            for kt in range(n_full):
                dqs = k_tile(kt, dqs, False)
            dqs = k_tile(n_full, dqs, True)
            for j in range(nh):
                dq_ref[0, r, heads[j][0]] = (MLA_SCALE * dqs[j]).astype(dq_ref.dtype)
            return 0

        for qi in range(nq):
            q_tile(qi, 0)

        def flush(kt, _):
            kr = pl.ds(pl.multiple_of(kt * tk, tk), tk)
            for j in range(nh):
                dk_ref[0, kr, LANE * j:LANE * (j + 1)] = ((1.0 / LOG2E) * dkt_ref[j, kt].T).astype(dk_ref.dtype)
            for t in range(hps):
                dv_ref[0, kr, LANE * t:LANE * (t + 1)] = dvt_ref[t, kt].T.astype(dv_ref.dtype)
            return 0

        lax.fori_loop(0, nk, flush, 0)

    grp = lambda w: pl.BlockSpec((1, S, w), lambda b, hp: (b, 0, hp))
    return pl.pallas_call(
        body, name="mla_bwd", grid=(B, MLA_HEADS // nh),
        out_shape=[jax.ShapeDtypeStruct((B, S, 1024), MLA_GRAD_DTYPE), jax.ShapeDtypeStruct((B, S, 1024), MLA_GRAD_DTYPE),
                   jax.ShapeDtypeStruct((B, S, 512), MLA_GRAD_DTYPE)],
        in_specs=[grp(nh * LANE), grp(nh * LANE), grp(hps * LANE), grp(hps * LANE), grp(hps * LANE),
                  pl.BlockSpec((1, nh, S, LANE), lambda b, hp: (b, hp, 0, 0))],
        out_specs=[grp(nh * LANE), grp(nh * LANE), grp(hps * LANE)],
        scratch_shapes=[pltpu.VMEM((nh, nk, LANE, tk), F32), pltpu.VMEM((hps, nk, LANE, tk), F32)],
        compiler_params=_cparams(("arbitrary", "arbitrary")),
    )(q3, k3, v3, o3, do3, lse)


def _swa_consts(sink_ref):
    W = WINDOW
    row = lax.broadcasted_iota(jnp.int32, (4 * W, LANE), 0)
    out = []
    for g in range(SWA_KV_HEADS):
        slope = jnp.zeros((4 * W, LANE), F32)
        sink = jnp.zeros((4 * W, LANE), F32)
        for p in range(4):
            h = p + 4 * g
            here = jnp.logical_and(row >= W * p, row < W * (p + 1))
            slope = jnp.where(here, float(LOG2E * 2.0 ** (-8.0 * (h + 1) / SWA_HEADS)), slope)
            sink = jnp.where(here, LOG2E * sink_ref[:, h:h + 1], sink)
        out.append((jnp.concatenate([slope, slope], axis=1), sink))
    return out


def _wide(col):
    return jnp.concatenate([col, col], axis=1)


def _swa_block(q_ref, k_ref, v_ref, pb_ref, pr_ref, n, i, ij):
    W = WINDOW
    kb = jnp.maximum(n - 1, 0)
    r = pl.ds(pl.multiple_of(i * W, W), W)
    kr = pl.ds(pl.multiple_of(kb * W, W), 2 * W)
    q4, k2, v2 = q_ref[0, r, :], k_ref[0, kr, :], v_ref[0, kr, :]
    pq = _wide(pb_ref[0, r, :])
    pk = jnp.concatenate([pr_ref[0, pl.ds(kb, 1), :], pr_ref[0, pl.ds(kb + 1, 1), :]], axis=1)
    rel = ij + (n - kb) * W
    mask = jnp.where(jnp.logical_and(rel >= 0, rel < W), 0.0, NEG_INF)
    dist4 = jnp.concatenate([pq - pk] * 4, axis=0)
    mask4 = jnp.concatenate([mask] * 4, axis=0)
    return r, kb, kr, q4, k2, v2, dist4, mask4


def _swa_stack(x4, g, dtype):
    lane = _lane_iota((WINDOW, LANE))
    mine = (lane < 64) if g == 0 else (lane >= 64)
    return jnp.concatenate([jnp.where(mine, x4[:, LANE * p:LANE * (p + 1)], 0).astype(dtype) for p in range(4)], axis=0)


def _swa_unstack(ref, r, lo, hi, scale=None):
    W = WINDOW
    low = _lane_iota((W, LANE)) < 64
    for p in range(4):
        t = jnp.where(low, lo[W * p:W * (p + 1)], hi[W * p:W * (p + 1)])
        ref[0, r, LANE * p:LANE * (p + 1)] = t if scale is None else scale * t


def _swa_fwd(qs3, ks3, vs3, posb, posr, sinks):
    B, S, _ = qs3.shape
    W = WINDOW
    nb = S // W
    nh = SWA_SEQ_SPLIT
    nbh = nb // nh

    def body(q_ref, k_ref, v_ref, pb_ref, pr_ref, sink_ref, o_ref, lse_ref):
        ij = lax.broadcasted_iota(jnp.int32, (W, 2 * W), 0) - lax.broadcasted_iota(jnp.int32, (W, 2 * W), 1)
        consts = _swa_consts(sink_ref)
        n0 = pl.program_id(1) * nbh

        def blk(i, _):
            n = n0 + i
            r, _, _, q4, k2, v2, dist4, mask4 = _swa_block(q_ref, k_ref, v_ref, pb_ref, pr_ref, n, i, ij)
            o_g = []
            for g, (slope, sink) in enumerate(consts):
                s = _dot_nt(_swa_stack(q4, g, q4.dtype), k2) - slope * dist4 + mask4
                m = jnp.maximum(jnp.max(s, axis=1, keepdims=True), sink)
                e = jnp.exp2(s - _wide(m))
                l = jnp.sum(e, axis=1, keepdims=True) + jnp.exp2(sink - m)
                o_g.append(_dot(_mx(e), v2) * (1.0 / l))
                lse_ref[0, i, g] = m + jnp.log2(l)
            _swa_unstack(o_ref, r, o_g[0], o_g[1])
            return 0

        for i in range(nbh):
            blk(i, 0)

    seq = lambda w: pl.BlockSpec((1, S, w), lambda b, h: (b, 0, 0))
    part = lambda w: pl.BlockSpec((1, S // nh, w), lambda b, h: (b, h, 0))
    lse_spec = pl.BlockSpec((1, nbh, 2, 4 * W, LANE), lambda b, h: (b, h, 0, 0, 0))
    return pl.pallas_call(
        body, name="swa_fwd", grid=(B, nh),
        out_shape=[jax.ShapeDtypeStruct((B, S, 512), F32), jax.ShapeDtypeStruct((B, nb, 2, 4 * W, LANE), F32)],
        in_specs=[part(512), seq(LANE), seq(LANE), part(LANE), pl.BlockSpec((1, nb, W), lambda b, h: (b, 0, 0)),
                  pl.BlockSpec((1, LANE), lambda b, h: (0, 0))],
        out_specs=[part(512), lse_spec],
        compiler_params=_cparams(("arbitrary", "arbitrary")),
    )(qs3, ks3, vs3, posb, posr, sinks)


def _swa_bwd(qs3, ks3, vs3, posb, posr, sinks, os3, do3, lse):
    B, S, _ = qs3.shape
    W = WINDOW
    nb = S // W
    nh = SWA_SEQ_SPLIT
    nbh = nb // nh
    assert nbh % 2 == 0 and nbh * nh * W == S

    def body(q_ref, k_ref, v_ref, pb_ref, pr_ref, sink_ref, o_ref, do_ref, lse_ref, dq_ref, dk_ref, dv_ref, dsink_ref,
             dkt_ref, dvt_ref):
        lane1 = _lane_iota((1, LANE))
        ij = lax.broadcasted_iota(jnp.int32, (W, 2 * W), 0) - lax.broadcasted_iota(jnp.int32, (W, 2 * W), 1)
        consts = _swa_consts(sink_ref)
        hh = pl.program_id(1)
        n0 = hh * nbh

        @pl.when(hh == 0)
        def _():
            dkt_ref[...] = jnp.zeros_like(dkt_ref)
            dvt_ref[...] = jnp.zeros_like(dvt_ref)

        @pl.when(jnp.logical_and(pl.program_id(0) == 0, hh == 0))
        def _():
            dsink_ref[...] = jnp.zeros_like(dsink_ref)

        def blk(i, dsink):
            n = n0 + i
            r, kb, _, q4, k2, v2, dist4, mask4 = _swa_block(q_ref, k_ref, v_ref, pb_ref, pr_ref, n, i, ij)
            o4, do4 = o_ref[0, r, :], do_ref[0, r, :]
            dq_g = []
            dkt = jnp.zeros((LANE, 2 * W), F32)
            dvt = jnp.zeros((LANE, 2 * W), F32)
            for g, (slope, sink) in enumerate(consts):
                q_st = _swa_stack(q4, g, F32)
                do_st = _swa_stack(do4, g, F32)
                dcol = jnp.sum(do_st * _swa_stack(o4, g, F32), axis=1, keepdims=True)
                qst, dob = _mx(q_st), _mx(do_st)
                lse_c = lse_ref[0, i, g]
                pr = jnp.exp2(_dot_nt(qst, k2) - slope * dist4 + mask4 - _wide(lse_c))
                dsb = _mx(pr * (_dot_nt(dob, v2) - dcol))
                psd = jnp.exp2(sink - lse_c)[:, 0:1] * dcol
                for p in range(4):
                    dsink = dsink - jnp.where(lane1 == p + 4 * g,
                                              jnp.sum(psd[W * p:W * (p + 1)], axis=0, keepdims=True), 0.0)
                dq_g.append(_dot(dsb, k2))
                dkt = dkt + _dot(_mx(q_st.T), dsb)
                dvt = dvt + _dot(_mx(do_st.T), _mx(pr))
            _swa_unstack(dq_ref, r, dq_g[0], dq_g[1], SWA_SCALE)
            dkt_ref[kb] += dkt[:, 0:W]
            dkt_ref[kb + 1] += dkt[:, W:2 * W]
            dvt_ref[kb] += dvt[:, 0:W]
            dvt_ref[kb + 1] += dvt[:, W:2 * W]
            return dsink

        dsink = jnp.zeros((1, LANE), F32)
        for i in range(nbh):
            dsink = blk(i, dsink)
        dsink_ref[...] += dsink

        @pl.when(hh == nh - 1)
        def _():
            def flush(n, _):
                r = pl.ds(pl.multiple_of(n * W, W), W)
                dk_ref[0, r, :] = (1.0 / LOG2E) * dkt_ref[n].T
                dv_ref[0, r, :] = dvt_ref[n].T
                return 0

            lax.fori_loop(0, nb, flush, 0)

    seq = lambda w: pl.BlockSpec((1, S, w), lambda b, h: (b, 0, 0))
    part = lambda w: pl.BlockSpec((1, S // nh, w), lambda b, h: (b, h, 0))
    return pl.pallas_call(
        body, name="swa_bwd", grid=(B, nh),
        out_shape=[jax.ShapeDtypeStruct((B, S, 512), F32), jax.ShapeDtypeStruct((B, S, LANE), F32),
                   jax.ShapeDtypeStruct((B, S, LANE), F32), jax.ShapeDtypeStruct((1, LANE), F32)],
        in_specs=[part(512), seq(LANE), seq(LANE), part(LANE), pl.BlockSpec((1, nb, W), lambda b, h: (b, 0, 0)),
                  pl.BlockSpec((1, LANE), lambda b, h: (0, 0)), part(512),
                  pl.BlockSpec((1, S // nh, 512), lambda b, h: (b, h, 1)),
                  pl.BlockSpec((1, nbh, 2, 4 * W, LANE), lambda b, h: (b, h, 0, 0, 0))],
        out_specs=[part(512), seq(LANE), seq(LANE), pl.BlockSpec((1, LANE), lambda b, h: (0, 0))],
        scratch_shapes=[pltpu.VMEM((nb, LANE, W), F32), pltpu.VMEM((nb, LANE, W), F32)],
        compiler_params=_cparams(("arbitrary", "arbitrary")),
    )(qs3, ks3, vs3, posb, posr, sinks, os3, do3, lse)


def _post(om, osw, g, w_out, x2, gate, fg, tgt, S, ts):
    T, D = x2.shape
    nsb = S // ts

    def body(om_ref, os_ref, g_ref, w_ref, x_ref, gate_ref, fg_ref, t_ref,
             dx2_ref, do_ref, dg_ref, loss_ref, dfg_ref, dgate_ref, dw_ref):
        i = pl.program_id(0)
        gv = g_ref[...]
        sg = 1.0 / (1.0 + jnp.exp(-gv))
        silu = gv * sg
        o = jnp.concatenate([om_ref[...], os_ref[...]], axis=1)
        ab = _mx(o * silu)
        y = _dot(ab, w_ref[...])
        gate = gate_ref[0]
        xo = x_ref[...] + gate * y
        r2 = lax.rsqrt(jnp.mean(xo * xo, axis=-1, keepdims=True) + EPS)
        xh = xo * r2
        fg = fg_ref[...]
        diff = xh * fg - t_ref[...]
        sq = jnp.sum(diff * diff, axis=0, keepdims=True)
        part = sq[:, 0:LANE]
        for t in range(1, D // LANE):
            part = part + sq[:, LANE * t:LANE * (t + 1)]
        dout = diff * (1.0 / D)
        dxh = dout * fg
        dx2 = r2 * (dxh - xh * jnp.mean(dxh * xh, axis=-1, keepdims=True))
        dx2_ref[...] = dx2
        dyb = _mx(dx2 * gate)
        da = _dot_nt(dyb, w_ref[...])
        do_ref[...] = da * silu
        dg_ref[...] = _mx(da * o * (sg * (1.0 + gv * (1.0 - sg))))

        @pl.when(i == 0)
        def _():
            loss_ref[...] = jnp.zeros_like(loss_ref)
            dfg_ref[...] = jnp.zeros_like(dfg_ref)
            dw_ref[...] = jnp.zeros_like(dw_ref)

        @pl.when(i % nsb == 0)
        def _():
            dgate_ref[...] = jnp.zeros_like(dgate_ref)

        loss_ref[...] += (0.5 / D) * part
        dfg_ref[...] += jnp.sum(dout * xh, axis=0, keepdims=True)
        dgate_ref[0] += jnp.sum(dx2 * y, axis=0, keepdims=True)
        dw_ref[...] += _dot_tn(ab, dyb)

        @pl.when(i == T // ts - 1)
        def _():
            chunks = [dw_ref[512 + 64 * PAIR_INV[h]:512 + 64 * (PAIR_INV[h] + 1), :] for h in range(SWA_HEADS)]
            for h in range(SWA_HEADS):
                dw_ref[512 + 64 * h:512 + 64 * (h + 1), :] = chunks[h]

    row = lambda w: pl.BlockSpec((ts, w), lambda i: (i, 0))
    full = lambda a: pl.BlockSpec(a.shape, lambda i: (0,) * a.ndim)
    per_b = pl.BlockSpec((1, 1, D), lambda i: (i // nsb, 0, 0))
    return pl.pallas_call(
        body, name="post", grid=(T // ts,),
        out_shape=[jax.ShapeDtypeStruct((T, D), F32), jax.ShapeDtypeStruct((T, 1024), F32),
                   jax.ShapeDtypeStruct((T, 1024), _MXU_DTYPE), jax.ShapeDtypeStruct((1, LANE), F32),
                   jax.ShapeDtypeStruct((1, D), F32), jax.ShapeDtypeStruct(gate.shape, F32),
                   jax.ShapeDtypeStruct(w_out.shape, F32)],
        in_specs=[row(512), row(512), row(1024), full(w_out), row(D), per_b, full(fg), row(D)],
        out_specs=[row(D), row(1024), row(1024),
                   pl.BlockSpec((1, LANE), lambda i: (0, 0)), pl.BlockSpec((1, D), lambda i: (0, 0)), per_b,
                   full(w_out)],
        compiler_params=_cparams(("arbitrary",)),
    )(om, osw, g, w_out, x2, gate, fg, tgt)


def _pre_bwd(dq, dk, dv, dqs, dks, dvs, dg, zq, zkv, ql, kvl, x2, dx2, scale, ng, w_in, gq, gkv, w_uq, w_uk, w_uv,
             rope, S, ts):
    T, D = x2.shape
    nsb = S // ts
    WQ = MLA_HEADS * LANE

    def body(dq_ref, dk_ref, dv_ref, dqs_ref, dks_ref, dvs_ref, dg_ref, zq_ref, zkv_ref, ql_ref, kvl_ref, x_ref, dx2_ref,
             sc_ref,
             ng_ref, w_ref, gq_ref, gkv_ref, wuq_ref, wuk_ref, wuv_ref, cd_ref, sd_ref,
             gx_ref, dz_ref, dgq_ref, dgkv_ref, dng_ref, dsh_ref, dsc_ref, dwuq_ref, dwuk_ref, dwuv_ref):
        i = pl.program_id(0)

        @pl.when(i == 0)
        def _():
            dgq_ref[...] = jnp.zeros_like(dgq_ref)
            dgkv_ref[...] = jnp.zeros_like(dgkv_ref)
            dng_ref[...] = jnp.zeros_like(dng_ref)
            dwuq_ref[...] = jnp.zeros_like(dwuq_ref)
            dwuk_ref[...] = jnp.zeros_like(dwuk_ref)
            dwuv_ref[...] = jnp.zeros_like(dwuv_ref)

        @pl.when(i % nsb == 0)
        def _():
            dsh_ref[...] = jnp.zeros_like(dsh_ref)
            dsc_ref[...] = jnp.zeros_like(dsc_ref)

        def norm_bwd(z, dy, gain):
            r = lax.rsqrt(jnp.mean(z * z, axis=-1, keepdims=True) + EPS)
            zh = z * r
            dzh = dy * gain
            return r * (dzh - zh * jnp.mean(dzh * zh, axis=-1, keepdims=True)), jnp.sum(dy * zh, axis=0, keepdims=True)

        tables = _rope_split(cd_ref[...], sd_ref[...])
        ng = ng_ref[...]
        sc1 = 1.0 + sc_ref[0]

        def rows_chain(rs):
            cq, ck, sa, sb = (t[rs] for t in tables)
            dqr = dq_ref[rs, :].astype(F32)
            dqb = _mx(dqr * _tile_heads(cq) + pltpu.roll(dqr * _tile_heads(sa), 16, 1)
                      + pltpu.roll(dqr * _tile_heads(sb), WQ - 16, 1))
            p_uq = _dot_tn(ql_ref[rs, :], dqb)
            dzq, dgq = norm_bwd(zq_ref[rs, :], _dot_nt(dqb, wuq_ref[...]), gq_ref[...])
            dkr = dk_ref[rs, :].astype(F32)
            dkb = _mx(dkr)
            p_uk = _dot_tn(kvl_ref[rs, :], dkb)
            dvb = _mx(dv_ref[rs, :])
            p_uv = _dot_tn(kvl_ref[rs, :], dvb)
            dzkv, dgkv = norm_bwd(zkv_ref[rs, :], _dot_nt(dkb, wuk_ref[...]) + _dot_nt(dvb, wuv_ref[...]), gkv_ref[...])
            dkpe = dkr[:, 0:LANE]
            for h in range(1, MLA_HEADS):
                dkpe = dkpe + dkr[:, LANE * h:LANE * (h + 1)]
            dkro = dkpe * ck + pltpu.roll(dkpe * sa, 16, 1) + pltpu.roll(dkpe * sb, LANE - 16, 1)
            dz_ref[rs, 0:384] = _mx(dzq)
            dz_ref[rs, 384:640] = _mx(dzkv)
            dz_ref[rs, 640:768] = _mx(dkro)
            dz_ref[rs, 768:1280] = dg_ref[rs, 0:512]
            dz_ref[rs, 1280:1792] = _mx(dqs_ref[rs, :])
            dz_ref[rs, 1792:1920] = _mx(dks_ref[rs, :])
            dz_ref[rs, 1920:2048] = _mx(dvs_ref[rs, :])
            dz_ref[rs, 2048:2560] = dg_ref[rs, 512:1024]
            dh = _dot(dz_ref[rs, :], w_ref[...])
            x = x_ref[rs, :]
            r1 = lax.rsqrt(jnp.mean(x * x, axis=-1, keepdims=True) + EPS)
            xn = x * r1
            dxn = dh * ng * sc1
            gx_ref[rs, :] = dx2_ref[rs, :] + r1 * (dxn - xn * jnp.mean(dxn * xn, axis=-1, keepdims=True))
            return (p_uq, p_uk, p_uv, dgq, dgkv, jnp.sum(dh, axis=0, keepdims=True),
                    jnp.sum(dh * (xn * ng), axis=0, keepdims=True), jnp.sum(dh * xn * sc1, axis=0, keepdims=True))

        hr = ts // 2
        parts = [rows_chain(slice(hr * t, hr * (t + 1))) for t in range(2)]
        p_uq, p_uk, p_uv, dgq, dgkv, dsh, dsc, dng = (a + b for a, b in zip(*parts))
        dwuq_ref[...] += p_uq
        dwuk_ref[...] += p_uk
        dwuv_ref[...] += p_uv
        dgq_ref[...] += dgq
        dgkv_ref[...] += dgkv
        dsh_ref[0] += dsh
        dsc_ref[0] += dsc
        dng_ref[...] += dng

    row = lambda w: pl.BlockSpec((ts, w), lambda i: (i, 0))
    full = lambda a: pl.BlockSpec(a.shape, lambda i: (0,) * a.ndim, pipeline_mode=pl.Buffered(1))
    per_b = pl.BlockSpec((1, 1, D), lambda i: (i // nsb, 0, 0))
    dense = pl.BlockSpec((ts // 8, LANE), lambda i: (i, 0))
    vec = lambda w: pl.BlockSpec((1, w), lambda i: (0, 0))
    return pl.pallas_call(
        body, name="pre_bwd", grid=(T // ts,),
        out_shape=[jax.ShapeDtypeStruct((T, D), F32), jax.ShapeDtypeStruct((T, D_IN_PAD), _MXU_DTYPE), jax.ShapeDtypeStruct((1, 384), F32),
                   jax.ShapeDtypeStruct((1, 256), F32), jax.ShapeDtypeStruct((1, D), F32),
                   jax.ShapeDtypeStruct(scale.shape, F32), jax.ShapeDtypeStruct(scale.shape, F32),
                   jax.ShapeDtypeStruct(w_uq.shape, F32), jax.ShapeDtypeStruct(w_uk.shape, F32),
                   jax.ShapeDtypeStruct(w_uv.shape, F32)],
        in_specs=[row(WQ), row(WQ), row(512), row(512), row(LANE), row(LANE), row(1024), row(384), row(256), row(384),
                  row(256), row(D), row(D), per_b, full(ng), full(w_in), full(gq), full(gkv), full(w_uq), full(w_uk), full(w_uv),
                  dense, dense],
        out_specs=[row(D), row(D_IN_PAD), vec(384), vec(256), vec(D), per_b, per_b, full(w_uq), full(w_uk), full(w_uv)],
        compiler_params=_cparams(("arbitrary",)),
    )(dq, dk, dv, dqs, dks, dvs, dg, zq, zkv, ql, kvl, x2, dx2, scale, ng, w_in, gq, gkv, w_uq, w_uk, w_uv, *rope)


def _w_in_row_runs():
    runs = [(0, 0, 640), (640, 704, 32), (672, 768, 512)]
    runs += [(1184 + 64 * h, 1280 + 64 * PAIR_INV[h], 64) for h in range(8)]
    runs += [(1696, 1792, 256)]
    runs += [(1952 + 64 * h, 2048 + 64 * PAIR_INV[h], 64) for h in range(8)]
    return runs


W_IN_SHARD = D_IN // N_DEV
W_IN_SLOT = 320


def _dw_in_t(dz, hb, tn, tk):
    T, M = dz.shape
    N = hb.shape[1]
    nk = T // tk

    def body(a_ref, b_ref, o_ref, ob_ref, acc_ref):
        k = pl.program_id(1)

        @pl.when(k == 0)
        def _():
            acc_ref[...] = jnp.zeros_like(acc_ref)

        acc_ref[...] += _dot_tn(a_ref[...], b_ref[...])

        @pl.when(k == nk - 1)
        def _():
            for d in range(N_DEV):
                lo, hi = W_IN_SHARD * d, W_IN_SHARD * (d + 1)
                for nat, pad, size in _w_in_row_runs():
                    a, b = max(nat, lo), min(nat + size, hi)
                    if a < b:
                        piece = acc_ref[pad + a - nat:pad + b - nat, :]
                        o_ref[d, a - lo:b - lo, :] = piece
                        ob_ref[d, a - lo:b - lo, :] = piece.astype(ob_ref.dtype)
                o_ref[d, W_IN_SHARD:W_IN_SLOT, :] = jnp.zeros((W_IN_SLOT - W_IN_SHARD, tn), F32)
                ob_ref[d, W_IN_SHARD:W_IN_SLOT, :] = jnp.zeros((W_IN_SLOT - W_IN_SHARD, tn), ob_ref.dtype)

    slots = pl.BlockSpec((N_DEV, W_IN_SLOT, tn), lambda j, k: (0, 0, j))
    return pl.pallas_call(
        body, name="dw_in", grid=(N // tn, nk),
        out_shape=[jax.ShapeDtypeStruct((N_DEV, W_IN_SLOT, N), F32), jax.ShapeDtypeStruct((N_DEV, W_IN_SLOT, N), jnp.bfloat16)],
        in_specs=[pl.BlockSpec((tk, M), lambda j, k: (k, 0)), pl.BlockSpec((tk, tn), lambda j, k: (k, j))],
        out_specs=[slots, slots],
        scratch_shapes=[pltpu.VMEM((M, tn), F32)],
        compiler_params=_cparams(("arbitrary", "arbitrary")),
    )(dz, hb)


def _finalize(parts_all, dmod_all, dmod_cols, c_act_all):
    nparts = parts_all.shape[-1]
    nmod = dmod_all.shape[-1]

    def body(p_ref, dm_ref, dmc_ref, c_ref, ps_ref, loss_ref, db_ref, dw_ref):
        acc = p_ref[0]
        for j in range(1, N_DEV):
            acc = acc + p_ref[j]
        ps_ref[...] = acc
        loss_ref[...] = jnp.sum(acc[:, 0:LANE], axis=1, keepdims=True)
        db = dm_ref[0:1, :]
        for j in range(1, dm_ref.shape[0]):
            db = db + dm_ref[j:j + 1, :]
        db_ref[...] = db
        dw_ref[...] = _dot_tn(_mx(_silu(c_ref[...])), _mx(dmc_ref[...]))

    return pl.pallas_call(
        body, name="finalize",
        out_shape=[jax.ShapeDtypeStruct((1, nparts), F32), jax.ShapeDtypeStruct((1, 1), F32),
                   jax.ShapeDtypeStruct((1, nmod), F32),
                   jax.ShapeDtypeStruct((c_act_all.shape[1], dmod_cols.shape[1]), F32)],
        in_specs=[_vmem()] * 4, out_specs=[_vmem()] * 4,
        compiler_params=_cparams(),
    )(parts_all, dmod_all, dmod_cols, c_act_all)


def _adamw(ws, gs, ms, vs):
    n = len(ws)

    def body(*refs):
        w_refs, g_refs, m_refs, v_refs, d_refs, nm_refs, nv_refs = (refs[t * n:(t + 1) * n] for t in range(7))
        for t in range(n):
            gv = g_refs[t][...]
            nm = ADAM_B1 * m_refs[t][...] + (1.0 - ADAM_B1) * gv
            nv = ADAM_B2 * v_refs[t][...] + (1.0 - ADAM_B2) * (gv * gv)
            m_hat = nm / (1.0 - ADAM_B1 ** ADAM_STEP)
            v_hat = nv / (1.0 - ADAM_B2 ** ADAM_STEP)
            d_refs[t][...] = -ADAM_LR * (m_hat / (jnp.sqrt(v_hat) + ADAM_EPS) + ADAM_WD * w_refs[t][...])
            nm_refs[t][...] = nm
            nv_refs[t][...] = nv

    out = pl.pallas_call(
        body, name="adamw",
        out_shape=[jax.ShapeDtypeStruct(w.shape, F32) for w in ws] * 3,
        in_specs=[_vmem()] * (4 * n), out_specs=[_vmem()] * (3 * n),
        compiler_params=_cparams(),
    )(*ws, *gs, *ms, *vs)
    return out[:n], out[n:2 * n], out[2 * n:]


def _pair_perm(a, axis, order):
    a = jnp.moveaxis(a, axis, -1)
    lead = a.shape[:-1]
    a = a.reshape(lead + (8, 64))[..., list(order), :].reshape(lead + (512,))
    return jnp.moveaxis(a, -1, axis)


def _rope_table(positions):
    T = positions.size
    inv = ROPE_THETA ** (-jnp.arange(0, MLA_ROPE, 2, dtype=F32) / MLA_ROPE)
    pos = jnp.repeat(positions.reshape(T // 8, 8)[:, ::-1].astype(F32), MLA_ROPE // 2, axis=1)
    ang = pos * jnp.tile(inv, 8)[None, :]
    return jnp.cos(ang), jnp.sin(ang)


def _local_step(x, mod, positions, ng, w_in_t, gq, gkv, w_uq, w_ukv, sinks, w_out, fg, tgt,
                ts=512, fq=512, fk=512, bq=512, bk=512):
    B, S, D = x.shape
    T = B * S
    x2 = x.reshape(T, D)
    shift, scale, gate = (mod[:, None, k * D:(k + 1) * D] for k in range(3))
    w_uq_p = jnp.pad(w_uq.reshape(Q_LORA, MLA_HEADS, 96), ((0, 0), (0, 0), (0, 32))).reshape(Q_LORA, MLA_HEADS * LANE)
    w_ukv3 = w_ukv.reshape(KV_LORA, MLA_HEADS, 128)
    w_uk_p = jnp.pad(w_ukv3[:, :, :64], ((0, 0), (0, 0), (0, 64))).reshape(KV_LORA, MLA_HEADS * LANE)
    w_uv = w_ukv3[:, :, 64:].reshape(KV_LORA, 512)
    w_out_p = jnp.concatenate([w_out[:512], _pair_perm(w_out[512:], 0, PAIR_ORDER)], axis=0)
    rope = _rope_table(positions)
    posf = positions.astype(F32)
    posb = jnp.broadcast_to(posf[:, :, None], (B, S, LANE))
    posr = posf.reshape(B, S // WINDOW, WINDOW)
    sinks_l = jnp.pad(sinks.reshape(1, SWA_HEADS), ((0, 0), (0, LANE - SWA_HEADS)))

    (hb, zq, zkv, ql, kvl, q, k, v, qs, ks, vs, g, w_in_p) = _pre_fwd(
        x2, shift, scale, ng, w_in_t, gq, gkv, w_uq_p, w_uk_p, w_uv, rope, S, ts)
    r3 = lambda a: a.reshape(B, S, a.shape[-1])
    om, lse_m = _mla_fwd(r3(q), r3(k), r3(v), fq, fk)
    osw, lse_s = _swa_fwd(r3(qs), r3(ks), r3(vs), posb, posr, sinks_l)
    dx2, do, dg, loss_v, dfg, dgate, dw_out = _post(
        om.reshape(T, 512), osw.reshape(T, 512), g, w_out_p, x2, gate, fg.reshape(1, D), tgt.reshape(T, D), S, ts)
    do3 = r3(do)
    dq, dk, dv = _mla_bwd(r3(q), r3(k), r3(v), om, do3, lse_m, bq, bk)
    dqs, dks, dvs, dsink = _swa_bwd(r3(qs), r3(ks), r3(vs), posb, posr, sinks_l, osw, do3, lse_s)
    f2 = lambda a: a.reshape(T, a.shape[-1])
    gx, dz, dgq, dgkv, dng, dsh, dsc, dw_uq_p, dw_uk_p, dw_uv = _pre_bwd(
        f2(dq), f2(dk), f2(dv), f2(dqs), f2(dks), f2(dvs), dg, zq, zkv, ql, kvl, x2, dx2, scale, ng, w_in_p, gq, gkv,
        w_uq_p, w_uk_p, w_uv, rope, S, ts)
    tk = min(T, 1024)
    dw_in_t = _dw_in_t(dz, hb, 512, tk)
    dw_uq = dw_uq_p.reshape(Q_LORA, MLA_HEADS, LANE)[:, :, :96].reshape(Q_LORA, 768)
    dw_uk = dw_uk_p.reshape(KV_LORA, MLA_HEADS, LANE)[:, :, :64]
    dw_uv = dw_uv.reshape(KV_LORA, MLA_HEADS, 64)
    dw_ukv = jnp.concatenate([dw_uk, dw_uv], axis=2).reshape(KV_LORA, 1024)
    parts = jnp.concatenate([loss_v, dfg, dng, dgq, dgkv, dsink], axis=1)
    dmod = jnp.concatenate([dsh, dsc, dgate], axis=2).reshape(B, 3 * D)
    return gx.reshape(B, S, D), dw_in_t, dw_uq, dw_ukv, dw_out, parts, dmod


def kernel(x, c, positions, w_ada, b_ada, norm_gain, w_in, q_norm_gain, kv_norm_gain, w_uq, w_ukv, swa_sinks, w_out, final_gain, loss_target, m_w_ada, m_b_ada, m_norm_gain, m_w_in, m_q_norm_gain, m_kv_norm_gain, m_w_uq, m_w_ukv, m_swa_sinks, m_w_out, m_final_gain, v_w_ada, v_b_ada, v_norm_gain, v_w_in, v_q_norm_gain, v_kv_norm_gain, v_w_uq, v_w_ukv, v_swa_sinks, v_w_out, v_final_gain):
    B, S, D = x.shape
    me = 4 * lax.axis_index("x") + 2 * lax.axis_index("y") + lax.axis_index("c")
    bf = _MXU_DTYPE

    ncol = w_ada.shape[2]
    b_cols = lax.dynamic_slice_in_dim(b_ada, me * ncol, ncol, axis=1)
    c_all, win_g, wuq_g, wukv_g, wout_g, mod_g = _all_gather(
        [c, w_in[0].T.astype(bf), w_uq[0].astype(bf), w_ukv[0].astype(bf), w_out[0].astype(bf)], "ag_weights",
        fused=((w_ada[0], b_cols), _ada_cols, jax.ShapeDtypeStruct((N_DEV, B, ncol), F32)))
    c_all = c_all.reshape(N_DEV * B, D)
    cat_cols = lambda a: jnp.transpose(a, (1, 0, 2)).reshape(a.shape[1], N_DEV * a.shape[2])
    w_in_t, w_uq_f, w_ukv_f = win_g, cat_cols(wuq_g), cat_cols(wukv_g)
    w_out_f = wout_g.reshape(D, D)

    mod = lax.dynamic_index_in_dim(mod_g, me, axis=1, keepdims=False)
    mod = jnp.transpose(mod, (1, 0, 2)).reshape(B, 3 * D)

    gx, dw_in_t, dw_uq, dw_ukv, dw_out, parts, dmod = _local_step(
        x, mod, positions, norm_gain, w_in_t, q_norm_gain, kv_norm_gain, w_uq_f, w_ukv_f, swa_sinks,
        w_out_f, final_gain, loss_target)

    split_cols = lambda a: jnp.transpose(a.reshape(a.shape[0], 4, 2, a.shape[1] // N_DEV), (1, 2, 0, 3))
    split_rows = lambda a: a.reshape(4, 2, a.shape[0] // N_DEV, a.shape[1])
    slots = lambda a: a.reshape((4, 2) + a.shape[1:])
    g_w_in_t, g_w_uq, g_w_ukv, g_w_out, parts_g, dmod_g = _reduce_scatter(
        [(slots(dw_in_t[0]), slots(dw_in_t[1])), split_cols(dw_uq), split_cols(dw_ukv), split_rows(dw_out)], "rs_grads",
        gather=(parts, dmod))
    g_w_in = g_w_in_t[:W_IN_SHARD].T

    dmod_all = dmod_g.reshape(N_DEV * B, 3 * D)
    dmod_cols = lax.dynamic_slice_in_dim(dmod_all, me * ncol, ncol, axis=1)
    psum, loss, g_b_ada, g_w_ada = _finalize(parts_g, dmod_all, dmod_cols, c_all)
    loss = loss.reshape(())
    o = LANE
    g_final_gain = psum[0, o:o + D]
    g_norm_gain = psum[:, o + D:o + 2 * D]
    o += 2 * D
    g_q_norm_gain = psum[:, o:o + Q_LORA]
    g_kv_norm_gain = psum[:, o + Q_LORA:o + Q_LORA + KV_LORA]
    o += Q_LORA + KV_LORA
    g_sinks = psum[:, o:o + SWA_HEADS]

    grads = [g_w_ada, g_b_ada, g_norm_gain, g_w_in, g_q_norm_gain, g_kv_norm_gain, g_w_uq, g_w_ukv, g_sinks, g_w_out,
             g_final_gain]
    ws = [w_ada, b_ada, norm_gain, w_in, q_norm_gain, kv_norm_gain, w_uq, w_ukv, swa_sinks, w_out, final_gain]
    ms = [m_w_ada, m_b_ada, m_norm_gain, m_w_in, m_q_norm_gain, m_kv_norm_gain, m_w_uq, m_w_ukv, m_swa_sinks, m_w_out,
          m_final_gain]
    vs = [v_w_ada, v_b_ada, v_norm_gain, v_w_in, v_q_norm_gain, v_kv_norm_gain, v_w_uq, v_w_ukv, v_swa_sinks, v_w_out,
          v_final_gain]
    two_d = [(1, w.shape[0]) if w.ndim == 1 else w.shape[-2:] for w in ws]
    flat = lambda arrs: [a.reshape(s) for a, s in zip(arrs, two_d)]
    deltas, new_ms, new_vs = _adamw(flat(ws), flat(grads), flat(ms), flat(vs))
    shaped = lambda arrs: [a.reshape(w.shape) for a, w in zip(arrs, ws)]
    return (loss, gx, *shaped(grads), *shaped(deltas), *shaped(new_ms), *shaped(new_vs))
```

```python
import functools

import jax
import jax.numpy as jnp
from jax import lax
from jax.experimental import pallas as pl
from jax.experimental.pallas import tpu as pltpu

F32 = jnp.float32
_MXU_DTYPE = jnp.bfloat16
MLA_GRAD_DTYPE = jnp.bfloat16

N_DEV = 8
MLA_HEADS = 8
MLA_NOPE = 64
MLA_ROPE = 32
Q_LORA = 384
KV_LORA = 256
SWA_HEADS = 8
SWA_KV_HEADS = 2
SWA_HEAD_DIM = 64
WINDOW = 128
ROPE_THETA = 10000.0
EPS = 1e-6
MLA_SCALE = float((MLA_NOPE + MLA_ROPE) ** -0.5)
SWA_SCALE = float(SWA_HEAD_DIM ** -0.5)
LOG2E = 1.4426950408889634
MLA_QSCALE = MLA_SCALE * LOG2E
D_IN = 2464
D_IN_PAD = 2560
PAIR_ORDER = (0, 4, 1, 5, 2, 6, 3, 7)
PAIR_INV = (0, 2, 4, 6, 1, 3, 5, 7)

ADAM_LR = 0.001
ADAM_B1 = 0.9
ADAM_B2 = 0.999
ADAM_EPS = 1e-08
ADAM_WD = 0.01
ADAM_STEP = 10

LANE = 128
VMEM_LIMIT = 56 * 1024 * 1024

MESH = pl.DeviceIdType.MESH
NEG_INF = float("-inf")
SWA_SEQ_SPLIT = 2


def _mx(a):
    return a.astype(_MXU_DTYPE)


def _dot(a, b):
    return jnp.dot(a, b, preferred_element_type=F32)


def _dot_nt(a, b):
    return lax.dot_general(a, b, (((1,), (1,)), ((), ())), preferred_element_type=F32)


def _dot_tn(a, b):
    return lax.dot_general(a, b, (((0,), (0,)), ((), ())), preferred_element_type=F32)


def _cparams(sem=None):
    return pltpu.CompilerParams(dimension_semantics=sem, vmem_limit_bytes=VMEM_LIMIT)


def _vmem():
    return pl.BlockSpec(memory_space=pltpu.VMEM)


def _lane_iota(shape):
    return lax.broadcasted_iota(jnp.int32, shape, len(shape) - 1)


def _gather_program(srcs, outs, send_sems, recv_sems, local_sems):
    x, y, c = lax.axis_index("x"), lax.axis_index("y"), lax.axis_index("c")
    me, sibling = (x, y, c), (x, y, 1 - c)
    chips = [(1 - x, y), (x, 1 - y), (1 - x, 1 - y)]

    def slot(a, dev):
        return outs[a].at[4 * dev[0] + 2 * dev[1] + dev[2]]

    def copy(a, k, block, to, src=None):
        return pltpu.make_async_remote_copy(
            src_ref=slot(a, block) if src is None else src, dst_ref=slot(a, block),
            send_sem=send_sems.at[7 * a + k], recv_sem=recv_sems.at[7 * a + k],
            device_id=to, device_id_type=MESH)

    def local(a):
        return pltpu.make_async_copy(srcs[a], slot(a, me), local_sems.at[a])

    def start(a):
        local(a).start()
        cps = [copy(a, 0, me, sibling, src=srcs[a])]
        cps += [copy(a, 1 + j, me, (*chip, c), src=srcs[a]) for j, chip in enumerate(chips)]
        for cp in cps:
            cp.start()
        return cps

    def finish(group):
        cps = []
        for j, chip in enumerate(chips):
            for a in group:
                copy(a, 1 + j, (*chip, c), me).wait_recv()
                cp = copy(a, 4 + j, (*chip, c), sibling)
                cp.start()
                cps.append(cp)
        for a in group:
            copy(a, 0, sibling, me).wait_recv()
            for j, chip in enumerate(chips):
                copy(a, 4 + j, (*chip, 1 - c), me).wait_recv()
            local(a).wait()
        return cps

    return start, finish


def _gather_sems(m):
    return [pltpu.SemaphoreType.DMA((7 * m,)), pltpu.SemaphoreType.DMA((7 * m,)), pltpu.SemaphoreType.DMA((m,))]


def _all_gather(arrs, name, fused=None):
    n = len(arrs)
    extra, fn, piece = fused if fused else ((), None, None)
    ne, m = len(extra), n + (1 if fused else 0)

    def body(*refs):
        ins, ex, outs = refs[:n], refs[n:n + ne], refs[n + ne:n + ne + m]
        rest = refs[n + ne + m:]
        srcs = list(ins) + ([rest[0]] if fused else [])
        start, finish = _gather_program(srcs, outs, *rest[-3:])
        pending = []
        for a in range(n):
            pending += start(a)
        if fused:
            pending += finish([0])
            fn(srcs[n], outs[0], *ex)
            pending += start(n)
            pending += finish([n] + list(range(1, n)))
        else:
            pending += finish(list(range(n)))
        for cp in pending:
            cp.wait_send()

    out_shape = [jax.ShapeDtypeStruct((N_DEV,) + a.shape, a.dtype) for a in arrs]
    scratch = []
    if fused:
        out_shape.append(jax.ShapeDtypeStruct((N_DEV,) + piece.shape, piece.dtype))
        scratch.append(pltpu.VMEM(piece.shape, piece.dtype))
    return pl.pallas_call(
        body, name=name, out_shape=out_shape,
        in_specs=[_vmem()] * (n + ne), out_specs=[_vmem()] * m,
        scratch_shapes=scratch + _gather_sems(m),
        compiler_params=pltpu.CompilerParams(vmem_limit_bytes=VMEM_LIMIT),
    )(*arrs, *extra)


def _reduce_scatter(arrs, name, gather=()):
    n, g = len(arrs), len(gather)
    halves = [a[1] if isinstance(a, tuple) else a.astype(jnp.bfloat16) for a in arrs]
    arrs = [a[0] if isinstance(a, tuple) else a for a in arrs]

    def body(*refs):
        xs, xbs, gins = refs[:n], refs[n:2 * n], refs[2 * n:2 * n + g]
        outs, gouts = refs[2 * n + g:3 * n + g], refs[3 * n + g:3 * n + 2 * g]
        rest = refs[3 * n + 2 * g:]
        parts, recv_a, send_b, recv_b = (rest[t * n:(t + 1) * n] for t in range(4))
        send_sems, recv_sems, local_sems = rest[4 * n:4 * n + 3]
        x, y, c = lax.axis_index("x"), lax.axis_index("y"), lax.axis_index("c")
        myq = 2 * x + y
        gather_start, gather_finish = _gather_program(gins, gouts, *rest[4 * n + 3:])

        def chip(k):
            return (1 - x if k & 2 else x, 1 - y if k & 1 else y)

        def from_sibling(a):
            return pltpu.make_async_remote_copy(
                src_ref=xbs[a].at[:, 1 - c], dst_ref=recv_a[a], send_sem=send_sems.at[4 * a], recv_sem=recv_sems.at[4 * a],
                device_id=(x, y, 1 - c), device_id_type=MESH)

        def to_owner(a, k):
            qx, qy = chip(k)
            return pltpu.make_async_remote_copy(
                src_ref=send_b[a].at[2 * qx + qy], dst_ref=recv_b[a].at[myq],
                send_sem=send_sems.at[4 * a + k], recv_sem=recv_sems.at[4 * a + k],
                device_id=(qx, qy, c), device_id_type=MESH)

        mine = [pltpu.make_async_copy(xs[a].at[:, c], parts[a], local_sems.at[a]) for a in range(n)]
        first = [from_sibling(a) for a in range(n)]
        for cp in mine + first:
            cp.start()
        second = []
        for b in range(g):
            second += gather_start(b)
        for a in range(n):
            mine[a].wait()
            first[a].wait_recv()
            parts[a][...] = parts[a][...] + recv_a[a][...].astype(F32)
            send_b[a][...] = parts[a][...].astype(jnp.bfloat16)
            for k in range(1, 4):
                cp = to_owner(a, k)
                cp.start()
                second.append(cp)
        second += gather_finish(list(range(g)))
        for a in range(n):
            acc = parts[a][myq]
            for k in range(1, 4):
                to_owner(a, k).wait_recv()
                qx, qy = chip(k)
                acc = acc + recv_b[a][2 * qx + qy].astype(F32)
            outs[a][...] = acc
        for cp in first + second:
            cp.wait_send()

    quarter = lambda a, dt: pltpu.VMEM((4,) + a.shape[2:], dt)
    return pl.pallas_call(
        body, name=name,
        out_shape=[jax.ShapeDtypeStruct(a.shape[2:], F32) for a in arrs]
        + [jax.ShapeDtypeStruct((N_DEV,) + a.shape, a.dtype) for a in gather],
        in_specs=[pl.BlockSpec(memory_space=pl.ANY)] * (2 * n) + [_vmem()] * g, out_specs=[_vmem()] * (n + g),
        scratch_shapes=[quarter(a, F32) for a in arrs] + [quarter(a, jnp.bfloat16) for a in arrs] * 3
        + [pltpu.SemaphoreType.DMA((4 * n,)), pltpu.SemaphoreType.DMA((4 * n,)), pltpu.SemaphoreType.DMA((n,))]
        + _gather_sems(g),
        compiler_params=pltpu.CompilerParams(vmem_limit_bytes=VMEM_LIMIT),
    )(*arrs, *halves, *gather)


def _silu(t):
    return t * (1.0 / (1.0 + jnp.exp(-t)))


def _ada_cols(piece_ref, c_all_ref, w_ref, b_ref):
    w = _mx(w_ref[...])
    for d in range(N_DEV):
        piece_ref[d] = _dot(_mx(_silu(c_all_ref[d])), w) + b_ref[...]


def _rope_split(cd, sd):
    n = cd.shape[0]

    def expand(d):
        rep = jnp.broadcast_to(d[:, None, :], (n, 8, LANE)).reshape(8 * n, LANE)
        return pltpu.roll(rep, 0, 1, stride=16, stride_axis=0)

    c, s = expand(cd), expand(sd)
    lane = _lane_iota(c.shape)
    first = jnp.logical_and(lane >= 64, lane < 80)
    second = jnp.logical_and(lane >= 80, lane < 96)
    ck = jnp.where(first, pltpu.roll(c, 80, 1), jnp.where(second, pltpu.roll(c, 96, 1), 0.0))
    cq = jnp.where(lane < 64, 1.0, ck)
    sa = jnp.where(first, -pltpu.roll(s, 80, 1), 0.0)
    sb = jnp.where(second, pltpu.roll(s, 96, 1), 0.0)
    return cq, ck, sa, sb


def _tile_heads(t):
    return jnp.concatenate([t] * MLA_HEADS, axis=1)


def _pre_fwd(x2, shift, scale, ng, w_in, gq, gkv, w_uq, w_uk, w_uv, rope, S, ts):
    T, D = x2.shape
    nsb = S // ts
    WQ = MLA_HEADS * LANE

    def body(x_ref, sh_ref, sc_ref, ng_ref, wn_ref, gq_ref, gkv_ref, wuq_ref, wuk_ref, wuv_ref,
             cd_ref, sd_ref,
             hb_ref, zq_ref, zkv_ref, ql_ref, kvl_ref, q_ref, k_ref, v_ref, qs_ref, ks_ref, vs_ref, g_ref, wp_ref):
        @pl.when(pl.program_id(0) == 0)
        def _():
            wp_ref[640:704, :] = jnp.zeros((64, D), wp_ref.dtype)
            wp_ref[736:768, :] = jnp.zeros((32, D), wp_ref.dtype)
            for d in range(N_DEV):
                lo, hi = W_IN_SHARD * d, W_IN_SHARD * (d + 1)
                for nat, pad, size in _w_in_row_runs():
                    a, b = max(nat, lo), min(nat + size, hi)
                    if a < b:
                        wp_ref[pad + a - nat:pad + b - nat, :] = wn_ref[d, a - lo:b - lo, :]

        w_ref = wp_ref
        x = x_ref[...]
        r1 = lax.rsqrt(jnp.mean(x * x, axis=-1, keepdims=True) + EPS)
        h = ((x * r1) * ng_ref[...]) * (1.0 + sc_ref[0]) + sh_ref[0]
        hb = _mx(h)
        hb_ref[...] = hb
        zq = _dot_nt(hb, w_ref[0:384, :])
        zq_ref[...] = zq
        rq = lax.rsqrt(jnp.mean(zq * zq, axis=-1, keepdims=True) + EPS)
        ql = _mx((zq * rq) * gq_ref[...])
        ql_ref[...] = ql
        q = _dot(ql, wuq_ref[...])
        cq, ck, sa, sb = _rope_split(cd_ref[...], sd_ref[...])
        q = (q * _tile_heads(cq) + pltpu.roll(q, WQ - 16, 1) * _tile_heads(sa)
             + pltpu.roll(q, 16, 1) * _tile_heads(sb))
        q_ref[...] = _mx(q * MLA_QSCALE)
        zkv = _dot_nt(hb, w_ref[384:640, :])
        zkv_ref[...] = zkv
        rkv = lax.rsqrt(jnp.mean(zkv * zkv, axis=-1, keepdims=True) + EPS)
        kvl = _mx((zkv * rkv) * gkv_ref[...])
        kvl_ref[...] = kvl
        kr = _dot_nt(hb, w_ref[640:768, :])
        kpe = kr * ck + pltpu.roll(kr, LANE - 16, 1) * sa + pltpu.roll(kr, 16, 1) * sb
        kf = _dot(kvl, wuk_ref[...])
        k_ref[...] = _mx(kf + jnp.concatenate([kpe] * MLA_HEADS, axis=1))
        v_ref[...] = _mx(_dot(kvl, wuv_ref[...]))
        g_ref[:, 0:512] = _dot_nt(hb, w_ref[768:1280, :])
        qs_ref[...] = _mx(_dot_nt(hb, w_ref[1280:1792, :]) * (SWA_SCALE * LOG2E))
        ks_ref[...] = _mx(_dot_nt(hb, w_ref[1792:1920, :]))
        vs_ref[...] = _mx(_dot_nt(hb, w_ref[1920:2048, :]))
        g_ref[:, 512:1024] = _dot_nt(hb, w_ref[2048:2560, :])

    row = lambda w: pl.BlockSpec((ts, w), lambda i: (i, 0))
    dense = pl.BlockSpec((ts // 8, LANE), lambda i: (i, 0))
    full = lambda a: pl.BlockSpec(a.shape, lambda i: (0,) * a.ndim)
    per_b = pl.BlockSpec((1, 1, D), lambda i: (i // nsb, 0, 0))
    out_w = [(D, _MXU_DTYPE), (384, F32), (256, F32), (384, _MXU_DTYPE), (256, _MXU_DTYPE), (WQ, _MXU_DTYPE),
             (WQ, _MXU_DTYPE), (512, _MXU_DTYPE), (512, _MXU_DTYPE), (128, _MXU_DTYPE), (128, _MXU_DTYPE), (1024, F32)]
    return pl.pallas_call(
        body, name="pre_fwd", grid=(T // ts,),
        out_shape=[jax.ShapeDtypeStruct((T, w), dt) for w, dt in out_w] + [jax.ShapeDtypeStruct((D_IN_PAD, D), w_in.dtype)],
        in_specs=[row(D), per_b, per_b, full(ng), full(w_in), full(gq), full(gkv), full(w_uq), full(w_uk), full(w_uv),
                  dense, dense],
        out_specs=[row(w) for w, _ in out_w] + [pl.BlockSpec((D_IN_PAD, D), lambda i: (0, 0))],
        compiler_params=_cparams(("arbitrary",)),
    )(x2, shift, scale, ng, w_in, gq, gkv, w_uq, w_uk, w_uv, *rope)


def _mla_fwd(q3, k3, v3, tq, tk):
    B, S, _ = q3.shape
    nq = S // tq
    assert tq == tk
    HPS = 2
    NH = 2 * HPS

    def body(q_ref, k_ref, v_ref, o_ref, lse_ref):
        rows = lax.broadcasted_iota(jnp.int32, (tq, tk), 0)
        cols = lax.broadcasted_iota(jnp.int32, (tq, tk), 1)
        low_k = _lane_iota((tk, LANE)) < 64
        low = _lane_iota((tq, LANE)) < 64

        def step(qs, kt, carry, masked):
            r0 = kt * tk
            out = []
            for j in range(NH):
                v2 = v_ref[0, r0:r0 + tk, LANE * (j // 2):LANE * (j // 2 + 1)]
                vj = (jnp.where(low_k, v2, 1.0) if j % 2 == 0 else jnp.where(low_k, 1.0, v2)).astype(v2.dtype)
                m, acc = carry[j]
                s = _dot_nt(qs[j], k_ref[0, r0:r0 + tk, LANE * j:LANE * (j + 1)])
                if masked:
                    s = jnp.where(rows >= cols, s, NEG_INF)
                m_new = jnp.maximum(m, jnp.max(s, axis=1, keepdims=True))
                alpha = jnp.exp2(m - m_new)
                p = jnp.exp2(s - m_new)
                acc = alpha * acc + _dot(_mx(p), vj)
                out.append((m_new, acc))
            return tuple(out)

        for qi in range(nq):
            r = slice(qi * tq, (qi + 1) * tq)
            qs = [q_ref[0, r, LANE * j:LANE * (j + 1)] for j in range(NH)]
            init = (jnp.full((tq, 1), NEG_INF, F32), jnp.zeros((tq, LANE), F32))
            carry = (init,) * NH
            for kt in range(qi):
                carry = step(qs, kt, carry, False)
            carry = step(qs, qi, carry, True)
            for t in range(HPS):
                (m0, a0), (m1, a1) = carry[2 * t], carry[2 * t + 1]
                l0 = jnp.where(low, pltpu.roll(a0, 64, 1), a0)
                l1 = jnp.where(low, a1, pltpu.roll(a1, 64, 1))
                o_ref[0, r, LANE * t:LANE * (t + 1)] = jnp.where(low, a0 / l0, a1 / l1)
                lse_ref[0, 2 * t, r, :] = m0 + jnp.log2(l0)
                lse_ref[0, 2 * t + 1, r, :] = m1 + jnp.log2(l1)

    grp = lambda w: pl.BlockSpec((1, S, w), lambda b, hp: (b, 0, hp))
    return pl.pallas_call(
        body, name="mla_fwd", grid=(B, MLA_HEADS // NH),
        out_shape=[jax.ShapeDtypeStruct((B, S, 512), F32), jax.ShapeDtypeStruct((B, MLA_HEADS, S, LANE), F32)],
        in_specs=[grp(NH * LANE), grp(NH * LANE), grp(HPS * LANE)],
        out_specs=[grp(HPS * LANE), pl.BlockSpec((1, NH, S, LANE), lambda b, hp: (b, hp, 0, 0))],
        compiler_params=_cparams(("arbitrary", "arbitrary")),
    )(q3, k3, v3)


def _mla_bwd(q3, k3, v3, o3, do3, lse, tq, tk, hps=2):
    B, S, _ = q3.shape
    nq, nk = S // tq, S // tk
    nh = 2 * hps
    assert tk % tq == 0

    def body(q_ref, k_ref, v_ref, o_ref, do_ref, lse_ref, dq_ref, dk_ref, dv_ref, dkt_ref, dvt_ref):
        rows = lax.broadcasted_iota(jnp.int32, (tq, tk), 0)
        cols = lax.broadcasted_iota(jnp.int32, (tq, tk), 1)
        lane = _lane_iota((tq, LANE))

        def q_tile(qi, _):
            r = pl.ds(pl.multiple_of(qi * tq, tq), tq)
            heads = []
            for j in range(nh):
                lanes = slice(LANE * j, LANE * (j + 1))
                pair = slice(LANE * (j // 2), LANE * (j // 2 + 1))
                do = jnp.where((lane < 64) if j % 2 == 0 else (lane >= 64), do_ref[0, r, pair], 0.0)
                q = q_ref[0, r, lanes]
                heads.append((lanes, pair, q, _mx(q.astype(F32).T), _mx(do), _mx(do.T),
                              jnp.sum(do * o_ref[0, r, pair], axis=1, keepdims=True),
                              jnp.concatenate([lse_ref[0, j, r, :]] * (tk // LANE), axis=1)))
            n_full = (qi * tq) // tk

            def k_tile(kt, dqs, masked):
                kr = pl.ds(pl.multiple_of(kt * tk, tk), tk)
                first = masked and (qi * tq) % tk == 0
                out = []
                dvt = [None] * hps
                for j, (lanes, pair, q, qt, dob, dot_, dcol, lse_c) in enumerate(heads):
                    k = k_ref[0, kr, lanes]
                    s = _dot_nt(q, k)
                    if masked:
                        s = jnp.where(rows + qi * tq >= cols + kt * tk, s, NEG_INF)
                    p = jnp.exp2(s - lse_c)
                    dp = _dot_nt(dob, v_ref[0, kr, pair])
                    dsb = _mx(p * (dp - dcol))
                    if first:
                        dkt_ref[j, kt] = _dot(qt, dsb)
                    else:
                        dkt_ref[j, kt] += _dot(qt, dsb)
                    pv = _dot(dot_, _mx(p))
                    dvt[j // 2] = pv if dvt[j // 2] is None else dvt[j // 2] + pv
                    out.append(dqs[j] + _dot(dsb, k))
                for t in range(hps):
                    if first:
                        dvt_ref[t, kt] = dvt[t]
                    else:
                        dvt_ref[t, kt] += dvt[t]
                return tuple(out)

            dqs = (jnp.zeros((tq, LANE), F32),) * nh
            for kt in range(n_full):
                dqs = k_tile(kt, dqs, False)
            dqs = k_tile(n_full, dqs, True)
            for j in range(nh):
                dq_ref[0, r, heads[j][0]] = (MLA_SCALE * dqs[j]).astype(dq_ref.dtype)
            return 0

        for qi in range(nq):
            q_tile(qi, 0)

        def flush(kt, _):
            kr = pl.ds(pl.multiple_of(kt * tk, tk), tk)
            for j in range(nh):
                dk_ref[0, kr, LANE * j:LANE * (j + 1)] = ((1.0 / LOG2E) * dkt_ref[j, kt].T).astype(dk_ref.dtype)
            for t in range(hps):
                dv_ref[0, kr, LANE * t:LANE * (t + 1)] = dvt_ref[t, kt].T.astype(dv_ref.dtype)
            return 0

        lax.fori_loop(0, nk, flush, 0)

    grp = lambda w: pl.BlockSpec((1, S, w), lambda b, hp: (b, 0, hp))
    return pl.pallas_call(
        body, name="mla_bwd", grid=(B, MLA_HEADS // nh),
        out_shape=[jax.ShapeDtypeStruct((B, S, 1024), MLA_GRAD_DTYPE), jax.ShapeDtypeStruct((B, S, 1024), MLA_GRAD_DTYPE),
                   jax.ShapeDtypeStruct((B, S, 512), MLA_GRAD_DTYPE)],
        in_specs=[grp(nh * LANE), grp(nh * LANE), grp(hps * LANE), grp(hps * LANE), grp(hps * LANE),
                  pl.BlockSpec((1, nh, S, LANE), lambda b, hp: (b, hp, 0, 0))],
        out_specs=[grp(nh * LANE), grp(nh * LANE), grp(hps * LANE)],
        scratch_shapes=[pltpu.VMEM((nh, nk, LANE, tk), F32), pltpu.VMEM((hps, nk, LANE, tk), F32)],
        compiler_params=_cparams(("arbitrary", "arbitrary")),
    )(q3, k3, v3, o3, do3, lse)


def _swa_consts(sink_ref):
    W = WINDOW
    row = lax.broadcasted_iota(jnp.int32, (4 * W, LANE), 0)
    out = []
    for g in range(SWA_KV_HEADS):
        slope = jnp.zeros((4 * W, LANE), F32)
        sink = jnp.zeros((4 * W, LANE), F32)
        for p in range(4):
            h = p + 4 * g
            here = jnp.logical_and(row >= W * p, row < W * (p + 1))
            slope = jnp.where(here, float(LOG2E * 2.0 ** (-8.0 * (h + 1) / SWA_HEADS)), slope)
            sink = jnp.where(here, LOG2E * sink_ref[:, h:h + 1], sink)
        out.append((jnp.concatenate([slope, slope], axis=1), sink))
    return out


def _wide(col):
    return jnp.concatenate([col, col], axis=1)


def _swa_block(q_ref, k_ref, v_ref, pb_ref, pr_ref, n, i, ij):
    W = WINDOW
    kb = jnp.maximum(n - 1, 0)
    r = pl.ds(pl.multiple_of(i * W, W), W)
    kr = pl.ds(pl.multiple_of(kb * W, W), 2 * W)
    q4, k2, v2 = q_ref[0, r, :], k_ref[0, kr, :], v_ref[0, kr, :]
    pq = _wide(pb_ref[0, r, :])
    pk = jnp.concatenate([pr_ref[0, pl.ds(kb, 1), :], pr_ref[0, pl.ds(kb + 1, 1), :]], axis=1)
    rel = ij + (n - kb) * W
    dist = jnp.where(jnp.logical_and(rel >= 0, rel < W), pq - pk, float("inf"))
    return r, kb, kr, q4, k2, v2, jnp.concatenate([dist] * 4, axis=0)


def _swa_stack(x4, g, dtype):
    lane = _lane_iota((WINDOW, LANE))
    mine = (lane < 64) if g == 0 else (lane >= 64)
    return jnp.concatenate([jnp.where(mine, x4[:, LANE * p:LANE * (p + 1)], 0).astype(dtype) for p in range(4)], axis=0)


def _swa_unstack(ref, r, lo, hi, scale=None):
    W = WINDOW
    low = _lane_iota((W, LANE)) < 64
    for p in range(4):
        t = jnp.where(low, lo[W * p:W * (p + 1)], hi[W * p:W * (p + 1)])
        ref[0, r, LANE * p:LANE * (p + 1)] = t if scale is None else scale * t


def _swa_fwd(qs3, ks3, vs3, posb, posr, sinks):
    B, S, _ = qs3.shape
    W = WINDOW
    nb = S // W
    nh = SWA_SEQ_SPLIT
    nbh = nb // nh

    def body(q_ref, k_ref, v_ref, pb_ref, pr_ref, sink_ref, o_ref, lse_ref):
        ij = lax.broadcasted_iota(jnp.int32, (W, 2 * W), 0) - lax.broadcasted_iota(jnp.int32, (W, 2 * W), 1)
        consts = _swa_consts(sink_ref)
        n0 = pl.program_id(1) * nbh

        def blk(i, _):
            n = n0 + i
            r, _, _, q4, k2, v2, dist4 = _swa_block(q_ref, k_ref, v_ref, pb_ref, pr_ref, n, i, ij)
            o_g = []
            for g, (slope, sink) in enumerate(consts):
                s = _dot_nt(_swa_stack(q4, g, q4.dtype), k2) - slope * dist4
                m = jnp.maximum(jnp.max(s, axis=1, keepdims=True), sink)
                e = jnp.exp2(s - _wide(m))
                l = jnp.sum(e, axis=1, keepdims=True) + jnp.exp2(sink - m)
                o_g.append(_dot(_mx(e), v2) * (1.0 / l))
                lse_ref[0, i, g] = m + jnp.log2(l)
            _swa_unstack(o_ref, r, o_g[0], o_g[1])
            return 0

        for i in range(nbh):
            blk(i, 0)

    seq = lambda w: pl.BlockSpec((1, S, w), lambda b, h: (b, 0, 0))
    part = lambda w: pl.BlockSpec((1, S // nh, w), lambda b, h: (b, h, 0))
    lse_spec = pl.BlockSpec((1, nbh, 2, 4 * W, LANE), lambda b, h: (b, h, 0, 0, 0))
    return pl.pallas_call(
        body, name="swa_fwd", grid=(B, nh),
        out_shape=[jax.ShapeDtypeStruct((B, S, 512), F32), jax.ShapeDtypeStruct((B, nb, 2, 4 * W, LANE), F32)],
        in_specs=[part(512), seq(LANE), seq(LANE), part(LANE), pl.BlockSpec((1, nb, W), lambda b, h: (b, 0, 0)),
                  pl.BlockSpec((1, LANE), lambda b, h: (0, 0))],
        out_specs=[part(512), lse_spec],
        compiler_params=_cparams(("arbitrary", "arbitrary")),
    )(qs3, ks3, vs3, posb, posr, sinks)


def _swa_bwd(qs3, ks3, vs3, posb, posr, sinks, os3, do3, lse):
    B, S, _ = qs3.shape
    W = WINDOW
    nb = S // W
    nh = SWA_SEQ_SPLIT
    nbh = nb // nh
    assert nbh % 2 == 0 and nbh * nh * W == S

    def body(q_ref, k_ref, v_ref, pb_ref, pr_ref, sink_ref, o_ref, do_ref, lse_ref, dq_ref, dk_ref, dv_ref, dsink_ref,
             dkt_ref, dvt_ref):
        lane1 = _lane_iota((1, LANE))
        ij = lax.broadcasted_iota(jnp.int32, (W, 2 * W), 0) - lax.broadcasted_iota(jnp.int32, (W, 2 * W), 1)
        consts = _swa_consts(sink_ref)
        hh = pl.program_id(1)
        n0 = hh * nbh

        @pl.when(hh == 0)
        def _():
            dkt_ref[...] = jnp.zeros_like(dkt_ref)
            dvt_ref[...] = jnp.zeros_like(dvt_ref)

        @pl.when(jnp.logical_and(pl.program_id(0) == 0, hh == 0))
        def _():
            dsink_ref[...] = jnp.zeros_like(dsink_ref)

        def blk(i, dsink):
            n = n0 + i
            r, kb, _, q4, k2, v2, dist4 = _swa_block(q_ref, k_ref, v_ref, pb_ref, pr_ref, n, i, ij)
            o4, do4 = o_ref[0, r, :], do_ref[0, r, :]
            dq_g = []
            dkt = jnp.zeros((LANE, 2 * W), F32)
            dvt = jnp.zeros((LANE, 2 * W), F32)
            for g, (slope, sink) in enumerate(consts):
                q_st = _swa_stack(q4, g, F32)
                do_st = _swa_stack(do4, g, F32)
                dcol = jnp.sum(do_st * _swa_stack(o4, g, F32), axis=1, keepdims=True)
                qst, dob = _mx(q_st), _mx(do_st)
                lse_c = lse_ref[0, i, g]
                pr = jnp.exp2(_dot_nt(qst, k2) - slope * dist4 - _wide(lse_c))
                dsb = _mx(pr * (_dot_nt(dob, v2) - dcol))
                psd = jnp.exp2(sink - lse_c)[:, 0:1] * dcol
                for p in range(4):
                    dsink = dsink - jnp.where(lane1 == p + 4 * g,
                                              jnp.sum(psd[W * p:W * (p + 1)], axis=0, keepdims=True), 0.0)
                dq_g.append(_dot(dsb, k2))
                dkt = dkt + _dot(_mx(q_st.T), dsb)
                dvt = dvt + _dot(_mx(do_st.T), _mx(pr))
            _swa_unstack(dq_ref, r, dq_g[0], dq_g[1], SWA_SCALE)
            dkt_ref[kb] += dkt[:, 0:W]
            dkt_ref[kb + 1] += dkt[:, W:2 * W]
            dvt_ref[kb] += dvt[:, 0:W]
            dvt_ref[kb + 1] += dvt[:, W:2 * W]
            return dsink

        dsink = jnp.zeros((1, LANE), F32)
        for i in range(nbh):
            dsink = blk(i, dsink)
        dsink_ref[...] += dsink

        @pl.when(hh == nh - 1)
        def _():
            def flush(n, _):
                r = pl.ds(pl.multiple_of(n * W, W), W)
                dk_ref[0, r, :] = (1.0 / LOG2E) * dkt_ref[n].T
                dv_ref[0, r, :] = dvt_ref[n].T
                return 0

            lax.fori_loop(0, nb, flush, 0)

    seq = lambda w: pl.BlockSpec((1, S, w), lambda b, h: (b, 0, 0))
    part = lambda w: pl.BlockSpec((1, S // nh, w), lambda b, h: (b, h, 0))
    return pl.pallas_call(
        body, name="swa_bwd", grid=(B, nh),
        out_shape=[jax.ShapeDtypeStruct((B, S, 512), F32), jax.ShapeDtypeStruct((B, S, LANE), F32),
                   jax.ShapeDtypeStruct((B, S, LANE), F32), jax.ShapeDtypeStruct((1, LANE), F32)],
        in_specs=[part(512), seq(LANE), seq(LANE), part(LANE), pl.BlockSpec((1, nb, W), lambda b, h: (b, 0, 0)),
                  pl.BlockSpec((1, LANE), lambda b, h: (0, 0)), part(512),
                  pl.BlockSpec((1, S // nh, 512), lambda b, h: (b, h, 1)),
                  pl.BlockSpec((1, nbh, 2, 4 * W, LANE), lambda b, h: (b, h, 0, 0, 0))],
        out_specs=[part(512), seq(LANE), seq(LANE), pl.BlockSpec((1, LANE), lambda b, h: (0, 0))],
        scratch_shapes=[pltpu.VMEM((nb, LANE, W), F32), pltpu.VMEM((nb, LANE, W), F32)],
        compiler_params=_cparams(("arbitrary", "arbitrary")),
    )(qs3, ks3, vs3, posb, posr, sinks, os3, do3, lse)


def _post(om, osw, g, w_out, x2, gate, fg, tgt, S, ts):
    T, D = x2.shape
    nsb = S // ts

    def body(om_ref, os_ref, g_ref, w_ref, x_ref, gate_ref, fg_ref, t_ref,
             dx2_ref, do_ref, dg_ref, loss_ref, dfg_ref, dgate_ref, dw_ref):
        i = pl.program_id(0)
        gv = g_ref[...]
        sg = 1.0 / (1.0 + jnp.exp(-gv))
        silu = gv * sg
        o = jnp.concatenate([om_ref[...], os_ref[...]], axis=1)
        ab = _mx(o * silu)
        y = _dot(ab, w_ref[...])
        gate = gate_ref[0]
        xo = x_ref[...] + gate * y
        r2 = lax.rsqrt(jnp.mean(xo * xo, axis=-1, keepdims=True) + EPS)
        xh = xo * r2
        fg = fg_ref[...]
        diff = xh * fg - t_ref[...]
        sq = jnp.sum(diff * diff, axis=0, keepdims=True)
        part = sq[:, 0:LANE]
        for t in range(1, D // LANE):
            part = part + sq[:, LANE * t:LANE * (t + 1)]
        dout = diff * (1.0 / D)
        dxh = dout * fg
        dx2 = r2 * (dxh - xh * jnp.mean(dxh * xh, axis=-1, keepdims=True))
        dx2_ref[...] = dx2
        dyb = _mx(dx2 * gate)
        da = _dot_nt(dyb, w_ref[...])
        do_ref[...] = da * silu
        dg_ref[...] = _mx(da * o * (sg * (1.0 + gv * (1.0 - sg))))

        @pl.when(i == 0)
        def _():
            loss_ref[...] = jnp.zeros_like(loss_ref)
            dfg_ref[...] = jnp.zeros_like(dfg_ref)
            dw_ref[...] = jnp.zeros_like(dw_ref)

        @pl.when(i % nsb == 0)
        def _():
            dgate_ref[...] = jnp.zeros_like(dgate_ref)

        loss_ref[...] += (0.5 / D) * part
        dfg_ref[...] += jnp.sum(dout * xh, axis=0, keepdims=True)
        dgate_ref[0] += jnp.sum(dx2 * y, axis=0, keepdims=True)
        dw_ref[...] += _dot_tn(ab, dyb)

        @pl.when(i == T // ts - 1)
        def _():
            chunks = [dw_ref[512 + 64 * PAIR_INV[h]:512 + 64 * (PAIR_INV[h] + 1), :] for h in range(SWA_HEADS)]
            for h in range(SWA_HEADS):
                dw_ref[512 + 64 * h:512 + 64 * (h + 1), :] = chunks[h]

    row = lambda w: pl.BlockSpec((ts, w), lambda i: (i, 0))
    full = lambda a: pl.BlockSpec(a.shape, lambda i: (0,) * a.ndim)
    per_b = pl.BlockSpec((1, 1, D), lambda i: (i // nsb, 0, 0))
    return pl.pallas_call(
        body, name="post", grid=(T // ts,),
        out_shape=[jax.ShapeDtypeStruct((T, D), F32), jax.ShapeDtypeStruct((T, 1024), F32),
                   jax.ShapeDtypeStruct((T, 1024), _MXU_DTYPE), jax.ShapeDtypeStruct((1, LANE), F32),
                   jax.ShapeDtypeStruct((1, D), F32), jax.ShapeDtypeStruct(gate.shape, F32),
                   jax.ShapeDtypeStruct(w_out.shape, F32)],
        in_specs=[row(512), row(512), row(1024), full(w_out), row(D), per_b, full(fg), row(D)],
        out_specs=[row(D), row(1024), row(1024),
                   pl.BlockSpec((1, LANE), lambda i: (0, 0)), pl.BlockSpec((1, D), lambda i: (0, 0)), per_b,
                   full(w_out)],
        compiler_params=_cparams(("arbitrary",)),
    )(om, osw, g, w_out, x2, gate, fg, tgt)


def _pre_bwd(dq, dk, dv, dqs, dks, dvs, dg, zq, zkv, ql, kvl, x2, dx2, scale, ng, w_in, gq, gkv, w_uq, w_uk, w_uv,
             rope, S, ts):
    T, D = x2.shape
    nsb = S // ts
    WQ = MLA_HEADS * LANE

    def body(dq_ref, dk_ref, dv_ref, dqs_ref, dks_ref, dvs_ref, dg_ref, zq_ref, zkv_ref, ql_ref, kvl_ref, x_ref, dx2_ref,
             sc_ref,
             ng_ref, w_ref, gq_ref, gkv_ref, wuq_ref, wuk_ref, wuv_ref, cd_ref, sd_ref,
             gx_ref, dz_ref, dgq_ref, dgkv_ref, dng_ref, dsh_ref, dsc_ref, dwuq_ref, dwuk_ref, dwuv_ref):
        i = pl.program_id(0)

        @pl.when(i == 0)
        def _():
            dgq_ref[...] = jnp.zeros_like(dgq_ref)
            dgkv_ref[...] = jnp.zeros_like(dgkv_ref)
            dng_ref[...] = jnp.zeros_like(dng_ref)
            dwuq_ref[...] = jnp.zeros_like(dwuq_ref)
            dwuk_ref[...] = jnp.zeros_like(dwuk_ref)
            dwuv_ref[...] = jnp.zeros_like(dwuv_ref)

        @pl.when(i % nsb == 0)
        def _():
            dsh_ref[...] = jnp.zeros_like(dsh_ref)
            dsc_ref[...] = jnp.zeros_like(dsc_ref)

        def norm_bwd(z, dy, gain):
            r = lax.rsqrt(jnp.mean(z * z, axis=-1, keepdims=True) + EPS)
            zh = z * r
            dzh = dy * gain
            return r * (dzh - zh * jnp.mean(dzh * zh, axis=-1, keepdims=True)), jnp.sum(dy * zh, axis=0, keepdims=True)

        tables = _rope_split(cd_ref[...], sd_ref[...])
        ng = ng_ref[...]
        sc1 = 1.0 + sc_ref[0]

        def rows_chain(rs):
            cq, ck, sa, sb = (t[rs] for t in tables)
            dqr = dq_ref[rs, :].astype(F32)
            dqb = _mx(dqr * _tile_heads(cq) + pltpu.roll(dqr * _tile_heads(sa), 16, 1)
                      + pltpu.roll(dqr * _tile_heads(sb), WQ - 16, 1))
            p_uq = _dot_tn(ql_ref[rs, :], dqb)
            dzq, dgq = norm_bwd(zq_ref[rs, :], _dot_nt(dqb, wuq_ref[...]), gq_ref[...])
            dkr = dk_ref[rs, :].astype(F32)
            dkb = _mx(dkr)
            p_uk = _dot_tn(kvl_ref[rs, :], dkb)
            dvb = _mx(dv_ref[rs, :])
            p_uv = _dot_tn(kvl_ref[rs, :], dvb)
            dzkv, dgkv = norm_bwd(zkv_ref[rs, :], _dot_nt(dkb, wuk_ref[...]) + _dot_nt(dvb, wuv_ref[...]), gkv_ref[...])
            dkpe = dkr[:, 0:LANE]
            for h in range(1, MLA_HEADS):
                dkpe = dkpe + dkr[:, LANE * h:LANE * (h + 1)]
            dkro = dkpe * ck + pltpu.roll(dkpe * sa, 16, 1) + pltpu.roll(dkpe * sb, LANE - 16, 1)
            dz_ref[rs, 0:384] = _mx(dzq)
            dz_ref[rs, 384:640] = _mx(dzkv)
            dz_ref[rs, 640:768] = _mx(dkro)
            dz_ref[rs, 768:1280] = dg_ref[rs, 0:512]
            dz_ref[rs, 1280:1792] = _mx(dqs_ref[rs, :])
            dz_ref[rs, 1792:1920] = _mx(dks_ref[rs, :])
            dz_ref[rs, 1920:2048] = _mx(dvs_ref[rs, :])
            dz_ref[rs, 2048:2560] = dg_ref[rs, 512:1024]
            dh = _dot(dz_ref[rs, :], w_ref[...])
            x = x_ref[rs, :]
            r1 = lax.rsqrt(jnp.mean(x * x, axis=-1, keepdims=True) + EPS)
            xn = x * r1
            dxn = dh * ng * sc1
            gx_ref[rs, :] = dx2_ref[rs, :] + r1 * (dxn - xn * jnp.mean(dxn * xn, axis=-1, keepdims=True))
            return (p_uq, p_uk, p_uv, dgq, dgkv, jnp.sum(dh, axis=0, keepdims=True),
                    jnp.sum(dh * (xn * ng), axis=0, keepdims=True), jnp.sum(dh * xn * sc1, axis=0, keepdims=True))

        hr = ts // 2
        parts = [rows_chain(slice(hr * t, hr * (t + 1))) for t in range(2)]
        p_uq, p_uk, p_uv, dgq, dgkv, dsh, dsc, dng = (a + b for a, b in zip(*parts))
        dwuq_ref[...] += p_uq
        dwuk_ref[...] += p_uk
        dwuv_ref[...] += p_uv
        dgq_ref[...] += dgq
        dgkv_ref[...] += dgkv
        dsh_ref[0] += dsh
        dsc_ref[0] += dsc
        dng_ref[...] += dng

    row = lambda w: pl.BlockSpec((ts, w), lambda i: (i, 0))
    full = lambda a: pl.BlockSpec(a.shape, lambda i: (0,) * a.ndim, pipeline_mode=pl.Buffered(1))
    per_b = pl.BlockSpec((1, 1, D), lambda i: (i // nsb, 0, 0))
    dense = pl.BlockSpec((ts // 8, LANE), lambda i: (i, 0))
    vec = lambda w: pl.BlockSpec((1, w), lambda i: (0, 0))
    return pl.pallas_call(
        body, name="pre_bwd", grid=(T // ts,),
        out_shape=[jax.ShapeDtypeStruct((T, D), F32), jax.ShapeDtypeStruct((T, D_IN_PAD), _MXU_DTYPE), jax.ShapeDtypeStruct((1, 384), F32),
                   jax.ShapeDtypeStruct((1, 256), F32), jax.ShapeDtypeStruct((1, D), F32),
                   jax.ShapeDtypeStruct(scale.shape, F32), jax.ShapeDtypeStruct(scale.shape, F32),
                   jax.ShapeDtypeStruct(w_uq.shape, F32), jax.ShapeDtypeStruct(w_uk.shape, F32),
                   jax.ShapeDtypeStruct(w_uv.shape, F32)],
        in_specs=[row(WQ), row(WQ), row(512), row(512), row(LANE), row(LANE), row(1024), row(384), row(256), row(384),
                  row(256), row(D), row(D), per_b, full(ng), full(w_in), full(gq), full(gkv), full(w_uq), full(w_uk), full(w_uv),
                  dense, dense],
        out_specs=[row(D), row(D_IN_PAD), vec(384), vec(256), vec(D), per_b, per_b, full(w_uq), full(w_uk), full(w_uv)],
        compiler_params=_cparams(("arbitrary",)),
    )(dq, dk, dv, dqs, dks, dvs, dg, zq, zkv, ql, kvl, x2, dx2, scale, ng, w_in, gq, gkv, w_uq, w_uk, w_uv, *rope)


def _w_in_row_runs():
    runs = [(0, 0, 640), (640, 704, 32), (672, 768, 512)]
    runs += [(1184 + 64 * h, 1280 + 64 * PAIR_INV[h], 64) for h in range(8)]
    runs += [(1696, 1792, 256)]
    runs += [(1952 + 64 * h, 2048 + 64 * PAIR_INV[h], 64) for h in range(8)]
    return runs


W_IN_SHARD = D_IN // N_DEV
W_IN_SLOT = 320


def _dw_in_t(dz, hb, tn, tk):
    T, M = dz.shape
    N = hb.shape[1]
    nk = T // tk

    def body(a_ref, b_ref, o_ref, ob_ref, acc_ref):
        k = pl.program_id(1)

        @pl.when(k == 0)
        def _():
            acc_ref[...] = jnp.zeros_like(acc_ref)

        acc_ref[...] += _dot_tn(a_ref[...], b_ref[...])

        @pl.when(k == nk - 1)
        def _():
            for d in range(N_DEV):
                lo, hi = W_IN_SHARD * d, W_IN_SHARD * (d + 1)
                for nat, pad, size in _w_in_row_runs():
                    a, b = max(nat, lo), min(nat + size, hi)
                    if a < b:
                        piece = acc_ref[pad + a - nat:pad + b - nat, :]
                        o_ref[d, a - lo:b - lo, :] = piece
                        ob_ref[d, a - lo:b - lo, :] = piece.astype(ob_ref.dtype)
                o_ref[d, W_IN_SHARD:W_IN_SLOT, :] = jnp.zeros((W_IN_SLOT - W_IN_SHARD, tn), F32)
                ob_ref[d, W_IN_SHARD:W_IN_SLOT, :] = jnp.zeros((W_IN_SLOT - W_IN_SHARD, tn), ob_ref.dtype)

    slots = pl.BlockSpec((N_DEV, W_IN_SLOT, tn), lambda j, k: (0, 0, j))
    return pl.pallas_call(
        body, name="dw_in", grid=(N // tn, nk),
        out_shape=[jax.ShapeDtypeStruct((N_DEV, W_IN_SLOT, N), F32), jax.ShapeDtypeStruct((N_DEV, W_IN_SLOT, N), jnp.bfloat16)],
        in_specs=[pl.BlockSpec((tk, M), lambda j, k: (k, 0)), pl.BlockSpec((tk, tn), lambda j, k: (k, j))],
        out_specs=[slots, slots],
        scratch_shapes=[pltpu.VMEM((M, tn), F32)],
        compiler_params=_cparams(("arbitrary", "arbitrary")),
    )(dz, hb)


def _finalize(parts_all, dmod_all, dmod_cols, c_act_all):
    nparts = parts_all.shape[-1]
    nmod = dmod_all.shape[-1]

    def body(p_ref, dm_ref, dmc_ref, c_ref, ps_ref, loss_ref, db_ref, dw_ref):
        acc = p_ref[0]
        for j in range(1, N_DEV):
            acc = acc + p_ref[j]
        ps_ref[...] = acc
        loss_ref[...] = jnp.sum(acc[:, 0:LANE], axis=1, keepdims=True)
        db = dm_ref[0:1, :]
        for j in range(1, dm_ref.shape[0]):
            db = db + dm_ref[j:j + 1, :]
        db_ref[...] = db
        dw_ref[...] = _dot_tn(_mx(_silu(c_ref[...])), _mx(dmc_ref[...]))

    return pl.pallas_call(
        body, name="finalize",
        out_shape=[jax.ShapeDtypeStruct((1, nparts), F32), jax.ShapeDtypeStruct((1, 1), F32),
                   jax.ShapeDtypeStruct((1, nmod), F32),
                   jax.ShapeDtypeStruct((c_act_all.shape[1], dmod_cols.shape[1]), F32)],
        in_specs=[_vmem()] * 4, out_specs=[_vmem()] * 4,
        compiler_params=_cparams(),
    )(parts_all, dmod_all, dmod_cols, c_act_all)


def _adamw(ws, gs, ms, vs):
    n = len(ws)

    def body(*refs):
        w_refs, g_refs, m_refs, v_refs, d_refs, nm_refs, nv_refs = (refs[t * n:(t + 1) * n] for t in range(7))
        for t in range(n):
            gv = g_refs[t][...]
            nm = ADAM_B1 * m_refs[t][...] + (1.0 - ADAM_B1) * gv
            nv = ADAM_B2 * v_refs[t][...] + (1.0 - ADAM_B2) * (gv * gv)
            m_hat = nm / (1.0 - ADAM_B1 ** ADAM_STEP)
            v_hat = nv / (1.0 - ADAM_B2 ** ADAM_STEP)
            d_refs[t][...] = -ADAM_LR * (m_hat / (jnp.sqrt(v_hat) + ADAM_EPS) + ADAM_WD * w_refs[t][...])
            nm_refs[t][...] = nm
            nv_refs[t][...] = nv

    out = pl.pallas_call(
        body, name="adamw",
        out_shape=[jax.ShapeDtypeStruct(w.shape, F32) for w in ws] * 3,
        in_specs=[_vmem()] * (4 * n), out_specs=[_vmem()] * (3 * n),
        compiler_params=_cparams(),
    )(*ws, *gs, *ms, *vs)
    return out[:n], out[n:2 * n], out[2 * n:]


def _pair_perm(a, axis, order):
    a = jnp.moveaxis(a, axis, -1)
    lead = a.shape[:-1]
    a = a.reshape(lead + (8, 64))[..., list(order), :].reshape(lead + (512,))
    return jnp.moveaxis(a, -1, axis)


def _rope_table(positions):
    T = positions.size
    inv = ROPE_THETA ** (-jnp.arange(0, MLA_ROPE, 2, dtype=F32) / MLA_ROPE)
    pos = jnp.repeat(positions.reshape(T // 8, 8)[:, ::-1].astype(F32), MLA_ROPE // 2, axis=1)
    ang = pos * jnp.tile(inv, 8)[None, :]
    return jnp.cos(ang), jnp.sin(ang)


def _local_step(x, mod, positions, ng, w_in_t, gq, gkv, w_uq, w_ukv, sinks, w_out, fg, tgt,
                ts=512, fq=512, fk=512, bq=512, bk=512):
    B, S, D = x.shape
    T = B * S
    x2 = x.reshape(T, D)
    shift, scale, gate = (mod[:, None, k * D:(k + 1) * D] for k in range(3))
    w_uq_p = jnp.pad(w_uq.reshape(Q_LORA, MLA_HEADS, 96), ((0, 0), (0, 0), (0, 32))).reshape(Q_LORA, MLA_HEADS * LANE)
    w_ukv3 = w_ukv.reshape(KV_LORA, MLA_HEADS, 128)
    w_uk_p = jnp.pad(w_ukv3[:, :, :64], ((0, 0), (0, 0), (0, 64))).reshape(KV_LORA, MLA_HEADS * LANE)
    w_uv = w_ukv3[:, :, 64:].reshape(KV_LORA, 512)
    w_out_p = jnp.concatenate([w_out[:512], _pair_perm(w_out[512:], 0, PAIR_ORDER)], axis=0)
    rope = _rope_table(positions)
    posf = positions.astype(F32)
    posb = jnp.broadcast_to(posf[:, :, None], (B, S, LANE))
    posr = posf.reshape(B, S // WINDOW, WINDOW)
    sinks_l = jnp.pad(sinks.reshape(1, SWA_HEADS), ((0, 0), (0, LANE - SWA_HEADS)))

    (hb, zq, zkv, ql, kvl, q, k, v, qs, ks, vs, g, w_in_p) = _pre_fwd(
        x2, shift, scale, ng, w_in_t, gq, gkv, w_uq_p, w_uk_p, w_uv, rope, S, ts)
    r3 = lambda a: a.reshape(B, S, a.shape[-1])
    om, lse_m = _mla_fwd(r3(q), r3(k), r3(v), fq, fk)
    osw, lse_s = _swa_fwd(r3(qs), r3(ks), r3(vs), posb, posr, sinks_l)
    dx2, do, dg, loss_v, dfg, dgate, dw_out = _post(
        om.reshape(T, 512), osw.reshape(T, 512), g, w_out_p, x2, gate, fg.reshape(1, D), tgt.reshape(T, D), S, ts)
    do3 = r3(do)
    dq, dk, dv = _mla_bwd(r3(q), r3(k), r3(v), om, do3, lse_m, bq, bk)
    dqs, dks, dvs, dsink = _swa_bwd(r3(qs), r3(ks), r3(vs), posb, posr, sinks_l, osw, do3, lse_s)
    f2 = lambda a: a.reshape(T, a.shape[-1])
    gx, dz, dgq, dgkv, dng, dsh, dsc, dw_uq_p, dw_uk_p, dw_uv = _pre_bwd(
        f2(dq), f2(dk), f2(dv), f2(dqs), f2(dks), f2(dvs), dg, zq, zkv, ql, kvl, x2, dx2, scale, ng, w_in_p, gq, gkv,
        w_uq_p, w_uk_p, w_uv, rope, S, ts)
    tk = min(T, 1024)
    dw_in_t = _dw_in_t(dz, hb, 512, tk)
    dw_uq = dw_uq_p.reshape(Q_LORA, MLA_HEADS, LANE)[:, :, :96].reshape(Q_LORA, 768)
    dw_uk = dw_uk_p.reshape(KV_LORA, MLA_HEADS, LANE)[:, :, :64]
    dw_uv = dw_uv.reshape(KV_LORA, MLA_HEADS, 64)
    dw_ukv = jnp.concatenate([dw_uk, dw_uv], axis=2).reshape(KV_LORA, 1024)
    parts = jnp.concatenate([loss_v, dfg, dng, dgq, dgkv, dsink], axis=1)
    dmod = jnp.concatenate([dsh, dsc, dgate], axis=2).reshape(B, 3 * D)
    return gx.reshape(B, S, D), dw_in_t, dw_uq, dw_ukv, dw_out, parts, dmod


def kernel(x, c, positions, w_ada, b_ada, norm_gain, w_in, q_norm_gain, kv_norm_gain, w_uq, w_ukv, swa_sinks, w_out, final_gain, loss_target, m_w_ada, m_b_ada, m_norm_gain, m_w_in, m_q_norm_gain, m_kv_norm_gain, m_w_uq, m_w_ukv, m_swa_sinks, m_w_out, m_final_gain, v_w_ada, v_b_ada, v_norm_gain, v_w_in, v_q_norm_gain, v_kv_norm_gain, v_w_uq, v_w_ukv, v_swa_sinks, v_w_out, v_final_gain):
    B, S, D = x.shape
    me = 4 * lax.axis_index("x") + 2 * lax.axis_index("y") + lax.axis_index("c")
    bf = _MXU_DTYPE

    ncol = w_ada.shape[2]
    b_cols = lax.dynamic_slice_in_dim(b_ada, me * ncol, ncol, axis=1)
    c_all, win_g, wuq_g, wukv_g, wout_g, mod_g = _all_gather(
        [c, w_in[0].T.astype(bf), w_uq[0].astype(bf), w_ukv[0].astype(bf), w_out[0].astype(bf)], "ag_weights",
        fused=((w_ada[0], b_cols), _ada_cols, jax.ShapeDtypeStruct((N_DEV, B, ncol), F32)))
    c_all = c_all.reshape(N_DEV * B, D)
    cat_cols = lambda a: jnp.transpose(a, (1, 0, 2)).reshape(a.shape[1], N_DEV * a.shape[2])
    w_in_t, w_uq_f, w_ukv_f = win_g, cat_cols(wuq_g), cat_cols(wukv_g)
    w_out_f = wout_g.reshape(D, D)

    mod = lax.dynamic_index_in_dim(mod_g, me, axis=1, keepdims=False)
    mod = jnp.transpose(mod, (1, 0, 2)).reshape(B, 3 * D)

    gx, dw_in_t, dw_uq, dw_ukv, dw_out, parts, dmod = _local_step(
        x, mod, positions, norm_gain, w_in_t, q_norm_gain, kv_norm_gain, w_uq_f, w_ukv_f, swa_sinks,
        w_out_f, final_gain, loss_target)

    split_cols = lambda a: jnp.transpose(a.reshape(a.shape[0], 4, 2, a.shape[1] // N_DEV), (1, 2, 0, 3))
    split_rows = lambda a: a.reshape(4, 2, a.shape[0] // N_DEV, a.shape[1])
    slots = lambda a: a.reshape((4, 2) + a.shape[1:])
    g_w_in_t, g_w_uq, g_w_ukv, g_w_out, parts_g, dmod_g = _reduce_scatter(
        [(slots(dw_in_t[0]), slots(dw_in_t[1])), split_cols(dw_uq), split_cols(dw_ukv), split_rows(dw_out)], "rs_grads",
        gather=(parts, dmod))
    g_w_in = g_w_in_t[:W_IN_SHARD].T

    dmod_all = dmod_g.reshape(N_DEV * B, 3 * D)
    dmod_cols = lax.dynamic_slice_in_dim(dmod_all, me * ncol, ncol, axis=1)
    psum, loss, g_b_ada, g_w_ada = _finalize(parts_g, dmod_all, dmod_cols, c_all)
    loss = loss.reshape(())
    o = LANE
    g_final_gain = psum[0, o:o + D]
    g_norm_gain = psum[:, o + D:o + 2 * D]
    o += 2 * D
    g_q_norm_gain = psum[:, o:o + Q_LORA]
    g_kv_norm_gain = psum[:, o + Q_LORA:o + Q_LORA + KV_LORA]
    o += Q_LORA + KV_LORA
    g_sinks = psum[:, o:o + SWA_HEADS]

    grads = [g_w_ada, g_b_ada, g_norm_gain, g_w_in, g_q_norm_gain, g_kv_norm_gain, g_w_uq, g_w_ukv, g_sinks, g_w_out,
             g_final_gain]
    ws = [w_ada, b_ada, norm_gain, w_in, q_norm_gain, kv_norm_gain, w_uq, w_ukv, swa_sinks, w_out, final_gain]
    ms = [m_w_ada, m_b_ada, m_norm_gain, m_w_in, m_q_norm_gain, m_kv_norm_gain, m_w_uq, m_w_ukv, m_swa_sinks, m_w_out,
          m_final_gain]
    vs = [v_w_ada, v_b_ada, v_norm_gain, v_w_in, v_q_norm_gain, v_kv_norm_gain, v_w_uq, v_w_ukv, v_swa_sinks, v_w_out,
          v_final_gain]
    two_d = [(1, w.shape[0]) if w.ndim == 1 else w.shape[-2:] for w in ws]
    flat = lambda arrs: [a.reshape(s) for a, s in zip(arrs, two_d)]
    deltas, new_ms, new_vs = _adamw(flat(ws), flat(grads), flat(ms), flat(vs))
    shaped = lambda arrs: [a.reshape(w.shape) for a, w in zip(arrs, ws)]
    return (loss, gx, *shaped(grads), *shaped(deltas), *shaped(new_ms), *shaped(new_vs))
```

```python
import functools

import jax
import jax.numpy as jnp
from jax import lax
from jax.experimental import pallas as pl
from jax.experimental.pallas import tpu as pltpu

F32 = jnp.float32
_MXU_DTYPE = jnp.bfloat16
MLA_GRAD_DTYPE = jnp.bfloat16

N_DEV = 8
MLA_HEADS = 8
MLA_NOPE = 64
MLA_ROPE = 32
Q_LORA = 384
KV_LORA = 256
SWA_HEADS = 8
SWA_KV_HEADS = 2
SWA_HEAD_DIM = 64
WINDOW = 128
ROPE_THETA = 10000.0
EPS = 1e-6
MLA_SCALE = float((MLA_NOPE + MLA_ROPE) ** -0.5)
SWA_SCALE = float(SWA_HEAD_DIM ** -0.5)
LOG2E = 1.4426950408889634
MLA_QSCALE = MLA_SCALE * LOG2E
D_IN = 2464
D_IN_PAD = 2560
PAIR_ORDER = (0, 4, 1, 5, 2, 6, 3, 7)
PAIR_INV = (0, 2, 4, 6, 1, 3, 5, 7)

ADAM_LR = 0.001
ADAM_B1 = 0.9
ADAM_B2 = 0.999
ADAM_EPS = 1e-08
ADAM_WD = 0.01
ADAM_STEP = 10

LANE = 128
VMEM_LIMIT = 56 * 1024 * 1024

MESH = pl.DeviceIdType.MESH
NEG_INF = float("-inf")
SWA_SEQ_SPLIT = 2


def _mx(a):
    return a.astype(_MXU_DTYPE)


def _dot(a, b):
    return jnp.dot(a, b, preferred_element_type=F32)


def _dot_nt(a, b):
    return lax.dot_general(a, b, (((1,), (1,)), ((), ())), preferred_element_type=F32)


def _dot_tn(a, b):
    return lax.dot_general(a, b, (((0,), (0,)), ((), ())), preferred_element_type=F32)


def _cparams(sem=None):
    return pltpu.CompilerParams(dimension_semantics=sem, vmem_limit_bytes=VMEM_LIMIT)


def _vmem():
    return pl.BlockSpec(memory_space=pltpu.VMEM)


def _lane_iota(shape):
    return lax.broadcasted_iota(jnp.int32, shape, len(shape) - 1)


def _gather_program(srcs, outs, send_sems, recv_sems, local_sems):
    x, y, c = lax.axis_index("x"), lax.axis_index("y"), lax.axis_index("c")
    me, sibling = (x, y, c), (x, y, 1 - c)
    chips = [(1 - x, y), (x, 1 - y), (1 - x, 1 - y)]

    def slot(a, dev):
        return outs[a].at[4 * dev[0] + 2 * dev[1] + dev[2]]

    def copy(a, k, block, to, src=None):
        return pltpu.make_async_remote_copy(
            src_ref=slot(a, block) if src is None else src, dst_ref=slot(a, block),
            send_sem=send_sems.at[7 * a + k], recv_sem=recv_sems.at[7 * a + k],
            device_id=to, device_id_type=MESH)

    def local(a):
        return pltpu.make_async_copy(srcs[a], slot(a, me), local_sems.at[a])

    def start(a):
        local(a).start()
        cps = [copy(a, 0, me, sibling, src=srcs[a])]
        cps += [copy(a, 1 + j, me, (*chip, c), src=srcs[a]) for j, chip in enumerate(chips)]
        for cp in cps:
            cp.start()
        return cps

    def finish(group):
        cps = []
        for j, chip in enumerate(chips):
            for a in group:
                copy(a, 1 + j, (*chip, c), me).wait_recv()
                cp = copy(a, 4 + j, (*chip, c), sibling)
                cp.start()
                cps.append(cp)
        for a in group:
            copy(a, 0, sibling, me).wait_recv()
            for j, chip in enumerate(chips):
                copy(a, 4 + j, (*chip, 1 - c), me).wait_recv()
            local(a).wait()
        return cps

    return start, finish


def _gather_sems(m):
    return [pltpu.SemaphoreType.DMA((7 * m,)), pltpu.SemaphoreType.DMA((7 * m,)), pltpu.SemaphoreType.DMA((m,))]


def _all_gather(arrs, name, fused=None):
    n = len(arrs)
    extra, fn, piece = fused if fused else ((), None, None)
    ne, m = len(extra), n + (1 if fused else 0)

    def body(*refs):
        ins, ex, outs = refs[:n], refs[n:n + ne], refs[n + ne:n + ne + m]
        rest = refs[n + ne + m:]
        srcs = list(ins) + ([rest[0]] if fused else [])
        start, finish = _gather_program(srcs, outs, *rest[-3:])
        pending = []
        for a in range(n):
            pending += start(a)
        if fused:
            pending += finish([0])
            fn(srcs[n], outs[0], *ex)
            pending += start(n)
            pending += finish([n] + list(range(1, n)))
        else:
            pending += finish(list(range(n)))
        for cp in pending:
            cp.wait_send()

    out_shape = [jax.ShapeDtypeStruct((N_DEV,) + a.shape, a.dtype) for a in arrs]
    scratch = []
    if fused:
        out_shape.append(jax.ShapeDtypeStruct((N_DEV,) + piece.shape, piece.dtype))
        scratch.append(pltpu.VMEM(piece.shape, piece.dtype))
    return pl.pallas_call(
        body, name=name, out_shape=out_shape,
        in_specs=[_vmem()] * (n + ne), out_specs=[_vmem()] * m,
        scratch_shapes=scratch + _gather_sems(m),
        compiler_params=pltpu.CompilerParams(vmem_limit_bytes=VMEM_LIMIT),
    )(*arrs, *extra)


def _reduce_scatter(arrs, name, gather=()):
    n, g = len(arrs), len(gather)
    halves = [a[1] if isinstance(a, tuple) else a.astype(jnp.bfloat16) for a in arrs]
    arrs = [a[0] if isinstance(a, tuple) else a for a in arrs]

    def body(*refs):
        xs, xbs, gins = refs[:n], refs[n:2 * n], refs[2 * n:2 * n + g]
        outs, gouts = refs[2 * n + g:3 * n + g], refs[3 * n + g:3 * n + 2 * g]
        rest = refs[3 * n + 2 * g:]
        parts, recv_a, send_b, recv_b = (rest[t * n:(t + 1) * n] for t in range(4))
        send_sems, recv_sems, local_sems = rest[4 * n:4 * n + 3]
        x, y, c = lax.axis_index("x"), lax.axis_index("y"), lax.axis_index("c")
        myq = 2 * x + y
        gather_start, gather_finish = _gather_program(gins, gouts, *rest[4 * n + 3:])

        def chip(k):
            return (1 - x if k & 2 else x, 1 - y if k & 1 else y)

        def from_sibling(a):
            return pltpu.make_async_remote_copy(
                src_ref=xbs[a].at[:, 1 - c], dst_ref=recv_a[a], send_sem=send_sems.at[4 * a], recv_sem=recv_sems.at[4 * a],
                device_id=(x, y, 1 - c), device_id_type=MESH)

        def to_owner(a, k):
            qx, qy = chip(k)
            return pltpu.make_async_remote_copy(
                src_ref=send_b[a].at[2 * qx + qy], dst_ref=recv_b[a].at[myq],
                send_sem=send_sems.at[4 * a + k], recv_sem=recv_sems.at[4 * a + k],
                device_id=(qx, qy, c), device_id_type=MESH)

        mine = [pltpu.make_async_copy(xs[a].at[:, c], parts[a], local_sems.at[a]) for a in range(n)]
        first = [from_sibling(a) for a in range(n)]
        for cp in mine + first:
            cp.start()
        second = []
        for b in range(g):
            second += gather_start(b)
        for a in range(n):
            mine[a].wait()
            first[a].wait_recv()
            parts[a][...] = parts[a][...] + recv_a[a][...].astype(F32)
            send_b[a][...] = parts[a][...].astype(jnp.bfloat16)
            for k in range(1, 4):
                cp = to_owner(a, k)
                cp.start()
                second.append(cp)
        second += gather_finish(list(range(g)))
        for a in range(n):
            acc = parts[a][myq]
            for k in range(1, 4):
                to_owner(a, k).wait_recv()
                qx, qy = chip(k)
                acc = acc + recv_b[a][2 * qx + qy].astype(F32)
            outs[a][...] = acc
        for cp in first + second:
            cp.wait_send()

    quarter = lambda a, dt: pltpu.VMEM((4,) + a.shape[2:], dt)
    return pl.pallas_call(
        body, name=name,
        out_shape=[jax.ShapeDtypeStruct(a.shape[2:], F32) for a in arrs]
        + [jax.ShapeDtypeStruct((N_DEV,) + a.shape, a.dtype) for a in gather],
        in_specs=[pl.BlockSpec(memory_space=pl.ANY)] * (2 * n) + [_vmem()] * g, out_specs=[_vmem()] * (n + g),
        scratch_shapes=[quarter(a, F32) for a in arrs] + [quarter(a, jnp.bfloat16) for a in arrs] * 3
        + [pltpu.SemaphoreType.DMA((4 * n,)), pltpu.SemaphoreType.DMA((4 * n,)), pltpu.SemaphoreType.DMA((n,))]
        + _gather_sems(g),
        compiler_params=pltpu.CompilerParams(vmem_limit_bytes=VMEM_LIMIT),
    )(*arrs, *halves, *gather)


def _silu(t):
    return t * (1.0 / (1.0 + jnp.exp(-t)))


def _ada_cols(piece_ref, c_all_ref, w_ref, b_ref):
    w = _mx(w_ref[...])
    for d in range(N_DEV):
        piece_ref[d] = _dot(_mx(_silu(c_all_ref[d])), w) + b_ref[...]


def _rope_split(cd, sd):
    n = cd.shape[0]

    def expand(d):
        rep = jnp.broadcast_to(d[:, None, :], (n, 8, LANE)).reshape(8 * n, LANE)
        return pltpu.roll(rep, 0, 1, stride=16, stride_axis=0)

    c, s = expand(cd), expand(sd)
    lane = _lane_iota(c.shape)
    first = jnp.logical_and(lane >= 64, lane < 80)
    second = jnp.logical_and(lane >= 80, lane < 96)
    ck = jnp.where(first, pltpu.roll(c, 80, 1), jnp.where(second, pltpu.roll(c, 96, 1), 0.0))
    cq = jnp.where(lane < 64, 1.0, ck)
    sa = jnp.where(first, -pltpu.roll(s, 80, 1), 0.0)
    sb = jnp.where(second, pltpu.roll(s, 96, 1), 0.0)
    return cq, ck, sa, sb


def _tile_heads(t):
    return jnp.concatenate([t] * MLA_HEADS, axis=1)


def _pre_fwd(x2, shift, scale, ng, w_in, gq, gkv, w_uq, w_uk, w_uv, rope, S, ts):
    T, D = x2.shape
    nsb = S // ts
    WQ = MLA_HEADS * LANE

    def body(x_ref, sh_ref, sc_ref, ng_ref, wn_ref, gq_ref, gkv_ref, wuq_ref, wuk_ref, wuv_ref,
             cd_ref, sd_ref,
             hb_ref, zq_ref, zkv_ref, ql_ref, kvl_ref, q_ref, k_ref, v_ref, qs_ref, ks_ref, vs_ref, g_ref, wp_ref):
        @pl.when(pl.program_id(0) == 0)
        def _():
            wp_ref[640:704, :] = jnp.zeros((64, D), wp_ref.dtype)
            wp_ref[736:768, :] = jnp.zeros((32, D), wp_ref.dtype)
            for d in range(N_DEV):
                lo, hi = W_IN_SHARD * d, W_IN_SHARD * (d + 1)
                for nat, pad, size in _w_in_row_runs():
                    a, b = max(nat, lo), min(nat + size, hi)
                    if a < b:
                        wp_ref[pad + a - nat:pad + b - nat, :] = wn_ref[d, a - lo:b - lo, :]

        w_ref = wp_ref
        x = x_ref[...]
        r1 = lax.rsqrt(jnp.mean(x * x, axis=-1, keepdims=True) + EPS)
        h = ((x * r1) * ng_ref[...]) * (1.0 + sc_ref[0]) + sh_ref[0]
        hb = _mx(h)
        hb_ref[...] = hb
        zq = _dot_nt(hb, w_ref[0:384, :])
        zq_ref[...] = zq
        rq = lax.rsqrt(jnp.mean(zq * zq, axis=-1, keepdims=True) + EPS)
        ql = _mx((zq * rq) * gq_ref[...])
        ql_ref[...] = ql
        q = _dot(ql, wuq_ref[...])
        cq, ck, sa, sb = _rope_split(cd_ref[...], sd_ref[...])
        q = (q * _tile_heads(cq) + pltpu.roll(q, WQ - 16, 1) * _tile_heads(sa)
             + pltpu.roll(q, 16, 1) * _tile_heads(sb))
        q_ref[...] = _mx(q * MLA_QSCALE)
        zkv = _dot_nt(hb, w_ref[384:640, :])
        zkv_ref[...] = zkv
        rkv = lax.rsqrt(jnp.mean(zkv * zkv, axis=-1, keepdims=True) + EPS)
        kvl = _mx((zkv * rkv) * gkv_ref[...])
        kvl_ref[...] = kvl
        kr = _dot_nt(hb, w_ref[640:768, :])
        kpe = kr * ck + pltpu.roll(kr, LANE - 16, 1) * sa + pltpu.roll(kr, 16, 1) * sb
        kf = _dot(kvl, wuk_ref[...])
        k_ref[...] = _mx(kf + jnp.concatenate([kpe] * MLA_HEADS, axis=1))
        v_ref[...] = _mx(_dot(kvl, wuv_ref[...]))
        g_ref[:, 0:512] = _dot_nt(hb, w_ref[768:1280, :])
        qs_ref[...] = _mx(_dot_nt(hb, w_ref[1280:1792, :]) * (SWA_SCALE * LOG2E))
        ks_ref[...] = _mx(_dot_nt(hb, w_ref[1792:1920, :]))
        vs_ref[...] = _mx(_dot_nt(hb, w_ref[1920:2048, :]))
        g_ref[:, 512:1024] = _dot_nt(hb, w_ref[2048:2560, :])

    row = lambda w: pl.BlockSpec((ts, w), lambda i: (i, 0))
    dense = pl.BlockSpec((ts // 8, LANE), lambda i: (i, 0))
    full = lambda a: pl.BlockSpec(a.shape, lambda i: (0,) * a.ndim)
    per_b = pl.BlockSpec((1, 1, D), lambda i: (i // nsb, 0, 0))
    out_w = [(D, _MXU_DTYPE), (384, F32), (256, F32), (384, _MXU_DTYPE), (256, _MXU_DTYPE), (WQ, _MXU_DTYPE),
             (WQ, _MXU_DTYPE), (512, _MXU_DTYPE), (512, _MXU_DTYPE), (128, _MXU_DTYPE), (128, _MXU_DTYPE), (1024, F32)]
    return pl.pallas_call(
        body, name="pre_fwd", grid=(T // ts,),
        out_shape=[jax.ShapeDtypeStruct((T, w), dt) for w, dt in out_w] + [jax.ShapeDtypeStruct((D_IN_PAD, D), w_in.dtype)],
        in_specs=[row(D), per_b, per_b, full(ng), full(w_in), full(gq), full(gkv), full(w_uq), full(w_uk), full(w_uv),
                  dense, dense],
        out_specs=[row(w) for w, _ in out_w] + [pl.BlockSpec((D_IN_PAD, D), lambda i: (0, 0))],
        compiler_params=_cparams(("arbitrary",)),
    )(x2, shift, scale, ng, w_in, gq, gkv, w_uq, w_uk, w_uv, *rope)


def _mla_fwd(q3, k3, v3, tq, tk):
    B, S, _ = q3.shape
    nq = S // tq
    assert tq == tk
    HPS = 2
    NH = 2 * HPS

    def body(q_ref, k_ref, v_ref, o_ref, lse_ref):
        rows = lax.broadcasted_iota(jnp.int32, (tq, tk), 0)
        cols = lax.broadcasted_iota(jnp.int32, (tq, tk), 1)
        low_k = _lane_iota((tk, LANE)) < 64
        low = _lane_iota((tq, LANE)) < 64

        def step(qs, kt, carry, masked):
            r0 = kt * tk
            out = []
            for j in range(NH):
                v2 = v_ref[0, r0:r0 + tk, LANE * (j // 2):LANE * (j // 2 + 1)]
                vj = (jnp.where(low_k, v2, 1.0) if j % 2 == 0 else jnp.where(low_k, 1.0, v2)).astype(v2.dtype)
                m, acc = carry[j]
                s = _dot_nt(qs[j], k_ref[0, r0:r0 + tk, LANE * j:LANE * (j + 1)])
                if masked:
                    s = jnp.where(rows >= cols, s, NEG_INF)
                m_new = jnp.maximum(m, jnp.max(s, axis=1, keepdims=True))
                alpha = jnp.exp2(m - m_new)
                p = jnp.exp2(s - m_new)
                acc = alpha * acc + _dot(_mx(p), vj)
                out.append((m_new, acc))
            return tuple(out)

        for qi in range(nq):
            r = slice(qi * tq, (qi + 1) * tq)
            qs = [q_ref[0, r, LANE * j:LANE * (j + 1)] for j in range(NH)]
            init = (jnp.full((tq, 1), NEG_INF, F32), jnp.zeros((tq, LANE), F32))
            carry = (init,) * NH
            for kt in range(qi):
                carry = step(qs, kt, carry, False)
            carry = step(qs, qi, carry, True)
            for t in range(HPS):
                (m0, a0), (m1, a1) = carry[2 * t], carry[2 * t + 1]
                l0 = jnp.where(low, pltpu.roll(a0, 64, 1), a0)
                l1 = jnp.where(low, a1, pltpu.roll(a1, 64, 1))
                o_ref[0, r, LANE * t:LANE * (t + 1)] = jnp.where(low, a0 / l0, a1 / l1)
                lse_ref[0, 2 * t, r, :] = m0 + jnp.log2(l0)
                lse_ref[0, 2 * t + 1, r, :] = m1 + jnp.log2(l1)

    grp = lambda w: pl.BlockSpec((1, S, w), lambda b, hp: (b, 0, hp))
    return pl.pallas_call(
        body, name="mla_fwd", grid=(B, MLA_HEADS // NH),
        out_shape=[jax.ShapeDtypeStruct((B, S, 512), F32), jax.ShapeDtypeStruct((B, MLA_HEADS, S, LANE), F32)],
        in_specs=[grp(NH * LANE), grp(NH * LANE), grp(HPS * LANE)],
        out_specs=[grp(HPS * LANE), pl.BlockSpec((1, NH, S, LANE), lambda b, hp: (b, hp, 0, 0))],
        compiler_params=_cparams(("arbitrary", "arbitrary")),
    )(q3, k3, v3)


def _mla_bwd(q3, k3, v3, o3, do3, lse, tq, tk, hps=2):
    B, S, _ = q3.shape
    nq, nk = S // tq, S // tk
    nh = 2 * hps
    assert tk % tq == 0

    def body(q_ref, k_ref, v_ref, o_ref, do_ref, lse_ref, dq_ref, dk_ref, dv_ref, dkt_ref, dvt_ref):
        rows = lax.broadcasted_iota(jnp.int32, (tq, tk), 0)
        cols = lax.broadcasted_iota(jnp.int32, (tq, tk), 1)
        lane = _lane_iota((tq, LANE))

        def q_tile(qi, _):
            r = pl.ds(pl.multiple_of(qi * tq, tq), tq)
            heads = []
            for j in range(nh):
                lanes = slice(LANE * j, LANE * (j + 1))
                pair = slice(LANE * (j // 2), LANE * (j // 2 + 1))
                do = jnp.where((lane < 64) if j % 2 == 0 else (lane >= 64), do_ref[0, r, pair], 0.0)
                q = q_ref[0, r, lanes]
                heads.append((lanes, pair, q, _mx(q.astype(F32).T), _mx(do), _mx(do.T),
                              jnp.sum(do * o_ref[0, r, pair], axis=1, keepdims=True),
                              jnp.concatenate([lse_ref[0, j, r, :]] * (tk // LANE), axis=1)))
            n_full = (qi * tq) // tk

            def k_tile(kt, dqs, masked):
                kr = pl.ds(pl.multiple_of(kt * tk, tk), tk)
                first = masked and (qi * tq) % tk == 0
                out = []
                dvt = [None] * hps
                for j, (lanes, pair, q, qt, dob, dot_, dcol, lse_c) in enumerate(heads):
                    k = k_ref[0, kr, lanes]
                    s = _dot_nt(q, k)
                    if masked:
                        s = jnp.where(rows + qi * tq >= cols + kt * tk, s, NEG_INF)
                    p = jnp.exp2(s - lse_c)
                    dp = _dot_nt(dob, v_ref[0, kr, pair])
                    dsb = _mx(p * (dp - dcol))
                    if first:
                        dkt_ref[j, kt] = _dot(qt, dsb)
                    else:
                        dkt_ref[j, kt] += _dot(qt, dsb)
                    pv = _dot(dot_, _mx(p))
                    dvt[j // 2] = pv if dvt[j // 2] is None else dvt[j // 2] + pv
                    out.append(dqs[j] + _dot(dsb, k))
                for t in range(hps):
                    if first:
                        dvt_ref[t, kt] = dvt[t]
                    else:
                        dvt_ref[t, kt] += dvt[t]
                return tuple(out)

            dqs = (jnp.zeros((tq, LANE), F32),) * nh
            for kt in range(n_full):
                dqs = k_tile(kt, dqs, False)
            dqs = k_tile(n_full, dqs, True)
            for j in range(nh):
                dq_ref[0, r, heads[j][0]] = (MLA_SCALE * dqs[j]).astype(dq_ref.dtype)
            return 0

        for qi in range(nq):
            q_tile(qi, 0)

        def flush(kt, _):
            kr = pl.ds(pl.multiple_of(kt * tk, tk), tk)
            for j in range(nh):
                dk_ref[0, kr, LANE * j:LANE * (j + 1)] = ((1.0 / LOG2E) * dkt_ref[j, kt].T).astype(dk_ref.dtype)
            for t in range(hps):
                dv_ref[0, kr, LANE * t:LANE * (t + 1)] = dvt_ref[t, kt].T.astype(dv_ref.dtype)
            return 0

        lax.fori_loop(0, nk, flush, 0)

    grp = lambda w: pl.BlockSpec((1, S, w), lambda b, hp: (b, 0, hp))
    return pl.pallas_call(
        body, name="mla_bwd", grid=(B, MLA_HEADS // nh),
        out_shape=[jax.ShapeDtypeStruct((B, S, 1024), MLA_GRAD_DTYPE), jax.ShapeDtypeStruct((B, S, 1024), MLA_GRAD_DTYPE),
                   jax.ShapeDtypeStruct((B, S, 512), MLA_GRAD_DTYPE)],
        in_specs=[grp(nh * LANE), grp(nh * LANE), grp(hps * LANE), grp(hps * LANE), grp(hps * LANE),
                  pl.BlockSpec((1, nh, S, LANE), lambda b, hp: (b, hp, 0, 0))],
        out_specs=[grp(nh * LANE), grp(nh * LANE), grp(hps * LANE)],
        scratch_shapes=[pltpu.VMEM((nh, nk, LANE, tk), F32), pltpu.VMEM((hps, nk, LANE, tk), F32)],
        compiler_params=_cparams(("arbitrary", "arbitrary")),
    )(q3, k3, v3, o3, do3, lse)


def _swa_consts(sink_ref):
    W = WINDOW
    row = lax.broadcasted_iota(jnp.int32, (4 * W, LANE), 0)
    out = []
    for g in range(SWA_KV_HEADS):
        slope = jnp.zeros((4 * W, LANE), F32)
        sink = jnp.zeros((4 * W, LANE), F32)
        for p in range(4):
            h = p + 4 * g
            here = jnp.logical_and(row >= W * p, row < W * (p + 1))
            slope = jnp.where(here, float(LOG2E * 2.0 ** (-8.0 * (h + 1) / SWA_HEADS)), slope)
            sink = jnp.where(here, LOG2E * sink_ref[:, h:h + 1], sink)
        out.append((jnp.concatenate([slope, slope], axis=1), sink))
    return out


def _wide(col):
    return jnp.concatenate([col, col], axis=1)


def _swa_block(q_ref, k_ref, v_ref, pb_ref, pr_ref, n, i, ij):
    W = WINDOW
    kb = jnp.maximum(n - 1, 0)
    r = pl.ds(pl.multiple_of(i * W, W), W)
    kr = pl.ds(pl.multiple_of(kb * W, W), 2 * W)
    q4, k2, v2 = q_ref[0, r, :], k_ref[0, kr, :], v_ref[0, kr, :]
    pq = _wide(pb_ref[0, r, :])
    pk = jnp.concatenate([pr_ref[0, pl.ds(kb, 1), :], pr_ref[0, pl.ds(kb + 1, 1), :]], axis=1)
    rel = ij + (n - kb) * W
    dist = jnp.where(jnp.logical_and(rel >= 0, rel < W), pq - pk, float("inf"))
    return r, kb, kr, q4, k2, v2, jnp.concatenate([dist] * 4, axis=0)


def _swa_stack(x4, g, dtype):
    lane = _lane_iota((WINDOW, LANE))
    mine = (lane < 64) if g == 0 else (lane >= 64)
    return jnp.concatenate([jnp.where(mine, x4[:, LANE * p:LANE * (p + 1)], 0).astype(dtype) for p in range(4)], axis=0)


def _swa_unstack(ref, r, lo, hi, scale=None):
    W = WINDOW
    low = _lane_iota((W, LANE)) < 64
    for p in range(4):
        t = jnp.where(low, lo[W * p:W * (p + 1)], hi[W * p:W * (p + 1)])
        ref[0, r, LANE * p:LANE * (p + 1)] = t if scale is None else scale * t


def _swa_fwd(qs3, ks3, vs3, posb, posr, sinks):
    B, S, _ = qs3.shape
    W = WINDOW
    nb = S // W
    nh = SWA_SEQ_SPLIT
    nbh = nb // nh

    def body(q_ref, k_ref, v_ref, pb_ref, pr_ref, sink_ref, o_ref, lse_ref):
        ij = lax.broadcasted_iota(jnp.int32, (W, 2 * W), 0) - lax.broadcasted_iota(jnp.int32, (W, 2 * W), 1)
        consts = _swa_consts(sink_ref)
        n0 = pl.program_id(1) * nbh

        def blk(i, _):
            n = n0 + i
            r, _, _, q4, k2, v2, dist4 = _swa_block(q_ref, k_ref, v_ref, pb_ref, pr_ref, n, i, ij)
            o_g = []
            for g, (slope, sink) in enumerate(consts):
                s = _dot_nt(_swa_stack(q4, g, q4.dtype), k2) - slope * dist4
                m = jnp.maximum(jnp.max(s, axis=1, keepdims=True), sink)
                e = jnp.exp2(s - _wide(m))
                l = jnp.sum(e, axis=1, keepdims=True) + jnp.exp2(sink - m)
                o_g.append(_dot(_mx(e), v2) * (1.0 / l))
                lse_ref[0, i, g] = m + jnp.log2(l)
            _swa_unstack(o_ref, r, o_g[0], o_g[1])
            return 0

        for i in range(nbh):
            blk(i, 0)

    seq = lambda w: pl.BlockSpec((1, S, w), lambda b, h: (b, 0, 0))
    part = lambda w: pl.BlockSpec((1, S // nh, w), lambda b, h: (b, h, 0))
    lse_spec = pl.BlockSpec((1, nbh, 2, 4 * W, LANE), lambda b, h: (b, h, 0, 0, 0))
    return pl.pallas_call(
        body, name="swa_fwd", grid=(B, nh),
        out_shape=[jax.ShapeDtypeStruct((B, S, 512), F32), jax.ShapeDtypeStruct((B, nb, 2, 4 * W, LANE), F32)],
        in_specs=[part(512), seq(LANE), seq(LANE), part(LANE), pl.BlockSpec((1, nb, W), lambda b, h: (b, 0, 0)),
                  pl.BlockSpec((1, LANE), lambda b, h: (0, 0))],
        out_specs=[part(512), lse_spec],
        compiler_params=_cparams(("arbitrary", "arbitrary")),
    )(qs3, ks3, vs3, posb, posr, sinks)


def _swa_bwd(qs3, ks3, vs3, posb, posr, sinks, os3, do3, lse):
    B, S, _ = qs3.shape
    W = WINDOW
    nb = S // W
    nh = SWA_SEQ_SPLIT
    nbh = nb // nh
    assert nbh % 2 == 0 and nbh * nh * W == S

    def body(q_ref, k_ref, v_ref, pb_ref, pr_ref, sink_ref, o_ref, do_ref, lse_ref, dq_ref, dk_ref, dv_ref, dsink_ref,
             dkt_ref, dvt_ref):
        lane1 = _lane_iota((1, LANE))
        ij = lax.broadcasted_iota(jnp.int32, (W, 2 * W), 0) - lax.broadcasted_iota(jnp.int32, (W, 2 * W), 1)
        consts = _swa_consts(sink_ref)
        hh = pl.program_id(1)
        n0 = hh * nbh

        @pl.when(hh == 0)
        def _():
            dkt_ref[...] = jnp.zeros_like(dkt_ref)
            dvt_ref[...] = jnp.zeros_like(dvt_ref)

        @pl.when(jnp.logical_and(pl.program_id(0) == 0, hh == 0))
        def _():
            dsink_ref[...] = jnp.zeros_like(dsink_ref)

        def blk(i, dsink):
            n = n0 + i
            r, kb, _, q4, k2, v2, dist4 = _swa_block(q_ref, k_ref, v_ref, pb_ref, pr_ref, n, i, ij)
            o4, do4 = o_ref[0, r, :], do_ref[0, r, :]
            dq_g = []
            dkt = jnp.zeros((LANE, 2 * W), F32)
            dvt = jnp.zeros((LANE, 2 * W), F32)
            for g, (slope, sink) in enumerate(consts):
                q_st = _swa_stack(q4, g, F32)
                do_st = _swa_stack(do4, g, F32)
                dcol = jnp.sum(do_st * _swa_stack(o4, g, F32), axis=1, keepdims=True)
                qst, dob = _mx(q_st), _mx(do_st)
                lse_c = lse_ref[0, i, g]
                pr = jnp.exp2(_dot_nt(qst, k2) - slope * dist4 - _wide(lse_c))
                dsb = _mx(pr * (_dot_nt(dob, v2) - dcol))
                psd = jnp.exp2(sink - lse_c)[:, 0:1] * dcol
                for p in range(4):
                    dsink = dsink - jnp.where(lane1 == p + 4 * g,
                                              jnp.sum(psd[W * p:W * (p + 1)], axis=0, keepdims=True), 0.0)
                dq_g.append(_dot(dsb, k2))
                dkt = dkt + _dot(_mx(q_st.T), dsb)
                dvt = dvt + _dot(_mx(do_st.T), _mx(pr))
            _swa_unstack(dq_ref, r, dq_g[0], dq_g[1], SWA_SCALE)
            dkt_ref[kb] += dkt[:, 0:W]
            dkt_ref[kb + 1] += dkt[:, W:2 * W]
            dvt_ref[kb] += dvt[:, 0:W]
            dvt_ref[kb + 1] += dvt[:, W:2 * W]
            return dsink

        dsink = jnp.zeros((1, LANE), F32)
        for i in range(nbh):
            dsink = blk(i, dsink)
        dsink_ref[...] += dsink

        @pl.when(hh == nh - 1)
        def _():
            def flush(n, _):
                r = pl.ds(pl.multiple_of(n * W, W), W)
                dk_ref[0, r, :] = (1.0 / LOG2E) * dkt_ref[n].T
                dv_ref[0, r, :] = dvt_ref[n].T
                return 0

            lax.fori_loop(0, nb, flush, 0)

    seq = lambda w: pl.BlockSpec((1, S, w), lambda b, h: (b, 0, 0))
    part = lambda w: pl.BlockSpec((1, S // nh, w), lambda b, h: (b, h, 0))
    return pl.pallas_call(
        body, name="swa_bwd", grid=(B, nh),
        out_shape=[jax.ShapeDtypeStruct((B, S, 512), F32), jax.ShapeDtypeStruct((B, S, LANE), F32),
                   jax.ShapeDtypeStruct((B, S, LANE), F32), jax.ShapeDtypeStruct((1, LANE), F32)],
        in_specs=[part(512), seq(LANE), seq(LANE), part(LANE), pl.BlockSpec((1, nb, W), lambda b, h: (b, 0, 0)),
                  pl.BlockSpec((1, LANE), lambda b, h: (0, 0)), part(512),
                  pl.BlockSpec((1, S // nh, 512), lambda b, h: (b, h, 1)),
                  pl.BlockSpec((1, nbh, 2, 4 * W, LANE), lambda b, h: (b, h, 0, 0, 0))],
        out_specs=[part(512), seq(LANE), seq(LANE), pl.BlockSpec((1, LANE), lambda b, h: (0, 0))],
        scratch_shapes=[pltpu.VMEM((nb, LANE, W), F32), pltpu.VMEM((nb, LANE, W), F32)],
        compiler_params=_cparams(("arbitrary", "arbitrary")),
    )(qs3, ks3, vs3, posb, posr, sinks, os3, do3, lse)


def _post(om, osw, g, w_out, x2, gate, fg, tgt, S, ts):
    T, D = x2.shape
    nsb = S // ts

    def body(om_ref, os_ref, g_ref, w_ref, x_ref, gate_ref, fg_ref, t_ref,
             dx2_ref, do_ref, dg_ref, loss_ref, dfg_ref, dgate_ref, dw_ref):
        i = pl.program_id(0)
        gv = g_ref[...]
        sg = 1.0 / (1.0 + jnp.exp(-gv))
        silu = gv * sg
        o = jnp.concatenate([om_ref[...], os_ref[...]], axis=1)
        ab = _mx(o * silu)
        y = _dot(ab, w_ref[...])
        gate = gate_ref[0]
        xo = x_ref[...] + gate * y
        r2 = lax.rsqrt(jnp.mean(xo * xo, axis=-1, keepdims=True) + EPS)
        xh = xo * r2
        fg = fg_ref[...]
        diff = xh * fg - t_ref[...]
        sq = jnp.sum(diff * diff, axis=0, keepdims=True)
        part = sq[:, 0:LANE]
        for t in range(1, D // LANE):
            part = part + sq[:, LANE * t:LANE * (t + 1)]
        dout = diff * (1.0 / D)
        dxh = dout * fg
        dx2 = r2 * (dxh - xh * jnp.mean(dxh * xh, axis=-1, keepdims=True))
        dx2_ref[...] = dx2
        dyb = _mx(dx2 * gate)
        da = _dot_nt(dyb, w_ref[...])
        do_ref[...] = da * silu
        dg_ref[...] = _mx(da * o * (sg * (1.0 + gv * (1.0 - sg))))

        @pl.when(i == 0)
        def _():
            loss_ref[...] = jnp.zeros_like(loss_ref)
            dfg_ref[...] = jnp.zeros_like(dfg_ref)
            dw_ref[...] = jnp.zeros_like(dw_ref)

        @pl.when(i % nsb == 0)
        def _():
            dgate_ref[...] = jnp.zeros_like(dgate_ref)

        loss_ref[...] += (0.5 / D) * part
        dfg_ref[...] += jnp.sum(dout * xh, axis=0, keepdims=True)
        dgate_ref[0] += jnp.sum(dx2 * y, axis=0, keepdims=True)
        dw_ref[...] += _dot_tn(ab, dyb)

        @pl.when(i == T // ts - 1)
        def _():
            chunks = [dw_ref[512 + 64 * PAIR_INV[h]:512 + 64 * (PAIR_INV[h] + 1), :] for h in range(SWA_HEADS)]
            for h in range(SWA_HEADS):
                dw_ref[512 + 64 * h:512 + 64 * (h + 1), :] = chunks[h]

    row = lambda w: pl.BlockSpec((ts, w), lambda i: (i, 0))
    full = lambda a: pl.BlockSpec(a.shape, lambda i: (0,) * a.ndim)
    per_b = pl.BlockSpec((1, 1, D), lambda i: (i // nsb, 0, 0))
    return pl.pallas_call(
        body, name="post", grid=(T // ts,),
        out_shape=[jax.ShapeDtypeStruct((T, D), F32), jax.ShapeDtypeStruct((T, 1024), F32),
                   jax.ShapeDtypeStruct((T, 1024), _MXU_DTYPE), jax.ShapeDtypeStruct((1, LANE), F32),
                   jax.ShapeDtypeStruct((1, D), F32), jax.ShapeDtypeStruct(gate.shape, F32),
                   jax.ShapeDtypeStruct(w_out.shape, F32)],
        in_specs=[row(512), row(512), row(1024), full(w_out), row(D), per_b, full(fg), row(D)],
        out_specs=[row(D), row(1024), row(1024),
                   pl.BlockSpec((1, LANE), lambda i: (0, 0)), pl.BlockSpec((1, D), lambda i: (0, 0)), per_b,
                   full(w_out)],
        compiler_params=_cparams(("arbitrary",)),
    )(om, osw, g, w_out, x2, gate, fg, tgt)


def _pre_bwd(dq, dk, dv, dqs, dks, dvs, dg, zq, zkv, ql, kvl, x2, dx2, scale, ng, w_in, gq, gkv, w_uq, w_uk, w_uv,
             rope, S, ts):
    T, D = x2.shape
    nsb = S // ts
    WQ = MLA_HEADS * LANE

    def body(dq_ref, dk_ref, dv_ref, dqs_ref, dks_ref, dvs_ref, dg_ref, zq_ref, zkv_ref, ql_ref, kvl_ref, x_ref, dx2_ref,
             sc_ref,
             ng_ref, w_ref, gq_ref, gkv_ref, wuq_ref, wuk_ref, wuv_ref, cd_ref, sd_ref,
             gx_ref, dz_ref, dgq_ref, dgkv_ref, dng_ref, dsh_ref, dsc_ref, dwuq_ref, dwuk_ref, dwuv_ref):
        i = pl.program_id(0)

        @pl.when(i == 0)
        def _():
            dgq_ref[...] = jnp.zeros_like(dgq_ref)
            dgkv_ref[...] = jnp.zeros_like(dgkv_ref)
            dng_ref[...] = jnp.zeros_like(dng_ref)
            dwuq_ref[...] = jnp.zeros_like(dwuq_ref)
            dwuk_ref[...] = jnp.zeros_like(dwuk_ref)
            dwuv_ref[...] = jnp.zeros_like(dwuv_ref)

        @pl.when(i % nsb == 0)
        def _():
            dsh_ref[...] = jnp.zeros_like(dsh_ref)
            dsc_ref[...] = jnp.zeros_like(dsc_ref)

        def norm_bwd(z, dy, gain):
            r = lax.rsqrt(jnp.mean(z * z, axis=-1, keepdims=True) + EPS)
            zh = z * r
            dzh = dy * gain
            return r * (dzh - zh * jnp.mean(dzh * zh, axis=-1, keepdims=True)), jnp.sum(dy * zh, axis=0, keepdims=True)

        tables = _rope_split(cd_ref[...], sd_ref[...])
        ng = ng_ref[...]
        sc1 = 1.0 + sc_ref[0]

        def rows_chain(rs):
            cq, ck, sa, sb = (t[rs] for t in tables)
            dqr = dq_ref[rs, :].astype(F32)
            dqb = _mx(dqr * _tile_heads(cq) + pltpu.roll(dqr * _tile_heads(sa), 16, 1)
                      + pltpu.roll(dqr * _tile_heads(sb), WQ - 16, 1))
            p_uq = _dot_tn(ql_ref[rs, :], dqb)
            dzq, dgq = norm_bwd(zq_ref[rs, :], _dot_nt(dqb, wuq_ref[...]), gq_ref[...])
            dkr = dk_ref[rs, :].astype(F32)
            dkb = _mx(dkr)
            p_uk = _dot_tn(kvl_ref[rs, :], dkb)
            dvb = _mx(dv_ref[rs, :])
            p_uv = _dot_tn(kvl_ref[rs, :], dvb)
            dzkv, dgkv = norm_bwd(zkv_ref[rs, :], _dot_nt(dkb, wuk_ref[...]) + _dot_nt(dvb, wuv_ref[...]), gkv_ref[...])
            dkpe = dkr[:, 0:LANE]
            for h in range(1, MLA_HEADS):
                dkpe = dkpe + dkr[:, LANE * h:LANE * (h + 1)]
            dkro = dkpe * ck + pltpu.roll(dkpe * sa, 16, 1) + pltpu.roll(dkpe * sb, LANE - 16, 1)
            dz_ref[rs, 0:384] = _mx(dzq)
            dz_ref[rs, 384:640] = _mx(dzkv)
            dz_ref[rs, 640:768] = _mx(dkro)
            dz_ref[rs, 768:1280] = dg_ref[rs, 0:512]
            dz_ref[rs, 1280:1792] = _mx(dqs_ref[rs, :])
            dz_ref[rs, 1792:1920] = _mx(dks_ref[rs, :])
            dz_ref[rs, 1920:2048] = _mx(dvs_ref[rs, :])
            dz_ref[rs, 2048:2560] = dg_ref[rs, 512:1024]
            dh = _dot(dz_ref[rs, :], w_ref[...])
            x = x_ref[rs, :]
            r1 = lax.rsqrt(jnp.mean(x * x, axis=-1, keepdims=True) + EPS)
            xn = x * r1
            dxn = dh * ng * sc1
            gx_ref[rs, :] = dx2_ref[rs, :] + r1 * (dxn - xn * jnp.mean(dxn * xn, axis=-1, keepdims=True))
            return (p_uq, p_uk, p_uv, dgq, dgkv, jnp.sum(dh, axis=0, keepdims=True),
                    jnp.sum(dh * (xn * ng), axis=0, keepdims=True), jnp.sum(dh * xn * sc1, axis=0, keepdims=True))

        hr = ts // 2
        parts = [rows_chain(slice(hr * t, hr * (t + 1))) for t in range(2)]
        p_uq, p_uk, p_uv, dgq, dgkv, dsh, dsc, dng = (a + b for a, b in zip(*parts))
        dwuq_ref[...] += p_uq
        dwuk_ref[...] += p_uk
        dwuv_ref[...] += p_uv
        dgq_ref[...] += dgq
        dgkv_ref[...] += dgkv
        dsh_ref[0] += dsh
        dsc_ref[0] += dsc
        dng_ref[...] += dng

    row = lambda w: pl.BlockSpec((ts, w), lambda i: (i, 0))
    full = lambda a: pl.BlockSpec(a.shape, lambda i: (0,) * a.ndim, pipeline_mode=pl.Buffered(1))
    per_b = pl.BlockSpec((1, 1, D), lambda i: (i // nsb, 0, 0))
    dense = pl.BlockSpec((ts // 8, LANE), lambda i: (i, 0))
    vec = lambda w: pl.BlockSpec((1, w), lambda i: (0, 0))
    return pl.pallas_call(
        body, name="pre_bwd", grid=(T // ts,),
        out_shape=[jax.ShapeDtypeStruct((T, D), F32), jax.ShapeDtypeStruct((T, D_IN_PAD), _MXU_DTYPE), jax.ShapeDtypeStruct((1, 384), F32),
                   jax.ShapeDtypeStruct((1, 256), F32), jax.ShapeDtypeStruct((1, D), F32),
                   jax.ShapeDtypeStruct(scale.shape, F32), jax.ShapeDtypeStruct(scale.shape, F32),
                   jax.ShapeDtypeStruct(w_uq.shape, F32), jax.ShapeDtypeStruct(w_uk.shape, F32),
                   jax.ShapeDtypeStruct(w_uv.shape, F32)],
        in_specs=[row(WQ), row(WQ), row(512), row(512), row(LANE), row(LANE), row(1024), row(384), row(256), row(384),
                  row(256), row(D), row(D), per_b, full(ng), full(w_in), full(gq), full(gkv), full(w_uq), full(w_uk), full(w_uv),
                  dense, dense],
        out_specs=[row(D), row(D_IN_PAD), vec(384), vec(256), vec(D), per_b, per_b, full(w_uq), full(w_uk), full(w_uv)],
        compiler_params=_cparams(("arbitrary",)),
    )(dq, dk, dv, dqs, dks, dvs, dg, zq, zkv, ql, kvl, x2, dx2, scale, ng, w_in, gq, gkv, w_uq, w_uk, w_uv, *rope)


def _w_in_row_runs():
    runs = [(0, 0, 640), (640, 704, 32), (672, 768, 512)]
    runs += [(1184 + 64 * h, 1280 + 64 * PAIR_INV[h], 64) for h in range(8)]
    runs += [(1696, 1792, 256)]
    runs += [(1952 + 64 * h, 2048 + 64 * PAIR_INV[h], 64) for h in range(8)]
    return runs


W_IN_SHARD = D_IN // N_DEV
W_IN_SLOT = 320


def _dw_in_t(dz, hb, tn, tk):
    T, M = dz.shape
    N = hb.shape[1]
    nk = T // tk

    def body(a_ref, b_ref, o_ref, ob_ref, acc_ref):
        k = pl.program_id(1)

        @pl.when(k == 0)
        def _():
            acc_ref[...] = jnp.zeros_like(acc_ref)

        acc_ref[...] += _dot_tn(a_ref[...], b_ref[...])

        @pl.when(k == nk - 1)
        def _():
            for d in range(N_DEV):
                lo, hi = W_IN_SHARD * d, W_IN_SHARD * (d + 1)
                for nat, pad, size in _w_in_row_runs():
                    a, b = max(nat, lo), min(nat + size, hi)
                    if a < b:
                        piece = acc_ref[pad + a - nat:pad + b - nat, :]
                        o_ref[d, a - lo:b - lo, :] = piece
                        ob_ref[d, a - lo:b - lo, :] = piece.astype(ob_ref.dtype)
                o_ref[d, W_IN_SHARD:W_IN_SLOT, :] = jnp.zeros((W_IN_SLOT - W_IN_SHARD, tn), F32)
                ob_ref[d, W_IN_SHARD:W_IN_SLOT, :] = jnp.zeros((W_IN_SLOT - W_IN_SHARD, tn), ob_ref.dtype)

    slots = pl.BlockSpec((N_DEV, W_IN_SLOT, tn), lambda j, k: (0, 0, j))
    return pl.pallas_call(
        body, name="dw_in", grid=(N // tn, nk),
        out_shape=[jax.ShapeDtypeStruct((N_DEV, W_IN_SLOT, N), F32), jax.ShapeDtypeStruct((N_DEV, W_IN_SLOT, N), jnp.bfloat16)],
        in_specs=[pl.BlockSpec((tk, M), lambda j, k: (k, 0)), pl.BlockSpec((tk, tn), lambda j, k: (k, j))],
        out_specs=[slots, slots],
        scratch_shapes=[pltpu.VMEM((M, tn), F32)],
        compiler_params=_cparams(("arbitrary", "arbitrary")),
    )(dz, hb)


SHARDED = (3, 6, 7, 9)
PART_SLICES = {10: (128, 1152), 2: (1152, 2176), 4: (2176, 2560), 5: (2560, 2816), 8: (2816, 2824)}


def _finalize_adamw(parts_all, dmod_all, dmod_cols, c_all, ws, sharded_grads, ms, vs):
    n = len(ws)
    local = [t for t in range(n) if t not in SHARDED]

    def body(*refs):
        p_ref, dm_ref, dmc_ref, c_ref = refs[:4]
        w_refs, gs_refs, m_refs, v_refs = refs[4:4 + n], refs[4 + n:8 + n], refs[8 + n:8 + 2 * n], refs[8 + 2 * n:8 + 3 * n]
        outs = refs[8 + 3 * n:]
        loss_ref, gl_refs = outs[0], outs[1:1 + len(local)]
        d_refs, nm_refs, nv_refs = (outs[1 + len(local) + t * n:1 + len(local) + (t + 1) * n] for t in range(3))
        ps_ref = outs[-1]
        acc = p_ref[0]
        for j in range(1, N_DEV):
            acc = acc + p_ref[j]
        ps_ref[...] = acc
        loss_ref[...] = jnp.sum(acc[:, 0:LANE], axis=1, keepdims=True)
        db = dm_ref[0:1, :]
        for j in range(1, dm_ref.shape[0]):
            db = db + dm_ref[j:j + 1, :]
        grads = {0: _dot_tn(_mx(_silu(c_ref[...])), _mx(dmc_ref[...])), 1: db}
        for t, (lo, hi) in PART_SLICES.items():
            grads[t] = ps_ref[:, lo:hi]
        for i, t in enumerate(SHARDED):
            grads[t] = gs_refs[i][...]
        for i, t in enumerate(local):
            gl_refs[i][...] = grads[t]
        for t in range(n):
            gv = grads[t]
            nm = ADAM_B1 * m_refs[t][...] + (1.0 - ADAM_B1) * gv
            nv = ADAM_B2 * v_refs[t][...] + (1.0 - ADAM_B2) * (gv * gv)
            m_hat = nm / (1.0 - ADAM_B1 ** ADAM_STEP)
            v_hat = nv / (1.0 - ADAM_B2 ** ADAM_STEP)
            d_refs[t][...] = -ADAM_LR * (m_hat / (jnp.sqrt(v_hat) + ADAM_EPS) + ADAM_WD * w_refs[t][...])
            nm_refs[t][...] = nm
            nv_refs[t][...] = nv

    like = lambda arrs: [jax.ShapeDtypeStruct(a.shape, F32) for a in arrs]
    out = pl.pallas_call(
        body, name="finalize_adamw",
        out_shape=[jax.ShapeDtypeStruct((1, 1), F32)] + like([ws[t] for t in local]) + like(ws) * 3,
        in_specs=[_vmem()] * (8 + 3 * n), out_specs=[_vmem()] * (1 + len(local) + 3 * n),
        scratch_shapes=[pltpu.VMEM(parts_all.shape[1:], F32)],
        compiler_params=_cparams(),
    )(parts_all, dmod_all, dmod_cols, c_all, *ws, *sharded_grads, *ms, *vs)
    k = 1 + len(local)
    return out[0], dict(zip(local, out[1:k])), out[k:k + n], out[k + n:k + 2 * n], out[k + 2 * n:]


def _pair_perm(a, axis, order):
    a = jnp.moveaxis(a, axis, -1)
    lead = a.shape[:-1]
    a = a.reshape(lead + (8, 64))[..., list(order), :].reshape(lead + (512,))
    return jnp.moveaxis(a, -1, axis)


def _rope_table(positions):
    T = positions.size
    inv = ROPE_THETA ** (-jnp.arange(0, MLA_ROPE, 2, dtype=F32) / MLA_ROPE)
    pos = jnp.repeat(positions.reshape(T // 8, 8)[:, ::-1].astype(F32), MLA_ROPE // 2, axis=1)
    ang = pos * jnp.tile(inv, 8)[None, :]
    return jnp.cos(ang), jnp.sin(ang)


def _local_step(x, mod, positions, ng, w_in_t, gq, gkv, w_uq, w_ukv, sinks, w_out, fg, tgt,
                ts=512, fq=512, fk=512, bq=512, bk=512):
    B, S, D = x.shape
    T = B * S
    x2 = x.reshape(T, D)
    shift, scale, gate = (mod[:, None, k * D:(k + 1) * D] for k in range(3))
    w_uq_p = jnp.pad(w_uq.reshape(Q_LORA, MLA_HEADS, 96), ((0, 0), (0, 0), (0, 32))).reshape(Q_LORA, MLA_HEADS * LANE)
    w_ukv3 = w_ukv.reshape(KV_LORA, MLA_HEADS, 128)
    w_uk_p = jnp.pad(w_ukv3[:, :, :64], ((0, 0), (0, 0), (0, 64))).reshape(KV_LORA, MLA_HEADS * LANE)
    w_uv = w_ukv3[:, :, 64:].reshape(KV_LORA, 512)
    w_out_p = jnp.concatenate([w_out[:512], _pair_perm(w_out[512:], 0, PAIR_ORDER)], axis=0)
    rope = _rope_table(positions)
    posf = positions.astype(F32)
    posb = jnp.broadcast_to(posf[:, :, None], (B, S, LANE))
    posr = posf.reshape(B, S // WINDOW, WINDOW)
    sinks_l = jnp.pad(sinks.reshape(1, SWA_HEADS), ((0, 0), (0, LANE - SWA_HEADS)))

    (hb, zq, zkv, ql, kvl, q, k, v, qs, ks, vs, g, w_in_p) = _pre_fwd(
        x2, shift, scale, ng, w_in_t, gq, gkv, w_uq_p, w_uk_p, w_uv, rope, S, ts)
    r3 = lambda a: a.reshape(B, S, a.shape[-1])
    om, lse_m = _mla_fwd(r3(q), r3(k), r3(v), fq, fk)
    osw, lse_s = _swa_fwd(r3(qs), r3(ks), r3(vs), posb, posr, sinks_l)
    dx2, do, dg, loss_v, dfg, dgate, dw_out = _post(
        om.reshape(T, 512), osw.reshape(T, 512), g, w_out_p, x2, gate, fg.reshape(1, D), tgt.reshape(T, D), S, ts)
    do3 = r3(do)
    dq, dk, dv = _mla_bwd(r3(q), r3(k), r3(v), om, do3, lse_m, bq, bk)
    dqs, dks, dvs, dsink = _swa_bwd(r3(qs), r3(ks), r3(vs), posb, posr, sinks_l, osw, do3, lse_s)
    f2 = lambda a: a.reshape(T, a.shape[-1])
    gx, dz, dgq, dgkv, dng, dsh, dsc, dw_uq_p, dw_uk_p, dw_uv = _pre_bwd(
        f2(dq), f2(dk), f2(dv), f2(dqs), f2(dks), f2(dvs), dg, zq, zkv, ql, kvl, x2, dx2, scale, ng, w_in_p, gq, gkv,
        w_uq_p, w_uk_p, w_uv, rope, S, ts)
    tk = min(T, 1024)
    dw_in_t = _dw_in_t(dz, hb, 512, tk)
    dw_uq = dw_uq_p.reshape(Q_LORA, MLA_HEADS, LANE)[:, :, :96].reshape(Q_LORA, 768)
    dw_uk = dw_uk_p.reshape(KV_LORA, MLA_HEADS, LANE)[:, :, :64]
    dw_uv = dw_uv.reshape(KV_LORA, MLA_HEADS, 64)
    dw_ukv = jnp.concatenate([dw_uk, dw_uv], axis=2).reshape(KV_LORA, 1024)
    parts = jnp.concatenate([loss_v, dfg, dng, dgq, dgkv, dsink], axis=1)
    dmod = jnp.concatenate([dsh, dsc, dgate], axis=2).reshape(B, 3 * D)
    return gx.reshape(B, S, D), dw_in_t, dw_uq, dw_ukv, dw_out, parts, dmod


def kernel(x, c, positions, w_ada, b_ada, norm_gain, w_in, q_norm_gain, kv_norm_gain, w_uq, w_ukv, swa_sinks, w_out, final_gain, loss_target, m_w_ada, m_b_ada, m_norm_gain, m_w_in, m_q_norm_gain, m_kv_norm_gain, m_w_uq, m_w_ukv, m_swa_sinks, m_w_out, m_final_gain, v_w_ada, v_b_ada, v_norm_gain, v_w_in, v_q_norm_gain, v_kv_norm_gain, v_w_uq, v_w_ukv, v_swa_sinks, v_w_out, v_final_gain):
    B, S, D = x.shape
    me = 4 * lax.axis_index("x") + 2 * lax.axis_index("y") + lax.axis_index("c")
    bf = _MXU_DTYPE

    ncol = w_ada.shape[2]
    b_cols = lax.dynamic_slice_in_dim(b_ada, me * ncol, ncol, axis=1)
    c_all, win_g, wuq_g, wukv_g, wout_g, mod_g = _all_gather(
        [c, w_in[0].T.astype(bf), w_uq[0].astype(bf), w_ukv[0].astype(bf), w_out[0].astype(bf)], "ag_weights",
        fused=((w_ada[0], b_cols), _ada_cols, jax.ShapeDtypeStruct((N_DEV, B, ncol), F32)))
    c_all = c_all.reshape(N_DEV * B, D)
    cat_cols = lambda a: jnp.transpose(a, (1, 0, 2)).reshape(a.shape[1], N_DEV * a.shape[2])
    w_in_t, w_uq_f, w_ukv_f = win_g, cat_cols(wuq_g), cat_cols(wukv_g)
    w_out_f = wout_g.reshape(D, D)

    mod = lax.dynamic_index_in_dim(mod_g, me, axis=1, keepdims=False)
    mod = jnp.transpose(mod, (1, 0, 2)).reshape(B, 3 * D)

    gx, dw_in_t, dw_uq, dw_ukv, dw_out, parts, dmod = _local_step(
        x, mod, positions, norm_gain, w_in_t, q_norm_gain, kv_norm_gain, w_uq_f, w_ukv_f, swa_sinks,
        w_out_f, final_gain, loss_target)

    split_cols = lambda a: jnp.transpose(a.reshape(a.shape[0], 4, 2, a.shape[1] // N_DEV), (1, 2, 0, 3))
    split_rows = lambda a: a.reshape(4, 2, a.shape[0] // N_DEV, a.shape[1])
    slots = lambda a: a.reshape((4, 2) + a.shape[1:])
    g_w_in_t, g_w_uq, g_w_ukv, g_w_out, parts_g, dmod_g = _reduce_scatter(
        [(slots(dw_in_t[0]), slots(dw_in_t[1])), split_cols(dw_uq), split_cols(dw_ukv), split_rows(dw_out)], "rs_grads",
        gather=(parts, dmod))
    g_w_in = g_w_in_t[:W_IN_SHARD].T

    dmod_all = dmod_g.reshape(N_DEV * B, 3 * D)
    dmod_cols = lax.dynamic_slice_in_dim(dmod_all, me * ncol, ncol, axis=1)
    ws = [w_ada, b_ada, norm_gain, w_in, q_norm_gain, kv_norm_gain, w_uq, w_ukv, swa_sinks, w_out, final_gain]
    ms = [m_w_ada, m_b_ada, m_norm_gain, m_w_in, m_q_norm_gain, m_kv_norm_gain, m_w_uq, m_w_ukv, m_swa_sinks, m_w_out,
          m_final_gain]
    vs = [v_w_ada, v_b_ada, v_norm_gain, v_w_in, v_q_norm_gain, v_kv_norm_gain, v_w_uq, v_w_ukv, v_swa_sinks, v_w_out,
          v_final_gain]
    two_d = [(1, w.shape[0]) if w.ndim == 1 else w.shape[-2:] for w in ws]
    flat = lambda arrs: [a.reshape(s) for a, s in zip(arrs, two_d)]
    sharded = {3: g_w_in, 6: g_w_uq, 7: g_w_ukv, 9: g_w_out}
    loss, local_grads, deltas, new_ms, new_vs = _finalize_adamw(
        parts_g, dmod_all, dmod_cols, c_all, flat(ws), [sharded[t] for t in SHARDED], flat(ms), flat(vs))
    grads = [sharded[t] if t in sharded else local_grads[t] for t in range(len(ws))]
    shaped = lambda arrs: [a.reshape(w.shape) for a, w in zip(arrs, ws)]
    return (loss.reshape(()), gx, *shaped(grads), *shaped(deltas), *shaped(new_ms), *shaped(new_vs))
```

```python
import functools

import jax
import jax.numpy as jnp
from jax import lax
from jax.experimental import pallas as pl
from jax.experimental.pallas import tpu as pltpu

F32 = jnp.float32
_MXU_DTYPE = jnp.bfloat16
MLA_GRAD_DTYPE = jnp.bfloat16

N_DEV = 8
MLA_HEADS = 8
MLA_NOPE = 64
MLA_ROPE = 32
Q_LORA = 384
KV_LORA = 256
SWA_HEADS = 8
SWA_KV_HEADS = 2
SWA_HEAD_DIM = 64
WINDOW = 128
ROPE_THETA = 10000.0
EPS = 1e-6
MLA_SCALE = float((MLA_NOPE + MLA_ROPE) ** -0.5)
SWA_SCALE = float(SWA_HEAD_DIM ** -0.5)
LOG2E = 1.4426950408889634
MLA_QSCALE = MLA_SCALE * LOG2E
D_IN = 2464
D_IN_PAD = 2560
PAIR_ORDER = (0, 4, 1, 5, 2, 6, 3, 7)
PAIR_INV = (0, 2, 4, 6, 1, 3, 5, 7)

ADAM_LR = 0.001
ADAM_B1 = 0.9
ADAM_B2 = 0.999
ADAM_EPS = 1e-08
ADAM_WD = 0.01
ADAM_STEP = 10

LANE = 128
VMEM_LIMIT = 56 * 1024 * 1024

MESH = pl.DeviceIdType.MESH
NEG_INF = float("-inf")
SWA_SEQ_SPLIT = 2


def _mx(a):
    return a.astype(_MXU_DTYPE)


def _dot(a, b):
    return jnp.dot(a, b, preferred_element_type=F32)


def _dot_nt(a, b):
    return lax.dot_general(a, b, (((1,), (1,)), ((), ())), preferred_element_type=F32)


def _dot_tn(a, b):
    return lax.dot_general(a, b, (((0,), (0,)), ((), ())), preferred_element_type=F32)


def _cparams(sem=None):
    return pltpu.CompilerParams(dimension_semantics=sem, vmem_limit_bytes=VMEM_LIMIT)


def _vmem():
    return pl.BlockSpec(memory_space=pltpu.VMEM)


def _lane_iota(shape):
    return lax.broadcasted_iota(jnp.int32, shape, len(shape) - 1)


def _gather_program(srcs, outs, send_sems, recv_sems, local_sems):
    x, y, c = lax.axis_index("x"), lax.axis_index("y"), lax.axis_index("c")
    me, sibling = (x, y, c), (x, y, 1 - c)
    chips = [(1 - x, y), (x, 1 - y), (1 - x, 1 - y)]

    def slot(a, dev):
        return outs[a].at[4 * dev[0] + 2 * dev[1] + dev[2]]

    def copy(a, k, block, to, src=None):
        return pltpu.make_async_remote_copy(
            src_ref=slot(a, block) if src is None else src, dst_ref=slot(a, block),
            send_sem=send_sems.at[7 * a + k], recv_sem=recv_sems.at[7 * a + k],
            device_id=to, device_id_type=MESH)

    def local(a):
        return pltpu.make_async_copy(srcs[a], slot(a, me), local_sems.at[a])

    def start(a):
        local(a).start()
        cps = [copy(a, 0, me, sibling, src=srcs[a])]
        cps += [copy(a, 1 + j, me, (*chip, c), src=srcs[a]) for j, chip in enumerate(chips)]
        for cp in cps:
            cp.start()
        return cps

    def finish(group):
        cps = []
        for j, chip in enumerate(chips):
            for a in group:
                copy(a, 1 + j, (*chip, c), me).wait_recv()
                cp = copy(a, 4 + j, (*chip, c), sibling)
                cp.start()
                cps.append(cp)
        for a in group:
            copy(a, 0, sibling, me).wait_recv()
            for j, chip in enumerate(chips):
                copy(a, 4 + j, (*chip, 1 - c), me).wait_recv()
            local(a).wait()
        return cps

    return start, finish


def _gather_sems(m):
    return [pltpu.SemaphoreType.DMA((7 * m,)), pltpu.SemaphoreType.DMA((7 * m,)), pltpu.SemaphoreType.DMA((m,))]


def _all_gather(arrs, name, fused=None):
    n = len(arrs)
    extra, fn, piece = fused if fused else ((), None, None)
    ne, m = len(extra), n + (1 if fused else 0)

    def body(*refs):
        ins, ex, outs = refs[:n], refs[n:n + ne], refs[n + ne:n + ne + m]
        rest = refs[n + ne + m:]
        srcs = list(ins) + ([rest[0]] if fused else [])
        start, finish = _gather_program(srcs, outs, *rest[-3:])
        pending = []
        for a in range(n):
            pending += start(a)
        if fused:
            pending += finish([0])
            fn(srcs[n], outs[0], *ex)
            pending += start(n)
            pending += finish([n] + list(range(1, n)))
        else:
            pending += finish(list(range(n)))
        for cp in pending:
            cp.wait_send()

    out_shape = [jax.ShapeDtypeStruct((N_DEV,) + a.shape, a.dtype) for a in arrs]
    scratch = []
    if fused:
        out_shape.append(jax.ShapeDtypeStruct((N_DEV,) + piece.shape, piece.dtype))
        scratch.append(pltpu.VMEM(piece.shape, piece.dtype))
    return pl.pallas_call(
        body, name=name, out_shape=out_shape,
        in_specs=[_vmem()] * (n + ne), out_specs=[_vmem()] * m,
        scratch_shapes=scratch + _gather_sems(m),
        compiler_params=pltpu.CompilerParams(vmem_limit_bytes=VMEM_LIMIT),
    )(*arrs, *extra)


def _reduce_scatter(arrs, name, gather=()):
    n, g = len(arrs), len(gather)
    halves = [a[1] if isinstance(a, tuple) else a.astype(jnp.bfloat16) for a in arrs]
    arrs = [a[0] if isinstance(a, tuple) else a for a in arrs]

    def body(*refs):
        xs, xbs, gins = refs[:n], refs[n:2 * n], refs[2 * n:2 * n + g]
        outs, gouts = refs[2 * n + g:3 * n + g], refs[3 * n + g:3 * n + 2 * g]
        rest = refs[3 * n + 2 * g:]
        parts, recv_a, send_b, recv_b = (rest[t * n:(t + 1) * n] for t in range(4))
        send_sems, recv_sems, local_sems = rest[4 * n:4 * n + 3]
        x, y, c = lax.axis_index("x"), lax.axis_index("y"), lax.axis_index("c")
        myq = 2 * x + y
        gather_start, gather_finish = _gather_program(gins, gouts, *rest[4 * n + 3:])

        def chip(k):
            return (1 - x if k & 2 else x, 1 - y if k & 1 else y)

        def from_sibling(a):
            return pltpu.make_async_remote_copy(
                src_ref=xbs[a].at[:, 1 - c], dst_ref=recv_a[a], send_sem=send_sems.at[4 * a], recv_sem=recv_sems.at[4 * a],
                device_id=(x, y, 1 - c), device_id_type=MESH)

        def to_owner(a, k):
            qx, qy = chip(k)
            return pltpu.make_async_remote_copy(
                src_ref=send_b[a].at[2 * qx + qy], dst_ref=recv_b[a].at[myq],
                send_sem=send_sems.at[4 * a + k], recv_sem=recv_sems.at[4 * a + k],
                device_id=(qx, qy, c), device_id_type=MESH)

        mine = [pltpu.make_async_copy(xs[a].at[:, c], parts[a], local_sems.at[a]) for a in range(n)]
        first = [from_sibling(a) for a in range(n)]
        for cp in mine + first:
            cp.start()
        second = []
        for b in range(g):
            second += gather_start(b)
        for a in range(n):
            mine[a].wait()
            first[a].wait_recv()
            parts[a][...] = parts[a][...] + recv_a[a][...].astype(F32)
            send_b[a][...] = parts[a][...].astype(jnp.bfloat16)
            for k in range(1, 4):
                cp = to_owner(a, k)
                cp.start()
                second.append(cp)
        second += gather_finish(list(range(g)))
        for a in range(n):
            acc = parts[a][myq]
            for k in range(1, 4):
                to_owner(a, k).wait_recv()
                qx, qy = chip(k)
                acc = acc + recv_b[a][2 * qx + qy].astype(F32)
            outs[a][...] = acc
        for cp in first + second:
            cp.wait_send()

    quarter = lambda a, dt: pltpu.VMEM((4,) + a.shape[2:], dt)
    return pl.pallas_call(
        body, name=name,
        out_shape=[jax.ShapeDtypeStruct(a.shape[2:], F32) for a in arrs]
        + [jax.ShapeDtypeStruct((N_DEV,) + a.shape, a.dtype) for a in gather],
        in_specs=[pl.BlockSpec(memory_space=pl.ANY)] * (2 * n) + [_vmem()] * g, out_specs=[_vmem()] * (n + g),
        scratch_shapes=[quarter(a, F32) for a in arrs] + [quarter(a, jnp.bfloat16) for a in arrs] * 3
        + [pltpu.SemaphoreType.DMA((4 * n,)), pltpu.SemaphoreType.DMA((4 * n,)), pltpu.SemaphoreType.DMA((n,))]
        + _gather_sems(g),
        compiler_params=pltpu.CompilerParams(vmem_limit_bytes=VMEM_LIMIT),
    )(*arrs, *halves, *gather)


def _silu(t):
    return t * (1.0 / (1.0 + jnp.exp(-t)))


def _ada_cols(piece_ref, c_all_ref, w_ref, b_ref):
    w = _mx(w_ref[...])
    for d in range(N_DEV):
        piece_ref[d] = _dot(_mx(_silu(c_all_ref[d])), w) + b_ref[...]


def _rope_split(cd, sd):
    n = cd.shape[0]

    def expand(d):
        rep = jnp.broadcast_to(d[:, None, :], (n, 8, LANE)).reshape(8 * n, LANE)
        return pltpu.roll(rep, 0, 1, stride=16, stride_axis=0)

    c, s = expand(cd), expand(sd)
    lane = _lane_iota(c.shape)
    first = jnp.logical_and(lane >= 64, lane < 80)
    second = jnp.logical_and(lane >= 80, lane < 96)
    ck = jnp.where(first, pltpu.roll(c, 80, 1), jnp.where(second, pltpu.roll(c, 96, 1), 0.0))
    cq = jnp.where(lane < 64, 1.0, ck)
    sa = jnp.where(first, -pltpu.roll(s, 80, 1), 0.0)
    sb = jnp.where(second, pltpu.roll(s, 96, 1), 0.0)
    return cq, ck, sa, sb


def _tile_heads(t):
    return jnp.concatenate([t] * MLA_HEADS, axis=1)


def _pre_fwd(x2, shift, scale, ng, w_in, gq, gkv, w_uq, w_uk, w_uv, rope, S, ts):
    T, D = x2.shape
    nsb = S // ts
    WQ = MLA_HEADS * LANE

    def body(x_ref, sh_ref, sc_ref, ng_ref, wn_ref, gq_ref, gkv_ref, wuq_ref, wuk_ref, wuv_ref,
             cd_ref, sd_ref,
             hb_ref, zq_ref, zkv_ref, ql_ref, kvl_ref, q_ref, k_ref, v_ref, qs_ref, ks_ref, vs_ref, g_ref, wp_ref):
        @pl.when(pl.program_id(0) == 0)
        def _():
            wp_ref[640:704, :] = jnp.zeros((64, D), wp_ref.dtype)
            wp_ref[736:768, :] = jnp.zeros((32, D), wp_ref.dtype)
            for d in range(N_DEV):
                lo, hi = W_IN_SHARD * d, W_IN_SHARD * (d + 1)
                for nat, pad, size in _w_in_row_runs():
                    a, b = max(nat, lo), min(nat + size, hi)
                    if a < b:
                        wp_ref[pad + a - nat:pad + b - nat, :] = wn_ref[d, a - lo:b - lo, :]

        w_ref = wp_ref
        x = x_ref[...]
        r1 = lax.rsqrt(jnp.mean(x * x, axis=-1, keepdims=True) + EPS)
        h = ((x * r1) * ng_ref[...]) * (1.0 + sc_ref[0]) + sh_ref[0]
        hb = _mx(h)
        hb_ref[...] = hb
        zq = _dot_nt(hb, w_ref[0:384, :])
        zq_ref[...] = zq
        rq = lax.rsqrt(jnp.mean(zq * zq, axis=-1, keepdims=True) + EPS)
        ql = _mx((zq * rq) * gq_ref[...])
        ql_ref[...] = ql
        q = _dot(ql, wuq_ref[...])
        cq, ck, sa, sb = _rope_split(cd_ref[...], sd_ref[...])
        q = (q * _tile_heads(cq) + pltpu.roll(q, WQ - 16, 1) * _tile_heads(sa)
             + pltpu.roll(q, 16, 1) * _tile_heads(sb))
        q_ref[...] = _mx(q * MLA_QSCALE)
        zkv = _dot_nt(hb, w_ref[384:640, :])
        zkv_ref[...] = zkv
        rkv = lax.rsqrt(jnp.mean(zkv * zkv, axis=-1, keepdims=True) + EPS)
        kvl = _mx((zkv * rkv) * gkv_ref[...])
        kvl_ref[...] = kvl
        kr = _dot_nt(hb, w_ref[640:768, :])
        kpe = kr * ck + pltpu.roll(kr, LANE - 16, 1) * sa + pltpu.roll(kr, 16, 1) * sb
        kf = _dot(kvl, wuk_ref[...])
        k_ref[...] = _mx(kf + jnp.concatenate([kpe] * MLA_HEADS, axis=1))
        v_ref[...] = _mx(_dot(kvl, wuv_ref[...]))
        g_ref[:, 0:512] = _dot_nt(hb, w_ref[768:1280, :])
        qs_ref[...] = _mx(_dot_nt(hb, w_ref[1280:1792, :]) * (SWA_SCALE * LOG2E))
        ks_ref[...] = _mx(_dot_nt(hb, w_ref[1792:1920, :]))
        vs_ref[...] = _mx(_dot_nt(hb, w_ref[1920:2048, :]))
        g_ref[:, 512:1024] = _dot_nt(hb, w_ref[2048:2560, :])

    row = lambda w: pl.BlockSpec((ts, w), lambda i: (i, 0))
    dense = pl.BlockSpec((ts // 8, LANE), lambda i: (i, 0))
    full = lambda a: pl.BlockSpec(a.shape, lambda i: (0,) * a.ndim)
    per_b = pl.BlockSpec((1, 1, D), lambda i: (i // nsb, 0, 0))
    out_w = [(D, _MXU_DTYPE), (384, F32), (256, F32), (384, _MXU_DTYPE), (256, _MXU_DTYPE), (WQ, _MXU_DTYPE),
             (WQ, _MXU_DTYPE), (512, _MXU_DTYPE), (512, _MXU_DTYPE), (128, _MXU_DTYPE), (128, _MXU_DTYPE), (1024, F32)]
    return pl.pallas_call(
        body, name="pre_fwd", grid=(T // ts,),
        out_shape=[jax.ShapeDtypeStruct((T, w), dt) for w, dt in out_w] + [jax.ShapeDtypeStruct((D_IN_PAD, D), w_in.dtype)],
        in_specs=[row(D), per_b, per_b, full(ng), full(w_in), full(gq), full(gkv), full(w_uq), full(w_uk), full(w_uv),
                  dense, dense],
        out_specs=[row(w) for w, _ in out_w] + [pl.BlockSpec((D_IN_PAD, D), lambda i: (0, 0))],
        compiler_params=_cparams(("arbitrary",)),
    )(x2, shift, scale, ng, w_in, gq, gkv, w_uq, w_uk, w_uv, *rope)


def _mla_fwd(q3, k3, v3, tq, tk):
    B, S, _ = q3.shape
    nq = S // tq
    assert tq == tk
    HPS = 2
    NH = 2 * HPS

    def body(q_ref, k_ref, v_ref, o_ref, lse_ref):
        rows = lax.broadcasted_iota(jnp.int32, (tq, tk), 0)
        cols = lax.broadcasted_iota(jnp.int32, (tq, tk), 1)
        low_k = _lane_iota((tk, LANE)) < 64
        low = _lane_iota((tq, LANE)) < 64

        def step(qs, kt, carry, masked):
            r0 = kt * tk
            out = []
            for j in range(NH):
                v2 = v_ref[0, r0:r0 + tk, LANE * (j // 2):LANE * (j // 2 + 1)]
                vj = (jnp.where(low_k, v2, 1.0) if j % 2 == 0 else jnp.where(low_k, 1.0, v2)).astype(v2.dtype)
                m, acc = carry[j]
                s = _dot_nt(qs[j], k_ref[0, r0:r0 + tk, LANE * j:LANE * (j + 1)])
                if masked:
                    s = jnp.where(rows >= cols, s, NEG_INF)
                m_new = jnp.maximum(m, jnp.max(s, axis=1, keepdims=True))
                alpha = jnp.exp2(m - m_new)
                p = jnp.exp2(s - m_new)
                acc = alpha * acc + _dot(_mx(p), vj)
                out.append((m_new, acc))
            return tuple(out)

        for qi in range(nq):
            r = slice(qi * tq, (qi + 1) * tq)
            qs = [q_ref[0, r, LANE * j:LANE * (j + 1)] for j in range(NH)]
            init = (jnp.full((tq, 1), NEG_INF, F32), jnp.zeros((tq, LANE), F32))
            carry = (init,) * NH
            for kt in range(qi):
                carry = step(qs, kt, carry, False)
            carry = step(qs, qi, carry, True)
            for t in range(HPS):
                (m0, a0), (m1, a1) = carry[2 * t], carry[2 * t + 1]
                l0 = jnp.where(low, pltpu.roll(a0, 64, 1), a0)
                l1 = jnp.where(low, a1, pltpu.roll(a1, 64, 1))
                o_ref[0, r, LANE * t:LANE * (t + 1)] = jnp.where(low, a0 / l0, a1 / l1)
                lse_ref[0, 2 * t, r, :] = m0 + jnp.log2(l0)
                lse_ref[0, 2 * t + 1, r, :] = m1 + jnp.log2(l1)

    grp = lambda w: pl.BlockSpec((1, S, w), lambda b, hp: (b, 0, hp))
    return pl.pallas_call(
        body, name="mla_fwd", grid=(B, MLA_HEADS // NH),
        out_shape=[jax.ShapeDtypeStruct((B, S, 512), F32), jax.ShapeDtypeStruct((B, MLA_HEADS, S, LANE), F32)],
        in_specs=[grp(NH * LANE), grp(NH * LANE), grp(HPS * LANE)],
        out_specs=[grp(HPS * LANE), pl.BlockSpec((1, NH, S, LANE), lambda b, hp: (b, hp, 0, 0))],
        compiler_params=_cparams(("arbitrary", "arbitrary")),
    )(q3, k3, v3)


def _mla_bwd(q3, k3, v3, o3, do3, lse, tq, tk, hps=2):
    B, S, _ = q3.shape
    nq, nk = S // tq, S // tk
    nh = 2 * hps
    assert tk % tq == 0

    def body(q_ref, k_ref, v_ref, o_ref, do_ref, lse_ref, dq_ref, dk_ref, dv_ref, dkt_ref, dvt_ref):
        rows = lax.broadcasted_iota(jnp.int32, (tq, tk), 0)
        cols = lax.broadcasted_iota(jnp.int32, (tq, tk), 1)
        lane = _lane_iota((tq, LANE))

        def q_tile(qi, _):
            r = pl.ds(pl.multiple_of(qi * tq, tq), tq)
            heads = []
            for j in range(nh):
                lanes = slice(LANE * j, LANE * (j + 1))
                pair = slice(LANE * (j // 2), LANE * (j // 2 + 1))
                do = jnp.where((lane < 64) if j % 2 == 0 else (lane >= 64), do_ref[0, r, pair], 0.0)
                q = q_ref[0, r, lanes]
                heads.append((lanes, pair, q, _mx(q.astype(F32).T), _mx(do), _mx(do.T),
                              jnp.sum(do * o_ref[0, r, pair], axis=1, keepdims=True),
                              jnp.concatenate([lse_ref[0, j, r, :]] * (tk // LANE), axis=1)))
            n_full = (qi * tq) // tk

            def k_tile(kt, dqs, masked):
                kr = pl.ds(pl.multiple_of(kt * tk, tk), tk)
                first = masked and (qi * tq) % tk == 0
                out = []
                dvt = [None] * hps
                for j, (lanes, pair, q, qt, dob, dot_, dcol, lse_c) in enumerate(heads):
                    k = k_ref[0, kr, lanes]
                    s = _dot_nt(q, k)
                    if masked:
                        s = jnp.where(rows + qi * tq >= cols + kt * tk, s, NEG_INF)
                    p = jnp.exp2(s - lse_c)
                    dp = _dot_nt(dob, v_ref[0, kr, pair])
                    dsb = _mx(p * (dp - dcol))
                    if first:
                        dkt_ref[j, kt] = _dot(qt, dsb)
                    else:
                        dkt_ref[j, kt] += _dot(qt, dsb)
                    pv = _dot(dot_, _mx(p))
                    dvt[j // 2] = pv if dvt[j // 2] is None else dvt[j // 2] + pv
                    out.append(dqs[j] + _dot(dsb, k))
                for t in range(hps):
                    if first:
                        dvt_ref[t, kt] = dvt[t]
                    else:
                        dvt_ref[t, kt] += dvt[t]
                return tuple(out)

            dqs = (jnp.zeros((tq, LANE), F32),) * nh
            for kt in range(n_full):
                dqs = k_tile(kt, dqs, False)
            dqs = k_tile(n_full, dqs, True)
            for j in range(nh):
                dq_ref[0, r, heads[j][0]] = (MLA_SCALE * dqs[j]).astype(dq_ref.dtype)
            return 0

        for qi in range(nq):
            q_tile(qi, 0)

        def flush(kt, _):
            kr = pl.ds(pl.multiple_of(kt * tk, tk), tk)
            for j in range(nh):
                dk_ref[0, kr, LANE * j:LANE * (j + 1)] = ((1.0 / LOG2E) * dkt_ref[j, kt].T).astype(dk_ref.dtype)
            for t in range(hps):
                dv_ref[0, kr, LANE * t:LANE * (t + 1)] = dvt_ref[t, kt].T.astype(dv_ref.dtype)
            return 0

        lax.fori_loop(0, nk, flush, 0)

    grp = lambda w: pl.BlockSpec((1, S, w), lambda b, hp: (b, 0, hp))
    return pl.pallas_call(
        body, name="mla_bwd", grid=(B, MLA_HEADS // nh),
        out_shape=[jax.ShapeDtypeStruct((B, S, 1024), MLA_GRAD_DTYPE), jax.ShapeDtypeStruct((B, S, 1024), MLA_GRAD_DTYPE),
                   jax.ShapeDtypeStruct((B, S, 512), MLA_GRAD_DTYPE)],
        in_specs=[grp(nh * LANE), grp(nh * LANE), grp(hps * LANE), grp(hps * LANE), grp(hps * LANE),
                  pl.BlockSpec((1, nh, S, LANE), lambda b, hp: (b, hp, 0, 0))],
        out_specs=[grp(nh * LANE), grp(nh * LANE), grp(hps * LANE)],
        scratch_shapes=[pltpu.VMEM((nh, nk, LANE, tk), F32), pltpu.VMEM((hps, nk, LANE, tk), F32)],
        compiler_params=_cparams(("arbitrary", "arbitrary")),
    )(q3, k3, v3, o3, do3, lse)


def _swa_consts(sink_ref):
    W = WINDOW
    row = lax.broadcasted_iota(jnp.int32, (4 * W, LANE), 0)
    out = []
    for g in range(SWA_KV_HEADS):
        slope = jnp.zeros((4 * W, LANE), F32)
        sink = jnp.zeros((4 * W, LANE), F32)
        for p in range(4):
            h = p + 4 * g
            here = jnp.logical_and(row >= W * p, row < W * (p + 1))
            slope = jnp.where(here, float(LOG2E * 2.0 ** (-8.0 * (h + 1) / SWA_HEADS)), slope)
            sink = jnp.where(here, LOG2E * sink_ref[:, h:h + 1], sink)
        out.append((jnp.concatenate([slope, slope], axis=1), sink))
    return out


def _wide(col):
    return jnp.concatenate([col, col], axis=1)


def _swa_block(q_ref, k_ref, v_ref, pb_ref, pr_ref, n, i, ij):
    W = WINDOW
    kb = jnp.maximum(n - 1, 0)
    r = pl.ds(pl.multiple_of(i * W, W), W)
    kr = pl.ds(pl.multiple_of(kb * W, W), 2 * W)
    q4, k2, v2 = q_ref[0, r, :], k_ref[0, kr, :], v_ref[0, kr, :]
    pq = _wide(pb_ref[0, r, :])
    pk = jnp.concatenate([pr_ref[0, pl.ds(kb, 1), :], pr_ref[0, pl.ds(kb + 1, 1), :]], axis=1)
    rel = ij + (n - kb) * W
    dist = jnp.where(jnp.logical_and(rel >= 0, rel < W), pq - pk, float("inf"))
    return r, kb, kr, q4, k2, v2, jnp.concatenate([dist] * 4, axis=0)


def _swa_stack(x4, g, dtype):
    lane = _lane_iota((WINDOW, LANE))
    mine = (lane < 64) if g == 0 else (lane >= 64)
    return jnp.concatenate([jnp.where(mine, x4[:, LANE * p:LANE * (p + 1)], 0).astype(dtype) for p in range(4)], axis=0)


def _swa_unstack(ref, r, lo, hi, scale=None):
    W = WINDOW
    low = _lane_iota((W, LANE)) < 64
    for p in range(4):
        t = jnp.where(low, lo[W * p:W * (p + 1)], hi[W * p:W * (p + 1)])
        ref[0, r, LANE * p:LANE * (p + 1)] = t if scale is None else scale * t


def _swa_fwd(qs3, ks3, vs3, posb, posr, sinks):
    B, S, _ = qs3.shape
    W = WINDOW
    nb = S // W
    nh = SWA_SEQ_SPLIT
    nbh = nb // nh

    def body(q_ref, k_ref, v_ref, pb_ref, pr_ref, sink_ref, o_ref, lse_ref):
        ij = lax.broadcasted_iota(jnp.int32, (W, 2 * W), 0) - lax.broadcasted_iota(jnp.int32, (W, 2 * W), 1)
        consts = _swa_consts(sink_ref)
        n0 = pl.program_id(1) * nbh

        def blk(i, _):
            n = n0 + i
            r, _, _, q4, k2, v2, dist4 = _swa_block(q_ref, k_ref, v_ref, pb_ref, pr_ref, n, i, ij)
            o_g = []
            for g, (slope, sink) in enumerate(consts):
                s = _dot_nt(_swa_stack(q4, g, q4.dtype), k2) - slope * dist4
                m = jnp.maximum(jnp.max(s, axis=1, keepdims=True), sink)
                e = jnp.exp2(s - _wide(m))
                l = jnp.sum(e, axis=1, keepdims=True) + jnp.exp2(sink - m)
                o_g.append(_dot(_mx(e), v2) * (1.0 / l))
                lse_ref[0, i, g] = m + jnp.log2(l)
            _swa_unstack(o_ref, r, o_g[0], o_g[1])
            return 0

        for i in range(nbh):
            blk(i, 0)

    seq = lambda w: pl.BlockSpec((1, S, w), lambda b, h: (b, 0, 0))
    part = lambda w: pl.BlockSpec((1, S // nh, w), lambda b, h: (b, h, 0))
    lse_spec = pl.BlockSpec((1, nbh, 2, 4 * W, LANE), lambda b, h: (b, h, 0, 0, 0))
    return pl.pallas_call(
        body, name="swa_fwd", grid=(B, nh),
        out_shape=[jax.ShapeDtypeStruct((B, S, 512), F32), jax.ShapeDtypeStruct((B, nb, 2, 4 * W, LANE), F32)],
        in_specs=[part(512), seq(LANE), seq(LANE), part(LANE), pl.BlockSpec((1, nb, W), lambda b, h: (b, 0, 0)),
                  pl.BlockSpec((1, LANE), lambda b, h: (0, 0))],
        out_specs=[part(512), lse_spec],
        compiler_params=_cparams(("arbitrary", "arbitrary")),
    )(qs3, ks3, vs3, posb, posr, sinks)


def _swa_bwd(qs3, ks3, vs3, posb, posr, sinks, os3, do3, lse):
    B, S, _ = qs3.shape
    W = WINDOW
    nb = S // W
    nh = SWA_SEQ_SPLIT
    nbh = nb // nh
    assert nbh % 2 == 0 and nbh * nh * W == S

    def body(q_ref, k_ref, v_ref, pb_ref, pr_ref, sink_ref, o_ref, do_ref, lse_ref, dq_ref, dk_ref, dv_ref, dsink_ref,
             dkt_ref, dvt_ref):
        lane1 = _lane_iota((1, LANE))
        ij = lax.broadcasted_iota(jnp.int32, (W, 2 * W), 0) - lax.broadcasted_iota(jnp.int32, (W, 2 * W), 1)
        consts = _swa_consts(sink_ref)
        hh = pl.program_id(1)
        n0 = hh * nbh

        @pl.when(hh == 0)
        def _():
            dkt_ref[...] = jnp.zeros_like(dkt_ref)
            dvt_ref[...] = jnp.zeros_like(dvt_ref)

        @pl.when(jnp.logical_and(pl.program_id(0) == 0, hh == 0))
        def _():
            dsink_ref[...] = jnp.zeros_like(dsink_ref)

        def blk(i, dsink):
            n = n0 + i
            r, kb, _, q4, k2, v2, dist4 = _swa_block(q_ref, k_ref, v_ref, pb_ref, pr_ref, n, i, ij)
            o4, do4 = o_ref[0, r, :], do_ref[0, r, :]
            dq_g = []
            dkt = jnp.zeros((LANE, 2 * W), F32)
            dvt = jnp.zeros((LANE, 2 * W), F32)
            for g, (slope, sink) in enumerate(consts):
                q_st = _swa_stack(q4, g, F32)
                do_st = _swa_stack(do4, g, F32)
                dcol = jnp.sum(do_st * _swa_stack(o4, g, F32), axis=1, keepdims=True)
                qst, dob = _mx(q_st), _mx(do_st)
                lse_c = lse_ref[0, i, g]
                pr = jnp.exp2(_dot_nt(qst, k2) - slope * dist4 - _wide(lse_c))
                dsb = _mx(pr * (_dot_nt(dob, v2) - dcol))
                psd = jnp.exp2(sink - lse_c)[:, 0:1] * dcol
                for p in range(4):
                    dsink = dsink - jnp.where(lane1 == p + 4 * g,
                                              jnp.sum(psd[W * p:W * (p + 1)], axis=0, keepdims=True), 0.0)
                dq_g.append(_dot(dsb, k2))
                dkt = dkt + _dot(_mx(q_st.T), dsb)
                dvt = dvt + _dot(_mx(do_st.T), _mx(pr))
            _swa_unstack(dq_ref, r, dq_g[0], dq_g[1], SWA_SCALE)
            dkt_ref[kb] += dkt[:, 0:W]
            dkt_ref[kb + 1] += dkt[:, W:2 * W]
            dvt_ref[kb] += dvt[:, 0:W]
            dvt_ref[kb + 1] += dvt[:, W:2 * W]
            return dsink

        dsink = jnp.zeros((1, LANE), F32)
        for i in range(nbh):
            dsink = blk(i, dsink)
        dsink_ref[...] += dsink

        @pl.when(hh == nh - 1)
        def _():
            def flush(n, _):
                r = pl.ds(pl.multiple_of(n * W, W), W)
                dk_ref[0, r, :] = (1.0 / LOG2E) * dkt_ref[n].T
                dv_ref[0, r, :] = dvt_ref[n].T
                return 0

            lax.fori_loop(0, nb, flush, 0)

    seq = lambda w: pl.BlockSpec((1, S, w), lambda b, h: (b, 0, 0))
    part = lambda w: pl.BlockSpec((1, S // nh, w), lambda b, h: (b, h, 0))
    return pl.pallas_call(
        body, name="swa_bwd", grid=(B, nh),
        out_shape=[jax.ShapeDtypeStruct((B, S, 512), F32), jax.ShapeDtypeStruct((B, S, LANE), F32),
                   jax.ShapeDtypeStruct((B, S, LANE), F32), jax.ShapeDtypeStruct((1, LANE), F32)],
        in_specs=[part(512), seq(LANE), seq(LANE), part(LANE), pl.BlockSpec((1, nb, W), lambda b, h: (b, 0, 0)),
                  pl.BlockSpec((1, LANE), lambda b, h: (0, 0)), part(512),
                  pl.BlockSpec((1, S // nh, 512), lambda b, h: (b, h, 1)),
                  pl.BlockSpec((1, nbh, 2, 4 * W, LANE), lambda b, h: (b, h, 0, 0, 0))],
        out_specs=[part(512), seq(LANE), seq(LANE), pl.BlockSpec((1, LANE), lambda b, h: (0, 0))],
        scratch_shapes=[pltpu.VMEM((nb, LANE, W), F32), pltpu.VMEM((nb, LANE, W), F32)],
        compiler_params=_cparams(("arbitrary", "arbitrary")),
    )(qs3, ks3, vs3, posb, posr, sinks, os3, do3, lse)


def _post(om, osw, g, w_out, x2, gate, fg, tgt, S, ts):
    T, D = x2.shape
    nsb = S // ts

    def body(om_ref, os_ref, g_ref, w_ref, x_ref, gate_ref, fg_ref, t_ref,
             dx2_ref, do_ref, dg_ref, loss_ref, dfg_ref, dgate_ref, dw_ref):
        i = pl.program_id(0)
        gv = g_ref[...]
        sg = 1.0 / (1.0 + jnp.exp(-gv))
        silu = gv * sg
        o = jnp.concatenate([om_ref[...], os_ref[...]], axis=1)
        ab = _mx(o * silu)
        y = _dot(ab, w_ref[...])
        gate = gate_ref[0]
        xo = x_ref[...] + gate * y
        r2 = lax.rsqrt(jnp.mean(xo * xo, axis=-1, keepdims=True) + EPS)
        xh = xo * r2
        fg = fg_ref[...]
        diff = xh * fg - t_ref[...]
        sq = jnp.sum(diff * diff, axis=0, keepdims=True)
        part = sq[:, 0:LANE]
        for t in range(1, D // LANE):
            part = part + sq[:, LANE * t:LANE * (t + 1)]
        dout = diff * (1.0 / D)
        dxh = dout * fg
        dx2 = r2 * (dxh - xh * jnp.mean(dxh * xh, axis=-1, keepdims=True))
        dx2_ref[...] = dx2
        dyb = _mx(dx2 * gate)
        da = _dot_nt(dyb, w_ref[...])
        do_ref[...] = da * silu
        dg_ref[...] = _mx(da * o * (sg * (1.0 + gv * (1.0 - sg))))

        @pl.when(i == 0)
        def _():
            loss_ref[...] = jnp.zeros_like(loss_ref)
            dfg_ref[...] = jnp.zeros_like(dfg_ref)
            dw_ref[...] = jnp.zeros_like(dw_ref)

        @pl.when(i % nsb == 0)
        def _():
            dgate_ref[...] = jnp.zeros_like(dgate_ref)

        loss_ref[...] += (0.5 / D) * part
        dfg_ref[...] += jnp.sum(dout * xh, axis=0, keepdims=True)
        dgate_ref[0] += jnp.sum(dx2 * y, axis=0, keepdims=True)
        dw_ref[...] += _dot_tn(ab, dyb)

        @pl.when(i == T // ts - 1)
        def _():
            chunks = [dw_ref[512 + 64 * PAIR_INV[h]:512 + 64 * (PAIR_INV[h] + 1), :] for h in range(SWA_HEADS)]
            for h in range(SWA_HEADS):
                dw_ref[512 + 64 * h:512 + 64 * (h + 1), :] = chunks[h]

    row = lambda w: pl.BlockSpec((ts, w), lambda i: (i, 0))
    full = lambda a: pl.BlockSpec(a.shape, lambda i: (0,) * a.ndim)
    per_b = pl.BlockSpec((1, 1, D), lambda i: (i // nsb, 0, 0))
    return pl.pallas_call(
        body, name="post", grid=(T // ts,),
        out_shape=[jax.ShapeDtypeStruct((T, D), F32), jax.ShapeDtypeStruct((T, 1024), F32),
                   jax.ShapeDtypeStruct((T, 1024), _MXU_DTYPE), jax.ShapeDtypeStruct((1, LANE), F32),
                   jax.ShapeDtypeStruct((1, D), F32), jax.ShapeDtypeStruct(gate.shape, F32),
                   jax.ShapeDtypeStruct(w_out.shape, F32)],
        in_specs=[row(512), row(512), row(1024), full(w_out), row(D), per_b, full(fg), row(D)],
        out_specs=[row(D), row(1024), row(1024),
                   pl.BlockSpec((1, LANE), lambda i: (0, 0)), pl.BlockSpec((1, D), lambda i: (0, 0)), per_b,
                   full(w_out)],
        compiler_params=_cparams(("arbitrary",)),
    )(om, osw, g, w_out, x2, gate, fg, tgt)


def _pre_bwd(dq, dk, dv, dqs, dks, dvs, dg, zq, zkv, ql, kvl, x2, dx2, scale, ng, w_in, gq, gkv, w_uq, w_uk, w_uv,
             rope, S, ts):
    T, D = x2.shape
    nsb = S // ts
    WQ = MLA_HEADS * LANE

    def body(dq_ref, dk_ref, dv_ref, dqs_ref, dks_ref, dvs_ref, dg_ref, zq_ref, zkv_ref, ql_ref, kvl_ref, x_ref, dx2_ref,
             sc_ref,
             ng_ref, w_ref, gq_ref, gkv_ref, wuq_ref, wuk_ref, wuv_ref, cd_ref, sd_ref,
             gx_ref, dz_ref, dgq_ref, dgkv_ref, dng_ref, dsh_ref, dsc_ref, dwuq_ref, dwuk_ref, dwuv_ref):
        i = pl.program_id(0)

        @pl.when(i == 0)
        def _():
            dgq_ref[...] = jnp.zeros_like(dgq_ref)
            dgkv_ref[...] = jnp.zeros_like(dgkv_ref)
            dng_ref[...] = jnp.zeros_like(dng_ref)
            dwuq_ref[...] = jnp.zeros_like(dwuq_ref)
            dwuk_ref[...] = jnp.zeros_like(dwuk_ref)
            dwuv_ref[...] = jnp.zeros_like(dwuv_ref)

        @pl.when(i % nsb == 0)
        def _():
            dsh_ref[...] = jnp.zeros_like(dsh_ref)
            dsc_ref[...] = jnp.zeros_like(dsc_ref)

        def norm_bwd(z, dy, gain):
            r = lax.rsqrt(jnp.mean(z * z, axis=-1, keepdims=True) + EPS)
            zh = z * r
            dzh = dy * gain
            return r * (dzh - zh * jnp.mean(dzh * zh, axis=-1, keepdims=True)), jnp.sum(dy * zh, axis=0, keepdims=True)

        tables = _rope_split(cd_ref[...], sd_ref[...])
        ng = ng_ref[...]
        sc1 = 1.0 + sc_ref[0]

        def rows_chain(rs):
            cq, ck, sa, sb = (t[rs] for t in tables)
            dqr = dq_ref[rs, :].astype(F32)
            dqb = _mx(dqr * _tile_heads(cq) + pltpu.roll(dqr * _tile_heads(sa), 16, 1)
                      + pltpu.roll(dqr * _tile_heads(sb), WQ - 16, 1))
            p_uq = _dot_tn(ql_ref[rs, :], dqb)
            dzq, dgq = norm_bwd(zq_ref[rs, :], _dot_nt(dqb, wuq_ref[...]), gq_ref[...])
            dkr = dk_ref[rs, :].astype(F32)
            dkb = _mx(dkr)
            p_uk = _dot_tn(kvl_ref[rs, :], dkb)
            dvb = _mx(dv_ref[rs, :])
            p_uv = _dot_tn(kvl_ref[rs, :], dvb)
            dzkv, dgkv = norm_bwd(zkv_ref[rs, :], _dot_nt(dkb, wuk_ref[...]) + _dot_nt(dvb, wuv_ref[...]), gkv_ref[...])
            dkpe = dkr[:, 0:LANE]
            for h in range(1, MLA_HEADS):
                dkpe = dkpe + dkr[:, LANE * h:LANE * (h + 1)]
            dkro = dkpe * ck + pltpu.roll(dkpe * sa, 16, 1) + pltpu.roll(dkpe * sb, LANE - 16, 1)
            dz_ref[rs, 0:384] = _mx(dzq)
            dz_ref[rs, 384:640] = _mx(dzkv)
            dz_ref[rs, 640:768] = _mx(dkro)
            dz_ref[rs, 768:1280] = dg_ref[rs, 0:512]
            dz_ref[rs, 1280:1792] = _mx(dqs_ref[rs, :])
            dz_ref[rs, 1792:1920] = _mx(dks_ref[rs, :])
            dz_ref[rs, 1920:2048] = _mx(dvs_ref[rs, :])
            dz_ref[rs, 2048:2560] = dg_ref[rs, 512:1024]
            dh = _dot(dz_ref[rs, :], w_ref[...])
            x = x_ref[rs, :]
            r1 = lax.rsqrt(jnp.mean(x * x, axis=-1, keepdims=True) + EPS)
            xn = x * r1
            dxn = dh * ng * sc1
            gx_ref[rs, :] = dx2_ref[rs, :] + r1 * (dxn - xn * jnp.mean(dxn * xn, axis=-1, keepdims=True))
            return (p_uq, p_uk, p_uv, dgq, dgkv, jnp.sum(dh, axis=0, keepdims=True),
                    jnp.sum(dh * (xn * ng), axis=0, keepdims=True), jnp.sum(dh * xn * sc1, axis=0, keepdims=True))

        hr = ts // 2
        parts = [rows_chain(slice(hr * t, hr * (t + 1))) for t in range(2)]
        p_uq, p_uk, p_uv, dgq, dgkv, dsh, dsc, dng = (a + b for a, b in zip(*parts))
        dwuq_ref[...] += p_uq
        dwuk_ref[...] += p_uk
        dwuv_ref[...] += p_uv
        dgq_ref[...] += dgq
        dgkv_ref[...] += dgkv
        dsh_ref[0] += dsh
        dsc_ref[0] += dsc
        dng_ref[...] += dng

    row = lambda w: pl.BlockSpec((ts, w), lambda i: (i, 0))
    full = lambda a: pl.BlockSpec(a.shape, lambda i: (0,) * a.ndim, pipeline_mode=pl.Buffered(1))
    per_b = pl.BlockSpec((1, 1, D), lambda i: (i // nsb, 0, 0))
    dense = pl.BlockSpec((ts // 8, LANE), lambda i: (i, 0))
    vec = lambda w: pl.BlockSpec((1, w), lambda i: (0, 0))
    return pl.pallas_call(
        body, name="pre_bwd", grid=(T // ts,),
        out_shape=[jax.ShapeDtypeStruct((T, D), F32), jax.ShapeDtypeStruct((T, D_IN_PAD), _MXU_DTYPE), jax.ShapeDtypeStruct((1, 384), F32),
                   jax.ShapeDtypeStruct((1, 256), F32), jax.ShapeDtypeStruct((1, D), F32),
                   jax.ShapeDtypeStruct(scale.shape, F32), jax.ShapeDtypeStruct(scale.shape, F32),
                   jax.ShapeDtypeStruct(w_uq.shape, F32), jax.ShapeDtypeStruct(w_uk.shape, F32),
                   jax.ShapeDtypeStruct(w_uv.shape, F32)],
        in_specs=[row(WQ), row(WQ), row(512), row(512), row(LANE), row(LANE), row(1024), row(384), row(256), row(384),
                  row(256), row(D), row(D), per_b, full(ng), full(w_in), full(gq), full(gkv), full(w_uq), full(w_uk), full(w_uv),
                  dense, dense],
        out_specs=[row(D), row(D_IN_PAD), vec(384), vec(256), vec(D), per_b, per_b, full(w_uq), full(w_uk), full(w_uv)],
        compiler_params=_cparams(("arbitrary",)),
    )(dq, dk, dv, dqs, dks, dvs, dg, zq, zkv, ql, kvl, x2, dx2, scale, ng, w_in, gq, gkv, w_uq, w_uk, w_uv, *rope)


def _w_in_row_runs():
    runs = [(0, 0, 640), (640, 704, 32), (672, 768, 512)]
    runs += [(1184 + 64 * h, 1280 + 64 * PAIR_INV[h], 64) for h in range(8)]
    runs += [(1696, 1792, 256)]
    runs += [(1952 + 64 * h, 2048 + 64 * PAIR_INV[h], 64) for h in range(8)]
    return runs


W_IN_SHARD = D_IN // N_DEV
W_IN_SLOT = 320


def _dw_in_t(dz, hb, tn, tk):
    T, M = dz.shape
    N = hb.shape[1]
    nk = T // tk

    def body(a_ref, b_ref, o_ref, ob_ref, acc_ref):
        k = pl.program_id(1)

        @pl.when(k == 0)
        def _():
            acc_ref[...] = jnp.zeros_like(acc_ref)

        acc_ref[...] += _dot_tn(a_ref[...], b_ref[...])

        @pl.when(k == nk - 1)
        def _():
            for d in range(N_DEV):
                lo, hi = W_IN_SHARD * d, W_IN_SHARD * (d + 1)
                for nat, pad, size in _w_in_row_runs():
                    a, b = max(nat, lo), min(nat + size, hi)
                    if a < b:
                        piece = acc_ref[pad + a - nat:pad + b - nat, :]
                        o_ref[d, a - lo:b - lo, :] = piece
                        ob_ref[d, a - lo:b - lo, :] = piece.astype(ob_ref.dtype)
                o_ref[d, W_IN_SHARD:W_IN_SLOT, :] = jnp.zeros((W_IN_SLOT - W_IN_SHARD, tn), F32)
                ob_ref[d, W_IN_SHARD:W_IN_SLOT, :] = jnp.zeros((W_IN_SLOT - W_IN_SHARD, tn), ob_ref.dtype)

    slots = pl.BlockSpec((N_DEV, W_IN_SLOT, tn), lambda j, k: (0, 0, j))
    return pl.pallas_call(
        body, name="dw_in", grid=(N // tn, nk),
        out_shape=[jax.ShapeDtypeStruct((N_DEV, W_IN_SLOT, N), F32), jax.ShapeDtypeStruct((N_DEV, W_IN_SLOT, N), jnp.bfloat16)],
        in_specs=[pl.BlockSpec((tk, M), lambda j, k: (k, 0)), pl.BlockSpec((tk, tn), lambda j, k: (k, j))],
        out_specs=[slots, slots],
        scratch_shapes=[pltpu.VMEM((M, tn), F32)],
        compiler_params=_cparams(("arbitrary", "arbitrary")),
    )(dz, hb)


SHARDED = (3, 6, 7, 9)
PART_SLICES = {10: (128, 1152), 2: (1152, 2176), 4: (2176, 2560), 5: (2560, 2816), 8: (2816, 2824)}


def _finalize_adamw(parts_all, dmod_all, dmod_cols, c_all, ws, sharded_grads, ms, vs):
    n = len(ws)
    local = [t for t in range(n) if t not in SHARDED[1:]]

    def body(*refs):
        p_ref, dm_ref, dmc_ref, c_ref = refs[:4]
        w_refs, gs_refs, m_refs, v_refs = refs[4:4 + n], refs[4 + n:8 + n], refs[8 + n:8 + 2 * n], refs[8 + 2 * n:8 + 3 * n]
        outs = refs[8 + 3 * n:]
        loss_ref, gl_refs = outs[0], outs[1:1 + len(local)]
        d_refs, nm_refs, nv_refs = (outs[1 + len(local) + t * n:1 + len(local) + (t + 1) * n] for t in range(3))
        ps_ref = outs[-1]
        acc = p_ref[0]
        for j in range(1, N_DEV):
            acc = acc + p_ref[j]
        ps_ref[...] = acc
        loss_ref[...] = jnp.sum(acc[:, 0:LANE], axis=1, keepdims=True)
        db = dm_ref[0:1, :]
        for j in range(1, dm_ref.shape[0]):
            db = db + dm_ref[j:j + 1, :]
        grads = {0: _dot_tn(_mx(_silu(c_ref[...])), _mx(dmc_ref[...])), 1: db}
        for t, (lo, hi) in PART_SLICES.items():
            grads[t] = ps_ref[:, lo:hi]
        for i, t in enumerate(SHARDED):
            grads[t] = gs_refs[i][...]
        grads[SHARDED[0]] = grads[SHARDED[0]].T[:, 0:W_IN_SHARD]
        for i, t in enumerate(local):
            gl_refs[i][...] = grads[t]
        for t in range(n):
            gv = grads[t]
            nm = ADAM_B1 * m_refs[t][...] + (1.0 - ADAM_B1) * gv
            nv = ADAM_B2 * v_refs[t][...] + (1.0 - ADAM_B2) * (gv * gv)
            m_hat = nm / (1.0 - ADAM_B1 ** ADAM_STEP)
            v_hat = nv / (1.0 - ADAM_B2 ** ADAM_STEP)
            d_refs[t][...] = -ADAM_LR * (m_hat / (jnp.sqrt(v_hat) + ADAM_EPS) + ADAM_WD * w_refs[t][...])
            nm_refs[t][...] = nm
            nv_refs[t][...] = nv

    like = lambda arrs: [jax.ShapeDtypeStruct(a.shape, F32) for a in arrs]
    out = pl.pallas_call(
        body, name="finalize_adamw",
        out_shape=[jax.ShapeDtypeStruct((1, 1), F32)] + like([ws[t] for t in local]) + like(ws) * 3,
        in_specs=[_vmem()] * (8 + 3 * n), out_specs=[_vmem()] * (1 + len(local) + 3 * n),
        scratch_shapes=[pltpu.VMEM(parts_all.shape[1:], F32)],
        compiler_params=_cparams(),
    )(parts_all, dmod_all, dmod_cols, c_all, *ws, *sharded_grads, *ms, *vs)
    k = 1 + len(local)
    return out[0], dict(zip(local, out[1:k])), out[k:k + n], out[k + n:k + 2 * n], out[k + 2 * n:]


def _pair_perm(a, axis, order):
    a = jnp.moveaxis(a, axis, -1)
    lead = a.shape[:-1]
    a = a.reshape(lead + (8, 64))[..., list(order), :].reshape(lead + (512,))
    return jnp.moveaxis(a, -1, axis)


def _rope_table(positions):
    T = positions.size
    inv = ROPE_THETA ** (-jnp.arange(0, MLA_ROPE, 2, dtype=F32) / MLA_ROPE)
    pos = jnp.repeat(positions.reshape(T // 8, 8)[:, ::-1].astype(F32), MLA_ROPE // 2, axis=1)
    ang = pos * jnp.tile(inv, 8)[None, :]
    return jnp.cos(ang), jnp.sin(ang)


def _local_step(x, mod, positions, ng, w_in_t, gq, gkv, w_uq, w_ukv, sinks, w_out, fg, tgt,
                ts=512, fq=512, fk=512, bq=512, bk=512):
    B, S, D = x.shape
    T = B * S
    x2 = x.reshape(T, D)
    shift, scale, gate = (mod[:, None, k * D:(k + 1) * D] for k in range(3))
    w_uq_p = jnp.pad(w_uq.reshape(Q_LORA, MLA_HEADS, 96), ((0, 0), (0, 0), (0, 32))).reshape(Q_LORA, MLA_HEADS * LANE)
    w_ukv3 = w_ukv.reshape(KV_LORA, MLA_HEADS, 128)
    w_uk_p = jnp.pad(w_ukv3[:, :, :64], ((0, 0), (0, 0), (0, 64))).reshape(KV_LORA, MLA_HEADS * LANE)
    w_uv = w_ukv3[:, :, 64:].reshape(KV_LORA, 512)
    w_out_p = jnp.concatenate([w_out[:512], _pair_perm(w_out[512:], 0, PAIR_ORDER)], axis=0)
    rope = _rope_table(positions)
    posf = positions.astype(F32)
    posb = jnp.broadcast_to(posf[:, :, None], (B, S, LANE))
    posr = posf.reshape(B, S // WINDOW, WINDOW)
    sinks_l = jnp.pad(sinks.reshape(1, SWA_HEADS), ((0, 0), (0, LANE - SWA_HEADS)))

    (hb, zq, zkv, ql, kvl, q, k, v, qs, ks, vs, g, w_in_p) = _pre_fwd(
        x2, shift, scale, ng, w_in_t, gq, gkv, w_uq_p, w_uk_p, w_uv, rope, S, ts)
    r3 = lambda a: a.reshape(B, S, a.shape[-1])
    om, lse_m = _mla_fwd(r3(q), r3(k), r3(v), fq, fk)
    osw, lse_s = _swa_fwd(r3(qs), r3(ks), r3(vs), posb, posr, sinks_l)
    dx2, do, dg, loss_v, dfg, dgate, dw_out = _post(
        om.reshape(T, 512), osw.reshape(T, 512), g, w_out_p, x2, gate, fg.reshape(1, D), tgt.reshape(T, D), S, ts)
    do3 = r3(do)
    dq, dk, dv = _mla_bwd(r3(q), r3(k), r3(v), om, do3, lse_m, bq, bk)
    dqs, dks, dvs, dsink = _swa_bwd(r3(qs), r3(ks), r3(vs), posb, posr, sinks_l, osw, do3, lse_s)
    f2 = lambda a: a.reshape(T, a.shape[-1])
    gx, dz, dgq, dgkv, dng, dsh, dsc, dw_uq_p, dw_uk_p, dw_uv = _pre_bwd(
        f2(dq), f2(dk), f2(dv), f2(dqs), f2(dks), f2(dvs), dg, zq, zkv, ql, kvl, x2, dx2, scale, ng, w_in_p, gq, gkv,
        w_uq_p, w_uk_p, w_uv, rope, S, ts)
    tk = min(T, 1024)
    dw_in_t = _dw_in_t(dz, hb, 512, tk)
    dw_uq = dw_uq_p.reshape(Q_LORA, MLA_HEADS, LANE)[:, :, :96].reshape(Q_LORA, 768)
    dw_uk = dw_uk_p.reshape(KV_LORA, MLA_HEADS, LANE)[:, :, :64]
    dw_uv = dw_uv.reshape(KV_LORA, MLA_HEADS, 64)
    dw_ukv = jnp.concatenate([dw_uk, dw_uv], axis=2).reshape(KV_LORA, 1024)
    parts = jnp.concatenate([loss_v, dfg, dng, dgq, dgkv, dsink], axis=1)
    dmod = jnp.concatenate([dsh, dsc, dgate], axis=2).reshape(B, 3 * D)
    return gx.reshape(B, S, D), dw_in_t, dw_uq, dw_ukv, dw_out, parts, dmod


def kernel(x, c, positions, w_ada, b_ada, norm_gain, w_in, q_norm_gain, kv_norm_gain, w_uq, w_ukv, swa_sinks, w_out, final_gain, loss_target, m_w_ada, m_b_ada, m_norm_gain, m_w_in, m_q_norm_gain, m_kv_norm_gain, m_w_uq, m_w_ukv, m_swa_sinks, m_w_out, m_final_gain, v_w_ada, v_b_ada, v_norm_gain, v_w_in, v_q_norm_gain, v_kv_norm_gain, v_w_uq, v_w_ukv, v_swa_sinks, v_w_out, v_final_gain):
    B, S, D = x.shape
    me = 4 * lax.axis_index("x") + 2 * lax.axis_index("y") + lax.axis_index("c")
    bf = _MXU_DTYPE

    ncol = w_ada.shape[2]
    b_cols = lax.dynamic_slice_in_dim(b_ada, me * ncol, ncol, axis=1)
    c_all, win_g, wuq_g, wukv_g, wout_g, mod_g = _all_gather(
        [c, w_in[0].T.astype(bf), w_uq[0].astype(bf), w_ukv[0].astype(bf), w_out[0].astype(bf)], "ag_weights",
        fused=((w_ada[0], b_cols), _ada_cols, jax.ShapeDtypeStruct((N_DEV, B, ncol), F32)))
    c_all = c_all.reshape(N_DEV * B, D)
    cat_cols = lambda a: jnp.transpose(a, (1, 0, 2)).reshape(a.shape[1], N_DEV * a.shape[2])
    w_in_t, w_uq_f, w_ukv_f = win_g, cat_cols(wuq_g), cat_cols(wukv_g)
    w_out_f = wout_g.reshape(D, D)

    mod = lax.dynamic_index_in_dim(mod_g, me, axis=1, keepdims=False)
    mod = jnp.transpose(mod, (1, 0, 2)).reshape(B, 3 * D)

    gx, dw_in_t, dw_uq, dw_ukv, dw_out, parts, dmod = _local_step(
        x, mod, positions, norm_gain, w_in_t, q_norm_gain, kv_norm_gain, w_uq_f, w_ukv_f, swa_sinks,
        w_out_f, final_gain, loss_target)

    split_cols = lambda a: jnp.transpose(a.reshape(a.shape[0], 4, 2, a.shape[1] // N_DEV), (1, 2, 0, 3))
    split_rows = lambda a: a.reshape(4, 2, a.shape[0] // N_DEV, a.shape[1])
    slots = lambda a: a.reshape((4, 2) + a.shape[1:])
    g_w_in_t, g_w_uq, g_w_ukv, g_w_out, parts_g, dmod_g = _reduce_scatter(
        [(slots(dw_in_t[0]), slots(dw_in_t[1])), split_cols(dw_uq), split_cols(dw_ukv), split_rows(dw_out)], "rs_grads",
        gather=(parts, dmod))

    dmod_all = dmod_g.reshape(N_DEV * B, 3 * D)
    dmod_cols = lax.dynamic_slice_in_dim(dmod_all, me * ncol, ncol, axis=1)
    ws = [w_ada, b_ada, norm_gain, w_in, q_norm_gain, kv_norm_gain, w_uq, w_ukv, swa_sinks, w_out, final_gain]
    ms = [m_w_ada, m_b_ada, m_norm_gain, m_w_in, m_q_norm_gain, m_kv_norm_gain, m_w_uq, m_w_ukv, m_swa_sinks, m_w_out,
          m_final_gain]
    vs = [v_w_ada, v_b_ada, v_norm_gain, v_w_in, v_q_norm_gain, v_kv_norm_gain, v_w_uq, v_w_ukv, v_swa_sinks, v_w_out,
          v_final_gain]
    two_d = [(1, w.shape[0]) if w.ndim == 1 else w.shape[-2:] for w in ws]
    flat = lambda arrs: [a.reshape(s) for a, s in zip(arrs, two_d)]
    sharded = {3: g_w_in_t, 6: g_w_uq, 7: g_w_ukv, 9: g_w_out}
    loss, local_grads, deltas, new_ms, new_vs = _finalize_adamw(
        parts_g, dmod_all, dmod_cols, c_all, flat(ws), [sharded[t] for t in SHARDED], flat(ms), flat(vs))
    grads = [local_grads[t] if t in local_grads else sharded[t] for t in range(len(ws))]
    shaped = lambda arrs: [a.reshape(w.shape) for a, w in zip(arrs, ws)]
    return (loss.reshape(()), gx, *shaped(grads), *shaped(deltas), *shaped(new_ms), *shaped(new_vs))
```

```python
import functools

import jax
import jax.numpy as jnp
from jax import lax
from jax.experimental import pallas as pl
from jax.experimental.pallas import tpu as pltpu

F32 = jnp.float32
_MXU_DTYPE = jnp.bfloat16
MLA_GRAD_DTYPE = jnp.bfloat16

N_DEV = 8
MLA_HEADS = 8
MLA_NOPE = 64
MLA_ROPE = 32
Q_LORA = 384
KV_LORA = 256
SWA_HEADS = 8
SWA_KV_HEADS = 2
SWA_HEAD_DIM = 64
WINDOW = 128
ROPE_THETA = 10000.0
EPS = 1e-6
MLA_SCALE = float((MLA_NOPE + MLA_ROPE) ** -0.5)
SWA_SCALE = float(SWA_HEAD_DIM ** -0.5)
LOG2E = 1.4426950408889634
MLA_QSCALE = MLA_SCALE * LOG2E
D_IN = 2464
D_IN_PAD = 2560
PAIR_ORDER = (0, 4, 1, 5, 2, 6, 3, 7)
PAIR_INV = (0, 2, 4, 6, 1, 3, 5, 7)

ADAM_LR = 0.001
ADAM_B1 = 0.9
ADAM_B2 = 0.999
ADAM_EPS = 1e-08
ADAM_WD = 0.01
ADAM_STEP = 10

LANE = 128
VMEM_LIMIT = 56 * 1024 * 1024

MESH = pl.DeviceIdType.MESH
NEG_INF = float("-inf")
SWA_SEQ_SPLIT = 2


def _mx(a):
    return a.astype(_MXU_DTYPE)


def _dot(a, b):
    return jnp.dot(a, b, preferred_element_type=F32)


def _dot_nt(a, b):
    return lax.dot_general(a, b, (((1,), (1,)), ((), ())), preferred_element_type=F32)


def _dot_tn(a, b):
    return lax.dot_general(a, b, (((0,), (0,)), ((), ())), preferred_element_type=F32)


def _cparams(sem=None):
    return pltpu.CompilerParams(dimension_semantics=sem, vmem_limit_bytes=VMEM_LIMIT)


def _vmem():
    return pl.BlockSpec(memory_space=pltpu.VMEM)


def _lane_iota(shape):
    return lax.broadcasted_iota(jnp.int32, shape, len(shape) - 1)


def _gather_program(srcs, outs, send_sems, recv_sems, local_sems):
    x, y, c = lax.axis_index("x"), lax.axis_index("y"), lax.axis_index("c")
    me, sibling = (x, y, c), (x, y, 1 - c)
    chips = [(1 - x, y), (x, 1 - y), (1 - x, 1 - y)]

    def slot(a, dev):
        return outs[a].at[4 * dev[0] + 2 * dev[1] + dev[2]]

    def copy(a, k, block, to, src=None):
        return pltpu.make_async_remote_copy(
            src_ref=slot(a, block) if src is None else src, dst_ref=slot(a, block),
            send_sem=send_sems.at[7 * a + k], recv_sem=recv_sems.at[7 * a + k],
            device_id=to, device_id_type=MESH)

    def local(a):
        return pltpu.make_async_copy(srcs[a], slot(a, me), local_sems.at[a])

    def start(a):
        local(a).start()
        cps = [copy(a, 0, me, sibling, src=srcs[a])]
        cps += [copy(a, 1 + j, me, (*chip, c), src=srcs[a]) for j, chip in enumerate(chips)]
        for cp in cps:
            cp.start()
        return cps

    def finish(group):
        cps = []
        for j, chip in enumerate(chips):
            for a in group:
                copy(a, 1 + j, (*chip, c), me).wait_recv()
                cp = copy(a, 4 + j, (*chip, c), sibling)
                cp.start()
                cps.append(cp)
        for a in group:
            copy(a, 0, sibling, me).wait_recv()
            for j, chip in enumerate(chips):
                copy(a, 4 + j, (*chip, 1 - c), me).wait_recv()
            local(a).wait()
        return cps

    return start, finish


def _gather_sems(m):
    return [pltpu.SemaphoreType.DMA((7 * m,)), pltpu.SemaphoreType.DMA((7 * m,)), pltpu.SemaphoreType.DMA((m,))]


def _all_gather(arrs, name, fused=None):
    n = len(arrs)
    extra, fn, piece = fused if fused else ((), None, None)
    ne, m = len(extra), n + (1 if fused else 0)

    def body(*refs):
        ins, ex, outs = refs[:n], refs[n:n + ne], refs[n + ne:n + ne + m]
        rest = refs[n + ne + m:]
        srcs = list(ins) + ([rest[0]] if fused else [])
        start, finish = _gather_program(srcs, outs, *rest[-3:])
        pending = []
        for a in range(n):
            pending += start(a)
        if fused:
            pending += finish([0])
            fn(srcs[n], outs[0], *ex)
            pending += start(n)
            pending += finish([n] + list(range(1, n)))
        else:
            pending += finish(list(range(n)))
        for cp in pending:
            cp.wait_send()

    out_shape = [jax.ShapeDtypeStruct((N_DEV,) + a.shape, a.dtype) for a in arrs]
    scratch = []
    if fused:
        out_shape.append(jax.ShapeDtypeStruct((N_DEV,) + piece.shape, piece.dtype))
        scratch.append(pltpu.VMEM(piece.shape, piece.dtype))
    return pl.pallas_call(
        body, name=name, out_shape=out_shape,
        in_specs=[_vmem()] * (n + ne), out_specs=[_vmem()] * m,
        scratch_shapes=scratch + _gather_sems(m),
        compiler_params=pltpu.CompilerParams(vmem_limit_bytes=VMEM_LIMIT),
    )(*arrs, *extra)


def _reduce_scatter(arrs, name, gather=()):
    n, g = len(arrs), len(gather)
    halves = [a[1] if isinstance(a, tuple) else a.astype(jnp.bfloat16) for a in arrs]
    arrs = [a[0] if isinstance(a, tuple) else a for a in arrs]

    def body(*refs):
        xs, xbs, gins = refs[:n], refs[n:2 * n], refs[2 * n:2 * n + g]
        outs, gouts = refs[2 * n + g:3 * n + g], refs[3 * n + g:3 * n + 2 * g]
        rest = refs[3 * n + 2 * g:]
        parts, recv_a, send_b, recv_b = (rest[t * n:(t + 1) * n] for t in range(4))
        send_sems, recv_sems, local_sems = rest[4 * n:4 * n + 3]
        x, y, c = lax.axis_index("x"), lax.axis_index("y"), lax.axis_index("c")
        myq = 2 * x + y
        gather_start, gather_finish = _gather_program(gins, gouts, *rest[4 * n + 3:])

        def chip(k):
            return (1 - x if k & 2 else x, 1 - y if k & 1 else y)

        def from_sibling(a):
            return pltpu.make_async_remote_copy(
                src_ref=xbs[a].at[:, 1 - c], dst_ref=recv_a[a], send_sem=send_sems.at[4 * a], recv_sem=recv_sems.at[4 * a],
                device_id=(x, y, 1 - c), device_id_type=MESH)

        def to_owner(a, k):
            qx, qy = chip(k)
            return pltpu.make_async_remote_copy(
                src_ref=send_b[a].at[2 * qx + qy], dst_ref=recv_b[a].at[myq],
                send_sem=send_sems.at[4 * a + k], recv_sem=recv_sems.at[4 * a + k],
                device_id=(qx, qy, c), device_id_type=MESH)

        mine = [pltpu.make_async_copy(xs[a].at[:, c], parts[a], local_sems.at[a]) for a in range(n)]
        first = [from_sibling(a) for a in range(n)]
        for cp in mine + first:
            cp.start()
        second = []
        for b in range(g):
            second += gather_start(b)
        for a in range(n):
            mine[a].wait()
            first[a].wait_recv()
            parts[a][...] = parts[a][...] + recv_a[a][...].astype(F32)
            send_b[a][...] = parts[a][...].astype(jnp.bfloat16)
            for k in range(1, 4):
                cp = to_owner(a, k)
                cp.start()
                second.append(cp)
        second += gather_finish(list(range(g)))
        for a in range(n):
            acc = parts[a][myq]
            for k in range(1, 4):
                to_owner(a, k).wait_recv()
                qx, qy = chip(k)
                acc = acc + recv_b[a][2 * qx + qy].astype(F32)
            outs[a][...] = acc
        for cp in first + second:
            cp.wait_send()

    quarter = lambda a, dt: pltpu.VMEM((4,) + a.shape[2:], dt)
    return pl.pallas_call(
        body, name=name,
        out_shape=[jax.ShapeDtypeStruct(a.shape[2:], F32) for a in arrs]
        + [jax.ShapeDtypeStruct((N_DEV,) + a.shape, a.dtype) for a in gather],
        in_specs=[pl.BlockSpec(memory_space=pl.ANY)] * (2 * n) + [_vmem()] * g, out_specs=[_vmem()] * (n + g),
        scratch_shapes=[quarter(a, F32) for a in arrs] + [quarter(a, jnp.bfloat16) for a in arrs] * 3
        + [pltpu.SemaphoreType.DMA((4 * n,)), pltpu.SemaphoreType.DMA((4 * n,)), pltpu.SemaphoreType.DMA((n,))]
        + _gather_sems(g),
        compiler_params=pltpu.CompilerParams(vmem_limit_bytes=VMEM_LIMIT),
    )(*arrs, *halves, *gather)


def _silu(t):
    return t * (1.0 / (1.0 + jnp.exp(-t)))


def _ada_cols(piece_ref, c_all_ref, w_ref, b_ref):
    w = _mx(w_ref[...])
    for d in range(N_DEV):
        piece_ref[d] = _dot(_mx(_silu(c_all_ref[d])), w) + b_ref[...]


def _rope_split(cd, sd):
    n = cd.shape[0]

    def expand(d):
        rep = jnp.broadcast_to(d[:, None, :], (n, 8, LANE)).reshape(8 * n, LANE)
        return pltpu.roll(rep, 0, 1, stride=16, stride_axis=0)

    c, s = expand(cd), expand(sd)
    lane = _lane_iota(c.shape)
    first = jnp.logical_and(lane >= 64, lane < 80)
    second = jnp.logical_and(lane >= 80, lane < 96)
    ck = jnp.where(first, pltpu.roll(c, 80, 1), jnp.where(second, pltpu.roll(c, 96, 1), 0.0))
    cq = jnp.where(lane < 64, 1.0, ck)
    sa = jnp.where(first, -pltpu.roll(s, 80, 1), 0.0)
    sb = jnp.where(second, pltpu.roll(s, 96, 1), 0.0)
    return cq, ck, sa, sb


def _tile_heads(t):
    return jnp.concatenate([t] * MLA_HEADS, axis=1)


def _pre_fwd(x2, shift, scale, ng, w_in, gq, gkv, w_uq, w_uk, w_uv, rope, S, ts):
    T, D = x2.shape
    nsb = S // ts
    WQ = MLA_HEADS * LANE

    def body(x_ref, sh_ref, sc_ref, ng_ref, wn_ref, gq_ref, gkv_ref, wuq_ref, wuk_ref, wuv_ref,
             cd_ref, sd_ref,
             hb_ref, zq_ref, zkv_ref, ql_ref, kvl_ref, q_ref, k_ref, v_ref, qs_ref, ks_ref, vs_ref, g_ref, wp_ref):
        @pl.when(pl.program_id(0) == 0)
        def _():
            wp_ref[640:704, :] = jnp.zeros((64, D), wp_ref.dtype)
            wp_ref[736:768, :] = jnp.zeros((32, D), wp_ref.dtype)
            for d in range(N_DEV):
                lo, hi = W_IN_SHARD * d, W_IN_SHARD * (d + 1)
                for nat, pad, size in _w_in_row_runs():
                    a, b = max(nat, lo), min(nat + size, hi)
                    if a < b:
                        wp_ref[pad + a - nat:pad + b - nat, :] = wn_ref[d, a - lo:b - lo, :]

        w_ref = wp_ref
        x = x_ref[...]
        r1 = lax.rsqrt(jnp.mean(x * x, axis=-1, keepdims=True) + EPS)
        h = ((x * r1) * ng_ref[...]) * (1.0 + sc_ref[0]) + sh_ref[0]
        hb = _mx(h)
        hb_ref[...] = hb
        zq = _dot_nt(hb, w_ref[0:384, :])
        zq_ref[...] = zq
        rq = lax.rsqrt(jnp.mean(zq * zq, axis=-1, keepdims=True) + EPS)
        ql = _mx((zq * rq) * gq_ref[...])
        ql_ref[...] = ql
        q = _dot(ql, wuq_ref[...])
        cq, ck, sa, sb = _rope_split(cd_ref[...], sd_ref[...])
        q = (q * _tile_heads(cq) + pltpu.roll(q, WQ - 16, 1) * _tile_heads(sa)
             + pltpu.roll(q, 16, 1) * _tile_heads(sb))
        q_ref[...] = _mx(q * MLA_QSCALE)
        zkv = _dot_nt(hb, w_ref[384:640, :])
        zkv_ref[...] = zkv
        rkv = lax.rsqrt(jnp.mean(zkv * zkv, axis=-1, keepdims=True) + EPS)
        kvl = _mx((zkv * rkv) * gkv_ref[...])
        kvl_ref[...] = kvl
        kr = _dot_nt(hb, w_ref[640:768, :])
        kpe = kr * ck + pltpu.roll(kr, LANE - 16, 1) * sa + pltpu.roll(kr, 16, 1) * sb
        kf = _dot(kvl, wuk_ref[...])
        k_ref[...] = _mx(kf + jnp.concatenate([kpe] * MLA_HEADS, axis=1))
        v_ref[...] = _mx(_dot(kvl, wuv_ref[...]))
        g_ref[:, 0:512] = _dot_nt(hb, w_ref[768:1280, :])
        qs_ref[...] = _mx(_dot_nt(hb, w_ref[1280:1792, :]) * (SWA_SCALE * LOG2E))
        ks_ref[...] = _mx(_dot_nt(hb, w_ref[1792:1920, :]))
        vs_ref[...] = _mx(_dot_nt(hb, w_ref[1920:2048, :]))
        g_ref[:, 512:1024] = _dot_nt(hb, w_ref[2048:2560, :])

    row = lambda w: pl.BlockSpec((ts, w), lambda i: (i, 0))
    dense = pl.BlockSpec((ts // 8, LANE), lambda i: (i, 0))
    full = lambda a: pl.BlockSpec(a.shape, lambda i: (0,) * a.ndim)
    per_b = pl.BlockSpec((1, 1, D), lambda i: (i // nsb, 0, 0))
    out_w = [(D, _MXU_DTYPE), (384, F32), (256, F32), (384, _MXU_DTYPE), (256, _MXU_DTYPE), (WQ, _MXU_DTYPE),
             (WQ, _MXU_DTYPE), (512, _MXU_DTYPE), (512, _MXU_DTYPE), (128, _MXU_DTYPE), (128, _MXU_DTYPE), (1024, F32)]
    return pl.pallas_call(
        body, name="pre_fwd", grid=(T // ts,),
        out_shape=[jax.ShapeDtypeStruct((T, w), dt) for w, dt in out_w] + [jax.ShapeDtypeStruct((D_IN_PAD, D), w_in.dtype)],
        in_specs=[row(D), per_b, per_b, full(ng), full(w_in), full(gq), full(gkv), full(w_uq), full(w_uk), full(w_uv),
                  dense, dense],
        out_specs=[row(w) for w, _ in out_w] + [pl.BlockSpec((D_IN_PAD, D), lambda i: (0, 0))],
        compiler_params=_cparams(("arbitrary",)),
    )(x2, shift, scale, ng, w_in, gq, gkv, w_uq, w_uk, w_uv, *rope)


def _mla_fwd(q3, k3, v3, tq, tk):
    B, S, _ = q3.shape
    nq = S // tq
    assert tq == tk
    HPS = 2
    NH = 2 * HPS

    def body(q_ref, k_ref, v_ref, o_ref, lse_ref):
        rows = lax.broadcasted_iota(jnp.int32, (tq, tk), 0)
        cols = lax.broadcasted_iota(jnp.int32, (tq, tk), 1)
        low_k = _lane_iota((tk, LANE)) < 64
        low = _lane_iota((tq, LANE)) < 64

        def step(qs, kt, carry, masked):
            r0 = kt * tk
            out = []
            for j in range(NH):
                v2 = v_ref[0, r0:r0 + tk, LANE * (j // 2):LANE * (j // 2 + 1)]
                vj = (jnp.where(low_k, v2, 1.0) if j % 2 == 0 else jnp.where(low_k, 1.0, v2)).astype(v2.dtype)
                m, acc = carry[j]
                s = _dot_nt(qs[j], k_ref[0, r0:r0 + tk, LANE * j:LANE * (j + 1)])
                if masked:
                    s = jnp.where(rows >= cols, s, NEG_INF)
                m_new = jnp.maximum(m, jnp.max(s, axis=1, keepdims=True))
                alpha = jnp.exp2(m - m_new)
                p = jnp.exp2(s - m_new)
                acc = alpha * acc + _dot(_mx(p), vj)
                out.append((m_new, acc))
            return tuple(out)

        for qi in range(nq):
            r = slice(qi * tq, (qi + 1) * tq)
            qs = [q_ref[0, r, LANE * j:LANE * (j + 1)] for j in range(NH)]
            init = (jnp.full((tq, 1), NEG_INF, F32), jnp.zeros((tq, LANE), F32))
            carry = (init,) * NH
            for kt in range(qi):
                carry = step(qs, kt, carry, False)
            carry = step(qs, qi, carry, True)
            for t in range(HPS):
                (m0, a0), (m1, a1) = carry[2 * t], carry[2 * t + 1]
                l0 = jnp.where(low, pltpu.roll(a0, 64, 1), a0)
                l1 = jnp.where(low, a1, pltpu.roll(a1, 64, 1))
                o_ref[0, r, LANE * t:LANE * (t + 1)] = jnp.where(low, a0 / l0, a1 / l1)
                lse_ref[0, 2 * t, r, :] = m0 + jnp.log2(l0)
                lse_ref[0, 2 * t + 1, r, :] = m1 + jnp.log2(l1)

    grp = lambda w: pl.BlockSpec((1, S, w), lambda b, hp: (b, 0, hp))
    return pl.pallas_call(
        body, name="mla_fwd", grid=(B, MLA_HEADS // NH),
        out_shape=[jax.ShapeDtypeStruct((B, S, 512), F32), jax.ShapeDtypeStruct((B, MLA_HEADS, S, LANE), F32)],
        in_specs=[grp(NH * LANE), grp(NH * LANE), grp(HPS * LANE)],
        out_specs=[grp(HPS * LANE), pl.BlockSpec((1, NH, S, LANE), lambda b, hp: (b, hp, 0, 0))],
        compiler_params=_cparams(("arbitrary", "arbitrary")),
    )(q3, k3, v3)


def _mla_bwd(q3, k3, v3, o3, do3, lse, tq, tk, hps=2):
    B, S, _ = q3.shape
    nq, nk = S // tq, S // tk
    nh = 2 * hps
    assert tk % tq == 0

    def body(q_ref, k_ref, v_ref, o_ref, do_ref, lse_ref, dq_ref, dk_ref, dv_ref, dkt_ref, dvt_ref):
        rows = lax.broadcasted_iota(jnp.int32, (tq, tk), 0)
        cols = lax.broadcasted_iota(jnp.int32, (tq, tk), 1)
        lane = _lane_iota((tq, LANE))

        def q_tile(qi, _):
            r = pl.ds(pl.multiple_of(qi * tq, tq), tq)
            heads = []
            for j in range(nh):
                lanes = slice(LANE * j, LANE * (j + 1))
                pair = slice(LANE * (j // 2), LANE * (j // 2 + 1))
                do = jnp.where((lane < 64) if j % 2 == 0 else (lane >= 64), do_ref[0, r, pair], 0.0)
                q = q_ref[0, r, lanes]
                heads.append((lanes, pair, q, _mx(q.astype(F32).T), _mx(do), _mx(do.T),
                              jnp.sum(do * o_ref[0, r, pair], axis=1, keepdims=True),
                              jnp.concatenate([lse_ref[0, j, r, :]] * (tk // LANE), axis=1)))
            n_full = (qi * tq) // tk

            def k_tile(kt, dqs, masked):
                kr = pl.ds(pl.multiple_of(kt * tk, tk), tk)
                first = masked and (qi * tq) % tk == 0
                out = []
                dvt = [None] * hps
                for j, (lanes, pair, q, qt, dob, dot_, dcol, lse_c) in enumerate(heads):
                    k = k_ref[0, kr, lanes]
                    s = _dot_nt(q, k)
                    if masked:
                        s = jnp.where(rows + qi * tq >= cols + kt * tk, s, NEG_INF)
                    p = jnp.exp2(s - lse_c)
                    dp = _dot_nt(dob, v_ref[0, kr, pair])
                    dsb = _mx(p * (dp - dcol))
                    if first:
                        dkt_ref[j, kt] = _dot(qt, dsb)
                    else:
                        dkt_ref[j, kt] += _dot(qt, dsb)
                    pv = _dot(dot_, _mx(p))
                    dvt[j // 2] = pv if dvt[j // 2] is None else dvt[j // 2] + pv
                    out.append(dqs[j] + _dot(dsb, k))
                for t in range(hps):
                    if first:
                        dvt_ref[t, kt] = dvt[t]
                    else:
                        dvt_ref[t, kt] += dvt[t]
                return tuple(out)

            dqs = (jnp.zeros((tq, LANE), F32),) * nh
            for kt in range(n_full):
                dqs = k_tile(kt, dqs, False)
            dqs = k_tile(n_full, dqs, True)
            for j in range(nh):
                dq_ref[0, r, heads[j][0]] = (MLA_SCALE * dqs[j]).astype(dq_ref.dtype)
            return 0

        for qi in range(nq):
            q_tile(qi, 0)

        def flush(kt, _):
            kr = pl.ds(pl.multiple_of(kt * tk, tk), tk)
            for j in range(nh):
                dk_ref[0, kr, LANE * j:LANE * (j + 1)] = ((1.0 / LOG2E) * dkt_ref[j, kt].T).astype(dk_ref.dtype)
            for t in range(hps):
                dv_ref[0, kr, LANE * t:LANE * (t + 1)] = dvt_ref[t, kt].T.astype(dv_ref.dtype)
            return 0

        lax.fori_loop(0, nk, flush, 0)

    grp = lambda w: pl.BlockSpec((1, S, w), lambda b, hp: (b, 0, hp))
    return pl.pallas_call(
        body, name="mla_bwd", grid=(B, MLA_HEADS // nh),
        out_shape=[jax.ShapeDtypeStruct((B, S, 1024), MLA_GRAD_DTYPE), jax.ShapeDtypeStruct((B, S, 1024), MLA_GRAD_DTYPE),
                   jax.ShapeDtypeStruct((B, S, 512), MLA_GRAD_DTYPE)],
        in_specs=[grp(nh * LANE), grp(nh * LANE), grp(hps * LANE), grp(hps * LANE), grp(hps * LANE),
                  pl.BlockSpec((1, nh, S, LANE), lambda b, hp: (b, hp, 0, 0))],
        out_specs=[grp(nh * LANE), grp(nh * LANE), grp(hps * LANE)],
        scratch_shapes=[pltpu.VMEM((nh, nk, LANE, tk), F32), pltpu.VMEM((hps, nk, LANE, tk), F32)],
        compiler_params=_cparams(("arbitrary", "arbitrary")),
    )(q3, k3, v3, o3, do3, lse)


def _swa_consts(sink_ref):
    W = WINDOW
    row = lax.broadcasted_iota(jnp.int32, (4 * W, LANE), 0)
    out = []
    for g in range(SWA_KV_HEADS):
        slope = jnp.zeros((4 * W, LANE), F32)
        sink = jnp.zeros((4 * W, LANE), F32)
        for p in range(4):
            h = p + 4 * g
            here = jnp.logical_and(row >= W * p, row < W * (p + 1))
            slope = jnp.where(here, float(LOG2E * 2.0 ** (-8.0 * (h + 1) / SWA_HEADS)), slope)
            sink = jnp.where(here, LOG2E * sink_ref[:, h:h + 1], sink)
        out.append((jnp.concatenate([slope, slope], axis=1), sink))
    return out


def _wide(col):
    return jnp.concatenate([col, col], axis=1)


def _swa_block(q_ref, k_ref, v_ref, pb_ref, pr_ref, n, i, ij):
    W = WINDOW
    kb = jnp.maximum(n - 1, 0)
    r = pl.ds(pl.multiple_of(i * W, W), W)
    kr = pl.ds(pl.multiple_of(kb * W, W), 2 * W)
    q4, k2, v2 = q_ref[0, r, :], k_ref[0, kr, :], v_ref[0, kr, :]
    pq = _wide(pb_ref[0, r, :])
    pk = jnp.concatenate([pr_ref[0, pl.ds(kb, 1), :], pr_ref[0, pl.ds(kb + 1, 1), :]], axis=1)
    rel = ij + (n - kb) * W
    dist = jnp.where(jnp.logical_and(rel >= 0, rel < W), pq - pk, float("inf"))
    return r, kb, kr, q4, k2, v2, jnp.concatenate([dist] * 4, axis=0)


def _swa_stack(x4, g, dtype):
    lane = _lane_iota((WINDOW, LANE))
    mine = (lane < 64) if g == 0 else (lane >= 64)
    return jnp.concatenate([jnp.where(mine, x4[:, LANE * p:LANE * (p + 1)], 0).astype(dtype) for p in range(4)], axis=0)


def _swa_unstack(ref, r, lo, hi, scale=None):
    W = WINDOW
    low = _lane_iota((W, LANE)) < 64
    for p in range(4):
        t = jnp.where(low, lo[W * p:W * (p + 1)], hi[W * p:W * (p + 1)])
        ref[0, r, LANE * p:LANE * (p + 1)] = t if scale is None else scale * t


def _swa_fwd(qs3, ks3, vs3, posb, posr, sinks):
    B, S, _ = qs3.shape
    W = WINDOW
    nb = S // W
    nh = SWA_SEQ_SPLIT
    nbh = nb // nh

    def body(q_ref, k_ref, v_ref, pb_ref, pr_ref, sink_ref, o_ref, lse_ref):
        ij = lax.broadcasted_iota(jnp.int32, (W, 2 * W), 0) - lax.broadcasted_iota(jnp.int32, (W, 2 * W), 1)
        consts = _swa_consts(sink_ref)
        n0 = pl.program_id(1) * nbh

        def blk(i, _):
            n = n0 + i
            r, _, _, q4, k2, v2, dist4 = _swa_block(q_ref, k_ref, v_ref, pb_ref, pr_ref, n, i, ij)
            o_g = []
            for g, (slope, sink) in enumerate(consts):
                s = _dot_nt(_swa_stack(q4, g, q4.dtype), k2) - slope * dist4
                m = jnp.maximum(jnp.max(s, axis=1, keepdims=True), sink)
                e = jnp.exp2(s - _wide(m))
                l = jnp.sum(e, axis=1, keepdims=True) + jnp.exp2(sink - m)
                o_g.append(_dot(_mx(e), v2) * (1.0 / l))
                lse_ref[0, i, g] = m + jnp.log2(l)
            _swa_unstack(o_ref, r, o_g[0], o_g[1])
            return 0

        for i in range(nbh):
            blk(i, 0)

    seq = lambda w: pl.BlockSpec((1, S, w), lambda b, h: (b, 0, 0))
    part = lambda w: pl.BlockSpec((1, S // nh, w), lambda b, h: (b, h, 0))
    lse_spec = pl.BlockSpec((1, nbh, 2, 4 * W, LANE), lambda b, h: (b, h, 0, 0, 0))
    return pl.pallas_call(
        body, name="swa_fwd", grid=(B, nh),
        out_shape=[jax.ShapeDtypeStruct((B, S, 512), F32), jax.ShapeDtypeStruct((B, nb, 2, 4 * W, LANE), F32)],
        in_specs=[part(512), seq(LANE), seq(LANE), part(LANE), pl.BlockSpec((1, nb, W), lambda b, h: (b, 0, 0)),
                  pl.BlockSpec((1, LANE), lambda b, h: (0, 0))],
        out_specs=[part(512), lse_spec],
        compiler_params=_cparams(("arbitrary", "arbitrary")),
    )(qs3, ks3, vs3, posb, posr, sinks)


def _swa_bwd(qs3, ks3, vs3, posb, posr, sinks, os3, do3, lse):
    B, S, _ = qs3.shape
    W = WINDOW
    nb = S // W
    nh = SWA_SEQ_SPLIT
    nbh = nb // nh
    assert nbh % 2 == 0 and nbh * nh * W == S

    def body(q_ref, k_ref, v_ref, pb_ref, pr_ref, sink_ref, o_ref, do_ref, lse_ref, dq_ref, dk_ref, dv_ref, dsink_ref,
             dkt_ref, dvt_ref):
        lane1 = _lane_iota((1, LANE))
        ij = lax.broadcasted_iota(jnp.int32, (W, 2 * W), 0) - lax.broadcasted_iota(jnp.int32, (W, 2 * W), 1)
        consts = _swa_consts(sink_ref)
        hh = pl.program_id(1)
        n0 = hh * nbh

        @pl.when(hh == 0)
        def _():
            dkt_ref[...] = jnp.zeros_like(dkt_ref)
            dvt_ref[...] = jnp.zeros_like(dvt_ref)

        @pl.when(jnp.logical_and(pl.program_id(0) == 0, hh == 0))
        def _():
            dsink_ref[...] = jnp.zeros_like(dsink_ref)

        def blk(i, dsink):
            n = n0 + i
            r, kb, _, q4, k2, v2, dist4 = _swa_block(q_ref, k_ref, v_ref, pb_ref, pr_ref, n, i, ij)
            o4, do4 = o_ref[0, r, :], do_ref[0, r, :]
            dq_g = []
            dkt = jnp.zeros((LANE, 2 * W), F32)
            dvt = jnp.zeros((LANE, 2 * W), F32)
            for g, (slope, sink) in enumerate(consts):
                q_st = _swa_stack(q4, g, F32)
                do_st = _swa_stack(do4, g, F32)
                dcol = jnp.sum(do_st * _swa_stack(o4, g, F32), axis=1, keepdims=True)
                qst, dob = _mx(q_st), _mx(do_st)
                lse_c = lse_ref[0, i, g]
                pr = jnp.exp2(_dot_nt(qst, k2) - slope * dist4 - _wide(lse_c))
                dsb = _mx(pr * (_dot_nt(dob, v2) - dcol))
                psd = jnp.exp2(sink - lse_c)[:, 0:1] * dcol
                for p in range(4):
                    dsink = dsink - jnp.where(lane1 == p + 4 * g,
                                              jnp.sum(psd[W * p:W * (p + 1)], axis=0, keepdims=True), 0.0)
                dq_g.append(_dot(dsb, k2))
                dkt = dkt + _dot(_mx(q_st.T), dsb)
                dvt = dvt + _dot(_mx(do_st.T), _mx(pr))
            _swa_unstack(dq_ref, r, dq_g[0], dq_g[1], SWA_SCALE)
            dkt_ref[kb] += dkt[:, 0:W]
            dkt_ref[kb + 1] += dkt[:, W:2 * W]
            dvt_ref[kb] += dvt[:, 0:W]
            dvt_ref[kb + 1] += dvt[:, W:2 * W]
            return dsink

        dsink = jnp.zeros((1, LANE), F32)
        for i in range(nbh):
            dsink = blk(i, dsink)
        dsink_ref[...] += dsink

        @pl.when(hh == nh - 1)
        def _():
            def flush(n, _):
                r = pl.ds(pl.multiple_of(n * W, W), W)
                dk_ref[0, r, :] = (1.0 / LOG2E) * dkt_ref[n].T
                dv_ref[0, r, :] = dvt_ref[n].T
                return 0

            lax.fori_loop(0, nb, flush, 0)

    seq = lambda w: pl.BlockSpec((1, S, w), lambda b, h: (b, 0, 0))
    part = lambda w: pl.BlockSpec((1, S // nh, w), lambda b, h: (b, h, 0))
    return pl.pallas_call(
        body, name="swa_bwd", grid=(B, nh),
        out_shape=[jax.ShapeDtypeStruct((B, S, 512), F32), jax.ShapeDtypeStruct((B, S, LANE), F32),
                   jax.ShapeDtypeStruct((B, S, LANE), F32), jax.ShapeDtypeStruct((1, LANE), F32)],
        in_specs=[part(512), seq(LANE), seq(LANE), part(LANE), pl.BlockSpec((1, nb, W), lambda b, h: (b, 0, 0)),
                  pl.BlockSpec((1, LANE), lambda b, h: (0, 0)), part(512),
                  pl.BlockSpec((1, S // nh, 512), lambda b, h: (b, h, 1)),
                  pl.BlockSpec((1, nbh, 2, 4 * W, LANE), lambda b, h: (b, h, 0, 0, 0))],
        out_specs=[part(512), seq(LANE), seq(LANE), pl.BlockSpec((1, LANE), lambda b, h: (0, 0))],
        scratch_shapes=[pltpu.VMEM((nb, LANE, W), F32), pltpu.VMEM((nb, LANE, W), F32)],
        compiler_params=_cparams(("arbitrary", "arbitrary")),
    )(qs3, ks3, vs3, posb, posr, sinks, os3, do3, lse)


def _post(om, osw, g, w_out, x2, gate, fg, tgt, S, ts):
    T, D = x2.shape
    nsb = S // ts

    def body(om_ref, os_ref, g_ref, w_ref, x_ref, gate_ref, fg_ref, t_ref,
             dx2_ref, do_ref, dg_ref, loss_ref, dfg_ref, dgate_ref, dw_ref):
        i = pl.program_id(0)
        gv = g_ref[...]
        sg = 1.0 / (1.0 + jnp.exp(-gv))
        silu = gv * sg
        o = jnp.concatenate([om_ref[...], os_ref[...]], axis=1)
        ab = _mx(o * silu)
        y = _dot(ab, w_ref[...])
        gate = gate_ref[0]
        xo = x_ref[...] + gate * y
        r2 = lax.rsqrt(jnp.mean(xo * xo, axis=-1, keepdims=True) + EPS)
        xh = xo * r2
        fg = fg_ref[...]
        diff = xh * fg - t_ref[...]
        sq = jnp.sum(diff * diff, axis=0, keepdims=True)
        part = sq[:, 0:LANE]
        for t in range(1, D // LANE):
            part = part + sq[:, LANE * t:LANE * (t + 1)]
        dout = diff * (1.0 / D)
        dxh = dout * fg
        dx2 = r2 * (dxh - xh * jnp.mean(dxh * xh, axis=-1, keepdims=True))
        dx2_ref[...] = dx2
        dyb = _mx(dx2 * gate)
        da = _dot_nt(dyb, w_ref[...])
        do_ref[...] = da * silu
        dg_ref[...] = _mx(da * o * (sg * (1.0 + gv * (1.0 - sg))))

        @pl.when(i == 0)
        def _():
            loss_ref[...] = jnp.zeros_like(loss_ref)
            dfg_ref[...] = jnp.zeros_like(dfg_ref)
            dw_ref[...] = jnp.zeros_like(dw_ref)

        @pl.when(i % nsb == 0)
        def _():
            dgate_ref[...] = jnp.zeros_like(dgate_ref)

        loss_ref[...] += (0.5 / D) * part
        dfg_ref[...] += jnp.sum(dout * xh, axis=0, keepdims=True)
        dgate_ref[0] += jnp.sum(dx2 * y, axis=0, keepdims=True)
        dw_ref[...] += _dot_tn(ab, dyb)

        @pl.when(i == T // ts - 1)
        def _():
            chunks = [dw_ref[512 + 64 * PAIR_INV[h]:512 + 64 * (PAIR_INV[h] + 1), :] for h in range(SWA_HEADS)]
            for h in range(SWA_HEADS):
                dw_ref[512 + 64 * h:512 + 64 * (h + 1), :] = chunks[h]

    row = lambda w: pl.BlockSpec((ts, w), lambda i: (i, 0))
    full = lambda a: pl.BlockSpec(a.shape, lambda i: (0,) * a.ndim)
    per_b = pl.BlockSpec((1, 1, D), lambda i: (i // nsb, 0, 0))
    return pl.pallas_call(
        body, name="post", grid=(T // ts,),
        out_shape=[jax.ShapeDtypeStruct((T, D), F32), jax.ShapeDtypeStruct((T, 1024), F32),
                   jax.ShapeDtypeStruct((T, 1024), _MXU_DTYPE), jax.ShapeDtypeStruct((1, LANE), F32),
                   jax.ShapeDtypeStruct((1, D), F32), jax.ShapeDtypeStruct(gate.shape, F32),
                   jax.ShapeDtypeStruct(w_out.shape, F32)],
        in_specs=[row(512), row(512), row(1024), full(w_out), row(D), per_b, full(fg), row(D)],
        out_specs=[row(D), row(1024), row(1024),
                   pl.BlockSpec((1, LANE), lambda i: (0, 0)), pl.BlockSpec((1, D), lambda i: (0, 0)), per_b,
                   full(w_out)],
        compiler_params=_cparams(("arbitrary",)),
    )(om, osw, g, w_out, x2, gate, fg, tgt)


def _pre_bwd(dq, dk, dv, dqs, dks, dvs, dg, zq, zkv, ql, kvl, x2, dx2, scale, ng, w_in, gq, gkv, w_uq, w_uk, w_uv,
             rope, S, ts):
    T, D = x2.shape
    nsb = S // ts
    WQ = MLA_HEADS * LANE

    def body(dq_ref, dk_ref, dv_ref, dqs_ref, dks_ref, dvs_ref, dg_ref, zq_ref, zkv_ref, ql_ref, kvl_ref, x_ref, dx2_ref,
             sc_ref,
             ng_ref, w_ref, gq_ref, gkv_ref, wuq_ref, wuk_ref, wuv_ref, cd_ref, sd_ref,
             gx_ref, dz_ref, dgq_ref, dgkv_ref, dng_ref, dsh_ref, dsc_ref, dwuq_ref, dwuk_ref, dwuv_ref):
        i = pl.program_id(0)

        @pl.when(i == 0)
        def _():
            dgq_ref[...] = jnp.zeros_like(dgq_ref)
            dgkv_ref[...] = jnp.zeros_like(dgkv_ref)
            dng_ref[...] = jnp.zeros_like(dng_ref)
            dwuq_ref[...] = jnp.zeros_like(dwuq_ref)
            dwuk_ref[...] = jnp.zeros_like(dwuk_ref)
            dwuv_ref[...] = jnp.zeros_like(dwuv_ref)

        @pl.when(i % nsb == 0)
        def _():
            dsh_ref[...] = jnp.zeros_like(dsh_ref)
            dsc_ref[...] = jnp.zeros_like(dsc_ref)

        def norm_bwd(z, dy, gain):
            r = lax.rsqrt(jnp.mean(z * z, axis=-1, keepdims=True) + EPS)
            zh = z * r
            dzh = dy * gain
            return r * (dzh - zh * jnp.mean(dzh * zh, axis=-1, keepdims=True)), jnp.sum(dy * zh, axis=0, keepdims=True)

        tables = _rope_split(cd_ref[...], sd_ref[...])
        ng = ng_ref[...]
        sc1 = 1.0 + sc_ref[0]

        def rows_chain(rs):
            cq, ck, sa, sb = (t[rs] for t in tables)
            dqr = dq_ref[rs, :].astype(F32)
            dqb = _mx(dqr * _tile_heads(cq) + pltpu.roll(dqr * _tile_heads(sa), 16, 1)
                      + pltpu.roll(dqr * _tile_heads(sb), WQ - 16, 1))
            p_uq = _dot_tn(ql_ref[rs, :], dqb)
            dzq, dgq = norm_bwd(zq_ref[rs, :], _dot_nt(dqb, wuq_ref[...]), gq_ref[...])
            dkr = dk_ref[rs, :].astype(F32)
            dkb = _mx(dkr)
            p_uk = _dot_tn(kvl_ref[rs, :], dkb)
            dvb = _mx(dv_ref[rs, :])
            p_uv = _dot_tn(kvl_ref[rs, :], dvb)
            dzkv, dgkv = norm_bwd(zkv_ref[rs, :], _dot_nt(dkb, wuk_ref[...]) + _dot_nt(dvb, wuv_ref[...]), gkv_ref[...])
            dkpe = dkr[:, 0:LANE]
            for h in range(1, MLA_HEADS):
                dkpe = dkpe + dkr[:, LANE * h:LANE * (h + 1)]
            dkro = dkpe * ck + pltpu.roll(dkpe * sa, 16, 1) + pltpu.roll(dkpe * sb, LANE - 16, 1)
            dz_ref[rs, 0:384] = _mx(dzq)
            dz_ref[rs, 384:640] = _mx(dzkv)
            dz_ref[rs, 640:768] = _mx(dkro)
            dz_ref[rs, 768:1280] = dg_ref[rs, 0:512]
            dz_ref[rs, 1280:1792] = _mx(dqs_ref[rs, :])
            dz_ref[rs, 1792:1920] = _mx(dks_ref[rs, :])
            dz_ref[rs, 1920:2048] = _mx(dvs_ref[rs, :])
            dz_ref[rs, 2048:2560] = dg_ref[rs, 512:1024]
            dh = _dot(dz_ref[rs, :], w_ref[...])
            x = x_ref[rs, :]
            r1 = lax.rsqrt(jnp.mean(x * x, axis=-1, keepdims=True) + EPS)
            xn = x * r1
            dxn = dh * ng * sc1
            gx_ref[rs, :] = dx2_ref[rs, :] + r1 * (dxn - xn * jnp.mean(dxn * xn, axis=-1, keepdims=True))
            return (p_uq, p_uk, p_uv, dgq, dgkv, jnp.sum(dh, axis=0, keepdims=True),
                    jnp.sum(dh * (xn * ng), axis=0, keepdims=True), jnp.sum(dh * xn * sc1, axis=0, keepdims=True))

        hr = ts // 2
        parts = [rows_chain(slice(hr * t, hr * (t + 1))) for t in range(2)]
        p_uq, p_uk, p_uv, dgq, dgkv, dsh, dsc, dng = (a + b for a, b in zip(*parts))
        dwuq_ref[...] += p_uq
        dwuk_ref[...] += p_uk
        dwuv_ref[...] += p_uv
        dgq_ref[...] += dgq
        dgkv_ref[...] += dgkv
        dsh_ref[0] += dsh
        dsc_ref[0] += dsc
        dng_ref[...] += dng

    row = lambda w: pl.BlockSpec((ts, w), lambda i: (i, 0))
    full = lambda a: pl.BlockSpec(a.shape, lambda i: (0,) * a.ndim, pipeline_mode=pl.Buffered(1))
    per_b = pl.BlockSpec((1, 1, D), lambda i: (i // nsb, 0, 0))
    dense = pl.BlockSpec((ts // 8, LANE), lambda i: (i, 0))
    vec = lambda w: pl.BlockSpec((1, w), lambda i: (0, 0))
    return pl.pallas_call(
        body, name="pre_bwd", grid=(T // ts,),
        out_shape=[jax.ShapeDtypeStruct((T, D), F32), jax.ShapeDtypeStruct((T, D_IN_PAD), _MXU_DTYPE), jax.ShapeDtypeStruct((1, 384), F32),
                   jax.ShapeDtypeStruct((1, 256), F32), jax.ShapeDtypeStruct((1, D), F32),
                   jax.ShapeDtypeStruct(scale.shape, F32), jax.ShapeDtypeStruct(scale.shape, F32),
                   jax.ShapeDtypeStruct(w_uq.shape, F32), jax.ShapeDtypeStruct(w_uk.shape, F32),
                   jax.ShapeDtypeStruct(w_uv.shape, F32)],
        in_specs=[row(WQ), row(WQ), row(512), row(512), row(LANE), row(LANE), row(1024), row(384), row(256), row(384),
                  row(256), row(D), row(D), per_b, full(ng), full(w_in), full(gq), full(gkv), full(w_uq), full(w_uk), full(w_uv),
                  dense, dense],
        out_specs=[row(D), row(D_IN_PAD), vec(384), vec(256), vec(D), per_b, per_b, full(w_uq), full(w_uk), full(w_uv)],
        compiler_params=_cparams(("arbitrary",)),
    )(dq, dk, dv, dqs, dks, dvs, dg, zq, zkv, ql, kvl, x2, dx2, scale, ng, w_in, gq, gkv, w_uq, w_uk, w_uv, *rope)


def _w_in_row_runs():
    runs = [(0, 0, 640), (640, 704, 32), (672, 768, 512)]
    runs += [(1184 + 64 * h, 1280 + 64 * PAIR_INV[h], 64) for h in range(8)]
    runs += [(1696, 1792, 256)]
    runs += [(1952 + 64 * h, 2048 + 64 * PAIR_INV[h], 64) for h in range(8)]
    return runs


W_IN_SHARD = D_IN // N_DEV
W_IN_SLOT = 320


def _dw_in_t(dz, hb, tn, tk):
    T, M = dz.shape
    N = hb.shape[1]
    nk = T // tk

    def body(a_ref, b_ref, o_ref, ob_ref, acc_ref):
        k = pl.program_id(1)

        @pl.when(k == 0)
        def _():
            acc_ref[...] = jnp.zeros_like(acc_ref)

        acc_ref[...] += _dot_tn(a_ref[...], b_ref[...])

        @pl.when(k == nk - 1)
        def _():
            for d in range(N_DEV):
                lo, hi = W_IN_SHARD * d, W_IN_SHARD * (d + 1)
                for nat, pad, size in _w_in_row_runs():
                    a, b = max(nat, lo), min(nat + size, hi)
                    if a < b:
                        piece = acc_ref[pad + a - nat:pad + b - nat, :]
                        o_ref[d, a - lo:b - lo, :] = piece
                        ob_ref[d, a - lo:b - lo, :] = piece.astype(ob_ref.dtype)
                o_ref[d, W_IN_SHARD:W_IN_SLOT, :] = jnp.zeros((W_IN_SLOT - W_IN_SHARD, tn), F32)
                ob_ref[d, W_IN_SHARD:W_IN_SLOT, :] = jnp.zeros((W_IN_SLOT - W_IN_SHARD, tn), ob_ref.dtype)

    slots = pl.BlockSpec((N_DEV, W_IN_SLOT, tn), lambda j, k: (0, 0, j))
    return pl.pallas_call(
        body, name="dw_in", grid=(N // tn, nk),
        out_shape=[jax.ShapeDtypeStruct((N_DEV, W_IN_SLOT, N), F32), jax.ShapeDtypeStruct((N_DEV, W_IN_SLOT, N), jnp.bfloat16)],
        in_specs=[pl.BlockSpec((tk, M), lambda j, k: (k, 0)), pl.BlockSpec((tk, tn), lambda j, k: (k, j))],
        out_specs=[slots, slots],
        scratch_shapes=[pltpu.VMEM((M, tn), F32)],
        compiler_params=_cparams(("arbitrary", "arbitrary")),
    )(dz, hb)


SHARDED = (3, 6, 7, 9)
PART_SLICES = {10: (128, 1152), 2: (1152, 2176), 4: (2176, 2560), 5: (2560, 2816), 8: (2816, 2824)}


def _finalize_adamw(parts_all, dmod_all, dmod_cols, c_all, ws, sharded_grads, ms, vs):
    n = len(ws)
    local = [t for t in range(n) if t not in SHARDED[1:]]

    def body(*refs):
        p_ref, dm_ref, dmc_ref, c_ref = refs[:4]
        w_refs, gs_refs, m_refs, v_refs = refs[4:4 + n], refs[4 + n:8 + n], refs[8 + n:8 + 2 * n], refs[8 + 2 * n:8 + 3 * n]
        outs = refs[8 + 3 * n:]
        loss_ref, gl_refs = outs[0], outs[1:1 + len(local)]
        d_refs, nm_refs, nv_refs = (outs[1 + len(local) + t * n:1 + len(local) + (t + 1) * n] for t in range(3))
        ps_ref = outs[-1]
        acc = p_ref[0]
        for j in range(1, N_DEV):
            acc = acc + p_ref[j]
        ps_ref[...] = acc
        loss_ref[...] = jnp.sum(acc[:, 0:LANE], axis=1, keepdims=True)
        db = dm_ref[0:1, :]
        for j in range(1, dm_ref.shape[0]):
            db = db + dm_ref[j:j + 1, :]
        grads = {0: _dot_tn(_mx(_silu(c_ref[...])), _mx(dmc_ref[...])), 1: db}
        for t, (lo, hi) in PART_SLICES.items():
            grads[t] = ps_ref[:, lo:hi]
        for i, t in enumerate(SHARDED):
            grads[t] = gs_refs[i][...]
        grads[SHARDED[0]] = gs_refs[0][0:W_IN_SHARD, :]
        for i, t in enumerate(local):
            gl_refs[i][...] = grads[t]
        for t in range(n):
            gv = grads[t]
            nm = ADAM_B1 * m_refs[t][...] + (1.0 - ADAM_B1) * gv
            nv = ADAM_B2 * v_refs[t][...] + (1.0 - ADAM_B2) * (gv * gv)
            m_hat = nm / (1.0 - ADAM_B1 ** ADAM_STEP)
            v_hat = nv / (1.0 - ADAM_B2 ** ADAM_STEP)
            d_refs[t][...] = -ADAM_LR * (m_hat / (jnp.sqrt(v_hat) + ADAM_EPS) + ADAM_WD * w_refs[t][...])
            nm_refs[t][...] = nm
            nv_refs[t][...] = nv

    like = lambda arrs: [jax.ShapeDtypeStruct(a.shape, F32) for a in arrs]
    out = pl.pallas_call(
        body, name="finalize_adamw",
        out_shape=[jax.ShapeDtypeStruct((1, 1), F32)] + like([ws[t] for t in local]) + like(ws) * 3,
        in_specs=[_vmem()] * (8 + 3 * n), out_specs=[_vmem()] * (1 + len(local) + 3 * n),
        scratch_shapes=[pltpu.VMEM(parts_all.shape[1:], F32)],
        compiler_params=_cparams(),
    )(parts_all, dmod_all, dmod_cols, c_all, *ws, *sharded_grads, *ms, *vs)
    k = 1 + len(local)
    return out[0], dict(zip(local, out[1:k])), out[k:k + n], out[k + n:k + 2 * n], out[k + 2 * n:]


def _pair_perm(a, axis, order):
    a = jnp.moveaxis(a, axis, -1)
    lead = a.shape[:-1]
    a = a.reshape(lead + (8, 64))[..., list(order), :].reshape(lead + (512,))
    return jnp.moveaxis(a, -1, axis)


def _rope_table(positions):
    T = positions.size
    inv = ROPE_THETA ** (-jnp.arange(0, MLA_ROPE, 2, dtype=F32) / MLA_ROPE)
    pos = jnp.repeat(positions.reshape(T // 8, 8)[:, ::-1].astype(F32), MLA_ROPE // 2, axis=1)
    ang = pos * jnp.tile(inv, 8)[None, :]
    return jnp.cos(ang), jnp.sin(ang)


def _local_step(x, mod, positions, ng, w_in_t, gq, gkv, w_uq, w_ukv, sinks, w_out, fg, tgt,
                ts=512, fq=512, fk=512, bq=512, bk=512):
    B, S, D = x.shape
    T = B * S
    x2 = x.reshape(T, D)
    shift, scale, gate = (mod[:, None, k * D:(k + 1) * D] for k in range(3))
    w_uq_p = jnp.pad(w_uq.reshape(Q_LORA, MLA_HEADS, 96), ((0, 0), (0, 0), (0, 32))).reshape(Q_LORA, MLA_HEADS * LANE)
    w_ukv3 = w_ukv.reshape(KV_LORA, MLA_HEADS, 128)
    w_uk_p = jnp.pad(w_ukv3[:, :, :64], ((0, 0), (0, 0), (0, 64))).reshape(KV_LORA, MLA_HEADS * LANE)
    w_uv = w_ukv3[:, :, 64:].reshape(KV_LORA, 512)
    w_out_p = jnp.concatenate([w_out[:512], _pair_perm(w_out[512:], 0, PAIR_ORDER)], axis=0)
    rope = _rope_table(positions)
    posf = positions.astype(F32)
    posb = jnp.broadcast_to(posf[:, :, None], (B, S, LANE))
    posr = posf.reshape(B, S // WINDOW, WINDOW)
    sinks_l = jnp.pad(sinks.reshape(1, SWA_HEADS), ((0, 0), (0, LANE - SWA_HEADS)))

    (hb, zq, zkv, ql, kvl, q, k, v, qs, ks, vs, g, w_in_p) = _pre_fwd(
        x2, shift, scale, ng, w_in_t, gq, gkv, w_uq_p, w_uk_p, w_uv, rope, S, ts)
    r3 = lambda a: a.reshape(B, S, a.shape[-1])
    om, lse_m = _mla_fwd(r3(q), r3(k), r3(v), fq, fk)
    osw, lse_s = _swa_fwd(r3(qs), r3(ks), r3(vs), posb, posr, sinks_l)
    dx2, do, dg, loss_v, dfg, dgate, dw_out = _post(
        om.reshape(T, 512), osw.reshape(T, 512), g, w_out_p, x2, gate, fg.reshape(1, D), tgt.reshape(T, D), S, ts)
    do3 = r3(do)
    dq, dk, dv = _mla_bwd(r3(q), r3(k), r3(v), om, do3, lse_m, bq, bk)
    dqs, dks, dvs, dsink = _swa_bwd(r3(qs), r3(ks), r3(vs), posb, posr, sinks_l, osw, do3, lse_s)
    f2 = lambda a: a.reshape(T, a.shape[-1])
    gx, dz, dgq, dgkv, dng, dsh, dsc, dw_uq_p, dw_uk_p, dw_uv = _pre_bwd(
        f2(dq), f2(dk), f2(dv), f2(dqs), f2(dks), f2(dvs), dg, zq, zkv, ql, kvl, x2, dx2, scale, ng, w_in_p, gq, gkv,
        w_uq_p, w_uk_p, w_uv, rope, S, ts)
    tk = min(T, 1024)
    dw_in_t = _dw_in_t(dz, hb, 512, tk)
    dw_uq = dw_uq_p.reshape(Q_LORA, MLA_HEADS, LANE)[:, :, :96].reshape(Q_LORA, 768)
    dw_uk = dw_uk_p.reshape(KV_LORA, MLA_HEADS, LANE)[:, :, :64]
    dw_uv = dw_uv.reshape(KV_LORA, MLA_HEADS, 64)
    dw_ukv = jnp.concatenate([dw_uk, dw_uv], axis=2).reshape(KV_LORA, 1024)
    parts = jnp.concatenate([loss_v, dfg, dng, dgq, dgkv, dsink], axis=1)
    dmod = jnp.concatenate([dsh, dsc, dgate], axis=2).reshape(B, 3 * D)
    return gx.reshape(B, S, D), dw_in_t, dw_uq, dw_ukv, dw_out, parts, dmod


def kernel(x, c, positions, w_ada, b_ada, norm_gain, w_in, q_norm_gain, kv_norm_gain, w_uq, w_ukv, swa_sinks, w_out, final_gain, loss_target, m_w_ada, m_b_ada, m_norm_gain, m_w_in, m_q_norm_gain, m_kv_norm_gain, m_w_uq, m_w_ukv, m_swa_sinks, m_w_out, m_final_gain, v_w_ada, v_b_ada, v_norm_gain, v_w_in, v_q_norm_gain, v_kv_norm_gain, v_w_uq, v_w_ukv, v_swa_sinks, v_w_out, v_final_gain):
    B, S, D = x.shape
    me = 4 * lax.axis_index("x") + 2 * lax.axis_index("y") + lax.axis_index("c")
    bf = _MXU_DTYPE

    ncol = w_ada.shape[2]
    b_cols = lax.dynamic_slice_in_dim(b_ada, me * ncol, ncol, axis=1)
    c_all, win_g, wuq_g, wukv_g, wout_g, mod_g = _all_gather(
        [c, w_in[0].T.astype(bf), w_uq[0].astype(bf), w_ukv[0].astype(bf), w_out[0].astype(bf)], "ag_weights",
        fused=((w_ada[0], b_cols), _ada_cols, jax.ShapeDtypeStruct((N_DEV, B, ncol), F32)))
    c_all = c_all.reshape(N_DEV * B, D)
    cat_cols = lambda a: jnp.transpose(a, (1, 0, 2)).reshape(a.shape[1], N_DEV * a.shape[2])
    w_in_t, w_uq_f, w_ukv_f = win_g, cat_cols(wuq_g), cat_cols(wukv_g)
    w_out_f = wout_g.reshape(D, D)

    mod = lax.dynamic_index_in_dim(mod_g, me, axis=1, keepdims=False)
    mod = jnp.transpose(mod, (1, 0, 2)).reshape(B, 3 * D)

    gx, dw_in_t, dw_uq, dw_ukv, dw_out, parts, dmod = _local_step(
        x, mod, positions, norm_gain, w_in_t, q_norm_gain, kv_norm_gain, w_uq_f, w_ukv_f, swa_sinks,
        w_out_f, final_gain, loss_target)

    split_cols = lambda a: jnp.transpose(a.reshape(a.shape[0], 4, 2, a.shape[1] // N_DEV), (1, 2, 0, 3))
    split_rows = lambda a: a.reshape(4, 2, a.shape[0] // N_DEV, a.shape[1])
    slots = lambda a: a.reshape((4, 2) + a.shape[1:])
    g_w_in_t, g_w_uq, g_w_ukv, g_w_out, parts_g, dmod_g = _reduce_scatter(
        [(slots(dw_in_t[0]), slots(dw_in_t[1])), split_cols(dw_uq), split_cols(dw_ukv), split_rows(dw_out)], "rs_grads",
        gather=(parts, dmod))

    dmod_all = dmod_g.reshape(N_DEV * B, 3 * D)
    dmod_cols = lax.dynamic_slice_in_dim(dmod_all, me * ncol, ncol, axis=1)
    ws = [w_ada, b_ada, norm_gain, w_in, q_norm_gain, kv_norm_gain, w_uq, w_ukv, swa_sinks, w_out, final_gain]
    ms = [m_w_ada, m_b_ada, m_norm_gain, m_w_in, m_q_norm_gain, m_kv_norm_gain, m_w_uq, m_w_ukv, m_swa_sinks, m_w_out,
          m_final_gain]
    vs = [v_w_ada, v_b_ada, v_norm_gain, v_w_in, v_q_norm_gain, v_kv_norm_gain, v_w_uq, v_w_ukv, v_swa_sinks, v_w_out,
          v_final_gain]
    two_d = lambda a: a.reshape((1, a.shape[0]) if a.ndim == 1 else a.shape[-2:])
    flat = lambda arrs: [jnp.transpose(a[0]) if t == SHARDED[0] else two_d(a) for t, a in enumerate(arrs)]
    sharded = {3: g_w_in_t, 6: g_w_uq, 7: g_w_ukv, 9: g_w_out}
    loss, local_grads, deltas, new_ms, new_vs = _finalize_adamw(
        parts_g, dmod_all, dmod_cols, c_all, flat(ws), [sharded[t] for t in SHARDED], flat(ms), flat(vs))
    grads = [local_grads[t] if t in local_grads else sharded[t] for t in range(len(ws))]
    shaped = lambda arrs: [(jnp.transpose(a) if t == SHARDED[0] else a).reshape(ws[t].shape) for t, a in enumerate(arrs)]
    return (loss.reshape(()), gx, *shaped(grads), *shaped(deltas), *shaped(new_ms), *shaped(new_vs))
```

```python
import functools

import jax
import jax.numpy as jnp
from jax import lax
from jax.experimental import pallas as pl
from jax.experimental.pallas import tpu as pltpu

F32 = jnp.float32
_MXU_DTYPE = jnp.bfloat16
MLA_GRAD_DTYPE = jnp.bfloat16

N_DEV = 8
MLA_HEADS = 8
MLA_NOPE = 64
MLA_ROPE = 32
Q_LORA = 384
KV_LORA = 256
SWA_HEADS = 8
SWA_KV_HEADS = 2
SWA_HEAD_DIM = 64
WINDOW = 128
ROPE_THETA = 10000.0
EPS = 1e-6
MLA_SCALE = float((MLA_NOPE + MLA_ROPE) ** -0.5)
SWA_SCALE = float(SWA_HEAD_DIM ** -0.5)
LOG2E = 1.4426950408889634
MLA_QSCALE = MLA_SCALE * LOG2E
D_IN = 2464
D_IN_PAD = 2560
PAIR_ORDER = (0, 4, 1, 5, 2, 6, 3, 7)
PAIR_INV = (0, 2, 4, 6, 1, 3, 5, 7)

ADAM_LR = 0.001
ADAM_B1 = 0.9
ADAM_B2 = 0.999
ADAM_EPS = 1e-08
ADAM_WD = 0.01
ADAM_STEP = 10

LANE = 128
VMEM_LIMIT = 56 * 1024 * 1024

MESH = pl.DeviceIdType.MESH
NEG_INF = float("-inf")
SWA_SEQ_SPLIT = 2


def _mx(a):
    return a.astype(_MXU_DTYPE)


def _dot(a, b):
    return jnp.dot(a, b, preferred_element_type=F32)


def _dot_nt(a, b):
    return lax.dot_general(a, b, (((1,), (1,)), ((), ())), preferred_element_type=F32)


def _dot_tn(a, b):
    return lax.dot_general(a, b, (((0,), (0,)), ((), ())), preferred_element_type=F32)


def _cparams(sem=None):
    return pltpu.CompilerParams(dimension_semantics=sem, vmem_limit_bytes=VMEM_LIMIT)


def _vmem():
    return pl.BlockSpec(memory_space=pltpu.VMEM)


def _lane_iota(shape):
    return lax.broadcasted_iota(jnp.int32, shape, len(shape) - 1)


def _gather_program(srcs, outs, send_sems, recv_sems, local_sems):
    x, y, c = lax.axis_index("x"), lax.axis_index("y"), lax.axis_index("c")
    me, sibling = (x, y, c), (x, y, 1 - c)
    chips = [(1 - x, y), (x, 1 - y), (1 - x, 1 - y)]

    def slot(a, dev):
        return outs[a].at[4 * dev[0] + 2 * dev[1] + dev[2]]

    def copy(a, k, block, to, src=None):
        return pltpu.make_async_remote_copy(
            src_ref=slot(a, block) if src is None else src, dst_ref=slot(a, block),
            send_sem=send_sems.at[7 * a + k], recv_sem=recv_sems.at[7 * a + k],
            device_id=to, device_id_type=MESH)

    def local(a):
        return pltpu.make_async_copy(srcs[a], slot(a, me), local_sems.at[a])

    def start(a):
        local(a).start()
        cps = [copy(a, 0, me, sibling, src=srcs[a])]
        cps += [copy(a, 1 + j, me, (*chip, c), src=srcs[a]) for j, chip in enumerate(chips)]
        for cp in cps:
            cp.start()
        return cps

    def finish(group):
        cps = []
        for j, chip in enumerate(chips):
            for a in group:
                copy(a, 1 + j, (*chip, c), me).wait_recv()
                cp = copy(a, 4 + j, (*chip, c), sibling)
                cp.start()
                cps.append(cp)
        for a in group:
            copy(a, 0, sibling, me).wait_recv()
            for j, chip in enumerate(chips):
                copy(a, 4 + j, (*chip, 1 - c), me).wait_recv()
            local(a).wait()
        return cps

    return start, finish


def _gather_sems(m):
    return [pltpu.SemaphoreType.DMA((7 * m,)), pltpu.SemaphoreType.DMA((7 * m,)), pltpu.SemaphoreType.DMA((m,))]


def _all_gather(arrs, name, fused=None):
    n = len(arrs)
    extra, fn, piece = fused if fused else ((), None, None)
    ne, m = len(extra), n + (1 if fused else 0)

    def body(*refs):
        ins, ex, outs = refs[:n], refs[n:n + ne], refs[n + ne:n + ne + m]
        rest = refs[n + ne + m:]
        srcs = list(ins) + ([rest[0]] if fused else [])
        start, finish = _gather_program(srcs, outs, *rest[-3:])
        pending = []
        for a in range(n):
            pending += start(a)
        if fused:
            pending += finish([0])
            fn(srcs[n], outs[0], *ex)
            pending += start(n)
            pending += finish([n] + list(range(1, n)))
        else:
            pending += finish(list(range(n)))
        for cp in pending:
            cp.wait_send()

    out_shape = [jax.ShapeDtypeStruct((N_DEV,) + a.shape, a.dtype) for a in arrs]
    scratch = []
    if fused:
        out_shape.append(jax.ShapeDtypeStruct((N_DEV,) + piece.shape, piece.dtype))
        scratch.append(pltpu.VMEM(piece.shape, piece.dtype))
    return pl.pallas_call(
        body, name=name, out_shape=out_shape,
        in_specs=[_vmem()] * (n + ne), out_specs=[_vmem()] * m,
        scratch_shapes=scratch + _gather_sems(m),
        compiler_params=pltpu.CompilerParams(vmem_limit_bytes=VMEM_LIMIT),
    )(*arrs, *extra)


def _reduce_scatter(arrs, name, gather=()):
    n, g = len(arrs), len(gather)
    halves = [a[1] if isinstance(a, tuple) else a.astype(jnp.bfloat16) for a in arrs]
    arrs = [a[0] if isinstance(a, tuple) else a for a in arrs]

    def body(*refs):
        xs, xbs, gins = refs[:n], refs[n:2 * n], refs[2 * n:2 * n + g]
        outs, gouts = refs[2 * n + g:3 * n + g], refs[3 * n + g:3 * n + 2 * g]
        rest = refs[3 * n + 2 * g:]
        parts, recv_a, send_b, recv_b = (rest[t * n:(t + 1) * n] for t in range(4))
        send_sems, recv_sems, local_sems = rest[4 * n:4 * n + 3]
        x, y, c = lax.axis_index("x"), lax.axis_index("y"), lax.axis_index("c")
        myq = 2 * x + y
        gather_start, gather_finish = _gather_program(gins, gouts, *rest[4 * n + 3:])

        def chip(k):
            return (1 - x if k & 2 else x, 1 - y if k & 1 else y)

        def from_sibling(a):
            return pltpu.make_async_remote_copy(
                src_ref=xbs[a].at[:, 1 - c], dst_ref=recv_a[a], send_sem=send_sems.at[4 * a], recv_sem=recv_sems.at[4 * a],
                device_id=(x, y, 1 - c), device_id_type=MESH)

        def to_owner(a, k):
            qx, qy = chip(k)
            return pltpu.make_async_remote_copy(
                src_ref=send_b[a].at[2 * qx + qy], dst_ref=recv_b[a].at[myq],
                send_sem=send_sems.at[4 * a + k], recv_sem=recv_sems.at[4 * a + k],
                device_id=(qx, qy, c), device_id_type=MESH)

        mine = [pltpu.make_async_copy(xs[a].at[:, c], parts[a], local_sems.at[a]) for a in range(n)]
        first = [from_sibling(a) for a in range(n)]
        for cp in mine + first:
            cp.start()
        second = []
        for b in range(g):
            second += gather_start(b)
        for a in range(n):
            mine[a].wait()
            first[a].wait_recv()
            parts[a][...] = parts[a][...] + recv_a[a][...].astype(F32)
            send_b[a][...] = parts[a][...].astype(jnp.bfloat16)
            for k in range(1, 4):
                cp = to_owner(a, k)
                cp.start()
                second.append(cp)
        second += gather_finish(list(range(g)))
        for a in range(n):
            acc = parts[a][myq]
            for k in range(1, 4):
                to_owner(a, k).wait_recv()
                qx, qy = chip(k)
                acc = acc + recv_b[a][2 * qx + qy].astype(F32)
            outs[a][...] = acc
        for cp in first + second:
            cp.wait_send()

    quarter = lambda a, dt: pltpu.VMEM((4,) + a.shape[2:], dt)
    return pl.pallas_call(
        body, name=name,
        out_shape=[jax.ShapeDtypeStruct(a.shape[2:], F32) for a in arrs]
        + [jax.ShapeDtypeStruct((N_DEV,) + a.shape, a.dtype) for a in gather],
        in_specs=[pl.BlockSpec(memory_space=pl.ANY)] * (2 * n) + [_vmem()] * g, out_specs=[_vmem()] * (n + g),
        scratch_shapes=[quarter(a, F32) for a in arrs] + [quarter(a, jnp.bfloat16) for a in arrs] * 3
        + [pltpu.SemaphoreType.DMA((4 * n,)), pltpu.SemaphoreType.DMA((4 * n,)), pltpu.SemaphoreType.DMA((n,))]
        + _gather_sems(g),
        compiler_params=pltpu.CompilerParams(vmem_limit_bytes=VMEM_LIMIT),
    )(*arrs, *halves, *gather)


def _silu(t):
    return t * (1.0 / (1.0 + jnp.exp(-t)))


def _ada_cols(piece_ref, c_all_ref, w_ref, b_ref):
    w = _mx(w_ref[...])
    for d in range(N_DEV):
        piece_ref[d] = _dot(_mx(_silu(c_all_ref[d])), w) + b_ref[...]


def _rope_split(cd, sd):
    n = cd.shape[0]

    def expand(d):
        rep = jnp.broadcast_to(d[:, None, :], (n, 8, LANE)).reshape(8 * n, LANE)
        return pltpu.roll(rep, 0, 1, stride=16, stride_axis=0)

    c, s = expand(cd), expand(sd)
    lane = _lane_iota(c.shape)
    first = jnp.logical_and(lane >= 64, lane < 80)
    second = jnp.logical_and(lane >= 80, lane < 96)
    ck = jnp.where(first, pltpu.roll(c, 80, 1), jnp.where(second, pltpu.roll(c, 96, 1), 0.0))
    cq = jnp.where(lane < 64, 1.0, ck)
    sa = jnp.where(first, -pltpu.roll(s, 80, 1), 0.0)
    sb = jnp.where(second, pltpu.roll(s, 96, 1), 0.0)
    return cq, ck, sa, sb


def _tile_heads(t):
    return jnp.concatenate([t] * MLA_HEADS, axis=1)


def _pre_fwd(x2, shift, scale, ng, w_in, gq, gkv, w_uq, w_uk, w_uv, rope, S, ts):
    T, D = x2.shape
    nsb = S // ts
    WQ = MLA_HEADS * LANE

    def body(x_ref, sh_ref, sc_ref, ng_ref, wn_ref, gq_ref, gkv_ref, wuq_ref, wuk_ref, wuv_ref,
             cd_ref, sd_ref,
             hb_ref, zq_ref, zkv_ref, ql_ref, kvl_ref, q_ref, k_ref, v_ref, qs_ref, ks_ref, vs_ref, g_ref, wp_ref):
        @pl.when(pl.program_id(0) == 0)
        def _():
            wp_ref[640:704, :] = jnp.zeros((64, D), wp_ref.dtype)
            wp_ref[736:768, :] = jnp.zeros((32, D), wp_ref.dtype)
            for d in range(N_DEV):
                lo, hi = W_IN_SHARD * d, W_IN_SHARD * (d + 1)
                for nat, pad, size in _w_in_row_runs():
                    a, b = max(nat, lo), min(nat + size, hi)
                    if a < b:
                        wp_ref[pad + a - nat:pad + b - nat, :] = wn_ref[d, a - lo:b - lo, :]

        w_ref = wp_ref
        x = x_ref[...]
        r1 = lax.rsqrt(jnp.mean(x * x, axis=-1, keepdims=True) + EPS)
        h = ((x * r1) * ng_ref[...]) * (1.0 + sc_ref[0]) + sh_ref[0]
        hb = _mx(h)
        hb_ref[...] = hb
        zq = _dot_nt(hb, w_ref[0:384, :])
        zq_ref[...] = zq
        rq = lax.rsqrt(jnp.mean(zq * zq, axis=-1, keepdims=True) + EPS)
        ql = _mx((zq * rq) * gq_ref[...])
        ql_ref[...] = ql
        q = _dot(ql, wuq_ref[...])
        cq, ck, sa, sb = _rope_split(cd_ref[...], sd_ref[...])
        q = (q * _tile_heads(cq) + pltpu.roll(q, WQ - 16, 1) * _tile_heads(sa)
             + pltpu.roll(q, 16, 1) * _tile_heads(sb))
        q_ref[...] = _mx(q * MLA_QSCALE)
        zkv = _dot_nt(hb, w_ref[384:640, :])
        zkv_ref[...] = zkv
        rkv = lax.rsqrt(jnp.mean(zkv * zkv, axis=-1, keepdims=True) + EPS)
        kvl = _mx((zkv * rkv) * gkv_ref[...])
        kvl_ref[...] = kvl
        kr = _dot_nt(hb, w_ref[640:768, :])
        kpe = kr * ck + pltpu.roll(kr, LANE - 16, 1) * sa + pltpu.roll(kr, 16, 1) * sb
        kf = _dot(kvl, wuk_ref[...])
        k_ref[...] = _mx(kf + jnp.concatenate([kpe] * MLA_HEADS, axis=1))
        v_ref[...] = _mx(_dot(kvl, wuv_ref[...]))
        g_ref[:, 0:512] = _dot_nt(hb, w_ref[768:1280, :])
        qs_ref[...] = _mx(_dot_nt(hb, w_ref[1280:1792, :]) * (SWA_SCALE * LOG2E))
        ks_ref[...] = _mx(_dot_nt(hb, w_ref[1792:1920, :]))
        vs_ref[...] = _mx(_dot_nt(hb, w_ref[1920:2048, :]))
        g_ref[:, 512:1024] = _dot_nt(hb, w_ref[2048:2560, :])

    row = lambda w: pl.BlockSpec((ts, w), lambda i: (i, 0))
    dense = pl.BlockSpec((ts // 8, LANE), lambda i: (i, 0))
    full = lambda a: pl.BlockSpec(a.shape, lambda i: (0,) * a.ndim)
    per_b = pl.BlockSpec((1, 1, D), lambda i: (i // nsb, 0, 0))
    out_w = [(D, _MXU_DTYPE), (384, F32), (256, F32), (384, _MXU_DTYPE), (256, _MXU_DTYPE), (WQ, _MXU_DTYPE),
             (WQ, _MXU_DTYPE), (512, _MXU_DTYPE), (512, _MXU_DTYPE), (128, _MXU_DTYPE), (128, _MXU_DTYPE), (1024, F32)]
    return pl.pallas_call(
        body, name="pre_fwd", grid=(T // ts,),
        out_shape=[jax.ShapeDtypeStruct((T, w), dt) for w, dt in out_w] + [jax.ShapeDtypeStruct((D_IN_PAD, D), w_in.dtype)],
        in_specs=[row(D), per_b, per_b, full(ng), full(w_in), full(gq), full(gkv), full(w_uq), full(w_uk), full(w_uv),
                  dense, dense],
        out_specs=[row(w) for w, _ in out_w] + [pl.BlockSpec((D_IN_PAD, D), lambda i: (0, 0))],
        compiler_params=_cparams(("arbitrary",)),
    )(x2, shift, scale, ng, w_in, gq, gkv, w_uq, w_uk, w_uv, *rope)


def _mla_fwd(q3, k3, v3, tq, tk):
    B, S, _ = q3.shape
    nq = S // tq
    assert tq == tk
    HPS = 2
    NH = 2 * HPS

    def body(q_ref, k_ref, v_ref, o_ref, lse_ref):
        rows = lax.broadcasted_iota(jnp.int32, (tq, tk), 0)
        cols = lax.broadcasted_iota(jnp.int32, (tq, tk), 1)
        low_k = _lane_iota((tk, LANE)) < 64
        low = _lane_iota((tq, LANE)) < 64

        def step(qs, kt, carry, masked):
            r0 = kt * tk
            out = []
            for j in range(NH):
                v2 = v_ref[0, r0:r0 + tk, LANE * (j // 2):LANE * (j // 2 + 1)]
                vj = (jnp.where(low_k, v2, 1.0) if j % 2 == 0 else jnp.where(low_k, 1.0, v2)).astype(v2.dtype)
                m, acc = carry[j]
                s = _dot_nt(qs[j], k_ref[0, r0:r0 + tk, LANE * j:LANE * (j + 1)])
                if masked:
                    s = jnp.where(rows >= cols, s, NEG_INF)
                m_new = jnp.maximum(m, jnp.max(s, axis=1, keepdims=True))
                alpha = jnp.exp2(m - m_new)
                p = jnp.exp2(s - m_new)
                acc = alpha * acc + _dot(_mx(p), vj)
                out.append((m_new, acc))
            return tuple(out)

        for qi in range(nq):
            r = slice(qi * tq, (qi + 1) * tq)
            qs = [q_ref[0, r, LANE * j:LANE * (j + 1)] for j in range(NH)]
            init = (jnp.full((tq, 1), NEG_INF, F32), jnp.zeros((tq, LANE), F32))
            carry = (init,) * NH
            for kt in range(qi):
                carry = step(qs, kt, carry, False)
            carry = step(qs, qi, carry, True)
            for t in range(HPS):
                (m0, a0), (m1, a1) = carry[2 * t], carry[2 * t + 1]
                l0 = jnp.where(low, pltpu.roll(a0, 64, 1), a0)
                l1 = jnp.where(low, a1, pltpu.roll(a1, 64, 1))
                o_ref[0, r, LANE * t:LANE * (t + 1)] = jnp.where(low, a0 / l0, a1 / l1)
                lse_ref[0, 2 * t, r, :] = m0 + jnp.log2(l0)
                lse_ref[0, 2 * t + 1, r, :] = m1 + jnp.log2(l1)

    grp = lambda w: pl.BlockSpec((1, S, w), lambda b, hp: (b, 0, hp))
    return pl.pallas_call(
        body, name="mla_fwd", grid=(B, MLA_HEADS // NH),
        out_shape=[jax.ShapeDtypeStruct((B, S, 512), F32), jax.ShapeDtypeStruct((B, MLA_HEADS, S, LANE), F32)],
        in_specs=[grp(NH * LANE), grp(NH * LANE), grp(HPS * LANE)],
        out_specs=[grp(HPS * LANE), pl.BlockSpec((1, NH, S, LANE), lambda b, hp: (b, hp, 0, 0))],
        compiler_params=_cparams(("arbitrary", "arbitrary")),
    )(q3, k3, v3)


def _mla_bwd(q3, k3, v3, o3, do3, lse, tq, tk, hps=2):
    B, S, _ = q3.shape
    nq, nk = S // tq, S // tk
    nh = 2 * hps
    assert tk % tq == 0

    def body(q_ref, k_ref, v_ref, o_ref, do_ref, lse_ref, dq_ref, dk_ref, dv_ref, dkt_ref, dvt_ref):
        rows = lax.broadcasted_iota(jnp.int32, (tq, tk), 0)
        cols = lax.broadcasted_iota(jnp.int32, (tq, tk), 1)
        lane = _lane_iota((tq, LANE))

        def q_tile(qi, _):
            r = pl.ds(pl.multiple_of(qi * tq, tq), tq)
            heads = []
            for j in range(nh):
                lanes = slice(LANE * j, LANE * (j + 1))
                pair = slice(LANE * (j // 2), LANE * (j // 2 + 1))
                do = jnp.where((lane < 64) if j % 2 == 0 else (lane >= 64), do_ref[0, r, pair], 0.0)
                q = q_ref[0, r, lanes]
                heads.append((lanes, pair, q, _mx(q.astype(F32).T), _mx(do), _mx(do.T),
                              jnp.sum(do * o_ref[0, r, pair], axis=1, keepdims=True),
                              jnp.concatenate([lse_ref[0, j, r, :]] * (tk // LANE), axis=1)))
            n_full = (qi * tq) // tk

            def k_tile(kt, dqs, masked):
                kr = pl.ds(pl.multiple_of(kt * tk, tk), tk)
                first = masked and (qi * tq) % tk == 0
                out = []
                dvt = [None] * hps
                for j, (lanes, pair, q, qt, dob, dot_, dcol, lse_c) in enumerate(heads):
                    k = k_ref[0, kr, lanes]
                    s = _dot_nt(q, k)
                    if masked:
                        s = jnp.where(rows + qi * tq >= cols + kt * tk, s, NEG_INF)
                    p = jnp.exp2(s - lse_c)
                    dp = _dot_nt(dob, v_ref[0, kr, pair])
                    dsb = _mx(p * (dp - dcol))
                    if first:
                        dkt_ref[j, kt] = _dot(qt, dsb)
                    else:
                        dkt_ref[j, kt] += _dot(qt, dsb)
                    pv = _dot(dot_, _mx(p))
                    dvt[j // 2] = pv if dvt[j // 2] is None else dvt[j // 2] + pv
                    out.append(dqs[j] + _dot(dsb, k))
                for t in range(hps):
                    if first:
                        dvt_ref[t, kt] = dvt[t]
                    else:
                        dvt_ref[t, kt] += dvt[t]
                return tuple(out)

            dqs = (jnp.zeros((tq, LANE), F32),) * nh
            for kt in range(n_full):
                dqs = k_tile(kt, dqs, False)
            dqs = k_tile(n_full, dqs, True)
            for j in range(nh):
                dq_ref[0, r, heads[j][0]] = (MLA_SCALE * dqs[j]).astype(dq_ref.dtype)
            return 0

        for qi in range(nq):
            q_tile(qi, 0)

        def flush(kt, _):
            kr = pl.ds(pl.multiple_of(kt * tk, tk), tk)
            for j in range(nh):
                dk_ref[0, kr, LANE * j:LANE * (j + 1)] = ((1.0 / LOG2E) * dkt_ref[j, kt].T).astype(dk_ref.dtype)
            for t in range(hps):
                dv_ref[0, kr, LANE * t:LANE * (t + 1)] = dvt_ref[t, kt].T.astype(dv_ref.dtype)
            return 0

        lax.fori_loop(0, nk, flush, 0)

    grp = lambda w: pl.BlockSpec((1, S, w), lambda b, hp: (b, 0, hp))
    return pl.pallas_call(
        body, name="mla_bwd", grid=(B, MLA_HEADS // nh),
        out_shape=[jax.ShapeDtypeStruct((B, S, 1024), MLA_GRAD_DTYPE), jax.ShapeDtypeStruct((B, S, 1024), MLA_GRAD_DTYPE),
                   jax.ShapeDtypeStruct((B, S, 512), MLA_GRAD_DTYPE)],
        in_specs=[grp(nh * LANE), grp(nh * LANE), grp(hps * LANE), grp(hps * LANE), grp(hps * LANE),
                  pl.BlockSpec((1, nh, S, LANE), lambda b, hp: (b, hp, 0, 0))],
        out_specs=[grp(nh * LANE), grp(nh * LANE), grp(hps * LANE)],
        scratch_shapes=[pltpu.VMEM((nh, nk, LANE, tk), F32), pltpu.VMEM((hps, nk, LANE, tk), F32)],
        compiler_params=_cparams(("arbitrary", "arbitrary")),
    )(q3, k3, v3, o3, do3, lse)


def _swa_consts(sink_ref):
    W = WINDOW
    row = lax.broadcasted_iota(jnp.int32, (4 * W, LANE), 0)
    out = []
    for g in range(SWA_KV_HEADS):
        slope = jnp.zeros((4 * W, LANE), F32)
        sink = jnp.zeros((4 * W, LANE), F32)
        for p in range(4):
            h = p + 4 * g
            here = jnp.logical_and(row >= W * p, row < W * (p + 1))
            slope = jnp.where(here, float(LOG2E * 2.0 ** (-8.0 * (h + 1) / SWA_HEADS)), slope)
            sink = jnp.where(here, LOG2E * sink_ref[:, h:h + 1], sink)
        out.append((jnp.concatenate([slope, slope], axis=1), sink))
    return out


def _wide(col):
    return jnp.concatenate([col, col], axis=1)


def _swa_block(q_ref, k_ref, v_ref, pb_ref, pr_ref, n, i, ij):
    W = WINDOW
    kb = jnp.maximum(n - 1, 0)
    r = pl.ds(pl.multiple_of(i * W, W), W)
    kr = pl.ds(pl.multiple_of(kb * W, W), 2 * W)
    q4, k2, v2 = q_ref[0, r, :], k_ref[0, kr, :], v_ref[0, kr, :]
    pq = _wide(pb_ref[0, r, :])
    pk = jnp.concatenate([pr_ref[0, pl.ds(kb, 1), :], pr_ref[0, pl.ds(kb + 1, 1), :]], axis=1)
    rel = ij + (n - kb) * W
    dist = jnp.where(jnp.logical_and(rel >= 0, rel < W), pq - pk, float("inf"))
    return r, kb, kr, q4, k2, v2, jnp.concatenate([dist] * 4, axis=0)


def _swa_stack(x4, g, dtype):
    lane = _lane_iota((WINDOW, LANE))
    mine = (lane < 64) if g == 0 else (lane >= 64)
    return jnp.concatenate([jnp.where(mine, x4[:, LANE * p:LANE * (p + 1)], 0).astype(dtype) for p in range(4)], axis=0)


def _swa_unstack(ref, r, lo, hi, scale=None):
    W = WINDOW
    low = _lane_iota((W, LANE)) < 64
    for p in range(4):
        t = jnp.where(low, lo[W * p:W * (p + 1)], hi[W * p:W * (p + 1)])
        ref[0, r, LANE * p:LANE * (p + 1)] = t if scale is None else scale * t


def _swa_fwd(qs3, ks3, vs3, posb, posr, sinks):
    B, S, _ = qs3.shape
    W = WINDOW
    nb = S // W
    nh = SWA_SEQ_SPLIT
    nbh = nb // nh

    def body(q_ref, k_ref, v_ref, pb_ref, pr_ref, sink_ref, o_ref, lse_ref):
        ij = lax.broadcasted_iota(jnp.int32, (W, 2 * W), 0) - lax.broadcasted_iota(jnp.int32, (W, 2 * W), 1)
        consts = _swa_consts(sink_ref)
        n0 = pl.program_id(1) * nbh

        def blk(i, _):
            n = n0 + i
            r, _, _, q4, k2, v2, dist4 = _swa_block(q_ref, k_ref, v_ref, pb_ref, pr_ref, n, i, ij)
            o_g = []
            for g, (slope, sink) in enumerate(consts):
                s = _dot_nt(_swa_stack(q4, g, q4.dtype), k2) - slope * dist4
                m = jnp.maximum(jnp.max(s, axis=1, keepdims=True), sink)
                e = jnp.exp2(s - _wide(m))
                l = jnp.sum(e, axis=1, keepdims=True) + jnp.exp2(sink - m)
                o_g.append(_dot(_mx(e), v2) * (1.0 / l))
                lse_ref[0, i, g] = m + jnp.log2(l)
            _swa_unstack(o_ref, r, o_g[0], o_g[1])
            return 0

        for i in range(nbh):
            blk(i, 0)

    seq = lambda w: pl.BlockSpec((1, S, w), lambda b, h: (b, 0, 0))
    part = lambda w: pl.BlockSpec((1, S // nh, w), lambda b, h: (b, h, 0))
    lse_spec = pl.BlockSpec((1, nbh, 2, 4 * W, LANE), lambda b, h: (b, h, 0, 0, 0))
    return pl.pallas_call(
        body, name="swa_fwd", grid=(B, nh),
        out_shape=[jax.ShapeDtypeStruct((B, S, 512), F32), jax.ShapeDtypeStruct((B, nb, 2, 4 * W, LANE), F32)],
        in_specs=[part(512), seq(LANE), seq(LANE), part(LANE), pl.BlockSpec((1, nb, W), lambda b, h: (b, 0, 0)),
                  pl.BlockSpec((1, LANE), lambda b, h: (0, 0))],
        out_specs=[part(512), lse_spec],
        compiler_params=_cparams(("arbitrary", "arbitrary")),
    )(qs3, ks3, vs3, posb, posr, sinks)


def _swa_bwd(qs3, ks3, vs3, posb, posr, sinks, os3, do3, lse):
    B, S, _ = qs3.shape
    W = WINDOW
    nb = S // W
    nh = SWA_SEQ_SPLIT
    nbh = nb // nh
    assert nbh % 2 == 0 and nbh * nh * W == S

    def body(q_ref, k_ref, v_ref, pb_ref, pr_ref, sink_ref, o_ref, do_ref, lse_ref, dq_ref, dk_ref, dv_ref, dsink_ref,
             dkt_ref, dvt_ref):
        lane1 = _lane_iota((1, LANE))
        ij = lax.broadcasted_iota(jnp.int32, (W, 2 * W), 0) - lax.broadcasted_iota(jnp.int32, (W, 2 * W), 1)
        consts = _swa_consts(sink_ref)
        hh = pl.program_id(1)
        n0 = hh * nbh

        @pl.when(hh == 0)
        def _():
            dkt_ref[...] = jnp.zeros_like(dkt_ref)
            dvt_ref[...] = jnp.zeros_like(dvt_ref)

        @pl.when(jnp.logical_and(pl.program_id(0) == 0, hh == 0))
        def _():
            dsink_ref[...] = jnp.zeros_like(dsink_ref)

        def blk(i, dsink):
            n = n0 + i
            r, kb, _, q4, k2, v2, dist4 = _swa_block(q_ref, k_ref, v_ref, pb_ref, pr_ref, n, i, ij)
            o4, do4 = o_ref[0, r, :], do_ref[0, r, :]
            dq_g = []
            dkt = jnp.zeros((LANE, 2 * W), F32)
            dvt = jnp.zeros((LANE, 2 * W), F32)
            for g, (slope, sink) in enumerate(consts):
                q_st = _swa_stack(q4, g, F32)
                do_st = _swa_stack(do4, g, F32)
                dcol = jnp.sum(do_st * _swa_stack(o4, g, F32), axis=1, keepdims=True)
                qst, dob = _mx(q_st), _mx(do_st)
                lse_c = lse_ref[0, i, g]
                pr = jnp.exp2(_dot_nt(qst, k2) - slope * dist4 - _wide(lse_c))
                dsb = _mx(pr * (_dot_nt(dob, v2) - dcol))
                psd = jnp.exp2(sink - lse_c)[:, 0:1] * dcol
                for p in range(4):
                    dsink = dsink - jnp.where(lane1 == p + 4 * g,
                                              jnp.sum(psd[W * p:W * (p + 1)], axis=0, keepdims=True), 0.0)
                dq_g.append(_dot(dsb, k2))
                dkt = dkt + _dot(_mx(q_st.T), dsb)
                dvt = dvt + _dot(_mx(do_st.T), _mx(pr))
            _swa_unstack(dq_ref, r, dq_g[0], dq_g[1], SWA_SCALE)
            dkt_ref[kb] += dkt[:, 0:W]
            dkt_ref[kb + 1] += dkt[:, W:2 * W]
            dvt_ref[kb] += dvt[:, 0:W]
            dvt_ref[kb + 1] += dvt[:, W:2 * W]
            return dsink

        dsink = jnp.zeros((1, LANE), F32)
        for i in range(nbh):
            dsink = blk(i, dsink)
        dsink_ref[...] += dsink

        @pl.when(hh == nh - 1)
        def _():
            def flush(n, _):
                r = pl.ds(pl.multiple_of(n * W, W), W)
                dk_ref[0, r, :] = (1.0 / LOG2E) * dkt_ref[n].T
                dv_ref[0, r, :] = dvt_ref[n].T
                return 0

            lax.fori_loop(0, nb, flush, 0)

    seq = lambda w: pl.BlockSpec((1, S, w), lambda b, h: (b, 0, 0))
    part = lambda w: pl.BlockSpec((1, S // nh, w), lambda b, h: (b, h, 0))
    return pl.pallas_call(
        body, name="swa_bwd", grid=(B, nh),
        out_shape=[jax.ShapeDtypeStruct((B, S, 512), F32), jax.ShapeDtypeStruct((B, S, LANE), F32),
                   jax.ShapeDtypeStruct((B, S, LANE), F32), jax.ShapeDtypeStruct((1, LANE), F32)],
        in_specs=[part(512), seq(LANE), seq(LANE), part(LANE), pl.BlockSpec((1, nb, W), lambda b, h: (b, 0, 0)),
                  pl.BlockSpec((1, LANE), lambda b, h: (0, 0)), part(512),
                  pl.BlockSpec((1, S // nh, 512), lambda b, h: (b, h, 1)),
                  pl.BlockSpec((1, nbh, 2, 4 * W, LANE), lambda b, h: (b, h, 0, 0, 0))],
        out_specs=[part(512), seq(LANE), seq(LANE), pl.BlockSpec((1, LANE), lambda b, h: (0, 0))],
        scratch_shapes=[pltpu.VMEM((nb, LANE, W), F32), pltpu.VMEM((nb, LANE, W), F32)],
        compiler_params=_cparams(("arbitrary", "arbitrary")),
    )(qs3, ks3, vs3, posb, posr, sinks, os3, do3, lse)


def _post(om, osw, g, w_out, x2, gate, fg, tgt, S, ts):
    T, D = x2.shape
    nsb = S // ts

    def body(om_ref, os_ref, g_ref, w_ref, x_ref, gate_ref, fg_ref, t_ref,
             dx2_ref, do_ref, dg_ref, loss_ref, dfg_ref, dgate_ref, dw_ref):
        i = pl.program_id(0)
        gv = g_ref[...]
        sg = 1.0 / (1.0 + jnp.exp(-gv))
        silu = gv * sg
        o = jnp.concatenate([om_ref[...], os_ref[...]], axis=1)
        ab = _mx(o * silu)
        y = _dot(ab, w_ref[...])
        gate = gate_ref[0]
        xo = x_ref[...] + gate * y
        r2 = lax.rsqrt(jnp.mean(xo * xo, axis=-1, keepdims=True) + EPS)
        xh = xo * r2
        fg = fg_ref[...]
        diff = xh * fg - t_ref[...]
        sq = jnp.sum(diff * diff, axis=0, keepdims=True)
        part = sq[:, 0:LANE]
        for t in range(1, D // LANE):
            part = part + sq[:, LANE * t:LANE * (t + 1)]
        dout = diff * (1.0 / D)
        dxh = dout * fg
        dx2 = r2 * (dxh - xh * jnp.mean(dxh * xh, axis=-1, keepdims=True))
        dx2_ref[...] = dx2
        dyb = _mx(dx2 * gate)
        da = _dot_nt(dyb, w_ref[...])
        do_ref[...] = da * silu
        dg_ref[...] = _mx(da * o * (sg * (1.0 + gv * (1.0 - sg))))

        @pl.when(i == 0)
        def _():
            loss_ref[...] = jnp.zeros_like(loss_ref)
            dfg_ref[...] = jnp.zeros_like(dfg_ref)
            dw_ref[...] = jnp.zeros_like(dw_ref)

        @pl.when(i % nsb == 0)
        def _():
            dgate_ref[...] = jnp.zeros_like(dgate_ref)

        loss_ref[...] += (0.5 / D) * part
        dfg_ref[...] += jnp.sum(dout * xh, axis=0, keepdims=True)
        dgate_ref[0] += jnp.sum(dx2 * y, axis=0, keepdims=True)
        dw_ref[...] += _dot_tn(ab, dyb)

        @pl.when(i == T // ts - 1)
        def _():
            chunks = [dw_ref[512 + 64 * PAIR_INV[h]:512 + 64 * (PAIR_INV[h] + 1), :] for h in range(SWA_HEADS)]
            for h in range(SWA_HEADS):
                dw_ref[512 + 64 * h:512 + 64 * (h + 1), :] = chunks[h]

    row = lambda w: pl.BlockSpec((ts, w), lambda i: (i, 0))
    full = lambda a: pl.BlockSpec(a.shape, lambda i: (0,) * a.ndim)
    per_b = pl.BlockSpec((1, 1, D), lambda i: (i // nsb, 0, 0))
    return pl.pallas_call(
        body, name="post", grid=(T // ts,),
        out_shape=[jax.ShapeDtypeStruct((T, D), F32), jax.ShapeDtypeStruct((T, 1024), F32),
                   jax.ShapeDtypeStruct((T, 1024), _MXU_DTYPE), jax.ShapeDtypeStruct((1, LANE), F32),
                   jax.ShapeDtypeStruct((1, D), F32), jax.ShapeDtypeStruct(gate.shape, F32),
                   jax.ShapeDtypeStruct(w_out.shape, F32)],
        in_specs=[row(512), row(512), row(1024), full(w_out), row(D), per_b, full(fg), row(D)],
        out_specs=[row(D), row(1024), row(1024),
                   pl.BlockSpec((1, LANE), lambda i: (0, 0)), pl.BlockSpec((1, D), lambda i: (0, 0)), per_b,
                   full(w_out)],
        compiler_params=_cparams(("arbitrary",)),
    )(om, osw, g, w_out, x2, gate, fg, tgt)


def _pre_bwd(dq, dk, dv, dqs, dks, dvs, dg, zq, zkv, ql, kvl, x2, dx2, scale, ng, w_in, gq, gkv, w_uq, w_uk, w_uv,
             rope, S, ts):
    T, D = x2.shape
    nsb = S // ts
    WQ = MLA_HEADS * LANE

    def body(dq_ref, dk_ref, dv_ref, dqs_ref, dks_ref, dvs_ref, dg_ref, zq_ref, zkv_ref, ql_ref, kvl_ref, x_ref, dx2_ref,
             sc_ref,
             ng_ref, w_ref, gq_ref, gkv_ref, wuq_ref, wuk_ref, wuv_ref, cd_ref, sd_ref,
             gx_ref, dz_ref, dgq_ref, dgkv_ref, dng_ref, dsh_ref, dsc_ref, dwuq_ref, dwuk_ref, dwuv_ref):
        i = pl.program_id(0)

        @pl.when(i == 0)
        def _():
            dgq_ref[...] = jnp.zeros_like(dgq_ref)
            dgkv_ref[...] = jnp.zeros_like(dgkv_ref)
            dng_ref[...] = jnp.zeros_like(dng_ref)
            dwuq_ref[...] = jnp.zeros_like(dwuq_ref)
            dwuk_ref[...] = jnp.zeros_like(dwuk_ref)
            dwuv_ref[...] = jnp.zeros_like(dwuv_ref)

        @pl.when(i % nsb == 0)
        def _():
            dsh_ref[...] = jnp.zeros_like(dsh_ref)
            dsc_ref[...] = jnp.zeros_like(dsc_ref)

        def norm_bwd(z, dy, gain):
            r = lax.rsqrt(jnp.mean(z * z, axis=-1, keepdims=True) + EPS)
            zh = z * r
            dzh = dy * gain
            return r * (dzh - zh * jnp.mean(dzh * zh, axis=-1, keepdims=True)), jnp.sum(dy * zh, axis=0, keepdims=True)

        tables = _rope_split(cd_ref[...], sd_ref[...])
        ng = ng_ref[...]
        sc1 = 1.0 + sc_ref[0]

        def rows_chain(rs):
            cq, ck, sa, sb = (t[rs] for t in tables)
            dqr = dq_ref[rs, :].astype(F32)
            dqb = _mx(dqr * _tile_heads(cq) + pltpu.roll(dqr * _tile_heads(sa), 16, 1)
                      + pltpu.roll(dqr * _tile_heads(sb), WQ - 16, 1))
            p_uq = _dot_tn(ql_ref[rs, :], dqb)
            dzq, dgq = norm_bwd(zq_ref[rs, :], _dot_nt(dqb, wuq_ref[...]), gq_ref[...])
            dkr = dk_ref[rs, :].astype(F32)
            dkb = _mx(dkr)
            p_uk = _dot_tn(kvl_ref[rs, :], dkb)
            dvb = _mx(dv_ref[rs, :])
            p_uv = _dot_tn(kvl_ref[rs, :], dvb)
            dzkv, dgkv = norm_bwd(zkv_ref[rs, :], _dot_nt(dkb, wuk_ref[...]) + _dot_nt(dvb, wuv_ref[...]), gkv_ref[...])
            dkpe = dkr[:, 0:LANE]
            for h in range(1, MLA_HEADS):
                dkpe = dkpe + dkr[:, LANE * h:LANE * (h + 1)]
            dkro = dkpe * ck + pltpu.roll(dkpe * sa, 16, 1) + pltpu.roll(dkpe * sb, LANE - 16, 1)
            dz_ref[rs, 0:384] = _mx(dzq)
            dz_ref[rs, 384:640] = _mx(dzkv)
            dz_ref[rs, 640:768] = _mx(dkro)
            dz_ref[rs, 768:1280] = dg_ref[rs, 0:512]
            dz_ref[rs, 1280:1792] = _mx(dqs_ref[rs, :])
            dz_ref[rs, 1792:1920] = _mx(dks_ref[rs, :])
            dz_ref[rs, 1920:2048] = _mx(dvs_ref[rs, :])
            dz_ref[rs, 2048:2560] = dg_ref[rs, 512:1024]
            dh = _dot(dz_ref[rs, :], w_ref[...])
            x = x_ref[rs, :]
            r1 = lax.rsqrt(jnp.mean(x * x, axis=-1, keepdims=True) + EPS)
            xn = x * r1
            dxn = dh * ng * sc1
            gx_ref[rs, :] = dx2_ref[rs, :] + r1 * (dxn - xn * jnp.mean(dxn * xn, axis=-1, keepdims=True))
            return (p_uq, p_uk, p_uv, dgq, dgkv, jnp.sum(dh, axis=0, keepdims=True),
                    jnp.sum(dh * (xn * ng), axis=0, keepdims=True), jnp.sum(dh * xn * sc1, axis=0, keepdims=True))

        hr = ts // 2
        parts = [rows_chain(slice(hr * t, hr * (t + 1))) for t in range(2)]
        p_uq, p_uk, p_uv, dgq, dgkv, dsh, dsc, dng = (a + b for a, b in zip(*parts))
        dwuq_ref[...] += p_uq
        dwuk_ref[...] += p_uk
        dwuv_ref[...] += p_uv
        dgq_ref[...] += dgq
        dgkv_ref[...] += dgkv
        dsh_ref[0] += dsh
        dsc_ref[0] += dsc
        dng_ref[...] += dng

    row = lambda w: pl.BlockSpec((ts, w), lambda i: (i, 0))
    full = lambda a: pl.BlockSpec(a.shape, lambda i: (0,) * a.ndim, pipeline_mode=pl.Buffered(1))
    per_b = pl.BlockSpec((1, 1, D), lambda i: (i // nsb, 0, 0))
    dense = pl.BlockSpec((ts // 8, LANE), lambda i: (i, 0))
    vec = lambda w: pl.BlockSpec((1, w), lambda i: (0, 0))
    return pl.pallas_call(
        body, name="pre_bwd", grid=(T // ts,),
        out_shape=[jax.ShapeDtypeStruct((T, D), F32), jax.ShapeDtypeStruct((T, D_IN_PAD), _MXU_DTYPE), jax.ShapeDtypeStruct((1, 384), F32),
                   jax.ShapeDtypeStruct((1, 256), F32), jax.ShapeDtypeStruct((1, D), F32),
                   jax.ShapeDtypeStruct(scale.shape, F32), jax.ShapeDtypeStruct(scale.shape, F32),
                   jax.ShapeDtypeStruct(w_uq.shape, F32), jax.ShapeDtypeStruct(w_uk.shape, F32),
                   jax.ShapeDtypeStruct(w_uv.shape, F32)],
        in_specs=[row(WQ), row(WQ), row(512), row(512), row(LANE), row(LANE), row(1024), row(384), row(256), row(384),
                  row(256), row(D), row(D), per_b, full(ng), full(w_in), full(gq), full(gkv), full(w_uq), full(w_uk), full(w_uv),
                  dense, dense],
        out_specs=[row(D), row(D_IN_PAD), vec(384), vec(256), vec(D), per_b, per_b, full(w_uq), full(w_uk), full(w_uv)],
        compiler_params=_cparams(("arbitrary",)),
    )(dq, dk, dv, dqs, dks, dvs, dg, zq, zkv, ql, kvl, x2, dx2, scale, ng, w_in, gq, gkv, w_uq, w_uk, w_uv, *rope)


def _w_in_row_runs():
    runs = [(0, 0, 640), (640, 704, 32), (672, 768, 512)]
    runs += [(1184 + 64 * h, 1280 + 64 * PAIR_INV[h], 64) for h in range(8)]
    runs += [(1696, 1792, 256)]
    runs += [(1952 + 64 * h, 2048 + 64 * PAIR_INV[h], 64) for h in range(8)]
    return runs


W_IN_SHARD = D_IN // N_DEV
W_IN_SLOT = 320


def _dw_in_t(dz, hb, tn, tk):
    T, M = dz.shape
    N = hb.shape[1]
    nk = T // tk

    def body(a_ref, b_ref, o_ref, ob_ref, acc_ref):
        k = pl.program_id(1)

        @pl.when(k == 0)
        def _():
            acc_ref[...] = jnp.zeros_like(acc_ref)

        acc_ref[...] += _dot_tn(a_ref[...], b_ref[...])

        @pl.when(k == nk - 1)
        def _():
            for d in range(N_DEV):
                lo, hi = W_IN_SHARD * d, W_IN_SHARD * (d + 1)
                for nat, pad, size in _w_in_row_runs():
                    a, b = max(nat, lo), min(nat + size, hi)
                    if a < b:
                        piece = acc_ref[pad + a - nat:pad + b - nat, :]
                        o_ref[d, a - lo:b - lo, :] = piece
                        ob_ref[d, a - lo:b - lo, :] = piece.astype(ob_ref.dtype)
                o_ref[d, W_IN_SHARD:W_IN_SLOT, :] = jnp.zeros((W_IN_SLOT - W_IN_SHARD, tn), F32)
                ob_ref[d, W_IN_SHARD:W_IN_SLOT, :] = jnp.zeros((W_IN_SLOT - W_IN_SHARD, tn), ob_ref.dtype)

    slots = pl.BlockSpec((N_DEV, W_IN_SLOT, tn), lambda j, k: (0, 0, j))
    return pl.pallas_call(
        body, name="dw_in", grid=(N // tn, nk),
        out_shape=[jax.ShapeDtypeStruct((N_DEV, W_IN_SLOT, N), F32), jax.ShapeDtypeStruct((N_DEV, W_IN_SLOT, N), jnp.bfloat16)],
        in_specs=[pl.BlockSpec((tk, M), lambda j, k: (k, 0)), pl.BlockSpec((tk, tn), lambda j, k: (k, j))],
        out_specs=[slots, slots],
        scratch_shapes=[pltpu.VMEM((M, tn), F32)],
        compiler_params=_cparams(("arbitrary", "arbitrary")),
    )(dz, hb)


SHARDED = (3, 6, 7, 9)
PART_SLICES = {10: (128, 1152), 2: (1152, 2176), 4: (2176, 2560), 5: (2560, 2816), 8: (2816, 2824)}


def _finalize_adamw(parts_all, dmod_all, dmod_cols, c_all, ws, sharded_grads, ms, vs):
    n = len(ws)
    local = [t for t in range(n) if t not in SHARDED[1:]]

    def body(*refs):
        p_ref, dm_ref, dmc_ref, c_ref = refs[:4]
        w_refs, gs_refs, m_refs, v_refs = refs[4:4 + n], refs[4 + n:8 + n], refs[8 + n:8 + 2 * n], refs[8 + 2 * n:8 + 3 * n]
        outs = refs[8 + 3 * n:]
        loss_ref, gl_refs = outs[0], outs[1:1 + len(local)]
        d_refs, nm_refs, nv_refs = (outs[1 + len(local) + t * n:1 + len(local) + (t + 1) * n] for t in range(3))
        ps_ref = outs[-1]
        rd = lambda r: r[:, 0, :] if len(r.shape) == 3 else r[...]

        def wr(r, val):
            if len(r.shape) == 3:
                r[:, 0, :] = val
            else:
                r[...] = val

        acc = p_ref[0]
        for j in range(1, N_DEV):
            acc = acc + p_ref[j]
        ps_ref[...] = acc
        loss_ref[...] = jnp.sum(acc[:, 0:LANE], axis=1, keepdims=True)
        db = dm_ref[0:1, :]
        for j in range(1, dm_ref.shape[0]):
            db = db + dm_ref[j:j + 1, :]
        grads = {0: _dot_tn(_mx(_silu(c_ref[...])), _mx(dmc_ref[...])), 1: db}
        for t, (lo, hi) in PART_SLICES.items():
            grads[t] = ps_ref[:, lo:hi]
        for i, t in enumerate(SHARDED):
            grads[t] = gs_refs[i][...]
        grads[SHARDED[0]] = gs_refs[0][0:W_IN_SHARD, :]
        for i, t in enumerate(local):
            wr(gl_refs[i], grads[t])
        for t in range(n):
            gv = grads[t]
            nm = ADAM_B1 * rd(m_refs[t]) + (1.0 - ADAM_B1) * gv
            nv = ADAM_B2 * rd(v_refs[t]) + (1.0 - ADAM_B2) * (gv * gv)
            m_hat = nm / (1.0 - ADAM_B1 ** ADAM_STEP)
            v_hat = nv / (1.0 - ADAM_B2 ** ADAM_STEP)
            wr(d_refs[t], -ADAM_LR * (m_hat / (jnp.sqrt(v_hat) + ADAM_EPS) + ADAM_WD * rd(w_refs[t])))
            wr(nm_refs[t], nm)
            wr(nv_refs[t], nv)

    like = lambda arrs: [jax.ShapeDtypeStruct(a.shape, F32) for a in arrs]
    out = pl.pallas_call(
        body, name="finalize_adamw",
        out_shape=[jax.ShapeDtypeStruct((1, 1), F32)] + like([ws[t] for t in local]) + like(ws) * 3,
        in_specs=[_vmem()] * (8 + 3 * n), out_specs=[_vmem()] * (1 + len(local) + 3 * n),
        scratch_shapes=[pltpu.VMEM(parts_all.shape[1:], F32)],
        compiler_params=_cparams(),
    )(parts_all, dmod_all, dmod_cols, c_all, *ws, *sharded_grads, *ms, *vs)
    k = 1 + len(local)
    return out[0], dict(zip(local, out[1:k])), out[k:k + n], out[k + n:k + 2 * n], out[k + 2 * n:]


def _pair_perm(a, axis, order):
    a = jnp.moveaxis(a, axis, -1)
    lead = a.shape[:-1]
    a = a.reshape(lead + (8, 64))[..., list(order), :].reshape(lead + (512,))
    return jnp.moveaxis(a, -1, axis)


def _rope_table(positions):
    T = positions.size
    inv = ROPE_THETA ** (-jnp.arange(0, MLA_ROPE, 2, dtype=F32) / MLA_ROPE)
    pos = jnp.repeat(positions.reshape(T // 8, 8)[:, ::-1].astype(F32), MLA_ROPE // 2, axis=1)
    ang = pos * jnp.tile(inv, 8)[None, :]
    return jnp.cos(ang), jnp.sin(ang)


def _local_step(x, mod, positions, ng, w_in_t, gq, gkv, w_uq, w_ukv, sinks, w_out, fg, tgt,
                ts=512, fq=512, fk=512, bq=512, bk=512):
    B, S, D = x.shape
    T = B * S
    x2 = x.reshape(T, D)
    shift, scale, gate = (mod[:, None, k * D:(k + 1) * D] for k in range(3))
    w_uq_p = jnp.pad(w_uq.reshape(Q_LORA, MLA_HEADS, 96), ((0, 0), (0, 0), (0, 32))).reshape(Q_LORA, MLA_HEADS * LANE)
    w_ukv3 = w_ukv.reshape(KV_LORA, MLA_HEADS, 128)
    w_uk_p = jnp.pad(w_ukv3[:, :, :64], ((0, 0), (0, 0), (0, 64))).reshape(KV_LORA, MLA_HEADS * LANE)
    w_uv = w_ukv3[:, :, 64:].reshape(KV_LORA, 512)
    w_out_p = jnp.concatenate([w_out[:512], _pair_perm(w_out[512:], 0, PAIR_ORDER)], axis=0)
    rope = _rope_table(positions)
    posf = positions.astype(F32)
    posb = jnp.broadcast_to(posf[:, :, None], (B, S, LANE))
    posr = posf.reshape(B, S // WINDOW, WINDOW)
    sinks_l = jnp.pad(sinks.reshape(1, SWA_HEADS), ((0, 0), (0, LANE - SWA_HEADS)))

    (hb, zq, zkv, ql, kvl, q, k, v, qs, ks, vs, g, w_in_p) = _pre_fwd(
        x2, shift, scale, ng, w_in_t, gq, gkv, w_uq_p, w_uk_p, w_uv, rope, S, ts)
    r3 = lambda a: a.reshape(B, S, a.shape[-1])
    om, lse_m = _mla_fwd(r3(q), r3(k), r3(v), fq, fk)
    osw, lse_s = _swa_fwd(r3(qs), r3(ks), r3(vs), posb, posr, sinks_l)
    dx2, do, dg, loss_v, dfg, dgate, dw_out = _post(
        om.reshape(T, 512), osw.reshape(T, 512), g, w_out_p, x2, gate, fg.reshape(1, D), tgt.reshape(T, D), S, ts)
    do3 = r3(do)
    dq, dk, dv = _mla_bwd(r3(q), r3(k), r3(v), om, do3, lse_m, bq, bk)
    dqs, dks, dvs, dsink = _swa_bwd(r3(qs), r3(ks), r3(vs), posb, posr, sinks_l, osw, do3, lse_s)
    f2 = lambda a: a.reshape(T, a.shape[-1])
    gx, dz, dgq, dgkv, dng, dsh, dsc, dw_uq_p, dw_uk_p, dw_uv = _pre_bwd(
        f2(dq), f2(dk), f2(dv), f2(dqs), f2(dks), f2(dvs), dg, zq, zkv, ql, kvl, x2, dx2, scale, ng, w_in_p, gq, gkv,
        w_uq_p, w_uk_p, w_uv, rope, S, ts)
    tk = min(T, 1024)
    dw_in_t = _dw_in_t(dz, hb, 512, tk)
    dw_uq = dw_uq_p.reshape(Q_LORA, MLA_HEADS, LANE)[:, :, :96].reshape(Q_LORA, 768)
    dw_uk = dw_uk_p.reshape(KV_LORA, MLA_HEADS, LANE)[:, :, :64]
    dw_uv = dw_uv.reshape(KV_LORA, MLA_HEADS, 64)
    dw_ukv = jnp.concatenate([dw_uk, dw_uv], axis=2).reshape(KV_LORA, 1024)
    parts = jnp.concatenate([loss_v, dfg, dng, dgq, dgkv, dsink], axis=1)
    dmod = jnp.concatenate([dsh, dsc, dgate], axis=2).reshape(B, 3 * D)
    return gx.reshape(B, S, D), dw_in_t, dw_uq, dw_ukv, dw_out, parts, dmod


def kernel(x, c, positions, w_ada, b_ada, norm_gain, w_in, q_norm_gain, kv_norm_gain, w_uq, w_ukv, swa_sinks, w_out, final_gain, loss_target, m_w_ada, m_b_ada, m_norm_gain, m_w_in, m_q_norm_gain, m_kv_norm_gain, m_w_uq, m_w_ukv, m_swa_sinks, m_w_out, m_final_gain, v_w_ada, v_b_ada, v_norm_gain, v_w_in, v_q_norm_gain, v_kv_norm_gain, v_w_uq, v_w_ukv, v_swa_sinks, v_w_out, v_final_gain):
    B, S, D = x.shape
    me = 4 * lax.axis_index("x") + 2 * lax.axis_index("y") + lax.axis_index("c")
    bf = _MXU_DTYPE

    ncol = w_ada.shape[2]
    b_cols = lax.dynamic_slice_in_dim(b_ada, me * ncol, ncol, axis=1)
    c_all, win_g, wuq_g, wukv_g, wout_g, mod_g = _all_gather(
        [c, w_in[0].T.astype(bf), w_uq[0].astype(bf), w_ukv[0].astype(bf), w_out[0].astype(bf)], "ag_weights",
        fused=((w_ada[0], b_cols), _ada_cols, jax.ShapeDtypeStruct((N_DEV, B, ncol), F32)))
    c_all = c_all.reshape(N_DEV * B, D)
    cat_cols = lambda a: jnp.transpose(a, (1, 0, 2)).reshape(a.shape[1], N_DEV * a.shape[2])
    w_in_t, w_uq_f, w_ukv_f = win_g, cat_cols(wuq_g), cat_cols(wukv_g)
    w_out_f = wout_g.reshape(D, D)

    mod = lax.dynamic_index_in_dim(mod_g, me, axis=1, keepdims=False)
    mod = jnp.transpose(mod, (1, 0, 2)).reshape(B, 3 * D)

    gx, dw_in_t, dw_uq, dw_ukv, dw_out, parts, dmod = _local_step(
        x, mod, positions, norm_gain, w_in_t, q_norm_gain, kv_norm_gain, w_uq_f, w_ukv_f, swa_sinks,
        w_out_f, final_gain, loss_target)

    split_cols = lambda a: jnp.transpose(a.reshape(a.shape[0], 4, 2, a.shape[1] // N_DEV), (1, 2, 0, 3))
    split_rows = lambda a: a.reshape(4, 2, a.shape[0] // N_DEV, a.shape[1])
    slots = lambda a: a.reshape((4, 2) + a.shape[1:])
    g_w_in_t, g_w_uq, g_w_ukv, g_w_out, parts_g, dmod_g = _reduce_scatter(
        [(slots(dw_in_t[0]), slots(dw_in_t[1])), split_cols(dw_uq), split_cols(dw_ukv), split_rows(dw_out)], "rs_grads",
        gather=(parts, dmod))

    dmod_all = dmod_g.reshape(N_DEV * B, 3 * D)
    dmod_cols = lax.dynamic_slice_in_dim(dmod_all, me * ncol, ncol, axis=1)
    ws = [w_ada, b_ada, norm_gain, w_in, q_norm_gain, kv_norm_gain, w_uq, w_ukv, swa_sinks, w_out, final_gain]
    ms = [m_w_ada, m_b_ada, m_norm_gain, m_w_in, m_q_norm_gain, m_kv_norm_gain, m_w_uq, m_w_ukv, m_swa_sinks, m_w_out,
          m_final_gain]
    vs = [v_w_ada, v_b_ada, v_norm_gain, v_w_in, v_q_norm_gain, v_kv_norm_gain, v_w_uq, v_w_ukv, v_swa_sinks, v_w_out,
          v_final_gain]
    two_d = lambda a: a.reshape((1, a.shape[0]) if a.ndim == 1 else a.shape[-2:])
    flat = lambda arrs: [jnp.transpose(a, (2, 0, 1)) if t == SHARDED[0] else two_d(a) for t, a in enumerate(arrs)]
    sharded = {3: g_w_in_t, 6: g_w_uq, 7: g_w_ukv, 9: g_w_out}
    loss, local_grads, deltas, new_ms, new_vs = _finalize_adamw(
        parts_g, dmod_all, dmod_cols, c_all, flat(ws), [sharded[t] for t in SHARDED], flat(ms), flat(vs))
    grads = [local_grads[t] if t in local_grads else sharded[t] for t in range(len(ws))]
    shaped = lambda arrs: [jnp.transpose(a, (1, 2, 0)) if t == SHARDED[0] else a.reshape(ws[t].shape) for t, a in enumerate(arrs)]
    return (loss.reshape(()), gx, *shaped(grads), *shaped(deltas), *shaped(new_ms), *shaped(new_vs))
```

```python
import functools

import jax
import jax.numpy as jnp
from jax import lax
from jax.experimental import pallas as pl
from jax.experimental.pallas import tpu as pltpu

F32 = jnp.float32
_MXU_DTYPE = jnp.bfloat16
MLA_GRAD_DTYPE = jnp.bfloat16

N_DEV = 8
MLA_HEADS = 8
MLA_NOPE = 64
MLA_ROPE = 32
Q_LORA = 384
KV_LORA = 256
SWA_HEADS = 8
SWA_KV_HEADS = 2
SWA_HEAD_DIM = 64
WINDOW = 128
ROPE_THETA = 10000.0
EPS = 1e-6
MLA_SCALE = float((MLA_NOPE + MLA_ROPE) ** -0.5)
SWA_SCALE = float(SWA_HEAD_DIM ** -0.5)
LOG2E = 1.4426950408889634
MLA_QSCALE = MLA_SCALE * LOG2E
D_IN = 2464
D_IN_PAD = 2560
PAIR_ORDER = (0, 4, 1, 5, 2, 6, 3, 7)
PAIR_INV = (0, 2, 4, 6, 1, 3, 5, 7)

ADAM_LR = 0.001
ADAM_B1 = 0.9
ADAM_B2 = 0.999
ADAM_EPS = 1e-08
ADAM_WD = 0.01
ADAM_STEP = 10

LANE = 128
VMEM_LIMIT = 56 * 1024 * 1024

MESH = pl.DeviceIdType.MESH
NEG_INF = float("-inf")
SWA_SEQ_SPLIT = 2


def _mx(a):
    return a.astype(_MXU_DTYPE)


def _dot(a, b):
    return jnp.dot(a, b, preferred_element_type=F32)


def _dot_nt(a, b):
    return lax.dot_general(a, b, (((1,), (1,)), ((), ())), preferred_element_type=F32)


def _dot_tn(a, b):
    return lax.dot_general(a, b, (((0,), (0,)), ((), ())), preferred_element_type=F32)


def _cparams(sem=None):
    return pltpu.CompilerParams(dimension_semantics=sem, vmem_limit_bytes=VMEM_LIMIT)


def _vmem():
    return pl.BlockSpec(memory_space=pltpu.VMEM)


def _lane_iota(shape):
    return lax.broadcasted_iota(jnp.int32, shape, len(shape) - 1)


def _gather_program(srcs, outs, send_sems, recv_sems, local_sems):
    x, y, c = lax.axis_index("x"), lax.axis_index("y"), lax.axis_index("c")
    me, sibling = (x, y, c), (x, y, 1 - c)
    chips = [(1 - x, y), (x, 1 - y), (1 - x, 1 - y)]

    def slot(a, dev):
        return outs[a].at[4 * dev[0] + 2 * dev[1] + dev[2]]

    def copy(a, k, block, to, src=None):
        return pltpu.make_async_remote_copy(
            src_ref=slot(a, block) if src is None else src, dst_ref=slot(a, block),
            send_sem=send_sems.at[7 * a + k], recv_sem=recv_sems.at[7 * a + k],
            device_id=to, device_id_type=MESH)

    def local(a):
        return pltpu.make_async_copy(srcs[a], slot(a, me), local_sems.at[a])

    def start(a):
        local(a).start()
        cps = [copy(a, 0, me, sibling, src=srcs[a])]
        cps += [copy(a, 1 + j, me, (*chip, c), src=srcs[a]) for j, chip in enumerate(chips)]
        for cp in cps:
            cp.start()
        return cps

    def finish(group):
        cps = []
        for j, chip in enumerate(chips):
            for a in group:
                copy(a, 1 + j, (*chip, c), me).wait_recv()
                cp = copy(a, 4 + j, (*chip, c), sibling)
                cp.start()
                cps.append(cp)
        for a in group:
            copy(a, 0, sibling, me).wait_recv()
            for j, chip in enumerate(chips):
                copy(a, 4 + j, (*chip, 1 - c), me).wait_recv()
            local(a).wait()
        return cps

    return start, finish


def _gather_sems(m):
    return [pltpu.SemaphoreType.DMA((7 * m,)), pltpu.SemaphoreType.DMA((7 * m,)), pltpu.SemaphoreType.DMA((m,))]


def _all_gather(arrs, name, fused=None):
    n = len(arrs)
    cast = {a: e[1] for a, e in enumerate(arrs) if isinstance(e, tuple)}
    arrs = [e[0] if isinstance(e, tuple) else e for e in arrs]
    shapes = [(a.shape[0], a.shape[-1]) for a in arrs]
    extra, fn, piece = fused if fused else ((), None, None)
    ne, m = len(extra), n + (1 if fused else 0)

    def body(*refs):
        ins, ex, outs = refs[:n], refs[n:n + ne], refs[n + ne:n + ne + m]
        rest = refs[n + ne + m:]
        staged = dict(zip(cast, rest))
        srcs = [staged.get(a, ins[a]) for a in range(n)] + ([rest[len(cast)]] if fused else [])
        start, finish = _gather_program(srcs, outs, *rest[-3:])
        pending = []
        for a in range(n):
            if a in staged:
                staged[a][...] = (ins[a][:, 0, :] if len(ins[a].shape) == 3 else ins[a][...]).astype(cast[a])
            pending += start(a)
        if fused:
            pending += finish([0])
            fn(srcs[n], outs[0], *ex)
            pending += start(n)
            pending += finish([n] + list(range(1, n)))
        else:
            pending += finish(list(range(n)))
        for cp in pending:
            cp.wait_send()

    out_shape = [jax.ShapeDtypeStruct((N_DEV,) + shapes[a], cast.get(a, arrs[a].dtype)) for a in range(n)]
    scratch = [pltpu.VMEM(shapes[a], cast[a]) for a in cast]
    if fused:
        out_shape.append(jax.ShapeDtypeStruct((N_DEV,) + piece.shape, piece.dtype))
        scratch.append(pltpu.VMEM(piece.shape, piece.dtype))
    return pl.pallas_call(
        body, name=name, out_shape=out_shape,
        in_specs=[_vmem()] * (n + ne), out_specs=[_vmem()] * m,
        scratch_shapes=scratch + _gather_sems(m),
        compiler_params=pltpu.CompilerParams(vmem_limit_bytes=VMEM_LIMIT),
    )(*arrs, *extra)


def _reduce_scatter(arrs, name, gather=()):
    n, g = len(arrs), len(gather)
    halves = [a[1] if isinstance(a, tuple) else a.astype(jnp.bfloat16) for a in arrs]
    arrs = [a[0] if isinstance(a, tuple) else a for a in arrs]

    def body(*refs):
        xs, xbs, gins = refs[:n], refs[n:2 * n], refs[2 * n:2 * n + g]
        outs, gouts = refs[2 * n + g:3 * n + g], refs[3 * n + g:3 * n + 2 * g]
        rest = refs[3 * n + 2 * g:]
        parts, recv_a, send_b, recv_b = (rest[t * n:(t + 1) * n] for t in range(4))
        send_sems, recv_sems, local_sems = rest[4 * n:4 * n + 3]
        x, y, c = lax.axis_index("x"), lax.axis_index("y"), lax.axis_index("c")
        myq = 2 * x + y
        gather_start, gather_finish = _gather_program(gins, gouts, *rest[4 * n + 3:])

        def chip(k):
            return (1 - x if k & 2 else x, 1 - y if k & 1 else y)

        def from_sibling(a):
            return pltpu.make_async_remote_copy(
                src_ref=xbs[a].at[:, 1 - c], dst_ref=recv_a[a], send_sem=send_sems.at[4 * a], recv_sem=recv_sems.at[4 * a],
                device_id=(x, y, 1 - c), device_id_type=MESH)

        def to_owner(a, k):
            qx, qy = chip(k)
            return pltpu.make_async_remote_copy(
                src_ref=send_b[a].at[2 * qx + qy], dst_ref=recv_b[a].at[myq],
                send_sem=send_sems.at[4 * a + k], recv_sem=recv_sems.at[4 * a + k],
                device_id=(qx, qy, c), device_id_type=MESH)

        mine = [pltpu.make_async_copy(xs[a].at[:, c], parts[a], local_sems.at[a]) for a in range(n)]
        first = [from_sibling(a) for a in range(n)]
        for cp in mine + first:
            cp.start()
        second = []
        for b in range(g):
            second += gather_start(b)
        for a in range(n):
            mine[a].wait()
            first[a].wait_recv()
            parts[a][...] = parts[a][...] + recv_a[a][...].astype(F32)
            send_b[a][...] = parts[a][...].astype(jnp.bfloat16)
            for k in range(1, 4):
                cp = to_owner(a, k)
                cp.start()
                second.append(cp)
        second += gather_finish(list(range(g)))
        for a in range(n):
            acc = parts[a][myq]
            for k in range(1, 4):
                to_owner(a, k).wait_recv()
                qx, qy = chip(k)
                acc = acc + recv_b[a][2 * qx + qy].astype(F32)
            outs[a][...] = acc
        for cp in first + second:
            cp.wait_send()

    quarter = lambda a, dt: pltpu.VMEM((4,) + a.shape[2:], dt)
    return pl.pallas_call(
        body, name=name,
        out_shape=[jax.ShapeDtypeStruct(a.shape[2:], F32) for a in arrs]
        + [jax.ShapeDtypeStruct((N_DEV,) + a.shape, a.dtype) for a in gather],
        in_specs=[pl.BlockSpec(memory_space=pl.ANY)] * (2 * n) + [_vmem()] * g, out_specs=[_vmem()] * (n + g),
        scratch_shapes=[quarter(a, F32) for a in arrs] + [quarter(a, jnp.bfloat16) for a in arrs] * 3
        + [pltpu.SemaphoreType.DMA((4 * n,)), pltpu.SemaphoreType.DMA((4 * n,)), pltpu.SemaphoreType.DMA((n,))]
        + _gather_sems(g),
        compiler_params=pltpu.CompilerParams(vmem_limit_bytes=VMEM_LIMIT),
    )(*arrs, *halves, *gather)


def _silu(t):
    return t * (1.0 / (1.0 + jnp.exp(-t)))


def _ada_cols(piece_ref, c_all_ref, w_ref, b_ref):
    w = _mx(w_ref[...])
    for d in range(N_DEV):
        piece_ref[d] = _dot(_mx(_silu(c_all_ref[d])), w) + b_ref[...]


def _rope_split(cd, sd):
    n = cd.shape[0]

    def expand(d):
        rep = jnp.broadcast_to(d[:, None, :], (n, 8, LANE)).reshape(8 * n, LANE)
        return pltpu.roll(rep, 0, 1, stride=16, stride_axis=0)

    c, s = expand(cd), expand(sd)
    lane = _lane_iota(c.shape)
    first = jnp.logical_and(lane >= 64, lane < 80)
    second = jnp.logical_and(lane >= 80, lane < 96)
    ck = jnp.where(first, pltpu.roll(c, 80, 1), jnp.where(second, pltpu.roll(c, 96, 1), 0.0))
    cq = jnp.where(lane < 64, 1.0, ck)
    sa = jnp.where(first, -pltpu.roll(s, 80, 1), 0.0)
    sb = jnp.where(second, pltpu.roll(s, 96, 1), 0.0)
    return cq, ck, sa, sb


def _tile_heads(t):
    return jnp.concatenate([t] * MLA_HEADS, axis=1)


def _pre_fwd(x2, shift, scale, ng, w_in, gq, gkv, w_uq, w_uk, w_uv, rope, S, ts):
    T, D = x2.shape
    nsb = S // ts
    WQ = MLA_HEADS * LANE

    def body(x_ref, sh_ref, sc_ref, ng_ref, wn_ref, gq_ref, gkv_ref, wuq_ref, wuk_ref, wuv_ref,
             cd_ref, sd_ref,
             hb_ref, zq_ref, zkv_ref, ql_ref, kvl_ref, q_ref, k_ref, v_ref, qs_ref, ks_ref, vs_ref, g_ref, wp_ref):
        @pl.when(pl.program_id(0) == 0)
        def _():
            wp_ref[640:704, :] = jnp.zeros((64, D), wp_ref.dtype)
            wp_ref[736:768, :] = jnp.zeros((32, D), wp_ref.dtype)
            for d in range(N_DEV):
                lo, hi = W_IN_SHARD * d, W_IN_SHARD * (d + 1)
                for nat, pad, size in _w_in_row_runs():
                    a, b = max(nat, lo), min(nat + size, hi)
                    if a < b:
                        wp_ref[pad + a - nat:pad + b - nat, :] = wn_ref[d, a - lo:b - lo, :]

        w_ref = wp_ref
        x = x_ref[...]
        r1 = lax.rsqrt(jnp.mean(x * x, axis=-1, keepdims=True) + EPS)
        h = ((x * r1) * ng_ref[...]) * (1.0 + sc_ref[0]) + sh_ref[0]
        hb = _mx(h)
        hb_ref[...] = hb
        zq = _dot_nt(hb, w_ref[0:384, :])
        zq_ref[...] = zq
        rq = lax.rsqrt(jnp.mean(zq * zq, axis=-1, keepdims=True) + EPS)
        ql = _mx((zq * rq) * gq_ref[...])
        ql_ref[...] = ql
        q = _dot(ql, wuq_ref[...])
        cq, ck, sa, sb = _rope_split(cd_ref[...], sd_ref[...])
        q = (q * _tile_heads(cq) + pltpu.roll(q, WQ - 16, 1) * _tile_heads(sa)
             + pltpu.roll(q, 16, 1) * _tile_heads(sb))
        q_ref[...] = _mx(q * MLA_QSCALE)
        zkv = _dot_nt(hb, w_ref[384:640, :])
        zkv_ref[...] = zkv
        rkv = lax.rsqrt(jnp.mean(zkv * zkv, axis=-1, keepdims=True) + EPS)
        kvl = _mx((zkv * rkv) * gkv_ref[...])
        kvl_ref[...] = kvl
        kr = _dot_nt(hb, w_ref[640:768, :])
        kpe = kr * ck + pltpu.roll(kr, LANE - 16, 1) * sa + pltpu.roll(kr, 16, 1) * sb
        kf = _dot(kvl, wuk_ref[...])
        k_ref[...] = _mx(kf + jnp.concatenate([kpe] * MLA_HEADS, axis=1))
        v_ref[...] = _mx(_dot(kvl, wuv_ref[...]))
        g_ref[:, 0:512] = _dot_nt(hb, w_ref[768:1280, :])
        qs_ref[...] = _mx(_dot_nt(hb, w_ref[1280:1792, :]) * (SWA_SCALE * LOG2E))
        ks_ref[...] = _mx(_dot_nt(hb, w_ref[1792:1920, :]))
        vs_ref[...] = _mx(_dot_nt(hb, w_ref[1920:2048, :]))
        g_ref[:, 512:1024] = _dot_nt(hb, w_ref[2048:2560, :])

    row = lambda w: pl.BlockSpec((ts, w), lambda i: (i, 0))
    dense = pl.BlockSpec((ts // 8, LANE), lambda i: (i, 0))
    full = lambda a: pl.BlockSpec(a.shape, lambda i: (0,) * a.ndim)
    per_b = pl.BlockSpec((1, 1, D), lambda i: (i // nsb, 0, 0))
    out_w = [(D, _MXU_DTYPE), (384, F32), (256, F32), (384, _MXU_DTYPE), (256, _MXU_DTYPE), (WQ, _MXU_DTYPE),
             (WQ, _MXU_DTYPE), (512, _MXU_DTYPE), (512, _MXU_DTYPE), (128, _MXU_DTYPE), (128, _MXU_DTYPE), (1024, F32)]
    return pl.pallas_call(
        body, name="pre_fwd", grid=(T // ts,),
        out_shape=[jax.ShapeDtypeStruct((T, w), dt) for w, dt in out_w] + [jax.ShapeDtypeStruct((D_IN_PAD, D), w_in.dtype)],
        in_specs=[row(D), per_b, per_b, full(ng), full(w_in), full(gq), full(gkv), full(w_uq), full(w_uk), full(w_uv),
                  dense, dense],
        out_specs=[row(w) for w, _ in out_w] + [pl.BlockSpec((D_IN_PAD, D), lambda i: (0, 0))],
        compiler_params=_cparams(("arbitrary",)),
    )(x2, shift, scale, ng, w_in, gq, gkv, w_uq, w_uk, w_uv, *rope)


def _mla_fwd(q3, k3, v3, tq, tk):
    B, S, _ = q3.shape
    nq = S // tq
    assert tq == tk
    HPS = 2
    NH = 2 * HPS

    def body(q_ref, k_ref, v_ref, o_ref, lse_ref):
        rows = lax.broadcasted_iota(jnp.int32, (tq, tk), 0)
        cols = lax.broadcasted_iota(jnp.int32, (tq, tk), 1)
        low_k = _lane_iota((tk, LANE)) < 64
        low = _lane_iota((tq, LANE)) < 64

        def step(qs, kt, carry, masked):
            r0 = kt * tk
            out = []
            for j in range(NH):
                v2 = v_ref[0, r0:r0 + tk, LANE * (j // 2):LANE * (j // 2 + 1)]
                vj = (jnp.where(low_k, v2, 1.0) if j % 2 == 0 else jnp.where(low_k, 1.0, v2)).astype(v2.dtype)
                m, acc = carry[j]
                s = _dot_nt(qs[j], k_ref[0, r0:r0 + tk, LANE * j:LANE * (j + 1)])
                if masked:
                    s = jnp.where(rows >= cols, s, NEG_INF)
                m_new = jnp.maximum(m, jnp.max(s, axis=1, keepdims=True))
                alpha = jnp.exp2(m - m_new)
                p = jnp.exp2(s - m_new)
                acc = alpha * acc + _dot(_mx(p), vj)
                out.append((m_new, acc))
            return tuple(out)

        for qi in range(nq):
            r = slice(qi * tq, (qi + 1) * tq)
            qs = [q_ref[0, r, LANE * j:LANE * (j + 1)] for j in range(NH)]
            init = (jnp.full((tq, 1), NEG_INF, F32), jnp.zeros((tq, LANE), F32))
            carry = (init,) * NH
            for kt in range(qi):
                carry = step(qs, kt, carry, False)
            carry = step(qs, qi, carry, True)
            for t in range(HPS):
                (m0, a0), (m1, a1) = carry[2 * t], carry[2 * t + 1]
                l0 = jnp.where(low, pltpu.roll(a0, 64, 1), a0)
                l1 = jnp.where(low, a1, pltpu.roll(a1, 64, 1))
                o_ref[0, r, LANE * t:LANE * (t + 1)] = jnp.where(low, a0 / l0, a1 / l1)
                lse_ref[0, 2 * t, r, :] = m0 + jnp.log2(l0)
                lse_ref[0, 2 * t + 1, r, :] = m1 + jnp.log2(l1)

    grp = lambda w: pl.BlockSpec((1, S, w), lambda b, hp: (b, 0, hp))
    return pl.pallas_call(
        body, name="mla_fwd", grid=(B, MLA_HEADS // NH),
        out_shape=[jax.ShapeDtypeStruct((B, S, 512), F32), jax.ShapeDtypeStruct((B, MLA_HEADS, S, LANE), F32)],
        in_specs=[grp(NH * LANE), grp(NH * LANE), grp(HPS * LANE)],
        out_specs=[grp(HPS * LANE), pl.BlockSpec((1, NH, S, LANE), lambda b, hp: (b, hp, 0, 0))],
        compiler_params=_cparams(("arbitrary", "arbitrary")),
    )(q3, k3, v3)


def _mla_bwd(q3, k3, v3, o3, do3, lse, tq, tk, hps=2):
    B, S, _ = q3.shape
    nq, nk = S // tq, S // tk
    nh = 2 * hps
    assert tk % tq == 0

    def body(q_ref, k_ref, v_ref, o_ref, do_ref, lse_ref, dq_ref, dk_ref, dv_ref, dkt_ref, dvt_ref):
        rows = lax.broadcasted_iota(jnp.int32, (tq, tk), 0)
        cols = lax.broadcasted_iota(jnp.int32, (tq, tk), 1)
        lane = _lane_iota((tq, LANE))

        def q_tile(qi, _):
            r = pl.ds(pl.multiple_of(qi * tq, tq), tq)
            heads = []
            for j in range(nh):
                lanes = slice(LANE * j, LANE * (j + 1))
                pair = slice(LANE * (j // 2), LANE * (j // 2 + 1))
                do = jnp.where((lane < 64) if j % 2 == 0 else (lane >= 64), do_ref[0, r, pair], 0.0)
                q = q_ref[0, r, lanes]
                heads.append((lanes, pair, q, _mx(q.astype(F32).T), _mx(do), _mx(do.T),
                              jnp.sum(do * o_ref[0, r, pair], axis=1, keepdims=True),
                              jnp.concatenate([lse_ref[0, j, r, :]] * (tk // LANE), axis=1)))
            n_full = (qi * tq) // tk

            def k_tile(kt, dqs, masked):
                kr = pl.ds(pl.multiple_of(kt * tk, tk), tk)
                first = masked and (qi * tq) % tk == 0
                out = []
                dvt = [None] * hps
                for j, (lanes, pair, q, qt, dob, dot_, dcol, lse_c) in enumerate(heads):
                    k = k_ref[0, kr, lanes]
                    s = _dot_nt(q, k)
                    if masked:
                        s = jnp.where(rows + qi * tq >= cols + kt * tk, s, NEG_INF)
                    p = jnp.exp2(s - lse_c)
                    dp = _dot_nt(dob, v_ref[0, kr, pair])
                    dsb = _mx(p * (dp - dcol))
                    if first:
                        dkt_ref[j, kt] = _dot(qt, dsb)
                    else:
                        dkt_ref[j, kt] += _dot(qt, dsb)
                    pv = _dot(dot_, _mx(p))
                    dvt[j // 2] = pv if dvt[j // 2] is None else dvt[j // 2] + pv
                    out.append(dqs[j] + _dot(dsb, k))
                for t in range(hps):
                    if first:
                        dvt_ref[t, kt] = dvt[t]
                    else:
                        dvt_ref[t, kt] += dvt[t]
                return tuple(out)

            dqs = (jnp.zeros((tq, LANE), F32),) * nh
            for kt in range(n_full):
                dqs = k_tile(kt, dqs, False)
            dqs = k_tile(n_full, dqs, True)
            for j in range(nh):
                dq_ref[0, r, heads[j][0]] = (MLA_SCALE * dqs[j]).astype(dq_ref.dtype)
            return 0

        for qi in range(nq):
            q_tile(qi, 0)

        def flush(kt, _):
            kr = pl.ds(pl.multiple_of(kt * tk, tk), tk)
            for j in range(nh):
                dk_ref[0, kr, LANE * j:LANE * (j + 1)] = ((1.0 / LOG2E) * dkt_ref[j, kt].T).astype(dk_ref.dtype)
            for t in range(hps):
                dv_ref[0, kr, LANE * t:LANE * (t + 1)] = dvt_ref[t, kt].T.astype(dv_ref.dtype)
            return 0

        lax.fori_loop(0, nk, flush, 0)

    grp = lambda w: pl.BlockSpec((1, S, w), lambda b, hp: (b, 0, hp))
    return pl.pallas_call(
        body, name="mla_bwd", grid=(B, MLA_HEADS // nh),
        out_shape=[jax.ShapeDtypeStruct((B, S, 1024), MLA_GRAD_DTYPE), jax.ShapeDtypeStruct((B, S, 1024), MLA_GRAD_DTYPE),
                   jax.ShapeDtypeStruct((B, S, 512), MLA_GRAD_DTYPE)],
        in_specs=[grp(nh * LANE), grp(nh * LANE), grp(hps * LANE), grp(hps * LANE), grp(hps * LANE),
                  pl.BlockSpec((1, nh, S, LANE), lambda b, hp: (b, hp, 0, 0))],
        out_specs=[grp(nh * LANE), grp(nh * LANE), grp(hps * LANE)],
        scratch_shapes=[pltpu.VMEM((nh, nk, LANE, tk), F32), pltpu.VMEM((hps, nk, LANE, tk), F32)],
        compiler_params=_cparams(("arbitrary", "arbitrary")),
    )(q3, k3, v3, o3, do3, lse)


def _swa_consts(sink_ref):
    W = WINDOW
    row = lax.broadcasted_iota(jnp.int32, (4 * W, LANE), 0)
    out = []
    for g in range(SWA_KV_HEADS):
        slope = jnp.zeros((4 * W, LANE), F32)
        sink = jnp.zeros((4 * W, LANE), F32)
        for p in range(4):
            h = p + 4 * g
            here = jnp.logical_and(row >= W * p, row < W * (p + 1))
            slope = jnp.where(here, float(LOG2E * 2.0 ** (-8.0 * (h + 1) / SWA_HEADS)), slope)
            sink = jnp.where(here, LOG2E * sink_ref[:, h:h + 1], sink)
        out.append((jnp.concatenate([slope, slope], axis=1), sink))
    return out


def _wide(col):
    return jnp.concatenate([col, col], axis=1)


def _swa_block(q_ref, k_ref, v_ref, pb_ref, pr_ref, n, i, ij):
    W = WINDOW
    kb = jnp.maximum(n - 1, 0)
    r = pl.ds(pl.multiple_of(i * W, W), W)
    kr = pl.ds(pl.multiple_of(kb * W, W), 2 * W)
    q4, k2, v2 = q_ref[0, r, :], k_ref[0, kr, :], v_ref[0, kr, :]
    pq = _wide(pb_ref[0, r, :])
    pk = jnp.concatenate([pr_ref[0, pl.ds(kb, 1), :], pr_ref[0, pl.ds(kb + 1, 1), :]], axis=1)
    rel = ij + (n - kb) * W
    dist = jnp.where(jnp.logical_and(rel >= 0, rel < W), pq - pk, float("inf"))
    return r, kb, kr, q4, k2, v2, jnp.concatenate([dist] * 4, axis=0)


def _swa_stack(x4, g, dtype):
    lane = _lane_iota((WINDOW, LANE))
    mine = (lane < 64) if g == 0 else (lane >= 64)
    return jnp.concatenate([jnp.where(mine, x4[:, LANE * p:LANE * (p + 1)], 0).astype(dtype) for p in range(4)], axis=0)


def _swa_unstack(ref, r, lo, hi, scale=None):
    W = WINDOW
    low = _lane_iota((W, LANE)) < 64
    for p in range(4):
        t = jnp.where(low, lo[W * p:W * (p + 1)], hi[W * p:W * (p + 1)])
        ref[0, r, LANE * p:LANE * (p + 1)] = t if scale is None else scale * t


def _swa_fwd(qs3, ks3, vs3, posb, posr, sinks):
    B, S, _ = qs3.shape
    W = WINDOW
    nb = S // W
    nh = SWA_SEQ_SPLIT
    nbh = nb // nh

    def body(q_ref, k_ref, v_ref, pb_ref, pr_ref, sink_ref, o_ref, lse_ref):
        ij = lax.broadcasted_iota(jnp.int32, (W, 2 * W), 0) - lax.broadcasted_iota(jnp.int32, (W, 2 * W), 1)
        consts = _swa_consts(sink_ref)
        n0 = pl.program_id(1) * nbh

        def blk(i, _):
            n = n0 + i
            r, _, _, q4, k2, v2, dist4 = _swa_block(q_ref, k_ref, v_ref, pb_ref, pr_ref, n, i, ij)
            o_g = []
            for g, (slope, sink) in enumerate(consts):
                s = _dot_nt(_swa_stack(q4, g, q4.dtype), k2) - slope * dist4
                m = jnp.maximum(jnp.max(s, axis=1, keepdims=True), sink)
                e = jnp.exp2(s - _wide(m))
                l = jnp.sum(e, axis=1, keepdims=True) + jnp.exp2(sink - m)
                o_g.append(_dot(_mx(e), v2) * (1.0 / l))
                lse_ref[0, i, g] = m + jnp.log2(l)
            _swa_unstack(o_ref, r, o_g[0], o_g[1])
            return 0

        for i in range(nbh):
            blk(i, 0)

    seq = lambda w: pl.BlockSpec((1, S, w), lambda b, h: (b, 0, 0))
    part = lambda w: pl.BlockSpec((1, S // nh, w), lambda b, h: (b, h, 0))
    lse_spec = pl.BlockSpec((1, nbh, 2, 4 * W, LANE), lambda b, h: (b, h, 0, 0, 0))
    return pl.pallas_call(
        body, name="swa_fwd", grid=(B, nh),
        out_shape=[jax.ShapeDtypeStruct((B, S, 512), F32), jax.ShapeDtypeStruct((B, nb, 2, 4 * W, LANE), F32)],
        in_specs=[part(512), seq(LANE), seq(LANE), part(LANE), pl.BlockSpec((1, nb, W), lambda b, h: (b, 0, 0)),
                  pl.BlockSpec((1, LANE), lambda b, h: (0, 0))],
        out_specs=[part(512), lse_spec],
        compiler_params=_cparams(("arbitrary", "arbitrary")),
    )(qs3, ks3, vs3, posb, posr, sinks)


def _swa_bwd(qs3, ks3, vs3, posb, posr, sinks, os3, do3, lse):
    B, S, _ = qs3.shape
    W = WINDOW
    nb = S // W
    nh = SWA_SEQ_SPLIT
    nbh = nb // nh
    assert nbh % 2 == 0 and nbh * nh * W == S

    def body(q_ref, k_ref, v_ref, pb_ref, pr_ref, sink_ref, o_ref, do_ref, lse_ref, dq_ref, dk_ref, dv_ref, dsink_ref,
             dkt_ref, dvt_ref):
        lane1 = _lane_iota((1, LANE))
        ij = lax.broadcasted_iota(jnp.int32, (W, 2 * W), 0) - lax.broadcasted_iota(jnp.int32, (W, 2 * W), 1)
        consts = _swa_consts(sink_ref)
        hh = pl.program_id(1)
        n0 = hh * nbh

        @pl.when(hh == 0)
        def _():
            dkt_ref[...] = jnp.zeros_like(dkt_ref)
            dvt_ref[...] = jnp.zeros_like(dvt_ref)

        @pl.when(jnp.logical_and(pl.program_id(0) == 0, hh == 0))
        def _():
            dsink_ref[...] = jnp.zeros_like(dsink_ref)

        def blk(i, dsink):
            n = n0 + i
            r, kb, _, q4, k2, v2, dist4 = _swa_block(q_ref, k_ref, v_ref, pb_ref, pr_ref, n, i, ij)
            o4, do4 = o_ref[0, r, :], do_ref[0, r, :]
            dq_g = []
            dkt = jnp.zeros((LANE, 2 * W), F32)
            dvt = jnp.zeros((LANE, 2 * W), F32)
            for g, (slope, sink) in enumerate(consts):
                q_st = _swa_stack(q4, g, F32)
                do_st = _swa_stack(do4, g, F32)
                dcol = jnp.sum(do_st * _swa_stack(o4, g, F32), axis=1, keepdims=True)
                qst, dob = _mx(q_st), _mx(do_st)
                lse_c = lse_ref[0, i, g]
                pr = jnp.exp2(_dot_nt(qst, k2) - slope * dist4 - _wide(lse_c))
                dsb = _mx(pr * (_dot_nt(dob, v2) - dcol))
                psd = jnp.exp2(sink - lse_c)[:, 0:1] * dcol
                for p in range(4):
                    dsink = dsink - jnp.where(lane1 == p + 4 * g,
                                              jnp.sum(psd[W * p:W * (p + 1)], axis=0, keepdims=True), 0.0)
                dq_g.append(_dot(dsb, k2))
                dkt = dkt + _dot(_mx(q_st.T), dsb)
                dvt = dvt + _dot(_mx(do_st.T), _mx(pr))
            _swa_unstack(dq_ref, r, dq_g[0], dq_g[1], SWA_SCALE)
            dkt_ref[kb] += dkt[:, 0:W]
            dkt_ref[kb + 1] += dkt[:, W:2 * W]
            dvt_ref[kb] += dvt[:, 0:W]
            dvt_ref[kb + 1] += dvt[:, W:2 * W]
            return dsink

        dsink = jnp.zeros((1, LANE), F32)
        for i in range(nbh):
            dsink = blk(i, dsink)
        dsink_ref[...] += dsink

        @pl.when(hh == nh - 1)
        def _():
            def flush(n, _):
                r = pl.ds(pl.multiple_of(n * W, W), W)
                dk_ref[0, r, :] = (1.0 / LOG2E) * dkt_ref[n].T
                dv_ref[0, r, :] = dvt_ref[n].T
                return 0

            lax.fori_loop(0, nb, flush, 0)

    seq = lambda w: pl.BlockSpec((1, S, w), lambda b, h: (b, 0, 0))
    part = lambda w: pl.BlockSpec((1, S // nh, w), lambda b, h: (b, h, 0))
    return pl.pallas_call(
        body, name="swa_bwd", grid=(B, nh),
        out_shape=[jax.ShapeDtypeStruct((B, S, 512), F32), jax.ShapeDtypeStruct((B, S, LANE), F32),
                   jax.ShapeDtypeStruct((B, S, LANE), F32), jax.ShapeDtypeStruct((1, LANE), F32)],
        in_specs=[part(512), seq(LANE), seq(LANE), part(LANE), pl.BlockSpec((1, nb, W), lambda b, h: (b, 0, 0)),
                  pl.BlockSpec((1, LANE), lambda b, h: (0, 0)), part(512),
                  pl.BlockSpec((1, S // nh, 512), lambda b, h: (b, h, 1)),
                  pl.BlockSpec((1, nbh, 2, 4 * W, LANE), lambda b, h: (b, h, 0, 0, 0))],
        out_specs=[part(512), seq(LANE), seq(LANE), pl.BlockSpec((1, LANE), lambda b, h: (0, 0))],
        scratch_shapes=[pltpu.VMEM((nb, LANE, W), F32), pltpu.VMEM((nb, LANE, W), F32)],
        compiler_params=_cparams(("arbitrary", "arbitrary")),
    )(qs3, ks3, vs3, posb, posr, sinks, os3, do3, lse)


def _post(om, osw, g, w_out, x2, gate, fg, tgt, S, ts):
    T, D = x2.shape
    nsb = S // ts

    def body(om_ref, os_ref, g_ref, w_ref, x_ref, gate_ref, fg_ref, t_ref,
             dx2_ref, do_ref, dg_ref, loss_ref, dfg_ref, dgate_ref, dw_ref):
        i = pl.program_id(0)
        gv = g_ref[...]
        sg = 1.0 / (1.0 + jnp.exp(-gv))
        silu = gv * sg
        o = jnp.concatenate([om_ref[...], os_ref[...]], axis=1)
        ab = _mx(o * silu)
        y = _dot(ab, w_ref[...])
        gate = gate_ref[0]
        xo = x_ref[...] + gate * y
        r2 = lax.rsqrt(jnp.mean(xo * xo, axis=-1, keepdims=True) + EPS)
        xh = xo * r2
        fg = fg_ref[...]
        diff = xh * fg - t_ref[...]
        sq = jnp.sum(diff * diff, axis=0, keepdims=True)
        part = sq[:, 0:LANE]
        for t in range(1, D // LANE):
            part = part + sq[:, LANE * t:LANE * (t + 1)]
        dout = diff * (1.0 / D)
        dxh = dout * fg
        dx2 = r2 * (dxh - xh * jnp.mean(dxh * xh, axis=-1, keepdims=True))
        dx2_ref[...] = dx2
        dyb = _mx(dx2 * gate)
        da = _dot_nt(dyb, w_ref[...])
        do_ref[...] = da * silu
        dg_ref[...] = _mx(da * o * (sg * (1.0 + gv * (1.0 - sg))))

        @pl.when(i == 0)
        def _():
            loss_ref[...] = jnp.zeros_like(loss_ref)
            dfg_ref[...] = jnp.zeros_like(dfg_ref)
            dw_ref[...] = jnp.zeros_like(dw_ref)

        @pl.when(i % nsb == 0)
        def _():
            dgate_ref[...] = jnp.zeros_like(dgate_ref)

        loss_ref[...] += (0.5 / D) * part
        dfg_ref[...] += jnp.sum(dout * xh, axis=0, keepdims=True)
        dgate_ref[0] += jnp.sum(dx2 * y, axis=0, keepdims=True)
        dw_ref[...] += _dot_tn(ab, dyb)

        @pl.when(i == T // ts - 1)
        def _():
            chunks = [dw_ref[512 + 64 * PAIR_INV[h]:512 + 64 * (PAIR_INV[h] + 1), :] for h in range(SWA_HEADS)]
            for h in range(SWA_HEADS):
                dw_ref[512 + 64 * h:512 + 64 * (h + 1), :] = chunks[h]

    row = lambda w: pl.BlockSpec((ts, w), lambda i: (i, 0))
    full = lambda a: pl.BlockSpec(a.shape, lambda i: (0,) * a.ndim)
    per_b = pl.BlockSpec((1, 1, D), lambda i: (i // nsb, 0, 0))
    return pl.pallas_call(
        body, name="post", grid=(T // ts,),
        out_shape=[jax.ShapeDtypeStruct((T, D), F32), jax.ShapeDtypeStruct((T, 1024), F32),
                   jax.ShapeDtypeStruct((T, 1024), _MXU_DTYPE), jax.ShapeDtypeStruct((1, LANE), F32),
                   jax.ShapeDtypeStruct((1, D), F32), jax.ShapeDtypeStruct(gate.shape, F32),
                   jax.ShapeDtypeStruct(w_out.shape, F32)],
        in_specs=[row(512), row(512), row(1024), full(w_out), row(D), per_b, full(fg), row(D)],
        out_specs=[row(D), row(1024), row(1024),
                   pl.BlockSpec((1, LANE), lambda i: (0, 0)), pl.BlockSpec((1, D), lambda i: (0, 0)), per_b,
                   full(w_out)],
        compiler_params=_cparams(("arbitrary",)),
    )(om, osw, g, w_out, x2, gate, fg, tgt)


def _pre_bwd(dq, dk, dv, dqs, dks, dvs, dg, zq, zkv, ql, kvl, x2, dx2, scale, ng, w_in, gq, gkv, w_uq, w_uk, w_uv,
             rope, S, ts):
    T, D = x2.shape
    nsb = S // ts
    WQ = MLA_HEADS * LANE

    def body(dq_ref, dk_ref, dv_ref, dqs_ref, dks_ref, dvs_ref, dg_ref, zq_ref, zkv_ref, ql_ref, kvl_ref, x_ref, dx2_ref,
             sc_ref,
             ng_ref, w_ref, gq_ref, gkv_ref, wuq_ref, wuk_ref, wuv_ref, cd_ref, sd_ref,
             gx_ref, dz_ref, dgq_ref, dgkv_ref, dng_ref, dsh_ref, dsc_ref, dwuq_ref, dwuk_ref, dwuv_ref):
        i = pl.program_id(0)

        @pl.when(i == 0)
        def _():
            dgq_ref[...] = jnp.zeros_like(dgq_ref)
            dgkv_ref[...] = jnp.zeros_like(dgkv_ref)
            dng_ref[...] = jnp.zeros_like(dng_ref)
            dwuq_ref[...] = jnp.zeros_like(dwuq_ref)
            dwuk_ref[...] = jnp.zeros_like(dwuk_ref)
            dwuv_ref[...] = jnp.zeros_like(dwuv_ref)

        @pl.when(i % nsb == 0)
        def _():
            dsh_ref[...] = jnp.zeros_like(dsh_ref)
            dsc_ref[...] = jnp.zeros_like(dsc_ref)

        def norm_bwd(z, dy, gain):
            r = lax.rsqrt(jnp.mean(z * z, axis=-1, keepdims=True) + EPS)
            zh = z * r
            dzh = dy * gain
            return r * (dzh - zh * jnp.mean(dzh * zh, axis=-1, keepdims=True)), jnp.sum(dy * zh, axis=0, keepdims=True)

        tables = _rope_split(cd_ref[...], sd_ref[...])
        ng = ng_ref[...]
        sc1 = 1.0 + sc_ref[0]

        def rows_chain(rs):
            cq, ck, sa, sb = (t[rs] for t in tables)
            dqr = dq_ref[rs, :].astype(F32)
            dqb = _mx(dqr * _tile_heads(cq) + pltpu.roll(dqr * _tile_heads(sa), 16, 1)
                      + pltpu.roll(dqr * _tile_heads(sb), WQ - 16, 1))
            p_uq = _dot_tn(ql_ref[rs, :], dqb)
            dzq, dgq = norm_bwd(zq_ref[rs, :], _dot_nt(dqb, wuq_ref[...]), gq_ref[...])
            dkr = dk_ref[rs, :].astype(F32)
            dkb = _mx(dkr)
            p_uk = _dot_tn(kvl_ref[rs, :], dkb)
            dvb = _mx(dv_ref[rs, :])
            p_uv = _dot_tn(kvl_ref[rs, :], dvb)
            dzkv, dgkv = norm_bwd(zkv_ref[rs, :], _dot_nt(dkb, wuk_ref[...]) + _dot_nt(dvb, wuv_ref[...]), gkv_ref[...])
            dkpe = dkr[:, 0:LANE]
            for h in range(1, MLA_HEADS):
                dkpe = dkpe + dkr[:, LANE * h:LANE * (h + 1)]
            dkro = dkpe * ck + pltpu.roll(dkpe * sa, 16, 1) + pltpu.roll(dkpe * sb, LANE - 16, 1)
            dz_ref[rs, 0:384] = _mx(dzq)
            dz_ref[rs, 384:640] = _mx(dzkv)
            dz_ref[rs, 640:768] = _mx(dkro)
            dz_ref[rs, 768:1280] = dg_ref[rs, 0:512]
            dz_ref[rs, 1280:1792] = _mx(dqs_ref[rs, :])
            dz_ref[rs, 1792:1920] = _mx(dks_ref[rs, :])
            dz_ref[rs, 1920:2048] = _mx(dvs_ref[rs, :])
            dz_ref[rs, 2048:2560] = dg_ref[rs, 512:1024]
            dh = _dot(dz_ref[rs, :], w_ref[...])
            x = x_ref[rs, :]
            r1 = lax.rsqrt(jnp.mean(x * x, axis=-1, keepdims=True) + EPS)
            xn = x * r1
            dxn = dh * ng * sc1
            gx_ref[rs, :] = dx2_ref[rs, :] + r1 * (dxn - xn * jnp.mean(dxn * xn, axis=-1, keepdims=True))
            return (p_uq, p_uk, p_uv, dgq, dgkv, jnp.sum(dh, axis=0, keepdims=True),
                    jnp.sum(dh * (xn * ng), axis=0, keepdims=True), jnp.sum(dh * xn * sc1, axis=0, keepdims=True))

        hr = ts // 2
        parts = [rows_chain(slice(hr * t, hr * (t + 1))) for t in range(2)]
        p_uq, p_uk, p_uv, dgq, dgkv, dsh, dsc, dng = (a + b for a, b in zip(*parts))
        dwuq_ref[...] += p_uq
        dwuk_ref[...] += p_uk
        dwuv_ref[...] += p_uv
        dgq_ref[...] += dgq
        dgkv_ref[...] += dgkv
        dsh_ref[0] += dsh
        dsc_ref[0] += dsc
        dng_ref[...] += dng

    row = lambda w: pl.BlockSpec((ts, w), lambda i: (i, 0))
    full = lambda a: pl.BlockSpec(a.shape, lambda i: (0,) * a.ndim, pipeline_mode=pl.Buffered(1))
    per_b = pl.BlockSpec((1, 1, D), lambda i: (i // nsb, 0, 0))
    dense = pl.BlockSpec((ts // 8, LANE), lambda i: (i, 0))
    vec = lambda w: pl.BlockSpec((1, w), lambda i: (0, 0))
    return pl.pallas_call(
        body, name="pre_bwd", grid=(T // ts,),
        out_shape=[jax.ShapeDtypeStruct((T, D), F32), jax.ShapeDtypeStruct((T, D_IN_PAD), _MXU_DTYPE), jax.ShapeDtypeStruct((1, 384), F32),
                   jax.ShapeDtypeStruct((1, 256), F32), jax.ShapeDtypeStruct((1, D), F32),
                   jax.ShapeDtypeStruct(scale.shape, F32), jax.ShapeDtypeStruct(scale.shape, F32),
                   jax.ShapeDtypeStruct(w_uq.shape, F32), jax.ShapeDtypeStruct(w_uk.shape, F32),
                   jax.ShapeDtypeStruct(w_uv.shape, F32)],
        in_specs=[row(WQ), row(WQ), row(512), row(512), row(LANE), row(LANE), row(1024), row(384), row(256), row(384),
                  row(256), row(D), row(D), per_b, full(ng), full(w_in), full(gq), full(gkv), full(w_uq), full(w_uk), full(w_uv),
                  dense, dense],
        out_specs=[row(D), row(D_IN_PAD), vec(384), vec(256), vec(D), per_b, per_b, full(w_uq), full(w_uk), full(w_uv)],
        compiler_params=_cparams(("arbitrary",)),
    )(dq, dk, dv, dqs, dks, dvs, dg, zq, zkv, ql, kvl, x2, dx2, scale, ng, w_in, gq, gkv, w_uq, w_uk, w_uv, *rope)


def _w_in_row_runs():
    runs = [(0, 0, 640), (640, 704, 32), (672, 768, 512)]
    runs += [(1184 + 64 * h, 1280 + 64 * PAIR_INV[h], 64) for h in range(8)]
    runs += [(1696, 1792, 256)]
    runs += [(1952 + 64 * h, 2048 + 64 * PAIR_INV[h], 64) for h in range(8)]
    return runs


W_IN_SHARD = D_IN // N_DEV
W_IN_SLOT = 320


def _dw_in_t(dz, hb, tn, tk):
    T, M = dz.shape
    N = hb.shape[1]
    nk = T // tk

    def body(a_ref, b_ref, o_ref, ob_ref, acc_ref):
        k = pl.program_id(1)

        @pl.when(k == 0)
        def _():
            acc_ref[...] = jnp.zeros_like(acc_ref)

        acc_ref[...] += _dot_tn(a_ref[...], b_ref[...])

        @pl.when(k == nk - 1)
        def _():
            for d in range(N_DEV):
                lo, hi = W_IN_SHARD * d, W_IN_SHARD * (d + 1)
                for nat, pad, size in _w_in_row_runs():
                    a, b = max(nat, lo), min(nat + size, hi)
                    if a < b:
                        piece = acc_ref[pad + a - nat:pad + b - nat, :]
                        o_ref[d, a - lo:b - lo, :] = piece
                        ob_ref[d, a - lo:b - lo, :] = piece.astype(ob_ref.dtype)
                o_ref[d, W_IN_SHARD:W_IN_SLOT, :] = jnp.zeros((W_IN_SLOT - W_IN_SHARD, tn), F32)
                ob_ref[d, W_IN_SHARD:W_IN_SLOT, :] = jnp.zeros((W_IN_SLOT - W_IN_SHARD, tn), ob_ref.dtype)

    slots = pl.BlockSpec((N_DEV, W_IN_SLOT, tn), lambda j, k: (0, 0, j))
    return pl.pallas_call(
        body, name="dw_in", grid=(N // tn, nk),
        out_shape=[jax.ShapeDtypeStruct((N_DEV, W_IN_SLOT, N), F32), jax.ShapeDtypeStruct((N_DEV, W_IN_SLOT, N), jnp.bfloat16)],
        in_specs=[pl.BlockSpec((tk, M), lambda j, k: (k, 0)), pl.BlockSpec((tk, tn), lambda j, k: (k, j))],
        out_specs=[slots, slots],
        scratch_shapes=[pltpu.VMEM((M, tn), F32)],
        compiler_params=_cparams(("arbitrary", "arbitrary")),
    )(dz, hb)


SHARDED = (3, 6, 7, 9)
PART_SLICES = {10: (128, 1152), 2: (1152, 2176), 4: (2176, 2560), 5: (2560, 2816), 8: (2816, 2824)}


def _finalize_adamw(parts_all, dmod_all, dmod_cols, c_all, ws, sharded_grads, ms, vs):
    n = len(ws)
    local = [t for t in range(n) if t not in SHARDED[1:]]

    def body(*refs):
        p_ref, dm_ref, dmc_ref, c_ref = refs[:4]
        w_refs, gs_refs, m_refs, v_refs = refs[4:4 + n], refs[4 + n:8 + n], refs[8 + n:8 + 2 * n], refs[8 + 2 * n:8 + 3 * n]
        outs = refs[8 + 3 * n:]
        loss_ref, gl_refs = outs[0], outs[1:1 + len(local)]
        d_refs, nm_refs, nv_refs = (outs[1 + len(local) + t * n:1 + len(local) + (t + 1) * n] for t in range(3))
        ps_ref = outs[-1]
        rd = lambda r: r[:, 0, :] if len(r.shape) == 3 else r[...]

        def wr(r, val):
            if len(r.shape) == 3:
                r[:, 0, :] = val
            else:
                r[...] = val

        acc = p_ref[0]
        for j in range(1, N_DEV):
            acc = acc + p_ref[j]
        ps_ref[...] = acc
        loss_ref[...] = jnp.sum(acc[:, 0:LANE], axis=1, keepdims=True)
        db = dm_ref[0:1, :]
        for j in range(1, dm_ref.shape[0]):
            db = db + dm_ref[j:j + 1, :]
        grads = {0: _dot_tn(_mx(_silu(c_ref[...])), _mx(dmc_ref[...])), 1: db}
        for t, (lo, hi) in PART_SLICES.items():
            grads[t] = ps_ref[:, lo:hi]
        for i, t in enumerate(SHARDED):
            grads[t] = gs_refs[i][...]
        grads[SHARDED[0]] = gs_refs[0][0:W_IN_SHARD, :]
        for i, t in enumerate(local):
            wr(gl_refs[i], grads[t])
        for t in range(n):
            gv = grads[t]
            nm = ADAM_B1 * rd(m_refs[t]) + (1.0 - ADAM_B1) * gv
            nv = ADAM_B2 * rd(v_refs[t]) + (1.0 - ADAM_B2) * (gv * gv)
            m_hat = nm / (1.0 - ADAM_B1 ** ADAM_STEP)
            v_hat = nv / (1.0 - ADAM_B2 ** ADAM_STEP)
            wr(d_refs[t], -ADAM_LR * (m_hat / (jnp.sqrt(v_hat) + ADAM_EPS) + ADAM_WD * rd(w_refs[t])))
            wr(nm_refs[t], nm)
            wr(nv_refs[t], nv)

    like = lambda arrs: [jax.ShapeDtypeStruct(a.shape, F32) for a in arrs]
    out = pl.pallas_call(
        body, name="finalize_adamw",
        out_shape=[jax.ShapeDtypeStruct((1, 1), F32)] + like([ws[t] for t in local]) + like(ws) * 3,
        in_specs=[_vmem()] * (8 + 3 * n), out_specs=[_vmem()] * (1 + len(local) + 3 * n),
        scratch_shapes=[pltpu.VMEM(parts_all.shape[1:], F32)],
        compiler_params=_cparams(),
    )(parts_all, dmod_all, dmod_cols, c_all, *ws, *sharded_grads, *ms, *vs)
    k = 1 + len(local)
    return out[0], dict(zip(local, out[1:k])), out[k:k + n], out[k + n:k + 2 * n], out[k + 2 * n:]


def _pair_perm(a, axis, order):
    a = jnp.moveaxis(a, axis, -1)
    lead = a.shape[:-1]
    a = a.reshape(lead + (8, 64))[..., list(order), :].reshape(lead + (512,))
    return jnp.moveaxis(a, -1, axis)


def _rope_table(positions):
    T = positions.size
    inv = ROPE_THETA ** (-jnp.arange(0, MLA_ROPE, 2, dtype=F32) / MLA_ROPE)
    pos = jnp.repeat(positions.reshape(T // 8, 8)[:, ::-1].astype(F32), MLA_ROPE // 2, axis=1)
    ang = pos * jnp.tile(inv, 8)[None, :]
    return jnp.cos(ang), jnp.sin(ang)


def _local_step(x, mod, positions, ng, w_in_t, gq, gkv, w_uq, w_ukv, sinks, w_out, fg, tgt,
                ts=512, fq=512, fk=512, bq=512, bk=512):
    B, S, D = x.shape
    T = B * S
    x2 = x.reshape(T, D)
    shift, scale, gate = (mod[:, None, k * D:(k + 1) * D] for k in range(3))
    w_uq_p = jnp.pad(w_uq.reshape(Q_LORA, MLA_HEADS, 96), ((0, 0), (0, 0), (0, 32))).reshape(Q_LORA, MLA_HEADS * LANE)
    w_ukv3 = w_ukv.reshape(KV_LORA, MLA_HEADS, 128)
    w_uk_p = jnp.pad(w_ukv3[:, :, :64], ((0, 0), (0, 0), (0, 64))).reshape(KV_LORA, MLA_HEADS * LANE)
    w_uv = w_ukv3[:, :, 64:].reshape(KV_LORA, 512)
    w_out_p = jnp.concatenate([w_out[:512], _pair_perm(w_out[512:], 0, PAIR_ORDER)], axis=0)
    rope = _rope_table(positions)
    posf = positions.astype(F32)
    posb = jnp.broadcast_to(posf[:, :, None], (B, S, LANE))
    posr = posf.reshape(B, S // WINDOW, WINDOW)
    sinks_l = jnp.pad(sinks.reshape(1, SWA_HEADS), ((0, 0), (0, LANE - SWA_HEADS)))

    (hb, zq, zkv, ql, kvl, q, k, v, qs, ks, vs, g, w_in_p) = _pre_fwd(
        x2, shift, scale, ng, w_in_t, gq, gkv, w_uq_p, w_uk_p, w_uv, rope, S, ts)
    r3 = lambda a: a.reshape(B, S, a.shape[-1])
    om, lse_m = _mla_fwd(r3(q), r3(k), r3(v), fq, fk)
    osw, lse_s = _swa_fwd(r3(qs), r3(ks), r3(vs), posb, posr, sinks_l)
    dx2, do, dg, loss_v, dfg, dgate, dw_out = _post(
        om.reshape(T, 512), osw.reshape(T, 512), g, w_out_p, x2, gate, fg.reshape(1, D), tgt.reshape(T, D), S, ts)
    do3 = r3(do)
    dq, dk, dv = _mla_bwd(r3(q), r3(k), r3(v), om, do3, lse_m, bq, bk)
    dqs, dks, dvs, dsink = _swa_bwd(r3(qs), r3(ks), r3(vs), posb, posr, sinks_l, osw, do3, lse_s)
    f2 = lambda a: a.reshape(T, a.shape[-1])
    gx, dz, dgq, dgkv, dng, dsh, dsc, dw_uq_p, dw_uk_p, dw_uv = _pre_bwd(
        f2(dq), f2(dk), f2(dv), f2(dqs), f2(dks), f2(dvs), dg, zq, zkv, ql, kvl, x2, dx2, scale, ng, w_in_p, gq, gkv,
        w_uq_p, w_uk_p, w_uv, rope, S, ts)
    tk = min(T, 1024)
    dw_in_t = _dw_in_t(dz, hb, 512, tk)
    dw_uq = dw_uq_p.reshape(Q_LORA, MLA_HEADS, LANE)[:, :, :96].reshape(Q_LORA, 768)
    dw_uk = dw_uk_p.reshape(KV_LORA, MLA_HEADS, LANE)[:, :, :64]
    dw_uv = dw_uv.reshape(KV_LORA, MLA_HEADS, 64)
    dw_ukv = jnp.concatenate([dw_uk, dw_uv], axis=2).reshape(KV_LORA, 1024)
    parts = jnp.concatenate([loss_v, dfg, dng, dgq, dgkv, dsink], axis=1)
    dmod = jnp.concatenate([dsh, dsc, dgate], axis=2).reshape(B, 3 * D)
    return gx.reshape(B, S, D), dw_in_t, dw_uq, dw_ukv, dw_out, parts, dmod


def kernel(x, c, positions, w_ada, b_ada, norm_gain, w_in, q_norm_gain, kv_norm_gain, w_uq, w_ukv, swa_sinks, w_out, final_gain, loss_target, m_w_ada, m_b_ada, m_norm_gain, m_w_in, m_q_norm_gain, m_kv_norm_gain, m_w_uq, m_w_ukv, m_swa_sinks, m_w_out, m_final_gain, v_w_ada, v_b_ada, v_norm_gain, v_w_in, v_q_norm_gain, v_kv_norm_gain, v_w_uq, v_w_ukv, v_swa_sinks, v_w_out, v_final_gain):
    B, S, D = x.shape
    me = 4 * lax.axis_index("x") + 2 * lax.axis_index("y") + lax.axis_index("c")
    bf = _MXU_DTYPE

    ncol = w_ada.shape[2]
    b_cols = lax.dynamic_slice_in_dim(b_ada, me * ncol, ncol, axis=1)
    c_all, win_g, wuq_g, wukv_g, wout_g, mod_g = _all_gather(
        [c, (jnp.transpose(w_in, (2, 0, 1)), bf), (w_uq[0], bf), (w_ukv[0], bf), (w_out[0], bf)], "ag_weights",
        fused=((w_ada[0], b_cols), _ada_cols, jax.ShapeDtypeStruct((N_DEV, B, ncol), F32)))
    c_all = c_all.reshape(N_DEV * B, D)
    cat_cols = lambda a: jnp.transpose(a, (1, 0, 2)).reshape(a.shape[1], N_DEV * a.shape[2])
    w_in_t, w_uq_f, w_ukv_f = win_g, cat_cols(wuq_g), cat_cols(wukv_g)
    w_out_f = wout_g.reshape(D, D)

    mod = lax.dynamic_index_in_dim(mod_g, me, axis=1, keepdims=False)
    mod = jnp.transpose(mod, (1, 0, 2)).reshape(B, 3 * D)

    gx, dw_in_t, dw_uq, dw_ukv, dw_out, parts, dmod = _local_step(
        x, mod, positions, norm_gain, w_in_t, q_norm_gain, kv_norm_gain, w_uq_f, w_ukv_f, swa_sinks,
        w_out_f, final_gain, loss_target)

    split_cols = lambda a: jnp.transpose(a.reshape(a.shape[0], 4, 2, a.shape[1] // N_DEV), (1, 2, 0, 3))
    split_rows = lambda a: a.reshape(4, 2, a.shape[0] // N_DEV, a.shape[1])
    slots = lambda a: a.reshape((4, 2) + a.shape[1:])
    g_w_in_t, g_w_uq, g_w_ukv, g_w_out, parts_g, dmod_g = _reduce_scatter(
        [(slots(dw_in_t[0]), slots(dw_in_t[1])), split_cols(dw_uq), split_cols(dw_ukv), split_rows(dw_out)], "rs_grads",
        gather=(parts, dmod))

    dmod_all = dmod_g.reshape(N_DEV * B, 3 * D)
    dmod_cols = lax.dynamic_slice_in_dim(dmod_all, me * ncol, ncol, axis=1)
    ws = [w_ada, b_ada, norm_gain, w_in, q_norm_gain, kv_norm_gain, w_uq, w_ukv, swa_sinks, w_out, final_gain]
    ms = [m_w_ada, m_b_ada, m_norm_gain, m_w_in, m_q_norm_gain, m_kv_norm_gain, m_w_uq, m_w_ukv, m_swa_sinks, m_w_out,
          m_final_gain]
    vs = [v_w_ada, v_b_ada, v_norm_gain, v_w_in, v_q_norm_gain, v_kv_norm_gain, v_w_uq, v_w_ukv, v_swa_sinks, v_w_out,
          v_final_gain]
    two_d = lambda a: a.reshape((1, a.shape[0]) if a.ndim == 1 else a.shape[-2:])
    flat = lambda arrs: [jnp.transpose(a, (2, 0, 1)) if t == SHARDED[0] else two_d(a) for t, a in enumerate(arrs)]
    sharded = {3: g_w_in_t, 6: g_w_uq, 7: g_w_ukv, 9: g_w_out}
    loss, local_grads, deltas, new_ms, new_vs = _finalize_adamw(
        parts_g, dmod_all, dmod_cols, c_all, flat(ws), [sharded[t] for t in SHARDED], flat(ms), flat(vs))
    grads = [local_grads[t] if t in local_grads else sharded[t] for t in range(len(ws))]
    shaped = lambda arrs: [jnp.transpose(a, (1, 2, 0)) if t == SHARDED[0] else a.reshape(ws[t].shape) for t, a in enumerate(arrs)]
    return (loss.reshape(()), gx, *shaped(grads), *shaped(deltas), *shaped(new_ms), *shaped(new_vs))
```

```python
import functools

import jax
import jax.numpy as jnp
from jax import lax
from jax.experimental import pallas as pl
from jax.experimental.pallas import tpu as pltpu

F32 = jnp.float32
_MXU_DTYPE = jnp.bfloat16
MLA_GRAD_DTYPE = jnp.bfloat16

N_DEV = 8
MLA_HEADS = 8
MLA_NOPE = 64
MLA_ROPE = 32
MLA_QK = MLA_NOPE + MLA_ROPE
Q_LORA = 384
KV_LORA = 256
SWA_HEADS = 8
SWA_KV_HEADS = 2
SWA_HEAD_DIM = 64
WINDOW = 128
ROPE_THETA = 10000.0
EPS = 1e-6
MLA_SCALE = float((MLA_NOPE + MLA_ROPE) ** -0.5)
SWA_SCALE = float(SWA_HEAD_DIM ** -0.5)
LOG2E = 1.4426950408889634
MLA_QSCALE = MLA_SCALE * LOG2E
D_IN = 2464
D_IN_PAD = 2560
PAIR_ORDER = (0, 4, 1, 5, 2, 6, 3, 7)
PAIR_INV = (0, 2, 4, 6, 1, 3, 5, 7)

ADAM_LR = 0.001
ADAM_B1 = 0.9
ADAM_B2 = 0.999
ADAM_EPS = 1e-08
ADAM_WD = 0.01
ADAM_STEP = 10

LANE = 128
VMEM_LIMIT = 56 * 1024 * 1024

MESH = pl.DeviceIdType.MESH
NEG_INF = float("-inf")
SWA_SEQ_SPLIT = 2


def _mx(a):
    return a.astype(_MXU_DTYPE)


def _dot(a, b):
    return jnp.dot(a, b, preferred_element_type=F32)


def _dot_nt(a, b):
    return lax.dot_general(a, b, (((1,), (1,)), ((), ())), preferred_element_type=F32)


def _dot_tn(a, b):
    return lax.dot_general(a, b, (((0,), (0,)), ((), ())), preferred_element_type=F32)


def _cparams(sem=None):
    return pltpu.CompilerParams(dimension_semantics=sem, vmem_limit_bytes=VMEM_LIMIT)


def _vmem():
    return pl.BlockSpec(memory_space=pltpu.VMEM)


def _lane_iota(shape):
    return lax.broadcasted_iota(jnp.int32, shape, len(shape) - 1)


def _gather_program(srcs, outs, send_sems, recv_sems, local_sems):
    x, y, c = lax.axis_index("x"), lax.axis_index("y"), lax.axis_index("c")
    me, sibling = (x, y, c), (x, y, 1 - c)
    chips = [(1 - x, y), (x, 1 - y), (1 - x, 1 - y)]

    def slot(a, dev):
        return outs[a].at[4 * dev[0] + 2 * dev[1] + dev[2]]

    def copy(a, k, block, to, src=None):
        return pltpu.make_async_remote_copy(
            src_ref=slot(a, block) if src is None else src, dst_ref=slot(a, block),
            send_sem=send_sems.at[7 * a + k], recv_sem=recv_sems.at[7 * a + k],
            device_id=to, device_id_type=MESH)

    def local(a):
        return pltpu.make_async_copy(srcs[a], slot(a, me), local_sems.at[a])

    def start(a):
        local(a).start()
        cps = [copy(a, 0, me, sibling, src=srcs[a])]
        cps += [copy(a, 1 + j, me, (*chip, c), src=srcs[a]) for j, chip in enumerate(chips)]
        for cp in cps:
            cp.start()
        return cps

    def finish(group):
        cps = []
        for j, chip in enumerate(chips):
            for a in group:
                copy(a, 1 + j, (*chip, c), me).wait_recv()
                cp = copy(a, 4 + j, (*chip, c), sibling)
                cp.start()
                cps.append(cp)
        for a in group:
            copy(a, 0, sibling, me).wait_recv()
            for j, chip in enumerate(chips):
                copy(a, 4 + j, (*chip, 1 - c), me).wait_recv()
            local(a).wait()
        return cps

    return start, finish


def _gather_sems(m):
    return [pltpu.SemaphoreType.DMA((7 * m,)), pltpu.SemaphoreType.DMA((7 * m,)), pltpu.SemaphoreType.DMA((m,))]


def _all_gather(arrs, name, fused=None):
    n = len(arrs)
    cast = {a: e[1] for a, e in enumerate(arrs) if isinstance(e, tuple)}
    arrs = [e[0] if isinstance(e, tuple) else e for e in arrs]
    shapes = [(a.shape[0], a.shape[-1]) for a in arrs]
    extra, fn, piece = fused if fused else ((), None, None)
    ne, m = len(extra), n + (1 if fused else 0)

    def body(*refs):
        ins, ex, outs = refs[:n], refs[n:n + ne], refs[n + ne:n + ne + m]
        rest = refs[n + ne + m:]
        staged = dict(zip(cast, rest))
        srcs = [staged.get(a, ins[a]) for a in range(n)] + ([rest[len(cast)]] if fused else [])
        start, finish = _gather_program(srcs, outs, *rest[-3:])
        pending = []
        for a in range(n):
            if a in staged:
                staged[a][...] = (ins[a][:, 0, :] if len(ins[a].shape) == 3 else ins[a][...]).astype(cast[a])
            pending += start(a)
        if fused:
            pending += finish([0])
            fn(srcs[n], outs[0], *ex)
            pending += start(n)
            pending += finish([n] + list(range(1, n)))
        else:
            pending += finish(list(range(n)))
        for cp in pending:
            cp.wait_send()

    out_shape = [jax.ShapeDtypeStruct((N_DEV,) + shapes[a], cast.get(a, arrs[a].dtype)) for a in range(n)]
    scratch = [pltpu.VMEM(shapes[a], cast[a]) for a in cast]
    if fused:
        out_shape.append(jax.ShapeDtypeStruct((N_DEV,) + piece.shape, piece.dtype))
        scratch.append(pltpu.VMEM(piece.shape, piece.dtype))
    return pl.pallas_call(
        body, name=name, out_shape=out_shape,
        in_specs=[_vmem()] * (n + ne), out_specs=[_vmem()] * m,
        scratch_shapes=scratch + _gather_sems(m),
        compiler_params=pltpu.CompilerParams(vmem_limit_bytes=VMEM_LIMIT),
    )(*arrs, *extra)


def _reduce_scatter(arrs, name, gather=()):
    n, g = len(arrs), len(gather)
    halves = [a[1] if isinstance(a, tuple) else a.astype(jnp.bfloat16) for a in arrs]
    arrs = [a[0] if isinstance(a, tuple) else a for a in arrs]

    def body(*refs):
        xs, xbs, gins = refs[:n], refs[n:2 * n], refs[2 * n:2 * n + g]
        outs, gouts = refs[2 * n + g:3 * n + g], refs[3 * n + g:3 * n + 2 * g]
        rest = refs[3 * n + 2 * g:]
        parts, recv_a, send_b, recv_b = (rest[t * n:(t + 1) * n] for t in range(4))
        send_sems, recv_sems, local_sems = rest[4 * n:4 * n + 3]
        x, y, c = lax.axis_index("x"), lax.axis_index("y"), lax.axis_index("c")
        myq = 2 * x + y
        gather_start, gather_finish = _gather_program(gins, gouts, *rest[4 * n + 3:])

        def chip(k):
            return (1 - x if k & 2 else x, 1 - y if k & 1 else y)

        def from_sibling(a):
            return pltpu.make_async_remote_copy(
                src_ref=xbs[a].at[:, 1 - c], dst_ref=recv_a[a], send_sem=send_sems.at[4 * a], recv_sem=recv_sems.at[4 * a],
                device_id=(x, y, 1 - c), device_id_type=MESH)

        def to_owner(a, k):
            qx, qy = chip(k)
            return pltpu.make_async_remote_copy(
                src_ref=send_b[a].at[2 * qx + qy], dst_ref=recv_b[a].at[myq],
                send_sem=send_sems.at[4 * a + k], recv_sem=recv_sems.at[4 * a + k],
                device_id=(qx, qy, c), device_id_type=MESH)

        mine = [pltpu.make_async_copy(xs[a].at[:, c], parts[a], local_sems.at[a]) for a in range(n)]
        first = [from_sibling(a) for a in range(n)]
        for cp in mine + first:
            cp.start()
        second = []
        for b in range(g):
            second += gather_start(b)
        for a in range(n):
            mine[a].wait()
            first[a].wait_recv()
            parts[a][...] = parts[a][...] + recv_a[a][...].astype(F32)
            send_b[a][...] = parts[a][...].astype(jnp.bfloat16)
            for k in range(1, 4):
                cp = to_owner(a, k)
                cp.start()
                second.append(cp)
        second += gather_finish(list(range(g)))
        for a in range(n):
            acc = parts[a][myq]
            for k in range(1, 4):
                to_owner(a, k).wait_recv()
                qx, qy = chip(k)
                acc = acc + recv_b[a][2 * qx + qy].astype(F32)
            outs[a][...] = acc
        for cp in first + second:
            cp.wait_send()

    quarter = lambda a, dt: pltpu.VMEM((4,) + a.shape[2:], dt)
    return pl.pallas_call(
        body, name=name,
        out_shape=[jax.ShapeDtypeStruct(a.shape[2:], F32) for a in arrs]
        + [jax.ShapeDtypeStruct((N_DEV,) + a.shape, a.dtype) for a in gather],
        in_specs=[pl.BlockSpec(memory_space=pl.ANY)] * (2 * n) + [_vmem()] * g, out_specs=[_vmem()] * (n + g),
        scratch_shapes=[quarter(a, F32) for a in arrs] + [quarter(a, jnp.bfloat16) for a in arrs] * 3
        + [pltpu.SemaphoreType.DMA((4 * n,)), pltpu.SemaphoreType.DMA((4 * n,)), pltpu.SemaphoreType.DMA((n,))]
        + _gather_sems(g),
        compiler_params=pltpu.CompilerParams(vmem_limit_bytes=VMEM_LIMIT),
    )(*arrs, *halves, *gather)


def _silu(t):
    return t * (1.0 / (1.0 + jnp.exp(-t)))


def _ada_cols(piece_ref, c_all_ref, w_ref, b_ref):
    w = _mx(w_ref[...])
    for d in range(N_DEV):
        piece_ref[d] = _dot(_mx(_silu(c_all_ref[d])), w) + b_ref[...]


def _rope_split(cd, sd):
    n = cd.shape[0]

    def expand(d):
        rep = jnp.broadcast_to(d[:, None, :], (n, 8, LANE)).reshape(8 * n, LANE)
        return pltpu.roll(rep, 0, 1, stride=16, stride_axis=0)

    c, s = expand(cd), expand(sd)
    lane = _lane_iota(c.shape)
    first = jnp.logical_and(lane >= 64, lane < 80)
    second = jnp.logical_and(lane >= 80, lane < 96)
    ck = jnp.where(first, pltpu.roll(c, 80, 1), jnp.where(second, pltpu.roll(c, 96, 1), 0.0))
    cq = jnp.where(lane < 64, 1.0, ck)
    sa = jnp.where(first, -pltpu.roll(s, 80, 1), 0.0)
    sb = jnp.where(second, pltpu.roll(s, 96, 1), 0.0)
    return cq, ck, sa, sb


def _tile_heads(t):
    return jnp.concatenate([t] * MLA_HEADS, axis=1)


def _pre_fwd(x2, shift, scale, ng, w_in, gq, gkv, w_uq, w_uk, w_uv, rope, S, ts):
    T, D = x2.shape
    nsb = S // ts
    WQ = MLA_HEADS * LANE

    def body(x_ref, sh_ref, sc_ref, ng_ref, wn_ref, gq_ref, gkv_ref, wuq_ref, wuk_ref, wuv_ref,
             cd_ref, sd_ref,
             hb_ref, zq_ref, zkv_ref, ql_ref, kvl_ref, q_ref, k_ref, v_ref, qs_ref, ks_ref, vs_ref, g_ref, wp_ref):
        @pl.when(pl.program_id(0) == 0)
        def _():
            wp_ref[640:704, :] = jnp.zeros((64, D), wp_ref.dtype)
            wp_ref[736:768, :] = jnp.zeros((32, D), wp_ref.dtype)
            for d in range(N_DEV):
                lo, hi = W_IN_SHARD * d, W_IN_SHARD * (d + 1)
                for nat, pad, size in _w_in_row_runs():
                    a, b = max(nat, lo), min(nat + size, hi)
                    if a < b:
                        wp_ref[pad + a - nat:pad + b - nat, :] = wn_ref[d, a - lo:b - lo, :]

        w_ref = wp_ref
        x = x_ref[...]
        r1 = lax.rsqrt(jnp.mean(x * x, axis=-1, keepdims=True) + EPS)
        h = ((x * r1) * ng_ref[...]) * (1.0 + sc_ref[0]) + sh_ref[0]
        hb = _mx(h)
        hb_ref[...] = hb
        zq = _dot_nt(hb, w_ref[0:384, :])
        zq_ref[...] = zq
        rq = lax.rsqrt(jnp.mean(zq * zq, axis=-1, keepdims=True) + EPS)
        ql = _mx((zq * rq) * gq_ref[...])
        ql_ref[...] = ql
        q = _dot(ql, wuq_ref[...])
        cq, ck, sa, sb = _rope_split(cd_ref[...], sd_ref[...])
        q = (q * _tile_heads(cq) + pltpu.roll(q, WQ - 16, 1) * _tile_heads(sa)
             + pltpu.roll(q, 16, 1) * _tile_heads(sb))
        q_ref[...] = _mx(q * MLA_QSCALE)
        zkv = _dot_nt(hb, w_ref[384:640, :])
        zkv_ref[...] = zkv
        rkv = lax.rsqrt(jnp.mean(zkv * zkv, axis=-1, keepdims=True) + EPS)
        kvl = _mx((zkv * rkv) * gkv_ref[...])
        kvl_ref[...] = kvl
        kr = _dot_nt(hb, w_ref[640:768, :])
        kpe = kr * ck + pltpu.roll(kr, LANE - 16, 1) * sa + pltpu.roll(kr, 16, 1) * sb
        kf = _dot(kvl, wuk_ref[...])
        k_ref[...] = _mx(kf + jnp.concatenate([kpe] * MLA_HEADS, axis=1))
        v_ref[...] = _mx(_dot(kvl, wuv_ref[...]))
        g_ref[:, 0:512] = _dot_nt(hb, w_ref[768:1280, :])
        qs_ref[...] = _mx(_dot_nt(hb, w_ref[1280:1792, :]) * (SWA_SCALE * LOG2E))
        ks_ref[...] = _mx(_dot_nt(hb, w_ref[1792:1920, :]))
        vs_ref[...] = _mx(_dot_nt(hb, w_ref[1920:2048, :]))
        g_ref[:, 512:1024] = _dot_nt(hb, w_ref[2048:2560, :])

    row = lambda w: pl.BlockSpec((ts, w), lambda i: (i, 0))
    dense = pl.BlockSpec((ts // 8, LANE), lambda i: (i, 0))
    full = lambda a: pl.BlockSpec(a.shape, lambda i: (0,) * a.ndim)
    per_b = pl.BlockSpec((1, 1, D), lambda i: (i // nsb, 0, 0))
    out_w = [(D, _MXU_DTYPE), (384, F32), (256, F32), (384, _MXU_DTYPE), (256, _MXU_DTYPE), (WQ, _MXU_DTYPE),
             (WQ, _MXU_DTYPE), (512, _MXU_DTYPE), (512, _MXU_DTYPE), (128, _MXU_DTYPE), (128, _MXU_DTYPE), (1024, F32)]
    return pl.pallas_call(
        body, name="pre_fwd", grid=(T // ts,),
        out_shape=[jax.ShapeDtypeStruct((T, w), dt) for w, dt in out_w] + [jax.ShapeDtypeStruct((D_IN_PAD, D), w_in.dtype)],
        in_specs=[row(D), per_b, per_b, full(ng), full(w_in), full(gq), full(gkv), full(w_uq), full(w_uk), full(w_uv),
                  dense, dense],
        out_specs=[row(w) for w, _ in out_w] + [pl.BlockSpec((D_IN_PAD, D), lambda i: (0, 0))],
        compiler_params=_cparams(("arbitrary",)),
    )(x2, shift, scale, ng, w_in, gq, gkv, w_uq, w_uk, w_uv, *rope)


def _mla_fwd(q3, k3, v3, tq, tk):
    B, S, _ = q3.shape
    nq = S // tq
    assert tq == tk
    HPS = 2
    NH = 2 * HPS

    def body(q_ref, k_ref, v_ref, o_ref, lse_ref):
        rows = lax.broadcasted_iota(jnp.int32, (tq, tk), 0)
        cols = lax.broadcasted_iota(jnp.int32, (tq, tk), 1)
        low_k = _lane_iota((tk, LANE)) < 64
        low = _lane_iota((tq, LANE)) < 64

        def step(qs, kt, carry, masked):
            r0 = kt * tk
            out = []
            for j in range(NH):
                v2 = v_ref[0, r0:r0 + tk, LANE * (j // 2):LANE * (j // 2 + 1)]
                vj = (jnp.where(low_k, v2, 1.0) if j % 2 == 0 else jnp.where(low_k, 1.0, v2)).astype(v2.dtype)
                m, acc = carry[j]
                s = _dot_nt(qs[j], k_ref[0, r0:r0 + tk, LANE * j:LANE * (j + 1)])
                if masked:
                    s = jnp.where(rows >= cols, s, NEG_INF)
                m_new = jnp.maximum(m, jnp.max(s, axis=1, keepdims=True))
                alpha = jnp.exp2(m - m_new)
                p = jnp.exp2(s - m_new)
                acc = alpha * acc + _dot(_mx(p), vj)
                out.append((m_new, acc))
            return tuple(out)

        for qi in range(nq):
            r = slice(qi * tq, (qi + 1) * tq)
            qs = [q_ref[0, r, LANE * j:LANE * (j + 1)] for j in range(NH)]
            init = (jnp.full((tq, 1), NEG_INF, F32), jnp.zeros((tq, LANE), F32))
            carry = (init,) * NH
            for kt in range(qi):
                carry = step(qs, kt, carry, False)
            carry = step(qs, qi, carry, True)
            for t in range(HPS):
                (m0, a0), (m1, a1) = carry[2 * t], carry[2 * t + 1]
                l0 = jnp.where(low, pltpu.roll(a0, 64, 1), a0)
                l1 = jnp.where(low, a1, pltpu.roll(a1, 64, 1))
                o_ref[0, r, LANE * t:LANE * (t + 1)] = jnp.where(low, a0 / l0, a1 / l1)
                lse_ref[0, 2 * t, r, :] = m0 + jnp.log2(l0)
                lse_ref[0, 2 * t + 1, r, :] = m1 + jnp.log2(l1)

    grp = lambda w: pl.BlockSpec((1, S, w), lambda b, hp: (b, 0, hp))
    return pl.pallas_call(
        body, name="mla_fwd", grid=(B, MLA_HEADS // NH),
        out_shape=[jax.ShapeDtypeStruct((B, S, 512), F32), jax.ShapeDtypeStruct((B, MLA_HEADS, S, LANE), F32)],
        in_specs=[grp(NH * LANE), grp(NH * LANE), grp(HPS * LANE)],
        out_specs=[grp(HPS * LANE), pl.BlockSpec((1, NH, S, LANE), lambda b, hp: (b, hp, 0, 0))],
        compiler_params=_cparams(("arbitrary", "arbitrary")),
    )(q3, k3, v3)


def _mla_bwd(q3, k3, v3, o3, do3, lse, tq, tk, hps=2):
    B, S, _ = q3.shape
    nq, nk = S // tq, S // tk
    nh = 2 * hps
    assert tk % tq == 0

    def body(q_ref, k_ref, v_ref, o_ref, do_ref, lse_ref, dq_ref, dk_ref, dv_ref, dkt_ref, dvt_ref):
        rows = lax.broadcasted_iota(jnp.int32, (tq, tk), 0)
        cols = lax.broadcasted_iota(jnp.int32, (tq, tk), 1)
        lane = _lane_iota((tq, LANE))

        def q_tile(qi, _):
            r = pl.ds(pl.multiple_of(qi * tq, tq), tq)
            heads = []
            for j in range(nh):
                lanes = slice(LANE * j, LANE * (j + 1))
                pair = slice(LANE * (j // 2), LANE * (j // 2 + 1))
                do = jnp.where((lane < 64) if j % 2 == 0 else (lane >= 64), do_ref[0, r, pair], 0.0)
                q = q_ref[0, r, lanes]
                heads.append((lanes, pair, q, _mx(q.astype(F32).T), _mx(do), _mx(do.T),
                              jnp.sum(do * o_ref[0, r, pair], axis=1, keepdims=True),
                              jnp.concatenate([lse_ref[0, j, r, :]] * (tk // LANE), axis=1)))
            n_full = (qi * tq) // tk

            def k_tile(kt, dqs, masked):
                kr = pl.ds(pl.multiple_of(kt * tk, tk), tk)
                first = masked and (qi * tq) % tk == 0
                out = []
                dvt = [None] * hps
                for j, (lanes, pair, q, qt, dob, dot_, dcol, lse_c) in enumerate(heads):
                    k = k_ref[0, kr, lanes]
                    s = _dot_nt(q, k)
                    if masked:
                        s = jnp.where(rows + qi * tq >= cols + kt * tk, s, NEG_INF)
                    p = jnp.exp2(s - lse_c)
                    dp = _dot_nt(dob, v_ref[0, kr, pair])
                    dsb = _mx(p * (dp - dcol))
                    if first:
                        dkt_ref[j, kt] = _dot(qt, dsb)
                    else:
                        dkt_ref[j, kt] += _dot(qt, dsb)
                    pv = _dot(dot_, _mx(p))
                    dvt[j // 2] = pv if dvt[j // 2] is None else dvt[j // 2] + pv
                    out.append(dqs[j] + _dot(dsb, k))
                for t in range(hps):
                    if first:
                        dvt_ref[t, kt] = dvt[t]
                    else:
                        dvt_ref[t, kt] += dvt[t]
                return tuple(out)

            dqs = (jnp.zeros((tq, LANE), F32),) * nh
            for kt in range(n_full):
                dqs = k_tile(kt, dqs, False)
            dqs = k_tile(n_full, dqs, True)
            for j in range(nh):
                dq_ref[0, r, heads[j][0]] = (MLA_SCALE * dqs[j]).astype(dq_ref.dtype)
            return 0

        for qi in range(nq):
            q_tile(qi, 0)

        def flush(kt, _):
            kr = pl.ds(pl.multiple_of(kt * tk, tk), tk)
            for j in range(nh):
                dk_ref[0, kr, LANE * j:LANE * (j + 1)] = ((1.0 / LOG2E) * dkt_ref[j, kt].T).astype(dk_ref.dtype)
            for t in range(hps):
                dv_ref[0, kr, LANE * t:LANE * (t + 1)] = dvt_ref[t, kt].T.astype(dv_ref.dtype)
            return 0

        lax.fori_loop(0, nk, flush, 0)

    grp = lambda w: pl.BlockSpec((1, S, w), lambda b, hp: (b, 0, hp))
    return pl.pallas_call(
        body, name="mla_bwd", grid=(B, MLA_HEADS // nh),
        out_shape=[jax.ShapeDtypeStruct((B, S, 1024), MLA_GRAD_DTYPE), jax.ShapeDtypeStruct((B, S, 1024), MLA_GRAD_DTYPE),
                   jax.ShapeDtypeStruct((B, S, 512), MLA_GRAD_DTYPE)],
        in_specs=[grp(nh * LANE), grp(nh * LANE), grp(hps * LANE), grp(hps * LANE), grp(hps * LANE),
                  pl.BlockSpec((1, nh, S, LANE), lambda b, hp: (b, hp, 0, 0))],
        out_specs=[grp(nh * LANE), grp(nh * LANE), grp(hps * LANE)],
        scratch_shapes=[pltpu.VMEM((nh, nk, LANE, tk), F32), pltpu.VMEM((hps, nk, LANE, tk), F32)],
        compiler_params=_cparams(("arbitrary", "arbitrary")),
    )(q3, k3, v3, o3, do3, lse)


def _swa_consts(sink_ref):
    W = WINDOW
    row = lax.broadcasted_iota(jnp.int32, (4 * W, LANE), 0)
    out = []
    for g in range(SWA_KV_HEADS):
        slope = jnp.zeros((4 * W, LANE), F32)
        sink = jnp.zeros((4 * W, LANE), F32)
        for p in range(4):
            h = p + 4 * g
            here = jnp.logical_and(row >= W * p, row < W * (p + 1))
            slope = jnp.where(here, float(LOG2E * 2.0 ** (-8.0 * (h + 1) / SWA_HEADS)), slope)
            sink = jnp.where(here, LOG2E * sink_ref[:, h:h + 1], sink)
        out.append((jnp.concatenate([slope, slope], axis=1), sink))
    return out


def _wide(col):
    return jnp.concatenate([col, col], axis=1)


def _swa_block(q_ref, k_ref, v_ref, pb_ref, pr_ref, n, i, ij):
    W = WINDOW
    kb = jnp.maximum(n - 1, 0)
    r = pl.ds(pl.multiple_of(i * W, W), W)
    kr = pl.ds(pl.multiple_of(kb * W, W), 2 * W)
    q4, k2, v2 = q_ref[0, r, :], k_ref[0, kr, :], v_ref[0, kr, :]
    pq = _wide(pb_ref[0, r, :])
    pk = jnp.concatenate([pr_ref[0, pl.ds(kb, 1), :], pr_ref[0, pl.ds(kb + 1, 1), :]], axis=1)
    rel = ij + (n - kb) * W
    dist = jnp.where(jnp.logical_and(rel >= 0, rel < W), pq - pk, float("inf"))
    return r, kb, kr, q4, k2, v2, jnp.concatenate([dist] * 4, axis=0)


def _swa_stack(x4, g, dtype):
    lane = _lane_iota((WINDOW, LANE))
    mine = (lane < 64) if g == 0 else (lane >= 64)
    return jnp.concatenate([jnp.where(mine, x4[:, LANE * p:LANE * (p + 1)], 0).astype(dtype) for p in range(4)], axis=0)


def _swa_unstack(ref, r, lo, hi, scale=None):
    W = WINDOW
    low = _lane_iota((W, LANE)) < 64
    for p in range(4):
        t = jnp.where(low, lo[W * p:W * (p + 1)], hi[W * p:W * (p + 1)])
        ref[0, r, LANE * p:LANE * (p + 1)] = t if scale is None else scale * t


def _swa_fwd(qs3, ks3, vs3, posb, posr, sinks):
    B, S, _ = qs3.shape
    W = WINDOW
    nb = S // W
    nh = SWA_SEQ_SPLIT
    nbh = nb // nh

    def body(q_ref, k_ref, v_ref, pb_ref, pr_ref, sink_ref, o_ref, lse_ref):
        ij = lax.broadcasted_iota(jnp.int32, (W, 2 * W), 0) - lax.broadcasted_iota(jnp.int32, (W, 2 * W), 1)
        consts = _swa_consts(sink_ref)
        n0 = pl.program_id(1) * nbh

        def blk(i, _):
            n = n0 + i
            r, _, _, q4, k2, v2, dist4 = _swa_block(q_ref, k_ref, v_ref, pb_ref, pr_ref, n, i, ij)
            o_g = []
            for g, (slope, sink) in enumerate(consts):
                s = _dot_nt(_swa_stack(q4, g, q4.dtype), k2) - slope * dist4
                m = jnp.maximum(jnp.max(s, axis=1, keepdims=True), sink)
                e = jnp.exp2(s - _wide(m))
                l = jnp.sum(e, axis=1, keepdims=True) + jnp.exp2(sink - m)
                o_g.append(_dot(_mx(e), v2) * (1.0 / l))
                lse_ref[0, i, g] = m + jnp.log2(l)
            _swa_unstack(o_ref, r, o_g[0], o_g[1])
            return 0

        for i in range(nbh):
            blk(i, 0)

    seq = lambda w: pl.BlockSpec((1, S, w), lambda b, h: (b, 0, 0))
    part = lambda w: pl.BlockSpec((1, S // nh, w), lambda b, h: (b, h, 0))
    lse_spec = pl.BlockSpec((1, nbh, 2, 4 * W, LANE), lambda b, h: (b, h, 0, 0, 0))
    return pl.pallas_call(
        body, name="swa_fwd", grid=(B, nh),
        out_shape=[jax.ShapeDtypeStruct((B, S, 512), F32), jax.ShapeDtypeStruct((B, nb, 2, 4 * W, LANE), F32)],
        in_specs=[part(512), seq(LANE), seq(LANE), part(LANE), pl.BlockSpec((1, nb, W), lambda b, h: (b, 0, 0)),
                  pl.BlockSpec((1, LANE), lambda b, h: (0, 0))],
        out_specs=[part(512), lse_spec],
        compiler_params=_cparams(("arbitrary", "arbitrary")),
    )(qs3, ks3, vs3, posb, posr, sinks)


def _swa_bwd(qs3, ks3, vs3, posb, posr, sinks, os3, do3, lse):
    B, S, _ = qs3.shape
    W = WINDOW
    nb = S // W
    nh = SWA_SEQ_SPLIT
    nbh = nb // nh
    assert nbh % 2 == 0 and nbh * nh * W == S

    def body(q_ref, k_ref, v_ref, pb_ref, pr_ref, sink_ref, o_ref, do_ref, lse_ref, dq_ref, dk_ref, dv_ref, dsink_ref,
             dkt_ref, dvt_ref):
        lane1 = _lane_iota((1, LANE))
        ij = lax.broadcasted_iota(jnp.int32, (W, 2 * W), 0) - lax.broadcasted_iota(jnp.int32, (W, 2 * W), 1)
        consts = _swa_consts(sink_ref)
        hh = pl.program_id(1)
        n0 = hh * nbh

        @pl.when(hh == 0)
        def _():
            dkt_ref[...] = jnp.zeros_like(dkt_ref)
            dvt_ref[...] = jnp.zeros_like(dvt_ref)

        @pl.when(jnp.logical_and(pl.program_id(0) == 0, hh == 0))
        def _():
            dsink_ref[...] = jnp.zeros_like(dsink_ref)

        def blk(i, dsink):
            n = n0 + i
            r, kb, _, q4, k2, v2, dist4 = _swa_block(q_ref, k_ref, v_ref, pb_ref, pr_ref, n, i, ij)
            o4, do4 = o_ref[0, r, :], do_ref[0, r, :]
            dq_g = []
            dkt = jnp.zeros((LANE, 2 * W), F32)
            dvt = jnp.zeros((LANE, 2 * W), F32)
            for g, (slope, sink) in enumerate(consts):
                q_st = _swa_stack(q4, g, F32)
                do_st = _swa_stack(do4, g, F32)
                dcol = jnp.sum(do_st * _swa_stack(o4, g, F32), axis=1, keepdims=True)
                qst, dob = _mx(q_st), _mx(do_st)
                lse_c = lse_ref[0, i, g]
                pr = jnp.exp2(_dot_nt(qst, k2) - slope * dist4 - _wide(lse_c))
                dsb = _mx(pr * (_dot_nt(dob, v2) - dcol))
                psd = jnp.exp2(sink - lse_c)[:, 0:1] * dcol
                for p in range(4):
                    dsink = dsink - jnp.where(lane1 == p + 4 * g,
                                              jnp.sum(psd[W * p:W * (p + 1)], axis=0, keepdims=True), 0.0)
                dq_g.append(_dot(dsb, k2))
                dkt = dkt + _dot(_mx(q_st.T), dsb)
                dvt = dvt + _dot(_mx(do_st.T), _mx(pr))
            _swa_unstack(dq_ref, r, dq_g[0], dq_g[1], SWA_SCALE)
            dkt_ref[kb] += dkt[:, 0:W]
            dkt_ref[kb + 1] += dkt[:, W:2 * W]
            dvt_ref[kb] += dvt[:, 0:W]
            dvt_ref[kb + 1] += dvt[:, W:2 * W]
            return dsink

        dsink = jnp.zeros((1, LANE), F32)
        for i in range(nbh):
            dsink = blk(i, dsink)
        dsink_ref[...] += dsink

        @pl.when(hh == nh - 1)
        def _():
            def flush(n, _):
                r = pl.ds(pl.multiple_of(n * W, W), W)
                dk_ref[0, r, :] = (1.0 / LOG2E) * dkt_ref[n].T
                dv_ref[0, r, :] = dvt_ref[n].T
                return 0

            lax.fori_loop(0, nb, flush, 0)

    seq = lambda w: pl.BlockSpec((1, S, w), lambda b, h: (b, 0, 0))
    part = lambda w: pl.BlockSpec((1, S // nh, w), lambda b, h: (b, h, 0))
    return pl.pallas_call(
        body, name="swa_bwd", grid=(B, nh),
        out_shape=[jax.ShapeDtypeStruct((B, S, 512), F32), jax.ShapeDtypeStruct((B, S, LANE), F32),
                   jax.ShapeDtypeStruct((B, S, LANE), F32), jax.ShapeDtypeStruct((1, LANE), F32)],
        in_specs=[part(512), seq(LANE), seq(LANE), part(LANE), pl.BlockSpec((1, nb, W), lambda b, h: (b, 0, 0)),
                  pl.BlockSpec((1, LANE), lambda b, h: (0, 0)), part(512),
                  pl.BlockSpec((1, S // nh, 512), lambda b, h: (b, h, 1)),
                  pl.BlockSpec((1, nbh, 2, 4 * W, LANE), lambda b, h: (b, h, 0, 0, 0))],
        out_specs=[part(512), seq(LANE), seq(LANE), pl.BlockSpec((1, LANE), lambda b, h: (0, 0))],
        scratch_shapes=[pltpu.VMEM((nb, LANE, W), F32), pltpu.VMEM((nb, LANE, W), F32)],
        compiler_params=_cparams(("arbitrary", "arbitrary")),
    )(qs3, ks3, vs3, posb, posr, sinks, os3, do3, lse)


def _post(om, osw, g, w_out, x2, gate, fg, tgt, S, ts):
    T, D = x2.shape
    nsb = S // ts

    def body(om_ref, os_ref, g_ref, w_ref, x_ref, gate_ref, fg_ref, t_ref,
             dx2_ref, do_ref, dg_ref, loss_ref, dfg_ref, dgate_ref, dw_ref, dwb_ref):
        i = pl.program_id(0)
        gv = g_ref[...]
        sg = 1.0 / (1.0 + jnp.exp(-gv))
        silu = gv * sg
        o = jnp.concatenate([om_ref[...], os_ref[...]], axis=1)
        ab = _mx(o * silu)
        y = _dot(ab, w_ref[...])
        gate = gate_ref[0]
        xo = x_ref[...] + gate * y
        r2 = lax.rsqrt(jnp.mean(xo * xo, axis=-1, keepdims=True) + EPS)
        xh = xo * r2
        fg = fg_ref[...]
        diff = xh * fg - t_ref[...]
        sq = jnp.sum(diff * diff, axis=0, keepdims=True)
        part = sq[:, 0:LANE]
        for t in range(1, D // LANE):
            part = part + sq[:, LANE * t:LANE * (t + 1)]
        dout = diff * (1.0 / D)
        dxh = dout * fg
        dx2 = r2 * (dxh - xh * jnp.mean(dxh * xh, axis=-1, keepdims=True))
        dx2_ref[...] = dx2
        dyb = _mx(dx2 * gate)
        da = _dot_nt(dyb, w_ref[...])
        do_ref[...] = da * silu
        dg_ref[...] = _mx(da * o * (sg * (1.0 + gv * (1.0 - sg))))

        @pl.when(i == 0)
        def _():
            loss_ref[...] = jnp.zeros_like(loss_ref)
            dfg_ref[...] = jnp.zeros_like(dfg_ref)
            dw_ref[...] = jnp.zeros_like(dw_ref)

        @pl.when(i % nsb == 0)
        def _():
            dgate_ref[...] = jnp.zeros_like(dgate_ref)

        loss_ref[...] += (0.5 / D) * part
        dfg_ref[...] += jnp.sum(dout * xh, axis=0, keepdims=True)
        dgate_ref[0] += jnp.sum(dx2 * y, axis=0, keepdims=True)
        dw_ref[...] += _dot_tn(ab, dyb)

        @pl.when(i == T // ts - 1)
        def _():
            chunks = [dw_ref[512 + 64 * PAIR_INV[h]:512 + 64 * (PAIR_INV[h] + 1), :] for h in range(SWA_HEADS)]
            for h in range(SWA_HEADS):
                dw_ref[512 + 64 * h:512 + 64 * (h + 1), :] = chunks[h]
            dwb_ref[...] = dw_ref[...].astype(jnp.bfloat16)

    row = lambda w: pl.BlockSpec((ts, w), lambda i: (i, 0))
    full = lambda a: pl.BlockSpec(a.shape, lambda i: (0,) * a.ndim)
    per_b = pl.BlockSpec((1, 1, D), lambda i: (i // nsb, 0, 0))
    return pl.pallas_call(
        body, name="post", grid=(T // ts,),
        out_shape=[jax.ShapeDtypeStruct((T, D), F32), jax.ShapeDtypeStruct((T, 1024), F32),
                   jax.ShapeDtypeStruct((T, 1024), _MXU_DTYPE), jax.ShapeDtypeStruct((1, LANE), F32),
                   jax.ShapeDtypeStruct((1, D), F32), jax.ShapeDtypeStruct(gate.shape, F32),
                   jax.ShapeDtypeStruct(w_out.shape, F32), jax.ShapeDtypeStruct(w_out.shape, jnp.bfloat16)],
        in_specs=[row(512), row(512), row(1024), full(w_out), row(D), per_b, full(fg), row(D)],
        out_specs=[row(D), row(1024), row(1024),
                   pl.BlockSpec((1, LANE), lambda i: (0, 0)), pl.BlockSpec((1, D), lambda i: (0, 0)), per_b,
                   full(w_out), full(w_out)],
        compiler_params=_cparams(("arbitrary",)),
    )(om, osw, g, w_out, x2, gate, fg, tgt)


def _pre_bwd(dq, dk, dv, dqs, dks, dvs, dg, zq, zkv, ql, kvl, x2, dx2, scale, ng, w_in, gq, gkv, w_uq, w_uk, w_uv,
             rope, S, ts):
    T, D = x2.shape
    nsb = S // ts
    WQ = MLA_HEADS * LANE

    def body(dq_ref, dk_ref, dv_ref, dqs_ref, dks_ref, dvs_ref, dg_ref, zq_ref, zkv_ref, ql_ref, kvl_ref, x_ref, dx2_ref,
             sc_ref,
             ng_ref, w_ref, gq_ref, gkv_ref, wuq_ref, wuk_ref, wuv_ref, cd_ref, sd_ref,
             gx_ref, dz_ref, dgq_ref, dgkv_ref, dng_ref, dsh_ref, dsc_ref, uq_ref, uqb_ref, ukv_ref, ukvb_ref,
             dwuq_ref, dwuk_ref, dwuv_ref):
        i = pl.program_id(0)

        @pl.when(i == 0)
        def _():
            dgq_ref[...] = jnp.zeros_like(dgq_ref)
            dgkv_ref[...] = jnp.zeros_like(dgkv_ref)
            dng_ref[...] = jnp.zeros_like(dng_ref)
            dwuq_ref[...] = jnp.zeros_like(dwuq_ref)
            dwuk_ref[...] = jnp.zeros_like(dwuk_ref)
            dwuv_ref[...] = jnp.zeros_like(dwuv_ref)

        @pl.when(i % nsb == 0)
        def _():
            dsh_ref[...] = jnp.zeros_like(dsh_ref)
            dsc_ref[...] = jnp.zeros_like(dsc_ref)

        def norm_bwd(z, dy, gain):
            r = lax.rsqrt(jnp.mean(z * z, axis=-1, keepdims=True) + EPS)
            zh = z * r
            dzh = dy * gain
            return r * (dzh - zh * jnp.mean(dzh * zh, axis=-1, keepdims=True)), jnp.sum(dy * zh, axis=0, keepdims=True)

        tables = _rope_split(cd_ref[...], sd_ref[...])
        ng = ng_ref[...]
        sc1 = 1.0 + sc_ref[0]

        def rows_chain(rs):
            cq, ck, sa, sb = (t[rs] for t in tables)
            dqr = dq_ref[rs, :].astype(F32)
            dqb = _mx(dqr * _tile_heads(cq) + pltpu.roll(dqr * _tile_heads(sa), 16, 1)
                      + pltpu.roll(dqr * _tile_heads(sb), WQ - 16, 1))
            p_uq = _dot_tn(ql_ref[rs, :], dqb)
            dzq, dgq = norm_bwd(zq_ref[rs, :], _dot_nt(dqb, wuq_ref[...]), gq_ref[...])
            dkr = dk_ref[rs, :].astype(F32)
            dkb = _mx(dkr)
            p_uk = _dot_tn(kvl_ref[rs, :], dkb)
            dvb = _mx(dv_ref[rs, :])
            p_uv = _dot_tn(kvl_ref[rs, :], dvb)
            dzkv, dgkv = norm_bwd(zkv_ref[rs, :], _dot_nt(dkb, wuk_ref[...]) + _dot_nt(dvb, wuv_ref[...]), gkv_ref[...])
            dkpe = dkr[:, 0:LANE]
            for h in range(1, MLA_HEADS):
                dkpe = dkpe + dkr[:, LANE * h:LANE * (h + 1)]
            dkro = dkpe * ck + pltpu.roll(dkpe * sa, 16, 1) + pltpu.roll(dkpe * sb, LANE - 16, 1)
            dz_ref[rs, 0:384] = _mx(dzq)
            dz_ref[rs, 384:640] = _mx(dzkv)
            dz_ref[rs, 640:768] = _mx(dkro)
            dz_ref[rs, 768:1280] = dg_ref[rs, 0:512]
            dz_ref[rs, 1280:1792] = _mx(dqs_ref[rs, :])
            dz_ref[rs, 1792:1920] = _mx(dks_ref[rs, :])
            dz_ref[rs, 1920:2048] = _mx(dvs_ref[rs, :])
            dz_ref[rs, 2048:2560] = dg_ref[rs, 512:1024]
            dh = _dot(dz_ref[rs, :], w_ref[...])
            x = x_ref[rs, :]
            r1 = lax.rsqrt(jnp.mean(x * x, axis=-1, keepdims=True) + EPS)
            xn = x * r1
            dxn = dh * ng * sc1
            gx_ref[rs, :] = dx2_ref[rs, :] + r1 * (dxn - xn * jnp.mean(dxn * xn, axis=-1, keepdims=True))
            return (p_uq, p_uk, p_uv, dgq, dgkv, jnp.sum(dh, axis=0, keepdims=True),
                    jnp.sum(dh * (xn * ng), axis=0, keepdims=True), jnp.sum(dh * xn * sc1, axis=0, keepdims=True))

        hr = ts // 2
        parts = [rows_chain(slice(hr * t, hr * (t + 1))) for t in range(2)]
        p_uq, p_uk, p_uv, dgq, dgkv, dsh, dsc, dng = (a + b for a, b in zip(*parts))
        dwuq_ref[...] += p_uq
        dwuk_ref[...] += p_uk
        dwuv_ref[...] += p_uv
        dgq_ref[...] += dgq
        dgkv_ref[...] += dgkv
        dsh_ref[0] += dsh
        dsc_ref[0] += dsc
        dng_ref[...] += dng

        @pl.when(i == T // ts - 1)
        def _():
            for h in range(MLA_HEADS):
                uq = dwuq_ref[:, LANE * h:LANE * h + MLA_QK]
                ukv = jnp.concatenate([dwuk_ref[:, LANE * h:LANE * h + MLA_NOPE], dwuv_ref[:, 64 * h:64 * (h + 1)]], axis=1)
                uq_ref[h] = uq
                uqb_ref[h] = uq.astype(jnp.bfloat16)
                ukv_ref[h] = ukv
                ukvb_ref[h] = ukv.astype(jnp.bfloat16)

    row = lambda w: pl.BlockSpec((ts, w), lambda i: (i, 0))
    full = lambda a: pl.BlockSpec(a.shape, lambda i: (0,) * a.ndim, pipeline_mode=pl.Buffered(1))
    slot = lambda r, c, dt: (jax.ShapeDtypeStruct((MLA_HEADS, r, c), dt), pl.BlockSpec((MLA_HEADS, r, c), lambda i: (0, 0, 0)))
    slots = [slot(Q_LORA, MLA_QK, F32), slot(Q_LORA, MLA_QK, jnp.bfloat16), slot(KV_LORA, LANE, F32), slot(KV_LORA, LANE, jnp.bfloat16)]
    per_b = pl.BlockSpec((1, 1, D), lambda i: (i // nsb, 0, 0))
    dense = pl.BlockSpec((ts // 8, LANE), lambda i: (i, 0))
    vec = lambda w: pl.BlockSpec((1, w), lambda i: (0, 0))
    return pl.pallas_call(
        body, name="pre_bwd", grid=(T // ts,),
        out_shape=[jax.ShapeDtypeStruct((T, D), F32), jax.ShapeDtypeStruct((T, D_IN_PAD), _MXU_DTYPE), jax.ShapeDtypeStruct((1, 384), F32),
                   jax.ShapeDtypeStruct((1, 256), F32), jax.ShapeDtypeStruct((1, D), F32),
                   jax.ShapeDtypeStruct(scale.shape, F32), jax.ShapeDtypeStruct(scale.shape, F32)] + [t[0] for t in slots],
        in_specs=[row(WQ), row(WQ), row(512), row(512), row(LANE), row(LANE), row(1024), row(384), row(256), row(384),
                  row(256), row(D), row(D), per_b, full(ng), full(w_in), full(gq), full(gkv), full(w_uq), full(w_uk), full(w_uv),
                  dense, dense],
        out_specs=[row(D), row(D_IN_PAD), vec(384), vec(256), vec(D), per_b, per_b] + [t[1] for t in slots],
        scratch_shapes=[pltpu.VMEM(w_uq.shape, F32), pltpu.VMEM(w_uk.shape, F32), pltpu.VMEM(w_uv.shape, F32)],
        compiler_params=_cparams(("arbitrary",)),
    )(dq, dk, dv, dqs, dks, dvs, dg, zq, zkv, ql, kvl, x2, dx2, scale, ng, w_in, gq, gkv, w_uq, w_uk, w_uv, *rope)


def _w_in_row_runs():
    runs = [(0, 0, 640), (640, 704, 32), (672, 768, 512)]
    runs += [(1184 + 64 * h, 1280 + 64 * PAIR_INV[h], 64) for h in range(8)]
    runs += [(1696, 1792, 256)]
    runs += [(1952 + 64 * h, 2048 + 64 * PAIR_INV[h], 64) for h in range(8)]
    return runs


W_IN_SHARD = D_IN // N_DEV
W_IN_SLOT = 320


def _dw_in_t(dz, hb, tn, tk):
    T, M = dz.shape
    N = hb.shape[1]
    nk = T // tk

    def body(a_ref, b_ref, o_ref, ob_ref, acc_ref):
        k = pl.program_id(1)

        @pl.when(k == 0)
        def _():
            acc_ref[...] = jnp.zeros_like(acc_ref)

        acc_ref[...] += _dot_tn(a_ref[...], b_ref[...])

        @pl.when(k == nk - 1)
        def _():
            for d in range(N_DEV):
                lo, hi = W_IN_SHARD * d, W_IN_SHARD * (d + 1)
                for nat, pad, size in _w_in_row_runs():
                    a, b = max(nat, lo), min(nat + size, hi)
                    if a < b:
                        piece = acc_ref[pad + a - nat:pad + b - nat, :]
                        o_ref[d, a - lo:b - lo, :] = piece
                        ob_ref[d, a - lo:b - lo, :] = piece.astype(ob_ref.dtype)
                o_ref[d, W_IN_SHARD:W_IN_SLOT, :] = jnp.zeros((W_IN_SLOT - W_IN_SHARD, tn), F32)
                ob_ref[d, W_IN_SHARD:W_IN_SLOT, :] = jnp.zeros((W_IN_SLOT - W_IN_SHARD, tn), ob_ref.dtype)

    slots = pl.BlockSpec((N_DEV, W_IN_SLOT, tn), lambda j, k: (0, 0, j))
    return pl.pallas_call(
        body, name="dw_in", grid=(N // tn, nk),
        out_shape=[jax.ShapeDtypeStruct((N_DEV, W_IN_SLOT, N), F32), jax.ShapeDtypeStruct((N_DEV, W_IN_SLOT, N), jnp.bfloat16)],
        in_specs=[pl.BlockSpec((tk, M), lambda j, k: (k, 0)), pl.BlockSpec((tk, tn), lambda j, k: (k, j))],
        out_specs=[slots, slots],
        scratch_shapes=[pltpu.VMEM((M, tn), F32)],
        compiler_params=_cparams(("arbitrary", "arbitrary")),
    )(dz, hb)


SHARDED = (3, 6, 7, 9)
PART_SLICES = {10: (128, 1152), 2: (1152, 2176), 4: (2176, 2560), 5: (2560, 2816), 8: (2816, 2824)}


def _finalize_adamw(parts_all, dmod_all, dmod_cols, c_all, ws, sharded_grads, ms, vs):
    n = len(ws)
    local = [t for t in range(n) if t not in SHARDED[1:]]

    def body(*refs):
        p_ref, dm_ref, dmc_ref, c_ref = refs[:4]
        w_refs, gs_refs, m_refs, v_refs = refs[4:4 + n], refs[4 + n:8 + n], refs[8 + n:8 + 2 * n], refs[8 + 2 * n:8 + 3 * n]
        outs = refs[8 + 3 * n:]
        loss_ref, gl_refs = outs[0], outs[1:1 + len(local)]
        d_refs, nm_refs, nv_refs = (outs[1 + len(local) + t * n:1 + len(local) + (t + 1) * n] for t in range(3))
        ps_ref = outs[-1]
        rd = lambda r: r[:, 0, :] if len(r.shape) == 3 else r[...]

        def wr(r, val):
            if len(r.shape) == 3:
                r[:, 0, :] = val
            else:
                r[...] = val

        acc = p_ref[0]
        for j in range(1, N_DEV):
            acc = acc + p_ref[j]
        ps_ref[...] = acc
        loss_ref[...] = jnp.sum(acc[:, 0:LANE], axis=1, keepdims=True)
        db = dm_ref[0:1, :]
        for j in range(1, dm_ref.shape[0]):
            db = db + dm_ref[j:j + 1, :]
        grads = {0: _dot_tn(_mx(_silu(c_ref[...])), _mx(dmc_ref[...])), 1: db}
        for t, (lo, hi) in PART_SLICES.items():
            grads[t] = ps_ref[:, lo:hi]
        for i, t in enumerate(SHARDED):
            grads[t] = gs_refs[i][...]
        grads[SHARDED[0]] = gs_refs[0][0:W_IN_SHARD, :]
        for i, t in enumerate(local):
            wr(gl_refs[i], grads[t])
        for t in range(n):
            gv = grads[t]
            nm = ADAM_B1 * rd(m_refs[t]) + (1.0 - ADAM_B1) * gv
            nv = ADAM_B2 * rd(v_refs[t]) + (1.0 - ADAM_B2) * (gv * gv)
            m_hat = nm / (1.0 - ADAM_B1 ** ADAM_STEP)
            v_hat = nv / (1.0 - ADAM_B2 ** ADAM_STEP)
            wr(d_refs[t], -ADAM_LR * (m_hat / (jnp.sqrt(v_hat) + ADAM_EPS) + ADAM_WD * rd(w_refs[t])))
            wr(nm_refs[t], nm)
            wr(nv_refs[t], nv)

    like = lambda arrs: [jax.ShapeDtypeStruct(a.shape, F32) for a in arrs]
    out = pl.pallas_call(
        body, name="finalize_adamw",
        out_shape=[jax.ShapeDtypeStruct((1, 1), F32)] + like([ws[t] for t in local]) + like(ws) * 3,
        in_specs=[_vmem()] * (8 + 3 * n), out_specs=[_vmem()] * (1 + len(local) + 3 * n),
        scratch_shapes=[pltpu.VMEM(parts_all.shape[1:], F32)],
        compiler_params=_cparams(),
    )(parts_all, dmod_all, dmod_cols, c_all, *ws, *sharded_grads, *ms, *vs)
    k = 1 + len(local)
    return out[0], dict(zip(local, out[1:k])), out[k:k + n], out[k + n:k + 2 * n], out[k + 2 * n:]


def _pair_perm(a, axis, order):
    a = jnp.moveaxis(a, axis, -1)
    lead = a.shape[:-1]
    a = a.reshape(lead + (8, 64))[..., list(order), :].reshape(lead + (512,))
    return jnp.moveaxis(a, -1, axis)


def _rope_table(positions):
    T = positions.size
    inv = ROPE_THETA ** (-jnp.arange(0, MLA_ROPE, 2, dtype=F32) / MLA_ROPE)
    pos = jnp.repeat(positions.reshape(T // 8, 8)[:, ::-1].astype(F32), MLA_ROPE // 2, axis=1)
    ang = pos * jnp.tile(inv, 8)[None, :]
    return jnp.cos(ang), jnp.sin(ang)


def _local_step(x, mod, positions, ng, w_in_t, gq, gkv, w_uq, w_ukv, sinks, w_out, fg, tgt,
                ts=512, fq=512, fk=512, bq=512, bk=512):
    B, S, D = x.shape
    T = B * S
    x2 = x.reshape(T, D)
    shift, scale, gate = (mod[:, None, k * D:(k + 1) * D] for k in range(3))
    w_uq_p = jnp.pad(w_uq.reshape(Q_LORA, MLA_HEADS, 96), ((0, 0), (0, 0), (0, 32))).reshape(Q_LORA, MLA_HEADS * LANE)
    w_ukv3 = w_ukv.reshape(KV_LORA, MLA_HEADS, 128)
    w_uk_p = jnp.pad(w_ukv3[:, :, :64], ((0, 0), (0, 0), (0, 64))).reshape(KV_LORA, MLA_HEADS * LANE)
    w_uv = w_ukv3[:, :, 64:].reshape(KV_LORA, 512)
    w_out_p = jnp.concatenate([w_out[:512], _pair_perm(w_out[512:], 0, PAIR_ORDER)], axis=0)
    rope = _rope_table(positions)
    posf = positions.astype(F32)
    posb = jnp.broadcast_to(posf[:, :, None], (B, S, LANE))
    posr = posf.reshape(B, S // WINDOW, WINDOW)
    sinks_l = jnp.pad(sinks.reshape(1, SWA_HEADS), ((0, 0), (0, LANE - SWA_HEADS)))

    (hb, zq, zkv, ql, kvl, q, k, v, qs, ks, vs, g, w_in_p) = _pre_fwd(
        x2, shift, scale, ng, w_in_t, gq, gkv, w_uq_p, w_uk_p, w_uv, rope, S, ts)
    r3 = lambda a: a.reshape(B, S, a.shape[-1])
    om, lse_m = _mla_fwd(r3(q), r3(k), r3(v), fq, fk)
    osw, lse_s = _swa_fwd(r3(qs), r3(ks), r3(vs), posb, posr, sinks_l)
    dx2, do, dg, loss_v, dfg, dgate, *dw_out = _post(
        om.reshape(T, 512), osw.reshape(T, 512), g, w_out_p, x2, gate, fg.reshape(1, D), tgt.reshape(T, D), S, ts)
    do3 = r3(do)
    dq, dk, dv = _mla_bwd(r3(q), r3(k), r3(v), om, do3, lse_m, bq, bk)
    dqs, dks, dvs, dsink = _swa_bwd(r3(qs), r3(ks), r3(vs), posb, posr, sinks_l, osw, do3, lse_s)
    f2 = lambda a: a.reshape(T, a.shape[-1])
    gx, dz, dgq, dgkv, dng, dsh, dsc, *dw_u = _pre_bwd(
        f2(dq), f2(dk), f2(dv), f2(dqs), f2(dks), f2(dvs), dg, zq, zkv, ql, kvl, x2, dx2, scale, ng, w_in_p, gq, gkv,
        w_uq_p, w_uk_p, w_uv, rope, S, ts)
    tk = min(T, 1024)
    dw_in_t = _dw_in_t(dz, hb, 512, tk)
    parts = jnp.concatenate([loss_v, dfg, dng, dgq, dgkv, dsink], axis=1)
    dmod = jnp.concatenate([dsh, dsc, dgate], axis=2).reshape(B, 3 * D)
    return gx.reshape(B, S, D), dw_in_t, tuple(dw_u[0:2]), tuple(dw_u[2:4]), tuple(dw_out), parts, dmod


def kernel(x, c, positions, w_ada, b_ada, norm_gain, w_in, q_norm_gain, kv_norm_gain, w_uq, w_ukv, swa_sinks, w_out, final_gain, loss_target, m_w_ada, m_b_ada, m_norm_gain, m_w_in, m_q_norm_gain, m_kv_norm_gain, m_w_uq, m_w_ukv, m_swa_sinks, m_w_out, m_final_gain, v_w_ada, v_b_ada, v_norm_gain, v_w_in, v_q_norm_gain, v_kv_norm_gain, v_w_uq, v_w_ukv, v_swa_sinks, v_w_out, v_final_gain):
    B, S, D = x.shape
    me = 4 * lax.axis_index("x") + 2 * lax.axis_index("y") + lax.axis_index("c")
    bf = _MXU_DTYPE

    ncol = w_ada.shape[2]
    b_cols = lax.dynamic_slice_in_dim(b_ada, me * ncol, ncol, axis=1)
    c_all, win_g, wuq_g, wukv_g, wout_g, mod_g = _all_gather(
        [c, (jnp.transpose(w_in, (2, 0, 1)), bf), (w_uq[0], bf), (w_ukv[0], bf), (w_out[0], bf)], "ag_weights",
        fused=((w_ada[0], b_cols), _ada_cols, jax.ShapeDtypeStruct((N_DEV, B, ncol), F32)))
    c_all = c_all.reshape(N_DEV * B, D)
    cat_cols = lambda a: jnp.transpose(a, (1, 0, 2)).reshape(a.shape[1], N_DEV * a.shape[2])
    w_in_t, w_uq_f, w_ukv_f = win_g, cat_cols(wuq_g), cat_cols(wukv_g)
    w_out_f = wout_g.reshape(D, D)

    mod = lax.dynamic_index_in_dim(mod_g, me, axis=1, keepdims=False)
    mod = jnp.transpose(mod, (1, 0, 2)).reshape(B, 3 * D)

    gx, dw_in_t, dw_uq, dw_ukv, dw_out, parts, dmod = _local_step(
        x, mod, positions, norm_gain, w_in_t, q_norm_gain, kv_norm_gain, w_uq_f, w_ukv_f, swa_sinks,
        w_out_f, final_gain, loss_target)

    slots = lambda pair: tuple(a.reshape((4, 2, -1) + a.shape[-1:]) for a in pair)
    g_w_in_t, g_w_uq, g_w_ukv, g_w_out, parts_g, dmod_g = _reduce_scatter(
        [slots(dw_in_t), slots(dw_uq), slots(dw_ukv), slots(dw_out)], "rs_grads",
        gather=(parts, dmod))

    dmod_all = dmod_g.reshape(N_DEV * B, 3 * D)
    dmod_cols = lax.dynamic_slice_in_dim(dmod_all, me * ncol, ncol, axis=1)
    ws = [w_ada, b_ada, norm_gain, w_in, q_norm_gain, kv_norm_gain, w_uq, w_ukv, swa_sinks, w_out, final_gain]
    ms = [m_w_ada, m_b_ada, m_norm_gain, m_w_in, m_q_norm_gain, m_kv_norm_gain, m_w_uq, m_w_ukv, m_swa_sinks, m_w_out,
          m_final_gain]
    vs = [v_w_ada, v_b_ada, v_norm_gain, v_w_in, v_q_norm_gain, v_kv_norm_gain, v_w_uq, v_w_ukv, v_swa_sinks, v_w_out,
          v_final_gain]
    two_d = lambda a: a.reshape((1, a.shape[0]) if a.ndim == 1 else a.shape[-2:])
    flat = lambda arrs: [jnp.transpose(a, (2, 0, 1)) if t == SHARDED[0] else two_d(a) for t, a in enumerate(arrs)]
    sharded = {3: g_w_in_t, 6: g_w_uq, 7: g_w_ukv, 9: g_w_out}
    loss, local_grads, deltas, new_ms, new_vs = _finalize_adamw(
        parts_g, dmod_all, dmod_cols, c_all, flat(ws), [sharded[t] for t in SHARDED], flat(ms), flat(vs))
    grads = [local_grads[t] if t in local_grads else sharded[t] for t in range(len(ws))]
    shaped = lambda arrs: [jnp.transpose(a, (1, 2, 0)) if t == SHARDED[0] else a.reshape(ws[t].shape) for t, a in enumerate(arrs)]
    return (loss.reshape(()), gx, *shaped(grads), *shaped(deltas), *shaped(new_ms), *shaped(new_vs))
```

```python
import functools

import jax
import jax.numpy as jnp
from jax import lax
from jax.experimental import pallas as pl
from jax.experimental.pallas import tpu as pltpu

F32 = jnp.float32
_MXU_DTYPE = jnp.bfloat16
MLA_GRAD_DTYPE = jnp.bfloat16

N_DEV = 8
MLA_HEADS = 8
MLA_NOPE = 64
MLA_ROPE = 32
MLA_QK = MLA_NOPE + MLA_ROPE
Q_LORA = 384
KV_LORA = 256
SWA_HEADS = 8
SWA_KV_HEADS = 2
SWA_HEAD_DIM = 64
WINDOW = 128
ROPE_THETA = 10000.0
EPS = 1e-6
MLA_SCALE = float((MLA_NOPE + MLA_ROPE) ** -0.5)
SWA_SCALE = float(SWA_HEAD_DIM ** -0.5)
LOG2E = 1.4426950408889634
MLA_QSCALE = MLA_SCALE * LOG2E
D_IN = 2464
D_IN_PAD = 2560
PAIR_ORDER = (0, 4, 1, 5, 2, 6, 3, 7)
PAIR_INV = (0, 2, 4, 6, 1, 3, 5, 7)

ADAM_LR = 0.001
ADAM_B1 = 0.9
ADAM_B2 = 0.999
ADAM_EPS = 1e-08
ADAM_WD = 0.01
ADAM_STEP = 10

LANE = 128
VMEM_LIMIT = 56 * 1024 * 1024

MESH = pl.DeviceIdType.MESH
NEG_INF = float("-inf")
SWA_SEQ_SPLIT = 2


def _mx(a):
    return a.astype(_MXU_DTYPE)


def _dot(a, b):
    return jnp.dot(a, b, preferred_element_type=F32)


def _dot_nt(a, b):
    return lax.dot_general(a, b, (((1,), (1,)), ((), ())), preferred_element_type=F32)


def _dot_tn(a, b):
    return lax.dot_general(a, b, (((0,), (0,)), ((), ())), preferred_element_type=F32)


def _cparams(sem=None):
    return pltpu.CompilerParams(dimension_semantics=sem, vmem_limit_bytes=VMEM_LIMIT)


def _vmem():
    return pl.BlockSpec(memory_space=pltpu.VMEM)


def _lane_iota(shape):
    return lax.broadcasted_iota(jnp.int32, shape, len(shape) - 1)


def _gather_program(srcs, outs, send_sems, recv_sems, local_sems):
    x, y, c = lax.axis_index("x"), lax.axis_index("y"), lax.axis_index("c")
    me, sibling = (x, y, c), (x, y, 1 - c)
    chips = [(1 - x, y), (x, 1 - y), (1 - x, 1 - y)]

    def slot(a, dev):
        return outs[a].at[4 * dev[0] + 2 * dev[1] + dev[2]]

    def copy(a, k, block, to, src=None):
        return pltpu.make_async_remote_copy(
            src_ref=slot(a, block) if src is None else src, dst_ref=slot(a, block),
            send_sem=send_sems.at[7 * a + k], recv_sem=recv_sems.at[7 * a + k],
            device_id=to, device_id_type=MESH)

    def local(a):
        return pltpu.make_async_copy(srcs[a], slot(a, me), local_sems.at[a])

    def start(a):
        local(a).start()
        cps = [copy(a, 0, me, sibling, src=srcs[a])]
        cps += [copy(a, 1 + j, me, (*chip, c), src=srcs[a]) for j, chip in enumerate(chips)]
        for cp in cps:
            cp.start()
        return cps

    def finish(group):
        cps = []
        for j, chip in enumerate(chips):
            for a in group:
                copy(a, 1 + j, (*chip, c), me).wait_recv()
                cp = copy(a, 4 + j, (*chip, c), sibling)
                cp.start()
                cps.append(cp)
        for a in group:
            copy(a, 0, sibling, me).wait_recv()
            for j, chip in enumerate(chips):
                copy(a, 4 + j, (*chip, 1 - c), me).wait_recv()
            local(a).wait()
        return cps

    return start, finish


def _gather_sems(m):
    return [pltpu.SemaphoreType.DMA((7 * m,)), pltpu.SemaphoreType.DMA((7 * m,)), pltpu.SemaphoreType.DMA((m,))]


def _all_gather(arrs, name, fused=None):
    n = len(arrs)
    cast = {a: e[1] for a, e in enumerate(arrs) if isinstance(e, tuple)}
    arrs = [e[0] if isinstance(e, tuple) else e for e in arrs]
    shapes = [(a.shape[0], a.shape[-1]) for a in arrs]
    extra, fn, piece = fused if fused else ((), None, None)
    ne, m = len(extra), n + (1 if fused else 0)

    def body(*refs):
        ins, ex, outs = refs[:n], refs[n:n + ne], refs[n + ne:n + ne + m]
        rest = refs[n + ne + m:]
        staged = dict(zip(cast, rest))
        srcs = [staged.get(a, ins[a]) for a in range(n)] + ([rest[len(cast)]] if fused else [])
        start, finish = _gather_program(srcs, outs, *rest[-3:])
        pending = []
        for a in range(n):
            if a in staged:
                staged[a][...] = (ins[a][:, 0, :] if len(ins[a].shape) == 3 else ins[a][...]).astype(cast[a])
            pending += start(a)
        if fused:
            pending += finish([0])
            fn(srcs[n], outs[0], *ex)
            pending += start(n)
            pending += finish([n] + list(range(1, n)))
        else:
            pending += finish(list(range(n)))
        for cp in pending:
            cp.wait_send()

    out_shape = [jax.ShapeDtypeStruct((N_DEV,) + shapes[a], cast.get(a, arrs[a].dtype)) for a in range(n)]
    scratch = [pltpu.VMEM(shapes[a], cast[a]) for a in cast]
    if fused:
        out_shape.append(jax.ShapeDtypeStruct((N_DEV,) + piece.shape, piece.dtype))
        scratch.append(pltpu.VMEM(piece.shape, piece.dtype))
    return pl.pallas_call(
        body, name=name, out_shape=out_shape,
        in_specs=[_vmem()] * (n + ne), out_specs=[_vmem()] * m,
        scratch_shapes=scratch + _gather_sems(m),
        compiler_params=pltpu.CompilerParams(vmem_limit_bytes=VMEM_LIMIT),
    )(*arrs, *extra)


def _reduce_scatter(arrs, name, gather=()):
    n, g = len(arrs), len(gather)
    halves = [a[1] if isinstance(a, tuple) else a.astype(jnp.bfloat16) for a in arrs]
    arrs = [a[0] if isinstance(a, tuple) else a for a in arrs]

    def body(*refs):
        xs, xbs, gins = refs[:n], refs[n:2 * n], refs[2 * n:2 * n + g]
        outs, gouts = refs[2 * n + g:3 * n + g], refs[3 * n + g:3 * n + 2 * g]
        rest = refs[3 * n + 2 * g:]
        parts, recv_a, send_b, recv_b = (rest[t * n:(t + 1) * n] for t in range(4))
        send_sems, recv_sems, local_sems = rest[4 * n:4 * n + 3]
        x, y, c = lax.axis_index("x"), lax.axis_index("y"), lax.axis_index("c")
        myq = 2 * x + y
        gather_start, gather_finish = _gather_program(gins, gouts, *rest[4 * n + 3:])

        def chip(k):
            return (1 - x if k & 2 else x, 1 - y if k & 1 else y)

        def from_sibling(a):
            return pltpu.make_async_remote_copy(
                src_ref=xbs[a].at[:, 1 - c], dst_ref=recv_a[a], send_sem=send_sems.at[4 * a], recv_sem=recv_sems.at[4 * a],
                device_id=(x, y, 1 - c), device_id_type=MESH)

        def to_owner(a, k):
            qx, qy = chip(k)
            return pltpu.make_async_remote_copy(
                src_ref=send_b[a].at[2 * qx + qy], dst_ref=recv_b[a].at[myq],
                send_sem=send_sems.at[4 * a + k], recv_sem=recv_sems.at[4 * a + k],
                device_id=(qx, qy, c), device_id_type=MESH)

        mine = [pltpu.make_async_copy(xs[a].at[:, c], parts[a], local_sems.at[a]) for a in range(n)]
        first = [from_sibling(a) for a in range(n)]
        for cp in mine + first:
            cp.start()
        second = []
        for b in range(g):
            second += gather_start(b)
        for a in range(n):
            mine[a].wait()
            first[a].wait_recv()
            parts[a][...] = parts[a][...] + recv_a[a][...].astype(F32)
            send_b[a][...] = parts[a][...].astype(jnp.bfloat16)
            for k in range(1, 4):
                cp = to_owner(a, k)
                cp.start()
                second.append(cp)
        second += gather_finish(list(range(g)))
        for a in range(n):
            acc = parts[a][myq]
            for k in range(1, 4):
                to_owner(a, k).wait_recv()
                qx, qy = chip(k)
                acc = acc + recv_b[a][2 * qx + qy].astype(F32)
            outs[a][...] = acc
        for cp in first + second:
            cp.wait_send()

    quarter = lambda a, dt: pltpu.VMEM((4,) + a.shape[2:], dt)
    return pl.pallas_call(
        body, name=name,
        out_shape=[jax.ShapeDtypeStruct(a.shape[2:], F32) for a in arrs]
        + [jax.ShapeDtypeStruct((N_DEV,) + a.shape, a.dtype) for a in gather],
        in_specs=[pl.BlockSpec(memory_space=pl.ANY)] * (2 * n) + [_vmem()] * g, out_specs=[_vmem()] * (n + g),
        scratch_shapes=[quarter(a, F32) for a in arrs] + [quarter(a, jnp.bfloat16) for a in arrs] * 3
        + [pltpu.SemaphoreType.DMA((4 * n,)), pltpu.SemaphoreType.DMA((4 * n,)), pltpu.SemaphoreType.DMA((n,))]
        + _gather_sems(g),
        compiler_params=pltpu.CompilerParams(vmem_limit_bytes=VMEM_LIMIT),
    )(*arrs, *halves, *gather)


def _silu(t):
    return t * (1.0 / (1.0 + jnp.exp(-t)))


def _ada_cols(piece_ref, c_all_ref, w_ref, b_ref):
    w = _mx(w_ref[...])
    for d in range(N_DEV):
        piece_ref[d] = _dot(_mx(_silu(c_all_ref[d])), w) + b_ref[...]


def _rope_split(cd, sd):
    n = cd.shape[0]

    def expand(d):
        rep = jnp.broadcast_to(d[:, None, :], (n, 8, LANE)).reshape(8 * n, LANE)
        return pltpu.roll(rep, 0, 1, stride=16, stride_axis=0)

    c, s = expand(cd), expand(sd)
    lane = _lane_iota(c.shape)
    first = jnp.logical_and(lane >= 64, lane < 80)
    second = jnp.logical_and(lane >= 80, lane < 96)
    ck = jnp.where(first, pltpu.roll(c, 80, 1), jnp.where(second, pltpu.roll(c, 96, 1), 0.0))
    cq = jnp.where(lane < 64, 1.0, ck)
    sa = jnp.where(first, -pltpu.roll(s, 80, 1), 0.0)
    sb = jnp.where(second, pltpu.roll(s, 96, 1), 0.0)
    return cq, ck, sa, sb


def _tile_heads(t):
    return jnp.concatenate([t] * MLA_HEADS, axis=1)


def _pre_fwd(x2, shift, scale, ng, w_in, gq, gkv, w_uq, w_ukv, rope, S, ts):
    T, D = x2.shape
    nsb = S // ts
    WQ = MLA_HEADS * LANE

    def body(x_ref, sh_ref, sc_ref, ng_ref, wn_ref, gq_ref, gkv_ref, uqn_ref, ukvn_ref,
             cd_ref, sd_ref,
             hb_ref, zq_ref, zkv_ref, ql_ref, kvl_ref, q_ref, k_ref, v_ref, qs_ref, ks_ref, vs_ref, g_ref, wp_ref,
             wuq_ref, wuk_ref, wuv_ref):
        @pl.when(pl.program_id(0) == 0)
        def _():
            for h in range(MLA_HEADS):
                wuq_ref[:, LANE * h:LANE * h + MLA_QK] = uqn_ref[h]
                wuq_ref[:, LANE * h + MLA_QK:LANE * (h + 1)] = jnp.zeros((Q_LORA, LANE - MLA_QK), wuq_ref.dtype)
                wuk_ref[:, LANE * h:LANE * h + MLA_NOPE] = ukvn_ref[h, :, 0:MLA_NOPE]
                wuk_ref[:, LANE * h + MLA_NOPE:LANE * (h + 1)] = jnp.zeros((KV_LORA, LANE - MLA_NOPE), wuk_ref.dtype)
                wuv_ref[:, 64 * h:64 * (h + 1)] = ukvn_ref[h, :, MLA_NOPE:LANE]
            wp_ref[640:704, :] = jnp.zeros((64, D), wp_ref.dtype)
            wp_ref[736:768, :] = jnp.zeros((32, D), wp_ref.dtype)
            for d in range(N_DEV):
                lo, hi = W_IN_SHARD * d, W_IN_SHARD * (d + 1)
                for nat, pad, size in _w_in_row_runs():
                    a, b = max(nat, lo), min(nat + size, hi)
                    if a < b:
                        wp_ref[pad + a - nat:pad + b - nat, :] = wn_ref[d, a - lo:b - lo, :]

        w_ref = wp_ref
        x = x_ref[...]
        r1 = lax.rsqrt(jnp.mean(x * x, axis=-1, keepdims=True) + EPS)
        h = ((x * r1) * ng_ref[...]) * (1.0 + sc_ref[0]) + sh_ref[0]
        hb = _mx(h)
        hb_ref[...] = hb
        zq = _dot_nt(hb, w_ref[0:384, :])
        zq_ref[...] = zq
        rq = lax.rsqrt(jnp.mean(zq * zq, axis=-1, keepdims=True) + EPS)
        ql = _mx((zq * rq) * gq_ref[...])
        ql_ref[...] = ql
        q = _dot(ql, wuq_ref[...])
        cq, ck, sa, sb = _rope_split(cd_ref[...], sd_ref[...])
        q = (q * _tile_heads(cq) + pltpu.roll(q, WQ - 16, 1) * _tile_heads(sa)
             + pltpu.roll(q, 16, 1) * _tile_heads(sb))
        q_ref[...] = _mx(q * MLA_QSCALE)
        zkv = _dot_nt(hb, w_ref[384:640, :])
        zkv_ref[...] = zkv
        rkv = lax.rsqrt(jnp.mean(zkv * zkv, axis=-1, keepdims=True) + EPS)
        kvl = _mx((zkv * rkv) * gkv_ref[...])
        kvl_ref[...] = kvl
        kr = _dot_nt(hb, w_ref[640:768, :])
        kpe = kr * ck + pltpu.roll(kr, LANE - 16, 1) * sa + pltpu.roll(kr, 16, 1) * sb
        kf = _dot(kvl, wuk_ref[...])
        k_ref[...] = _mx(kf + jnp.concatenate([kpe] * MLA_HEADS, axis=1))
        v_ref[...] = _mx(_dot(kvl, wuv_ref[...]))
        g_ref[:, 0:512] = _dot_nt(hb, w_ref[768:1280, :])
        qs_ref[...] = _mx(_dot_nt(hb, w_ref[1280:1792, :]) * (SWA_SCALE * LOG2E))
        ks_ref[...] = _mx(_dot_nt(hb, w_ref[1792:1920, :]))
        vs_ref[...] = _mx(_dot_nt(hb, w_ref[1920:2048, :]))
        g_ref[:, 512:1024] = _dot_nt(hb, w_ref[2048:2560, :])

    row = lambda w: pl.BlockSpec((ts, w), lambda i: (i, 0))
    dense = pl.BlockSpec((ts // 8, LANE), lambda i: (i, 0))
    full = lambda a: pl.BlockSpec(a.shape, lambda i: (0,) * a.ndim)
    per_b = pl.BlockSpec((1, 1, D), lambda i: (i // nsb, 0, 0))
    out_w = [(D, _MXU_DTYPE), (384, F32), (256, F32), (384, _MXU_DTYPE), (256, _MXU_DTYPE), (WQ, _MXU_DTYPE),
             (WQ, _MXU_DTYPE), (512, _MXU_DTYPE), (512, _MXU_DTYPE), (128, _MXU_DTYPE), (128, _MXU_DTYPE), (1024, F32)]
    built = [(D_IN_PAD, D), (Q_LORA, WQ), (KV_LORA, WQ), (KV_LORA, 512)]
    return pl.pallas_call(
        body, name="pre_fwd", grid=(T // ts,),
        out_shape=[jax.ShapeDtypeStruct((T, w), dt) for w, dt in out_w]
        + [jax.ShapeDtypeStruct(s, w_in.dtype) for s in built],
        in_specs=[row(D), per_b, per_b, full(ng), full(w_in), full(gq), full(gkv), full(w_uq), full(w_ukv), dense, dense],
        out_specs=[row(w) for w, _ in out_w] + [pl.BlockSpec(s, lambda i: (0, 0)) for s in built],
        compiler_params=_cparams(("arbitrary",)),
    )(x2, shift, scale, ng, w_in, gq, gkv, w_uq, w_ukv, *rope)


def _mla_fwd(q3, k3, v3, tq, tk):
    B, S, _ = q3.shape
    nq = S // tq
    assert tq == tk
    HPS = 2
    NH = 2 * HPS

    def body(q_ref, k_ref, v_ref, o_ref, lse_ref):
        rows = lax.broadcasted_iota(jnp.int32, (tq, tk), 0)
        cols = lax.broadcasted_iota(jnp.int32, (tq, tk), 1)
        low_k = _lane_iota((tk, LANE)) < 64
        low = _lane_iota((tq, LANE)) < 64

        def step(qs, kt, carry, masked):
            r0 = kt * tk
            out = []
            for j in range(NH):
                v2 = v_ref[0, r0:r0 + tk, LANE * (j // 2):LANE * (j // 2 + 1)]
                vj = (jnp.where(low_k, v2, 1.0) if j % 2 == 0 else jnp.where(low_k, 1.0, v2)).astype(v2.dtype)
                m, acc = carry[j]
                s = _dot_nt(qs[j], k_ref[0, r0:r0 + tk, LANE * j:LANE * (j + 1)])
                if masked:
                    s = jnp.where(rows >= cols, s, NEG_INF)
                m_new = jnp.maximum(m, jnp.max(s, axis=1, keepdims=True))
                alpha = jnp.exp2(m - m_new)
                p = jnp.exp2(s - m_new)
                acc = alpha * acc + _dot(_mx(p), vj)
                out.append((m_new, acc))
            return tuple(out)

        for qi in range(nq):
            r = slice(qi * tq, (qi + 1) * tq)
            qs = [q_ref[0, r, LANE * j:LANE * (j + 1)] for j in range(NH)]
            init = (jnp.full((tq, 1), NEG_INF, F32), jnp.zeros((tq, LANE), F32))
            carry = (init,) * NH
            for kt in range(qi):
                carry = step(qs, kt, carry, False)
            carry = step(qs, qi, carry, True)
            for t in range(HPS):
                (m0, a0), (m1, a1) = carry[2 * t], carry[2 * t + 1]
                l0 = jnp.where(low, pltpu.roll(a0, 64, 1), a0)
                l1 = jnp.where(low, a1, pltpu.roll(a1, 64, 1))
                o_ref[0, r, LANE * t:LANE * (t + 1)] = jnp.where(low, a0 / l0, a1 / l1)
                lse_ref[0, 2 * t, r, :] = m0 + jnp.log2(l0)
                lse_ref[0, 2 * t + 1, r, :] = m1 + jnp.log2(l1)

    grp = lambda w: pl.BlockSpec((1, S, w), lambda b, hp: (b, 0, hp))
    return pl.pallas_call(
        body, name="mla_fwd", grid=(B, MLA_HEADS // NH),
        out_shape=[jax.ShapeDtypeStruct((B, S, 512), F32), jax.ShapeDtypeStruct((B, MLA_HEADS, S, LANE), F32)],
        in_specs=[grp(NH * LANE), grp(NH * LANE), grp(HPS * LANE)],
        out_specs=[grp(HPS * LANE), pl.BlockSpec((1, NH, S, LANE), lambda b, hp: (b, hp, 0, 0))],
        compiler_params=_cparams(("arbitrary", "arbitrary")),
    )(q3, k3, v3)


def _mla_bwd(q3, k3, v3, o3, do3, lse, tq, tk, hps=2):
    B, S, _ = q3.shape
    nq, nk = S // tq, S // tk
    nh = 2 * hps
    assert tk % tq == 0

    def body(q_ref, k_ref, v_ref, o_ref, do_ref, lse_ref, dq_ref, dk_ref, dv_ref, dkt_ref, dvt_ref):
        rows = lax.broadcasted_iota(jnp.int32, (tq, tk), 0)
        cols = lax.broadcasted_iota(jnp.int32, (tq, tk), 1)
        lane = _lane_iota((tq, LANE))

        def q_tile(qi, _):
            r = pl.ds(pl.multiple_of(qi * tq, tq), tq)
            heads = []
            for j in range(nh):
                lanes = slice(LANE * j, LANE * (j + 1))
                pair = slice(LANE * (j // 2), LANE * (j // 2 + 1))
                do = jnp.where((lane < 64) if j % 2 == 0 else (lane >= 64), do_ref[0, r, pair], 0.0)
                q = q_ref[0, r, lanes]
                heads.append((lanes, pair, q, _mx(q.astype(F32).T), _mx(do), _mx(do.T),
                              jnp.sum(do * o_ref[0, r, pair], axis=1, keepdims=True),
                              jnp.concatenate([lse_ref[0, j, r, :]] * (tk // LANE), axis=1)))
            n_full = (qi * tq) // tk

            def k_tile(kt, dqs, masked):
                kr = pl.ds(pl.multiple_of(kt * tk, tk), tk)
                first = masked and (qi * tq) % tk == 0
                out = []
                dvt = [None] * hps
                for j, (lanes, pair, q, qt, dob, dot_, dcol, lse_c) in enumerate(heads):
                    k = k_ref[0, kr, lanes]
                    s = _dot_nt(q, k)
                    if masked:
                        s = jnp.where(rows + qi * tq >= cols + kt * tk, s, NEG_INF)
                    p = jnp.exp2(s - lse_c)
                    dp = _dot_nt(dob, v_ref[0, kr, pair])
                    dsb = _mx(p * (dp - dcol))
                    if first:
                        dkt_ref[j, kt] = _dot(qt, dsb)
                    else:
                        dkt_ref[j, kt] += _dot(qt, dsb)
                    pv = _dot(dot_, _mx(p))
                    dvt[j // 2] = pv if dvt[j // 2] is None else dvt[j // 2] + pv
                    out.append(dqs[j] + _dot(dsb, k))
                for t in range(hps):
                    if first:
                        dvt_ref[t, kt] = dvt[t]
                    else:
                        dvt_ref[t, kt] += dvt[t]
                return tuple(out)

            dqs = (jnp.zeros((tq, LANE), F32),) * nh
            for kt in range(n_full):
                dqs = k_tile(kt, dqs, False)
            dqs = k_tile(n_full, dqs, True)
            for j in range(nh):
                dq_ref[0, r, heads[j][0]] = (MLA_SCALE * dqs[j]).astype(dq_ref.dtype)
            return 0

        for qi in range(nq):
            q_tile(qi, 0)

        def flush(kt, _):
            kr = pl.ds(pl.multiple_of(kt * tk, tk), tk)
            for j in range(nh):
                dk_ref[0, kr, LANE * j:LANE * (j + 1)] = ((1.0 / LOG2E) * dkt_ref[j, kt].T).astype(dk_ref.dtype)
            for t in range(hps):
                dv_ref[0, kr, LANE * t:LANE * (t + 1)] = dvt_ref[t, kt].T.astype(dv_ref.dtype)
            return 0

        lax.fori_loop(0, nk, flush, 0)

    grp = lambda w: pl.BlockSpec((1, S, w), lambda b, hp: (b, 0, hp))
    return pl.pallas_call(
        body, name="mla_bwd", grid=(B, MLA_HEADS // nh),
        out_shape=[jax.ShapeDtypeStruct((B, S, 1024), MLA_GRAD_DTYPE), jax.ShapeDtypeStruct((B, S, 1024), MLA_GRAD_DTYPE),
                   jax.ShapeDtypeStruct((B, S, 512), MLA_GRAD_DTYPE)],
        in_specs=[grp(nh * LANE), grp(nh * LANE), grp(hps * LANE), grp(hps * LANE), grp(hps * LANE),
                  pl.BlockSpec((1, nh, S, LANE), lambda b, hp: (b, hp, 0, 0))],
        out_specs=[grp(nh * LANE), grp(nh * LANE), grp(hps * LANE)],
        scratch_shapes=[pltpu.VMEM((nh, nk, LANE, tk), F32), pltpu.VMEM((hps, nk, LANE, tk), F32)],
        compiler_params=_cparams(("arbitrary", "arbitrary")),
    )(q3, k3, v3, o3, do3, lse)


def _swa_consts(sink_ref):
    W = WINDOW
    row = lax.broadcasted_iota(jnp.int32, (4 * W, LANE), 0)
    out = []
    for g in range(SWA_KV_HEADS):
        slope = jnp.zeros((4 * W, LANE), F32)
        sink = jnp.zeros((4 * W, LANE), F32)
        for p in range(4):
            h = p + 4 * g
            here = jnp.logical_and(row >= W * p, row < W * (p + 1))
            slope = jnp.where(here, float(LOG2E * 2.0 ** (-8.0 * (h + 1) / SWA_HEADS)), slope)
            sink = jnp.where(here, LOG2E * sink_ref[:, h:h + 1], sink)
        out.append((jnp.concatenate([slope, slope], axis=1), sink))
    return out


def _wide(col):
    return jnp.concatenate([col, col], axis=1)


def _swa_block(q_ref, k_ref, v_ref, pb_ref, pr_ref, n, i, ij):
    W = WINDOW
    kb = jnp.maximum(n - 1, 0)
    r = pl.ds(pl.multiple_of(i * W, W), W)
    kr = pl.ds(pl.multiple_of(kb * W, W), 2 * W)
    q4, k2, v2 = q_ref[0, r, :], k_ref[0, kr, :], v_ref[0, kr, :]
    pq = _wide(pb_ref[0, r, :])
    pk = jnp.concatenate([pr_ref[0, pl.ds(kb, 1), :], pr_ref[0, pl.ds(kb + 1, 1), :]], axis=1)
    rel = ij + (n - kb) * W
    dist = jnp.where(jnp.logical_and(rel >= 0, rel < W), pq - pk, float("inf"))
    return r, kb, kr, q4, k2, v2, jnp.concatenate([dist] * 4, axis=0)


def _swa_stack(x4, g, dtype):
    lane = _lane_iota((WINDOW, LANE))
    mine = (lane < 64) if g == 0 else (lane >= 64)
    return jnp.concatenate([jnp.where(mine, x4[:, LANE * p:LANE * (p + 1)], 0).astype(dtype) for p in range(4)], axis=0)


def _swa_unstack(ref, r, lo, hi, scale=None):
    W = WINDOW
    low = _lane_iota((W, LANE)) < 64
    for p in range(4):
        t = jnp.where(low, lo[W * p:W * (p + 1)], hi[W * p:W * (p + 1)])
        ref[0, r, LANE * p:LANE * (p + 1)] = t if scale is None else scale * t


def _swa_fwd(qs3, ks3, vs3, posb, posr, sinks):
    B, S, _ = qs3.shape
    W = WINDOW
    nb = S // W
    nh = SWA_SEQ_SPLIT
    nbh = nb // nh

    def body(q_ref, k_ref, v_ref, pb_ref, pr_ref, sink_ref, o_ref, lse_ref):
        ij = lax.broadcasted_iota(jnp.int32, (W, 2 * W), 0) - lax.broadcasted_iota(jnp.int32, (W, 2 * W), 1)
        consts = _swa_consts(sink_ref)
        n0 = pl.program_id(1) * nbh

        def blk(i, _):
            n = n0 + i
            r, _, _, q4, k2, v2, dist4 = _swa_block(q_ref, k_ref, v_ref, pb_ref, pr_ref, n, i, ij)
            o_g = []
            for g, (slope, sink) in enumerate(consts):
                s = _dot_nt(_swa_stack(q4, g, q4.dtype), k2) - slope * dist4
                m = jnp.maximum(jnp.max(s, axis=1, keepdims=True), sink)
                e = jnp.exp2(s - _wide(m))
                l = jnp.sum(e, axis=1, keepdims=True) + jnp.exp2(sink - m)
                o_g.append(_dot(_mx(e), v2) * (1.0 / l))
                lse_ref[0, i, g] = m + jnp.log2(l)
            _swa_unstack(o_ref, r, o_g[0], o_g[1])
            return 0

        for i in range(nbh):
            blk(i, 0)

    seq = lambda w: pl.BlockSpec((1, S, w), lambda b, h: (b, 0, 0))
    part = lambda w: pl.BlockSpec((1, S // nh, w), lambda b, h: (b, h, 0))
    lse_spec = pl.BlockSpec((1, nbh, 2, 4 * W, LANE), lambda b, h: (b, h, 0, 0, 0))
    return pl.pallas_call(
        body, name="swa_fwd", grid=(B, nh),
        out_shape=[jax.ShapeDtypeStruct((B, S, 512), F32), jax.ShapeDtypeStruct((B, nb, 2, 4 * W, LANE), F32)],
        in_specs=[part(512), seq(LANE), seq(LANE), part(LANE), pl.BlockSpec((1, nb, W), lambda b, h: (b, 0, 0)),
                  pl.BlockSpec((1, LANE), lambda b, h: (0, 0))],
        out_specs=[part(512), lse_spec],
        compiler_params=_cparams(("arbitrary", "arbitrary")),
    )(qs3, ks3, vs3, posb, posr, sinks)


def _swa_bwd(qs3, ks3, vs3, posb, posr, sinks, os3, do3, lse):
    B, S, _ = qs3.shape
    W = WINDOW
    nb = S // W
    nh = SWA_SEQ_SPLIT
    nbh = nb // nh
    assert nbh % 2 == 0 and nbh * nh * W == S

    def body(q_ref, k_ref, v_ref, pb_ref, pr_ref, sink_ref, o_ref, do_ref, lse_ref, dq_ref, dk_ref, dv_ref, dsink_ref,
             dkt_ref, dvt_ref):
        lane1 = _lane_iota((1, LANE))
        ij = lax.broadcasted_iota(jnp.int32, (W, 2 * W), 0) - lax.broadcasted_iota(jnp.int32, (W, 2 * W), 1)
        consts = _swa_consts(sink_ref)
        hh = pl.program_id(1)
        n0 = hh * nbh

        @pl.when(hh == 0)
        def _():
            dkt_ref[...] = jnp.zeros_like(dkt_ref)
            dvt_ref[...] = jnp.zeros_like(dvt_ref)

        @pl.when(jnp.logical_and(pl.program_id(0) == 0, hh == 0))
        def _():
            dsink_ref[...] = jnp.zeros_like(dsink_ref)

        def blk(i, dsink):
            n = n0 + i
            r, kb, _, q4, k2, v2, dist4 = _swa_block(q_ref, k_ref, v_ref, pb_ref, pr_ref, n, i, ij)
            o4, do4 = o_ref[0, r, :], do_ref[0, r, :]
            dq_g = []
            dkt = jnp.zeros((LANE, 2 * W), F32)
            dvt = jnp.zeros((LANE, 2 * W), F32)
            for g, (slope, sink) in enumerate(consts):
                q_st = _swa_stack(q4, g, F32)
                do_st = _swa_stack(do4, g, F32)
                dcol = jnp.sum(do_st * _swa_stack(o4, g, F32), axis=1, keepdims=True)
                qst, dob = _mx(q_st), _mx(do_st)
                lse_c = lse_ref[0, i, g]
                pr = jnp.exp2(_dot_nt(qst, k2) - slope * dist4 - _wide(lse_c))
                dsb = _mx(pr * (_dot_nt(dob, v2) - dcol))
                psd = jnp.exp2(sink - lse_c)[:, 0:1] * dcol
                for p in range(4):
                    dsink = dsink - jnp.where(lane1 == p + 4 * g,
                                              jnp.sum(psd[W * p:W * (p + 1)], axis=0, keepdims=True), 0.0)
                dq_g.append(_dot(dsb, k2))
                dkt = dkt + _dot(_mx(q_st.T), dsb)
                dvt = dvt + _dot(_mx(do_st.T), _mx(pr))
            _swa_unstack(dq_ref, r, dq_g[0], dq_g[1], SWA_SCALE)
            dkt_ref[kb] += dkt[:, 0:W]
            dkt_ref[kb + 1] += dkt[:, W:2 * W]
            dvt_ref[kb] += dvt[:, 0:W]
            dvt_ref[kb + 1] += dvt[:, W:2 * W]
            return dsink

        dsink = jnp.zeros((1, LANE), F32)
        for i in range(nbh):
            dsink = blk(i, dsink)
        dsink_ref[...] += dsink

        @pl.when(hh == nh - 1)
        def _():
            def flush(n, _):
                r = pl.ds(pl.multiple_of(n * W, W), W)
                dk_ref[0, r, :] = (1.0 / LOG2E) * dkt_ref[n].T
                dv_ref[0, r, :] = dvt_ref[n].T
                return 0

            lax.fori_loop(0, nb, flush, 0)

    seq = lambda w: pl.BlockSpec((1, S, w), lambda b, h: (b, 0, 0))
    part = lambda w: pl.BlockSpec((1, S // nh, w), lambda b, h: (b, h, 0))
    return pl.pallas_call(
        body, name="swa_bwd", grid=(B, nh),
        out_shape=[jax.ShapeDtypeStruct((B, S, 512), F32), jax.ShapeDtypeStruct((B, S, LANE), F32),
                   jax.ShapeDtypeStruct((B, S, LANE), F32), jax.ShapeDtypeStruct((1, LANE), F32)],
        in_specs=[part(512), seq(LANE), seq(LANE), part(LANE), pl.BlockSpec((1, nb, W), lambda b, h: (b, 0, 0)),
                  pl.BlockSpec((1, LANE), lambda b, h: (0, 0)), part(512),
                  pl.BlockSpec((1, S // nh, 512), lambda b, h: (b, h, 1)),
                  pl.BlockSpec((1, nbh, 2, 4 * W, LANE), lambda b, h: (b, h, 0, 0, 0))],
        out_specs=[part(512), seq(LANE), seq(LANE), pl.BlockSpec((1, LANE), lambda b, h: (0, 0))],
        scratch_shapes=[pltpu.VMEM((nb, LANE, W), F32), pltpu.VMEM((nb, LANE, W), F32)],
        compiler_params=_cparams(("arbitrary", "arbitrary")),
    )(qs3, ks3, vs3, posb, posr, sinks, os3, do3, lse)


def _post(om, osw, g, w_out, x2, gate, fg, tgt, S, ts):
    T, D = x2.shape
    nsb = S // ts

    def body(om_ref, os_ref, g_ref, w_ref, x_ref, gate_ref, fg_ref, t_ref,
             dx2_ref, do_ref, dg_ref, loss_ref, dfg_ref, dgate_ref, dw_ref, dwb_ref):
        i = pl.program_id(0)
        gv = g_ref[...]
        sg = 1.0 / (1.0 + jnp.exp(-gv))
        silu = gv * sg
        o = jnp.concatenate([om_ref[...], os_ref[...]], axis=1)
        ab = _mx(o * silu)
        y = _dot(ab, w_ref[...])
        gate = gate_ref[0]
        xo = x_ref[...] + gate * y
        r2 = lax.rsqrt(jnp.mean(xo * xo, axis=-1, keepdims=True) + EPS)
        xh = xo * r2
        fg = fg_ref[...]
        diff = xh * fg - t_ref[...]
        sq = jnp.sum(diff * diff, axis=0, keepdims=True)
        part = sq[:, 0:LANE]
        for t in range(1, D // LANE):
            part = part + sq[:, LANE * t:LANE * (t + 1)]
        dout = diff * (1.0 / D)
        dxh = dout * fg
        dx2 = r2 * (dxh - xh * jnp.mean(dxh * xh, axis=-1, keepdims=True))
        dx2_ref[...] = dx2
        dyb = _mx(dx2 * gate)
        da = _dot_nt(dyb, w_ref[...])
        do_ref[...] = da * silu
        dg_ref[...] = _mx(da * o * (sg * (1.0 + gv * (1.0 - sg))))

        @pl.when(i == 0)
        def _():
            loss_ref[...] = jnp.zeros_like(loss_ref)
            dfg_ref[...] = jnp.zeros_like(dfg_ref)
            dw_ref[...] = jnp.zeros_like(dw_ref)

        @pl.when(i % nsb == 0)
        def _():
            dgate_ref[...] = jnp.zeros_like(dgate_ref)

        loss_ref[...] += (0.5 / D) * part
        dfg_ref[...] += jnp.sum(dout * xh, axis=0, keepdims=True)
        dgate_ref[0] += jnp.sum(dx2 * y, axis=0, keepdims=True)
        dw_ref[...] += _dot_tn(ab, dyb)

        @pl.when(i == T // ts - 1)
        def _():
            chunks = [dw_ref[512 + 64 * PAIR_INV[h]:512 + 64 * (PAIR_INV[h] + 1), :] for h in range(SWA_HEADS)]
            for h in range(SWA_HEADS):
                dw_ref[512 + 64 * h:512 + 64 * (h + 1), :] = chunks[h]
            dwb_ref[...] = dw_ref[...].astype(jnp.bfloat16)

    row = lambda w: pl.BlockSpec((ts, w), lambda i: (i, 0))
    full = lambda a: pl.BlockSpec(a.shape, lambda i: (0,) * a.ndim)
    per_b = pl.BlockSpec((1, 1, D), lambda i: (i // nsb, 0, 0))
    return pl.pallas_call(
        body, name="post", grid=(T // ts,),
        out_shape=[jax.ShapeDtypeStruct((T, D), F32), jax.ShapeDtypeStruct((T, 1024), F32),
                   jax.ShapeDtypeStruct((T, 1024), _MXU_DTYPE), jax.ShapeDtypeStruct((1, LANE), F32),
                   jax.ShapeDtypeStruct((1, D), F32), jax.ShapeDtypeStruct(gate.shape, F32),
                   jax.ShapeDtypeStruct(w_out.shape, F32), jax.ShapeDtypeStruct(w_out.shape, jnp.bfloat16)],
        in_specs=[row(512), row(512), row(1024), full(w_out), row(D), per_b, full(fg), row(D)],
        out_specs=[row(D), row(1024), row(1024),
                   pl.BlockSpec((1, LANE), lambda i: (0, 0)), pl.BlockSpec((1, D), lambda i: (0, 0)), per_b,
                   full(w_out), full(w_out)],
        compiler_params=_cparams(("arbitrary",)),
    )(om, osw, g, w_out, x2, gate, fg, tgt)


def _pre_bwd(dq, dk, dv, dqs, dks, dvs, dg, zq, zkv, ql, kvl, x2, dx2, scale, ng, w_in, gq, gkv, w_uq, w_uk, w_uv,
             rope, S, ts):
    T, D = x2.shape
    nsb = S // ts
    WQ = MLA_HEADS * LANE

    def body(dq_ref, dk_ref, dv_ref, dqs_ref, dks_ref, dvs_ref, dg_ref, zq_ref, zkv_ref, ql_ref, kvl_ref, x_ref, dx2_ref,
             sc_ref,
             ng_ref, w_ref, gq_ref, gkv_ref, wuq_ref, wuk_ref, wuv_ref, cd_ref, sd_ref,
             gx_ref, dz_ref, dgq_ref, dgkv_ref, dng_ref, dsh_ref, dsc_ref, uq_ref, uqb_ref, ukv_ref, ukvb_ref,
             dwuq_ref, dwuk_ref, dwuv_ref):
        i = pl.program_id(0)

        @pl.when(i == 0)
        def _():
            dgq_ref[...] = jnp.zeros_like(dgq_ref)
            dgkv_ref[...] = jnp.zeros_like(dgkv_ref)
            dng_ref[...] = jnp.zeros_like(dng_ref)
            dwuq_ref[...] = jnp.zeros_like(dwuq_ref)
            dwuk_ref[...] = jnp.zeros_like(dwuk_ref)
            dwuv_ref[...] = jnp.zeros_like(dwuv_ref)

        @pl.when(i % nsb == 0)
        def _():
            dsh_ref[...] = jnp.zeros_like(dsh_ref)
            dsc_ref[...] = jnp.zeros_like(dsc_ref)

        def norm_bwd(z, dy, gain):
            r = lax.rsqrt(jnp.mean(z * z, axis=-1, keepdims=True) + EPS)
            zh = z * r
            dzh = dy * gain
            return r * (dzh - zh * jnp.mean(dzh * zh, axis=-1, keepdims=True)), jnp.sum(dy * zh, axis=0, keepdims=True)

        tables = _rope_split(cd_ref[...], sd_ref[...])
        ng = ng_ref[...]
        sc1 = 1.0 + sc_ref[0]

        def rows_chain(rs):
            cq, ck, sa, sb = (t[rs] for t in tables)
            dqr = dq_ref[rs, :].astype(F32)
            dqb = _mx(dqr * _tile_heads(cq) + pltpu.roll(dqr * _tile_heads(sa), 16, 1)
                      + pltpu.roll(dqr * _tile_heads(sb), WQ - 16, 1))
            p_uq = _dot_tn(ql_ref[rs, :], dqb)
            dzq, dgq = norm_bwd(zq_ref[rs, :], _dot_nt(dqb, wuq_ref[...]), gq_ref[...])
            dkr = dk_ref[rs, :].astype(F32)
            dkb = _mx(dkr)
            p_uk = _dot_tn(kvl_ref[rs, :], dkb)
            dvb = _mx(dv_ref[rs, :])
            p_uv = _dot_tn(kvl_ref[rs, :], dvb)
            dzkv, dgkv = norm_bwd(zkv_ref[rs, :], _dot_nt(dkb, wuk_ref[...]) + _dot_nt(dvb, wuv_ref[...]), gkv_ref[...])
            dkpe = dkr[:, 0:LANE]
            for h in range(1, MLA_HEADS):
                dkpe = dkpe + dkr[:, LANE * h:LANE * (h + 1)]
            dkro = dkpe * ck + pltpu.roll(dkpe * sa, 16, 1) + pltpu.roll(dkpe * sb, LANE - 16, 1)
            dz_ref[rs, 0:384] = _mx(dzq)
            dz_ref[rs, 384:640] = _mx(dzkv)
            dz_ref[rs, 640:768] = _mx(dkro)
            dz_ref[rs, 768:1280] = dg_ref[rs, 0:512]
            dz_ref[rs, 1280:1792] = _mx(dqs_ref[rs, :])
            dz_ref[rs, 1792:1920] = _mx(dks_ref[rs, :])
            dz_ref[rs, 1920:2048] = _mx(dvs_ref[rs, :])
            dz_ref[rs, 2048:2560] = dg_ref[rs, 512:1024]
            dh = _dot(dz_ref[rs, :], w_ref[...])
            x = x_ref[rs, :]
            r1 = lax.rsqrt(jnp.mean(x * x, axis=-1, keepdims=True) + EPS)
            xn = x * r1
            dxn = dh * ng * sc1
            gx_ref[rs, :] = dx2_ref[rs, :] + r1 * (dxn - xn * jnp.mean(dxn * xn, axis=-1, keepdims=True))
            return (p_uq, p_uk, p_uv, dgq, dgkv, jnp.sum(dh, axis=0, keepdims=True),
                    jnp.sum(dh * (xn * ng), axis=0, keepdims=True), jnp.sum(dh * xn * sc1, axis=0, keepdims=True))

        hr = ts // 2
        parts = [rows_chain(slice(hr * t, hr * (t + 1))) for t in range(2)]
        p_uq, p_uk, p_uv, dgq, dgkv, dsh, dsc, dng = (a + b for a, b in zip(*parts))
        dwuq_ref[...] += p_uq
        dwuk_ref[...] += p_uk
        dwuv_ref[...] += p_uv
        dgq_ref[...] += dgq
        dgkv_ref[...] += dgkv
        dsh_ref[0] += dsh
        dsc_ref[0] += dsc
        dng_ref[...] += dng

        @pl.when(i == T // ts - 1)
        def _():
            for h in range(MLA_HEADS):
                uq = dwuq_ref[:, LANE * h:LANE * h + MLA_QK]
                ukv = jnp.concatenate([dwuk_ref[:, LANE * h:LANE * h + MLA_NOPE], dwuv_ref[:, 64 * h:64 * (h + 1)]], axis=1)
                uq_ref[h] = uq
                uqb_ref[h] = uq.astype(jnp.bfloat16)
                ukv_ref[h] = ukv
                ukvb_ref[h] = ukv.astype(jnp.bfloat16)

    row = lambda w: pl.BlockSpec((ts, w), lambda i: (i, 0))
    full = lambda a: pl.BlockSpec(a.shape, lambda i: (0,) * a.ndim, pipeline_mode=pl.Buffered(1))
    slot = lambda r, c, dt: (jax.ShapeDtypeStruct((MLA_HEADS, r, c), dt), pl.BlockSpec((MLA_HEADS, r, c), lambda i: (0, 0, 0)))
    slots = [slot(Q_LORA, MLA_QK, F32), slot(Q_LORA, MLA_QK, jnp.bfloat16), slot(KV_LORA, LANE, F32), slot(KV_LORA, LANE, jnp.bfloat16)]
    per_b = pl.BlockSpec((1, 1, D), lambda i: (i // nsb, 0, 0))
    dense = pl.BlockSpec((ts // 8, LANE), lambda i: (i, 0))
    vec = lambda w: pl.BlockSpec((1, w), lambda i: (0, 0))
    return pl.pallas_call(
        body, name="pre_bwd", grid=(T // ts,),
        out_shape=[jax.ShapeDtypeStruct((T, D), F32), jax.ShapeDtypeStruct((T, D_IN_PAD), _MXU_DTYPE), jax.ShapeDtypeStruct((1, 384), F32),
                   jax.ShapeDtypeStruct((1, 256), F32), jax.ShapeDtypeStruct((1, D), F32),
                   jax.ShapeDtypeStruct(scale.shape, F32), jax.ShapeDtypeStruct(scale.shape, F32)] + [t[0] for t in slots],
        in_specs=[row(WQ), row(WQ), row(512), row(512), row(LANE), row(LANE), row(1024), row(384), row(256), row(384),
                  row(256), row(D), row(D), per_b, full(ng), full(w_in), full(gq), full(gkv), full(w_uq), full(w_uk), full(w_uv),
                  dense, dense],
        out_specs=[row(D), row(D_IN_PAD), vec(384), vec(256), vec(D), per_b, per_b] + [t[1] for t in slots],
        scratch_shapes=[pltpu.VMEM(w_uq.shape, F32), pltpu.VMEM(w_uk.shape, F32), pltpu.VMEM(w_uv.shape, F32)],
        compiler_params=_cparams(("arbitrary",)),
    )(dq, dk, dv, dqs, dks, dvs, dg, zq, zkv, ql, kvl, x2, dx2, scale, ng, w_in, gq, gkv, w_uq, w_uk, w_uv, *rope)


def _w_in_row_runs():
    runs = [(0, 0, 640), (640, 704, 32), (672, 768, 512)]
    runs += [(1184 + 64 * h, 1280 + 64 * PAIR_INV[h], 64) for h in range(8)]
    runs += [(1696, 1792, 256)]
    runs += [(1952 + 64 * h, 2048 + 64 * PAIR_INV[h], 64) for h in range(8)]
    return runs


W_IN_SHARD = D_IN // N_DEV
W_IN_SLOT = 320


def _dw_in_t(dz, hb, tn, tk):
    T, M = dz.shape
    N = hb.shape[1]
    nk = T // tk

    def body(a_ref, b_ref, o_ref, ob_ref, acc_ref):
        k = pl.program_id(1)

        @pl.when(k == 0)
        def _():
            acc_ref[...] = jnp.zeros_like(acc_ref)

        acc_ref[...] += _dot_tn(a_ref[...], b_ref[...])

        @pl.when(k == nk - 1)
        def _():
            for d in range(N_DEV):
                lo, hi = W_IN_SHARD * d, W_IN_SHARD * (d + 1)
                for nat, pad, size in _w_in_row_runs():
                    a, b = max(nat, lo), min(nat + size, hi)
                    if a < b:
                        piece = acc_ref[pad + a - nat:pad + b - nat, :]
                        o_ref[d, a - lo:b - lo, :] = piece
                        ob_ref[d, a - lo:b - lo, :] = piece.astype(ob_ref.dtype)
                o_ref[d, W_IN_SHARD:W_IN_SLOT, :] = jnp.zeros((W_IN_SLOT - W_IN_SHARD, tn), F32)
                ob_ref[d, W_IN_SHARD:W_IN_SLOT, :] = jnp.zeros((W_IN_SLOT - W_IN_SHARD, tn), ob_ref.dtype)

    slots = pl.BlockSpec((N_DEV, W_IN_SLOT, tn), lambda j, k: (0, 0, j))
    return pl.pallas_call(
        body, name="dw_in", grid=(N // tn, nk),
        out_shape=[jax.ShapeDtypeStruct((N_DEV, W_IN_SLOT, N), F32), jax.ShapeDtypeStruct((N_DEV, W_IN_SLOT, N), jnp.bfloat16)],
        in_specs=[pl.BlockSpec((tk, M), lambda j, k: (k, 0)), pl.BlockSpec((tk, tn), lambda j, k: (k, j))],
        out_specs=[slots, slots],
        scratch_shapes=[pltpu.VMEM((M, tn), F32)],
        compiler_params=_cparams(("arbitrary", "arbitrary")),
    )(dz, hb)


SHARDED = (3, 6, 7, 9)
PART_SLICES = {10: (128, 1152), 2: (1152, 2176), 4: (2176, 2560), 5: (2560, 2816), 8: (2816, 2824)}


def _finalize_adamw(parts_all, dmod_all, dmod_cols, c_all, ws, sharded_grads, ms, vs):
    n = len(ws)
    local = [t for t in range(n) if t not in SHARDED[1:]]

    def body(*refs):
        p_ref, dm_ref, dmc_ref, c_ref = refs[:4]
        w_refs, gs_refs, m_refs, v_refs = refs[4:4 + n], refs[4 + n:8 + n], refs[8 + n:8 + 2 * n], refs[8 + 2 * n:8 + 3 * n]
        outs = refs[8 + 3 * n:]
        loss_ref, gl_refs = outs[0], outs[1:1 + len(local)]
        d_refs, nm_refs, nv_refs = (outs[1 + len(local) + t * n:1 + len(local) + (t + 1) * n] for t in range(3))
        ps_ref = outs[-1]
        rd = lambda r: r[:, 0, :] if len(r.shape) == 3 else r[...]

        def wr(r, val):
            if len(r.shape) == 3:
                r[:, 0, :] = val
            else:
                r[...] = val

        acc = p_ref[0]
        for j in range(1, N_DEV):
            acc = acc + p_ref[j]
        ps_ref[...] = acc
        loss_ref[...] = jnp.sum(acc[:, 0:LANE], axis=1, keepdims=True)
        db = dm_ref[0:1, :]
        for j in range(1, dm_ref.shape[0]):
            db = db + dm_ref[j:j + 1, :]
        grads = {0: _dot_tn(_mx(_silu(c_ref[...])), _mx(dmc_ref[...])), 1: db}
        for t, (lo, hi) in PART_SLICES.items():
            grads[t] = ps_ref[:, lo:hi]
        for i, t in enumerate(SHARDED):
            grads[t] = gs_refs[i][...]
        grads[SHARDED[0]] = gs_refs[0][0:W_IN_SHARD, :]
        for i, t in enumerate(local):
            wr(gl_refs[i], grads[t])
        for t in range(n):
            gv = grads[t]
            nm = ADAM_B1 * rd(m_refs[t]) + (1.0 - ADAM_B1) * gv
            nv = ADAM_B2 * rd(v_refs[t]) + (1.0 - ADAM_B2) * (gv * gv)
            m_hat = nm / (1.0 - ADAM_B1 ** ADAM_STEP)
            v_hat = nv / (1.0 - ADAM_B2 ** ADAM_STEP)
            wr(d_refs[t], -ADAM_LR * (m_hat / (jnp.sqrt(v_hat) + ADAM_EPS) + ADAM_WD * rd(w_refs[t])))
            wr(nm_refs[t], nm)
            wr(nv_refs[t], nv)

    like = lambda arrs: [jax.ShapeDtypeStruct(a.shape, F32) for a in arrs]
    out = pl.pallas_call(
        body, name="finalize_adamw",
        out_shape=[jax.ShapeDtypeStruct((1, 1), F32)] + like([ws[t] for t in local]) + like(ws) * 3,
        in_specs=[_vmem()] * (8 + 3 * n), out_specs=[_vmem()] * (1 + len(local) + 3 * n),
        scratch_shapes=[pltpu.VMEM(parts_all.shape[1:], F32)],
        compiler_params=_cparams(),
    )(parts_all, dmod_all, dmod_cols, c_all, *ws, *sharded_grads, *ms, *vs)
    k = 1 + len(local)
    return out[0], dict(zip(local, out[1:k])), out[k:k + n], out[k + n:k + 2 * n], out[k + 2 * n:]


def _pair_perm(a, axis, order):
    a = jnp.moveaxis(a, axis, -1)
    lead = a.shape[:-1]
    a = a.reshape(lead + (8, 64))[..., list(order), :].reshape(lead + (512,))
    return jnp.moveaxis(a, -1, axis)


def _rope_table(positions):
    T = positions.size
    inv = ROPE_THETA ** (-jnp.arange(0, MLA_ROPE, 2, dtype=F32) / MLA_ROPE)
    pos = jnp.repeat(positions.reshape(T // 8, 8)[:, ::-1].astype(F32), MLA_ROPE // 2, axis=1)
    ang = pos * jnp.tile(inv, 8)[None, :]
    return jnp.cos(ang), jnp.sin(ang)


def _local_step(x, mod, positions, ng, w_in_t, gq, gkv, w_uq, w_ukv, sinks, w_out, fg, tgt,
                ts=512, fq=512, fk=512, bq=512, bk=512):
    B, S, D = x.shape
    T = B * S
    x2 = x.reshape(T, D)
    shift, scale, gate = (mod[:, None, k * D:(k + 1) * D] for k in range(3))
    w_out_p = jnp.concatenate([w_out[:512], _pair_perm(w_out[512:], 0, PAIR_ORDER)], axis=0)
    rope = _rope_table(positions)
    posf = positions.astype(F32)
    posb = jnp.broadcast_to(posf[:, :, None], (B, S, LANE))
    posr = posf.reshape(B, S // WINDOW, WINDOW)
    sinks_l = jnp.pad(sinks.reshape(1, SWA_HEADS), ((0, 0), (0, LANE - SWA_HEADS)))

    (hb, zq, zkv, ql, kvl, q, k, v, qs, ks, vs, g, w_in_p, w_uq_p, w_uk_p, w_uv) = _pre_fwd(
        x2, shift, scale, ng, w_in_t, gq, gkv, w_uq, w_ukv, rope, S, ts)
    r3 = lambda a: a.reshape(B, S, a.shape[-1])
    om, lse_m = _mla_fwd(r3(q), r3(k), r3(v), fq, fk)
    osw, lse_s = _swa_fwd(r3(qs), r3(ks), r3(vs), posb, posr, sinks_l)
    dx2, do, dg, loss_v, dfg, dgate, *dw_out = _post(
        om.reshape(T, 512), osw.reshape(T, 512), g, w_out_p, x2, gate, fg.reshape(1, D), tgt.reshape(T, D), S, ts)
    do3 = r3(do)
    dq, dk, dv = _mla_bwd(r3(q), r3(k), r3(v), om, do3, lse_m, bq, bk)
    dqs, dks, dvs, dsink = _swa_bwd(r3(qs), r3(ks), r3(vs), posb, posr, sinks_l, osw, do3, lse_s)
    f2 = lambda a: a.reshape(T, a.shape[-1])
    gx, dz, dgq, dgkv, dng, dsh, dsc, *dw_u = _pre_bwd(
        f2(dq), f2(dk), f2(dv), f2(dqs), f2(dks), f2(dvs), dg, zq, zkv, ql, kvl, x2, dx2, scale, ng, w_in_p, gq, gkv,
        w_uq_p, w_uk_p, w_uv, rope, S, ts)
    tk = min(T, 1024)
    dw_in_t = _dw_in_t(dz, hb, 512, tk)
    parts = jnp.concatenate([loss_v, dfg, dng, dgq, dgkv, dsink], axis=1)
    dmod = jnp.concatenate([dsh, dsc, dgate], axis=2).reshape(B, 3 * D)
    return gx.reshape(B, S, D), dw_in_t, tuple(dw_u[0:2]), tuple(dw_u[2:4]), tuple(dw_out), parts, dmod


def kernel(x, c, positions, w_ada, b_ada, norm_gain, w_in, q_norm_gain, kv_norm_gain, w_uq, w_ukv, swa_sinks, w_out, final_gain, loss_target, m_w_ada, m_b_ada, m_norm_gain, m_w_in, m_q_norm_gain, m_kv_norm_gain, m_w_uq, m_w_ukv, m_swa_sinks, m_w_out, m_final_gain, v_w_ada, v_b_ada, v_norm_gain, v_w_in, v_q_norm_gain, v_kv_norm_gain, v_w_uq, v_w_ukv, v_swa_sinks, v_w_out, v_final_gain):
    B, S, D = x.shape
    me = 4 * lax.axis_index("x") + 2 * lax.axis_index("y") + lax.axis_index("c")
    bf = _MXU_DTYPE

    ncol = w_ada.shape[2]
    b_cols = lax.dynamic_slice_in_dim(b_ada, me * ncol, ncol, axis=1)
    c_all, win_g, wuq_g, wukv_g, wout_g, mod_g = _all_gather(
        [c, (jnp.transpose(w_in, (2, 0, 1)), bf), (w_uq[0], bf), (w_ukv[0], bf), (w_out[0], bf)], "ag_weights",
        fused=((w_ada[0], b_cols), _ada_cols, jax.ShapeDtypeStruct((N_DEV, B, ncol), F32)))
    c_all = c_all.reshape(N_DEV * B, D)
    w_out_f = wout_g.reshape(D, D)

    mod = lax.dynamic_index_in_dim(mod_g, me, axis=1, keepdims=False)
    mod = jnp.transpose(mod, (1, 0, 2)).reshape(B, 3 * D)

    gx, dw_in_t, dw_uq, dw_ukv, dw_out, parts, dmod = _local_step(
        x, mod, positions, norm_gain, win_g, q_norm_gain, kv_norm_gain, wuq_g, wukv_g, swa_sinks,
        w_out_f, final_gain, loss_target)

    slots = lambda pair: tuple(a.reshape((4, 2, -1) + a.shape[-1:]) for a in pair)
    g_w_in_t, g_w_uq, g_w_ukv, g_w_out, parts_g, dmod_g = _reduce_scatter(
        [slots(dw_in_t), slots(dw_uq), slots(dw_ukv), slots(dw_out)], "rs_grads",
        gather=(parts, dmod))

    dmod_all = dmod_g.reshape(N_DEV * B, 3 * D)
    dmod_cols = lax.dynamic_slice_in_dim(dmod_all, me * ncol, ncol, axis=1)
    ws = [w_ada, b_ada, norm_gain, w_in, q_norm_gain, kv_norm_gain, w_uq, w_ukv, swa_sinks, w_out, final_gain]
    ms = [m_w_ada, m_b_ada, m_norm_gain, m_w_in, m_q_norm_gain, m_kv_norm_gain, m_w_uq, m_w_ukv, m_swa_sinks, m_w_out,
          m_final_gain]
    vs = [v_w_ada, v_b_ada, v_norm_gain, v_w_in, v_q_norm_gain, v_kv_norm_gain, v_w_uq, v_w_ukv, v_swa_sinks, v_w_out,
          v_final_gain]
    two_d = lambda a: a.reshape((1, a.shape[0]) if a.ndim == 1 else a.shape[-2:])
    flat = lambda arrs: [jnp.transpose(a, (2, 0, 1)) if t == SHARDED[0] else two_d(a) for t, a in enumerate(arrs)]
    sharded = {3: g_w_in_t, 6: g_w_uq, 7: g_w_ukv, 9: g_w_out}
    loss, local_grads, deltas, new_ms, new_vs = _finalize_adamw(
        parts_g, dmod_all, dmod_cols, c_all, flat(ws), [sharded[t] for t in SHARDED], flat(ms), flat(vs))
    grads = [local_grads[t] if t in local_grads else sharded[t] for t in range(len(ws))]
    shaped = lambda arrs: [jnp.transpose(a, (1, 2, 0)) if t == SHARDED[0] else a.reshape(ws[t].shape) for t, a in enumerate(arrs)]
    return (loss.reshape(()), gx, *shaped(grads), *shaped(deltas), *shaped(new_ms), *shaped(new_vs))
```

```python
import functools

import jax
import jax.numpy as jnp
from jax import lax
from jax.experimental import pallas as pl
from jax.experimental.pallas import tpu as pltpu

F32 = jnp.float32
_MXU_DTYPE = jnp.bfloat16
MLA_GRAD_DTYPE = jnp.bfloat16

N_DEV = 8
MLA_HEADS = 8
MLA_NOPE = 64
MLA_ROPE = 32
MLA_QK = MLA_NOPE + MLA_ROPE
Q_LORA = 384
KV_LORA = 256
SWA_HEADS = 8
SWA_KV_HEADS = 2
SWA_HEAD_DIM = 64
WINDOW = 128
ROPE_THETA = 10000.0
EPS = 1e-6
MLA_SCALE = float((MLA_NOPE + MLA_ROPE) ** -0.5)
SWA_SCALE = float(SWA_HEAD_DIM ** -0.5)
LOG2E = 1.4426950408889634
MLA_QSCALE = MLA_SCALE * LOG2E
D_IN = 2464
D_IN_PAD = 2560
PAIR_ORDER = (0, 4, 1, 5, 2, 6, 3, 7)
PAIR_INV = (0, 2, 4, 6, 1, 3, 5, 7)

ADAM_LR = 0.001
ADAM_B1 = 0.9
ADAM_B2 = 0.999
ADAM_EPS = 1e-08
ADAM_WD = 0.01
ADAM_STEP = 10

LANE = 128
VMEM_LIMIT = 56 * 1024 * 1024

MESH = pl.DeviceIdType.MESH
NEG_INF = float("-inf")
SWA_SEQ_SPLIT = 2


def _mx(a):
    return a.astype(_MXU_DTYPE)


def _dot(a, b):
    return jnp.dot(a, b, preferred_element_type=F32)


def _dot_nt(a, b):
    return lax.dot_general(a, b, (((1,), (1,)), ((), ())), preferred_element_type=F32)


def _dot_tn(a, b):
    return lax.dot_general(a, b, (((0,), (0,)), ((), ())), preferred_element_type=F32)


def _cparams(sem=None):
    return pltpu.CompilerParams(dimension_semantics=sem, vmem_limit_bytes=VMEM_LIMIT)


def _vmem():
    return pl.BlockSpec(memory_space=pltpu.VMEM)


def _lane_iota(shape):
    return lax.broadcasted_iota(jnp.int32, shape, len(shape) - 1)


def _gather_program(srcs, outs, send_sems, recv_sems, local_sems):
    x, y, c = lax.axis_index("x"), lax.axis_index("y"), lax.axis_index("c")
    me, sibling = (x, y, c), (x, y, 1 - c)
    chips = [(1 - x, y), (x, 1 - y), (1 - x, 1 - y)]

    def slot(a, dev):
        return outs[a].at[4 * dev[0] + 2 * dev[1] + dev[2]]

    def copy(a, k, block, to, src=None):
        return pltpu.make_async_remote_copy(
            src_ref=slot(a, block) if src is None else src, dst_ref=slot(a, block),
            send_sem=send_sems.at[7 * a + k], recv_sem=recv_sems.at[7 * a + k],
            device_id=to, device_id_type=MESH)

    def local(a):
        return pltpu.make_async_copy(srcs[a], slot(a, me), local_sems.at[a])

    def start(a):
        local(a).start()
        cps = [copy(a, 0, me, sibling, src=srcs[a])]
        cps += [copy(a, 1 + j, me, (*chip, c), src=srcs[a]) for j, chip in enumerate(chips)]
        for cp in cps:
            cp.start()
        return cps

    def finish(group):
        cps = []
        for j, chip in enumerate(chips):
            for a in group:
                copy(a, 1 + j, (*chip, c), me).wait_recv()
                cp = copy(a, 4 + j, (*chip, c), sibling)
                cp.start()
                cps.append(cp)
        for a in group:
            copy(a, 0, sibling, me).wait_recv()
            for j, chip in enumerate(chips):
                copy(a, 4 + j, (*chip, 1 - c), me).wait_recv()
            local(a).wait()
        return cps

    return start, finish


def _gather_sems(m):
    return [pltpu.SemaphoreType.DMA((7 * m,)), pltpu.SemaphoreType.DMA((7 * m,)), pltpu.SemaphoreType.DMA((m,))]


def _all_gather(arrs, name, fused=None):
    n = len(arrs)
    cast = {a: e[1] for a, e in enumerate(arrs) if isinstance(e, tuple)}
    flip = {a for a, e in enumerate(arrs) if isinstance(e, tuple) and len(e) == 3}
    arrs = [e[0] if isinstance(e, tuple) else e for e in arrs]
    shapes = [(a.shape[-1], a.shape[0]) if i in flip else (a.shape[0], a.shape[-1]) for i, a in enumerate(arrs)]
    extra, fn, piece = fused if fused else ((), None, None)
    ne, m = len(extra), n + (1 if fused else 0)

    def body(*refs):
        ins, ex, outs = refs[:n], refs[n:n + ne], refs[n + ne:n + ne + m]
        rest = refs[n + ne + m:]
        staged = dict(zip(cast, rest))
        srcs = [staged.get(a, ins[a]) for a in range(n)] + ([rest[len(cast)]] if fused else [])
        start, finish = _gather_program(srcs, outs, *rest[-3:])
        pending = []
        for a in range(n):
            if a in staged:
                val = ins[a][:, 0, :] if len(ins[a].shape) == 3 else ins[a][...]
                staged[a][...] = (val.T if a in flip else val).astype(cast[a])
            pending += start(a)
        if fused:
            pending += finish([0])
            fn(srcs[n], outs[0], *ex)
            pending += start(n)
            pending += finish([n] + list(range(1, n)))
        else:
            pending += finish(list(range(n)))
        for cp in pending:
            cp.wait_send()

    out_shape = [jax.ShapeDtypeStruct((N_DEV,) + shapes[a], cast.get(a, arrs[a].dtype)) for a in range(n)]
    scratch = [pltpu.VMEM(shapes[a], cast[a]) for a in cast]
    if fused:
        out_shape.append(jax.ShapeDtypeStruct((N_DEV,) + piece.shape, piece.dtype))
        scratch.append(pltpu.VMEM(piece.shape, piece.dtype))
    return pl.pallas_call(
        body, name=name, out_shape=out_shape,
        in_specs=[_vmem()] * (n + ne), out_specs=[_vmem()] * m,
        scratch_shapes=scratch + _gather_sems(m),
        compiler_params=pltpu.CompilerParams(vmem_limit_bytes=VMEM_LIMIT),
    )(*arrs, *extra)


def _reduce_scatter(arrs, name, gather=()):
    n, g = len(arrs), len(gather)
    halves = [a[1] if isinstance(a, tuple) else a.astype(jnp.bfloat16) for a in arrs]
    arrs = [a[0] if isinstance(a, tuple) else a for a in arrs]

    def body(*refs):
        xs, xbs, gins = refs[:n], refs[n:2 * n], refs[2 * n:2 * n + g]
        outs, gouts = refs[2 * n + g:3 * n + g], refs[3 * n + g:3 * n + 2 * g]
        rest = refs[3 * n + 2 * g:]
        parts, recv_a, send_b, recv_b = (rest[t * n:(t + 1) * n] for t in range(4))
        send_sems, recv_sems, local_sems = rest[4 * n:4 * n + 3]
        x, y, c = lax.axis_index("x"), lax.axis_index("y"), lax.axis_index("c")
        myq = 2 * x + y
        gather_start, gather_finish = _gather_program(gins, gouts, *rest[4 * n + 3:])

        def chip(k):
            return (1 - x if k & 2 else x, 1 - y if k & 1 else y)

        def from_sibling(a):
            return pltpu.make_async_remote_copy(
                src_ref=xbs[a].at[:, 1 - c], dst_ref=recv_a[a], send_sem=send_sems.at[4 * a], recv_sem=recv_sems.at[4 * a],
                device_id=(x, y, 1 - c), device_id_type=MESH)

        def to_owner(a, k):
            qx, qy = chip(k)
            return pltpu.make_async_remote_copy(
                src_ref=send_b[a].at[2 * qx + qy], dst_ref=recv_b[a].at[myq],
                send_sem=send_sems.at[4 * a + k], recv_sem=recv_sems.at[4 * a + k],
                device_id=(qx, qy, c), device_id_type=MESH)

        mine = [pltpu.make_async_copy(xs[a].at[:, c], parts[a], local_sems.at[a]) for a in range(n)]
        first = [from_sibling(a) for a in range(n)]
        for cp in mine + first:
            cp.start()
        second = []
        for b in range(g):
            second += gather_start(b)
        for a in range(n):
            mine[a].wait()
            first[a].wait_recv()
            parts[a][...] = parts[a][...] + recv_a[a][...].astype(F32)
            send_b[a][...] = parts[a][...].astype(jnp.bfloat16)
            for k in range(1, 4):
                cp = to_owner(a, k)
                cp.start()
                second.append(cp)
        second += gather_finish(list(range(g)))
        for a in range(n):
            acc = parts[a][myq]
            for k in range(1, 4):
                to_owner(a, k).wait_recv()
                qx, qy = chip(k)
                acc = acc + recv_b[a][2 * qx + qy].astype(F32)
            outs[a][...] = acc
        for cp in first + second:
            cp.wait_send()

    quarter = lambda a, dt: pltpu.VMEM((4,) + a.shape[2:], dt)
    return pl.pallas_call(
        body, name=name,
        out_shape=[jax.ShapeDtypeStruct(a.shape[2:], F32) for a in arrs]
        + [jax.ShapeDtypeStruct((N_DEV,) + a.shape, a.dtype) for a in gather],
        in_specs=[pl.BlockSpec(memory_space=pl.ANY)] * (2 * n) + [_vmem()] * g, out_specs=[_vmem()] * (n + g),
        scratch_shapes=[quarter(a, F32) for a in arrs] + [quarter(a, jnp.bfloat16) for a in arrs] * 3
        + [pltpu.SemaphoreType.DMA((4 * n,)), pltpu.SemaphoreType.DMA((4 * n,)), pltpu.SemaphoreType.DMA((n,))]
        + _gather_sems(g),
        compiler_params=pltpu.CompilerParams(vmem_limit_bytes=VMEM_LIMIT),
    )(*arrs, *halves, *gather)


def _silu(t):
    return t * (1.0 / (1.0 + jnp.exp(-t)))


def _ada_cols(piece_ref, c_all_ref, w_ref, b_ref):
    w = _mx(w_ref[...])
    for d in range(N_DEV):
        piece_ref[d] = _dot(_mx(_silu(c_all_ref[d])), w) + b_ref[...]


def _rope_split(cd, sd):
    n = cd.shape[0]

    def expand(d):
        rep = jnp.broadcast_to(d[:, None, :], (n, 8, LANE)).reshape(8 * n, LANE)
        return pltpu.roll(rep, 0, 1, stride=16, stride_axis=0)

    c, s = expand(cd), expand(sd)
    lane = _lane_iota(c.shape)
    first = jnp.logical_and(lane >= 64, lane < 80)
    second = jnp.logical_and(lane >= 80, lane < 96)
    ck = jnp.where(first, pltpu.roll(c, 80, 1), jnp.where(second, pltpu.roll(c, 96, 1), 0.0))
    cq = jnp.where(lane < 64, 1.0, ck)
    sa = jnp.where(first, -pltpu.roll(s, 80, 1), 0.0)
    sb = jnp.where(second, pltpu.roll(s, 96, 1), 0.0)
    return cq, ck, sa, sb


def _tile_heads(t):
    return jnp.concatenate([t] * MLA_HEADS, axis=1)


def _pre_fwd(x2, shift, scale, ng, w_in, gq, gkv, w_uq, w_ukv, rope, S, ts):
    T, D = x2.shape
    nsb = S // ts
    WQ = MLA_HEADS * LANE

    def body(x_ref, sh_ref, sc_ref, ng_ref, wn_ref, gq_ref, gkv_ref, uqn_ref, ukvn_ref,
             cd_ref, sd_ref,
             hb_ref, zq_ref, zkv_ref, ql_ref, kvl_ref, q_ref, k_ref, v_ref, qs_ref, ks_ref, vs_ref, g_ref, wp_ref,
             wuq_ref, wuk_ref, wuv_ref):
        @pl.when(pl.program_id(0) == 0)
        def _():
            for h in range(MLA_HEADS):
                wuq_ref[:, LANE * h:LANE * h + MLA_QK] = uqn_ref[h]
                wuq_ref[:, LANE * h + MLA_QK:LANE * (h + 1)] = jnp.zeros((Q_LORA, LANE - MLA_QK), wuq_ref.dtype)
                wuk_ref[:, LANE * h:LANE * h + MLA_NOPE] = ukvn_ref[h, :, 0:MLA_NOPE]
                wuk_ref[:, LANE * h + MLA_NOPE:LANE * (h + 1)] = jnp.zeros((KV_LORA, LANE - MLA_NOPE), wuk_ref.dtype)
                wuv_ref[:, 64 * h:64 * (h + 1)] = ukvn_ref[h, :, MLA_NOPE:LANE]
            wp_ref[640:704, :] = jnp.zeros((64, D), wp_ref.dtype)
            wp_ref[736:768, :] = jnp.zeros((32, D), wp_ref.dtype)
            for d in range(N_DEV):
                lo, hi = W_IN_SHARD * d, W_IN_SHARD * (d + 1)
                for nat, pad, size in _w_in_row_runs():
                    a, b = max(nat, lo), min(nat + size, hi)
                    if a < b:
                        wp_ref[pad + a - nat:pad + b - nat, :] = wn_ref[d, a - lo:b - lo, :]

        w_ref = wp_ref
        x = x_ref[...]
        r1 = lax.rsqrt(jnp.mean(x * x, axis=-1, keepdims=True) + EPS)
        h = ((x * r1) * ng_ref[...]) * (1.0 + sc_ref[0]) + sh_ref[0]
        hb = _mx(h)
        hb_ref[...] = hb
        zq = _dot_nt(hb, w_ref[0:384, :])
        zq_ref[...] = zq
        rq = lax.rsqrt(jnp.mean(zq * zq, axis=-1, keepdims=True) + EPS)
        ql = _mx((zq * rq) * gq_ref[...])
        ql_ref[...] = ql
        q = _dot(ql, wuq_ref[...])
        cq, ck, sa, sb = _rope_split(cd_ref[...], sd_ref[...])
        q = (q * _tile_heads(cq) + pltpu.roll(q, WQ - 16, 1) * _tile_heads(sa)
             + pltpu.roll(q, 16, 1) * _tile_heads(sb))
        q_ref[...] = _mx(q * MLA_QSCALE)
        zkv = _dot_nt(hb, w_ref[384:640, :])
        zkv_ref[...] = zkv
        rkv = lax.rsqrt(jnp.mean(zkv * zkv, axis=-1, keepdims=True) + EPS)
        kvl = _mx((zkv * rkv) * gkv_ref[...])
        kvl_ref[...] = kvl
        kr = _dot_nt(hb, w_ref[640:768, :])
        kpe = kr * ck + pltpu.roll(kr, LANE - 16, 1) * sa + pltpu.roll(kr, 16, 1) * sb
        kf = _dot(kvl, wuk_ref[...])
        k_ref[...] = _mx(kf + jnp.concatenate([kpe] * MLA_HEADS, axis=1))
        v_ref[...] = _mx(_dot(kvl, wuv_ref[...]))
        g_ref[:, 0:512] = _dot_nt(hb, w_ref[768:1280, :])
        qs_ref[...] = _mx(_dot_nt(hb, w_ref[1280:1792, :]) * (SWA_SCALE * LOG2E))
        ks_ref[...] = _mx(_dot_nt(hb, w_ref[1792:1920, :]))
        vs_ref[...] = _mx(_dot_nt(hb, w_ref[1920:2048, :]))
        g_ref[:, 512:1024] = _dot_nt(hb, w_ref[2048:2560, :])

    row = lambda w: pl.BlockSpec((ts, w), lambda i: (i, 0))
    dense = pl.BlockSpec((ts // 8, LANE), lambda i: (i, 0))
    full = lambda a: pl.BlockSpec(a.shape, lambda i: (0,) * a.ndim)
    per_b = pl.BlockSpec((1, 1, D), lambda i: (i // nsb, 0, 0))
    out_w = [(D, _MXU_DTYPE), (384, F32), (256, F32), (384, _MXU_DTYPE), (256, _MXU_DTYPE), (WQ, _MXU_DTYPE),
             (WQ, _MXU_DTYPE), (512, _MXU_DTYPE), (512, _MXU_DTYPE), (128, _MXU_DTYPE), (128, _MXU_DTYPE), (1024, F32)]
    built = [(D_IN_PAD, D), (Q_LORA, WQ), (KV_LORA, WQ), (KV_LORA, 512)]
    return pl.pallas_call(
        body, name="pre_fwd", grid=(T // ts,),
        out_shape=[jax.ShapeDtypeStruct((T, w), dt) for w, dt in out_w]
        + [jax.ShapeDtypeStruct(s, w_in.dtype) for s in built],
        in_specs=[row(D), per_b, per_b, full(ng), full(w_in), full(gq), full(gkv), full(w_uq), full(w_ukv), dense, dense],
        out_specs=[row(w) for w, _ in out_w] + [pl.BlockSpec(s, lambda i: (0, 0)) for s in built],
        compiler_params=_cparams(("arbitrary",)),
    )(x2, shift, scale, ng, w_in, gq, gkv, w_uq, w_ukv, *rope)


def _mla_fwd(q3, k3, v3, tq, tk):
    B, S, _ = q3.shape
    nq = S // tq
    assert tq == tk
    HPS = 2
    NH = 2 * HPS

    def body(q_ref, k_ref, v_ref, o_ref, lse_ref):
        rows = lax.broadcasted_iota(jnp.int32, (tq, tk), 0)
        cols = lax.broadcasted_iota(jnp.int32, (tq, tk), 1)
        low_k = _lane_iota((tk, LANE)) < 64
        low = _lane_iota((tq, LANE)) < 64

        def step(qs, kt, carry, masked):
            r0 = kt * tk
            out = []
            for j in range(NH):
                v2 = v_ref[0, r0:r0 + tk, LANE * (j // 2):LANE * (j // 2 + 1)]
                vj = (jnp.where(low_k, v2, 1.0) if j % 2 == 0 else jnp.where(low_k, 1.0, v2)).astype(v2.dtype)
                m, acc = carry[j]
                s = _dot_nt(qs[j], k_ref[0, r0:r0 + tk, LANE * j:LANE * (j + 1)])
                if masked:
                    s = jnp.where(rows >= cols, s, NEG_INF)
                m_new = jnp.maximum(m, jnp.max(s, axis=1, keepdims=True))
                alpha = jnp.exp2(m - m_new)
                p = jnp.exp2(s - m_new)
                acc = alpha * acc + _dot(_mx(p), vj)
                out.append((m_new, acc))
            return tuple(out)

        for qi in range(nq):
            r = slice(qi * tq, (qi + 1) * tq)
            qs = [q_ref[0, r, LANE * j:LANE * (j + 1)] for j in range(NH)]
            init = (jnp.full((tq, 1), NEG_INF, F32), jnp.zeros((tq, LANE), F32))
            carry = (init,) * NH
            for kt in range(qi):
                carry = step(qs, kt, carry, False)
            carry = step(qs, qi, carry, True)
            for t in range(HPS):
                (m0, a0), (m1, a1) = carry[2 * t], carry[2 * t + 1]
                l0 = jnp.where(low, pltpu.roll(a0, 64, 1), a0)
                l1 = jnp.where(low, a1, pltpu.roll(a1, 64, 1))
                o_ref[0, r, LANE * t:LANE * (t + 1)] = jnp.where(low, a0 / l0, a1 / l1)
                lse_ref[0, 2 * t, r, :] = m0 + jnp.log2(l0)
                lse_ref[0, 2 * t + 1, r, :] = m1 + jnp.log2(l1)

    grp = lambda w: pl.BlockSpec((1, S, w), lambda b, hp: (b, 0, hp))
    return pl.pallas_call(
        body, name="mla_fwd", grid=(B, MLA_HEADS // NH),
        out_shape=[jax.ShapeDtypeStruct((B, S, 512), F32), jax.ShapeDtypeStruct((B, MLA_HEADS, S, LANE), F32)],
        in_specs=[grp(NH * LANE), grp(NH * LANE), grp(HPS * LANE)],
        out_specs=[grp(HPS * LANE), pl.BlockSpec((1, NH, S, LANE), lambda b, hp: (b, hp, 0, 0))],
        compiler_params=_cparams(("arbitrary", "arbitrary")),
    )(q3, k3, v3)


def _mla_bwd(q3, k3, v3, o3, do3, lse, tq, tk, hps=2):
    B, S, _ = q3.shape
    nq, nk = S // tq, S // tk
    nh = 2 * hps
    assert tk % tq == 0

    def body(q_ref, k_ref, v_ref, o_ref, do_ref, lse_ref, dq_ref, dk_ref, dv_ref, dkt_ref, dvt_ref):
        rows = lax.broadcasted_iota(jnp.int32, (tq, tk), 0)
        cols = lax.broadcasted_iota(jnp.int32, (tq, tk), 1)
        lane = _lane_iota((tq, LANE))

        def q_tile(qi, _):
            r = pl.ds(pl.multiple_of(qi * tq, tq), tq)
            heads = []
            for j in range(nh):
                lanes = slice(LANE * j, LANE * (j + 1))
                pair = slice(LANE * (j // 2), LANE * (j // 2 + 1))
                do = jnp.where((lane < 64) if j % 2 == 0 else (lane >= 64), do_ref[0, r, pair], 0.0)
                q = q_ref[0, r, lanes]
                heads.append((lanes, pair, q, _mx(q.astype(F32).T), _mx(do), _mx(do.T),
                              jnp.sum(do * o_ref[0, r, pair], axis=1, keepdims=True),
                              jnp.concatenate([lse_ref[0, j, r, :]] * (tk // LANE), axis=1)))
            n_full = (qi * tq) // tk

            def k_tile(kt, dqs, masked):
                kr = pl.ds(pl.multiple_of(kt * tk, tk), tk)
                first = masked and (qi * tq) % tk == 0
                out = []
                dvt = [None] * hps
                for j, (lanes, pair, q, qt, dob, dot_, dcol, lse_c) in enumerate(heads):
                    k = k_ref[0, kr, lanes]
                    s = _dot_nt(q, k)
                    if masked:
                        s = jnp.where(rows + qi * tq >= cols + kt * tk, s, NEG_INF)
                    p = jnp.exp2(s - lse_c)
                    dp = _dot_nt(dob, v_ref[0, kr, pair])
                    dsb = _mx(p * (dp - dcol))
                    if first:
                        dkt_ref[j, kt] = _dot(qt, dsb)
                    else:
                        dkt_ref[j, kt] += _dot(qt, dsb)
                    pv = _dot(dot_, _mx(p))
                    dvt[j // 2] = pv if dvt[j // 2] is None else dvt[j // 2] + pv
                    out.append(dqs[j] + _dot(dsb, k))
                for t in range(hps):
                    if first:
                        dvt_ref[t, kt] = dvt[t]
                    else:
                        dvt_ref[t, kt] += dvt[t]
                return tuple(out)

            dqs = (jnp.zeros((tq, LANE), F32),) * nh
            for kt in range(n_full):
                dqs = k_tile(kt, dqs, False)
            dqs = k_tile(n_full, dqs, True)
            for j in range(nh):
                dq_ref[0, r, heads[j][0]] = (MLA_SCALE * dqs[j]).astype(dq_ref.dtype)
            return 0

        for qi in range(nq):
            q_tile(qi, 0)

        def flush(kt, _):
            kr = pl.ds(pl.multiple_of(kt * tk, tk), tk)
            for j in range(nh):
                dk_ref[0, kr, LANE * j:LANE * (j + 1)] = ((1.0 / LOG2E) * dkt_ref[j, kt].T).astype(dk_ref.dtype)
            for t in range(hps):
                dv_ref[0, kr, LANE * t:LANE * (t + 1)] = dvt_ref[t, kt].T.astype(dv_ref.dtype)
            return 0

        lax.fori_loop(0, nk, flush, 0)

    grp = lambda w: pl.BlockSpec((1, S, w), lambda b, hp: (b, 0, hp))
    return pl.pallas_call(
        body, name="mla_bwd", grid=(B, MLA_HEADS // nh),
        out_shape=[jax.ShapeDtypeStruct((B, S, 1024), MLA_GRAD_DTYPE), jax.ShapeDtypeStruct((B, S, 1024), MLA_GRAD_DTYPE),
                   jax.ShapeDtypeStruct((B, S, 512), MLA_GRAD_DTYPE)],
        in_specs=[grp(nh * LANE), grp(nh * LANE), grp(hps * LANE), grp(hps * LANE), grp(hps * LANE),
                  pl.BlockSpec((1, nh, S, LANE), lambda b, hp: (b, hp, 0, 0))],
        out_specs=[grp(nh * LANE), grp(nh * LANE), grp(hps * LANE)],
        scratch_shapes=[pltpu.VMEM((nh, nk, LANE, tk), F32), pltpu.VMEM((hps, nk, LANE, tk), F32)],
        compiler_params=_cparams(("arbitrary", "arbitrary")),
    )(q3, k3, v3, o3, do3, lse)


def _swa_consts(sink_ref):
    W = WINDOW
    row = lax.broadcasted_iota(jnp.int32, (4 * W, LANE), 0)
    out = []
    for g in range(SWA_KV_HEADS):
        slope = jnp.zeros((4 * W, LANE), F32)
        sink = jnp.zeros((4 * W, LANE), F32)
        for p in range(4):
            h = p + 4 * g
            here = jnp.logical_and(row >= W * p, row < W * (p + 1))
            slope = jnp.where(here, float(LOG2E * 2.0 ** (-8.0 * (h + 1) / SWA_HEADS)), slope)
            sink = jnp.where(here, LOG2E * sink_ref[:, h:h + 1], sink)
        out.append((jnp.concatenate([slope, slope], axis=1), sink))
    return out


def _wide(col):
    return jnp.concatenate([col, col], axis=1)


def _swa_block(q_ref, k_ref, v_ref, pb_ref, pr_ref, n, i, ij):
    W = WINDOW
    kb = jnp.maximum(n - 1, 0)
    r = pl.ds(pl.multiple_of(i * W, W), W)
    kr = pl.ds(pl.multiple_of(kb * W, W), 2 * W)
    q4, k2, v2 = q_ref[0, r, :], k_ref[0, kr, :], v_ref[0, kr, :]
    pq = _wide(pb_ref[0, r, :])
    pk = jnp.concatenate([pr_ref[0, pl.ds(kb, 1), :], pr_ref[0, pl.ds(kb + 1, 1), :]], axis=1)
    rel = ij + (n - kb) * W
    dist = jnp.where(jnp.logical_and(rel >= 0, rel < W), pq - pk, float("inf"))
    return r, kb, kr, q4, k2, v2, jnp.concatenate([dist] * 4, axis=0)


def _swa_stack(x4, g, dtype):
    lane = _lane_iota((WINDOW, LANE))
    mine = (lane < 64) if g == 0 else (lane >= 64)
    return jnp.concatenate([jnp.where(mine, x4[:, LANE * p:LANE * (p + 1)], 0).astype(dtype) for p in range(4)], axis=0)


def _swa_unstack(ref, r, lo, hi, scale=None):
    W = WINDOW
    low = _lane_iota((W, LANE)) < 64
    for p in range(4):
        t = jnp.where(low, lo[W * p:W * (p + 1)], hi[W * p:W * (p + 1)])
        ref[0, r, LANE * p:LANE * (p + 1)] = t if scale is None else scale * t


def _swa_fwd(qs3, ks3, vs3, posb, posr, sinks):
    B, S, _ = qs3.shape
    W = WINDOW
    nb = S // W
    nh = SWA_SEQ_SPLIT
    nbh = nb // nh

    def body(q_ref, k_ref, v_ref, pb_ref, pr_ref, sink_ref, o_ref, lse_ref):
        ij = lax.broadcasted_iota(jnp.int32, (W, 2 * W), 0) - lax.broadcasted_iota(jnp.int32, (W, 2 * W), 1)
        consts = _swa_consts(sink_ref)
        n0 = pl.program_id(1) * nbh

        def blk(i, _):
            n = n0 + i
            r, _, _, q4, k2, v2, dist4 = _swa_block(q_ref, k_ref, v_ref, pb_ref, pr_ref, n, i, ij)
            o_g = []
            for g, (slope, sink) in enumerate(consts):
                s = _dot_nt(_swa_stack(q4, g, q4.dtype), k2) - slope * dist4
                m = jnp.maximum(jnp.max(s, axis=1, keepdims=True), sink)
                e = jnp.exp2(s - _wide(m))
                l = jnp.sum(e, axis=1, keepdims=True) + jnp.exp2(sink - m)
                o_g.append(_dot(_mx(e), v2) * (1.0 / l))
                lse_ref[0, i, g] = m + jnp.log2(l)
            _swa_unstack(o_ref, r, o_g[0], o_g[1])
            return 0

        for i in range(nbh):
            blk(i, 0)

    seq = lambda w: pl.BlockSpec((1, S, w), lambda b, h: (b, 0, 0))
    part = lambda w: pl.BlockSpec((1, S // nh, w), lambda b, h: (b, h, 0))
    lse_spec = pl.BlockSpec((1, nbh, 2, 4 * W, LANE), lambda b, h: (b, h, 0, 0, 0))
    return pl.pallas_call(
        body, name="swa_fwd", grid=(B, nh),
        out_shape=[jax.ShapeDtypeStruct((B, S, 512), F32), jax.ShapeDtypeStruct((B, nb, 2, 4 * W, LANE), F32)],
        in_specs=[part(512), seq(LANE), seq(LANE), part(LANE), pl.BlockSpec((1, nb, W), lambda b, h: (b, 0, 0)),
                  pl.BlockSpec((1, LANE), lambda b, h: (0, 0))],
        out_specs=[part(512), lse_spec],
        compiler_params=_cparams(("arbitrary", "arbitrary")),
    )(qs3, ks3, vs3, posb, posr, sinks)


def _swa_bwd(qs3, ks3, vs3, posb, posr, sinks, os3, do3, lse):
    B, S, _ = qs3.shape
    W = WINDOW
    nb = S // W
    nh = SWA_SEQ_SPLIT
    nbh = nb // nh
    assert nbh % 2 == 0 and nbh * nh * W == S

    def body(q_ref, k_ref, v_ref, pb_ref, pr_ref, sink_ref, o_ref, do_ref, lse_ref, dq_ref, dk_ref, dv_ref, dsink_ref,
             dkt_ref, dvt_ref):
        lane1 = _lane_iota((1, LANE))
        ij = lax.broadcasted_iota(jnp.int32, (W, 2 * W), 0) - lax.broadcasted_iota(jnp.int32, (W, 2 * W), 1)
        consts = _swa_consts(sink_ref)
        hh = pl.program_id(1)
        n0 = hh * nbh

        @pl.when(hh == 0)
        def _():
            dkt_ref[...] = jnp.zeros_like(dkt_ref)
            dvt_ref[...] = jnp.zeros_like(dvt_ref)

        @pl.when(jnp.logical_and(pl.program_id(0) == 0, hh == 0))
        def _():
            dsink_ref[...] = jnp.zeros_like(dsink_ref)

        def blk(i, dsink):
            n = n0 + i
            r, kb, _, q4, k2, v2, dist4 = _swa_block(q_ref, k_ref, v_ref, pb_ref, pr_ref, n, i, ij)
            o4, do4 = o_ref[0, r, :], do_ref[0, r, :]
            dq_g = []
            dkt = jnp.zeros((LANE, 2 * W), F32)
            dvt = jnp.zeros((LANE, 2 * W), F32)
            for g, (slope, sink) in enumerate(consts):
                q_st = _swa_stack(q4, g, F32)
                do_st = _swa_stack(do4, g, F32)
                dcol = jnp.sum(do_st * _swa_stack(o4, g, F32), axis=1, keepdims=True)
                qst, dob = _mx(q_st), _mx(do_st)
                lse_c = lse_ref[0, i, g]
                pr = jnp.exp2(_dot_nt(qst, k2) - slope * dist4 - _wide(lse_c))
                dsb = _mx(pr * (_dot_nt(dob, v2) - dcol))
                psd = jnp.exp2(sink - lse_c)[:, 0:1] * dcol
                for p in range(4):
                    dsink = dsink - jnp.where(lane1 == p + 4 * g,
                                              jnp.sum(psd[W * p:W * (p + 1)], axis=0, keepdims=True), 0.0)
                dq_g.append(_dot(dsb, k2))
                dkt = dkt + _dot(_mx(q_st.T), dsb)
                dvt = dvt + _dot(_mx(do_st.T), _mx(pr))
            _swa_unstack(dq_ref, r, dq_g[0], dq_g[1], SWA_SCALE)
            dkt_ref[kb] += dkt[:, 0:W]
            dkt_ref[kb + 1] += dkt[:, W:2 * W]
            dvt_ref[kb] += dvt[:, 0:W]
            dvt_ref[kb + 1] += dvt[:, W:2 * W]
            return dsink

        dsink = jnp.zeros((1, LANE), F32)
        for i in range(nbh):
            dsink = blk(i, dsink)
        dsink_ref[...] += dsink

        @pl.when(hh == nh - 1)
        def _():
            def flush(n, _):
                r = pl.ds(pl.multiple_of(n * W, W), W)
                dk_ref[0, r, :] = (1.0 / LOG2E) * dkt_ref[n].T
                dv_ref[0, r, :] = dvt_ref[n].T
                return 0

            lax.fori_loop(0, nb, flush, 0)

    seq = lambda w: pl.BlockSpec((1, S, w), lambda b, h: (b, 0, 0))
    part = lambda w: pl.BlockSpec((1, S // nh, w), lambda b, h: (b, h, 0))
    return pl.pallas_call(
        body, name="swa_bwd", grid=(B, nh),
        out_shape=[jax.ShapeDtypeStruct((B, S, 512), F32), jax.ShapeDtypeStruct((B, S, LANE), F32),
                   jax.ShapeDtypeStruct((B, S, LANE), F32), jax.ShapeDtypeStruct((1, LANE), F32)],
        in_specs=[part(512), seq(LANE), seq(LANE), part(LANE), pl.BlockSpec((1, nb, W), lambda b, h: (b, 0, 0)),
                  pl.BlockSpec((1, LANE), lambda b, h: (0, 0)), part(512),
                  pl.BlockSpec((1, S // nh, 512), lambda b, h: (b, h, 1)),
                  pl.BlockSpec((1, nbh, 2, 4 * W, LANE), lambda b, h: (b, h, 0, 0, 0))],
        out_specs=[part(512), seq(LANE), seq(LANE), pl.BlockSpec((1, LANE), lambda b, h: (0, 0))],
        scratch_shapes=[pltpu.VMEM((nb, LANE, W), F32), pltpu.VMEM((nb, LANE, W), F32)],
        compiler_params=_cparams(("arbitrary", "arbitrary")),
    )(qs3, ks3, vs3, posb, posr, sinks, os3, do3, lse)


def _post(om, osw, g, w_out, x2, gate, fg, tgt, S, ts):
    T, D = x2.shape
    nsb = S // ts

    def body(om_ref, os_ref, g_ref, w_ref, x_ref, gate_ref, fg_ref, t_ref,
             dx2_ref, do_ref, dg_ref, loss_ref, dfg_ref, dgate_ref, dw_ref, dwb_ref):
        i = pl.program_id(0)
        gv = g_ref[...]
        sg = 1.0 / (1.0 + jnp.exp(-gv))
        silu = gv * sg
        o = jnp.concatenate([om_ref[...], os_ref[...]], axis=1)
        ab = _mx(o * silu)
        y = _dot(ab, w_ref[...])
        gate = gate_ref[0]
        xo = x_ref[...] + gate * y
        r2 = lax.rsqrt(jnp.mean(xo * xo, axis=-1, keepdims=True) + EPS)
        xh = xo * r2
        fg = fg_ref[...]
        diff = xh * fg - t_ref[...]
        sq = jnp.sum(diff * diff, axis=0, keepdims=True)
        part = sq[:, 0:LANE]
        for t in range(1, D // LANE):
            part = part + sq[:, LANE * t:LANE * (t + 1)]
        dout = diff * (1.0 / D)
        dxh = dout * fg
        dx2 = r2 * (dxh - xh * jnp.mean(dxh * xh, axis=-1, keepdims=True))
        dx2_ref[...] = dx2
        dyb = _mx(dx2 * gate)
        da = _dot_nt(dyb, w_ref[...])
        do_ref[...] = da * silu
        dg_ref[...] = _mx(da * o * (sg * (1.0 + gv * (1.0 - sg))))

        @pl.when(i == 0)
        def _():
            loss_ref[...] = jnp.zeros_like(loss_ref)
            dfg_ref[...] = jnp.zeros_like(dfg_ref)
            dw_ref[...] = jnp.zeros_like(dw_ref)

        @pl.when(i % nsb == 0)
        def _():
            dgate_ref[...] = jnp.zeros_like(dgate_ref)

        loss_ref[...] += (0.5 / D) * part
        dfg_ref[...] += jnp.sum(dout * xh, axis=0, keepdims=True)
        dgate_ref[0] += jnp.sum(dx2 * y, axis=0, keepdims=True)
        dw_ref[...] += _dot_tn(ab, dyb)

        @pl.when(i == T // ts - 1)
        def _():
            chunks = [dw_ref[512 + 64 * PAIR_INV[h]:512 + 64 * (PAIR_INV[h] + 1), :] for h in range(SWA_HEADS)]
            for h in range(SWA_HEADS):
                dw_ref[512 + 64 * h:512 + 64 * (h + 1), :] = chunks[h]
            dwb_ref[...] = dw_ref[...].astype(jnp.bfloat16)

    row = lambda w: pl.BlockSpec((ts, w), lambda i: (i, 0))
    full = lambda a: pl.BlockSpec(a.shape, lambda i: (0,) * a.ndim)
    per_b = pl.BlockSpec((1, 1, D), lambda i: (i // nsb, 0, 0))
    return pl.pallas_call(
        body, name="post", grid=(T // ts,),
        out_shape=[jax.ShapeDtypeStruct((T, D), F32), jax.ShapeDtypeStruct((T, 1024), F32),
                   jax.ShapeDtypeStruct((T, 1024), _MXU_DTYPE), jax.ShapeDtypeStruct((1, LANE), F32),
                   jax.ShapeDtypeStruct((1, D), F32), jax.ShapeDtypeStruct(gate.shape, F32),
                   jax.ShapeDtypeStruct(w_out.shape, F32), jax.ShapeDtypeStruct(w_out.shape, jnp.bfloat16)],
        in_specs=[row(512), row(512), row(1024), full(w_out), row(D), per_b, full(fg), row(D)],
        out_specs=[row(D), row(1024), row(1024),
                   pl.BlockSpec((1, LANE), lambda i: (0, 0)), pl.BlockSpec((1, D), lambda i: (0, 0)), per_b,
                   full(w_out), full(w_out)],
        compiler_params=_cparams(("arbitrary",)),
    )(om, osw, g, w_out, x2, gate, fg, tgt)


def _pre_bwd(dq, dk, dv, dqs, dks, dvs, dg, zq, zkv, ql, kvl, x2, dx2, scale, ng, w_in, gq, gkv, w_uq, w_uk, w_uv,
             rope, S, ts):
    T, D = x2.shape
    nsb = S // ts
    WQ = MLA_HEADS * LANE

    def body(dq_ref, dk_ref, dv_ref, dqs_ref, dks_ref, dvs_ref, dg_ref, zq_ref, zkv_ref, ql_ref, kvl_ref, x_ref, dx2_ref,
             sc_ref,
             ng_ref, w_ref, gq_ref, gkv_ref, wuq_ref, wuk_ref, wuv_ref, cd_ref, sd_ref,
             gx_ref, dz_ref, dgq_ref, dgkv_ref, dng_ref, dsh_ref, dsc_ref, uq_ref, uqb_ref, ukv_ref, ukvb_ref,
             dwuq_ref, dwuk_ref, dwuv_ref):
        i = pl.program_id(0)

        @pl.when(i == 0)
        def _():
            dgq_ref[...] = jnp.zeros_like(dgq_ref)
            dgkv_ref[...] = jnp.zeros_like(dgkv_ref)
            dng_ref[...] = jnp.zeros_like(dng_ref)
            dwuq_ref[...] = jnp.zeros_like(dwuq_ref)
            dwuk_ref[...] = jnp.zeros_like(dwuk_ref)
            dwuv_ref[...] = jnp.zeros_like(dwuv_ref)

        @pl.when(i % nsb == 0)
        def _():
            dsh_ref[...] = jnp.zeros_like(dsh_ref)
            dsc_ref[...] = jnp.zeros_like(dsc_ref)

        def norm_bwd(z, dy, gain):
            r = lax.rsqrt(jnp.mean(z * z, axis=-1, keepdims=True) + EPS)
            zh = z * r
            dzh = dy * gain
            return r * (dzh - zh * jnp.mean(dzh * zh, axis=-1, keepdims=True)), jnp.sum(dy * zh, axis=0, keepdims=True)

        tables = _rope_split(cd_ref[...], sd_ref[...])
        ng = ng_ref[...]
        sc1 = 1.0 + sc_ref[0]

        def rows_chain(rs):
            cq, ck, sa, sb = (t[rs] for t in tables)
            dqr = dq_ref[rs, :].astype(F32)
            dqb = _mx(dqr * _tile_heads(cq) + pltpu.roll(dqr * _tile_heads(sa), 16, 1)
                      + pltpu.roll(dqr * _tile_heads(sb), WQ - 16, 1))
            p_uq = _dot_tn(ql_ref[rs, :], dqb)
            dzq, dgq = norm_bwd(zq_ref[rs, :], _dot_nt(dqb, wuq_ref[...]), gq_ref[...])
            dkr = dk_ref[rs, :].astype(F32)
            dkb = _mx(dkr)
            p_uk = _dot_tn(kvl_ref[rs, :], dkb)
            dvb = _mx(dv_ref[rs, :])
            p_uv = _dot_tn(kvl_ref[rs, :], dvb)
            dzkv, dgkv = norm_bwd(zkv_ref[rs, :], _dot_nt(dkb, wuk_ref[...]) + _dot_nt(dvb, wuv_ref[...]), gkv_ref[...])
            dkpe = dkr[:, 0:LANE]
            for h in range(1, MLA_HEADS):
                dkpe = dkpe + dkr[:, LANE * h:LANE * (h + 1)]
            dkro = dkpe * ck + pltpu.roll(dkpe * sa, 16, 1) + pltpu.roll(dkpe * sb, LANE - 16, 1)
            dz_ref[rs, 0:384] = _mx(dzq)
            dz_ref[rs, 384:640] = _mx(dzkv)
            dz_ref[rs, 640:768] = _mx(dkro)
            dz_ref[rs, 768:1280] = dg_ref[rs, 0:512]
            dz_ref[rs, 1280:1792] = _mx(dqs_ref[rs, :])
            dz_ref[rs, 1792:1920] = _mx(dks_ref[rs, :])
            dz_ref[rs, 1920:2048] = _mx(dvs_ref[rs, :])
            dz_ref[rs, 2048:2560] = dg_ref[rs, 512:1024]
            dh = _dot(dz_ref[rs, :], w_ref[...])
            x = x_ref[rs, :]
            r1 = lax.rsqrt(jnp.mean(x * x, axis=-1, keepdims=True) + EPS)
            xn = x * r1
            dxn = dh * ng * sc1
            gx_ref[rs, :] = dx2_ref[rs, :] + r1 * (dxn - xn * jnp.mean(dxn * xn, axis=-1, keepdims=True))
            return (p_uq, p_uk, p_uv, dgq, dgkv, jnp.sum(dh, axis=0, keepdims=True),
                    jnp.sum(dh * (xn * ng), axis=0, keepdims=True), jnp.sum(dh * xn * sc1, axis=0, keepdims=True))

        hr = ts // 2
        parts = [rows_chain(slice(hr * t, hr * (t + 1))) for t in range(2)]
        p_uq, p_uk, p_uv, dgq, dgkv, dsh, dsc, dng = (a + b for a, b in zip(*parts))
        dwuq_ref[...] += p_uq
        dwuk_ref[...] += p_uk
        dwuv_ref[...] += p_uv
        dgq_ref[...] += dgq
        dgkv_ref[...] += dgkv
        dsh_ref[0] += dsh
        dsc_ref[0] += dsc
        dng_ref[...] += dng

        @pl.when(i == T // ts - 1)
        def _():
            for h in range(MLA_HEADS):
                uq = dwuq_ref[:, LANE * h:LANE * h + MLA_QK]
                ukv = jnp.concatenate([dwuk_ref[:, LANE * h:LANE * h + MLA_NOPE], dwuv_ref[:, 64 * h:64 * (h + 1)]], axis=1)
                uq_ref[h] = uq
                uqb_ref[h] = uq.astype(jnp.bfloat16)
                ukv_ref[h] = ukv
                ukvb_ref[h] = ukv.astype(jnp.bfloat16)

    row = lambda w: pl.BlockSpec((ts, w), lambda i: (i, 0))
    full = lambda a: pl.BlockSpec(a.shape, lambda i: (0,) * a.ndim, pipeline_mode=pl.Buffered(1))
    slot = lambda r, c, dt: (jax.ShapeDtypeStruct((MLA_HEADS, r, c), dt), pl.BlockSpec((MLA_HEADS, r, c), lambda i: (0, 0, 0)))
    slots = [slot(Q_LORA, MLA_QK, F32), slot(Q_LORA, MLA_QK, jnp.bfloat16), slot(KV_LORA, LANE, F32), slot(KV_LORA, LANE, jnp.bfloat16)]
    per_b = pl.BlockSpec((1, 1, D), lambda i: (i // nsb, 0, 0))
    dense = pl.BlockSpec((ts // 8, LANE), lambda i: (i, 0))
    vec = lambda w: pl.BlockSpec((1, w), lambda i: (0, 0))
    return pl.pallas_call(
        body, name="pre_bwd", grid=(T // ts,),
        out_shape=[jax.ShapeDtypeStruct((T, D), F32), jax.ShapeDtypeStruct((T, D_IN_PAD), _MXU_DTYPE), jax.ShapeDtypeStruct((1, 384), F32),
                   jax.ShapeDtypeStruct((1, 256), F32), jax.ShapeDtypeStruct((1, D), F32),
                   jax.ShapeDtypeStruct(scale.shape, F32), jax.ShapeDtypeStruct(scale.shape, F32)] + [t[0] for t in slots],
        in_specs=[row(WQ), row(WQ), row(512), row(512), row(LANE), row(LANE), row(1024), row(384), row(256), row(384),
                  row(256), row(D), row(D), per_b, full(ng), full(w_in), full(gq), full(gkv), full(w_uq), full(w_uk), full(w_uv),
                  dense, dense],
        out_specs=[row(D), row(D_IN_PAD), vec(384), vec(256), vec(D), per_b, per_b] + [t[1] for t in slots],
        scratch_shapes=[pltpu.VMEM(w_uq.shape, F32), pltpu.VMEM(w_uk.shape, F32), pltpu.VMEM(w_uv.shape, F32)],
        compiler_params=_cparams(("arbitrary",)),
    )(dq, dk, dv, dqs, dks, dvs, dg, zq, zkv, ql, kvl, x2, dx2, scale, ng, w_in, gq, gkv, w_uq, w_uk, w_uv, *rope)


def _w_in_row_runs():
    runs = [(0, 0, 640), (640, 704, 32), (672, 768, 512)]
    runs += [(1184 + 64 * h, 1280 + 64 * PAIR_INV[h], 64) for h in range(8)]
    runs += [(1696, 1792, 256)]
    runs += [(1952 + 64 * h, 2048 + 64 * PAIR_INV[h], 64) for h in range(8)]
    return runs


W_IN_SHARD = D_IN // N_DEV
W_IN_SLOT = 320


def _dw_in_t(dz, hb, tn, tk):
    T, M = dz.shape
    N = hb.shape[1]
    nk = T // tk

    def body(a_ref, b_ref, o_ref, ob_ref, acc_ref):
        k = pl.program_id(1)

        @pl.when(k == 0)
        def _():
            acc_ref[...] = jnp.zeros_like(acc_ref)

        acc_ref[...] += _dot_tn(a_ref[...], b_ref[...])

        @pl.when(k == nk - 1)
        def _():
            for d in range(N_DEV):
                lo, hi = W_IN_SHARD * d, W_IN_SHARD * (d + 1)
                for nat, pad, size in _w_in_row_runs():
                    a, b = max(nat, lo), min(nat + size, hi)
                    if a < b:
                        piece = acc_ref[pad + a - nat:pad + b - nat, :]
                        o_ref[d, a - lo:b - lo, :] = piece
                        ob_ref[d, a - lo:b - lo, :] = piece.astype(ob_ref.dtype)
                o_ref[d, W_IN_SHARD:W_IN_SLOT, :] = jnp.zeros((W_IN_SLOT - W_IN_SHARD, tn), F32)
                ob_ref[d, W_IN_SHARD:W_IN_SLOT, :] = jnp.zeros((W_IN_SLOT - W_IN_SHARD, tn), ob_ref.dtype)

    slots = pl.BlockSpec((N_DEV, W_IN_SLOT, tn), lambda j, k: (0, 0, j))
    return pl.pallas_call(
        body, name="dw_in", grid=(N // tn, nk),
        out_shape=[jax.ShapeDtypeStruct((N_DEV, W_IN_SLOT, N), F32), jax.ShapeDtypeStruct((N_DEV, W_IN_SLOT, N), jnp.bfloat16)],
        in_specs=[pl.BlockSpec((tk, M), lambda j, k: (k, 0)), pl.BlockSpec((tk, tn), lambda j, k: (k, j))],
        out_specs=[slots, slots],
        scratch_shapes=[pltpu.VMEM((M, tn), F32)],
        compiler_params=_cparams(("arbitrary", "arbitrary")),
    )(dz, hb)


SHARDED = (3, 6, 7, 9)
PART_SLICES = {10: (128, 1152), 2: (1152, 2176), 4: (2176, 2560), 5: (2560, 2816), 8: (2816, 2824)}


def _finalize_adamw(parts_all, dmod_all, dmod_cols, c_all, ws, sharded_grads, ms, vs):
    n = len(ws)
    local = [t for t in range(n) if t not in SHARDED[2:]]

    def body(*refs):
        p_ref, dm_ref, dmc_ref, c_ref = refs[:4]
        w_refs, gs_refs, m_refs, v_refs = refs[4:4 + n], refs[4 + n:8 + n], refs[8 + n:8 + 2 * n], refs[8 + 2 * n:8 + 3 * n]
        outs = refs[8 + 3 * n:]
        loss_ref, gl_refs = outs[0], outs[1:1 + len(local)]
        d_refs, nm_refs, nv_refs = (outs[1 + len(local) + t * n:1 + len(local) + (t + 1) * n] for t in range(3))
        ps_ref = outs[-1]
        rd = lambda r: r[:, 0, :] if len(r.shape) == 3 else r[...]

        def wr(r, val):
            if len(r.shape) == 3:
                r[:, 0, :] = val
            else:
                r[...] = val

        acc = p_ref[0]
        for j in range(1, N_DEV):
            acc = acc + p_ref[j]
        ps_ref[...] = acc
        loss_ref[...] = jnp.sum(acc[:, 0:LANE], axis=1, keepdims=True)
        db = dm_ref[0:1, :]
        for j in range(1, dm_ref.shape[0]):
            db = db + dm_ref[j:j + 1, :]
        grads = {0: _dot_tn(_mx(_silu(c_ref[...])), _mx(dmc_ref[...])), 1: db}
        for t, (lo, hi) in PART_SLICES.items():
            grads[t] = ps_ref[:, lo:hi]
        for i, t in enumerate(SHARDED):
            grads[t] = gs_refs[i][...]
        grads[SHARDED[0]] = gs_refs[0][0:W_IN_SHARD, :]
        grads[SHARDED[1]] = gs_refs[1][...].T
        for i, t in enumerate(local):
            wr(gl_refs[i], grads[t])
        for t in range(n):
            gv = grads[t]
            nm = ADAM_B1 * rd(m_refs[t]) + (1.0 - ADAM_B1) * gv
            nv = ADAM_B2 * rd(v_refs[t]) + (1.0 - ADAM_B2) * (gv * gv)
            m_hat = nm / (1.0 - ADAM_B1 ** ADAM_STEP)
            v_hat = nv / (1.0 - ADAM_B2 ** ADAM_STEP)
            wr(d_refs[t], -ADAM_LR * (m_hat / (jnp.sqrt(v_hat) + ADAM_EPS) + ADAM_WD * rd(w_refs[t])))
            wr(nm_refs[t], nm)
            wr(nv_refs[t], nv)

    like = lambda arrs: [jax.ShapeDtypeStruct(a.shape, F32) for a in arrs]
    out = pl.pallas_call(
        body, name="finalize_adamw",
        out_shape=[jax.ShapeDtypeStruct((1, 1), F32)] + like([ws[t] for t in local]) + like(ws) * 3,
        in_specs=[_vmem()] * (8 + 3 * n), out_specs=[_vmem()] * (1 + len(local) + 3 * n),
        scratch_shapes=[pltpu.VMEM(parts_all.shape[1:], F32)],
        compiler_params=_cparams(),
    )(parts_all, dmod_all, dmod_cols, c_all, *ws, *sharded_grads, *ms, *vs)
    k = 1 + len(local)
    return out[0], dict(zip(local, out[1:k])), out[k:k + n], out[k + n:k + 2 * n], out[k + 2 * n:]


def _pair_perm(a, axis, order):
    a = jnp.moveaxis(a, axis, -1)
    lead = a.shape[:-1]
    a = a.reshape(lead + (8, 64))[..., list(order), :].reshape(lead + (512,))
    return jnp.moveaxis(a, -1, axis)


def _rope_table(positions):
    T = positions.size
    inv = ROPE_THETA ** (-jnp.arange(0, MLA_ROPE, 2, dtype=F32) / MLA_ROPE)
    pos = jnp.repeat(positions.reshape(T // 8, 8)[:, ::-1].astype(F32), MLA_ROPE // 2, axis=1)
    ang = pos * jnp.tile(inv, 8)[None, :]
    return jnp.cos(ang), jnp.sin(ang)


def _local_step(x, mod, positions, ng, w_in_t, gq, gkv, w_uq, w_ukv, sinks, w_out, fg, tgt,
                ts=512, fq=512, fk=512, bq=512, bk=512):
    B, S, D = x.shape
    T = B * S
    x2 = x.reshape(T, D)
    shift, scale, gate = (mod[:, None, k * D:(k + 1) * D] for k in range(3))
    w_out_p = jnp.concatenate([w_out[:512], _pair_perm(w_out[512:], 0, PAIR_ORDER)], axis=0)
    rope = _rope_table(positions)
    posf = positions.astype(F32)
    posb = jnp.broadcast_to(posf[:, :, None], (B, S, LANE))
    posr = posf.reshape(B, S // WINDOW, WINDOW)
    sinks_l = jnp.pad(sinks.reshape(1, SWA_HEADS), ((0, 0), (0, LANE - SWA_HEADS)))

    (hb, zq, zkv, ql, kvl, q, k, v, qs, ks, vs, g, w_in_p, w_uq_p, w_uk_p, w_uv) = _pre_fwd(
        x2, shift, scale, ng, w_in_t, gq, gkv, w_uq, w_ukv, rope, S, ts)
    r3 = lambda a: a.reshape(B, S, a.shape[-1])
    om, lse_m = _mla_fwd(r3(q), r3(k), r3(v), fq, fk)
    osw, lse_s = _swa_fwd(r3(qs), r3(ks), r3(vs), posb, posr, sinks_l)
    dx2, do, dg, loss_v, dfg, dgate, *dw_out = _post(
        om.reshape(T, 512), osw.reshape(T, 512), g, w_out_p, x2, gate, fg.reshape(1, D), tgt.reshape(T, D), S, ts)
    do3 = r3(do)
    dq, dk, dv = _mla_bwd(r3(q), r3(k), r3(v), om, do3, lse_m, bq, bk)
    dqs, dks, dvs, dsink = _swa_bwd(r3(qs), r3(ks), r3(vs), posb, posr, sinks_l, osw, do3, lse_s)
    f2 = lambda a: a.reshape(T, a.shape[-1])
    gx, dz, dgq, dgkv, dng, dsh, dsc, *dw_u = _pre_bwd(
        f2(dq), f2(dk), f2(dv), f2(dqs), f2(dks), f2(dvs), dg, zq, zkv, ql, kvl, x2, dx2, scale, ng, w_in_p, gq, gkv,
        w_uq_p, w_uk_p, w_uv, rope, S, ts)
    tk = min(T, 1024)
    dw_in_t = _dw_in_t(dz, hb, 512, tk)
    parts = jnp.concatenate([loss_v, dfg, dng, dgq, dgkv, dsink], axis=1)
    dmod = jnp.concatenate([dsh, dsc, dgate], axis=2).reshape(B, 3 * D)
    return gx.reshape(B, S, D), dw_in_t, tuple(dw_u[0:2]), tuple(dw_u[2:4]), tuple(dw_out), parts, dmod


def kernel(x, c, positions, w_ada, b_ada, norm_gain, w_in, q_norm_gain, kv_norm_gain, w_uq, w_ukv, swa_sinks, w_out, final_gain, loss_target, m_w_ada, m_b_ada, m_norm_gain, m_w_in, m_q_norm_gain, m_kv_norm_gain, m_w_uq, m_w_ukv, m_swa_sinks, m_w_out, m_final_gain, v_w_ada, v_b_ada, v_norm_gain, v_w_in, v_q_norm_gain, v_kv_norm_gain, v_w_uq, v_w_ukv, v_swa_sinks, v_w_out, v_final_gain):
    B, S, D = x.shape
    me = 4 * lax.axis_index("x") + 2 * lax.axis_index("y") + lax.axis_index("c")
    bf = _MXU_DTYPE

    ncol = w_ada.shape[2]
    b_cols = lax.dynamic_slice_in_dim(b_ada, me * ncol, ncol, axis=1)
    c_all, win_g, wuq_g, wukv_g, wout_g, mod_g = _all_gather(
        [c, (jnp.transpose(w_in, (2, 0, 1)), bf), (jnp.transpose(w_uq[0]), bf, True), (w_ukv[0], bf), (w_out[0], bf)],
        "ag_weights",
        fused=((w_ada[0], b_cols), _ada_cols, jax.ShapeDtypeStruct((N_DEV, B, ncol), F32)))
    c_all = c_all.reshape(N_DEV * B, D)
    w_out_f = wout_g.reshape(D, D)

    mod = lax.dynamic_index_in_dim(mod_g, me, axis=1, keepdims=False)
    mod = jnp.transpose(mod, (1, 0, 2)).reshape(B, 3 * D)

    gx, dw_in_t, dw_uq, dw_ukv, dw_out, parts, dmod = _local_step(
        x, mod, positions, norm_gain, win_g, q_norm_gain, kv_norm_gain, wuq_g, wukv_g, swa_sinks,
        w_out_f, final_gain, loss_target)

    slots = lambda pair: tuple(a.reshape((4, 2, -1) + a.shape[-1:]) for a in pair)
    g_w_in_t, g_w_uq, g_w_ukv, g_w_out, parts_g, dmod_g = _reduce_scatter(
        [slots(dw_in_t), slots(dw_uq), slots(dw_ukv), slots(dw_out)], "rs_grads",
        gather=(parts, dmod))

    dmod_all = dmod_g.reshape(N_DEV * B, 3 * D)
    dmod_cols = lax.dynamic_slice_in_dim(dmod_all, me * ncol, ncol, axis=1)
    ws = [w_ada, b_ada, norm_gain, w_in, q_norm_gain, kv_norm_gain, w_uq, w_ukv, swa_sinks, w_out, final_gain]
    ms = [m_w_ada, m_b_ada, m_norm_gain, m_w_in, m_q_norm_gain, m_kv_norm_gain, m_w_uq, m_w_ukv, m_swa_sinks, m_w_out,
          m_final_gain]
    vs = [v_w_ada, v_b_ada, v_norm_gain, v_w_in, v_q_norm_gain, v_kv_norm_gain, v_w_uq, v_w_ukv, v_swa_sinks, v_w_out,
          v_final_gain]
    two_d = lambda a: a.reshape((1, a.shape[0]) if a.ndim == 1 else a.shape[-2:])
    view = {SHARDED[0]: lambda a: jnp.transpose(a, (2, 0, 1)), SHARDED[1]: lambda a: jnp.transpose(a[0])}
    flat = lambda arrs: [view[t](a) if t in view else two_d(a) for t, a in enumerate(arrs)]
    sharded = {3: g_w_in_t, 6: g_w_uq, 7: g_w_ukv, 9: g_w_out}
    loss, local_grads, deltas, new_ms, new_vs = _finalize_adamw(
        parts_g, dmod_all, dmod_cols, c_all, flat(ws), [sharded[t] for t in SHARDED], flat(ms), flat(vs))
    grads = [local_grads[t] if t in local_grads else sharded[t] for t in range(len(ws))]
    back = {SHARDED[0]: lambda a: jnp.transpose(a, (1, 2, 0)), SHARDED[1]: lambda a: jnp.transpose(a)[None]}
    shaped = lambda arrs: [back[t](a) if t in back else a.reshape(ws[t].shape) for t, a in enumerate(arrs)]
    return (loss.reshape(()), gx, *shaped(grads), *shaped(deltas), *shaped(new_ms), *shaped(new_vs))
```

```python
import functools

import jax
import jax.numpy as jnp
from jax import lax
from jax.experimental import pallas as pl
from jax.experimental.pallas import tpu as pltpu

F32 = jnp.float32
_MXU_DTYPE = jnp.bfloat16
MLA_GRAD_DTYPE = jnp.bfloat16

N_DEV = 8
MLA_HEADS = 8
MLA_NOPE = 64
MLA_ROPE = 32
MLA_QK = MLA_NOPE + MLA_ROPE
Q_LORA = 384
KV_LORA = 256
SWA_HEADS = 8
SWA_KV_HEADS = 2
SWA_HEAD_DIM = 64
WINDOW = 128
ROPE_THETA = 10000.0
EPS = 1e-6
MLA_SCALE = float((MLA_NOPE + MLA_ROPE) ** -0.5)
SWA_SCALE = float(SWA_HEAD_DIM ** -0.5)
LOG2E = 1.4426950408889634
MLA_QSCALE = MLA_SCALE * LOG2E
D_IN = 2464
D_IN_PAD = 2560
PAIR_ORDER = (0, 4, 1, 5, 2, 6, 3, 7)
PAIR_INV = (0, 2, 4, 6, 1, 3, 5, 7)

ADAM_LR = 0.001
ADAM_B1 = 0.9
ADAM_B2 = 0.999
ADAM_EPS = 1e-08
ADAM_WD = 0.01
ADAM_STEP = 10

LANE = 128
VMEM_LIMIT = 56 * 1024 * 1024

MESH = pl.DeviceIdType.MESH
NEG_INF = float("-inf")
SWA_SEQ_SPLIT = 2


def _mx(a):
    return a.astype(_MXU_DTYPE)


def _dot(a, b):
    return jnp.dot(a, b, preferred_element_type=F32)


def _dot_nt(a, b):
    return lax.dot_general(a, b, (((1,), (1,)), ((), ())), preferred_element_type=F32)


def _dot_tn(a, b):
    return lax.dot_general(a, b, (((0,), (0,)), ((), ())), preferred_element_type=F32)


def _cparams(sem=None):
    return pltpu.CompilerParams(dimension_semantics=sem, vmem_limit_bytes=VMEM_LIMIT)


def _vmem():
    return pl.BlockSpec(memory_space=pltpu.VMEM)


def _lane_iota(shape):
    return lax.broadcasted_iota(jnp.int32, shape, len(shape) - 1)


def _gather_program(srcs, outs, send_sems, recv_sems, local_sems):
    x, y, c = lax.axis_index("x"), lax.axis_index("y"), lax.axis_index("c")
    me, sibling = (x, y, c), (x, y, 1 - c)
    chips = [(1 - x, y), (x, 1 - y), (1 - x, 1 - y)]

    def slot(a, dev):
        return outs[a].at[4 * dev[0] + 2 * dev[1] + dev[2]]

    def copy(a, k, block, to, src=None):
        return pltpu.make_async_remote_copy(
            src_ref=slot(a, block) if src is None else src, dst_ref=slot(a, block),
            send_sem=send_sems.at[7 * a + k], recv_sem=recv_sems.at[7 * a + k],
            device_id=to, device_id_type=MESH)

    def local(a):
        return pltpu.make_async_copy(srcs[a], slot(a, me), local_sems.at[a])

    def start(a):
        local(a).start()
        cps = [copy(a, 0, me, sibling, src=srcs[a])]
        cps += [copy(a, 1 + j, me, (*chip, c), src=srcs[a]) for j, chip in enumerate(chips)]
        for cp in cps:
            cp.start()
        return cps

    def finish(group):
        cps = []
        for j, chip in enumerate(chips):
            for a in group:
                copy(a, 1 + j, (*chip, c), me).wait_recv()
                cp = copy(a, 4 + j, (*chip, c), sibling)
                cp.start()
                cps.append(cp)
        for a in group:
            copy(a, 0, sibling, me).wait_recv()
            for j, chip in enumerate(chips):
                copy(a, 4 + j, (*chip, 1 - c), me).wait_recv()
            local(a).wait()
        return cps

    return start, finish


def _gather_sems(m):
    return [pltpu.SemaphoreType.DMA((7 * m,)), pltpu.SemaphoreType.DMA((7 * m,)), pltpu.SemaphoreType.DMA((m,))]


def _all_gather(arrs, name, fused=None):
    n = len(arrs)
    cast = {a: e[1] for a, e in enumerate(arrs) if isinstance(e, tuple)}
    flip = {a for a, e in enumerate(arrs) if isinstance(e, tuple) and len(e) == 3}
    arrs = [e[0] if isinstance(e, tuple) else e for e in arrs]
    shapes = [(a.shape[-1], a.shape[0]) if i in flip else (a.shape[0], a.shape[-1]) for i, a in enumerate(arrs)]
    extra, fn, piece = fused if fused else ((), None, None)
    ne, m = len(extra), n + (1 if fused else 0)

    def body(*refs):
        ins, ex, outs = refs[:n], refs[n:n + ne], refs[n + ne:n + ne + m]
        rest = refs[n + ne + m:]
        staged = dict(zip(cast, rest))
        srcs = [staged.get(a, ins[a]) for a in range(n)] + ([rest[len(cast)]] if fused else [])
        start, finish = _gather_program(srcs, outs, *rest[-3:])
        pending = []
        for a in range(n):
            if a in staged:
                val = ins[a][:, 0, :] if len(ins[a].shape) == 3 else ins[a][...]
                staged[a][...] = (val.T if a in flip else val).astype(cast[a])
            pending += start(a)
        if fused:
            pending += finish([0])
            fn(srcs[n], outs[0], *ex)
            pending += start(n)
            pending += finish([n] + list(range(1, n)))
        else:
            pending += finish(list(range(n)))
        for cp in pending:
            cp.wait_send()

    out_shape = [jax.ShapeDtypeStruct((N_DEV,) + shapes[a], cast.get(a, arrs[a].dtype)) for a in range(n)]
    scratch = [pltpu.VMEM(shapes[a], cast[a]) for a in cast]
    if fused:
        out_shape.append(jax.ShapeDtypeStruct((N_DEV,) + piece.shape, piece.dtype))
        scratch.append(pltpu.VMEM(piece.shape, piece.dtype))
    return pl.pallas_call(
        body, name=name, out_shape=out_shape,
        in_specs=[_vmem()] * (n + ne), out_specs=[_vmem()] * m,
        scratch_shapes=scratch + _gather_sems(m),
        compiler_params=pltpu.CompilerParams(vmem_limit_bytes=VMEM_LIMIT),
    )(*arrs, *extra)


def _reduce_scatter(arrs, name, gather=()):
    n, g = len(arrs), len(gather)
    halves = [a[1] if isinstance(a, tuple) else a.astype(jnp.bfloat16) for a in arrs]
    arrs = [a[0] if isinstance(a, tuple) else a for a in arrs]

    def body(*refs):
        xs, xbs, gins = refs[:n], refs[n:2 * n], refs[2 * n:2 * n + g]
        outs, gouts = refs[2 * n + g:3 * n + g], refs[3 * n + g:3 * n + 2 * g]
        rest = refs[3 * n + 2 * g:]
        parts, recv_a, send_b, recv_b = (rest[t * n:(t + 1) * n] for t in range(4))
        send_sems, recv_sems, local_sems = rest[4 * n:4 * n + 3]
        x, y, c = lax.axis_index("x"), lax.axis_index("y"), lax.axis_index("c")
        myq = 2 * x + y
        gather_start, gather_finish = _gather_program(gins, gouts, *rest[4 * n + 3:])

        def chip(k):
            return (1 - x if k & 2 else x, 1 - y if k & 1 else y)

        def from_sibling(a):
            return pltpu.make_async_remote_copy(
                src_ref=xbs[a].at[:, 1 - c], dst_ref=recv_a[a], send_sem=send_sems.at[4 * a], recv_sem=recv_sems.at[4 * a],
                device_id=(x, y, 1 - c), device_id_type=MESH)

        def to_owner(a, k):
            qx, qy = chip(k)
            return pltpu.make_async_remote_copy(
                src_ref=send_b[a].at[2 * qx + qy], dst_ref=recv_b[a].at[myq],
                send_sem=send_sems.at[4 * a + k], recv_sem=recv_sems.at[4 * a + k],
                device_id=(qx, qy, c), device_id_type=MESH)

        mine = [pltpu.make_async_copy(xs[a].at[:, c], parts[a], local_sems.at[a]) for a in range(n)]
        first = [from_sibling(a) for a in range(n)]
        for cp in mine + first:
            cp.start()
        second = []
        for b in range(g):
            second += gather_start(b)
        for a in range(n):
            mine[a].wait()
            first[a].wait_recv()
            parts[a][...] = parts[a][...] + recv_a[a][...].astype(F32)
            send_b[a][...] = parts[a][...].astype(jnp.bfloat16)
            for k in range(1, 4):
                cp = to_owner(a, k)
                cp.start()
                second.append(cp)
        second += gather_finish(list(range(g)))
        for a in range(n):
            acc = parts[a][myq]
            for k in range(1, 4):
                to_owner(a, k).wait_recv()
                qx, qy = chip(k)
                acc = acc + recv_b[a][2 * qx + qy].astype(F32)
            outs[a][...] = acc
        for cp in first + second:
            cp.wait_send()

    quarter = lambda a, dt: pltpu.VMEM((4,) + a.shape[2:], dt)
    return pl.pallas_call(
        body, name=name,
        out_shape=[jax.ShapeDtypeStruct(a.shape[2:], F32) for a in arrs]
        + [jax.ShapeDtypeStruct((N_DEV,) + a.shape, a.dtype) for a in gather],
        in_specs=[pl.BlockSpec(memory_space=pl.ANY)] * (2 * n) + [_vmem()] * g, out_specs=[_vmem()] * (n + g),
        scratch_shapes=[quarter(a, F32) for a in arrs] + [quarter(a, jnp.bfloat16) for a in arrs] * 3
        + [pltpu.SemaphoreType.DMA((4 * n,)), pltpu.SemaphoreType.DMA((4 * n,)), pltpu.SemaphoreType.DMA((n,))]
        + _gather_sems(g),
        compiler_params=pltpu.CompilerParams(vmem_limit_bytes=VMEM_LIMIT),
    )(*arrs, *halves, *gather)


def _silu(t):
    return t * (1.0 / (1.0 + jnp.exp(-t)))


def _ada_cols(piece_ref, c_all_ref, w_ref, b_ref):
    w = _mx(w_ref[...])
    for d in range(N_DEV):
        piece_ref[d] = _dot(_mx(_silu(c_all_ref[d])), w) + b_ref[...]


def _rope_split(cd, sd):
    n = cd.shape[0]

    def expand(d):
        rep = jnp.broadcast_to(d[:, None, :], (n, 8, LANE)).reshape(8 * n, LANE)
        return pltpu.roll(rep, 0, 1, stride=16, stride_axis=0)

    c, s = expand(cd), expand(sd)
    lane = _lane_iota(c.shape)
    first = jnp.logical_and(lane >= 64, lane < 80)
    second = jnp.logical_and(lane >= 80, lane < 96)
    ck = jnp.where(first, pltpu.roll(c, 80, 1), jnp.where(second, pltpu.roll(c, 96, 1), 0.0))
    cq = jnp.where(lane < 64, 1.0, ck)
    sa = jnp.where(first, -pltpu.roll(s, 80, 1), 0.0)
    sb = jnp.where(second, pltpu.roll(s, 96, 1), 0.0)
    return cq, ck, sa, sb


def _tile_heads(t):
    return jnp.concatenate([t] * MLA_HEADS, axis=1)


def _pre_fwd(x2, shift, scale, ng, w_in, gq, gkv, w_uq, w_ukv, rope, S, ts):
    T, D = x2.shape
    nsb = S // ts
    WQ = MLA_HEADS * LANE

    def body(x_ref, sh_ref, sc_ref, ng_ref, wn_ref, gq_ref, gkv_ref, uqn_ref, ukvn_ref,
             cd_ref, sd_ref,
             hb_ref, zq_ref, zkv_ref, ql_ref, kvl_ref, q_ref, k_ref, v_ref, qs_ref, ks_ref, vs_ref, g_ref, wp_ref,
             wuq_ref, wuk_ref, wuv_ref):
        @pl.when(pl.program_id(0) == 0)
        def _():
            for h in range(MLA_HEADS):
                wuq_ref[:, LANE * h:LANE * h + MLA_QK] = uqn_ref[h]
                wuq_ref[:, LANE * h + MLA_QK:LANE * (h + 1)] = jnp.zeros((Q_LORA, LANE - MLA_QK), wuq_ref.dtype)
                wuk_ref[:, LANE * h:LANE * h + MLA_NOPE] = ukvn_ref[h, :, 0:MLA_NOPE]
                wuk_ref[:, LANE * h + MLA_NOPE:LANE * (h + 1)] = jnp.zeros((KV_LORA, LANE - MLA_NOPE), wuk_ref.dtype)
                wuv_ref[:, 64 * h:64 * (h + 1)] = ukvn_ref[h, :, MLA_NOPE:LANE]
            wp_ref[640:704, :] = jnp.zeros((64, D), wp_ref.dtype)
            wp_ref[736:768, :] = jnp.zeros((32, D), wp_ref.dtype)
            for d in range(N_DEV):
                lo, hi = W_IN_SHARD * d, W_IN_SHARD * (d + 1)
                for nat, pad, size in _w_in_row_runs():
                    a, b = max(nat, lo), min(nat + size, hi)
                    if a < b:
                        wp_ref[pad + a - nat:pad + b - nat, :] = wn_ref[d, a - lo:b - lo, :]

        w_ref = wp_ref
        x = x_ref[...]
        r1 = lax.rsqrt(jnp.mean(x * x, axis=-1, keepdims=True) + EPS)
        h = ((x * r1) * ng_ref[...]) * (1.0 + sc_ref[0]) + sh_ref[0]
        hb = _mx(h)
        hb_ref[...] = hb
        zq = _dot_nt(hb, w_ref[0:384, :])
        zq_ref[...] = zq
        rq = lax.rsqrt(jnp.mean(zq * zq, axis=-1, keepdims=True) + EPS)
        ql = _mx((zq * rq) * gq_ref[...])
        ql_ref[...] = ql
        q = _dot(ql, wuq_ref[...])
        cq, ck, sa, sb = _rope_split(cd_ref[...], sd_ref[...])
        q = (q * _tile_heads(cq) + pltpu.roll(q, WQ - 16, 1) * _tile_heads(sa)
             + pltpu.roll(q, 16, 1) * _tile_heads(sb))
        q_ref[...] = _mx(q * MLA_QSCALE)
        zkv = _dot_nt(hb, w_ref[384:640, :])
        zkv_ref[...] = zkv
        rkv = lax.rsqrt(jnp.mean(zkv * zkv, axis=-1, keepdims=True) + EPS)
        kvl = _mx((zkv * rkv) * gkv_ref[...])
        kvl_ref[...] = kvl
        kr = _dot_nt(hb, w_ref[640:768, :])
        kpe = kr * ck + pltpu.roll(kr, LANE - 16, 1) * sa + pltpu.roll(kr, 16, 1) * sb
        kf = _dot(kvl, wuk_ref[...])
        k_ref[...] = _mx(kf + jnp.concatenate([kpe] * MLA_HEADS, axis=1))
        v_ref[...] = _mx(_dot(kvl, wuv_ref[...]))
        g_ref[:, 0:512] = _dot_nt(hb, w_ref[768:1280, :])
        qs_ref[...] = _mx(_dot_nt(hb, w_ref[1280:1792, :]) * (SWA_SCALE * LOG2E))
        ks_ref[...] = _mx(_dot_nt(hb, w_ref[1792:1920, :]))
        vs_ref[...] = _mx(_dot_nt(hb, w_ref[1920:2048, :]))
        g_ref[:, 512:1024] = _dot_nt(hb, w_ref[2048:2560, :])

    row = lambda w: pl.BlockSpec((ts, w), lambda i: (i, 0))
    dense = pl.BlockSpec((ts // 8, LANE), lambda i: (i, 0))
    full = lambda a: pl.BlockSpec(a.shape, lambda i: (0,) * a.ndim)
    per_b = pl.BlockSpec((1, 1, D), lambda i: (i // nsb, 0, 0))
    out_w = [(D, _MXU_DTYPE), (384, F32), (256, F32), (384, _MXU_DTYPE), (256, _MXU_DTYPE), (WQ, _MXU_DTYPE),
             (WQ, _MXU_DTYPE), (512, _MXU_DTYPE), (512, _MXU_DTYPE), (128, _MXU_DTYPE), (128, _MXU_DTYPE), (1024, F32)]
    built = [(D_IN_PAD, D), (Q_LORA, WQ), (KV_LORA, WQ), (KV_LORA, 512)]
    return pl.pallas_call(
        body, name="pre_fwd", grid=(T // ts,),
        out_shape=[jax.ShapeDtypeStruct((T, w), dt) for w, dt in out_w]
        + [jax.ShapeDtypeStruct(s, w_in.dtype) for s in built],
        in_specs=[row(D), per_b, per_b, full(ng), full(w_in), full(gq), full(gkv), full(w_uq), full(w_ukv), dense, dense],
        out_specs=[row(w) for w, _ in out_w] + [pl.BlockSpec(s, lambda i: (0, 0)) for s in built],
        compiler_params=_cparams(("arbitrary",)),
    )(x2, shift, scale, ng, w_in, gq, gkv, w_uq, w_ukv, *rope)


def _mla_fwd(q3, k3, v3, tq, tk):
    B, S, _ = q3.shape
    nq = S // tq
    assert tq == tk
    HPS = 2
    NH = 2 * HPS

    def body(q_ref, k_ref, v_ref, o_ref, lse_ref):
        rows = lax.broadcasted_iota(jnp.int32, (tq, tk), 0)
        cols = lax.broadcasted_iota(jnp.int32, (tq, tk), 1)
        low_k = _lane_iota((tk, LANE)) < 64
        low = _lane_iota((tq, LANE)) < 64

        def step(qs, kt, carry, masked):
            r0 = kt * tk
            out = []
            for j in range(NH):
                v2 = v_ref[0, r0:r0 + tk, LANE * (j // 2):LANE * (j // 2 + 1)]
                vj = (jnp.where(low_k, v2, 1.0) if j % 2 == 0 else jnp.where(low_k, 1.0, v2)).astype(v2.dtype)
                m, acc = carry[j]
                s = _dot_nt(qs[j], k_ref[0, r0:r0 + tk, LANE * j:LANE * (j + 1)])
                if masked:
                    s = jnp.where(rows >= cols, s, NEG_INF)
                m_new = jnp.maximum(m, jnp.max(s, axis=1, keepdims=True))
                alpha = jnp.exp2(m - m_new)
                p = jnp.exp2(s - m_new)
                acc = alpha * acc + _dot(_mx(p), vj)
                out.append((m_new, acc))
            return tuple(out)

        for qi in range(nq):
            r = slice(qi * tq, (qi + 1) * tq)
            qs = [q_ref[0, r, LANE * j:LANE * (j + 1)] for j in range(NH)]
            init = (jnp.full((tq, 1), NEG_INF, F32), jnp.zeros((tq, LANE), F32))
            carry = (init,) * NH
            for kt in range(qi):
                carry = step(qs, kt, carry, False)
            carry = step(qs, qi, carry, True)
            for t in range(HPS):
                (m0, a0), (m1, a1) = carry[2 * t], carry[2 * t + 1]
                l0 = jnp.where(low, pltpu.roll(a0, 64, 1), a0)
                l1 = jnp.where(low, a1, pltpu.roll(a1, 64, 1))
                o_ref[0, r, LANE * t:LANE * (t + 1)] = jnp.where(low, a0 / l0, a1 / l1)
                lse_ref[0, 2 * t, r, :] = m0 + jnp.log2(l0)
                lse_ref[0, 2 * t + 1, r, :] = m1 + jnp.log2(l1)

    grp = lambda w: pl.BlockSpec((1, S, w), lambda b, hp: (b, 0, hp))
    return pl.pallas_call(
        body, name="mla_fwd", grid=(B, MLA_HEADS // NH),
        out_shape=[jax.ShapeDtypeStruct((B, S, 512), F32), jax.ShapeDtypeStruct((B, MLA_HEADS, S, LANE), F32)],
        in_specs=[grp(NH * LANE), grp(NH * LANE), grp(HPS * LANE)],
        out_specs=[grp(HPS * LANE), pl.BlockSpec((1, NH, S, LANE), lambda b, hp: (b, hp, 0, 0))],
        compiler_params=_cparams(("arbitrary", "arbitrary")),
    )(q3, k3, v3)


def _mla_bwd(q3, k3, v3, o3, do3, lse, tq, tk, hps=2):
    B, S, _ = q3.shape
    nq, nk = S // tq, S // tk
    nh = 2 * hps
    assert tk % tq == 0

    def body(q_ref, k_ref, v_ref, o_ref, do_ref, lse_ref, dq_ref, dk_ref, dv_ref, dkt_ref, dvt_ref):
        rows = lax.broadcasted_iota(jnp.int32, (tq, tk), 0)
        cols = lax.broadcasted_iota(jnp.int32, (tq, tk), 1)
        lane = _lane_iota((tq, LANE))

        def q_tile(qi, _):
            r = pl.ds(pl.multiple_of(qi * tq, tq), tq)
            heads = []
            for j in range(nh):
                lanes = slice(LANE * j, LANE * (j + 1))
                pair = slice(LANE * (j // 2), LANE * (j // 2 + 1))
                do = jnp.where((lane < 64) if j % 2 == 0 else (lane >= 64), do_ref[0, r, pair], 0.0)
                q = q_ref[0, r, lanes]
                heads.append((lanes, pair, q, _mx(q.astype(F32).T), _mx(do), _mx(do.T),
                              jnp.sum(do * o_ref[0, r, pair], axis=1, keepdims=True),
                              jnp.concatenate([lse_ref[0, j, r, :]] * (tk // LANE), axis=1)))
            n_full = (qi * tq) // tk

            def k_tile(kt, dqs, masked):
                kr = pl.ds(pl.multiple_of(kt * tk, tk), tk)
                first = masked and (qi * tq) % tk == 0
                out = []
                dvt = [None] * hps
                for j, (lanes, pair, q, qt, dob, dot_, dcol, lse_c) in enumerate(heads):
                    k = k_ref[0, kr, lanes]
                    s = _dot_nt(q, k)
                    if masked:
                        s = jnp.where(rows + qi * tq >= cols + kt * tk, s, NEG_INF)
                    p = jnp.exp2(s - lse_c)
                    dp = _dot_nt(dob, v_ref[0, kr, pair])
                    dsb = _mx(p * (dp - dcol))
                    if first:
                        dkt_ref[j, kt] = _dot(qt, dsb)
                    else:
                        dkt_ref[j, kt] += _dot(qt, dsb)
                    pv = _dot(dot_, _mx(p))
                    dvt[j // 2] = pv if dvt[j // 2] is None else dvt[j // 2] + pv
                    out.append(dqs[j] + _dot(dsb, k))
                for t in range(hps):
                    if first:
                        dvt_ref[t, kt] = dvt[t]
                    else:
                        dvt_ref[t, kt] += dvt[t]
                return tuple(out)

            dqs = (jnp.zeros((tq, LANE), F32),) * nh
            for kt in range(n_full):
                dqs = k_tile(kt, dqs, False)
            dqs = k_tile(n_full, dqs, True)
            for j in range(nh):
                dq_ref[0, r, heads[j][0]] = (MLA_SCALE * dqs[j]).astype(dq_ref.dtype)
            return 0

        for qi in range(nq):
            q_tile(qi, 0)

        def flush(kt, _):
            kr = pl.ds(pl.multiple_of(kt * tk, tk), tk)
            for j in range(nh):
                dk_ref[0, kr, LANE * j:LANE * (j + 1)] = ((1.0 / LOG2E) * dkt_ref[j, kt].T).astype(dk_ref.dtype)
            for t in range(hps):
                dv_ref[0, kr, LANE * t:LANE * (t + 1)] = dvt_ref[t, kt].T.astype(dv_ref.dtype)
            return 0

        lax.fori_loop(0, nk, flush, 0)

    grp = lambda w: pl.BlockSpec((1, S, w), lambda b, hp: (b, 0, hp))
    return pl.pallas_call(
        body, name="mla_bwd", grid=(B, MLA_HEADS // nh),
        out_shape=[jax.ShapeDtypeStruct((B, S, 1024), MLA_GRAD_DTYPE), jax.ShapeDtypeStruct((B, S, 1024), MLA_GRAD_DTYPE),
                   jax.ShapeDtypeStruct((B, S, 512), MLA_GRAD_DTYPE)],
        in_specs=[grp(nh * LANE), grp(nh * LANE), grp(hps * LANE), grp(hps * LANE), grp(hps * LANE),
                  pl.BlockSpec((1, nh, S, LANE), lambda b, hp: (b, hp, 0, 0))],
        out_specs=[grp(nh * LANE), grp(nh * LANE), grp(hps * LANE)],
        scratch_shapes=[pltpu.VMEM((nh, nk, LANE, tk), F32), pltpu.VMEM((hps, nk, LANE, tk), F32)],
        compiler_params=_cparams(("arbitrary", "arbitrary")),
    )(q3, k3, v3, o3, do3, lse)


def _swa_consts(sink_ref):
    W = WINDOW
    row = lax.broadcasted_iota(jnp.int32, (4 * W, LANE), 0)
    out = []
    for g in range(SWA_KV_HEADS):
        slope = jnp.zeros((4 * W, LANE), F32)
        sink = jnp.zeros((4 * W, LANE), F32)
        for p in range(4):
            h = p + 4 * g
            here = jnp.logical_and(row >= W * p, row < W * (p + 1))
            slope = jnp.where(here, float(LOG2E * 2.0 ** (-8.0 * (h + 1) / SWA_HEADS)), slope)
            sink = jnp.where(here, LOG2E * sink_ref[:, h:h + 1], sink)
        out.append((jnp.concatenate([slope, slope], axis=1), sink))
    return out


def _wide(col):
    return jnp.concatenate([col, col], axis=1)


def _swa_block(q_ref, k_ref, v_ref, pr_ref, n, i, ij):
    W = WINDOW
    kb = jnp.maximum(n - 1, 0)
    r = pl.ds(pl.multiple_of(i * W, W), W)
    kr = pl.ds(pl.multiple_of(kb * W, W), 2 * W)
    q4, k2, v2 = q_ref[0, r, :], k_ref[0, kr, :], v_ref[0, kr, :]
    pq = _wide(jnp.broadcast_to(pr_ref[0, pl.ds(n, 1), :], (W, W)).T)
    pk = jnp.concatenate([pr_ref[0, pl.ds(kb, 1), :], pr_ref[0, pl.ds(kb + 1, 1), :]], axis=1)
    rel = ij + (n - kb) * W
    dist = jnp.where(jnp.logical_and(rel >= 0, rel < W), pq - pk, float("inf"))
    return r, kb, kr, q4, k2, v2, jnp.concatenate([dist] * 4, axis=0)


def _swa_stack(x4, g, dtype):
    lane = _lane_iota((WINDOW, LANE))
    mine = (lane < 64) if g == 0 else (lane >= 64)
    return jnp.concatenate([jnp.where(mine, x4[:, LANE * p:LANE * (p + 1)], 0).astype(dtype) for p in range(4)], axis=0)


def _swa_unstack(ref, r, lo, hi, scale=None):
    W = WINDOW
    low = _lane_iota((W, LANE)) < 64
    for p in range(4):
        t = jnp.where(low, lo[W * p:W * (p + 1)], hi[W * p:W * (p + 1)])
        ref[0, r, LANE * p:LANE * (p + 1)] = t if scale is None else scale * t


def _swa_fwd(qs3, ks3, vs3, posr, sinks):
    B, S, _ = qs3.shape
    W = WINDOW
    nb = S // W
    nh = SWA_SEQ_SPLIT
    nbh = nb // nh

    def body(q_ref, k_ref, v_ref, pr_ref, sink_ref, o_ref, lse_ref):
        ij = lax.broadcasted_iota(jnp.int32, (W, 2 * W), 0) - lax.broadcasted_iota(jnp.int32, (W, 2 * W), 1)
        consts = _swa_consts(sink_ref)
        n0 = pl.program_id(1) * nbh

        def blk(i, _):
            n = n0 + i
            r, _, _, q4, k2, v2, dist4 = _swa_block(q_ref, k_ref, v_ref, pr_ref, n, i, ij)
            o_g = []
            for g, (slope, sink) in enumerate(consts):
                s = _dot_nt(_swa_stack(q4, g, q4.dtype), k2) - slope * dist4
                m = jnp.maximum(jnp.max(s, axis=1, keepdims=True), sink)
                e = jnp.exp2(s - _wide(m))
                l = jnp.sum(e, axis=1, keepdims=True) + jnp.exp2(sink - m)
                o_g.append(_dot(_mx(e), v2) * (1.0 / l))
                lse_ref[0, i, g] = m + jnp.log2(l)
            _swa_unstack(o_ref, r, o_g[0], o_g[1])
            return 0

        for i in range(nbh):
            blk(i, 0)

    seq = lambda w: pl.BlockSpec((1, S, w), lambda b, h: (b, 0, 0))
    part = lambda w: pl.BlockSpec((1, S // nh, w), lambda b, h: (b, h, 0))
    lse_spec = pl.BlockSpec((1, nbh, 2, 4 * W, LANE), lambda b, h: (b, h, 0, 0, 0))
    return pl.pallas_call(
        body, name="swa_fwd", grid=(B, nh),
        out_shape=[jax.ShapeDtypeStruct((B, S, 512), F32), jax.ShapeDtypeStruct((B, nb, 2, 4 * W, LANE), F32)],
        in_specs=[part(512), seq(LANE), seq(LANE), pl.BlockSpec((1, nb, W), lambda b, h: (b, 0, 0)),
                  pl.BlockSpec((1, LANE), lambda b, h: (0, 0))],
        out_specs=[part(512), lse_spec],
        compiler_params=_cparams(("arbitrary", "arbitrary")),
    )(qs3, ks3, vs3, posr, sinks)


def _swa_bwd(qs3, ks3, vs3, posr, sinks, os3, do3, lse):
    B, S, _ = qs3.shape
    W = WINDOW
    nb = S // W
    nh = SWA_SEQ_SPLIT
    nbh = nb // nh
    assert nbh % 2 == 0 and nbh * nh * W == S

    def body(q_ref, k_ref, v_ref, pr_ref, sink_ref, o_ref, do_ref, lse_ref, dq_ref, dk_ref, dv_ref, dsink_ref,
             dkt_ref, dvt_ref):
        lane1 = _lane_iota((1, LANE))
        ij = lax.broadcasted_iota(jnp.int32, (W, 2 * W), 0) - lax.broadcasted_iota(jnp.int32, (W, 2 * W), 1)
        consts = _swa_consts(sink_ref)
        hh = pl.program_id(1)
        n0 = hh * nbh

        @pl.when(hh == 0)
        def _():
            dkt_ref[...] = jnp.zeros_like(dkt_ref)
            dvt_ref[...] = jnp.zeros_like(dvt_ref)

        @pl.when(jnp.logical_and(pl.program_id(0) == 0, hh == 0))
        def _():
            dsink_ref[...] = jnp.zeros_like(dsink_ref)

        def blk(i, dsink):
            n = n0 + i
            r, kb, _, q4, k2, v2, dist4 = _swa_block(q_ref, k_ref, v_ref, pr_ref, n, i, ij)
            o4, do4 = o_ref[0, r, :], do_ref[0, r, :]
            dq_g = []
            dkt = jnp.zeros((LANE, 2 * W), F32)
            dvt = jnp.zeros((LANE, 2 * W), F32)
            for g, (slope, sink) in enumerate(consts):
                q_st = _swa_stack(q4, g, F32)
                do_st = _swa_stack(do4, g, F32)
                dcol = jnp.sum(do_st * _swa_stack(o4, g, F32), axis=1, keepdims=True)
                qst, dob = _mx(q_st), _mx(do_st)
                lse_c = lse_ref[0, i, g]
                pr = jnp.exp2(_dot_nt(qst, k2) - slope * dist4 - _wide(lse_c))
                dsb = _mx(pr * (_dot_nt(dob, v2) - dcol))
                psd = jnp.exp2(sink - lse_c)[:, 0:1] * dcol
                for p in range(4):
                    dsink = dsink - jnp.where(lane1 == p + 4 * g,
                                              jnp.sum(psd[W * p:W * (p + 1)], axis=0, keepdims=True), 0.0)
                dq_g.append(_dot(dsb, k2))
                dkt = dkt + _dot(_mx(q_st.T), dsb)
                dvt = dvt + _dot(_mx(do_st.T), _mx(pr))
            _swa_unstack(dq_ref, r, dq_g[0], dq_g[1], SWA_SCALE)
            dkt_ref[kb] += dkt[:, 0:W]
            dkt_ref[kb + 1] += dkt[:, W:2 * W]
            dvt_ref[kb] += dvt[:, 0:W]
            dvt_ref[kb + 1] += dvt[:, W:2 * W]
            return dsink

        dsink = jnp.zeros((1, LANE), F32)
        for i in range(nbh):
            dsink = blk(i, dsink)
        dsink_ref[...] += dsink

        @pl.when(hh == nh - 1)
        def _():
            def flush(n, _):
                r = pl.ds(pl.multiple_of(n * W, W), W)
                dk_ref[0, r, :] = (1.0 / LOG2E) * dkt_ref[n].T
                dv_ref[0, r, :] = dvt_ref[n].T
                return 0

            lax.fori_loop(0, nb, flush, 0)

    seq = lambda w: pl.BlockSpec((1, S, w), lambda b, h: (b, 0, 0))
    part = lambda w: pl.BlockSpec((1, S // nh, w), lambda b, h: (b, h, 0))
    return pl.pallas_call(
        body, name="swa_bwd", grid=(B, nh),
        out_shape=[jax.ShapeDtypeStruct((B, S, 512), F32), jax.ShapeDtypeStruct((B, S, LANE), F32),
                   jax.ShapeDtypeStruct((B, S, LANE), F32), jax.ShapeDtypeStruct((1, LANE), F32)],
        in_specs=[part(512), seq(LANE), seq(LANE), pl.BlockSpec((1, nb, W), lambda b, h: (b, 0, 0)),
                  pl.BlockSpec((1, LANE), lambda b, h: (0, 0)), part(512),
                  pl.BlockSpec((1, S // nh, 512), lambda b, h: (b, h, 1)),
                  pl.BlockSpec((1, nbh, 2, 4 * W, LANE), lambda b, h: (b, h, 0, 0, 0))],
        out_specs=[part(512), seq(LANE), seq(LANE), pl.BlockSpec((1, LANE), lambda b, h: (0, 0))],
        scratch_shapes=[pltpu.VMEM((nb, LANE, W), F32), pltpu.VMEM((nb, LANE, W), F32)],
        compiler_params=_cparams(("arbitrary", "arbitrary")),
    )(qs3, ks3, vs3, posr, sinks, os3, do3, lse)


def _post(om, osw, g, w_out, x2, gate, fg, tgt, S, ts):
    T, D = x2.shape
    nsb = S // ts

    def body(om_ref, os_ref, g_ref, w_ref, x_ref, gate_ref, fg_ref, t_ref,
             dx2_ref, do_ref, dg_ref, loss_ref, dfg_ref, dgate_ref, dw_ref, dwb_ref):
        i = pl.program_id(0)
        gv = g_ref[...]
        sg = 1.0 / (1.0 + jnp.exp(-gv))
        silu = gv * sg
        o = jnp.concatenate([om_ref[...], os_ref[...]], axis=1)
        ab = _mx(o * silu)
        y = _dot(ab, w_ref[...])
        gate = gate_ref[0]
        xo = x_ref[...] + gate * y
        r2 = lax.rsqrt(jnp.mean(xo * xo, axis=-1, keepdims=True) + EPS)
        xh = xo * r2
        fg = fg_ref[...]
        diff = xh * fg - t_ref[...]
        sq = jnp.sum(diff * diff, axis=0, keepdims=True)
        part = sq[:, 0:LANE]
        for t in range(1, D // LANE):
            part = part + sq[:, LANE * t:LANE * (t + 1)]
        dout = diff * (1.0 / D)
        dxh = dout * fg
        dx2 = r2 * (dxh - xh * jnp.mean(dxh * xh, axis=-1, keepdims=True))
        dx2_ref[...] = dx2
        dyb = _mx(dx2 * gate)
        da = _dot_nt(dyb, w_ref[...])
        do_ref[...] = da * silu
        dg_ref[...] = _mx(da * o * (sg * (1.0 + gv * (1.0 - sg))))

        @pl.when(i == 0)
        def _():
            loss_ref[...] = jnp.zeros_like(loss_ref)
            dfg_ref[...] = jnp.zeros_like(dfg_ref)
            dw_ref[...] = jnp.zeros_like(dw_ref)

        @pl.when(i % nsb == 0)
        def _():
            dgate_ref[...] = jnp.zeros_like(dgate_ref)

        loss_ref[...] += (0.5 / D) * part
        dfg_ref[...] += jnp.sum(dout * xh, axis=0, keepdims=True)
        dgate_ref[0] += jnp.sum(dx2 * y, axis=0, keepdims=True)
        dw_ref[...] += _dot_tn(ab, dyb)

        @pl.when(i == T // ts - 1)
        def _():
            chunks = [dw_ref[512 + 64 * PAIR_INV[h]:512 + 64 * (PAIR_INV[h] + 1), :] for h in range(SWA_HEADS)]
            for h in range(SWA_HEADS):
                dw_ref[512 + 64 * h:512 + 64 * (h + 1), :] = chunks[h]
            dwb_ref[...] = dw_ref[...].astype(jnp.bfloat16)

    row = lambda w: pl.BlockSpec((ts, w), lambda i: (i, 0))
    full = lambda a: pl.BlockSpec(a.shape, lambda i: (0,) * a.ndim)
    per_b = pl.BlockSpec((1, 1, D), lambda i: (i // nsb, 0, 0))
    return pl.pallas_call(
        body, name="post", grid=(T // ts,),
        out_shape=[jax.ShapeDtypeStruct((T, D), F32), jax.ShapeDtypeStruct((T, 1024), F32),
                   jax.ShapeDtypeStruct((T, 1024), _MXU_DTYPE), jax.ShapeDtypeStruct((1, LANE), F32),
                   jax.ShapeDtypeStruct((1, D), F32), jax.ShapeDtypeStruct(gate.shape, F32),
                   jax.ShapeDtypeStruct(w_out.shape, F32), jax.ShapeDtypeStruct(w_out.shape, jnp.bfloat16)],
        in_specs=[row(512), row(512), row(1024), full(w_out), row(D), per_b, full(fg), row(D)],
        out_specs=[row(D), row(1024), row(1024),
                   pl.BlockSpec((1, LANE), lambda i: (0, 0)), pl.BlockSpec((1, D), lambda i: (0, 0)), per_b,
                   full(w_out), full(w_out)],
        compiler_params=_cparams(("arbitrary",)),
    )(om, osw, g, w_out, x2, gate, fg, tgt)


def _pre_bwd(dq, dk, dv, dqs, dks, dvs, dg, zq, zkv, ql, kvl, x2, dx2, scale, ng, w_in, gq, gkv, w_uq, w_uk, w_uv,
             rope, S, ts):
    T, D = x2.shape
    nsb = S // ts
    WQ = MLA_HEADS * LANE

    def body(dq_ref, dk_ref, dv_ref, dqs_ref, dks_ref, dvs_ref, dg_ref, zq_ref, zkv_ref, ql_ref, kvl_ref, x_ref, dx2_ref,
             sc_ref,
             ng_ref, w_ref, gq_ref, gkv_ref, wuq_ref, wuk_ref, wuv_ref, cd_ref, sd_ref,
             gx_ref, dz_ref, dgq_ref, dgkv_ref, dng_ref, dsh_ref, dsc_ref, uq_ref, uqb_ref, ukv_ref, ukvb_ref,
             dwuq_ref, dwuk_ref, dwuv_ref):
        i = pl.program_id(0)

        @pl.when(i == 0)
        def _():
            dgq_ref[...] = jnp.zeros_like(dgq_ref)
            dgkv_ref[...] = jnp.zeros_like(dgkv_ref)
            dng_ref[...] = jnp.zeros_like(dng_ref)
            dwuq_ref[...] = jnp.zeros_like(dwuq_ref)
            dwuk_ref[...] = jnp.zeros_like(dwuk_ref)
            dwuv_ref[...] = jnp.zeros_like(dwuv_ref)

        @pl.when(i % nsb == 0)
        def _():
            dsh_ref[...] = jnp.zeros_like(dsh_ref)
            dsc_ref[...] = jnp.zeros_like(dsc_ref)

        def norm_bwd(z, dy, gain):
            r = lax.rsqrt(jnp.mean(z * z, axis=-1, keepdims=True) + EPS)
            zh = z * r
            dzh = dy * gain
            return r * (dzh - zh * jnp.mean(dzh * zh, axis=-1, keepdims=True)), jnp.sum(dy * zh, axis=0, keepdims=True)

        tables = _rope_split(cd_ref[...], sd_ref[...])
        ng = ng_ref[...]
        sc1 = 1.0 + sc_ref[0]

        def rows_chain(rs):
            cq, ck, sa, sb = (t[rs] for t in tables)
            dqr = dq_ref[rs, :].astype(F32)
            dqb = _mx(dqr * _tile_heads(cq) + pltpu.roll(dqr * _tile_heads(sa), 16, 1)
                      + pltpu.roll(dqr * _tile_heads(sb), WQ - 16, 1))
            p_uq = _dot_tn(ql_ref[rs, :], dqb)
            dzq, dgq = norm_bwd(zq_ref[rs, :], _dot_nt(dqb, wuq_ref[...]), gq_ref[...])
            dkr = dk_ref[rs, :].astype(F32)
            dkb = _mx(dkr)
            p_uk = _dot_tn(kvl_ref[rs, :], dkb)
            dvb = _mx(dv_ref[rs, :])
            p_uv = _dot_tn(kvl_ref[rs, :], dvb)
            dzkv, dgkv = norm_bwd(zkv_ref[rs, :], _dot_nt(dkb, wuk_ref[...]) + _dot_nt(dvb, wuv_ref[...]), gkv_ref[...])
            dkpe = dkr[:, 0:LANE]
            for h in range(1, MLA_HEADS):
                dkpe = dkpe + dkr[:, LANE * h:LANE * (h + 1)]
            dkro = dkpe * ck + pltpu.roll(dkpe * sa, 16, 1) + pltpu.roll(dkpe * sb, LANE - 16, 1)
            dz_ref[rs, 0:384] = _mx(dzq)
            dz_ref[rs, 384:640] = _mx(dzkv)
            dz_ref[rs, 640:768] = _mx(dkro)
            dz_ref[rs, 768:1280] = dg_ref[rs, 0:512]
            dz_ref[rs, 1280:1792] = _mx(dqs_ref[rs, :])
            dz_ref[rs, 1792:1920] = _mx(dks_ref[rs, :])
            dz_ref[rs, 1920:2048] = _mx(dvs_ref[rs, :])
            dz_ref[rs, 2048:2560] = dg_ref[rs, 512:1024]
            dh = _dot(dz_ref[rs, :], w_ref[...])
            x = x_ref[rs, :]
            r1 = lax.rsqrt(jnp.mean(x * x, axis=-1, keepdims=True) + EPS)
            xn = x * r1
            dxn = dh * ng * sc1
            gx_ref[rs, :] = dx2_ref[rs, :] + r1 * (dxn - xn * jnp.mean(dxn * xn, axis=-1, keepdims=True))
            return (p_uq, p_uk, p_uv, dgq, dgkv, jnp.sum(dh, axis=0, keepdims=True),
                    jnp.sum(dh * (xn * ng), axis=0, keepdims=True), jnp.sum(dh * xn * sc1, axis=0, keepdims=True))

        hr = ts // 2
        parts = [rows_chain(slice(hr * t, hr * (t + 1))) for t in range(2)]
        p_uq, p_uk, p_uv, dgq, dgkv, dsh, dsc, dng = (a + b for a, b in zip(*parts))
        dwuq_ref[...] += p_uq
        dwuk_ref[...] += p_uk
        dwuv_ref[...] += p_uv
        dgq_ref[...] += dgq
        dgkv_ref[...] += dgkv
        dsh_ref[0] += dsh
        dsc_ref[0] += dsc
        dng_ref[...] += dng

        @pl.when(i == T // ts - 1)
        def _():
            for h in range(MLA_HEADS):
                uq = dwuq_ref[:, LANE * h:LANE * h + MLA_QK]
                ukv = jnp.concatenate([dwuk_ref[:, LANE * h:LANE * h + MLA_NOPE], dwuv_ref[:, 64 * h:64 * (h + 1)]], axis=1)
                uq_ref[h] = uq
                uqb_ref[h] = uq.astype(jnp.bfloat16)
                ukv_ref[h] = ukv
                ukvb_ref[h] = ukv.astype(jnp.bfloat16)

    row = lambda w: pl.BlockSpec((ts, w), lambda i: (i, 0))
    full = lambda a: pl.BlockSpec(a.shape, lambda i: (0,) * a.ndim, pipeline_mode=pl.Buffered(1))
    slot = lambda r, c, dt: (jax.ShapeDtypeStruct((MLA_HEADS, r, c), dt), pl.BlockSpec((MLA_HEADS, r, c), lambda i: (0, 0, 0)))
    slots = [slot(Q_LORA, MLA_QK, F32), slot(Q_LORA, MLA_QK, jnp.bfloat16), slot(KV_LORA, LANE, F32), slot(KV_LORA, LANE, jnp.bfloat16)]
    per_b = pl.BlockSpec((1, 1, D), lambda i: (i // nsb, 0, 0))
    dense = pl.BlockSpec((ts // 8, LANE), lambda i: (i, 0))
    vec = lambda w: pl.BlockSpec((1, w), lambda i: (0, 0))
    return pl.pallas_call(
        body, name="pre_bwd", grid=(T // ts,),
        out_shape=[jax.ShapeDtypeStruct((T, D), F32), jax.ShapeDtypeStruct((T, D_IN_PAD), _MXU_DTYPE), jax.ShapeDtypeStruct((1, 384), F32),
                   jax.ShapeDtypeStruct((1, 256), F32), jax.ShapeDtypeStruct((1, D), F32),
                   jax.ShapeDtypeStruct(scale.shape, F32), jax.ShapeDtypeStruct(scale.shape, F32)] + [t[0] for t in slots],
        in_specs=[row(WQ), row(WQ), row(512), row(512), row(LANE), row(LANE), row(1024), row(384), row(256), row(384),
                  row(256), row(D), row(D), per_b, full(ng), full(w_in), full(gq), full(gkv), full(w_uq), full(w_uk), full(w_uv),
                  dense, dense],
        out_specs=[row(D), row(D_IN_PAD), vec(384), vec(256), vec(D), per_b, per_b] + [t[1] for t in slots],
        scratch_shapes=[pltpu.VMEM(w_uq.shape, F32), pltpu.VMEM(w_uk.shape, F32), pltpu.VMEM(w_uv.shape, F32)],
        compiler_params=_cparams(("arbitrary",)),
    )(dq, dk, dv, dqs, dks, dvs, dg, zq, zkv, ql, kvl, x2, dx2, scale, ng, w_in, gq, gkv, w_uq, w_uk, w_uv, *rope)


def _w_in_row_runs():
    runs = [(0, 0, 640), (640, 704, 32), (672, 768, 512)]
    runs += [(1184 + 64 * h, 1280 + 64 * PAIR_INV[h], 64) for h in range(8)]
    runs += [(1696, 1792, 256)]
    runs += [(1952 + 64 * h, 2048 + 64 * PAIR_INV[h], 64) for h in range(8)]
    return runs


W_IN_SHARD = D_IN // N_DEV
W_IN_SLOT = 320


def _dw_in_t(dz, hb, tn, tk):
    T, M = dz.shape
    N = hb.shape[1]
    nk = T // tk

    def body(a_ref, b_ref, o_ref, ob_ref, acc_ref):
        k = pl.program_id(1)

        @pl.when(k == 0)
        def _():
            acc_ref[...] = jnp.zeros_like(acc_ref)

        acc_ref[...] += _dot_tn(a_ref[...], b_ref[...])

        @pl.when(k == nk - 1)
        def _():
            for d in range(N_DEV):
                lo, hi = W_IN_SHARD * d, W_IN_SHARD * (d + 1)
                for nat, pad, size in _w_in_row_runs():
                    a, b = max(nat, lo), min(nat + size, hi)
                    if a < b:
                        piece = acc_ref[pad + a - nat:pad + b - nat, :]
                        o_ref[d, a - lo:b - lo, :] = piece
                        ob_ref[d, a - lo:b - lo, :] = piece.astype(ob_ref.dtype)
                o_ref[d, W_IN_SHARD:W_IN_SLOT, :] = jnp.zeros((W_IN_SLOT - W_IN_SHARD, tn), F32)
                ob_ref[d, W_IN_SHARD:W_IN_SLOT, :] = jnp.zeros((W_IN_SLOT - W_IN_SHARD, tn), ob_ref.dtype)

    slots = pl.BlockSpec((N_DEV, W_IN_SLOT, tn), lambda j, k: (0, 0, j))
    return pl.pallas_call(
        body, name="dw_in", grid=(N // tn, nk),
        out_shape=[jax.ShapeDtypeStruct((N_DEV, W_IN_SLOT, N), F32), jax.ShapeDtypeStruct((N_DEV, W_IN_SLOT, N), jnp.bfloat16)],
        in_specs=[pl.BlockSpec((tk, M), lambda j, k: (k, 0)), pl.BlockSpec((tk, tn), lambda j, k: (k, j))],
        out_specs=[slots, slots],
        scratch_shapes=[pltpu.VMEM((M, tn), F32)],
        compiler_params=_cparams(("arbitrary", "arbitrary")),
    )(dz, hb)


SHARDED = (3, 6, 7, 9)
PART_SLICES = {10: (128, 1152), 2: (1152, 2176), 4: (2176, 2560), 5: (2560, 2816), 8: (2816, 2824)}


def _finalize_adamw(parts_all, dmod_all, dmod_cols, c_all, ws, sharded_grads, ms, vs):
    n = len(ws)
    local = [t for t in range(n) if t not in SHARDED[2:]]

    def body(*refs):
        p_ref, dm_ref, dmc_ref, c_ref = refs[:4]
        w_refs, gs_refs, m_refs, v_refs = refs[4:4 + n], refs[4 + n:8 + n], refs[8 + n:8 + 2 * n], refs[8 + 2 * n:8 + 3 * n]
        outs = refs[8 + 3 * n:]
        loss_ref, gl_refs = outs[0], outs[1:1 + len(local)]
        d_refs, nm_refs, nv_refs = (outs[1 + len(local) + t * n:1 + len(local) + (t + 1) * n] for t in range(3))
        ps_ref = outs[-1]
        rd = lambda r: r[:, 0, :] if len(r.shape) == 3 else r[...]

        def wr(r, val):
            if len(r.shape) == 3:
                r[:, 0, :] = val
            else:
                r[...] = val

        acc = p_ref[0]
        for j in range(1, N_DEV):
            acc = acc + p_ref[j]
        ps_ref[...] = acc
        loss_ref[...] = jnp.sum(acc[:, 0:LANE], axis=1, keepdims=True)
        db = dm_ref[0:1, :]
        for j in range(1, dm_ref.shape[0]):
            db = db + dm_ref[j:j + 1, :]
        grads = {0: _dot_tn(_mx(_silu(c_ref[...])), _mx(dmc_ref[...])), 1: db}
        for t, (lo, hi) in PART_SLICES.items():
            grads[t] = ps_ref[:, lo:hi]
        for i, t in enumerate(SHARDED):
            grads[t] = gs_refs[i][...]
        grads[SHARDED[0]] = gs_refs[0][0:W_IN_SHARD, :]
        grads[SHARDED[1]] = gs_refs[1][...].T
        for i, t in enumerate(local):
            wr(gl_refs[i], grads[t])
        for t in range(n):
            gv = grads[t]
            nm = ADAM_B1 * rd(m_refs[t]) + (1.0 - ADAM_B1) * gv
            nv = ADAM_B2 * rd(v_refs[t]) + (1.0 - ADAM_B2) * (gv * gv)
            m_hat = nm / (1.0 - ADAM_B1 ** ADAM_STEP)
            v_hat = nv / (1.0 - ADAM_B2 ** ADAM_STEP)
            wr(d_refs[t], -ADAM_LR * (m_hat / (jnp.sqrt(v_hat) + ADAM_EPS) + ADAM_WD * rd(w_refs[t])))
            wr(nm_refs[t], nm)
            wr(nv_refs[t], nv)

    like = lambda arrs: [jax.ShapeDtypeStruct(a.shape, F32) for a in arrs]
    out = pl.pallas_call(
        body, name="finalize_adamw",
        out_shape=[jax.ShapeDtypeStruct((1, 1), F32)] + like([ws[t] for t in local]) + like(ws) * 3,
        in_specs=[_vmem()] * (8 + 3 * n), out_specs=[_vmem()] * (1 + len(local) + 3 * n),
        scratch_shapes=[pltpu.VMEM(parts_all.shape[1:], F32)],
        compiler_params=_cparams(),
    )(parts_all, dmod_all, dmod_cols, c_all, *ws, *sharded_grads, *ms, *vs)
    k = 1 + len(local)
    return out[0], dict(zip(local, out[1:k])), out[k:k + n], out[k + n:k + 2 * n], out[k + 2 * n:]


def _pair_perm(a, axis, order):
    a = jnp.moveaxis(a, axis, -1)
    lead = a.shape[:-1]
    a = a.reshape(lead + (8, 64))[..., list(order), :].reshape(lead + (512,))
    return jnp.moveaxis(a, -1, axis)


def _rope_table(positions):
    T = positions.size
    inv = ROPE_THETA ** (-jnp.arange(0, MLA_ROPE, 2, dtype=F32) / MLA_ROPE)
    pos = jnp.repeat(positions.reshape(T // 8, 8)[:, ::-1].astype(F32), MLA_ROPE // 2, axis=1)
    ang = pos * jnp.tile(inv, 8)[None, :]
    return jnp.cos(ang), jnp.sin(ang)


def _local_step(x, mod, positions, ng, w_in_t, gq, gkv, w_uq, w_ukv, sinks, w_out, fg, tgt,
                ts=512, fq=512, fk=512, bq=512, bk=512):
    B, S, D = x.shape
    T = B * S
    x2 = x.reshape(T, D)
    shift, scale, gate = (mod[:, None, k * D:(k + 1) * D] for k in range(3))
    w_out_p = jnp.concatenate([w_out[:512], _pair_perm(w_out[512:], 0, PAIR_ORDER)], axis=0)
    rope = _rope_table(positions)
    posf = positions.astype(F32)
    posr = posf.reshape(B, S // WINDOW, WINDOW)
    sinks_l = jnp.pad(sinks.reshape(1, SWA_HEADS), ((0, 0), (0, LANE - SWA_HEADS)))

    (hb, zq, zkv, ql, kvl, q, k, v, qs, ks, vs, g, w_in_p, w_uq_p, w_uk_p, w_uv) = _pre_fwd(
        x2, shift, scale, ng, w_in_t, gq, gkv, w_uq, w_ukv, rope, S, ts)
    r3 = lambda a: a.reshape(B, S, a.shape[-1])
    om, lse_m = _mla_fwd(r3(q), r3(k), r3(v), fq, fk)
    osw, lse_s = _swa_fwd(r3(qs), r3(ks), r3(vs), posr, sinks_l)
    dx2, do, dg, loss_v, dfg, dgate, *dw_out = _post(
        om.reshape(T, 512), osw.reshape(T, 512), g, w_out_p, x2, gate, fg.reshape(1, D), tgt.reshape(T, D), S, ts)
    do3 = r3(do)
    dq, dk, dv = _mla_bwd(r3(q), r3(k), r3(v), om, do3, lse_m, bq, bk)
    dqs, dks, dvs, dsink = _swa_bwd(r3(qs), r3(ks), r3(vs), posr, sinks_l, osw, do3, lse_s)
    f2 = lambda a: a.reshape(T, a.shape[-1])
    gx, dz, dgq, dgkv, dng, dsh, dsc, *dw_u = _pre_bwd(
        f2(dq), f2(dk), f2(dv), f2(dqs), f2(dks), f2(dvs), dg, zq, zkv, ql, kvl, x2, dx2, scale, ng, w_in_p, gq, gkv,
        w_uq_p, w_uk_p, w_uv, rope, S, ts)
    tk = min(T, 1024)
    dw_in_t = _dw_in_t(dz, hb, 512, tk)
    parts = jnp.concatenate([loss_v, dfg, dng, dgq, dgkv, dsink], axis=1)
    dmod = jnp.concatenate([dsh, dsc, dgate], axis=2).reshape(B, 3 * D)
    return gx.reshape(B, S, D), dw_in_t, tuple(dw_u[0:2]), tuple(dw_u[2:4]), tuple(dw_out), parts, dmod


def kernel(x, c, positions, w_ada, b_ada, norm_gain, w_in, q_norm_gain, kv_norm_gain, w_uq, w_ukv, swa_sinks, w_out, final_gain, loss_target, m_w_ada, m_b_ada, m_norm_gain, m_w_in, m_q_norm_gain, m_kv_norm_gain, m_w_uq, m_w_ukv, m_swa_sinks, m_w_out, m_final_gain, v_w_ada, v_b_ada, v_norm_gain, v_w_in, v_q_norm_gain, v_kv_norm_gain, v_w_uq, v_w_ukv, v_swa_sinks, v_w_out, v_final_gain):
    B, S, D = x.shape
    me = 4 * lax.axis_index("x") + 2 * lax.axis_index("y") + lax.axis_index("c")
    bf = _MXU_DTYPE

    ncol = w_ada.shape[2]
    b_cols = lax.dynamic_slice_in_dim(b_ada, me * ncol, ncol, axis=1)
    c_all, win_g, wuq_g, wukv_g, wout_g, mod_g = _all_gather(
        [c, (jnp.transpose(w_in, (2, 0, 1)), bf), (jnp.transpose(w_uq[0]), bf, True), (w_ukv[0], bf), (w_out[0], bf)],
        "ag_weights",
        fused=((w_ada[0], b_cols), _ada_cols, jax.ShapeDtypeStruct((N_DEV, B, ncol), F32)))
    c_all = c_all.reshape(N_DEV * B, D)
    w_out_f = wout_g.reshape(D, D)

    mod = lax.dynamic_index_in_dim(mod_g, me, axis=1, keepdims=False)
    mod = jnp.transpose(mod, (1, 0, 2)).reshape(B, 3 * D)

    gx, dw_in_t, dw_uq, dw_ukv, dw_out, parts, dmod = _local_step(
        x, mod, positions, norm_gain, win_g, q_norm_gain, kv_norm_gain, wuq_g, wukv_g, swa_sinks,
        w_out_f, final_gain, loss_target)

    slots = lambda pair: tuple(a.reshape((4, 2, -1) + a.shape[-1:]) for a in pair)
    g_w_in_t, g_w_uq, g_w_ukv, g_w_out, parts_g, dmod_g = _reduce_scatter(
        [slots(dw_in_t), slots(dw_uq), slots(dw_ukv), slots(dw_out)], "rs_grads",
        gather=(parts, dmod))

    dmod_all = dmod_g.reshape(N_DEV * B, 3 * D)
    dmod_cols = lax.dynamic_slice_in_dim(dmod_all, me * ncol, ncol, axis=1)
    ws = [w_ada, b_ada, norm_gain, w_in, q_norm_gain, kv_norm_gain, w_uq, w_ukv, swa_sinks, w_out, final_gain]
    ms = [m_w_ada, m_b_ada, m_norm_gain, m_w_in, m_q_norm_gain, m_kv_norm_gain, m_w_uq, m_w_ukv, m_swa_sinks, m_w_out,
          m_final_gain]
    vs = [v_w_ada, v_b_ada, v_norm_gain, v_w_in, v_q_norm_gain, v_kv_norm_gain, v_w_uq, v_w_ukv, v_swa_sinks, v_w_out,
          v_final_gain]
    two_d = lambda a: a.reshape((1, a.shape[0]) if a.ndim == 1 else a.shape[-2:])
    view = {SHARDED[0]: lambda a: jnp.transpose(a, (2, 0, 1)), SHARDED[1]: lambda a: jnp.transpose(a[0])}
    flat = lambda arrs: [view[t](a) if t in view else two_d(a) for t, a in enumerate(arrs)]
    sharded = {3: g_w_in_t, 6: g_w_uq, 7: g_w_ukv, 9: g_w_out}
    loss, local_grads, deltas, new_ms, new_vs = _finalize_adamw(
        parts_g, dmod_all, dmod_cols, c_all, flat(ws), [sharded[t] for t in SHARDED], flat(ms), flat(vs))
    grads = [local_grads[t] if t in local_grads else sharded[t] for t in range(len(ws))]
    back = {SHARDED[0]: lambda a: jnp.transpose(a, (1, 2, 0)), SHARDED[1]: lambda a: jnp.transpose(a)[None]}
    shaped = lambda arrs: [back[t](a) if t in back else a.reshape(ws[t].shape) for t, a in enumerate(arrs)]
    return (loss.reshape(()), gx, *shaped(grads), *shaped(deltas), *shaped(new_ms), *shaped(new_vs))
```

```python
import functools

import jax
import jax.numpy as jnp
from jax import lax
from jax.experimental import pallas as pl
from jax.experimental.pallas import tpu as pltpu

F32 = jnp.float32
_MXU_DTYPE = jnp.bfloat16
MLA_GRAD_DTYPE = jnp.bfloat16

N_DEV = 8
MLA_HEADS = 8
MLA_NOPE = 64
MLA_ROPE = 32
MLA_QK = MLA_NOPE + MLA_ROPE
Q_LORA = 384
KV_LORA = 256
SWA_HEADS = 8
SWA_KV_HEADS = 2
SWA_HEAD_DIM = 64
WINDOW = 128
ROPE_THETA = 10000.0
EPS = 1e-6
MLA_SCALE = float((MLA_NOPE + MLA_ROPE) ** -0.5)
SWA_SCALE = float(SWA_HEAD_DIM ** -0.5)
LOG2E = 1.4426950408889634
MLA_QSCALE = MLA_SCALE * LOG2E
D_IN = 2464
D_IN_PAD = 2560
PAIR_ORDER = (0, 4, 1, 5, 2, 6, 3, 7)
PAIR_INV = (0, 2, 4, 6, 1, 3, 5, 7)

ADAM_LR = 0.001
ADAM_B1 = 0.9
ADAM_B2 = 0.999
ADAM_EPS = 1e-08
ADAM_WD = 0.01
ADAM_STEP = 10

LANE = 128
VMEM_LIMIT = 56 * 1024 * 1024

MESH = pl.DeviceIdType.MESH
NEG_INF = float("-inf")
SWA_SEQ_SPLIT = 2


def _mx(a):
    return a.astype(_MXU_DTYPE)


def _dot(a, b):
    return jnp.dot(a, b, preferred_element_type=F32)


def _dot_nt(a, b):
    return lax.dot_general(a, b, (((1,), (1,)), ((), ())), preferred_element_type=F32)


def _dot_tn(a, b):
    return lax.dot_general(a, b, (((0,), (0,)), ((), ())), preferred_element_type=F32)


def _cparams(sem=None):
    return pltpu.CompilerParams(dimension_semantics=sem, vmem_limit_bytes=VMEM_LIMIT)


def _vmem():
    return pl.BlockSpec(memory_space=pltpu.VMEM)


def _lane_iota(shape):
    return lax.broadcasted_iota(jnp.int32, shape, len(shape) - 1)


def _gather_program(srcs, outs, send_sems, recv_sems, local_sems):
    x, y, c = lax.axis_index("x"), lax.axis_index("y"), lax.axis_index("c")
    me, sibling = (x, y, c), (x, y, 1 - c)
    chips = [(1 - x, y), (x, 1 - y), (1 - x, 1 - y)]

    def slot(a, dev):
        return outs[a].at[4 * dev[0] + 2 * dev[1] + dev[2]]

    def copy(a, k, block, to, src=None):
        return pltpu.make_async_remote_copy(
            src_ref=slot(a, block) if src is None else src, dst_ref=slot(a, block),
            send_sem=send_sems.at[7 * a + k], recv_sem=recv_sems.at[7 * a + k],
            device_id=to, device_id_type=MESH)

    def local(a):
        return pltpu.make_async_copy(srcs[a], slot(a, me), local_sems.at[a])

    def start(a):
        local(a).start()
        cps = [copy(a, 0, me, sibling, src=srcs[a])]
        cps += [copy(a, 1 + j, me, (*chip, c), src=srcs[a]) for j, chip in enumerate(chips)]
        for cp in cps:
            cp.start()
        return cps

    def finish(group):
        cps = []
        for j, chip in enumerate(chips):
            for a in group:
                copy(a, 1 + j, (*chip, c), me).wait_recv()
                cp = copy(a, 4 + j, (*chip, c), sibling)
                cp.start()
                cps.append(cp)
        for a in group:
            copy(a, 0, sibling, me).wait_recv()
            for j, chip in enumerate(chips):
                copy(a, 4 + j, (*chip, 1 - c), me).wait_recv()
            local(a).wait()
        return cps

    return start, finish


def _gather_sems(m):
    return [pltpu.SemaphoreType.DMA((7 * m,)), pltpu.SemaphoreType.DMA((7 * m,)), pltpu.SemaphoreType.DMA((m,))]


def _all_gather(arrs, name, fused=None):
    n = len(arrs)
    cast = {a: e[1] for a, e in enumerate(arrs) if isinstance(e, tuple)}
    flip = {a for a, e in enumerate(arrs) if isinstance(e, tuple) and len(e) == 3}
    arrs = [e[0] if isinstance(e, tuple) else e for e in arrs]
    shapes = [(a.shape[-1], a.shape[0]) if i in flip else (a.shape[0], a.shape[-1]) for i, a in enumerate(arrs)]
    extra, fn, piece = fused if fused else ((), None, None)
    ne, m = len(extra), n + (1 if fused else 0)

    def body(*refs):
        ins, ex, outs = refs[:n], refs[n:n + ne], refs[n + ne:n + ne + m]
        rest = refs[n + ne + m:]
        staged = dict(zip(cast, rest))
        srcs = [staged.get(a, ins[a]) for a in range(n)] + ([rest[len(cast)]] if fused else [])
        start, finish = _gather_program(srcs, outs, *rest[-3:])
        pending = []
        for a in range(n):
            if a in staged:
                val = ins[a][:, 0, :] if len(ins[a].shape) == 3 else ins[a][...]
                staged[a][...] = (val.T if a in flip else val).astype(cast[a])
            pending += start(a)
        if fused:
            pending += finish([0])
            fn(srcs[n], outs[0], *ex)
            pending += start(n)
            pending += finish([n] + list(range(1, n)))
        else:
            pending += finish(list(range(n)))
        for cp in pending:
            cp.wait_send()

    out_shape = [jax.ShapeDtypeStruct((N_DEV,) + shapes[a], cast.get(a, arrs[a].dtype)) for a in range(n)]
    scratch = [pltpu.VMEM(shapes[a], cast[a]) for a in cast]
    if fused:
        out_shape.append(jax.ShapeDtypeStruct((N_DEV,) + piece.shape, piece.dtype))
        scratch.append(pltpu.VMEM(piece.shape, piece.dtype))
    return pl.pallas_call(
        body, name=name, out_shape=out_shape,
        in_specs=[_vmem()] * (n + ne), out_specs=[_vmem()] * m,
        scratch_shapes=scratch + _gather_sems(m),
        compiler_params=pltpu.CompilerParams(vmem_limit_bytes=VMEM_LIMIT),
    )(*arrs, *extra)


def _reduce_scatter(arrs, name, gather=()):
    n, g = len(arrs), len(gather)
    halves = [a[1] if isinstance(a, tuple) else a.astype(jnp.bfloat16) for a in arrs]
    arrs = [a[0] if isinstance(a, tuple) else a for a in arrs]

    def body(*refs):
        xs, xbs, gins = refs[:n], refs[n:2 * n], refs[2 * n:2 * n + g]
        outs, gouts = refs[2 * n + g:3 * n + g], refs[3 * n + g:3 * n + 2 * g]
        rest = refs[3 * n + 2 * g:]
        parts, recv_a, send_b, recv_b = (rest[t * n:(t + 1) * n] for t in range(4))
        send_sems, recv_sems, local_sems = rest[4 * n:4 * n + 3]
        x, y, c = lax.axis_index("x"), lax.axis_index("y"), lax.axis_index("c")
        myq = 2 * x + y
        gather_start, gather_finish = _gather_program(gins, gouts, *rest[4 * n + 3:])

        def chip(k):
            return (1 - x if k & 2 else x, 1 - y if k & 1 else y)

        def from_sibling(a):
            return pltpu.make_async_remote_copy(
                src_ref=xbs[a].at[:, 1 - c], dst_ref=recv_a[a], send_sem=send_sems.at[4 * a], recv_sem=recv_sems.at[4 * a],
                device_id=(x, y, 1 - c), device_id_type=MESH)

        def to_owner(a, k):
            qx, qy = chip(k)
            return pltpu.make_async_remote_copy(
                src_ref=send_b[a].at[2 * qx + qy], dst_ref=recv_b[a].at[myq],
                send_sem=send_sems.at[4 * a + k], recv_sem=recv_sems.at[4 * a + k],
                device_id=(qx, qy, c), device_id_type=MESH)

        mine = [pltpu.make_async_copy(xs[a].at[:, c], parts[a], local_sems.at[a]) for a in range(n)]
        first = [from_sibling(a) for a in range(n)]
        for cp in mine + first:
            cp.start()
        second = []
        for b in range(g):
            second += gather_start(b)
        for a in range(n):
            mine[a].wait()
            first[a].wait_recv()
            parts[a][...] = parts[a][...] + recv_a[a][...].astype(F32)
            send_b[a][...] = parts[a][...].astype(jnp.bfloat16)
            for k in range(1, 4):
                cp = to_owner(a, k)
                cp.start()
                second.append(cp)
        second += gather_finish(list(range(g)))
        for a in range(n):
            acc = parts[a][myq]
            for k in range(1, 4):
                to_owner(a, k).wait_recv()
                qx, qy = chip(k)
                acc = acc + recv_b[a][2 * qx + qy].astype(F32)
            outs[a][...] = acc
        for cp in first + second:
            cp.wait_send()

    quarter = lambda a, dt: pltpu.VMEM((4,) + a.shape[2:], dt)
    return pl.pallas_call(
        body, name=name,
        out_shape=[jax.ShapeDtypeStruct(a.shape[2:], F32) for a in arrs]
        + [jax.ShapeDtypeStruct((N_DEV,) + a.shape, a.dtype) for a in gather],
        in_specs=[pl.BlockSpec(memory_space=pl.ANY)] * (2 * n) + [_vmem()] * g, out_specs=[_vmem()] * (n + g),
        scratch_shapes=[quarter(a, F32) for a in arrs] + [quarter(a, jnp.bfloat16) for a in arrs] * 3
        + [pltpu.SemaphoreType.DMA((4 * n,)), pltpu.SemaphoreType.DMA((4 * n,)), pltpu.SemaphoreType.DMA((n,))]
        + _gather_sems(g),
        compiler_params=pltpu.CompilerParams(vmem_limit_bytes=VMEM_LIMIT),
    )(*arrs, *halves, *gather)


def _silu(t):
    return t * (1.0 / (1.0 + jnp.exp(-t)))


def _ada_cols(piece_ref, c_all_ref, w_ref, b_ref):
    w = _mx(w_ref[...])
    for d in range(N_DEV):
        piece_ref[d] = _dot(_mx(_silu(c_all_ref[d])), w) + b_ref[...]


def _rope_split(cd, sd):
    n = cd.shape[0]

    def expand(d):
        rep = jnp.broadcast_to(d[:, None, :], (n, 8, LANE)).reshape(8 * n, LANE)
        return pltpu.roll(rep, 0, 1, stride=16, stride_axis=0)

    c, s = expand(cd), expand(sd)
    lane = _lane_iota(c.shape)
    first = jnp.logical_and(lane >= 64, lane < 80)
    second = jnp.logical_and(lane >= 80, lane < 96)
    ck = jnp.where(first, pltpu.roll(c, 80, 1), jnp.where(second, pltpu.roll(c, 96, 1), 0.0))
    cq = jnp.where(lane < 64, 1.0, ck)
    sa = jnp.where(first, -pltpu.roll(s, 80, 1), 0.0)
    sb = jnp.where(second, pltpu.roll(s, 96, 1), 0.0)
    return cq, ck, sa, sb


def _tile_heads(t):
    return jnp.concatenate([t] * MLA_HEADS, axis=1)


def _pre_fwd(x2, shift, scale, ng, w_in, gq, gkv, w_uq, w_ukv, rope, S, ts):
    T, D = x2.shape
    nsb = S // ts
    WQ = MLA_HEADS * LANE

    def body(x_ref, sh_ref, sc_ref, ng_ref, wn_ref, gq_ref, gkv_ref, uqn_ref, ukvn_ref,
             cd_ref, sd_ref,
             hb_ref, zq_ref, zkv_ref, ql_ref, kvl_ref, q_ref, k_ref, v_ref, qs_ref, ks_ref, vs_ref, g_ref, wp_ref,
             wuq_ref, wuk_ref, wuv_ref):
        @pl.when(pl.program_id(0) == 0)
        def _():
            for h in range(MLA_HEADS):
                wuq_ref[:, LANE * h:LANE * h + MLA_QK] = uqn_ref[h]
                wuq_ref[:, LANE * h + MLA_QK:LANE * (h + 1)] = jnp.zeros((Q_LORA, LANE - MLA_QK), wuq_ref.dtype)
                wuk_ref[:, LANE * h:LANE * h + MLA_NOPE] = ukvn_ref[h, :, 0:MLA_NOPE]
                wuk_ref[:, LANE * h + MLA_NOPE:LANE * (h + 1)] = jnp.zeros((KV_LORA, LANE - MLA_NOPE), wuk_ref.dtype)
                wuv_ref[:, 64 * h:64 * (h + 1)] = ukvn_ref[h, :, MLA_NOPE:LANE]
            wp_ref[640:704, :] = jnp.zeros((64, D), wp_ref.dtype)
            wp_ref[736:768, :] = jnp.zeros((32, D), wp_ref.dtype)
            for d in range(N_DEV):
                lo, hi = W_IN_SHARD * d, W_IN_SHARD * (d + 1)
                for nat, pad, size in _w_in_row_runs():
                    a, b = max(nat, lo), min(nat + size, hi)
                    if a < b:
                        wp_ref[pad + a - nat:pad + b - nat, :] = wn_ref[d, a - lo:b - lo, :]

        w_ref = wp_ref
        x = x_ref[...]
        r1 = lax.rsqrt(jnp.mean(x * x, axis=-1, keepdims=True) + EPS)
        h = ((x * r1) * ng_ref[...]) * (1.0 + sc_ref[0]) + sh_ref[0]
        hb = _mx(h)
        hb_ref[...] = hb
        zq = _dot_nt(hb, w_ref[0:384, :])
        zq_ref[...] = zq
        rq = lax.rsqrt(jnp.mean(zq * zq, axis=-1, keepdims=True) + EPS)
        ql = _mx((zq * rq) * gq_ref[...])
        ql_ref[...] = ql
        q = _dot(ql, wuq_ref[...])
        cq, ck, sa, sb = _rope_split(cd_ref[...], sd_ref[...])
        q = (q * _tile_heads(cq) + pltpu.roll(q, WQ - 16, 1) * _tile_heads(sa)
             + pltpu.roll(q, 16, 1) * _tile_heads(sb))
        q_ref[...] = _mx(q * MLA_QSCALE)
        zkv = _dot_nt(hb, w_ref[384:640, :])
        zkv_ref[...] = zkv
        rkv = lax.rsqrt(jnp.mean(zkv * zkv, axis=-1, keepdims=True) + EPS)
        kvl = _mx((zkv * rkv) * gkv_ref[...])
        kvl_ref[...] = kvl
        kr = _dot_nt(hb, w_ref[640:768, :])
        kpe = kr * ck + pltpu.roll(kr, LANE - 16, 1) * sa + pltpu.roll(kr, 16, 1) * sb
        kf = _dot(kvl, wuk_ref[...])
        k_ref[...] = _mx(kf + jnp.concatenate([kpe] * MLA_HEADS, axis=1))
        v_ref[...] = _mx(_dot(kvl, wuv_ref[...]))
        g_ref[:, 0:512] = _dot_nt(hb, w_ref[768:1280, :])
        qs_ref[...] = _mx(_dot_nt(hb, w_ref[1280:1792, :]) * (SWA_SCALE * LOG2E))
        ks_ref[...] = _mx(_dot_nt(hb, w_ref[1792:1920, :]))
        vs_ref[...] = _mx(_dot_nt(hb, w_ref[1920:2048, :]))
        g_ref[:, 512:1024] = _dot_nt(hb, w_ref[2048:2560, :])

    row = lambda w: pl.BlockSpec((ts, w), lambda i: (i, 0))
    dense = pl.BlockSpec((ts // 8, LANE), lambda i: (i, 0))
    full = lambda a: pl.BlockSpec(a.shape, lambda i: (0,) * a.ndim)
    per_b = pl.BlockSpec((1, 1, D), lambda i: (i // nsb, 0, 0))
    out_w = [(D, _MXU_DTYPE), (384, F32), (256, F32), (384, _MXU_DTYPE), (256, _MXU_DTYPE), (WQ, _MXU_DTYPE),
             (WQ, _MXU_DTYPE), (512, _MXU_DTYPE), (512, _MXU_DTYPE), (128, _MXU_DTYPE), (128, _MXU_DTYPE), (1024, F32)]
    built = [(D_IN_PAD, D), (Q_LORA, WQ), (KV_LORA, WQ), (KV_LORA, 512)]
    return pl.pallas_call(
        body, name="pre_fwd", grid=(T // ts,),
        out_shape=[jax.ShapeDtypeStruct((T, w), dt) for w, dt in out_w]
        + [jax.ShapeDtypeStruct(s, w_in.dtype) for s in built],
        in_specs=[row(D), per_b, per_b, full(ng), full(w_in), full(gq), full(gkv), full(w_uq), full(w_ukv), dense, dense],
        out_specs=[row(w) for w, _ in out_w] + [pl.BlockSpec(s, lambda i: (0, 0)) for s in built],
        compiler_params=_cparams(("arbitrary",)),
    )(x2, shift, scale, ng, w_in, gq, gkv, w_uq, w_ukv, *rope)


def _mla_fwd(q3, k3, v3, tq, tk):
    B, S, _ = q3.shape
    nq = S // tq
    assert tq == tk
    HPS = 2
    NH = 2 * HPS

    def body(q_ref, k_ref, v_ref, o_ref, lse_ref):
        rows = lax.broadcasted_iota(jnp.int32, (tq, tk), 0)
        cols = lax.broadcasted_iota(jnp.int32, (tq, tk), 1)
        low_k = _lane_iota((tk, LANE)) < 64
        low = _lane_iota((tq, LANE)) < 64

        def step(qs, kt, carry, masked):
            r0 = kt * tk
            out = []
            for j in range(NH):
                v2 = v_ref[0, r0:r0 + tk, LANE * (j // 2):LANE * (j // 2 + 1)]
                vj = (jnp.where(low_k, v2, 1.0) if j % 2 == 0 else jnp.where(low_k, 1.0, v2)).astype(v2.dtype)
                m, acc = carry[j]
                s = _dot_nt(qs[j], k_ref[0, r0:r0 + tk, LANE * j:LANE * (j + 1)])
                if masked:
                    s = jnp.where(rows >= cols, s, NEG_INF)
                m_new = jnp.maximum(m, jnp.max(s, axis=1, keepdims=True))
                alpha = jnp.exp2(m - m_new)
                p = jnp.exp2(s - m_new)
                acc = alpha * acc + _dot(_mx(p), vj)
                out.append((m_new, acc))
            return tuple(out)

        for qi in range(nq):
            r = slice(qi * tq, (qi + 1) * tq)
            qs = [q_ref[0, r, LANE * j:LANE * (j + 1)] for j in range(NH)]
            init = (jnp.full((tq, 1), NEG_INF, F32), jnp.zeros((tq, LANE), F32))
            carry = (init,) * NH
            for kt in range(qi):
                carry = step(qs, kt, carry, False)
            carry = step(qs, qi, carry, True)
            for t in range(HPS):
                (m0, a0), (m1, a1) = carry[2 * t], carry[2 * t + 1]
                l0 = jnp.where(low, pltpu.roll(a0, 64, 1), a0)
                l1 = jnp.where(low, a1, pltpu.roll(a1, 64, 1))
                o_ref[0, r, LANE * t:LANE * (t + 1)] = jnp.where(low, a0 / l0, a1 / l1)
                lse_ref[0, 2 * t, r, :] = m0 + jnp.log2(l0)
                lse_ref[0, 2 * t + 1, r, :] = m1 + jnp.log2(l1)

    grp = lambda w: pl.BlockSpec((1, S, w), lambda b, hp: (b, 0, hp))
    return pl.pallas_call(
        body, name="mla_fwd", grid=(B, MLA_HEADS // NH),
        out_shape=[jax.ShapeDtypeStruct((B, S, 512), F32), jax.ShapeDtypeStruct((B, MLA_HEADS, S, LANE), F32)],
        in_specs=[grp(NH * LANE), grp(NH * LANE), grp(HPS * LANE)],
        out_specs=[grp(HPS * LANE), pl.BlockSpec((1, NH, S, LANE), lambda b, hp: (b, hp, 0, 0))],
        compiler_params=_cparams(("arbitrary", "arbitrary")),
    )(q3, k3, v3)


def _mla_bwd(q3, k3, v3, o3, do3, lse, tq, tk, hps=2):
    B, S, _ = q3.shape
    nq, nk = S // tq, S // tk
    nh = 2 * hps
    assert tk % tq == 0

    def body(q_ref, k_ref, v_ref, o_ref, do_ref, lse_ref, dq_ref, dk_ref, dv_ref, dkt_ref, dvt_ref):
        rows = lax.broadcasted_iota(jnp.int32, (tq, tk), 0)
        cols = lax.broadcasted_iota(jnp.int32, (tq, tk), 1)
        lane = _lane_iota((tq, LANE))

        def q_tile(qi, _):
            r = pl.ds(pl.multiple_of(qi * tq, tq), tq)
            heads = []
            for j in range(nh):
                lanes = slice(LANE * j, LANE * (j + 1))
                pair = slice(LANE * (j // 2), LANE * (j // 2 + 1))
                do = jnp.where((lane < 64) if j % 2 == 0 else (lane >= 64), do_ref[0, r, pair], 0.0)
                q = q_ref[0, r, lanes]
                heads.append((lanes, pair, q, _mx(q.astype(F32).T), _mx(do), _mx(do.T),
                              jnp.sum(do * o_ref[0, r, pair], axis=1, keepdims=True),
                              jnp.concatenate([lse_ref[0, j, r, :]] * (tk // LANE), axis=1)))
            n_full = (qi * tq) // tk

            def k_tile(kt, dqs, masked):
                kr = pl.ds(pl.multiple_of(kt * tk, tk), tk)
                first = masked and (qi * tq) % tk == 0
                out = []
                dvt = [None] * hps
                for j, (lanes, pair, q, qt, dob, dot_, dcol, lse_c) in enumerate(heads):
                    k = k_ref[0, kr, lanes]
                    s = _dot_nt(q, k)
                    if masked:
                        s = jnp.where(rows + qi * tq >= cols + kt * tk, s, NEG_INF)
                    p = jnp.exp2(s - lse_c)
                    dp = _dot_nt(dob, v_ref[0, kr, pair])
                    dsb = _mx(p * (dp - dcol))
                    if first:
                        dkt_ref[j, kt] = _dot(qt, dsb)
                    else:
                        dkt_ref[j, kt] += _dot(qt, dsb)
                    pv = _dot(dot_, _mx(p))
                    dvt[j // 2] = pv if dvt[j // 2] is None else dvt[j // 2] + pv
                    out.append(dqs[j] + _dot(dsb, k))
                for t in range(hps):
                    if first:
                        dvt_ref[t, kt] = dvt[t]
                    else:
                        dvt_ref[t, kt] += dvt[t]
                return tuple(out)

            dqs = (jnp.zeros((tq, LANE), F32),) * nh
            for kt in range(n_full):
                dqs = k_tile(kt, dqs, False)
            dqs = k_tile(n_full, dqs, True)
            for j in range(nh):
                dq_ref[0, r, heads[j][0]] = (MLA_SCALE * dqs[j]).astype(dq_ref.dtype)
            return 0

        for qi in range(nq):
            q_tile(qi, 0)

        def flush(kt, _):
            kr = pl.ds(pl.multiple_of(kt * tk, tk), tk)
            for j in range(nh):
                dk_ref[0, kr, LANE * j:LANE * (j + 1)] = ((1.0 / LOG2E) * dkt_ref[j, kt].T).astype(dk_ref.dtype)
            for t in range(hps):
                dv_ref[0, kr, LANE * t:LANE * (t + 1)] = dvt_ref[t, kt].T.astype(dv_ref.dtype)
            return 0

        lax.fori_loop(0, nk, flush, 0)

    grp = lambda w: pl.BlockSpec((1, S, w), lambda b, hp: (b, 0, hp))
    return pl.pallas_call(
        body, name="mla_bwd", grid=(B, MLA_HEADS // nh),
        out_shape=[jax.ShapeDtypeStruct((B, S, 1024), MLA_GRAD_DTYPE), jax.ShapeDtypeStruct((B, S, 1024), MLA_GRAD_DTYPE),
                   jax.ShapeDtypeStruct((B, S, 512), MLA_GRAD_DTYPE)],
        in_specs=[grp(nh * LANE), grp(nh * LANE), grp(hps * LANE), grp(hps * LANE), grp(hps * LANE),
                  pl.BlockSpec((1, nh, S, LANE), lambda b, hp: (b, hp, 0, 0))],
        out_specs=[grp(nh * LANE), grp(nh * LANE), grp(hps * LANE)],
        scratch_shapes=[pltpu.VMEM((nh, nk, LANE, tk), F32), pltpu.VMEM((hps, nk, LANE, tk), F32)],
        compiler_params=_cparams(("arbitrary", "arbitrary")),
    )(q3, k3, v3, o3, do3, lse)


def _swa_consts(sink_ref):
    W = WINDOW
    row = lax.broadcasted_iota(jnp.int32, (4 * W, LANE), 0)
    out = []
    for g in range(SWA_KV_HEADS):
        slope = jnp.zeros((4 * W, LANE), F32)
        sink = jnp.zeros((4 * W, LANE), F32)
        for p in range(4):
            h = p + 4 * g
            here = jnp.logical_and(row >= W * p, row < W * (p + 1))
            slope = jnp.where(here, float(LOG2E * 2.0 ** (-8.0 * (h + 1) / SWA_HEADS)), slope)
            sink = jnp.where(here, LOG2E * sink_ref[:, h:h + 1], sink)
        out.append((jnp.concatenate([slope, slope], axis=1), sink))
    return out


def _wide(col):
    return jnp.concatenate([col, col], axis=1)


def _swa_block(q_ref, k_ref, v_ref, pr_ref, n, i, ij):
    W = WINDOW
    kb = jnp.maximum(n - 1, 0)
    r = pl.ds(pl.multiple_of(i * W, W), W)
    kr = pl.ds(pl.multiple_of(kb * W, W), 2 * W)
    q4, k2, v2 = q_ref[0, r, :], k_ref[0, kr, :], v_ref[0, kr, :]
    pq = _wide(jnp.broadcast_to(pr_ref[0, pl.ds(n, 1), :], (W, W)).T)
    pk = jnp.concatenate([pr_ref[0, pl.ds(kb, 1), :], pr_ref[0, pl.ds(kb + 1, 1), :]], axis=1)
    rel = ij + (n - kb) * W
    dist = jnp.where(jnp.logical_and(rel >= 0, rel < W), pq - pk, float("inf"))
    return r, kb, kr, q4, k2, v2, jnp.concatenate([dist] * 4, axis=0)


def _swa_stack(x4, g, dtype):
    lane = _lane_iota((WINDOW, LANE))
    mine = (lane < 64) if g == 0 else (lane >= 64)
    return jnp.concatenate([jnp.where(mine, x4[:, LANE * p:LANE * (p + 1)], 0).astype(dtype) for p in range(4)], axis=0)


def _swa_unstack(ref, r, lo, hi, scale=None):
    W = WINDOW
    low = _lane_iota((W, LANE)) < 64
    for p in range(4):
        t = jnp.where(low, lo[W * p:W * (p + 1)], hi[W * p:W * (p + 1)])
        ref[0, r, LANE * p:LANE * (p + 1)] = t if scale is None else scale * t


def _swa_fwd(qs3, ks3, vs3, posr, sinks):
    B, S, _ = qs3.shape
    W = WINDOW
    nb = S // W
    nh = SWA_SEQ_SPLIT
    nbh = nb // nh

    def body(q_ref, k_ref, v_ref, pr_ref, sink_ref, o_ref, lse_ref):
        ij = lax.broadcasted_iota(jnp.int32, (W, 2 * W), 0) - lax.broadcasted_iota(jnp.int32, (W, 2 * W), 1)
        consts = _swa_consts(sink_ref)
        n0 = pl.program_id(1) * nbh

        def blk(i, _):
            n = n0 + i
            r, _, _, q4, k2, v2, dist4 = _swa_block(q_ref, k_ref, v_ref, pr_ref, n, i, ij)
            o_g = []
            for g, (slope, sink) in enumerate(consts):
                s = _dot_nt(_swa_stack(q4, g, q4.dtype), k2) - slope * dist4
                m = jnp.maximum(jnp.max(s, axis=1, keepdims=True), sink)
                e = jnp.exp2(s - _wide(m))
                l = jnp.sum(e, axis=1, keepdims=True) + jnp.exp2(sink - m)
                o_g.append(_dot(_mx(e), v2) * (1.0 / l))
                lse_ref[0, i, g] = m + jnp.log2(l)
            _swa_unstack(o_ref, r, o_g[0], o_g[1])
            return 0

        for i in range(nbh):
            blk(i, 0)

    seq = lambda w: pl.BlockSpec((1, S, w), lambda b, h: (b, 0, 0))
    part = lambda w: pl.BlockSpec((1, S // nh, w), lambda b, h: (b, h, 0))
    lse_spec = pl.BlockSpec((1, nbh, 2, 4 * W, LANE), lambda b, h: (b, h, 0, 0, 0))
    return pl.pallas_call(
        body, name="swa_fwd", grid=(B, nh),
        out_shape=[jax.ShapeDtypeStruct((B, S, 512), F32), jax.ShapeDtypeStruct((B, nb, 2, 4 * W, LANE), F32)],
        in_specs=[part(512), seq(LANE), seq(LANE), pl.BlockSpec((1, nb, W), lambda b, h: (b, 0, 0)),
                  pl.BlockSpec((1, LANE), lambda b, h: (0, 0))],
        out_specs=[part(512), lse_spec],
        compiler_params=_cparams(("arbitrary", "arbitrary")),
    )(qs3, ks3, vs3, posr, sinks)


def _swa_bwd(qs3, ks3, vs3, posr, sinks, os3, do3, lse):
    B, S, _ = qs3.shape
    W = WINDOW
    nb = S // W
    nh = SWA_SEQ_SPLIT
    nbh = nb // nh
    assert nbh % 2 == 0 and nbh * nh * W == S

    def body(q_ref, k_ref, v_ref, pr_ref, sink_ref, o_ref, do_ref, lse_ref, dq_ref, dk_ref, dv_ref, dsink_ref,
             dkt_ref, dvt_ref):
        lane1 = _lane_iota((1, LANE))
        ij = lax.broadcasted_iota(jnp.int32, (W, 2 * W), 0) - lax.broadcasted_iota(jnp.int32, (W, 2 * W), 1)
        consts = _swa_consts(sink_ref)
        hh = pl.program_id(1)
        n0 = hh * nbh

        @pl.when(hh == 0)
        def _():
            dkt_ref[...] = jnp.zeros_like(dkt_ref)
            dvt_ref[...] = jnp.zeros_like(dvt_ref)

        @pl.when(jnp.logical_and(pl.program_id(0) == 0, hh == 0))
        def _():
            dsink_ref[...] = jnp.zeros_like(dsink_ref)

        def blk(i, dsink):
            n = n0 + i
            r, kb, _, q4, k2, v2, dist4 = _swa_block(q_ref, k_ref, v_ref, pr_ref, n, i, ij)
            o4, do4 = o_ref[0, r, :], do_ref[0, r, :]
            dq_g = []
            dkt = jnp.zeros((LANE, 2 * W), F32)
            dvt = jnp.zeros((LANE, 2 * W), F32)
            for g, (slope, sink) in enumerate(consts):
                q_st = _swa_stack(q4, g, F32)
                do_st = _swa_stack(do4, g, F32)
                dcol = jnp.sum(do_st * _swa_stack(o4, g, F32), axis=1, keepdims=True)
                qst, dob = _mx(q_st), _mx(do_st)
                lse_c = lse_ref[0, i, g]
                pr = jnp.exp2(_dot_nt(qst, k2) - slope * dist4 - _wide(lse_c))
                dsb = _mx(pr * (_dot_nt(dob, v2) - dcol))
                psd = jnp.exp2(sink - lse_c)[:, 0:1] * dcol
                for p in range(4):
                    dsink = dsink - jnp.where(lane1 == p + 4 * g,
                                              jnp.sum(psd[W * p:W * (p + 1)], axis=0, keepdims=True), 0.0)
                dq_g.append(_dot(dsb, k2))
                dkt = dkt + _dot(_mx(q_st.T), dsb)
                dvt = dvt + _dot(_mx(do_st.T), _mx(pr))
            _swa_unstack(dq_ref, r, dq_g[0], dq_g[1], SWA_SCALE)
            dkt_ref[kb] += dkt[:, 0:W]
            dkt_ref[kb + 1] += dkt[:, W:2 * W]
            dvt_ref[kb] += dvt[:, 0:W]
            dvt_ref[kb + 1] += dvt[:, W:2 * W]
            return dsink

        dsink = jnp.zeros((1, LANE), F32)
        for i in range(nbh):
            dsink = blk(i, dsink)
        dsink_ref[...] += dsink

        @pl.when(hh == nh - 1)
        def _():
            def flush(n, _):
                r = pl.ds(pl.multiple_of(n * W, W), W)
                dk_ref[0, r, :] = (1.0 / LOG2E) * dkt_ref[n].T
                dv_ref[0, r, :] = dvt_ref[n].T
                return 0

            lax.fori_loop(0, nb, flush, 0)

    seq = lambda w: pl.BlockSpec((1, S, w), lambda b, h: (b, 0, 0))
    part = lambda w: pl.BlockSpec((1, S // nh, w), lambda b, h: (b, h, 0))
    return pl.pallas_call(
        body, name="swa_bwd", grid=(B, nh),
        out_shape=[jax.ShapeDtypeStruct((B, S, 512), F32), jax.ShapeDtypeStruct((B, S, LANE), F32),
                   jax.ShapeDtypeStruct((B, S, LANE), F32), jax.ShapeDtypeStruct((1, LANE), F32)],
        in_specs=[part(512), seq(LANE), seq(LANE), pl.BlockSpec((1, nb, W), lambda b, h: (b, 0, 0)),
                  pl.BlockSpec((1, LANE), lambda b, h: (0, 0)), part(512),
                  pl.BlockSpec((1, S // nh, 512), lambda b, h: (b, h, 1)),
                  pl.BlockSpec((1, nbh, 2, 4 * W, LANE), lambda b, h: (b, h, 0, 0, 0))],
        out_specs=[part(512), seq(LANE), seq(LANE), pl.BlockSpec((1, LANE), lambda b, h: (0, 0))],
        scratch_shapes=[pltpu.VMEM((nb, LANE, W), F32), pltpu.VMEM((nb, LANE, W), F32)],
        compiler_params=_cparams(("arbitrary", "arbitrary")),
    )(qs3, ks3, vs3, posr, sinks, os3, do3, lse)


def _post(om, osw, g, w_out, x2, gate, fg, tgt, S, ts):
    T, D = x2.shape
    nsb = S // ts

    nstep = T // ts
    RING = 3

    def body(om_ref, os_ref, g_hbm, w_ref, x_hbm, gate_ref, fg_ref, t_hbm,
             dx2_ref, do_ref, dg_ref, loss_ref, dfg_ref, dgate_ref, dw_ref, dwb_ref, g_buf, x_buf, t_buf, sems):
        i = pl.program_id(0)
        slot = i % RING

        def fetch(step, into):
            rows = pl.ds(pl.multiple_of(step * ts, ts), ts)
            return [pltpu.make_async_copy(src.at[rows, :], dst.at[into], sems.at[k, into])
                    for k, (src, dst) in enumerate(((g_hbm, g_buf), (x_hbm, x_buf), (t_hbm, t_buf)))]

        @pl.when(i == 0)
        def _():
            for s in range(min(RING, nstep)):
                for cp in fetch(s, s):
                    cp.start()

        for cp in fetch(i, slot):
            cp.wait()
        g_ref, x_ref, t_ref = g_buf.at[slot], x_buf.at[slot], t_buf.at[slot]
        gv = g_ref[...]
        sg = 1.0 / (1.0 + jnp.exp(-gv))
        silu = gv * sg
        o = jnp.concatenate([om_ref[...], os_ref[...]], axis=1)
        ab = _mx(o * silu)
        y = _dot(ab, w_ref[...])
        gate = gate_ref[0]
        xo = x_ref[...] + gate * y
        r2 = lax.rsqrt(jnp.mean(xo * xo, axis=-1, keepdims=True) + EPS)
        xh = xo * r2
        fg = fg_ref[...]
        diff = xh * fg - t_ref[...]
        sq = jnp.sum(diff * diff, axis=0, keepdims=True)
        part = sq[:, 0:LANE]
        for t in range(1, D // LANE):
            part = part + sq[:, LANE * t:LANE * (t + 1)]
        dout = diff * (1.0 / D)
        dxh = dout * fg
        dx2 = r2 * (dxh - xh * jnp.mean(dxh * xh, axis=-1, keepdims=True))
        dx2_ref[...] = dx2
        dyb = _mx(dx2 * gate)
        da = _dot_nt(dyb, w_ref[...])
        do_ref[...] = da * silu
        dg_ref[...] = _mx(da * o * (sg * (1.0 + gv * (1.0 - sg))))

        @pl.when(i == 0)
        def _():
            loss_ref[...] = jnp.zeros_like(loss_ref)
            dfg_ref[...] = jnp.zeros_like(dfg_ref)
            dw_ref[...] = jnp.zeros_like(dw_ref)

        @pl.when(i % nsb == 0)
        def _():
            dgate_ref[...] = jnp.zeros_like(dgate_ref)

        loss_ref[...] += (0.5 / D) * part
        dfg_ref[...] += jnp.sum(dout * xh, axis=0, keepdims=True)
        dgate_ref[0] += jnp.sum(dx2 * y, axis=0, keepdims=True)
        dw_ref[...] += _dot_tn(ab, dyb)

        @pl.when(i == T // ts - 1)
        def _():
            chunks = [dw_ref[512 + 64 * PAIR_INV[h]:512 + 64 * (PAIR_INV[h] + 1), :] for h in range(SWA_HEADS)]
            for h in range(SWA_HEADS):
                dw_ref[512 + 64 * h:512 + 64 * (h + 1), :] = chunks[h]
            dwb_ref[...] = dw_ref[...].astype(jnp.bfloat16)

        @pl.when(i + RING < nstep)
        def _():
            for cp in fetch(i + RING, slot):
                cp.start()

    row = lambda w: pl.BlockSpec((ts, w), lambda i: (i, 0))
    hbm = pl.BlockSpec(memory_space=pl.ANY)
    full = lambda a: pl.BlockSpec(a.shape, lambda i: (0,) * a.ndim)
    per_b = pl.BlockSpec((1, 1, D), lambda i: (i // nsb, 0, 0))
    return pl.pallas_call(
        body, name="post", grid=(T // ts,),
        out_shape=[jax.ShapeDtypeStruct((T, D), F32), jax.ShapeDtypeStruct((T, 1024), F32),
                   jax.ShapeDtypeStruct((T, 1024), _MXU_DTYPE), jax.ShapeDtypeStruct((1, LANE), F32),
                   jax.ShapeDtypeStruct((1, D), F32), jax.ShapeDtypeStruct(gate.shape, F32),
                   jax.ShapeDtypeStruct(w_out.shape, F32), jax.ShapeDtypeStruct(w_out.shape, jnp.bfloat16)],
        in_specs=[row(512), row(512), hbm, full(w_out), hbm, per_b, full(fg), hbm],
        out_specs=[row(D), row(1024), row(1024),
                   pl.BlockSpec((1, LANE), lambda i: (0, 0)), pl.BlockSpec((1, D), lambda i: (0, 0)), per_b,
                   full(w_out), full(w_out)],
        scratch_shapes=[pltpu.VMEM((RING, ts, 1024), F32), pltpu.VMEM((RING, ts, D), F32), pltpu.VMEM((RING, ts, D), F32),
                        pltpu.SemaphoreType.DMA((3, RING))],
        compiler_params=_cparams(("arbitrary",)),
    )(om, osw, g, w_out, x2, gate, fg, tgt)


def _pre_bwd(dq, dk, dv, dqs, dks, dvs, dg, zq, zkv, ql, kvl, x2, dx2, scale, ng, w_in, gq, gkv, w_uq, w_uk, w_uv,
             rope, S, ts):
    T, D = x2.shape
    nsb = S // ts
    WQ = MLA_HEADS * LANE

    def body(dq_ref, dk_ref, dv_ref, dqs_ref, dks_ref, dvs_ref, dg_ref, zq_ref, zkv_ref, ql_ref, kvl_ref, x_ref, dx2_ref,
             sc_ref,
             ng_ref, w_ref, gq_ref, gkv_ref, wuq_ref, wuk_ref, wuv_ref, cd_ref, sd_ref,
             gx_ref, dz_ref, dgq_ref, dgkv_ref, dng_ref, dsh_ref, dsc_ref, uq_ref, uqb_ref, ukv_ref, ukvb_ref,
             dwuq_ref, dwuk_ref, dwuv_ref):
        i = pl.program_id(0)

        @pl.when(i == 0)
        def _():
            dgq_ref[...] = jnp.zeros_like(dgq_ref)
            dgkv_ref[...] = jnp.zeros_like(dgkv_ref)
            dng_ref[...] = jnp.zeros_like(dng_ref)
            dwuq_ref[...] = jnp.zeros_like(dwuq_ref)
            dwuk_ref[...] = jnp.zeros_like(dwuk_ref)
            dwuv_ref[...] = jnp.zeros_like(dwuv_ref)

        @pl.when(i % nsb == 0)
        def _():
            dsh_ref[...] = jnp.zeros_like(dsh_ref)
            dsc_ref[...] = jnp.zeros_like(dsc_ref)

        def norm_bwd(z, dy, gain):
            r = lax.rsqrt(jnp.mean(z * z, axis=-1, keepdims=True) + EPS)
            zh = z * r
            dzh = dy * gain
            return r * (dzh - zh * jnp.mean(dzh * zh, axis=-1, keepdims=True)), jnp.sum(dy * zh, axis=0, keepdims=True)

        tables = _rope_split(cd_ref[...], sd_ref[...])
        ng = ng_ref[...]
        sc1 = 1.0 + sc_ref[0]

        def rows_chain(rs):
            cq, ck, sa, sb = (t[rs] for t in tables)
            dqr = dq_ref[rs, :].astype(F32)
            dqb = _mx(dqr * _tile_heads(cq) + pltpu.roll(dqr * _tile_heads(sa), 16, 1)
                      + pltpu.roll(dqr * _tile_heads(sb), WQ - 16, 1))
            p_uq = _dot_tn(ql_ref[rs, :], dqb)
            dzq, dgq = norm_bwd(zq_ref[rs, :], _dot_nt(dqb, wuq_ref[...]), gq_ref[...])
            dkr = dk_ref[rs, :].astype(F32)
            dkb = _mx(dkr)
            p_uk = _dot_tn(kvl_ref[rs, :], dkb)
            dvb = _mx(dv_ref[rs, :])
            p_uv = _dot_tn(kvl_ref[rs, :], dvb)
            dzkv, dgkv = norm_bwd(zkv_ref[rs, :], _dot_nt(dkb, wuk_ref[...]) + _dot_nt(dvb, wuv_ref[...]), gkv_ref[...])
            dkpe = dkr[:, 0:LANE]
            for h in range(1, MLA_HEADS):
                dkpe = dkpe + dkr[:, LANE * h:LANE * (h + 1)]
            dkro = dkpe * ck + pltpu.roll(dkpe * sa, 16, 1) + pltpu.roll(dkpe * sb, LANE - 16, 1)
            dz_ref[rs, 0:384] = _mx(dzq)
            dz_ref[rs, 384:640] = _mx(dzkv)
            dz_ref[rs, 640:768] = _mx(dkro)
            dz_ref[rs, 768:1280] = dg_ref[rs, 0:512]
            dz_ref[rs, 1280:1792] = _mx(dqs_ref[rs, :])
            dz_ref[rs, 1792:1920] = _mx(dks_ref[rs, :])
            dz_ref[rs, 1920:2048] = _mx(dvs_ref[rs, :])
            dz_ref[rs, 2048:2560] = dg_ref[rs, 512:1024]
            dh = _dot(dz_ref[rs, :], w_ref[...])
            x = x_ref[rs, :]
            r1 = lax.rsqrt(jnp.mean(x * x, axis=-1, keepdims=True) + EPS)
            xn = x * r1
            dxn = dh * ng * sc1
            gx_ref[rs, :] = dx2_ref[rs, :] + r1 * (dxn - xn * jnp.mean(dxn * xn, axis=-1, keepdims=True))
            return (p_uq, p_uk, p_uv, dgq, dgkv, jnp.sum(dh, axis=0, keepdims=True),
                    jnp.sum(dh * (xn * ng), axis=0, keepdims=True), jnp.sum(dh * xn * sc1, axis=0, keepdims=True))

        hr = ts // 2
        parts = [rows_chain(slice(hr * t, hr * (t + 1))) for t in range(2)]
        p_uq, p_uk, p_uv, dgq, dgkv, dsh, dsc, dng = (a + b for a, b in zip(*parts))
        dwuq_ref[...] += p_uq
        dwuk_ref[...] += p_uk
        dwuv_ref[...] += p_uv
        dgq_ref[...] += dgq
        dgkv_ref[...] += dgkv
        dsh_ref[0] += dsh
        dsc_ref[0] += dsc
        dng_ref[...] += dng

        @pl.when(i == T // ts - 1)
        def _():
            for h in range(MLA_HEADS):
                uq = dwuq_ref[:, LANE * h:LANE * h + MLA_QK]
                ukv = jnp.concatenate([dwuk_ref[:, LANE * h:LANE * h + MLA_NOPE], dwuv_ref[:, 64 * h:64 * (h + 1)]], axis=1)
                uq_ref[h] = uq
                uqb_ref[h] = uq.astype(jnp.bfloat16)
                ukv_ref[h] = ukv
                ukvb_ref[h] = ukv.astype(jnp.bfloat16)

    row = lambda w: pl.BlockSpec((ts, w), lambda i: (i, 0))
    full = lambda a: pl.BlockSpec(a.shape, lambda i: (0,) * a.ndim, pipeline_mode=pl.Buffered(1))
    slot = lambda r, c, dt: (jax.ShapeDtypeStruct((MLA_HEADS, r, c), dt), pl.BlockSpec((MLA_HEADS, r, c), lambda i: (0, 0, 0)))
    slots = [slot(Q_LORA, MLA_QK, F32), slot(Q_LORA, MLA_QK, jnp.bfloat16), slot(KV_LORA, LANE, F32), slot(KV_LORA, LANE, jnp.bfloat16)]
    per_b = pl.BlockSpec((1, 1, D), lambda i: (i // nsb, 0, 0))
    dense = pl.BlockSpec((ts // 8, LANE), lambda i: (i, 0))
    vec = lambda w: pl.BlockSpec((1, w), lambda i: (0, 0))
    return pl.pallas_call(
        body, name="pre_bwd", grid=(T // ts,),
        out_shape=[jax.ShapeDtypeStruct((T, D), F32), jax.ShapeDtypeStruct((T, D_IN_PAD), _MXU_DTYPE), jax.ShapeDtypeStruct((1, 384), F32),
                   jax.ShapeDtypeStruct((1, 256), F32), jax.ShapeDtypeStruct((1, D), F32),
                   jax.ShapeDtypeStruct(scale.shape, F32), jax.ShapeDtypeStruct(scale.shape, F32)] + [t[0] for t in slots],
        in_specs=[row(WQ), row(WQ), row(512), row(512), row(LANE), row(LANE), row(1024), row(384), row(256), row(384),
                  row(256), row(D), row(D), per_b, full(ng), full(w_in), full(gq), full(gkv), full(w_uq), full(w_uk), full(w_uv),
                  dense, dense],
        out_specs=[row(D), row(D_IN_PAD), vec(384), vec(256), vec(D), per_b, per_b] + [t[1] for t in slots],
        scratch_shapes=[pltpu.VMEM(w_uq.shape, F32), pltpu.VMEM(w_uk.shape, F32), pltpu.VMEM(w_uv.shape, F32)],
        compiler_params=_cparams(("arbitrary",)),
    )(dq, dk, dv, dqs, dks, dvs, dg, zq, zkv, ql, kvl, x2, dx2, scale, ng, w_in, gq, gkv, w_uq, w_uk, w_uv, *rope)


def _w_in_row_runs():
    runs = [(0, 0, 640), (640, 704, 32), (672, 768, 512)]
    runs += [(1184 + 64 * h, 1280 + 64 * PAIR_INV[h], 64) for h in range(8)]
    runs += [(1696, 1792, 256)]
    runs += [(1952 + 64 * h, 2048 + 64 * PAIR_INV[h], 64) for h in range(8)]
    return runs


W_IN_SHARD = D_IN // N_DEV
W_IN_SLOT = 320


def _dw_in_t(dz, hb, tn, tk):
    T, M = dz.shape
    N = hb.shape[1]
    nk = T // tk

    def body(a_ref, b_ref, o_ref, ob_ref, acc_ref):
        k = pl.program_id(1)

        @pl.when(k == 0)
        def _():
            acc_ref[...] = jnp.zeros_like(acc_ref)

        acc_ref[...] += _dot_tn(a_ref[...], b_ref[...])

        @pl.when(k == nk - 1)
        def _():
            for d in range(N_DEV):
                lo, hi = W_IN_SHARD * d, W_IN_SHARD * (d + 1)
                for nat, pad, size in _w_in_row_runs():
                    a, b = max(nat, lo), min(nat + size, hi)
                    if a < b:
                        piece = acc_ref[pad + a - nat:pad + b - nat, :]
                        o_ref[d, a - lo:b - lo, :] = piece
                        ob_ref[d, a - lo:b - lo, :] = piece.astype(ob_ref.dtype)
                o_ref[d, W_IN_SHARD:W_IN_SLOT, :] = jnp.zeros((W_IN_SLOT - W_IN_SHARD, tn), F32)
                ob_ref[d, W_IN_SHARD:W_IN_SLOT, :] = jnp.zeros((W_IN_SLOT - W_IN_SHARD, tn), ob_ref.dtype)

    slots = pl.BlockSpec((N_DEV, W_IN_SLOT, tn), lambda j, k: (0, 0, j))
    return pl.pallas_call(
        body, name="dw_in", grid=(N // tn, nk),
        out_shape=[jax.ShapeDtypeStruct((N_DEV, W_IN_SLOT, N), F32), jax.ShapeDtypeStruct((N_DEV, W_IN_SLOT, N), jnp.bfloat16)],
        in_specs=[pl.BlockSpec((tk, M), lambda j, k: (k, 0)), pl.BlockSpec((tk, tn), lambda j, k: (k, j))],
        out_specs=[slots, slots],
        scratch_shapes=[pltpu.VMEM((M, tn), F32)],
        compiler_params=_cparams(("arbitrary", "arbitrary")),
    )(dz, hb)


SHARDED = (3, 6, 7, 9)
PART_SLICES = {10: (128, 1152), 2: (1152, 2176), 4: (2176, 2560), 5: (2560, 2816), 8: (2816, 2824)}


def _finalize_adamw(parts_all, dmod_all, dmod_cols, c_all, ws, sharded_grads, ms, vs):
    n = len(ws)
    local = [t for t in range(n) if t not in SHARDED[2:]]

    def body(*refs):
        p_ref, dm_ref, dmc_ref, c_ref = refs[:4]
        w_refs, gs_refs, m_refs, v_refs = refs[4:4 + n], refs[4 + n:8 + n], refs[8 + n:8 + 2 * n], refs[8 + 2 * n:8 + 3 * n]
        outs = refs[8 + 3 * n:]
        loss_ref, gl_refs = outs[0], outs[1:1 + len(local)]
        d_refs, nm_refs, nv_refs = (outs[1 + len(local) + t * n:1 + len(local) + (t + 1) * n] for t in range(3))
        ps_ref = outs[-1]
        rd = lambda r: r[:, 0, :] if len(r.shape) == 3 else r[...]

        def wr(r, val):
            if len(r.shape) == 3:
                r[:, 0, :] = val
            else:
                r[...] = val

        acc = p_ref[0]
        for j in range(1, N_DEV):
            acc = acc + p_ref[j]
        ps_ref[...] = acc
        loss_ref[...] = jnp.sum(acc[:, 0:LANE], axis=1, keepdims=True)
        db = dm_ref[0:1, :]
        for j in range(1, dm_ref.shape[0]):
            db = db + dm_ref[j:j + 1, :]
        grads = {0: _dot_tn(_mx(_silu(c_ref[...])), _mx(dmc_ref[...])), 1: db}
        for t, (lo, hi) in PART_SLICES.items():
            grads[t] = ps_ref[:, lo:hi]
        for i, t in enumerate(SHARDED):
            grads[t] = gs_refs[i][...]
        grads[SHARDED[0]] = gs_refs[0][0:W_IN_SHARD, :]
        grads[SHARDED[1]] = gs_refs[1][...].T
        for i, t in enumerate(local):
            wr(gl_refs[i], grads[t])
        for t in range(n):
            gv = grads[t]
            nm = ADAM_B1 * rd(m_refs[t]) + (1.0 - ADAM_B1) * gv
            nv = ADAM_B2 * rd(v_refs[t]) + (1.0 - ADAM_B2) * (gv * gv)
            m_hat = nm / (1.0 - ADAM_B1 ** ADAM_STEP)
            v_hat = nv / (1.0 - ADAM_B2 ** ADAM_STEP)
            wr(d_refs[t], -ADAM_LR * (m_hat / (jnp.sqrt(v_hat) + ADAM_EPS) + ADAM_WD * rd(w_refs[t])))
            wr(nm_refs[t], nm)
            wr(nv_refs[t], nv)

    like = lambda arrs: [jax.ShapeDtypeStruct(a.shape, F32) for a in arrs]
    out = pl.pallas_call(
        body, name="finalize_adamw",
        out_shape=[jax.ShapeDtypeStruct((1, 1), F32)] + like([ws[t] for t in local]) + like(ws) * 3,
        in_specs=[_vmem()] * (8 + 3 * n), out_specs=[_vmem()] * (1 + len(local) + 3 * n),
        scratch_shapes=[pltpu.VMEM(parts_all.shape[1:], F32)],
        compiler_params=_cparams(),
    )(parts_all, dmod_all, dmod_cols, c_all, *ws, *sharded_grads, *ms, *vs)
    k = 1 + len(local)
    return out[0], dict(zip(local, out[1:k])), out[k:k + n], out[k + n:k + 2 * n], out[k + 2 * n:]


def _pair_perm(a, axis, order):
    a = jnp.moveaxis(a, axis, -1)
    lead = a.shape[:-1]
    a = a.reshape(lead + (8, 64))[..., list(order), :].reshape(lead + (512,))
    return jnp.moveaxis(a, -1, axis)


def _rope_table(positions):
    T = positions.size
    inv = ROPE_THETA ** (-jnp.arange(0, MLA_ROPE, 2, dtype=F32) / MLA_ROPE)
    pos = jnp.repeat(positions.reshape(T // 8, 8)[:, ::-1].astype(F32), MLA_ROPE // 2, axis=1)
    ang = pos * jnp.tile(inv, 8)[None, :]
    return jnp.cos(ang), jnp.sin(ang)


def _local_step(x, mod, positions, ng, w_in_t, gq, gkv, w_uq, w_ukv, sinks, w_out, fg, tgt,
                ts=512, fq=512, fk=512, bq=512, bk=512):
    B, S, D = x.shape
    T = B * S
    x2 = x.reshape(T, D)
    shift, scale, gate = (mod[:, None, k * D:(k + 1) * D] for k in range(3))
    w_out_p = jnp.concatenate([w_out[:512], _pair_perm(w_out[512:], 0, PAIR_ORDER)], axis=0)
    rope = _rope_table(positions)
    posf = positions.astype(F32)
    posr = posf.reshape(B, S // WINDOW, WINDOW)
    sinks_l = jnp.pad(sinks.reshape(1, SWA_HEADS), ((0, 0), (0, LANE - SWA_HEADS)))

    (hb, zq, zkv, ql, kvl, q, k, v, qs, ks, vs, g, w_in_p, w_uq_p, w_uk_p, w_uv) = _pre_fwd(
        x2, shift, scale, ng, w_in_t, gq, gkv, w_uq, w_ukv, rope, S, ts)
    r3 = lambda a: a.reshape(B, S, a.shape[-1])
    om, lse_m = _mla_fwd(r3(q), r3(k), r3(v), fq, fk)
    osw, lse_s = _swa_fwd(r3(qs), r3(ks), r3(vs), posr, sinks_l)
    dx2, do, dg, loss_v, dfg, dgate, *dw_out = _post(
        om.reshape(T, 512), osw.reshape(T, 512), g, w_out_p, x2, gate, fg.reshape(1, D), tgt.reshape(T, D), S, ts)
    do3 = r3(do)
    dq, dk, dv = _mla_bwd(r3(q), r3(k), r3(v), om, do3, lse_m, bq, bk)
    dqs, dks, dvs, dsink = _swa_bwd(r3(qs), r3(ks), r3(vs), posr, sinks_l, osw, do3, lse_s)
    f2 = lambda a: a.reshape(T, a.shape[-1])
    gx, dz, dgq, dgkv, dng, dsh, dsc, *dw_u = _pre_bwd(
        f2(dq), f2(dk), f2(dv), f2(dqs), f2(dks), f2(dvs), dg, zq, zkv, ql, kvl, x2, dx2, scale, ng, w_in_p, gq, gkv,
        w_uq_p, w_uk_p, w_uv, rope, S, ts)
    tk = min(T, 1024)
    dw_in_t = _dw_in_t(dz, hb, 512, tk)
    parts = jnp.concatenate([loss_v, dfg, dng, dgq, dgkv, dsink], axis=1)
    dmod = jnp.concatenate([dsh, dsc, dgate], axis=2).reshape(B, 3 * D)
    return gx.reshape(B, S, D), dw_in_t, tuple(dw_u[0:2]), tuple(dw_u[2:4]), tuple(dw_out), parts, dmod


def kernel(x, c, positions, w_ada, b_ada, norm_gain, w_in, q_norm_gain, kv_norm_gain, w_uq, w_ukv, swa_sinks, w_out, final_gain, loss_target, m_w_ada, m_b_ada, m_norm_gain, m_w_in, m_q_norm_gain, m_kv_norm_gain, m_w_uq, m_w_ukv, m_swa_sinks, m_w_out, m_final_gain, v_w_ada, v_b_ada, v_norm_gain, v_w_in, v_q_norm_gain, v_kv_norm_gain, v_w_uq, v_w_ukv, v_swa_sinks, v_w_out, v_final_gain):
    B, S, D = x.shape
    me = 4 * lax.axis_index("x") + 2 * lax.axis_index("y") + lax.axis_index("c")
    bf = _MXU_DTYPE

    ncol = w_ada.shape[2]
    b_cols = lax.dynamic_slice_in_dim(b_ada, me * ncol, ncol, axis=1)
    c_all, win_g, wuq_g, wukv_g, wout_g, mod_g = _all_gather(
        [c, (jnp.transpose(w_in, (2, 0, 1)), bf), (jnp.transpose(w_uq[0]), bf, True), (w_ukv[0], bf), (w_out[0], bf)],
        "ag_weights",
        fused=((w_ada[0], b_cols), _ada_cols, jax.ShapeDtypeStruct((N_DEV, B, ncol), F32)))
    c_all = c_all.reshape(N_DEV * B, D)
    w_out_f = wout_g.reshape(D, D)

    mod = lax.dynamic_index_in_dim(mod_g, me, axis=1, keepdims=False)
    mod = jnp.transpose(mod, (1, 0, 2)).reshape(B, 3 * D)

    gx, dw_in_t, dw_uq, dw_ukv, dw_out, parts, dmod = _local_step(
        x, mod, positions, norm_gain, win_g, q_norm_gain, kv_norm_gain, wuq_g, wukv_g, swa_sinks,
        w_out_f, final_gain, loss_target)

    slots = lambda pair: tuple(a.reshape((4, 2, -1) + a.shape[-1:]) for a in pair)
    g_w_in_t, g_w_uq, g_w_ukv, g_w_out, parts_g, dmod_g = _reduce_scatter(
        [slots(dw_in_t), slots(dw_uq), slots(dw_ukv), slots(dw_out)], "rs_grads",
        gather=(parts, dmod))

    dmod_all = dmod_g.reshape(N_DEV * B, 3 * D)
    dmod_cols = lax.dynamic_slice_in_dim(dmod_all, me * ncol, ncol, axis=1)
    ws = [w_ada, b_ada, norm_gain, w_in, q_norm_gain, kv_norm_gain, w_uq, w_ukv, swa_sinks, w_out, final_gain]
    ms = [m_w_ada, m_b_ada, m_norm_gain, m_w_in, m_q_norm_gain, m_kv_norm_gain, m_w_uq, m_w_ukv, m_swa_sinks, m_w_out,
          m_final_gain]
    vs = [v_w_ada, v_b_ada, v_norm_gain, v_w_in, v_q_norm_gain, v_kv_norm_gain, v_w_uq, v_w_ukv, v_swa_sinks, v_w_out,
          v_final_gain]
    two_d = lambda a: a.reshape((1, a.shape[0]) if a.ndim == 1 else a.shape[-2:])
    view = {SHARDED[0]: lambda a: jnp.transpose(a, (2, 0, 1)), SHARDED[1]: lambda a: jnp.transpose(a[0])}
    flat = lambda arrs: [view[t](a) if t in view else two_d(a) for t, a in enumerate(arrs)]
    sharded = {3: g_w_in_t, 6: g_w_uq, 7: g_w_ukv, 9: g_w_out}
    loss, local_grads, deltas, new_ms, new_vs = _finalize_adamw(
        parts_g, dmod_all, dmod_cols, c_all, flat(ws), [sharded[t] for t in SHARDED], flat(ms), flat(vs))
    grads = [local_grads[t] if t in local_grads else sharded[t] for t in range(len(ws))]
    back = {SHARDED[0]: lambda a: jnp.transpose(a, (1, 2, 0)), SHARDED[1]: lambda a: jnp.transpose(a)[None]}
    shaped = lambda arrs: [back[t](a) if t in back else a.reshape(ws[t].shape) for t, a in enumerate(arrs)]
    return (loss.reshape(()), gx, *shaped(grads), *shaped(deltas), *shaped(new_ms), *shaped(new_vs))
```

```python
import functools

import jax
import jax.numpy as jnp
from jax import lax
from jax.experimental import pallas as pl
from jax.experimental.pallas import tpu as pltpu

F32 = jnp.float32
_MXU_DTYPE = jnp.bfloat16
MLA_GRAD_DTYPE = jnp.bfloat16

N_DEV = 8
MLA_HEADS = 8
MLA_NOPE = 64
MLA_ROPE = 32
MLA_QK = MLA_NOPE + MLA_ROPE
Q_LORA = 384
KV_LORA = 256
SWA_HEADS = 8
SWA_KV_HEADS = 2
SWA_HEAD_DIM = 64
WINDOW = 128
ROPE_THETA = 10000.0
EPS = 1e-6
MLA_SCALE = float((MLA_NOPE + MLA_ROPE) ** -0.5)
SWA_SCALE = float(SWA_HEAD_DIM ** -0.5)
LOG2E = 1.4426950408889634
MLA_QSCALE = MLA_SCALE * LOG2E
D_IN = 2464
D_IN_PAD = 2560
PAIR_ORDER = (0, 4, 1, 5, 2, 6, 3, 7)
PAIR_INV = (0, 2, 4, 6, 1, 3, 5, 7)

ADAM_LR = 0.001
ADAM_B1 = 0.9
ADAM_B2 = 0.999
ADAM_EPS = 1e-08
ADAM_WD = 0.01
ADAM_STEP = 10

LANE = 128
VMEM_LIMIT = 56 * 1024 * 1024

MESH = pl.DeviceIdType.MESH
NEG_INF = float("-inf")
SWA_SEQ_SPLIT = 2


def _mx(a):
    return a.astype(_MXU_DTYPE)


def _dot(a, b):
    return jnp.dot(a, b, preferred_element_type=F32)


def _dot_nt(a, b):
    return lax.dot_general(a, b, (((1,), (1,)), ((), ())), preferred_element_type=F32)


def _dot_tn(a, b):
    return lax.dot_general(a, b, (((0,), (0,)), ((), ())), preferred_element_type=F32)


def _cparams(sem=None):
    return pltpu.CompilerParams(dimension_semantics=sem, vmem_limit_bytes=VMEM_LIMIT)


def _vmem():
    return pl.BlockSpec(memory_space=pltpu.VMEM)


def _lane_iota(shape):
    return lax.broadcasted_iota(jnp.int32, shape, len(shape) - 1)


def _gather_program(srcs, outs, send_sems, recv_sems, local_sems):
    x, y, c = lax.axis_index("x"), lax.axis_index("y"), lax.axis_index("c")
    me, sibling = (x, y, c), (x, y, 1 - c)
    chips = [(1 - x, y), (x, 1 - y), (1 - x, 1 - y)]

    def slot(a, dev):
        return outs[a].at[4 * dev[0] + 2 * dev[1] + dev[2]]

    def copy(a, k, block, to, src=None):
        return pltpu.make_async_remote_copy(
            src_ref=slot(a, block) if src is None else src, dst_ref=slot(a, block),
            send_sem=send_sems.at[7 * a + k], recv_sem=recv_sems.at[7 * a + k],
            device_id=to, device_id_type=MESH)

    def local(a):
        return pltpu.make_async_copy(srcs[a], slot(a, me), local_sems.at[a])

    def start(a):
        local(a).start()
        cps = [copy(a, 0, me, sibling, src=srcs[a])]
        cps += [copy(a, 1 + j, me, (*chip, c), src=srcs[a]) for j, chip in enumerate(chips)]
        for cp in cps:
            cp.start()
        return cps

    def finish(group):
        cps = []
        for j, chip in enumerate(chips):
            for a in group:
                copy(a, 1 + j, (*chip, c), me).wait_recv()
                cp = copy(a, 4 + j, (*chip, c), sibling)
                cp.start()
                cps.append(cp)
        for a in group:
            copy(a, 0, sibling, me).wait_recv()
            for j, chip in enumerate(chips):
                copy(a, 4 + j, (*chip, 1 - c), me).wait_recv()
            local(a).wait()
        return cps

    return start, finish


def _gather_sems(m):
    return [pltpu.SemaphoreType.DMA((7 * m,)), pltpu.SemaphoreType.DMA((7 * m,)), pltpu.SemaphoreType.DMA((m,))]


def _all_gather(arrs, name, fused=None):
    n = len(arrs)
    cast = {a: e[1] for a, e in enumerate(arrs) if isinstance(e, tuple)}
    flip = {a for a, e in enumerate(arrs) if isinstance(e, tuple) and len(e) == 3}
    arrs = [e[0] if isinstance(e, tuple) else e for e in arrs]
    shapes = [(a.shape[-1], a.shape[0]) if i in flip else (a.shape[0], a.shape[-1]) for i, a in enumerate(arrs)]
    extra, fn, piece = fused if fused else ((), None, None)
    ne, m = len(extra), n + (1 if fused else 0)

    def body(*refs):
        ins, ex, outs = refs[:n], refs[n:n + ne], refs[n + ne:n + ne + m]
        rest = refs[n + ne + m:]
        staged = dict(zip(cast, rest))
        srcs = [staged.get(a, ins[a]) for a in range(n)] + ([rest[len(cast)]] if fused else [])
        start, finish = _gather_program(srcs, outs, *rest[-3:])
        pending = []
        for a in range(n):
            if a in staged:
                val = ins[a][:, 0, :] if len(ins[a].shape) == 3 else ins[a][...]
                staged[a][...] = (val.T if a in flip else val).astype(cast[a])
            pending += start(a)
        if fused:
            pending += finish([0])
            fn(srcs[n], outs[0], *ex)
            pending += start(n)
            pending += finish([n] + list(range(1, n)))
        else:
            pending += finish(list(range(n)))
        for cp in pending:
            cp.wait_send()

    out_shape = [jax.ShapeDtypeStruct((N_DEV,) + shapes[a], cast.get(a, arrs[a].dtype)) for a in range(n)]
    scratch = [pltpu.VMEM(shapes[a], cast[a]) for a in cast]
    if fused:
        out_shape.append(jax.ShapeDtypeStruct((N_DEV,) + piece.shape, piece.dtype))
        scratch.append(pltpu.VMEM(piece.shape, piece.dtype))
    return pl.pallas_call(
        body, name=name, out_shape=out_shape,
        in_specs=[_vmem()] * (n + ne), out_specs=[_vmem()] * m,
        scratch_shapes=scratch + _gather_sems(m),
        compiler_params=pltpu.CompilerParams(vmem_limit_bytes=VMEM_LIMIT),
    )(*arrs, *extra)


def _reduce_scatter(arrs, name, gather=()):
    n, g = len(arrs), len(gather)
    halves = [a[1] if isinstance(a, tuple) else a.astype(jnp.bfloat16) for a in arrs]
    arrs = [a[0] if isinstance(a, tuple) else a for a in arrs]
    pieces = [(a, j * (x.shape[2] // m), x.shape[2] // m) for a, x in enumerate(arrs)
              for m in [2 if x.shape[2] >= 256 else 1] for j in range(m)]
    npc = len(pieces)

    def body(*refs):
        xs, xbs, gins = refs[:n], refs[n:2 * n], refs[2 * n:2 * n + g]
        outs, gouts = refs[2 * n + g:3 * n + g], refs[3 * n + g:3 * n + 2 * g]
        rest = refs[3 * n + 2 * g:]
        parts, recv_a, send_b, recv_b = (rest[t * n:(t + 1) * n] for t in range(4))
        send_sems, recv_sems, local_sems = rest[4 * n:4 * n + 3]
        x, y, c = lax.axis_index("x"), lax.axis_index("y"), lax.axis_index("c")
        myq = 2 * x + y
        gather_start, gather_finish = _gather_program(gins, gouts, *rest[4 * n + 3:])

        def chip(k):
            return (1 - x if k & 2 else x, 1 - y if k & 1 else y)

        def from_sibling(p):
            a, r0, rh = pieces[p]
            return pltpu.make_async_remote_copy(
                src_ref=xbs[a].at[:, 1 - c, r0:r0 + rh], dst_ref=recv_a[a].at[:, r0:r0 + rh],
                send_sem=send_sems.at[4 * p], recv_sem=recv_sems.at[4 * p],
                device_id=(x, y, 1 - c), device_id_type=MESH)

        def to_owner(p, k):
            a, r0, rh = pieces[p]
            qx, qy = chip(k)
            return pltpu.make_async_remote_copy(
                src_ref=send_b[a].at[2 * qx + qy, r0:r0 + rh], dst_ref=recv_b[a].at[myq, r0:r0 + rh],
                send_sem=send_sems.at[4 * p + k], recv_sem=recv_sems.at[4 * p + k],
                device_id=(qx, qy, c), device_id_type=MESH)

        mine = [pltpu.make_async_copy(xs[a].at[:, c], parts[a], local_sems.at[a]) for a in range(n)]
        first = [from_sibling(p) for p in range(npc)]
        for cp in mine + first:
            cp.start()
        second = []
        for b in range(g):
            second += gather_start(b)
        for p, (a, r0, rh) in enumerate(pieces):
            if r0 == 0:
                mine[a].wait()
            first[p].wait_recv()
            rs = slice(r0, r0 + rh)
            parts[a][:, rs, :] = parts[a][:, rs, :] + recv_a[a][:, rs, :].astype(F32)
            send_b[a][:, rs, :] = parts[a][:, rs, :].astype(jnp.bfloat16)
            for k in range(1, 4):
                cp = to_owner(p, k)
                cp.start()
                second.append(cp)
        second += gather_finish(list(range(g)))
        for a in range(n):
            acc = parts[a][myq]
            for k in range(1, 4):
                for p in range(npc):
                    if pieces[p][0] == a:
                        to_owner(p, k).wait_recv()
                qx, qy = chip(k)
                acc = acc + recv_b[a][2 * qx + qy].astype(F32)
            outs[a][...] = acc
        for cp in first + second:
            cp.wait_send()

    quarter = lambda a, dt: pltpu.VMEM((4,) + a.shape[2:], dt)
    return pl.pallas_call(
        body, name=name,
        out_shape=[jax.ShapeDtypeStruct(a.shape[2:], F32) for a in arrs]
        + [jax.ShapeDtypeStruct((N_DEV,) + a.shape, a.dtype) for a in gather],
        in_specs=[pl.BlockSpec(memory_space=pl.ANY)] * (2 * n) + [_vmem()] * g, out_specs=[_vmem()] * (n + g),
        scratch_shapes=[quarter(a, F32) for a in arrs] + [quarter(a, jnp.bfloat16) for a in arrs] * 3
        + [pltpu.SemaphoreType.DMA((4 * npc,)), pltpu.SemaphoreType.DMA((4 * npc,)), pltpu.SemaphoreType.DMA((n,))]
        + _gather_sems(g),
        compiler_params=pltpu.CompilerParams(vmem_limit_bytes=VMEM_LIMIT),
    )(*arrs, *halves, *gather)


def _silu(t):
    return t * (1.0 / (1.0 + jnp.exp(-t)))


def _ada_cols(piece_ref, c_all_ref, w_ref, b_ref):
    w = _mx(w_ref[...])
    for d in range(N_DEV):
        piece_ref[d] = _dot(_mx(_silu(c_all_ref[d])), w) + b_ref[...]


def _rope_split(cd, sd):
    n = cd.shape[0]

    def expand(d):
        rep = jnp.broadcast_to(d[:, None, :], (n, 8, LANE)).reshape(8 * n, LANE)
        return pltpu.roll(rep, 0, 1, stride=16, stride_axis=0)

    c, s = expand(cd), expand(sd)
    lane = _lane_iota(c.shape)
    first = jnp.logical_and(lane >= 64, lane < 80)
    second = jnp.logical_and(lane >= 80, lane < 96)
    ck = jnp.where(first, pltpu.roll(c, 80, 1), jnp.where(second, pltpu.roll(c, 96, 1), 0.0))
    cq = jnp.where(lane < 64, 1.0, ck)
    sa = jnp.where(first, -pltpu.roll(s, 80, 1), 0.0)
    sb = jnp.where(second, pltpu.roll(s, 96, 1), 0.0)
    return cq, ck, sa, sb


def _tile_heads(t):
    return jnp.concatenate([t] * MLA_HEADS, axis=1)


def _pre_fwd(x2, shift, scale, ng, w_in, gq, gkv, w_uq, w_ukv, rope, S, ts):
    T, D = x2.shape
    nsb = S // ts
    WQ = MLA_HEADS * LANE

    def body(x_ref, sh_ref, sc_ref, ng_ref, wn_ref, gq_ref, gkv_ref, uqn_ref, ukvn_ref,
             cd_ref, sd_ref,
             hb_ref, zq_ref, zkv_ref, ql_ref, kvl_ref, q_ref, k_ref, v_ref, qs_ref, ks_ref, vs_ref, g_ref, wp_ref,
             wuq_ref, wuk_ref, wuv_ref):
        @pl.when(pl.program_id(0) == 0)
        def _():
            for h in range(MLA_HEADS):
                wuq_ref[:, LANE * h:LANE * h + MLA_QK] = uqn_ref[h]
                wuq_ref[:, LANE * h + MLA_QK:LANE * (h + 1)] = jnp.zeros((Q_LORA, LANE - MLA_QK), wuq_ref.dtype)
                wuk_ref[:, LANE * h:LANE * h + MLA_NOPE] = ukvn_ref[h, :, 0:MLA_NOPE]
                wuk_ref[:, LANE * h + MLA_NOPE:LANE * (h + 1)] = jnp.zeros((KV_LORA, LANE - MLA_NOPE), wuk_ref.dtype)
                wuv_ref[:, 64 * h:64 * (h + 1)] = ukvn_ref[h, :, MLA_NOPE:LANE]
            wp_ref[640:704, :] = jnp.zeros((64, D), wp_ref.dtype)
            wp_ref[736:768, :] = jnp.zeros((32, D), wp_ref.dtype)
            for d in range(N_DEV):
                lo, hi = W_IN_SHARD * d, W_IN_SHARD * (d + 1)
                for nat, pad, size in _w_in_row_runs():
                    a, b = max(nat, lo), min(nat + size, hi)
                    if a < b:
                        wp_ref[pad + a - nat:pad + b - nat, :] = wn_ref[d, a - lo:b - lo, :]

        w_ref = wp_ref
        x = x_ref[...]
        r1 = lax.rsqrt(jnp.mean(x * x, axis=-1, keepdims=True) + EPS)
        h = ((x * r1) * ng_ref[...]) * (1.0 + sc_ref[0]) + sh_ref[0]
        hb = _mx(h)
        hb_ref[...] = hb
        zq = _dot_nt(hb, w_ref[0:384, :])
        zq_ref[...] = zq
        rq = lax.rsqrt(jnp.mean(zq * zq, axis=-1, keepdims=True) + EPS)
        ql = _mx((zq * rq) * gq_ref[...])
        ql_ref[...] = ql
        q = _dot(ql, wuq_ref[...])
        cq, ck, sa, sb = _rope_split(cd_ref[...], sd_ref[...])
        q = (q * _tile_heads(cq) + pltpu.roll(q, WQ - 16, 1) * _tile_heads(sa)
             + pltpu.roll(q, 16, 1) * _tile_heads(sb))
        q_ref[...] = _mx(q * MLA_QSCALE)
        zkv = _dot_nt(hb, w_ref[384:640, :])
        zkv_ref[...] = zkv
        rkv = lax.rsqrt(jnp.mean(zkv * zkv, axis=-1, keepdims=True) + EPS)
        kvl = _mx((zkv * rkv) * gkv_ref[...])
        kvl_ref[...] = kvl
        kr = _dot_nt(hb, w_ref[640:768, :])
        kpe = kr * ck + pltpu.roll(kr, LANE - 16, 1) * sa + pltpu.roll(kr, 16, 1) * sb
        kf = _dot(kvl, wuk_ref[...])
        k_ref[...] = _mx(kf + jnp.concatenate([kpe] * MLA_HEADS, axis=1))
        v_ref[...] = _mx(_dot(kvl, wuv_ref[...]))
        g_ref[:, 0:512] = _dot_nt(hb, w_ref[768:1280, :])
        qs_ref[...] = _mx(_dot_nt(hb, w_ref[1280:1792, :]) * (SWA_SCALE * LOG2E))
        ks_ref[...] = _mx(_dot_nt(hb, w_ref[1792:1920, :]))
        vs_ref[...] = _mx(_dot_nt(hb, w_ref[1920:2048, :]))
        g_ref[:, 512:1024] = _dot_nt(hb, w_ref[2048:2560, :])

    row = lambda w: pl.BlockSpec((ts, w), lambda i: (i, 0))
    dense = pl.BlockSpec((ts // 8, LANE), lambda i: (i, 0))
    full = lambda a: pl.BlockSpec(a.shape, lambda i: (0,) * a.ndim)
    per_b = pl.BlockSpec((1, 1, D), lambda i: (i // nsb, 0, 0))
    out_w = [(D, _MXU_DTYPE), (384, F32), (256, F32), (384, _MXU_DTYPE), (256, _MXU_DTYPE), (WQ, _MXU_DTYPE),
             (WQ, _MXU_DTYPE), (512, _MXU_DTYPE), (512, _MXU_DTYPE), (128, _MXU_DTYPE), (128, _MXU_DTYPE), (1024, F32)]
    built = [(D_IN_PAD, D), (Q_LORA, WQ), (KV_LORA, WQ), (KV_LORA, 512)]
    return pl.pallas_call(
        body, name="pre_fwd", grid=(T // ts,),
        out_shape=[jax.ShapeDtypeStruct((T, w), dt) for w, dt in out_w]
        + [jax.ShapeDtypeStruct(s, w_in.dtype) for s in built],
        in_specs=[row(D), per_b, per_b, full(ng), full(w_in), full(gq), full(gkv), full(w_uq), full(w_ukv), dense, dense],
        out_specs=[row(w) for w, _ in out_w] + [pl.BlockSpec(s, lambda i: (0, 0)) for s in built],
        compiler_params=_cparams(("arbitrary",)),
    )(x2, shift, scale, ng, w_in, gq, gkv, w_uq, w_ukv, *rope)


def _mla_fwd(q3, k3, v3, tq, tk):
    B, S, _ = q3.shape
    nq = S // tq
    assert tq == tk
    HPS = 2
    NH = 2 * HPS

    def body(q_ref, k_ref, v_ref, o_ref, lse_ref):
        rows = lax.broadcasted_iota(jnp.int32, (tq, tk), 0)
        cols = lax.broadcasted_iota(jnp.int32, (tq, tk), 1)
        low_k = _lane_iota((tk, LANE)) < 64
        low = _lane_iota((tq, LANE)) < 64

        def step(qs, kt, carry, masked):
            r0 = kt * tk
            out = []
            for j in range(NH):
                v2 = v_ref[0, r0:r0 + tk, LANE * (j // 2):LANE * (j // 2 + 1)]
                vj = (jnp.where(low_k, v2, 1.0) if j % 2 == 0 else jnp.where(low_k, 1.0, v2)).astype(v2.dtype)
                m, acc = carry[j]
                s = _dot_nt(qs[j], k_ref[0, r0:r0 + tk, LANE * j:LANE * (j + 1)])
                if masked:
                    s = jnp.where(rows >= cols, s, NEG_INF)
                m_new = jnp.maximum(m, jnp.max(s, axis=1, keepdims=True))
                alpha = jnp.exp2(m - m_new)
                p = jnp.exp2(s - m_new)
                acc = alpha * acc + _dot(_mx(p), vj)
                out.append((m_new, acc))
            return tuple(out)

        for qi in range(nq):
            r = slice(qi * tq, (qi + 1) * tq)
            qs = [q_ref[0, r, LANE * j:LANE * (j + 1)] for j in range(NH)]
            init = (jnp.full((tq, 1), NEG_INF, F32), jnp.zeros((tq, LANE), F32))
            carry = (init,) * NH
            for kt in range(qi):
                carry = step(qs, kt, carry, False)
            carry = step(qs, qi, carry, True)
            for t in range(HPS):
                (m0, a0), (m1, a1) = carry[2 * t], carry[2 * t + 1]
                l0 = jnp.where(low, pltpu.roll(a0, 64, 1), a0)
                l1 = jnp.where(low, a1, pltpu.roll(a1, 64, 1))
                o_ref[0, r, LANE * t:LANE * (t + 1)] = jnp.where(low, a0 / l0, a1 / l1)
                lse_ref[0, 2 * t, r, :] = m0 + jnp.log2(l0)
                lse_ref[0, 2 * t + 1, r, :] = m1 + jnp.log2(l1)

    grp = lambda w: pl.BlockSpec((1, S, w), lambda b, hp: (b, 0, hp))
    return pl.pallas_call(
        body, name="mla_fwd", grid=(B, MLA_HEADS // NH),
        out_shape=[jax.ShapeDtypeStruct((B, S, 512), F32), jax.ShapeDtypeStruct((B, MLA_HEADS, S, LANE), F32)],
        in_specs=[grp(NH * LANE), grp(NH * LANE), grp(HPS * LANE)],
        out_specs=[grp(HPS * LANE), pl.BlockSpec((1, NH, S, LANE), lambda b, hp: (b, hp, 0, 0))],
        compiler_params=_cparams(("arbitrary", "arbitrary")),
    )(q3, k3, v3)


def _mla_bwd(q3, k3, v3, o3, do3, lse, tq, tk, hps=2):
    B, S, _ = q3.shape
    nq, nk = S // tq, S // tk
    nh = 2 * hps
    assert tk % tq == 0

    def body(q_ref, k_ref, v_ref, o_ref, do_ref, lse_ref, dq_ref, dk_ref, dv_ref, dkt_ref, dvt_ref):
        rows = lax.broadcasted_iota(jnp.int32, (tq, tk), 0)
        cols = lax.broadcasted_iota(jnp.int32, (tq, tk), 1)
        lane = _lane_iota((tq, LANE))

        def q_tile(qi, _):
            r = pl.ds(pl.multiple_of(qi * tq, tq), tq)
            heads = []
            for j in range(nh):
                lanes = slice(LANE * j, LANE * (j + 1))
                pair = slice(LANE * (j // 2), LANE * (j // 2 + 1))
                do = jnp.where((lane < 64) if j % 2 == 0 else (lane >= 64), do_ref[0, r, pair], 0.0)
                q = q_ref[0, r, lanes]
                heads.append((lanes, pair, q, _mx(q.astype(F32).T), _mx(do), _mx(do.T),
                              jnp.sum(do * o_ref[0, r, pair], axis=1, keepdims=True),
                              jnp.concatenate([lse_ref[0, j, r, :]] * (tk // LANE), axis=1)))
            n_full = (qi * tq) // tk

            def k_tile(kt, dqs, masked):
                kr = pl.ds(pl.multiple_of(kt * tk, tk), tk)
                first = masked and (qi * tq) % tk == 0
                out = []
                dvt = [None] * hps
                for j, (lanes, pair, q, qt, dob, dot_, dcol, lse_c) in enumerate(heads):
                    k = k_ref[0, kr, lanes]
                    s = _dot_nt(q, k)
                    if masked:
                        s = jnp.where(rows + qi * tq >= cols + kt * tk, s, NEG_INF)
                    p = jnp.exp2(s - lse_c)
                    dp = _dot_nt(dob, v_ref[0, kr, pair])
                    dsb = _mx(p * (dp - dcol))
                    if first:
                        dkt_ref[j, kt] = _dot(qt, dsb)
                    else:
                        dkt_ref[j, kt] += _dot(qt, dsb)
                    pv = _dot(dot_, _mx(p))
                    dvt[j // 2] = pv if dvt[j // 2] is None else dvt[j // 2] + pv
                    out.append(dqs[j] + _dot(dsb, k))
                for t in range(hps):
                    if first:
                        dvt_ref[t, kt] = dvt[t]
                    else:
                        dvt_ref[t, kt] += dvt[t]
                return tuple(out)

            dqs = (jnp.zeros((tq, LANE), F32),) * nh
            for kt in range(n_full):
                dqs = k_tile(kt, dqs, False)
            dqs = k_tile(n_full, dqs, True)
            for j in range(nh):
                dq_ref[0, r, heads[j][0]] = (MLA_SCALE * dqs[j]).astype(dq_ref.dtype)
            return 0

        for qi in range(nq):
            q_tile(qi, 0)

        def flush(kt, _):
            kr = pl.ds(pl.multiple_of(kt * tk, tk), tk)
            for j in range(nh):
                dk_ref[0, kr, LANE * j:LANE * (j + 1)] = ((1.0 / LOG2E) * dkt_ref[j, kt].T).astype(dk_ref.dtype)
            for t in range(hps):
                dv_ref[0, kr, LANE * t:LANE * (t + 1)] = dvt_ref[t, kt].T.astype(dv_ref.dtype)
            return 0

        lax.fori_loop(0, nk, flush, 0)

    grp = lambda w: pl.BlockSpec((1, S, w), lambda b, hp: (b, 0, hp))
    return pl.pallas_call(
        body, name="mla_bwd", grid=(B, MLA_HEADS // nh),
        out_shape=[jax.ShapeDtypeStruct((B, S, 1024), MLA_GRAD_DTYPE), jax.ShapeDtypeStruct((B, S, 1024), MLA_GRAD_DTYPE),
                   jax.ShapeDtypeStruct((B, S, 512), MLA_GRAD_DTYPE)],
        in_specs=[grp(nh * LANE), grp(nh * LANE), grp(hps * LANE), grp(hps * LANE), grp(hps * LANE),
                  pl.BlockSpec((1, nh, S, LANE), lambda b, hp: (b, hp, 0, 0))],
        out_specs=[grp(nh * LANE), grp(nh * LANE), grp(hps * LANE)],
        scratch_shapes=[pltpu.VMEM((nh, nk, LANE, tk), F32), pltpu.VMEM((hps, nk, LANE, tk), F32)],
        compiler_params=_cparams(("arbitrary", "arbitrary")),
    )(q3, k3, v3, o3, do3, lse)


def _swa_consts(sink_ref):
    W = WINDOW
    row = lax.broadcasted_iota(jnp.int32, (4 * W, LANE), 0)
    out = []
    for g in range(SWA_KV_HEADS):
        slope = jnp.zeros((4 * W, LANE), F32)
        sink = jnp.zeros((4 * W, LANE), F32)
        for p in range(4):
            h = p + 4 * g
            here = jnp.logical_and(row >= W * p, row < W * (p + 1))
            slope = jnp.where(here, float(LOG2E * 2.0 ** (-8.0 * (h + 1) / SWA_HEADS)), slope)
            sink = jnp.where(here, LOG2E * sink_ref[:, h:h + 1], sink)
        out.append((jnp.concatenate([slope, slope], axis=1), sink))
    return out


def _wide(col):
    return jnp.concatenate([col, col], axis=1)


def _swa_block(q_ref, k_ref, v_ref, pr_ref, n, i, ij):
    W = WINDOW
    kb = jnp.maximum(n - 1, 0)
    r = pl.ds(pl.multiple_of(i * W, W), W)
    kr = pl.ds(pl.multiple_of(kb * W, W), 2 * W)
    q4, k2, v2 = q_ref[0, r, :], k_ref[0, kr, :], v_ref[0, kr, :]
    pq = _wide(jnp.broadcast_to(pr_ref[0, pl.ds(n, 1), :], (W, W)).T)
    pk = jnp.concatenate([pr_ref[0, pl.ds(kb, 1), :], pr_ref[0, pl.ds(kb + 1, 1), :]], axis=1)
    rel = ij + (n - kb) * W
    dist = jnp.where(jnp.logical_and(rel >= 0, rel < W), pq - pk, float("inf"))
    return r, kb, kr, q4, k2, v2, jnp.concatenate([dist] * 4, axis=0)


def _swa_stack(x4, g, dtype):
    lane = _lane_iota((WINDOW, LANE))
    mine = (lane < 64) if g == 0 else (lane >= 64)
    return jnp.concatenate([jnp.where(mine, x4[:, LANE * p:LANE * (p + 1)], 0).astype(dtype) for p in range(4)], axis=0)


def _swa_unstack(ref, r, lo, hi, scale=None):
    W = WINDOW
    low = _lane_iota((W, LANE)) < 64
    for p in range(4):
        t = jnp.where(low, lo[W * p:W * (p + 1)], hi[W * p:W * (p + 1)])
        ref[0, r, LANE * p:LANE * (p + 1)] = t if scale is None else scale * t


def _swa_fwd(qs3, ks3, vs3, posr, sinks):
    B, S, _ = qs3.shape
    W = WINDOW
    nb = S // W
    nh = SWA_SEQ_SPLIT
    nbh = nb // nh

    def body(q_ref, k_ref, v_ref, pr_ref, sink_ref, o_ref, lse_ref):
        ij = lax.broadcasted_iota(jnp.int32, (W, 2 * W), 0) - lax.broadcasted_iota(jnp.int32, (W, 2 * W), 1)
        consts = _swa_consts(sink_ref)
        n0 = pl.program_id(1) * nbh

        def blk(i, _):
            n = n0 + i
            r, _, _, q4, k2, v2, dist4 = _swa_block(q_ref, k_ref, v_ref, pr_ref, n, i, ij)
            o_g = []
            for g, (slope, sink) in enumerate(consts):
                s = _dot_nt(_swa_stack(q4, g, q4.dtype), k2) - slope * dist4
                m = jnp.maximum(jnp.max(s, axis=1, keepdims=True), sink)
                e = jnp.exp2(s - _wide(m))
                l = jnp.sum(e, axis=1, keepdims=True) + jnp.exp2(sink - m)
                o_g.append(_dot(_mx(e), v2) * (1.0 / l))
                lse_ref[0, i, g] = m + jnp.log2(l)
            _swa_unstack(o_ref, r, o_g[0], o_g[1])
            return 0

        for i in range(nbh):
            blk(i, 0)

    seq = lambda w: pl.BlockSpec((1, S, w), lambda b, h: (b, 0, 0))
    part = lambda w: pl.BlockSpec((1, S // nh, w), lambda b, h: (b, h, 0))
    lse_spec = pl.BlockSpec((1, nbh, 2, 4 * W, LANE), lambda b, h: (b, h, 0, 0, 0))
    return pl.pallas_call(
        body, name="swa_fwd", grid=(B, nh),
        out_shape=[jax.ShapeDtypeStruct((B, S, 512), F32), jax.ShapeDtypeStruct((B, nb, 2, 4 * W, LANE), F32)],
        in_specs=[part(512), seq(LANE), seq(LANE), pl.BlockSpec((1, nb, W), lambda b, h: (b, 0, 0)),
                  pl.BlockSpec((1, LANE), lambda b, h: (0, 0))],
        out_specs=[part(512), lse_spec],
        compiler_params=_cparams(("arbitrary", "arbitrary")),
    )(qs3, ks3, vs3, posr, sinks)


def _swa_bwd(qs3, ks3, vs3, posr, sinks, os3, do3, lse):
    B, S, _ = qs3.shape
    W = WINDOW
    nb = S // W
    nh = SWA_SEQ_SPLIT
    nbh = nb // nh
    assert nbh % 2 == 0 and nbh * nh * W == S

    def body(q_ref, k_ref, v_ref, pr_ref, sink_ref, o_ref, do_ref, lse_ref, dq_ref, dk_ref, dv_ref, dsink_ref,
             dkt_ref, dvt_ref):
        lane1 = _lane_iota((1, LANE))
        ij = lax.broadcasted_iota(jnp.int32, (W, 2 * W), 0) - lax.broadcasted_iota(jnp.int32, (W, 2 * W), 1)
        consts = _swa_consts(sink_ref)
        hh = pl.program_id(1)
        n0 = hh * nbh

        @pl.when(hh == 0)
        def _():
            dkt_ref[...] = jnp.zeros_like(dkt_ref)
            dvt_ref[...] = jnp.zeros_like(dvt_ref)

        @pl.when(jnp.logical_and(pl.program_id(0) == 0, hh == 0))
        def _():
            dsink_ref[...] = jnp.zeros_like(dsink_ref)

        def blk(i, dsink):
            n = n0 + i
            r, kb, _, q4, k2, v2, dist4 = _swa_block(q_ref, k_ref, v_ref, pr_ref, n, i, ij)
            o4, do4 = o_ref[0, r, :], do_ref[0, r, :]
            dq_g = []
            dkt = jnp.zeros((LANE, 2 * W), F32)
            dvt = jnp.zeros((LANE, 2 * W), F32)
            for g, (slope, sink) in enumerate(consts):
                q_st = _swa_stack(q4, g, F32)
                do_st = _swa_stack(do4, g, F32)
                dcol = jnp.sum(do_st * _swa_stack(o4, g, F32), axis=1, keepdims=True)
                qst, dob = _mx(q_st), _mx(do_st)
                lse_c = lse_ref[0, i, g]
                pr = jnp.exp2(_dot_nt(qst, k2) - slope * dist4 - _wide(lse_c))
                dsb = _mx(pr * (_dot_nt(dob, v2) - dcol))
                psd = jnp.exp2(sink - lse_c)[:, 0:1] * dcol
                for p in range(4):
                    dsink = dsink - jnp.where(lane1 == p + 4 * g,
                                              jnp.sum(psd[W * p:W * (p + 1)], axis=0, keepdims=True), 0.0)
                dq_g.append(_dot(dsb, k2))
                dkt = dkt + _dot(_mx(q_st.T), dsb)
                dvt = dvt + _dot(_mx(do_st.T), _mx(pr))
            _swa_unstack(dq_ref, r, dq_g[0], dq_g[1], SWA_SCALE)
            dkt_ref[kb] += dkt[:, 0:W]
            dkt_ref[kb + 1] += dkt[:, W:2 * W]
            dvt_ref[kb] += dvt[:, 0:W]
            dvt_ref[kb + 1] += dvt[:, W:2 * W]
            return dsink

        dsink = jnp.zeros((1, LANE), F32)
        for i in range(nbh):
            dsink = blk(i, dsink)
        dsink_ref[...] += dsink

        @pl.when(hh == nh - 1)
        def _():
            def flush(n, _):
                r = pl.ds(pl.multiple_of(n * W, W), W)
                dk_ref[0, r, :] = (1.0 / LOG2E) * dkt_ref[n].T
                dv_ref[0, r, :] = dvt_ref[n].T
                return 0

            lax.fori_loop(0, nb, flush, 0)

    seq = lambda w: pl.BlockSpec((1, S, w), lambda b, h: (b, 0, 0))
    part = lambda w: pl.BlockSpec((1, S // nh, w), lambda b, h: (b, h, 0))
    return pl.pallas_call(
        body, name="swa_bwd", grid=(B, nh),
        out_shape=[jax.ShapeDtypeStruct((B, S, 512), F32), jax.ShapeDtypeStruct((B, S, LANE), F32),
                   jax.ShapeDtypeStruct((B, S, LANE), F32), jax.ShapeDtypeStruct((1, LANE), F32)],
        in_specs=[part(512), seq(LANE), seq(LANE), pl.BlockSpec((1, nb, W), lambda b, h: (b, 0, 0)),
                  pl.BlockSpec((1, LANE), lambda b, h: (0, 0)), part(512),
                  pl.BlockSpec((1, S // nh, 512), lambda b, h: (b, h, 1)),
                  pl.BlockSpec((1, nbh, 2, 4 * W, LANE), lambda b, h: (b, h, 0, 0, 0))],
        out_specs=[part(512), seq(LANE), seq(LANE), pl.BlockSpec((1, LANE), lambda b, h: (0, 0))],
        scratch_shapes=[pltpu.VMEM((nb, LANE, W), F32), pltpu.VMEM((nb, LANE, W), F32)],
        compiler_params=_cparams(("arbitrary", "arbitrary")),
    )(qs3, ks3, vs3, posr, sinks, os3, do3, lse)


def _post(om, osw, g, w_out, x2, gate, fg, tgt, S, ts):
    T, D = x2.shape
    nsb = S // ts

    def body(om_ref, os_ref, g_ref, w_ref, x_ref, gate_ref, fg_ref, t_ref,
             dx2_ref, do_ref, dg_ref, loss_ref, dfg_ref, dgate_ref, dw_ref, dwb_ref):
        i = pl.program_id(0)
        gv = g_ref[...]
        sg = 1.0 / (1.0 + jnp.exp(-gv))
        silu = gv * sg
        o = jnp.concatenate([om_ref[...], os_ref[...]], axis=1)
        ab = _mx(o * silu)
        y = _dot(ab, w_ref[...])
        gate = gate_ref[0]
        xo = x_ref[...] + gate * y
        r2 = lax.rsqrt(jnp.mean(xo * xo, axis=-1, keepdims=True) + EPS)
        xh = xo * r2
        fg = fg_ref[...]
        diff = xh * fg - t_ref[...]
        sq = jnp.sum(diff * diff, axis=0, keepdims=True)
        part = sq[:, 0:LANE]
        for t in range(1, D // LANE):
            part = part + sq[:, LANE * t:LANE * (t + 1)]
        dout = diff * (1.0 / D)
        dxh = dout * fg
        dx2 = r2 * (dxh - xh * jnp.mean(dxh * xh, axis=-1, keepdims=True))
        dx2_ref[...] = dx2
        dyb = _mx(dx2 * gate)
        da = _dot_nt(dyb, w_ref[...])
        do_ref[...] = da * silu
        dg_ref[...] = _mx(da * o * (sg * (1.0 + gv * (1.0 - sg))))

        @pl.when(i == 0)
        def _():
            loss_ref[...] = jnp.zeros_like(loss_ref)
            dfg_ref[...] = jnp.zeros_like(dfg_ref)
            dw_ref[...] = jnp.zeros_like(dw_ref)

        @pl.when(i % nsb == 0)
        def _():
            dgate_ref[...] = jnp.zeros_like(dgate_ref)

        loss_ref[...] += (0.5 / D) * part
        dfg_ref[...] += jnp.sum(dout * xh, axis=0, keepdims=True)
        dgate_ref[0] += jnp.sum(dx2 * y, axis=0, keepdims=True)
        dw_ref[...] += _dot_tn(ab, dyb)

        @pl.when(i == T // ts - 1)
        def _():
            chunks = [dw_ref[512 + 64 * PAIR_INV[h]:512 + 64 * (PAIR_INV[h] + 1), :] for h in range(SWA_HEADS)]
            for h in range(SWA_HEADS):
                dw_ref[512 + 64 * h:512 + 64 * (h + 1), :] = chunks[h]
            dwb_ref[...] = dw_ref[...].astype(jnp.bfloat16)

    row = lambda w: pl.BlockSpec((ts, w), lambda i: (i, 0))
    full = lambda a: pl.BlockSpec(a.shape, lambda i: (0,) * a.ndim)
    per_b = pl.BlockSpec((1, 1, D), lambda i: (i // nsb, 0, 0))
    return pl.pallas_call(
        body, name="post", grid=(T // ts,),
        out_shape=[jax.ShapeDtypeStruct((T, D), F32), jax.ShapeDtypeStruct((T, 1024), F32),
                   jax.ShapeDtypeStruct((T, 1024), _MXU_DTYPE), jax.ShapeDtypeStruct((1, LANE), F32),
                   jax.ShapeDtypeStruct((1, D), F32), jax.ShapeDtypeStruct(gate.shape, F32),
                   jax.ShapeDtypeStruct(w_out.shape, F32), jax.ShapeDtypeStruct(w_out.shape, jnp.bfloat16)],
        in_specs=[row(512), row(512), row(1024), full(w_out), row(D), per_b, full(fg), row(D)],
        out_specs=[row(D), row(1024), row(1024),
                   pl.BlockSpec((1, LANE), lambda i: (0, 0)), pl.BlockSpec((1, D), lambda i: (0, 0)), per_b,
                   full(w_out), full(w_out)],
        compiler_params=_cparams(("arbitrary",)),
    )(om, osw, g, w_out, x2, gate, fg, tgt)


def _pre_bwd(dq, dk, dv, dqs, dks, dvs, dg, zq, zkv, ql, kvl, x2, dx2, scale, ng, w_in, gq, gkv, w_uq, w_uk, w_uv,
             rope, S, ts):
    T, D = x2.shape
    nsb = S // ts
    WQ = MLA_HEADS * LANE

    def body(dq_ref, dk_ref, dv_ref, dqs_ref, dks_ref, dvs_ref, dg_ref, zq_ref, zkv_ref, ql_ref, kvl_ref, x_ref, dx2_ref,
             sc_ref,
             ng_ref, w_ref, gq_ref, gkv_ref, wuq_ref, wuk_ref, wuv_ref, cd_ref, sd_ref,
             gx_ref, dz_ref, dgq_ref, dgkv_ref, dng_ref, dsh_ref, dsc_ref, uq_ref, uqb_ref, ukv_ref, ukvb_ref,
             dwuq_ref, dwuk_ref, dwuv_ref):
        i = pl.program_id(0)

        @pl.when(i == 0)
        def _():
            dgq_ref[...] = jnp.zeros_like(dgq_ref)
            dgkv_ref[...] = jnp.zeros_like(dgkv_ref)
            dng_ref[...] = jnp.zeros_like(dng_ref)
            dwuq_ref[...] = jnp.zeros_like(dwuq_ref)
            dwuk_ref[...] = jnp.zeros_like(dwuk_ref)
            dwuv_ref[...] = jnp.zeros_like(dwuv_ref)

        @pl.when(i % nsb == 0)
        def _():
            dsh_ref[...] = jnp.zeros_like(dsh_ref)
            dsc_ref[...] = jnp.zeros_like(dsc_ref)

        def norm_bwd(z, dy, gain):
            r = lax.rsqrt(jnp.mean(z * z, axis=-1, keepdims=True) + EPS)
            zh = z * r
            dzh = dy * gain
            return r * (dzh - zh * jnp.mean(dzh * zh, axis=-1, keepdims=True)), jnp.sum(dy * zh, axis=0, keepdims=True)

        tables = _rope_split(cd_ref[...], sd_ref[...])
        ng = ng_ref[...]
        sc1 = 1.0 + sc_ref[0]

        def rows_chain(rs):
            cq, ck, sa, sb = (t[rs] for t in tables)
            dqr = dq_ref[rs, :].astype(F32)
            dqb = _mx(dqr * _tile_heads(cq) + pltpu.roll(dqr * _tile_heads(sa), 16, 1)
                      + pltpu.roll(dqr * _tile_heads(sb), WQ - 16, 1))
            p_uq = _dot_tn(ql_ref[rs, :], dqb)
            dzq, dgq = norm_bwd(zq_ref[rs, :], _dot_nt(dqb, wuq_ref[...]), gq_ref[...])
            dkr = dk_ref[rs, :].astype(F32)
            dkb = _mx(dkr)
            p_uk = _dot_tn(kvl_ref[rs, :], dkb)
            dvb = _mx(dv_ref[rs, :])
            p_uv = _dot_tn(kvl_ref[rs, :], dvb)
            dzkv, dgkv = norm_bwd(zkv_ref[rs, :], _dot_nt(dkb, wuk_ref[...]) + _dot_nt(dvb, wuv_ref[...]), gkv_ref[...])
            dkpe = dkr[:, 0:LANE]
            for h in range(1, MLA_HEADS):
                dkpe = dkpe + dkr[:, LANE * h:LANE * (h + 1)]
            dkro = dkpe * ck + pltpu.roll(dkpe * sa, 16, 1) + pltpu.roll(dkpe * sb, LANE - 16, 1)
            dz_ref[rs, 0:384] = _mx(dzq)
            dz_ref[rs, 384:640] = _mx(dzkv)
            dz_ref[rs, 640:768] = _mx(dkro)
            dz_ref[rs, 768:1280] = dg_ref[rs, 0:512]
            dz_ref[rs, 1280:1792] = _mx(dqs_ref[rs, :])
            dz_ref[rs, 1792:1920] = _mx(dks_ref[rs, :])
            dz_ref[rs, 1920:2048] = _mx(dvs_ref[rs, :])
            dz_ref[rs, 2048:2560] = dg_ref[rs, 512:1024]
            dh = _dot(dz_ref[rs, :], w_ref[...])
            x = x_ref[rs, :]
            r1 = lax.rsqrt(jnp.mean(x * x, axis=-1, keepdims=True) + EPS)
            xn = x * r1
            dxn = dh * ng * sc1
            gx_ref[rs, :] = dx2_ref[rs, :] + r1 * (dxn - xn * jnp.mean(dxn * xn, axis=-1, keepdims=True))
            return (p_uq, p_uk, p_uv, dgq, dgkv, jnp.sum(dh, axis=0, keepdims=True),
                    jnp.sum(dh * (xn * ng), axis=0, keepdims=True), jnp.sum(dh * xn * sc1, axis=0, keepdims=True))

        hr = ts // 2
        parts = [rows_chain(slice(hr * t, hr * (t + 1))) for t in range(2)]
        p_uq, p_uk, p_uv, dgq, dgkv, dsh, dsc, dng = (a + b for a, b in zip(*parts))
        dwuq_ref[...] += p_uq
        dwuk_ref[...] += p_uk
        dwuv_ref[...] += p_uv
        dgq_ref[...] += dgq
        dgkv_ref[...] += dgkv
        dsh_ref[0] += dsh
        dsc_ref[0] += dsc
        dng_ref[...] += dng

        @pl.when(i == T // ts - 1)
        def _():
            for h in range(MLA_HEADS):
                uq = dwuq_ref[:, LANE * h:LANE * h + MLA_QK]
                ukv = jnp.concatenate([dwuk_ref[:, LANE * h:LANE * h + MLA_NOPE], dwuv_ref[:, 64 * h:64 * (h + 1)]], axis=1)
                uq_ref[h] = uq
                uqb_ref[h] = uq.astype(jnp.bfloat16)
                ukv_ref[h] = ukv
                ukvb_ref[h] = ukv.astype(jnp.bfloat16)

    row = lambda w: pl.BlockSpec((ts, w), lambda i: (i, 0))
    full = lambda a: pl.BlockSpec(a.shape, lambda i: (0,) * a.ndim, pipeline_mode=pl.Buffered(1))
    slot = lambda r, c, dt: (jax.ShapeDtypeStruct((MLA_HEADS, r, c), dt), pl.BlockSpec((MLA_HEADS, r, c), lambda i: (0, 0, 0)))
    slots = [slot(Q_LORA, MLA_QK, F32), slot(Q_LORA, MLA_QK, jnp.bfloat16), slot(KV_LORA, LANE, F32), slot(KV_LORA, LANE, jnp.bfloat16)]
    per_b = pl.BlockSpec((1, 1, D), lambda i: (i // nsb, 0, 0))
    dense = pl.BlockSpec((ts // 8, LANE), lambda i: (i, 0))
    vec = lambda w: pl.BlockSpec((1, w), lambda i: (0, 0))
    return pl.pallas_call(
        body, name="pre_bwd", grid=(T // ts,),
        out_shape=[jax.ShapeDtypeStruct((T, D), F32), jax.ShapeDtypeStruct((T, D_IN_PAD), _MXU_DTYPE), jax.ShapeDtypeStruct((1, 384), F32),
                   jax.ShapeDtypeStruct((1, 256), F32), jax.ShapeDtypeStruct((1, D), F32),
                   jax.ShapeDtypeStruct(scale.shape, F32), jax.ShapeDtypeStruct(scale.shape, F32)] + [t[0] for t in slots],
        in_specs=[row(WQ), row(WQ), row(512), row(512), row(LANE), row(LANE), row(1024), row(384), row(256), row(384),
                  row(256), row(D), row(D), per_b, full(ng), full(w_in), full(gq), full(gkv), full(w_uq), full(w_uk), full(w_uv),
                  dense, dense],
        out_specs=[row(D), row(D_IN_PAD), vec(384), vec(256), vec(D), per_b, per_b] + [t[1] for t in slots],
        scratch_shapes=[pltpu.VMEM(w_uq.shape, F32), pltpu.VMEM(w_uk.shape, F32), pltpu.VMEM(w_uv.shape, F32)],
        compiler_params=_cparams(("arbitrary",)),
    )(dq, dk, dv, dqs, dks, dvs, dg, zq, zkv, ql, kvl, x2, dx2, scale, ng, w_in, gq, gkv, w_uq, w_uk, w_uv, *rope)


def _w_in_row_runs():
    runs = [(0, 0, 640), (640, 704, 32), (672, 768, 512)]
    runs += [(1184 + 64 * h, 1280 + 64 * PAIR_INV[h], 64) for h in range(8)]
    runs += [(1696, 1792, 256)]
    runs += [(1952 + 64 * h, 2048 + 64 * PAIR_INV[h], 64) for h in range(8)]
    return runs


W_IN_SHARD = D_IN // N_DEV
W_IN_SLOT = 320


def _dw_in_t(dz, hb, tn, tk):
    T, M = dz.shape
    N = hb.shape[1]
    nk = T // tk

    def body(a_ref, b_ref, o_ref, ob_ref, acc_ref):
        k = pl.program_id(1)

        @pl.when(k == 0)
        def _():
            acc_ref[...] = jnp.zeros_like(acc_ref)

        acc_ref[...] += _dot_tn(a_ref[...], b_ref[...])

        @pl.when(k == nk - 1)
        def _():
            for d in range(N_DEV):
                lo, hi = W_IN_SHARD * d, W_IN_SHARD * (d + 1)
                for nat, pad, size in _w_in_row_runs():
                    a, b = max(nat, lo), min(nat + size, hi)
                    if a < b:
                        piece = acc_ref[pad + a - nat:pad + b - nat, :]
                        o_ref[d, a - lo:b - lo, :] = piece
                        ob_ref[d, a - lo:b - lo, :] = piece.astype(ob_ref.dtype)
                o_ref[d, W_IN_SHARD:W_IN_SLOT, :] = jnp.zeros((W_IN_SLOT - W_IN_SHARD, tn), F32)
                ob_ref[d, W_IN_SHARD:W_IN_SLOT, :] = jnp.zeros((W_IN_SLOT - W_IN_SHARD, tn), ob_ref.dtype)

    slots = pl.BlockSpec((N_DEV, W_IN_SLOT, tn), lambda j, k: (0, 0, j))
    return pl.pallas_call(
        body, name="dw_in", grid=(N // tn, nk),
        out_shape=[jax.ShapeDtypeStruct((N_DEV, W_IN_SLOT, N), F32), jax.ShapeDtypeStruct((N_DEV, W_IN_SLOT, N), jnp.bfloat16)],
        in_specs=[pl.BlockSpec((tk, M), lambda j, k: (k, 0)), pl.BlockSpec((tk, tn), lambda j, k: (k, j))],
        out_specs=[slots, slots],
        scratch_shapes=[pltpu.VMEM((M, tn), F32)],
        compiler_params=_cparams(("arbitrary", "arbitrary")),
    )(dz, hb)


SHARDED = (3, 6, 7, 9)
PART_SLICES = {10: (128, 1152), 2: (1152, 2176), 4: (2176, 2560), 5: (2560, 2816), 8: (2816, 2824)}


def _finalize_adamw(parts_all, dmod_all, dmod_cols, c_all, ws, sharded_grads, ms, vs):
    n = len(ws)
    local = [t for t in range(n) if t not in SHARDED[2:]]

    def body(*refs):
        p_ref, dm_ref, dmc_ref, c_ref = refs[:4]
        w_refs, gs_refs, m_refs, v_refs = refs[4:4 + n], refs[4 + n:8 + n], refs[8 + n:8 + 2 * n], refs[8 + 2 * n:8 + 3 * n]
        outs = refs[8 + 3 * n:]
        loss_ref, gl_refs = outs[0], outs[1:1 + len(local)]
        d_refs, nm_refs, nv_refs = (outs[1 + len(local) + t * n:1 + len(local) + (t + 1) * n] for t in range(3))
        ps_ref = outs[-1]
        rd = lambda r: r[:, 0, :] if len(r.shape) == 3 else r[...]

        def wr(r, val):
            if len(r.shape) == 3:
                r[:, 0, :] = val
            else:
                r[...] = val

        acc = p_ref[0]
        for j in range(1, N_DEV):
            acc = acc + p_ref[j]
        ps_ref[...] = acc
        loss_ref[...] = jnp.sum(acc[:, 0:LANE], axis=1, keepdims=True)
        db = dm_ref[0:1, :]
        for j in range(1, dm_ref.shape[0]):
            db = db + dm_ref[j:j + 1, :]
        grads = {0: _dot_tn(_mx(_silu(c_ref[...])), _mx(dmc_ref[...])), 1: db}
        for t, (lo, hi) in PART_SLICES.items():
            grads[t] = ps_ref[:, lo:hi]
        for i, t in enumerate(SHARDED):
            grads[t] = gs_refs[i][...]
        grads[SHARDED[0]] = gs_refs[0][0:W_IN_SHARD, :]
        grads[SHARDED[1]] = gs_refs[1][...].T
        for i, t in enumerate(local):
            wr(gl_refs[i], grads[t])
        for t in range(n):
            gv = grads[t]
            nm = ADAM_B1 * rd(m_refs[t]) + (1.0 - ADAM_B1) * gv
            nv = ADAM_B2 * rd(v_refs[t]) + (1.0 - ADAM_B2) * (gv * gv)
            m_hat = nm / (1.0 - ADAM_B1 ** ADAM_STEP)
            v_hat = nv / (1.0 - ADAM_B2 ** ADAM_STEP)
            wr(d_refs[t], -ADAM_LR * (m_hat / (jnp.sqrt(v_hat) + ADAM_EPS) + ADAM_WD * rd(w_refs[t])))
            wr(nm_refs[t], nm)
            wr(nv_refs[t], nv)

    like = lambda arrs: [jax.ShapeDtypeStruct(a.shape, F32) for a in arrs]
    out = pl.pallas_call(
        body, name="finalize_adamw",
        out_shape=[jax.ShapeDtypeStruct((1, 1), F32)] + like([ws[t] for t in local]) + like(ws) * 3,
        in_specs=[_vmem()] * (8 + 3 * n), out_specs=[_vmem()] * (1 + len(local) + 3 * n),
        scratch_shapes=[pltpu.VMEM(parts_all.shape[1:], F32)],
        compiler_params=_cparams(),
    )(parts_all, dmod_all, dmod_cols, c_all, *ws, *sharded_grads, *ms, *vs)
    k = 1 + len(local)
    return out[0], dict(zip(local, out[1:k])), out[k:k + n], out[k + n:k + 2 * n], out[k + 2 * n:]


def _pair_perm(a, axis, order):
    a = jnp.moveaxis(a, axis, -1)
    lead = a.shape[:-1]
    a = a.reshape(lead + (8, 64))[..., list(order), :].reshape(lead + (512,))
    return jnp.moveaxis(a, -1, axis)


def _rope_table(positions):
    T = positions.size
    inv = ROPE_THETA ** (-jnp.arange(0, MLA_ROPE, 2, dtype=F32) / MLA_ROPE)
    pos = jnp.repeat(positions.reshape(T // 8, 8)[:, ::-1].astype(F32), MLA_ROPE // 2, axis=1)
    ang = pos * jnp.tile(inv, 8)[None, :]
    return jnp.cos(ang), jnp.sin(ang)


def _local_step(x, mod, positions, ng, w_in_t, gq, gkv, w_uq, w_ukv, sinks, w_out, fg, tgt,
                ts=512, fq=512, fk=512, bq=512, bk=512):
    B, S, D = x.shape
    T = B * S
    x2 = x.reshape(T, D)
    shift, scale, gate = (mod[:, None, k * D:(k + 1) * D] for k in range(3))
    w_out_p = jnp.concatenate([w_out[:512], _pair_perm(w_out[512:], 0, PAIR_ORDER)], axis=0)
    rope = _rope_table(positions)
    posf = positions.astype(F32)
    posr = posf.reshape(B, S // WINDOW, WINDOW)
    sinks_l = jnp.pad(sinks.reshape(1, SWA_HEADS), ((0, 0), (0, LANE - SWA_HEADS)))

    (hb, zq, zkv, ql, kvl, q, k, v, qs, ks, vs, g, w_in_p, w_uq_p, w_uk_p, w_uv) = _pre_fwd(
        x2, shift, scale, ng, w_in_t, gq, gkv, w_uq, w_ukv, rope, S, ts)
    r3 = lambda a: a.reshape(B, S, a.shape[-1])
    om, lse_m = _mla_fwd(r3(q), r3(k), r3(v), fq, fk)
    osw, lse_s = _swa_fwd(r3(qs), r3(ks), r3(vs), posr, sinks_l)
    dx2, do, dg, loss_v, dfg, dgate, *dw_out = _post(
        om.reshape(T, 512), osw.reshape(T, 512), g, w_out_p, x2, gate, fg.reshape(1, D), tgt.reshape(T, D), S, ts)
    do3 = r3(do)
    dq, dk, dv = _mla_bwd(r3(q), r3(k), r3(v), om, do3, lse_m, bq, bk)
    dqs, dks, dvs, dsink = _swa_bwd(r3(qs), r3(ks), r3(vs), posr, sinks_l, osw, do3, lse_s)
    f2 = lambda a: a.reshape(T, a.shape[-1])
    gx, dz, dgq, dgkv, dng, dsh, dsc, *dw_u = _pre_bwd(
        f2(dq), f2(dk), f2(dv), f2(dqs), f2(dks), f2(dvs), dg, zq, zkv, ql, kvl, x2, dx2, scale, ng, w_in_p, gq, gkv,
        w_uq_p, w_uk_p, w_uv, rope, S, ts)
    tk = min(T, 1024)
    dw_in_t = _dw_in_t(dz, hb, 512, tk)
    parts = jnp.concatenate([loss_v, dfg, dng, dgq, dgkv, dsink], axis=1)
    dmod = jnp.concatenate([dsh, dsc, dgate], axis=2).reshape(B, 3 * D)
    return gx.reshape(B, S, D), dw_in_t, tuple(dw_u[0:2]), tuple(dw_u[2:4]), tuple(dw_out), parts, dmod


def kernel(x, c, positions, w_ada, b_ada, norm_gain, w_in, q_norm_gain, kv_norm_gain, w_uq, w_ukv, swa_sinks, w_out, final_gain, loss_target, m_w_ada, m_b_ada, m_norm_gain, m_w_in, m_q_norm_gain, m_kv_norm_gain, m_w_uq, m_w_ukv, m_swa_sinks, m_w_out, m_final_gain, v_w_ada, v_b_ada, v_norm_gain, v_w_in, v_q_norm_gain, v_kv_norm_gain, v_w_uq, v_w_ukv, v_swa_sinks, v_w_out, v_final_gain):
    B, S, D = x.shape
    me = 4 * lax.axis_index("x") + 2 * lax.axis_index("y") + lax.axis_index("c")
    bf = _MXU_DTYPE

    ncol = w_ada.shape[2]
    b_cols = lax.dynamic_slice_in_dim(b_ada, me * ncol, ncol, axis=1)
    c_all, win_g, wuq_g, wukv_g, wout_g, mod_g = _all_gather(
        [c, (jnp.transpose(w_in, (2, 0, 1)), bf), (jnp.transpose(w_uq[0]), bf, True), (w_ukv[0], bf), (w_out[0], bf)],
        "ag_weights",
        fused=((w_ada[0], b_cols), _ada_cols, jax.ShapeDtypeStruct((N_DEV, B, ncol), F32)))
    c_all = c_all.reshape(N_DEV * B, D)
    w_out_f = wout_g.reshape(D, D)

    mod = lax.dynamic_index_in_dim(mod_g, me, axis=1, keepdims=False)
    mod = jnp.transpose(mod, (1, 0, 2)).reshape(B, 3 * D)

    gx, dw_in_t, dw_uq, dw_ukv, dw_out, parts, dmod = _local_step(
        x, mod, positions, norm_gain, win_g, q_norm_gain, kv_norm_gain, wuq_g, wukv_g, swa_sinks,
        w_out_f, final_gain, loss_target)

    slots = lambda pair: tuple(a.reshape((4, 2, -1) + a.shape[-1:]) for a in pair)
    g_w_in_t, g_w_uq, g_w_ukv, g_w_out, parts_g, dmod_g = _reduce_scatter(
        [slots(dw_in_t), slots(dw_uq), slots(dw_ukv), slots(dw_out)], "rs_grads",
        gather=(parts, dmod))

    dmod_all = dmod_g.reshape(N_DEV * B, 3 * D)
    dmod_cols = lax.dynamic_slice_in_dim(dmod_all, me * ncol, ncol, axis=1)
    ws = [w_ada, b_ada, norm_gain, w_in, q_norm_gain, kv_norm_gain, w_uq, w_ukv, swa_sinks, w_out, final_gain]
    ms = [m_w_ada, m_b_ada, m_norm_gain, m_w_in, m_q_norm_gain, m_kv_norm_gain, m_w_uq, m_w_ukv, m_swa_sinks, m_w_out,
          m_final_gain]
    vs = [v_w_ada, v_b_ada, v_norm_gain, v_w_in, v_q_norm_gain, v_kv_norm_gain, v_w_uq, v_w_ukv, v_swa_sinks, v_w_out,
          v_final_gain]
    two_d = lambda a: a.reshape((1, a.shape[0]) if a.ndim == 1 else a.shape[-2:])
    view = {SHARDED[0]: lambda a: jnp.transpose(a, (2, 0, 1)), SHARDED[1]: lambda a: jnp.transpose(a[0])}
    flat = lambda arrs: [view[t](a) if t in view else two_d(a) for t, a in enumerate(arrs)]
    sharded = {3: g_w_in_t, 6: g_w_uq, 7: g_w_ukv, 9: g_w_out}
    loss, local_grads, deltas, new_ms, new_vs = _finalize_adamw(
        parts_g, dmod_all, dmod_cols, c_all, flat(ws), [sharded[t] for t in SHARDED], flat(ms), flat(vs))
    grads = [local_grads[t] if t in local_grads else sharded[t] for t in range(len(ws))]
    back = {SHARDED[0]: lambda a: jnp.transpose(a, (1, 2, 0)), SHARDED[1]: lambda a: jnp.transpose(a)[None]}
    shaped = lambda arrs: [back[t](a) if t in back else a.reshape(ws[t].shape) for t, a in enumerate(arrs)]
    return (loss.reshape(()), gx, *shaped(grads), *shaped(deltas), *shaped(new_ms), *shaped(new_vs))
```

```python
import functools

import jax
import jax.numpy as jnp
from jax import lax
from jax.experimental import pallas as pl
from jax.experimental.pallas import tpu as pltpu

F32 = jnp.float32
_MXU_DTYPE = jnp.bfloat16
MLA_GRAD_DTYPE = jnp.bfloat16

N_DEV = 8
MLA_HEADS = 8
MLA_NOPE = 64
MLA_ROPE = 32
MLA_QK = MLA_NOPE + MLA_ROPE
Q_LORA = 384
KV_LORA = 256
SWA_HEADS = 8
SWA_KV_HEADS = 2
SWA_HEAD_DIM = 64
WINDOW = 128
ROPE_THETA = 10000.0
EPS = 1e-6
MLA_SCALE = float((MLA_NOPE + MLA_ROPE) ** -0.5)
SWA_SCALE = float(SWA_HEAD_DIM ** -0.5)
LOG2E = 1.4426950408889634
MLA_QSCALE = MLA_SCALE * LOG2E
D_IN = 2464
D_IN_PAD = 2560
PAIR_ORDER = (0, 4, 1, 5, 2, 6, 3, 7)
PAIR_INV = (0, 2, 4, 6, 1, 3, 5, 7)

ADAM_LR = 0.001
ADAM_B1 = 0.9
ADAM_B2 = 0.999
ADAM_EPS = 1e-08
ADAM_WD = 0.01
ADAM_STEP = 10

LANE = 128
VMEM_LIMIT = 56 * 1024 * 1024

MESH = pl.DeviceIdType.MESH
NEG_INF = float("-inf")
SWA_SEQ_SPLIT = 2


def _mx(a):
    return a.astype(_MXU_DTYPE)


def _dot(a, b):
    return jnp.dot(a, b, preferred_element_type=F32)


def _dot_nt(a, b):
    return lax.dot_general(a, b, (((1,), (1,)), ((), ())), preferred_element_type=F32)


def _dot_tn(a, b):
    return lax.dot_general(a, b, (((0,), (0,)), ((), ())), preferred_element_type=F32)


def _cparams(sem=None):
    return pltpu.CompilerParams(dimension_semantics=sem, vmem_limit_bytes=VMEM_LIMIT)


def _vmem():
    return pl.BlockSpec(memory_space=pltpu.VMEM)


def _lane_iota(shape):
    return lax.broadcasted_iota(jnp.int32, shape, len(shape) - 1)


def _gather_program(srcs, outs, send_sems, recv_sems, local_sems):
    x, y, c = lax.axis_index("x"), lax.axis_index("y"), lax.axis_index("c")
    me, sibling = (x, y, c), (x, y, 1 - c)
    chips = [(1 - x, y), (x, 1 - y), (1 - x, 1 - y)]

    def slot(a, dev):
        return outs[a].at[4 * dev[0] + 2 * dev[1] + dev[2]]

    def copy(a, k, block, to, src=None):
        return pltpu.make_async_remote_copy(
            src_ref=slot(a, block) if src is None else src, dst_ref=slot(a, block),
            send_sem=send_sems.at[7 * a + k], recv_sem=recv_sems.at[7 * a + k],
            device_id=to, device_id_type=MESH)

    def local(a):
        return pltpu.make_async_copy(srcs[a], slot(a, me), local_sems.at[a])

    def start(a):
        local(a).start()
        cps = [copy(a, 0, me, sibling, src=srcs[a])]
        cps += [copy(a, 1 + j, me, (*chip, c), src=srcs[a]) for j, chip in enumerate(chips)]
        for cp in cps:
            cp.start()
        return cps

    def finish(group):
        cps = []
        for j, chip in enumerate(chips):
            for a in group:
                copy(a, 1 + j, (*chip, c), me).wait_recv()
                cp = copy(a, 4 + j, (*chip, c), sibling)
                cp.start()
                cps.append(cp)
        for a in group:
            copy(a, 0, sibling, me).wait_recv()
            for j, chip in enumerate(chips):
                copy(a, 4 + j, (*chip, 1 - c), me).wait_recv()
            local(a).wait()
        return cps

    return start, finish


def _gather_sems(m):
    return [pltpu.SemaphoreType.DMA((7 * m,)), pltpu.SemaphoreType.DMA((7 * m,)), pltpu.SemaphoreType.DMA((m,))]


def _all_gather(arrs, name, fused=None):
    n = len(arrs)
    cast = {a: e[1] for a, e in enumerate(arrs) if isinstance(e, tuple)}
    flip = {a for a, e in enumerate(arrs) if isinstance(e, tuple) and len(e) == 3}
    arrs = [e[0] if isinstance(e, tuple) else e for e in arrs]
    shapes = [(a.shape[-1], a.shape[0]) if i in flip else (a.shape[0], a.shape[-1]) for i, a in enumerate(arrs)]
    extra, fn, piece = fused if fused else ((), None, None)
    ne, m = len(extra), n + (1 if fused else 0)

    def body(*refs):
        ins, ex, outs = refs[:n], refs[n:n + ne], refs[n + ne:n + ne + m]
        rest = refs[n + ne + m:]
        staged = dict(zip(cast, rest))
        srcs = [staged.get(a, ins[a]) for a in range(n)] + ([rest[len(cast)]] if fused else [])
        start, finish = _gather_program(srcs, outs, *rest[-3:])
        pending = []
        for a in range(n):
            if a in staged:
                val = ins[a][:, 0, :] if len(ins[a].shape) == 3 else ins[a][...]
                staged[a][...] = (val.T if a in flip else val).astype(cast[a])
            pending += start(a)
        if fused:
            pending += finish([0])
            fn(srcs[n], outs[0], *ex)
            pending += start(n)
            pending += finish([n] + list(range(1, n)))
        else:
            pending += finish(list(range(n)))
        for cp in pending:
            cp.wait_send()

    out_shape = [jax.ShapeDtypeStruct((N_DEV,) + shapes[a], cast.get(a, arrs[a].dtype)) for a in range(n)]
    scratch = [pltpu.VMEM(shapes[a], cast[a]) for a in cast]
    if fused:
        out_shape.append(jax.ShapeDtypeStruct((N_DEV,) + piece.shape, piece.dtype))
        scratch.append(pltpu.VMEM(piece.shape, piece.dtype))
    return pl.pallas_call(
        body, name=name, out_shape=out_shape,
        in_specs=[_vmem()] * (n + ne), out_specs=[_vmem()] * m,
        scratch_shapes=scratch + _gather_sems(m),
        compiler_params=pltpu.CompilerParams(vmem_limit_bytes=VMEM_LIMIT),
    )(*arrs, *extra)


def _reduce_scatter(arrs, name, gather=()):
    n, g = len(arrs), len(gather)
    halves = [a[1] if isinstance(a, tuple) else a.astype(jnp.bfloat16) for a in arrs]
    arrs = [a[0] if isinstance(a, tuple) else a for a in arrs]
    pieces = [(a, j * (x.shape[2] // m), x.shape[2] // m) for a, x in enumerate(arrs)
              for m in [4 if x.shape[2] >= 320 else 2 if x.shape[2] >= 256 else 1] for j in range(m)]
    npc = len(pieces)

    def body(*refs):
        xs, xbs, gins = refs[:n], refs[n:2 * n], refs[2 * n:2 * n + g]
        outs, gouts = refs[2 * n + g:3 * n + g], refs[3 * n + g:3 * n + 2 * g]
        rest = refs[3 * n + 2 * g:]
        parts, recv_a, send_b, recv_b = (rest[t * n:(t + 1) * n] for t in range(4))
        send_sems, recv_sems, local_sems = rest[4 * n:4 * n + 3]
        x, y, c = lax.axis_index("x"), lax.axis_index("y"), lax.axis_index("c")
        myq = 2 * x + y
        gather_start, gather_finish = _gather_program(gins, gouts, *rest[4 * n + 3:])

        def chip(k):
            return (1 - x if k & 2 else x, 1 - y if k & 1 else y)

        def from_sibling(p):
            a, r0, rh = pieces[p]
            return pltpu.make_async_remote_copy(
                src_ref=xbs[a].at[:, 1 - c, r0:r0 + rh], dst_ref=recv_a[a].at[:, r0:r0 + rh],
                send_sem=send_sems.at[4 * p], recv_sem=recv_sems.at[4 * p],
                device_id=(x, y, 1 - c), device_id_type=MESH)

        def to_owner(p, k):
            a, r0, rh = pieces[p]
            qx, qy = chip(k)
            return pltpu.make_async_remote_copy(
                src_ref=send_b[a].at[2 * qx + qy, r0:r0 + rh], dst_ref=recv_b[a].at[myq, r0:r0 + rh],
                send_sem=send_sems.at[4 * p + k], recv_sem=recv_sems.at[4 * p + k],
                device_id=(qx, qy, c), device_id_type=MESH)

        mine = [pltpu.make_async_copy(xs[a].at[:, c], parts[a], local_sems.at[a]) for a in range(n)]
        first = [from_sibling(p) for p in range(npc)]
        for cp in mine + first:
            cp.start()
        second = []
        for b in range(g):
            second += gather_start(b)
        for p, (a, r0, rh) in enumerate(pieces):
            if r0 == 0:
                mine[a].wait()
            first[p].wait_recv()
            rs = slice(r0, r0 + rh)
            parts[a][:, rs, :] = parts[a][:, rs, :] + recv_a[a][:, rs, :].astype(F32)
            send_b[a][:, rs, :] = parts[a][:, rs, :].astype(jnp.bfloat16)
            for k in range(1, 4):
                cp = to_owner(p, k)
                cp.start()
                second.append(cp)
        second += gather_finish(list(range(g)))
        for a in range(n):
            acc = parts[a][myq]
            for k in range(1, 4):
                for p in range(npc):
                    if pieces[p][0] == a:
                        to_owner(p, k).wait_recv()
                qx, qy = chip(k)
                acc = acc + recv_b[a][2 * qx + qy].astype(F32)
            outs[a][...] = acc
        for cp in first + second:
            cp.wait_send()

    quarter = lambda a, dt: pltpu.VMEM((4,) + a.shape[2:], dt)
    return pl.pallas_call(
        body, name=name,
        out_shape=[jax.ShapeDtypeStruct(a.shape[2:], F32) for a in arrs]
        + [jax.ShapeDtypeStruct((N_DEV,) + a.shape, a.dtype) for a in gather],
        in_specs=[pl.BlockSpec(memory_space=pl.ANY)] * (2 * n) + [_vmem()] * g, out_specs=[_vmem()] * (n + g),
        scratch_shapes=[quarter(a, F32) for a in arrs] + [quarter(a, jnp.bfloat16) for a in arrs] * 3
        + [pltpu.SemaphoreType.DMA((4 * npc,)), pltpu.SemaphoreType.DMA((4 * npc,)), pltpu.SemaphoreType.DMA((n,))]
        + _gather_sems(g),
        compiler_params=pltpu.CompilerParams(vmem_limit_bytes=VMEM_LIMIT),
    )(*arrs, *halves, *gather)


def _silu(t):
    return t * (1.0 / (1.0 + jnp.exp(-t)))


def _ada_cols(piece_ref, c_all_ref, w_ref, b_ref):
    w = _mx(w_ref[...])
    for d in range(N_DEV):
        piece_ref[d] = _dot(_mx(_silu(c_all_ref[d])), w) + b_ref[...]


def _rope_split(cd, sd):
    n = cd.shape[0]

    def expand(d):
        rep = jnp.broadcast_to(d[:, None, :], (n, 8, LANE)).reshape(8 * n, LANE)
        return pltpu.roll(rep, 0, 1, stride=16, stride_axis=0)

    c, s = expand(cd), expand(sd)
    lane = _lane_iota(c.shape)
    first = jnp.logical_and(lane >= 64, lane < 80)
    second = jnp.logical_and(lane >= 80, lane < 96)
    ck = jnp.where(first, pltpu.roll(c, 80, 1), jnp.where(second, pltpu.roll(c, 96, 1), 0.0))
    cq = jnp.where(lane < 64, 1.0, ck)
    sa = jnp.where(first, -pltpu.roll(s, 80, 1), 0.0)
    sb = jnp.where(second, pltpu.roll(s, 96, 1), 0.0)
    return cq, ck, sa, sb


def _tile_heads(t):
    return jnp.concatenate([t] * MLA_HEADS, axis=1)


def _pre_fwd(x2, shift, scale, ng, w_in, gq, gkv, w_uq, w_ukv, rope, S, ts):
    T, D = x2.shape
    nsb = S // ts
    WQ = MLA_HEADS * LANE

    def body(x_ref, sh_ref, sc_ref, ng_ref, wn_ref, gq_ref, gkv_ref, uqn_ref, ukvn_ref,
             cd_ref, sd_ref,
             hb_ref, zq_ref, zkv_ref, ql_ref, kvl_ref, q_ref, k_ref, v_ref, qs_ref, ks_ref, vs_ref, g_ref, wp_ref,
             wuq_ref, wuk_ref, wuv_ref):
        @pl.when(pl.program_id(0) == 0)
        def _():
            for h in range(MLA_HEADS):
                wuq_ref[:, LANE * h:LANE * h + MLA_QK] = uqn_ref[h]
                wuq_ref[:, LANE * h + MLA_QK:LANE * (h + 1)] = jnp.zeros((Q_LORA, LANE - MLA_QK), wuq_ref.dtype)
                wuk_ref[:, LANE * h:LANE * h + MLA_NOPE] = ukvn_ref[h, :, 0:MLA_NOPE]
                wuk_ref[:, LANE * h + MLA_NOPE:LANE * (h + 1)] = jnp.zeros((KV_LORA, LANE - MLA_NOPE), wuk_ref.dtype)
                wuv_ref[:, 64 * h:64 * (h + 1)] = ukvn_ref[h, :, MLA_NOPE:LANE]
            wp_ref[640:704, :] = jnp.zeros((64, D), wp_ref.dtype)
            wp_ref[736:768, :] = jnp.zeros((32, D), wp_ref.dtype)
            for d in range(N_DEV):
                lo, hi = W_IN_SHARD * d, W_IN_SHARD * (d + 1)
                for nat, pad, size in _w_in_row_runs():
                    a, b = max(nat, lo), min(nat + size, hi)
                    if a < b:
                        wp_ref[pad + a - nat:pad + b - nat, :] = wn_ref[d, a - lo:b - lo, :]

        w_ref = wp_ref
        x = x_ref[...]
        r1 = lax.rsqrt(jnp.mean(x * x, axis=-1, keepdims=True) + EPS)
        h = ((x * r1) * ng_ref[...]) * (1.0 + sc_ref[0]) + sh_ref[0]
        hb = _mx(h)
        hb_ref[...] = hb
        zq = _dot_nt(hb, w_ref[0:384, :])
        zq_ref[...] = zq
        rq = lax.rsqrt(jnp.mean(zq * zq, axis=-1, keepdims=True) + EPS)
        ql = _mx((zq * rq) * gq_ref[...])
        ql_ref[...] = ql
        q = _dot(ql, wuq_ref[...])
        cq, ck, sa, sb = _rope_split(cd_ref[...], sd_ref[...])
        q = (q * _tile_heads(cq) + pltpu.roll(q, WQ - 16, 1) * _tile_heads(sa)
             + pltpu.roll(q, 16, 1) * _tile_heads(sb))
        q_ref[...] = _mx(q * MLA_QSCALE)
        zkv = _dot_nt(hb, w_ref[384:640, :])
        zkv_ref[...] = zkv
        rkv = lax.rsqrt(jnp.mean(zkv * zkv, axis=-1, keepdims=True) + EPS)
        kvl = _mx((zkv * rkv) * gkv_ref[...])
        kvl_ref[...] = kvl
        kr = _dot_nt(hb, w_ref[640:768, :])
        kpe = kr * ck + pltpu.roll(kr, LANE - 16, 1) * sa + pltpu.roll(kr, 16, 1) * sb
        kf = _dot(kvl, wuk_ref[...])
        k_ref[...] = _mx(kf + jnp.concatenate([kpe] * MLA_HEADS, axis=1))
        v_ref[...] = _mx(_dot(kvl, wuv_ref[...]))
        g_ref[:, 0:512] = _dot_nt(hb, w_ref[768:1280, :])
        qs_ref[...] = _mx(_dot_nt(hb, w_ref[1280:1792, :]) * (SWA_SCALE * LOG2E))
        ks_ref[...] = _mx(_dot_nt(hb, w_ref[1792:1920, :]))
        vs_ref[...] = _mx(_dot_nt(hb, w_ref[1920:2048, :]))
        g_ref[:, 512:1024] = _dot_nt(hb, w_ref[2048:2560, :])

    row = lambda w: pl.BlockSpec((ts, w), lambda i: (i, 0))
    dense = pl.BlockSpec((ts // 8, LANE), lambda i: (i, 0))
    full = lambda a: pl.BlockSpec(a.shape, lambda i: (0,) * a.ndim)
    per_b = pl.BlockSpec((1, 1, D), lambda i: (i // nsb, 0, 0))
    out_w = [(D, _MXU_DTYPE), (384, F32), (256, F32), (384, _MXU_DTYPE), (256, _MXU_DTYPE), (WQ, _MXU_DTYPE),
             (WQ, _MXU_DTYPE), (512, _MXU_DTYPE), (512, _MXU_DTYPE), (128, _MXU_DTYPE), (128, _MXU_DTYPE), (1024, F32)]
    built = [(D_IN_PAD, D), (Q_LORA, WQ), (KV_LORA, WQ), (KV_LORA, 512)]
    return pl.pallas_call(
        body, name="pre_fwd", grid=(T // ts,),
        out_shape=[jax.ShapeDtypeStruct((T, w), dt) for w, dt in out_w]
        + [jax.ShapeDtypeStruct(s, w_in.dtype) for s in built],
        in_specs=[row(D), per_b, per_b, full(ng), full(w_in), full(gq), full(gkv), full(w_uq), full(w_ukv), dense, dense],
        out_specs=[row(w) for w, _ in out_w] + [pl.BlockSpec(s, lambda i: (0, 0)) for s in built],
        compiler_params=_cparams(("arbitrary",)),
    )(x2, shift, scale, ng, w_in, gq, gkv, w_uq, w_ukv, *rope)


def _mla_fwd(q3, k3, v3, tq, tk):
    B, S, _ = q3.shape
    nq = S // tq
    assert tq == tk
    HPS = 2
    NH = 2 * HPS

    def body(q_ref, k_ref, v_ref, o_ref, lse_ref):
        rows = lax.broadcasted_iota(jnp.int32, (tq, tk), 0)
        cols = lax.broadcasted_iota(jnp.int32, (tq, tk), 1)
        low_k = _lane_iota((tk, LANE)) < 64
        low = _lane_iota((tq, LANE)) < 64

        def step(qs, kt, carry, masked):
            r0 = kt * tk
            out = []
            for j in range(NH):
                v2 = v_ref[0, r0:r0 + tk, LANE * (j // 2):LANE * (j // 2 + 1)]
                vj = (jnp.where(low_k, v2, 1.0) if j % 2 == 0 else jnp.where(low_k, 1.0, v2)).astype(v2.dtype)
                m, acc = carry[j]
                s = _dot_nt(qs[j], k_ref[0, r0:r0 + tk, LANE * j:LANE * (j + 1)])
                if masked:
                    s = jnp.where(rows >= cols, s, NEG_INF)
                m_new = jnp.maximum(m, jnp.max(s, axis=1, keepdims=True))
                alpha = jnp.exp2(m - m_new)
                p = jnp.exp2(s - m_new)
                acc = alpha * acc + _dot(_mx(p), vj)
                out.append((m_new, acc))
            return tuple(out)

        for qi in range(nq):
            r = slice(qi * tq, (qi + 1) * tq)
            qs = [q_ref[0, r, LANE * j:LANE * (j + 1)] for j in range(NH)]
            init = (jnp.full((tq, 1), NEG_INF, F32), jnp.zeros((tq, LANE), F32))
            carry = (init,) * NH
            for kt in range(qi):
                carry = step(qs, kt, carry, False)
            carry = step(qs, qi, carry, True)
            for t in range(HPS):
                (m0, a0), (m1, a1) = carry[2 * t], carry[2 * t + 1]
                l0 = jnp.where(low, pltpu.roll(a0, 64, 1), a0)
                l1 = jnp.where(low, a1, pltpu.roll(a1, 64, 1))
                o_ref[0, r, LANE * t:LANE * (t + 1)] = jnp.where(low, a0 / l0, a1 / l1)
                lse_ref[0, 2 * t, r, :] = m0 + jnp.log2(l0)
                lse_ref[0, 2 * t + 1, r, :] = m1 + jnp.log2(l1)

    grp = lambda w: pl.BlockSpec((1, S, w), lambda b, hp: (b, 0, hp))
    return pl.pallas_call(
        body, name="mla_fwd", grid=(B, MLA_HEADS // NH),
        out_shape=[jax.ShapeDtypeStruct((B, S, 512), F32), jax.ShapeDtypeStruct((B, MLA_HEADS, S, LANE), F32)],
        in_specs=[grp(NH * LANE), grp(NH * LANE), grp(HPS * LANE)],
        out_specs=[grp(HPS * LANE), pl.BlockSpec((1, NH, S, LANE), lambda b, hp: (b, hp, 0, 0))],
        compiler_params=_cparams(("arbitrary", "arbitrary")),
    )(q3, k3, v3)


def _mla_bwd(q3, k3, v3, o3, do3, lse, tq, tk, hps=2):
    B, S, _ = q3.shape
    nq, nk = S // tq, S // tk
    nh = 2 * hps
    assert tk % tq == 0

    def body(q_ref, k_ref, v_ref, o_ref, do_ref, lse_ref, dq_ref, dk_ref, dv_ref, dkt_ref, dvt_ref):
        rows = lax.broadcasted_iota(jnp.int32, (tq, tk), 0)
        cols = lax.broadcasted_iota(jnp.int32, (tq, tk), 1)
        lane = _lane_iota((tq, LANE))

        def q_tile(qi, _):
            r = pl.ds(pl.multiple_of(qi * tq, tq), tq)
            heads = []
            for j in range(nh):
                lanes = slice(LANE * j, LANE * (j + 1))
                pair = slice(LANE * (j // 2), LANE * (j // 2 + 1))
                do = jnp.where((lane < 64) if j % 2 == 0 else (lane >= 64), do_ref[0, r, pair], 0.0)
                q = q_ref[0, r, lanes]
                heads.append((lanes, pair, q, _mx(q.astype(F32).T), _mx(do), _mx(do.T),
                              jnp.sum(do * o_ref[0, r, pair], axis=1, keepdims=True),
                              jnp.concatenate([lse_ref[0, j, r, :]] * (tk // LANE), axis=1)))
            n_full = (qi * tq) // tk

            def k_tile(kt, dqs, masked):
                kr = pl.ds(pl.multiple_of(kt * tk, tk), tk)
                first = masked and (qi * tq) % tk == 0
                out = []
                dvt = [None] * hps
                for j, (lanes, pair, q, qt, dob, dot_, dcol, lse_c) in enumerate(heads):
                    k = k_ref[0, kr, lanes]
                    s = _dot_nt(q, k)
                    if masked:
                        s = jnp.where(rows + qi * tq >= cols + kt * tk, s, NEG_INF)
                    p = jnp.exp2(s - lse_c)
                    dp = _dot_nt(dob, v_ref[0, kr, pair])
                    dsb = _mx(p * (dp - dcol))
                    if first:
                        dkt_ref[j, kt] = _dot(qt, dsb)
                    else:
                        dkt_ref[j, kt] += _dot(qt, dsb)
                    pv = _dot(dot_, _mx(p))
                    dvt[j // 2] = pv if dvt[j // 2] is None else dvt[j // 2] + pv
                    out.append(dqs[j] + _dot(dsb, k))
                for t in range(hps):
                    if first:
                        dvt_ref[t, kt] = dvt[t]
                    else:
                        dvt_ref[t, kt] += dvt[t]
                return tuple(out)

            dqs = (jnp.zeros((tq, LANE), F32),) * nh
            for kt in range(n_full):
                dqs = k_tile(kt, dqs, False)
            dqs = k_tile(n_full, dqs, True)
            for j in range(nh):
                dq_ref[0, r, heads[j][0]] = (MLA_SCALE * dqs[j]).astype(dq_ref.dtype)
            return 0

        for qi in range(nq):
            q_tile(qi, 0)

        def flush(kt, _):
            kr = pl.ds(pl.multiple_of(kt * tk, tk), tk)
            for j in range(nh):
                dk_ref[0, kr, LANE * j:LANE * (j + 1)] = ((1.0 / LOG2E) * dkt_ref[j, kt].T).astype(dk_ref.dtype)
            for t in range(hps):
                dv_ref[0, kr, LANE * t:LANE * (t + 1)] = dvt_ref[t, kt].T.astype(dv_ref.dtype)
            return 0

        lax.fori_loop(0, nk, flush, 0)

    grp = lambda w: pl.BlockSpec((1, S, w), lambda b, hp: (b, 0, hp))
    return pl.pallas_call(
        body, name="mla_bwd", grid=(B, MLA_HEADS // nh),
        out_shape=[jax.ShapeDtypeStruct((B, S, 1024), MLA_GRAD_DTYPE), jax.ShapeDtypeStruct((B, S, 1024), MLA_GRAD_DTYPE),
                   jax.ShapeDtypeStruct((B, S, 512), MLA_GRAD_DTYPE)],
        in_specs=[grp(nh * LANE), grp(nh * LANE), grp(hps * LANE), grp(hps * LANE), grp(hps * LANE),
                  pl.BlockSpec((1, nh, S, LANE), lambda b, hp: (b, hp, 0, 0))],
        out_specs=[grp(nh * LANE), grp(nh * LANE), grp(hps * LANE)],
        scratch_shapes=[pltpu.VMEM((nh, nk, LANE, tk), F32), pltpu.VMEM((hps, nk, LANE, tk), F32)],
        compiler_params=_cparams(("arbitrary", "arbitrary")),
    )(q3, k3, v3, o3, do3, lse)


def _swa_consts(sink_ref):
    W = WINDOW
    row = lax.broadcasted_iota(jnp.int32, (4 * W, LANE), 0)
    out = []
    for g in range(SWA_KV_HEADS):
        slope = jnp.zeros((4 * W, LANE), F32)
        sink = jnp.zeros((4 * W, LANE), F32)
        for p in range(4):
            h = p + 4 * g
            here = jnp.logical_and(row >= W * p, row < W * (p + 1))
            slope = jnp.where(here, float(LOG2E * 2.0 ** (-8.0 * (h + 1) / SWA_HEADS)), slope)
            sink = jnp.where(here, LOG2E * sink_ref[:, h:h + 1], sink)
        out.append((jnp.concatenate([slope, slope], axis=1), sink))
    return out


def _wide(col):
    return jnp.concatenate([col, col], axis=1)


def _swa_block(q_ref, k_ref, v_ref, pr_ref, n, i, ij):
    W = WINDOW
    kb = jnp.maximum(n - 1, 0)
    r = pl.ds(pl.multiple_of(i * W, W), W)
    kr = pl.ds(pl.multiple_of(kb * W, W), 2 * W)
    q4, k2, v2 = q_ref[0, r, :], k_ref[0, kr, :], v_ref[0, kr, :]
    pq = _wide(jnp.broadcast_to(pr_ref[0, pl.ds(n, 1), :], (W, W)).T)
    pk = jnp.concatenate([pr_ref[0, pl.ds(kb, 1), :], pr_ref[0, pl.ds(kb + 1, 1), :]], axis=1)
    rel = ij + (n - kb) * W
    dist = jnp.where(jnp.logical_and(rel >= 0, rel < W), pq - pk, float("inf"))
    return r, kb, kr, q4, k2, v2, jnp.concatenate([dist] * 4, axis=0)


def _swa_stack(x4, g, dtype):
    lane = _lane_iota((WINDOW, LANE))
    mine = (lane < 64) if g == 0 else (lane >= 64)
    return jnp.concatenate([jnp.where(mine, x4[:, LANE * p:LANE * (p + 1)], 0).astype(dtype) for p in range(4)], axis=0)


def _swa_unstack(ref, r, lo, hi, scale=None):
    W = WINDOW
    low = _lane_iota((W, LANE)) < 64
    for p in range(4):
        t = jnp.where(low, lo[W * p:W * (p + 1)], hi[W * p:W * (p + 1)])
        ref[0, r, LANE * p:LANE * (p + 1)] = t if scale is None else scale * t


def _swa_fwd(qs3, ks3, vs3, posr, sinks):
    B, S, _ = qs3.shape
    W = WINDOW
    nb = S // W
    nh = SWA_SEQ_SPLIT
    nbh = nb // nh

    def body(q_ref, k_ref, v_ref, pr_ref, sink_ref, o_ref, lse_ref):
        ij = lax.broadcasted_iota(jnp.int32, (W, 2 * W), 0) - lax.broadcasted_iota(jnp.int32, (W, 2 * W), 1)
        consts = _swa_consts(sink_ref)
        n0 = pl.program_id(1) * nbh

        def blk(i, _):
            n = n0 + i
            r, _, _, q4, k2, v2, dist4 = _swa_block(q_ref, k_ref, v_ref, pr_ref, n, i, ij)
            o_g = []
            for g, (slope, sink) in enumerate(consts):
                s = _dot_nt(_swa_stack(q4, g, q4.dtype), k2) - slope * dist4
                m = jnp.maximum(jnp.max(s, axis=1, keepdims=True), sink)
                e = jnp.exp2(s - _wide(m))
                l = jnp.sum(e, axis=1, keepdims=True) + jnp.exp2(sink - m)
                o_g.append(_dot(_mx(e), v2) * (1.0 / l))
                lse_ref[0, i, g] = m + jnp.log2(l)
            _swa_unstack(o_ref, r, o_g[0], o_g[1])
            return 0

        for i in range(nbh):
            blk(i, 0)

    seq = lambda w: pl.BlockSpec((1, S, w), lambda b, h: (b, 0, 0))
    part = lambda w: pl.BlockSpec((1, S // nh, w), lambda b, h: (b, h, 0))
    lse_spec = pl.BlockSpec((1, nbh, 2, 4 * W, LANE), lambda b, h: (b, h, 0, 0, 0))
    return pl.pallas_call(
        body, name="swa_fwd", grid=(B, nh),
        out_shape=[jax.ShapeDtypeStruct((B, S, 512), F32), jax.ShapeDtypeStruct((B, nb, 2, 4 * W, LANE), F32)],
        in_specs=[part(512), seq(LANE), seq(LANE), pl.BlockSpec((1, nb, W), lambda b, h: (b, 0, 0)),
                  pl.BlockSpec((1, LANE), lambda b, h: (0, 0))],
        out_specs=[part(512), lse_spec],
        compiler_params=_cparams(("arbitrary", "arbitrary")),
    )(qs3, ks3, vs3, posr, sinks)


def _swa_bwd(qs3, ks3, vs3, posr, sinks, os3, do3, lse):
    B, S, _ = qs3.shape
    W = WINDOW
    nb = S // W
    nh = SWA_SEQ_SPLIT
    nbh = nb // nh
    assert nbh % 2 == 0 and nbh * nh * W == S

    def body(q_ref, k_ref, v_ref, pr_ref, sink_ref, o_ref, do_ref, lse_ref, dq_ref, dk_ref, dv_ref, dsink_ref,
             dkt_ref, dvt_ref):
        lane1 = _lane_iota((1, LANE))
        ij = lax.broadcasted_iota(jnp.int32, (W, 2 * W), 0) - lax.broadcasted_iota(jnp.int32, (W, 2 * W), 1)
        consts = _swa_consts(sink_ref)
        hh = pl.program_id(1)
        n0 = hh * nbh

        @pl.when(hh == 0)
        def _():
            dkt_ref[...] = jnp.zeros_like(dkt_ref)
            dvt_ref[...] = jnp.zeros_like(dvt_ref)

        @pl.when(jnp.logical_and(pl.program_id(0) == 0, hh == 0))
        def _():
            dsink_ref[...] = jnp.zeros_like(dsink_ref)

        def blk(i, dsink):
            n = n0 + i
            r, kb, _, q4, k2, v2, dist4 = _swa_block(q_ref, k_ref, v_ref, pr_ref, n, i, ij)
            o4, do4 = o_ref[0, r, :], do_ref[0, r, :]
            dq_g = []
            dkt = jnp.zeros((LANE, 2 * W), F32)
            dvt = jnp.zeros((LANE, 2 * W), F32)
            for g, (slope, sink) in enumerate(consts):
                q_st = _swa_stack(q4, g, F32)
                do_st = _swa_stack(do4, g, F32)
                dcol = jnp.sum(do_st * _swa_stack(o4, g, F32), axis=1, keepdims=True)
                qst, dob = _mx(q_st), _mx(do_st)
                lse_c = lse_ref[0, i, g]
                pr = jnp.exp2(_dot_nt(qst, k2) - slope * dist4 - _wide(lse_c))
                dsb = _mx(pr * (_dot_nt(dob, v2) - dcol))
                psd = jnp.exp2(sink - lse_c)[:, 0:1] * dcol
                for p in range(4):
                    dsink = dsink - jnp.where(lane1 == p + 4 * g,
                                              jnp.sum(psd[W * p:W * (p + 1)], axis=0, keepdims=True), 0.0)
                dq_g.append(_dot(dsb, k2))
                dkt = dkt + _dot(_mx(q_st.T), dsb)
                dvt = dvt + _dot(_mx(do_st.T), _mx(pr))
            _swa_unstack(dq_ref, r, dq_g[0], dq_g[1], SWA_SCALE)
            dkt_ref[kb] += dkt[:, 0:W]
            dkt_ref[kb + 1] += dkt[:, W:2 * W]
            dvt_ref[kb] += dvt[:, 0:W]
            dvt_ref[kb + 1] += dvt[:, W:2 * W]
            return dsink

        dsink = jnp.zeros((1, LANE), F32)
        for i in range(nbh):
            dsink = blk(i, dsink)
        dsink_ref[...] += dsink

        @pl.when(hh == nh - 1)
        def _():
            def flush(n, _):
                r = pl.ds(pl.multiple_of(n * W, W), W)
                dk_ref[0, r, :] = (1.0 / LOG2E) * dkt_ref[n].T
                dv_ref[0, r, :] = dvt_ref[n].T
                return 0

            lax.fori_loop(0, nb, flush, 0)

    seq = lambda w: pl.BlockSpec((1, S, w), lambda b, h: (b, 0, 0))
    part = lambda w: pl.BlockSpec((1, S // nh, w), lambda b, h: (b, h, 0))
    return pl.pallas_call(
        body, name="swa_bwd", grid=(B, nh),
        out_shape=[jax.ShapeDtypeStruct((B, S, 512), F32), jax.ShapeDtypeStruct((B, S, LANE), F32),
                   jax.ShapeDtypeStruct((B, S, LANE), F32), jax.ShapeDtypeStruct((1, LANE), F32)],
        in_specs=[part(512), seq(LANE), seq(LANE), pl.BlockSpec((1, nb, W), lambda b, h: (b, 0, 0)),
                  pl.BlockSpec((1, LANE), lambda b, h: (0, 0)), part(512),
                  pl.BlockSpec((1, S // nh, 512), lambda b, h: (b, h, 1)),
                  pl.BlockSpec((1, nbh, 2, 4 * W, LANE), lambda b, h: (b, h, 0, 0, 0))],
        out_specs=[part(512), seq(LANE), seq(LANE), pl.BlockSpec((1, LANE), lambda b, h: (0, 0))],
        scratch_shapes=[pltpu.VMEM((nb, LANE, W), F32), pltpu.VMEM((nb, LANE, W), F32)],
        compiler_params=_cparams(("arbitrary", "arbitrary")),
    )(qs3, ks3, vs3, posr, sinks, os3, do3, lse)


def _post(om, osw, g, w_out, x2, gate, fg, tgt, S, ts):
    T, D = x2.shape
    nsb = S // ts

    def body(om_ref, os_ref, g_ref, w_ref, x_ref, gate_ref, fg_ref, t_ref,
             dx2_ref, do_ref, dg_ref, loss_ref, dfg_ref, dgate_ref, dw_ref, dwb_ref):
        i = pl.program_id(0)
        gv = g_ref[...]
        sg = 1.0 / (1.0 + jnp.exp(-gv))
        silu = gv * sg
        o = jnp.concatenate([om_ref[...], os_ref[...]], axis=1)
        ab = _mx(o * silu)
        y = _dot(ab, w_ref[...])
        gate = gate_ref[0]
        xo = x_ref[...] + gate * y
        r2 = lax.rsqrt(jnp.mean(xo * xo, axis=-1, keepdims=True) + EPS)
        xh = xo * r2
        fg = fg_ref[...]
        diff = xh * fg - t_ref[...]
        sq = jnp.sum(diff * diff, axis=0, keepdims=True)
        part = sq[:, 0:LANE]
        for t in range(1, D // LANE):
            part = part + sq[:, LANE * t:LANE * (t + 1)]
        dout = diff * (1.0 / D)
        dxh = dout * fg
        dx2 = r2 * (dxh - xh * jnp.mean(dxh * xh, axis=-1, keepdims=True))
        dx2_ref[...] = dx2
        dyb = _mx(dx2 * gate)
        da = _dot_nt(dyb, w_ref[...])
        do_ref[...] = da * silu
        dg_ref[...] = _mx(da * o * (sg * (1.0 + gv * (1.0 - sg))))

        @pl.when(i == 0)
        def _():
            loss_ref[...] = jnp.zeros_like(loss_ref)
            dfg_ref[...] = jnp.zeros_like(dfg_ref)
            dw_ref[...] = jnp.zeros_like(dw_ref)

        @pl.when(i % nsb == 0)
        def _():
            dgate_ref[...] = jnp.zeros_like(dgate_ref)

        loss_ref[...] += (0.5 / D) * part
        dfg_ref[...] += jnp.sum(dout * xh, axis=0, keepdims=True)
        dgate_ref[0] += jnp.sum(dx2 * y, axis=0, keepdims=True)
        dw_ref[...] += _dot_tn(ab, dyb)

        @pl.when(i == T // ts - 1)
        def _():
            chunks = [dw_ref[512 + 64 * PAIR_INV[h]:512 + 64 * (PAIR_INV[h] + 1), :] for h in range(SWA_HEADS)]
            for h in range(SWA_HEADS):
                dw_ref[512 + 64 * h:512 + 64 * (h + 1), :] = chunks[h]
            dwb_ref[...] = dw_ref[...].astype(jnp.bfloat16)

    row = lambda w: pl.BlockSpec((ts, w), lambda i: (i, 0))
    full = lambda a: pl.BlockSpec(a.shape, lambda i: (0,) * a.ndim)
    per_b = pl.BlockSpec((1, 1, D), lambda i: (i // nsb, 0, 0))
    return pl.pallas_call(
        body, name="post", grid=(T // ts,),
        out_shape=[jax.ShapeDtypeStruct((T, D), F32), jax.ShapeDtypeStruct((T, 1024), F32),
                   jax.ShapeDtypeStruct((T, 1024), _MXU_DTYPE), jax.ShapeDtypeStruct((1, LANE), F32),
                   jax.ShapeDtypeStruct((1, D), F32), jax.ShapeDtypeStruct(gate.shape, F32),
                   jax.ShapeDtypeStruct(w_out.shape, F32), jax.ShapeDtypeStruct(w_out.shape, jnp.bfloat16)],
        in_specs=[row(512), row(512), row(1024), full(w_out), row(D), per_b, full(fg), row(D)],
        out_specs=[row(D), row(1024), row(1024),
                   pl.BlockSpec((1, LANE), lambda i: (0, 0)), pl.BlockSpec((1, D), lambda i: (0, 0)), per_b,
                   full(w_out), full(w_out)],
        compiler_params=_cparams(("arbitrary",)),
    )(om, osw, g, w_out, x2, gate, fg, tgt)


def _pre_bwd(dq, dk, dv, dqs, dks, dvs, dg, zq, zkv, ql, kvl, x2, dx2, scale, ng, w_in, gq, gkv, w_uq, w_uk, w_uv,
             rope, S, ts):
    T, D = x2.shape
    nsb = S // ts
    WQ = MLA_HEADS * LANE

    def body(dq_ref, dk_ref, dv_ref, dqs_ref, dks_ref, dvs_ref, dg_ref, zq_ref, zkv_ref, ql_ref, kvl_ref, x_ref, dx2_ref,
             sc_ref,
             ng_ref, w_ref, gq_ref, gkv_ref, wuq_ref, wuk_ref, wuv_ref, cd_ref, sd_ref,
             gx_ref, dz_ref, dgq_ref, dgkv_ref, dng_ref, dsh_ref, dsc_ref, uq_ref, uqb_ref, ukv_ref, ukvb_ref,
             dwuq_ref, dwuk_ref, dwuv_ref):
        i = pl.program_id(0)

        @pl.when(i == 0)
        def _():
            dgq_ref[...] = jnp.zeros_like(dgq_ref)
            dgkv_ref[...] = jnp.zeros_like(dgkv_ref)
            dng_ref[...] = jnp.zeros_like(dng_ref)
            dwuq_ref[...] = jnp.zeros_like(dwuq_ref)
            dwuk_ref[...] = jnp.zeros_like(dwuk_ref)
            dwuv_ref[...] = jnp.zeros_like(dwuv_ref)

        @pl.when(i % nsb == 0)
        def _():
            dsh_ref[...] = jnp.zeros_like(dsh_ref)
            dsc_ref[...] = jnp.zeros_like(dsc_ref)

        def norm_bwd(z, dy, gain):
            r = lax.rsqrt(jnp.mean(z * z, axis=-1, keepdims=True) + EPS)
            zh = z * r
            dzh = dy * gain
            return r * (dzh - zh * jnp.mean(dzh * zh, axis=-1, keepdims=True)), jnp.sum(dy * zh, axis=0, keepdims=True)

        tables = _rope_split(cd_ref[...], sd_ref[...])
        ng = ng_ref[...]
        sc1 = 1.0 + sc_ref[0]

        def rows_chain(rs):
            cq, ck, sa, sb = (t[rs] for t in tables)
            dqr = dq_ref[rs, :].astype(F32)
            dqb = _mx(dqr * _tile_heads(cq) + pltpu.roll(dqr * _tile_heads(sa), 16, 1)
                      + pltpu.roll(dqr * _tile_heads(sb), WQ - 16, 1))
            p_uq = _dot_tn(ql_ref[rs, :], dqb)
            dzq, dgq = norm_bwd(zq_ref[rs, :], _dot_nt(dqb, wuq_ref[...]), gq_ref[...])
            dkr = dk_ref[rs, :].astype(F32)
            dkb = _mx(dkr)
            p_uk = _dot_tn(kvl_ref[rs, :], dkb)
            dvb = _mx(dv_ref[rs, :])
            p_uv = _dot_tn(kvl_ref[rs, :], dvb)
            dzkv, dgkv = norm_bwd(zkv_ref[rs, :], _dot_nt(dkb, wuk_ref[...]) + _dot_nt(dvb, wuv_ref[...]), gkv_ref[...])
            dkpe = dkr[:, 0:LANE]
            for h in range(1, MLA_HEADS):
                dkpe = dkpe + dkr[:, LANE * h:LANE * (h + 1)]
            dkro = dkpe * ck + pltpu.roll(dkpe * sa, 16, 1) + pltpu.roll(dkpe * sb, LANE - 16, 1)
            dz_ref[rs, 0:384] = _mx(dzq)
            dz_ref[rs, 384:640] = _mx(dzkv)
            dz_ref[rs, 640:768] = _mx(dkro)
            dz_ref[rs, 768:1280] = dg_ref[rs, 0:512]
            dz_ref[rs, 1280:1792] = _mx(dqs_ref[rs, :])
            dz_ref[rs, 1792:1920] = _mx(dks_ref[rs, :])
            dz_ref[rs, 1920:2048] = _mx(dvs_ref[rs, :])
            dz_ref[rs, 2048:2560] = dg_ref[rs, 512:1024]
            dh = _dot(dz_ref[rs, :], w_ref[...])
            x = x_ref[rs, :]
            r1 = lax.rsqrt(jnp.mean(x * x, axis=-1, keepdims=True) + EPS)
            xn = x * r1
            dxn = dh * ng * sc1
            gx_ref[rs, :] = dx2_ref[rs, :] + r1 * (dxn - xn * jnp.mean(dxn * xn, axis=-1, keepdims=True))
            return (p_uq, p_uk, p_uv, dgq, dgkv, jnp.sum(dh, axis=0, keepdims=True),
                    jnp.sum(dh * (xn * ng), axis=0, keepdims=True), jnp.sum(dh * xn * sc1, axis=0, keepdims=True))

        hr = ts // 2
        parts = [rows_chain(slice(hr * t, hr * (t + 1))) for t in range(2)]
        p_uq, p_uk, p_uv, dgq, dgkv, dsh, dsc, dng = (a + b for a, b in zip(*parts))
        dwuq_ref[...] += p_uq
        dwuk_ref[...] += p_uk
        dwuv_ref[...] += p_uv
        dgq_ref[...] += dgq
        dgkv_ref[...] += dgkv
        dsh_ref[0] += dsh
        dsc_ref[0] += dsc
        dng_ref[...] += dng

        @pl.when(i == T // ts - 1)
        def _():
            for h in range(MLA_HEADS):
                uq = dwuq_ref[:, LANE * h:LANE * h + MLA_QK]
                ukv = jnp.concatenate([dwuk_ref[:, LANE * h:LANE * h + MLA_NOPE], dwuv_ref[:, 64 * h:64 * (h + 1)]], axis=1)
                uq_ref[h] = uq
                uqb_ref[h] = uq.astype(jnp.bfloat16)
                ukv_ref[h] = ukv
                ukvb_ref[h] = ukv.astype(jnp.bfloat16)

    row = lambda w: pl.BlockSpec((ts, w), lambda i: (i, 0))
    full = lambda a: pl.BlockSpec(a.shape, lambda i: (0,) * a.ndim, pipeline_mode=pl.Buffered(1))
    slot = lambda r, c, dt: (jax.ShapeDtypeStruct((MLA_HEADS, r, c), dt), pl.BlockSpec((MLA_HEADS, r, c), lambda i: (0, 0, 0)))
    slots = [slot(Q_LORA, MLA_QK, F32), slot(Q_LORA, MLA_QK, jnp.bfloat16), slot(KV_LORA, LANE, F32), slot(KV_LORA, LANE, jnp.bfloat16)]
    per_b = pl.BlockSpec((1, 1, D), lambda i: (i // nsb, 0, 0))
    dense = pl.BlockSpec((ts // 8, LANE), lambda i: (i, 0))
    vec = lambda w: pl.BlockSpec((1, w), lambda i: (0, 0))
    return pl.pallas_call(
        body, name="pre_bwd", grid=(T // ts,),
        out_shape=[jax.ShapeDtypeStruct((T, D), F32), jax.ShapeDtypeStruct((T, D_IN_PAD), _MXU_DTYPE), jax.ShapeDtypeStruct((1, 384), F32),
                   jax.ShapeDtypeStruct((1, 256), F32), jax.ShapeDtypeStruct((1, D), F32),
                   jax.ShapeDtypeStruct(scale.shape, F32), jax.ShapeDtypeStruct(scale.shape, F32)] + [t[0] for t in slots],
        in_specs=[row(WQ), row(WQ), row(512), row(512), row(LANE), row(LANE), row(1024), row(384), row(256), row(384),
                  row(256), row(D), row(D), per_b, full(ng), full(w_in), full(gq), full(gkv), full(w_uq), full(w_uk), full(w_uv),
                  dense, dense],
        out_specs=[row(D), row(D_IN_PAD), vec(384), vec(256), vec(D), per_b, per_b] + [t[1] for t in slots],
        scratch_shapes=[pltpu.VMEM(w_uq.shape, F32), pltpu.VMEM(w_uk.shape, F32), pltpu.VMEM(w_uv.shape, F32)],
        compiler_params=_cparams(("arbitrary",)),
    )(dq, dk, dv, dqs, dks, dvs, dg, zq, zkv, ql, kvl, x2, dx2, scale, ng, w_in, gq, gkv, w_uq, w_uk, w_uv, *rope)


def _w_in_row_runs():
    runs = [(0, 0, 640), (640, 704, 32), (672, 768, 512)]
    runs += [(1184 + 64 * h, 1280 + 64 * PAIR_INV[h], 64) for h in range(8)]
    runs += [(1696, 1792, 256)]
    runs += [(1952 + 64 * h, 2048 + 64 * PAIR_INV[h], 64) for h in range(8)]
    return runs


W_IN_SHARD = D_IN // N_DEV
W_IN_SLOT = 320


def _dw_in_t(dz, hb, tn, tk):
    T, M = dz.shape
    N = hb.shape[1]
    nk = T // tk

    def body(a_ref, b_ref, o_ref, ob_ref, acc_ref):
        k = pl.program_id(1)

        @pl.when(k == 0)
        def _():
            acc_ref[...] = jnp.zeros_like(acc_ref)

        acc_ref[...] += _dot_tn(a_ref[...], b_ref[...])

        @pl.when(k == nk - 1)
        def _():
            for d in range(N_DEV):
                lo, hi = W_IN_SHARD * d, W_IN_SHARD * (d + 1)
                for nat, pad, size in _w_in_row_runs():
                    a, b = max(nat, lo), min(nat + size, hi)
                    if a < b:
                        piece = acc_ref[pad + a - nat:pad + b - nat, :]
                        o_ref[d, a - lo:b - lo, :] = piece
                        ob_ref[d, a - lo:b - lo, :] = piece.astype(ob_ref.dtype)
                o_ref[d, W_IN_SHARD:W_IN_SLOT, :] = jnp.zeros((W_IN_SLOT - W_IN_SHARD, tn), F32)
                ob_ref[d, W_IN_SHARD:W_IN_SLOT, :] = jnp.zeros((W_IN_SLOT - W_IN_SHARD, tn), ob_ref.dtype)

    slots = pl.BlockSpec((N_DEV, W_IN_SLOT, tn), lambda j, k: (0, 0, j))
    return pl.pallas_call(
        body, name="dw_in", grid=(N // tn, nk),
        out_shape=[jax.ShapeDtypeStruct((N_DEV, W_IN_SLOT, N), F32), jax.ShapeDtypeStruct((N_DEV, W_IN_SLOT, N), jnp.bfloat16)],
        in_specs=[pl.BlockSpec((tk, M), lambda j, k: (k, 0)), pl.BlockSpec((tk, tn), lambda j, k: (k, j))],
        out_specs=[slots, slots],
        scratch_shapes=[pltpu.VMEM((M, tn), F32)],
        compiler_params=_cparams(("arbitrary", "arbitrary")),
    )(dz, hb)


SHARDED = (3, 6, 7, 9)
PART_SLICES = {10: (128, 1152), 2: (1152, 2176), 4: (2176, 2560), 5: (2560, 2816), 8: (2816, 2824)}


def _finalize_adamw(parts_all, dmod_all, dmod_cols, c_all, ws, sharded_grads, ms, vs):
    n = len(ws)
    local = [t for t in range(n) if t not in SHARDED[2:]]

    def body(*refs):
        p_ref, dm_ref, dmc_ref, c_ref = refs[:4]
        w_refs, gs_refs, m_refs, v_refs = refs[4:4 + n], refs[4 + n:8 + n], refs[8 + n:8 + 2 * n], refs[8 + 2 * n:8 + 3 * n]
        outs = refs[8 + 3 * n:]
        loss_ref, gl_refs = outs[0], outs[1:1 + len(local)]
        d_refs, nm_refs, nv_refs = (outs[1 + len(local) + t * n:1 + len(local) + (t + 1) * n] for t in range(3))
        ps_ref = outs[-1]
        rd = lambda r: r[:, 0, :] if len(r.shape) == 3 else r[...]

        def wr(r, val):
            if len(r.shape) == 3:
                r[:, 0, :] = val
            else:
                r[...] = val

        acc = p_ref[0]
        for j in range(1, N_DEV):
            acc = acc + p_ref[j]
        ps_ref[...] = acc
        loss_ref[...] = jnp.sum(acc[:, 0:LANE], axis=1, keepdims=True)
        db = dm_ref[0:1, :]
        for j in range(1, dm_ref.shape[0]):
            db = db + dm_ref[j:j + 1, :]
        grads = {0: _dot_tn(_mx(_silu(c_ref[...])), _mx(dmc_ref[...])), 1: db}
        for t, (lo, hi) in PART_SLICES.items():
            grads[t] = ps_ref[:, lo:hi]
        for i, t in enumerate(SHARDED):
            grads[t] = gs_refs[i][...]
        grads[SHARDED[0]] = gs_refs[0][0:W_IN_SHARD, :]
        grads[SHARDED[1]] = gs_refs[1][...].T
        for i, t in enumerate(local):
            wr(gl_refs[i], grads[t])
        for t in range(n):
            gv = grads[t]
            nm = ADAM_B1 * rd(m_refs[t]) + (1.0 - ADAM_B1) * gv
            nv = ADAM_B2 * rd(v_refs[t]) + (1.0 - ADAM_B2) * (gv * gv)
            m_hat = nm / (1.0 - ADAM_B1 ** ADAM_STEP)
            v_hat = nv / (1.0 - ADAM_B2 ** ADAM_STEP)
            wr(d_refs[t], -ADAM_LR * (m_hat / (jnp.sqrt(v_hat) + ADAM_EPS) + ADAM_WD * rd(w_refs[t])))
            wr(nm_refs[t], nm)
            wr(nv_refs[t], nv)

    like = lambda arrs: [jax.ShapeDtypeStruct(a.shape, F32) for a in arrs]
    out = pl.pallas_call(
        body, name="finalize_adamw",
        out_shape=[jax.ShapeDtypeStruct((1, 1), F32)] + like([ws[t] for t in local]) + like(ws) * 3,
        in_specs=[_vmem()] * (8 + 3 * n), out_specs=[_vmem()] * (1 + len(local) + 3 * n),
        scratch_shapes=[pltpu.VMEM(parts_all.shape[1:], F32)],
        compiler_params=_cparams(),
    )(parts_all, dmod_all, dmod_cols, c_all, *ws, *sharded_grads, *ms, *vs)
    k = 1 + len(local)
    return out[0], dict(zip(local, out[1:k])), out[k:k + n], out[k + n:k + 2 * n], out[k + 2 * n:]


def _pair_perm(a, axis, order):
    a = jnp.moveaxis(a, axis, -1)
    lead = a.shape[:-1]
    a = a.reshape(lead + (8, 64))[..., list(order), :].reshape(lead + (512,))
    return jnp.moveaxis(a, -1, axis)


def _rope_table(positions):
    T = positions.size
    inv = ROPE_THETA ** (-jnp.arange(0, MLA_ROPE, 2, dtype=F32) / MLA_ROPE)
    pos = jnp.repeat(positions.reshape(T // 8, 8)[:, ::-1].astype(F32), MLA_ROPE // 2, axis=1)
    ang = pos * jnp.tile(inv, 8)[None, :]
    return jnp.cos(ang), jnp.sin(ang)


def _local_step(x, mod, positions, ng, w_in_t, gq, gkv, w_uq, w_ukv, sinks, w_out, fg, tgt,
                ts=512, fq=512, fk=512, bq=512, bk=512):
    B, S, D = x.shape
    T = B * S
    x2 = x.reshape(T, D)
    shift, scale, gate = (mod[:, None, k * D:(k + 1) * D] for k in range(3))
    w_out_p = jnp.concatenate([w_out[:512], _pair_perm(w_out[512:], 0, PAIR_ORDER)], axis=0)
    rope = _rope_table(positions)
    posf = positions.astype(F32)
    posr = posf.reshape(B, S // WINDOW, WINDOW)
    sinks_l = jnp.pad(sinks.reshape(1, SWA_HEADS), ((0, 0), (0, LANE - SWA_HEADS)))

    (hb, zq, zkv, ql, kvl, q, k, v, qs, ks, vs, g, w_in_p, w_uq_p, w_uk_p, w_uv) = _pre_fwd(
        x2, shift, scale, ng, w_in_t, gq, gkv, w_uq, w_ukv, rope, S, ts)
    r3 = lambda a: a.reshape(B, S, a.shape[-1])
    om, lse_m = _mla_fwd(r3(q), r3(k), r3(v), fq, fk)
    osw, lse_s = _swa_fwd(r3(qs), r3(ks), r3(vs), posr, sinks_l)
    dx2, do, dg, loss_v, dfg, dgate, *dw_out = _post(
        om.reshape(T, 512), osw.reshape(T, 512), g, w_out_p, x2, gate, fg.reshape(1, D), tgt.reshape(T, D), S, ts)
    do3 = r3(do)
    dq, dk, dv = _mla_bwd(r3(q), r3(k), r3(v), om, do3, lse_m, bq, bk)
    dqs, dks, dvs, dsink = _swa_bwd(r3(qs), r3(ks), r3(vs), posr, sinks_l, osw, do3, lse_s)
    f2 = lambda a: a.reshape(T, a.shape[-1])
    gx, dz, dgq, dgkv, dng, dsh, dsc, *dw_u = _pre_bwd(
        f2(dq), f2(dk), f2(dv), f2(dqs), f2(dks), f2(dvs), dg, zq, zkv, ql, kvl, x2, dx2, scale, ng, w_in_p, gq, gkv,
        w_uq_p, w_uk_p, w_uv, rope, S, ts)
    tk = min(T, 1024)
    dw_in_t = _dw_in_t(dz, hb, 512, tk)
    parts = jnp.concatenate([loss_v, dfg, dng, dgq, dgkv, dsink], axis=1)
    dmod = jnp.concatenate([dsh, dsc, dgate], axis=2).reshape(B, 3 * D)
    return gx.reshape(B, S, D), dw_in_t, tuple(dw_u[0:2]), tuple(dw_u[2:4]), tuple(dw_out), parts, dmod


def kernel(x, c, positions, w_ada, b_ada, norm_gain, w_in, q_norm_gain, kv_norm_gain, w_uq, w_ukv, swa_sinks, w_out, final_gain, loss_target, m_w_ada, m_b_ada, m_norm_gain, m_w_in, m_q_norm_gain, m_kv_norm_gain, m_w_uq, m_w_ukv, m_swa_sinks, m_w_out, m_final_gain, v_w_ada, v_b_ada, v_norm_gain, v_w_in, v_q_norm_gain, v_kv_norm_gain, v_w_uq, v_w_ukv, v_swa_sinks, v_w_out, v_final_gain):
    B, S, D = x.shape
    me = 4 * lax.axis_index("x") + 2 * lax.axis_index("y") + lax.axis_index("c")
    bf = _MXU_DTYPE

    ncol = w_ada.shape[2]
    b_cols = lax.dynamic_slice_in_dim(b_ada, me * ncol, ncol, axis=1)
    c_all, win_g, wuq_g, wukv_g, wout_g, mod_g = _all_gather(
        [c, (jnp.transpose(w_in, (2, 0, 1)), bf), (jnp.transpose(w_uq[0]), bf, True), (w_ukv[0], bf), (w_out[0], bf)],
        "ag_weights",
        fused=((w_ada[0], b_cols), _ada_cols, jax.ShapeDtypeStruct((N_DEV, B, ncol), F32)))
    c_all = c_all.reshape(N_DEV * B, D)
    w_out_f = wout_g.reshape(D, D)

    mod = lax.dynamic_index_in_dim(mod_g, me, axis=1, keepdims=False)
    mod = jnp.transpose(mod, (1, 0, 2)).reshape(B, 3 * D)

    gx, dw_in_t, dw_uq, dw_ukv, dw_out, parts, dmod = _local_step(
        x, mod, positions, norm_gain, win_g, q_norm_gain, kv_norm_gain, wuq_g, wukv_g, swa_sinks,
        w_out_f, final_gain, loss_target)

    slots = lambda pair: tuple(a.reshape((4, 2, -1) + a.shape[-1:]) for a in pair)
    g_w_in_t, g_w_uq, g_w_ukv, g_w_out, parts_g, dmod_g = _reduce_scatter(
        [slots(dw_in_t), slots(dw_uq), slots(dw_ukv), slots(dw_out)], "rs_grads",
        gather=(parts, dmod))

    dmod_all = dmod_g.reshape(N_DEV * B, 3 * D)
    dmod_cols = lax.dynamic_slice_in_dim(dmod_all, me * ncol, ncol, axis=1)
    ws = [w_ada, b_ada, norm_gain, w_in, q_norm_gain, kv_norm_gain, w_uq, w_ukv, swa_sinks, w_out, final_gain]
    ms = [m_w_ada, m_b_ada, m_norm_gain, m_w_in, m_q_norm_gain, m_kv_norm_gain, m_w_uq, m_w_ukv, m_swa_sinks, m_w_out,
          m_final_gain]
    vs = [v_w_ada, v_b_ada, v_norm_gain, v_w_in, v_q_norm_gain, v_kv_norm_gain, v_w_uq, v_w_ukv, v_swa_sinks, v_w_out,
          v_final_gain]
    two_d = lambda a: a.reshape((1, a.shape[0]) if a.ndim == 1 else a.shape[-2:])
    view = {SHARDED[0]: lambda a: jnp.transpose(a, (2, 0, 1)), SHARDED[1]: lambda a: jnp.transpose(a[0])}
    flat = lambda arrs: [view[t](a) if t in view else two_d(a) for t, a in enumerate(arrs)]
    sharded = {3: g_w_in_t, 6: g_w_uq, 7: g_w_ukv, 9: g_w_out}
    loss, local_grads, deltas, new_ms, new_vs = _finalize_adamw(
        parts_g, dmod_all, dmod_cols, c_all, flat(ws), [sharded[t] for t in SHARDED], flat(ms), flat(vs))
    grads = [local_grads[t] if t in local_grads else sharded[t] for t in range(len(ws))]
    back = {SHARDED[0]: lambda a: jnp.transpose(a, (1, 2, 0)), SHARDED[1]: lambda a: jnp.transpose(a)[None]}
    shaped = lambda arrs: [back[t](a) if t in back else a.reshape(ws[t].shape) for t, a in enumerate(arrs)]
    return (loss.reshape(()), gx, *shaped(grads), *shaped(deltas), *shaped(new_ms), *shaped(new_vs))
```
